```python
import jax, jax.numpy as jnp
from jax import lax
import numpy as np

D_MODEL = 1024
BATCH = 8
SEQ = 4096
DEPTH = 2

N_MIXERS = 2
N_ATTN_LAYERS = (DEPTH + 1) // 2
N_HGRN_LAYERS = DEPTH // 2
ATTN_HEAD_DIM = 64
ATTN_HEADS = D_MODEL // ATTN_HEAD_DIM
DILATED_PATTERNS = ((128, 1), (512, 4), (2048, 16))
N_GROUPS = len(DILATED_PATTERNS)
ROPE_THETA = 10000.0
HGRN_EXPAND = 128
HGRN_HEADS = D_MODEL // HGRN_EXPAND
HGRN_DK = HGRN_EXPAND
HGRN_DV = D_MODEL // HGRN_HEADS
HGRN_CHUNK = 64
D_FF = 4 * D_MODEL
LN_EPS = 1e-5
RMS_EPS = 1e-6
DEEPNORM_ALPHA = (2 * DEPTH) ** 0.25
DEEPNORM_BETA = (8 * DEPTH) ** -0.25

kernel_name = 'hybrid_dilated_attn_hgrn2_deepnorm'

F32 = jnp.float32


def layer_norm(x, g, b):
    xf = x.astype(F32)
    mu = jnp.mean(xf, axis=-1, keepdims=True)
    var = jnp.mean(jnp.square(xf - mu), axis=-1, keepdims=True)
    return ((xf - mu) * lax.rsqrt(var + LN_EPS) * g.astype(F32) + b.astype(F32)).astype(x.dtype)


def rotary(x, pos):
    e = x.shape[-1]
    half = e // 2
    inv = ROPE_THETA ** (-jnp.arange(half, dtype=F32) * (2.0 / e))
    ang = pos.astype(F32)[:, None] * inv[None, :]
    cos = jnp.cos(ang)[None, :, None, :]
    sin = jnp.sin(ang)[None, :, None, :]
    xf = x.astype(F32)
    x1, x2 = xf[..., :half], xf[..., half:]
    return jnp.concatenate([x1 * cos - x2 * sin, x2 * cos + x1 * sin], axis=-1).astype(x.dtype)


def dilated_window_attention(q, k, v, window, dilation):
    B, S, H, E = q.shape
    blk = window // dilation
    span = dilation * blk
    s_pad = -(-S // span) * span
    n = s_pad // dilation
    nb = n // blk
    pad = ((0, 0), (0, s_pad - S), (0, 0), (0, 0))

    def to_blocks(t):
        t = jnp.pad(t, pad).reshape(B, n, dilation, H, E).transpose(0, 2, 1, 3, 4)
        return t.reshape(B * dilation, nb, blk, H, E)

    def with_prev(t):
        prev = jnp.pad(t, ((0, 0), (1, 0), (0, 0), (0, 0), (0, 0)))[:, :-1]
        return jnp.concatenate([prev, t], axis=2)

    qb, kb, vb = to_blocks(q), to_blocks(k), to_blocks(v)
    kk, vv = with_prev(kb), with_prev(vb)
    s = jnp.einsum('znqhe,znkhe->znhqk', qb, kk).astype(F32) * (E ** -0.5)
    qi = jnp.arange(blk)[:, None]
    kj = jnp.arange(2 * blk)[None, :]
    dist = qi + blk - kj
    in_band = (dist >= 0) & (dist <= blk)
    kabs = jnp.arange(nb)[:, None, None] * blk + kj[None] - blk
    valid = in_band[None] & (kabs >= 0)
    s = jnp.where(valid[None, :, None], s, -jnp.inf)
    m = jnp.max(s, axis=-1, keepdims=True)
    p = jnp.exp(s - m)
    l = jnp.sum(p, axis=-1, keepdims=True)
    o = jnp.einsum('znhqk,znkhe->znqhe', (p / l).astype(v.dtype), vv)
    lse = (m + jnp.log(l))[..., 0].transpose(0, 1, 3, 2)

    def from_blocks(t):
        t = t.reshape(B, dilation, n, *t.shape[3:]).swapaxes(1, 2)
        return t.reshape(B, s_pad, *t.shape[3:])[:, :S]

    return from_blocks(o), from_blocks(lse)


def dilated_attention_mixer(x, w_in, w_out):
    B, S, _ = x.shape
    proj = (x @ w_in).reshape(B, S, N_GROUPS, 3, ATTN_HEADS, ATTN_HEAD_DIM)
    pos = jnp.arange(S)
    outs, lses = [], []
    for g, (window, dilation) in enumerate(DILATED_PATTERNS):
        q = rotary(proj[:, :, g, 0], pos)
        k = rotary(proj[:, :, g, 1], pos)
        o, lse = dilated_window_attention(q, k, proj[:, :, g, 2], window, dilation)
        outs.append(o)
        lses.append(lse)
    wts = jax.nn.softmax(jnp.stack(lses, axis=0), axis=0)
    o = jnp.einsum('gbsh,gbshe->bshe', wts, jnp.stack(outs, axis=0).astype(F32))
    return o.reshape(B, S, ATTN_HEADS * ATTN_HEAD_DIM).astype(x.dtype) @ w_out


def forget_lower_bounds(lb_logits):
    c = jnp.cumsum(jax.nn.softmax(lb_logits.astype(F32), axis=0), axis=0)
    return c - c[0]


def hgrn2_mixer(x, w_in, w_out, norm_g, lb):
    B, S, _ = x.shape
    H, K, V, C = HGRN_HEADS, HGRN_DK, HGRN_DV, HGRN_CHUNK
    nc = S // C
    q_raw, f_raw, i_raw = jnp.split(x @ w_in, [H * K, 2 * H * K], axis=-1)
    z = f_raw.astype(F32)
    log_f = jnp.logaddexp(jnp.log(lb), jnp.log1p(-lb) + jax.nn.log_sigmoid(z))
    key = (1.0 - lb) * jax.nn.sigmoid(-z)
    q = jax.nn.silu(q_raw.astype(F32))
    v = i_raw.astype(F32)

    def chunks(t, d):
        return t.reshape(B, nc, C, H, d).transpose(0, 3, 1, 2, 4)

    q, key, log_f, v = chunks(q, K), chunks(key, K), chunks(log_f, K), chunks(v, V)
    b = jnp.cumsum(log_f, axis=3)
    q_dec = q * jnp.exp(b)
    k_dec = key * jnp.exp(-b)
    causal = jnp.tril(jnp.ones((C, C), dtype=bool))
    a = jnp.where(causal, jnp.einsum('bhcid,bhcjd->bhcij', q_dec, k_dec), 0.0)
    o_intra = jnp.einsum('bhcij,bhcje->bhcie', a, v)
    b_last = b[:, :, :, -1:, :]
    kv = jnp.einsum('bhcjd,bhcje->bhcde', key * jnp.exp(b_last - b), v)
    chunk_decay = jnp.exp(b_last[:, :, :, 0, :])

    def step(state, inp):
        dec, kv_c = inp
        return dec[..., None] * state + kv_c, state

    s0 = jnp.zeros((B, H, K, V), F32)
    _, states = lax.scan(step, s0, (jnp.moveaxis(chunk_decay, 2, 0), jnp.moveaxis(kv, 2, 0)))
    o_inter = jnp.einsum('bhcid,cbhde->bhcie', q_dec, states)
    o = (o_intra + o_inter).transpose(0, 2, 3, 1, 4).reshape(B, S, H, V)
    o = o * lax.rsqrt(jnp.mean(o * o, axis=-1, keepdims=True) + RMS_EPS) * norm_g.astype(F32).reshape(H, V)
    return o.reshape(B, S, H * V).astype(x.dtype) @ w_out


def squared_relu_mlp(x, w_up, w_down):
    return jnp.square(jax.nn.relu(x @ w_up)) @ w_down


def _fwd_setup_inputs(seed: int = 0) -> dict:
    key = jax.random.key(seed)
    ks = jax.random.split(key, 13)
    d_attn_in = N_GROUPS * 3 * ATTN_HEADS * ATTN_HEAD_DIM
    d_attn_out = ATTN_HEADS * ATTN_HEAD_DIM
    d_hgrn_in = 2 * HGRN_HEADS * HGRN_DK + HGRN_HEADS * HGRN_DV
    d_hgrn_out = HGRN_HEADS * HGRN_DV
    nrm = lambda k, shape, scale: jax.random.normal(k, shape, F32) * scale
    return {
        'x': nrm(ks[0], (BATCH, SEQ, D_MODEL), 1.0),
        'attn_w_in': nrm(ks[1], (N_ATTN_LAYERS, D_MODEL, d_attn_in), D_MODEL ** -0.5),
        'attn_w_out': nrm(ks[2], (N_ATTN_LAYERS, d_attn_out, D_MODEL), d_attn_out ** -0.5 * DEEPNORM_BETA),
        'hgrn_w_in': nrm(ks[3], (N_HGRN_LAYERS, D_MODEL, d_hgrn_in), D_MODEL ** -0.5),
        'hgrn_w_out': nrm(ks[4], (N_HGRN_LAYERS, d_hgrn_out, D_MODEL), d_hgrn_out ** -0.5 * DEEPNORM_BETA),
        'hgrn_norm_g': 1.0 + nrm(ks[5], (N_HGRN_LAYERS, d_hgrn_out), 0.02),
        'lb_logits': nrm(ks[6], (DEPTH, HGRN_HEADS * HGRN_DK), 0.1),
        'ln_mix_g': 1.0 + nrm(ks[7], (DEPTH, D_MODEL), 0.02),
        'ln_mix_b': nrm(ks[8], (DEPTH, D_MODEL), 0.02),
        'ln_ffn_g': 1.0 + nrm(ks[9], (DEPTH, D_MODEL), 0.02),
        'ln_ffn_b': nrm(ks[10], (DEPTH, D_MODEL), 0.02),
        'ffn_w_up': nrm(ks[11], (DEPTH, D_MODEL, D_FF), D_MODEL ** -0.5),
        'ffn_w_down': nrm(ks[12], (DEPTH, D_FF, D_MODEL), D_FF ** -0.5 * DEEPNORM_BETA),
    }


def _fwd_reference(x, attn_w_in, attn_w_out, hgrn_w_in, hgrn_w_out, hgrn_norm_g, lb_logits,
              ln_mix_g, ln_mix_b, ln_ffn_g, ln_ffn_b, ffn_w_up, ffn_w_down):
    lbs = forget_lower_bounds(lb_logits)
    for i in range(DEPTH):
        j = i // N_MIXERS
        if i % N_MIXERS == 0:
            y = dilated_attention_mixer(x, attn_w_in[j], attn_w_out[j])
        else:
            y = hgrn2_mixer(x, hgrn_w_in[j], hgrn_w_out[j], hgrn_norm_g[j], lbs[i])
        x = layer_norm(DEEPNORM_ALPHA * x + y, ln_mix_g[i], ln_mix_b[i])
        y = squared_relu_mlp(x, ffn_w_up[i], ffn_w_down[i])
        x = layer_norm(DEEPNORM_ALPHA * x + y, ln_ffn_g[i], ln_ffn_b[i])
    return x


import jax as _jax
import jax.numpy as _jnp

TWIN_FORMAT = 'train_step'
FWD_PARAMS = ['x', 'attn_w_in', 'attn_w_out', 'hgrn_w_in', 'hgrn_w_out', 'hgrn_norm_g', 'lb_logits', 'ln_mix_g', 'ln_mix_b', 'ln_ffn_g', 'ln_ffn_b', 'ffn_w_up', 'ffn_w_down']
TWIN_WEIGHTS = ['attn_w_in', 'attn_w_out', 'hgrn_w_in', 'hgrn_w_out', 'hgrn_norm_g', 'lb_logits', 'ln_mix_g', 'ln_mix_b', 'ln_ffn_g', 'ln_ffn_b', 'ffn_w_up', 'ffn_w_down']
TWIN_DIFF_INPUT = 'x'
TWIN_INPUTS = ['x', 'attn_w_in', 'attn_w_out', 'hgrn_w_in', 'hgrn_w_out', 'hgrn_norm_g', 'lb_logits', 'ln_mix_g', 'ln_mix_b', 'ln_ffn_g', 'ln_ffn_b', 'ffn_w_up', 'ffn_w_down', 'loss_target', 'm_attn_w_in', 'm_attn_w_out', 'm_hgrn_w_in', 'm_hgrn_w_out', 'm_hgrn_norm_g', 'm_lb_logits', 'm_ln_mix_g', 'm_ln_mix_b', 'm_ln_ffn_g', 'm_ln_ffn_b', 'm_ffn_w_up', 'm_ffn_w_down', 'v_attn_w_in', 'v_attn_w_out', 'v_hgrn_w_in', 'v_hgrn_w_out', 'v_hgrn_norm_g', 'v_lb_logits', 'v_ln_mix_g', 'v_ln_mix_b', 'v_ln_ffn_g', 'v_ln_ffn_b', 'v_ffn_w_up', 'v_ffn_w_down']
TWIN_OUTPUTS = ['loss', 'grad_x', 'grad_attn_w_in', 'grad_attn_w_out', 'grad_hgrn_w_in', 'grad_hgrn_w_out', 'grad_hgrn_norm_g', 'grad_lb_logits', 'grad_ln_mix_g', 'grad_ln_mix_b', 'grad_ln_ffn_g', 'grad_ln_ffn_b', 'grad_ffn_w_up', 'grad_ffn_w_down', 'delta_attn_w_in', 'delta_attn_w_out', 'delta_hgrn_w_in', 'delta_hgrn_w_out', 'delta_hgrn_norm_g', 'delta_lb_logits', 'delta_ln_mix_g', 'delta_ln_mix_b', 'delta_ln_ffn_g', 'delta_ln_ffn_b', 'delta_ffn_w_up', 'delta_ffn_w_down', 'new_m_attn_w_in', 'new_m_attn_w_out', 'new_m_hgrn_w_in', 'new_m_hgrn_w_out', 'new_m_hgrn_norm_g', 'new_m_lb_logits', 'new_m_ln_mix_g', 'new_m_ln_mix_b', 'new_m_ln_ffn_g', 'new_m_ln_ffn_b', 'new_m_ffn_w_up', 'new_m_ffn_w_down', 'new_v_attn_w_in', 'new_v_attn_w_out', 'new_v_hgrn_w_in', 'new_v_hgrn_w_out', 'new_v_hgrn_norm_g', 'new_v_lb_logits', 'new_v_ln_mix_g', 'new_v_ln_mix_b', 'new_v_ln_ffn_g', 'new_v_ln_ffn_b', 'new_v_ffn_w_up', 'new_v_ffn_w_down']
TWIN_LEAF_KINDS = {'loss': 'loss', 'grad_x': 'grad_x', 'grad_attn_w_in': 'grad_w', 'grad_attn_w_out': 'grad_w', 'grad_hgrn_w_in': 'grad_w', 'grad_hgrn_w_out': 'grad_w', 'grad_hgrn_norm_g': 'grad_w', 'grad_lb_logits': 'grad_w', 'grad_ln_mix_g': 'grad_w', 'grad_ln_mix_b': 'grad_w', 'grad_ln_ffn_g': 'grad_w', 'grad_ln_ffn_b': 'grad_w', 'grad_ffn_w_up': 'grad_w', 'grad_ffn_w_down': 'grad_w', 'delta_attn_w_in': 'delta_w', 'delta_attn_w_out': 'delta_w', 'delta_hgrn_w_in': 'delta_w', 'delta_hgrn_w_out': 'delta_w', 'delta_hgrn_norm_g': 'delta_w', 'delta_lb_logits': 'delta_w', 'delta_ln_mix_g': 'delta_w', 'delta_ln_mix_b': 'delta_w', 'delta_ln_ffn_g': 'delta_w', 'delta_ln_ffn_b': 'delta_w', 'delta_ffn_w_up': 'delta_w', 'delta_ffn_w_down': 'delta_w', 'new_m_attn_w_in': 'new_m', 'new_m_attn_w_out': 'new_m', 'new_m_hgrn_w_in': 'new_m', 'new_m_hgrn_w_out': 'new_m', 'new_m_hgrn_norm_g': 'new_m', 'new_m_lb_logits': 'new_m', 'new_m_ln_mix_g': 'new_m', 'new_m_ln_mix_b': 'new_m', 'new_m_ln_ffn_g': 'new_m', 'new_m_ln_ffn_b': 'new_m', 'new_m_ffn_w_up': 'new_m', 'new_m_ffn_w_down': 'new_m', 'new_v_attn_w_in': 'new_v', 'new_v_attn_w_out': 'new_v', 'new_v_hgrn_w_in': 'new_v', 'new_v_hgrn_w_out': 'new_v', 'new_v_hgrn_norm_g': 'new_v', 'new_v_lb_logits': 'new_v', 'new_v_ln_mix_g': 'new_v', 'new_v_ln_mix_b': 'new_v', 'new_v_ln_ffn_g': 'new_v', 'new_v_ln_ffn_b': 'new_v', 'new_v_ffn_w_up': 'new_v', 'new_v_ffn_w_down': 'new_v'}


def _forward(args):
    return _fwd_reference(*[args[k] for k in FWD_PARAMS])


def _output_shape():
    def fwd():
        inp = _fwd_setup_inputs(0)
        return _fwd_reference(*[inp[k] for k in FWD_PARAMS])
    out = _jax.eval_shape(fwd)
    return out.shape, out.dtype

N_MICROBATCH = 1
ADAM_LR = 0.001
ADAM_B1 = 0.9
ADAM_B2 = 0.999
ADAM_EPS = 1e-08
ADAM_WD = 0.01
ADAM_STEP = 10
PER_EXAMPLE_BATCH_AXIS = {'x': 0, 'loss_target': 0}
SHARED_INPUTS = []
_WEIGHT_DTYPES = {'attn_w_in': _jnp.float32, 'attn_w_out': _jnp.float32, 'hgrn_w_in': _jnp.float32, 'hgrn_w_out': _jnp.float32, 'hgrn_norm_g': _jnp.float32, 'lb_logits': _jnp.float32, 'ln_mix_g': _jnp.float32, 'ln_mix_b': _jnp.float32, 'ln_ffn_g': _jnp.float32, 'ln_ffn_b': _jnp.float32, 'ffn_w_up': _jnp.float32, 'ffn_w_down': _jnp.float32}
MOMENT_SCALE = {'attn_w_in': 6.762771e-03, 'attn_w_out': 2.628513e-02, 'hgrn_w_in': 5.308214e-02, 'hgrn_w_out': 1.825000e-01, 'hgrn_norm_g': 8.858374e-02, 'lb_logits': 5.614045e-03, 'ln_mix_g': 9.226381e-01, 'ln_mix_b': 6.875652e-01, 'ln_ffn_g': 2.274759e+01, 'ln_ffn_b': 5.865132e+00, 'ffn_w_up': 4.244252e-02, 'ffn_w_down': 2.445103e-01}


def _to_microbatches(a, axis):
    t = _jnp.moveaxis(a, axis, 0)
    t = t.reshape((N_MICROBATCH, t.shape[0] // N_MICROBATCH) + t.shape[1:])
    return _jnp.moveaxis(t, 1, axis + 1)


def setup_inputs(seed: int = 0) -> dict:
    inp = _fwd_setup_inputs(seed)
    key = _jax.random.fold_in(_jax.random.key(seed), 7919)
    shape, _ = _output_shape()
    out = dict(inp)
    out["loss_target"] = _jax.random.normal(_jax.random.fold_in(key, 0), shape, _jnp.float32)
    for i, name in enumerate(TWIN_WEIGHTS):
        w = inp[name].astype(_jnp.float32)
        if MOMENT_SCALE is None:
            s = _jnp.sqrt(_jnp.mean(_jnp.square(w)) + 1e-30)
        else:
            s = MOMENT_SCALE[name]
        km, kv = _jax.random.split(_jax.random.fold_in(key, i + 1))
        out[name] = w
        out["m_" + name] = s * _jax.random.normal(km, w.shape, _jnp.float32)
        out["v_" + name] = (s * s) * _jax.random.uniform(kv, w.shape, _jnp.float32, 0.5, 1.5)
    if N_MICROBATCH > 1:
        for name, axis in PER_EXAMPLE_BATCH_AXIS.items():
            out[name] = _to_microbatches(out[name], axis)
    return {'x': out['x'], 'attn_w_in': out['attn_w_in'], 'attn_w_out': out['attn_w_out'], 'hgrn_w_in': out['hgrn_w_in'], 'hgrn_w_out': out['hgrn_w_out'], 'hgrn_norm_g': out['hgrn_norm_g'], 'lb_logits': out['lb_logits'], 'ln_mix_g': out['ln_mix_g'], 'ln_mix_b': out['ln_mix_b'], 'ln_ffn_g': out['ln_ffn_g'], 'ln_ffn_b': out['ln_ffn_b'], 'ffn_w_up': out['ffn_w_up'], 'ffn_w_down': out['ffn_w_down'], 'loss_target': out['loss_target'], 'm_attn_w_in': out['m_attn_w_in'], 'm_attn_w_out': out['m_attn_w_out'], 'm_hgrn_w_in': out['m_hgrn_w_in'], 'm_hgrn_w_out': out['m_hgrn_w_out'], 'm_hgrn_norm_g': out['m_hgrn_norm_g'], 'm_lb_logits': out['m_lb_logits'], 'm_ln_mix_g': out['m_ln_mix_g'], 'm_ln_mix_b': out['m_ln_mix_b'], 'm_ln_ffn_g': out['m_ln_ffn_g'], 'm_ln_ffn_b': out['m_ln_ffn_b'], 'm_ffn_w_up': out['m_ffn_w_up'], 'm_ffn_w_down': out['m_ffn_w_down'], 'v_attn_w_in': out['v_attn_w_in'], 'v_attn_w_out': out['v_attn_w_out'], 'v_hgrn_w_in': out['v_hgrn_w_in'], 'v_hgrn_w_out': out['v_hgrn_w_out'], 'v_hgrn_norm_g': out['v_hgrn_norm_g'], 'v_lb_logits': out['v_lb_logits'], 'v_ln_mix_g': out['v_ln_mix_g'], 'v_ln_mix_b': out['v_ln_mix_b'], 'v_ln_ffn_g': out['v_ln_ffn_g'], 'v_ln_ffn_b': out['v_ln_ffn_b'], 'v_ffn_w_up': out['v_ffn_w_up'], 'v_ffn_w_down': out['v_ffn_w_down']}


def _loss(weights, diff, rest, loss_target):
    with _jax.named_scope("forward"):
        args = {**rest, TWIN_DIFF_INPUT: diff, **{k: w.astype(_WEIGHT_DTYPES[k]) for k, w in weights.items()}}
        y = _forward(args)
    with _jax.named_scope("loss_head"):
        err = _jnp.square(y.astype(_jnp.float32) - loss_target)
        return 0.5 * _jnp.sum(_jnp.mean(err, axis=-1)) if err.ndim else 0.5 * err


def _adamw(w, g, m, v):
    m = ADAM_B1 * m + (1.0 - ADAM_B1) * g
    v = ADAM_B2 * v + (1.0 - ADAM_B2) * _jnp.square(g)
    m_hat = m / (1.0 - ADAM_B1 ** ADAM_STEP)
    v_hat = v / (1.0 - ADAM_B2 ** ADAM_STEP)
    delta = -ADAM_LR * (m_hat / (_jnp.sqrt(v_hat) + ADAM_EPS) + ADAM_WD * w)
    return delta, m, v


def reference(x, attn_w_in, attn_w_out, hgrn_w_in, hgrn_w_out, hgrn_norm_g, lb_logits, ln_mix_g, ln_mix_b, ln_ffn_g, ln_ffn_b, ffn_w_up, ffn_w_down, loss_target, m_attn_w_in, m_attn_w_out, m_hgrn_w_in, m_hgrn_w_out, m_hgrn_norm_g, m_lb_logits, m_ln_mix_g, m_ln_mix_b, m_ln_ffn_g, m_ln_ffn_b, m_ffn_w_up, m_ffn_w_down, v_attn_w_in, v_attn_w_out, v_hgrn_w_in, v_hgrn_w_out, v_hgrn_norm_g, v_lb_logits, v_ln_mix_g, v_ln_mix_b, v_ln_ffn_g, v_ln_ffn_b, v_ffn_w_up, v_ffn_w_down):
    given = dict(x=x, attn_w_in=attn_w_in, attn_w_out=attn_w_out, hgrn_w_in=hgrn_w_in, hgrn_w_out=hgrn_w_out, hgrn_norm_g=hgrn_norm_g, lb_logits=lb_logits, ln_mix_g=ln_mix_g, ln_mix_b=ln_mix_b, ln_ffn_g=ln_ffn_g, ln_ffn_b=ln_ffn_b, ffn_w_up=ffn_w_up, ffn_w_down=ffn_w_down, loss_target=loss_target, m_attn_w_in=m_attn_w_in, m_attn_w_out=m_attn_w_out, m_hgrn_w_in=m_hgrn_w_in, m_hgrn_w_out=m_hgrn_w_out, m_hgrn_norm_g=m_hgrn_norm_g, m_lb_logits=m_lb_logits, m_ln_mix_g=m_ln_mix_g, m_ln_mix_b=m_ln_mix_b, m_ln_ffn_g=m_ln_ffn_g, m_ln_ffn_b=m_ln_ffn_b, m_ffn_w_up=m_ffn_w_up, m_ffn_w_down=m_ffn_w_down, v_attn_w_in=v_attn_w_in, v_attn_w_out=v_attn_w_out, v_hgrn_w_in=v_hgrn_w_in, v_hgrn_w_out=v_hgrn_w_out, v_hgrn_norm_g=v_hgrn_norm_g, v_lb_logits=v_lb_logits, v_ln_mix_g=v_ln_mix_g, v_ln_mix_b=v_ln_mix_b, v_ln_ffn_g=v_ln_ffn_g, v_ln_ffn_b=v_ln_ffn_b, v_ffn_w_up=v_ffn_w_up, v_ffn_w_down=v_ffn_w_down)
    weights = {n: given[n] for n in TWIN_WEIGHTS}
    shared = {n: given[n] for n in SHARED_INPUTS}
    per_example = {n: given[n] for n in ['x']}
    grad_fn = _jax.value_and_grad(_loss, argnums=(0, 1))

    def one_microbatch(ex, loss_target):
        ex = dict(ex)
        diff = ex.pop(TWIN_DIFF_INPUT)
        return grad_fn(weights, diff, {**shared, **ex}, loss_target)

    if N_MICROBATCH == 1:
        loss, (grad_w, grad_x) = one_microbatch(per_example, given["loss_target"])
    else:
        def body(carry, xs):
            loss_sum, grad_sum = carry
            l_k, (gw_k, gx_k) = one_microbatch(xs[0], xs[1])
            with _jax.named_scope("update"):
                return (loss_sum + l_k, _jax.tree.map(_jnp.add, grad_sum, gw_k)), gx_k

        init = (_jnp.zeros((), _jnp.float32), _jax.tree.map(_jnp.zeros_like, weights))
        (loss, grad_w), grad_x = _jax.lax.scan(body, init, (per_example, given["loss_target"]))
    with _jax.named_scope("update"):
        delta_w, new_m, new_v = {}, {}, {}
        for n in TWIN_WEIGHTS:
            delta_w[n], new_m[n], new_v[n] = _adamw(weights[n], grad_w[n], given["m_" + n], given["v_" + n])
    return (loss, grad_x, *[grad_w[n] for n in TWIN_WEIGHTS], *[delta_w[n] for n in TWIN_WEIGHTS],
            *[new_m[n] for n in TWIN_WEIGHTS], *[new_v[n] for n in TWIN_WEIGHTS])
```

```python
import functools
import math

import jax
import jax.numpy as jnp
from jax import lax
from jax.experimental import pallas as pl
from jax.experimental.pallas import tpu as pltpu

F32 = jnp.float32
BF16 = jnp.bfloat16
MXU_DTYPE = BF16

HEAD_DIM = 64
ATTN_BLK = 128
DILATIONS = (1, 4, 16)
ROPE_THETA = 10000.0
HGRN_DK = 128
HGRN_CHUNK = 64
DEPTH = 2
LN_EPS = 1e-5
RMS_EPS = 1e-6
ALPHA = (2 * DEPTH) ** 0.25
ADAM_LR, ADAM_B1, ADAM_B2, ADAM_EPS, ADAM_WD, ADAM_STEP = 0.001, 0.9, 0.999, 1e-08, 0.01, 10

LANES = 128
VMEM_LIMIT = 56 * 1024 * 1024
NEG = -1e30
MESH = pl.DeviceIdType.MESH


def _cparams(sem=None):
    return pltpu.CompilerParams(dimension_semantics=sem, vmem_limit_bytes=VMEM_LIMIT)


def _sds(shape, dtype):
    return jax.ShapeDtypeStruct(tuple(shape), dtype)


def _dg(a, b, ca, cb):
    return lax.dot_general(a, b, (((ca,), (cb,)), ((), ())), preferred_element_type=F32)


def _nn(a, b):
    return _dg(a, b, 1, 0)


def _nt(a, b):
    return _dg(a, b, 1, 1)


def _tn(a, b):
    return _dg(a, b, 0, 0)


def _split3(a):
    hi = a.astype(BF16)
    r = a - hi.astype(F32)
    mid = r.astype(BF16)
    lo = (r - mid.astype(F32)).astype(BF16)
    return hi, mid, lo


def _exact_nn(a, sel):
    hi, mid, lo = _split3(a)
    return _nn(hi, sel) + _nn(mid, sel) + _nn(lo, sel)


def _exact_sel_nn(sel, a):
    hi, mid, lo = _split3(a)
    return _nn(sel, hi) + _nn(sel, mid) + _nn(sel, lo)


def _pick(n, prefs):
    for p in prefs:
        if n % p == 0:
            return p
    return n


def _matmul(name, a, b, form, tm, tn, tk, outs, epilogue, extras=()):
    if form == "nn":
        (M, K), N = a.shape, b.shape[1]
        a_spec = pl.BlockSpec((tm, tk), lambda i, j, k: (i, k))
        b_spec = pl.BlockSpec((tk, tn), lambda i, j, k: (k, j))
        ca, cb = 1, 0
    elif form == "nt":
        (M, K), N = a.shape, b.shape[0]
        a_spec = pl.BlockSpec((tm, tk), lambda i, j, k: (i, k))
        b_spec = pl.BlockSpec((tn, tk), lambda i, j, k: (j, k))
        ca, cb = 1, 1
    else:
        (K, M), N = a.shape, b.shape[1]
        a_spec = pl.BlockSpec((tk, tm), lambda i, j, k: (k, i))
        b_spec = pl.BlockSpec((tk, tn), lambda i, j, k: (k, j))
        ca, cb = 0, 0
    assert M % tm == 0 and N % tn == 0 and K % tk == 0, (name, M, N, K, tm, tn, tk)
    nk = K // tk
    ne, no = len(extras), len(outs)

    def body(a_ref, b_ref, *rest):
        extra_refs, out_refs = rest[:ne], rest[ne:ne + no]
        j = pl.program_id(1)
        part = _dg(a_ref[...].astype(MXU_DTYPE), b_ref[...].astype(MXU_DTYPE), ca, cb)
        if nk == 1:
            epilogue(part, extra_refs, out_refs, j)
            return
        acc_ref = rest[-1]
        k = pl.program_id(2)

        @pl.when(k == 0)
        def _():
            acc_ref[...] = part

        @pl.when(k > 0)
        def _():
            acc_ref[...] += part

        @pl.when(k == nk - 1)
        def _():
            epilogue(acc_ref[...], extra_refs, out_refs, j)

    res = pl.pallas_call(
        body,
        name=name,
        grid=(M // tm, N // tn, nk),
        in_specs=[a_spec, b_spec] + [s for _, s in extras],
        out_specs=[s for _, s in outs],
        out_shape=[o for o, _ in outs],
        scratch_shapes=[pltpu.VMEM((tm, tn), F32)] if nk > 1 else [],
        compiler_params=_cparams(("parallel", "parallel", "arbitrary")),
    )(a, b, *[e for e, _ in extras])
    return res


def _ij_spec(tm, tn):
    return pl.BlockSpec((tm, tn), lambda i, j, k: (i, j))


def _store_epilogue(acc, extra_refs, out_refs, j):
    out_refs[0][...] = acc.astype(out_refs[0].dtype)


def _plain_mm(name, a, b, form, out_dtype, tm, tn, tk):
    M = a.shape[1] if form == "tn" else a.shape[0]
    N = b.shape[0] if form == "nt" else b.shape[1]
    return _matmul(name, a, b, form, tm, tn, tk, [(_sds((M, N), out_dtype), _ij_spec(tm, tn))], _store_epilogue)[0]


def _rope_tables(seq):
    half = HEAD_DIM // 2
    inv = ROPE_THETA ** (-jnp.arange(half, dtype=F32) * (2.0 / HEAD_DIM))
    ang = jnp.arange(seq).astype(F32)[:, None] * inv[None, :]
    reps = LANES // half
    cos = jnp.tile(jnp.cos(ang), (1, reps))
    sin = jnp.tile(jnp.sin(ang), (1, reps))
    first = (jnp.arange(LANES) % HEAD_DIM) < half
    sin_fwd = jnp.where(first[None, :], -sin, sin)
    return cos, sin_fwd


def _partner(x):
    half = HEAD_DIM // 2
    lane = lax.broadcasted_iota(jnp.int32, x.shape, 1)
    first = (lane % HEAD_DIM) < half
    return jnp.where(first, pltpu.roll(x, LANES - half, 1), pltpu.roll(x, half, 1))


def _attn_proj(x, w_full, cos, sin_fwd, tm, tn):
    S, D = x.shape
    N = w_full.shape[1]
    per_part = D // tn

    def epilogue(acc, extra_refs, out_refs, j):
        cos_ref, sin_ref = extra_refs
        o_ref = out_refs[0]
        is_rot = (j // per_part) % 3 < 2

        @pl.when(is_rot)
        def _():
            c, s = cos_ref[...], sin_ref[...]
            for t in range(tn // LANES):
                xs = acc[:, t * LANES:(t + 1) * LANES]
                o_ref[:, t * LANES:(t + 1) * LANES] = (xs * c + _partner(xs) * s).astype(o_ref.dtype)

        @pl.when(jnp.logical_not(is_rot))
        def _():
            o_ref[...] = acc.astype(o_ref.dtype)

    tab = pl.BlockSpec((tm, LANES), lambda i, j, k: (i, 0))
    return _matmul("attn_proj", x, w_full, "nn", tm, tn, D, [(_sds((S, N), MXU_DTYPE), _ij_spec(tm, tn))],
                   epilogue, extras=[(cos, tab), (sin_fwd, tab)])[0]


def _head_sel(d_model):
    h = jnp.arange(LANES)[:, None]
    l = jnp.arange(d_model)[None, :]
    return (l // HEAD_DIM == h).astype(BF16)


def _attn_fwd_group(P, g, dil, D):
    S = P.shape[0]
    n = S // dil
    nb = n // ATTN_BLK
    npairs = D // LANES
    scale = HEAD_DIM ** -0.5
    Pv = P.reshape(n, dil * 9 * D)
    col = lambda r, part: r * 9 + g * 3 + part

    def body(q_ref, kc_ref, vc_ref, kp_ref, vp_ref, o_ref, lse_ref):
        ib = pl.program_id(1)
        has_prev = ib > 0
        row = lax.broadcasted_iota(jnp.int32, (ATTN_BLK, ATTN_BLK), 0)
        colm = lax.broadcasted_iota(jnp.int32, (ATTN_BLK, ATTN_BLK), 1)
        lane = lax.broadcasted_iota(jnp.int32, (ATTN_BLK, LANES), 1)
        ok_c = row >= colm
        ok_p = jnp.logical_and(colm >= row, has_prev)
        lse_acc = jnp.zeros((ATTN_BLK, LANES), F32)
        for j in range(npairs):
            sl = slice(j * LANES, (j + 1) * LANES)
            q2 = q_ref[:, sl] * scale
            kc2, vc2, kp2, vp2 = kc_ref[:, sl], vc_ref[:, sl], kp_ref[:, sl], vp_ref[:, sl]
            o2 = jnp.zeros((ATTN_BLK, LANES), F32)
            for h in range(2):
                mh = (lane < HEAD_DIM) if h == 0 else (lane >= HEAD_DIM)
                qm = jnp.where(mh, q2, jnp.zeros_like(q2))
                s_c = jnp.where(ok_c, _nt(qm, kc2), NEG)
                s_p = jnp.where(ok_p, _nt(qm, kp2), NEG)
                m = jnp.maximum(jnp.max(s_c, axis=1, keepdims=True), jnp.max(s_p, axis=1, keepdims=True))
                p_c = jnp.exp(s_c - m)
                p_p = jnp.exp(s_p - m)
                l = jnp.sum(p_c, axis=1, keepdims=True) + jnp.sum(p_p, axis=1, keepdims=True)
                inv = 1.0 / l
                o_h = _nn((p_c * inv).astype(MXU_DTYPE), vc2) + _nn((p_p * inv).astype(MXU_DTYPE), vp2)
                o2 = jnp.where(mh, o_h, o2)
                lse_acc = jnp.where(lane == 2 * j + h, m + jnp.log(l), lse_acc)
            o_ref[:, sl] = o2
        lse_ref[...] = lse_acc

    blk = lambda part, prev: pl.BlockSpec(
        (ATTN_BLK, D), (lambda r, ib: (jnp.maximum(ib - 1, 0), col(r, part))) if prev else (lambda r, ib: (ib, col(r, part))))
    o, lse = pl.pallas_call(
        body,
        name=f"attn_fwd_g{g}",
        grid=(dil, nb),
        in_specs=[blk(0, False), blk(1, False), blk(2, False), blk(1, True), blk(2, True)],
        out_specs=[pl.BlockSpec((ATTN_BLK, D), lambda r, ib: (ib, r)),
                   pl.BlockSpec((ATTN_BLK, LANES), lambda r, ib: (ib, r))],
        out_shape=[_sds((n, dil * D), F32), _sds((n, dil * LANES), F32)],
        compiler_params=_cparams(("parallel", "arbitrary")),
    )(Pv, Pv, Pv, Pv, Pv)
    return o.reshape(S, D), lse.reshape(S, LANES)


def _attn_mix(os, lses, sel, tm):
    S, D = os[0].shape

    def body(o0, o1, o2, l0, l1, l2, sel_ref, o_ref, L_ref):
        a, b, c = l0[...], l1[...], l2[...]
        m = jnp.maximum(jnp.maximum(a, b), c)
        L = m + jnp.log(jnp.exp(a - m) + jnp.exp(b - m) + jnp.exp(c - m))
        L_ref[...] = L
        s = sel_ref[...]
        acc = _exact_nn(jnp.exp(a - L), s) * o0[...]
        acc += _exact_nn(jnp.exp(b - L), s) * o1[...]
        acc += _exact_nn(jnp.exp(c - L), s) * o2[...]
        o_ref[...] = acc

    big = pl.BlockSpec((tm, D), lambda i: (i, 0))
    small = pl.BlockSpec((tm, LANES), lambda i: (i, 0))
    return pl.pallas_call(
        body,
        name="attn_mix",
        grid=(S // tm,),
        in_specs=[big, big, big, small, small, small, pl.BlockSpec((LANES, D), lambda i: (0, 0))],
        out_specs=[big, small],
        out_shape=[_sds((S, D), F32), _sds((S, LANES), F32)],
        compiler_params=_cparams(("parallel",)),
    )(*os, *lses, sel)


def _attn_bwd_group(P, do, L, delta, cos, sin_fwd, dP_in, g, dil, D):
    S = P.shape[0]
    n = S // dil
    nb = n // ATTN_BLK
    npairs = D // LANES
    scale = HEAD_DIM ** -0.5
    Pv = P.reshape(n, dil * 9 * D)
    dov = do.reshape(n, dil * D)
    Lv = L.reshape(n, dil * LANES)
    dlv = delta.reshape(n, dil * LANES)
    cosv = cos.reshape(n, dil * LANES)
    sinv = sin_fwd.reshape(n, dil * LANES)

    def body(c_ref, p_ref, n_ref, doc_ref, don_ref, Lc_ref, Ln_ref, dc_ref, dn_ref, cos_ref, sin_ref, *rest):
        out_ref = rest[-1]
        ib = pl.program_id(1)
        has_prev = ib > 0
        has_next = ib < nb - 1
        row = lax.broadcasted_iota(jnp.int32, (ATTN_BLK, ATTN_BLK), 0)
        colm = lax.broadcasted_iota(jnp.int32, (ATTN_BLK, ATTN_BLK), 1)
        lane = lax.broadcasted_iota(jnp.int32, (ATTN_BLK, LANES), 1)
        ok_c = row >= colm
        ok_p = jnp.logical_and(colm >= row, has_prev)
        ok_n = jnp.logical_and(colm >= row, has_next)
        cos_t = cos_ref[...]
        sin_inv = -sin_ref[...]
        Lc_all, Ln_all, dc_all, dn_all = Lc_ref[...], Ln_ref[...], dc_ref[...], dn_ref[...]
        for j in range(npairs):
            sl = lambda part: slice(part * D + j * LANES, part * D + (j + 1) * LANES)
            qc2 = c_ref[:, sl(0)] * scale
            kc2, vc2 = c_ref[:, sl(1)], c_ref[:, sl(2)]
            kp2, vp2 = p_ref[:, sl(1)], p_ref[:, sl(2)]
            qn2 = n_ref[:, sl(0)] * scale
            doc2 = doc_ref[:, j * LANES:(j + 1) * LANES].astype(MXU_DTYPE)
            don2 = don_ref[:, j * LANES:(j + 1) * LANES].astype(MXU_DTYPE)
            dq2 = jnp.zeros((ATTN_BLK, LANES), F32)
            dk2 = jnp.zeros((ATTN_BLK, LANES), F32)
            dv2 = jnp.zeros((ATTN_BLK, LANES), F32)
            for h in range(2):
                mh = (lane < HEAD_DIM) if h == 0 else (lane >= HEAD_DIM)
                zero = jnp.zeros_like(qc2)
                qm, qnm = jnp.where(mh, qc2, zero), jnp.where(mh, qn2, zero)
                dom, donm = jnp.where(mh, doc2, zero), jnp.where(mh, don2, zero)
                hh = 2 * j + h
                Lc, Ln = Lc_all[:, hh:hh + 1], Ln_all[:, hh:hh + 1]
                dlc, dln = dc_all[:, hh:hh + 1], dn_all[:, hh:hh + 1]
                P_c = jnp.where(ok_c, jnp.exp(_nt(qm, kc2) - Lc), 0.0)
                P_p = jnp.where(ok_p, jnp.exp(_nt(qm, kp2) - Lc), 0.0)
                P_n = jnp.where(ok_n, jnp.exp(_nt(qnm, kc2) - Ln), 0.0)
                dS_c = (P_c * (_nt(dom, vc2) - dlc)).astype(MXU_DTYPE)
                dS_p = (P_p * (_nt(dom, vp2) - dlc)).astype(MXU_DTYPE)
                dS_n = (P_n * (_nt(donm, vc2) - dln)).astype(MXU_DTYPE)
                dq_h = _nn(dS_c, kc2) + _nn(dS_p, kp2)
                dk_h = _tn(dS_c, qm) + _tn(dS_n, qnm)
                dv_h = _tn(P_c.astype(MXU_DTYPE), dom) + _tn(P_n.astype(MXU_DTYPE), donm)
                dq2 = jnp.where(mh, dq_h, dq2)
                dk2 = jnp.where(mh, dk_h, dk2)
                dv2 = jnp.where(mh, dv_h, dv2)
            dq2 = dq2 * scale
            out_ref[:, sl(0)] = (dq2 * cos_t + _partner(dq2) * sin_inv).astype(out_ref.dtype)
            out_ref[:, sl(1)] = (dk2 * cos_t + _partner(dk2) * sin_inv).astype(out_ref.dtype)
            out_ref[:, sl(2)] = dv2.astype(out_ref.dtype)

    cur = lambda r, ib: ib
    prv = lambda r, ib: jnp.maximum(ib - 1, 0)
    nxt = lambda r, ib: jnp.minimum(ib + 1, nb - 1)
    qkv = lambda f: pl.BlockSpec((ATTN_BLK, 3 * D), lambda r, ib: (f(r, ib), r * 3 + g))
    wide = lambda f: pl.BlockSpec((ATTN_BLK, D), lambda r, ib: (f(r, ib), r))
    slim = lambda f: pl.BlockSpec((ATTN_BLK, LANES), lambda r, ib: (f(r, ib), r))
    in_specs = [qkv(cur), qkv(prv), qkv(nxt), wide(cur), wide(nxt), slim(cur), slim(nxt), slim(cur), slim(nxt),
                slim(cur), slim(cur)]
    args = [Pv, Pv, Pv, dov, dov, Lv, Lv, dlv, dlv, cosv, sinv]
    aliases = {}
    if dP_in is not None:
        in_specs.append(pl.BlockSpec(memory_space=pl.ANY))
        args.append(dP_in.reshape(n, dil * 9 * D))
        aliases = {len(args) - 1: 0}
    dP = pl.pallas_call(
        body,
        name=f"attn_bwd_g{g}",
        grid=(dil, nb),
        in_specs=in_specs,
        out_specs=qkv(cur),
        out_shape=_sds((n, dil * 9 * D), MXU_DTYPE),
        input_output_aliases=aliases,
        compiler_params=_cparams(("parallel", "arbitrary")),
    )(*args)
    return dP.reshape(S, 9 * D)


def _tri(lower):
    r = lax.broadcasted_iota(jnp.int32, (HGRN_CHUNK, HGRN_CHUNK), 0)
    c = lax.broadcasted_iota(jnp.int32, (HGRN_CHUNK, HGRN_CHUNK), 1)
    return ((r >= c) if lower else (r <= c)).astype(BF16)


def _lower_bound(lb_ref):
    l0, l1 = lb_ref[0:1, :], lb_ref[1:2, :]
    m = jnp.maximum(l0, l1)
    e0, e1 = jnp.exp(l0 - m), jnp.exp(l1 - m)
    return e1 / (e0 + e1)


def _hgrn_gates(q_raw, z, lb):
    sg = 1.0 / (1.0 + jnp.exp(-z))
    sn = 1.0 / (1.0 + jnp.exp(z))
    f = lb + (1.0 - lb) * sg
    key = (1.0 - lb) * sn
    sq = 1.0 / (1.0 + jnp.exp(-q_raw))
    return sg, sn, f, key, sq


def _hgrn_fwd(P1, lb_logits, norm_g, tb):
    S = P1.shape[0]
    D = P1.shape[1] // 3
    H = D // HGRN_DK
    C = HGRN_CHUNK
    cpb = tb // C
    nt = S // tb

    def body(q_ref, f_ref, i_ref, lb_ref, g_ref, o_ref, n_ref, st_ref, state):
        t = pl.program_id(1)

        @pl.when(t == 0)
        def _():
            state[...] = jnp.zeros_like(state)

        lb = _lower_bound(lb_ref)
        gn = g_ref[...]
        tri = _tri(True)
        r = lax.broadcasted_iota(jnp.int32, (C, C), 0)
        c = lax.broadcasted_iota(jnp.int32, (C, C), 1)
        causal = r >= c
        for ci in range(cpb):
            rows = slice(ci * C, (ci + 1) * C)
            q_raw, z, v = q_ref[rows, :], f_ref[rows, :], i_ref[rows, :]
            sg, sn, f, key, sq = _hgrn_gates(q_raw, z, lb)
            q = q_raw * sq
            b = _exact_sel_nn(tri, jnp.log(f))
            b_last = b[C - 1:C, :]
            qd = (q * jnp.exp(b)).astype(MXU_DTYPE)
            kd = (key * jnp.exp(-b)).astype(MXU_DTYPE)
            kb = (key * jnp.exp(b_last - b)).astype(MXU_DTYPE)
            vm = v.astype(MXU_DTYPE)
            st = state[...]
            st_ref[ci] = st
            a = jnp.where(causal, _nt(qd, kd), 0.0)
            o = _nn(a.astype(MXU_DTYPE), vm) + _nt(qd, st.astype(MXU_DTYPE))
            state[...] = st * jnp.exp(b_last) + _tn(vm, kb)
            o_ref[rows, :] = o
            rs = lax.rsqrt(jnp.mean(o * o, axis=1, keepdims=True) + RMS_EPS)
            n_ref[rows, :] = o * rs * gn

    tok = lambda part: pl.BlockSpec((tb, HGRN_DK), lambda h, t: (t, part * H + h))
    vec = lambda rows: pl.BlockSpec((rows, HGRN_DK), lambda h, t: (0, h))
    return pl.pallas_call(
        body,
        name="hgrn_fwd",
        grid=(H, nt),
        in_specs=[tok(0), tok(1), tok(2), vec(2), vec(1)],
        out_specs=[tok(0), tok(0), pl.BlockSpec((None, cpb, HGRN_DK, HGRN_DK), lambda h, t: (h, t, 0, 0))],
        out_shape=[_sds((S, D), F32), _sds((S, D), F32), _sds((H, S // C, HGRN_DK, HGRN_DK), F32)],
        scratch_shapes=[pltpu.VMEM((HGRN_DK, HGRN_DK), F32)],
        compiler_params=_cparams(("parallel", "arbitrary")),
    )(P1, P1, P1, lb_logits, norm_g)


def _hgrn_bwd(P1, o_pre, states, dn, lb_logits, norm_g, tb):
    S = P1.shape[0]
    D = P1.shape[1] // 3
    H = D // HGRN_DK
    C = HGRN_CHUNK
    cpb = tb // C
    nt = S // tb

    def body(q_ref, f_ref, i_ref, o_ref, st_ref, dn_ref, lb_ref, g_ref, dq_ref, dz_ref, dv_ref, dg_ref, dlb_ref, dstate):
        t = pl.program_id(1)

        @pl.when(t == 0)
        def _():
            dstate[...] = jnp.zeros_like(dstate)
            dg_ref[...] = jnp.zeros_like(dg_ref)
            dlb_ref[...] = jnp.zeros_like(dlb_ref)

        lb = _lower_bound(lb_ref)
        gn = g_ref[...]
        tri_l, tri_u = _tri(True), _tri(False)
        r = lax.broadcasted_iota(jnp.int32, (C, C), 0)
        c = lax.broadcasted_iota(jnp.int32, (C, C), 1)
        causal = r >= c
        last_row = lax.broadcasted_iota(jnp.int32, (C, HGRN_DK), 0) == C - 1
        dg_acc = jnp.zeros((1, HGRN_DK), F32)
        dlb_acc = jnp.zeros((1, HGRN_DK), F32)
        for ci in reversed(range(cpb)):
            rows = slice(ci * C, (ci + 1) * C)
            q_raw, z, v = q_ref[rows, :], f_ref[rows, :], i_ref[rows, :]
            sg, sn, f, key, sq = _hgrn_gates(q_raw, z, lb)
            q = q_raw * sq
            b = _exact_sel_nn(tri_l, jnp.log(f))
            b_last = b[C - 1:C, :]
            e_pos, e_neg, e_rel = jnp.exp(b), jnp.exp(-b), jnp.exp(b_last - b)
            dec = jnp.exp(b_last)
            qd_f, kd_f, kb_f = q * e_pos, key * e_neg, key * e_rel
            qd, kd, kb = qd_f.astype(MXU_DTYPE), kd_f.astype(MXU_DTYPE), kb_f.astype(MXU_DTYPE)
            vm = v.astype(MXU_DTYPE)
            st = st_ref[ci]
            dst = dstate[...]
            stm, dstm = st.astype(MXU_DTYPE), dst.astype(MXU_DTYPE)
            a = jnp.where(causal, _nt(qd, kd), 0.0).astype(MXU_DTYPE)
            o = o_ref[rows, :]
            dnn = dn_ref[rows, :]
            rs = lax.rsqrt(jnp.mean(o * o, axis=1, keepdims=True) + RMS_EPS)
            dg_acc += jnp.sum(dnn * o * rs, axis=0, keepdims=True)
            tg = dnn * gn
            do_f = rs * tg - o * (rs * rs * rs) * jnp.mean(tg * o, axis=1, keepdims=True)
            dom = do_f.astype(MXU_DTYPE)
            da = jnp.where(causal, _nt(dom, vm), 0.0).astype(MXU_DTYPE)
            dv = _tn(a, dom) + _nt(kb, dstm)
            dqd = _nn(da, kd) + _nn(dom, stm)
            dkd = _tn(da, qd)
            dkb = _nn(vm, dstm)
            ddec = jnp.sum(dst * st, axis=0, keepdims=True)
            dstate[...] = dst * dec + _tn(dom, qd)
            dq = dqd * e_pos
            dkey = dkd * e_neg + dkb * e_rel
            tk = dkb * kb_f
            db_last = jnp.sum(tk, axis=0, keepdims=True) + ddec * dec
            db = dqd * qd_f - dkd * kd_f - tk
            db = jnp.where(last_row, db + db_last, db)
            dlogf = _exact_sel_nn(tri_u, db)
            gz = (1.0 - lb) * sg * sn
            dz_ref[rows, :] = (dlogf * gz / f - dkey * gz).astype(dz_ref.dtype)
            dlb_acc += jnp.sum(dlogf * sn / f - dkey * sn, axis=0, keepdims=True)
            dq_ref[rows, :] = (dq * (sq + q_raw * sq * (1.0 - sq))).astype(dq_ref.dtype)
            dv_ref[rows, :] = dv.astype(dv_ref.dtype)
        dg_ref[...] += dg_acc
        dlb_ref[...] += dlb_acc

    rev = lambda t: nt - 1 - t
    tok = lambda part: pl.BlockSpec((tb, HGRN_DK), lambda h, t: (rev(t), part * H + h))
    vec = lambda rows: pl.BlockSpec((rows, HGRN_DK), lambda h, t: (0, h))
    outs = pl.pallas_call(
        body,
        name="hgrn_bwd",
        grid=(H, nt),
        in_specs=[tok(0), tok(1), tok(2), tok(0),
                  pl.BlockSpec((None, cpb, HGRN_DK, HGRN_DK), lambda h, t: (h, rev(t), 0, 0)),
                  tok(0), vec(2), vec(1)],
        out_specs=[tok(0), tok(0), tok(0), vec(1), vec(1)],
        out_shape=[_sds((S, D), MXU_DTYPE)] * 3 + [_sds((1, D), F32)] * 2,
        scratch_shapes=[pltpu.VMEM((HGRN_DK, HGRN_DK), F32)],
        compiler_params=_cparams(("parallel", "arbitrary")),
    )(P1, P1, P1, o_pre, states, dn, lb_logits, norm_g)
    return outs


def _lb_logits_grad(dlb, lb_logits):
    def body(d_ref, l_ref, o_ref):
        s1 = _lower_bound(l_ref)
        d = d_ref[...]
        o_ref[0:1, :] = -(1.0 - s1) * s1 * d
        o_ref[1:2, :] = s1 * (1.0 - s1) * d

    return pl.pallas_call(body, name="lb_logits_grad", out_shape=_sds(lb_logits.shape, F32))(dlb, lb_logits)


def _ln_epilogue(acc, extra_refs, out_refs, j):
    res_ref, g_ref, b_ref = extra_refs
    x_ref, xhat_ref, rstd_ref = out_refs
    u = ALPHA * res_ref[...] + acc
    mu = jnp.mean(u, axis=1, keepdims=True)
    cen = u - mu
    rstd = lax.rsqrt(jnp.mean(cen * cen, axis=1, keepdims=True) + LN_EPS)
    xhat = cen * rstd
    xhat_ref[...] = xhat
    x_ref[...] = xhat * g_ref[...] + b_ref[...]
    rstd_ref[...] = rstd


def _mm_res_ln(name, a, w_full, res, g, b, tm, tk):
    S, D = res.shape
    row = pl.BlockSpec((tm, D), lambda i, j, k: (i, 0))
    vec = pl.BlockSpec((1, D), lambda i, j, k: (0, 0))
    outs = [(_sds((S, D), F32), row), (_sds((S, D), F32), row),
            (_sds((S, 1), F32), pl.BlockSpec((tm, 1), lambda i, j, k: (i, 0)))]
    return _matmul(name, a, w_full, "nn", tm, D, tk, outs, _ln_epilogue, extras=[(res, row), (g, vec), (b, vec)])


def _ln_bwd(name, dy, xhat, rstd, g, tm):
    S, D = dy.shape

    def body(dy_ref, xh_ref, r_ref, g_ref, du_ref, dg_ref, db_ref):
        @pl.when(pl.program_id(0) == 0)
        def _():
            dg_ref[...] = jnp.zeros_like(dg_ref)
            db_ref[...] = jnp.zeros_like(db_ref)

        dy_, xh = dy_ref[...], xh_ref[...]
        dg_ref[...] += jnp.sum(dy_ * xh, axis=0, keepdims=True)
        db_ref[...] += jnp.sum(dy_, axis=0, keepdims=True)
        dxh = dy_ * g_ref[...]
        m1 = jnp.mean(dxh, axis=1, keepdims=True)
        m2 = jnp.mean(dxh * xh, axis=1, keepdims=True)
        du_ref[...] = r_ref[...] * (dxh - m1 - xh * m2)

    row = pl.BlockSpec((tm, D), lambda i: (i, 0))
    vec = pl.BlockSpec((1, D), lambda i: (0, 0))
    return pl.pallas_call(
        body,
        name=name,
        grid=(S // tm,),
        in_specs=[row, row, pl.BlockSpec((tm, 1), lambda i: (i, 0)), vec],
        out_specs=[row, vec, vec],
        out_shape=[_sds((S, D), F32), _sds((1, D), F32), _sds((1, D), F32)],
        compiler_params=_cparams(("arbitrary",)),
    )(dy, xhat, rstd, g)


def _loss_head(y, target, tm):
    S, D = y.shape

    def body(y_ref, t_ref, sq_ref, dy_ref):
        @pl.when(pl.program_id(0) == 0)
        def _():
            sq_ref[...] = jnp.zeros_like(sq_ref)

        e = y_ref[...] - t_ref[...]
        sq_ref[...] += jnp.sum(e * e, axis=0, keepdims=True)
        dy_ref[...] = e / D

    row = pl.BlockSpec((tm, D), lambda i: (i, 0))
    vec = pl.BlockSpec((1, D), lambda i: (0, 0))
    return pl.pallas_call(
        body,
        name="loss_head",
        grid=(S // tm,),
        in_specs=[row, row],
        out_specs=[vec, row],
        out_shape=[_sds((1, D), F32), _sds((S, D), F32)],
        compiler_params=_cparams(("arbitrary",)),
    )(y, target)


def _mlp_up(name, x, w_up, tm, tn, tk):
    S = x.shape[0]
    F = w_up.shape[1]

    def epilogue(acc, extra_refs, out_refs, j):
        out_refs[0][...] = acc
        r = jnp.maximum(acc, 0.0)
        out_refs[1][...] = (r * r).astype(out_refs[1].dtype)

    return _matmul(name, x, w_up, "nn", tm, tn, tk,
                   [(_sds((S, F), F32), _ij_spec(tm, tn)), (_sds((S, F), MXU_DTYPE), _ij_spec(tm, tn))], epilogue)


def _mlp_down_bwd(name, dy, w_down, h, tm, tn, tk):
    S, F = h.shape

    def epilogue(acc, extra_refs, out_refs, j):
        out_refs[0][...] = (acc * (2.0 * jnp.maximum(extra_refs[0][...], 0.0))).astype(out_refs[0].dtype)

    return _matmul(name, dy, w_down, "nt", tm, tn, tk, [(_sds((S, F), MXU_DTYPE), _ij_spec(tm, tn))], epilogue,
                   extras=[(h, _ij_spec(tm, tn))])[0]


def _mm_nt_res(name, dy, w, du, tm, tn, tk):
    S = dy.shape[0]
    N = w.shape[0]

    def epilogue(acc, extra_refs, out_refs, j):
        out_refs[0][...] = ALPHA * extra_refs[0][...] + acc

    return _matmul(name, dy, w, "nt", tm, tn, tk, [(_sds((S, N), F32), _ij_spec(tm, tn))], epilogue,
                   extras=[(du, _ij_spec(tm, tn))])[0]


def _attn_out_bwd(du, w_out, o, sel_t, tm, tk):
    S, D = o.shape

    def epilogue(acc, extra_refs, out_refs, j):
        out_refs[0][...] = acc.astype(out_refs[0].dtype)
        out_refs[1][...] = _exact_nn(acc * extra_refs[0][...], extra_refs[1][...])

    row = pl.BlockSpec((tm, D), lambda i, j, k: (i, 0))
    slim = pl.BlockSpec((tm, LANES), lambda i, j, k: (i, 0))
    return _matmul("attn_out_bwd", du, w_out, "nt", tm, D, tk,
                   [(_sds((S, D), MXU_DTYPE), row), (_sds((S, LANES), F32), slim)], epilogue,
                   extras=[(o, row), (sel_t, pl.BlockSpec((D, LANES), lambda i, j, k: (0, 0)))])


def _adamw(name, w, g, m, v):
    shape = w.shape
    cols = shape[-1]
    rows = math.prod(shape[:-1])
    w2, g2, m2, v2 = (t.reshape(rows, cols) for t in (w, g, m, v))
    tr = _pick(rows, (256, 128, 64, 32, 16, 8))
    c1 = 1.0 - ADAM_B1 ** ADAM_STEP
    c2 = 1.0 - ADAM_B2 ** ADAM_STEP

    def body(w_ref, g_ref, m_ref, v_ref, d_ref, nm_ref, nv_ref):
        gg = g_ref[...]
        nm = ADAM_B1 * m_ref[...] + (1.0 - ADAM_B1) * gg
        nv = ADAM_B2 * v_ref[...] + (1.0 - ADAM_B2) * (gg * gg)
        nm_ref[...] = nm
        nv_ref[...] = nv
        d_ref[...] = -ADAM_LR * ((nm / c1) / (jnp.sqrt(nv / c2) + ADAM_EPS) + ADAM_WD * w_ref[...])

    blk = pl.BlockSpec((tr, cols), lambda i: (i, 0))
    outs = pl.pallas_call(
        body,
        name=name,
        grid=(rows // tr,),
        in_specs=[blk] * 4,
        out_specs=[blk] * 3,
        out_shape=[_sds((rows, cols), F32)] * 3,
        compiler_params=_cparams(("parallel",)),
    )(w2, g2, m2, v2)
    return tuple(o.reshape(shape) for o in outs)


HBM = pl.BlockSpec(memory_space=pl.ANY)


def _shard_slice(ref, axis, size, index):
    idx = [slice(None)] * len(ref.shape)
    idx[axis] = pl.ds(pl.multiple_of(index * size, 8), size)
    return ref.at[tuple(idx)]


def _all_gather_weights(shards, axes):
    n = len(shards)
    full_shapes = []
    for s, ax in zip(shards, axes):
        fs = list(s.shape)
        fs[ax] *= 4
        full_shapes.append(tuple(fs))

    def body(*refs):
        ins, outs = refs[:n], refs[n:2 * n]
        send, recv, loc = refs[2 * n:]
        x, y, c = lax.axis_index("x"), lax.axis_index("y"), lax.axis_index("c")
        me = 2 * x + y
        chips = [(1 - x, y), (x, 1 - y), (1 - x, 1 - y)]
        started = []
        for a in range(n):
            size = ins[a].shape[axes[a]]
            mine = _shard_slice(outs[a], axes[a], size, me)
            lc = pltpu.make_async_copy(ins[a], mine, loc.at[a])
            lc.start()
            started.append(lc)
            for k, (px, py) in enumerate(chips):
                cp = pltpu.make_async_remote_copy(src_ref=ins[a], dst_ref=mine, send_sem=send.at[a, k], recv_sem=recv.at[a, k],
                                                  device_id=(px, py, c), device_id_type=MESH)
                cp.start()
        for a in range(n):
            size = ins[a].shape[axes[a]]
            for k, (px, py) in enumerate(chips):
                theirs = _shard_slice(outs[a], axes[a], size, 2 * px + py)
                cp = pltpu.make_async_remote_copy(src_ref=ins[a], dst_ref=theirs, send_sem=send.at[a, k], recv_sem=recv.at[a, k],
                                                  device_id=(px, py, c), device_id_type=MESH)
                cp.wait_send()
                cp.wait_recv()
        for lc in started:
            lc.wait()

    return pl.pallas_call(
        body,
        name="gather_weights",
        in_specs=[HBM] * n,
        out_specs=[HBM] * n,
        out_shape=[_sds(fs, s.dtype) for fs, s in zip(full_shapes, shards)],
        scratch_shapes=[pltpu.SemaphoreType.DMA((n, 3)), pltpu.SemaphoreType.DMA((n, 3)), pltpu.SemaphoreType.DMA((n,))],
    )(*shards)


FLIPS = [(fx, fy, fc) for fx in (0, 1) for fy in (0, 1) for fc in (0, 1)][1:]


def _scatter_grads(grads, axes):
    n = len(grads)
    piece_shapes = []
    for gfull, ax in zip(grads, axes):
        ps = list(gfull.shape)
        ps[ax] //= 8
        piece_shapes.append(tuple(ps))

    def body(*refs):
        ins, outs = refs[:n], refs[n:2 * n]
        send, recv, loc = refs[2 * n:]
        x, y, c = lax.axis_index("x"), lax.axis_index("y"), lax.axis_index("c")
        my_slot = 4 * x + 2 * y + c
        started = []
        for a in range(n):
            size = piece_shapes[a][axes[a]]
            lc = pltpu.make_async_copy(_shard_slice(ins[a], axes[a], size, my_slot), outs[a].at[my_slot], loc.at[a])
            lc.start()
            started.append(lc)
            for k, (fx, fy, fc) in enumerate(FLIPS):
                tx, ty, tc = x ^ fx, y ^ fy, c ^ fc
                cp = pltpu.make_async_remote_copy(
                    src_ref=_shard_slice(ins[a], axes[a], size, 4 * tx + 2 * ty + tc), dst_ref=outs[a].at[my_slot],
                    send_sem=send.at[a, k], recv_sem=recv.at[a, k], device_id=(tx, ty, tc), device_id_type=MESH)
                cp.start()
        for a in range(n):
            size = piece_shapes[a][axes[a]]
            for k, (fx, fy, fc) in enumerate(FLIPS):
                tx, ty, tc = x ^ fx, y ^ fy, c ^ fc
                their_slot = 4 * tx + 2 * ty + tc
                cp = pltpu.make_async_remote_copy(
                    src_ref=_shard_slice(ins[a], axes[a], size, their_slot), dst_ref=outs[a].at[their_slot],
                    send_sem=send.at[a, k], recv_sem=recv.at[a, k], device_id=(tx, ty, tc), device_id_type=MESH)
                cp.wait_send()
                cp.wait_recv()
        for lc in started:
            lc.wait()

    return pl.pallas_call(
        body,
        name="scatter_grads",
        in_specs=[HBM] * n,
        out_specs=[HBM] * n,
        out_shape=[_sds((8,) + ps, gfull.dtype) for ps, gfull in zip(piece_shapes, grads)],
        scratch_shapes=[pltpu.SemaphoreType.DMA((n, 7)), pltpu.SemaphoreType.DMA((n, 7)), pltpu.SemaphoreType.DMA((n,))],
    )(*grads)


def _sum_slots(name, landing):
    shape = landing.shape[1:]
    cols = shape[-1]
    rows = math.prod(shape[:-1])
    l2 = landing.reshape(8, rows, cols)
    tr = _pick(rows, (256, 128, 64, 32, 16, 8))

    def body(l_ref, o_ref):
        acc = l_ref[0].astype(F32)
        for s in range(1, 8):
            acc = acc + l_ref[s].astype(F32)
        o_ref[...] = acc

    out = pl.pallas_call(
        body,
        name=name,
        grid=(rows // tr,),
        in_specs=[pl.BlockSpec((8, tr, cols), lambda i: (0, i, 0))],
        out_specs=pl.BlockSpec((tr, cols), lambda i: (i, 0)),
        out_shape=_sds((rows, cols), F32),
        compiler_params=_cparams(("parallel",)),
    )(l2)
    return out.reshape(shape)


def _join_halves(halves, axes):
    n = len(halves)
    full_shapes = []
    for h, ax in zip(halves, axes):
        fs = list(h.shape)
        fs[ax] *= 2
        full_shapes.append(tuple(fs))

    def body(*refs):
        ins, outs = refs[:n], refs[n:2 * n]
        send, recv, loc = refs[2 * n:]
        x, y, c = lax.axis_index("x"), lax.axis_index("y"), lax.axis_index("c")
        started = []
        for a in range(n):
            size = ins[a].shape[axes[a]]
            mine = _shard_slice(outs[a], axes[a], size, c)
            lc = pltpu.make_async_copy(ins[a], mine, loc.at[a])
            lc.start()
            started.append(lc)
            cp = pltpu.make_async_remote_copy(src_ref=ins[a], dst_ref=mine, send_sem=send.at[a], recv_sem=recv.at[a],
                                              device_id=(x, y, 1 - c), device_id_type=MESH)
            cp.start()
        for a in range(n):
            size = ins[a].shape[axes[a]]
            cp = pltpu.make_async_remote_copy(src_ref=ins[a], dst_ref=_shard_slice(outs[a], axes[a], size, 1 - c),
                                              send_sem=send.at[a], recv_sem=recv.at[a], device_id=(x, y, 1 - c), device_id_type=MESH)
            cp.wait_send()
            cp.wait_recv()
        for lc in started:
            lc.wait()

    return pl.pallas_call(
        body,
        name="join_halves",
        in_specs=[HBM] * n,
        out_specs=[HBM] * n,
        out_shape=[_sds(fs, h.dtype) for fs, h in zip(full_shapes, halves)],
        scratch_shapes=[pltpu.SemaphoreType.DMA((n,)), pltpu.SemaphoreType.DMA((n,)), pltpu.SemaphoreType.DMA((n,))],
    )(*halves)


def _all_reduce_small(v):
    R, D = v.shape

    def body(v_ref, o_ref, land, send, recv):
        x, y, c = lax.axis_index("x"), lax.axis_index("y"), lax.axis_index("c")
        my_slot = 4 * x + 2 * y + c
        land[my_slot] = v_ref[...]
        for k, (fx, fy, fc) in enumerate(FLIPS):
            tx, ty, tc = x ^ fx, y ^ fy, c ^ fc
            pltpu.make_async_remote_copy(src_ref=v_ref, dst_ref=land.at[my_slot], send_sem=send.at[k], recv_sem=recv.at[k],
                                         device_id=(tx, ty, tc), device_id_type=MESH).start()
        for k, (fx, fy, fc) in enumerate(FLIPS):
            tx, ty, tc = x ^ fx, y ^ fy, c ^ fc
            cp = pltpu.make_async_remote_copy(src_ref=v_ref, dst_ref=land.at[4 * tx + 2 * ty + tc], send_sem=send.at[k],
                                              recv_sem=recv.at[k], device_id=(tx, ty, tc), device_id_type=MESH)
            cp.wait_send()
            cp.wait_recv()
        acc = land[0]
        for s in range(1, 8):
            acc = acc + land[s]
        o_ref[...] = acc

    return pl.pallas_call(
        body,
        name="all_reduce_small",
        in_specs=[pl.BlockSpec(memory_space=pltpu.VMEM)],
        out_specs=pl.BlockSpec(memory_space=pltpu.VMEM),
        out_shape=_sds((R, D), F32),
        scratch_shapes=[pltpu.VMEM((8, R, D), F32), pltpu.SemaphoreType.DMA((7,)), pltpu.SemaphoreType.DMA((7,))],
    )(v)


def kernel(x, attn_w_in, attn_w_out, hgrn_w_in, hgrn_w_out, hgrn_norm_g, lb_logits, ln_mix_g, ln_mix_b, ln_ffn_g, ln_ffn_b, ffn_w_up, ffn_w_down, loss_target, m_attn_w_in, m_attn_w_out, m_hgrn_w_in, m_hgrn_w_out, m_hgrn_norm_g, m_lb_logits, m_ln_mix_g, m_ln_mix_b, m_ln_ffn_g, m_ln_ffn_b, m_ffn_w_up, m_ffn_w_down, v_attn_w_in, v_attn_w_out, v_hgrn_w_in, v_hgrn_w_out, v_hgrn_norm_g, v_lb_logits, v_ln_mix_g, v_ln_mix_b, v_ln_ffn_g, v_ln_ffn_b, v_ffn_w_up, v_ffn_w_down):
    xs = x[0]
    tgt = loss_target[0]
    S, D = xs.shape
    F = ffn_w_up.shape[2] * 4
    TM = _pick(S, (1024, 512, 256))
    TR = _pick(S, (512, 256))
    TN = _pick(D, (512, 256, 128))
    TK = _pick(D, (1024, 512, 256))
    TB = _pick(S, (512, 256))

    cast = lambda w: w.astype(MXU_DTYPE)
    shards = [cast(attn_w_in[0]), cast(attn_w_out[0]), cast(hgrn_w_in[0]), cast(hgrn_w_out[0]), cast(ffn_w_up), cast(ffn_w_down)]
    axes = [1, 0, 1, 0, 2, 1]
    wa_in, wa_out, wh_in, wh_out, w_up, w_down, norm_g = _all_gather_weights(shards + [hgrn_norm_g], axes + [1])

    cos, sin_fwd = _rope_tables(S)
    sel = _head_sel(D)
    sel_t = sel.T

    P = _attn_proj(xs, wa_in, cos, sin_fwd, TM, TN)
    og, lg = [], []
    for g, dil in enumerate(DILATIONS):
        o_g, lse_g = _attn_fwd_group(P, g, dil, D)
        og.append(o_g)
        lg.append(lse_g)
    o_att, L_att = _attn_mix(og, lg, sel, TR)
    x1, xh1, r1 = _mm_res_ln("attn_out_ln", o_att, wa_out, xs, ln_mix_g[0:1], ln_mix_b[0:1], TR, TK)
    h0, a0 = _mlp_up("mlp0_up", x1, w_up[0], TM, TN, TK)
    x2, xh2, r2 = _mm_res_ln("mlp0_down_ln", a0, w_down[0], x1, ln_ffn_g[0:1], ln_ffn_b[0:1], TR, TK)

    P1 = _plain_mm("hgrn_proj", x2, wh_in, "nn", F32, TM, TN, TK)
    o_h, n_h, states = _hgrn_fwd(P1, lb_logits, norm_g, TB)
    x3, xh3, r3 = _mm_res_ln("hgrn_out_ln", n_h, wh_out, x2, ln_mix_g[1:2], ln_mix_b[1:2], TR, TK)
    h1, a1 = _mlp_up("mlp1_up", x3, w_up[1], TM, TN, TK)
    x4, xh4, r4 = _mm_res_ln("mlp1_down_ln", a1, w_down[1], x3, ln_ffn_g[1:2], ln_ffn_b[1:2], TR, TK)

    sq, dx4 = _loss_head(x4, tgt, TR)
    loss = lax.psum(0.5 * jnp.sum(sq) / D, ("x", "y", "c"))

    wgrad = lambda name, a, dy: _plain_mm(name, a, dy, "tn", MXU_DTYPE, TN, TK, TK)
    du4, dg_ffn1, db_ffn1 = _ln_bwd("ln_ffn1_bwd", dx4, xh4, r4, ln_ffn_g[1:2], TR)
    dh1 = _mlp_down_bwd("mlp1_down_bwd", du4, w_down[1], h1, TM, TN, TK)
    g_down1 = wgrad("g_down1", a1, du4)
    dx3 = _mm_nt_res("mlp1_up_bwd", dh1, w_up[1], du4, TM, TN, TK)
    g_up1 = wgrad("g_up1", x3, dh1)
    du3, dg_mix1, db_mix1 = _ln_bwd("ln_mix1_bwd", dx3, xh3, r3, ln_mix_g[1:2], TR)
    dn = _plain_mm("hgrn_out_bwd", du3, wh_out, "nt", F32, TM, TN, TK)
    g_hout = wgrad("g_hgrn_out", n_h, du3)
    dq_raw, dz, dv, dg_norm, dlb = _hgrn_bwd(P1, o_h, states, dn, lb_logits, norm_g, TB)
    dP1 = jnp.concatenate([dq_raw, dz, dv], axis=1)
    dx2 = _mm_nt_res("hgrn_in_bwd", dP1, wh_in, du3, TM, TN, TK)
    g_hin = wgrad("g_hgrn_in", x2, dP1)
    d_lb_logits = _lb_logits_grad(dlb, lb_logits)

    du2, dg_ffn0, db_ffn0 = _ln_bwd("ln_ffn0_bwd", dx2, xh2, r2, ln_ffn_g[0:1], TR)
    dh0 = _mlp_down_bwd("mlp0_down_bwd", du2, w_down[0], h0, TM, TN, TK)
    g_down0 = wgrad("g_down0", a0, du2)
    dx1 = _mm_nt_res("mlp0_up_bwd", dh0, w_up[0], du2, TM, TN, TK)
    g_up0 = wgrad("g_up0", x1, dh0)
    du1, dg_mix0, db_mix0 = _ln_bwd("ln_mix0_bwd", dx1, xh1, r1, ln_mix_g[0:1], TR)
    do, delta = _attn_out_bwd(du1, wa_out, o_att, sel_t, TR, TK)
    g_aout = wgrad("g_attn_out", o_att, du1)
    dP = None
    for g, dil in enumerate(DILATIONS):
        dP = _attn_bwd_group(P, do, L_att, delta, cos, sin_fwd, dP, g, dil, D)
    grad_x = _mm_nt_res("attn_in_bwd", dP, wa_in, du1, TM, TN, TK)
    g_ain = wgrad("g_attn_in", xs, dP)

    big = [g_ain, g_aout, g_hin, g_hout, jnp.stack([g_up0, g_up1]), jnp.stack([g_down0, g_down1])]
    landing = _scatter_grads(big, axes)
    halves = [_sum_slots(f"sum_grads_{i}", l) for i, l in enumerate(landing)]
    g_big = _join_halves(halves, axes)

    small = jnp.concatenate([d_lb_logits, dg_mix0, dg_mix1, db_mix0, db_mix1, dg_ffn0, dg_ffn1, db_ffn0, db_ffn1,
                             dg_norm, jnp.zeros((5, D), F32)], axis=0)
    small = _all_reduce_small(small)
    my_chip = 2 * lax.axis_index("x") + lax.axis_index("y")
    nsh = hgrn_norm_g.shape[1]
    g_norm = lax.dynamic_slice(small[10:11], (0, my_chip * nsh), (1, nsh))

    names = ["attn_w_in", "attn_w_out", "hgrn_w_in", "hgrn_w_out", "ffn_w_up", "ffn_w_down"]
    ws = [attn_w_in, attn_w_out, hgrn_w_in, hgrn_w_out, ffn_w_up, ffn_w_down]
    ms = [m_attn_w_in, m_attn_w_out, m_hgrn_w_in, m_hgrn_w_out, m_ffn_w_up, m_ffn_w_down]
    vs = [v_attn_w_in, v_attn_w_out, v_hgrn_w_in, v_hgrn_w_out, v_ffn_w_up, v_ffn_w_down]
    grads, upd = {}, {}
    for nm, w, gr, m, v in zip(names, ws, g_big, ms, vs):
        grads[nm] = gr.reshape(w.shape)
        upd[nm] = _adamw("adamw_" + nm, w, grads[nm], m, v)
    grads["hgrn_norm_g"] = g_norm
    upd["hgrn_norm_g"] = _adamw("adamw_hgrn_norm_g", hgrn_norm_g, g_norm, m_hgrn_norm_g, v_hgrn_norm_g)
    cat = lambda ts: jnp.concatenate(ts, axis=0)
    small_w = cat([lb_logits, ln_mix_g, ln_mix_b, ln_ffn_g, ln_ffn_b])
    small_m = cat([m_lb_logits, m_ln_mix_g, m_ln_mix_b, m_ln_ffn_g, m_ln_ffn_b])
    small_v = cat([v_lb_logits, v_ln_mix_g, v_ln_mix_b, v_ln_ffn_g, v_ln_ffn_b])
    small_upd = _adamw("adamw_small", small_w, small[0:10], small_m, small_v)
    for i, nm in enumerate(["lb_logits", "ln_mix_g", "ln_mix_b", "ln_ffn_g", "ln_ffn_b"]):
        grads[nm] = small[2 * i:2 * i + 2]
        upd[nm] = tuple(t[2 * i:2 * i + 2] for t in small_upd)

    order = ["attn_w_in", "attn_w_out", "hgrn_w_in", "hgrn_w_out", "hgrn_norm_g", "lb_logits", "ln_mix_g", "ln_mix_b",
             "ln_ffn_g", "ln_ffn_b", "ffn_w_up", "ffn_w_down"]
    return (loss, grad_x[None], *[grads[k] for k in order], *[upd[k][0] for k in order],
            *[upd[k][1] for k in order], *[upd[k][2] for k in order])
```

```python
import functools
import math

import jax
import jax.numpy as jnp
from jax import lax
from jax.experimental import pallas as pl
from jax.experimental.pallas import tpu as pltpu

F32 = jnp.float32
BF16 = jnp.bfloat16
MXU_DTYPE = BF16

HEAD_DIM = 64
ATTN_BLK = 128
DILATIONS = (1, 4, 16)
ROPE_THETA = 10000.0
HGRN_DK = 128
HGRN_CHUNK = 64
DEPTH = 2
LN_EPS = 1e-5
RMS_EPS = 1e-6
ALPHA = (2 * DEPTH) ** 0.25
ADAM_LR, ADAM_B1, ADAM_B2, ADAM_EPS, ADAM_WD, ADAM_STEP = 0.001, 0.9, 0.999, 1e-08, 0.01, 10

LANES = 128
VMEM_LIMIT = 56 * 1024 * 1024
NEG = -1e30
MESH = pl.DeviceIdType.MESH


def _cparams(sem=None):
    return pltpu.CompilerParams(dimension_semantics=sem, vmem_limit_bytes=VMEM_LIMIT)


def _sds(shape, dtype):
    return jax.ShapeDtypeStruct(tuple(shape), dtype)


def _dg(a, b, ca, cb):
    return lax.dot_general(a, b, (((ca,), (cb,)), ((), ())), preferred_element_type=F32)


def _nn(a, b):
    return _dg(a, b, 1, 0)


def _nt(a, b):
    return _dg(a, b, 1, 1)


def _tn(a, b):
    return _dg(a, b, 0, 0)


def _split3(a):
    hi = a.astype(BF16)
    r = a - hi.astype(F32)
    mid = r.astype(BF16)
    lo = (r - mid.astype(F32)).astype(BF16)
    return hi, mid, lo


def _exact_nn(a, sel):
    hi, mid, lo = _split3(a)
    return _nn(hi, sel) + _nn(mid, sel) + _nn(lo, sel)


def _exact_sel_nn(sel, a):
    hi, mid, lo = _split3(a)
    return _nn(sel, hi) + _nn(sel, mid) + _nn(sel, lo)


def _pick(n, prefs):
    for p in prefs:
        if n % p == 0:
            return p
    return n


def _matmul(name, a, b, form, tm, tn, tk, outs, epilogue, extras=(), a_map=None, b_map=None, mnk=None):
    if form == "nn":
        (M, K), N = a.shape, b.shape[1]
        a_spec = pl.BlockSpec((tm, tk), a_map or (lambda i, j, k: (i, k)))
        b_spec = pl.BlockSpec((tk, tn), b_map or (lambda i, j, k: (k, j)))
        ca, cb = 1, 0
    elif form == "nt":
        (M, K), N = a.shape, b.shape[0]
        a_spec = pl.BlockSpec((tm, tk), a_map or (lambda i, j, k: (i, k)))
        b_spec = pl.BlockSpec((tn, tk), b_map or (lambda i, j, k: (j, k)))
        ca, cb = 1, 1
    else:
        (K, M), N = a.shape, b.shape[1]
        a_spec = pl.BlockSpec((tk, tm), a_map or (lambda i, j, k: (k, i)))
        b_spec = pl.BlockSpec((tk, tn), b_map or (lambda i, j, k: (k, j)))
        ca, cb = 0, 0
    if mnk is not None:
        M, N, K = mnk
    assert M % tm == 0 and N % tn == 0 and K % tk == 0, (name, M, N, K, tm, tn, tk)
    nk = K // tk
    ne, no = len(extras), len(outs)

    def body(a_ref, b_ref, *rest):
        extra_refs, out_refs = rest[:ne], rest[ne:ne + no]
        j = pl.program_id(1)
        part = _dg(a_ref[...].astype(MXU_DTYPE), b_ref[...].astype(MXU_DTYPE), ca, cb)
        if nk == 1:
            epilogue(part, extra_refs, out_refs, j)
            return
        acc_ref = rest[-1]
        k = pl.program_id(2)

        @pl.when(k == 0)
        def _():
            acc_ref[...] = part

        @pl.when(k > 0)
        def _():
            acc_ref[...] += part

        @pl.when(k == nk - 1)
        def _():
            epilogue(acc_ref[...], extra_refs, out_refs, j)

    res = pl.pallas_call(
        body,
        name=name,
        grid=(M // tm, N // tn, nk),
        in_specs=[a_spec, b_spec] + [s for _, s in extras],
        out_specs=[s for _, s in outs],
        out_shape=[o for o, _ in outs],
        scratch_shapes=[pltpu.VMEM((tm, tn), F32)] if nk > 1 else [],
        compiler_params=_cparams(("parallel", "parallel", "arbitrary")),
    )(a, b, *[e for e, _ in extras])
    return res


def _ij_spec(tm, tn):
    return pl.BlockSpec((tm, tn), lambda i, j, k: (i, j))


def _store_epilogue(acc, extra_refs, out_refs, j):
    out_refs[0][...] = acc.astype(out_refs[0].dtype)


def _plain_mm(name, a, b, form, out_dtype, tm, tn, tk):
    M = a.shape[1] if form == "tn" else a.shape[0]
    N = b.shape[0] if form == "nt" else b.shape[1]
    return _matmul(name, a, b, form, tm, tn, tk, [(_sds((M, N), out_dtype), _ij_spec(tm, tn))], _store_epilogue)[0]


def _to_classes(t, dil):
    S, W = t.shape
    return t if dil == 1 else t.reshape(S // dil, dil, W).transpose(1, 0, 2).reshape(S, W)


def _from_classes(t, dil):
    S, W = t.shape
    return t if dil == 1 else t.reshape(dil, S // dil, W).transpose(1, 0, 2).reshape(S, W)


def _stack_classes(t):
    return jnp.concatenate([_to_classes(t, d) for d in DILATIONS], axis=0)


def _rope_tables(seq):
    half = HEAD_DIM // 2
    inv = ROPE_THETA ** (-jnp.arange(half, dtype=F32) * (2.0 / HEAD_DIM))
    ang = jnp.arange(seq).astype(F32)[:, None] * inv[None, :]
    reps = LANES // half
    cos = jnp.tile(jnp.cos(ang), (1, reps))
    sin = jnp.tile(jnp.sin(ang), (1, reps))
    first = (jnp.arange(LANES) % HEAD_DIM) < half
    sin_fwd = jnp.where(first[None, :], -sin, sin)
    return cos, sin_fwd


def _partner(x):
    half = HEAD_DIM // 2
    lane = lax.broadcasted_iota(jnp.int32, x.shape, 1)
    first = (lane % HEAD_DIM) < half
    return jnp.where(first, pltpu.roll(x, LANES - half, 1), pltpu.roll(x, half, 1))


def _attn_proj(x3, w_full, cos3, sin3, tm, tn):
    S3, D = x3.shape
    S = S3 // 3
    per_part = D // tn
    per_group = 3 * per_part

    def epilogue(acc, extra_refs, out_refs, j):
        cos_ref, sin_ref = extra_refs
        o_ref = out_refs[0]
        is_rot = j // per_part < 2

        @pl.when(is_rot)
        def _():
            c, s = cos_ref[...], sin_ref[...]
            for t in range(tn // LANES):
                xs = acc[:, t * LANES:(t + 1) * LANES]
                o_ref[:, t * LANES:(t + 1) * LANES] = (xs * c + _partner(xs) * s).astype(o_ref.dtype)

        @pl.when(jnp.logical_not(is_rot))
        def _():
            o_ref[...] = acc.astype(o_ref.dtype)

    tab = pl.BlockSpec((tm, LANES), lambda i, j, k: (i, 0))
    return _matmul("attn_proj", x3, w_full, "nn", tm, tn, D, [(_sds((S3, 3 * D), MXU_DTYPE), _ij_spec(tm, tn))],
                   epilogue, extras=[(cos3, tab), (sin3, tab)],
                   b_map=lambda i, j, k: (k, j + (i // (S // tm)) * per_group), mnk=(S3, 3 * D, D))[0]


def _head_sel(d_model):
    h = jnp.arange(LANES)[:, None]
    l = jnp.arange(d_model)[None, :]
    return (l // HEAD_DIM == h).astype(BF16)


def _class_edges(b, nblk):
    g = b // nblk
    per_class = jnp.where(g == 0, nblk // DILATIONS[0], jnp.where(g == 1, nblk // DILATIONS[1], nblk // DILATIONS[2]))
    pos = (b % nblk) % per_class
    return pos != 0, pos != per_class - 1


def _two_heads(t, top):
    zero = jnp.zeros_like(t)
    return jnp.concatenate([jnp.where(top, t, zero), jnp.where(top, zero, t)], axis=0)


def _band_mask(has_prev):
    B = ATTN_BLK
    row = lax.broadcasted_iota(jnp.int32, (2 * B, 2 * B), 0) % B
    col = lax.broadcasted_iota(jnp.int32, (2 * B, 2 * B), 1)
    in_prev = jnp.logical_and(jnp.logical_and(col < B, col >= row), has_prev)
    in_own = jnp.logical_and(col >= B, col - B <= row)
    return jnp.logical_or(in_prev, in_own)


def _attn_fwd(P3, D):
    S3 = P3.shape[0]
    B = ATTN_BLK
    nblk = S3 // 3 // B
    npairs = D // LANES
    scale = HEAD_DIM ** -0.5

    def body(q_ref, kc_ref, vc_ref, kp_ref, vp_ref, o_ref, lse_ref):
        has_prev, _ = _class_edges(pl.program_id(0), nblk)
        ok = _band_mask(has_prev)
        lane = lax.broadcasted_iota(jnp.int32, (B, LANES), 1)
        top = lane < HEAD_DIM
        lse_acc = jnp.zeros((B, LANES), F32)
        for j in range(npairs):
            sl = slice(j * LANES, (j + 1) * LANES)
            Q = _two_heads(q_ref[:, sl] * scale, top)
            K2 = jnp.concatenate([kp_ref[:, sl], kc_ref[:, sl]], axis=0)
            V2 = jnp.concatenate([vp_ref[:, sl], vc_ref[:, sl]], axis=0)
            s = jnp.where(ok, _nt(Q, K2), NEG)
            m = jnp.max(s, axis=1, keepdims=True)
            p = jnp.exp(s - m)
            l = jnp.sum(p, axis=1, keepdims=True)
            o = _nn((p * (1.0 / l)).astype(MXU_DTYPE), V2)
            o_ref[:, sl] = jnp.where(top, o[:B], o[B:])
            lse = m + jnp.log(l)
            lse_acc = jnp.where(lane == 2 * j, lse[:B], jnp.where(lane == 2 * j + 1, lse[B:], lse_acc))
        lse_ref[...] = lse_acc

    blk = lambda part, prev: pl.BlockSpec(
        (B, D), (lambda b: (jnp.maximum(b - 1, 0), part)) if prev else (lambda b: (b, part)))
    return pl.pallas_call(
        body,
        name="attn_fwd",
        grid=(3 * nblk,),
        in_specs=[blk(0, False), blk(1, False), blk(2, False), blk(1, True), blk(2, True)],
        out_specs=[pl.BlockSpec((B, D), lambda b: (b, 0)), pl.BlockSpec((B, LANES), lambda b: (b, 0))],
        out_shape=[_sds((S3, D), F32), _sds((S3, LANES), F32)],
        compiler_params=_cparams(("parallel",)),
    )(P3, P3, P3, P3, P3)


def _attn_mix(os, lses, sel, tm):
    S, D = min(o.shape[0] for o in os), os[0].shape[1]

    def body(o0, o1, o2, l0, l1, l2, sel_ref, o_ref, L_ref):
        a, b, c = l0[...], l1[...], l2[...]
        m = jnp.maximum(jnp.maximum(a, b), c)
        L = m + jnp.log(jnp.exp(a - m) + jnp.exp(b - m) + jnp.exp(c - m))
        L_ref[...] = L
        s = sel_ref[...]
        acc = _exact_nn(jnp.exp(a - L), s) * o0[...]
        acc += _exact_nn(jnp.exp(b - L), s) * o1[...]
        acc += _exact_nn(jnp.exp(c - L), s) * o2[...]
        o_ref[...] = acc

    big = pl.BlockSpec((tm, D), lambda i: (i, 0))
    small = pl.BlockSpec((tm, LANES), lambda i: (i, 0))
    return pl.pallas_call(
        body,
        name="attn_mix",
        grid=(S // tm,),
        in_specs=[big, big, big, small, small, small, pl.BlockSpec((LANES, D), lambda i: (0, 0))],
        out_specs=[big, small],
        out_shape=[_sds((S, D), F32), _sds((S, LANES), F32)],
        compiler_params=_cparams(("parallel",)),
    )(*os, *lses, sel)


def _attn_bwd(P3, do3, L3, delta3, cos3, sin3, D):
    S3 = P3.shape[0]
    B = ATTN_BLK
    nblk = S3 // 3 // B
    npairs = D // LANES
    scale = HEAD_DIM ** -0.5

    def body(c_ref, p_ref, n_ref, doc_ref, don_ref, Lc_ref, Ln_ref, dc_ref, dn_ref, cos_ref, sin_ref, out_ref):
        has_prev, has_next = _class_edges(pl.program_id(0), nblk)
        ok = _band_mask(has_prev)
        row = lax.broadcasted_iota(jnp.int32, (2 * B, B), 0) % B
        col = lax.broadcasted_iota(jnp.int32, (2 * B, B), 1)
        ok_n = jnp.logical_and(col >= row, has_next)
        lane = lax.broadcasted_iota(jnp.int32, (B, LANES), 1)
        top = lane < HEAD_DIM
        cos_t = cos_ref[...]
        sin_inv = -sin_ref[...]
        Lc_all, Ln_all, dc_all, dn_all = Lc_ref[...], Ln_ref[...], dc_ref[...], dn_ref[...]
        pair_col = lambda t, j: jnp.concatenate([t[:, 2 * j:2 * j + 1], t[:, 2 * j + 1:2 * j + 2]], axis=0)
        for j in range(npairs):
            sl = lambda part: slice(part * D + j * LANES, part * D + (j + 1) * LANES)
            kc2, vc2 = c_ref[:, sl(1)], c_ref[:, sl(2)]
            K2 = jnp.concatenate([p_ref[:, sl(1)], kc2], axis=0)
            V2 = jnp.concatenate([p_ref[:, sl(2)], vc2], axis=0)
            Qc = _two_heads(c_ref[:, sl(0)] * scale, top)
            Qn = _two_heads(n_ref[:, sl(0)] * scale, top)
            DOc = _two_heads(doc_ref[:, j * LANES:(j + 1) * LANES].astype(MXU_DTYPE), top)
            DOn = _two_heads(don_ref[:, j * LANES:(j + 1) * LANES].astype(MXU_DTYPE), top)
            P_c = jnp.where(ok, jnp.exp(_nt(Qc, K2) - pair_col(Lc_all, j)), 0.0)
            dS_c = P_c * (_nt(DOc, V2) - pair_col(dc_all, j))
            P_n = jnp.where(ok_n, jnp.exp(_nt(Qn, kc2) - pair_col(Ln_all, j)), 0.0)
            dS_n = P_n * (_nt(DOn, vc2) - pair_col(dn_all, j))
            dq = _nn(dS_c.astype(MXU_DTYPE), K2)
            dq2 = jnp.where(top, dq[:B], dq[B:]) * scale
            Qk = jnp.concatenate([Qc, Qn], axis=0)
            DOk = jnp.concatenate([DOc, DOn], axis=0)
            dk2 = _tn(jnp.concatenate([dS_c[:, B:], dS_n], axis=0).astype(MXU_DTYPE), Qk)
            dv2 = _tn(jnp.concatenate([P_c[:, B:], P_n], axis=0).astype(MXU_DTYPE), DOk)
            out_ref[:, sl(0)] = (dq2 * cos_t + _partner(dq2) * sin_inv).astype(out_ref.dtype)
            out_ref[:, sl(1)] = (dk2 * cos_t + _partner(dk2) * sin_inv).astype(out_ref.dtype)
            out_ref[:, sl(2)] = dv2.astype(out_ref.dtype)

    cur = lambda b: b
    prv = lambda b: jnp.maximum(b - 1, 0)
    nxt = lambda b: jnp.minimum(b + 1, 3 * nblk - 1)
    spec = lambda w, f: pl.BlockSpec((B, w), lambda b: (f(b), 0))
    return pl.pallas_call(
        body,
        name="attn_bwd",
        grid=(3 * nblk,),
        in_specs=[spec(3 * D, cur), spec(3 * D, prv), spec(3 * D, nxt), spec(D, cur), spec(D, nxt),
                  spec(LANES, cur), spec(LANES, nxt), spec(LANES, cur), spec(LANES, nxt), spec(LANES, cur), spec(LANES, cur)],
        out_specs=spec(3 * D, cur),
        out_shape=_sds((S3, 3 * D), MXU_DTYPE),
        compiler_params=_cparams(("parallel",)),
    )(P3, P3, P3, do3, do3, L3, L3, delta3, delta3, cos3, sin3)


def _sum4(a, b, c, d, tm):
    S, D = a.shape

    def body(a_ref, b_ref, c_ref, d_ref, o_ref):
        o_ref[...] = ALPHA * a_ref[...] + b_ref[...] + c_ref[...] + d_ref[...]

    row = pl.BlockSpec((tm, D), lambda i: (i, 0))
    return pl.pallas_call(body, name="sum4", grid=(S // tm,), in_specs=[row] * 4, out_specs=row,
                          out_shape=_sds((S, D), F32), compiler_params=_cparams(("parallel",)))(a, b, c, d)


def _tri(lower):
    r = lax.broadcasted_iota(jnp.int32, (HGRN_CHUNK, HGRN_CHUNK), 0)
    c = lax.broadcasted_iota(jnp.int32, (HGRN_CHUNK, HGRN_CHUNK), 1)
    return ((r >= c) if lower else (r <= c)).astype(BF16)


def _lower_bound(lb_ref):
    l0, l1 = lb_ref[0:1, :], lb_ref[1:2, :]
    m = jnp.maximum(l0, l1)
    e0, e1 = jnp.exp(l0 - m), jnp.exp(l1 - m)
    return e1 / (e0 + e1)


def _hgrn_gates(q_raw, z, lb):
    sg = 1.0 / (1.0 + jnp.exp(-z))
    sn = 1.0 / (1.0 + jnp.exp(z))
    f = lb + (1.0 - lb) * sg
    key = (1.0 - lb) * sn
    sq = 1.0 / (1.0 + jnp.exp(-q_raw))
    return sg, sn, f, key, sq


def _hgrn_fwd(P1, lb_logits, norm_g, tb):
    S = P1.shape[0]
    D = P1.shape[1] // 3
    H = D // HGRN_DK
    C = HGRN_CHUNK
    cpb = tb // C
    nt = S // tb

    def body(q_ref, f_ref, i_ref, lb_ref, g_ref, o_ref, n_ref, st_ref, state):
        t = pl.program_id(1)

        @pl.when(t == 0)
        def _():
            state[...] = jnp.zeros_like(state)

        lb = _lower_bound(lb_ref)
        gn = g_ref[...]
        tri = _tri(True)
        r = lax.broadcasted_iota(jnp.int32, (C, C), 0)
        c = lax.broadcasted_iota(jnp.int32, (C, C), 1)
        causal = r >= c
        for ci in range(cpb):
            rows = slice(ci * C, (ci + 1) * C)
            q_raw, z, v = q_ref[rows, :], f_ref[rows, :], i_ref[rows, :]
            sg, sn, f, key, sq = _hgrn_gates(q_raw, z, lb)
            q = q_raw * sq
            b = _exact_sel_nn(tri, jnp.log(f))
            b_last = b[C - 1:C, :]
            qd = (q * jnp.exp(b)).astype(MXU_DTYPE)
            kd = (key * jnp.exp(-b)).astype(MXU_DTYPE)
            kb = (key * jnp.exp(b_last - b)).astype(MXU_DTYPE)
            vm = v.astype(MXU_DTYPE)
            st = state[...]
            st_ref[ci] = st
            a = jnp.where(causal, _nt(qd, kd), 0.0)
            o = _nn(a.astype(MXU_DTYPE), vm) + _nt(qd, st.astype(MXU_DTYPE))
            state[...] = st * jnp.exp(b_last) + _tn(vm, kb)
            o_ref[rows, :] = o
            rs = lax.rsqrt(jnp.mean(o * o, axis=1, keepdims=True) + RMS_EPS)
            n_ref[rows, :] = o * rs * gn

    tok = lambda part: pl.BlockSpec((tb, HGRN_DK), lambda h, t: (t, part * H + h))
    vec = lambda rows: pl.BlockSpec((rows, HGRN_DK), lambda h, t: (0, h))
    return pl.pallas_call(
        body,
        name="hgrn_fwd",
        grid=(H, nt),
        in_specs=[tok(0), tok(1), tok(2), vec(2), vec(1)],
        out_specs=[tok(0), tok(0), pl.BlockSpec((None, cpb, HGRN_DK, HGRN_DK), lambda h, t: (h, t, 0, 0))],
        out_shape=[_sds((S, D), F32), _sds((S, D), F32), _sds((H, S // C, HGRN_DK, HGRN_DK), F32)],
        scratch_shapes=[pltpu.VMEM((HGRN_DK, HGRN_DK), F32)],
        compiler_params=_cparams(("parallel", "arbitrary")),
    )(P1, P1, P1, lb_logits, norm_g)


def _hgrn_bwd(P1, o_pre, states, dn, lb_logits, norm_g, tb):
    S = P1.shape[0]
    D = P1.shape[1] // 3
    H = D // HGRN_DK
    C = HGRN_CHUNK
    cpb = tb // C
    nt = S // tb

    def body(q_ref, f_ref, i_ref, o_ref, st_ref, dn_ref, lb_ref, g_ref, dq_ref, dz_ref, dv_ref, dg_ref, dlb_ref, dstate):
        t = pl.program_id(1)

        @pl.when(t == 0)
        def _():
            dstate[...] = jnp.zeros_like(dstate)
            dg_ref[...] = jnp.zeros_like(dg_ref)
            dlb_ref[...] = jnp.zeros_like(dlb_ref)

        lb = _lower_bound(lb_ref)
        gn = g_ref[...]
        tri_l, tri_u = _tri(True), _tri(False)
        r = lax.broadcasted_iota(jnp.int32, (C, C), 0)
        c = lax.broadcasted_iota(jnp.int32, (C, C), 1)
        causal = r >= c
        last_row = lax.broadcasted_iota(jnp.int32, (C, HGRN_DK), 0) == C - 1
        dg_acc = jnp.zeros((1, HGRN_DK), F32)
        dlb_acc = jnp.zeros((1, HGRN_DK), F32)
        for ci in reversed(range(cpb)):
            rows = slice(ci * C, (ci + 1) * C)
            q_raw, z, v = q_ref[rows, :], f_ref[rows, :], i_ref[rows, :]
            sg, sn, f, key, sq = _hgrn_gates(q_raw, z, lb)
            q = q_raw * sq
            b = _exact_sel_nn(tri_l, jnp.log(f))
            b_last = b[C - 1:C, :]
            e_pos, e_neg, e_rel = jnp.exp(b), jnp.exp(-b), jnp.exp(b_last - b)
            dec = jnp.exp(b_last)
            qd_f, kd_f, kb_f = q * e_pos, key * e_neg, key * e_rel
            qd, kd, kb = qd_f.astype(MXU_DTYPE), kd_f.astype(MXU_DTYPE), kb_f.astype(MXU_DTYPE)
            vm = v.astype(MXU_DTYPE)
            st = st_ref[ci]
            dst = dstate[...]
            stm, dstm = st.astype(MXU_DTYPE), dst.astype(MXU_DTYPE)
            a = jnp.where(causal, _nt(qd, kd), 0.0).astype(MXU_DTYPE)
            o = o_ref[rows, :]
            dnn = dn_ref[rows, :]
            rs = lax.rsqrt(jnp.mean(o * o, axis=1, keepdims=True) + RMS_EPS)
            dg_acc += jnp.sum(dnn * o * rs, axis=0, keepdims=True)
            tg = dnn * gn
            do_f = rs * tg - o * (rs * rs * rs) * jnp.mean(tg * o, axis=1, keepdims=True)
            dom = do_f.astype(MXU_DTYPE)
            da = jnp.where(causal, _nt(dom, vm), 0.0).astype(MXU_DTYPE)
            dv = _tn(a, dom) + _nt(kb, dstm)
            dqd = _nn(da, kd) + _nn(dom, stm)
            dkd = _tn(da, qd)
            dkb = _nn(vm, dstm)
            ddec = jnp.sum(dst * st, axis=0, keepdims=True)
            dstate[...] = dst * dec + _tn(dom, qd)
            dq = dqd * e_pos
            dkey = dkd * e_neg + dkb * e_rel
            tk = dkb * kb_f
            db_last = jnp.sum(tk, axis=0, keepdims=True) + ddec * dec
            db = dqd * qd_f - dkd * kd_f - tk
            db = jnp.where(last_row, db + db_last, db)
            dlogf = _exact_sel_nn(tri_u, db)
            gz = (1.0 - lb) * sg * sn
            dz_ref[rows, :] = (dlogf * gz / f - dkey * gz).astype(dz_ref.dtype)
            dlb_acc += jnp.sum(dlogf * sn / f - dkey * sn, axis=0, keepdims=True)
            dq_ref[rows, :] = (dq * (sq + q_raw * sq * (1.0 - sq))).astype(dq_ref.dtype)
            dv_ref[rows, :] = dv.astype(dv_ref.dtype)
        dg_ref[...] += dg_acc
        dlb_ref[...] += dlb_acc

    rev = lambda t: nt - 1 - t
    tok = lambda part: pl.BlockSpec((tb, HGRN_DK), lambda h, t: (rev(t), part * H + h))
    vec = lambda rows: pl.BlockSpec((rows, HGRN_DK), lambda h, t: (0, h))
    outs = pl.pallas_call(
        body,
        name="hgrn_bwd",
        grid=(H, nt),
        in_specs=[tok(0), tok(1), tok(2), tok(0),
                  pl.BlockSpec((None, cpb, HGRN_DK, HGRN_DK), lambda h, t: (h, rev(t), 0, 0)),
                  tok(0), vec(2), vec(1)],
        out_specs=[tok(0), tok(0), tok(0), vec(1), vec(1)],
        out_shape=[_sds((S, D), MXU_DTYPE)] * 3 + [_sds((1, D), F32)] * 2,
        scratch_shapes=[pltpu.VMEM((HGRN_DK, HGRN_DK), F32)],
        compiler_params=_cparams(("parallel", "arbitrary")),
    )(P1, P1, P1, o_pre, states, dn, lb_logits, norm_g)
    return outs


def _lb_logits_grad(dlb, lb_logits):
    def body(d_ref, l_ref, o_ref):
        s1 = _lower_bound(l_ref)
        d = d_ref[...]
        o_ref[0:1, :] = -(1.0 - s1) * s1 * d
        o_ref[1:2, :] = s1 * (1.0 - s1) * d

    return pl.pallas_call(body, name="lb_logits_grad", out_shape=_sds(lb_logits.shape, F32))(dlb, lb_logits)


def _ln_epilogue(acc, extra_refs, out_refs, j):
    res_ref, g_ref, b_ref = extra_refs
    x_ref, xhat_ref, rstd_ref = out_refs
    u = ALPHA * res_ref[...] + acc
    mu = jnp.mean(u, axis=1, keepdims=True)
    cen = u - mu
    rstd = lax.rsqrt(jnp.mean(cen * cen, axis=1, keepdims=True) + LN_EPS)
    xhat = cen * rstd
    xhat_ref[...] = xhat
    x_ref[...] = xhat * g_ref[...] + b_ref[...]
    rstd_ref[...] = rstd


def _mm_res_ln(name, a, w_full, res, g, b, tm, tk):
    S, D = res.shape
    row = pl.BlockSpec((tm, D), lambda i, j, k: (i, 0))
    vec = pl.BlockSpec((1, D), lambda i, j, k: (0, 0))
    outs = [(_sds((S, D), F32), row), (_sds((S, D), F32), row),
            (_sds((S, 1), F32), pl.BlockSpec((tm, 1), lambda i, j, k: (i, 0)))]
    return _matmul(name, a, w_full, "nn", tm, D, tk, outs, _ln_epilogue, extras=[(res, row), (g, vec), (b, vec)])


def _ln_bwd(name, dy, xhat, rstd, g, tm):
    S, D = dy.shape

    def body(dy_ref, xh_ref, r_ref, g_ref, du_ref, dg_ref, db_ref):
        @pl.when(pl.program_id(0) == 0)
        def _():
            dg_ref[...] = jnp.zeros_like(dg_ref)
            db_ref[...] = jnp.zeros_like(db_ref)

        dy_, xh = dy_ref[...], xh_ref[...]
        dg_ref[...] += jnp.sum(dy_ * xh, axis=0, keepdims=True)
        db_ref[...] += jnp.sum(dy_, axis=0, keepdims=True)
        dxh = dy_ * g_ref[...]
        m1 = jnp.mean(dxh, axis=1, keepdims=True)
        m2 = jnp.mean(dxh * xh, axis=1, keepdims=True)
        du_ref[...] = r_ref[...] * (dxh - m1 - xh * m2)

    row = pl.BlockSpec((tm, D), lambda i: (i, 0))
    vec = pl.BlockSpec((1, D), lambda i: (0, 0))
    return pl.pallas_call(
        body,
        name=name,
        grid=(S // tm,),
        in_specs=[row, row, pl.BlockSpec((tm, 1), lambda i: (i, 0)), vec],
        out_specs=[row, vec, vec],
        out_shape=[_sds((S, D), F32), _sds((1, D), F32), _sds((1, D), F32)],
        compiler_params=_cparams(("arbitrary",)),
    )(dy, xhat, rstd, g)


def _loss_head(y, target, tm):
    S, D = y.shape

    def body(y_ref, t_ref, sq_ref, dy_ref):
        @pl.when(pl.program_id(0) == 0)
        def _():
            sq_ref[...] = jnp.zeros_like(sq_ref)

        e = y_ref[...] - t_ref[...]
        sq_ref[...] += jnp.sum(e * e, axis=0, keepdims=True)
        dy_ref[...] = e / D

    row = pl.BlockSpec((tm, D), lambda i: (i, 0))
    vec = pl.BlockSpec((1, D), lambda i: (0, 0))
    return pl.pallas_call(
        body,
        name="loss_head",
        grid=(S // tm,),
        in_specs=[row, row],
        out_specs=[vec, row],
        out_shape=[_sds((1, D), F32), _sds((S, D), F32)],
        compiler_params=_cparams(("arbitrary",)),
    )(y, target)


def _mlp_up(name, x, w_up, tm, tn, tk):
    S = x.shape[0]
    F = w_up.shape[1]

    def epilogue(acc, extra_refs, out_refs, j):
        out_refs[0][...] = acc
        r = jnp.maximum(acc, 0.0)
        out_refs[1][...] = (r * r).astype(out_refs[1].dtype)

    return _matmul(name, x, w_up, "nn", tm, tn, tk,
                   [(_sds((S, F), F32), _ij_spec(tm, tn)), (_sds((S, F), MXU_DTYPE), _ij_spec(tm, tn))], epilogue)


def _mlp_down_bwd(name, dy, w_down, h, tm, tn, tk):
    S, F = h.shape

    def epilogue(acc, extra_refs, out_refs, j):
        out_refs[0][...] = (acc * (2.0 * jnp.maximum(extra_refs[0][...], 0.0))).astype(out_refs[0].dtype)

    return _matmul(name, dy, w_down, "nt", tm, tn, tk, [(_sds((S, F), MXU_DTYPE), _ij_spec(tm, tn))], epilogue,
                   extras=[(h, _ij_spec(tm, tn))])[0]


def _mm_nt_res(name, dy, w, du, tm, tn, tk):
    S = dy.shape[0]
    N = w.shape[0]

    def epilogue(acc, extra_refs, out_refs, j):
        out_refs[0][...] = ALPHA * extra_refs[0][...] + acc

    return _matmul(name, dy, w, "nt", tm, tn, tk, [(_sds((S, N), F32), _ij_spec(tm, tn))], epilogue,
                   extras=[(du, _ij_spec(tm, tn))])[0]


def _attn_out_bwd(du, w_out, o, sel_t, tm, tk):
    S, D = o.shape

    def epilogue(acc, extra_refs, out_refs, j):
        out_refs[0][...] = acc.astype(out_refs[0].dtype)
        out_refs[1][...] = _exact_nn(acc * extra_refs[0][...], extra_refs[1][...])

    row = pl.BlockSpec((tm, D), lambda i, j, k: (i, 0))
    slim = pl.BlockSpec((tm, LANES), lambda i, j, k: (i, 0))
    return _matmul("attn_out_bwd", du, w_out, "nt", tm, D, tk,
                   [(_sds((S, D), MXU_DTYPE), row), (_sds((S, LANES), F32), slim)], epilogue,
                   extras=[(o, row), (sel_t, pl.BlockSpec((D, LANES), lambda i, j, k: (0, 0)))])


def _adamw(name, w, g, m, v):
    shape = w.shape
    cols = shape[-1]
    rows = math.prod(shape[:-1])
    w2, g2, m2, v2 = (t.reshape(rows, cols) for t in (w, g, m, v))
    tr = _pick(rows, (256, 128, 64, 32, 16, 8))
    c1 = 1.0 - ADAM_B1 ** ADAM_STEP
    c2 = 1.0 - ADAM_B2 ** ADAM_STEP

    def body(w_ref, g_ref, m_ref, v_ref, d_ref, nm_ref, nv_ref):
        gg = g_ref[...]
        nm = ADAM_B1 * m_ref[...] + (1.0 - ADAM_B1) * gg
        nv = ADAM_B2 * v_ref[...] + (1.0 - ADAM_B2) * (gg * gg)
        nm_ref[...] = nm
        nv_ref[...] = nv
        d_ref[...] = -ADAM_LR * ((nm / c1) / (jnp.sqrt(nv / c2) + ADAM_EPS) + ADAM_WD * w_ref[...])

    blk = pl.BlockSpec((tr, cols), lambda i: (i, 0))
    outs = pl.pallas_call(
        body,
        name=name,
        grid=(rows // tr,),
        in_specs=[blk] * 4,
        out_specs=[blk] * 3,
        out_shape=[_sds((rows, cols), F32)] * 3,
        compiler_params=_cparams(("parallel",)),
    )(w2, g2, m2, v2)
    return tuple(o.reshape(shape) for o in outs)


HBM = pl.BlockSpec(memory_space=pl.ANY)


def _shard_slice(ref, axis, size, index):
    idx = [slice(None)] * len(ref.shape)
    idx[axis] = pl.ds(pl.multiple_of(index * size, 8), size)
    return ref.at[tuple(idx)]


def _all_gather_weights(shards, axes):
    n = len(shards)
    full_shapes = []
    for s, ax in zip(shards, axes):
        fs = list(s.shape)
        fs[ax] *= 4
        full_shapes.append(tuple(fs))

    def body(*refs):
        ins, outs = refs[:n], refs[n:2 * n]
        send, recv, loc = refs[2 * n:]
        x, y, c = lax.axis_index("x"), lax.axis_index("y"), lax.axis_index("c")
        me = 2 * x + y
        chips = [(1 - x, y), (x, 1 - y), (1 - x, 1 - y)]
        started = []
        for a in range(n):
            size = ins[a].shape[axes[a]]
            mine = _shard_slice(outs[a], axes[a], size, me)
            lc = pltpu.make_async_copy(ins[a], mine, loc.at[a])
            lc.start()
            started.append(lc)
            for k, (px, py) in enumerate(chips):
                cp = pltpu.make_async_remote_copy(src_ref=ins[a], dst_ref=mine, send_sem=send.at[a, k], recv_sem=recv.at[a, k],
                                                  device_id=(px, py, c), device_id_type=MESH)
                cp.start()
        for a in range(n):
            size = ins[a].shape[axes[a]]
            for k, (px, py) in enumerate(chips):
                theirs = _shard_slice(outs[a], axes[a], size, 2 * px + py)
                cp = pltpu.make_async_remote_copy(src_ref=ins[a], dst_ref=theirs, send_sem=send.at[a, k], recv_sem=recv.at[a, k],
                                                  device_id=(px, py, c), device_id_type=MESH)
                cp.wait_send()
                cp.wait_recv()
        for lc in started:
            lc.wait()

    return pl.pallas_call(
        body,
        name="gather_weights",
        in_specs=[HBM] * n,
        out_specs=[HBM] * n,
        out_shape=[_sds(fs, s.dtype) for fs, s in zip(full_shapes, shards)],
        scratch_shapes=[pltpu.SemaphoreType.DMA((n, 3)), pltpu.SemaphoreType.DMA((n, 3)), pltpu.SemaphoreType.DMA((n,))],
    )(*shards)


FLIPS = [(fx, fy, fc) for fx in (0, 1) for fy in (0, 1) for fc in (0, 1)][1:]


def _piece_shape(shape, axis):
    ps = list(shape)
    if axis == 0:
        ps[0] //= 8
    else:
        ps[0] //= 2
        ps[axis] //= 4
    return tuple(ps)


def _piece(ref, axis, q, c):
    shape = ref.shape
    idx = [slice(None)] * len(shape)
    if axis == 0:
        h = shape[0] // 8
        idx[0] = pl.ds(pl.multiple_of((2 * q + c) * h, 8), h)
    else:
        h, w = shape[0] // 2, shape[axis] // 4
        idx[0] = pl.ds(c * h, h)
        idx[axis] = pl.ds(pl.multiple_of(q * w, LANES if axis == len(shape) - 1 else 8), w)
    return ref.at[tuple(idx)]


def _scatter_grads(grads, axes):
    n = len(grads)
    piece_shapes = [_piece_shape(gfull.shape, ax) for gfull, ax in zip(grads, axes)]

    def body(*refs):
        ins, outs = refs[:n], refs[n:2 * n]
        send, recv, loc = refs[2 * n:]
        x, y, c = lax.axis_index("x"), lax.axis_index("y"), lax.axis_index("c")
        my_slot = 4 * x + 2 * y + c
        started = []
        for a in range(n):
            lc = pltpu.make_async_copy(_piece(ins[a], axes[a], 2 * x + y, c), outs[a].at[my_slot], loc.at[a])
            lc.start()
            started.append(lc)
            for k, (fx, fy, fc) in enumerate(FLIPS):
                tx, ty, tc = x ^ fx, y ^ fy, c ^ fc
                cp = pltpu.make_async_remote_copy(
                    src_ref=_piece(ins[a], axes[a], 2 * tx + ty, tc), dst_ref=outs[a].at[my_slot],
                    send_sem=send.at[a, k], recv_sem=recv.at[a, k], device_id=(tx, ty, tc), device_id_type=MESH)
                cp.start()
        for a in range(n):
            for k, (fx, fy, fc) in enumerate(FLIPS):
                tx, ty, tc = x ^ fx, y ^ fy, c ^ fc
                cp = pltpu.make_async_remote_copy(
                    src_ref=_piece(ins[a], axes[a], 2 * tx + ty, tc), dst_ref=outs[a].at[4 * tx + 2 * ty + tc],
                    send_sem=send.at[a, k], recv_sem=recv.at[a, k], device_id=(tx, ty, tc), device_id_type=MESH)
                cp.wait_send()
                cp.wait_recv()
        for lc in started:
            lc.wait()

    return pl.pallas_call(
        body,
        name="scatter_grads",
        in_specs=[HBM] * n,
        out_specs=[HBM] * n,
        out_shape=[_sds((8,) + ps, gfull.dtype) for ps, gfull in zip(piece_shapes, grads)],
        scratch_shapes=[pltpu.SemaphoreType.DMA((n, 7)), pltpu.SemaphoreType.DMA((n, 7)), pltpu.SemaphoreType.DMA((n,))],
    )(*grads)


def _reduce_join(name, landing):
    piece = landing.shape[1:]
    C = piece[-1]
    R = math.prod(piece[:-1])
    l3 = landing.reshape(8, R, C)
    tr = _pick(R, [t for t in (512, 256, 128, 64, 32, 16, 8) if t * C <= 256 * 1024])
    nsteps = R // tr

    def body(l_ref, o_ref, buf, send, loc, recv):
        i = pl.program_id(0)
        x, y, c = lax.axis_index("x"), lax.axis_index("y"), lax.axis_index("c")
        sibling = (x, y, 1 - c)

        def copies(slot, step):
            dst = o_ref.at[pl.ds(pl.multiple_of(c * R + step * tr, 8), tr), :]
            return (pltpu.make_async_copy(buf.at[slot], dst, loc.at[slot]),
                    pltpu.make_async_remote_copy(src_ref=buf.at[slot], dst_ref=dst, send_sem=send.at[slot], recv_sem=recv,
                                                 device_id=sibling, device_id_type=MESH))

        @pl.when(i >= 2)
        def _():
            lc, rc = copies(i % 2, i - 2)
            lc.wait()
            rc.wait_send()

        acc = l_ref[0].astype(F32)
        for s in range(1, 8):
            acc = acc + l_ref[s].astype(F32)
        buf[i % 2] = acc
        lc, rc = copies(i % 2, i)
        lc.start()
        rc.start()

        @pl.when(i == nsteps - 1)
        def _():
            for st in range(max(nsteps - 2, 0), nsteps):
                lc, rc = copies(st % 2, st)
                lc.wait()
                rc.wait_send()
            theirs = o_ref.at[pl.ds(pl.multiple_of((1 - c) * R, 8), R), :]
            pltpu.make_async_remote_copy(src_ref=theirs, dst_ref=theirs, send_sem=send.at[0], recv_sem=recv,
                                         device_id=sibling, device_id_type=MESH).wait_recv()

    return pl.pallas_call(
        body,
        name=name,
        grid=(nsteps,),
        in_specs=[pl.BlockSpec((8, tr, C), lambda i: (0, i, 0))],
        out_specs=HBM,
        out_shape=_sds((2 * R, C), F32),
        scratch_shapes=[pltpu.VMEM((2, tr, C), F32), pltpu.SemaphoreType.DMA((2,)), pltpu.SemaphoreType.DMA((2,)),
                        pltpu.SemaphoreType.DMA(())],
        compiler_params=_cparams(("arbitrary",)),
    )(l3)


def _all_reduce_small(v):
    R, D = v.shape

    def body(v_ref, o_ref, land, send, recv):
        x, y, c = lax.axis_index("x"), lax.axis_index("y"), lax.axis_index("c")
        my_slot = 4 * x + 2 * y + c
        land[my_slot] = v_ref[...]
        for k, (fx, fy, fc) in enumerate(FLIPS):
            tx, ty, tc = x ^ fx, y ^ fy, c ^ fc
            pltpu.make_async_remote_copy(src_ref=v_ref, dst_ref=land.at[my_slot], send_sem=send.at[k], recv_sem=recv.at[k],
                                         device_id=(tx, ty, tc), device_id_type=MESH).start()
        for k, (fx, fy, fc) in enumerate(FLIPS):
            tx, ty, tc = x ^ fx, y ^ fy, c ^ fc
            cp = pltpu.make_async_remote_copy(src_ref=v_ref, dst_ref=land.at[4 * tx + 2 * ty + tc], send_sem=send.at[k],
                                              recv_sem=recv.at[k], device_id=(tx, ty, tc), device_id_type=MESH)
            cp.wait_send()
            cp.wait_recv()
        acc = land[0]
        for s in range(1, 8):
            acc = acc + land[s]
        o_ref[...] = acc

    return pl.pallas_call(
        body,
        name="all_reduce_small",
        in_specs=[pl.BlockSpec(memory_space=pltpu.VMEM)],
        out_specs=pl.BlockSpec(memory_space=pltpu.VMEM),
        out_shape=_sds((R, D), F32),
        scratch_shapes=[pltpu.VMEM((8, R, D), F32), pltpu.SemaphoreType.DMA((7,)), pltpu.SemaphoreType.DMA((7,))],
    )(v)


def kernel(x, attn_w_in, attn_w_out, hgrn_w_in, hgrn_w_out, hgrn_norm_g, lb_logits, ln_mix_g, ln_mix_b, ln_ffn_g, ln_ffn_b, ffn_w_up, ffn_w_down, loss_target, m_attn_w_in, m_attn_w_out, m_hgrn_w_in, m_hgrn_w_out, m_hgrn_norm_g, m_lb_logits, m_ln_mix_g, m_ln_mix_b, m_ln_ffn_g, m_ln_ffn_b, m_ffn_w_up, m_ffn_w_down, v_attn_w_in, v_attn_w_out, v_hgrn_w_in, v_hgrn_w_out, v_hgrn_norm_g, v_lb_logits, v_ln_mix_g, v_ln_mix_b, v_ln_ffn_g, v_ln_ffn_b, v_ffn_w_up, v_ffn_w_down):
    xs = x[0]
    tgt = loss_target[0]
    S, D = xs.shape
    F = ffn_w_up.shape[2] * 4
    TM = _pick(S, (1024, 512, 256))
    TR = _pick(S, (512, 256))
    TN = _pick(D, (512, 256, 128))
    TK = _pick(D, (1024, 512, 256))
    TB = _pick(S, (512, 256))

    cast = lambda w: w.astype(MXU_DTYPE)
    shards = [cast(attn_w_in[0]), cast(attn_w_out[0]), cast(hgrn_w_in[0]), cast(hgrn_w_out[0]), cast(ffn_w_up), cast(ffn_w_down)]
    axes = [1, 0, 1, 0, 2, 1]
    wa_in, wa_out, wh_in, wh_out, w_up, w_down, norm_g = _all_gather_weights(shards + [hgrn_norm_g], axes + [1])

    cos, sin_fwd = _rope_tables(S)
    sel = _head_sel(D)
    sel_t = sel.T

    xc3 = _stack_classes(xs.astype(MXU_DTYPE))
    cos3, sin3 = _stack_classes(cos), _stack_classes(sin_fwd)
    P3 = _attn_proj(xc3, wa_in, cos3, sin3, TM, TN)
    o3, lse3 = _attn_fwd(P3, D)
    back = lambda t, g: _from_classes(t[g * S:(g + 1) * S], DILATIONS[g])
    o_att, L_att = _attn_mix([o3, back(o3, 1), back(o3, 2)], [lse3, back(lse3, 1), back(lse3, 2)], sel, TR)
    x1, xh1, r1 = _mm_res_ln("attn_out_ln", o_att, wa_out, xs, ln_mix_g[0:1], ln_mix_b[0:1], TR, TK)
    h0, a0 = _mlp_up("mlp0_up", x1, w_up[0], TM, TN, TK)
    x2, xh2, r2 = _mm_res_ln("mlp0_down_ln", a0, w_down[0], x1, ln_ffn_g[0:1], ln_ffn_b[0:1], TR, TK)

    P1 = _plain_mm("hgrn_proj", x2, wh_in, "nn", F32, TM, TN, TK)
    o_h, n_h, states = _hgrn_fwd(P1, lb_logits, norm_g, TB)
    x3, xh3, r3 = _mm_res_ln("hgrn_out_ln", n_h, wh_out, x2, ln_mix_g[1:2], ln_mix_b[1:2], TR, TK)
    h1, a1 = _mlp_up("mlp1_up", x3, w_up[1], TM, TN, TK)
    x4, xh4, r4 = _mm_res_ln("mlp1_down_ln", a1, w_down[1], x3, ln_ffn_g[1:2], ln_ffn_b[1:2], TR, TK)

    sq, dx4 = _loss_head(x4, tgt, TR)
    loss = lax.psum(0.5 * jnp.sum(sq) / D, ("x", "y", "c"))

    wgrad = lambda name, a, dy: _plain_mm(name, a, dy, "tn", MXU_DTYPE, TN, TK, TK)
    du4, dg_ffn1, db_ffn1 = _ln_bwd("ln_ffn1_bwd", dx4, xh4, r4, ln_ffn_g[1:2], TR)
    dh1 = _mlp_down_bwd("mlp1_down_bwd", du4, w_down[1], h1, TM, TN, TK)
    g_down1 = wgrad("g_down1", a1, du4)
    dx3 = _mm_nt_res("mlp1_up_bwd", dh1, w_up[1], du4, TM, TN, TK)
    g_up1 = wgrad("g_up1", x3, dh1)
    du3, dg_mix1, db_mix1 = _ln_bwd("ln_mix1_bwd", dx3, xh3, r3, ln_mix_g[1:2], TR)
    dn = _plain_mm("hgrn_out_bwd", du3, wh_out, "nt", F32, TM, TN, TK)
    g_hout = wgrad("g_hgrn_out", n_h, du3)
    dq_raw, dz, dv, dg_norm, dlb = _hgrn_bwd(P1, o_h, states, dn, lb_logits, norm_g, TB)
    dP1 = jnp.concatenate([dq_raw, dz, dv], axis=1)
    dx2 = _mm_nt_res("hgrn_in_bwd", dP1, wh_in, du3, TM, TN, TK)
    g_hin = wgrad("g_hgrn_in", x2, dP1)
    d_lb_logits = _lb_logits_grad(dlb, lb_logits)

    du2, dg_ffn0, db_ffn0 = _ln_bwd("ln_ffn0_bwd", dx2, xh2, r2, ln_ffn_g[0:1], TR)
    dh0 = _mlp_down_bwd("mlp0_down_bwd", du2, w_down[0], h0, TM, TN, TK)
    g_down0 = wgrad("g_down0", a0, du2)
    dx1 = _mm_nt_res("mlp0_up_bwd", dh0, w_up[0], du2, TM, TN, TK)
    g_up0 = wgrad("g_up0", x1, dh0)
    du1, dg_mix0, db_mix0 = _ln_bwd("ln_mix0_bwd", dx1, xh1, r1, ln_mix_g[0:1], TR)
    do, delta = _attn_out_bwd(du1, wa_out, o_att, sel_t, TR, TK)
    g_aout = wgrad("g_attn_out", o_att, du1)
    dP3 = _attn_bwd(P3, _stack_classes(do), _stack_classes(L_att), _stack_classes(delta), cos3, sin3, D)
    dxc3 = _matmul("attn_in_bwd", dP3, wa_in, "nt", TM, TN, TK, [(_sds((3 * S, D), F32), _ij_spec(TM, TN))], _store_epilogue,
                   b_map=lambda i, j, k: (j, k + (i // (S // TM)) * (3 * D // TK)), mnk=(3 * S, D, 3 * D))[0]
    grad_x = _sum4(du1, dxc3, back(dxc3, 1), back(dxc3, 2), TR)
    grp = lambda j: j // (3 * D // TK)
    g_ain = _matmul("g_attn_in", xc3, dP3, "tn", TN, TK, TK, [(_sds((D, 9 * D), MXU_DTYPE), _ij_spec(TN, TK))], _store_epilogue,
                    a_map=lambda i, j, k: (k + grp(j) * (S // TK), i),
                    b_map=lambda i, j, k: (k + grp(j) * (S // TK), j % (3 * D // TK)), mnk=(D, 9 * D, S))[0]

    big = [g_ain, g_aout, g_hin, g_hout, jnp.stack([g_up0, g_up1]), jnp.stack([g_down0, g_down1])]
    landing = _scatter_grads(big, axes)
    g_big = [_reduce_join(f"reduce_grads_{i}", l) for i, l in enumerate(landing)]

    small = jnp.concatenate([d_lb_logits, dg_mix0, dg_mix1, db_mix0, db_mix1, dg_ffn0, dg_ffn1, db_ffn0, db_ffn1,
                             dg_norm, jnp.zeros((5, D), F32)], axis=0)
    small = _all_reduce_small(small)
    my_chip = 2 * lax.axis_index("x") + lax.axis_index("y")
    nsh = hgrn_norm_g.shape[1]
    g_norm = lax.dynamic_slice(small[10:11], (0, my_chip * nsh), (1, nsh))

    names = ["attn_w_in", "attn_w_out", "hgrn_w_in", "hgrn_w_out", "ffn_w_up", "ffn_w_down"]
    ws = [attn_w_in, attn_w_out, hgrn_w_in, hgrn_w_out, ffn_w_up, ffn_w_down]
    ms = [m_attn_w_in, m_attn_w_out, m_hgrn_w_in, m_hgrn_w_out, m_ffn_w_up, m_ffn_w_down]
    vs = [v_attn_w_in, v_attn_w_out, v_hgrn_w_in, v_hgrn_w_out, v_ffn_w_up, v_ffn_w_down]
    grads, upd = {}, {}
    for nm, w, gr, m, v in zip(names, ws, g_big, ms, vs):
        grads[nm] = gr.reshape(w.shape)
        upd[nm] = _adamw("adamw_" + nm, w, grads[nm], m, v)
    grads["hgrn_norm_g"] = g_norm
    upd["hgrn_norm_g"] = _adamw("adamw_hgrn_norm_g", hgrn_norm_g, g_norm, m_hgrn_norm_g, v_hgrn_norm_g)
    cat = lambda ts: jnp.concatenate(ts, axis=0)
    small_w = cat([lb_logits, ln_mix_g, ln_mix_b, ln_ffn_g, ln_ffn_b])
    small_m = cat([m_lb_logits, m_ln_mix_g, m_ln_mix_b, m_ln_ffn_g, m_ln_ffn_b])
    small_v = cat([v_lb_logits, v_ln_mix_g, v_ln_mix_b, v_ln_ffn_g, v_ln_ffn_b])
    small_upd = _adamw("adamw_small", small_w, small[0:10], small_m, small_v)
    for i, nm in enumerate(["lb_logits", "ln_mix_g", "ln_mix_b", "ln_ffn_g", "ln_ffn_b"]):
        grads[nm] = small[2 * i:2 * i + 2]
        upd[nm] = tuple(t[2 * i:2 * i + 2] for t in small_upd)

    order = ["attn_w_in", "attn_w_out", "hgrn_w_in", "hgrn_w_out", "hgrn_norm_g", "lb_logits", "ln_mix_g", "ln_mix_b",
             "ln_ffn_g", "ln_ffn_b", "ffn_w_up", "ffn_w_down"]
    return (loss, grad_x[None], *[grads[k] for k in order], *[upd[k][0] for k in order],
            *[upd[k][1] for k in order], *[upd[k][2] for k in order])
```

```python
import functools
import math

import jax
import jax.numpy as jnp
from jax import lax
from jax.experimental import pallas as pl
from jax.experimental.pallas import tpu as pltpu

F32 = jnp.float32
BF16 = jnp.bfloat16
MXU_DTYPE = BF16

HEAD_DIM = 64
ATTN_BLK = 128
DILATIONS = (1, 4, 16)
ROPE_THETA = 10000.0
HGRN_DK = 128
HGRN_CHUNK = 64
DEPTH = 2
LN_EPS = 1e-5
RMS_EPS = 1e-6
ALPHA = (2 * DEPTH) ** 0.25
ADAM_LR, ADAM_B1, ADAM_B2, ADAM_EPS, ADAM_WD, ADAM_STEP = 0.001, 0.9, 0.999, 1e-08, 0.01, 10

LANES = 128
VMEM_LIMIT = 56 * 1024 * 1024
NEG = -1e30
MESH = pl.DeviceIdType.MESH


def _cparams(sem=None):
    return pltpu.CompilerParams(dimension_semantics=sem, vmem_limit_bytes=VMEM_LIMIT)


def _sds(shape, dtype):
    return jax.ShapeDtypeStruct(tuple(shape), dtype)


def _dg(a, b, ca, cb):
    return lax.dot_general(a, b, (((ca,), (cb,)), ((), ())), preferred_element_type=F32)


def _nn(a, b):
    return _dg(a, b, 1, 0)


def _nt(a, b):
    return _dg(a, b, 1, 1)


def _tn(a, b):
    return _dg(a, b, 0, 0)


def _split3(a):
    hi = a.astype(BF16)
    r = a - hi.astype(F32)
    mid = r.astype(BF16)
    lo = (r - mid.astype(F32)).astype(BF16)
    return hi, mid, lo


def _exact_nn(a, sel):
    hi, mid, lo = _split3(a)
    return _nn(hi, sel) + _nn(mid, sel) + _nn(lo, sel)


def _exact_sel_nn(sel, a):
    hi, mid, lo = _split3(a)
    return _nn(sel, hi) + _nn(sel, mid) + _nn(sel, lo)


def _pick(n, prefs):
    for p in prefs:
        if n % p == 0:
            return p
    return n


def _matmul(name, a, b, form, tm, tn, tk, outs, epilogue, extras=(), a_map=None, b_map=None, mnk=None, dep=None):
    if form == "nn":
        (M, K), N = a.shape, b.shape[1]
        a_spec = pl.BlockSpec((tm, tk), a_map or (lambda i, j, k: (i, k)))
        b_spec = pl.BlockSpec((tk, tn), b_map or (lambda i, j, k: (k, j)))
        ca, cb = 1, 0
    elif form == "nt":
        (M, K), N = a.shape, b.shape[0]
        a_spec = pl.BlockSpec((tm, tk), a_map or (lambda i, j, k: (i, k)))
        b_spec = pl.BlockSpec((tn, tk), b_map or (lambda i, j, k: (j, k)))
        ca, cb = 1, 1
    else:
        (K, M), N = a.shape, b.shape[1]
        a_spec = pl.BlockSpec((tk, tm), a_map or (lambda i, j, k: (k, i)))
        b_spec = pl.BlockSpec((tk, tn), b_map or (lambda i, j, k: (k, j)))
        ca, cb = 0, 0
    if mnk is not None:
        M, N, K = mnk
    assert M % tm == 0 and N % tn == 0 and K % tk == 0, (name, M, N, K, tm, tn, tk)
    nk = K // tk
    ne, no = len(extras), len(outs)
    deps = [] if dep is None else [dep]
    nd = len(deps)

    def body(a_ref, b_ref, *rest):
        extra_refs, out_refs = rest[:ne], rest[ne + nd:ne + nd + no]
        j = pl.program_id(1)
        part = _dg(a_ref[...].astype(MXU_DTYPE), b_ref[...].astype(MXU_DTYPE), ca, cb)
        if nk == 1:
            epilogue(part, extra_refs, out_refs, j)
            return
        acc_ref = rest[-1]
        k = pl.program_id(2)

        @pl.when(k == 0)
        def _():
            acc_ref[...] = part

        @pl.when(k > 0)
        def _():
            acc_ref[...] += part

        @pl.when(k == nk - 1)
        def _():
            epilogue(acc_ref[...], extra_refs, out_refs, j)

    res = pl.pallas_call(
        body,
        name=name,
        grid=(M // tm, N // tn, nk),
        in_specs=[a_spec, b_spec] + [s for _, s in extras] + [pl.BlockSpec(memory_space=pl.ANY)] * nd,
        out_specs=[s for _, s in outs],
        out_shape=[o for o, _ in outs],
        scratch_shapes=[pltpu.VMEM((tm, tn), F32)] if nk > 1 else [],
        compiler_params=_cparams(("parallel", "parallel", "arbitrary")),
    )(a, b, *[e for e, _ in extras], *deps)
    return res


def _ij_spec(tm, tn):
    return pl.BlockSpec((tm, tn), lambda i, j, k: (i, j))


def _store_epilogue(acc, extra_refs, out_refs, j):
    out_refs[0][...] = acc.astype(out_refs[0].dtype)


def _plain_mm(name, a, b, form, out_dtype, tm, tn, tk):
    M = a.shape[1] if form == "tn" else a.shape[0]
    N = b.shape[0] if form == "nt" else b.shape[1]
    return _matmul(name, a, b, form, tm, tn, tk, [(_sds((M, N), out_dtype), _ij_spec(tm, tn))], _store_epilogue)[0]


def _to_classes(t, dil):
    S, W = t.shape
    return t if dil == 1 else t.reshape(S // dil, dil, W).transpose(1, 0, 2).reshape(S, W)


def _from_classes(t, dil):
    S, W = t.shape
    return t if dil == 1 else t.reshape(dil, S // dil, W).transpose(1, 0, 2).reshape(S, W)


def _stack_classes(t):
    return jnp.concatenate([_to_classes(t, d) for d in DILATIONS], axis=0)


def _rope_tables(seq):
    half = HEAD_DIM // 2
    inv = ROPE_THETA ** (-jnp.arange(half, dtype=F32) * (2.0 / HEAD_DIM))
    ang = jnp.arange(seq).astype(F32)[:, None] * inv[None, :]
    reps = LANES // half
    cos = jnp.tile(jnp.cos(ang), (1, reps))
    sin = jnp.tile(jnp.sin(ang), (1, reps))
    first = (jnp.arange(LANES) % HEAD_DIM) < half
    sin_fwd = jnp.where(first[None, :], -sin, sin)
    return cos, sin_fwd


def _partner(x):
    half = HEAD_DIM // 2
    lane = lax.broadcasted_iota(jnp.int32, x.shape, 1)
    first = (lane % HEAD_DIM) < half
    return jnp.where(first, pltpu.roll(x, LANES - half, 1), pltpu.roll(x, half, 1))


def _attn_proj(x3, w_full, cos3, sin3, tm, tn):
    S3, D = x3.shape
    S = S3 // 3
    per_part = D // tn
    per_group = 3 * per_part

    def epilogue(acc, extra_refs, out_refs, j):
        cos_ref, sin_ref = extra_refs
        o_ref = out_refs[0]
        is_rot = j // per_part < 2

        @pl.when(is_rot)
        def _():
            c, s = cos_ref[...], sin_ref[...]
            for t in range(tn // LANES):
                xs = acc[:, t * LANES:(t + 1) * LANES]
                o_ref[:, t * LANES:(t + 1) * LANES] = (xs * c + _partner(xs) * s).astype(o_ref.dtype)

        @pl.when(jnp.logical_not(is_rot))
        def _():
            o_ref[...] = acc.astype(o_ref.dtype)

    tab = pl.BlockSpec((tm, LANES), lambda i, j, k: (i, 0))
    return _matmul("attn_proj", x3, w_full, "nn", tm, tn, D, [(_sds((S3, 3 * D), MXU_DTYPE), _ij_spec(tm, tn))],
                   epilogue, extras=[(cos3, tab), (sin3, tab)],
                   b_map=lambda i, j, k: (k, j + (i // (S // tm)) * per_group), mnk=(S3, 3 * D, D))[0]


def _head_sel(d_model):
    h = jnp.arange(LANES)[:, None]
    l = jnp.arange(d_model)[None, :]
    return (l // HEAD_DIM == h).astype(BF16)


def _class_edges(b, nblk):
    g = b // nblk
    per_class = jnp.where(g == 0, nblk // DILATIONS[0], jnp.where(g == 1, nblk // DILATIONS[1], nblk // DILATIONS[2]))
    pos = (b % nblk) % per_class
    return pos != 0, pos != per_class - 1


def _two_heads(t, top):
    zero = jnp.zeros_like(t)
    return jnp.concatenate([jnp.where(top, t, zero), jnp.where(top, zero, t)], axis=0)


def _band_mask(has_prev):
    B = ATTN_BLK
    row = lax.broadcasted_iota(jnp.int32, (2 * B, 2 * B), 0) % B
    col = lax.broadcasted_iota(jnp.int32, (2 * B, 2 * B), 1)
    in_prev = jnp.logical_and(jnp.logical_and(col < B, col >= row), has_prev)
    in_own = jnp.logical_and(col >= B, col - B <= row)
    return jnp.logical_or(in_prev, in_own)


def _attn_fwd(P3, D):
    S3 = P3.shape[0]
    B = ATTN_BLK
    nblk = S3 // 3 // B
    npairs = D // LANES
    scale = HEAD_DIM ** -0.5

    def body(q_ref, kc_ref, vc_ref, kp_ref, vp_ref, o_ref, lse_ref):
        has_prev, _ = _class_edges(pl.program_id(0), nblk)
        ok = _band_mask(has_prev)
        lane = lax.broadcasted_iota(jnp.int32, (B, LANES), 1)
        top = lane < HEAD_DIM
        lse_acc = jnp.zeros((B, LANES), F32)
        for j in range(npairs):
            sl = slice(j * LANES, (j + 1) * LANES)
            Q = _two_heads(q_ref[:, sl] * scale, top)
            K2 = jnp.concatenate([kp_ref[:, sl], kc_ref[:, sl]], axis=0)
            V2 = jnp.concatenate([vp_ref[:, sl], vc_ref[:, sl]], axis=0)
            s = jnp.where(ok, _nt(Q, K2), NEG)
            m = jnp.max(s, axis=1, keepdims=True)
            p = jnp.exp(s - m)
            l = jnp.sum(p, axis=1, keepdims=True)
            o = _nn((p * (1.0 / l)).astype(MXU_DTYPE), V2)
            o_ref[:, sl] = jnp.where(top, o[:B], o[B:])
            lse = m + jnp.log(l)
            lse_acc = jnp.where(lane == 2 * j, lse[:B], jnp.where(lane == 2 * j + 1, lse[B:], lse_acc))
        lse_ref[...] = lse_acc

    blk = lambda part, prev: pl.BlockSpec(
        (B, D), (lambda b: (jnp.maximum(b - 1, 0), part)) if prev else (lambda b: (b, part)))
    return pl.pallas_call(
        body,
        name="attn_fwd",
        grid=(3 * nblk,),
        in_specs=[blk(0, False), blk(1, False), blk(2, False), blk(1, True), blk(2, True)],
        out_specs=[pl.BlockSpec((B, D), lambda b: (b, 0)), pl.BlockSpec((B, LANES), lambda b: (b, 0))],
        out_shape=[_sds((S3, D), F32), _sds((S3, LANES), F32)],
        compiler_params=_cparams(("parallel",)),
    )(P3, P3, P3, P3, P3)


def _attn_mix(os, lses, sel, tm):
    S, D = min(o.shape[0] for o in os), os[0].shape[1]

    def body(o0, o1, o2, l0, l1, l2, sel_ref, o_ref, L_ref):
        a, b, c = l0[...], l1[...], l2[...]
        m = jnp.maximum(jnp.maximum(a, b), c)
        L = m + jnp.log(jnp.exp(a - m) + jnp.exp(b - m) + jnp.exp(c - m))
        L_ref[...] = L
        s = sel_ref[...]
        acc = _exact_nn(jnp.exp(a - L), s) * o0[...]
        acc += _exact_nn(jnp.exp(b - L), s) * o1[...]
        acc += _exact_nn(jnp.exp(c - L), s) * o2[...]
        o_ref[...] = acc

    big = pl.BlockSpec((tm, D), lambda i: (i, 0))
    small = pl.BlockSpec((tm, LANES), lambda i: (i, 0))
    return pl.pallas_call(
        body,
        name="attn_mix",
        grid=(S // tm,),
        in_specs=[big, big, big, small, small, small, pl.BlockSpec((LANES, D), lambda i: (0, 0))],
        out_specs=[big, small],
        out_shape=[_sds((S, D), F32), _sds((S, LANES), F32)],
        compiler_params=_cparams(("parallel",)),
    )(*os, *lses, sel)


def _attn_bwd(P3, do3, L3, delta3, cos3, sin3, D):
    S3 = P3.shape[0]
    B = ATTN_BLK
    nblk = S3 // 3 // B
    npairs = D // LANES
    scale = HEAD_DIM ** -0.5

    def body(c_ref, p_ref, n_ref, doc_ref, don_ref, Lc_ref, Ln_ref, dc_ref, dn_ref, cos_ref, sin_ref, out_ref):
        has_prev, has_next = _class_edges(pl.program_id(0), nblk)
        ok = _band_mask(has_prev)
        row = lax.broadcasted_iota(jnp.int32, (2 * B, B), 0) % B
        col = lax.broadcasted_iota(jnp.int32, (2 * B, B), 1)
        ok_n = jnp.logical_and(col >= row, has_next)
        lane = lax.broadcasted_iota(jnp.int32, (B, LANES), 1)
        top = lane < HEAD_DIM
        cos_t = cos_ref[...]
        sin_inv = -sin_ref[...]
        Lc_all, Ln_all, dc_all, dn_all = Lc_ref[...], Ln_ref[...], dc_ref[...], dn_ref[...]
        pair_col = lambda t, j: jnp.concatenate([t[:, 2 * j:2 * j + 1], t[:, 2 * j + 1:2 * j + 2]], axis=0)
        for j in range(npairs):
            sl = lambda part: slice(part * D + j * LANES, part * D + (j + 1) * LANES)
            kc2, vc2 = c_ref[:, sl(1)], c_ref[:, sl(2)]
            K2 = jnp.concatenate([p_ref[:, sl(1)], kc2], axis=0)
            V2 = jnp.concatenate([p_ref[:, sl(2)], vc2], axis=0)
            Qc = _two_heads(c_ref[:, sl(0)] * scale, top)
            Qn = _two_heads(n_ref[:, sl(0)] * scale, top)
            DOc = _two_heads(doc_ref[:, j * LANES:(j + 1) * LANES].astype(MXU_DTYPE), top)
            DOn = _two_heads(don_ref[:, j * LANES:(j + 1) * LANES].astype(MXU_DTYPE), top)
            P_c = jnp.where(ok, jnp.exp(_nt(Qc, K2) - pair_col(Lc_all, j)), 0.0)
            dS_c = P_c * (_nt(DOc, V2) - pair_col(dc_all, j))
            P_n = jnp.where(ok_n, jnp.exp(_nt(Qn, kc2) - pair_col(Ln_all, j)), 0.0)
            dS_n = P_n * (_nt(DOn, vc2) - pair_col(dn_all, j))
            dq = _nn(dS_c.astype(MXU_DTYPE), K2)
            dq2 = jnp.where(top, dq[:B], dq[B:]) * scale
            Qk = jnp.concatenate([Qc, Qn], axis=0)
            DOk = jnp.concatenate([DOc, DOn], axis=0)
            dk2 = _tn(jnp.concatenate([dS_c[:, B:], dS_n], axis=0).astype(MXU_DTYPE), Qk)
            dv2 = _tn(jnp.concatenate([P_c[:, B:], P_n], axis=0).astype(MXU_DTYPE), DOk)
            out_ref[:, sl(0)] = (dq2 * cos_t + _partner(dq2) * sin_inv).astype(out_ref.dtype)
            out_ref[:, sl(1)] = (dk2 * cos_t + _partner(dk2) * sin_inv).astype(out_ref.dtype)
            out_ref[:, sl(2)] = dv2.astype(out_ref.dtype)

    cur = lambda b: b
    prv = lambda b: jnp.maximum(b - 1, 0)
    nxt = lambda b: jnp.minimum(b + 1, 3 * nblk - 1)
    spec = lambda w, f: pl.BlockSpec((B, w), lambda b: (f(b), 0))
    return pl.pallas_call(
        body,
        name="attn_bwd",
        grid=(3 * nblk,),
        in_specs=[spec(3 * D, cur), spec(3 * D, prv), spec(3 * D, nxt), spec(D, cur), spec(D, nxt),
                  spec(LANES, cur), spec(LANES, nxt), spec(LANES, cur), spec(LANES, nxt), spec(LANES, cur), spec(LANES, cur)],
        out_specs=spec(3 * D, cur),
        out_shape=_sds((S3, 3 * D), MXU_DTYPE),
        compiler_params=_cparams(("parallel",)),
    )(P3, P3, P3, do3, do3, L3, L3, delta3, delta3, cos3, sin3)


def _sum4(a, b, c, d, tm):
    S, D = a.shape

    def body(a_ref, b_ref, c_ref, d_ref, o_ref):
        o_ref[...] = ALPHA * a_ref[...] + b_ref[...] + c_ref[...] + d_ref[...]

    row = pl.BlockSpec((tm, D), lambda i: (i, 0))
    return pl.pallas_call(body, name="sum4", grid=(S // tm,), in_specs=[row] * 4, out_specs=row,
                          out_shape=_sds((S, D), F32), compiler_params=_cparams(("parallel",)))(a, b, c, d)


def _tri(lower):
    r = lax.broadcasted_iota(jnp.int32, (HGRN_CHUNK, HGRN_CHUNK), 0)
    c = lax.broadcasted_iota(jnp.int32, (HGRN_CHUNK, HGRN_CHUNK), 1)
    return ((r >= c) if lower else (r <= c)).astype(BF16)


def _lower_bound(lb_ref):
    l0, l1 = lb_ref[0:1, :], lb_ref[1:2, :]
    m = jnp.maximum(l0, l1)
    e0, e1 = jnp.exp(l0 - m), jnp.exp(l1 - m)
    return e1 / (e0 + e1)


def _hgrn_gates(q_raw, z, lb):
    sg = 1.0 / (1.0 + jnp.exp(-z))
    sn = 1.0 / (1.0 + jnp.exp(z))
    f = lb + (1.0 - lb) * sg
    key = (1.0 - lb) * sn
    sq = 1.0 / (1.0 + jnp.exp(-q_raw))
    return sg, sn, f, key, sq


def _hgrn_fwd(P1, lb_logits, norm_g, tb):
    S = P1.shape[0]
    D = P1.shape[1] // 3
    H = D // HGRN_DK
    C = HGRN_CHUNK
    cpb = tb // C
    nt = S // tb

    def body(q_ref, f_ref, i_ref, lb_ref, g_ref, o_ref, n_ref, st_ref, state):
        t = pl.program_id(1)

        @pl.when(t == 0)
        def _():
            state[...] = jnp.zeros_like(state)

        lb = _lower_bound(lb_ref)
        gn = g_ref[...]
        tri = _tri(True)
        r = lax.broadcasted_iota(jnp.int32, (C, C), 0)
        c = lax.broadcasted_iota(jnp.int32, (C, C), 1)
        causal = r >= c
        for ci in range(cpb):
            rows = slice(ci * C, (ci + 1) * C)
            q_raw, z, v = q_ref[rows, :], f_ref[rows, :], i_ref[rows, :]
            sg, sn, f, key, sq = _hgrn_gates(q_raw, z, lb)
            q = q_raw * sq
            b = _exact_sel_nn(tri, jnp.log(f))
            b_last = b[C - 1:C, :]
            qd = (q * jnp.exp(b)).astype(MXU_DTYPE)
            kd = (key * jnp.exp(-b)).astype(MXU_DTYPE)
            kb = (key * jnp.exp(b_last - b)).astype(MXU_DTYPE)
            vm = v.astype(MXU_DTYPE)
            st = state[...]
            st_ref[ci] = st
            a = jnp.where(causal, _nt(qd, kd), 0.0)
            o = _nn(a.astype(MXU_DTYPE), vm) + _nt(qd, st.astype(MXU_DTYPE))
            state[...] = st * jnp.exp(b_last) + _tn(vm, kb)
            o_ref[rows, :] = o
            rs = lax.rsqrt(jnp.mean(o * o, axis=1, keepdims=True) + RMS_EPS)
            n_ref[rows, :] = o * rs * gn

    tok = lambda part: pl.BlockSpec((tb, HGRN_DK), lambda h, t: (t, part * H + h))
    vec = lambda rows: pl.BlockSpec((rows, HGRN_DK), lambda h, t: (0, h))
    return pl.pallas_call(
        body,
        name="hgrn_fwd",
        grid=(H, nt),
        in_specs=[tok(0), tok(1), tok(2), vec(2), vec(1)],
        out_specs=[tok(0), tok(0), pl.BlockSpec((None, cpb, HGRN_DK, HGRN_DK), lambda h, t: (h, t, 0, 0))],
        out_shape=[_sds((S, D), F32), _sds((S, D), F32), _sds((H, S // C, HGRN_DK, HGRN_DK), F32)],
        scratch_shapes=[pltpu.VMEM((HGRN_DK, HGRN_DK), F32)],
        compiler_params=_cparams(("parallel", "arbitrary")),
    )(P1, P1, P1, lb_logits, norm_g)


def _hgrn_bwd(P1, o_pre, states, dn, lb_logits, norm_g, tb):
    S = P1.shape[0]
    D = P1.shape[1] // 3
    H = D // HGRN_DK
    C = HGRN_CHUNK
    cpb = tb // C
    nt = S // tb

    def body(q_ref, f_ref, i_ref, o_ref, st_ref, dn_ref, lb_ref, g_ref, dq_ref, dz_ref, dv_ref, dg_ref, dlb_ref, dstate):
        t = pl.program_id(1)

        @pl.when(t == 0)
        def _():
            dstate[...] = jnp.zeros_like(dstate)
            dg_ref[...] = jnp.zeros_like(dg_ref)
            dlb_ref[...] = jnp.zeros_like(dlb_ref)

        lb = _lower_bound(lb_ref)
        gn = g_ref[...]
        tri_l, tri_u = _tri(True), _tri(False)
        r = lax.broadcasted_iota(jnp.int32, (C, C), 0)
        c = lax.broadcasted_iota(jnp.int32, (C, C), 1)
        causal = r >= c
        last_row = lax.broadcasted_iota(jnp.int32, (C, HGRN_DK), 0) == C - 1
        dg_acc = jnp.zeros((1, HGRN_DK), F32)
        dlb_acc = jnp.zeros((1, HGRN_DK), F32)
        for ci in reversed(range(cpb)):
            rows = slice(ci * C, (ci + 1) * C)
            q_raw, z, v = q_ref[rows, :], f_ref[rows, :], i_ref[rows, :]
            sg, sn, f, key, sq = _hgrn_gates(q_raw, z, lb)
            q = q_raw * sq
            b = _exact_sel_nn(tri_l, jnp.log(f))
            b_last = b[C - 1:C, :]
            e_pos, e_neg, e_rel = jnp.exp(b), jnp.exp(-b), jnp.exp(b_last - b)
            dec = jnp.exp(b_last)
            qd_f, kd_f, kb_f = q * e_pos, key * e_neg, key * e_rel
            qd, kd, kb = qd_f.astype(MXU_DTYPE), kd_f.astype(MXU_DTYPE), kb_f.astype(MXU_DTYPE)
            vm = v.astype(MXU_DTYPE)
            st = st_ref[ci]
            dst = dstate[...]
            stm, dstm = st.astype(MXU_DTYPE), dst.astype(MXU_DTYPE)
            a = jnp.where(causal, _nt(qd, kd), 0.0).astype(MXU_DTYPE)
            o = o_ref[rows, :]
            dnn = dn_ref[rows, :]
            rs = lax.rsqrt(jnp.mean(o * o, axis=1, keepdims=True) + RMS_EPS)
            dg_acc += jnp.sum(dnn * o * rs, axis=0, keepdims=True)
            tg = dnn * gn
            do_f = rs * tg - o * (rs * rs * rs) * jnp.mean(tg * o, axis=1, keepdims=True)
            dom = do_f.astype(MXU_DTYPE)
            da = jnp.where(causal, _nt(dom, vm), 0.0).astype(MXU_DTYPE)
            dv = _tn(a, dom) + _nt(kb, dstm)
            dqd = _nn(da, kd) + _nn(dom, stm)
            dkd = _tn(da, qd)
            dkb = _nn(vm, dstm)
            ddec = jnp.sum(dst * st, axis=0, keepdims=True)
            dstate[...] = dst * dec + _tn(dom, qd)
            dq = dqd * e_pos
            dkey = dkd * e_neg + dkb * e_rel
            tk = dkb * kb_f
            db_last = jnp.sum(tk, axis=0, keepdims=True) + ddec * dec
            db = dqd * qd_f - dkd * kd_f - tk
            db = jnp.where(last_row, db + db_last, db)
            dlogf = _exact_sel_nn(tri_u, db)
            gz = (1.0 - lb) * sg * sn
            dz_ref[rows, :] = (dlogf * gz / f - dkey * gz).astype(dz_ref.dtype)
            dlb_acc += jnp.sum(dlogf * sn / f - dkey * sn, axis=0, keepdims=True)
            dq_ref[rows, :] = (dq * (sq + q_raw * sq * (1.0 - sq))).astype(dq_ref.dtype)
            dv_ref[rows, :] = dv.astype(dv_ref.dtype)
        dg_ref[...] += dg_acc
        dlb_ref[...] += dlb_acc

    rev = lambda t: nt - 1 - t
    tok = lambda part: pl.BlockSpec((tb, HGRN_DK), lambda h, t: (rev(t), part * H + h))
    vec = lambda rows: pl.BlockSpec((rows, HGRN_DK), lambda h, t: (0, h))
    outs = pl.pallas_call(
        body,
        name="hgrn_bwd",
        grid=(H, nt),
        in_specs=[tok(0), tok(1), tok(2), tok(0),
                  pl.BlockSpec((None, cpb, HGRN_DK, HGRN_DK), lambda h, t: (h, rev(t), 0, 0)),
                  tok(0), vec(2), vec(1)],
        out_specs=[tok(0), tok(0), tok(0), vec(1), vec(1)],
        out_shape=[_sds((S, D), MXU_DTYPE)] * 3 + [_sds((1, D), F32)] * 2,
        scratch_shapes=[pltpu.VMEM((HGRN_DK, HGRN_DK), F32)],
        compiler_params=_cparams(("parallel", "arbitrary")),
    )(P1, P1, P1, o_pre, states, dn, lb_logits, norm_g)
    return outs


def _lb_logits_grad(dlb, lb_logits):
    def body(d_ref, l_ref, o_ref):
        s1 = _lower_bound(l_ref)
        d = d_ref[...]
        o_ref[0:1, :] = -(1.0 - s1) * s1 * d
        o_ref[1:2, :] = s1 * (1.0 - s1) * d

    return pl.pallas_call(body, name="lb_logits_grad", out_shape=_sds(lb_logits.shape, F32))(dlb, lb_logits)


def _ln_epilogue(acc, extra_refs, out_refs, j):
    res_ref, g_ref, b_ref = extra_refs
    x_ref, xhat_ref, rstd_ref = out_refs
    u = ALPHA * res_ref[...] + acc
    mu = jnp.mean(u, axis=1, keepdims=True)
    cen = u - mu
    rstd = lax.rsqrt(jnp.mean(cen * cen, axis=1, keepdims=True) + LN_EPS)
    xhat = cen * rstd
    xhat_ref[...] = xhat
    x_ref[...] = xhat * g_ref[...] + b_ref[...]
    rstd_ref[...] = rstd


def _mm_res_ln(name, a, w_full, res, g, b, tm, tk):
    S, D = res.shape
    row = pl.BlockSpec((tm, D), lambda i, j, k: (i, 0))
    vec = pl.BlockSpec((1, D), lambda i, j, k: (0, 0))
    outs = [(_sds((S, D), F32), row), (_sds((S, D), F32), row),
            (_sds((S, 1), F32), pl.BlockSpec((tm, 1), lambda i, j, k: (i, 0)))]
    return _matmul(name, a, w_full, "nn", tm, D, tk, outs, _ln_epilogue, extras=[(res, row), (g, vec), (b, vec)])


def _ln_bwd(name, dy, xhat, rstd, g, tm, dep):
    S, D = dy.shape

    def body(dy_ref, xh_ref, r_ref, g_ref, dep_ref, du_ref, dg_ref, db_ref):
        @pl.when(pl.program_id(0) == 0)
        def _():
            dg_ref[...] = jnp.zeros_like(dg_ref)
            db_ref[...] = jnp.zeros_like(db_ref)

        dy_, xh = dy_ref[...], xh_ref[...]
        dg_ref[...] += jnp.sum(dy_ * xh, axis=0, keepdims=True)
        db_ref[...] += jnp.sum(dy_, axis=0, keepdims=True)
        dxh = dy_ * g_ref[...]
        m1 = jnp.mean(dxh, axis=1, keepdims=True)
        m2 = jnp.mean(dxh * xh, axis=1, keepdims=True)
        du_ref[...] = r_ref[...] * (dxh - m1 - xh * m2)

    row = pl.BlockSpec((tm, D), lambda i: (i, 0))
    vec = pl.BlockSpec((1, D), lambda i: (0, 0))
    return pl.pallas_call(
        body,
        name=name,
        grid=(S // tm,),
        in_specs=[row, row, pl.BlockSpec((tm, 1), lambda i: (i, 0)), vec, pl.BlockSpec(memory_space=pl.ANY)],
        out_specs=[row, vec, vec],
        out_shape=[_sds((S, D), F32), _sds((1, D), F32), _sds((1, D), F32)],
        compiler_params=_cparams(("arbitrary",)),
    )(dy, xhat, rstd, g, dep)


def _loss_head(y, target, tm):
    S, D = y.shape

    def body(y_ref, t_ref, sq_ref, dy_ref):
        @pl.when(pl.program_id(0) == 0)
        def _():
            sq_ref[...] = jnp.zeros_like(sq_ref)

        e = y_ref[...] - t_ref[...]
        sq_ref[...] += jnp.sum(e * e, axis=0, keepdims=True)
        dy_ref[...] = e / D

    row = pl.BlockSpec((tm, D), lambda i: (i, 0))
    vec = pl.BlockSpec((1, D), lambda i: (0, 0))
    return pl.pallas_call(
        body,
        name="loss_head",
        grid=(S // tm,),
        in_specs=[row, row],
        out_specs=[vec, row],
        out_shape=[_sds((1, D), F32), _sds((S, D), F32)],
        compiler_params=_cparams(("arbitrary",)),
    )(y, target)


def _mlp_up(name, x, w_up, tm, tn, tk):
    S = x.shape[0]
    F = w_up.shape[1]

    def epilogue(acc, extra_refs, out_refs, j):
        out_refs[0][...] = acc
        r = jnp.maximum(acc, 0.0)
        out_refs[1][...] = (r * r).astype(out_refs[1].dtype)

    return _matmul(name, x, w_up, "nn", tm, tn, tk,
                   [(_sds((S, F), F32), _ij_spec(tm, tn)), (_sds((S, F), MXU_DTYPE), _ij_spec(tm, tn))], epilogue)


def _mlp_down_bwd(name, dy, w_down, h, tm, tn, tk):
    S, F = h.shape

    def epilogue(acc, extra_refs, out_refs, j):
        out_refs[0][...] = (acc * (2.0 * jnp.maximum(extra_refs[0][...], 0.0))).astype(out_refs[0].dtype)

    return _matmul(name, dy, w_down, "nt", tm, tn, tk, [(_sds((S, F), MXU_DTYPE), _ij_spec(tm, tn))], epilogue,
                   extras=[(h, _ij_spec(tm, tn))])[0]


def _mm_nt_res(name, dy, w, du, tm, tn, tk):
    S = dy.shape[0]
    N = w.shape[0]

    def epilogue(acc, extra_refs, out_refs, j):
        out_refs[0][...] = ALPHA * extra_refs[0][...] + acc

    return _matmul(name, dy, w, "nt", tm, tn, tk, [(_sds((S, N), F32), _ij_spec(tm, tn))], epilogue,
                   extras=[(du, _ij_spec(tm, tn))])[0]


def _attn_out_bwd(du, w_out, o, sel_t, tm, tk):
    S, D = o.shape

    def epilogue(acc, extra_refs, out_refs, j):
        out_refs[0][...] = acc.astype(out_refs[0].dtype)
        out_refs[1][...] = _exact_nn(acc * extra_refs[0][...], extra_refs[1][...])

    row = pl.BlockSpec((tm, D), lambda i, j, k: (i, 0))
    slim = pl.BlockSpec((tm, LANES), lambda i, j, k: (i, 0))
    return _matmul("attn_out_bwd", du, w_out, "nt", tm, D, tk,
                   [(_sds((S, D), MXU_DTYPE), row), (_sds((S, LANES), F32), slim)], epilogue,
                   extras=[(o, row), (sel_t, pl.BlockSpec((D, LANES), lambda i, j, k: (0, 0)))])


def _adamw(name, w, g, m, v):
    shape = w.shape
    cols = shape[-1]
    rows = math.prod(shape[:-1])
    w2, g2, m2, v2 = (t.reshape(rows, cols) for t in (w, g, m, v))
    tr = _pick(rows, (256, 128, 64, 32, 16, 8))
    c1 = 1.0 - ADAM_B1 ** ADAM_STEP
    c2 = 1.0 - ADAM_B2 ** ADAM_STEP

    def body(w_ref, g_ref, m_ref, v_ref, d_ref, nm_ref, nv_ref):
        gg = g_ref[...]
        nm = ADAM_B1 * m_ref[...] + (1.0 - ADAM_B1) * gg
        nv = ADAM_B2 * v_ref[...] + (1.0 - ADAM_B2) * (gg * gg)
        nm_ref[...] = nm
        nv_ref[...] = nv
        d_ref[...] = -ADAM_LR * ((nm / c1) / (jnp.sqrt(nv / c2) + ADAM_EPS) + ADAM_WD * w_ref[...])

    blk = pl.BlockSpec((tr, cols), lambda i: (i, 0))
    outs = pl.pallas_call(
        body,
        name=name,
        grid=(rows // tr,),
        in_specs=[blk] * 4,
        out_specs=[blk] * 3,
        out_shape=[_sds((rows, cols), F32)] * 3,
        compiler_params=_cparams(("parallel",)),
    )(w2, g2, m2, v2)
    return tuple(o.reshape(shape) for o in outs)


HBM = pl.BlockSpec(memory_space=pl.ANY)


def _shard_slice(ref, axis, size, index):
    idx = [slice(None)] * len(ref.shape)
    idx[axis] = pl.ds(pl.multiple_of(index * size, 8), size)
    return ref.at[tuple(idx)]


IN_HBM = pl.BlockSpec(memory_space=pltpu.HBM)
IN_SEM = pl.BlockSpec(memory_space=pltpu.SEMAPHORE)
DATAFLOW = pltpu.SideEffectType.DATAFLOW_SIDE_EFFECTING


def _hbm(t):
    return pltpu.with_memory_space_constraint(t, pltpu.HBM)


def _token_spec():
    return pl.BlockSpec(memory_space=pltpu.VMEM)


def _with_own_block(shard, axis):
    fs = list(shard.shape)
    fs[axis] *= 4
    start = [0] * len(fs)
    start[axis] = (2 * lax.axis_index("x") + lax.axis_index("y")) * shard.shape[axis]
    return lax.dynamic_update_slice(lax.empty(tuple(fs), shard.dtype), shard, start)


def _gather_copies(s_refs, f_refs, axes, send, recv, arrival):
    x, y, c = lax.axis_index("x"), lax.axis_index("y"), lax.axis_index("c")
    chips = [(1 - x, y), (x, 1 - y), (1 - x, 1 - y)]
    out = []
    for a in range(len(s_refs)):
        size = s_refs[a].shape[axes[a]]
        for k, (px, py) in enumerate(chips):
            block = (2 * px + py) if arrival else (2 * x + y)
            out.append(pltpu.make_async_remote_copy(
                src_ref=s_refs[a], dst_ref=_shard_slice(f_refs[a], axes[a], size, block), send_sem=send.at[3 * a + k],
                recv_sem=recv.at[3 * a + k], device_id=(px, py, c), device_id_type=MESH))
    return out


def _gather_start(name, shards, axes, after):
    n = len(shards)
    fulls = [_with_own_block(s, ax) for s, ax in zip(shards, axes)]

    def body(*refs):
        s_refs, f_refs = refs[:n], refs[n:2 * n]
        send, recv, token = refs[2 * n + 1], refs[2 * n + 2], refs[-1]
        for cp in _gather_copies(s_refs, f_refs, axes, send, recv, arrival=False):
            cp.start()
        token[...] = jnp.zeros_like(token)

    outs = pl.pallas_call(
        body,
        name=name,
        out_shape=(pltpu.SemaphoreType.DMA((3 * n,)), pltpu.SemaphoreType.DMA((3 * n,)),
                   *[pltpu.HBM(t.shape, t.dtype) for t in shards + fulls], _sds((8, LANES), F32)),
        in_specs=[IN_HBM] * (2 * n) + [HBM],
        out_specs=(IN_SEM, IN_SEM, *[IN_HBM] * (2 * n), _token_spec()),
        input_output_aliases={i: 2 + i for i in range(2 * n)},
        compiler_params=pltpu.CompilerParams(has_side_effects=DATAFLOW),
    )(*[_hbm(t) for t in shards + fulls], after)
    return (outs[0], outs[1], list(outs[2:2 + n]), list(outs[2 + n:2 + 2 * n]), axes), outs[-1]


def _gather_wait(name, state, after):
    send, recv, s_thru, f_thru, axes = state
    n = len(s_thru)

    def body(*refs):
        s_refs, f_refs = refs[:n], refs[n:2 * n]
        send_ref, recv_ref = refs[2 * n], refs[2 * n + 1]
        for cp in _gather_copies(s_refs, f_refs, axes, send_ref, recv_ref, arrival=True):
            cp.wait_send()
            cp.wait_recv()

    outs = pl.pallas_call(
        body,
        name=name,
        out_shape=tuple(pltpu.HBM(t.shape, t.dtype) for t in s_thru + f_thru),
        in_specs=[IN_HBM] * (2 * n) + [IN_SEM, IN_SEM, HBM],
        out_specs=tuple([IN_HBM] * (2 * n)),
        input_output_aliases={i: i for i in range(2 * n)},
        compiler_params=pltpu.CompilerParams(has_side_effects=DATAFLOW),
    )(*s_thru, *f_thru, send, recv, after)
    return list(outs[n:2 * n])


FLIPS = [(fx, fy, fc) for fx in (0, 1) for fy in (0, 1) for fc in (0, 1)][1:]


def _piece_shape(shape, axis):
    ps = list(shape)
    if axis == 0:
        ps[0] //= 8
    else:
        ps[0] //= 2
        ps[axis] //= 4
    return tuple(ps)


def _piece(ref, axis, q, c):
    shape = ref.shape
    idx = [slice(None)] * len(shape)
    if axis == 0:
        h = shape[0] // 8
        idx[0] = pl.ds(pl.multiple_of((2 * q + c) * h, 8), h)
    else:
        h, w = shape[0] // 2, shape[axis] // 4
        idx[0] = pl.ds(c * h, h)
        idx[axis] = pl.ds(pl.multiple_of(q * w, LANES if axis == len(shape) - 1 else 8), w)
    return ref.at[tuple(idx)]


def _own_piece(g, axis):
    ps = _piece_shape(g.shape, axis)
    q, c = 2 * lax.axis_index("x") + lax.axis_index("y"), lax.axis_index("c")
    start = [0] * len(ps)
    if axis == 0:
        start[0] = (2 * q + c) * ps[0]
    else:
        start[0] = c * ps[0]
        start[axis] = q * ps[axis]
    return lax.dynamic_slice(g, start, ps)


def _scatter_copies(g_refs, l_refs, axes, send, recv):
    x, y, c = lax.axis_index("x"), lax.axis_index("y"), lax.axis_index("c")
    out = []
    for a in range(len(g_refs)):
        for k, (fx, fy, fc) in enumerate(FLIPS):
            tx, ty, tc = x ^ fx, y ^ fy, c ^ fc
            out.append(pltpu.make_async_remote_copy(
                src_ref=_piece(g_refs[a], axes[a], 2 * tx + ty, tc), dst_ref=l_refs[a].at[k],
                send_sem=send.at[7 * a + k], recv_sem=recv.at[7 * a + k], device_id=(tx, ty, tc), device_id_type=MESH))
    return out


def _scatter_start(name, grads, axes):
    n = len(grads)
    lands = [lax.empty((7,) + _piece_shape(g.shape, ax), g.dtype) for g, ax in zip(grads, axes)]

    def body(*refs):
        g_refs, l_refs = refs[:n], refs[n:2 * n]
        send, recv, token = refs[2 * n], refs[2 * n + 1], refs[-1]
        for cp in _scatter_copies(g_refs, l_refs, axes, send, recv):
            cp.start()
        token[...] = jnp.zeros_like(token)

    outs = pl.pallas_call(
        body,
        name=name,
        out_shape=(pltpu.SemaphoreType.DMA((7 * n,)), pltpu.SemaphoreType.DMA((7 * n,)),
                   *[pltpu.HBM(t.shape, t.dtype) for t in grads + lands], _sds((8, LANES), F32)),
        in_specs=[IN_HBM] * (2 * n),
        out_specs=(IN_SEM, IN_SEM, *[IN_HBM] * (2 * n), _token_spec()),
        input_output_aliases={i: 2 + i for i in range(2 * n)},
        compiler_params=pltpu.CompilerParams(has_side_effects=DATAFLOW),
    )(*[_hbm(t) for t in grads + lands])
    return (outs[0], outs[1], list(outs[2:2 + n]), list(outs[2 + n:2 + 2 * n]), axes), outs[-1]


def _scatter_wait(name, state, after):
    send, recv, g_thru, l_thru, axes = state
    n = len(g_thru)

    def body(*refs):
        g_refs, l_refs = refs[:n], refs[n:2 * n]
        for cp in _scatter_copies(g_refs, l_refs, axes, refs[2 * n], refs[2 * n + 1]):
            cp.wait_send()
            cp.wait_recv()

    outs = pl.pallas_call(
        body,
        name=name,
        out_shape=tuple(pltpu.HBM(t.shape, t.dtype) for t in g_thru + l_thru),
        in_specs=[IN_HBM] * (2 * n) + [IN_SEM, IN_SEM, HBM],
        out_specs=tuple([IN_HBM] * (2 * n)),
        input_output_aliases={i: i for i in range(2 * n)},
        compiler_params=pltpu.CompilerParams(has_side_effects=DATAFLOW),
    )(*g_thru, *l_thru, send, recv, after)
    return list(outs[:n]), list(outs[n:2 * n])


def _reduce_join(name, landing, own):
    piece = own.shape
    C = piece[-1]
    R = math.prod(piece[:-1])
    l3 = landing.reshape(7, R, C)
    own2 = own.reshape(R, C)
    tr = _pick(R, [t for t in (512, 256, 128, 64, 32, 16, 8) if t * C <= 256 * 1024])
    nsteps = R // tr

    def body(own_ref, l_ref, o_ref, buf, send, loc, recv):
        i = pl.program_id(0)
        x, y, c = lax.axis_index("x"), lax.axis_index("y"), lax.axis_index("c")
        sibling = (x, y, 1 - c)

        def copies(slot, step):
            dst = o_ref.at[pl.ds(pl.multiple_of(c * R + step * tr, 8), tr), :]
            return (pltpu.make_async_copy(buf.at[slot], dst, loc.at[slot]),
                    pltpu.make_async_remote_copy(src_ref=buf.at[slot], dst_ref=dst, send_sem=send.at[slot], recv_sem=recv,
                                                 device_id=sibling, device_id_type=MESH))

        @pl.when(i >= 2)
        def _():
            lc, rc = copies(i % 2, i - 2)
            lc.wait()
            rc.wait_send()

        acc = own_ref[...].astype(F32)
        for s in range(7):
            acc = acc + l_ref[s].astype(F32)
        buf[i % 2] = acc
        lc, rc = copies(i % 2, i)
        lc.start()
        rc.start()

        @pl.when(i == nsteps - 1)
        def _():
            for st in range(max(nsteps - 2, 0), nsteps):
                lc, rc = copies(st % 2, st)
                lc.wait()
                rc.wait_send()
            theirs = o_ref.at[pl.ds(pl.multiple_of((1 - c) * R, 8), R), :]
            pltpu.make_async_remote_copy(src_ref=theirs, dst_ref=theirs, send_sem=send.at[0], recv_sem=recv,
                                         device_id=sibling, device_id_type=MESH).wait_recv()

    return pl.pallas_call(
        body,
        name=name,
        grid=(nsteps,),
        in_specs=[pl.BlockSpec((tr, C), lambda i: (i, 0)), pl.BlockSpec((7, tr, C), lambda i: (0, i, 0))],
        out_specs=HBM,
        out_shape=_sds((2 * R, C), F32),
        scratch_shapes=[pltpu.VMEM((2, tr, C), F32), pltpu.SemaphoreType.DMA((2,)), pltpu.SemaphoreType.DMA((2,)),
                        pltpu.SemaphoreType.DMA(())],
        compiler_params=_cparams(("arbitrary",)),
    )(own2, l3)


def _all_reduce_small(v):
    R, D = v.shape

    def body(v_ref, o_ref, land, send, recv):
        x, y, c = lax.axis_index("x"), lax.axis_index("y"), lax.axis_index("c")
        my_slot = 4 * x + 2 * y + c
        land[my_slot] = v_ref[...]
        for k, (fx, fy, fc) in enumerate(FLIPS):
            tx, ty, tc = x ^ fx, y ^ fy, c ^ fc
            pltpu.make_async_remote_copy(src_ref=v_ref, dst_ref=land.at[my_slot], send_sem=send.at[k], recv_sem=recv.at[k],
                                         device_id=(tx, ty, tc), device_id_type=MESH).start()
        for k, (fx, fy, fc) in enumerate(FLIPS):
            tx, ty, tc = x ^ fx, y ^ fy, c ^ fc
            cp = pltpu.make_async_remote_copy(src_ref=v_ref, dst_ref=land.at[4 * tx + 2 * ty + tc], send_sem=send.at[k],
                                              recv_sem=recv.at[k], device_id=(tx, ty, tc), device_id_type=MESH)
            cp.wait_send()
            cp.wait_recv()
        acc = land[0]
        for s in range(1, 8):
            acc = acc + land[s]
        o_ref[...] = acc

    return pl.pallas_call(
        body,
        name="all_reduce_small",
        in_specs=[pl.BlockSpec(memory_space=pltpu.VMEM)],
        out_specs=pl.BlockSpec(memory_space=pltpu.VMEM),
        out_shape=_sds((R, D), F32),
        scratch_shapes=[pltpu.VMEM((8, R, D), F32), pltpu.SemaphoreType.DMA((7,)), pltpu.SemaphoreType.DMA((7,))],
    )(v)


def kernel(x, attn_w_in, attn_w_out, hgrn_w_in, hgrn_w_out, hgrn_norm_g, lb_logits, ln_mix_g, ln_mix_b, ln_ffn_g, ln_ffn_b, ffn_w_up, ffn_w_down, loss_target, m_attn_w_in, m_attn_w_out, m_hgrn_w_in, m_hgrn_w_out, m_hgrn_norm_g, m_lb_logits, m_ln_mix_g, m_ln_mix_b, m_ln_ffn_g, m_ln_ffn_b, m_ffn_w_up, m_ffn_w_down, v_attn_w_in, v_attn_w_out, v_hgrn_w_in, v_hgrn_w_out, v_hgrn_norm_g, v_lb_logits, v_ln_mix_g, v_ln_mix_b, v_ln_ffn_g, v_ln_ffn_b, v_ffn_w_up, v_ffn_w_down):
    xs = x[0]
    tgt = loss_target[0]
    S, D = xs.shape
    F = ffn_w_up.shape[2] * 4
    TM = _pick(S, (1024, 512, 256))
    TR = _pick(S, (512, 256))
    TN = _pick(D, (512, 256, 128))
    TK = _pick(D, (1024, 512, 256))
    TB = _pick(S, (512, 256))

    cast = lambda w: w.astype(MXU_DTYPE)
    st_a, tok = _gather_start("gather_a", [cast(attn_w_in[0])], [1], jnp.zeros((8, LANES), F32))
    st_b, tok = _gather_start("gather_b", [cast(attn_w_out[0]), cast(ffn_w_up[0]), cast(ffn_w_down[0])], [0, 1, 0], tok)
    st_c, tok = _gather_start("gather_c", [cast(hgrn_w_in[0]), cast(hgrn_w_out[0]), hgrn_norm_g, cast(ffn_w_up[1]),
                                           cast(ffn_w_down[1])], [1, 0, 1, 1, 0], tok)

    cos, sin_fwd = _rope_tables(S)
    sel = _head_sel(D)
    sel_t = sel.T

    xc3 = _stack_classes(xs.astype(MXU_DTYPE))
    cos3, sin3 = _stack_classes(cos), _stack_classes(sin_fwd)
    (wa_in,) = _gather_wait("gather_a_wait", st_a, tok)
    P3 = _attn_proj(xc3, wa_in, cos3, sin3, TM, TN)
    o3, lse3 = _attn_fwd(P3, D)
    back = lambda t, g: _from_classes(t[g * S:(g + 1) * S], DILATIONS[g])
    o_att, L_att = _attn_mix([o3, back(o3, 1), back(o3, 2)], [lse3, back(lse3, 1), back(lse3, 2)], sel, TR)
    wa_out, w_up0, w_down0 = _gather_wait("gather_b_wait", st_b, L_att)
    x1, xh1, r1 = _mm_res_ln("attn_out_ln", o_att, wa_out, xs, ln_mix_g[0:1], ln_mix_b[0:1], TR, TK)
    h0, a0 = _mlp_up("mlp0_up", x1, w_up0, TM, TN, TK)
    x2, xh2, r2 = _mm_res_ln("mlp0_down_ln", a0, w_down0, x1, ln_ffn_g[0:1], ln_ffn_b[0:1], TR, TK)

    wh_in, wh_out, norm_g, w_up1, w_down1 = _gather_wait("gather_c_wait", st_c, r2)
    P1 = _plain_mm("hgrn_proj", x2, wh_in, "nn", F32, TM, TN, TK)
    o_h, n_h, states = _hgrn_fwd(P1, lb_logits, norm_g, TB)
    x3, xh3, r3 = _mm_res_ln("hgrn_out_ln", n_h, wh_out, x2, ln_mix_g[1:2], ln_mix_b[1:2], TR, TK)
    h1, a1 = _mlp_up("mlp1_up", x3, w_up1, TM, TN, TK)
    x4, xh4, r4 = _mm_res_ln("mlp1_down_ln", a1, w_down1, x3, ln_ffn_g[1:2], ln_ffn_b[1:2], TR, TK)

    sq, dx4 = _loss_head(x4, tgt, TR)
    loss = lax.psum(0.5 * jnp.sum(sq) / D, ("x", "y", "c"))

    wgrad = lambda name, a, dy: _plain_mm(name, a, dy, "tn", MXU_DTYPE, TN, TK, TK)
    du4, dg_ffn1, db_ffn1 = _ln_bwd("ln_ffn1_bwd", dx4, xh4, r4, ln_ffn_g[1:2], TR, sq)
    dh1 = _mlp_down_bwd("mlp1_down_bwd", du4, w_down1, h1, TM, TN, TK)
    g_down1 = wgrad("g_down1", a1, du4)
    dx3 = _mm_nt_res("mlp1_up_bwd", dh1, w_up1, du4, TM, TN, TK)
    g_up1 = wgrad("g_up1", x3, dh1)
    sc_1, tok = _scatter_start("scatter_1", [g_down1, g_up1], [0, 1])
    du3, dg_mix1, db_mix1 = _ln_bwd("ln_mix1_bwd", dx3, xh3, r3, ln_mix_g[1:2], TR, tok)
    dn = _plain_mm("hgrn_out_bwd", du3, wh_out, "nt", F32, TM, TN, TK)
    g_hout = wgrad("g_hgrn_out", n_h, du3)
    dq_raw, dz, dv, dg_norm, dlb = _hgrn_bwd(P1, o_h, states, dn, lb_logits, norm_g, TB)
    dP1 = jnp.concatenate([dq_raw, dz, dv], axis=1)
    dx2 = _mm_nt_res("hgrn_in_bwd", dP1, wh_in, du3, TM, TN, TK)
    g_hin = wgrad("g_hgrn_in", x2, dP1)
    d_lb_logits = _lb_logits_grad(dlb, lb_logits)
    sc_2, tok = _scatter_start("scatter_2", [g_hout, g_hin], [0, 1])

    du2, dg_ffn0, db_ffn0 = _ln_bwd("ln_ffn0_bwd", dx2, xh2, r2, ln_ffn_g[0:1], TR, tok)
    dh0 = _mlp_down_bwd("mlp0_down_bwd", du2, w_down0, h0, TM, TN, TK)
    g_down0 = wgrad("g_down0", a0, du2)
    dx1 = _mm_nt_res("mlp0_up_bwd", dh0, w_up0, du2, TM, TN, TK)
    g_up0 = wgrad("g_up0", x1, dh0)
    sc_3, tok = _scatter_start("scatter_3", [g_down0, g_up0], [0, 1])
    du1, dg_mix0, db_mix0 = _ln_bwd("ln_mix0_bwd", dx1, xh1, r1, ln_mix_g[0:1], TR, tok)
    do, delta = _attn_out_bwd(du1, wa_out, o_att, sel_t, TR, TK)
    g_aout = wgrad("g_attn_out", o_att, du1)
    dP3 = _attn_bwd(P3, _stack_classes(do), _stack_classes(L_att), _stack_classes(delta), cos3, sin3, D)
    grp = lambda j: j // (3 * D // TK)
    g_ain = _matmul("g_attn_in", xc3, dP3, "tn", TN, TK, TK, [(_sds((D, 9 * D), MXU_DTYPE), _ij_spec(TN, TK))], _store_epilogue,
                    a_map=lambda i, j, k: (k + grp(j) * (S // TK), i),
                    b_map=lambda i, j, k: (k + grp(j) * (S // TK), j % (3 * D // TK)), mnk=(D, 9 * D, S))[0]
    sc_4, tok = _scatter_start("scatter_4", [g_aout, g_ain], [0, 1])
    dxc3 = _matmul("attn_in_bwd", dP3, wa_in, "nt", TM, TN, TK, [(_sds((3 * S, D), F32), _ij_spec(TM, TN))], _store_epilogue,
                   b_map=lambda i, j, k: (j, k + (i // (S // TM)) * (3 * D // TK)), mnk=(3 * S, D, 3 * D), dep=tok)[0]
    grad_x = _sum4(du1, dxc3, back(dxc3, 1), back(dxc3, 2), TR)

    def reduced(name, state, after):
        gs, lands = _scatter_wait(name + "_wait", state, after)
        return [_reduce_join(f"{name}_reduce_{i}", l, _own_piece(g, ax)) for i, (l, g, ax) in enumerate(zip(lands, gs, state[4]))]

    r_down1, r_up1 = reduced("scatter_1", sc_1, grad_x)
    r_hout, r_hin = reduced("scatter_2", sc_2, r_up1)
    r_down0, r_up0 = reduced("scatter_3", sc_3, r_hin)

    small = jnp.concatenate([d_lb_logits, dg_mix0, dg_mix1, db_mix0, db_mix1, dg_ffn0, dg_ffn1, db_ffn0, db_ffn1,
                             dg_norm, jnp.zeros((5, D), F32)], axis=0)
    small = _all_reduce_small(small)
    my_chip = 2 * lax.axis_index("x") + lax.axis_index("y")
    nsh = hgrn_norm_g.shape[1]
    g_norm = lax.dynamic_slice(small[10:11], (0, my_chip * nsh), (1, nsh))

    grads, upd = {}, {}

    def update(nm, w, gr, m, v):
        grads[nm] = gr.reshape(w.shape)
        upd[nm] = _adamw("adamw_" + nm, w, grads[nm], m, v)

    update("hgrn_w_in", hgrn_w_in, r_hin, m_hgrn_w_in, v_hgrn_w_in)
    update("hgrn_w_out", hgrn_w_out, r_hout, m_hgrn_w_out, v_hgrn_w_out)
    update("ffn_w_up", ffn_w_up, jnp.stack([r_up0, r_up1]), m_ffn_w_up, v_ffn_w_up)
    update("ffn_w_down", ffn_w_down, jnp.stack([r_down0, r_down1]), m_ffn_w_down, v_ffn_w_down)
    r_aout, r_ain = reduced("scatter_4", sc_4, upd["ffn_w_down"][2])
    update("attn_w_in", attn_w_in, r_ain, m_attn_w_in, v_attn_w_in)
    update("attn_w_out", attn_w_out, r_aout, m_attn_w_out, v_attn_w_out)
    grads["hgrn_norm_g"] = g_norm
    upd["hgrn_norm_g"] = _adamw("adamw_hgrn_norm_g", hgrn_norm_g, g_norm, m_hgrn_norm_g, v_hgrn_norm_g)
    cat = lambda ts: jnp.concatenate(ts, axis=0)
    small_w = cat([lb_logits, ln_mix_g, ln_mix_b, ln_ffn_g, ln_ffn_b])
    small_m = cat([m_lb_logits, m_ln_mix_g, m_ln_mix_b, m_ln_ffn_g, m_ln_ffn_b])
    small_v = cat([v_lb_logits, v_ln_mix_g, v_ln_mix_b, v_ln_ffn_g, v_ln_ffn_b])
    small_upd = _adamw("adamw_small", small_w, small[0:10], small_m, small_v)
    for i, nm in enumerate(["lb_logits", "ln_mix_g", "ln_mix_b", "ln_ffn_g", "ln_ffn_b"]):
        grads[nm] = small[2 * i:2 * i + 2]
        upd[nm] = tuple(t[2 * i:2 * i + 2] for t in small_upd)

    order = ["attn_w_in", "attn_w_out", "hgrn_w_in", "hgrn_w_out", "hgrn_norm_g", "lb_logits", "ln_mix_g", "ln_mix_b",
             "ln_ffn_g", "ln_ffn_b", "ffn_w_up", "ffn_w_down"]
    return (loss, grad_x[None], *[grads[k] for k in order], *[upd[k][0] for k in order],
            *[upd[k][1] for k in order], *[upd[k][2] for k in order])
```

```python
import functools
import math

import jax
import jax.numpy as jnp
from jax import lax
from jax.experimental import pallas as pl
from jax.experimental.pallas import tpu as pltpu

F32 = jnp.float32
BF16 = jnp.bfloat16
MXU_DTYPE = BF16

HEAD_DIM = 64
ATTN_BLK = 128
DILATIONS = (1, 4, 16)
ROPE_THETA = 10000.0
HGRN_DK = 128
HGRN_CHUNK = 64
DEPTH = 2
LN_EPS = 1e-5
RMS_EPS = 1e-6
ALPHA = (2 * DEPTH) ** 0.25
ADAM_LR, ADAM_B1, ADAM_B2, ADAM_EPS, ADAM_WD, ADAM_STEP = 0.001, 0.9, 0.999, 1e-08, 0.01, 10

LANES = 128
VMEM_LIMIT = 56 * 1024 * 1024
NEG = -1e30
MESH = pl.DeviceIdType.MESH


def _cparams(sem=None):
    return pltpu.CompilerParams(dimension_semantics=sem, vmem_limit_bytes=VMEM_LIMIT)


def _sds(shape, dtype):
    return jax.ShapeDtypeStruct(tuple(shape), dtype)


def _dg(a, b, ca, cb):
    return lax.dot_general(a, b, (((ca,), (cb,)), ((), ())), preferred_element_type=F32)


def _nn(a, b):
    return _dg(a, b, 1, 0)


def _nt(a, b):
    return _dg(a, b, 1, 1)


def _tn(a, b):
    return _dg(a, b, 0, 0)


def _split3(a):
    hi = a.astype(BF16)
    r = a - hi.astype(F32)
    mid = r.astype(BF16)
    lo = (r - mid.astype(F32)).astype(BF16)
    return hi, mid, lo


def _exact_nn(a, sel):
    hi, mid, lo = _split3(a)
    return _nn(hi, sel) + _nn(mid, sel) + _nn(lo, sel)


def _exact_sel_nn(sel, a):
    hi, mid, lo = _split3(a)
    return _nn(sel, hi) + _nn(sel, mid) + _nn(sel, lo)


def _pick(n, prefs):
    for p in prefs:
        if n % p == 0:
            return p
    return n


def _matmul(name, a, b, form, tm, tn, tk, outs, epilogue, extras=(), a_map=None, b_map=None, mnk=None, dep=None):
    if form == "nn":
        (M, K), N = a.shape, b.shape[1]
        a_spec = pl.BlockSpec((tm, tk), a_map or (lambda i, j, k: (i, k)))
        b_spec = pl.BlockSpec((tk, tn), b_map or (lambda i, j, k: (k, j)))
        ca, cb = 1, 0
    elif form == "nt":
        (M, K), N = a.shape, b.shape[0]
        a_spec = pl.BlockSpec((tm, tk), a_map or (lambda i, j, k: (i, k)))
        b_spec = pl.BlockSpec((tn, tk), b_map or (lambda i, j, k: (j, k)))
        ca, cb = 1, 1
    else:
        (K, M), N = a.shape, b.shape[1]
        a_spec = pl.BlockSpec((tk, tm), a_map or (lambda i, j, k: (k, i)))
        b_spec = pl.BlockSpec((tk, tn), b_map or (lambda i, j, k: (k, j)))
        ca, cb = 0, 0
    if mnk is not None:
        M, N, K = mnk
    assert M % tm == 0 and N % tn == 0 and K % tk == 0, (name, M, N, K, tm, tn, tk)
    nk = K // tk
    ne, no = len(extras), len(outs)
    deps = [] if dep is None else [dep]
    nd = len(deps)

    def body(a_ref, b_ref, *rest):
        extra_refs, out_refs = rest[:ne], rest[ne + nd:ne + nd + no]
        j = pl.program_id(1)
        part = _dg(a_ref[...].astype(MXU_DTYPE), b_ref[...].astype(MXU_DTYPE), ca, cb)
        if nk == 1:
            epilogue(part, extra_refs, out_refs, j)
            return
        acc_ref = rest[-1]
        k = pl.program_id(2)

        @pl.when(k == 0)
        def _():
            acc_ref[...] = part

        @pl.when(k > 0)
        def _():
            acc_ref[...] += part

        @pl.when(k == nk - 1)
        def _():
            epilogue(acc_ref[...], extra_refs, out_refs, j)

    res = pl.pallas_call(
        body,
        name=name,
        grid=(M // tm, N // tn, nk),
        in_specs=[a_spec, b_spec] + [s for _, s in extras] + [pl.BlockSpec(memory_space=pl.ANY)] * nd,
        out_specs=[s for _, s in outs],
        out_shape=[o for o, _ in outs],
        scratch_shapes=[pltpu.VMEM((tm, tn), F32)] if nk > 1 else [],
        compiler_params=_cparams(("parallel", "parallel", "arbitrary")),
    )(a, b, *[e for e, _ in extras], *deps)
    return res


def _ij_spec(tm, tn):
    return pl.BlockSpec((tm, tn), lambda i, j, k: (i, j))


def _store_epilogue(acc, extra_refs, out_refs, j):
    out_refs[0][...] = acc.astype(out_refs[0].dtype)


def _plain_mm(name, a, b, form, out_dtype, tm, tn, tk):
    M = a.shape[1] if form == "tn" else a.shape[0]
    N = b.shape[0] if form == "nt" else b.shape[1]
    return _matmul(name, a, b, form, tm, tn, tk, [(_sds((M, N), out_dtype), _ij_spec(tm, tn))], _store_epilogue)[0]


def _to_classes(t, dil):
    S, W = t.shape
    return t if dil == 1 else t.reshape(S // dil, dil, W).transpose(1, 0, 2).reshape(S, W)


def _from_classes(t, dil):
    S, W = t.shape
    return t if dil == 1 else t.reshape(dil, S // dil, W).transpose(1, 0, 2).reshape(S, W)


def _stack_classes(t):
    return jnp.concatenate([_to_classes(t, d) for d in DILATIONS], axis=0)


def _rope_tables(seq):
    half = HEAD_DIM // 2
    inv = ROPE_THETA ** (-jnp.arange(half, dtype=F32) * (2.0 / HEAD_DIM))
    inv = jnp.tile(inv, LANES // half)
    pos = []
    for d in DILATIONS:
        row = jnp.arange(seq)
        pos.append((row % (seq // d)) * d + row // (seq // d))
    ang = jnp.concatenate(pos).astype(F32)[:, None] * inv[None, :]
    first = (jnp.arange(LANES) % HEAD_DIM) < half
    sin = jnp.sin(ang)
    return jnp.cos(ang), jnp.where(first[None, :], -sin, sin)


def _partner(x):
    half = HEAD_DIM // 2
    lane = lax.broadcasted_iota(jnp.int32, x.shape, 1)
    first = (lane % HEAD_DIM) < half
    return jnp.where(first, pltpu.roll(x, LANES - half, 1), pltpu.roll(x, half, 1))


def _attn_proj(x3, w_full, cos3, sin3, tm, tn):
    S3, D = x3.shape
    S = S3 // 3
    per_part = D // tn
    per_group = 3 * per_part

    def epilogue(acc, extra_refs, out_refs, j):
        cos_ref, sin_ref = extra_refs
        o_ref = out_refs[0]
        is_rot = j // per_part < 2

        @pl.when(is_rot)
        def _():
            c, s = cos_ref[...], sin_ref[...]
            for t in range(tn // LANES):
                xs = acc[:, t * LANES:(t + 1) * LANES]
                o_ref[:, t * LANES:(t + 1) * LANES] = (xs * c + _partner(xs) * s).astype(o_ref.dtype)

        @pl.when(jnp.logical_not(is_rot))
        def _():
            o_ref[...] = acc.astype(o_ref.dtype)

    tab = pl.BlockSpec((tm, LANES), lambda i, j, k: (i, 0))
    return _matmul("attn_proj", x3, w_full, "nn", tm, tn, D, [(_sds((S3, 3 * D), MXU_DTYPE), _ij_spec(tm, tn))],
                   epilogue, extras=[(cos3, tab), (sin3, tab)],
                   b_map=lambda i, j, k: (k, j + (i // (S // tm)) * per_group), mnk=(S3, 3 * D, D))[0]


def _head_sel(d_model):
    h = jnp.arange(LANES)[:, None]
    l = jnp.arange(d_model)[None, :]
    return (l // HEAD_DIM == h).astype(BF16)


def _class_edges(b, nblk):
    g = b // nblk
    per_class = jnp.where(g == 0, nblk // DILATIONS[0], jnp.where(g == 1, nblk // DILATIONS[1], nblk // DILATIONS[2]))
    pos = (b % nblk) % per_class
    return pos != 0, pos != per_class - 1


def _two_heads(t, top):
    zero = jnp.zeros_like(t)
    return jnp.concatenate([jnp.where(top, t, zero), jnp.where(top, zero, t)], axis=0)


def _band_mask(has_prev):
    B = ATTN_BLK
    row = lax.broadcasted_iota(jnp.int32, (2 * B, 2 * B), 0) % B
    col = lax.broadcasted_iota(jnp.int32, (2 * B, 2 * B), 1)
    in_prev = jnp.logical_and(jnp.logical_and(col < B, col >= row), has_prev)
    in_own = jnp.logical_and(col >= B, col - B <= row)
    return jnp.logical_or(in_prev, in_own)


def _attn_fwd(P3, D):
    S3 = P3.shape[0]
    B = ATTN_BLK
    nblk = S3 // 3 // B
    npairs = D // LANES
    scale = HEAD_DIM ** -0.5

    def body(q_ref, kc_ref, vc_ref, kp_ref, vp_ref, o_ref, lse_ref):
        has_prev, _ = _class_edges(pl.program_id(0), nblk)
        ok = _band_mask(has_prev)
        lane = lax.broadcasted_iota(jnp.int32, (B, LANES), 1)
        top = lane < HEAD_DIM
        lse_acc = jnp.zeros((B, LANES), F32)
        for j in range(npairs):
            sl = slice(j * LANES, (j + 1) * LANES)
            Q = _two_heads(q_ref[:, sl] * scale, top)
            K2 = jnp.concatenate([kp_ref[:, sl], kc_ref[:, sl]], axis=0)
            V2 = jnp.concatenate([vp_ref[:, sl], vc_ref[:, sl]], axis=0)
            s = jnp.where(ok, _nt(Q, K2), NEG)
            m = jnp.max(s, axis=1, keepdims=True)
            p = jnp.exp(s - m)
            l = jnp.sum(p, axis=1, keepdims=True)
            o = _nn((p * (1.0 / l)).astype(MXU_DTYPE), V2)
            o_ref[:, sl] = jnp.where(top, o[:B], o[B:])
            lse = m + jnp.log(l)
            lse_acc = jnp.where(lane == 2 * j, lse[:B], jnp.where(lane == 2 * j + 1, lse[B:], lse_acc))
        lse_ref[...] = lse_acc

    blk = lambda part, prev: pl.BlockSpec(
        (B, D), (lambda b: (jnp.maximum(b - 1, 0), part)) if prev else (lambda b: (b, part)))
    return pl.pallas_call(
        body,
        name="attn_fwd",
        grid=(3 * nblk,),
        in_specs=[blk(0, False), blk(1, False), blk(2, False), blk(1, True), blk(2, True)],
        out_specs=[pl.BlockSpec((B, D), lambda b: (b, 0)), pl.BlockSpec((B, LANES), lambda b: (b, 0))],
        out_shape=[_sds((S3, D), F32), _sds((S3, LANES), F32)],
        compiler_params=_cparams(("parallel",)),
    )(P3, P3, P3, P3, P3)


def _attn_mix(os, lses, sel, tm):
    S, D = min(o.shape[0] for o in os), os[0].shape[1]

    def body(o0, o1, o2, l0, l1, l2, sel_ref, o_ref, L_ref):
        a, b, c = l0[...], l1[...], l2[...]
        m = jnp.maximum(jnp.maximum(a, b), c)
        L = m + jnp.log(jnp.exp(a - m) + jnp.exp(b - m) + jnp.exp(c - m))
        L_ref[...] = L
        s = sel_ref[...]
        acc = _exact_nn(jnp.exp(a - L), s) * o0[...]
        acc += _exact_nn(jnp.exp(b - L), s) * o1[...]
        acc += _exact_nn(jnp.exp(c - L), s) * o2[...]
        o_ref[...] = acc

    big = pl.BlockSpec((tm, D), lambda i: (i, 0))
    small = pl.BlockSpec((tm, LANES), lambda i: (i, 0))
    return pl.pallas_call(
        body,
        name="attn_mix",
        grid=(S // tm,),
        in_specs=[big, big, big, small, small, small, pl.BlockSpec((LANES, D), lambda i: (0, 0))],
        out_specs=[big, small],
        out_shape=[_sds((S, D), F32), _sds((S, LANES), F32)],
        compiler_params=_cparams(("parallel",)),
    )(*os, *lses, sel)


def _attn_bwd(P3, do3, L3, delta3, cos3, sin3, D):
    S3 = P3.shape[0]
    B = ATTN_BLK
    nblk = S3 // 3 // B
    npairs = D // LANES
    scale = HEAD_DIM ** -0.5

    def body(c_ref, p_ref, n_ref, doc_ref, don_ref, Lc_ref, Ln_ref, dc_ref, dn_ref, cos_ref, sin_ref, out_ref):
        has_prev, has_next = _class_edges(pl.program_id(0), nblk)
        ok = _band_mask(has_prev)
        row = lax.broadcasted_iota(jnp.int32, (2 * B, B), 0) % B
        col = lax.broadcasted_iota(jnp.int32, (2 * B, B), 1)
        ok_n = jnp.logical_and(col >= row, has_next)
        lane = lax.broadcasted_iota(jnp.int32, (B, LANES), 1)
        top = lane < HEAD_DIM
        cos_t = cos_ref[...]
        sin_inv = -sin_ref[...]
        Lc_all, Ln_all, dc_all, dn_all = Lc_ref[...], Ln_ref[...], dc_ref[...], dn_ref[...]
        pair_col = lambda t, j: jnp.concatenate([t[:, 2 * j:2 * j + 1], t[:, 2 * j + 1:2 * j + 2]], axis=0)
        for j in range(npairs):
            sl = lambda part: slice(part * D + j * LANES, part * D + (j + 1) * LANES)
            kc2, vc2 = c_ref[:, sl(1)], c_ref[:, sl(2)]
            K2 = jnp.concatenate([p_ref[:, sl(1)], kc2], axis=0)
            V2 = jnp.concatenate([p_ref[:, sl(2)], vc2], axis=0)
            Qc = _two_heads(c_ref[:, sl(0)] * scale, top)
            Qn = _two_heads(n_ref[:, sl(0)] * scale, top)
            DOc = _two_heads(doc_ref[:, j * LANES:(j + 1) * LANES].astype(MXU_DTYPE), top)
            DOn = _two_heads(don_ref[:, j * LANES:(j + 1) * LANES].astype(MXU_DTYPE), top)
            P_c = jnp.where(ok, jnp.exp(_nt(Qc, K2) - pair_col(Lc_all, j)), 0.0)
            dS_c = P_c * (_nt(DOc, V2) - pair_col(dc_all, j))
            P_n = jnp.where(ok_n, jnp.exp(_nt(Qn, kc2) - pair_col(Ln_all, j)), 0.0)
            dS_n = P_n * (_nt(DOn, vc2) - pair_col(dn_all, j))
            dq = _nn(dS_c.astype(MXU_DTYPE), K2)
            dq2 = jnp.where(top, dq[:B], dq[B:]) * scale
            Qk = jnp.concatenate([Qc, Qn], axis=0)
            DOk = jnp.concatenate([DOc, DOn], axis=0)
            dk2 = _tn(jnp.concatenate([dS_c[:, B:], dS_n], axis=0).astype(MXU_DTYPE), Qk)
            dv2 = _tn(jnp.concatenate([P_c[:, B:], P_n], axis=0).astype(MXU_DTYPE), DOk)
            out_ref[:, sl(0)] = (dq2 * cos_t + _partner(dq2) * sin_inv).astype(out_ref.dtype)
            out_ref[:, sl(1)] = (dk2 * cos_t + _partner(dk2) * sin_inv).astype(out_ref.dtype)
            out_ref[:, sl(2)] = dv2.astype(out_ref.dtype)

    cur = lambda b: b
    prv = lambda b: jnp.maximum(b - 1, 0)
    nxt = lambda b: jnp.minimum(b + 1, 3 * nblk - 1)
    spec = lambda w, f: pl.BlockSpec((B, w), lambda b: (f(b), 0))
    return pl.pallas_call(
        body,
        name="attn_bwd",
        grid=(3 * nblk,),
        in_specs=[spec(3 * D, cur), spec(3 * D, prv), spec(3 * D, nxt), spec(D, cur), spec(D, nxt),
                  spec(LANES, cur), spec(LANES, nxt), spec(LANES, cur), spec(LANES, nxt), spec(LANES, cur), spec(LANES, cur)],
        out_specs=spec(3 * D, cur),
        out_shape=_sds((S3, 3 * D), MXU_DTYPE),
        compiler_params=_cparams(("parallel",)),
    )(P3, P3, P3, do3, do3, L3, L3, delta3, delta3, cos3, sin3)


def _sum4(a, b, c, d, tm):
    S, D = a.shape

    def body(a_ref, b_ref, c_ref, d_ref, o_ref):
        o_ref[...] = ALPHA * a_ref[...] + b_ref[...] + c_ref[...] + d_ref[...]

    row = pl.BlockSpec((tm, D), lambda i: (i, 0))
    return pl.pallas_call(body, name="sum4", grid=(S // tm,), in_specs=[row] * 4, out_specs=row,
                          out_shape=_sds((S, D), F32), compiler_params=_cparams(("parallel",)))(a, b, c, d)


def _tri(lower):
    r = lax.broadcasted_iota(jnp.int32, (HGRN_CHUNK, HGRN_CHUNK), 0)
    c = lax.broadcasted_iota(jnp.int32, (HGRN_CHUNK, HGRN_CHUNK), 1)
    return ((r >= c) if lower else (r <= c)).astype(BF16)


def _lower_bound(lb_ref):
    l0, l1 = lb_ref[0:1, :], lb_ref[1:2, :]
    m = jnp.maximum(l0, l1)
    e0, e1 = jnp.exp(l0 - m), jnp.exp(l1 - m)
    return e1 / (e0 + e1)


def _hgrn_gates(q_raw, z, lb):
    sg = 1.0 / (1.0 + jnp.exp(-z))
    sn = 1.0 / (1.0 + jnp.exp(z))
    f = lb + (1.0 - lb) * sg
    key = (1.0 - lb) * sn
    sq = 1.0 / (1.0 + jnp.exp(-q_raw))
    return sg, sn, f, key, sq


def _hgrn_fwd(P1, lb_logits, norm_g, tb):
    S = P1.shape[0]
    D = P1.shape[1] // 3
    H = D // HGRN_DK
    C = HGRN_CHUNK
    cpb = tb // C
    nt = S // tb

    def body(q_ref, f_ref, i_ref, lb_ref, g_ref, o_ref, n_ref, st_ref, state):
        t = pl.program_id(1)

        @pl.when(t == 0)
        def _():
            state[...] = jnp.zeros_like(state)

        lb = _lower_bound(lb_ref)
        gn = g_ref[...]
        tri = _tri(True)
        r = lax.broadcasted_iota(jnp.int32, (C, C), 0)
        c = lax.broadcasted_iota(jnp.int32, (C, C), 1)
        causal = r >= c
        for ci in range(cpb):
            rows = slice(ci * C, (ci + 1) * C)
            q_raw, z, v = q_ref[rows, :], f_ref[rows, :], i_ref[rows, :]
            sg, sn, f, key, sq = _hgrn_gates(q_raw, z, lb)
            q = q_raw * sq
            b = _exact_sel_nn(tri, jnp.log(f))
            b_last = b[C - 1:C, :]
            qd = (q * jnp.exp(b)).astype(MXU_DTYPE)
            kd = (key * jnp.exp(-b)).astype(MXU_DTYPE)
            kb = (key * jnp.exp(b_last - b)).astype(MXU_DTYPE)
            vm = v.astype(MXU_DTYPE)
            st = state[...]
            st_ref[ci] = st
            a = jnp.where(causal, _nt(qd, kd), 0.0)
            o = _nn(a.astype(MXU_DTYPE), vm) + _nt(qd, st.astype(MXU_DTYPE))
            state[...] = st * jnp.exp(b_last) + _tn(vm, kb)
            o_ref[rows, :] = o
            rs = lax.rsqrt(jnp.mean(o * o, axis=1, keepdims=True) + RMS_EPS)
            n_ref[rows, :] = o * rs * gn

    tok = lambda part: pl.BlockSpec((tb, HGRN_DK), lambda h, t: (t, part * H + h))
    vec = lambda rows: pl.BlockSpec((rows, HGRN_DK), lambda h, t: (0, h))
    return pl.pallas_call(
        body,
        name="hgrn_fwd",
        grid=(H, nt),
        in_specs=[tok(0), tok(1), tok(2), vec(2), vec(1)],
        out_specs=[tok(0), tok(0), pl.BlockSpec((None, cpb, HGRN_DK, HGRN_DK), lambda h, t: (h, t, 0, 0))],
        out_shape=[_sds((S, D), F32), _sds((S, D), F32), _sds((H, S // C, HGRN_DK, HGRN_DK), F32)],
        scratch_shapes=[pltpu.VMEM((HGRN_DK, HGRN_DK), F32)],
        compiler_params=_cparams(("parallel", "arbitrary")),
    )(P1, P1, P1, lb_logits, norm_g)


def _hgrn_bwd(P1, o_pre, states, dn, lb_logits, norm_g, tb):
    S = P1.shape[0]
    D = P1.shape[1] // 3
    H = D // HGRN_DK
    C = HGRN_CHUNK
    cpb = tb // C
    nt = S // tb

    def body(q_ref, f_ref, i_ref, o_ref, st_ref, dn_ref, lb_ref, g_ref, dq_ref, dz_ref, dv_ref, dg_ref, dlb_ref, dstate):
        t = pl.program_id(1)

        @pl.when(t == 0)
        def _():
            dstate[...] = jnp.zeros_like(dstate)
            dg_ref[...] = jnp.zeros_like(dg_ref)
            dlb_ref[...] = jnp.zeros_like(dlb_ref)

        lb = _lower_bound(lb_ref)
        gn = g_ref[...]
        tri_l, tri_u = _tri(True), _tri(False)
        r = lax.broadcasted_iota(jnp.int32, (C, C), 0)
        c = lax.broadcasted_iota(jnp.int32, (C, C), 1)
        causal = r >= c
        last_row = lax.broadcasted_iota(jnp.int32, (C, HGRN_DK), 0) == C - 1
        dg_acc = jnp.zeros((1, HGRN_DK), F32)
        dlb_acc = jnp.zeros((1, HGRN_DK), F32)
        for ci in reversed(range(cpb)):
            rows = slice(ci * C, (ci + 1) * C)
            q_raw, z, v = q_ref[rows, :], f_ref[rows, :], i_ref[rows, :]
            sg, sn, f, key, sq = _hgrn_gates(q_raw, z, lb)
            q = q_raw * sq
            b = _exact_sel_nn(tri_l, jnp.log(f))
            b_last = b[C - 1:C, :]
            e_pos, e_neg, e_rel = jnp.exp(b), jnp.exp(-b), jnp.exp(b_last - b)
            dec = jnp.exp(b_last)
            qd_f, kd_f, kb_f = q * e_pos, key * e_neg, key * e_rel
            qd, kd, kb = qd_f.astype(MXU_DTYPE), kd_f.astype(MXU_DTYPE), kb_f.astype(MXU_DTYPE)
            vm = v.astype(MXU_DTYPE)
            st = st_ref[ci]
            dst = dstate[...]
            stm, dstm = st.astype(MXU_DTYPE), dst.astype(MXU_DTYPE)
            a = jnp.where(causal, _nt(qd, kd), 0.0).astype(MXU_DTYPE)
            o = o_ref[rows, :]
            dnn = dn_ref[rows, :]
            rs = lax.rsqrt(jnp.mean(o * o, axis=1, keepdims=True) + RMS_EPS)
            dg_acc += jnp.sum(dnn * o * rs, axis=0, keepdims=True)
            tg = dnn * gn
            do_f = rs * tg - o * (rs * rs * rs) * jnp.mean(tg * o, axis=1, keepdims=True)
            dom = do_f.astype(MXU_DTYPE)
            da = jnp.where(causal, _nt(dom, vm), 0.0).astype(MXU_DTYPE)
            dv = _tn(a, dom) + _nt(kb, dstm)
            dqd = _nn(da, kd) + _nn(dom, stm)
            dkd = _tn(da, qd)
            dkb = _nn(vm, dstm)
            ddec = jnp.sum(dst * st, axis=0, keepdims=True)
            dstate[...] = dst * dec + _tn(dom, qd)
            dq = dqd * e_pos
            dkey = dkd * e_neg + dkb * e_rel
            tk = dkb * kb_f
            db_last = jnp.sum(tk, axis=0, keepdims=True) + ddec * dec
            db = dqd * qd_f - dkd * kd_f - tk
            db = jnp.where(last_row, db + db_last, db)
            dlogf = _exact_sel_nn(tri_u, db)
            gz = (1.0 - lb) * sg * sn
            dz_ref[rows, :] = (dlogf * gz / f - dkey * gz).astype(dz_ref.dtype)
            dlb_acc += jnp.sum(dlogf * sn / f - dkey * sn, axis=0, keepdims=True)
            dq_ref[rows, :] = (dq * (sq + q_raw * sq * (1.0 - sq))).astype(dq_ref.dtype)
            dv_ref[rows, :] = dv.astype(dv_ref.dtype)
        dg_ref[...] += dg_acc
        dlb_ref[...] += dlb_acc

    rev = lambda t: nt - 1 - t
    tok = lambda part: pl.BlockSpec((tb, HGRN_DK), lambda h, t: (rev(t), part * H + h))
    vec = lambda rows: pl.BlockSpec((rows, HGRN_DK), lambda h, t: (0, h))
    outs = pl.pallas_call(
        body,
        name="hgrn_bwd",
        grid=(H, nt),
        in_specs=[tok(0), tok(1), tok(2), tok(0),
                  pl.BlockSpec((None, cpb, HGRN_DK, HGRN_DK), lambda h, t: (h, rev(t), 0, 0)),
                  tok(0), vec(2), vec(1)],
        out_specs=[tok(0), tok(0), tok(0), vec(1), vec(1)],
        out_shape=[_sds((S, D), MXU_DTYPE)] * 3 + [_sds((1, D), F32)] * 2,
        scratch_shapes=[pltpu.VMEM((HGRN_DK, HGRN_DK), F32)],
        compiler_params=_cparams(("parallel", "arbitrary")),
    )(P1, P1, P1, o_pre, states, dn, lb_logits, norm_g)
    return outs


def _lb_logits_grad(dlb, lb_logits):
    def body(d_ref, l_ref, o_ref):
        s1 = _lower_bound(l_ref)
        d = d_ref[...]
        o_ref[0:1, :] = -(1.0 - s1) * s1 * d
        o_ref[1:2, :] = s1 * (1.0 - s1) * d

    return pl.pallas_call(body, name="lb_logits_grad", out_shape=_sds(lb_logits.shape, F32))(dlb, lb_logits)


def _ln_epilogue(acc, extra_refs, out_refs, j):
    res_ref, g_ref, b_ref = extra_refs
    x_ref, xhat_ref, rstd_ref = out_refs
    u = ALPHA * res_ref[...] + acc
    mu = jnp.mean(u, axis=1, keepdims=True)
    cen = u - mu
    rstd = lax.rsqrt(jnp.mean(cen * cen, axis=1, keepdims=True) + LN_EPS)
    xhat = cen * rstd
    xhat_ref[...] = xhat
    x_ref[...] = xhat * g_ref[...] + b_ref[...]
    rstd_ref[...] = rstd


def _mm_res_ln(name, a, w_full, res, g, b, tm, tk):
    S, D = res.shape
    row = pl.BlockSpec((tm, D), lambda i, j, k: (i, 0))
    vec = pl.BlockSpec((1, D), lambda i, j, k: (0, 0))
    outs = [(_sds((S, D), F32), row), (_sds((S, D), F32), row),
            (_sds((S, 1), F32), pl.BlockSpec((tm, 1), lambda i, j, k: (i, 0)))]
    return _matmul(name, a, w_full, "nn", tm, D, tk, outs, _ln_epilogue, extras=[(res, row), (g, vec), (b, vec)])


def _ln_bwd(name, dy, xhat, rstd, g, tm, dep):
    S, D = dy.shape

    def body(dy_ref, xh_ref, r_ref, g_ref, dep_ref, du_ref, dg_ref, db_ref):
        @pl.when(pl.program_id(0) == 0)
        def _():
            dg_ref[...] = jnp.zeros_like(dg_ref)
            db_ref[...] = jnp.zeros_like(db_ref)

        dy_, xh = dy_ref[...], xh_ref[...]
        dg_ref[...] += jnp.sum(dy_ * xh, axis=0, keepdims=True)
        db_ref[...] += jnp.sum(dy_, axis=0, keepdims=True)
        dxh = dy_ * g_ref[...]
        m1 = jnp.mean(dxh, axis=1, keepdims=True)
        m2 = jnp.mean(dxh * xh, axis=1, keepdims=True)
        du_ref[...] = r_ref[...] * (dxh - m1 - xh * m2)

    row = pl.BlockSpec((tm, D), lambda i: (i, 0))
    vec = pl.BlockSpec((1, D), lambda i: (0, 0))
    return pl.pallas_call(
        body,
        name=name,
        grid=(S // tm,),
        in_specs=[row, row, pl.BlockSpec((tm, 1), lambda i: (i, 0)), vec, pl.BlockSpec(memory_space=pl.ANY)],
        out_specs=[row, vec, vec],
        out_shape=[_sds((S, D), F32), _sds((1, D), F32), _sds((1, D), F32)],
        compiler_params=_cparams(("arbitrary",)),
    )(dy, xhat, rstd, g, dep)


def _loss_head(y, target, tm):
    S, D = y.shape

    def body(y_ref, t_ref, sq_ref, dy_ref):
        @pl.when(pl.program_id(0) == 0)
        def _():
            sq_ref[...] = jnp.zeros_like(sq_ref)

        e = y_ref[...] - t_ref[...]
        sq_ref[...] += jnp.sum(e * e, axis=0, keepdims=True)
        dy_ref[...] = e / D

    row = pl.BlockSpec((tm, D), lambda i: (i, 0))
    vec = pl.BlockSpec((1, D), lambda i: (0, 0))
    return pl.pallas_call(
        body,
        name="loss_head",
        grid=(S // tm,),
        in_specs=[row, row],
        out_specs=[vec, row],
        out_shape=[_sds((1, D), F32), _sds((S, D), F32)],
        compiler_params=_cparams(("arbitrary",)),
    )(y, target)


def _mlp_up(name, x, w_up, tm, tn, tk):
    S = x.shape[0]
    F = w_up.shape[1]

    def epilogue(acc, extra_refs, out_refs, j):
        r = jnp.maximum(acc, 0.0)
        out_refs[0][...] = (r * r).astype(out_refs[0].dtype)

    return _matmul(name, x, w_up, "nn", tm, tn, tk, [(_sds((S, F), MXU_DTYPE), _ij_spec(tm, tn))], epilogue)[0]


def _mlp_down_bwd(name, dy, w_down, a, tm, tn, tk):
    S, F = a.shape

    def epilogue(acc, extra_refs, out_refs, j):
        out_refs[0][...] = (acc * (2.0 * jnp.sqrt(extra_refs[0][...].astype(F32)))).astype(out_refs[0].dtype)

    return _matmul(name, dy, w_down, "nt", tm, tn, tk, [(_sds((S, F), MXU_DTYPE), _ij_spec(tm, tn))], epilogue,
                   extras=[(a, _ij_spec(tm, tn))])[0]


def _mm_nt_res(name, dy, w, du, tm, tn, tk):
    S = dy.shape[0]
    N = w.shape[0]

    def epilogue(acc, extra_refs, out_refs, j):
        out_refs[0][...] = ALPHA * extra_refs[0][...] + acc

    return _matmul(name, dy, w, "nt", tm, tn, tk, [(_sds((S, N), F32), _ij_spec(tm, tn))], epilogue,
                   extras=[(du, _ij_spec(tm, tn))])[0]


def _attn_out_bwd(du, w_out, o, sel_t, tm, tk):
    S, D = o.shape

    def epilogue(acc, extra_refs, out_refs, j):
        out_refs[0][...] = acc.astype(out_refs[0].dtype)
        out_refs[1][...] = _exact_nn(acc * extra_refs[0][...], extra_refs[1][...])

    row = pl.BlockSpec((tm, D), lambda i, j, k: (i, 0))
    slim = pl.BlockSpec((tm, LANES), lambda i, j, k: (i, 0))
    return _matmul("attn_out_bwd", du, w_out, "nt", tm, D, tk,
                   [(_sds((S, D), MXU_DTYPE), row), (_sds((S, LANES), F32), slim)], epilogue,
                   extras=[(o, row), (sel_t, pl.BlockSpec((D, LANES), lambda i, j, k: (0, 0)))])


def _adamw(name, w, g, m, v):
    shape = w.shape
    cols = shape[-1]
    rows = math.prod(shape[:-1])
    w2, g2, m2, v2 = (t.reshape(rows, cols) for t in (w, g, m, v))
    tr = _pick(rows, (256, 128, 64, 32, 16, 8))
    c1 = 1.0 - ADAM_B1 ** ADAM_STEP
    c2 = 1.0 - ADAM_B2 ** ADAM_STEP

    def body(w_ref, g_ref, m_ref, v_ref, d_ref, nm_ref, nv_ref):
        gg = g_ref[...]
        nm = ADAM_B1 * m_ref[...] + (1.0 - ADAM_B1) * gg
        nv = ADAM_B2 * v_ref[...] + (1.0 - ADAM_B2) * (gg * gg)
        nm_ref[...] = nm
        nv_ref[...] = nv
        d_ref[...] = -ADAM_LR * ((nm / c1) / (jnp.sqrt(nv / c2) + ADAM_EPS) + ADAM_WD * w_ref[...])

    blk = pl.BlockSpec((tr, cols), lambda i: (i, 0))
    outs = pl.pallas_call(
        body,
        name=name,
        grid=(rows // tr,),
        in_specs=[blk] * 4,
        out_specs=[blk] * 3,
        out_shape=[_sds((rows, cols), F32)] * 3,
        compiler_params=_cparams(("parallel",)),
    )(w2, g2, m2, v2)
    return tuple(o.reshape(shape) for o in outs)


HBM = pl.BlockSpec(memory_space=pl.ANY)


def _shard_slice(ref, axis, size, index):
    idx = [slice(None)] * len(ref.shape)
    idx[axis] = pl.ds(pl.multiple_of(index * size, 8), size)
    return ref.at[tuple(idx)]


IN_HBM = pl.BlockSpec(memory_space=pltpu.HBM)
IN_SEM = pl.BlockSpec(memory_space=pltpu.SEMAPHORE)
DATAFLOW = pltpu.SideEffectType.DATAFLOW_SIDE_EFFECTING


def _hbm(t):
    return pltpu.with_memory_space_constraint(t, pltpu.HBM)


def _token_spec():
    return pl.BlockSpec(memory_space=pltpu.VMEM)


def _gather_copies(s_refs, f_refs, axes, send, recv, loc, arrival):
    x, y, c = lax.axis_index("x"), lax.axis_index("y"), lax.axis_index("c")
    chips = [(1 - x, y), (x, 1 - y), (1 - x, 1 - y)]
    local, remote = [], []
    for a in range(len(s_refs)):
        size = s_refs[a].shape[axes[a]]
        local.append(pltpu.make_async_copy(s_refs[a], _shard_slice(f_refs[a], axes[a], size, 2 * x + y), loc.at[a]))
        for k, (px, py) in enumerate(chips):
            block = (2 * px + py) if arrival else (2 * x + y)
            remote.append(pltpu.make_async_remote_copy(
                src_ref=s_refs[a], dst_ref=_shard_slice(f_refs[a], axes[a], size, block), send_sem=send.at[3 * a + k],
                recv_sem=recv.at[3 * a + k], device_id=(px, py, c), device_id_type=MESH))
    return local, remote


def _gather_start(name, shards, axes, after):
    n = len(shards)
    fulls = []
    for s, ax in zip(shards, axes):
        fs = list(s.shape)
        fs[ax] *= 4
        fulls.append(lax.empty(tuple(fs), s.dtype))

    def body(*refs):
        s_refs, f_refs = refs[:n], refs[n:2 * n]
        send, recv, loc, token = refs[2 * n + 1], refs[2 * n + 2], refs[2 * n + 3], refs[-1]
        local, remote = _gather_copies(s_refs, f_refs, axes, send, recv, loc, arrival=False)
        for cp in remote + local:
            cp.start()
        token[...] = jnp.zeros_like(token)

    outs = pl.pallas_call(
        body,
        name=name,
        out_shape=(pltpu.SemaphoreType.DMA((3 * n,)), pltpu.SemaphoreType.DMA((3 * n,)), pltpu.SemaphoreType.DMA((n,)),
                   *[pltpu.HBM(t.shape, t.dtype) for t in shards + fulls], _sds((8, LANES), F32)),
        in_specs=[IN_HBM] * (2 * n) + [HBM],
        out_specs=(IN_SEM, IN_SEM, IN_SEM, *[IN_HBM] * (2 * n), _token_spec()),
        input_output_aliases={i: 3 + i for i in range(2 * n)},
        compiler_params=pltpu.CompilerParams(has_side_effects=DATAFLOW),
    )(*[_hbm(t) for t in shards + fulls], after)
    return (outs[0], outs[1], outs[2], list(outs[3:3 + n]), list(outs[3 + n:3 + 2 * n]), axes), outs[-1]


def _gather_wait(name, state, after):
    send, recv, loc, s_thru, f_thru, axes = state
    n = len(s_thru)

    def body(*refs):
        s_refs, f_refs = refs[:n], refs[n:2 * n]
        local, remote = _gather_copies(s_refs, f_refs, axes, refs[2 * n], refs[2 * n + 1], refs[2 * n + 2], arrival=True)
        for cp in local:
            cp.wait()
        for cp in remote:
            cp.wait_send()
            cp.wait_recv()

    outs = pl.pallas_call(
        body,
        name=name,
        out_shape=tuple(pltpu.HBM(t.shape, t.dtype) for t in s_thru + f_thru),
        in_specs=[IN_HBM] * (2 * n) + [IN_SEM, IN_SEM, IN_SEM, HBM],
        out_specs=tuple([IN_HBM] * (2 * n)),
        input_output_aliases={i: i for i in range(2 * n)},
        compiler_params=pltpu.CompilerParams(has_side_effects=DATAFLOW),
    )(*s_thru, *f_thru, send, recv, loc, after)
    return list(outs[n:2 * n])


FLIPS = [(fx, fy, fc) for fx in (0, 1) for fy in (0, 1) for fc in (0, 1)][1:]


def _piece_shape(shape, axis):
    ps = list(shape)
    if axis == 0:
        ps[0] //= 8
    else:
        ps[0] //= 2
        ps[axis] //= 4
    return tuple(ps)


def _piece(ref, axis, q, c):
    shape = ref.shape
    idx = [slice(None)] * len(shape)
    if axis == 0:
        h = shape[0] // 8
        idx[0] = pl.ds(pl.multiple_of((2 * q + c) * h, 8), h)
    else:
        h, w = shape[0] // 2, shape[axis] // 4
        idx[0] = pl.ds(c * h, h)
        idx[axis] = pl.ds(pl.multiple_of(q * w, LANES if axis == len(shape) - 1 else 8), w)
    return ref.at[tuple(idx)]


def _own_piece(g, axis):
    ps = _piece_shape(g.shape, axis)
    q, c = 2 * lax.axis_index("x") + lax.axis_index("y"), lax.axis_index("c")
    start = [0] * len(ps)
    if axis == 0:
        start[0] = (2 * q + c) * ps[0]
    else:
        start[0] = c * ps[0]
        start[axis] = q * ps[axis]
    return lax.dynamic_slice(g, start, ps)


def _scatter_copies(g_refs, l_refs, axes, send, recv):
    x, y, c = lax.axis_index("x"), lax.axis_index("y"), lax.axis_index("c")
    out = []
    for a in range(len(g_refs)):
        for k, (fx, fy, fc) in enumerate(FLIPS):
            tx, ty, tc = x ^ fx, y ^ fy, c ^ fc
            out.append(pltpu.make_async_remote_copy(
                src_ref=_piece(g_refs[a], axes[a], 2 * tx + ty, tc), dst_ref=l_refs[a].at[k],
                send_sem=send.at[7 * a + k], recv_sem=recv.at[7 * a + k], device_id=(tx, ty, tc), device_id_type=MESH))
    return out


def _scatter_start(name, grads, axes):
    n = len(grads)
    lands = [lax.empty((7,) + _piece_shape(g.shape, ax), g.dtype) for g, ax in zip(grads, axes)]

    def body(*refs):
        g_refs, l_refs = refs[:n], refs[n:2 * n]
        send, recv, token = refs[2 * n], refs[2 * n + 1], refs[-1]
        for cp in _scatter_copies(g_refs, l_refs, axes, send, recv):
            cp.start()
        token[...] = jnp.zeros_like(token)

    outs = pl.pallas_call(
        body,
        name=name,
        out_shape=(pltpu.SemaphoreType.DMA((7 * n,)), pltpu.SemaphoreType.DMA((7 * n,)),
                   *[pltpu.HBM(t.shape, t.dtype) for t in grads + lands], _sds((8, LANES), F32)),
        in_specs=[IN_HBM] * (2 * n),
        out_specs=(IN_SEM, IN_SEM, *[IN_HBM] * (2 * n), _token_spec()),
        input_output_aliases={i: 2 + i for i in range(2 * n)},
        compiler_params=pltpu.CompilerParams(has_side_effects=DATAFLOW),
    )(*[_hbm(t) for t in grads + lands])
    return (outs[0], outs[1], list(outs[2:2 + n]), list(outs[2 + n:2 + 2 * n]), axes), outs[-1]


def _scatter_wait(name, state, after):
    send, recv, g_thru, l_thru, axes = state
    n = len(g_thru)

    def body(*refs):
        g_refs, l_refs = refs[:n], refs[n:2 * n]
        for cp in _scatter_copies(g_refs, l_refs, axes, refs[2 * n], refs[2 * n + 1]):
            cp.wait_send()
            cp.wait_recv()

    outs = pl.pallas_call(
        body,
        name=name,
        out_shape=tuple(pltpu.HBM(t.shape, t.dtype) for t in g_thru + l_thru),
        in_specs=[IN_HBM] * (2 * n) + [IN_SEM, IN_SEM, HBM],
        out_specs=tuple([IN_HBM] * (2 * n)),
        input_output_aliases={i: i for i in range(2 * n)},
        compiler_params=pltpu.CompilerParams(has_side_effects=DATAFLOW),
    )(*g_thru, *l_thru, send, recv, after)
    return list(outs[:n]), list(outs[n:2 * n])


def _reduce_join(name, landing, own):
    piece = own.shape
    C = piece[-1]
    R = math.prod(piece[:-1])
    l3 = landing.reshape(7, R, C)
    own2 = own.reshape(R, C)
    tr = _pick(R, [t for t in (512, 256, 128, 64, 32, 16, 8) if t * C <= 256 * 1024])
    nsteps = R // tr

    def body(own_ref, l_ref, o_ref, buf, send, loc, recv):
        i = pl.program_id(0)
        x, y, c = lax.axis_index("x"), lax.axis_index("y"), lax.axis_index("c")
        sibling = (x, y, 1 - c)

        def copies(slot, step):
            dst = o_ref.at[pl.ds(pl.multiple_of(c * R + step * tr, 8), tr), :]
            return (pltpu.make_async_copy(buf.at[slot], dst, loc.at[slot]),
                    pltpu.make_async_remote_copy(src_ref=buf.at[slot], dst_ref=dst, send_sem=send.at[slot], recv_sem=recv,
                                                 device_id=sibling, device_id_type=MESH))

        @pl.when(i >= 2)
        def _():
            lc, rc = copies(i % 2, i - 2)
            lc.wait()
            rc.wait_send()

        acc = own_ref[...].astype(F32)
        for s in range(7):
            acc = acc + l_ref[s].astype(F32)
        buf[i % 2] = acc
        lc, rc = copies(i % 2, i)
        lc.start()
        rc.start()

        @pl.when(i == nsteps - 1)
        def _():
            for st in range(max(nsteps - 2, 0), nsteps):
                lc, rc = copies(st % 2, st)
                lc.wait()
                rc.wait_send()
            theirs = o_ref.at[pl.ds(pl.multiple_of((1 - c) * R, 8), R), :]
            pltpu.make_async_remote_copy(src_ref=theirs, dst_ref=theirs, send_sem=send.at[0], recv_sem=recv,
                                         device_id=sibling, device_id_type=MESH).wait_recv()

    return pl.pallas_call(
        body,
        name=name,
        grid=(nsteps,),
        in_specs=[pl.BlockSpec((tr, C), lambda i: (i, 0)), pl.BlockSpec((7, tr, C), lambda i: (0, i, 0))],
        out_specs=HBM,
        out_shape=_sds((2 * R, C), F32),
        scratch_shapes=[pltpu.VMEM((2, tr, C), F32), pltpu.SemaphoreType.DMA((2,)), pltpu.SemaphoreType.DMA((2,)),
                        pltpu.SemaphoreType.DMA(())],
        compiler_params=_cparams(("arbitrary",)),
    )(own2, l3)


def _all_reduce_small(v):
    R, D = v.shape

    def body(v_ref, o_ref, land, send, recv):
        x, y, c = lax.axis_index("x"), lax.axis_index("y"), lax.axis_index("c")
        my_slot = 4 * x + 2 * y + c
        land[my_slot] = v_ref[...]
        for k, (fx, fy, fc) in enumerate(FLIPS):
            tx, ty, tc = x ^ fx, y ^ fy, c ^ fc
            pltpu.make_async_remote_copy(src_ref=v_ref, dst_ref=land.at[my_slot], send_sem=send.at[k], recv_sem=recv.at[k],
                                         device_id=(tx, ty, tc), device_id_type=MESH).start()
        for k, (fx, fy, fc) in enumerate(FLIPS):
            tx, ty, tc = x ^ fx, y ^ fy, c ^ fc
            cp = pltpu.make_async_remote_copy(src_ref=v_ref, dst_ref=land.at[4 * tx + 2 * ty + tc], send_sem=send.at[k],
                                              recv_sem=recv.at[k], device_id=(tx, ty, tc), device_id_type=MESH)
            cp.wait_send()
            cp.wait_recv()
        acc = land[0]
        for s in range(1, 8):
            acc = acc + land[s]
        o_ref[...] = acc

    return pl.pallas_call(
        body,
        name="all_reduce_small",
        in_specs=[pl.BlockSpec(memory_space=pltpu.VMEM)],
        out_specs=pl.BlockSpec(memory_space=pltpu.VMEM),
        out_shape=_sds((R, D), F32),
        scratch_shapes=[pltpu.VMEM((8, R, D), F32), pltpu.SemaphoreType.DMA((7,)), pltpu.SemaphoreType.DMA((7,))],
    )(v)


def kernel(x, attn_w_in, attn_w_out, hgrn_w_in, hgrn_w_out, hgrn_norm_g, lb_logits, ln_mix_g, ln_mix_b, ln_ffn_g, ln_ffn_b, ffn_w_up, ffn_w_down, loss_target, m_attn_w_in, m_attn_w_out, m_hgrn_w_in, m_hgrn_w_out, m_hgrn_norm_g, m_lb_logits, m_ln_mix_g, m_ln_mix_b, m_ln_ffn_g, m_ln_ffn_b, m_ffn_w_up, m_ffn_w_down, v_attn_w_in, v_attn_w_out, v_hgrn_w_in, v_hgrn_w_out, v_hgrn_norm_g, v_lb_logits, v_ln_mix_g, v_ln_mix_b, v_ln_ffn_g, v_ln_ffn_b, v_ffn_w_up, v_ffn_w_down):
    xs = x[0]
    tgt = loss_target[0]
    S, D = xs.shape
    F = ffn_w_up.shape[2] * 4
    T1 = _pick(S, (1024, 512, 256))
    T2 = _pick(S, (2048, 1024, 512))
    TH = _pick(S, (512, 256))
    TB = _pick(S, (512, 256))
    TN = _pick(D, (512, 256, 128))
    TF = _pick(F, (1024, 512))
    TG = _pick(3 * D, (1536, 1024, 768))
    TW = _pick(F, (2048, 1024))

    cast = lambda w: w.astype(MXU_DTYPE)
    st_a, tok = _gather_start("gather_a", [cast(attn_w_in[0])], [1], jnp.zeros((8, LANES), F32))
    st_b, tok = _gather_start("gather_b", [cast(attn_w_out[0]), cast(ffn_w_up[0]), cast(ffn_w_down[0])], [0, 1, 0], tok)
    st_c, tok = _gather_start("gather_c", [cast(hgrn_w_in[0]), cast(hgrn_w_out[0]), hgrn_norm_g, cast(ffn_w_up[1]),
                                           cast(ffn_w_down[1])], [1, 0, 1, 1, 0], tok)

    cos3, sin3 = _rope_tables(S)
    sel = _head_sel(D)
    sel_t = sel.T

    xc3 = _stack_classes(xs.astype(MXU_DTYPE))
    (wa_in,) = _gather_wait("gather_a_wait", st_a, tok)
    P3 = _attn_proj(xc3, wa_in, cos3, sin3, T2, TN)
    o3, lse3 = _attn_fwd(P3, D)
    back = lambda t, g: _from_classes(t[g * S:(g + 1) * S], DILATIONS[g])
    o_att, L_att = _attn_mix([o3, back(o3, 1), back(o3, 2)], [lse3, back(lse3, 1), back(lse3, 2)], sel, TH)
    wa_out, w_up0, w_down0 = _gather_wait("gather_b_wait", st_b, L_att)
    x1, xh1, r1 = _mm_res_ln("attn_out_ln", o_att, wa_out, xs, ln_mix_g[0:1], ln_mix_b[0:1], TH, D)
    a0 = _mlp_up("mlp0_up", x1, w_up0, T1, TF, D)
    x2, xh2, r2 = _mm_res_ln("mlp0_down_ln", a0, w_down0, x1, ln_ffn_g[0:1], ln_ffn_b[0:1], TH, F)

    wh_in, wh_out, norm_g, w_up1, w_down1 = _gather_wait("gather_c_wait", st_c, r2)
    P1 = _plain_mm("hgrn_proj", x2, wh_in, "nn", F32, T1, _pick(3 * D, (1024, 768, 512)), D)
    o_h, n_h, states = _hgrn_fwd(P1, lb_logits, norm_g, TB)
    x3, xh3, r3 = _mm_res_ln("hgrn_out_ln", n_h, wh_out, x2, ln_mix_g[1:2], ln_mix_b[1:2], TH, D)
    a1 = _mlp_up("mlp1_up", x3, w_up1, T1, TF, D)
    x4, xh4, r4 = _mm_res_ln("mlp1_down_ln", a1, w_down1, x3, ln_ffn_g[1:2], ln_ffn_b[1:2], TH, F)

    sq, dx4 = _loss_head(x4, tgt, TH)

    wgrad = lambda name, a, dy, tm, tn: _plain_mm(name, a, dy, "tn", MXU_DTYPE, tm, tn, T1)
    du4, dg_ffn1, db_ffn1 = _ln_bwd("ln_ffn1_bwd", dx4, xh4, r4, ln_ffn_g[1:2], TH, sq)
    dh1 = _mlp_down_bwd("mlp1_down_bwd", du4, w_down1, a1, T1, TF, D)
    g_down1 = wgrad("g_down1", a1, du4, TW, D)
    dx3 = _mm_nt_res("mlp1_up_bwd", dh1, w_up1, du4, TH, D, F)
    g_up1 = wgrad("g_up1", x3, dh1, D, TW)
    sc_1, tok = _scatter_start("scatter_1", [g_down1, g_up1], [0, 1])
    du3, dg_mix1, db_mix1 = _ln_bwd("ln_mix1_bwd", dx3, xh3, r3, ln_mix_g[1:2], TH, tok)
    dn = _plain_mm("hgrn_out_bwd", du3, wh_out, "nt", F32, T1, D, D)
    g_hout = wgrad("g_hgrn_out", n_h, du3, D, D)
    dq_raw, dz, dv, dg_norm, dlb = _hgrn_bwd(P1, o_h, states, dn, lb_logits, norm_g, TB)
    dP1 = jnp.concatenate([dq_raw, dz, dv], axis=1)
    dx2 = _mm_nt_res("hgrn_in_bwd", dP1, wh_in, du3, TH, D, 3 * D)
    g_hin = wgrad("g_hgrn_in", x2, dP1, D, TG)
    d_lb_logits = _lb_logits_grad(dlb, lb_logits)
    sc_2, tok = _scatter_start("scatter_2", [g_hout, g_hin], [0, 1])

    du2, dg_ffn0, db_ffn0 = _ln_bwd("ln_ffn0_bwd", dx2, xh2, r2, ln_ffn_g[0:1], TH, tok)
    dh0 = _mlp_down_bwd("mlp0_down_bwd", du2, w_down0, a0, T1, TF, D)
    g_down0 = wgrad("g_down0", a0, du2, TW, D)
    dx1 = _mm_nt_res("mlp0_up_bwd", dh0, w_up0, du2, TH, D, F)
    g_up0 = wgrad("g_up0", x1, dh0, D, TW)
    sc_3, tok = _scatter_start("scatter_3", [g_down0, g_up0], [0, 1])
    du1, dg_mix0, db_mix0 = _ln_bwd("ln_mix0_bwd", dx1, xh1, r1, ln_mix_g[0:1], TH, tok)
    do, delta = _attn_out_bwd(du1, wa_out, o_att, sel_t, TH, D)
    g_aout = wgrad("g_attn_out", o_att, du1, D, D)
    dP3 = _attn_bwd(P3, _stack_classes(do), _stack_classes(L_att), _stack_classes(delta), cos3, sin3, D)
    grp = lambda j: j // (3 * D // TG)
    g_ain = _matmul("g_attn_in", xc3, dP3, "tn", D, TG, T1, [(_sds((D, 9 * D), MXU_DTYPE), _ij_spec(D, TG))], _store_epilogue,
                    a_map=lambda i, j, k: (k + grp(j) * (S // T1), i),
                    b_map=lambda i, j, k: (k + grp(j) * (S // T1), j % (3 * D // TG)), mnk=(D, 9 * D, S))[0]
    sc_4, tok = _scatter_start("scatter_4", [g_aout, g_ain], [0, 1])
    dxc3 = _matmul("attn_in_bwd", dP3, wa_in, "nt", TH, D, 3 * D, [(_sds((3 * S, D), F32), _ij_spec(TH, D))], _store_epilogue,
                   b_map=lambda i, j, k: (j, k + i // (S // TH)), mnk=(3 * S, D, 3 * D), dep=tok)[0]
    grad_x = _sum4(du1, dxc3, back(dxc3, 1), back(dxc3, 2), TH)

    def reduced(name, state, after):
        gs, lands = _scatter_wait(name + "_wait", state, after)
        return [_reduce_join(f"{name}_reduce_{i}", l, _own_piece(g, ax)) for i, (l, g, ax) in enumerate(zip(lands, gs, state[4]))]

    r_down1, r_up1 = reduced("scatter_1", sc_1, grad_x)
    r_hout, r_hin = reduced("scatter_2", sc_2, r_up1)
    r_down0, r_up0 = reduced("scatter_3", sc_3, r_hin)

    small = jnp.concatenate([d_lb_logits, dg_mix0, dg_mix1, db_mix0, db_mix1, dg_ffn0, dg_ffn1, db_ffn0, db_ffn1,
                             dg_norm, sq, jnp.zeros((4, D), F32)], axis=0)
    small = _all_reduce_small(small)
    loss = 0.5 * jnp.sum(small[11]) / D
    my_chip = 2 * lax.axis_index("x") + lax.axis_index("y")
    nsh = hgrn_norm_g.shape[1]
    g_norm = lax.dynamic_slice(small[10:11], (0, my_chip * nsh), (1, nsh))

    grads, upd = {}, {}

    def update(nm, w, gr, m, v):
        grads[nm] = gr.reshape(w.shape)
        upd[nm] = _adamw("adamw_" + nm, w, grads[nm], m, v)

    update("hgrn_w_in", hgrn_w_in, r_hin, m_hgrn_w_in, v_hgrn_w_in)
    update("hgrn_w_out", hgrn_w_out, r_hout, m_hgrn_w_out, v_hgrn_w_out)
    update("ffn_w_up", ffn_w_up, jnp.stack([r_up0, r_up1]), m_ffn_w_up, v_ffn_w_up)
    update("ffn_w_down", ffn_w_down, jnp.stack([r_down0, r_down1]), m_ffn_w_down, v_ffn_w_down)
    r_aout, r_ain = reduced("scatter_4", sc_4, upd["ffn_w_down"][2])
    update("attn_w_in", attn_w_in, r_ain, m_attn_w_in, v_attn_w_in)
    update("attn_w_out", attn_w_out, r_aout, m_attn_w_out, v_attn_w_out)
    grads["hgrn_norm_g"] = g_norm
    upd["hgrn_norm_g"] = _adamw("adamw_hgrn_norm_g", hgrn_norm_g, g_norm, m_hgrn_norm_g, v_hgrn_norm_g)
    cat = lambda ts: jnp.concatenate(ts, axis=0)
    small_w = cat([lb_logits, ln_mix_g, ln_mix_b, ln_ffn_g, ln_ffn_b])
    small_m = cat([m_lb_logits, m_ln_mix_g, m_ln_mix_b, m_ln_ffn_g, m_ln_ffn_b])
    small_v = cat([v_lb_logits, v_ln_mix_g, v_ln_mix_b, v_ln_ffn_g, v_ln_ffn_b])
    small_upd = _adamw("adamw_small", small_w, small[0:10], small_m, small_v)
    for i, nm in enumerate(["lb_logits", "ln_mix_g", "ln_mix_b", "ln_ffn_g", "ln_ffn_b"]):
        grads[nm] = small[2 * i:2 * i + 2]
        upd[nm] = tuple(t[2 * i:2 * i + 2] for t in small_upd)

    order = ["attn_w_in", "attn_w_out", "hgrn_w_in", "hgrn_w_out", "hgrn_norm_g", "lb_logits", "ln_mix_g", "ln_mix_b",
             "ln_ffn_g", "ln_ffn_b", "ffn_w_up", "ffn_w_down"]
    return (loss, grad_x[None], *[grads[k] for k in order], *[upd[k][0] for k in order],
            *[upd[k][1] for k in order], *[upd[k][2] for k in order])
```

```python
import functools
import math

import jax
import jax.numpy as jnp
from jax import lax
from jax.experimental import pallas as pl
from jax.experimental.pallas import tpu as pltpu

F32 = jnp.float32
BF16 = jnp.bfloat16
MXU_DTYPE = BF16

HEAD_DIM = 64
ATTN_BLK = 128
DILATIONS = (1, 4, 16)
ROPE_THETA = 10000.0
HGRN_DK = 128
HGRN_CHUNK = 64
DEPTH = 2
LN_EPS = 1e-5
RMS_EPS = 1e-6
ALPHA = (2 * DEPTH) ** 0.25
ADAM_LR, ADAM_B1, ADAM_B2, ADAM_EPS, ADAM_WD, ADAM_STEP = 0.001, 0.9, 0.999, 1e-08, 0.01, 10

LANES = 128
VMEM_LIMIT = 56 * 1024 * 1024
NEG = -1e30
MESH = pl.DeviceIdType.MESH


def _cparams(sem=None):
    return pltpu.CompilerParams(dimension_semantics=sem, vmem_limit_bytes=VMEM_LIMIT)


def _sds(shape, dtype):
    return jax.ShapeDtypeStruct(tuple(shape), dtype)


def _dg(a, b, ca, cb):
    return lax.dot_general(a, b, (((ca,), (cb,)), ((), ())), preferred_element_type=F32)


def _nn(a, b):
    return _dg(a, b, 1, 0)


def _nt(a, b):
    return _dg(a, b, 1, 1)


def _tn(a, b):
    return _dg(a, b, 0, 0)


def _split3(a):
    hi = a.astype(BF16)
    r = a - hi.astype(F32)
    mid = r.astype(BF16)
    lo = (r - mid.astype(F32)).astype(BF16)
    return hi, mid, lo


def _exact_nn(a, sel):
    hi, mid, lo = _split3(a)
    return _nn(hi, sel) + _nn(mid, sel) + _nn(lo, sel)


def _exact_sel_nn(sel, a):
    hi, mid, lo = _split3(a)
    return _nn(sel, hi) + _nn(sel, mid) + _nn(sel, lo)


def _pick(n, prefs):
    for p in prefs:
        if n % p == 0:
            return p
    return n


def _matmul(name, a, b, form, tm, tn, tk, outs, epilogue, extras=(), a_map=None, b_map=None, mnk=None, dep=None):
    if form == "nn":
        (M, K), N = a.shape, b.shape[1]
        a_spec = pl.BlockSpec((tm, tk), a_map or (lambda i, j, k: (i, k)))
        b_spec = pl.BlockSpec((tk, tn), b_map or (lambda i, j, k: (k, j)))
        ca, cb = 1, 0
    elif form == "nt":
        (M, K), N = a.shape, b.shape[0]
        a_spec = pl.BlockSpec((tm, tk), a_map or (lambda i, j, k: (i, k)))
        b_spec = pl.BlockSpec((tn, tk), b_map or (lambda i, j, k: (j, k)))
        ca, cb = 1, 1
    else:
        (K, M), N = a.shape, b.shape[1]
        a_spec = pl.BlockSpec((tk, tm), a_map or (lambda i, j, k: (k, i)))
        b_spec = pl.BlockSpec((tk, tn), b_map or (lambda i, j, k: (k, j)))
        ca, cb = 0, 0
    if mnk is not None:
        M, N, K = mnk
    assert M % tm == 0 and N % tn == 0 and K % tk == 0, (name, M, N, K, tm, tn, tk)
    nk = K // tk
    ne, no = len(extras), len(outs)
    deps = [] if dep is None else [dep]
    nd = len(deps)

    def body(a_ref, b_ref, *rest):
        extra_refs, out_refs = rest[:ne], rest[ne + nd:ne + nd + no]
        j = pl.program_id(1)
        part = _dg(a_ref[...].astype(MXU_DTYPE), b_ref[...].astype(MXU_DTYPE), ca, cb)
        if nk == 1:
            epilogue(part, extra_refs, out_refs, j)
            return
        acc_ref = rest[-1]
        k = pl.program_id(2)

        @pl.when(k == 0)
        def _():
            acc_ref[...] = part

        @pl.when(k > 0)
        def _():
            acc_ref[...] += part

        @pl.when(k == nk - 1)
        def _():
            epilogue(acc_ref[...], extra_refs, out_refs, j)

    res = pl.pallas_call(
        body,
        name=name,
        grid=(M // tm, N // tn, nk),
        in_specs=[a_spec, b_spec] + [s for _, s in extras] + [pl.BlockSpec(memory_space=pl.ANY)] * nd,
        out_specs=[s for _, s in outs],
        out_shape=[o for o, _ in outs],
        scratch_shapes=[pltpu.VMEM((tm, tn), F32)] if nk > 1 else [],
        compiler_params=_cparams(("parallel", "parallel", "arbitrary")),
    )(a, b, *[e for e, _ in extras], *deps)
    return res


def _ij_spec(tm, tn):
    return pl.BlockSpec((tm, tn), lambda i, j, k: (i, j))


def _store_epilogue(acc, extra_refs, out_refs, j):
    out_refs[0][...] = acc.astype(out_refs[0].dtype)


def _plain_mm(name, a, b, form, out_dtype, tm, tn, tk):
    M = a.shape[1] if form == "tn" else a.shape[0]
    N = b.shape[0] if form == "nt" else b.shape[1]
    return _matmul(name, a, b, form, tm, tn, tk, [(_sds((M, N), out_dtype), _ij_spec(tm, tn))], _store_epilogue)[0]


def _to_classes(t, dil):
    S, W = t.shape
    return t if dil == 1 else t.reshape(S // dil, dil, W).transpose(1, 0, 2).reshape(S, W)


def _from_classes(t, dil):
    S, W = t.shape
    return t if dil == 1 else t.reshape(dil, S // dil, W).transpose(1, 0, 2).reshape(S, W)


def _stack_classes(t):
    return jnp.concatenate([_to_classes(t, d) for d in DILATIONS], axis=0)


def _rope_tables(seq):
    half = HEAD_DIM // 2
    inv = ROPE_THETA ** (-jnp.arange(half, dtype=F32) * (2.0 / HEAD_DIM))
    inv = jnp.tile(inv, LANES // half)
    pos = []
    for d in DILATIONS:
        row = jnp.arange(seq)
        pos.append((row % (seq // d)) * d + row // (seq // d))
    ang = jnp.concatenate(pos).astype(F32)[:, None] * inv[None, :]
    first = (jnp.arange(LANES) % HEAD_DIM) < half
    sin = jnp.sin(ang)
    return jnp.cos(ang), jnp.where(first[None, :], -sin, sin)


def _partner(x):
    half = HEAD_DIM // 2
    lane = lax.broadcasted_iota(jnp.int32, x.shape, 1)
    first = (lane % HEAD_DIM) < half
    return jnp.where(first, pltpu.roll(x, LANES - half, 1), pltpu.roll(x, half, 1))


def _attn_proj(x3, w_full, cos3, sin3, tm, tn):
    S3, D = x3.shape
    S = S3 // 3
    per_part = D // tn
    per_group = 3 * per_part

    def epilogue(acc, extra_refs, out_refs, j):
        cos_ref, sin_ref = extra_refs
        o_ref = out_refs[0]
        is_rot = j // per_part < 2

        @pl.when(is_rot)
        def _():
            c, s = cos_ref[...], sin_ref[...]
            for t in range(tn // LANES):
                xs = acc[:, t * LANES:(t + 1) * LANES]
                o_ref[:, t * LANES:(t + 1) * LANES] = (xs * c + _partner(xs) * s).astype(o_ref.dtype)

        @pl.when(jnp.logical_not(is_rot))
        def _():
            o_ref[...] = acc.astype(o_ref.dtype)

    tab = pl.BlockSpec((tm, LANES), lambda i, j, k: (i, 0))
    return _matmul("attn_proj", x3, w_full, "nn", tm, tn, D, [(_sds((S3, 3 * D), MXU_DTYPE), _ij_spec(tm, tn))],
                   epilogue, extras=[(cos3, tab), (sin3, tab)],
                   b_map=lambda i, j, k: (k, j + (i // (S // tm)) * per_group), mnk=(S3, 3 * D, D))[0]


def _head_sel(d_model):
    h = jnp.arange(LANES)[:, None]
    l = jnp.arange(d_model)[None, :]
    return (l // HEAD_DIM == h).astype(BF16)


def _class_edges(b, nblk):
    g = b // nblk
    per_class = jnp.where(g == 0, nblk // DILATIONS[0], jnp.where(g == 1, nblk // DILATIONS[1], nblk // DILATIONS[2]))
    pos = (b % nblk) % per_class
    return pos != 0, pos != per_class - 1


def _two_heads(t, top):
    zero = jnp.zeros_like(t)
    return jnp.concatenate([jnp.where(top, t, zero), jnp.where(top, zero, t)], axis=0)


def _band_mask(has_prev):
    B = ATTN_BLK
    row = lax.broadcasted_iota(jnp.int32, (2 * B, 2 * B), 0) % B
    col = lax.broadcasted_iota(jnp.int32, (2 * B, 2 * B), 1)
    in_prev = jnp.logical_and(jnp.logical_and(col < B, col >= row), has_prev)
    in_own = jnp.logical_and(col >= B, col - B <= row)
    return jnp.logical_or(in_prev, in_own)


def _attn_fwd(P3, D):
    S3 = P3.shape[0]
    B = ATTN_BLK
    nblk = S3 // 3 // B
    npairs = D // LANES
    scale = HEAD_DIM ** -0.5

    def body(q_ref, kc_ref, vc_ref, kp_ref, vp_ref, o_ref, lse_ref):
        has_prev, _ = _class_edges(pl.program_id(0), nblk)
        ok = _band_mask(has_prev)
        lane = lax.broadcasted_iota(jnp.int32, (B, LANES), 1)
        top = lane < HEAD_DIM
        lse_acc = jnp.zeros((B, LANES), F32)
        for j in range(npairs):
            sl = slice(j * LANES, (j + 1) * LANES)
            Q = _two_heads(q_ref[:, sl] * scale, top)
            K2 = jnp.concatenate([kp_ref[:, sl], kc_ref[:, sl]], axis=0)
            V2 = jnp.concatenate([vp_ref[:, sl], vc_ref[:, sl]], axis=0)
            s = jnp.where(ok, _nt(Q, K2), NEG)
            m = jnp.max(s, axis=1, keepdims=True)
            p = jnp.exp(s - m)
            l = jnp.sum(p, axis=1, keepdims=True)
            o = _nn((p * (1.0 / l)).astype(MXU_DTYPE), V2)
            o_ref[:, sl] = jnp.where(top, o[:B], o[B:])
            lse = m + jnp.log(l)
            lse_acc = jnp.where(lane == 2 * j, lse[:B], jnp.where(lane == 2 * j + 1, lse[B:], lse_acc))
        lse_ref[...] = lse_acc

    blk = lambda part, prev: pl.BlockSpec(
        (B, D), (lambda b: (jnp.maximum(b - 1, 0), part)) if prev else (lambda b: (b, part)))
    return pl.pallas_call(
        body,
        name="attn_fwd",
        grid=(3 * nblk,),
        in_specs=[blk(0, False), blk(1, False), blk(2, False), blk(1, True), blk(2, True)],
        out_specs=[pl.BlockSpec((B, D), lambda b: (b, 0)), pl.BlockSpec((B, LANES), lambda b: (b, 0))],
        out_shape=[_sds((S3, D), F32), _sds((S3, LANES), F32)],
        compiler_params=_cparams(("parallel",)),
    )(P3, P3, P3, P3, P3)


def _attn_mix(os, lses, sel, tm):
    S, D = min(o.shape[0] for o in os), os[0].shape[1]

    def body(o0, o1, o2, l0, l1, l2, sel_ref, o_ref, L_ref):
        a, b, c = l0[...], l1[...], l2[...]
        m = jnp.maximum(jnp.maximum(a, b), c)
        L = m + jnp.log(jnp.exp(a - m) + jnp.exp(b - m) + jnp.exp(c - m))
        L_ref[...] = L
        s = sel_ref[...]
        acc = _exact_nn(jnp.exp(a - L), s) * o0[...]
        acc += _exact_nn(jnp.exp(b - L), s) * o1[...]
        acc += _exact_nn(jnp.exp(c - L), s) * o2[...]
        o_ref[...] = acc

    big = pl.BlockSpec((tm, D), lambda i: (i, 0))
    small = pl.BlockSpec((tm, LANES), lambda i: (i, 0))
    return pl.pallas_call(
        body,
        name="attn_mix",
        grid=(S // tm,),
        in_specs=[big, big, big, small, small, small, pl.BlockSpec((LANES, D), lambda i: (0, 0))],
        out_specs=[big, small],
        out_shape=[_sds((S, D), F32), _sds((S, LANES), F32)],
        compiler_params=_cparams(("parallel",)),
    )(*os, *lses, sel)


def _attn_bwd(P3, do3, L3, delta3, cos3, sin3, D):
    S3 = P3.shape[0]
    B = ATTN_BLK
    nblk = S3 // 3 // B
    npairs = D // LANES
    scale = HEAD_DIM ** -0.5

    def body(c_ref, p_ref, n_ref, doc_ref, don_ref, Lc_ref, Ln_ref, dc_ref, dn_ref, cos_ref, sin_ref, out_ref):
        has_prev, has_next = _class_edges(pl.program_id(0), nblk)
        ok = _band_mask(has_prev)
        row = lax.broadcasted_iota(jnp.int32, (2 * B, B), 0) % B
        col = lax.broadcasted_iota(jnp.int32, (2 * B, B), 1)
        ok_n = jnp.logical_and(col >= row, has_next)
        lane = lax.broadcasted_iota(jnp.int32, (B, LANES), 1)
        top = lane < HEAD_DIM
        cos_t = cos_ref[...]
        sin_inv = -sin_ref[...]
        Lc_all, Ln_all, dc_all, dn_all = Lc_ref[...], Ln_ref[...], dc_ref[...], dn_ref[...]
        pair_col = lambda t, j: jnp.concatenate([t[:, 2 * j:2 * j + 1], t[:, 2 * j + 1:2 * j + 2]], axis=0)
        for j in range(npairs):
            sl = lambda part: slice(part * D + j * LANES, part * D + (j + 1) * LANES)
            kc2, vc2 = c_ref[:, sl(1)], c_ref[:, sl(2)]
            K2 = jnp.concatenate([p_ref[:, sl(1)], kc2], axis=0)
            V2 = jnp.concatenate([p_ref[:, sl(2)], vc2], axis=0)
            Qc = _two_heads(c_ref[:, sl(0)] * scale, top)
            Qn = _two_heads(n_ref[:, sl(0)] * scale, top)
            DOc = _two_heads(doc_ref[:, j * LANES:(j + 1) * LANES].astype(MXU_DTYPE), top)
            DOn = _two_heads(don_ref[:, j * LANES:(j + 1) * LANES].astype(MXU_DTYPE), top)
            P_c = jnp.where(ok, jnp.exp(_nt(Qc, K2) - pair_col(Lc_all, j)), 0.0)
            dS_c = P_c * (_nt(DOc, V2) - pair_col(dc_all, j))
            P_n = jnp.where(ok_n, jnp.exp(_nt(Qn, kc2) - pair_col(Ln_all, j)), 0.0)
            dS_n = P_n * (_nt(DOn, vc2) - pair_col(dn_all, j))
            dq = _nn(dS_c.astype(MXU_DTYPE), K2)
            dq2 = jnp.where(top, dq[:B], dq[B:]) * scale
            Qk = jnp.concatenate([Qc, Qn], axis=0)
            DOk = jnp.concatenate([DOc, DOn], axis=0)
            dk2 = _tn(jnp.concatenate([dS_c[:, B:], dS_n], axis=0).astype(MXU_DTYPE), Qk)
            dv2 = _tn(jnp.concatenate([P_c[:, B:], P_n], axis=0).astype(MXU_DTYPE), DOk)
            out_ref[:, sl(0)] = (dq2 * cos_t + _partner(dq2) * sin_inv).astype(out_ref.dtype)
            out_ref[:, sl(1)] = (dk2 * cos_t + _partner(dk2) * sin_inv).astype(out_ref.dtype)
            out_ref[:, sl(2)] = dv2.astype(out_ref.dtype)

    cur = lambda b: b
    prv = lambda b: jnp.maximum(b - 1, 0)
    nxt = lambda b: jnp.minimum(b + 1, 3 * nblk - 1)
    spec = lambda w, f: pl.BlockSpec((B, w), lambda b: (f(b), 0))
    return pl.pallas_call(
        body,
        name="attn_bwd",
        grid=(3 * nblk,),
        in_specs=[spec(3 * D, cur), spec(3 * D, prv), spec(3 * D, nxt), spec(D, cur), spec(D, nxt),
                  spec(LANES, cur), spec(LANES, nxt), spec(LANES, cur), spec(LANES, nxt), spec(LANES, cur), spec(LANES, cur)],
        out_specs=spec(3 * D, cur),
        out_shape=_sds((S3, 3 * D), MXU_DTYPE),
        compiler_params=_cparams(("parallel",)),
    )(P3, P3, P3, do3, do3, L3, L3, delta3, delta3, cos3, sin3)


def _sum4(a, b, c, d, tm):
    S, D = a.shape

    def body(a_ref, b_ref, c_ref, d_ref, o_ref):
        o_ref[...] = ALPHA * a_ref[...] + b_ref[...] + c_ref[...] + d_ref[...]

    row = pl.BlockSpec((tm, D), lambda i: (i, 0))
    return pl.pallas_call(body, name="sum4", grid=(S // tm,), in_specs=[row] * 4, out_specs=row,
                          out_shape=_sds((S, D), F32), compiler_params=_cparams(("parallel",)))(a, b, c, d)


def _tri(lower):
    r = lax.broadcasted_iota(jnp.int32, (HGRN_CHUNK, HGRN_CHUNK), 0)
    c = lax.broadcasted_iota(jnp.int32, (HGRN_CHUNK, HGRN_CHUNK), 1)
    return ((r >= c) if lower else (r <= c)).astype(BF16)


def _lower_bound(lb_ref):
    l0, l1 = lb_ref[0:1, :], lb_ref[1:2, :]
    m = jnp.maximum(l0, l1)
    e0, e1 = jnp.exp(l0 - m), jnp.exp(l1 - m)
    return e1 / (e0 + e1)


def _hgrn_gates(q_raw, z, lb):
    sg = 1.0 / (1.0 + jnp.exp(-z))
    sn = 1.0 / (1.0 + jnp.exp(z))
    f = lb + (1.0 - lb) * sg
    key = (1.0 - lb) * sn
    sq = 1.0 / (1.0 + jnp.exp(-q_raw))
    return sg, sn, f, key, sq


HGRN_HEADS_PER_STEP = 2


def _hgrn_fwd(P1, lb_logits, norm_g, tb):
    S = P1.shape[0]
    D = P1.shape[1] // 3
    K = HGRN_DK
    H = D // K
    HP = HGRN_HEADS_PER_STEP
    C = HGRN_CHUNK
    cpb = tb // C
    nt = S // tb

    def body(q_ref, f_ref, i_ref, lb_ref, g_ref, o_ref, n_ref, st_ref, state):
        t = pl.program_id(1)

        @pl.when(t == 0)
        def _():
            state[...] = jnp.zeros_like(state)

        lb_all = _lower_bound(lb_ref)
        tri = _tri(True)
        r = lax.broadcasted_iota(jnp.int32, (C, C), 0)
        c = lax.broadcasted_iota(jnp.int32, (C, C), 1)
        causal = r >= c
        for ci in range(cpb):
            rows = slice(ci * C, (ci + 1) * C)
            for hh in range(HP):
                lanes = slice(hh * K, (hh + 1) * K)
                lb = lb_all[:, lanes]
                q_raw, z, v = q_ref[rows, lanes], f_ref[rows, lanes], i_ref[rows, lanes]
                sg, sn, f, key, sq = _hgrn_gates(q_raw, z, lb)
                q = q_raw * sq
                b = _exact_sel_nn(tri, jnp.log(f))
                b_last = b[C - 1:C, :]
                qd = (q * jnp.exp(b)).astype(MXU_DTYPE)
                kd = (key * jnp.exp(-b)).astype(MXU_DTYPE)
                kb = (key * jnp.exp(b_last - b)).astype(MXU_DTYPE)
                vm = v.astype(MXU_DTYPE)
                st = state[hh]
                st_ref[hh, ci] = st
                a = jnp.where(causal, _nt(qd, kd), 0.0)
                o = _nn(a.astype(MXU_DTYPE), vm) + _nt(qd, st.astype(MXU_DTYPE))
                state[hh] = st * jnp.exp(b_last) + _tn(vm, kb)
                o_ref[rows, lanes] = o
                rs = lax.rsqrt(jnp.mean(o * o, axis=1, keepdims=True) + RMS_EPS)
                n_ref[rows, lanes] = o * rs * g_ref[:, lanes]

    tok = lambda part: pl.BlockSpec((tb, HP * K), lambda h, t: (t, part * (H // HP) + h))
    vec = lambda rows: pl.BlockSpec((rows, HP * K), lambda h, t: (0, h))
    return pl.pallas_call(
        body,
        name="hgrn_fwd",
        grid=(H // HP, nt),
        in_specs=[tok(0), tok(1), tok(2), vec(2), vec(1)],
        out_specs=[tok(0), tok(0), pl.BlockSpec((HP, cpb, K, K), lambda h, t: (h, t, 0, 0))],
        out_shape=[_sds((S, D), F32), _sds((S, D), F32), _sds((H, S // C, K, K), F32)],
        scratch_shapes=[pltpu.VMEM((HP, K, K), F32)],
        compiler_params=_cparams(("parallel", "arbitrary")),
    )(P1, P1, P1, lb_logits, norm_g)


def _hgrn_bwd(P1, o_pre, states, dn, lb_logits, norm_g, tb):
    S = P1.shape[0]
    D = P1.shape[1] // 3
    K = HGRN_DK
    H = D // K
    HP = HGRN_HEADS_PER_STEP
    C = HGRN_CHUNK
    cpb = tb // C
    nt = S // tb

    def body(q_ref, f_ref, i_ref, o_ref, st_ref, dn_ref, lb_ref, g_ref, dq_ref, dz_ref, dv_ref, dg_ref, dlb_ref, dstate):
        t = pl.program_id(1)

        @pl.when(t == 0)
        def _():
            dstate[...] = jnp.zeros_like(dstate)
            dg_ref[...] = jnp.zeros_like(dg_ref)
            dlb_ref[...] = jnp.zeros_like(dlb_ref)

        lb_all = _lower_bound(lb_ref)
        tri_l, tri_u = _tri(True), _tri(False)
        r = lax.broadcasted_iota(jnp.int32, (C, C), 0)
        c = lax.broadcasted_iota(jnp.int32, (C, C), 1)
        causal = r >= c
        last_row = lax.broadcasted_iota(jnp.int32, (C, K), 0) == C - 1
        dg_acc = [jnp.zeros((1, K), F32) for _ in range(HP)]
        dlb_acc = [jnp.zeros((1, K), F32) for _ in range(HP)]
        for ci in reversed(range(cpb)):
            rows = slice(ci * C, (ci + 1) * C)
            for hh in range(HP):
                lanes = slice(hh * K, (hh + 1) * K)
                lb = lb_all[:, lanes]
                gn = g_ref[:, lanes]
                q_raw, z, v = q_ref[rows, lanes], f_ref[rows, lanes], i_ref[rows, lanes]
                sg, sn, f, key, sq = _hgrn_gates(q_raw, z, lb)
                q = q_raw * sq
                b = _exact_sel_nn(tri_l, jnp.log(f))
                b_last = b[C - 1:C, :]
                e_pos, e_neg, e_rel = jnp.exp(b), jnp.exp(-b), jnp.exp(b_last - b)
                dec = jnp.exp(b_last)
                qd_f, kd_f, kb_f = q * e_pos, key * e_neg, key * e_rel
                qd, kd, kb = qd_f.astype(MXU_DTYPE), kd_f.astype(MXU_DTYPE), kb_f.astype(MXU_DTYPE)
                vm = v.astype(MXU_DTYPE)
                st = st_ref[hh, ci]
                dst = dstate[hh]
                stm, dstm = st.astype(MXU_DTYPE), dst.astype(MXU_DTYPE)
                a = jnp.where(causal, _nt(qd, kd), 0.0).astype(MXU_DTYPE)
                o = o_ref[rows, lanes]
                dnn = dn_ref[rows, lanes]
                rs = lax.rsqrt(jnp.mean(o * o, axis=1, keepdims=True) + RMS_EPS)
                dg_acc[hh] = dg_acc[hh] + jnp.sum(dnn * o * rs, axis=0, keepdims=True)
                tg = dnn * gn
                do_f = rs * tg - o * (rs * rs * rs) * jnp.mean(tg * o, axis=1, keepdims=True)
                dom = do_f.astype(MXU_DTYPE)
                da = jnp.where(causal, _nt(dom, vm), 0.0).astype(MXU_DTYPE)
                dv = _tn(a, dom) + _nt(kb, dstm)
                dqd = _nn(da, kd) + _nn(dom, stm)
                dkd = _tn(da, qd)
                dkb = _nn(vm, dstm)
                ddec = jnp.sum(dst * st, axis=0, keepdims=True)
                dstate[hh] = dst * dec + _tn(dom, qd)
                dq = dqd * e_pos
                dkey = dkd * e_neg + dkb * e_rel
                tk = dkb * kb_f
                db_last = jnp.sum(tk, axis=0, keepdims=True) + ddec * dec
                db = dqd * qd_f - dkd * kd_f - tk
                db = jnp.where(last_row, db + db_last, db)
                dlogf = _exact_sel_nn(tri_u, db)
                gz = (1.0 - lb) * sg * sn
                dz_ref[rows, lanes] = (dlogf * gz / f - dkey * gz).astype(dz_ref.dtype)
                dlb_acc[hh] = dlb_acc[hh] + jnp.sum(dlogf * sn / f - dkey * sn, axis=0, keepdims=True)
                dq_ref[rows, lanes] = (dq * (sq + q_raw * sq * (1.0 - sq))).astype(dq_ref.dtype)
                dv_ref[rows, lanes] = dv.astype(dv_ref.dtype)
        for hh in range(HP):
            lanes = slice(hh * K, (hh + 1) * K)
            dg_ref[:, lanes] += dg_acc[hh]
            dlb_ref[:, lanes] += dlb_acc[hh]

    rev = lambda t: nt - 1 - t
    tok = lambda part: pl.BlockSpec((tb, HP * K), lambda h, t: (rev(t), part * (H // HP) + h))
    vec = lambda rows: pl.BlockSpec((rows, HP * K), lambda h, t: (0, h))
    outs = pl.pallas_call(
        body,
        name="hgrn_bwd",
        grid=(H // HP, nt),
        in_specs=[tok(0), tok(1), tok(2), tok(0),
                  pl.BlockSpec((HP, cpb, K, K), lambda h, t: (h, rev(t), 0, 0)),
                  tok(0), vec(2), vec(1)],
        out_specs=[tok(0), tok(0), tok(0), vec(1), vec(1)],
        out_shape=[_sds((S, D), MXU_DTYPE)] * 3 + [_sds((1, D), F32)] * 2,
        scratch_shapes=[pltpu.VMEM((HP, K, K), F32)],
        compiler_params=_cparams(("parallel", "arbitrary")),
    )(P1, P1, P1, o_pre, states, dn, lb_logits, norm_g)
    return outs


def _lb_logits_grad(dlb, lb_logits):
    def body(d_ref, l_ref, o_ref):
        s1 = _lower_bound(l_ref)
        d = d_ref[...]
        o_ref[0:1, :] = -(1.0 - s1) * s1 * d
        o_ref[1:2, :] = s1 * (1.0 - s1) * d

    return pl.pallas_call(body, name="lb_logits_grad", out_shape=_sds(lb_logits.shape, F32))(dlb, lb_logits)


def _ln_epilogue(acc, extra_refs, out_refs, j):
    res_ref, g_ref, b_ref = extra_refs
    x_ref, xhat_ref, rstd_ref = out_refs
    u = ALPHA * res_ref[...] + acc
    mu = jnp.mean(u, axis=1, keepdims=True)
    cen = u - mu
    rstd = lax.rsqrt(jnp.mean(cen * cen, axis=1, keepdims=True) + LN_EPS)
    xhat = cen * rstd
    xhat_ref[...] = xhat
    x_ref[...] = xhat * g_ref[...] + b_ref[...]
    rstd_ref[...] = rstd


def _mm_res_ln(name, a, w_full, res, g, b, tm, tk):
    S, D = res.shape
    row = pl.BlockSpec((tm, D), lambda i, j, k: (i, 0))
    vec = pl.BlockSpec((1, D), lambda i, j, k: (0, 0))
    outs = [(_sds((S, D), F32), row), (_sds((S, D), F32), row),
            (_sds((S, 1), F32), pl.BlockSpec((tm, 1), lambda i, j, k: (i, 0)))]
    return _matmul(name, a, w_full, "nn", tm, D, tk, outs, _ln_epilogue, extras=[(res, row), (g, vec), (b, vec)])


def _ln_bwd(name, dy, xhat, rstd, g, tm, dep):
    S, D = dy.shape

    def body(dy_ref, xh_ref, r_ref, g_ref, dep_ref, du_ref, dg_ref, db_ref):
        @pl.when(pl.program_id(0) == 0)
        def _():
            dg_ref[...] = jnp.zeros_like(dg_ref)
            db_ref[...] = jnp.zeros_like(db_ref)

        dy_, xh = dy_ref[...], xh_ref[...]
        dg_ref[...] += jnp.sum(dy_ * xh, axis=0, keepdims=True)
        db_ref[...] += jnp.sum(dy_, axis=0, keepdims=True)
        dxh = dy_ * g_ref[...]
        m1 = jnp.mean(dxh, axis=1, keepdims=True)
        m2 = jnp.mean(dxh * xh, axis=1, keepdims=True)
        du_ref[...] = r_ref[...] * (dxh - m1 - xh * m2)

    row = pl.BlockSpec((tm, D), lambda i: (i, 0))
    vec = pl.BlockSpec((1, D), lambda i: (0, 0))
    return pl.pallas_call(
        body,
        name=name,
        grid=(S // tm,),
        in_specs=[row, row, pl.BlockSpec((tm, 1), lambda i: (i, 0)), vec, pl.BlockSpec(memory_space=pl.ANY)],
        out_specs=[row, vec, vec],
        out_shape=[_sds((S, D), F32), _sds((1, D), F32), _sds((1, D), F32)],
        compiler_params=_cparams(("arbitrary",)),
    )(dy, xhat, rstd, g, dep)


def _loss_head(y, target, tm):
    S, D = y.shape

    def body(y_ref, t_ref, sq_ref, dy_ref):
        @pl.when(pl.program_id(0) == 0)
        def _():
            sq_ref[...] = jnp.zeros_like(sq_ref)

        e = y_ref[...] - t_ref[...]
        sq_ref[...] += jnp.sum(e * e, axis=0, keepdims=True)
        dy_ref[...] = e / D

    row = pl.BlockSpec((tm, D), lambda i: (i, 0))
    vec = pl.BlockSpec((1, D), lambda i: (0, 0))
    return pl.pallas_call(
        body,
        name="loss_head",
        grid=(S // tm,),
        in_specs=[row, row],
        out_specs=[vec, row],
        out_shape=[_sds((1, D), F32), _sds((S, D), F32)],
        compiler_params=_cparams(("arbitrary",)),
    )(y, target)


def _mlp_up(name, x, w_up, tm, tn, tk):
    S = x.shape[0]
    F = w_up.shape[1]

    def epilogue(acc, extra_refs, out_refs, j):
        r = jnp.maximum(acc, 0.0)
        out_refs[0][...] = (r * r).astype(out_refs[0].dtype)

    return _matmul(name, x, w_up, "nn", tm, tn, tk, [(_sds((S, F), MXU_DTYPE), _ij_spec(tm, tn))], epilogue)[0]


def _mlp_down_bwd(name, dy, w_down, a, tm, tn, tk):
    S, F = a.shape

    def epilogue(acc, extra_refs, out_refs, j):
        out_refs[0][...] = (acc * (2.0 * jnp.sqrt(extra_refs[0][...].astype(F32)))).astype(out_refs[0].dtype)

    return _matmul(name, dy, w_down, "nt", tm, tn, tk, [(_sds((S, F), MXU_DTYPE), _ij_spec(tm, tn))], epilogue,
                   extras=[(a, _ij_spec(tm, tn))])[0]


def _mm_nt_res(name, dy, w, du, tm, tn, tk):
    S = dy.shape[0]
    N = w.shape[0]

    def epilogue(acc, extra_refs, out_refs, j):
        out_refs[0][...] = ALPHA * extra_refs[0][...] + acc

    return _matmul(name, dy, w, "nt", tm, tn, tk, [(_sds((S, N), F32), _ij_spec(tm, tn))], epilogue,
                   extras=[(du, _ij_spec(tm, tn))])[0]


def _attn_out_bwd(du, w_out, o, sel_t, tm, tk):
    S, D = o.shape

    def epilogue(acc, extra_refs, out_refs, j):
        out_refs[0][...] = acc.astype(out_refs[0].dtype)
        out_refs[1][...] = _exact_nn(acc * extra_refs[0][...], extra_refs[1][...])

    row = pl.BlockSpec((tm, D), lambda i, j, k: (i, 0))
    slim = pl.BlockSpec((tm, LANES), lambda i, j, k: (i, 0))
    return _matmul("attn_out_bwd", du, w_out, "nt", tm, D, tk,
                   [(_sds((S, D), MXU_DTYPE), row), (_sds((S, LANES), F32), slim)], epilogue,
                   extras=[(o, row), (sel_t, pl.BlockSpec((D, LANES), lambda i, j, k: (0, 0)))])


def _adamw(name, w, g, m, v):
    shape = w.shape
    cols = shape[-1]
    rows = math.prod(shape[:-1])
    w2, g2, m2, v2 = (t.reshape(rows, cols) for t in (w, g, m, v))
    tr = _pick(rows, (256, 128, 64, 32, 16, 8))
    c1 = 1.0 - ADAM_B1 ** ADAM_STEP
    c2 = 1.0 - ADAM_B2 ** ADAM_STEP

    def body(w_ref, g_ref, m_ref, v_ref, d_ref, nm_ref, nv_ref):
        gg = g_ref[...]
        nm = ADAM_B1 * m_ref[...] + (1.0 - ADAM_B1) * gg
        nv = ADAM_B2 * v_ref[...] + (1.0 - ADAM_B2) * (gg * gg)
        nm_ref[...] = nm
        nv_ref[...] = nv
        d_ref[...] = -ADAM_LR * ((nm / c1) / (jnp.sqrt(nv / c2) + ADAM_EPS) + ADAM_WD * w_ref[...])

    blk = pl.BlockSpec((tr, cols), lambda i: (i, 0))
    outs = pl.pallas_call(
        body,
        name=name,
        grid=(rows // tr,),
        in_specs=[blk] * 4,
        out_specs=[blk] * 3,
        out_shape=[_sds((rows, cols), F32)] * 3,
        compiler_params=_cparams(("parallel",)),
    )(w2, g2, m2, v2)
    return tuple(o.reshape(shape) for o in outs)


HBM = pl.BlockSpec(memory_space=pl.ANY)


def _shard_slice(ref, axis, size, index):
    idx = [slice(None)] * len(ref.shape)
    idx[axis] = pl.ds(pl.multiple_of(index * size, 8), size)
    return ref.at[tuple(idx)]


IN_HBM = pl.BlockSpec(memory_space=pltpu.HBM)
IN_SEM = pl.BlockSpec(memory_space=pltpu.SEMAPHORE)
DATAFLOW = pltpu.SideEffectType.DATAFLOW_SIDE_EFFECTING


def _hbm(t):
    return pltpu.with_memory_space_constraint(t, pltpu.HBM)


def _token_spec():
    return pl.BlockSpec(memory_space=pltpu.VMEM)


def _gather_copies(s_refs, f_refs, axes, send, recv, loc, arrival):
    x, y, c = lax.axis_index("x"), lax.axis_index("y"), lax.axis_index("c")
    chips = [(1 - x, y), (x, 1 - y), (1 - x, 1 - y)]
    local, remote = [], []
    for a in range(len(s_refs)):
        size = s_refs[a].shape[axes[a]]
        local.append(pltpu.make_async_copy(s_refs[a], _shard_slice(f_refs[a], axes[a], size, 2 * x + y), loc.at[a]))
        for k, (px, py) in enumerate(chips):
            block = (2 * px + py) if arrival else (2 * x + y)
            remote.append(pltpu.make_async_remote_copy(
                src_ref=s_refs[a], dst_ref=_shard_slice(f_refs[a], axes[a], size, block), send_sem=send.at[3 * a + k],
                recv_sem=recv.at[3 * a + k], device_id=(px, py, c), device_id_type=MESH))
    return local, remote


def _gather_start(name, shards, axes, after):
    n = len(shards)
    fulls = []
    for s, ax in zip(shards, axes):
        fs = list(s.shape)
        fs[ax] *= 4
        fulls.append(lax.empty(tuple(fs), s.dtype))

    def body(*refs):
        s_refs, f_refs = refs[:n], refs[n:2 * n]
        send, recv, loc, token = refs[2 * n + 1], refs[2 * n + 2], refs[2 * n + 3], refs[-1]
        local, remote = _gather_copies(s_refs, f_refs, axes, send, recv, loc, arrival=False)
        for cp in remote + local:
            cp.start()
        token[...] = jnp.zeros_like(token)

    outs = pl.pallas_call(
        body,
        name=name,
        out_shape=(pltpu.SemaphoreType.DMA((3 * n,)), pltpu.SemaphoreType.DMA((3 * n,)), pltpu.SemaphoreType.DMA((n,)),
                   *[pltpu.HBM(t.shape, t.dtype) for t in shards + fulls], _sds((8, LANES), F32)),
        in_specs=[IN_HBM] * (2 * n) + [HBM],
        out_specs=(IN_SEM, IN_SEM, IN_SEM, *[IN_HBM] * (2 * n), _token_spec()),
        input_output_aliases={i: 3 + i for i in range(2 * n)},
        compiler_params=pltpu.CompilerParams(has_side_effects=DATAFLOW),
    )(*[_hbm(t) for t in shards + fulls], after)
    return (outs[0], outs[1], outs[2], list(outs[3:3 + n]), list(outs[3 + n:3 + 2 * n]), axes), outs[-1]


def _gather_wait(name, state, after):
    send, recv, loc, s_thru, f_thru, axes = state
    n = len(s_thru)

    def body(*refs):
        s_refs, f_refs = refs[:n], refs[n:2 * n]
        local, remote = _gather_copies(s_refs, f_refs, axes, refs[2 * n], refs[2 * n + 1], refs[2 * n + 2], arrival=True)
        for cp in local:
            cp.wait()
        for cp in remote:
            cp.wait_send()
            cp.wait_recv()

    outs = pl.pallas_call(
        body,
        name=name,
        out_shape=tuple(pltpu.HBM(t.shape, t.dtype) for t in s_thru + f_thru),
        in_specs=[IN_HBM] * (2 * n) + [IN_SEM, IN_SEM, IN_SEM, HBM],
        out_specs=tuple([IN_HBM] * (2 * n)),
        input_output_aliases={i: i for i in range(2 * n)},
        compiler_params=pltpu.CompilerParams(has_side_effects=DATAFLOW),
    )(*s_thru, *f_thru, send, recv, loc, after)
    return list(outs[n:2 * n])


FLIPS = [(fx, fy, fc) for fx in (0, 1) for fy in (0, 1) for fc in (0, 1)][1:]


def _piece_shape(shape, axis):
    ps = list(shape)
    if axis == 0:
        ps[0] //= 8
    else:
        ps[0] //= 2
        ps[axis] //= 4
    return tuple(ps)


def _piece(ref, axis, q, c):
    shape = ref.shape
    idx = [slice(None)] * len(shape)
    if axis == 0:
        h = shape[0] // 8
        idx[0] = pl.ds(pl.multiple_of((2 * q + c) * h, 8), h)
    else:
        h, w = shape[0] // 2, shape[axis] // 4
        idx[0] = pl.ds(c * h, h)
        idx[axis] = pl.ds(pl.multiple_of(q * w, LANES if axis == len(shape) - 1 else 8), w)
    return ref.at[tuple(idx)]


def _own_piece(g, axis):
    ps = _piece_shape(g.shape, axis)
    q, c = 2 * lax.axis_index("x") + lax.axis_index("y"), lax.axis_index("c")
    start = [0] * len(ps)
    if axis == 0:
        start[0] = (2 * q + c) * ps[0]
    else:
        start[0] = c * ps[0]
        start[axis] = q * ps[axis]
    return lax.dynamic_slice(g, start, ps)


def _scatter_copies(g_refs, l_refs, axes, send, recv):
    x, y, c = lax.axis_index("x"), lax.axis_index("y"), lax.axis_index("c")
    out = []
    for a in range(len(g_refs)):
        for k, (fx, fy, fc) in enumerate(FLIPS):
            tx, ty, tc = x ^ fx, y ^ fy, c ^ fc
            out.append(pltpu.make_async_remote_copy(
                src_ref=_piece(g_refs[a], axes[a], 2 * tx + ty, tc), dst_ref=l_refs[a].at[k],
                send_sem=send.at[7 * a + k], recv_sem=recv.at[7 * a + k], device_id=(tx, ty, tc), device_id_type=MESH))
    return out


def _scatter_start(name, grads, axes):
    n = len(grads)
    lands = [lax.empty((7,) + _piece_shape(g.shape, ax), g.dtype) for g, ax in zip(grads, axes)]

    def body(*refs):
        g_refs, l_refs = refs[:n], refs[n:2 * n]
        send, recv, token = refs[2 * n], refs[2 * n + 1], refs[-1]
        for cp in _scatter_copies(g_refs, l_refs, axes, send, recv):
            cp.start()
        token[...] = jnp.zeros_like(token)

    outs = pl.pallas_call(
        body,
        name=name,
        out_shape=(pltpu.SemaphoreType.DMA((7 * n,)), pltpu.SemaphoreType.DMA((7 * n,)),
                   *[pltpu.HBM(t.shape, t.dtype) for t in grads + lands], _sds((8, LANES), F32)),
        in_specs=[IN_HBM] * (2 * n),
        out_specs=(IN_SEM, IN_SEM, *[IN_HBM] * (2 * n), _token_spec()),
        input_output_aliases={i: 2 + i for i in range(2 * n)},
        compiler_params=pltpu.CompilerParams(has_side_effects=DATAFLOW),
    )(*[_hbm(t) for t in grads + lands])
    return (outs[0], outs[1], list(outs[2:2 + n]), list(outs[2 + n:2 + 2 * n]), axes), outs[-1]


def _scatter_wait(name, state, after):
    send, recv, g_thru, l_thru, axes = state
    n = len(g_thru)

    def body(*refs):
        g_refs, l_refs = refs[:n], refs[n:2 * n]
        for cp in _scatter_copies(g_refs, l_refs, axes, refs[2 * n], refs[2 * n + 1]):
            cp.wait_send()
            cp.wait_recv()

    outs = pl.pallas_call(
        body,
        name=name,
        out_shape=tuple(pltpu.HBM(t.shape, t.dtype) for t in g_thru + l_thru),
        in_specs=[IN_HBM] * (2 * n) + [IN_SEM, IN_SEM, HBM],
        out_specs=tuple([IN_HBM] * (2 * n)),
        input_output_aliases={i: i for i in range(2 * n)},
        compiler_params=pltpu.CompilerParams(has_side_effects=DATAFLOW),
    )(*g_thru, *l_thru, send, recv, after)
    return list(outs[:n]), list(outs[n:2 * n])


def _reduce_join(name, landing, own):
    piece = own.shape
    C = piece[-1]
    R = math.prod(piece[:-1])
    l3 = landing.reshape(7, R, C)
    own2 = own.reshape(R, C)
    tr = _pick(R, [t for t in (512, 256, 128, 64, 32, 16, 8) if t * C <= 256 * 1024])
    nsteps = R // tr

    def body(own_ref, l_ref, o_ref, buf, send, loc, recv):
        i = pl.program_id(0)
        x, y, c = lax.axis_index("x"), lax.axis_index("y"), lax.axis_index("c")
        sibling = (x, y, 1 - c)

        def copies(slot, step):
            dst = o_ref.at[pl.ds(pl.multiple_of(c * R + step * tr, 8), tr), :]
            return (pltpu.make_async_copy(buf.at[slot], dst, loc.at[slot]),
                    pltpu.make_async_remote_copy(src_ref=buf.at[slot], dst_ref=dst, send_sem=send.at[slot], recv_sem=recv,
                                                 device_id=sibling, device_id_type=MESH))

        @pl.when(i >= 2)
        def _():
            lc, rc = copies(i % 2, i - 2)
            lc.wait()
            rc.wait_send()

        acc = own_ref[...].astype(F32)
        for s in range(7):
            acc = acc + l_ref[s].astype(F32)
        buf[i % 2] = acc
        lc, rc = copies(i % 2, i)
        lc.start()
        rc.start()

        @pl.when(i == nsteps - 1)
        def _():
            for st in range(max(nsteps - 2, 0), nsteps):
                lc, rc = copies(st % 2, st)
                lc.wait()
                rc.wait_send()
            theirs = o_ref.at[pl.ds(pl.multiple_of((1 - c) * R, 8), R), :]
            pltpu.make_async_remote_copy(src_ref=theirs, dst_ref=theirs, send_sem=send.at[0], recv_sem=recv,
                                         device_id=sibling, device_id_type=MESH).wait_recv()

    return pl.pallas_call(
        body,
        name=name,
        grid=(nsteps,),
        in_specs=[pl.BlockSpec((tr, C), lambda i: (i, 0)), pl.BlockSpec((7, tr, C), lambda i: (0, i, 0))],
        out_specs=HBM,
        out_shape=_sds((2 * R, C), F32),
        scratch_shapes=[pltpu.VMEM((2, tr, C), F32), pltpu.SemaphoreType.DMA((2,)), pltpu.SemaphoreType.DMA((2,)),
                        pltpu.SemaphoreType.DMA(())],
        compiler_params=_cparams(("arbitrary",)),
    )(own2, l3)


def _all_reduce_small(v, dep):
    R, D = v.shape

    def body(v_ref, dep_ref, o_ref, land, send, recv):
        x, y, c = lax.axis_index("x"), lax.axis_index("y"), lax.axis_index("c")
        my_slot = 4 * x + 2 * y + c
        land[my_slot] = v_ref[...]
        for k, (fx, fy, fc) in enumerate(FLIPS):
            tx, ty, tc = x ^ fx, y ^ fy, c ^ fc
            pltpu.make_async_remote_copy(src_ref=v_ref, dst_ref=land.at[my_slot], send_sem=send.at[k], recv_sem=recv.at[k],
                                         device_id=(tx, ty, tc), device_id_type=MESH).start()
        for k, (fx, fy, fc) in enumerate(FLIPS):
            tx, ty, tc = x ^ fx, y ^ fy, c ^ fc
            cp = pltpu.make_async_remote_copy(src_ref=v_ref, dst_ref=land.at[4 * tx + 2 * ty + tc], send_sem=send.at[k],
                                              recv_sem=recv.at[k], device_id=(tx, ty, tc), device_id_type=MESH)
            cp.wait_send()
            cp.wait_recv()
        acc = land[0]
        for s in range(1, 8):
            acc = acc + land[s]
        o_ref[...] = acc

    return pl.pallas_call(
        body,
        name="all_reduce_small",
        in_specs=[pl.BlockSpec(memory_space=pltpu.VMEM), pl.BlockSpec(memory_space=pl.ANY)],
        out_specs=pl.BlockSpec(memory_space=pltpu.VMEM),
        out_shape=_sds((R, D), F32),
        scratch_shapes=[pltpu.VMEM((8, R, D), F32), pltpu.SemaphoreType.DMA((7,)), pltpu.SemaphoreType.DMA((7,))],
    )(v, dep)


def kernel(x, attn_w_in, attn_w_out, hgrn_w_in, hgrn_w_out, hgrn_norm_g, lb_logits, ln_mix_g, ln_mix_b, ln_ffn_g, ln_ffn_b, ffn_w_up, ffn_w_down, loss_target, m_attn_w_in, m_attn_w_out, m_hgrn_w_in, m_hgrn_w_out, m_hgrn_norm_g, m_lb_logits, m_ln_mix_g, m_ln_mix_b, m_ln_ffn_g, m_ln_ffn_b, m_ffn_w_up, m_ffn_w_down, v_attn_w_in, v_attn_w_out, v_hgrn_w_in, v_hgrn_w_out, v_hgrn_norm_g, v_lb_logits, v_ln_mix_g, v_ln_mix_b, v_ln_ffn_g, v_ln_ffn_b, v_ffn_w_up, v_ffn_w_down):
    xs = x[0]
    tgt = loss_target[0]
    S, D = xs.shape
    F = ffn_w_up.shape[2] * 4
    T1 = _pick(S, (1024, 512, 256))
    T2 = _pick(S, (2048, 1024, 512))
    TH = _pick(S, (512, 256))
    TB = _pick(S, (512, 256))
    TN = _pick(D, (512, 256, 128))
    TF = _pick(F, (1024, 512))
    TG = _pick(3 * D, (1536, 1024, 768))
    TW = _pick(F, (2048, 1024))

    cast = lambda w: w.astype(MXU_DTYPE)
    st_a, tok = _gather_start("gather_a", [cast(attn_w_in[0])], [1], jnp.zeros((8, LANES), F32))
    st_b, tok = _gather_start("gather_b", [cast(attn_w_out[0]), cast(ffn_w_up[0]), cast(ffn_w_down[0])], [0, 1, 0], tok)
    st_c, tok = _gather_start("gather_c", [cast(hgrn_w_in[0]), cast(hgrn_w_out[0]), hgrn_norm_g, cast(ffn_w_up[1]),
                                           cast(ffn_w_down[1])], [1, 0, 1, 1, 0], tok)

    cos3, sin3 = _rope_tables(S)
    sel = _head_sel(D)
    sel_t = sel.T

    xc3 = _stack_classes(xs.astype(MXU_DTYPE))
    (wa_in,) = _gather_wait("gather_a_wait", st_a, tok)
    P3 = _attn_proj(xc3, wa_in, cos3, sin3, T2, TN)
    o3, lse3 = _attn_fwd(P3, D)
    back = lambda t, g: _from_classes(t[g * S:(g + 1) * S], DILATIONS[g])
    o_att, L_att = _attn_mix([o3, back(o3, 1), back(o3, 2)], [lse3, back(lse3, 1), back(lse3, 2)], sel, TH)
    wa_out, w_up0, w_down0 = _gather_wait("gather_b_wait", st_b, L_att)
    x1, xh1, r1 = _mm_res_ln("attn_out_ln", o_att, wa_out, xs, ln_mix_g[0:1], ln_mix_b[0:1], TH, D)
    a0 = _mlp_up("mlp0_up", x1, w_up0, T1, TF, D)
    x2, xh2, r2 = _mm_res_ln("mlp0_down_ln", a0, w_down0, x1, ln_ffn_g[0:1], ln_ffn_b[0:1], TH, F)

    wh_in, wh_out, norm_g, w_up1, w_down1 = _gather_wait("gather_c_wait", st_c, r2)
    P1 = _plain_mm("hgrn_proj", x2, wh_in, "nn", F32, T1, _pick(3 * D, (1024, 768, 512)), D)
    o_h, n_h, states = _hgrn_fwd(P1, lb_logits, norm_g, TB)
    x3, xh3, r3 = _mm_res_ln("hgrn_out_ln", n_h, wh_out, x2, ln_mix_g[1:2], ln_mix_b[1:2], TH, D)
    a1 = _mlp_up("mlp1_up", x3, w_up1, T1, TF, D)
    x4, xh4, r4 = _mm_res_ln("mlp1_down_ln", a1, w_down1, x3, ln_ffn_g[1:2], ln_ffn_b[1:2], TH, F)

    sq, dx4 = _loss_head(x4, tgt, TH)

    wgrad = lambda name, a, dy, tm, tn: _plain_mm(name, a, dy, "tn", MXU_DTYPE, tm, tn, T1)
    du4, dg_ffn1, db_ffn1 = _ln_bwd("ln_ffn1_bwd", dx4, xh4, r4, ln_ffn_g[1:2], TH, sq)
    dh1 = _mlp_down_bwd("mlp1_down_bwd", du4, w_down1, a1, T1, TF, D)
    g_down1 = wgrad("g_down1", a1, du4, TW, D)
    dx3 = _mm_nt_res("mlp1_up_bwd", dh1, w_up1, du4, TH, D, F)
    g_up1 = wgrad("g_up1", x3, dh1, D, TW)
    sc_1, tok = _scatter_start("scatter_1", [g_down1, g_up1], [0, 1])
    du3, dg_mix1, db_mix1 = _ln_bwd("ln_mix1_bwd", dx3, xh3, r3, ln_mix_g[1:2], TH, tok)
    dn = _plain_mm("hgrn_out_bwd", du3, wh_out, "nt", F32, T1, D, D)
    g_hout = wgrad("g_hgrn_out", n_h, du3, D, D)
    dq_raw, dz, dv, dg_norm, dlb = _hgrn_bwd(P1, o_h, states, dn, lb_logits, norm_g, TB)
    dP1 = jnp.concatenate([dq_raw, dz, dv], axis=1)
    dx2 = _mm_nt_res("hgrn_in_bwd", dP1, wh_in, du3, TH, D, 3 * D)
    g_hin = wgrad("g_hgrn_in", x2, dP1, D, TG)
    d_lb_logits = _lb_logits_grad(dlb, lb_logits)
    sc_2, tok = _scatter_start("scatter_2", [g_hout, g_hin], [0, 1])

    du2, dg_ffn0, db_ffn0 = _ln_bwd("ln_ffn0_bwd", dx2, xh2, r2, ln_ffn_g[0:1], TH, tok)
    dh0 = _mlp_down_bwd("mlp0_down_bwd", du2, w_down0, a0, T1, TF, D)
    g_down0 = wgrad("g_down0", a0, du2, TW, D)
    dx1 = _mm_nt_res("mlp0_up_bwd", dh0, w_up0, du2, TH, D, F)
    g_up0 = wgrad("g_up0", x1, dh0, D, TW)
    sc_3, tok = _scatter_start("scatter_3", [g_down0, g_up0], [0, 1])
    du1, dg_mix0, db_mix0 = _ln_bwd("ln_mix0_bwd", dx1, xh1, r1, ln_mix_g[0:1], TH, tok)
    do, delta = _attn_out_bwd(du1, wa_out, o_att, sel_t, TH, D)
    g_aout = wgrad("g_attn_out", o_att, du1, D, D)
    dP3 = _attn_bwd(P3, _stack_classes(do), _stack_classes(L_att), _stack_classes(delta), cos3, sin3, D)
    small = jnp.concatenate([d_lb_logits, dg_mix0, dg_mix1, db_mix0, db_mix1, dg_ffn0, dg_ffn1, db_ffn0, db_ffn1,
                             dg_norm, sq, jnp.zeros((4, D), F32)], axis=0)
    small = _all_reduce_small(small, dP3)
    loss = 0.5 * jnp.sum(small[11]) / D
    grp = lambda j: j // (3 * D // TG)
    g_ain = _matmul("g_attn_in", xc3, dP3, "tn", D, TG, T1, [(_sds((D, 9 * D), MXU_DTYPE), _ij_spec(D, TG))], _store_epilogue,
                    a_map=lambda i, j, k: (k + grp(j) * (S // T1), i),
                    b_map=lambda i, j, k: (k + grp(j) * (S // T1), j % (3 * D // TG)), mnk=(D, 9 * D, S), dep=small)[0]
    sc_4, tok = _scatter_start("scatter_4", [g_aout, g_ain], [0, 1])
    dxc3 = _matmul("attn_in_bwd", dP3, wa_in, "nt", TH, D, 3 * D, [(_sds((3 * S, D), F32), _ij_spec(TH, D))], _store_epilogue,
                   b_map=lambda i, j, k: (j, k + i // (S // TH)), mnk=(3 * S, D, 3 * D), dep=tok)[0]
    grad_x = _sum4(du1, dxc3, back(dxc3, 1), back(dxc3, 2), TH)

    def reduced(name, state, after):
        gs, lands = _scatter_wait(name + "_wait", state, after)
        return [_reduce_join(f"{name}_reduce_{i}", l, _own_piece(g, ax)) for i, (l, g, ax) in enumerate(zip(lands, gs, state[4]))]

    r_down1, r_up1 = reduced("scatter_1", sc_1, grad_x)
    r_hout, r_hin = reduced("scatter_2", sc_2, r_up1)
    r_down0, r_up0 = reduced("scatter_3", sc_3, r_hin)

    my_chip = 2 * lax.axis_index("x") + lax.axis_index("y")
    nsh = hgrn_norm_g.shape[1]
    g_norm = lax.dynamic_slice(small[10:11], (0, my_chip * nsh), (1, nsh))

    grads, upd = {}, {}

    def update(nm, w, gr, m, v):
        grads[nm] = gr.reshape(w.shape)
        upd[nm] = _adamw("adamw_" + nm, w, grads[nm], m, v)

    update("hgrn_w_in", hgrn_w_in, r_hin, m_hgrn_w_in, v_hgrn_w_in)
    update("hgrn_w_out", hgrn_w_out, r_hout, m_hgrn_w_out, v_hgrn_w_out)
    update("ffn_w_up", ffn_w_up, jnp.stack([r_up0, r_up1]), m_ffn_w_up, v_ffn_w_up)
    update("ffn_w_down", ffn_w_down, jnp.stack([r_down0, r_down1]), m_ffn_w_down, v_ffn_w_down)
    r_aout, r_ain = reduced("scatter_4", sc_4, upd["ffn_w_down"][2])
    update("attn_w_in", attn_w_in, r_ain, m_attn_w_in, v_attn_w_in)
    update("attn_w_out", attn_w_out, r_aout, m_attn_w_out, v_attn_w_out)
    grads["hgrn_norm_g"] = g_norm
    upd["hgrn_norm_g"] = _adamw("adamw_hgrn_norm_g", hgrn_norm_g, g_norm, m_hgrn_norm_g, v_hgrn_norm_g)
    cat = lambda ts: jnp.concatenate(ts, axis=0)
    small_w = cat([lb_logits, ln_mix_g, ln_mix_b, ln_ffn_g, ln_ffn_b])
    small_m = cat([m_lb_logits, m_ln_mix_g, m_ln_mix_b, m_ln_ffn_g, m_ln_ffn_b])
    small_v = cat([v_lb_logits, v_ln_mix_g, v_ln_mix_b, v_ln_ffn_g, v_ln_ffn_b])
    small_upd = _adamw("adamw_small", small_w, small[0:10], small_m, small_v)
    for i, nm in enumerate(["lb_logits", "ln_mix_g", "ln_mix_b", "ln_ffn_g", "ln_ffn_b"]):
        grads[nm] = small[2 * i:2 * i + 2]
        upd[nm] = tuple(t[2 * i:2 * i + 2] for t in small_upd)

    order = ["attn_w_in", "attn_w_out", "hgrn_w_in", "hgrn_w_out", "hgrn_norm_g", "lb_logits", "ln_mix_g", "ln_mix_b",
             "ln_ffn_g", "ln_ffn_b", "ffn_w_up", "ffn_w_down"]
    return (loss, grad_x[None], *[grads[k] for k in order], *[upd[k][0] for k in order],
            *[upd[k][1] for k in order], *[upd[k][2] for k in order])
```

```python
import functools
import math

import jax
import jax.numpy as jnp
from jax import lax
from jax.experimental import pallas as pl
from jax.experimental.pallas import tpu as pltpu

F32 = jnp.float32
BF16 = jnp.bfloat16
MXU_DTYPE = BF16

HEAD_DIM = 64
ATTN_BLK = 128
DILATIONS = (1, 4, 16)
ROPE_THETA = 10000.0
HGRN_DK = 128
HGRN_CHUNK = 64
DEPTH = 2
LN_EPS = 1e-5
RMS_EPS = 1e-6
ALPHA = (2 * DEPTH) ** 0.25
ADAM_LR, ADAM_B1, ADAM_B2, ADAM_EPS, ADAM_WD, ADAM_STEP = 0.001, 0.9, 0.999, 1e-08, 0.01, 10

LANES = 128
VMEM_LIMIT = 56 * 1024 * 1024
NEG = -1e30
MESH = pl.DeviceIdType.MESH


def _cparams(sem=None):
    return pltpu.CompilerParams(dimension_semantics=sem, vmem_limit_bytes=VMEM_LIMIT)


def _sds(shape, dtype):
    return jax.ShapeDtypeStruct(tuple(shape), dtype)


def _dg(a, b, ca, cb):
    return lax.dot_general(a, b, (((ca,), (cb,)), ((), ())), preferred_element_type=F32)


def _nn(a, b):
    return _dg(a, b, 1, 0)


def _nt(a, b):
    return _dg(a, b, 1, 1)


def _tn(a, b):
    return _dg(a, b, 0, 0)


def _split3(a):
    hi = a.astype(BF16)
    r = a - hi.astype(F32)
    mid = r.astype(BF16)
    lo = (r - mid.astype(F32)).astype(BF16)
    return hi, mid, lo


def _exact_nn(a, sel):
    hi, mid, lo = _split3(a)
    return _nn(hi, sel) + _nn(mid, sel) + _nn(lo, sel)


def _exact_sel_nn(sel, a):
    hi, mid, lo = _split3(a)
    return _nn(sel, hi) + _nn(sel, mid) + _nn(sel, lo)


def _pick(n, prefs):
    for p in prefs:
        if n % p == 0:
            return p
    return n


def _matmul(name, a, b, form, tm, tn, tk, outs, epilogue, extras=(), a_map=None, b_map=None, mnk=None, dep=None):
    if form == "nn":
        (M, K), N = a.shape, b.shape[1]
        a_spec = pl.BlockSpec((tm, tk), a_map or (lambda i, j, k: (i, k)))
        b_spec = pl.BlockSpec((tk, tn), b_map or (lambda i, j, k: (k, j)))
        ca, cb = 1, 0
    elif form == "nt":
        (M, K), N = a.shape, b.shape[0]
        a_spec = pl.BlockSpec((tm, tk), a_map or (lambda i, j, k: (i, k)))
        b_spec = pl.BlockSpec((tn, tk), b_map or (lambda i, j, k: (j, k)))
        ca, cb = 1, 1
    else:
        (K, M), N = a.shape, b.shape[1]
        a_spec = pl.BlockSpec((tk, tm), a_map or (lambda i, j, k: (k, i)))
        b_spec = pl.BlockSpec((tk, tn), b_map or (lambda i, j, k: (k, j)))
        ca, cb = 0, 0
    if mnk is not None:
        M, N, K = mnk
    assert M % tm == 0 and N % tn == 0 and K % tk == 0, (name, M, N, K, tm, tn, tk)
    nk = K // tk
    ne, no = len(extras), len(outs)
    deps = [] if dep is None else [dep]
    nd = len(deps)

    def body(a_ref, b_ref, *rest):
        extra_refs, out_refs = rest[:ne], rest[ne + nd:ne + nd + no]
        j = pl.program_id(1)
        part = _dg(a_ref[...].astype(MXU_DTYPE), b_ref[...].astype(MXU_DTYPE), ca, cb)
        if nk == 1:
            epilogue(part, extra_refs, out_refs, j)
            return
        acc_ref = rest[-1]
        k = pl.program_id(2)

        @pl.when(k == 0)
        def _():
            acc_ref[...] = part

        @pl.when(k > 0)
        def _():
            acc_ref[...] += part

        @pl.when(k == nk - 1)
        def _():
            epilogue(acc_ref[...], extra_refs, out_refs, j)

    res = pl.pallas_call(
        body,
        name=name,
        grid=(M // tm, N // tn, nk),
        in_specs=[a_spec, b_spec] + [s for _, s in extras] + [pl.BlockSpec(memory_space=pl.ANY)] * nd,
        out_specs=[s for _, s in outs],
        out_shape=[o for o, _ in outs],
        scratch_shapes=[pltpu.VMEM((tm, tn), F32)] if nk > 1 else [],
        compiler_params=_cparams(("parallel", "parallel", "arbitrary")),
    )(a, b, *[e for e, _ in extras], *deps)
    return res


def _ij_spec(tm, tn):
    return pl.BlockSpec((tm, tn), lambda i, j, k: (i, j))


def _store_epilogue(acc, extra_refs, out_refs, j):
    out_refs[0][...] = acc.astype(out_refs[0].dtype)


def _plain_mm(name, a, b, form, out_dtype, tm, tn, tk):
    M = a.shape[1] if form == "tn" else a.shape[0]
    N = b.shape[0] if form == "nt" else b.shape[1]
    return _matmul(name, a, b, form, tm, tn, tk, [(_sds((M, N), out_dtype), _ij_spec(tm, tn))], _store_epilogue)[0]


def _class_slabs(S):
    assert DILATIONS[0] == 1
    return [(g, d, r, S // d) for g, d in enumerate(DILATIONS) if d > 1 for r in range(d)]


def _stack_classes(name, t, out_dtype):
    S, W = t.shape

    def body(x_ref, o_ref):
        o_ref[0:S, :] = x_ref[...].astype(out_dtype)
        for g, d, r, n in _class_slabs(S):
            o_ref[g * S + r * n:g * S + (r + 1) * n, :] = x_ref[pl.ds(r, n, stride=d), :].astype(out_dtype)

    return pl.pallas_call(
        body,
        name=name,
        grid=(W // LANES,),
        in_specs=[pl.BlockSpec((S, LANES), lambda j: (0, j))],
        out_specs=pl.BlockSpec((3 * S, LANES), lambda j: (0, j)),
        out_shape=_sds((3 * S, W), out_dtype),
        compiler_params=_cparams(("parallel",)),
    )(t)


def _rope_tables(seq):
    half = HEAD_DIM // 2
    inv = ROPE_THETA ** (-jnp.arange(half, dtype=F32) * (2.0 / HEAD_DIM))
    inv = jnp.tile(inv, LANES // half)
    pos = []
    for d in DILATIONS:
        row = jnp.arange(seq)
        pos.append((row % (seq // d)) * d + row // (seq // d))
    ang = jnp.concatenate(pos).astype(F32)[:, None] * inv[None, :]
    first = (jnp.arange(LANES) % HEAD_DIM) < half
    sin = jnp.sin(ang)
    return jnp.cos(ang), jnp.where(first[None, :], -sin, sin)


def _partner(x):
    half = HEAD_DIM // 2
    lane = lax.broadcasted_iota(jnp.int32, x.shape, 1)
    first = (lane % HEAD_DIM) < half
    return jnp.where(first, pltpu.roll(x, LANES - half, 1), pltpu.roll(x, half, 1))


def _attn_proj(x3, w_full, cos3, sin3, tm, tn):
    S3, D = x3.shape
    S = S3 // 3
    per_part = D // tn
    per_group = 3 * per_part

    def epilogue(acc, extra_refs, out_refs, j):
        cos_ref, sin_ref = extra_refs
        o_ref = out_refs[0]
        is_rot = j // per_part < 2

        @pl.when(is_rot)
        def _():
            c, s = cos_ref[...], sin_ref[...]
            for t in range(tn // LANES):
                xs = acc[:, t * LANES:(t + 1) * LANES]
                o_ref[:, t * LANES:(t + 1) * LANES] = (xs * c + _partner(xs) * s).astype(o_ref.dtype)

        @pl.when(jnp.logical_not(is_rot))
        def _():
            o_ref[...] = acc.astype(o_ref.dtype)

    tab = pl.BlockSpec((tm, LANES), lambda i, j, k: (i, 0))
    return _matmul("attn_proj", x3, w_full, "nn", tm, tn, D, [(_sds((S3, 3 * D), MXU_DTYPE), _ij_spec(tm, tn))],
                   epilogue, extras=[(cos3, tab), (sin3, tab)],
                   b_map=lambda i, j, k: (k, j + (i // (S // tm)) * per_group), mnk=(S3, 3 * D, D))[0]


def _head_sel(d_model):
    h = jnp.arange(LANES)[:, None]
    l = jnp.arange(d_model)[None, :]
    return (l // HEAD_DIM == h).astype(BF16)


def _class_edges(b, nblk):
    g = b // nblk
    per_class = jnp.where(g == 0, nblk // DILATIONS[0], jnp.where(g == 1, nblk // DILATIONS[1], nblk // DILATIONS[2]))
    pos = (b % nblk) % per_class
    return pos != 0, pos != per_class - 1


def _two_heads(t, top):
    zero = jnp.zeros_like(t)
    return jnp.concatenate([jnp.where(top, t, zero), jnp.where(top, zero, t)], axis=0)


def _band_mask(has_prev):
    B = ATTN_BLK
    row = lax.broadcasted_iota(jnp.int32, (2 * B, 2 * B), 0) % B
    col = lax.broadcasted_iota(jnp.int32, (2 * B, 2 * B), 1)
    in_prev = jnp.logical_and(jnp.logical_and(col < B, col >= row), has_prev)
    in_own = jnp.logical_and(col >= B, col - B <= row)
    return jnp.logical_or(in_prev, in_own)


def _attn_fwd(P3, D):
    S3 = P3.shape[0]
    B = ATTN_BLK
    nblk = S3 // 3 // B
    npairs = D // LANES
    scale = HEAD_DIM ** -0.5

    def body(q_ref, kc_ref, vc_ref, kp_ref, vp_ref, o_ref, lse_ref):
        has_prev, _ = _class_edges(pl.program_id(0), nblk)
        ok = _band_mask(has_prev)
        lane = lax.broadcasted_iota(jnp.int32, (B, LANES), 1)
        top = lane < HEAD_DIM
        lse_acc = jnp.zeros((B, LANES), F32)
        for j in range(npairs):
            sl = slice(j * LANES, (j + 1) * LANES)
            Q = _two_heads(q_ref[:, sl] * scale, top)
            K2 = jnp.concatenate([kp_ref[:, sl], kc_ref[:, sl]], axis=0)
            V2 = jnp.concatenate([vp_ref[:, sl], vc_ref[:, sl]], axis=0)
            s = jnp.where(ok, _nt(Q, K2), NEG)
            m = jnp.max(s, axis=1, keepdims=True)
            p = jnp.exp(s - m)
            l = jnp.sum(p, axis=1, keepdims=True)
            o = _nn((p * (1.0 / l)).astype(MXU_DTYPE), V2)
            o_ref[:, sl] = jnp.where(top, o[:B], o[B:])
            lse = m + jnp.log(l)
            lse_acc = jnp.where(lane == 2 * j, lse[:B], jnp.where(lane == 2 * j + 1, lse[B:], lse_acc))
        lse_ref[...] = lse_acc

    blk = lambda part, prev: pl.BlockSpec(
        (B, D), (lambda b: (jnp.maximum(b - 1, 0), part)) if prev else (lambda b: (b, part)))
    return pl.pallas_call(
        body,
        name="attn_fwd",
        grid=(3 * nblk,),
        in_specs=[blk(0, False), blk(1, False), blk(2, False), blk(1, True), blk(2, True)],
        out_specs=[pl.BlockSpec((B, D), lambda b: (b, 0)), pl.BlockSpec((B, LANES), lambda b: (b, 0))],
        out_shape=[_sds((S3, D), F32), _sds((S3, LANES), F32)],
        compiler_params=_cparams(("parallel",)),
    )(P3, P3, P3, P3, P3)


def _attn_mix(o3, lse3, sel):
    S3, D = o3.shape
    S = S3 // 3

    def body(o3_ref, lse_ref, sel_ref, o_ref, L_ref, w_ref):
        @pl.when(pl.program_id(0) == 0)
        def _():
            w_ref[0] = lse_ref[0:S, :]
            for g, d, r, n in _class_slabs(S):
                w_ref[g, pl.ds(r, n, stride=d), :] = lse_ref[g * S + r * n:g * S + (r + 1) * n, :]
            a, b, c = w_ref[0], w_ref[1], w_ref[2]
            m = jnp.maximum(jnp.maximum(a, b), c)
            L = m + jnp.log(jnp.exp(a - m) + jnp.exp(b - m) + jnp.exp(c - m))
            L_ref[...] = L
            w_ref[0] = jnp.exp(a - L)
            w_ref[1] = jnp.exp(b - L)
            w_ref[2] = jnp.exp(c - L)

        s = sel_ref[...]
        o_ref[...] = _exact_nn(w_ref[0], s) * o3_ref[0:S, :]
        for g, d, r, n in _class_slabs(S):
            rows = pl.ds(r, n, stride=d)
            o_ref[rows, :] += _exact_nn(w_ref[g, rows, :], s) * o3_ref[g * S + r * n:g * S + (r + 1) * n, :]

    return pl.pallas_call(
        body,
        name="attn_mix",
        grid=(D // LANES,),
        in_specs=[pl.BlockSpec((S3, LANES), lambda j: (0, j)), pl.BlockSpec((S3, LANES), lambda j: (0, 0)),
                  pl.BlockSpec((LANES, LANES), lambda j: (0, j))],
        out_specs=[pl.BlockSpec((S, LANES), lambda j: (0, j)), pl.BlockSpec((S, LANES), lambda j: (0, 0))],
        out_shape=[_sds((S, D), F32), _sds((S, LANES), F32)],
        scratch_shapes=[pltpu.VMEM((3, S, LANES), F32)],
        compiler_params=_cparams(("arbitrary",)),
    )(o3, lse3, sel)


def _attn_bwd(P3, do3, L3, delta3, cos3, sin3, D):
    S3 = P3.shape[0]
    B = ATTN_BLK
    nblk = S3 // 3 // B
    npairs = D // LANES
    scale = HEAD_DIM ** -0.5

    def body(c_ref, p_ref, n_ref, doc_ref, don_ref, Lc_ref, Ln_ref, dc_ref, dn_ref, cos_ref, sin_ref, out_ref):
        has_prev, has_next = _class_edges(pl.program_id(0), nblk)
        ok = _band_mask(has_prev)
        row = lax.broadcasted_iota(jnp.int32, (2 * B, B), 0) % B
        col = lax.broadcasted_iota(jnp.int32, (2 * B, B), 1)
        ok_n = jnp.logical_and(col >= row, has_next)
        lane = lax.broadcasted_iota(jnp.int32, (B, LANES), 1)
        top = lane < HEAD_DIM
        cos_t = cos_ref[...]
        sin_inv = -sin_ref[...]
        Lc_all, Ln_all, dc_all, dn_all = Lc_ref[...], Ln_ref[...], dc_ref[...], dn_ref[...]
        pair_col = lambda t, j: jnp.concatenate([t[:, 2 * j:2 * j + 1], t[:, 2 * j + 1:2 * j + 2]], axis=0)
        for j in range(npairs):
            sl = lambda part: slice(part * D + j * LANES, part * D + (j + 1) * LANES)
            kc2, vc2 = c_ref[:, sl(1)], c_ref[:, sl(2)]
            K2 = jnp.concatenate([p_ref[:, sl(1)], kc2], axis=0)
            V2 = jnp.concatenate([p_ref[:, sl(2)], vc2], axis=0)
            Qc = _two_heads(c_ref[:, sl(0)] * scale, top)
            Qn = _two_heads(n_ref[:, sl(0)] * scale, top)
            DOc = _two_heads(doc_ref[:, j * LANES:(j + 1) * LANES].astype(MXU_DTYPE), top)
            DOn = _two_heads(don_ref[:, j * LANES:(j + 1) * LANES].astype(MXU_DTYPE), top)
            P_c = jnp.where(ok, jnp.exp(_nt(Qc, K2) - pair_col(Lc_all, j)), 0.0)
            dS_c = P_c * (_nt(DOc, V2) - pair_col(dc_all, j))
            P_n = jnp.where(ok_n, jnp.exp(_nt(Qn, kc2) - pair_col(Ln_all, j)), 0.0)
            dS_n = P_n * (_nt(DOn, vc2) - pair_col(dn_all, j))
            dq = _nn(dS_c.astype(MXU_DTYPE), K2)
            dq2 = jnp.where(top, dq[:B], dq[B:]) * scale
            Qk = jnp.concatenate([Qc, Qn], axis=0)
            DOk = jnp.concatenate([DOc, DOn], axis=0)
            dk2 = _tn(jnp.concatenate([dS_c[:, B:], dS_n], axis=0).astype(MXU_DTYPE), Qk)
            dv2 = _tn(jnp.concatenate([P_c[:, B:], P_n], axis=0).astype(MXU_DTYPE), DOk)
            out_ref[:, sl(0)] = (dq2 * cos_t + _partner(dq2) * sin_inv).astype(out_ref.dtype)
            out_ref[:, sl(1)] = (dk2 * cos_t + _partner(dk2) * sin_inv).astype(out_ref.dtype)
            out_ref[:, sl(2)] = dv2.astype(out_ref.dtype)

    cur = lambda b: b
    prv = lambda b: jnp.maximum(b - 1, 0)
    nxt = lambda b: jnp.minimum(b + 1, 3 * nblk - 1)
    spec = lambda w, f: pl.BlockSpec((B, w), lambda b: (f(b), 0))
    return pl.pallas_call(
        body,
        name="attn_bwd",
        grid=(3 * nblk,),
        in_specs=[spec(3 * D, cur), spec(3 * D, prv), spec(3 * D, nxt), spec(D, cur), spec(D, nxt),
                  spec(LANES, cur), spec(LANES, nxt), spec(LANES, cur), spec(LANES, nxt), spec(LANES, cur), spec(LANES, cur)],
        out_specs=spec(3 * D, cur),
        out_shape=_sds((S3, 3 * D), MXU_DTYPE),
        compiler_params=_cparams(("parallel",)),
    )(P3, P3, P3, do3, do3, L3, L3, delta3, delta3, cos3, sin3)


def _input_grad(du, dx3):
    S, D = du.shape

    def body(du_ref, dx_ref, o_ref):
        o_ref[...] = ALPHA * du_ref[...] + dx_ref[0:S, :]
        for g, d, r, n in _class_slabs(S):
            o_ref[pl.ds(r, n, stride=d), :] += dx_ref[g * S + r * n:g * S + (r + 1) * n, :]

    return pl.pallas_call(
        body,
        name="input_grad",
        grid=(D // LANES,),
        in_specs=[pl.BlockSpec((S, LANES), lambda j: (0, j)), pl.BlockSpec((3 * S, LANES), lambda j: (0, j))],
        out_specs=pl.BlockSpec((S, LANES), lambda j: (0, j)),
        out_shape=_sds((S, D), F32),
        compiler_params=_cparams(("parallel",)),
    )(du, dx3)


def _tri(lower):
    r = lax.broadcasted_iota(jnp.int32, (HGRN_CHUNK, HGRN_CHUNK), 0)
    c = lax.broadcasted_iota(jnp.int32, (HGRN_CHUNK, HGRN_CHUNK), 1)
    return ((r >= c) if lower else (r <= c)).astype(BF16)


def _lower_bound(lb_ref):
    l0, l1 = lb_ref[0:1, :], lb_ref[1:2, :]
    m = jnp.maximum(l0, l1)
    e0, e1 = jnp.exp(l0 - m), jnp.exp(l1 - m)
    return e1 / (e0 + e1)


def _hgrn_gates(q_raw, z, lb):
    sg = 1.0 / (1.0 + jnp.exp(-z))
    sn = 1.0 / (1.0 + jnp.exp(z))
    f = lb + (1.0 - lb) * sg
    key = (1.0 - lb) * sn
    sq = 1.0 / (1.0 + jnp.exp(-q_raw))
    return sg, sn, f, key, sq


HGRN_HEADS_PER_STEP = 2


def _hgrn_fwd(P1, lb_logits, norm_g, tb):
    S = P1.shape[0]
    D = P1.shape[1] // 3
    K = HGRN_DK
    H = D // K
    HP = HGRN_HEADS_PER_STEP
    C = HGRN_CHUNK
    cpb = tb // C
    nt = S // tb

    def body(q_ref, f_ref, i_ref, lb_ref, g_ref, o_ref, n_ref, st_ref, state):
        t = pl.program_id(1)

        @pl.when(t == 0)
        def _():
            state[...] = jnp.zeros_like(state)

        lb_all = _lower_bound(lb_ref)
        tri = _tri(True)
        r = lax.broadcasted_iota(jnp.int32, (C, C), 0)
        c = lax.broadcasted_iota(jnp.int32, (C, C), 1)
        causal = r >= c
        for ci in range(cpb):
            rows = slice(ci * C, (ci + 1) * C)
            for hh in range(HP):
                lanes = slice(hh * K, (hh + 1) * K)
                lb = lb_all[:, lanes]
                q_raw, z, v = q_ref[rows, lanes], f_ref[rows, lanes], i_ref[rows, lanes]
                sg, sn, f, key, sq = _hgrn_gates(q_raw, z, lb)
                q = q_raw * sq
                b = _exact_sel_nn(tri, jnp.log(f))
                b_last = b[C - 1:C, :]
                qd = (q * jnp.exp(b)).astype(MXU_DTYPE)
                kd = (key * jnp.exp(-b)).astype(MXU_DTYPE)
                kb = (key * jnp.exp(b_last - b)).astype(MXU_DTYPE)
                vm = v.astype(MXU_DTYPE)
                st = state[hh]
                st_ref[hh, ci] = st
                a = jnp.where(causal, _nt(qd, kd), 0.0)
                o = _nn(a.astype(MXU_DTYPE), vm) + _nt(qd, st.astype(MXU_DTYPE))
                state[hh] = st * jnp.exp(b_last) + _tn(vm, kb)
                o_ref[rows, lanes] = o
                rs = lax.rsqrt(jnp.mean(o * o, axis=1, keepdims=True) + RMS_EPS)
                n_ref[rows, lanes] = o * rs * g_ref[:, lanes]

    tok = lambda part: pl.BlockSpec((tb, HP * K), lambda h, t: (t, part * (H // HP) + h))
    vec = lambda rows: pl.BlockSpec((rows, HP * K), lambda h, t: (0, h))
    return pl.pallas_call(
        body,
        name="hgrn_fwd",
        grid=(H // HP, nt),
        in_specs=[tok(0), tok(1), tok(2), vec(2), vec(1)],
        out_specs=[tok(0), tok(0), pl.BlockSpec((HP, cpb, K, K), lambda h, t: (h, t, 0, 0))],
        out_shape=[_sds((S, D), F32), _sds((S, D), F32), _sds((H, S // C, K, K), F32)],
        scratch_shapes=[pltpu.VMEM((HP, K, K), F32)],
        compiler_params=_cparams(("parallel", "arbitrary")),
    )(P1, P1, P1, lb_logits, norm_g)


def _hgrn_bwd(P1, o_pre, states, dn, lb_logits, norm_g, tb):
    S = P1.shape[0]
    D = P1.shape[1] // 3
    K = HGRN_DK
    H = D // K
    HP = HGRN_HEADS_PER_STEP
    C = HGRN_CHUNK
    cpb = tb // C
    nt = S // tb

    def body(q_ref, f_ref, i_ref, o_ref, st_ref, dn_ref, lb_ref, g_ref, dq_ref, dz_ref, dv_ref, dg_ref, dlb_ref, dstate):
        t = pl.program_id(1)

        @pl.when(t == 0)
        def _():
            dstate[...] = jnp.zeros_like(dstate)
            dg_ref[...] = jnp.zeros_like(dg_ref)
            dlb_ref[...] = jnp.zeros_like(dlb_ref)

        lb_all = _lower_bound(lb_ref)
        tri_l, tri_u = _tri(True), _tri(False)
        r = lax.broadcasted_iota(jnp.int32, (C, C), 0)
        c = lax.broadcasted_iota(jnp.int32, (C, C), 1)
        causal = r >= c
        last_row = lax.broadcasted_iota(jnp.int32, (C, K), 0) == C - 1
        dg_acc = [jnp.zeros((1, K), F32) for _ in range(HP)]
        dlb_acc = [jnp.zeros((1, K), F32) for _ in range(HP)]
        for ci in reversed(range(cpb)):
            rows = slice(ci * C, (ci + 1) * C)
            for hh in range(HP):
                lanes = slice(hh * K, (hh + 1) * K)
                lb = lb_all[:, lanes]
                gn = g_ref[:, lanes]
                q_raw, z, v = q_ref[rows, lanes], f_ref[rows, lanes], i_ref[rows, lanes]
                sg, sn, f, key, sq = _hgrn_gates(q_raw, z, lb)
                q = q_raw * sq
                b = _exact_sel_nn(tri_l, jnp.log(f))
                b_last = b[C - 1:C, :]
                e_pos, e_neg, e_rel = jnp.exp(b), jnp.exp(-b), jnp.exp(b_last - b)
                dec = jnp.exp(b_last)
                qd_f, kd_f, kb_f = q * e_pos, key * e_neg, key * e_rel
                qd, kd, kb = qd_f.astype(MXU_DTYPE), kd_f.astype(MXU_DTYPE), kb_f.astype(MXU_DTYPE)
                vm = v.astype(MXU_DTYPE)
                st = st_ref[hh, ci]
                dst = dstate[hh]
                stm, dstm = st.astype(MXU_DTYPE), dst.astype(MXU_DTYPE)
                a = jnp.where(causal, _nt(qd, kd), 0.0).astype(MXU_DTYPE)
                o = o_ref[rows, lanes]
                dnn = dn_ref[rows, lanes]
                rs = lax.rsqrt(jnp.mean(o * o, axis=1, keepdims=True) + RMS_EPS)
                dg_acc[hh] = dg_acc[hh] + jnp.sum(dnn * o * rs, axis=0, keepdims=True)
                tg = dnn * gn
                do_f = rs * tg - o * (rs * rs * rs) * jnp.mean(tg * o, axis=1, keepdims=True)
                dom = do_f.astype(MXU_DTYPE)
                da = jnp.where(causal, _nt(dom, vm), 0.0).astype(MXU_DTYPE)
                dv = _tn(a, dom) + _nt(kb, dstm)
                dqd = _nn(da, kd) + _nn(dom, stm)
                dkd = _tn(da, qd)
                dkb = _nn(vm, dstm)
                ddec = jnp.sum(dst * st, axis=0, keepdims=True)
                dstate[hh] = dst * dec + _tn(dom, qd)
                dq = dqd * e_pos
                dkey = dkd * e_neg + dkb * e_rel
                tk = dkb * kb_f
                db_last = jnp.sum(tk, axis=0, keepdims=True) + ddec * dec
                db = dqd * qd_f - dkd * kd_f - tk
                db = jnp.where(last_row, db + db_last, db)
                dlogf = _exact_sel_nn(tri_u, db)
                gz = (1.0 - lb) * sg * sn
                dz_ref[rows, lanes] = (dlogf * gz / f - dkey * gz).astype(dz_ref.dtype)
                dlb_acc[hh] = dlb_acc[hh] + jnp.sum(dlogf * sn / f - dkey * sn, axis=0, keepdims=True)
                dq_ref[rows, lanes] = (dq * (sq + q_raw * sq * (1.0 - sq))).astype(dq_ref.dtype)
                dv_ref[rows, lanes] = dv.astype(dv_ref.dtype)
        for hh in range(HP):
            lanes = slice(hh * K, (hh + 1) * K)
            dg_ref[:, lanes] += dg_acc[hh]
            dlb_ref[:, lanes] += dlb_acc[hh]

    rev = lambda t: nt - 1 - t
    tok = lambda part: pl.BlockSpec((tb, HP * K), lambda h, t: (rev(t), part * (H // HP) + h))
    vec = lambda rows: pl.BlockSpec((rows, HP * K), lambda h, t: (0, h))
    outs = pl.pallas_call(
        body,
        name="hgrn_bwd",
        grid=(H // HP, nt),
        in_specs=[tok(0), tok(1), tok(2), tok(0),
                  pl.BlockSpec((HP, cpb, K, K), lambda h, t: (h, rev(t), 0, 0)),
                  tok(0), vec(2), vec(1)],
        out_specs=[tok(0), tok(0), tok(0), vec(1), vec(1)],
        out_shape=[_sds((S, D), MXU_DTYPE)] * 3 + [_sds((1, D), F32)] * 2,
        scratch_shapes=[pltpu.VMEM((HP, K, K), F32)],
        compiler_params=_cparams(("parallel", "arbitrary")),
    )(P1, P1, P1, o_pre, states, dn, lb_logits, norm_g)
    return outs


def _lb_logits_grad(dlb, lb_logits):
    def body(d_ref, l_ref, o_ref):
        s1 = _lower_bound(l_ref)
        d = d_ref[...]
        o_ref[0:1, :] = -(1.0 - s1) * s1 * d
        o_ref[1:2, :] = s1 * (1.0 - s1) * d

    return pl.pallas_call(body, name="lb_logits_grad", out_shape=_sds(lb_logits.shape, F32))(dlb, lb_logits)


def _ln_epilogue(acc, extra_refs, out_refs, j):
    res_ref, g_ref, b_ref = extra_refs
    x_ref, xhat_ref, rstd_ref = out_refs
    u = ALPHA * res_ref[...] + acc
    mu = jnp.mean(u, axis=1, keepdims=True)
    cen = u - mu
    rstd = lax.rsqrt(jnp.mean(cen * cen, axis=1, keepdims=True) + LN_EPS)
    xhat = cen * rstd
    xhat_ref[...] = xhat
    x_ref[...] = xhat * g_ref[...] + b_ref[...]
    rstd_ref[...] = rstd


def _mm_res_ln(name, a, w_full, res, g, b, tm, tk):
    S, D = res.shape
    row = pl.BlockSpec((tm, D), lambda i, j, k: (i, 0))
    vec = pl.BlockSpec((1, D), lambda i, j, k: (0, 0))
    outs = [(_sds((S, D), F32), row), (_sds((S, D), F32), row),
            (_sds((S, 1), F32), pl.BlockSpec((tm, 1), lambda i, j, k: (i, 0)))]
    return _matmul(name, a, w_full, "nn", tm, D, tk, outs, _ln_epilogue, extras=[(res, row), (g, vec), (b, vec)])


def _ln_bwd(name, dy, xhat, rstd, g, tm, dep):
    S, D = dy.shape

    def body(dy_ref, xh_ref, r_ref, g_ref, dep_ref, du_ref, dg_ref, db_ref):
        @pl.when(pl.program_id(0) == 0)
        def _():
            dg_ref[...] = jnp.zeros_like(dg_ref)
            db_ref[...] = jnp.zeros_like(db_ref)

        dy_, xh = dy_ref[...], xh_ref[...]
        dg_ref[...] += jnp.sum(dy_ * xh, axis=0, keepdims=True)
        db_ref[...] += jnp.sum(dy_, axis=0, keepdims=True)
        dxh = dy_ * g_ref[...]
        m1 = jnp.mean(dxh, axis=1, keepdims=True)
        m2 = jnp.mean(dxh * xh, axis=1, keepdims=True)
        du_ref[...] = r_ref[...] * (dxh - m1 - xh * m2)

    row = pl.BlockSpec((tm, D), lambda i: (i, 0))
    vec = pl.BlockSpec((1, D), lambda i: (0, 0))
    return pl.pallas_call(
        body,
        name=name,
        grid=(S // tm,),
        in_specs=[row, row, pl.BlockSpec((tm, 1), lambda i: (i, 0)), vec, pl.BlockSpec(memory_space=pl.ANY)],
        out_specs=[row, vec, vec],
        out_shape=[_sds((S, D), F32), _sds((1, D), F32), _sds((1, D), F32)],
        compiler_params=_cparams(("arbitrary",)),
    )(dy, xhat, rstd, g, dep)


def _loss_head(y, target, tm):
    S, D = y.shape

    def body(y_ref, t_ref, sq_ref, dy_ref):
        @pl.when(pl.program_id(0) == 0)
        def _():
            sq_ref[...] = jnp.zeros_like(sq_ref)

        e = y_ref[...] - t_ref[...]
        sq_ref[...] += jnp.sum(e * e, axis=0, keepdims=True)
        dy_ref[...] = e / D

    row = pl.BlockSpec((tm, D), lambda i: (i, 0))
    vec = pl.BlockSpec((1, D), lambda i: (0, 0))
    return pl.pallas_call(
        body,
        name="loss_head",
        grid=(S // tm,),
        in_specs=[row, row],
        out_specs=[vec, row],
        out_shape=[_sds((1, D), F32), _sds((S, D), F32)],
        compiler_params=_cparams(("arbitrary",)),
    )(y, target)


def _mlp_up(name, x, w_up, tm, tn, tk):
    S = x.shape[0]
    F = w_up.shape[1]

    def epilogue(acc, extra_refs, out_refs, j):
        r = jnp.maximum(acc, 0.0)
        out_refs[0][...] = (r * r).astype(out_refs[0].dtype)

    return _matmul(name, x, w_up, "nn", tm, tn, tk, [(_sds((S, F), MXU_DTYPE), _ij_spec(tm, tn))], epilogue)[0]


def _mlp_down_bwd(name, dy, w_down, a, tm, tn, tk):
    S, F = a.shape

    def epilogue(acc, extra_refs, out_refs, j):
        out_refs[0][...] = (acc * (2.0 * jnp.sqrt(extra_refs[0][...].astype(F32)))).astype(out_refs[0].dtype)

    return _matmul(name, dy, w_down, "nt", tm, tn, tk, [(_sds((S, F), MXU_DTYPE), _ij_spec(tm, tn))], epilogue,
                   extras=[(a, _ij_spec(tm, tn))])[0]


def _mm_nt_res(name, dy, w, du, tm, tn, tk):
    S = dy.shape[0]
    N = w.shape[0]

    def epilogue(acc, extra_refs, out_refs, j):
        out_refs[0][...] = ALPHA * extra_refs[0][...] + acc

    return _matmul(name, dy, w, "nt", tm, tn, tk, [(_sds((S, N), F32), _ij_spec(tm, tn))], epilogue,
                   extras=[(du, _ij_spec(tm, tn))])[0]


def _attn_out_bwd(du, w_out, o, sel_t, tm, tk):
    S, D = o.shape

    def epilogue(acc, extra_refs, out_refs, j):
        out_refs[0][...] = acc
        out_refs[1][...] = _exact_nn(acc * extra_refs[0][...], extra_refs[1][...])

    row = pl.BlockSpec((tm, D), lambda i, j, k: (i, 0))
    slim = pl.BlockSpec((tm, LANES), lambda i, j, k: (i, 0))
    return _matmul("attn_out_bwd", du, w_out, "nt", tm, D, tk,
                   [(_sds((S, D), F32), row), (_sds((S, LANES), F32), slim)], epilogue,
                   extras=[(o, row), (sel_t, pl.BlockSpec((D, LANES), lambda i, j, k: (0, 0)))])


def _adamw(name, w, g, m, v):
    shape = w.shape
    cols = shape[-1]
    rows = math.prod(shape[:-1])
    w2, g2, m2, v2 = (t.reshape(rows, cols) for t in (w, g, m, v))
    tr = _pick(rows, (256, 128, 64, 32, 16, 8))
    c1 = 1.0 - ADAM_B1 ** ADAM_STEP
    c2 = 1.0 - ADAM_B2 ** ADAM_STEP

    def body(w_ref, g_ref, m_ref, v_ref, d_ref, nm_ref, nv_ref):
        gg = g_ref[...]
        nm = ADAM_B1 * m_ref[...] + (1.0 - ADAM_B1) * gg
        nv = ADAM_B2 * v_ref[...] + (1.0 - ADAM_B2) * (gg * gg)
        nm_ref[...] = nm
        nv_ref[...] = nv
        d_ref[...] = -ADAM_LR * ((nm / c1) / (jnp.sqrt(nv / c2) + ADAM_EPS) + ADAM_WD * w_ref[...])

    blk = pl.BlockSpec((tr, cols), lambda i: (i, 0))
    outs = pl.pallas_call(
        body,
        name=name,
        grid=(rows // tr,),
        in_specs=[blk] * 4,
        out_specs=[blk] * 3,
        out_shape=[_sds((rows, cols), F32)] * 3,
        compiler_params=_cparams(("parallel",)),
    )(w2, g2, m2, v2)
    return tuple(o.reshape(shape) for o in outs)


HBM = pl.BlockSpec(memory_space=pl.ANY)


def _shard_slice(ref, axis, size, index):
    idx = [slice(None)] * len(ref.shape)
    idx[axis] = pl.ds(pl.multiple_of(index * size, 8), size)
    return ref.at[tuple(idx)]


IN_HBM = pl.BlockSpec(memory_space=pltpu.HBM)
IN_SEM = pl.BlockSpec(memory_space=pltpu.SEMAPHORE)
DATAFLOW = pltpu.SideEffectType.DATAFLOW_SIDE_EFFECTING


def _hbm(t):
    return pltpu.with_memory_space_constraint(t, pltpu.HBM)


def _token_spec():
    return pl.BlockSpec(memory_space=pltpu.VMEM)


def _gather_copies(s_refs, f_refs, axes, send, recv, loc, arrival):
    x, y, c = lax.axis_index("x"), lax.axis_index("y"), lax.axis_index("c")
    chips = [(1 - x, y), (x, 1 - y), (1 - x, 1 - y)]
    local, remote = [], []
    for a in range(len(s_refs)):
        size = s_refs[a].shape[axes[a]]
        local.append(pltpu.make_async_copy(s_refs[a], _shard_slice(f_refs[a], axes[a], size, 2 * x + y), loc.at[a]))
        for k, (px, py) in enumerate(chips):
            block = (2 * px + py) if arrival else (2 * x + y)
            remote.append(pltpu.make_async_remote_copy(
                src_ref=s_refs[a], dst_ref=_shard_slice(f_refs[a], axes[a], size, block), send_sem=send.at[3 * a + k],
                recv_sem=recv.at[3 * a + k], device_id=(px, py, c), device_id_type=MESH))
    return local, remote


def _gather_start(name, shards, axes, after):
    n = len(shards)
    fulls = []
    for s, ax in zip(shards, axes):
        fs = list(s.shape)
        fs[ax] *= 4
        fulls.append(lax.empty(tuple(fs), s.dtype))

    def body(*refs):
        s_refs, f_refs = refs[:n], refs[n:2 * n]
        send, recv, loc, token = refs[2 * n + 1], refs[2 * n + 2], refs[2 * n + 3], refs[-1]
        local, remote = _gather_copies(s_refs, f_refs, axes, send, recv, loc, arrival=False)
        for cp in remote + local:
            cp.start()
        token[...] = jnp.zeros_like(token)

    outs = pl.pallas_call(
        body,
        name=name,
        out_shape=(pltpu.SemaphoreType.DMA((3 * n,)), pltpu.SemaphoreType.DMA((3 * n,)), pltpu.SemaphoreType.DMA((n,)),
                   *[pltpu.HBM(t.shape, t.dtype) for t in shards + fulls], _sds((8, LANES), F32)),
        in_specs=[IN_HBM] * (2 * n) + [HBM],
        out_specs=(IN_SEM, IN_SEM, IN_SEM, *[IN_HBM] * (2 * n), _token_spec()),
        input_output_aliases={i: 3 + i for i in range(2 * n)},
        compiler_params=pltpu.CompilerParams(has_side_effects=DATAFLOW),
    )(*[_hbm(t) for t in shards + fulls], after)
    return (outs[0], outs[1], outs[2], list(outs[3:3 + n]), list(outs[3 + n:3 + 2 * n]), axes), outs[-1]


def _gather_wait(name, state, *after):
    send, recv, loc, s_thru, f_thru, axes = state
    n = len(s_thru)

    def body(*refs):
        s_refs, f_refs = refs[:n], refs[n:2 * n]
        local, remote = _gather_copies(s_refs, f_refs, axes, refs[2 * n], refs[2 * n + 1], refs[2 * n + 2], arrival=True)
        for cp in local:
            cp.wait()
        for cp in remote:
            cp.wait_send()
            cp.wait_recv()

    outs = pl.pallas_call(
        body,
        name=name,
        out_shape=tuple(pltpu.HBM(t.shape, t.dtype) for t in s_thru + f_thru),
        in_specs=[IN_HBM] * (2 * n) + [IN_SEM, IN_SEM, IN_SEM] + [HBM] * len(after),
        out_specs=tuple([IN_HBM] * (2 * n)),
        input_output_aliases={i: i for i in range(2 * n)},
        compiler_params=pltpu.CompilerParams(has_side_effects=DATAFLOW),
    )(*s_thru, *f_thru, send, recv, loc, *after)
    return list(outs[n:2 * n])


FLIPS = [(fx, fy, fc) for fx in (0, 1) for fy in (0, 1) for fc in (0, 1)][1:]


def _piece_shape(shape, axis):
    ps = list(shape)
    if axis == 0:
        ps[0] //= 8
    else:
        ps[0] //= 2
        ps[axis] //= 4
    return tuple(ps)


def _piece(ref, axis, q, c):
    shape = ref.shape
    idx = [slice(None)] * len(shape)
    if axis == 0:
        h = shape[0] // 8
        idx[0] = pl.ds(pl.multiple_of((2 * q + c) * h, 8), h)
    else:
        h, w = shape[0] // 2, shape[axis] // 4
        idx[0] = pl.ds(c * h, h)
        idx[axis] = pl.ds(pl.multiple_of(q * w, LANES if axis == len(shape) - 1 else 8), w)
    return ref.at[tuple(idx)]


def _own_piece(g, axis):
    ps = _piece_shape(g.shape, axis)
    q, c = 2 * lax.axis_index("x") + lax.axis_index("y"), lax.axis_index("c")
    start = [0] * len(ps)
    if axis == 0:
        start[0] = (2 * q + c) * ps[0]
    else:
        start[0] = c * ps[0]
        start[axis] = q * ps[axis]
    return lax.dynamic_slice(g, start, ps)


def _scatter_copies(g_refs, l_refs, axes, send, recv):
    x, y, c = lax.axis_index("x"), lax.axis_index("y"), lax.axis_index("c")
    out = []
    for a in range(len(g_refs)):
        for k, (fx, fy, fc) in enumerate(FLIPS):
            tx, ty, tc = x ^ fx, y ^ fy, c ^ fc
            out.append(pltpu.make_async_remote_copy(
                src_ref=_piece(g_refs[a], axes[a], 2 * tx + ty, tc), dst_ref=l_refs[a].at[k],
                send_sem=send.at[7 * a + k], recv_sem=recv.at[7 * a + k], device_id=(tx, ty, tc), device_id_type=MESH))
    return out


def _scatter_start(name, grads, axes):
    n = len(grads)
    lands = [lax.empty((7,) + _piece_shape(g.shape, ax), g.dtype) for g, ax in zip(grads, axes)]

    def body(*refs):
        g_refs, l_refs = refs[:n], refs[n:2 * n]
        send, recv, token = refs[2 * n], refs[2 * n + 1], refs[-1]
        for cp in _scatter_copies(g_refs, l_refs, axes, send, recv):
            cp.start()
        token[...] = jnp.zeros_like(token)

    outs = pl.pallas_call(
        body,
        name=name,
        out_shape=(pltpu.SemaphoreType.DMA((7 * n,)), pltpu.SemaphoreType.DMA((7 * n,)),
                   *[pltpu.HBM(t.shape, t.dtype) for t in grads + lands], _sds((8, LANES), F32)),
        in_specs=[IN_HBM] * (2 * n),
        out_specs=(IN_SEM, IN_SEM, *[IN_HBM] * (2 * n), _token_spec()),
        input_output_aliases={i: 2 + i for i in range(2 * n)},
        compiler_params=pltpu.CompilerParams(has_side_effects=DATAFLOW),
    )(*[_hbm(t) for t in grads + lands])
    return (outs[0], outs[1], list(outs[2:2 + n]), list(outs[2 + n:2 + 2 * n]), axes), outs[-1]


def _scatter_wait(name, state, after):
    send, recv, g_thru, l_thru, axes = state
    n = len(g_thru)

    def body(*refs):
        g_refs, l_refs = refs[:n], refs[n:2 * n]
        for cp in _scatter_copies(g_refs, l_refs, axes, refs[2 * n], refs[2 * n + 1]):
            cp.wait_send()
            cp.wait_recv()

    outs = pl.pallas_call(
        body,
        name=name,
        out_shape=tuple(pltpu.HBM(t.shape, t.dtype) for t in g_thru + l_thru),
        in_specs=[IN_HBM] * (2 * n) + [IN_SEM, IN_SEM, HBM],
        out_specs=tuple([IN_HBM] * (2 * n)),
        input_output_aliases={i: i for i in range(2 * n)},
        compiler_params=pltpu.CompilerParams(has_side_effects=DATAFLOW),
    )(*g_thru, *l_thru, send, recv, after)
    return list(outs[:n]), list(outs[n:2 * n])


def _reduce_join(name, landing, own):
    piece = own.shape
    C = piece[-1]
    R = math.prod(piece[:-1])
    l3 = landing.reshape(7, R, C)
    own2 = own.reshape(R, C)
    tr = _pick(R, [t for t in (512, 256, 128, 64, 32, 16, 8) if t * C <= 256 * 1024])
    nsteps = R // tr

    def body(own_ref, l_ref, o_ref, buf, send, loc, recv):
        i = pl.program_id(0)
        x, y, c = lax.axis_index("x"), lax.axis_index("y"), lax.axis_index("c")
        sibling = (x, y, 1 - c)

        def copies(slot, step):
            dst = o_ref.at[pl.ds(pl.multiple_of(c * R + step * tr, 8), tr), :]
            return (pltpu.make_async_copy(buf.at[slot], dst, loc.at[slot]),
                    pltpu.make_async_remote_copy(src_ref=buf.at[slot], dst_ref=dst, send_sem=send.at[slot], recv_sem=recv,
                                                 device_id=sibling, device_id_type=MESH))

        @pl.when(i >= 2)
        def _():
            lc, rc = copies(i % 2, i - 2)
            lc.wait()
            rc.wait_send()

        acc = own_ref[...].astype(F32)
        for s in range(7):
            acc = acc + l_ref[s].astype(F32)
        buf[i % 2] = acc
        lc, rc = copies(i % 2, i)
        lc.start()
        rc.start()

        @pl.when(i == nsteps - 1)
        def _():
            for st in range(max(nsteps - 2, 0), nsteps):
                lc, rc = copies(st % 2, st)
                lc.wait()
                rc.wait_send()
            theirs = o_ref.at[pl.ds(pl.multiple_of((1 - c) * R, 8), R), :]
            pltpu.make_async_remote_copy(src_ref=theirs, dst_ref=theirs, send_sem=send.at[0], recv_sem=recv,
                                         device_id=sibling, device_id_type=MESH).wait_recv()

    return pl.pallas_call(
        body,
        name=name,
        grid=(nsteps,),
        in_specs=[pl.BlockSpec((tr, C), lambda i: (i, 0)), pl.BlockSpec((7, tr, C), lambda i: (0, i, 0))],
        out_specs=HBM,
        out_shape=_sds((2 * R, C), F32),
        scratch_shapes=[pltpu.VMEM((2, tr, C), F32), pltpu.SemaphoreType.DMA((2,)), pltpu.SemaphoreType.DMA((2,)),
                        pltpu.SemaphoreType.DMA(())],
        compiler_params=_cparams(("arbitrary",)),
    )(own2, l3)


def _all_reduce_small(v, dep):
    R, D = v.shape

    def body(v_ref, dep_ref, o_ref, land, send, recv):
        x, y, c = lax.axis_index("x"), lax.axis_index("y"), lax.axis_index("c")
        my_slot = 4 * x + 2 * y + c
        land[my_slot] = v_ref[...]
        for k, (fx, fy, fc) in enumerate(FLIPS):
            tx, ty, tc = x ^ fx, y ^ fy, c ^ fc
            pltpu.make_async_remote_copy(src_ref=v_ref, dst_ref=land.at[my_slot], send_sem=send.at[k], recv_sem=recv.at[k],
                                         device_id=(tx, ty, tc), device_id_type=MESH).start()
        for k, (fx, fy, fc) in enumerate(FLIPS):
            tx, ty, tc = x ^ fx, y ^ fy, c ^ fc
            cp = pltpu.make_async_remote_copy(src_ref=v_ref, dst_ref=land.at[4 * tx + 2 * ty + tc], send_sem=send.at[k],
                                              recv_sem=recv.at[k], device_id=(tx, ty, tc), device_id_type=MESH)
            cp.wait_send()
            cp.wait_recv()
        acc = land[0]
        for s in range(1, 8):
            acc = acc + land[s]
        o_ref[...] = acc

    return pl.pallas_call(
        body,
        name="all_reduce_small",
        in_specs=[pl.BlockSpec(memory_space=pltpu.VMEM), pl.BlockSpec(memory_space=pl.ANY)],
        out_specs=pl.BlockSpec(memory_space=pltpu.VMEM),
        out_shape=_sds((R, D), F32),
        scratch_shapes=[pltpu.VMEM((8, R, D), F32), pltpu.SemaphoreType.DMA((7,)), pltpu.SemaphoreType.DMA((7,))],
    )(v, dep)


def kernel(x, attn_w_in, attn_w_out, hgrn_w_in, hgrn_w_out, hgrn_norm_g, lb_logits, ln_mix_g, ln_mix_b, ln_ffn_g, ln_ffn_b, ffn_w_up, ffn_w_down, loss_target, m_attn_w_in, m_attn_w_out, m_hgrn_w_in, m_hgrn_w_out, m_hgrn_norm_g, m_lb_logits, m_ln_mix_g, m_ln_mix_b, m_ln_ffn_g, m_ln_ffn_b, m_ffn_w_up, m_ffn_w_down, v_attn_w_in, v_attn_w_out, v_hgrn_w_in, v_hgrn_w_out, v_hgrn_norm_g, v_lb_logits, v_ln_mix_g, v_ln_mix_b, v_ln_ffn_g, v_ln_ffn_b, v_ffn_w_up, v_ffn_w_down):
    xs = x[0]
    tgt = loss_target[0]
    S, D = xs.shape
    F = ffn_w_up.shape[2] * 4
    T1 = _pick(S, (1024, 512, 256))
    T2 = _pick(S, (2048, 1024, 512))
    TH = _pick(S, (512, 256))
    TB = _pick(S, (512, 256))
    TN = _pick(D, (512, 256, 128))
    TF = _pick(F, (1024, 512))
    TG = _pick(3 * D, (1536, 1024, 768))
    TW = _pick(F, (2048, 1024))

    cast = lambda w: w.astype(MXU_DTYPE)
    st_a, tok = _gather_start("gather_a", [cast(attn_w_in[0])], [1], jnp.zeros((8, LANES), F32))
    tok, (xs_late, w_aout, w_fup, w_fdown, w_hin, w_hout) = lax.optimization_barrier(
        (tok, (xs, attn_w_out, ffn_w_up, ffn_w_down, hgrn_w_in, hgrn_w_out)))
    st_b, tok = _gather_start("gather_b", [cast(w_aout[0]), cast(w_fup[0]), cast(w_fdown[0])], [0, 1, 0], tok)
    st_c, tok = _gather_start("gather_c", [cast(w_hin[0]), cast(w_hout[0]), hgrn_norm_g, cast(w_fup[1]), cast(w_fdown[1])],
                              [1, 0, 1, 1, 0], tok)

    cos3, sin3 = _rope_tables(S)
    sel = _head_sel(D)
    sel_t = sel.T

    xc3 = _stack_classes("x_classes", xs_late, MXU_DTYPE)
    (wa_in,) = _gather_wait("gather_a_wait", st_a, tok, xc3, cos3, sin3)
    P3 = _attn_proj(xc3, wa_in, cos3, sin3, T2, TN)
    o3, lse3 = _attn_fwd(P3, D)
    o_att, L_att = _attn_mix(o3, lse3, sel)
    wa_out, w_up0, w_down0 = _gather_wait("gather_b_wait", st_b, L_att)
    x1, xh1, r1 = _mm_res_ln("attn_out_ln", o_att, wa_out, xs, ln_mix_g[0:1], ln_mix_b[0:1], TH, D)
    a0 = _mlp_up("mlp0_up", x1, w_up0, T1, TF, D)
    x2, xh2, r2 = _mm_res_ln("mlp0_down_ln", a0, w_down0, x1, ln_ffn_g[0:1], ln_ffn_b[0:1], TH, F)

    wh_in, wh_out, norm_g, w_up1, w_down1 = _gather_wait("gather_c_wait", st_c, r2)
    P1 = _plain_mm("hgrn_proj", x2, wh_in, "nn", F32, T1, _pick(3 * D, (1024, 768, 512)), D)
    o_h, n_h, states = _hgrn_fwd(P1, lb_logits, norm_g, TB)
    x3, xh3, r3 = _mm_res_ln("hgrn_out_ln", n_h, wh_out, x2, ln_mix_g[1:2], ln_mix_b[1:2], TH, D)
    a1 = _mlp_up("mlp1_up", x3, w_up1, T1, TF, D)
    x4, xh4, r4 = _mm_res_ln("mlp1_down_ln", a1, w_down1, x3, ln_ffn_g[1:2], ln_ffn_b[1:2], TH, F)

    sq, dx4 = _loss_head(x4, tgt, TH)

    wgrad = lambda name, a, dy, tm, tn: _plain_mm(name, a, dy, "tn", MXU_DTYPE, tm, tn, T1)
    du4, dg_ffn1, db_ffn1 = _ln_bwd("ln_ffn1_bwd", dx4, xh4, r4, ln_ffn_g[1:2], TH, sq)
    dh1 = _mlp_down_bwd("mlp1_down_bwd", du4, w_down1, a1, T1, TF, D)
    g_down1 = wgrad("g_down1", a1, du4, TW, D)
    dx3 = _mm_nt_res("mlp1_up_bwd", dh1, w_up1, du4, TH, D, F)
    g_up1 = wgrad("g_up1", x3, dh1, D, TW)
    sc_1, tok = _scatter_start("scatter_1", [g_down1, g_up1], [0, 1])
    du3, dg_mix1, db_mix1 = _ln_bwd("ln_mix1_bwd", dx3, xh3, r3, ln_mix_g[1:2], TH, tok)
    dn = _plain_mm("hgrn_out_bwd", du3, wh_out, "nt", F32, T1, D, D)
    g_hout = wgrad("g_hgrn_out", n_h, du3, D, D)
    dq_raw, dz, dv, dg_norm, dlb = _hgrn_bwd(P1, o_h, states, dn, lb_logits, norm_g, TB)
    dP1 = jnp.concatenate([dq_raw, dz, dv], axis=1)
    dx2 = _mm_nt_res("hgrn_in_bwd", dP1, wh_in, du3, TH, D, 3 * D)
    g_hin = wgrad("g_hgrn_in", x2, dP1, D, TG)
    d_lb_logits = _lb_logits_grad(dlb, lb_logits)
    sc_2, tok = _scatter_start("scatter_2", [g_hout, g_hin], [0, 1])

    du2, dg_ffn0, db_ffn0 = _ln_bwd("ln_ffn0_bwd", dx2, xh2, r2, ln_ffn_g[0:1], TH, tok)
    dh0 = _mlp_down_bwd("mlp0_down_bwd", du2, w_down0, a0, T1, TF, D)
    g_down0 = wgrad("g_down0", a0, du2, TW, D)
    dx1 = _mm_nt_res("mlp0_up_bwd", dh0, w_up0, du2, TH, D, F)
    g_up0 = wgrad("g_up0", x1, dh0, D, TW)
    sc_3, tok = _scatter_start("scatter_3", [g_down0, g_up0], [0, 1])
    du1, dg_mix0, db_mix0 = _ln_bwd("ln_mix0_bwd", dx1, xh1, r1, ln_mix_g[0:1], TH, tok)
    do, delta = _attn_out_bwd(du1, wa_out, o_att, sel_t, TH, D)
    g_aout = wgrad("g_attn_out", o_att, du1, D, D)
    dP3 = _attn_bwd(P3, _stack_classes("do_classes", do, MXU_DTYPE), _stack_classes("lse_classes", L_att, F32),
                    _stack_classes("delta_classes", delta, F32), cos3, sin3, D)
    small = jnp.concatenate([d_lb_logits, dg_mix0, dg_mix1, db_mix0, db_mix1, dg_ffn0, dg_ffn1, db_ffn0, db_ffn1,
                             dg_norm, sq, jnp.zeros((4, D), F32)], axis=0)
    small = _all_reduce_small(small, dP3)
    loss = 0.5 * jnp.sum(small[11]) / D
    grp = lambda j: j // (3 * D // TG)
    g_ain = _matmul("g_attn_in", xc3, dP3, "tn", D, TG, T1, [(_sds((D, 9 * D), MXU_DTYPE), _ij_spec(D, TG))], _store_epilogue,
                    a_map=lambda i, j, k: (k + grp(j) * (S // T1), i),
                    b_map=lambda i, j, k: (k + grp(j) * (S // T1), j % (3 * D // TG)), mnk=(D, 9 * D, S), dep=small)[0]
    sc_4, tok = _scatter_start("scatter_4", [g_aout, g_ain], [0, 1])
    dxc3 = _matmul("attn_in_bwd", dP3, wa_in, "nt", TH, D, 3 * D, [(_sds((3 * S, D), F32), _ij_spec(TH, D))], _store_epilogue,
                   b_map=lambda i, j, k: (j, k + i // (S // TH)), mnk=(3 * S, D, 3 * D), dep=tok)[0]
    grad_x = _input_grad(du1, dxc3)

    def reduced(name, state, after):
        gs, lands = _scatter_wait(name + "_wait", state, after)
        return [_reduce_join(f"{name}_reduce_{i}", l, _own_piece(g, ax)) for i, (l, g, ax) in enumerate(zip(lands, gs, state[4]))]

    r_down1, r_up1 = reduced("scatter_1", sc_1, grad_x)
    r_hout, r_hin = reduced("scatter_2", sc_2, r_up1)
    r_down0, r_up0 = reduced("scatter_3", sc_3, r_hin)

    my_chip = 2 * lax.axis_index("x") + lax.axis_index("y")
    nsh = hgrn_norm_g.shape[1]
    g_norm = lax.dynamic_slice(small[10:11], (0, my_chip * nsh), (1, nsh))

    grads, upd = {}, {}

    def update(nm, w, gr, m, v):
        grads[nm] = gr.reshape(w.shape)
        upd[nm] = _adamw("adamw_" + nm, w, grads[nm], m, v)

    update("hgrn_w_in", hgrn_w_in, r_hin, m_hgrn_w_in, v_hgrn_w_in)
    update("hgrn_w_out", hgrn_w_out, r_hout, m_hgrn_w_out, v_hgrn_w_out)
    update("ffn_w_up", ffn_w_up, jnp.stack([r_up0, r_up1]), m_ffn_w_up, v_ffn_w_up)
    update("ffn_w_down", ffn_w_down, jnp.stack([r_down0, r_down1]), m_ffn_w_down, v_ffn_w_down)
    r_aout, r_ain = reduced("scatter_4", sc_4, upd["ffn_w_down"][2])
    update("attn_w_in", attn_w_in, r_ain, m_attn_w_in, v_attn_w_in)
    update("attn_w_out", attn_w_out, r_aout, m_attn_w_out, v_attn_w_out)
    grads["hgrn_norm_g"] = g_norm
    upd["hgrn_norm_g"] = _adamw("adamw_hgrn_norm_g", hgrn_norm_g, g_norm, m_hgrn_norm_g, v_hgrn_norm_g)
    cat = lambda ts: jnp.concatenate(ts, axis=0)
    small_w = cat([lb_logits, ln_mix_g, ln_mix_b, ln_ffn_g, ln_ffn_b])
    small_m = cat([m_lb_logits, m_ln_mix_g, m_ln_mix_b, m_ln_ffn_g, m_ln_ffn_b])
    small_v = cat([v_lb_logits, v_ln_mix_g, v_ln_mix_b, v_ln_ffn_g, v_ln_ffn_b])
    small_upd = _adamw("adamw_small", small_w, small[0:10], small_m, small_v)
    for i, nm in enumerate(["lb_logits", "ln_mix_g", "ln_mix_b", "ln_ffn_g", "ln_ffn_b"]):
        grads[nm] = small[2 * i:2 * i + 2]
        upd[nm] = tuple(t[2 * i:2 * i + 2] for t in small_upd)

    order = ["attn_w_in", "attn_w_out", "hgrn_w_in", "hgrn_w_out", "hgrn_norm_g", "lb_logits", "ln_mix_g", "ln_mix_b",
             "ln_ffn_g", "ln_ffn_b", "ffn_w_up", "ffn_w_down"]
    return (loss, grad_x[None], *[grads[k] for k in order], *[upd[k][0] for k in order],
            *[upd[k][1] for k in order], *[upd[k][2] for k in order])
```

```python
import math

import jax
import jax.numpy as jnp
from jax import lax
from jax.experimental import pallas as pl
from jax.experimental.pallas import tpu as pltpu

F32 = jnp.float32
BF16 = jnp.bfloat16
MXU_DTYPE = BF16

HEAD_DIM = 64
ATTN_BLK = 128
DILATIONS = (1, 4, 16)
ROPE_THETA = 10000.0
HGRN_DK = 128
HGRN_CHUNK = 64
DEPTH = 2
LN_EPS = 1e-5
RMS_EPS = 1e-6
ALPHA = (2 * DEPTH) ** 0.25
ADAM_LR, ADAM_B1, ADAM_B2, ADAM_EPS, ADAM_WD, ADAM_STEP = 0.001, 0.9, 0.999, 1e-08, 0.01, 10

LANES = 128
VMEM_LIMIT = 56 * 1024 * 1024
NEG = -1e30
MESH = pl.DeviceIdType.MESH


def _cparams(sem=None):
    return pltpu.CompilerParams(dimension_semantics=sem, vmem_limit_bytes=VMEM_LIMIT)


def _sds(shape, dtype):
    return jax.ShapeDtypeStruct(tuple(shape), dtype)


def _dg(a, b, ca, cb):
    return lax.dot_general(a, b, (((ca,), (cb,)), ((), ())), preferred_element_type=F32)


def _nn(a, b):
    return _dg(a, b, 1, 0)


def _nt(a, b):
    return _dg(a, b, 1, 1)


def _tn(a, b):
    return _dg(a, b, 0, 0)


def _split3(a):
    hi = a.astype(BF16)
    r = a - hi.astype(F32)
    mid = r.astype(BF16)
    lo = (r - mid.astype(F32)).astype(BF16)
    return hi, mid, lo


def _exact_nn(a, sel):
    hi, mid, lo = _split3(a)
    return _nn(hi, sel) + _nn(mid, sel) + _nn(lo, sel)


def _pick(n, prefs):
    for p in prefs:
        if n % p == 0:
            return p
    return n


def _matmul(name, a, b, form, tm, tn, tk, outs, epilogue, extras=(), a_map=None, b_map=None, mnk=None, dep=None):
    if form == "nn":
        (M, K), N = a.shape, b.shape[1]
        a_spec = pl.BlockSpec((tm, tk), a_map or (lambda i, j, k: (i, k)))
        b_spec = pl.BlockSpec((tk, tn), b_map or (lambda i, j, k: (k, j)))
        ca, cb = 1, 0
    elif form == "nt":
        (M, K), N = a.shape, b.shape[0]
        a_spec = pl.BlockSpec((tm, tk), a_map or (lambda i, j, k: (i, k)))
        b_spec = pl.BlockSpec((tn, tk), b_map or (lambda i, j, k: (j, k)))
        ca, cb = 1, 1
    else:
        (K, M), N = a.shape, b.shape[1]
        a_spec = pl.BlockSpec((tk, tm), a_map or (lambda i, j, k: (k, i)))
        b_spec = pl.BlockSpec((tk, tn), b_map or (lambda i, j, k: (k, j)))
        ca, cb = 0, 0
    if mnk is not None:
        M, N, K = mnk
    assert M % tm == 0 and N % tn == 0 and K % tk == 0, (name, M, N, K, tm, tn, tk)
    nk = K // tk
    ne, no = len(extras), len(outs)
    deps = [] if dep is None else [dep]
    nd = len(deps)

    def body(a_ref, b_ref, *rest):
        extra_refs, out_refs = rest[:ne], rest[ne + nd:ne + nd + no]
        j = pl.program_id(1)
        part = _dg(a_ref[...].astype(MXU_DTYPE), b_ref[...].astype(MXU_DTYPE), ca, cb)
        if nk == 1:
            epilogue(part, extra_refs, out_refs, j)
            return
        acc_ref = rest[-1]
        k = pl.program_id(2)

        @pl.when(k == 0)
        def _():
            acc_ref[...] = part

        @pl.when(k > 0)
        def _():
            acc_ref[...] += part

        @pl.when(k == nk - 1)
        def _():
            epilogue(acc_ref[...], extra_refs, out_refs, j)

    res = pl.pallas_call(
        body,
        name=name,
        grid=(M // tm, N // tn, nk),
        in_specs=[a_spec, b_spec] + [s for _, s in extras] + [pl.BlockSpec(memory_space=pl.ANY)] * nd,
        out_specs=[s for _, s in outs],
        out_shape=[o for o, _ in outs],
        scratch_shapes=[pltpu.VMEM((tm, tn), F32)] if nk > 1 else [],
        compiler_params=_cparams(("parallel", "parallel", "arbitrary")),
    )(a, b, *[e for e, _ in extras], *deps)
    return res


def _ij_spec(tm, tn):
    return pl.BlockSpec((tm, tn), lambda i, j, k: (i, j))


def _store_epilogue(acc, extra_refs, out_refs, j):
    out_refs[0][...] = acc.astype(out_refs[0].dtype)


def _plain_mm(name, a, b, form, out_dtype, tm, tn, tk):
    M = a.shape[1] if form == "tn" else a.shape[0]
    N = b.shape[0] if form == "nt" else b.shape[1]
    return _matmul(name, a, b, form, tm, tn, tk, [(_sds((M, N), out_dtype), _ij_spec(tm, tn))], _store_epilogue)[0]


def _class_slabs(S):
    assert DILATIONS[0] == 1
    return [(g, d, r, S // d) for g, d in enumerate(DILATIONS) if d > 1 for r in range(d)]


def _stack_classes(name, t, out_dtype):
    S, W = t.shape

    def body(x_ref, o_ref):
        o_ref[0:S, :] = x_ref[...].astype(out_dtype)
        for g, d, r, n in _class_slabs(S):
            o_ref[g * S + r * n:g * S + (r + 1) * n, :] = x_ref[pl.ds(r, n, stride=d), :].astype(out_dtype)

    return pl.pallas_call(
        body,
        name=name,
        grid=(W // LANES,),
        in_specs=[pl.BlockSpec((S, LANES), lambda j: (0, j))],
        out_specs=pl.BlockSpec((3 * S, LANES), lambda j: (0, j)),
        out_shape=_sds((3 * S, W), out_dtype),
        compiler_params=_cparams(("parallel",)),
    )(t)


def _rope_tables(seq):
    half = HEAD_DIM // 2
    inv = ROPE_THETA ** (-jnp.arange(half, dtype=F32) * (2.0 / HEAD_DIM))
    inv = jnp.tile(inv, LANES // half)
    pos = []
    for d in DILATIONS:
        row = jnp.arange(seq)
        pos.append((row % (seq // d)) * d + row // (seq // d))
    ang = jnp.concatenate(pos).astype(F32)[:, None] * inv[None, :]
    first = (jnp.arange(LANES) % HEAD_DIM) < half
    sin = jnp.sin(ang)
    return jnp.cos(ang), jnp.where(first[None, :], -sin, sin)


def _partner(x):
    half = HEAD_DIM // 2
    lane = lax.broadcasted_iota(jnp.int32, x.shape, 1)
    first = (lane % HEAD_DIM) < half
    return jnp.where(first, pltpu.roll(x, LANES - half, 1), pltpu.roll(x, half, 1))


def _attn_proj(x3, w_full, cos3, sin3, tm, tn):
    S3, D = x3.shape
    S = S3 // 3
    per_part = D // tn
    per_group = 3 * per_part

    def epilogue(acc, extra_refs, out_refs, j):
        cos_ref, sin_ref = extra_refs
        o_ref = out_refs[0]
        is_rot = j // per_part < 2

        @pl.when(is_rot)
        def _():
            c, s = cos_ref[...], sin_ref[...]
            for t in range(tn // LANES):
                xs = acc[:, t * LANES:(t + 1) * LANES]
                o_ref[:, t * LANES:(t + 1) * LANES] = (xs * c + _partner(xs) * s).astype(o_ref.dtype)

        @pl.when(jnp.logical_not(is_rot))
        def _():
            o_ref[...] = acc.astype(o_ref.dtype)

    tab = pl.BlockSpec((tm, LANES), lambda i, j, k: (i, 0))
    return _matmul("attn_proj", x3, w_full, "nn", tm, tn, D, [(_sds((S3, 3 * D), MXU_DTYPE), _ij_spec(tm, tn))],
                   epilogue, extras=[(cos3, tab), (sin3, tab)],
                   b_map=lambda i, j, k: (k, j + (i // (S // tm)) * per_group), mnk=(S3, 3 * D, D))[0]


def _head_sel(d_model):
    h = jnp.arange(LANES)[:, None]
    l = jnp.arange(d_model)[None, :]
    return (l // HEAD_DIM == h).astype(BF16)


def _class_edges(b, nblk):
    g = b // nblk
    per_class = jnp.where(g == 0, nblk // DILATIONS[0], jnp.where(g == 1, nblk // DILATIONS[1], nblk // DILATIONS[2]))
    pos = (b % nblk) % per_class
    return pos != 0, pos != per_class - 1


def _two_heads(t, top):
    zero = jnp.zeros_like(t)
    return jnp.concatenate([jnp.where(top, t, zero), jnp.where(top, zero, t)], axis=0)


def _band_mask(has_prev):
    B = ATTN_BLK
    row = lax.broadcasted_iota(jnp.int32, (2 * B, 2 * B), 0) % B
    col = lax.broadcasted_iota(jnp.int32, (2 * B, 2 * B), 1)
    in_prev = jnp.logical_and(jnp.logical_and(col < B, col >= row), has_prev)
    in_own = jnp.logical_and(col >= B, col - B <= row)
    return jnp.logical_or(in_prev, in_own)


def _attn_fwd(P3, D):
    S3 = P3.shape[0]
    B = ATTN_BLK
    nblk = S3 // 3 // B
    npairs = D // LANES
    scale = HEAD_DIM ** -0.5

    def body(q_ref, kc_ref, vc_ref, kp_ref, vp_ref, o_ref, lse_ref):
        has_prev, _ = _class_edges(pl.program_id(0), nblk)
        ok = _band_mask(has_prev)
        lane = lax.broadcasted_iota(jnp.int32, (B, LANES), 1)
        top = lane < HEAD_DIM
        lse_acc = jnp.zeros((B, LANES), F32)
        for j in range(npairs):
            sl = slice(j * LANES, (j + 1) * LANES)
            Q = _two_heads(q_ref[:, sl] * scale, top)
            K2 = jnp.concatenate([kp_ref[:, sl], kc_ref[:, sl]], axis=0)
            V2 = jnp.concatenate([vp_ref[:, sl], vc_ref[:, sl]], axis=0)
            s = jnp.where(ok, _nt(Q, K2), NEG)
            m = jnp.max(s, axis=1, keepdims=True)
            p = jnp.exp(s - m)
            l = jnp.sum(p, axis=1, keepdims=True)
            o = _nn((p * (1.0 / l)).astype(MXU_DTYPE), V2)
            o_ref[:, sl] = jnp.where(top, o[:B], o[B:])
            lse = m + jnp.log(l)
            lse_acc = jnp.where(lane == 2 * j, lse[:B], jnp.where(lane == 2 * j + 1, lse[B:], lse_acc))
        lse_ref[...] = lse_acc

    blk = lambda part, prev: pl.BlockSpec(
        (B, D), (lambda b: (jnp.maximum(b - 1, 0), part)) if prev else (lambda b: (b, part)))
    return pl.pallas_call(
        body,
        name="attn_fwd",
        grid=(3 * nblk,),
        in_specs=[blk(0, False), blk(1, False), blk(2, False), blk(1, True), blk(2, True)],
        out_specs=[pl.BlockSpec((B, D), lambda b: (b, 0)), pl.BlockSpec((B, LANES), lambda b: (b, 0))],
        out_shape=[_sds((S3, D), F32), _sds((S3, LANES), F32)],
        compiler_params=_cparams(("parallel",)),
    )(P3, P3, P3, P3, P3)


def _attn_mix(o3, lse3, sel):
    S3, D = o3.shape
    S = S3 // 3

    def body(o3_ref, lse_ref, sel_ref, o_ref, L_ref, w_ref):
        @pl.when(pl.program_id(0) == 0)
        def _():
            w_ref[0] = lse_ref[0:S, :]
            for g, d, r, n in _class_slabs(S):
                w_ref[g, pl.ds(r, n, stride=d), :] = lse_ref[g * S + r * n:g * S + (r + 1) * n, :]
            a, b, c = w_ref[0], w_ref[1], w_ref[2]
            m = jnp.maximum(jnp.maximum(a, b), c)
            L = m + jnp.log(jnp.exp(a - m) + jnp.exp(b - m) + jnp.exp(c - m))
            L_ref[...] = L
            w_ref[0] = jnp.exp(a - L)
            w_ref[1] = jnp.exp(b - L)
            w_ref[2] = jnp.exp(c - L)

        s = sel_ref[...]
        o_ref[...] = _exact_nn(w_ref[0], s) * o3_ref[0:S, :]
        for g, d, r, n in _class_slabs(S):
            rows = pl.ds(r, n, stride=d)
            o_ref[rows, :] += _exact_nn(w_ref[g, rows, :], s) * o3_ref[g * S + r * n:g * S + (r + 1) * n, :]

    return pl.pallas_call(
        body,
        name="attn_mix",
        grid=(D // LANES,),
        in_specs=[pl.BlockSpec((S3, LANES), lambda j: (0, j)), pl.BlockSpec((S3, LANES), lambda j: (0, 0)),
                  pl.BlockSpec((LANES, LANES), lambda j: (0, j))],
        out_specs=[pl.BlockSpec((S, LANES), lambda j: (0, j)), pl.BlockSpec((S, LANES), lambda j: (0, 0))],
        out_shape=[_sds((S, D), F32), _sds((S, LANES), F32)],
        scratch_shapes=[pltpu.VMEM((3, S, LANES), F32)],
        compiler_params=_cparams(("arbitrary",)),
    )(o3, lse3, sel)


def _attn_bwd(P3, do3, L3, delta3, cos3, sin3, D):
    S3 = P3.shape[0]
    B = ATTN_BLK
    nblk = S3 // 3 // B
    npairs = D // LANES
    scale = HEAD_DIM ** -0.5

    def body(c_ref, p_ref, n_ref, doc_ref, don_ref, Lc_ref, Ln_ref, dc_ref, dn_ref, cos_ref, sin_ref, out_ref):
        has_prev, has_next = _class_edges(pl.program_id(0), nblk)
        ok = _band_mask(has_prev)
        row = lax.broadcasted_iota(jnp.int32, (2 * B, B), 0) % B
        col = lax.broadcasted_iota(jnp.int32, (2 * B, B), 1)
        ok_n = jnp.logical_and(col >= row, has_next)
        lane = lax.broadcasted_iota(jnp.int32, (B, LANES), 1)
        top = lane < HEAD_DIM
        cos_t = cos_ref[...]
        sin_inv = -sin_ref[...]
        Lc_all, Ln_all, dc_all, dn_all = Lc_ref[...], Ln_ref[...], dc_ref[...], dn_ref[...]
        pair_col = lambda t, j: jnp.concatenate([t[:, 2 * j:2 * j + 1], t[:, 2 * j + 1:2 * j + 2]], axis=0)
        for j in range(npairs):
            sl = lambda part: slice(part * D + j * LANES, part * D + (j + 1) * LANES)
            kc2, vc2 = c_ref[:, sl(1)], c_ref[:, sl(2)]
            K2 = jnp.concatenate([p_ref[:, sl(1)], kc2], axis=0)
            V2 = jnp.concatenate([p_ref[:, sl(2)], vc2], axis=0)
            Qc = _two_heads(c_ref[:, sl(0)] * scale, top)
            Qn = _two_heads(n_ref[:, sl(0)] * scale, top)
            DOc = _two_heads(doc_ref[:, j * LANES:(j + 1) * LANES].astype(MXU_DTYPE), top)
            DOn = _two_heads(don_ref[:, j * LANES:(j + 1) * LANES].astype(MXU_DTYPE), top)
            P_c = jnp.where(ok, jnp.exp(_nt(Qc, K2) - pair_col(Lc_all, j)), 0.0)
            dS_c = P_c * (_nt(DOc, V2) - pair_col(dc_all, j))
            P_n = jnp.where(ok_n, jnp.exp(_nt(Qn, kc2) - pair_col(Ln_all, j)), 0.0)
            dS_n = P_n * (_nt(DOn, vc2) - pair_col(dn_all, j))
            dq = _nn(dS_c.astype(MXU_DTYPE), K2)
            dq2 = jnp.where(top, dq[:B], dq[B:]) * scale
            Qk = jnp.concatenate([Qc, Qn], axis=0)
            DOk = jnp.concatenate([DOc, DOn], axis=0)
            dk2 = _tn(jnp.concatenate([dS_c[:, B:], dS_n], axis=0).astype(MXU_DTYPE), Qk)
            dv2 = _tn(jnp.concatenate([P_c[:, B:], P_n], axis=0).astype(MXU_DTYPE), DOk)
            out_ref[:, sl(0)] = (dq2 * cos_t + _partner(dq2) * sin_inv).astype(out_ref.dtype)
            out_ref[:, sl(1)] = (dk2 * cos_t + _partner(dk2) * sin_inv).astype(out_ref.dtype)
            out_ref[:, sl(2)] = dv2.astype(out_ref.dtype)

    cur = lambda b: b
    prv = lambda b: jnp.maximum(b - 1, 0)
    nxt = lambda b: jnp.minimum(b + 1, 3 * nblk - 1)
    spec = lambda w, f: pl.BlockSpec((B, w), lambda b: (f(b), 0))
    return pl.pallas_call(
        body,
        name="attn_bwd",
        grid=(3 * nblk,),
        in_specs=[spec(3 * D, cur), spec(3 * D, prv), spec(3 * D, nxt), spec(D, cur), spec(D, nxt),
                  spec(LANES, cur), spec(LANES, nxt), spec(LANES, cur), spec(LANES, nxt), spec(LANES, cur), spec(LANES, cur)],
        out_specs=spec(3 * D, cur),
        out_shape=_sds((S3, 3 * D), MXU_DTYPE),
        compiler_params=_cparams(("parallel",)),
    )(P3, P3, P3, do3, do3, L3, L3, delta3, delta3, cos3, sin3)


def _input_grad(du, dx3):
    S, D = du.shape

    def body(du_ref, dx_ref, o_ref):
        o_ref[...] = ALPHA * du_ref[...] + dx_ref[0:S, :]
        for g, d, r, n in _class_slabs(S):
            o_ref[pl.ds(r, n, stride=d), :] += dx_ref[g * S + r * n:g * S + (r + 1) * n, :]

    return pl.pallas_call(
        body,
        name="input_grad",
        grid=(D // LANES,),
        in_specs=[pl.BlockSpec((S, LANES), lambda j: (0, j)), pl.BlockSpec((3 * S, LANES), lambda j: (0, j))],
        out_specs=pl.BlockSpec((S, LANES), lambda j: (0, j)),
        out_shape=_sds((S, D), F32),
        compiler_params=_cparams(("parallel",)),
    )(du, dx3)


def _chunk_masks(tb):
    r = lax.broadcasted_iota(jnp.int32, (tb, tb), 0)
    c = lax.broadcasted_iota(jnp.int32, (tb, tb), 1)
    same = (r // HGRN_CHUNK) == (c // HGRN_CHUNK)
    return jnp.logical_and(same, r >= c), jnp.logical_and(same, r < c), jnp.logical_and(same, r <= c)


def _exact_sel_parts(sel, parts):
    return _nn(sel, parts[0]) + _nn(sel, parts[1]) + _nn(sel, parts[2])


def _lower_bound(lb_ref):
    l0, l1 = lb_ref[0:1, :], lb_ref[1:2, :]
    m = jnp.maximum(l0, l1)
    e0, e1 = jnp.exp(l0 - m), jnp.exp(l1 - m)
    return e1 / (e0 + e1)


def _hgrn_gates(q_raw, z, lb):
    sg = 1.0 / (1.0 + jnp.exp(-z))
    sn = 1.0 / (1.0 + jnp.exp(z))
    f = lb + (1.0 - lb) * sg
    key = (1.0 - lb) * sn
    sq = 1.0 / (1.0 + jnp.exp(-q_raw))
    return sg, sn, f, key, sq


HGRN_HEADS_PER_STEP = 2


def _hgrn_fwd(P1, lb_logits, norm_g, tb):
    S = P1.shape[0]
    D = P1.shape[1] // 3
    K = HGRN_DK
    H = D // K
    HP = HGRN_HEADS_PER_STEP
    C = HGRN_CHUNK
    cpb = tb // C
    nt = S // tb

    def body(q_ref, f_ref, i_ref, lb_ref, g_ref, o_ref, n_ref, st_ref, state):
        t = pl.program_id(1)

        @pl.when(t == 0)
        def _():
            state[...] = jnp.zeros_like(state)

        lb_all = _lower_bound(lb_ref)
        low, upper_strict, _ = _chunk_masks(tb)
        sum_to, sum_after = low.astype(BF16), upper_strict.astype(BF16)
        for hh in range(HP):
            lanes = slice(hh * K, (hh + 1) * K)
            q_raw, z, v = q_ref[:, lanes], f_ref[:, lanes], i_ref[:, lanes]
            sg, sn, f, key, sq = _hgrn_gates(q_raw, z, lb_all[:, lanes])
            logf = _split3(jnp.log(f))
            b = _exact_sel_parts(sum_to, logf)
            rel = _exact_sel_parts(sum_after, logf)
            qd = (q_raw * sq * jnp.exp(b)).astype(MXU_DTYPE)
            kd = (key * jnp.exp(-b)).astype(MXU_DTYPE)
            kb = (key * jnp.exp(rel)).astype(MXU_DTYPE)
            vm = v.astype(MXU_DTYPE)
            a = jnp.where(low, _nt(qd, kd), 0.0).astype(MXU_DTYPE)
            o_intra = _nn(a, vm)
            st = state[hh]
            outs = []
            for ci in range(cpb):
                rows = slice(ci * C, (ci + 1) * C)
                st_ref[hh, ci] = st
                outs.append(o_intra[rows] + _nt(qd[rows], st.astype(MXU_DTYPE)))
                st = st * jnp.exp(b[(ci + 1) * C - 1:(ci + 1) * C, :]) + _tn(vm[rows], kb[rows])
            state[hh] = st
            o = jnp.concatenate(outs, axis=0)
            o_ref[:, lanes] = o
            rs = lax.rsqrt(jnp.mean(o * o, axis=1, keepdims=True) + RMS_EPS)
            n_ref[:, lanes] = o * rs * g_ref[:, lanes]

    tok = lambda part: pl.BlockSpec((tb, HP * K), lambda h, t: (t, part * (H // HP) + h))
    vec = lambda rows: pl.BlockSpec((rows, HP * K), lambda h, t: (0, h))
    return pl.pallas_call(
        body,
        name="hgrn_fwd",
        grid=(H // HP, nt),
        in_specs=[tok(0), tok(1), tok(2), vec(2), vec(1)],
        out_specs=[tok(0), tok(0), pl.BlockSpec((HP, cpb, K, K), lambda h, t: (h, t, 0, 0))],
        out_shape=[_sds((S, D), F32), _sds((S, D), F32), _sds((H, S // C, K, K), F32)],
        scratch_shapes=[pltpu.VMEM((HP, K, K), F32)],
        compiler_params=_cparams(("parallel", "arbitrary")),
    )(P1, P1, P1, lb_logits, norm_g)


def _hgrn_bwd(P1, o_pre, states, dn, lb_logits, norm_g, tb):
    S = P1.shape[0]
    D = P1.shape[1] // 3
    K = HGRN_DK
    H = D // K
    HP = HGRN_HEADS_PER_STEP
    C = HGRN_CHUNK
    cpb = tb // C
    nt = S // tb

    def body(q_ref, f_ref, i_ref, o_ref, st_ref, dn_ref, lb_ref, g_ref, dq_ref, dz_ref, dv_ref, dg_ref, dlb_ref, dstate):
        t = pl.program_id(1)

        @pl.when(t == 0)
        def _():
            dstate[...] = jnp.zeros_like(dstate)
            dg_ref[...] = jnp.zeros_like(dg_ref)
            dlb_ref[...] = jnp.zeros_like(dlb_ref)

        lb_all = _lower_bound(lb_ref)
        low, upper_strict, upper = _chunk_masks(tb)
        sum_to, sum_after, sum_from = low.astype(BF16), upper_strict.astype(BF16), upper.astype(BF16)
        for hh in range(HP):
            lanes = slice(hh * K, (hh + 1) * K)
            lb = lb_all[:, lanes]
            gn = g_ref[:, lanes]
            q_raw, z, v = q_ref[:, lanes], f_ref[:, lanes], i_ref[:, lanes]
            sg, sn, f, key, sq = _hgrn_gates(q_raw, z, lb)
            logf = _split3(jnp.log(f))
            b = _exact_sel_parts(sum_to, logf)
            e_pos, e_neg, e_rel = jnp.exp(b), jnp.exp(-b), jnp.exp(_exact_sel_parts(sum_after, logf))
            qd_f, kd_f, kb_f = q_raw * sq * e_pos, key * e_neg, key * e_rel
            qd, kd, kb = qd_f.astype(MXU_DTYPE), kd_f.astype(MXU_DTYPE), kb_f.astype(MXU_DTYPE)
            vm = v.astype(MXU_DTYPE)
            a = jnp.where(low, _nt(qd, kd), 0.0).astype(MXU_DTYPE)
            o = o_ref[:, lanes]
            dnn = dn_ref[:, lanes]
            rs = lax.rsqrt(jnp.mean(o * o, axis=1, keepdims=True) + RMS_EPS)
            dg_ref[:, lanes] += jnp.sum(dnn * o * rs, axis=0, keepdims=True)
            tg = dnn * gn
            dom = (rs * tg - o * (rs * rs * rs) * jnp.mean(tg * o, axis=1, keepdims=True)).astype(MXU_DTYPE)
            da = jnp.where(low, _nt(dom, vm), 0.0).astype(MXU_DTYPE)
            dv = _tn(a, dom)
            dqd = _nn(da, kd)
            dkd = _tn(da, qd)
            dst = dstate[hh]
            dv_s, dqd_s, dkb_s, dbl_s = [None] * cpb, [None] * cpb, [None] * cpb, [None] * cpb
            for ci in reversed(range(cpb)):
                rows = slice(ci * C, (ci + 1) * C)
                st = st_ref[hh, ci]
                dstm = dst.astype(MXU_DTYPE)
                dec = jnp.exp(b[(ci + 1) * C - 1:(ci + 1) * C, :])
                dv_s[ci] = _nt(kb[rows], dstm)
                dkb_s[ci] = _nn(vm[rows], dstm)
                dqd_s[ci] = _nn(dom[rows], st.astype(MXU_DTYPE))
                db_last = jnp.sum(dkb_s[ci] * kb_f[rows], axis=0, keepdims=True) + jnp.sum(dst * st, axis=0, keepdims=True) * dec
                dbl_s[ci] = jnp.broadcast_to(db_last, (C, K))
                dst = dst * dec + _tn(dom[rows], qd[rows])
            dstate[hh] = dst
            dv = dv + jnp.concatenate(dv_s, axis=0)
            dqd = dqd + jnp.concatenate(dqd_s, axis=0)
            dkb = jnp.concatenate(dkb_s, axis=0)
            dkey = dkd * e_neg + dkb * e_rel
            db = dqd * qd_f - dkd * kd_f - dkb * kb_f
            dlogf = _exact_sel_parts(sum_from, _split3(db)) + jnp.concatenate(dbl_s, axis=0)
            gz = (1.0 - lb) * sg * sn
            dz_ref[:, lanes] = (dlogf * gz / f - dkey * gz).astype(dz_ref.dtype)
            dlb_ref[:, lanes] += jnp.sum(dlogf * sn / f - dkey * sn, axis=0, keepdims=True)
            dq_ref[:, lanes] = (dqd * e_pos * (sq + q_raw * sq * (1.0 - sq))).astype(dq_ref.dtype)
            dv_ref[:, lanes] = dv.astype(dv_ref.dtype)

    rev = lambda t: nt - 1 - t
    tok = lambda part: pl.BlockSpec((tb, HP * K), lambda h, t: (rev(t), part * (H // HP) + h))
    vec = lambda rows: pl.BlockSpec((rows, HP * K), lambda h, t: (0, h))
    outs = pl.pallas_call(
        body,
        name="hgrn_bwd",
        grid=(H // HP, nt),
        in_specs=[tok(0), tok(1), tok(2), tok(0),
                  pl.BlockSpec((HP, cpb, K, K), lambda h, t: (h, rev(t), 0, 0)),
                  tok(0), vec(2), vec(1)],
        out_specs=[tok(0), tok(0), tok(0), vec(1), vec(1)],
        out_shape=[_sds((S, D), MXU_DTYPE)] * 3 + [_sds((1, D), F32)] * 2,
        scratch_shapes=[pltpu.VMEM((HP, K, K), F32)],
        compiler_params=_cparams(("parallel", "arbitrary")),
    )(P1, P1, P1, o_pre, states, dn, lb_logits, norm_g)
    return outs


def _lb_logits_grad(dlb, lb_logits):
    def body(d_ref, l_ref, o_ref):
        s1 = _lower_bound(l_ref)
        d = d_ref[...]
        o_ref[0:1, :] = -(1.0 - s1) * s1 * d
        o_ref[1:2, :] = s1 * (1.0 - s1) * d

    return pl.pallas_call(body, name="lb_logits_grad", out_shape=_sds(lb_logits.shape, F32))(dlb, lb_logits)


def _ln_epilogue(acc, extra_refs, out_refs, j):
    res_ref, g_ref, b_ref = extra_refs
    x_ref, xhat_ref, rstd_ref = out_refs
    u = ALPHA * res_ref[...] + acc
    mu = jnp.mean(u, axis=1, keepdims=True)
    cen = u - mu
    rstd = lax.rsqrt(jnp.mean(cen * cen, axis=1, keepdims=True) + LN_EPS)
    xhat = cen * rstd
    xhat_ref[...] = xhat
    x_ref[...] = xhat * g_ref[...] + b_ref[...]
    rstd_ref[...] = rstd


def _mm_res_ln(name, a, w_full, res, g, b, tm, tk):
    S, D = res.shape
    row = pl.BlockSpec((tm, D), lambda i, j, k: (i, 0))
    vec = pl.BlockSpec((1, D), lambda i, j, k: (0, 0))
    outs = [(_sds((S, D), F32), row), (_sds((S, D), F32), row),
            (_sds((S, 1), F32), pl.BlockSpec((tm, 1), lambda i, j, k: (i, 0)))]
    return _matmul(name, a, w_full, "nn", tm, D, tk, outs, _ln_epilogue, extras=[(res, row), (g, vec), (b, vec)])


def _ln_bwd(name, dy, xhat, rstd, g, tm, dep):
    S, D = dy.shape

    def body(dy_ref, xh_ref, r_ref, g_ref, dep_ref, du_ref, dg_ref, db_ref):
        @pl.when(pl.program_id(0) == 0)
        def _():
            dg_ref[...] = jnp.zeros_like(dg_ref)
            db_ref[...] = jnp.zeros_like(db_ref)

        dy_, xh = dy_ref[...], xh_ref[...]
        dg_ref[...] += jnp.sum(dy_ * xh, axis=0, keepdims=True)
        db_ref[...] += jnp.sum(dy_, axis=0, keepdims=True)
        dxh = dy_ * g_ref[...]
        m1 = jnp.mean(dxh, axis=1, keepdims=True)
        m2 = jnp.mean(dxh * xh, axis=1, keepdims=True)
        du_ref[...] = r_ref[...] * (dxh - m1 - xh * m2)

    row = pl.BlockSpec((tm, D), lambda i: (i, 0))
    vec = pl.BlockSpec((1, D), lambda i: (0, 0))
    return pl.pallas_call(
        body,
        name=name,
        grid=(S // tm,),
        in_specs=[row, row, pl.BlockSpec((tm, 1), lambda i: (i, 0)), vec, pl.BlockSpec(memory_space=pl.ANY)],
        out_specs=[row, vec, vec],
        out_shape=[_sds((S, D), F32), _sds((1, D), F32), _sds((1, D), F32)],
        compiler_params=_cparams(("arbitrary",)),
    )(dy, xhat, rstd, g, dep)


def _loss_head(y, target, tm):
    S, D = y.shape

    def body(y_ref, t_ref, sq_ref, dy_ref):
        @pl.when(pl.program_id(0) == 0)
        def _():
            sq_ref[...] = jnp.zeros_like(sq_ref)

        e = y_ref[...] - t_ref[...]
        sq_ref[...] += jnp.sum(e * e, axis=0, keepdims=True)
        dy_ref[...] = e / D

    row = pl.BlockSpec((tm, D), lambda i: (i, 0))
    vec = pl.BlockSpec((1, D), lambda i: (0, 0))
    return pl.pallas_call(
        body,
        name="loss_head",
        grid=(S // tm,),
        in_specs=[row, row],
        out_specs=[vec, row],
        out_shape=[_sds((1, D), F32), _sds((S, D), F32)],
        compiler_params=_cparams(("arbitrary",)),
    )(y, target)


def _mlp_up(name, x, w_up, tm, tn, tk):
    S = x.shape[0]
    F = w_up.shape[1]

    def epilogue(acc, extra_refs, out_refs, j):
        r = jnp.maximum(acc, 0.0)
        out_refs[0][...] = (r * r).astype(out_refs[0].dtype)

    return _matmul(name, x, w_up, "nn", tm, tn, tk, [(_sds((S, F), MXU_DTYPE), _ij_spec(tm, tn))], epilogue)[0]


def _mlp_down_bwd(name, dy, w_down, a, tm, tn, tk):
    S, F = a.shape

    def epilogue(acc, extra_refs, out_refs, j):
        out_refs[0][...] = (acc * (2.0 * jnp.sqrt(extra_refs[0][...].astype(F32)))).astype(out_refs[0].dtype)

    return _matmul(name, dy, w_down, "nt", tm, tn, tk, [(_sds((S, F), MXU_DTYPE), _ij_spec(tm, tn))], epilogue,
                   extras=[(a, _ij_spec(tm, tn))])[0]


def _mm_nt_res(name, dy, w, du, tm, tn, tk):
    S = dy.shape[0]
    N = w.shape[0]

    def epilogue(acc, extra_refs, out_refs, j):
        out_refs[0][...] = ALPHA * extra_refs[0][...] + acc

    return _matmul(name, dy, w, "nt", tm, tn, tk, [(_sds((S, N), F32), _ij_spec(tm, tn))], epilogue,
                   extras=[(du, _ij_spec(tm, tn))])[0]


def _attn_out_bwd(du, w_out, o, sel_t, tm, tk):
    S, D = o.shape

    def epilogue(acc, extra_refs, out_refs, j):
        out_refs[0][...] = acc
        out_refs[1][...] = _exact_nn(acc * extra_refs[0][...], extra_refs[1][...])

    row = pl.BlockSpec((tm, D), lambda i, j, k: (i, 0))
    slim = pl.BlockSpec((tm, LANES), lambda i, j, k: (i, 0))
    return _matmul("attn_out_bwd", du, w_out, "nt", tm, D, tk,
                   [(_sds((S, D), F32), row), (_sds((S, LANES), F32), slim)], epilogue,
                   extras=[(o, row), (sel_t, pl.BlockSpec((D, LANES), lambda i, j, k: (0, 0)))])


def _adamw(name, w, g, m, v):
    shape = w.shape
    cols = shape[-1]
    rows = math.prod(shape[:-1])
    w2, g2, m2, v2 = (t.reshape(rows, cols) for t in (w, g, m, v))
    tr = _pick(rows, (256, 128, 64, 32, 16, 8))
    c1 = 1.0 - ADAM_B1 ** ADAM_STEP
    c2 = 1.0 - ADAM_B2 ** ADAM_STEP

    def body(w_ref, g_ref, m_ref, v_ref, d_ref, nm_ref, nv_ref):
        gg = g_ref[...]
        nm = ADAM_B1 * m_ref[...] + (1.0 - ADAM_B1) * gg
        nv = ADAM_B2 * v_ref[...] + (1.0 - ADAM_B2) * (gg * gg)
        nm_ref[...] = nm
        nv_ref[...] = nv
        d_ref[...] = -ADAM_LR * ((nm / c1) / (jnp.sqrt(nv / c2) + ADAM_EPS) + ADAM_WD * w_ref[...])

    blk = pl.BlockSpec((tr, cols), lambda i: (i, 0))
    outs = pl.pallas_call(
        body,
        name=name,
        grid=(rows // tr,),
        in_specs=[blk] * 4,
        out_specs=[blk] * 3,
        out_shape=[_sds((rows, cols), F32)] * 3,
        compiler_params=_cparams(("parallel",)),
    )(w2, g2, m2, v2)
    return tuple(o.reshape(shape) for o in outs)


HBM = pl.BlockSpec(memory_space=pl.ANY)


def _shard_slice(ref, axis, size, index):
    idx = [slice(None)] * len(ref.shape)
    idx[axis] = pl.ds(pl.multiple_of(index * size, 8), size)
    return ref.at[tuple(idx)]


IN_HBM = pl.BlockSpec(memory_space=pltpu.HBM)
IN_SEM = pl.BlockSpec(memory_space=pltpu.SEMAPHORE)
DATAFLOW = pltpu.SideEffectType.DATAFLOW_SIDE_EFFECTING


def _hbm(t):
    return pltpu.with_memory_space_constraint(t, pltpu.HBM)


def _token_spec():
    return pl.BlockSpec(memory_space=pltpu.VMEM)


def _gather_copies(s_refs, f_refs, axes, send, recv, loc, arrival):
    x, y, c = lax.axis_index("x"), lax.axis_index("y"), lax.axis_index("c")
    chips = [(1 - x, y), (x, 1 - y), (1 - x, 1 - y)]
    local, remote = [], []
    for a in range(len(s_refs)):
        size = s_refs[a].shape[axes[a]]
        local.append(pltpu.make_async_copy(s_refs[a], _shard_slice(f_refs[a], axes[a], size, 2 * x + y), loc.at[a]))
        for k, (px, py) in enumerate(chips):
            block = (2 * px + py) if arrival else (2 * x + y)
            remote.append(pltpu.make_async_remote_copy(
                src_ref=s_refs[a], dst_ref=_shard_slice(f_refs[a], axes[a], size, block), send_sem=send.at[3 * a + k],
                recv_sem=recv.at[3 * a + k], device_id=(px, py, c), device_id_type=MESH))
    return local, remote


def _gather_start(name, shards, axes, after):
    n = len(shards)
    fulls = []
    for s, ax in zip(shards, axes):
        fs = list(s.shape)
        fs[ax] *= 4
        fulls.append(lax.empty(tuple(fs), s.dtype))

    def body(*refs):
        s_refs, f_refs = refs[:n], refs[n:2 * n]
        send, recv, loc, token = refs[2 * n + 1], refs[2 * n + 2], refs[2 * n + 3], refs[-1]
        local, remote = _gather_copies(s_refs, f_refs, axes, send, recv, loc, arrival=False)
        for cp in remote + local:
            cp.start()
        token[...] = jnp.zeros_like(token)

    outs = pl.pallas_call(
        body,
        name=name,
        out_shape=(pltpu.SemaphoreType.DMA((3 * n,)), pltpu.SemaphoreType.DMA((3 * n,)), pltpu.SemaphoreType.DMA((n,)),
                   *[pltpu.HBM(t.shape, t.dtype) for t in shards + fulls], _sds((8, LANES), F32)),
        in_specs=[IN_HBM] * (2 * n) + [HBM],
        out_specs=(IN_SEM, IN_SEM, IN_SEM, *[IN_HBM] * (2 * n), _token_spec()),
        input_output_aliases={i: 3 + i for i in range(2 * n)},
        compiler_params=pltpu.CompilerParams(has_side_effects=DATAFLOW),
    )(*[_hbm(t) for t in shards + fulls], after)
    return (outs[0], outs[1], outs[2], list(outs[3:3 + n]), list(outs[3 + n:3 + 2 * n]), axes), outs[-1]


def _gather_wait(name, state, *after):
    send, recv, loc, s_thru, f_thru, axes = state
    n = len(s_thru)

    def body(*refs):
        s_refs, f_refs = refs[:n], refs[n:2 * n]
        local, remote = _gather_copies(s_refs, f_refs, axes, refs[2 * n], refs[2 * n + 1], refs[2 * n + 2], arrival=True)
        for cp in local:
            cp.wait()
        for cp in remote:
            cp.wait_send()
            cp.wait_recv()

    outs = pl.pallas_call(
        body,
        name=name,
        out_shape=tuple(pltpu.HBM(t.shape, t.dtype) for t in s_thru + f_thru),
        in_specs=[IN_HBM] * (2 * n) + [IN_SEM, IN_SEM, IN_SEM] + [HBM] * len(after),
        out_specs=tuple([IN_HBM] * (2 * n)),
        input_output_aliases={i: i for i in range(2 * n)},
        compiler_params=pltpu.CompilerParams(has_side_effects=DATAFLOW),
    )(*s_thru, *f_thru, send, recv, loc, *after)
    return list(outs[n:2 * n])


FLIPS = [(fx, fy, fc) for fx in (0, 1) for fy in (0, 1) for fc in (0, 1)][1:]


def _piece_shape(shape, axis):
    ps = list(shape)
    if axis == 0:
        ps[0] //= 8
    else:
        ps[0] //= 2
        ps[axis] //= 4
    return tuple(ps)


def _piece(ref, axis, q, c):
    shape = ref.shape
    idx = [slice(None)] * len(shape)
    if axis == 0:
        h = shape[0] // 8
        idx[0] = pl.ds(pl.multiple_of((2 * q + c) * h, 8), h)
    else:
        h, w = shape[0] // 2, shape[axis] // 4
        idx[0] = pl.ds(c * h, h)
        idx[axis] = pl.ds(pl.multiple_of(q * w, LANES if axis == len(shape) - 1 else 8), w)
    return ref.at[tuple(idx)]


def _own_piece(g, axis):
    ps = _piece_shape(g.shape, axis)
    q, c = 2 * lax.axis_index("x") + lax.axis_index("y"), lax.axis_index("c")
    start = [0] * len(ps)
    if axis == 0:
        start[0] = (2 * q + c) * ps[0]
    else:
        start[0] = c * ps[0]
        start[axis] = q * ps[axis]
    return lax.dynamic_slice(g, start, ps)


def _scatter_copies(g_refs, l_refs, axes, send, recv):
    x, y, c = lax.axis_index("x"), lax.axis_index("y"), lax.axis_index("c")
    out = []
    for a in range(len(g_refs)):
        for k, (fx, fy, fc) in enumerate(FLIPS):
            tx, ty, tc = x ^ fx, y ^ fy, c ^ fc
            out.append(pltpu.make_async_remote_copy(
                src_ref=_piece(g_refs[a], axes[a], 2 * tx + ty, tc), dst_ref=l_refs[a].at[k],
                send_sem=send.at[7 * a + k], recv_sem=recv.at[7 * a + k], device_id=(tx, ty, tc), device_id_type=MESH))
    return out


def _scatter_start(name, grads, axes):
    n = len(grads)
    lands = [lax.empty((7,) + _piece_shape(g.shape, ax), g.dtype) for g, ax in zip(grads, axes)]

    def body(*refs):
        g_refs, l_refs = refs[:n], refs[n:2 * n]
        send, recv, token = refs[2 * n], refs[2 * n + 1], refs[-1]
        for cp in _scatter_copies(g_refs, l_refs, axes, send, recv):
            cp.start()
        token[...] = jnp.zeros_like(token)

    outs = pl.pallas_call(
        body,
        name=name,
        out_shape=(pltpu.SemaphoreType.DMA((7 * n,)), pltpu.SemaphoreType.DMA((7 * n,)),
                   *[pltpu.HBM(t.shape, t.dtype) for t in grads + lands], _sds((8, LANES), F32)),
        in_specs=[IN_HBM] * (2 * n),
        out_specs=(IN_SEM, IN_SEM, *[IN_HBM] * (2 * n), _token_spec()),
        input_output_aliases={i: 2 + i for i in range(2 * n)},
        compiler_params=pltpu.CompilerParams(has_side_effects=DATAFLOW),
    )(*[_hbm(t) for t in grads + lands])
    return (outs[0], outs[1], list(outs[2:2 + n]), list(outs[2 + n:2 + 2 * n]), axes), outs[-1]


def _scatter_wait(name, state, after):
    send, recv, g_thru, l_thru, axes = state
    n = len(g_thru)

    def body(*refs):
        g_refs, l_refs = refs[:n], refs[n:2 * n]
        for cp in _scatter_copies(g_refs, l_refs, axes, refs[2 * n], refs[2 * n + 1]):
            cp.wait_send()
            cp.wait_recv()

    outs = pl.pallas_call(
        body,
        name=name,
        out_shape=tuple(pltpu.HBM(t.shape, t.dtype) for t in g_thru + l_thru),
        in_specs=[IN_HBM] * (2 * n) + [IN_SEM, IN_SEM, HBM],
        out_specs=tuple([IN_HBM] * (2 * n)),
        input_output_aliases={i: i for i in range(2 * n)},
        compiler_params=pltpu.CompilerParams(has_side_effects=DATAFLOW),
    )(*g_thru, *l_thru, send, recv, after)
    return list(outs[:n]), list(outs[n:2 * n])


def _reduce_join(name, landing, own):
    piece = own.shape
    C = piece[-1]
    R = math.prod(piece[:-1])
    l3 = landing.reshape(7, R, C)
    own2 = own.reshape(R, C)
    tr = _pick(R, [t for t in (512, 256, 128, 64, 32, 16, 8) if t * C <= 256 * 1024])
    nsteps = R // tr

    def body(own_ref, l_ref, o_ref, buf, send, loc, recv):
        i = pl.program_id(0)
        x, y, c = lax.axis_index("x"), lax.axis_index("y"), lax.axis_index("c")
        sibling = (x, y, 1 - c)

        def copies(slot, step):
            dst = o_ref.at[pl.ds(pl.multiple_of(c * R + step * tr, 8), tr), :]
            return (pltpu.make_async_copy(buf.at[slot], dst, loc.at[slot]),
                    pltpu.make_async_remote_copy(src_ref=buf.at[slot], dst_ref=dst, send_sem=send.at[slot], recv_sem=recv,
                                                 device_id=sibling, device_id_type=MESH))

        @pl.when(i >= 2)
        def _():
            lc, rc = copies(i % 2, i - 2)
            lc.wait()
            rc.wait_send()

        acc = own_ref[...].astype(F32)
        for s in range(7):
            acc = acc + l_ref[s].astype(F32)
        buf[i % 2] = acc
        lc, rc = copies(i % 2, i)
        lc.start()
        rc.start()

        @pl.when(i == nsteps - 1)
        def _():
            for st in range(max(nsteps - 2, 0), nsteps):
                lc, rc = copies(st % 2, st)
                lc.wait()
                rc.wait_send()
            theirs = o_ref.at[pl.ds(pl.multiple_of((1 - c) * R, 8), R), :]
            pltpu.make_async_remote_copy(src_ref=theirs, dst_ref=theirs, send_sem=send.at[0], recv_sem=recv,
                                         device_id=sibling, device_id_type=MESH).wait_recv()

    return pl.pallas_call(
        body,
        name=name,
        grid=(nsteps,),
        in_specs=[pl.BlockSpec((tr, C), lambda i: (i, 0)), pl.BlockSpec((7, tr, C), lambda i: (0, i, 0))],
        out_specs=HBM,
        out_shape=_sds((2 * R, C), F32),
        scratch_shapes=[pltpu.VMEM((2, tr, C), F32), pltpu.SemaphoreType.DMA((2,)), pltpu.SemaphoreType.DMA((2,)),
                        pltpu.SemaphoreType.DMA(())],
        compiler_params=_cparams(("arbitrary",)),
    )(own2, l3)


def _all_reduce_small(v, dep):
    R, D = v.shape

    def body(v_ref, dep_ref, o_ref, land, send, recv):
        x, y, c = lax.axis_index("x"), lax.axis_index("y"), lax.axis_index("c")
        my_slot = 4 * x + 2 * y + c
        land[my_slot] = v_ref[...]
        for k, (fx, fy, fc) in enumerate(FLIPS):
            tx, ty, tc = x ^ fx, y ^ fy, c ^ fc
            pltpu.make_async_remote_copy(src_ref=v_ref, dst_ref=land.at[my_slot], send_sem=send.at[k], recv_sem=recv.at[k],
                                         device_id=(tx, ty, tc), device_id_type=MESH).start()
        for k, (fx, fy, fc) in enumerate(FLIPS):
            tx, ty, tc = x ^ fx, y ^ fy, c ^ fc
            cp = pltpu.make_async_remote_copy(src_ref=v_ref, dst_ref=land.at[4 * tx + 2 * ty + tc], send_sem=send.at[k],
                                              recv_sem=recv.at[k], device_id=(tx, ty, tc), device_id_type=MESH)
            cp.wait_send()
            cp.wait_recv()
        acc = land[0]
        for s in range(1, 8):
            acc = acc + land[s]
        o_ref[...] = acc

    return pl.pallas_call(
        body,
        name="all_reduce_small",
        in_specs=[pl.BlockSpec(memory_space=pltpu.VMEM), pl.BlockSpec(memory_space=pl.ANY)],
        out_specs=pl.BlockSpec(memory_space=pltpu.VMEM),
        out_shape=_sds((R, D), F32),
        scratch_shapes=[pltpu.VMEM((8, R, D), F32), pltpu.SemaphoreType.DMA((7,)), pltpu.SemaphoreType.DMA((7,))],
    )(v, dep)


def kernel(x, attn_w_in, attn_w_out, hgrn_w_in, hgrn_w_out, hgrn_norm_g, lb_logits, ln_mix_g, ln_mix_b, ln_ffn_g, ln_ffn_b, ffn_w_up, ffn_w_down, loss_target, m_attn_w_in, m_attn_w_out, m_hgrn_w_in, m_hgrn_w_out, m_hgrn_norm_g, m_lb_logits, m_ln_mix_g, m_ln_mix_b, m_ln_ffn_g, m_ln_ffn_b, m_ffn_w_up, m_ffn_w_down, v_attn_w_in, v_attn_w_out, v_hgrn_w_in, v_hgrn_w_out, v_hgrn_norm_g, v_lb_logits, v_ln_mix_g, v_ln_mix_b, v_ln_ffn_g, v_ln_ffn_b, v_ffn_w_up, v_ffn_w_down):
    xs = x[0]
    tgt = loss_target[0]
    S, D = xs.shape
    F = ffn_w_up.shape[2] * 4
    T1 = _pick(S, (1024, 512, 256))
    T2 = _pick(S, (2048, 1024, 512))
    TH = _pick(S, (512, 256))
    TB = _pick(S, (512, 256))
    TN = _pick(D, (512, 256, 128))
    TF = _pick(F, (1024, 512))
    TG = _pick(3 * D, (1536, 1024, 768))
    TW = _pick(F, (2048, 1024))

    cast = lambda w: w.astype(MXU_DTYPE)
    st_a, tok = _gather_start("gather_a", [cast(attn_w_in[0])], [1], jnp.zeros((8, LANES), F32))
    tok, (xs_late, w_aout, w_fup, w_fdown, w_hin, w_hout) = lax.optimization_barrier(
        (tok, (xs, attn_w_out, ffn_w_up, ffn_w_down, hgrn_w_in, hgrn_w_out)))
    st_b, tok = _gather_start("gather_b", [cast(w_aout[0]), cast(w_fup[0]), cast(w_fdown[0])], [0, 1, 0], tok)
    st_c, tok = _gather_start("gather_c", [cast(w_hin[0]), cast(w_hout[0]), hgrn_norm_g, cast(w_fup[1]), cast(w_fdown[1])],
                              [1, 0, 1, 1, 0], tok)

    cos3, sin3 = _rope_tables(S)
    sel = _head_sel(D)
    sel_t = sel.T

    xc3 = _stack_classes("x_classes", xs_late, MXU_DTYPE)
    (wa_in,) = _gather_wait("gather_a_wait", st_a, tok, xc3, cos3, sin3)
    P3 = _attn_proj(xc3, wa_in, cos3, sin3, T2, TN)
    o3, lse3 = _attn_fwd(P3, D)
    o_att, L_att = _attn_mix(o3, lse3, sel)
    wa_out, w_up0, w_down0 = _gather_wait("gather_b_wait", st_b, L_att)
    x1, xh1, r1 = _mm_res_ln("attn_out_ln", o_att, wa_out, xs, ln_mix_g[0:1], ln_mix_b[0:1], TH, D)
    a0 = _mlp_up("mlp0_up", x1, w_up0, T1, TF, D)
    x2, xh2, r2 = _mm_res_ln("mlp0_down_ln", a0, w_down0, x1, ln_ffn_g[0:1], ln_ffn_b[0:1], TH, F)

    wh_in, wh_out, norm_g, w_up1, w_down1 = _gather_wait("gather_c_wait", st_c, r2)
    P1 = _plain_mm("hgrn_proj", x2, wh_in, "nn", F32, T1, _pick(3 * D, (1024, 768, 512)), D)
    o_h, n_h, states = _hgrn_fwd(P1, lb_logits, norm_g, TB)
    x3, xh3, r3 = _mm_res_ln("hgrn_out_ln", n_h, wh_out, x2, ln_mix_g[1:2], ln_mix_b[1:2], TH, D)
    a1 = _mlp_up("mlp1_up", x3, w_up1, T1, TF, D)
    x4, xh4, r4 = _mm_res_ln("mlp1_down_ln", a1, w_down1, x3, ln_ffn_g[1:2], ln_ffn_b[1:2], TH, F)

    sq, dx4 = _loss_head(x4, tgt, TH)

    wgrad = lambda name, a, dy, tm, tn: _plain_mm(name, a, dy, "tn", MXU_DTYPE, tm, tn, T1)
    du4, dg_ffn1, db_ffn1 = _ln_bwd("ln_ffn1_bwd", dx4, xh4, r4, ln_ffn_g[1:2], TH, sq)
    dh1 = _mlp_down_bwd("mlp1_down_bwd", du4, w_down1, a1, T1, TF, D)
    g_down1 = wgrad("g_down1", a1, du4, TW, D)
    dx3 = _mm_nt_res("mlp1_up_bwd", dh1, w_up1, du4, TH, D, F)
    g_up1 = wgrad("g_up1", x3, dh1, D, TW)
    sc_1, tok = _scatter_start("scatter_1", [g_down1, g_up1], [0, 1])
    du3, dg_mix1, db_mix1 = _ln_bwd("ln_mix1_bwd", dx3, xh3, r3, ln_mix_g[1:2], TH, tok)
    dn = _plain_mm("hgrn_out_bwd", du3, wh_out, "nt", F32, T1, D, D)
    g_hout = wgrad("g_hgrn_out", n_h, du3, D, D)
    dq_raw, dz, dv, dg_norm, dlb = _hgrn_bwd(P1, o_h, states, dn, lb_logits, norm_g, TB)
    dP1 = jnp.concatenate([dq_raw, dz, dv], axis=1)
    dx2 = _mm_nt_res("hgrn_in_bwd", dP1, wh_in, du3, TH, D, 3 * D)
    g_hin = wgrad("g_hgrn_in", x2, dP1, D, TG)
    d_lb_logits = _lb_logits_grad(dlb, lb_logits)
    sc_2, tok = _scatter_start("scatter_2", [g_hout, g_hin], [0, 1])

    du2, dg_ffn0, db_ffn0 = _ln_bwd("ln_ffn0_bwd", dx2, xh2, r2, ln_ffn_g[0:1], TH, tok)
    dh0 = _mlp_down_bwd("mlp0_down_bwd", du2, w_down0, a0, T1, TF, D)
    g_down0 = wgrad("g_down0", a0, du2, TW, D)
    dx1 = _mm_nt_res("mlp0_up_bwd", dh0, w_up0, du2, TH, D, F)
    g_up0 = wgrad("g_up0", x1, dh0, D, TW)
    sc_3, tok = _scatter_start("scatter_3", [g_down0, g_up0], [0, 1])
    du1, dg_mix0, db_mix0 = _ln_bwd("ln_mix0_bwd", dx1, xh1, r1, ln_mix_g[0:1], TH, tok)
    do, delta = _attn_out_bwd(du1, wa_out, o_att, sel_t, TH, D)
    g_aout = wgrad("g_attn_out", o_att, du1, D, D)
    dP3 = _attn_bwd(P3, _stack_classes("do_classes", do, MXU_DTYPE), _stack_classes("lse_classes", L_att, F32),
                    _stack_classes("delta_classes", delta, F32), cos3, sin3, D)
    small = jnp.concatenate([d_lb_logits, dg_mix0, dg_mix1, db_mix0, db_mix1, dg_ffn0, dg_ffn1, db_ffn0, db_ffn1,
                             dg_norm, sq, jnp.zeros((4, D), F32)], axis=0)
    small = _all_reduce_small(small, dP3)
    loss = 0.5 * jnp.sum(small[11]) / D
    grp = lambda j: j // (3 * D // TG)
    g_ain = _matmul("g_attn_in", xc3, dP3, "tn", D, TG, T1, [(_sds((D, 9 * D), MXU_DTYPE), _ij_spec(D, TG))], _store_epilogue,
                    a_map=lambda i, j, k: (k + grp(j) * (S // T1), i),
                    b_map=lambda i, j, k: (k + grp(j) * (S // T1), j % (3 * D // TG)), mnk=(D, 9 * D, S), dep=small)[0]
    sc_4, tok = _scatter_start("scatter_4", [g_aout, g_ain], [0, 1])
    dxc3 = _matmul("attn_in_bwd", dP3, wa_in, "nt", TH, D, 3 * D, [(_sds((3 * S, D), F32), _ij_spec(TH, D))], _store_epilogue,
                   b_map=lambda i, j, k: (j, k + i // (S // TH)), mnk=(3 * S, D, 3 * D), dep=tok)[0]
    grad_x = _input_grad(du1, dxc3)

    def reduced(name, state, after):
        gs, lands = _scatter_wait(name + "_wait", state, after)
        return [_reduce_join(f"{name}_reduce_{i}", l, _own_piece(g, ax)) for i, (l, g, ax) in enumerate(zip(lands, gs, state[4]))]

    r_down1, r_up1 = reduced("scatter_1", sc_1, grad_x)
    r_hout, r_hin = reduced("scatter_2", sc_2, r_up1)
    r_down0, r_up0 = reduced("scatter_3", sc_3, r_hin)

    my_chip = 2 * lax.axis_index("x") + lax.axis_index("y")
    nsh = hgrn_norm_g.shape[1]
    g_norm = lax.dynamic_slice(small[10:11], (0, my_chip * nsh), (1, nsh))

    grads, upd = {}, {}

    def update(nm, w, gr, m, v):
        grads[nm] = gr.reshape(w.shape)
        upd[nm] = _adamw("adamw_" + nm, w, grads[nm], m, v)

    update("hgrn_w_in", hgrn_w_in, r_hin, m_hgrn_w_in, v_hgrn_w_in)
    update("hgrn_w_out", hgrn_w_out, r_hout, m_hgrn_w_out, v_hgrn_w_out)
    update("ffn_w_up", ffn_w_up, jnp.stack([r_up0, r_up1]), m_ffn_w_up, v_ffn_w_up)
    update("ffn_w_down", ffn_w_down, jnp.stack([r_down0, r_down1]), m_ffn_w_down, v_ffn_w_down)
    r_aout, r_ain = reduced("scatter_4", sc_4, upd["ffn_w_down"][2])
    update("attn_w_in", attn_w_in, r_ain, m_attn_w_in, v_attn_w_in)
    update("attn_w_out", attn_w_out, r_aout, m_attn_w_out, v_attn_w_out)
    grads["hgrn_norm_g"] = g_norm
    upd["hgrn_norm_g"] = _adamw("adamw_hgrn_norm_g", hgrn_norm_g, g_norm, m_hgrn_norm_g, v_hgrn_norm_g)
    cat = lambda ts: jnp.concatenate(ts, axis=0)
    small_w = cat([lb_logits, ln_mix_g, ln_mix_b, ln_ffn_g, ln_ffn_b])
    small_m = cat([m_lb_logits, m_ln_mix_g, m_ln_mix_b, m_ln_ffn_g, m_ln_ffn_b])
    small_v = cat([v_lb_logits, v_ln_mix_g, v_ln_mix_b, v_ln_ffn_g, v_ln_ffn_b])
    small_upd = _adamw("adamw_small", small_w, small[0:10], small_m, small_v)
    for i, nm in enumerate(["lb_logits", "ln_mix_g", "ln_mix_b", "ln_ffn_g", "ln_ffn_b"]):
        grads[nm] = small[2 * i:2 * i + 2]
        upd[nm] = tuple(t[2 * i:2 * i + 2] for t in small_upd)

    order = ["attn_w_in", "attn_w_out", "hgrn_w_in", "hgrn_w_out", "hgrn_norm_g", "lb_logits", "ln_mix_g", "ln_mix_b",
             "ln_ffn_g", "ln_ffn_b", "ffn_w_up", "ffn_w_down"]
    return (loss, grad_x[None], *[grads[k] for k in order], *[upd[k][0] for k in order],
            *[upd[k][1] for k in order], *[upd[k][2] for k in order])
```

```python
import math

import jax
import jax.numpy as jnp
from jax import lax
from jax.experimental import pallas as pl
from jax.experimental.pallas import tpu as pltpu

F32 = jnp.float32
BF16 = jnp.bfloat16
MXU_DTYPE = BF16

HEAD_DIM = 64
ATTN_BLK = 128
DILATIONS = (1, 4, 16)
ROPE_THETA = 10000.0
HGRN_DK = 128
HGRN_CHUNK = 64
DEPTH = 2
LN_EPS = 1e-5
RMS_EPS = 1e-6
ALPHA = (2 * DEPTH) ** 0.25
ADAM_LR, ADAM_B1, ADAM_B2, ADAM_EPS, ADAM_WD, ADAM_STEP = 0.001, 0.9, 0.999, 1e-08, 0.01, 10

LANES = 128
VMEM_LIMIT = 56 * 1024 * 1024
NEG = -1e30
MESH = pl.DeviceIdType.MESH


def _cparams(sem=None):
    return pltpu.CompilerParams(dimension_semantics=sem, vmem_limit_bytes=VMEM_LIMIT)


def _sds(shape, dtype):
    return jax.ShapeDtypeStruct(tuple(shape), dtype)


def _dg(a, b, ca, cb):
    return lax.dot_general(a, b, (((ca,), (cb,)), ((), ())), preferred_element_type=F32)


def _nn(a, b):
    return _dg(a, b, 1, 0)


def _nt(a, b):
    return _dg(a, b, 1, 1)


def _tn(a, b):
    return _dg(a, b, 0, 0)


def _split3(a):
    hi = a.astype(BF16)
    r = a - hi.astype(F32)
    mid = r.astype(BF16)
    lo = (r - mid.astype(F32)).astype(BF16)
    return hi, mid, lo


def _exact_nn(a, sel):
    hi, mid, lo = _split3(a)
    return _nn(hi, sel) + _nn(mid, sel) + _nn(lo, sel)


def _pick(n, prefs):
    for p in prefs:
        if n % p == 0:
            return p
    return n


def _matmul(name, a, b, form, tm, tn, tk, outs, epilogue, extras=(), a_map=None, b_map=None, mnk=None, dep=None):
    if form == "nn":
        (M, K), N = a.shape, b.shape[1]
        a_spec = pl.BlockSpec((tm, tk), a_map or (lambda i, j, k: (i, k)))
        b_spec = pl.BlockSpec((tk, tn), b_map or (lambda i, j, k: (k, j)))
        ca, cb = 1, 0
    elif form == "nt":
        (M, K), N = a.shape, b.shape[0]
        a_spec = pl.BlockSpec((tm, tk), a_map or (lambda i, j, k: (i, k)))
        b_spec = pl.BlockSpec((tn, tk), b_map or (lambda i, j, k: (j, k)))
        ca, cb = 1, 1
    else:
        (K, M), N = a.shape, b.shape[1]
        a_spec = pl.BlockSpec((tk, tm), a_map or (lambda i, j, k: (k, i)))
        b_spec = pl.BlockSpec((tk, tn), b_map or (lambda i, j, k: (k, j)))
        ca, cb = 0, 0
    if mnk is not None:
        M, N, K = mnk
    assert M % tm == 0 and N % tn == 0 and K % tk == 0, (name, M, N, K, tm, tn, tk)
    nk = K // tk
    ne, no = len(extras), len(outs)
    deps = [] if dep is None else [dep]
    nd = len(deps)

    def body(a_ref, b_ref, *rest):
        extra_refs, out_refs = rest[:ne], rest[ne + nd:ne + nd + no]
        j = pl.program_id(1)
        part = _dg(a_ref[...].astype(MXU_DTYPE), b_ref[...].astype(MXU_DTYPE), ca, cb)
        if nk == 1:
            epilogue(part, extra_refs, out_refs, j)
            return
        acc_ref = rest[-1]
        k = pl.program_id(2)

        @pl.when(k == 0)
        def _():
            acc_ref[...] = part

        @pl.when(k > 0)
        def _():
            acc_ref[...] += part

        @pl.when(k == nk - 1)
        def _():
            epilogue(acc_ref[...], extra_refs, out_refs, j)

    res = pl.pallas_call(
        body,
        name=name,
        grid=(M // tm, N // tn, nk),
        in_specs=[a_spec, b_spec] + [s for _, s in extras] + [pl.BlockSpec(memory_space=pl.ANY)] * nd,
        out_specs=[s for _, s in outs],
        out_shape=[o for o, _ in outs],
        scratch_shapes=[pltpu.VMEM((tm, tn), F32)] if nk > 1 else [],
        compiler_params=_cparams(("parallel", "parallel", "arbitrary")),
    )(a, b, *[e for e, _ in extras], *deps)
    return res


def _ij_spec(tm, tn):
    return pl.BlockSpec((tm, tn), lambda i, j, k: (i, j))


def _store_epilogue(acc, extra_refs, out_refs, j):
    out_refs[0][...] = acc.astype(out_refs[0].dtype)


def _plain_mm(name, a, b, form, out_dtype, tm, tn, tk):
    M = a.shape[1] if form == "tn" else a.shape[0]
    N = b.shape[0] if form == "nt" else b.shape[1]
    return _matmul(name, a, b, form, tm, tn, tk, [(_sds((M, N), out_dtype), _ij_spec(tm, tn))], _store_epilogue)[0]


def _class_slabs(S):
    assert DILATIONS[0] == 1
    return [(g, d, r, S // d) for g, d in enumerate(DILATIONS) if d > 1 for r in range(d)]


def _stack_classes(name, t, out_dtype):
    S, W = t.shape

    def body(x_ref, o_ref):
        o_ref[0:S, :] = x_ref[...].astype(out_dtype)
        for g, d, r, n in _class_slabs(S):
            o_ref[g * S + r * n:g * S + (r + 1) * n, :] = x_ref[pl.ds(r, n, stride=d), :].astype(out_dtype)

    return pl.pallas_call(
        body,
        name=name,
        grid=(W // LANES,),
        in_specs=[pl.BlockSpec((S, LANES), lambda j: (0, j))],
        out_specs=pl.BlockSpec((3 * S, LANES), lambda j: (0, j)),
        out_shape=_sds((3 * S, W), out_dtype),
        compiler_params=_cparams(("parallel",)),
    )(t)


def _rope_tables(seq):
    half = HEAD_DIM // 2
    inv = ROPE_THETA ** (-jnp.arange(half, dtype=F32) * (2.0 / HEAD_DIM))
    inv = jnp.tile(inv, LANES // half)
    pos = []
    for d in DILATIONS:
        row = jnp.arange(seq)
        pos.append((row % (seq // d)) * d + row // (seq // d))
    ang = jnp.concatenate(pos).astype(F32)[:, None] * inv[None, :]
    first = (jnp.arange(LANES) % HEAD_DIM) < half
    sin = jnp.sin(ang)
    return jnp.cos(ang), jnp.where(first[None, :], -sin, sin)


def _partner(x):
    half = HEAD_DIM // 2
    lane = lax.broadcasted_iota(jnp.int32, x.shape, 1)
    first = (lane % HEAD_DIM) < half
    return jnp.where(first, pltpu.roll(x, LANES - half, 1), pltpu.roll(x, half, 1))


def _attn_proj(x3, w_full, cos3, sin3, tm, tn):
    S3, D = x3.shape
    S = S3 // 3
    per_part = D // tn
    per_group = 3 * per_part

    def epilogue(acc, extra_refs, out_refs, j):
        cos_ref, sin_ref = extra_refs
        o_ref = out_refs[0]
        is_rot = j // per_part < 2

        @pl.when(is_rot)
        def _():
            c, s = cos_ref[...], sin_ref[...]
            for t in range(tn // LANES):
                xs = acc[:, t * LANES:(t + 1) * LANES]
                o_ref[:, t * LANES:(t + 1) * LANES] = (xs * c + _partner(xs) * s).astype(o_ref.dtype)

        @pl.when(jnp.logical_not(is_rot))
        def _():
            o_ref[...] = acc.astype(o_ref.dtype)

    tab = pl.BlockSpec((tm, LANES), lambda i, j, k: (i, 0))
    return _matmul("attn_proj", x3, w_full, "nn", tm, tn, D, [(_sds((S3, 3 * D), MXU_DTYPE), _ij_spec(tm, tn))],
                   epilogue, extras=[(cos3, tab), (sin3, tab)],
                   b_map=lambda i, j, k: (k, j + (i // (S // tm)) * per_group), mnk=(S3, 3 * D, D))[0]


def _head_sel(d_model):
    h = jnp.arange(LANES)[:, None]
    l = jnp.arange(d_model)[None, :]
    return (l // HEAD_DIM == h).astype(BF16)


def _class_edges(b, nblk):
    g = b // nblk
    per_class = jnp.where(g == 0, nblk // DILATIONS[0], jnp.where(g == 1, nblk // DILATIONS[1], nblk // DILATIONS[2]))
    pos = (b % nblk) % per_class
    return pos != 0, pos != per_class - 1


def _two_heads(t, top):
    zero = jnp.zeros_like(t)
    return jnp.concatenate([jnp.where(top, t, zero), jnp.where(top, zero, t)], axis=0)


def _band_mask(has_prev):
    B = ATTN_BLK
    row = lax.broadcasted_iota(jnp.int32, (2 * B, 2 * B), 0) % B
    col = lax.broadcasted_iota(jnp.int32, (2 * B, 2 * B), 1)
    in_prev = jnp.logical_and(jnp.logical_and(col < B, col >= row), has_prev)
    in_own = jnp.logical_and(col >= B, col - B <= row)
    return jnp.logical_or(in_prev, in_own)


def _attn_fwd(P3, D):
    S3 = P3.shape[0]
    B = ATTN_BLK
    nblk = S3 // 3 // B
    npairs = D // LANES
    scale = HEAD_DIM ** -0.5

    def body(q_ref, kc_ref, vc_ref, kp_ref, vp_ref, o_ref, lse_ref):
        has_prev, _ = _class_edges(pl.program_id(0), nblk)
        ok = _band_mask(has_prev)
        lane = lax.broadcasted_iota(jnp.int32, (B, LANES), 1)
        top = lane < HEAD_DIM
        lse_acc = jnp.zeros((B, LANES), F32)
        for j in range(npairs):
            sl = slice(j * LANES, (j + 1) * LANES)
            Q = _two_heads(q_ref[:, sl] * scale, top)
            K2 = jnp.concatenate([kp_ref[:, sl], kc_ref[:, sl]], axis=0)
            V2 = jnp.concatenate([vp_ref[:, sl], vc_ref[:, sl]], axis=0)
            s = jnp.where(ok, _nt(Q, K2), NEG)
            m = jnp.max(s, axis=1, keepdims=True)
            p = jnp.exp(s - m)
            l = jnp.sum(p, axis=1, keepdims=True)
            o = _nn((p * (1.0 / l)).astype(MXU_DTYPE), V2)
            o_ref[:, sl] = jnp.where(top, o[:B], o[B:])
            lse = m + jnp.log(l)
            lse_acc = jnp.where(lane == 2 * j, lse[:B], jnp.where(lane == 2 * j + 1, lse[B:], lse_acc))
        lse_ref[...] = lse_acc

    blk = lambda part, prev: pl.BlockSpec(
        (B, D), (lambda b: (jnp.maximum(b - 1, 0), part)) if prev else (lambda b: (b, part)))
    return pl.pallas_call(
        body,
        name="attn_fwd",
        grid=(3 * nblk,),
        in_specs=[blk(0, False), blk(1, False), blk(2, False), blk(1, True), blk(2, True)],
        out_specs=[pl.BlockSpec((B, D), lambda b: (b, 0)), pl.BlockSpec((B, LANES), lambda b: (b, 0))],
        out_shape=[_sds((S3, D), F32), _sds((S3, LANES), F32)],
        compiler_params=_cparams(("parallel",)),
    )(P3, P3, P3, P3, P3)


def _attn_mix(o3, lse3, sel):
    S3, D = o3.shape
    S = S3 // 3

    def body(o3_ref, lse_ref, sel_ref, o_ref, L_ref, w_ref):
        @pl.when(pl.program_id(0) == 0)
        def _():
            w_ref[0] = lse_ref[0:S, :]
            for g, d, r, n in _class_slabs(S):
                w_ref[g, pl.ds(r, n, stride=d), :] = lse_ref[g * S + r * n:g * S + (r + 1) * n, :]
            a, b, c = w_ref[0], w_ref[1], w_ref[2]
            m = jnp.maximum(jnp.maximum(a, b), c)
            L = m + jnp.log(jnp.exp(a - m) + jnp.exp(b - m) + jnp.exp(c - m))
            L_ref[...] = L
            w_ref[0] = jnp.exp(a - L)
            w_ref[1] = jnp.exp(b - L)
            w_ref[2] = jnp.exp(c - L)

        s = sel_ref[...]
        o_ref[...] = _exact_nn(w_ref[0], s) * o3_ref[0:S, :]
        for g, d, r, n in _class_slabs(S):
            rows = pl.ds(r, n, stride=d)
            o_ref[rows, :] += _exact_nn(w_ref[g, rows, :], s) * o3_ref[g * S + r * n:g * S + (r + 1) * n, :]

    return pl.pallas_call(
        body,
        name="attn_mix",
        grid=(D // LANES,),
        in_specs=[pl.BlockSpec((S3, LANES), lambda j: (0, j)), pl.BlockSpec((S3, LANES), lambda j: (0, 0)),
                  pl.BlockSpec((LANES, LANES), lambda j: (0, j))],
        out_specs=[pl.BlockSpec((S, LANES), lambda j: (0, j)), pl.BlockSpec((S, LANES), lambda j: (0, 0))],
        out_shape=[_sds((S, D), F32), _sds((S, LANES), F32)],
        scratch_shapes=[pltpu.VMEM((3, S, LANES), F32)],
        compiler_params=_cparams(("arbitrary",)),
    )(o3, lse3, sel)


def _attn_bwd(P3, do3, L3, delta3, cos3, sin3, D):
    S3 = P3.shape[0]
    B = ATTN_BLK
    nblk = S3 // 3 // B
    npairs = D // LANES
    scale = HEAD_DIM ** -0.5

    def body(c_ref, p_ref, n_ref, doc_ref, don_ref, Lc_ref, Ln_ref, dc_ref, dn_ref, cos_ref, sin_ref, out_ref):
        has_prev, has_next = _class_edges(pl.program_id(0), nblk)
        ok = _band_mask(has_prev)
        row = lax.broadcasted_iota(jnp.int32, (2 * B, B), 0) % B
        col = lax.broadcasted_iota(jnp.int32, (2 * B, B), 1)
        ok_n = jnp.logical_and(col >= row, has_next)
        lane = lax.broadcasted_iota(jnp.int32, (B, LANES), 1)
        top = lane < HEAD_DIM
        cos_t = cos_ref[...]
        sin_inv = -sin_ref[...]
        Lc_all, Ln_all, dc_all, dn_all = Lc_ref[...], Ln_ref[...], dc_ref[...], dn_ref[...]
        pair_col = lambda t, j: jnp.concatenate([t[:, 2 * j:2 * j + 1], t[:, 2 * j + 1:2 * j + 2]], axis=0)
        for j in range(npairs):
            sl = lambda part: slice(part * D + j * LANES, part * D + (j + 1) * LANES)
            kc2, vc2 = c_ref[:, sl(1)], c_ref[:, sl(2)]
            K2 = jnp.concatenate([p_ref[:, sl(1)], kc2], axis=0)
            V2 = jnp.concatenate([p_ref[:, sl(2)], vc2], axis=0)
            Qc = _two_heads(c_ref[:, sl(0)] * scale, top)
            Qn = _two_heads(n_ref[:, sl(0)] * scale, top)
            DOc = _two_heads(doc_ref[:, j * LANES:(j + 1) * LANES].astype(MXU_DTYPE), top)
            DOn = _two_heads(don_ref[:, j * LANES:(j + 1) * LANES].astype(MXU_DTYPE), top)
            P_c = jnp.where(ok, jnp.exp(_nt(Qc, K2) - pair_col(Lc_all, j)), 0.0)
            dS_c = P_c * (_nt(DOc, V2) - pair_col(dc_all, j))
            P_n = jnp.where(ok_n, jnp.exp(_nt(Qn, kc2) - pair_col(Ln_all, j)), 0.0)
            dS_n = P_n * (_nt(DOn, vc2) - pair_col(dn_all, j))
            dq = _nn(dS_c.astype(MXU_DTYPE), K2)
            dq2 = jnp.where(top, dq[:B], dq[B:]) * scale
            Qk = jnp.concatenate([Qc, Qn], axis=0)
            DOk = jnp.concatenate([DOc, DOn], axis=0)
            dk2 = _tn(jnp.concatenate([dS_c[:, B:], dS_n], axis=0).astype(MXU_DTYPE), Qk)
            dv2 = _tn(jnp.concatenate([P_c[:, B:], P_n], axis=0).astype(MXU_DTYPE), DOk)
            out_ref[:, sl(0)] = (dq2 * cos_t + _partner(dq2) * sin_inv).astype(out_ref.dtype)
            out_ref[:, sl(1)] = (dk2 * cos_t + _partner(dk2) * sin_inv).astype(out_ref.dtype)
            out_ref[:, sl(2)] = dv2.astype(out_ref.dtype)

    cur = lambda b: b
    prv = lambda b: jnp.maximum(b - 1, 0)
    nxt = lambda b: jnp.minimum(b + 1, 3 * nblk - 1)
    spec = lambda w, f: pl.BlockSpec((B, w), lambda b: (f(b), 0))
    return pl.pallas_call(
        body,
        name="attn_bwd",
        grid=(3 * nblk,),
        in_specs=[spec(3 * D, cur), spec(3 * D, prv), spec(3 * D, nxt), spec(D, cur), spec(D, nxt),
                  spec(LANES, cur), spec(LANES, nxt), spec(LANES, cur), spec(LANES, nxt), spec(LANES, cur), spec(LANES, cur)],
        out_specs=spec(3 * D, cur),
        out_shape=_sds((S3, 3 * D), MXU_DTYPE),
        compiler_params=_cparams(("parallel",)),
    )(P3, P3, P3, do3, do3, L3, L3, delta3, delta3, cos3, sin3)


def _input_grad(du, dx3):
    S, D = du.shape

    def body(du_ref, dx_ref, o_ref):
        o_ref[...] = ALPHA * du_ref[...] + dx_ref[0:S, :]
        for g, d, r, n in _class_slabs(S):
            o_ref[pl.ds(r, n, stride=d), :] += dx_ref[g * S + r * n:g * S + (r + 1) * n, :]

    return pl.pallas_call(
        body,
        name="input_grad",
        grid=(D // LANES,),
        in_specs=[pl.BlockSpec((S, LANES), lambda j: (0, j)), pl.BlockSpec((3 * S, LANES), lambda j: (0, j))],
        out_specs=pl.BlockSpec((S, LANES), lambda j: (0, j)),
        out_shape=_sds((S, D), F32),
        compiler_params=_cparams(("parallel",)),
    )(du, dx3)


def _chunk_causal(tb):
    r = lax.broadcasted_iota(jnp.int32, (tb, tb), 0)
    c = lax.broadcasted_iota(jnp.int32, (tb, tb), 1)
    return jnp.logical_and((r // HGRN_CHUNK) == (c // HGRN_CHUNK), r >= c)


def _chunk_sums(a, lower):
    C = HGRN_CHUNK
    r = lax.broadcasted_iota(jnp.int32, (C, C), 0)
    c = lax.broadcasted_iota(jnp.int32, (C, C), 1)
    tri = ((r >= c) if lower else (r <= c)).astype(BF16)
    parts = _split3(a)
    out = []
    for ci in range(a.shape[0] // C):
        rows = slice(ci * C, (ci + 1) * C)
        out.append(_nn(tri, parts[0][rows]) + _nn(tri, parts[1][rows]) + _nn(tri, parts[2][rows]))
    return jnp.concatenate(out, axis=0)


def _chunk_last(b):
    C = HGRN_CHUNK
    return jnp.concatenate([jnp.broadcast_to(b[(ci + 1) * C - 1:(ci + 1) * C, :], (C, b.shape[1]))
                            for ci in range(b.shape[0] // C)], axis=0)


def _lower_bound(lb_ref):
    l0, l1 = lb_ref[0:1, :], lb_ref[1:2, :]
    m = jnp.maximum(l0, l1)
    e0, e1 = jnp.exp(l0 - m), jnp.exp(l1 - m)
    return e1 / (e0 + e1)


def _hgrn_gates(q_raw, z, lb):
    sg = 1.0 / (1.0 + jnp.exp(-z))
    sn = 1.0 / (1.0 + jnp.exp(z))
    f = lb + (1.0 - lb) * sg
    key = (1.0 - lb) * sn
    sq = 1.0 / (1.0 + jnp.exp(-q_raw))
    return sg, sn, f, key, sq


HGRN_HEADS_PER_STEP = 2


def _hgrn_fwd(P1, lb_logits, norm_g, tb):
    S = P1.shape[0]
    D = P1.shape[1] // 3
    K = HGRN_DK
    H = D // K
    HP = HGRN_HEADS_PER_STEP
    C = HGRN_CHUNK
    cpb = tb // C
    nt = S // tb

    def body(q_ref, f_ref, i_ref, lb_ref, g_ref, o_ref, n_ref, st_ref, state):
        t = pl.program_id(1)

        @pl.when(t == 0)
        def _():
            state[...] = jnp.zeros_like(state)

        lb_all = _lower_bound(lb_ref)
        low = _chunk_causal(tb)
        for hh in range(HP):
            lanes = slice(hh * K, (hh + 1) * K)
            q_raw, z, v = q_ref[:, lanes], f_ref[:, lanes], i_ref[:, lanes]
            sg, sn, f, key, sq = _hgrn_gates(q_raw, z, lb_all[:, lanes])
            b = _chunk_sums(jnp.log(f), lower=True)
            qd = (q_raw * sq * jnp.exp(b)).astype(MXU_DTYPE)
            kd = (key * jnp.exp(-b)).astype(MXU_DTYPE)
            kb = (key * jnp.exp(_chunk_last(b) - b)).astype(MXU_DTYPE)
            vm = v.astype(MXU_DTYPE)
            a = jnp.where(low, _nt(qd, kd), 0.0).astype(MXU_DTYPE)
            o_intra = _nn(a, vm)
            st = state[hh]
            outs = []
            for ci in range(cpb):
                rows = slice(ci * C, (ci + 1) * C)
                st_ref[hh, ci] = st
                outs.append(o_intra[rows] + _nt(qd[rows], st.astype(MXU_DTYPE)))
                st = st * jnp.exp(b[(ci + 1) * C - 1:(ci + 1) * C, :]) + _tn(vm[rows], kb[rows])
            state[hh] = st
            o = jnp.concatenate(outs, axis=0)
            o_ref[:, lanes] = o
            rs = lax.rsqrt(jnp.mean(o * o, axis=1, keepdims=True) + RMS_EPS)
            n_ref[:, lanes] = o * rs * g_ref[:, lanes]

    tok = lambda part: pl.BlockSpec((tb, HP * K), lambda h, t: (t, part * (H // HP) + h))
    vec = lambda rows: pl.BlockSpec((rows, HP * K), lambda h, t: (0, h))
    return pl.pallas_call(
        body,
        name="hgrn_fwd",
        grid=(H // HP, nt),
        in_specs=[tok(0), tok(1), tok(2), vec(2), vec(1)],
        out_specs=[tok(0), tok(0), pl.BlockSpec((HP, cpb, K, K), lambda h, t: (h, t, 0, 0))],
        out_shape=[_sds((S, D), F32), _sds((S, D), F32), _sds((H, S // C, K, K), F32)],
        scratch_shapes=[pltpu.VMEM((HP, K, K), F32)],
        compiler_params=_cparams(("parallel", "arbitrary")),
    )(P1, P1, P1, lb_logits, norm_g)


def _hgrn_bwd(P1, o_pre, states, dn, lb_logits, norm_g, tb):
    S = P1.shape[0]
    D = P1.shape[1] // 3
    K = HGRN_DK
    H = D // K
    HP = HGRN_HEADS_PER_STEP
    C = HGRN_CHUNK
    cpb = tb // C
    nt = S // tb

    def body(q_ref, f_ref, i_ref, o_ref, st_ref, dn_ref, lb_ref, g_ref, dq_ref, dz_ref, dv_ref, dg_ref, dlb_ref, dstate):
        t = pl.program_id(1)

        @pl.when(t == 0)
        def _():
            dstate[...] = jnp.zeros_like(dstate)
            dg_ref[...] = jnp.zeros_like(dg_ref)
            dlb_ref[...] = jnp.zeros_like(dlb_ref)

        lb_all = _lower_bound(lb_ref)
        low = _chunk_causal(tb)
        for hh in range(HP):
            lanes = slice(hh * K, (hh + 1) * K)
            lb = lb_all[:, lanes]
            gn = g_ref[:, lanes]
            q_raw, z, v = q_ref[:, lanes], f_ref[:, lanes], i_ref[:, lanes]
            sg, sn, f, key, sq = _hgrn_gates(q_raw, z, lb)
            b = _chunk_sums(jnp.log(f), lower=True)
            e_pos, e_neg, e_rel = jnp.exp(b), jnp.exp(-b), jnp.exp(_chunk_last(b) - b)
            qd_f, kd_f, kb_f = q_raw * sq * e_pos, key * e_neg, key * e_rel
            qd, kd, kb = qd_f.astype(MXU_DTYPE), kd_f.astype(MXU_DTYPE), kb_f.astype(MXU_DTYPE)
            vm = v.astype(MXU_DTYPE)
            a = jnp.where(low, _nt(qd, kd), 0.0).astype(MXU_DTYPE)
            o = o_ref[:, lanes]
            dnn = dn_ref[:, lanes]
            rs = lax.rsqrt(jnp.mean(o * o, axis=1, keepdims=True) + RMS_EPS)
            dg_ref[:, lanes] += jnp.sum(dnn * o * rs, axis=0, keepdims=True)
            tg = dnn * gn
            dom = (rs * tg - o * (rs * rs * rs) * jnp.mean(tg * o, axis=1, keepdims=True)).astype(MXU_DTYPE)
            da = jnp.where(low, _nt(dom, vm), 0.0).astype(MXU_DTYPE)
            dv = _tn(a, dom)
            dqd = _nn(da, kd)
            dkd = _tn(da, qd)
            dst = dstate[hh]
            dv_s, dqd_s, dkb_s, dbl_s = [None] * cpb, [None] * cpb, [None] * cpb, [None] * cpb
            for ci in reversed(range(cpb)):
                rows = slice(ci * C, (ci + 1) * C)
                st = st_ref[hh, ci]
                dstm = dst.astype(MXU_DTYPE)
                dec = jnp.exp(b[(ci + 1) * C - 1:(ci + 1) * C, :])
                dv_s[ci] = _nt(kb[rows], dstm)
                dkb_s[ci] = _nn(vm[rows], dstm)
                dqd_s[ci] = _nn(dom[rows], st.astype(MXU_DTYPE))
                db_last = jnp.sum(dkb_s[ci] * kb_f[rows], axis=0, keepdims=True) + jnp.sum(dst * st, axis=0, keepdims=True) * dec
                dbl_s[ci] = jnp.broadcast_to(db_last, (C, K))
                dst = dst * dec + _tn(dom[rows], qd[rows])
            dstate[hh] = dst
            dv = dv + jnp.concatenate(dv_s, axis=0)
            dqd = dqd + jnp.concatenate(dqd_s, axis=0)
            dkb = jnp.concatenate(dkb_s, axis=0)
            dkey = dkd * e_neg + dkb * e_rel
            db = dqd * qd_f - dkd * kd_f - dkb * kb_f
            dlogf = _chunk_sums(db, lower=False) + jnp.concatenate(dbl_s, axis=0)
            gz = (1.0 - lb) * sg * sn
            dz_ref[:, lanes] = (dlogf * gz / f - dkey * gz).astype(dz_ref.dtype)
            dlb_ref[:, lanes] += jnp.sum(dlogf * sn / f - dkey * sn, axis=0, keepdims=True)
            dq_ref[:, lanes] = (dqd * e_pos * (sq + q_raw * sq * (1.0 - sq))).astype(dq_ref.dtype)
            dv_ref[:, lanes] = dv.astype(dv_ref.dtype)

    rev = lambda t: nt - 1 - t
    tok = lambda part: pl.BlockSpec((tb, HP * K), lambda h, t: (rev(t), part * (H // HP) + h))
    vec = lambda rows: pl.BlockSpec((rows, HP * K), lambda h, t: (0, h))
    outs = pl.pallas_call(
        body,
        name="hgrn_bwd",
        grid=(H // HP, nt),
        in_specs=[tok(0), tok(1), tok(2), tok(0),
                  pl.BlockSpec((HP, cpb, K, K), lambda h, t: (h, rev(t), 0, 0)),
                  tok(0), vec(2), vec(1)],
        out_specs=[tok(0), tok(0), tok(0), vec(1), vec(1)],
        out_shape=[_sds((S, D), MXU_DTYPE)] * 3 + [_sds((1, D), F32)] * 2,
        scratch_shapes=[pltpu.VMEM((HP, K, K), F32)],
        compiler_params=_cparams(("parallel", "arbitrary")),
    )(P1, P1, P1, o_pre, states, dn, lb_logits, norm_g)
    return outs


def _lb_logits_grad(dlb, lb_logits):
    def body(d_ref, l_ref, o_ref):
        s1 = _lower_bound(l_ref)
        d = d_ref[...]
        o_ref[0:1, :] = -(1.0 - s1) * s1 * d
        o_ref[1:2, :] = s1 * (1.0 - s1) * d

    return pl.pallas_call(body, name="lb_logits_grad", out_shape=_sds(lb_logits.shape, F32))(dlb, lb_logits)


def _ln_epilogue(acc, extra_refs, out_refs, j):
    res_ref, g_ref, b_ref = extra_refs
    x_ref, xhat_ref, rstd_ref = out_refs
    u = ALPHA * res_ref[...] + acc
    mu = jnp.mean(u, axis=1, keepdims=True)
    cen = u - mu
    rstd = lax.rsqrt(jnp.mean(cen * cen, axis=1, keepdims=True) + LN_EPS)
    xhat = cen * rstd
    xhat_ref[...] = xhat
    x_ref[...] = xhat * g_ref[...] + b_ref[...]
    rstd_ref[...] = rstd


def _mm_res_ln(name, a, w_full, res, g, b, tm, tk):
    S, D = res.shape
    row = pl.BlockSpec((tm, D), lambda i, j, k: (i, 0))
    vec = pl.BlockSpec((1, D), lambda i, j, k: (0, 0))
    outs = [(_sds((S, D), F32), row), (_sds((S, D), F32), row),
            (_sds((S, 1), F32), pl.BlockSpec((tm, 1), lambda i, j, k: (i, 0)))]
    return _matmul(name, a, w_full, "nn", tm, D, tk, outs, _ln_epilogue, extras=[(res, row), (g, vec), (b, vec)])


def _ln_bwd(name, dy, xhat, rstd, g, tm, dep):
    S, D = dy.shape

    def body(dy_ref, xh_ref, r_ref, g_ref, dep_ref, du_ref, dg_ref, db_ref):
        @pl.when(pl.program_id(0) == 0)
        def _():
            dg_ref[...] = jnp.zeros_like(dg_ref)
            db_ref[...] = jnp.zeros_like(db_ref)

        dy_, xh = dy_ref[...], xh_ref[...]
        dg_ref[...] += jnp.sum(dy_ * xh, axis=0, keepdims=True)
        db_ref[...] += jnp.sum(dy_, axis=0, keepdims=True)
        dxh = dy_ * g_ref[...]
        m1 = jnp.mean(dxh, axis=1, keepdims=True)
        m2 = jnp.mean(dxh * xh, axis=1, keepdims=True)
        du_ref[...] = r_ref[...] * (dxh - m1 - xh * m2)

    row = pl.BlockSpec((tm, D), lambda i: (i, 0))
    vec = pl.BlockSpec((1, D), lambda i: (0, 0))
    return pl.pallas_call(
        body,
        name=name,
        grid=(S // tm,),
        in_specs=[row, row, pl.BlockSpec((tm, 1), lambda i: (i, 0)), vec, pl.BlockSpec(memory_space=pl.ANY)],
        out_specs=[row, vec, vec],
        out_shape=[_sds((S, D), F32), _sds((1, D), F32), _sds((1, D), F32)],
        compiler_params=_cparams(("arbitrary",)),
    )(dy, xhat, rstd, g, dep)


def _loss_head(y, target, tm):
    S, D = y.shape

    def body(y_ref, t_ref, sq_ref, dy_ref):
        @pl.when(pl.program_id(0) == 0)
        def _():
            sq_ref[...] = jnp.zeros_like(sq_ref)

        e = y_ref[...] - t_ref[...]
        sq_ref[...] += jnp.sum(e * e, axis=0, keepdims=True)
        dy_ref[...] = e / D

    row = pl.BlockSpec((tm, D), lambda i: (i, 0))
    vec = pl.BlockSpec((1, D), lambda i: (0, 0))
    return pl.pallas_call(
        body,
        name="loss_head",
        grid=(S // tm,),
        in_specs=[row, row],
        out_specs=[vec, row],
        out_shape=[_sds((1, D), F32), _sds((S, D), F32)],
        compiler_params=_cparams(("arbitrary",)),
    )(y, target)


def _mlp_up(name, x, w_up, tm, tn, tk):
    S = x.shape[0]
    F = w_up.shape[1]

    def epilogue(acc, extra_refs, out_refs, j):
        r = jnp.maximum(acc, 0.0)
        out_refs[0][...] = (r * r).astype(out_refs[0].dtype)

    return _matmul(name, x, w_up, "nn", tm, tn, tk, [(_sds((S, F), MXU_DTYPE), _ij_spec(tm, tn))], epilogue)[0]


def _mlp_down_bwd(name, dy, w_down, a, tm, tn, tk):
    S, F = a.shape

    def epilogue(acc, extra_refs, out_refs, j):
        out_refs[0][...] = (acc * (2.0 * jnp.sqrt(extra_refs[0][...].astype(F32)))).astype(out_refs[0].dtype)

    return _matmul(name, dy, w_down, "nt", tm, tn, tk, [(_sds((S, F), MXU_DTYPE), _ij_spec(tm, tn))], epilogue,
                   extras=[(a, _ij_spec(tm, tn))])[0]


def _mm_nt_res(name, dy, w, du, tm, tn, tk):
    S = dy.shape[0]
    N = w.shape[0]

    def epilogue(acc, extra_refs, out_refs, j):
        out_refs[0][...] = ALPHA * extra_refs[0][...] + acc

    return _matmul(name, dy, w, "nt", tm, tn, tk, [(_sds((S, N), F32), _ij_spec(tm, tn))], epilogue,
                   extras=[(du, _ij_spec(tm, tn))])[0]


def _attn_out_bwd(du, w_out, o, sel_t, tm, tk):
    S, D = o.shape

    def epilogue(acc, extra_refs, out_refs, j):
        out_refs[0][...] = acc
        out_refs[1][...] = _exact_nn(acc * extra_refs[0][...], extra_refs[1][...])

    row = pl.BlockSpec((tm, D), lambda i, j, k: (i, 0))
    slim = pl.BlockSpec((tm, LANES), lambda i, j, k: (i, 0))
    return _matmul("attn_out_bwd", du, w_out, "nt", tm, D, tk,
                   [(_sds((S, D), F32), row), (_sds((S, LANES), F32), slim)], epilogue,
                   extras=[(o, row), (sel_t, pl.BlockSpec((D, LANES), lambda i, j, k: (0, 0)))])


def _adamw(name, w, g, m, v):
    shape = w.shape
    cols = shape[-1]
    rows = math.prod(shape[:-1])
    w2, g2, m2, v2 = (t.reshape(rows, cols) for t in (w, g, m, v))
    tr = _pick(rows, (256, 128, 64, 32, 16, 8))
    c1 = 1.0 - ADAM_B1 ** ADAM_STEP
    c2 = 1.0 - ADAM_B2 ** ADAM_STEP

    def body(w_ref, g_ref, m_ref, v_ref, d_ref, nm_ref, nv_ref):
        gg = g_ref[...]
        nm = ADAM_B1 * m_ref[...] + (1.0 - ADAM_B1) * gg
        nv = ADAM_B2 * v_ref[...] + (1.0 - ADAM_B2) * (gg * gg)
        nm_ref[...] = nm
        nv_ref[...] = nv
        d_ref[...] = -ADAM_LR * ((nm / c1) / (jnp.sqrt(nv / c2) + ADAM_EPS) + ADAM_WD * w_ref[...])

    blk = pl.BlockSpec((tr, cols), lambda i: (i, 0))
    outs = pl.pallas_call(
        body,
        name=name,
        grid=(rows // tr,),
        in_specs=[blk] * 4,
        out_specs=[blk] * 3,
        out_shape=[_sds((rows, cols), F32)] * 3,
        compiler_params=_cparams(("parallel",)),
    )(w2, g2, m2, v2)
    return tuple(o.reshape(shape) for o in outs)


HBM = pl.BlockSpec(memory_space=pl.ANY)


def _shard_slice(ref, axis, size, index):
    idx = [slice(None)] * len(ref.shape)
    idx[axis] = pl.ds(pl.multiple_of(index * size, 8), size)
    return ref.at[tuple(idx)]


IN_HBM = pl.BlockSpec(memory_space=pltpu.HBM)
IN_SEM = pl.BlockSpec(memory_space=pltpu.SEMAPHORE)
DATAFLOW = pltpu.SideEffectType.DATAFLOW_SIDE_EFFECTING


def _hbm(t):
    return pltpu.with_memory_space_constraint(t, pltpu.HBM)


def _token_spec():
    return pl.BlockSpec(memory_space=pltpu.VMEM)


def _gather_copies(s_refs, f_refs, axes, send, recv, loc, arrival):
    x, y, c = lax.axis_index("x"), lax.axis_index("y"), lax.axis_index("c")
    chips = [(1 - x, y), (x, 1 - y), (1 - x, 1 - y)]
    local, remote = [], []
    for a in range(len(s_refs)):
        size = s_refs[a].shape[axes[a]]
        local.append(pltpu.make_async_copy(s_refs[a], _shard_slice(f_refs[a], axes[a], size, 2 * x + y), loc.at[a]))
        for k, (px, py) in enumerate(chips):
            block = (2 * px + py) if arrival else (2 * x + y)
            remote.append(pltpu.make_async_remote_copy(
                src_ref=s_refs[a], dst_ref=_shard_slice(f_refs[a], axes[a], size, block), send_sem=send.at[3 * a + k],
                recv_sem=recv.at[3 * a + k], device_id=(px, py, c), device_id_type=MESH))
    return local, remote


def _gather_start(name, shards, axes, after):
    n = len(shards)
    fulls = []
    for s, ax in zip(shards, axes):
        fs = list(s.shape)
        fs[ax] *= 4
        fulls.append(lax.empty(tuple(fs), s.dtype))

    def body(*refs):
        s_refs, f_refs = refs[:n], refs[n:2 * n]
        send, recv, loc, token = refs[2 * n + 1], refs[2 * n + 2], refs[2 * n + 3], refs[-1]
        local, remote = _gather_copies(s_refs, f_refs, axes, send, recv, loc, arrival=False)
        for cp in remote + local:
            cp.start()
        token[...] = jnp.zeros_like(token)

    outs = pl.pallas_call(
        body,
        name=name,
        out_shape=(pltpu.SemaphoreType.DMA((3 * n,)), pltpu.SemaphoreType.DMA((3 * n,)), pltpu.SemaphoreType.DMA((n,)),
                   *[pltpu.HBM(t.shape, t.dtype) for t in shards + fulls], _sds((8, LANES), F32)),
        in_specs=[IN_HBM] * (2 * n) + [HBM],
        out_specs=(IN_SEM, IN_SEM, IN_SEM, *[IN_HBM] * (2 * n), _token_spec()),
        input_output_aliases={i: 3 + i for i in range(2 * n)},
        compiler_params=pltpu.CompilerParams(has_side_effects=DATAFLOW),
    )(*[_hbm(t) for t in shards + fulls], after)
    return (outs[0], outs[1], outs[2], list(outs[3:3 + n]), list(outs[3 + n:3 + 2 * n]), axes), outs[-1]


def _gather_wait(name, state, *after):
    send, recv, loc, s_thru, f_thru, axes = state
    n = len(s_thru)

    def body(*refs):
        s_refs, f_refs = refs[:n], refs[n:2 * n]
        local, remote = _gather_copies(s_refs, f_refs, axes, refs[2 * n], refs[2 * n + 1], refs[2 * n + 2], arrival=True)
        for cp in local:
            cp.wait()
        for cp in remote:
            cp.wait_send()
            cp.wait_recv()

    outs = pl.pallas_call(
        body,
        name=name,
        out_shape=tuple(pltpu.HBM(t.shape, t.dtype) for t in s_thru + f_thru),
        in_specs=[IN_HBM] * (2 * n) + [IN_SEM, IN_SEM, IN_SEM] + [HBM] * len(after),
        out_specs=tuple([IN_HBM] * (2 * n)),
        input_output_aliases={i: i for i in range(2 * n)},
        compiler_params=pltpu.CompilerParams(has_side_effects=DATAFLOW),
    )(*s_thru, *f_thru, send, recv, loc, *after)
    return list(outs[n:2 * n])


FLIPS = [(fx, fy, fc) for fx in (0, 1) for fy in (0, 1) for fc in (0, 1)][1:]


def _piece_shape(shape, axis):
    ps = list(shape)
    if axis == 0:
        ps[0] //= 8
    else:
        ps[0] //= 2
        ps[axis] //= 4
    return tuple(ps)


def _piece(ref, axis, q, c):
    shape = ref.shape
    idx = [slice(None)] * len(shape)
    if axis == 0:
        h = shape[0] // 8
        idx[0] = pl.ds(pl.multiple_of((2 * q + c) * h, 8), h)
    else:
        h, w = shape[0] // 2, shape[axis] // 4
        idx[0] = pl.ds(c * h, h)
        idx[axis] = pl.ds(pl.multiple_of(q * w, LANES if axis == len(shape) - 1 else 8), w)
    return ref.at[tuple(idx)]


def _own_piece(g, axis):
    ps = _piece_shape(g.shape, axis)
    q, c = 2 * lax.axis_index("x") + lax.axis_index("y"), lax.axis_index("c")
    start = [0] * len(ps)
    if axis == 0:
        start[0] = (2 * q + c) * ps[0]
    else:
        start[0] = c * ps[0]
        start[axis] = q * ps[axis]
    return lax.dynamic_slice(g, start, ps)


def _scatter_copies(g_refs, l_refs, axes, send, recv):
    x, y, c = lax.axis_index("x"), lax.axis_index("y"), lax.axis_index("c")
    out = []
    for a in range(len(g_refs)):
        for k, (fx, fy, fc) in enumerate(FLIPS):
            tx, ty, tc = x ^ fx, y ^ fy, c ^ fc
            out.append(pltpu.make_async_remote_copy(
                src_ref=_piece(g_refs[a], axes[a], 2 * tx + ty, tc), dst_ref=l_refs[a].at[k],
                send_sem=send.at[7 * a + k], recv_sem=recv.at[7 * a + k], device_id=(tx, ty, tc), device_id_type=MESH))
    return out


def _scatter_start(name, grads, axes):
    n = len(grads)
    lands = [lax.empty((7,) + _piece_shape(g.shape, ax), g.dtype) for g, ax in zip(grads, axes)]

    def body(*refs):
        g_refs, l_refs = refs[:n], refs[n:2 * n]
        send, recv, token = refs[2 * n], refs[2 * n + 1], refs[-1]
        for cp in _scatter_copies(g_refs, l_refs, axes, send, recv):
            cp.start()
        token[...] = jnp.zeros_like(token)

    outs = pl.pallas_call(
        body,
        name=name,
        out_shape=(pltpu.SemaphoreType.DMA((7 * n,)), pltpu.SemaphoreType.DMA((7 * n,)),
                   *[pltpu.HBM(t.shape, t.dtype) for t in grads + lands], _sds((8, LANES), F32)),
        in_specs=[IN_HBM] * (2 * n),
        out_specs=(IN_SEM, IN_SEM, *[IN_HBM] * (2 * n), _token_spec()),
        input_output_aliases={i: 2 + i for i in range(2 * n)},
        compiler_params=pltpu.CompilerParams(has_side_effects=DATAFLOW),
    )(*[_hbm(t) for t in grads + lands])
    return (outs[0], outs[1], list(outs[2:2 + n]), list(outs[2 + n:2 + 2 * n]), axes), outs[-1]


def _scatter_wait(name, state, after):
    send, recv, g_thru, l_thru, axes = state
    n = len(g_thru)

    def body(*refs):
        g_refs, l_refs = refs[:n], refs[n:2 * n]
        for cp in _scatter_copies(g_refs, l_refs, axes, refs[2 * n], refs[2 * n + 1]):
            cp.wait_send()
            cp.wait_recv()

    outs = pl.pallas_call(
        body,
        name=name,
        out_shape=tuple(pltpu.HBM(t.shape, t.dtype) for t in g_thru + l_thru),
        in_specs=[IN_HBM] * (2 * n) + [IN_SEM, IN_SEM, HBM],
        out_specs=tuple([IN_HBM] * (2 * n)),
        input_output_aliases={i: i for i in range(2 * n)},
        compiler_params=pltpu.CompilerParams(has_side_effects=DATAFLOW),
    )(*g_thru, *l_thru, send, recv, after)
    return list(outs[:n]), list(outs[n:2 * n])


def _reduce_join(name, landing, own):
    piece = own.shape
    C = piece[-1]
    R = math.prod(piece[:-1])
    l3 = landing.reshape(7, R, C)
    own2 = own.reshape(R, C)
    tr = _pick(R, [t for t in (512, 256, 128, 64, 32, 16, 8) if t * C <= 256 * 1024])
    nsteps = R // tr

    def body(own_ref, l_ref, o_ref, buf, send, loc, recv):
        i = pl.program_id(0)
        x, y, c = lax.axis_index("x"), lax.axis_index("y"), lax.axis_index("c")
        sibling = (x, y, 1 - c)

        def copies(slot, step):
            dst = o_ref.at[pl.ds(pl.multiple_of(c * R + step * tr, 8), tr), :]
            return (pltpu.make_async_copy(buf.at[slot], dst, loc.at[slot]),
                    pltpu.make_async_remote_copy(src_ref=buf.at[slot], dst_ref=dst, send_sem=send.at[slot], recv_sem=recv,
                                                 device_id=sibling, device_id_type=MESH))

        @pl.when(i >= 2)
        def _():
            lc, rc = copies(i % 2, i - 2)
            lc.wait()
            rc.wait_send()

        acc = own_ref[...].astype(F32)
        for s in range(7):
            acc = acc + l_ref[s].astype(F32)
        buf[i % 2] = acc
        lc, rc = copies(i % 2, i)
        lc.start()
        rc.start()

        @pl.when(i == nsteps - 1)
        def _():
            for st in range(max(nsteps - 2, 0), nsteps):
                lc, rc = copies(st % 2, st)
                lc.wait()
                rc.wait_send()
            theirs = o_ref.at[pl.ds(pl.multiple_of((1 - c) * R, 8), R), :]
            pltpu.make_async_remote_copy(src_ref=theirs, dst_ref=theirs, send_sem=send.at[0], recv_sem=recv,
                                         device_id=sibling, device_id_type=MESH).wait_recv()

    return pl.pallas_call(
        body,
        name=name,
        grid=(nsteps,),
        in_specs=[pl.BlockSpec((tr, C), lambda i: (i, 0)), pl.BlockSpec((7, tr, C), lambda i: (0, i, 0))],
        out_specs=HBM,
        out_shape=_sds((2 * R, C), F32),
        scratch_shapes=[pltpu.VMEM((2, tr, C), F32), pltpu.SemaphoreType.DMA((2,)), pltpu.SemaphoreType.DMA((2,)),
                        pltpu.SemaphoreType.DMA(())],
        compiler_params=_cparams(("arbitrary",)),
    )(own2, l3)


def _all_reduce_small(v, dep):
    R, D = v.shape

    def body(v_ref, dep_ref, o_ref, land, send, recv):
        x, y, c = lax.axis_index("x"), lax.axis_index("y"), lax.axis_index("c")
        my_slot = 4 * x + 2 * y + c
        land[my_slot] = v_ref[...]
        for k, (fx, fy, fc) in enumerate(FLIPS):
            tx, ty, tc = x ^ fx, y ^ fy, c ^ fc
            pltpu.make_async_remote_copy(src_ref=v_ref, dst_ref=land.at[my_slot], send_sem=send.at[k], recv_sem=recv.at[k],
                                         device_id=(tx, ty, tc), device_id_type=MESH).start()
        for k, (fx, fy, fc) in enumerate(FLIPS):
            tx, ty, tc = x ^ fx, y ^ fy, c ^ fc
            cp = pltpu.make_async_remote_copy(src_ref=v_ref, dst_ref=land.at[4 * tx + 2 * ty + tc], send_sem=send.at[k],
                                              recv_sem=recv.at[k], device_id=(tx, ty, tc), device_id_type=MESH)
            cp.wait_send()
            cp.wait_recv()
        acc = land[0]
        for s in range(1, 8):
            acc = acc + land[s]
        o_ref[...] = acc

    return pl.pallas_call(
        body,
        name="all_reduce_small",
        in_specs=[pl.BlockSpec(memory_space=pltpu.VMEM), pl.BlockSpec(memory_space=pl.ANY)],
        out_specs=pl.BlockSpec(memory_space=pltpu.VMEM),
        out_shape=_sds((R, D), F32),
        scratch_shapes=[pltpu.VMEM((8, R, D), F32), pltpu.SemaphoreType.DMA((7,)), pltpu.SemaphoreType.DMA((7,))],
    )(v, dep)


def kernel(x, attn_w_in, attn_w_out, hgrn_w_in, hgrn_w_out, hgrn_norm_g, lb_logits, ln_mix_g, ln_mix_b, ln_ffn_g, ln_ffn_b, ffn_w_up, ffn_w_down, loss_target, m_attn_w_in, m_attn_w_out, m_hgrn_w_in, m_hgrn_w_out, m_hgrn_norm_g, m_lb_logits, m_ln_mix_g, m_ln_mix_b, m_ln_ffn_g, m_ln_ffn_b, m_ffn_w_up, m_ffn_w_down, v_attn_w_in, v_attn_w_out, v_hgrn_w_in, v_hgrn_w_out, v_hgrn_norm_g, v_lb_logits, v_ln_mix_g, v_ln_mix_b, v_ln_ffn_g, v_ln_ffn_b, v_ffn_w_up, v_ffn_w_down):
    xs = x[0]
    tgt = loss_target[0]
    S, D = xs.shape
    F = ffn_w_up.shape[2] * 4
    T1 = _pick(S, (1024, 512, 256))
    T2 = _pick(S, (2048, 1024, 512))
    TH = _pick(S, (512, 256))
    TB = _pick(S, (512, 256))
    TN = _pick(D, (512, 256, 128))
    TF = _pick(F, (1024, 512))
    TG = _pick(3 * D, (1536, 1024, 768))
    TW = _pick(F, (2048, 1024))

    cast = lambda w: w.astype(MXU_DTYPE)
    st_a, tok = _gather_start("gather_a", [cast(attn_w_in[0])], [1], jnp.zeros((8, LANES), F32))
    tok, (xs_late, w_aout, w_fup, w_fdown, w_hin, w_hout) = lax.optimization_barrier(
        (tok, (xs, attn_w_out, ffn_w_up, ffn_w_down, hgrn_w_in, hgrn_w_out)))
    st_b, tok = _gather_start("gather_b", [cast(w_aout[0]), cast(w_fup[0]), cast(w_fdown[0])], [0, 1, 0], tok)
    st_c, tok = _gather_start("gather_c", [cast(w_hin[0]), cast(w_hout[0]), hgrn_norm_g, cast(w_fup[1]), cast(w_fdown[1])],
                              [1, 0, 1, 1, 0], tok)

    cos3, sin3 = _rope_tables(S)
    sel = _head_sel(D)
    sel_t = sel.T

    xc3 = _stack_classes("x_classes", xs_late, MXU_DTYPE)
    (wa_in,) = _gather_wait("gather_a_wait", st_a, tok, xc3, cos3, sin3)
    P3 = _attn_proj(xc3, wa_in, cos3, sin3, T2, TN)
    o3, lse3 = _attn_fwd(P3, D)
    o_att, L_att = _attn_mix(o3, lse3, sel)
    wa_out, w_up0, w_down0 = _gather_wait("gather_b_wait", st_b, L_att)
    x1, xh1, r1 = _mm_res_ln("attn_out_ln", o_att, wa_out, xs, ln_mix_g[0:1], ln_mix_b[0:1], TH, D)
    a0 = _mlp_up("mlp0_up", x1, w_up0, T1, TF, D)
    x2, xh2, r2 = _mm_res_ln("mlp0_down_ln", a0, w_down0, x1, ln_ffn_g[0:1], ln_ffn_b[0:1], TH, F)

    wh_in, wh_out, norm_g, w_up1, w_down1 = _gather_wait("gather_c_wait", st_c, r2)
    P1 = _plain_mm("hgrn_proj", x2, wh_in, "nn", F32, T1, _pick(3 * D, (1024, 768, 512)), D)
    o_h, n_h, states = _hgrn_fwd(P1, lb_logits, norm_g, TB)
    x3, xh3, r3 = _mm_res_ln("hgrn_out_ln", n_h, wh_out, x2, ln_mix_g[1:2], ln_mix_b[1:2], TH, D)
    a1 = _mlp_up("mlp1_up", x3, w_up1, T1, TF, D)
    x4, xh4, r4 = _mm_res_ln("mlp1_down_ln", a1, w_down1, x3, ln_ffn_g[1:2], ln_ffn_b[1:2], TH, F)

    sq, dx4 = _loss_head(x4, tgt, TH)

    wgrad = lambda name, a, dy, tm, tn: _plain_mm(name, a, dy, "tn", MXU_DTYPE, tm, tn, T1)
    du4, dg_ffn1, db_ffn1 = _ln_bwd("ln_ffn1_bwd", dx4, xh4, r4, ln_ffn_g[1:2], TH, sq)
    dh1 = _mlp_down_bwd("mlp1_down_bwd", du4, w_down1, a1, T1, TF, D)
    g_down1 = wgrad("g_down1", a1, du4, TW, D)
    dx3 = _mm_nt_res("mlp1_up_bwd", dh1, w_up1, du4, TH, D, F)
    g_up1 = wgrad("g_up1", x3, dh1, D, TW)
    sc_1, tok = _scatter_start("scatter_1", [g_down1, g_up1], [0, 1])
    du3, dg_mix1, db_mix1 = _ln_bwd("ln_mix1_bwd", dx3, xh3, r3, ln_mix_g[1:2], TH, tok)
    dn = _plain_mm("hgrn_out_bwd", du3, wh_out, "nt", F32, T1, D, D)
    g_hout = wgrad("g_hgrn_out", n_h, du3, D, D)
    dq_raw, dz, dv, dg_norm, dlb = _hgrn_bwd(P1, o_h, states, dn, lb_logits, norm_g, TB)
    dP1 = jnp.concatenate([dq_raw, dz, dv], axis=1)
    dx2 = _mm_nt_res("hgrn_in_bwd", dP1, wh_in, du3, TH, D, 3 * D)
    g_hin = wgrad("g_hgrn_in", x2, dP1, D, TG)
    d_lb_logits = _lb_logits_grad(dlb, lb_logits)
    sc_2, tok = _scatter_start("scatter_2", [g_hout, g_hin], [0, 1])

    du2, dg_ffn0, db_ffn0 = _ln_bwd("ln_ffn0_bwd", dx2, xh2, r2, ln_ffn_g[0:1], TH, tok)
    dh0 = _mlp_down_bwd("mlp0_down_bwd", du2, w_down0, a0, T1, TF, D)
    g_down0 = wgrad("g_down0", a0, du2, TW, D)
    dx1 = _mm_nt_res("mlp0_up_bwd", dh0, w_up0, du2, TH, D, F)
    g_up0 = wgrad("g_up0", x1, dh0, D, TW)
    sc_3, tok = _scatter_start("scatter_3", [g_down0, g_up0], [0, 1])
    du1, dg_mix0, db_mix0 = _ln_bwd("ln_mix0_bwd", dx1, xh1, r1, ln_mix_g[0:1], TH, tok)
    do, delta = _attn_out_bwd(du1, wa_out, o_att, sel_t, TH, D)
    g_aout = wgrad("g_attn_out", o_att, du1, D, D)
    dP3 = _attn_bwd(P3, _stack_classes("do_classes", do, MXU_DTYPE), _stack_classes("lse_classes", L_att, F32),
                    _stack_classes("delta_classes", delta, F32), cos3, sin3, D)
    small = jnp.concatenate([d_lb_logits, dg_mix0, dg_mix1, db_mix0, db_mix1, dg_ffn0, dg_ffn1, db_ffn0, db_ffn1,
                             dg_norm, sq, jnp.zeros((4, D), F32)], axis=0)
    small = _all_reduce_small(small, dP3)
    loss = 0.5 * jnp.sum(small[11]) / D
    grp = lambda j: j // (3 * D // TG)
    g_ain = _matmul("g_attn_in", xc3, dP3, "tn", D, TG, T1, [(_sds((D, 9 * D), MXU_DTYPE), _ij_spec(D, TG))], _store_epilogue,
                    a_map=lambda i, j, k: (k + grp(j) * (S // T1), i),
                    b_map=lambda i, j, k: (k + grp(j) * (S // T1), j % (3 * D // TG)), mnk=(D, 9 * D, S), dep=small)[0]
    sc_4, tok = _scatter_start("scatter_4", [g_aout, g_ain], [0, 1])
    dxc3 = _matmul("attn_in_bwd", dP3, wa_in, "nt", TH, D, 3 * D, [(_sds((3 * S, D), F32), _ij_spec(TH, D))], _store_epilogue,
                   b_map=lambda i, j, k: (j, k + i // (S // TH)), mnk=(3 * S, D, 3 * D), dep=tok)[0]
    grad_x = _input_grad(du1, dxc3)

    def reduced(name, state, after):
        gs, lands = _scatter_wait(name + "_wait", state, after)
        return [_reduce_join(f"{name}_reduce_{i}", l, _own_piece(g, ax)) for i, (l, g, ax) in enumerate(zip(lands, gs, state[4]))]

    r_down1, r_up1 = reduced("scatter_1", sc_1, grad_x)
    r_hout, r_hin = reduced("scatter_2", sc_2, r_up1)
    r_down0, r_up0 = reduced("scatter_3", sc_3, r_hin)

    my_chip = 2 * lax.axis_index("x") + lax.axis_index("y")
    nsh = hgrn_norm_g.shape[1]
    g_norm = lax.dynamic_slice(small[10:11], (0, my_chip * nsh), (1, nsh))

    grads, upd = {}, {}

    def update(nm, w, gr, m, v):
        grads[nm] = gr.reshape(w.shape)
        upd[nm] = _adamw("adamw_" + nm, w, grads[nm], m, v)

    update("hgrn_w_in", hgrn_w_in, r_hin, m_hgrn_w_in, v_hgrn_w_in)
    update("hgrn_w_out", hgrn_w_out, r_hout, m_hgrn_w_out, v_hgrn_w_out)
    update("ffn_w_up", ffn_w_up, jnp.stack([r_up0, r_up1]), m_ffn_w_up, v_ffn_w_up)
    update("ffn_w_down", ffn_w_down, jnp.stack([r_down0, r_down1]), m_ffn_w_down, v_ffn_w_down)
    r_aout, r_ain = reduced("scatter_4", sc_4, upd["ffn_w_down"][2])
    update("attn_w_in", attn_w_in, r_ain, m_attn_w_in, v_attn_w_in)
    update("attn_w_out", attn_w_out, r_aout, m_attn_w_out, v_attn_w_out)
    grads["hgrn_norm_g"] = g_norm
    upd["hgrn_norm_g"] = _adamw("adamw_hgrn_norm_g", hgrn_norm_g, g_norm, m_hgrn_norm_g, v_hgrn_norm_g)
    cat = lambda ts: jnp.concatenate(ts, axis=0)
    small_w = cat([lb_logits, ln_mix_g, ln_mix_b, ln_ffn_g, ln_ffn_b])
    small_m = cat([m_lb_logits, m_ln_mix_g, m_ln_mix_b, m_ln_ffn_g, m_ln_ffn_b])
    small_v = cat([v_lb_logits, v_ln_mix_g, v_ln_mix_b, v_ln_ffn_g, v_ln_ffn_b])
    small_upd = _adamw("adamw_small", small_w, small[0:10], small_m, small_v)
    for i, nm in enumerate(["lb_logits", "ln_mix_g", "ln_mix_b", "ln_ffn_g", "ln_ffn_b"]):
        grads[nm] = small[2 * i:2 * i + 2]
        upd[nm] = tuple(t[2 * i:2 * i + 2] for t in small_upd)

    order = ["attn_w_in", "attn_w_out", "hgrn_w_in", "hgrn_w_out", "hgrn_norm_g", "lb_logits", "ln_mix_g", "ln_mix_b",
             "ln_ffn_g", "ln_ffn_b", "ffn_w_up", "ffn_w_down"]
    return (loss, grad_x[None], *[grads[k] for k in order], *[upd[k][0] for k in order],
            *[upd[k][1] for k in order], *[upd[k][2] for k in order])
```

```python
import math

import jax
import jax.numpy as jnp
from jax import lax
from jax.experimental import pallas as pl
from jax.experimental.pallas import tpu as pltpu

F32 = jnp.float32
BF16 = jnp.bfloat16
MXU_DTYPE = BF16

HEAD_DIM = 64
ATTN_BLK = 128
DILATIONS = (1, 4, 16)
ROPE_THETA = 10000.0
HGRN_DK = 128
HGRN_CHUNK = 64
DEPTH = 2
LN_EPS = 1e-5
RMS_EPS = 1e-6
ALPHA = (2 * DEPTH) ** 0.25
ADAM_LR, ADAM_B1, ADAM_B2, ADAM_EPS, ADAM_WD, ADAM_STEP = 0.001, 0.9, 0.999, 1e-08, 0.01, 10

LANES = 128
VMEM_LIMIT = 56 * 1024 * 1024
NEG = -1e30
MESH = pl.DeviceIdType.MESH


def _cparams(sem=None):
    return pltpu.CompilerParams(dimension_semantics=sem, vmem_limit_bytes=VMEM_LIMIT)


def _sds(shape, dtype):
    return jax.ShapeDtypeStruct(tuple(shape), dtype)


def _dg(a, b, ca, cb):
    return lax.dot_general(a, b, (((ca,), (cb,)), ((), ())), preferred_element_type=F32)


def _nn(a, b):
    return _dg(a, b, 1, 0)


def _nt(a, b):
    return _dg(a, b, 1, 1)


def _tn(a, b):
    return _dg(a, b, 0, 0)


def _split3(a):
    hi = a.astype(BF16)
    r = a - hi.astype(F32)
    mid = r.astype(BF16)
    lo = (r - mid.astype(F32)).astype(BF16)
    return hi, mid, lo


def _exact_nn(a, sel):
    hi, mid, lo = _split3(a)
    return _nn(hi, sel) + _nn(mid, sel) + _nn(lo, sel)


def _pick(n, prefs):
    for p in prefs:
        if n % p == 0:
            return p
    return n


def _matmul(name, a, b, form, tm, tn, tk, outs, epilogue, extras=(), a_map=None, b_map=None, mnk=None, dep=None,
            sem=("parallel", "parallel", "arbitrary")):
    if form == "nn":
        (M, K), N = a.shape, b.shape[1]
        a_spec = pl.BlockSpec((tm, tk), a_map or (lambda i, j, k: (i, k)))
        b_spec = pl.BlockSpec((tk, tn), b_map or (lambda i, j, k: (k, j)))
        ca, cb = 1, 0
    elif form == "nt":
        (M, K), N = a.shape, b.shape[0]
        a_spec = pl.BlockSpec((tm, tk), a_map or (lambda i, j, k: (i, k)))
        b_spec = pl.BlockSpec((tn, tk), b_map or (lambda i, j, k: (j, k)))
        ca, cb = 1, 1
    else:
        (K, M), N = a.shape, b.shape[1]
        a_spec = pl.BlockSpec((tk, tm), a_map or (lambda i, j, k: (k, i)))
        b_spec = pl.BlockSpec((tk, tn), b_map or (lambda i, j, k: (k, j)))
        ca, cb = 0, 0
    if mnk is not None:
        M, N, K = mnk
    assert M % tm == 0 and N % tn == 0 and K % tk == 0, (name, M, N, K, tm, tn, tk)
    nk = K // tk
    ne, no = len(extras), len(outs)
    deps = [] if dep is None else [dep]
    nd = len(deps)

    def body(a_ref, b_ref, *rest):
        extra_refs, out_refs = rest[:ne], rest[ne + nd:ne + nd + no]
        j = pl.program_id(1)
        part = _dg(a_ref[...].astype(MXU_DTYPE), b_ref[...].astype(MXU_DTYPE), ca, cb)
        if nk == 1:
            epilogue(part, extra_refs, out_refs, j)
            return
        acc_ref = rest[-1]
        k = pl.program_id(2)

        @pl.when(k == 0)
        def _():
            acc_ref[...] = part

        @pl.when(k > 0)
        def _():
            acc_ref[...] += part

        @pl.when(k == nk - 1)
        def _():
            epilogue(acc_ref[...], extra_refs, out_refs, j)

    res = pl.pallas_call(
        body,
        name=name,
        grid=(M // tm, N // tn, nk),
        in_specs=[a_spec, b_spec] + [s for _, s in extras] + [pl.BlockSpec(memory_space=pl.ANY)] * nd,
        out_specs=[s for _, s in outs],
        out_shape=[o for o, _ in outs],
        scratch_shapes=[pltpu.VMEM((tm, tn), F32)] if nk > 1 else [],
        compiler_params=_cparams(sem),
    )(a, b, *[e for e, _ in extras], *deps)
    return res


def _ij_spec(tm, tn):
    return pl.BlockSpec((tm, tn), lambda i, j, k: (i, j))


def _store_epilogue(acc, extra_refs, out_refs, j):
    out_refs[0][...] = acc.astype(out_refs[0].dtype)


def _plain_mm(name, a, b, form, out_dtype, tm, tn, tk):
    M = a.shape[1] if form == "tn" else a.shape[0]
    N = b.shape[0] if form == "nt" else b.shape[1]
    return _matmul(name, a, b, form, tm, tn, tk, [(_sds((M, N), out_dtype), _ij_spec(tm, tn))], _store_epilogue)[0]


def _class_slabs(S):
    assert DILATIONS[0] == 1
    return [(g, d, r, S // d) for g, d in enumerate(DILATIONS) if d > 1 for r in range(d)]


def _stack_classes(name, t, out_dtype):
    S, W = t.shape

    def body(x_ref, o_ref):
        o_ref[0:S, :] = x_ref[...].astype(out_dtype)
        for g, d, r, n in _class_slabs(S):
            o_ref[g * S + r * n:g * S + (r + 1) * n, :] = x_ref[pl.ds(r, n, stride=d), :].astype(out_dtype)

    return pl.pallas_call(
        body,
        name=name,
        grid=(W // LANES,),
        in_specs=[pl.BlockSpec((S, LANES), lambda j: (0, j))],
        out_specs=pl.BlockSpec((3 * S, LANES), lambda j: (0, j)),
        out_shape=_sds((3 * S, W), out_dtype),
        compiler_params=_cparams(("parallel",)),
    )(t)


def _rope_tables(seq):
    half = HEAD_DIM // 2
    inv = ROPE_THETA ** (-jnp.arange(half, dtype=F32) * (2.0 / HEAD_DIM))
    inv = jnp.tile(inv, LANES // half)
    pos = []
    for d in DILATIONS:
        row = jnp.arange(seq)
        pos.append((row % (seq // d)) * d + row // (seq // d))
    ang = jnp.concatenate(pos).astype(F32)[:, None] * inv[None, :]
    first = (jnp.arange(LANES) % HEAD_DIM) < half
    sin = jnp.sin(ang)
    return jnp.cos(ang), jnp.where(first[None, :], -sin, sin)


def _partner(x):
    half = HEAD_DIM // 2
    lane = lax.broadcasted_iota(jnp.int32, x.shape, 1)
    first = (lane % HEAD_DIM) < half
    return jnp.where(first, pltpu.roll(x, LANES - half, 1), pltpu.roll(x, half, 1))


def _attn_proj(x3, w_full, cos3, sin3, tm, tn):
    S3, D = x3.shape
    S = S3 // 3
    per_part = D // tn
    per_group = 3 * per_part

    def epilogue(acc, extra_refs, out_refs, j):
        cos_ref, sin_ref = extra_refs
        o_ref = out_refs[0]
        is_rot = j // per_part < 2

        @pl.when(is_rot)
        def _():
            c, s = cos_ref[...], sin_ref[...]
            for t in range(tn // LANES):
                xs = acc[:, t * LANES:(t + 1) * LANES]
                o_ref[:, t * LANES:(t + 1) * LANES] = (xs * c + _partner(xs) * s).astype(o_ref.dtype)

        @pl.when(jnp.logical_not(is_rot))
        def _():
            o_ref[...] = acc.astype(o_ref.dtype)

    tab = pl.BlockSpec((tm, LANES), lambda i, j, k: (i, 0))
    return _matmul("attn_proj", x3, w_full, "nn", tm, tn, D, [(_sds((S3, 3 * D), MXU_DTYPE), _ij_spec(tm, tn))],
                   epilogue, extras=[(cos3, tab), (sin3, tab)],
                   b_map=lambda i, j, k: (k, j + (i // (S // tm)) * per_group), mnk=(S3, 3 * D, D))[0]


def _head_sel(d_model):
    h = jnp.arange(LANES)[:, None]
    l = jnp.arange(d_model)[None, :]
    return (l // HEAD_DIM == h).astype(BF16)


def _class_edges(b, nblk):
    g = b // nblk
    per_class = jnp.where(g == 0, nblk // DILATIONS[0], jnp.where(g == 1, nblk // DILATIONS[1], nblk // DILATIONS[2]))
    pos = (b % nblk) % per_class
    return pos != 0, pos != per_class - 1


def _two_heads(t, top):
    zero = jnp.zeros_like(t)
    return jnp.concatenate([jnp.where(top, t, zero), jnp.where(top, zero, t)], axis=0)


def _band_mask(has_prev):
    B = ATTN_BLK
    row = lax.broadcasted_iota(jnp.int32, (2 * B, 2 * B), 0) % B
    col = lax.broadcasted_iota(jnp.int32, (2 * B, 2 * B), 1)
    in_prev = jnp.logical_and(jnp.logical_and(col < B, col >= row), has_prev)
    in_own = jnp.logical_and(col >= B, col - B <= row)
    return jnp.logical_or(in_prev, in_own)


def _attn_fwd(P3, D):
    S3 = P3.shape[0]
    B = ATTN_BLK
    nblk = S3 // 3 // B
    npairs = D // LANES
    scale = HEAD_DIM ** -0.5

    def body(q_ref, kc_ref, vc_ref, kp_ref, vp_ref, o_ref, lse_ref):
        has_prev, _ = _class_edges(pl.program_id(0), nblk)
        ok = _band_mask(has_prev)
        lane = lax.broadcasted_iota(jnp.int32, (B, LANES), 1)
        top = lane < HEAD_DIM
        lse_acc = jnp.zeros((B, LANES), F32)
        for j in range(npairs):
            sl = slice(j * LANES, (j + 1) * LANES)
            Q = _two_heads(q_ref[:, sl] * scale, top)
            K2 = jnp.concatenate([kp_ref[:, sl], kc_ref[:, sl]], axis=0)
            V2 = jnp.concatenate([vp_ref[:, sl], vc_ref[:, sl]], axis=0)
            s = jnp.where(ok, _nt(Q, K2), NEG)
            m = jnp.max(s, axis=1, keepdims=True)
            p = jnp.exp(s - m)
            l = jnp.sum(p, axis=1, keepdims=True)
            o = _nn((p * (1.0 / l)).astype(MXU_DTYPE), V2)
            o_ref[:, sl] = jnp.where(top, o[:B], o[B:])
            lse = m + jnp.log(l)
            lse_acc = jnp.where(lane == 2 * j, lse[:B], jnp.where(lane == 2 * j + 1, lse[B:], lse_acc))
        lse_ref[...] = lse_acc

    blk = lambda part, prev: pl.BlockSpec(
        (B, D), (lambda b: (jnp.maximum(b - 1, 0), part)) if prev else (lambda b: (b, part)))
    return pl.pallas_call(
        body,
        name="attn_fwd",
        grid=(3 * nblk,),
        in_specs=[blk(0, False), blk(1, False), blk(2, False), blk(1, True), blk(2, True)],
        out_specs=[pl.BlockSpec((B, D), lambda b: (b, 0)), pl.BlockSpec((B, LANES), lambda b: (b, 0))],
        out_shape=[_sds((S3, D), F32), _sds((S3, LANES), F32)],
        compiler_params=_cparams(("parallel",)),
    )(P3, P3, P3, P3, P3)


def _attn_mix(o3, lse3, sel):
    S3, D = o3.shape
    S = S3 // 3

    def body(o3_ref, lse_ref, sel_ref, o_ref, L_ref, w_ref):
        @pl.when(pl.program_id(0) == 0)
        def _():
            w_ref[0] = lse_ref[0:S, :]
            for g, d, r, n in _class_slabs(S):
                w_ref[g, pl.ds(r, n, stride=d), :] = lse_ref[g * S + r * n:g * S + (r + 1) * n, :]
            a, b, c = w_ref[0], w_ref[1], w_ref[2]
            m = jnp.maximum(jnp.maximum(a, b), c)
            L = m + jnp.log(jnp.exp(a - m) + jnp.exp(b - m) + jnp.exp(c - m))
            L_ref[...] = L
            w_ref[0] = jnp.exp(a - L)
            w_ref[1] = jnp.exp(b - L)
            w_ref[2] = jnp.exp(c - L)

        s = sel_ref[...]
        o_ref[...] = _exact_nn(w_ref[0], s) * o3_ref[0:S, :]
        for g, d, r, n in _class_slabs(S):
            rows = pl.ds(r, n, stride=d)
            o_ref[rows, :] += _exact_nn(w_ref[g, rows, :], s) * o3_ref[g * S + r * n:g * S + (r + 1) * n, :]

    return pl.pallas_call(
        body,
        name="attn_mix",
        grid=(D // LANES,),
        in_specs=[pl.BlockSpec((S3, LANES), lambda j: (0, j)), pl.BlockSpec((S3, LANES), lambda j: (0, 0)),
                  pl.BlockSpec((LANES, LANES), lambda j: (0, j))],
        out_specs=[pl.BlockSpec((S, LANES), lambda j: (0, j)), pl.BlockSpec((S, LANES), lambda j: (0, 0))],
        out_shape=[_sds((S, D), F32), _sds((S, LANES), F32)],
        scratch_shapes=[pltpu.VMEM((3, S, LANES), F32)],
        compiler_params=_cparams(("arbitrary",)),
    )(o3, lse3, sel)


def _attn_bwd(P3, do3, L3, delta3, cos3, sin3, D):
    S3 = P3.shape[0]
    B = ATTN_BLK
    nblk = S3 // 3 // B
    npairs = D // LANES
    scale = HEAD_DIM ** -0.5

    def body(c_ref, p_ref, n_ref, doc_ref, don_ref, Lc_ref, Ln_ref, dc_ref, dn_ref, cos_ref, sin_ref, out_ref):
        has_prev, has_next = _class_edges(pl.program_id(0), nblk)
        ok = _band_mask(has_prev)
        row = lax.broadcasted_iota(jnp.int32, (2 * B, B), 0) % B
        col = lax.broadcasted_iota(jnp.int32, (2 * B, B), 1)
        ok_n = jnp.logical_and(col >= row, has_next)
        lane = lax.broadcasted_iota(jnp.int32, (B, LANES), 1)
        top = lane < HEAD_DIM
        cos_t = cos_ref[...]
        sin_inv = -sin_ref[...]
        Lc_all, Ln_all, dc_all, dn_all = Lc_ref[...], Ln_ref[...], dc_ref[...], dn_ref[...]
        pair_col = lambda t, j: jnp.concatenate([t[:, 2 * j:2 * j + 1], t[:, 2 * j + 1:2 * j + 2]], axis=0)
        for j in range(npairs):
            sl = lambda part: slice(part * D + j * LANES, part * D + (j + 1) * LANES)
            kc2, vc2 = c_ref[:, sl(1)], c_ref[:, sl(2)]
            K2 = jnp.concatenate([p_ref[:, sl(1)], kc2], axis=0)
            V2 = jnp.concatenate([p_ref[:, sl(2)], vc2], axis=0)
            Qc = _two_heads(c_ref[:, sl(0)] * scale, top)
            Qn = _two_heads(n_ref[:, sl(0)] * scale, top)
            DOc = _two_heads(doc_ref[:, j * LANES:(j + 1) * LANES].astype(MXU_DTYPE), top)
            DOn = _two_heads(don_ref[:, j * LANES:(j + 1) * LANES].astype(MXU_DTYPE), top)
            P_c = jnp.where(ok, jnp.exp(_nt(Qc, K2) - pair_col(Lc_all, j)), 0.0)
            dS_c = P_c * (_nt(DOc, V2) - pair_col(dc_all, j))
            P_n = jnp.where(ok_n, jnp.exp(_nt(Qn, kc2) - pair_col(Ln_all, j)), 0.0)
            dS_n = P_n * (_nt(DOn, vc2) - pair_col(dn_all, j))
            dq = _nn(dS_c.astype(MXU_DTYPE), K2)
            dq2 = jnp.where(top, dq[:B], dq[B:]) * scale
            Qk = jnp.concatenate([Qc, Qn], axis=0)
            DOk = jnp.concatenate([DOc, DOn], axis=0)
            dk2 = _tn(jnp.concatenate([dS_c[:, B:], dS_n], axis=0).astype(MXU_DTYPE), Qk)
            dv2 = _tn(jnp.concatenate([P_c[:, B:], P_n], axis=0).astype(MXU_DTYPE), DOk)
            out_ref[:, sl(0)] = (dq2 * cos_t + _partner(dq2) * sin_inv).astype(out_ref.dtype)
            out_ref[:, sl(1)] = (dk2 * cos_t + _partner(dk2) * sin_inv).astype(out_ref.dtype)
            out_ref[:, sl(2)] = dv2.astype(out_ref.dtype)

    cur = lambda b: b
    prv = lambda b: jnp.maximum(b - 1, 0)
    nxt = lambda b: jnp.minimum(b + 1, 3 * nblk - 1)
    spec = lambda w, f: pl.BlockSpec((B, w), lambda b: (f(b), 0))
    return pl.pallas_call(
        body,
        name="attn_bwd",
        grid=(3 * nblk,),
        in_specs=[spec(3 * D, cur), spec(3 * D, prv), spec(3 * D, nxt), spec(D, cur), spec(D, nxt),
                  spec(LANES, cur), spec(LANES, nxt), spec(LANES, cur), spec(LANES, nxt), spec(LANES, cur), spec(LANES, cur)],
        out_specs=spec(3 * D, cur),
        out_shape=_sds((S3, 3 * D), MXU_DTYPE),
        compiler_params=_cparams(("parallel",)),
    )(P3, P3, P3, do3, do3, L3, L3, delta3, delta3, cos3, sin3)


def _input_grad(du, dx3):
    S, D = du.shape

    def body(du_ref, dx_ref, o_ref):
        o_ref[...] = ALPHA * du_ref[...] + dx_ref[0:S, :]
        for g, d, r, n in _class_slabs(S):
            o_ref[pl.ds(r, n, stride=d), :] += dx_ref[g * S + r * n:g * S + (r + 1) * n, :]

    return pl.pallas_call(
        body,
        name="input_grad",
        grid=(D // LANES,),
        in_specs=[pl.BlockSpec((S, LANES), lambda j: (0, j)), pl.BlockSpec((3 * S, LANES), lambda j: (0, j))],
        out_specs=pl.BlockSpec((S, LANES), lambda j: (0, j)),
        out_shape=_sds((S, D), F32),
        compiler_params=_cparams(("parallel",)),
    )(du, dx3)


def _chunk_causal(tb):
    r = lax.broadcasted_iota(jnp.int32, (tb, tb), 0)
    c = lax.broadcasted_iota(jnp.int32, (tb, tb), 1)
    return jnp.logical_and((r // HGRN_CHUNK) == (c // HGRN_CHUNK), r >= c)


def _chunk_sums(a, lower):
    C = HGRN_CHUNK
    r = lax.broadcasted_iota(jnp.int32, (C, C), 0)
    c = lax.broadcasted_iota(jnp.int32, (C, C), 1)
    tri = ((r >= c) if lower else (r <= c)).astype(BF16)
    parts = _split3(a)
    out = []
    for ci in range(a.shape[0] // C):
        rows = slice(ci * C, (ci + 1) * C)
        out.append(_nn(tri, parts[0][rows]) + _nn(tri, parts[1][rows]) + _nn(tri, parts[2][rows]))
    return jnp.concatenate(out, axis=0)


def _chunk_last(b):
    C = HGRN_CHUNK
    return jnp.concatenate([jnp.broadcast_to(b[(ci + 1) * C - 1:(ci + 1) * C, :], (C, b.shape[1]))
                            for ci in range(b.shape[0] // C)], axis=0)


def _lower_bound(lb_ref):
    l0, l1 = lb_ref[0:1, :], lb_ref[1:2, :]
    m = jnp.maximum(l0, l1)
    e0, e1 = jnp.exp(l0 - m), jnp.exp(l1 - m)
    return e1 / (e0 + e1)


def _hgrn_gates(q_raw, z, lb):
    sg = 1.0 / (1.0 + jnp.exp(-z))
    sn = 1.0 / (1.0 + jnp.exp(z))
    f = lb + (1.0 - lb) * sg
    key = (1.0 - lb) * sn
    sq = 1.0 / (1.0 + jnp.exp(-q_raw))
    return sg, sn, f, key, sq


HGRN_HEADS_PER_STEP = 2


def _hgrn_fwd(P1, lb_logits, norm_g, tb):
    S = P1.shape[0]
    D = P1.shape[1] // 3
    K = HGRN_DK
    H = D // K
    HP = HGRN_HEADS_PER_STEP
    C = HGRN_CHUNK
    cpb = tb // C
    nt = S // tb

    def body(q_ref, f_ref, i_ref, lb_ref, g_ref, o_ref, n_ref, st_ref, state):
        t = pl.program_id(1)

        @pl.when(t == 0)
        def _():
            state[...] = jnp.zeros_like(state)

        lb_all = _lower_bound(lb_ref)
        low = _chunk_causal(tb)
        for hh in range(HP):
            lanes = slice(hh * K, (hh + 1) * K)
            q_raw, z, v = q_ref[:, lanes], f_ref[:, lanes], i_ref[:, lanes]
            sg, sn, f, key, sq = _hgrn_gates(q_raw, z, lb_all[:, lanes])
            b = _chunk_sums(jnp.log(f), lower=True)
            qd = (q_raw * sq * jnp.exp(b)).astype(MXU_DTYPE)
            kd = (key * jnp.exp(-b)).astype(MXU_DTYPE)
            kb = (key * jnp.exp(_chunk_last(b) - b)).astype(MXU_DTYPE)
            vm = v.astype(MXU_DTYPE)
            a = jnp.where(low, _nt(qd, kd), 0.0).astype(MXU_DTYPE)
            o_intra = _nn(a, vm)
            st = state[hh]
            outs = []
            for ci in range(cpb):
                rows = slice(ci * C, (ci + 1) * C)
                st_ref[hh, ci] = st
                outs.append(o_intra[rows] + _nt(qd[rows], st.astype(MXU_DTYPE)))
                st = st * jnp.exp(b[(ci + 1) * C - 1:(ci + 1) * C, :]) + _tn(vm[rows], kb[rows])
            state[hh] = st
            o = jnp.concatenate(outs, axis=0)
            o_ref[:, lanes] = o
            rs = lax.rsqrt(jnp.mean(o * o, axis=1, keepdims=True) + RMS_EPS)
            n_ref[:, lanes] = o * rs * g_ref[:, lanes]

    tok = lambda part: pl.BlockSpec((tb, HP * K), lambda h, t: (t, part * (H // HP) + h))
    vec = lambda rows: pl.BlockSpec((rows, HP * K), lambda h, t: (0, h))
    return pl.pallas_call(
        body,
        name="hgrn_fwd",
        grid=(H // HP, nt),
        in_specs=[tok(0), tok(1), tok(2), vec(2), vec(1)],
        out_specs=[tok(0), tok(0), pl.BlockSpec((HP, cpb, K, K), lambda h, t: (h, t, 0, 0))],
        out_shape=[_sds((S, D), F32), _sds((S, D), F32), _sds((H, S // C, K, K), F32)],
        scratch_shapes=[pltpu.VMEM((HP, K, K), F32)],
        compiler_params=_cparams(("parallel", "arbitrary")),
    )(P1, P1, P1, lb_logits, norm_g)


def _hgrn_bwd(P1, o_pre, states, dn, lb_logits, norm_g, tb):
    S = P1.shape[0]
    D = P1.shape[1] // 3
    K = HGRN_DK
    H = D // K
    HP = HGRN_HEADS_PER_STEP
    C = HGRN_CHUNK
    cpb = tb // C
    nt = S // tb

    def body(q_ref, f_ref, i_ref, o_ref, st_ref, dn_ref, lb_ref, g_ref, d_ref, dg_ref, dlb_ref, dstate):
        t = pl.program_id(1)

        @pl.when(t == 0)
        def _():
            dstate[...] = jnp.zeros_like(dstate)
            dg_ref[...] = jnp.zeros_like(dg_ref)
            dlb_ref[...] = jnp.zeros_like(dlb_ref)

        lb_all = _lower_bound(lb_ref)
        low = _chunk_causal(tb)
        for hh in range(HP):
            lanes = slice(hh * K, (hh + 1) * K)
            lb = lb_all[:, lanes]
            gn = g_ref[:, lanes]
            q_raw, z, v = q_ref[:, lanes], f_ref[:, lanes], i_ref[:, lanes]
            sg, sn, f, key, sq = _hgrn_gates(q_raw, z, lb)
            b = _chunk_sums(jnp.log(f), lower=True)
            e_pos, e_neg, e_rel = jnp.exp(b), jnp.exp(-b), jnp.exp(_chunk_last(b) - b)
            qd_f, kd_f, kb_f = q_raw * sq * e_pos, key * e_neg, key * e_rel
            qd, kd, kb = qd_f.astype(MXU_DTYPE), kd_f.astype(MXU_DTYPE), kb_f.astype(MXU_DTYPE)
            vm = v.astype(MXU_DTYPE)
            a = jnp.where(low, _nt(qd, kd), 0.0).astype(MXU_DTYPE)
            o = o_ref[:, lanes]
            dnn = dn_ref[:, lanes]
            rs = lax.rsqrt(jnp.mean(o * o, axis=1, keepdims=True) + RMS_EPS)
            dg_ref[:, lanes] += jnp.sum(dnn * o * rs, axis=0, keepdims=True)
            tg = dnn * gn
            dom = (rs * tg - o * (rs * rs * rs) * jnp.mean(tg * o, axis=1, keepdims=True)).astype(MXU_DTYPE)
            da = jnp.where(low, _nt(dom, vm), 0.0).astype(MXU_DTYPE)
            dv = _tn(a, dom)
            dqd = _nn(da, kd)
            dkd = _tn(da, qd)
            dst = dstate[hh]
            dv_s, dqd_s, dkb_s, dbl_s = [None] * cpb, [None] * cpb, [None] * cpb, [None] * cpb
            for ci in reversed(range(cpb)):
                rows = slice(ci * C, (ci + 1) * C)
                st = st_ref[hh, ci]
                dstm = dst.astype(MXU_DTYPE)
                dec = jnp.exp(b[(ci + 1) * C - 1:(ci + 1) * C, :])
                dv_s[ci] = _nt(kb[rows], dstm)
                dkb_s[ci] = _nn(vm[rows], dstm)
                dqd_s[ci] = _nn(dom[rows], st.astype(MXU_DTYPE))
                db_last = jnp.sum(dkb_s[ci] * kb_f[rows], axis=0, keepdims=True) + jnp.sum(dst * st, axis=0, keepdims=True) * dec
                dbl_s[ci] = jnp.broadcast_to(db_last, (C, K))
                dst = dst * dec + _tn(dom[rows], qd[rows])
            dstate[hh] = dst
            dv = dv + jnp.concatenate(dv_s, axis=0)
            dqd = dqd + jnp.concatenate(dqd_s, axis=0)
            dkb = jnp.concatenate(dkb_s, axis=0)
            dkey = dkd * e_neg + dkb * e_rel
            db = dqd * qd_f - dkd * kd_f - dkb * kb_f
            dlogf = _chunk_sums(db, lower=False) + jnp.concatenate(dbl_s, axis=0)
            gz = (1.0 - lb) * sg * sn
            d_ref[0, :, lanes] = (dqd * e_pos * (sq + q_raw * sq * (1.0 - sq))).astype(d_ref.dtype)
            d_ref[1, :, lanes] = (dlogf * gz / f - dkey * gz).astype(d_ref.dtype)
            d_ref[2, :, lanes] = dv.astype(d_ref.dtype)
            dlb_ref[:, lanes] += jnp.sum(dlogf * sn / f - dkey * sn, axis=0, keepdims=True)

    rev = lambda t: nt - 1 - t
    tok = lambda part: pl.BlockSpec((tb, HP * K), lambda h, t: (rev(t), part * (H // HP) + h))
    vec = lambda rows: pl.BlockSpec((rows, HP * K), lambda h, t: (0, h))
    outs = pl.pallas_call(
        body,
        name="hgrn_bwd",
        grid=(H // HP, nt),
        in_specs=[tok(0), tok(1), tok(2), tok(0),
                  pl.BlockSpec((HP, cpb, K, K), lambda h, t: (h, rev(t), 0, 0)),
                  tok(0), vec(2), vec(1)],
        out_specs=[pl.BlockSpec((3, tb, HP * K), lambda h, t: (0, rev(t), h)), vec(1), vec(1)],
        out_shape=[_sds((3, S, D), MXU_DTYPE)] + [_sds((1, D), F32)] * 2,
        scratch_shapes=[pltpu.VMEM((HP, K, K), F32)],
        compiler_params=_cparams(("parallel", "arbitrary")),
    )(P1, P1, P1, o_pre, states, dn, lb_logits, norm_g)
    return outs


def _lb_logits_grad(dlb, lb_logits):
    def body(d_ref, l_ref, o_ref):
        s1 = _lower_bound(l_ref)
        d = d_ref[...]
        o_ref[0:1, :] = -(1.0 - s1) * s1 * d
        o_ref[1:2, :] = s1 * (1.0 - s1) * d

    return pl.pallas_call(body, name="lb_logits_grad", out_shape=_sds(lb_logits.shape, F32))(dlb, lb_logits)


def _ln_epilogue(acc, extra_refs, out_refs, j):
    res_ref, g_ref, b_ref = extra_refs
    x_ref, xhat_ref, rstd_ref = out_refs
    u = ALPHA * res_ref[...] + acc
    mu = jnp.mean(u, axis=1, keepdims=True)
    cen = u - mu
    rstd = lax.rsqrt(jnp.mean(cen * cen, axis=1, keepdims=True) + LN_EPS)
    xhat = cen * rstd
    xhat_ref[...] = xhat
    x_ref[...] = xhat * g_ref[...] + b_ref[...]
    rstd_ref[...] = rstd


def _mm_res_ln(name, a, w_full, res, g, b, tm, tk):
    S, D = res.shape
    row = pl.BlockSpec((tm, D), lambda i, j, k: (i, 0))
    vec = pl.BlockSpec((1, D), lambda i, j, k: (0, 0))
    outs = [(_sds((S, D), F32), row), (_sds((S, D), F32), row),
            (_sds((S, 1), F32), pl.BlockSpec((tm, 1), lambda i, j, k: (i, 0)))]
    return _matmul(name, a, w_full, "nn", tm, D, tk, outs, _ln_epilogue, extras=[(res, row), (g, vec), (b, vec)])


def _ln_bwd_rows(dy, xh, rstd, g, first, du_ref, dg_ref, db_ref):
    @pl.when(first)
    def _():
        dg_ref[...] = jnp.zeros_like(dg_ref)
        db_ref[...] = jnp.zeros_like(db_ref)

    dg_ref[...] += jnp.sum(dy * xh, axis=0, keepdims=True)
    db_ref[...] += jnp.sum(dy, axis=0, keepdims=True)
    dxh = dy * g
    m1 = jnp.mean(dxh, axis=1, keepdims=True)
    m2 = jnp.mean(dxh * xh, axis=1, keepdims=True)
    du_ref[...] = rstd * (dxh - m1 - xh * m2)


def _loss_ln_bwd(y, target, xhat, rstd, g, tm):
    S, D = y.shape

    def body(y_ref, t_ref, xh_ref, r_ref, g_ref, sq_ref, du_ref, dg_ref, db_ref):
        first = pl.program_id(0) == 0

        @pl.when(first)
        def _():
            sq_ref[...] = jnp.zeros_like(sq_ref)

        e = y_ref[...] - t_ref[...]
        sq_ref[...] += jnp.sum(e * e, axis=0, keepdims=True)
        _ln_bwd_rows(e / D, xh_ref[...], r_ref[...], g_ref[...], first, du_ref, dg_ref, db_ref)

    row = pl.BlockSpec((tm, D), lambda i: (i, 0))
    vec = pl.BlockSpec((1, D), lambda i: (0, 0))
    return pl.pallas_call(
        body,
        name="loss_ln_bwd",
        grid=(S // tm,),
        in_specs=[row, row, row, pl.BlockSpec((tm, 1), lambda i: (i, 0)), vec],
        out_specs=[vec, row, vec, vec],
        out_shape=[_sds((1, D), F32), _sds((S, D), F32), _sds((1, D), F32), _sds((1, D), F32)],
        compiler_params=_cparams(("arbitrary",)),
    )(y, target, xhat, rstd, g)


def _mlp_up(name, x, w_up, tm, tn, tk):
    S = x.shape[0]
    F = w_up.shape[1]

    def epilogue(acc, extra_refs, out_refs, j):
        r = jnp.maximum(acc, 0.0)
        out_refs[0][...] = (r * r).astype(out_refs[0].dtype)

    return _matmul(name, x, w_up, "nn", tm, tn, tk, [(_sds((S, F), MXU_DTYPE), _ij_spec(tm, tn))], epilogue)[0]


def _mlp_down_bwd(name, dy, w_down, a, tm, tn, tk):
    S, F = a.shape

    def epilogue(acc, extra_refs, out_refs, j):
        out_refs[0][...] = (acc * (2.0 * jnp.sqrt(extra_refs[0][...].astype(F32)))).astype(out_refs[0].dtype)

    return _matmul(name, dy, w_down, "nt", tm, tn, tk, [(_sds((S, F), MXU_DTYPE), _ij_spec(tm, tn))], epilogue,
                   extras=[(a, _ij_spec(tm, tn))])[0]


def _mm_nt_res_ln_bwd(name, dy, w, du, xhat, rstd, g, tm, tk, dep, a_map=None, mk=None):
    S, D = du.shape

    def epilogue(acc, extra_refs, out_refs, j):
        du_ref, xh_ref, r_ref, g_ref = extra_refs
        _ln_bwd_rows(ALPHA * du_ref[...] + acc, xh_ref[...], r_ref[...], g_ref[...], pl.program_id(0) == 0, *out_refs)

    row = pl.BlockSpec((tm, D), lambda i, j, k: (i, 0))
    vec = pl.BlockSpec((1, D), lambda i, j, k: (0, 0))
    return _matmul(name, dy, w, "nt", tm, D, tk, [(_sds((S, D), F32), row), (_sds((1, D), F32), vec), (_sds((1, D), F32), vec)],
                   epilogue, extras=[(du, row), (xhat, row), (rstd, pl.BlockSpec((tm, 1), lambda i, j, k: (i, 0))), (g, vec)],
                   a_map=a_map, mnk=None if mk is None else (S, D, mk), dep=dep, sem=("arbitrary", "arbitrary", "arbitrary"))


def _attn_out_bwd(du, w_out, o, sel_t, tm, tk):
    S, D = o.shape

    def epilogue(acc, extra_refs, out_refs, j):
        out_refs[0][...] = acc
        out_refs[1][...] = _exact_nn(acc * extra_refs[0][...], extra_refs[1][...])

    row = pl.BlockSpec((tm, D), lambda i, j, k: (i, 0))
    slim = pl.BlockSpec((tm, LANES), lambda i, j, k: (i, 0))
    return _matmul("attn_out_bwd", du, w_out, "nt", tm, D, tk,
                   [(_sds((S, D), F32), row), (_sds((S, LANES), F32), slim)], epilogue,
                   extras=[(o, row), (sel_t, pl.BlockSpec((D, LANES), lambda i, j, k: (0, 0)))])


def _adamw(name, w, g, m, v):
    shape = w.shape
    cols = shape[-1]
    rows = math.prod(shape[:-1])
    w2, g2, m2, v2 = (t.reshape(rows, cols) for t in (w, g, m, v))
    tr = _pick(rows, (256, 128, 64, 32, 16, 8))
    c1 = 1.0 - ADAM_B1 ** ADAM_STEP
    c2 = 1.0 - ADAM_B2 ** ADAM_STEP

    def body(w_ref, g_ref, m_ref, v_ref, d_ref, nm_ref, nv_ref):
        gg = g_ref[...]
        nm = ADAM_B1 * m_ref[...] + (1.0 - ADAM_B1) * gg
        nv = ADAM_B2 * v_ref[...] + (1.0 - ADAM_B2) * (gg * gg)
        nm_ref[...] = nm
        nv_ref[...] = nv
        d_ref[...] = -ADAM_LR * ((nm / c1) / (jnp.sqrt(nv / c2) + ADAM_EPS) + ADAM_WD * w_ref[...])

    blk = pl.BlockSpec((tr, cols), lambda i: (i, 0))
    outs = pl.pallas_call(
        body,
        name=name,
        grid=(rows // tr,),
        in_specs=[blk] * 4,
        out_specs=[blk] * 3,
        out_shape=[_sds((rows, cols), F32)] * 3,
        compiler_params=_cparams(("parallel",)),
    )(w2, g2, m2, v2)
    return tuple(o.reshape(shape) for o in outs)


HBM = pl.BlockSpec(memory_space=pl.ANY)


def _shard_slice(ref, axis, size, index):
    idx = [slice(None)] * len(ref.shape)
    idx[axis] = pl.ds(pl.multiple_of(index * size, 8), size)
    return ref.at[tuple(idx)]


IN_HBM = pl.BlockSpec(memory_space=pltpu.HBM)
IN_SEM = pl.BlockSpec(memory_space=pltpu.SEMAPHORE)
DATAFLOW = pltpu.SideEffectType.DATAFLOW_SIDE_EFFECTING


def _hbm(t):
    return pltpu.with_memory_space_constraint(t, pltpu.HBM)


def _token_spec():
    return pl.BlockSpec(memory_space=pltpu.VMEM)


def _gather_copies(s_refs, f_refs, axes, send, recv, loc, arrival):
    x, y, c = lax.axis_index("x"), lax.axis_index("y"), lax.axis_index("c")
    chips = [(1 - x, y), (x, 1 - y), (1 - x, 1 - y)]
    local, remote = [], []
    for a in range(len(s_refs)):
        size = s_refs[a].shape[axes[a]]
        local.append(pltpu.make_async_copy(s_refs[a], _shard_slice(f_refs[a], axes[a], size, 2 * x + y), loc.at[a]))
        for k, (px, py) in enumerate(chips):
            block = (2 * px + py) if arrival else (2 * x + y)
            remote.append(pltpu.make_async_remote_copy(
                src_ref=s_refs[a], dst_ref=_shard_slice(f_refs[a], axes[a], size, block), send_sem=send.at[3 * a + k],
                recv_sem=recv.at[3 * a + k], device_id=(px, py, c), device_id_type=MESH))
    return local, remote


def _gather_start(name, shards, axes, after):
    n = len(shards)
    fulls = []
    for s, ax in zip(shards, axes):
        fs = list(s.shape)
        fs[ax] *= 4
        fulls.append(lax.empty(tuple(fs), s.dtype))

    def body(*refs):
        s_refs, f_refs = refs[:n], refs[n:2 * n]
        send, recv, loc, token = refs[2 * n + 1], refs[2 * n + 2], refs[2 * n + 3], refs[-1]
        local, remote = _gather_copies(s_refs, f_refs, axes, send, recv, loc, arrival=False)
        for cp in remote + local:
            cp.start()
        token[...] = jnp.zeros_like(token)

    outs = pl.pallas_call(
        body,
        name=name,
        out_shape=(pltpu.SemaphoreType.DMA((3 * n,)), pltpu.SemaphoreType.DMA((3 * n,)), pltpu.SemaphoreType.DMA((n,)),
                   *[pltpu.HBM(t.shape, t.dtype) for t in shards + fulls], _sds((8, LANES), F32)),
        in_specs=[IN_HBM] * (2 * n) + [HBM],
        out_specs=(IN_SEM, IN_SEM, IN_SEM, *[IN_HBM] * (2 * n), _token_spec()),
        input_output_aliases={i: 3 + i for i in range(2 * n)},
        compiler_params=pltpu.CompilerParams(has_side_effects=DATAFLOW),
    )(*[_hbm(t) for t in shards + fulls], after)
    return (outs[0], outs[1], outs[2], list(outs[3:3 + n]), list(outs[3 + n:3 + 2 * n]), axes), outs[-1]


def _gather_wait(name, state, *after):
    send, recv, loc, s_thru, f_thru, axes = state
    n = len(s_thru)

    def body(*refs):
        s_refs, f_refs = refs[:n], refs[n:2 * n]
        local, remote = _gather_copies(s_refs, f_refs, axes, refs[2 * n], refs[2 * n + 1], refs[2 * n + 2], arrival=True)
        for cp in local:
            cp.wait()
        for cp in remote:
            cp.wait_send()
            cp.wait_recv()

    outs = pl.pallas_call(
        body,
        name=name,
        out_shape=tuple(pltpu.HBM(t.shape, t.dtype) for t in s_thru + f_thru),
        in_specs=[IN_HBM] * (2 * n) + [IN_SEM, IN_SEM, IN_SEM] + [HBM] * len(after),
        out_specs=tuple([IN_HBM] * (2 * n)),
        input_output_aliases={i: i for i in range(2 * n)},
        compiler_params=pltpu.CompilerParams(has_side_effects=DATAFLOW),
    )(*s_thru, *f_thru, send, recv, loc, *after)
    return list(outs[n:2 * n])


FLIPS = [(fx, fy, fc) for fx in (0, 1) for fy in (0, 1) for fc in (0, 1)][1:]


def _piece_shape(shape, axis):
    ps = list(shape)
    if axis == 0:
        ps[0] //= 8
    else:
        ps[0] //= 2
        ps[axis] //= 4
    return tuple(ps)


def _piece(ref, axis, q, c):
    shape = ref.shape
    idx = [slice(None)] * len(shape)
    if axis == 0:
        h = shape[0] // 8
        idx[0] = pl.ds(pl.multiple_of((2 * q + c) * h, 8), h)
    else:
        h, w = shape[0] // 2, shape[axis] // 4
        idx[0] = pl.ds(c * h, h)
        idx[axis] = pl.ds(pl.multiple_of(q * w, LANES if axis == len(shape) - 1 else 8), w)
    return ref.at[tuple(idx)]


def _own_piece(g, axis):
    ps = _piece_shape(g.shape, axis)
    q, c = 2 * lax.axis_index("x") + lax.axis_index("y"), lax.axis_index("c")
    start = [0] * len(ps)
    if axis == 0:
        start[0] = (2 * q + c) * ps[0]
    else:
        start[0] = c * ps[0]
        start[axis] = q * ps[axis]
    return lax.dynamic_slice(g, start, ps)


def _scatter_copies(g_refs, l_refs, axes, send, recv):
    x, y, c = lax.axis_index("x"), lax.axis_index("y"), lax.axis_index("c")
    out = []
    for a in range(len(g_refs)):
        for k, (fx, fy, fc) in enumerate(FLIPS):
            tx, ty, tc = x ^ fx, y ^ fy, c ^ fc
            out.append(pltpu.make_async_remote_copy(
                src_ref=_piece(g_refs[a], axes[a], 2 * tx + ty, tc), dst_ref=l_refs[a].at[k],
                send_sem=send.at[7 * a + k], recv_sem=recv.at[7 * a + k], device_id=(tx, ty, tc), device_id_type=MESH))
    return out


def _scatter_start(name, grads, axes):
    n = len(grads)
    lands = [lax.empty((7,) + _piece_shape(g.shape, ax), g.dtype) for g, ax in zip(grads, axes)]

    def body(*refs):
        g_refs, l_refs = refs[:n], refs[n:2 * n]
        send, recv, token = refs[2 * n], refs[2 * n + 1], refs[-1]
        for cp in _scatter_copies(g_refs, l_refs, axes, send, recv):
            cp.start()
        token[...] = jnp.zeros_like(token)

    outs = pl.pallas_call(
        body,
        name=name,
        out_shape=(pltpu.SemaphoreType.DMA((7 * n,)), pltpu.SemaphoreType.DMA((7 * n,)),
                   *[pltpu.HBM(t.shape, t.dtype) for t in grads + lands], _sds((8, LANES), F32)),
        in_specs=[IN_HBM] * (2 * n),
        out_specs=(IN_SEM, IN_SEM, *[IN_HBM] * (2 * n), _token_spec()),
        input_output_aliases={i: 2 + i for i in range(2 * n)},
        compiler_params=pltpu.CompilerParams(has_side_effects=DATAFLOW),
    )(*[_hbm(t) for t in grads + lands])
    return (outs[0], outs[1], list(outs[2:2 + n]), list(outs[2 + n:2 + 2 * n]), axes), outs[-1]


def _scatter_wait(name, state, after):
    send, recv, g_thru, l_thru, axes = state
    n = len(g_thru)

    def body(*refs):
        g_refs, l_refs = refs[:n], refs[n:2 * n]
        for cp in _scatter_copies(g_refs, l_refs, axes, refs[2 * n], refs[2 * n + 1]):
            cp.wait_send()
            cp.wait_recv()

    outs = pl.pallas_call(
        body,
        name=name,
        out_shape=tuple(pltpu.HBM(t.shape, t.dtype) for t in g_thru + l_thru),
        in_specs=[IN_HBM] * (2 * n) + [IN_SEM, IN_SEM, HBM],
        out_specs=tuple([IN_HBM] * (2 * n)),
        input_output_aliases={i: i for i in range(2 * n)},
        compiler_params=pltpu.CompilerParams(has_side_effects=DATAFLOW),
    )(*g_thru, *l_thru, send, recv, after)
    return list(outs[:n]), list(outs[n:2 * n])


def _reduce_join(name, landing, own):
    piece = own.shape
    C = piece[-1]
    R = math.prod(piece[:-1])
    l3 = landing.reshape(7, R, C)
    own2 = own.reshape(R, C)
    tr = _pick(R, [t for t in (512, 256, 128, 64, 32, 16, 8) if t * C <= 256 * 1024])
    nsteps = R // tr

    def body(own_ref, l_ref, o_ref, buf, send, loc, recv):
        i = pl.program_id(0)
        x, y, c = lax.axis_index("x"), lax.axis_index("y"), lax.axis_index("c")
        sibling = (x, y, 1 - c)

        def copies(slot, step):
            dst = o_ref.at[pl.ds(pl.multiple_of(c * R + step * tr, 8), tr), :]
            return (pltpu.make_async_copy(buf.at[slot], dst, loc.at[slot]),
                    pltpu.make_async_remote_copy(src_ref=buf.at[slot], dst_ref=dst, send_sem=send.at[slot], recv_sem=recv,
                                                 device_id=sibling, device_id_type=MESH))

        @pl.when(i >= 2)
        def _():
            lc, rc = copies(i % 2, i - 2)
            lc.wait()
            rc.wait_send()

        acc = own_ref[...].astype(F32)
        for s in range(7):
            acc = acc + l_ref[s].astype(F32)
        buf[i % 2] = acc
        lc, rc = copies(i % 2, i)
        lc.start()
        rc.start()

        @pl.when(i == nsteps - 1)
        def _():
            for st in range(max(nsteps - 2, 0), nsteps):
                lc, rc = copies(st % 2, st)
                lc.wait()
                rc.wait_send()
            theirs = o_ref.at[pl.ds(pl.multiple_of((1 - c) * R, 8), R), :]
            pltpu.make_async_remote_copy(src_ref=theirs, dst_ref=theirs, send_sem=send.at[0], recv_sem=recv,
                                         device_id=sibling, device_id_type=MESH).wait_recv()

    return pl.pallas_call(
        body,
        name=name,
        grid=(nsteps,),
        in_specs=[pl.BlockSpec((tr, C), lambda i: (i, 0)), pl.BlockSpec((7, tr, C), lambda i: (0, i, 0))],
        out_specs=HBM,
        out_shape=_sds((2 * R, C), F32),
        scratch_shapes=[pltpu.VMEM((2, tr, C), F32), pltpu.SemaphoreType.DMA((2,)), pltpu.SemaphoreType.DMA((2,)),
                        pltpu.SemaphoreType.DMA(())],
        compiler_params=_cparams(("arbitrary",)),
    )(own2, l3)


def _all_reduce_small(v, dep):
    R, D = v.shape

    def body(v_ref, dep_ref, o_ref, land, send, recv):
        x, y, c = lax.axis_index("x"), lax.axis_index("y"), lax.axis_index("c")
        my_slot = 4 * x + 2 * y + c
        land[my_slot] = v_ref[...]
        for k, (fx, fy, fc) in enumerate(FLIPS):
            tx, ty, tc = x ^ fx, y ^ fy, c ^ fc
            pltpu.make_async_remote_copy(src_ref=v_ref, dst_ref=land.at[my_slot], send_sem=send.at[k], recv_sem=recv.at[k],
                                         device_id=(tx, ty, tc), device_id_type=MESH).start()
        for k, (fx, fy, fc) in enumerate(FLIPS):
            tx, ty, tc = x ^ fx, y ^ fy, c ^ fc
            cp = pltpu.make_async_remote_copy(src_ref=v_ref, dst_ref=land.at[4 * tx + 2 * ty + tc], send_sem=send.at[k],
                                              recv_sem=recv.at[k], device_id=(tx, ty, tc), device_id_type=MESH)
            cp.wait_send()
            cp.wait_recv()
        acc = land[0]
        for s in range(1, 8):
            acc = acc + land[s]
        o_ref[...] = acc

    return pl.pallas_call(
        body,
        name="all_reduce_small",
        in_specs=[pl.BlockSpec(memory_space=pltpu.VMEM), pl.BlockSpec(memory_space=pl.ANY)],
        out_specs=pl.BlockSpec(memory_space=pltpu.VMEM),
        out_shape=_sds((R, D), F32),
        scratch_shapes=[pltpu.VMEM((8, R, D), F32), pltpu.SemaphoreType.DMA((7,)), pltpu.SemaphoreType.DMA((7,))],
    )(v, dep)


def kernel(x, attn_w_in, attn_w_out, hgrn_w_in, hgrn_w_out, hgrn_norm_g, lb_logits, ln_mix_g, ln_mix_b, ln_ffn_g, ln_ffn_b, ffn_w_up, ffn_w_down, loss_target, m_attn_w_in, m_attn_w_out, m_hgrn_w_in, m_hgrn_w_out, m_hgrn_norm_g, m_lb_logits, m_ln_mix_g, m_ln_mix_b, m_ln_ffn_g, m_ln_ffn_b, m_ffn_w_up, m_ffn_w_down, v_attn_w_in, v_attn_w_out, v_hgrn_w_in, v_hgrn_w_out, v_hgrn_norm_g, v_lb_logits, v_ln_mix_g, v_ln_mix_b, v_ln_ffn_g, v_ln_ffn_b, v_ffn_w_up, v_ffn_w_down):
    xs = x[0]
    tgt = loss_target[0]
    S, D = xs.shape
    F = ffn_w_up.shape[2] * 4
    T1 = _pick(S, (1024, 512, 256))
    T2 = _pick(S, (2048, 1024, 512))
    TH = _pick(S, (512, 256))
    TB = _pick(S, (512, 256))
    TN = _pick(D, (512, 256, 128))
    TF = _pick(F, (1024, 512))
    TG = _pick(3 * D, (1536, 1024, 768))
    TW = _pick(F, (2048, 1024))

    cast = lambda w: w.astype(MXU_DTYPE)
    st_a, tok = _gather_start("gather_a", [cast(attn_w_in[0])], [1], jnp.zeros((8, LANES), F32))
    tok, (xs_late, w_aout, w_fup, w_fdown, w_hin, w_hout) = lax.optimization_barrier(
        (tok, (xs, attn_w_out, ffn_w_up, ffn_w_down, hgrn_w_in, hgrn_w_out)))
    st_b, tok = _gather_start("gather_b", [cast(w_aout[0]), cast(w_fup[0]), cast(w_fdown[0])], [0, 1, 0], tok)
    st_c, tok = _gather_start("gather_c", [cast(w_hin[0]), cast(w_hout[0]), hgrn_norm_g, cast(w_fup[1]), cast(w_fdown[1])],
                              [1, 0, 1, 1, 0], tok)

    cos3, sin3 = _rope_tables(S)
    sel = _head_sel(D)
    sel_t = sel.T

    xc3 = _stack_classes("x_classes", xs_late, MXU_DTYPE)
    (wa_in,) = _gather_wait("gather_a_wait", st_a, tok, xc3, cos3, sin3)
    P3 = _attn_proj(xc3, wa_in, cos3, sin3, T2, TN)
    o3, lse3 = _attn_fwd(P3, D)
    o_att, L_att = _attn_mix(o3, lse3, sel)
    wa_out, w_up0, w_down0 = _gather_wait("gather_b_wait", st_b, L_att)
    x1, xh1, r1 = _mm_res_ln("attn_out_ln", o_att, wa_out, xs, ln_mix_g[0:1], ln_mix_b[0:1], TH, D)
    a0 = _mlp_up("mlp0_up", x1, w_up0, T1, TF, D)
    x2, xh2, r2 = _mm_res_ln("mlp0_down_ln", a0, w_down0, x1, ln_ffn_g[0:1], ln_ffn_b[0:1], TH, F)

    wh_in, wh_out, norm_g, w_up1, w_down1 = _gather_wait("gather_c_wait", st_c, r2)
    P1 = _plain_mm("hgrn_proj", x2, wh_in, "nn", F32, T1, _pick(3 * D, (1024, 768, 512)), D)
    o_h, n_h, states = _hgrn_fwd(P1, lb_logits, norm_g, TB)
    x3, xh3, r3 = _mm_res_ln("hgrn_out_ln", n_h, wh_out, x2, ln_mix_g[1:2], ln_mix_b[1:2], TH, D)
    a1 = _mlp_up("mlp1_up", x3, w_up1, T1, TF, D)
    x4, xh4, r4 = _mm_res_ln("mlp1_down_ln", a1, w_down1, x3, ln_ffn_g[1:2], ln_ffn_b[1:2], TH, F)

    wgrad = lambda name, a, dy, tm, tn: _plain_mm(name, a, dy, "tn", MXU_DTYPE, tm, tn, T1)
    sq, du4, dg_ffn1, db_ffn1 = _loss_ln_bwd(x4, tgt, xh4, r4, ln_ffn_g[1:2], TH)
    dh1 = _mlp_down_bwd("mlp1_down_bwd", du4, w_down1, a1, T1, TF, D)
    g_down1 = wgrad("g_down1", a1, du4, TW, D)
    g_up1 = wgrad("g_up1", x3, dh1, D, TW)
    sc_1, tok = _scatter_start("scatter_1", [g_down1, g_up1], [0, 1])
    du3, dg_mix1, db_mix1 = _mm_nt_res_ln_bwd("mlp1_up_bwd", dh1, w_up1, du4, xh3, r3, ln_mix_g[1:2], TH, F, tok)
    dn = _plain_mm("hgrn_out_bwd", du3, wh_out, "nt", F32, T1, D, D)
    g_hout = wgrad("g_hgrn_out", n_h, du3, D, D)
    dP1, dg_norm, dlb = _hgrn_bwd(P1, o_h, states, dn, lb_logits, norm_g, TB)
    dP1 = dP1.reshape(3 * S, D)
    g_hin = _matmul("g_hgrn_in", x2, dP1, "tn", D, D, T1, [(_sds((D, 3 * D), MXU_DTYPE), _ij_spec(D, D))], _store_epilogue,
                    b_map=lambda i, j, k: (k + j * (S // T1), 0), mnk=(D, 3 * D, S))[0]
    d_lb_logits = _lb_logits_grad(dlb, lb_logits)
    sc_2, tok = _scatter_start("scatter_2", [g_hout, g_hin], [0, 1])

    du2, dg_ffn0, db_ffn0 = _mm_nt_res_ln_bwd("hgrn_in_bwd", dP1, wh_in, du3, xh2, r2, ln_ffn_g[0:1], TH, D, tok,
                                              a_map=lambda i, j, k: (i + k * (S // TH), 0), mk=3 * D)
    dh0 = _mlp_down_bwd("mlp0_down_bwd", du2, w_down0, a0, T1, TF, D)
    g_down0 = wgrad("g_down0", a0, du2, TW, D)
    g_up0 = wgrad("g_up0", x1, dh0, D, TW)
    sc_3, tok = _scatter_start("scatter_3", [g_down0, g_up0], [0, 1])
    du1, dg_mix0, db_mix0 = _mm_nt_res_ln_bwd("mlp0_up_bwd", dh0, w_up0, du2, xh1, r1, ln_mix_g[0:1], TH, F, tok)
    do, delta = _attn_out_bwd(du1, wa_out, o_att, sel_t, TH, D)
    g_aout = wgrad("g_attn_out", o_att, du1, D, D)
    dP3 = _attn_bwd(P3, _stack_classes("do_classes", do, MXU_DTYPE), _stack_classes("lse_classes", L_att, F32),
                    _stack_classes("delta_classes", delta, F32), cos3, sin3, D)
    small = jnp.concatenate([d_lb_logits, dg_mix0, dg_mix1, db_mix0, db_mix1, dg_ffn0, dg_ffn1, db_ffn0, db_ffn1,
                             dg_norm, sq, jnp.zeros((4, D), F32)], axis=0)
    small = _all_reduce_small(small, dP3)
    loss = 0.5 * jnp.sum(small[11]) / D
    grp = lambda j: j // (3 * D // TG)
    g_ain = _matmul("g_attn_in", xc3, dP3, "tn", D, TG, T1, [(_sds((D, 9 * D), MXU_DTYPE), _ij_spec(D, TG))], _store_epilogue,
                    a_map=lambda i, j, k: (k + grp(j) * (S // T1), i),
                    b_map=lambda i, j, k: (k + grp(j) * (S // T1), j % (3 * D // TG)), mnk=(D, 9 * D, S), dep=small)[0]
    sc_4, tok = _scatter_start("scatter_4", [g_aout, g_ain], [0, 1])
    dxc3 = _matmul("attn_in_bwd", dP3, wa_in, "nt", TH, D, 3 * D, [(_sds((3 * S, D), F32), _ij_spec(TH, D))], _store_epilogue,
                   b_map=lambda i, j, k: (j, k + i // (S // TH)), mnk=(3 * S, D, 3 * D), dep=tok)[0]
    grad_x = _input_grad(du1, dxc3)

    def reduced(name, state, after):
        gs, lands = _scatter_wait(name + "_wait", state, after)
        return [_reduce_join(f"{name}_reduce_{i}", l, _own_piece(g, ax)) for i, (l, g, ax) in enumerate(zip(lands, gs, state[4]))]

    r_down1, r_up1 = reduced("scatter_1", sc_1, grad_x)
    r_hout, r_hin = reduced("scatter_2", sc_2, r_up1)
    r_down0, r_up0 = reduced("scatter_3", sc_3, r_hin)

    my_chip = 2 * lax.axis_index("x") + lax.axis_index("y")
    nsh = hgrn_norm_g.shape[1]
    g_norm = lax.dynamic_slice(small[10:11], (0, my_chip * nsh), (1, nsh))

    grads, upd = {}, {}

    def update(nm, w, gr, m, v):
        grads[nm] = gr.reshape(w.shape)
        upd[nm] = _adamw("adamw_" + nm, w, grads[nm], m, v)

    update("hgrn_w_in", hgrn_w_in, r_hin, m_hgrn_w_in, v_hgrn_w_in)
    update("hgrn_w_out", hgrn_w_out, r_hout, m_hgrn_w_out, v_hgrn_w_out)
    update("ffn_w_up", ffn_w_up, jnp.stack([r_up0, r_up1]), m_ffn_w_up, v_ffn_w_up)
    update("ffn_w_down", ffn_w_down, jnp.stack([r_down0, r_down1]), m_ffn_w_down, v_ffn_w_down)
    r_aout, r_ain = reduced("scatter_4", sc_4, upd["ffn_w_down"][2])
    update("attn_w_in", attn_w_in, r_ain, m_attn_w_in, v_attn_w_in)
    update("attn_w_out", attn_w_out, r_aout, m_attn_w_out, v_attn_w_out)
    grads["hgrn_norm_g"] = g_norm
    upd["hgrn_norm_g"] = _adamw("adamw_hgrn_norm_g", hgrn_norm_g, g_norm, m_hgrn_norm_g, v_hgrn_norm_g)
    cat = lambda ts: jnp.concatenate(ts, axis=0)
    small_w = cat([lb_logits, ln_mix_g, ln_mix_b, ln_ffn_g, ln_ffn_b])
    small_m = cat([m_lb_logits, m_ln_mix_g, m_ln_mix_b, m_ln_ffn_g, m_ln_ffn_b])
    small_v = cat([v_lb_logits, v_ln_mix_g, v_ln_mix_b, v_ln_ffn_g, v_ln_ffn_b])
    small_upd = _adamw("adamw_small", small_w, small[0:10], small_m, small_v)
    for i, nm in enumerate(["lb_logits", "ln_mix_g", "ln_mix_b", "ln_ffn_g", "ln_ffn_b"]):
        grads[nm] = small[2 * i:2 * i + 2]
        upd[nm] = tuple(t[2 * i:2 * i + 2] for t in small_upd)

    order = ["attn_w_in", "attn_w_out", "hgrn_w_in", "hgrn_w_out", "hgrn_norm_g", "lb_logits", "ln_mix_g", "ln_mix_b",
             "ln_ffn_g", "ln_ffn_b", "ffn_w_up", "ffn_w_down"]
    return (loss, grad_x[None], *[grads[k] for k in order], *[upd[k][0] for k in order],
            *[upd[k][1] for k in order], *[upd[k][2] for k in order])
```

```python
import math

import jax
import jax.numpy as jnp
from jax import lax
from jax.experimental import pallas as pl
from jax.experimental.pallas import tpu as pltpu

F32 = jnp.float32
BF16 = jnp.bfloat16
MXU_DTYPE = BF16

HEAD_DIM = 64
ATTN_BLK = 128
DILATIONS = (1, 4, 16)
ROPE_THETA = 10000.0
HGRN_DK = 128
HGRN_CHUNK = 64
DEPTH = 2
LN_EPS = 1e-5
RMS_EPS = 1e-6
ALPHA = (2 * DEPTH) ** 0.25
ADAM_LR, ADAM_B1, ADAM_B2, ADAM_EPS, ADAM_WD, ADAM_STEP = 0.001, 0.9, 0.999, 1e-08, 0.01, 10

LANES = 128
VMEM_LIMIT = 56 * 1024 * 1024
NEG = -1e30
MESH = pl.DeviceIdType.MESH


def _cparams(sem=None):
    return pltpu.CompilerParams(dimension_semantics=sem, vmem_limit_bytes=VMEM_LIMIT)


def _sds(shape, dtype):
    return jax.ShapeDtypeStruct(tuple(shape), dtype)


def _dg(a, b, ca, cb):
    return lax.dot_general(a, b, (((ca,), (cb,)), ((), ())), preferred_element_type=F32)


def _nn(a, b):
    return _dg(a, b, 1, 0)


def _nt(a, b):
    return _dg(a, b, 1, 1)


def _tn(a, b):
    return _dg(a, b, 0, 0)


def _split3(a):
    hi = a.astype(BF16)
    r = a - hi.astype(F32)
    mid = r.astype(BF16)
    lo = (r - mid.astype(F32)).astype(BF16)
    return hi, mid, lo


def _exact_nn(a, sel):
    hi, mid, lo = _split3(a)
    return _nn(hi, sel) + _nn(mid, sel) + _nn(lo, sel)


def _pick(n, prefs):
    for p in prefs:
        if n % p == 0:
            return p
    return n


def _matmul(name, a, b, form, tm, tn, tk, outs, epilogue, extras=(), a_map=None, b_map=None, mnk=None, dep=None,
            sem=("parallel", "parallel", "arbitrary")):
    if form == "nn":
        (M, K), N = a.shape, b.shape[1]
        a_spec = pl.BlockSpec((tm, tk), a_map or (lambda i, j, k: (i, k)))
        b_spec = pl.BlockSpec((tk, tn), b_map or (lambda i, j, k: (k, j)))
        ca, cb = 1, 0
    elif form == "nt":
        (M, K), N = a.shape, b.shape[0]
        a_spec = pl.BlockSpec((tm, tk), a_map or (lambda i, j, k: (i, k)))
        b_spec = pl.BlockSpec((tn, tk), b_map or (lambda i, j, k: (j, k)))
        ca, cb = 1, 1
    else:
        (K, M), N = a.shape, b.shape[1]
        a_spec = pl.BlockSpec((tk, tm), a_map or (lambda i, j, k: (k, i)))
        b_spec = pl.BlockSpec((tk, tn), b_map or (lambda i, j, k: (k, j)))
        ca, cb = 0, 0
    if mnk is not None:
        M, N, K = mnk
    assert M % tm == 0 and N % tn == 0 and K % tk == 0, (name, M, N, K, tm, tn, tk)
    nk = K // tk
    ne, no = len(extras), len(outs)
    deps = [] if dep is None else [dep]
    nd = len(deps)

    def body(a_ref, b_ref, *rest):
        extra_refs, out_refs = rest[:ne], rest[ne + nd:ne + nd + no]
        j = pl.program_id(1)
        part = _dg(a_ref[...].astype(MXU_DTYPE), b_ref[...].astype(MXU_DTYPE), ca, cb)
        if nk == 1:
            epilogue(part, extra_refs, out_refs, j)
            return
        acc_ref = rest[-1]
        k = pl.program_id(2)

        @pl.when(k == 0)
        def _():
            acc_ref[...] = part

        @pl.when(k > 0)
        def _():
            acc_ref[...] += part

        @pl.when(k == nk - 1)
        def _():
            epilogue(acc_ref[...], extra_refs, out_refs, j)

    res = pl.pallas_call(
        body,
        name=name,
        grid=(M // tm, N // tn, nk),
        in_specs=[a_spec, b_spec] + [s for _, s in extras] + [pl.BlockSpec(memory_space=pl.ANY)] * nd,
        out_specs=[s for _, s in outs],
        out_shape=[o for o, _ in outs],
        scratch_shapes=[pltpu.VMEM((tm, tn), F32)] if nk > 1 else [],
        compiler_params=_cparams(sem),
    )(a, b, *[e for e, _ in extras], *deps)
    return res


def _ij_spec(tm, tn):
    return pl.BlockSpec((tm, tn), lambda i, j, k: (i, j))


def _store_epilogue(acc, extra_refs, out_refs, j):
    out_refs[0][...] = acc.astype(out_refs[0].dtype)


def _plain_mm(name, a, b, form, out_dtype, tm, tn, tk):
    M = a.shape[1] if form == "tn" else a.shape[0]
    N = b.shape[0] if form == "nt" else b.shape[1]
    return _matmul(name, a, b, form, tm, tn, tk, [(_sds((M, N), out_dtype), _ij_spec(tm, tn))], _store_epilogue)[0]


def _class_slabs(S):
    assert DILATIONS[0] == 1
    return [(g, d, r, S // d) for g, d in enumerate(DILATIONS) if d > 1 for r in range(d)]


def _stack_classes(name, t, out_dtype):
    S, W = t.shape

    def body(x_ref, o_ref):
        o_ref[0:S, :] = x_ref[...].astype(out_dtype)
        for g, d, r, n in _class_slabs(S):
            o_ref[g * S + r * n:g * S + (r + 1) * n, :] = x_ref[pl.ds(r, n, stride=d), :].astype(out_dtype)

    return pl.pallas_call(
        body,
        name=name,
        grid=(W // LANES,),
        in_specs=[pl.BlockSpec((S, LANES), lambda j: (0, j))],
        out_specs=pl.BlockSpec((3 * S, LANES), lambda j: (0, j)),
        out_shape=_sds((3 * S, W), out_dtype),
        compiler_params=_cparams(("parallel",)),
    )(t)


def _rope_tables(seq):
    half = HEAD_DIM // 2
    inv = ROPE_THETA ** (-jnp.arange(half, dtype=F32) * (2.0 / HEAD_DIM))
    inv = jnp.tile(inv, LANES // half)
    pos = []
    for d in DILATIONS:
        row = jnp.arange(seq)
        pos.append((row % (seq // d)) * d + row // (seq // d))
    ang = jnp.concatenate(pos).astype(F32)[:, None] * inv[None, :]
    first = (jnp.arange(LANES) % HEAD_DIM) < half
    sin = jnp.sin(ang)
    return jnp.cos(ang), jnp.where(first[None, :], -sin, sin)


def _partner(x):
    half = HEAD_DIM // 2
    lane = lax.broadcasted_iota(jnp.int32, x.shape, 1)
    first = (lane % HEAD_DIM) < half
    return jnp.where(first, pltpu.roll(x, LANES - half, 1), pltpu.roll(x, half, 1))


def _attn_proj(x3, w_full, cos3, sin3, tm, tn):
    S3, D = x3.shape
    S = S3 // 3
    per_part = D // tn
    per_group = 3 * per_part

    def epilogue(acc, extra_refs, out_refs, j):
        cos_ref, sin_ref = extra_refs
        o_ref = out_refs[0]
        is_rot = j // per_part < 2

        @pl.when(is_rot)
        def _():
            c, s = cos_ref[...], sin_ref[...]
            for t in range(tn // LANES):
                xs = acc[:, t * LANES:(t + 1) * LANES]
                o_ref[:, t * LANES:(t + 1) * LANES] = (xs * c + _partner(xs) * s).astype(o_ref.dtype)

        @pl.when(jnp.logical_not(is_rot))
        def _():
            o_ref[...] = acc.astype(o_ref.dtype)

    tab = pl.BlockSpec((tm, LANES), lambda i, j, k: (i, 0))
    return _matmul("attn_proj", x3, w_full, "nn", tm, tn, D, [(_sds((S3, 3 * D), MXU_DTYPE), _ij_spec(tm, tn))],
                   epilogue, extras=[(cos3, tab), (sin3, tab)],
                   b_map=lambda i, j, k: (k, j + (i // (S // tm)) * per_group), mnk=(S3, 3 * D, D))[0]


def _head_sel(d_model):
    h = jnp.arange(LANES)[:, None]
    l = jnp.arange(d_model)[None, :]
    return (l // HEAD_DIM == h).astype(BF16)


def _class_edges(b, nblk):
    g = b // nblk
    per_class = jnp.where(g == 0, nblk // DILATIONS[0], jnp.where(g == 1, nblk // DILATIONS[1], nblk // DILATIONS[2]))
    pos = (b % nblk) % per_class
    return pos != 0, pos != per_class - 1


def _two_heads(t, top):
    zero = jnp.zeros_like(t)
    return jnp.concatenate([jnp.where(top, t, zero), jnp.where(top, zero, t)], axis=0)


def _band_mask(has_prev):
    B = ATTN_BLK
    row = lax.broadcasted_iota(jnp.int32, (2 * B, 2 * B), 0) % B
    col = lax.broadcasted_iota(jnp.int32, (2 * B, 2 * B), 1)
    in_prev = jnp.logical_and(jnp.logical_and(col < B, col >= row), has_prev)
    in_own = jnp.logical_and(col >= B, col - B <= row)
    return jnp.logical_or(in_prev, in_own)


def _attn_fwd(P3, D):
    S3 = P3.shape[0]
    B = ATTN_BLK
    nblk = S3 // 3 // B
    npairs = D // LANES
    scale = HEAD_DIM ** -0.5

    def body(q_ref, kc_ref, vc_ref, kp_ref, vp_ref, o_ref, lse_ref):
        has_prev, _ = _class_edges(pl.program_id(0), nblk)
        ok = _band_mask(has_prev)
        lane = lax.broadcasted_iota(jnp.int32, (B, LANES), 1)
        top = lane < HEAD_DIM
        lse_acc = jnp.zeros((B, LANES), F32)
        for j in range(npairs):
            sl = slice(j * LANES, (j + 1) * LANES)
            Q = _two_heads(q_ref[:, sl] * scale, top)
            K2 = jnp.concatenate([kp_ref[:, sl], kc_ref[:, sl]], axis=0)
            V2 = jnp.concatenate([vp_ref[:, sl], vc_ref[:, sl]], axis=0)
            s = jnp.where(ok, _nt(Q, K2), NEG)
            m = jnp.max(s, axis=1, keepdims=True)
            p = jnp.exp(s - m)
            l = jnp.sum(p, axis=1, keepdims=True)
            o = _nn((p * (1.0 / l)).astype(MXU_DTYPE), V2)
            o_ref[:, sl] = jnp.where(top, o[:B], o[B:])
            lse = m + jnp.log(l)
            lse_acc = jnp.where(lane == 2 * j, lse[:B], jnp.where(lane == 2 * j + 1, lse[B:], lse_acc))
        lse_ref[...] = lse_acc

    blk = lambda part, prev: pl.BlockSpec(
        (B, D), (lambda b: (jnp.maximum(b - 1, 0), part)) if prev else (lambda b: (b, part)))
    return pl.pallas_call(
        body,
        name="attn_fwd",
        grid=(3 * nblk,),
        in_specs=[blk(0, False), blk(1, False), blk(2, False), blk(1, True), blk(2, True)],
        out_specs=[pl.BlockSpec((B, D), lambda b: (b, 0)), pl.BlockSpec((B, LANES), lambda b: (b, 0))],
        out_shape=[_sds((S3, D), F32), _sds((S3, LANES), F32)],
        compiler_params=_cparams(("parallel",)),
    )(P3, P3, P3, P3, P3)


def _attn_mix(o3, lse3, sel):
    S3, D = o3.shape
    S = S3 // 3

    def body(o3_ref, lse_ref, sel_ref, o_ref, L_ref, w_ref):
        @pl.when(pl.program_id(0) == 0)
        def _():
            w_ref[0] = lse_ref[0:S, :]
            for g, d, r, n in _class_slabs(S):
                w_ref[g, pl.ds(r, n, stride=d), :] = lse_ref[g * S + r * n:g * S + (r + 1) * n, :]
            a, b, c = w_ref[0], w_ref[1], w_ref[2]
            m = jnp.maximum(jnp.maximum(a, b), c)
            L = m + jnp.log(jnp.exp(a - m) + jnp.exp(b - m) + jnp.exp(c - m))
            L_ref[...] = L
            w_ref[0] = jnp.exp(a - L)
            w_ref[1] = jnp.exp(b - L)
            w_ref[2] = jnp.exp(c - L)

        s = sel_ref[...]
        o_ref[...] = _exact_nn(w_ref[0], s) * o3_ref[0:S, :]
        for g, d, r, n in _class_slabs(S):
            rows = pl.ds(r, n, stride=d)
            o_ref[rows, :] += _exact_nn(w_ref[g, rows, :], s) * o3_ref[g * S + r * n:g * S + (r + 1) * n, :]

    return pl.pallas_call(
        body,
        name="attn_mix",
        grid=(D // LANES,),
        in_specs=[pl.BlockSpec((S3, LANES), lambda j: (0, j)), pl.BlockSpec((S3, LANES), lambda j: (0, 0)),
                  pl.BlockSpec((LANES, LANES), lambda j: (0, j))],
        out_specs=[pl.BlockSpec((S, LANES), lambda j: (0, j)), pl.BlockSpec((S, LANES), lambda j: (0, 0))],
        out_shape=[_sds((S, D), F32), _sds((S, LANES), F32)],
        scratch_shapes=[pltpu.VMEM((3, S, LANES), F32)],
        compiler_params=_cparams(("arbitrary",)),
    )(o3, lse3, sel)


def _attn_bwd(P3, do3, L3, delta3, cos3, sin3, D):
    S3 = P3.shape[0]
    B = ATTN_BLK
    nblk = S3 // 3 // B
    npairs = D // LANES
    scale = HEAD_DIM ** -0.5

    def body(c_ref, p_ref, n_ref, doc_ref, don_ref, Lc_ref, Ln_ref, dc_ref, dn_ref, cos_ref, sin_ref, out_ref):
        has_prev, has_next = _class_edges(pl.program_id(0), nblk)
        ok = _band_mask(has_prev)
        row = lax.broadcasted_iota(jnp.int32, (2 * B, B), 0) % B
        col = lax.broadcasted_iota(jnp.int32, (2 * B, B), 1)
        ok_n = jnp.logical_and(col >= row, has_next)
        lane = lax.broadcasted_iota(jnp.int32, (B, LANES), 1)
        top = lane < HEAD_DIM
        cos_t = cos_ref[...]
        sin_inv = -sin_ref[...]
        Lc_all, Ln_all, dc_all, dn_all = Lc_ref[...], Ln_ref[...], dc_ref[...], dn_ref[...]
        pair_col = lambda t, j: jnp.concatenate([t[:, 2 * j:2 * j + 1], t[:, 2 * j + 1:2 * j + 2]], axis=0)
        for j in range(npairs):
            sl = lambda part: slice(part * D + j * LANES, part * D + (j + 1) * LANES)
            kc2, vc2 = c_ref[:, sl(1)], c_ref[:, sl(2)]
            K2 = jnp.concatenate([p_ref[:, sl(1)], kc2], axis=0)
            V2 = jnp.concatenate([p_ref[:, sl(2)], vc2], axis=0)
            Qc = _two_heads(c_ref[:, sl(0)] * scale, top)
            Qn = _two_heads(n_ref[:, sl(0)] * scale, top)
            DOc = _two_heads(doc_ref[:, j * LANES:(j + 1) * LANES].astype(MXU_DTYPE), top)
            DOn = _two_heads(don_ref[:, j * LANES:(j + 1) * LANES].astype(MXU_DTYPE), top)
            P_c = jnp.where(ok, jnp.exp(_nt(Qc, K2) - pair_col(Lc_all, j)), 0.0)
            dS_c = P_c * (_nt(DOc, V2) - pair_col(dc_all, j))
            P_n = jnp.where(ok_n, jnp.exp(_nt(Qn, kc2) - pair_col(Ln_all, j)), 0.0)
            dS_n = P_n * (_nt(DOn, vc2) - pair_col(dn_all, j))
            dq = _nn(dS_c.astype(MXU_DTYPE), K2)
            dq2 = jnp.where(top, dq[:B], dq[B:]) * scale
            Qk = jnp.concatenate([Qc, Qn], axis=0)
            DOk = jnp.concatenate([DOc, DOn], axis=0)
            dk2 = _tn(jnp.concatenate([dS_c[:, B:], dS_n], axis=0).astype(MXU_DTYPE), Qk)
            dv2 = _tn(jnp.concatenate([P_c[:, B:], P_n], axis=0).astype(MXU_DTYPE), DOk)
            out_ref[:, sl(0)] = (dq2 * cos_t + _partner(dq2) * sin_inv).astype(out_ref.dtype)
            out_ref[:, sl(1)] = (dk2 * cos_t + _partner(dk2) * sin_inv).astype(out_ref.dtype)
            out_ref[:, sl(2)] = dv2.astype(out_ref.dtype)

    cur = lambda b: b
    prv = lambda b: jnp.maximum(b - 1, 0)
    nxt = lambda b: jnp.minimum(b + 1, 3 * nblk - 1)
    spec = lambda w, f: pl.BlockSpec((B, w), lambda b: (f(b), 0))
    return pl.pallas_call(
        body,
        name="attn_bwd",
        grid=(3 * nblk,),
        in_specs=[spec(3 * D, cur), spec(3 * D, prv), spec(3 * D, nxt), spec(D, cur), spec(D, nxt),
                  spec(LANES, cur), spec(LANES, nxt), spec(LANES, cur), spec(LANES, nxt), spec(LANES, cur), spec(LANES, cur)],
        out_specs=spec(3 * D, cur),
        out_shape=_sds((S3, 3 * D), MXU_DTYPE),
        compiler_params=_cparams(("parallel",)),
    )(P3, P3, P3, do3, do3, L3, L3, delta3, delta3, cos3, sin3)


def _input_grad(du, dx3):
    S, D = du.shape

    def body(du_ref, dx_ref, o_ref):
        o_ref[...] = ALPHA * du_ref[...] + dx_ref[0:S, :]
        for g, d, r, n in _class_slabs(S):
            o_ref[pl.ds(r, n, stride=d), :] += dx_ref[g * S + r * n:g * S + (r + 1) * n, :]

    return pl.pallas_call(
        body,
        name="input_grad",
        grid=(D // LANES,),
        in_specs=[pl.BlockSpec((S, LANES), lambda j: (0, j)), pl.BlockSpec((3 * S, LANES), lambda j: (0, j))],
        out_specs=pl.BlockSpec((S, LANES), lambda j: (0, j)),
        out_shape=_sds((S, D), F32),
        compiler_params=_cparams(("parallel",)),
    )(du, dx3)


def _chunk_causal(tb):
    r = lax.broadcasted_iota(jnp.int32, (tb, tb), 0)
    c = lax.broadcasted_iota(jnp.int32, (tb, tb), 1)
    return jnp.logical_and((r // HGRN_CHUNK) == (c // HGRN_CHUNK), r >= c)


def _chunk_sums(a, lower):
    C = HGRN_CHUNK
    r = lax.broadcasted_iota(jnp.int32, (C, C), 0)
    c = lax.broadcasted_iota(jnp.int32, (C, C), 1)
    tri = ((r >= c) if lower else (r <= c)).astype(BF16)
    parts = _split3(a)
    out = []
    for ci in range(a.shape[0] // C):
        rows = slice(ci * C, (ci + 1) * C)
        out.append(_nn(tri, parts[0][rows]) + _nn(tri, parts[1][rows]) + _nn(tri, parts[2][rows]))
    return jnp.concatenate(out, axis=0)


def _chunk_last(b):
    C = HGRN_CHUNK
    return jnp.concatenate([jnp.broadcast_to(b[(ci + 1) * C - 1:(ci + 1) * C, :], (C, b.shape[1]))
                            for ci in range(b.shape[0] // C)], axis=0)


def _lower_bound(lb_ref):
    l0, l1 = lb_ref[0:1, :], lb_ref[1:2, :]
    m = jnp.maximum(l0, l1)
    e0, e1 = jnp.exp(l0 - m), jnp.exp(l1 - m)
    return e1 / (e0 + e1)


def _hgrn_gates(q_raw, z, lb):
    sg = 1.0 / (1.0 + jnp.exp(-z))
    sn = 1.0 / (1.0 + jnp.exp(z))
    f = lb + (1.0 - lb) * sg
    key = (1.0 - lb) * sn
    sq = 1.0 / (1.0 + jnp.exp(-q_raw))
    return sg, sn, f, key, sq


HGRN_HEADS_PER_STEP = 2


def _hgrn_fwd(P1, lb_logits, norm_g, tb):
    S = P1.shape[0]
    D = P1.shape[1] // 3
    K = HGRN_DK
    H = D // K
    HP = HGRN_HEADS_PER_STEP
    C = HGRN_CHUNK
    cpb = tb // C
    nt = S // tb

    def body(q_ref, f_ref, i_ref, lb_ref, g_ref, o_ref, n_ref, st_ref, state):
        t = pl.program_id(1)

        @pl.when(t == 0)
        def _():
            state[...] = jnp.zeros_like(state)

        lb_all = _lower_bound(lb_ref)
        low = _chunk_causal(tb)
        for hh in range(HP):
            lanes = slice(hh * K, (hh + 1) * K)
            q_raw, z, v = q_ref[:, lanes], f_ref[:, lanes], i_ref[:, lanes]
            sg, sn, f, key, sq = _hgrn_gates(q_raw, z, lb_all[:, lanes])
            b = _chunk_sums(jnp.log(f), lower=True)
            qd = (q_raw * sq * jnp.exp(b)).astype(MXU_DTYPE)
            kd = (key * jnp.exp(-b)).astype(MXU_DTYPE)
            kb = (key * jnp.exp(_chunk_last(b) - b)).astype(MXU_DTYPE)
            vm = v.astype(MXU_DTYPE)
            a = jnp.where(low, _nt(qd, kd), 0.0).astype(MXU_DTYPE)
            o_intra = _nn(a, vm)
            st = state[hh]
            outs = []
            for ci in range(cpb):
                rows = slice(ci * C, (ci + 1) * C)
                st_ref[hh, ci] = st
                outs.append(o_intra[rows] + _nt(qd[rows], st.astype(MXU_DTYPE)))
                st = st * jnp.exp(b[(ci + 1) * C - 1:(ci + 1) * C, :]) + _tn(vm[rows], kb[rows])
            state[hh] = st
            o = jnp.concatenate(outs, axis=0)
            o_ref[:, lanes] = o
            rs = lax.rsqrt(jnp.mean(o * o, axis=1, keepdims=True) + RMS_EPS)
            n_ref[:, lanes] = o * rs * g_ref[:, lanes]

    tok = lambda part: pl.BlockSpec((tb, HP * K), lambda h, t: (t, part * (H // HP) + h))
    vec = lambda rows: pl.BlockSpec((rows, HP * K), lambda h, t: (0, h))
    return pl.pallas_call(
        body,
        name="hgrn_fwd",
        grid=(H // HP, nt),
        in_specs=[tok(0), tok(1), tok(2), vec(2), vec(1)],
        out_specs=[tok(0), tok(0), pl.BlockSpec((HP, cpb, K, K), lambda h, t: (h, t, 0, 0))],
        out_shape=[_sds((S, D), F32), _sds((S, D), F32), _sds((H, S // C, K, K), F32)],
        scratch_shapes=[pltpu.VMEM((HP, K, K), F32)],
        compiler_params=_cparams(("parallel", "arbitrary")),
    )(P1, P1, P1, lb_logits, norm_g)


def _hgrn_bwd(P1, o_pre, states, dn, lb_logits, norm_g, tb):
    S = P1.shape[0]
    D = P1.shape[1] // 3
    K = HGRN_DK
    H = D // K
    HP = HGRN_HEADS_PER_STEP
    C = HGRN_CHUNK
    cpb = tb // C
    nt = S // tb

    def body(q_ref, f_ref, i_ref, o_ref, st_ref, dn_ref, lb_ref, g_ref, d_ref, dg_ref, dlb_ref, dstate):
        t = pl.program_id(1)

        @pl.when(t == 0)
        def _():
            dstate[...] = jnp.zeros_like(dstate)
            dg_ref[...] = jnp.zeros_like(dg_ref)
            dlb_ref[...] = jnp.zeros_like(dlb_ref)

        lb_all = _lower_bound(lb_ref)
        low = _chunk_causal(tb)
        for hh in range(HP):
            lanes = slice(hh * K, (hh + 1) * K)
            lb = lb_all[:, lanes]
            gn = g_ref[:, lanes]
            q_raw, z, v = q_ref[:, lanes], f_ref[:, lanes], i_ref[:, lanes]
            sg, sn, f, key, sq = _hgrn_gates(q_raw, z, lb)
            b = _chunk_sums(jnp.log(f), lower=True)
            e_pos, e_neg, e_rel = jnp.exp(b), jnp.exp(-b), jnp.exp(_chunk_last(b) - b)
            qd_f, kd_f, kb_f = q_raw * sq * e_pos, key * e_neg, key * e_rel
            qd, kd, kb = qd_f.astype(MXU_DTYPE), kd_f.astype(MXU_DTYPE), kb_f.astype(MXU_DTYPE)
            vm = v.astype(MXU_DTYPE)
            a = jnp.where(low, _nt(qd, kd), 0.0).astype(MXU_DTYPE)
            o = o_ref[:, lanes]
            dnn = dn_ref[:, lanes]
            rs = lax.rsqrt(jnp.mean(o * o, axis=1, keepdims=True) + RMS_EPS)
            dg_ref[:, lanes] += jnp.sum(dnn * o * rs, axis=0, keepdims=True)
            tg = dnn * gn
            dom = (rs * tg - o * (rs * rs * rs) * jnp.mean(tg * o, axis=1, keepdims=True)).astype(MXU_DTYPE)
            da = jnp.where(low, _nt(dom, vm), 0.0).astype(MXU_DTYPE)
            dv = _tn(a, dom)
            dqd = _nn(da, kd)
            dkd = _tn(da, qd)
            dst = dstate[hh]
            dv_s, dqd_s, dkb_s, dbl_s = [None] * cpb, [None] * cpb, [None] * cpb, [None] * cpb
            for ci in reversed(range(cpb)):
                rows = slice(ci * C, (ci + 1) * C)
                st = st_ref[hh, ci]
                dstm = dst.astype(MXU_DTYPE)
                dec = jnp.exp(b[(ci + 1) * C - 1:(ci + 1) * C, :])
                dv_s[ci] = _nt(kb[rows], dstm)
                dkb_s[ci] = _nn(vm[rows], dstm)
                dqd_s[ci] = _nn(dom[rows], st.astype(MXU_DTYPE))
                db_last = jnp.sum(dkb_s[ci] * kb_f[rows], axis=0, keepdims=True) + jnp.sum(dst * st, axis=0, keepdims=True) * dec
                dbl_s[ci] = jnp.broadcast_to(db_last, (C, K))
                dst = dst * dec + _tn(dom[rows], qd[rows])
            dstate[hh] = dst
            dv = dv + jnp.concatenate(dv_s, axis=0)
            dqd = dqd + jnp.concatenate(dqd_s, axis=0)
            dkb = jnp.concatenate(dkb_s, axis=0)
            dkey = dkd * e_neg + dkb * e_rel
            db = dqd * qd_f - dkd * kd_f - dkb * kb_f
            dlogf = _chunk_sums(db, lower=False) + jnp.concatenate(dbl_s, axis=0)
            gz = (1.0 - lb) * sg * sn
            d_ref[0, :, lanes] = (dqd * e_pos * (sq + q_raw * sq * (1.0 - sq))).astype(d_ref.dtype)
            d_ref[1, :, lanes] = (dlogf * gz / f - dkey * gz).astype(d_ref.dtype)
            d_ref[2, :, lanes] = dv.astype(d_ref.dtype)
            dlb_ref[:, lanes] += jnp.sum(dlogf * sn / f - dkey * sn, axis=0, keepdims=True)

    rev = lambda t: nt - 1 - t
    tok = lambda part: pl.BlockSpec((tb, HP * K), lambda h, t: (rev(t), part * (H // HP) + h))
    vec = lambda rows: pl.BlockSpec((rows, HP * K), lambda h, t: (0, h))
    outs = pl.pallas_call(
        body,
        name="hgrn_bwd",
        grid=(H // HP, nt),
        in_specs=[tok(0), tok(1), tok(2), tok(0),
                  pl.BlockSpec((HP, cpb, K, K), lambda h, t: (h, rev(t), 0, 0)),
                  tok(0), vec(2), vec(1)],
        out_specs=[pl.BlockSpec((3, tb, HP * K), lambda h, t: (0, rev(t), h)), vec(1), vec(1)],
        out_shape=[_sds((3, S, D), MXU_DTYPE)] + [_sds((1, D), F32)] * 2,
        scratch_shapes=[pltpu.VMEM((HP, K, K), F32)],
        compiler_params=_cparams(("parallel", "arbitrary")),
    )(P1, P1, P1, o_pre, states, dn, lb_logits, norm_g)
    return outs


def _lb_logits_grad(dlb, lb_logits):
    def body(d_ref, l_ref, o_ref):
        s1 = _lower_bound(l_ref)
        d = d_ref[...]
        o_ref[0:1, :] = -(1.0 - s1) * s1 * d
        o_ref[1:2, :] = s1 * (1.0 - s1) * d

    return pl.pallas_call(body, name="lb_logits_grad", out_shape=_sds(lb_logits.shape, F32))(dlb, lb_logits)


def _ln_epilogue(acc, extra_refs, out_refs, j):
    res_ref, g_ref, b_ref = extra_refs
    x_ref, xhat_ref, rstd_ref = out_refs
    u = ALPHA * res_ref[...] + acc
    mu = jnp.mean(u, axis=1, keepdims=True)
    cen = u - mu
    rstd = lax.rsqrt(jnp.mean(cen * cen, axis=1, keepdims=True) + LN_EPS)
    xhat = cen * rstd
    xhat_ref[...] = xhat
    x_ref[...] = xhat * g_ref[...] + b_ref[...]
    rstd_ref[...] = rstd


def _mm_res_ln(name, a, w_full, res, g, b, tm, tk):
    S, D = res.shape
    row = pl.BlockSpec((tm, D), lambda i, j, k: (i, 0))
    vec = pl.BlockSpec((1, D), lambda i, j, k: (0, 0))
    outs = [(_sds((S, D), F32), row), (_sds((S, D), F32), row),
            (_sds((S, 1), F32), pl.BlockSpec((tm, 1), lambda i, j, k: (i, 0)))]
    return _matmul(name, a, w_full, "nn", tm, D, tk, outs, _ln_epilogue, extras=[(res, row), (g, vec), (b, vec)])


def _ln_bwd_rows(dy, xh, rstd, g, first, du_ref, dg_ref, db_ref):
    @pl.when(first)
    def _():
        dg_ref[...] = jnp.zeros_like(dg_ref)
        db_ref[...] = jnp.zeros_like(db_ref)

    dg_ref[...] += jnp.sum(dy * xh, axis=0, keepdims=True)
    db_ref[...] += jnp.sum(dy, axis=0, keepdims=True)
    dxh = dy * g
    m1 = jnp.mean(dxh, axis=1, keepdims=True)
    m2 = jnp.mean(dxh * xh, axis=1, keepdims=True)
    du_ref[...] = rstd * (dxh - m1 - xh * m2)


def _loss_ln_bwd(y, target, xhat, rstd, g, tm):
    S, D = y.shape

    def body(y_ref, t_ref, xh_ref, r_ref, g_ref, sq_ref, du_ref, dg_ref, db_ref):
        first = pl.program_id(0) == 0

        @pl.when(first)
        def _():
            sq_ref[...] = jnp.zeros_like(sq_ref)

        e = y_ref[...] - t_ref[...]
        sq_ref[...] += jnp.sum(e * e, axis=0, keepdims=True)
        _ln_bwd_rows(e / D, xh_ref[...], r_ref[...], g_ref[...], first, du_ref, dg_ref, db_ref)

    row = pl.BlockSpec((tm, D), lambda i: (i, 0))
    vec = pl.BlockSpec((1, D), lambda i: (0, 0))
    return pl.pallas_call(
        body,
        name="loss_ln_bwd",
        grid=(S // tm,),
        in_specs=[row, row, row, pl.BlockSpec((tm, 1), lambda i: (i, 0)), vec],
        out_specs=[vec, row, vec, vec],
        out_shape=[_sds((1, D), F32), _sds((S, D), F32), _sds((1, D), F32), _sds((1, D), F32)],
        compiler_params=_cparams(("arbitrary",)),
    )(y, target, xhat, rstd, g)


def _mlp_up(name, x, w_up, tm, tn, tk):
    S = x.shape[0]
    F = w_up.shape[1]

    def epilogue(acc, extra_refs, out_refs, j):
        r = jnp.maximum(acc, 0.0)
        out_refs[0][...] = (r * r).astype(out_refs[0].dtype)

    return _matmul(name, x, w_up, "nn", tm, tn, tk, [(_sds((S, F), MXU_DTYPE), _ij_spec(tm, tn))], epilogue)[0]


def _mlp_down_bwd(name, dy, w_down, a, tm, tn, tk):
    S, F = a.shape

    def epilogue(acc, extra_refs, out_refs, j):
        out_refs[0][...] = (acc * (2.0 * jnp.sqrt(extra_refs[0][...].astype(F32)))).astype(out_refs[0].dtype)

    return _matmul(name, dy, w_down, "nt", tm, tn, tk, [(_sds((S, F), MXU_DTYPE), _ij_spec(tm, tn))], epilogue,
                   extras=[(a, _ij_spec(tm, tn))])[0]


def _mm_nt_res_ln_bwd(name, dy, w, du, xhat, rstd, g, tm, tk, dep, a_map=None, mk=None):
    S, D = du.shape

    def epilogue(acc, extra_refs, out_refs, j):
        du_ref, xh_ref, r_ref, g_ref = extra_refs
        _ln_bwd_rows(ALPHA * du_ref[...] + acc, xh_ref[...], r_ref[...], g_ref[...], pl.program_id(0) == 0, *out_refs)

    row = pl.BlockSpec((tm, D), lambda i, j, k: (i, 0))
    vec = pl.BlockSpec((1, D), lambda i, j, k: (0, 0))
    return _matmul(name, dy, w, "nt", tm, D, tk, [(_sds((S, D), F32), row), (_sds((1, D), F32), vec), (_sds((1, D), F32), vec)],
                   epilogue, extras=[(du, row), (xhat, row), (rstd, pl.BlockSpec((tm, 1), lambda i, j, k: (i, 0))), (g, vec)],
                   a_map=a_map, mnk=None if mk is None else (S, D, mk), dep=dep, sem=("arbitrary", "arbitrary", "arbitrary"))


def _attn_out_bwd(du, w_out, o, sel_t, tm, tk):
    S, D = o.shape

    def epilogue(acc, extra_refs, out_refs, j):
        out_refs[0][...] = acc
        out_refs[1][...] = _exact_nn(acc * extra_refs[0][...], extra_refs[1][...])

    row = pl.BlockSpec((tm, D), lambda i, j, k: (i, 0))
    slim = pl.BlockSpec((tm, LANES), lambda i, j, k: (i, 0))
    return _matmul("attn_out_bwd", du, w_out, "nt", tm, D, tk,
                   [(_sds((S, D), F32), row), (_sds((S, LANES), F32), slim)], epilogue,
                   extras=[(o, row), (sel_t, pl.BlockSpec((D, LANES), lambda i, j, k: (0, 0)))])


def _adamw(name, w, gs, m, v):
    shape = w.shape
    cols = shape[-1]
    rows = math.prod(shape[:-1])
    w2, m2, v2 = (t.reshape(rows, cols) for t in (w, m, v))
    gs2 = [g.reshape(-1, cols) for g in gs]
    ng = len(gs2)
    tr = _pick(rows // ng, (256, 128, 64, 32, 16, 8))
    per = rows // ng // tr
    c1 = 1.0 - ADAM_B1 ** ADAM_STEP
    c2 = 1.0 - ADAM_B2 ** ADAM_STEP

    def body(w_ref, m_ref, v_ref, *rest):
        g_refs, (d_ref, nm_ref, nv_ref), g_out = rest[:ng], rest[ng:ng + 3], rest[ng + 3:]
        gg = g_refs[0][...]
        if ng == 2:
            gg = jnp.where(pl.program_id(0) < per, gg, g_refs[1][...])
            g_out[0][...] = gg
        nm = ADAM_B1 * m_ref[...] + (1.0 - ADAM_B1) * gg
        nv = ADAM_B2 * v_ref[...] + (1.0 - ADAM_B2) * (gg * gg)
        nm_ref[...] = nm
        nv_ref[...] = nv
        d_ref[...] = -ADAM_LR * ((nm / c1) / (jnp.sqrt(nv / c2) + ADAM_EPS) + ADAM_WD * w_ref[...])

    blk = pl.BlockSpec((tr, cols), lambda i: (i, 0))
    g_specs = [blk] if ng == 1 else [pl.BlockSpec((tr, cols), lambda i: (jnp.minimum(i, per - 1), 0)),
                                     pl.BlockSpec((tr, cols), lambda i: (jnp.maximum(i - per, 0), 0))]
    nout = 3 if ng == 1 else 4
    outs = pl.pallas_call(
        body,
        name=name,
        grid=(rows // tr,),
        in_specs=[blk] * 3 + g_specs,
        out_specs=[blk] * nout,
        out_shape=[_sds((rows, cols), F32)] * nout,
        compiler_params=_cparams(("parallel",)),
    )(w2, m2, v2, *gs2)
    g_full = outs[3] if ng == 2 else gs2[0]
    return tuple(o.reshape(shape) for o in (outs[0], outs[1], outs[2], g_full))


HBM = pl.BlockSpec(memory_space=pl.ANY)


def _shard_slice(ref, axis, size, index):
    idx = [slice(None)] * len(ref.shape)
    idx[axis] = pl.ds(pl.multiple_of(index * size, 8), size)
    return ref.at[tuple(idx)]


IN_HBM = pl.BlockSpec(memory_space=pltpu.HBM)
IN_SEM = pl.BlockSpec(memory_space=pltpu.SEMAPHORE)
DATAFLOW = pltpu.SideEffectType.DATAFLOW_SIDE_EFFECTING


def _hbm(t):
    return pltpu.with_memory_space_constraint(t, pltpu.HBM)


def _token_spec():
    return pl.BlockSpec(memory_space=pltpu.VMEM)


def _gather_copies(s_refs, f_refs, axes, send, recv, loc, arrival):
    x, y, c = lax.axis_index("x"), lax.axis_index("y"), lax.axis_index("c")
    chips = [(1 - x, y), (x, 1 - y), (1 - x, 1 - y)]
    local, remote = [], []
    for a in range(len(s_refs)):
        size = s_refs[a].shape[axes[a]]
        local.append(pltpu.make_async_copy(s_refs[a], _shard_slice(f_refs[a], axes[a], size, 2 * x + y), loc.at[a]))
        for k, (px, py) in enumerate(chips):
            block = (2 * px + py) if arrival else (2 * x + y)
            remote.append(pltpu.make_async_remote_copy(
                src_ref=s_refs[a], dst_ref=_shard_slice(f_refs[a], axes[a], size, block), send_sem=send.at[3 * a + k],
                recv_sem=recv.at[3 * a + k], device_id=(px, py, c), device_id_type=MESH))
    return local, remote


def _gather_start(name, shards, axes, after):
    n = len(shards)
    fulls = []
    for s, ax in zip(shards, axes):
        fs = list(s.shape)
        fs[ax] *= 4
        fulls.append(lax.empty(tuple(fs), s.dtype))

    def body(*refs):
        s_refs, f_refs = refs[:n], refs[n:2 * n]
        send, recv, loc, token = refs[2 * n + 1], refs[2 * n + 2], refs[2 * n + 3], refs[-1]
        local, remote = _gather_copies(s_refs, f_refs, axes, send, recv, loc, arrival=False)
        for cp in remote + local:
            cp.start()
        token[...] = jnp.zeros_like(token)

    outs = pl.pallas_call(
        body,
        name=name,
        out_shape=(pltpu.SemaphoreType.DMA((3 * n,)), pltpu.SemaphoreType.DMA((3 * n,)), pltpu.SemaphoreType.DMA((n,)),
                   *[pltpu.HBM(t.shape, t.dtype) for t in shards + fulls], _sds((8, LANES), F32)),
        in_specs=[IN_HBM] * (2 * n) + [HBM],
        out_specs=(IN_SEM, IN_SEM, IN_SEM, *[IN_HBM] * (2 * n), _token_spec()),
        input_output_aliases={i: 3 + i for i in range(2 * n)},
        compiler_params=pltpu.CompilerParams(has_side_effects=DATAFLOW),
    )(*[_hbm(t) for t in shards + fulls], after)
    return (outs[0], outs[1], outs[2], list(outs[3:3 + n]), list(outs[3 + n:3 + 2 * n]), axes), outs[-1]


def _gather_wait(name, state, *after):
    send, recv, loc, s_thru, f_thru, axes = state
    n = len(s_thru)

    def body(*refs):
        s_refs, f_refs = refs[:n], refs[n:2 * n]
        local, remote = _gather_copies(s_refs, f_refs, axes, refs[2 * n], refs[2 * n + 1], refs[2 * n + 2], arrival=True)
        for cp in local:
            cp.wait()
        for cp in remote:
            cp.wait_send()
            cp.wait_recv()

    outs = pl.pallas_call(
        body,
        name=name,
        out_shape=tuple(pltpu.HBM(t.shape, t.dtype) for t in s_thru + f_thru),
        in_specs=[IN_HBM] * (2 * n) + [IN_SEM, IN_SEM, IN_SEM] + [HBM] * len(after),
        out_specs=tuple([IN_HBM] * (2 * n)),
        input_output_aliases={i: i for i in range(2 * n)},
        compiler_params=pltpu.CompilerParams(has_side_effects=DATAFLOW),
    )(*s_thru, *f_thru, send, recv, loc, *after)
    return list(outs[n:2 * n])


FLIPS = [(fx, fy, fc) for fx in (0, 1) for fy in (0, 1) for fc in (0, 1)][1:]


def _piece_shape(shape, axis):
    ps = list(shape)
    if axis == 0:
        ps[0] //= 8
    else:
        ps[0] //= 2
        ps[axis] //= 4
    return tuple(ps)


def _piece(ref, axis, q, c):
    shape = ref.shape
    idx = [slice(None)] * len(shape)
    if axis == 0:
        h = shape[0] // 8
        idx[0] = pl.ds(pl.multiple_of((2 * q + c) * h, 8), h)
    else:
        h, w = shape[0] // 2, shape[axis] // 4
        idx[0] = pl.ds(c * h, h)
        idx[axis] = pl.ds(pl.multiple_of(q * w, LANES if axis == len(shape) - 1 else 8), w)
    return ref.at[tuple(idx)]


def _own_piece(g, axis):
    ps = _piece_shape(g.shape, axis)
    q, c = 2 * lax.axis_index("x") + lax.axis_index("y"), lax.axis_index("c")
    start = [0] * len(ps)
    if axis == 0:
        start[0] = (2 * q + c) * ps[0]
    else:
        start[0] = c * ps[0]
        start[axis] = q * ps[axis]
    return lax.dynamic_slice(g, start, ps)


def _scatter_copies(g_refs, l_refs, axes, send, recv):
    x, y, c = lax.axis_index("x"), lax.axis_index("y"), lax.axis_index("c")
    out = []
    for a in range(len(g_refs)):
        for k, (fx, fy, fc) in enumerate(FLIPS):
            tx, ty, tc = x ^ fx, y ^ fy, c ^ fc
            out.append(pltpu.make_async_remote_copy(
                src_ref=_piece(g_refs[a], axes[a], 2 * tx + ty, tc), dst_ref=l_refs[a].at[k],
                send_sem=send.at[7 * a + k], recv_sem=recv.at[7 * a + k], device_id=(tx, ty, tc), device_id_type=MESH))
    return out


def _scatter_start(name, grads, axes):
    n = len(grads)
    lands = [lax.empty((7,) + _piece_shape(g.shape, ax), g.dtype) for g, ax in zip(grads, axes)]

    def body(*refs):
        g_refs, l_refs = refs[:n], refs[n:2 * n]
        send, recv, token = refs[2 * n], refs[2 * n + 1], refs[-1]
        for cp in _scatter_copies(g_refs, l_refs, axes, send, recv):
            cp.start()
        token[...] = jnp.zeros_like(token)

    outs = pl.pallas_call(
        body,
        name=name,
        out_shape=(pltpu.SemaphoreType.DMA((7 * n,)), pltpu.SemaphoreType.DMA((7 * n,)),
                   *[pltpu.HBM(t.shape, t.dtype) for t in grads + lands], _sds((8, LANES), F32)),
        in_specs=[IN_HBM] * (2 * n),
        out_specs=(IN_SEM, IN_SEM, *[IN_HBM] * (2 * n), _token_spec()),
        input_output_aliases={i: 2 + i for i in range(2 * n)},
        compiler_params=pltpu.CompilerParams(has_side_effects=DATAFLOW),
    )(*[_hbm(t) for t in grads + lands])
    return (outs[0], outs[1], list(outs[2:2 + n]), list(outs[2 + n:2 + 2 * n]), axes), outs[-1]


def _scatter_wait(name, state, *after):
    send, recv, g_thru, l_thru, axes = state
    n = len(g_thru)

    def body(*refs):
        g_refs, l_refs = refs[:n], refs[n:2 * n]
        for cp in _scatter_copies(g_refs, l_refs, axes, refs[2 * n], refs[2 * n + 1]):
            cp.wait_send()
            cp.wait_recv()

    outs = pl.pallas_call(
        body,
        name=name,
        out_shape=tuple(pltpu.HBM(t.shape, t.dtype) for t in g_thru + l_thru),
        in_specs=[IN_HBM] * (2 * n) + [IN_SEM, IN_SEM] + [HBM] * len(after),
        out_specs=tuple([IN_HBM] * (2 * n)),
        input_output_aliases={i: i for i in range(2 * n)},
        compiler_params=pltpu.CompilerParams(has_side_effects=DATAFLOW),
    )(*g_thru, *l_thru, send, recv, *after)
    return list(outs[:n]), list(outs[n:2 * n])


def _reduce_join(name, landing, own):
    piece = own.shape
    C = piece[-1]
    R = math.prod(piece[:-1])
    l3 = landing.reshape(7, R, C)
    own2 = own.reshape(R, C)
    tr = _pick(R, [t for t in (512, 256, 128, 64, 32, 16, 8) if t * C <= 256 * 1024])
    nsteps = R // tr

    def body(own_ref, l_ref, o_ref, buf, send, loc, recv):
        i = pl.program_id(0)
        x, y, c = lax.axis_index("x"), lax.axis_index("y"), lax.axis_index("c")
        sibling = (x, y, 1 - c)

        def copies(slot, step):
            dst = o_ref.at[pl.ds(pl.multiple_of(c * R + step * tr, 8), tr), :]
            return (pltpu.make_async_copy(buf.at[slot], dst, loc.at[slot]),
                    pltpu.make_async_remote_copy(src_ref=buf.at[slot], dst_ref=dst, send_sem=send.at[slot], recv_sem=recv,
                                                 device_id=sibling, device_id_type=MESH))

        @pl.when(i >= 2)
        def _():
            lc, rc = copies(i % 2, i - 2)
            lc.wait()
            rc.wait_send()

        acc = own_ref[...].astype(F32)
        for s in range(7):
            acc = acc + l_ref[s].astype(F32)
        buf[i % 2] = acc
        lc, rc = copies(i % 2, i)
        lc.start()
        rc.start()

        @pl.when(i == nsteps - 1)
        def _():
            for st in range(max(nsteps - 2, 0), nsteps):
                lc, rc = copies(st % 2, st)
                lc.wait()
                rc.wait_send()
            theirs = o_ref.at[pl.ds(pl.multiple_of((1 - c) * R, 8), R), :]
            pltpu.make_async_remote_copy(src_ref=theirs, dst_ref=theirs, send_sem=send.at[0], recv_sem=recv,
                                         device_id=sibling, device_id_type=MESH).wait_recv()

    return pl.pallas_call(
        body,
        name=name,
        grid=(nsteps,),
        in_specs=[pl.BlockSpec((tr, C), lambda i: (i, 0)), pl.BlockSpec((7, tr, C), lambda i: (0, i, 0))],
        out_specs=HBM,
        out_shape=_sds((2 * R, C), F32),
        scratch_shapes=[pltpu.VMEM((2, tr, C), F32), pltpu.SemaphoreType.DMA((2,)), pltpu.SemaphoreType.DMA((2,)),
                        pltpu.SemaphoreType.DMA(())],
        compiler_params=_cparams(("arbitrary",)),
    )(own2, l3)


def _all_reduce_small(v, dep):
    R, D = v.shape

    def body(v_ref, dep_ref, o_ref, land, send, recv):
        x, y, c = lax.axis_index("x"), lax.axis_index("y"), lax.axis_index("c")
        my_slot = 4 * x + 2 * y + c
        land[my_slot] = v_ref[...]
        for k, (fx, fy, fc) in enumerate(FLIPS):
            tx, ty, tc = x ^ fx, y ^ fy, c ^ fc
            pltpu.make_async_remote_copy(src_ref=v_ref, dst_ref=land.at[my_slot], send_sem=send.at[k], recv_sem=recv.at[k],
                                         device_id=(tx, ty, tc), device_id_type=MESH).start()
        for k, (fx, fy, fc) in enumerate(FLIPS):
            tx, ty, tc = x ^ fx, y ^ fy, c ^ fc
            cp = pltpu.make_async_remote_copy(src_ref=v_ref, dst_ref=land.at[4 * tx + 2 * ty + tc], send_sem=send.at[k],
                                              recv_sem=recv.at[k], device_id=(tx, ty, tc), device_id_type=MESH)
            cp.wait_send()
            cp.wait_recv()
        acc = land[0]
        for s in range(1, 8):
            acc = acc + land[s]
        o_ref[...] = acc

    return pl.pallas_call(
        body,
        name="all_reduce_small",
        in_specs=[pl.BlockSpec(memory_space=pltpu.VMEM), pl.BlockSpec(memory_space=pl.ANY)],
        out_specs=pl.BlockSpec(memory_space=pltpu.VMEM),
        out_shape=_sds((R, D), F32),
        scratch_shapes=[pltpu.VMEM((8, R, D), F32), pltpu.SemaphoreType.DMA((7,)), pltpu.SemaphoreType.DMA((7,))],
    )(v, dep)


def kernel(x, attn_w_in, attn_w_out, hgrn_w_in, hgrn_w_out, hgrn_norm_g, lb_logits, ln_mix_g, ln_mix_b, ln_ffn_g, ln_ffn_b, ffn_w_up, ffn_w_down, loss_target, m_attn_w_in, m_attn_w_out, m_hgrn_w_in, m_hgrn_w_out, m_hgrn_norm_g, m_lb_logits, m_ln_mix_g, m_ln_mix_b, m_ln_ffn_g, m_ln_ffn_b, m_ffn_w_up, m_ffn_w_down, v_attn_w_in, v_attn_w_out, v_hgrn_w_in, v_hgrn_w_out, v_hgrn_norm_g, v_lb_logits, v_ln_mix_g, v_ln_mix_b, v_ln_ffn_g, v_ln_ffn_b, v_ffn_w_up, v_ffn_w_down):
    xs = x[0]
    tgt = loss_target[0]
    S, D = xs.shape
    F = ffn_w_up.shape[2] * 4
    T1 = _pick(S, (1024, 512, 256))
    T2 = _pick(S, (2048, 1024, 512))
    TH = _pick(S, (512, 256))
    TB = _pick(S, (512, 256))
    TN = _pick(D, (512, 256, 128))
    TF = _pick(F, (1024, 512))
    TG = _pick(3 * D, (1536, 1024, 768))
    TW = _pick(F, (2048, 1024))

    cast = lambda w: w.astype(MXU_DTYPE)
    st_a, tok = _gather_start("gather_a", [cast(attn_w_in[0])], [1], jnp.zeros((8, LANES), F32))
    tok, (xs_late, w_aout, w_fup, w_fdown, w_hin, w_hout) = lax.optimization_barrier(
        (tok, (xs, attn_w_out, ffn_w_up, ffn_w_down, hgrn_w_in, hgrn_w_out)))
    st_b, tok = _gather_start("gather_b", [cast(w_aout[0]), cast(w_fup[0]), cast(w_fdown[0])], [0, 1, 0], tok)
    st_c, tok = _gather_start("gather_c", [cast(w_hin[0]), cast(w_hout[0]), hgrn_norm_g, cast(w_fup[1]), cast(w_fdown[1])],
                              [1, 0, 1, 1, 0], tok)

    cos3, sin3 = _rope_tables(S)
    sel = _head_sel(D)
    sel_t = sel.T

    xc3 = _stack_classes("x_classes", xs_late, MXU_DTYPE)
    (wa_in,) = _gather_wait("gather_a_wait", st_a, tok, xc3, cos3, sin3)
    P3 = _attn_proj(xc3, wa_in, cos3, sin3, T2, TN)
    o3, lse3 = _attn_fwd(P3, D)
    o_att, L_att = _attn_mix(o3, lse3, sel)
    wa_out, w_up0, w_down0 = _gather_wait("gather_b_wait", st_b, L_att)
    x1, xh1, r1 = _mm_res_ln("attn_out_ln", o_att, wa_out, xs, ln_mix_g[0:1], ln_mix_b[0:1], TH, D)
    a0 = _mlp_up("mlp0_up", x1, w_up0, T1, TF, D)
    x2, xh2, r2 = _mm_res_ln("mlp0_down_ln", a0, w_down0, x1, ln_ffn_g[0:1], ln_ffn_b[0:1], TH, F)

    wh_in, wh_out, norm_g, w_up1, w_down1 = _gather_wait("gather_c_wait", st_c, r2)
    P1 = _plain_mm("hgrn_proj", x2, wh_in, "nn", F32, T1, _pick(3 * D, (1024, 768, 512)), D)
    o_h, n_h, states = _hgrn_fwd(P1, lb_logits, norm_g, TB)
    x3, xh3, r3 = _mm_res_ln("hgrn_out_ln", n_h, wh_out, x2, ln_mix_g[1:2], ln_mix_b[1:2], TH, D)
    a1 = _mlp_up("mlp1_up", x3, w_up1, T1, TF, D)
    x4, xh4, r4 = _mm_res_ln("mlp1_down_ln", a1, w_down1, x3, ln_ffn_g[1:2], ln_ffn_b[1:2], TH, F)

    wgrad = lambda name, a, dy, tm, tn: _plain_mm(name, a, dy, "tn", MXU_DTYPE, tm, tn, T1)
    sq, du4, dg_ffn1, db_ffn1 = _loss_ln_bwd(x4, tgt, xh4, r4, ln_ffn_g[1:2], TH)
    dh1 = _mlp_down_bwd("mlp1_down_bwd", du4, w_down1, a1, T1, TF, D)
    g_down1 = wgrad("g_down1", a1, du4, TW, D)
    g_up1 = wgrad("g_up1", x3, dh1, D, TW)
    sc_1, tok = _scatter_start("scatter_1", [g_down1, g_up1], [0, 1])
    du3, dg_mix1, db_mix1 = _mm_nt_res_ln_bwd("mlp1_up_bwd", dh1, w_up1, du4, xh3, r3, ln_mix_g[1:2], TH, F, tok)
    dn = _plain_mm("hgrn_out_bwd", du3, wh_out, "nt", F32, T1, D, D)
    g_hout = wgrad("g_hgrn_out", n_h, du3, D, D)
    dP1, dg_norm, dlb = _hgrn_bwd(P1, o_h, states, dn, lb_logits, norm_g, TB)
    dP1 = dP1.reshape(3 * S, D)
    g_hin = _matmul("g_hgrn_in", x2, dP1, "tn", D, D, T1, [(_sds((D, 3 * D), MXU_DTYPE), _ij_spec(D, D))], _store_epilogue,
                    b_map=lambda i, j, k: (k + j * (S // T1), 0), mnk=(D, 3 * D, S))[0]
    d_lb_logits = _lb_logits_grad(dlb, lb_logits)
    sc_2, tok = _scatter_start("scatter_2", [g_hout, g_hin], [0, 1])

    du2, dg_ffn0, db_ffn0 = _mm_nt_res_ln_bwd("hgrn_in_bwd", dP1, wh_in, du3, xh2, r2, ln_ffn_g[0:1], T1, D, tok,
                                              a_map=lambda i, j, k: (i + k * (S // T1), 0), mk=3 * D)
    dh0 = _mlp_down_bwd("mlp0_down_bwd", du2, w_down0, a0, T1, TF, D)
    g_down0 = wgrad("g_down0", a0, du2, TW, D)
    g_up0 = wgrad("g_up0", x1, dh0, D, TW)
    sc_3, tok = _scatter_start("scatter_3", [g_down0, g_up0], [0, 1])
    du1, dg_mix0, db_mix0 = _mm_nt_res_ln_bwd("mlp0_up_bwd", dh0, w_up0, du2, xh1, r1, ln_mix_g[0:1], TH, F, tok)
    do, delta = _attn_out_bwd(du1, wa_out, o_att, sel_t, TH, D)
    g_aout = wgrad("g_attn_out", o_att, du1, D, D)
    sc_5, tok = _scatter_start("scatter_5", [g_aout], [0])
    dP3 = _attn_bwd(P3, _stack_classes("do_classes", do, MXU_DTYPE), _stack_classes("lse_classes", L_att, F32),
                    _stack_classes("delta_classes", delta, F32), cos3, sin3, D)
    small = jnp.concatenate([d_lb_logits, dg_mix0, dg_mix1, db_mix0, db_mix1, dg_ffn0, dg_ffn1, db_ffn0, db_ffn1,
                             dg_norm, sq, jnp.zeros((4, D), F32)], axis=0)
    small = _all_reduce_small(small, dP3)
    loss = 0.5 * jnp.sum(small[11]) / D
    grp = lambda j: j // (3 * D // TG)
    g_ain = _matmul("g_attn_in", xc3, dP3, "tn", D, TG, T1, [(_sds((D, 9 * D), MXU_DTYPE), _ij_spec(D, TG))], _store_epilogue,
                    a_map=lambda i, j, k: (k + grp(j) * (S // T1), i),
                    b_map=lambda i, j, k: (k + grp(j) * (S // T1), j % (3 * D // TG)), mnk=(D, 9 * D, S), dep=small)[0]
    sc_4, tok = _scatter_start("scatter_4", [g_ain], [1])
    dxc3 = _matmul("attn_in_bwd", dP3, wa_in, "nt", TH, D, 3 * D, [(_sds((3 * S, D), F32), _ij_spec(TH, D))], _store_epilogue,
                   b_map=lambda i, j, k: (j, k + i // (S // TH)), mnk=(3 * S, D, 3 * D), dep=tok)[0]
    grad_x = _input_grad(du1, dxc3)

    def reduced(name, state, *after):
        gs, lands = _scatter_wait(name + "_wait", state, *after)
        return [_reduce_join(f"{name}_reduce_{i}", l, _own_piece(g, ax)) for i, (l, g, ax) in enumerate(zip(lands, gs, state[4]))]

    r_down1, r_up1 = reduced("scatter_1", sc_1, grad_x)
    r_hout, r_hin = reduced("scatter_2", sc_2, r_up1)
    r_down0, r_up0 = reduced("scatter_3", sc_3, r_hin)
    (r_aout,) = reduced("scatter_5", sc_5, r_up0)

    my_chip = 2 * lax.axis_index("x") + lax.axis_index("y")
    nsh = hgrn_norm_g.shape[1]
    g_norm = lax.dynamic_slice(small[10:11], (0, my_chip * nsh), (1, nsh))

    grads, upd = {}, {}

    def update(nm, w, gs, m, v):
        upd[nm] = _adamw("adamw_" + nm, w, gs, m, v)
        grads[nm] = upd[nm][3]

    update("hgrn_w_in", hgrn_w_in, [r_hin], m_hgrn_w_in, v_hgrn_w_in)
    update("hgrn_w_out", hgrn_w_out, [r_hout], m_hgrn_w_out, v_hgrn_w_out)
    update("ffn_w_up", ffn_w_up, [r_up0, r_up1], m_ffn_w_up, v_ffn_w_up)
    update("ffn_w_down", ffn_w_down, [r_down0, r_down1], m_ffn_w_down, v_ffn_w_down)
    update("attn_w_out", attn_w_out, [r_aout], m_attn_w_out, v_attn_w_out)
    update("hgrn_norm_g", hgrn_norm_g, [g_norm], m_hgrn_norm_g, v_hgrn_norm_g)
    cat = lambda ts: jnp.concatenate(ts, axis=0)
    small_w = cat([lb_logits, ln_mix_g, ln_mix_b, ln_ffn_g, ln_ffn_b])
    small_m = cat([m_lb_logits, m_ln_mix_g, m_ln_mix_b, m_ln_ffn_g, m_ln_ffn_b])
    small_v = cat([v_lb_logits, v_ln_mix_g, v_ln_mix_b, v_ln_ffn_g, v_ln_ffn_b])
    small_upd = _adamw("adamw_small", small_w, [small[0:10]], small_m, small_v)
    for i, nm in enumerate(["lb_logits", "ln_mix_g", "ln_mix_b", "ln_ffn_g", "ln_ffn_b"]):
        grads[nm] = small[2 * i:2 * i + 2]
        upd[nm] = tuple(t[2 * i:2 * i + 2] for t in small_upd)
    done = [upd[k][2] for k in ("hgrn_w_in", "hgrn_w_out", "ffn_w_up", "ffn_w_down", "attn_w_out", "hgrn_norm_g")]
    (r_ain,) = reduced("scatter_4", sc_4, small_upd[2], *done)
    update("attn_w_in", attn_w_in, [r_ain], m_attn_w_in, v_attn_w_in)

    order = ["attn_w_in", "attn_w_out", "hgrn_w_in", "hgrn_w_out", "hgrn_norm_g", "lb_logits", "ln_mix_g", "ln_mix_b",
             "ln_ffn_g", "ln_ffn_b", "ffn_w_up", "ffn_w_down"]
    return (loss, grad_x[None], *[grads[k] for k in order], *[upd[k][0] for k in order],
            *[upd[k][1] for k in order], *[upd[k][2] for k in order])
```

```python
import math

import jax
import jax.numpy as jnp
from jax import lax
from jax.experimental import pallas as pl
from jax.experimental.pallas import tpu as pltpu

F32 = jnp.float32
BF16 = jnp.bfloat16
MXU_DTYPE = BF16

HEAD_DIM = 64
ATTN_BLK = 128
DILATIONS = (1, 4, 16)
ROPE_THETA = 10000.0
HGRN_DK = 128
HGRN_CHUNK = 64
DEPTH = 2
LN_EPS = 1e-5
RMS_EPS = 1e-6
ALPHA = (2 * DEPTH) ** 0.25
ADAM_LR, ADAM_B1, ADAM_B2, ADAM_EPS, ADAM_WD, ADAM_STEP = 0.001, 0.9, 0.999, 1e-08, 0.01, 10

LANES = 128
VMEM_LIMIT = 56 * 1024 * 1024
NEG = -1e30
MESH = pl.DeviceIdType.MESH


def _cparams(sem=None):
    return pltpu.CompilerParams(dimension_semantics=sem, vmem_limit_bytes=VMEM_LIMIT)


def _sds(shape, dtype):
    return jax.ShapeDtypeStruct(tuple(shape), dtype)


def _dg(a, b, ca, cb):
    return lax.dot_general(a, b, (((ca,), (cb,)), ((), ())), preferred_element_type=F32)


def _nn(a, b):
    return _dg(a, b, 1, 0)


def _nt(a, b):
    return _dg(a, b, 1, 1)


def _tn(a, b):
    return _dg(a, b, 0, 0)


def _split3(a):
    hi = a.astype(BF16)
    r = a - hi.astype(F32)
    mid = r.astype(BF16)
    lo = (r - mid.astype(F32)).astype(BF16)
    return hi, mid, lo


def _exact_nn(a, sel):
    hi, mid, lo = _split3(a)
    return _nn(hi, sel) + _nn(mid, sel) + _nn(lo, sel)


def _pick(n, prefs):
    for p in prefs:
        if n % p == 0:
            return p
    return n


def _matmul(name, a, b, form, tm, tn, tk, outs, epilogue, extras=(), a_map=None, b_map=None, mnk=None, dep=None,
            sem=("parallel", "parallel", "arbitrary")):
    if form == "nn":
        (M, K), N = a.shape, b.shape[1]
        a_spec = pl.BlockSpec((tm, tk), a_map or (lambda i, j, k: (i, k)))
        b_spec = pl.BlockSpec((tk, tn), b_map or (lambda i, j, k: (k, j)))
        ca, cb = 1, 0
    elif form == "nt":
        (M, K), N = a.shape, b.shape[0]
        a_spec = pl.BlockSpec((tm, tk), a_map or (lambda i, j, k: (i, k)))
        b_spec = pl.BlockSpec((tn, tk), b_map or (lambda i, j, k: (j, k)))
        ca, cb = 1, 1
    else:
        (K, M), N = a.shape, b.shape[1]
        a_spec = pl.BlockSpec((tk, tm), a_map or (lambda i, j, k: (k, i)))
        b_spec = pl.BlockSpec((tk, tn), b_map or (lambda i, j, k: (k, j)))
        ca, cb = 0, 0
    if mnk is not None:
        M, N, K = mnk
    assert M % tm == 0 and N % tn == 0 and K % tk == 0, (name, M, N, K, tm, tn, tk)
    nk = K // tk
    ne, no = len(extras), len(outs)
    deps = [] if dep is None else [dep]
    nd = len(deps)

    def body(a_ref, b_ref, *rest):
        extra_refs, out_refs = rest[:ne], rest[ne + nd:ne + nd + no]
        j = pl.program_id(1)
        part = _dg(a_ref[...].astype(MXU_DTYPE), b_ref[...].astype(MXU_DTYPE), ca, cb)
        if nk == 1:
            epilogue(part, extra_refs, out_refs, j)
            return
        acc_ref = rest[-1]
        k = pl.program_id(2)

        @pl.when(k == 0)
        def _():
            acc_ref[...] = part

        @pl.when(k > 0)
        def _():
            acc_ref[...] += part

        @pl.when(k == nk - 1)
        def _():
            epilogue(acc_ref[...], extra_refs, out_refs, j)

    res = pl.pallas_call(
        body,
        name=name,
        grid=(M // tm, N // tn, nk),
        in_specs=[a_spec, b_spec] + [s for _, s in extras] + [pl.BlockSpec(memory_space=pl.ANY)] * nd,
        out_specs=[s for _, s in outs],
        out_shape=[o for o, _ in outs],
        scratch_shapes=[pltpu.VMEM((tm, tn), F32)] if nk > 1 else [],
        compiler_params=_cparams(sem),
    )(a, b, *[e for e, _ in extras], *deps)
    return res


def _ij_spec(tm, tn):
    return pl.BlockSpec((tm, tn), lambda i, j, k: (i, j))


def _store_epilogue(acc, extra_refs, out_refs, j):
    out_refs[0][...] = acc.astype(out_refs[0].dtype)


def _plain_mm(name, a, b, form, out_dtype, tm, tn, tk):
    M = a.shape[1] if form == "tn" else a.shape[0]
    N = b.shape[0] if form == "nt" else b.shape[1]
    return _matmul(name, a, b, form, tm, tn, tk, [(_sds((M, N), out_dtype), _ij_spec(tm, tn))], _store_epilogue)[0]


def _class_slabs(S):
    assert DILATIONS[0] == 1
    return [(g, d, r, S // d) for g, d in enumerate(DILATIONS) if d > 1 for r in range(d)]


def _stack_classes(name, t, out_dtype):
    S, W = t.shape

    def body(x_ref, o_ref):
        o_ref[0:S, :] = x_ref[...].astype(out_dtype)
        for g, d, r, n in _class_slabs(S):
            o_ref[g * S + r * n:g * S + (r + 1) * n, :] = x_ref[pl.ds(r, n, stride=d), :].astype(out_dtype)

    return pl.pallas_call(
        body,
        name=name,
        grid=(W // LANES,),
        in_specs=[pl.BlockSpec((S, LANES), lambda j: (0, j))],
        out_specs=pl.BlockSpec((3 * S, LANES), lambda j: (0, j)),
        out_shape=_sds((3 * S, W), out_dtype),
        compiler_params=_cparams(("parallel",)),
    )(t)


def _rope_tables(seq):
    half = HEAD_DIM // 2
    inv = ROPE_THETA ** (-jnp.arange(half, dtype=F32) * (2.0 / HEAD_DIM))
    inv = jnp.tile(inv, LANES // half)
    pos = []
    for d in DILATIONS:
        row = jnp.arange(seq)
        pos.append((row % (seq // d)) * d + row // (seq // d))
    ang = jnp.concatenate(pos).astype(F32)[:, None] * inv[None, :]
    first = (jnp.arange(LANES) % HEAD_DIM) < half
    sin = jnp.sin(ang)
    return jnp.cos(ang), jnp.where(first[None, :], -sin, sin)


def _partner(x):
    half = HEAD_DIM // 2
    lane = lax.broadcasted_iota(jnp.int32, x.shape, 1)
    first = (lane % HEAD_DIM) < half
    return jnp.where(first, pltpu.roll(x, LANES - half, 1), pltpu.roll(x, half, 1))


def _attn_proj(x3, w_full, cos3, sin3, tm, tn):
    S3, D = x3.shape
    S = S3 // 3
    per_part = D // tn
    per_group = 3 * per_part

    def epilogue(acc, extra_refs, out_refs, j):
        cos_ref, sin_ref = extra_refs
        o_ref = out_refs[0]
        is_rot = j // per_part < 2

        @pl.when(is_rot)
        def _():
            c, s = cos_ref[...], sin_ref[...]
            for t in range(tn // LANES):
                xs = acc[:, t * LANES:(t + 1) * LANES]
                o_ref[:, t * LANES:(t + 1) * LANES] = (xs * c + _partner(xs) * s).astype(o_ref.dtype)

        @pl.when(jnp.logical_not(is_rot))
        def _():
            o_ref[...] = acc.astype(o_ref.dtype)

    tab = pl.BlockSpec((tm, LANES), lambda i, j, k: (i, 0))
    return _matmul("attn_proj", x3, w_full, "nn", tm, tn, D, [(_sds((S3, 3 * D), MXU_DTYPE), _ij_spec(tm, tn))],
                   epilogue, extras=[(cos3, tab), (sin3, tab)],
                   b_map=lambda i, j, k: (k, j + (i // (S // tm)) * per_group), mnk=(S3, 3 * D, D))[0]


def _head_sel(d_model):
    h = jnp.arange(LANES)[:, None]
    l = jnp.arange(d_model)[None, :]
    return (l // HEAD_DIM == h).astype(BF16)


def _class_edges(b, nblk):
    g = b // nblk
    per_class = jnp.where(g == 0, nblk // DILATIONS[0], jnp.where(g == 1, nblk // DILATIONS[1], nblk // DILATIONS[2]))
    pos = (b % nblk) % per_class
    return pos != 0, pos != per_class - 1


def _two_heads(t, top):
    zero = jnp.zeros_like(t)
    return jnp.concatenate([jnp.where(top, t, zero), jnp.where(top, zero, t)], axis=0)


def _band_mask(has_prev):
    B = ATTN_BLK
    row = lax.broadcasted_iota(jnp.int32, (2 * B, 2 * B), 0) % B
    col = lax.broadcasted_iota(jnp.int32, (2 * B, 2 * B), 1)
    in_prev = jnp.logical_and(jnp.logical_and(col < B, col >= row), has_prev)
    in_own = jnp.logical_and(col >= B, col - B <= row)
    return jnp.logical_or(in_prev, in_own)


def _attn_fwd(P3, D):
    S3 = P3.shape[0]
    B = ATTN_BLK
    nblk = S3 // 3 // B
    npairs = D // LANES
    scale = HEAD_DIM ** -0.5

    def body(q_ref, kc_ref, vc_ref, kp_ref, vp_ref, o_ref, lse_ref):
        has_prev, _ = _class_edges(pl.program_id(0), nblk)
        ok = _band_mask(has_prev)
        lane = lax.broadcasted_iota(jnp.int32, (B, LANES), 1)
        top = lane < HEAD_DIM
        lse_acc = jnp.zeros((B, LANES), F32)
        for j in range(npairs):
            sl = slice(j * LANES, (j + 1) * LANES)
            Q = _two_heads(q_ref[:, sl] * scale, top)
            K2 = jnp.concatenate([kp_ref[:, sl], kc_ref[:, sl]], axis=0)
            V2 = jnp.concatenate([vp_ref[:, sl], vc_ref[:, sl]], axis=0)
            s = jnp.where(ok, _nt(Q, K2), NEG)
            m = jnp.max(s, axis=1, keepdims=True)
            p = jnp.exp(s - m)
            l = jnp.sum(p, axis=1, keepdims=True)
            o = _nn((p * (1.0 / l)).astype(MXU_DTYPE), V2)
            o_ref[:, sl] = jnp.where(top, o[:B], o[B:])
            lse = m + jnp.log(l)
            lse_acc = jnp.where(lane == 2 * j, lse[:B], jnp.where(lane == 2 * j + 1, lse[B:], lse_acc))
        lse_ref[...] = lse_acc

    blk = lambda part, prev: pl.BlockSpec(
        (B, D), (lambda b: (jnp.maximum(b - 1, 0), part)) if prev else (lambda b: (b, part)))
    return pl.pallas_call(
        body,
        name="attn_fwd",
        grid=(3 * nblk,),
        in_specs=[blk(0, False), blk(1, False), blk(2, False), blk(1, True), blk(2, True)],
        out_specs=[pl.BlockSpec((B, D), lambda b: (b, 0)), pl.BlockSpec((B, LANES), lambda b: (b, 0))],
        out_shape=[_sds((S3, D), F32), _sds((S3, LANES), F32)],
        compiler_params=_cparams(("parallel",)),
    )(P3, P3, P3, P3, P3)


def _attn_mix(o3, lse3, sel):
    S3, D = o3.shape
    S = S3 // 3

    def body(o3_ref, lse_ref, sel_ref, o_ref, L_ref, w_ref):
        @pl.when(pl.program_id(0) == 0)
        def _():
            w_ref[0] = lse_ref[0:S, :]
            for g, d, r, n in _class_slabs(S):
                w_ref[g, pl.ds(r, n, stride=d), :] = lse_ref[g * S + r * n:g * S + (r + 1) * n, :]
            a, b, c = w_ref[0], w_ref[1], w_ref[2]
            m = jnp.maximum(jnp.maximum(a, b), c)
            L = m + jnp.log(jnp.exp(a - m) + jnp.exp(b - m) + jnp.exp(c - m))
            L_ref[...] = L
            w_ref[0] = jnp.exp(a - L)
            w_ref[1] = jnp.exp(b - L)
            w_ref[2] = jnp.exp(c - L)

        s = sel_ref[...]
        o_ref[...] = _exact_nn(w_ref[0], s) * o3_ref[0:S, :]
        for g, d, r, n in _class_slabs(S):
            rows = pl.ds(r, n, stride=d)
            o_ref[rows, :] += _exact_nn(w_ref[g, rows, :], s) * o3_ref[g * S + r * n:g * S + (r + 1) * n, :]

    return pl.pallas_call(
        body,
        name="attn_mix",
        grid=(D // LANES,),
        in_specs=[pl.BlockSpec((S3, LANES), lambda j: (0, j)), pl.BlockSpec((S3, LANES), lambda j: (0, 0)),
                  pl.BlockSpec((LANES, LANES), lambda j: (0, j))],
        out_specs=[pl.BlockSpec((S, LANES), lambda j: (0, j)), pl.BlockSpec((S, LANES), lambda j: (0, 0))],
        out_shape=[_sds((S, D), F32), _sds((S, LANES), F32)],
        scratch_shapes=[pltpu.VMEM((3, S, LANES), F32)],
        compiler_params=_cparams(("arbitrary",)),
    )(o3, lse3, sel)


def _attn_bwd(P3, do3, L3, delta3, cos3, sin3, D, dep):
    S3 = P3.shape[0]
    B = ATTN_BLK
    nblk = S3 // 3 // B
    npairs = D // LANES
    scale = HEAD_DIM ** -0.5

    def body(c_ref, p_ref, n_ref, doc_ref, don_ref, Lc_ref, Ln_ref, dc_ref, dn_ref, cos_ref, sin_ref, dep_ref, out_ref):
        has_prev, has_next = _class_edges(pl.program_id(0), nblk)
        ok = _band_mask(has_prev)
        row = lax.broadcasted_iota(jnp.int32, (2 * B, B), 0) % B
        col = lax.broadcasted_iota(jnp.int32, (2 * B, B), 1)
        ok_n = jnp.logical_and(col >= row, has_next)
        lane = lax.broadcasted_iota(jnp.int32, (B, LANES), 1)
        top = lane < HEAD_DIM
        cos_t = cos_ref[...]
        sin_inv = -sin_ref[...]
        Lc_all, Ln_all, dc_all, dn_all = Lc_ref[...], Ln_ref[...], dc_ref[...], dn_ref[...]
        pair_col = lambda t, j: jnp.concatenate([t[:, 2 * j:2 * j + 1], t[:, 2 * j + 1:2 * j + 2]], axis=0)
        for j in range(npairs):
            sl = lambda part: slice(part * D + j * LANES, part * D + (j + 1) * LANES)
            kc2, vc2 = c_ref[:, sl(1)], c_ref[:, sl(2)]
            K2 = jnp.concatenate([p_ref[:, sl(1)], kc2], axis=0)
            V2 = jnp.concatenate([p_ref[:, sl(2)], vc2], axis=0)
            Qc = _two_heads(c_ref[:, sl(0)] * scale, top)
            Qn = _two_heads(n_ref[:, sl(0)] * scale, top)
            DOc = _two_heads(doc_ref[:, j * LANES:(j + 1) * LANES].astype(MXU_DTYPE), top)
            DOn = _two_heads(don_ref[:, j * LANES:(j + 1) * LANES].astype(MXU_DTYPE), top)
            P_c = jnp.where(ok, jnp.exp(_nt(Qc, K2) - pair_col(Lc_all, j)), 0.0)
            dS_c = P_c * (_nt(DOc, V2) - pair_col(dc_all, j))
            P_n = jnp.where(ok_n, jnp.exp(_nt(Qn, kc2) - pair_col(Ln_all, j)), 0.0)
            dS_n = P_n * (_nt(DOn, vc2) - pair_col(dn_all, j))
            dq = _nn(dS_c.astype(MXU_DTYPE), K2)
            dq2 = jnp.where(top, dq[:B], dq[B:]) * scale
            Qk = jnp.concatenate([Qc, Qn], axis=0)
            DOk = jnp.concatenate([DOc, DOn], axis=0)
            dk2 = _tn(jnp.concatenate([dS_c[:, B:], dS_n], axis=0).astype(MXU_DTYPE), Qk)
            dv2 = _tn(jnp.concatenate([P_c[:, B:], P_n], axis=0).astype(MXU_DTYPE), DOk)
            out_ref[:, sl(0)] = (dq2 * cos_t + _partner(dq2) * sin_inv).astype(out_ref.dtype)
            out_ref[:, sl(1)] = (dk2 * cos_t + _partner(dk2) * sin_inv).astype(out_ref.dtype)
            out_ref[:, sl(2)] = dv2.astype(out_ref.dtype)

    cur = lambda b: b
    prv = lambda b: jnp.maximum(b - 1, 0)
    nxt = lambda b: jnp.minimum(b + 1, 3 * nblk - 1)
    spec = lambda w, f: pl.BlockSpec((B, w), lambda b: (f(b), 0))
    return pl.pallas_call(
        body,
        name="attn_bwd",
        grid=(3 * nblk,),
        in_specs=[spec(3 * D, cur), spec(3 * D, prv), spec(3 * D, nxt), spec(D, cur), spec(D, nxt),
                  spec(LANES, cur), spec(LANES, nxt), spec(LANES, cur), spec(LANES, nxt), spec(LANES, cur), spec(LANES, cur),
                  pl.BlockSpec(memory_space=pl.ANY)],
        out_specs=spec(3 * D, cur),
        out_shape=_sds((S3, 3 * D), MXU_DTYPE),
        compiler_params=_cparams(("parallel",)),
    )(P3, P3, P3, do3, do3, L3, L3, delta3, delta3, cos3, sin3, dep)


def _input_grad(du, dx3):
    S, D = du.shape

    def body(du_ref, dx_ref, o_ref):
        o_ref[...] = ALPHA * du_ref[...] + dx_ref[0:S, :]
        for g, d, r, n in _class_slabs(S):
            o_ref[pl.ds(r, n, stride=d), :] += dx_ref[g * S + r * n:g * S + (r + 1) * n, :]

    return pl.pallas_call(
        body,
        name="input_grad",
        grid=(D // LANES,),
        in_specs=[pl.BlockSpec((S, LANES), lambda j: (0, j)), pl.BlockSpec((3 * S, LANES), lambda j: (0, j))],
        out_specs=pl.BlockSpec((S, LANES), lambda j: (0, j)),
        out_shape=_sds((S, D), F32),
        compiler_params=_cparams(("parallel",)),
    )(du, dx3)


def _chunk_causal(tb):
    r = lax.broadcasted_iota(jnp.int32, (tb, tb), 0)
    c = lax.broadcasted_iota(jnp.int32, (tb, tb), 1)
    return jnp.logical_and((r // HGRN_CHUNK) == (c // HGRN_CHUNK), r >= c)


def _chunk_sums(a, lower):
    C = HGRN_CHUNK
    r = lax.broadcasted_iota(jnp.int32, (C, C), 0)
    c = lax.broadcasted_iota(jnp.int32, (C, C), 1)
    tri = ((r >= c) if lower else (r <= c)).astype(BF16)
    parts = _split3(a)
    out = []
    for ci in range(a.shape[0] // C):
        rows = slice(ci * C, (ci + 1) * C)
        out.append(_nn(tri, parts[0][rows]) + _nn(tri, parts[1][rows]) + _nn(tri, parts[2][rows]))
    return jnp.concatenate(out, axis=0)


def _chunk_last(b):
    C = HGRN_CHUNK
    return jnp.concatenate([jnp.broadcast_to(b[(ci + 1) * C - 1:(ci + 1) * C, :], (C, b.shape[1]))
                            for ci in range(b.shape[0] // C)], axis=0)


def _lower_bound(lb_ref):
    l0, l1 = lb_ref[0:1, :], lb_ref[1:2, :]
    m = jnp.maximum(l0, l1)
    e0, e1 = jnp.exp(l0 - m), jnp.exp(l1 - m)
    return e1 / (e0 + e1)


def _hgrn_gates(q_raw, z, lb):
    sg = 1.0 / (1.0 + jnp.exp(-z))
    sn = 1.0 / (1.0 + jnp.exp(z))
    f = lb + (1.0 - lb) * sg
    key = (1.0 - lb) * sn
    sq = 1.0 / (1.0 + jnp.exp(-q_raw))
    return sg, sn, f, key, sq


HGRN_HEADS_PER_STEP = 2


def _hgrn_fwd(P1, lb_logits, norm_g, tb):
    S = P1.shape[0]
    D = P1.shape[1] // 3
    K = HGRN_DK
    H = D // K
    HP = HGRN_HEADS_PER_STEP
    C = HGRN_CHUNK
    cpb = tb // C
    nt = S // tb

    def body(q_ref, f_ref, i_ref, lb_ref, g_ref, o_ref, n_ref, st_ref, state):
        t = pl.program_id(1)

        @pl.when(t == 0)
        def _():
            state[...] = jnp.zeros_like(state)

        lb_all = _lower_bound(lb_ref)
        low = _chunk_causal(tb)
        for hh in range(HP):
            lanes = slice(hh * K, (hh + 1) * K)
            q_raw, z, v = q_ref[:, lanes], f_ref[:, lanes], i_ref[:, lanes]
            sg, sn, f, key, sq = _hgrn_gates(q_raw, z, lb_all[:, lanes])
            b = _chunk_sums(jnp.log(f), lower=True)
            qd = (q_raw * sq * jnp.exp(b)).astype(MXU_DTYPE)
            kd = (key * jnp.exp(-b)).astype(MXU_DTYPE)
            kb = (key * jnp.exp(_chunk_last(b) - b)).astype(MXU_DTYPE)
            vm = v.astype(MXU_DTYPE)
            a = jnp.where(low, _nt(qd, kd), 0.0).astype(MXU_DTYPE)
            o_intra = _nn(a, vm)
            st = state[hh]
            outs = []
            for ci in range(cpb):
                rows = slice(ci * C, (ci + 1) * C)
                st_ref[hh, ci] = st
                outs.append(o_intra[rows] + _nt(qd[rows], st.astype(MXU_DTYPE)))
                st = st * jnp.exp(b[(ci + 1) * C - 1:(ci + 1) * C, :]) + _tn(vm[rows], kb[rows])
            state[hh] = st
            o = jnp.concatenate(outs, axis=0)
            o_ref[:, lanes] = o
            rs = lax.rsqrt(jnp.mean(o * o, axis=1, keepdims=True) + RMS_EPS)
            n_ref[:, lanes] = o * rs * g_ref[:, lanes]

    tok = lambda part: pl.BlockSpec((tb, HP * K), lambda h, t: (t, part * (H // HP) + h))
    vec = lambda rows: pl.BlockSpec((rows, HP * K), lambda h, t: (0, h))
    return pl.pallas_call(
        body,
        name="hgrn_fwd",
        grid=(H // HP, nt),
        in_specs=[tok(0), tok(1), tok(2), vec(2), vec(1)],
        out_specs=[tok(0), tok(0), pl.BlockSpec((HP, cpb, K, K), lambda h, t: (h, t, 0, 0))],
        out_shape=[_sds((S, D), F32), _sds((S, D), F32), _sds((H, S // C, K, K), F32)],
        scratch_shapes=[pltpu.VMEM((HP, K, K), F32)],
        compiler_params=_cparams(("parallel", "arbitrary")),
    )(P1, P1, P1, lb_logits, norm_g)


def _hgrn_bwd(P1, o_pre, states, dn, lb_logits, norm_g, tb):
    S = P1.shape[0]
    D = P1.shape[1] // 3
    K = HGRN_DK
    H = D // K
    HP = HGRN_HEADS_PER_STEP
    C = HGRN_CHUNK
    cpb = tb // C
    nt = S // tb

    def body(q_ref, f_ref, i_ref, o_ref, st_ref, dn_ref, lb_ref, g_ref, d_ref, dg_ref, dlb_ref, dstate):
        t = pl.program_id(1)

        @pl.when(t == 0)
        def _():
            dstate[...] = jnp.zeros_like(dstate)
            dg_ref[...] = jnp.zeros_like(dg_ref)
            dlb_ref[...] = jnp.zeros_like(dlb_ref)

        lb_all = _lower_bound(lb_ref)
        low = _chunk_causal(tb)
        for hh in range(HP):
            lanes = slice(hh * K, (hh + 1) * K)
            lb = lb_all[:, lanes]
            gn = g_ref[:, lanes]
            q_raw, z, v = q_ref[:, lanes], f_ref[:, lanes], i_ref[:, lanes]
            sg, sn, f, key, sq = _hgrn_gates(q_raw, z, lb)
            b = _chunk_sums(jnp.log(f), lower=True)
            e_pos, e_neg, e_rel = jnp.exp(b), jnp.exp(-b), jnp.exp(_chunk_last(b) - b)
            qd_f, kd_f, kb_f = q_raw * sq * e_pos, key * e_neg, key * e_rel
            qd, kd, kb = qd_f.astype(MXU_DTYPE), kd_f.astype(MXU_DTYPE), kb_f.astype(MXU_DTYPE)
            vm = v.astype(MXU_DTYPE)
            a = jnp.where(low, _nt(qd, kd), 0.0).astype(MXU_DTYPE)
            o = o_ref[:, lanes]
            dnn = dn_ref[:, lanes]
            rs = lax.rsqrt(jnp.mean(o * o, axis=1, keepdims=True) + RMS_EPS)
            dg_ref[:, lanes] += jnp.sum(dnn * o * rs, axis=0, keepdims=True)
            tg = dnn * gn
            dom = (rs * tg - o * (rs * rs * rs) * jnp.mean(tg * o, axis=1, keepdims=True)).astype(MXU_DTYPE)
            da = jnp.where(low, _nt(dom, vm), 0.0).astype(MXU_DTYPE)
            dv = _tn(a, dom)
            dqd = _nn(da, kd)
            dkd = _tn(da, qd)
            dst = dstate[hh]
            dv_s, dqd_s, dkb_s, dbl_s = [None] * cpb, [None] * cpb, [None] * cpb, [None] * cpb
            for ci in reversed(range(cpb)):
                rows = slice(ci * C, (ci + 1) * C)
                st = st_ref[hh, ci]
                dstm = dst.astype(MXU_DTYPE)
                dec = jnp.exp(b[(ci + 1) * C - 1:(ci + 1) * C, :])
                dv_s[ci] = _nt(kb[rows], dstm)
                dkb_s[ci] = _nn(vm[rows], dstm)
                dqd_s[ci] = _nn(dom[rows], st.astype(MXU_DTYPE))
                db_last = jnp.sum(dkb_s[ci] * kb_f[rows], axis=0, keepdims=True) + jnp.sum(dst * st, axis=0, keepdims=True) * dec
                dbl_s[ci] = jnp.broadcast_to(db_last, (C, K))
                dst = dst * dec + _tn(dom[rows], qd[rows])
            dstate[hh] = dst
            dv = dv + jnp.concatenate(dv_s, axis=0)
            dqd = dqd + jnp.concatenate(dqd_s, axis=0)
            dkb = jnp.concatenate(dkb_s, axis=0)
            dkey = dkd * e_neg + dkb * e_rel
            db = dqd * qd_f - dkd * kd_f - dkb * kb_f
            dlogf = _chunk_sums(db, lower=False) + jnp.concatenate(dbl_s, axis=0)
            gz = (1.0 - lb) * sg * sn
            d_ref[0, :, lanes] = (dqd * e_pos * (sq + q_raw * sq * (1.0 - sq))).astype(d_ref.dtype)
            d_ref[1, :, lanes] = (dlogf * gz / f - dkey * gz).astype(d_ref.dtype)
            d_ref[2, :, lanes] = dv.astype(d_ref.dtype)
            dlb_ref[:, lanes] += jnp.sum(dlogf * sn / f - dkey * sn, axis=0, keepdims=True)

    rev = lambda t: nt - 1 - t
    tok = lambda part: pl.BlockSpec((tb, HP * K), lambda h, t: (rev(t), part * (H // HP) + h))
    vec = lambda rows: pl.BlockSpec((rows, HP * K), lambda h, t: (0, h))
    outs = pl.pallas_call(
        body,
        name="hgrn_bwd",
        grid=(H // HP, nt),
        in_specs=[tok(0), tok(1), tok(2), tok(0),
                  pl.BlockSpec((HP, cpb, K, K), lambda h, t: (h, rev(t), 0, 0)),
                  tok(0), vec(2), vec(1)],
        out_specs=[pl.BlockSpec((3, tb, HP * K), lambda h, t: (0, rev(t), h)), vec(1), vec(1)],
        out_shape=[_sds((3, S, D), MXU_DTYPE)] + [_sds((1, D), F32)] * 2,
        scratch_shapes=[pltpu.VMEM((HP, K, K), F32)],
        compiler_params=_cparams(("parallel", "arbitrary")),
    )(P1, P1, P1, o_pre, states, dn, lb_logits, norm_g)
    return outs


def _lb_logits_grad(dlb, lb_logits):
    def body(d_ref, l_ref, o_ref):
        s1 = _lower_bound(l_ref)
        d = d_ref[...]
        o_ref[0:1, :] = -(1.0 - s1) * s1 * d
        o_ref[1:2, :] = s1 * (1.0 - s1) * d

    return pl.pallas_call(body, name="lb_logits_grad", out_shape=_sds(lb_logits.shape, F32))(dlb, lb_logits)


def _ln_epilogue(acc, extra_refs, out_refs, j):
    res_ref, g_ref, b_ref = extra_refs
    x_ref, xhat_ref, rstd_ref = out_refs
    u = ALPHA * res_ref[...] + acc
    mu = jnp.mean(u, axis=1, keepdims=True)
    cen = u - mu
    rstd = lax.rsqrt(jnp.mean(cen * cen, axis=1, keepdims=True) + LN_EPS)
    xhat = cen * rstd
    xhat_ref[...] = xhat
    x_ref[...] = xhat * g_ref[...] + b_ref[...]
    rstd_ref[...] = rstd


def _mm_res_ln(name, a, w_full, res, g, b, tm, tk):
    S, D = res.shape
    row = pl.BlockSpec((tm, D), lambda i, j, k: (i, 0))
    vec = pl.BlockSpec((1, D), lambda i, j, k: (0, 0))
    outs = [(_sds((S, D), F32), row), (_sds((S, D), F32), row),
            (_sds((S, 1), F32), pl.BlockSpec((tm, 1), lambda i, j, k: (i, 0)))]
    return _matmul(name, a, w_full, "nn", tm, D, tk, outs, _ln_epilogue, extras=[(res, row), (g, vec), (b, vec)])


def _ln_bwd_rows(dy, xh, rstd, g, first, du_ref, dg_ref, db_ref):
    @pl.when(first)
    def _():
        dg_ref[...] = jnp.zeros_like(dg_ref)
        db_ref[...] = jnp.zeros_like(db_ref)

    dg_ref[...] += jnp.sum(dy * xh, axis=0, keepdims=True)
    db_ref[...] += jnp.sum(dy, axis=0, keepdims=True)
    dxh = dy * g
    m1 = jnp.mean(dxh, axis=1, keepdims=True)
    m2 = jnp.mean(dxh * xh, axis=1, keepdims=True)
    du_ref[...] = rstd * (dxh - m1 - xh * m2)


def _loss_ln_bwd(y, target, xhat, rstd, g, tm):
    S, D = y.shape

    def body(y_ref, t_ref, xh_ref, r_ref, g_ref, sq_ref, du_ref, dg_ref, db_ref):
        first = pl.program_id(0) == 0

        @pl.when(first)
        def _():
            sq_ref[...] = jnp.zeros_like(sq_ref)

        e = y_ref[...] - t_ref[...]
        sq_ref[...] += jnp.sum(e * e, axis=0, keepdims=True)
        _ln_bwd_rows(e / D, xh_ref[...], r_ref[...], g_ref[...], first, du_ref, dg_ref, db_ref)

    row = pl.BlockSpec((tm, D), lambda i: (i, 0))
    vec = pl.BlockSpec((1, D), lambda i: (0, 0))
    return pl.pallas_call(
        body,
        name="loss_ln_bwd",
        grid=(S // tm,),
        in_specs=[row, row, row, pl.BlockSpec((tm, 1), lambda i: (i, 0)), vec],
        out_specs=[vec, row, vec, vec],
        out_shape=[_sds((1, D), F32), _sds((S, D), F32), _sds((1, D), F32), _sds((1, D), F32)],
        compiler_params=_cparams(("arbitrary",)),
    )(y, target, xhat, rstd, g)


def _mlp_up(name, x, w_up, tm, tn, tk):
    S = x.shape[0]
    F = w_up.shape[1]

    def epilogue(acc, extra_refs, out_refs, j):
        r = jnp.maximum(acc, 0.0)
        out_refs[0][...] = (r * r).astype(out_refs[0].dtype)

    return _matmul(name, x, w_up, "nn", tm, tn, tk, [(_sds((S, F), MXU_DTYPE), _ij_spec(tm, tn))], epilogue)[0]


def _mlp_down_bwd(name, dy, w_down, a, tm, tn, tk):
    S, F = a.shape

    def epilogue(acc, extra_refs, out_refs, j):
        out_refs[0][...] = (acc * (2.0 * jnp.sqrt(extra_refs[0][...].astype(F32)))).astype(out_refs[0].dtype)

    return _matmul(name, dy, w_down, "nt", tm, tn, tk, [(_sds((S, F), MXU_DTYPE), _ij_spec(tm, tn))], epilogue,
                   extras=[(a, _ij_spec(tm, tn))])[0]


def _mm_nt_res_ln_bwd(name, dy, w, du, xhat, rstd, g, tm, tk, dep, a_map=None, mk=None):
    S, D = du.shape

    def epilogue(acc, extra_refs, out_refs, j):
        du_ref, xh_ref, r_ref, g_ref = extra_refs
        _ln_bwd_rows(ALPHA * du_ref[...] + acc, xh_ref[...], r_ref[...], g_ref[...], pl.program_id(0) == 0, *out_refs)

    row = pl.BlockSpec((tm, D), lambda i, j, k: (i, 0))
    vec = pl.BlockSpec((1, D), lambda i, j, k: (0, 0))
    return _matmul(name, dy, w, "nt", tm, D, tk, [(_sds((S, D), F32), row), (_sds((1, D), F32), vec), (_sds((1, D), F32), vec)],
                   epilogue, extras=[(du, row), (xhat, row), (rstd, pl.BlockSpec((tm, 1), lambda i, j, k: (i, 0))), (g, vec)],
                   a_map=a_map, mnk=None if mk is None else (S, D, mk), dep=dep, sem=("arbitrary", "arbitrary", "arbitrary"))


def _attn_out_bwd(du, w_out, o, sel_t, tm, tk):
    S, D = o.shape

    def epilogue(acc, extra_refs, out_refs, j):
        out_refs[0][...] = acc
        out_refs[1][...] = _exact_nn(acc * extra_refs[0][...], extra_refs[1][...])

    row = pl.BlockSpec((tm, D), lambda i, j, k: (i, 0))
    slim = pl.BlockSpec((tm, LANES), lambda i, j, k: (i, 0))
    return _matmul("attn_out_bwd", du, w_out, "nt", tm, D, tk,
                   [(_sds((S, D), F32), row), (_sds((S, LANES), F32), slim)], epilogue,
                   extras=[(o, row), (sel_t, pl.BlockSpec((D, LANES), lambda i, j, k: (0, 0)))])


def _adamw(name, w, gs, m, v):
    shape = w.shape
    cols = shape[-1]
    rows = math.prod(shape[:-1])
    w2, m2, v2 = (t.reshape(rows, cols) for t in (w, m, v))
    gs2 = [g.reshape(-1, cols) for g in gs]
    ng = len(gs2)
    tr = _pick(rows // ng, (256, 128, 64, 32, 16, 8))
    per = rows // ng // tr
    c1 = 1.0 - ADAM_B1 ** ADAM_STEP
    c2 = 1.0 - ADAM_B2 ** ADAM_STEP

    def body(w_ref, m_ref, v_ref, *rest):
        g_refs, (d_ref, nm_ref, nv_ref), g_out = rest[:ng], rest[ng:ng + 3], rest[ng + 3:]
        gg = g_refs[0][...]
        if ng == 2:
            gg = jnp.where(pl.program_id(0) < per, gg, g_refs[1][...])
            g_out[0][...] = gg
        nm = ADAM_B1 * m_ref[...] + (1.0 - ADAM_B1) * gg
        nv = ADAM_B2 * v_ref[...] + (1.0 - ADAM_B2) * (gg * gg)
        nm_ref[...] = nm
        nv_ref[...] = nv
        d_ref[...] = -ADAM_LR * ((nm / c1) / (jnp.sqrt(nv / c2) + ADAM_EPS) + ADAM_WD * w_ref[...])

    blk = pl.BlockSpec((tr, cols), lambda i: (i, 0))
    g_specs = [blk] if ng == 1 else [pl.BlockSpec((tr, cols), lambda i: (jnp.minimum(i, per - 1), 0)),
                                     pl.BlockSpec((tr, cols), lambda i: (jnp.maximum(i - per, 0), 0))]
    nout = 3 if ng == 1 else 4
    outs = pl.pallas_call(
        body,
        name=name,
        grid=(rows // tr,),
        in_specs=[blk] * 3 + g_specs,
        out_specs=[blk] * nout,
        out_shape=[_sds((rows, cols), F32)] * nout,
        compiler_params=_cparams(("parallel",)),
    )(w2, m2, v2, *gs2)
    g_full = outs[3] if ng == 2 else gs2[0]
    return tuple(o.reshape(shape) for o in (outs[0], outs[1], outs[2], g_full))


HBM = pl.BlockSpec(memory_space=pl.ANY)


def _shard_slice(ref, axis, size, index):
    idx = [slice(None)] * len(ref.shape)
    idx[axis] = pl.ds(pl.multiple_of(index * size, 8), size)
    return ref.at[tuple(idx)]


def _share_halves(name, full, tr):
    R, W4 = full.shape
    W, h = W4 // 4, R // 2
    steps = [(k, t) for k in range(3) for t in range(h // tr)]

    def body(f_in, f_ref, buf, lsem, ssem, rsem):
        x, y, c = lax.axis_index("x"), lax.axis_index("y"), lax.axis_index("c")
        sibling = (x, y, 1 - c)
        chips = [(1 - x, y), (x, 1 - y), (1 - x, 1 - y)]

        def tile(k, t):
            px, py = chips[k]
            return f_ref.at[pl.ds(pl.multiple_of(c * h + t * tr, 8), tr), pl.ds(pl.multiple_of((2 * px + py) * W, LANES), W)]

        sends = []
        for s, (k, t) in enumerate(steps):
            slot = s % 2
            if s >= 2:
                sends[s - 2].wait_send()
            lc = pltpu.make_async_copy(tile(k, t), buf.at[slot], lsem.at[slot])
            lc.start()
            lc.wait()
            rc = pltpu.make_async_remote_copy(src_ref=buf.at[slot], dst_ref=tile(k, t), send_sem=ssem.at[slot], recv_sem=rsem,
                                              device_id=sibling, device_id_type=MESH)
            rc.start()
            sends.append(rc)
        for rc in sends[-2:]:
            rc.wait_send()
        whole = f_ref.at[pl.ds(0, h), pl.ds(0, 3 * W)]
        pltpu.make_async_remote_copy(src_ref=whole, dst_ref=whole, send_sem=ssem.at[0], recv_sem=rsem,
                                     device_id=sibling, device_id_type=MESH).wait_recv()

    return pl.pallas_call(
        body,
        name=name,
        in_specs=[HBM],
        out_specs=HBM,
        out_shape=_sds(full.shape, full.dtype),
        input_output_aliases={0: 0},
        scratch_shapes=[pltpu.VMEM((2, tr, W), full.dtype), pltpu.SemaphoreType.DMA((2,)), pltpu.SemaphoreType.DMA((2,)),
                        pltpu.SemaphoreType.DMA(())],
    )(full)


IN_HBM = pl.BlockSpec(memory_space=pltpu.HBM)
IN_SEM = pl.BlockSpec(memory_space=pltpu.SEMAPHORE)
DATAFLOW = pltpu.SideEffectType.DATAFLOW_SIDE_EFFECTING


def _hbm(t):
    return pltpu.with_memory_space_constraint(t, pltpu.HBM)


def _token_spec():
    return pl.BlockSpec(memory_space=pltpu.VMEM)


def _gather_copies(s_refs, f_refs, axes, halves, send, recv, loc, arrival):
    x, y, c = lax.axis_index("x"), lax.axis_index("y"), lax.axis_index("c")
    chips = [(1 - x, y), (x, 1 - y), (1 - x, 1 - y)]
    local, remote = [], []
    for a in range(len(s_refs)):
        size = s_refs[a].shape[axes[a]]
        local.append(pltpu.make_async_copy(s_refs[a], _shard_slice(f_refs[a], axes[a], size, 2 * x + y), loc.at[a]))
        for k, (px, py) in enumerate(chips):
            block = (2 * px + py) if arrival else (2 * x + y)
            src, dst = s_refs[a], _shard_slice(f_refs[a], axes[a], size, block)
            if halves:
                assert axes[a] == 1 and len(s_refs[a].shape) == 2
                h = s_refs[a].shape[0] // 2
                rows = pl.ds(pl.multiple_of(c * h, 8), h)
                src = s_refs[a].at[rows, :]
                dst = f_refs[a].at[rows, pl.ds(pl.multiple_of(block * size, LANES), size)]
            remote.append(pltpu.make_async_remote_copy(src_ref=src, dst_ref=dst, send_sem=send.at[3 * a + k],
                                                       recv_sem=recv.at[3 * a + k], device_id=(px, py, c), device_id_type=MESH))
    return local, remote


def _gather_start(name, shards, axes, after, halves=False):
    n = len(shards)
    fulls = []
    for s, ax in zip(shards, axes):
        fs = list(s.shape)
        fs[ax] *= 4
        fulls.append(lax.empty(tuple(fs), s.dtype))

    def body(*refs):
        s_refs, f_refs = refs[:n], refs[n:2 * n]
        send, recv, loc, token = refs[2 * n + 1], refs[2 * n + 2], refs[2 * n + 3], refs[-1]
        local, remote = _gather_copies(s_refs, f_refs, axes, halves, send, recv, loc, arrival=False)
        for cp in remote + local:
            cp.start()
        token[...] = jnp.zeros_like(token)

    outs = pl.pallas_call(
        body,
        name=name,
        out_shape=(pltpu.SemaphoreType.DMA((3 * n,)), pltpu.SemaphoreType.DMA((3 * n,)), pltpu.SemaphoreType.DMA((n,)),
                   *[pltpu.HBM(t.shape, t.dtype) for t in shards + fulls], _sds((8, LANES), F32)),
        in_specs=[IN_HBM] * (2 * n) + [HBM],
        out_specs=(IN_SEM, IN_SEM, IN_SEM, *[IN_HBM] * (2 * n), _token_spec()),
        input_output_aliases={i: 3 + i for i in range(2 * n)},
        compiler_params=pltpu.CompilerParams(has_side_effects=DATAFLOW),
    )(*[_hbm(t) for t in shards + fulls], after)
    return (outs[0], outs[1], outs[2], list(outs[3:3 + n]), list(outs[3 + n:3 + 2 * n]), axes, halves), outs[-1]


def _gather_wait(name, state, *after):
    send, recv, loc, s_thru, f_thru, axes, halves = state
    n = len(s_thru)

    def body(*refs):
        s_refs, f_refs = refs[:n], refs[n:2 * n]
        local, remote = _gather_copies(s_refs, f_refs, axes, halves, refs[2 * n], refs[2 * n + 1], refs[2 * n + 2], arrival=True)
        for cp in local:
            cp.wait()
        for cp in remote:
            cp.wait_send()
            cp.wait_recv()

    outs = pl.pallas_call(
        body,
        name=name,
        out_shape=tuple(pltpu.HBM(t.shape, t.dtype) for t in s_thru + f_thru),
        in_specs=[IN_HBM] * (2 * n) + [IN_SEM, IN_SEM, IN_SEM] + [HBM] * len(after),
        out_specs=tuple([IN_HBM] * (2 * n)),
        input_output_aliases={i: i for i in range(2 * n)},
        compiler_params=pltpu.CompilerParams(has_side_effects=DATAFLOW),
    )(*s_thru, *f_thru, send, recv, loc, *after)
    return list(outs[n:2 * n])


FLIPS = [(fx, fy, fc) for fx in (0, 1) for fy in (0, 1) for fc in (0, 1)][1:]


def _piece_shape(shape, axis):
    ps = list(shape)
    if axis == 0:
        ps[0] //= 8
    else:
        ps[0] //= 2
        ps[axis] //= 4
    return tuple(ps)


def _piece(ref, axis, q, c):
    shape = ref.shape
    idx = [slice(None)] * len(shape)
    if axis == 0:
        h = shape[0] // 8
        idx[0] = pl.ds(pl.multiple_of((2 * q + c) * h, 8), h)
    else:
        h, w = shape[0] // 2, shape[axis] // 4
        idx[0] = pl.ds(c * h, h)
        idx[axis] = pl.ds(pl.multiple_of(q * w, LANES if axis == len(shape) - 1 else 8), w)
    return ref.at[tuple(idx)]


def _own_piece(g, axis):
    ps = _piece_shape(g.shape, axis)
    q, c = 2 * lax.axis_index("x") + lax.axis_index("y"), lax.axis_index("c")
    start = [0] * len(ps)
    if axis == 0:
        start[0] = (2 * q + c) * ps[0]
    else:
        start[0] = c * ps[0]
        start[axis] = q * ps[axis]
    return lax.dynamic_slice(g, start, ps)


def _scatter_copies(g_refs, l_refs, axes, send, recv):
    x, y, c = lax.axis_index("x"), lax.axis_index("y"), lax.axis_index("c")
    out = []
    for a in range(len(g_refs)):
        for k, (fx, fy, fc) in enumerate(FLIPS):
            tx, ty, tc = x ^ fx, y ^ fy, c ^ fc
            out.append(pltpu.make_async_remote_copy(
                src_ref=_piece(g_refs[a], axes[a], 2 * tx + ty, tc), dst_ref=l_refs[a].at[k],
                send_sem=send.at[7 * a + k], recv_sem=recv.at[7 * a + k], device_id=(tx, ty, tc), device_id_type=MESH))
    return out


def _scatter_start(name, grads, axes):
    n = len(grads)
    lands = [lax.empty((7,) + _piece_shape(g.shape, ax), g.dtype) for g, ax in zip(grads, axes)]

    def body(*refs):
        g_refs, l_refs = refs[:n], refs[n:2 * n]
        send, recv, token = refs[2 * n], refs[2 * n + 1], refs[-1]
        for cp in _scatter_copies(g_refs, l_refs, axes, send, recv):
            cp.start()
        token[...] = jnp.zeros_like(token)

    outs = pl.pallas_call(
        body,
        name=name,
        out_shape=(pltpu.SemaphoreType.DMA((7 * n,)), pltpu.SemaphoreType.DMA((7 * n,)),
                   *[pltpu.HBM(t.shape, t.dtype) for t in grads + lands], _sds((8, LANES), F32)),
        in_specs=[IN_HBM] * (2 * n),
        out_specs=(IN_SEM, IN_SEM, *[IN_HBM] * (2 * n), _token_spec()),
        input_output_aliases={i: 2 + i for i in range(2 * n)},
        compiler_params=pltpu.CompilerParams(has_side_effects=DATAFLOW),
    )(*[_hbm(t) for t in grads + lands])
    return (outs[0], outs[1], list(outs[2:2 + n]), list(outs[2 + n:2 + 2 * n]), axes), outs[-1]


def _scatter_wait(name, state, *after):
    send, recv, g_thru, l_thru, axes = state
    n = len(g_thru)

    def body(*refs):
        g_refs, l_refs = refs[:n], refs[n:2 * n]
        for cp in _scatter_copies(g_refs, l_refs, axes, refs[2 * n], refs[2 * n + 1]):
            cp.wait_send()
            cp.wait_recv()

    outs = pl.pallas_call(
        body,
        name=name,
        out_shape=tuple(pltpu.HBM(t.shape, t.dtype) for t in g_thru + l_thru),
        in_specs=[IN_HBM] * (2 * n) + [IN_SEM, IN_SEM] + [HBM] * len(after),
        out_specs=tuple([IN_HBM] * (2 * n)),
        input_output_aliases={i: i for i in range(2 * n)},
        compiler_params=pltpu.CompilerParams(has_side_effects=DATAFLOW),
    )(*g_thru, *l_thru, send, recv, *after)
    return list(outs[:n]), list(outs[n:2 * n])


def _reduce_join(name, landing, own):
    piece = own.shape
    C = piece[-1]
    R = math.prod(piece[:-1])
    l3 = landing.reshape(7, R, C)
    own2 = own.reshape(R, C)
    tr = _pick(R, [t for t in (512, 256, 128, 64, 32, 16, 8) if t * C <= 256 * 1024])
    nsteps = R // tr

    def body(own_ref, l_ref, o_ref, buf, send, loc, recv):
        i = pl.program_id(0)
        x, y, c = lax.axis_index("x"), lax.axis_index("y"), lax.axis_index("c")
        sibling = (x, y, 1 - c)

        def copies(slot, step):
            dst = o_ref.at[pl.ds(pl.multiple_of(c * R + step * tr, 8), tr), :]
            return (pltpu.make_async_copy(buf.at[slot], dst, loc.at[slot]),
                    pltpu.make_async_remote_copy(src_ref=buf.at[slot], dst_ref=dst, send_sem=send.at[slot], recv_sem=recv,
                                                 device_id=sibling, device_id_type=MESH))

        @pl.when(i >= 2)
        def _():
            lc, rc = copies(i % 2, i - 2)
            lc.wait()
            rc.wait_send()

        acc = own_ref[...].astype(F32)
        for s in range(7):
            acc = acc + l_ref[s].astype(F32)
        buf[i % 2] = acc
        lc, rc = copies(i % 2, i)
        lc.start()
        rc.start()

        @pl.when(i == nsteps - 1)
        def _():
            for st in range(max(nsteps - 2, 0), nsteps):
                lc, rc = copies(st % 2, st)
                lc.wait()
                rc.wait_send()
            theirs = o_ref.at[pl.ds(pl.multiple_of((1 - c) * R, 8), R), :]
            pltpu.make_async_remote_copy(src_ref=theirs, dst_ref=theirs, send_sem=send.at[0], recv_sem=recv,
                                         device_id=sibling, device_id_type=MESH).wait_recv()

    return pl.pallas_call(
        body,
        name=name,
        grid=(nsteps,),
        in_specs=[pl.BlockSpec((tr, C), lambda i: (i, 0)), pl.BlockSpec((7, tr, C), lambda i: (0, i, 0))],
        out_specs=HBM,
        out_shape=_sds((2 * R, C), F32),
        scratch_shapes=[pltpu.VMEM((2, tr, C), F32), pltpu.SemaphoreType.DMA((2,)), pltpu.SemaphoreType.DMA((2,)),
                        pltpu.SemaphoreType.DMA(())],
        compiler_params=_cparams(("arbitrary",)),
    )(own2, l3)


def _all_reduce_small(v, dep):
    R, D = v.shape

    def body(v_ref, dep_ref, o_ref, land, send, recv):
        x, y, c = lax.axis_index("x"), lax.axis_index("y"), lax.axis_index("c")
        my_slot = 4 * x + 2 * y + c
        land[my_slot] = v_ref[...]
        for k, (fx, fy, fc) in enumerate(FLIPS):
            tx, ty, tc = x ^ fx, y ^ fy, c ^ fc
            pltpu.make_async_remote_copy(src_ref=v_ref, dst_ref=land.at[my_slot], send_sem=send.at[k], recv_sem=recv.at[k],
                                         device_id=(tx, ty, tc), device_id_type=MESH).start()
        for k, (fx, fy, fc) in enumerate(FLIPS):
            tx, ty, tc = x ^ fx, y ^ fy, c ^ fc
            cp = pltpu.make_async_remote_copy(src_ref=v_ref, dst_ref=land.at[4 * tx + 2 * ty + tc], send_sem=send.at[k],
                                              recv_sem=recv.at[k], device_id=(tx, ty, tc), device_id_type=MESH)
            cp.wait_send()
            cp.wait_recv()
        acc = land[0]
        for s in range(1, 8):
            acc = acc + land[s]
        o_ref[...] = acc

    return pl.pallas_call(
        body,
        name="all_reduce_small",
        in_specs=[pl.BlockSpec(memory_space=pltpu.VMEM), pl.BlockSpec(memory_space=pl.ANY)],
        out_specs=pl.BlockSpec(memory_space=pltpu.VMEM),
        out_shape=_sds((R, D), F32),
        scratch_shapes=[pltpu.VMEM((8, R, D), F32), pltpu.SemaphoreType.DMA((7,)), pltpu.SemaphoreType.DMA((7,))],
    )(v, dep)


def kernel(x, attn_w_in, attn_w_out, hgrn_w_in, hgrn_w_out, hgrn_norm_g, lb_logits, ln_mix_g, ln_mix_b, ln_ffn_g, ln_ffn_b, ffn_w_up, ffn_w_down, loss_target, m_attn_w_in, m_attn_w_out, m_hgrn_w_in, m_hgrn_w_out, m_hgrn_norm_g, m_lb_logits, m_ln_mix_g, m_ln_mix_b, m_ln_ffn_g, m_ln_ffn_b, m_ffn_w_up, m_ffn_w_down, v_attn_w_in, v_attn_w_out, v_hgrn_w_in, v_hgrn_w_out, v_hgrn_norm_g, v_lb_logits, v_ln_mix_g, v_ln_mix_b, v_ln_ffn_g, v_ln_ffn_b, v_ffn_w_up, v_ffn_w_down):
    xs = x[0]
    tgt = loss_target[0]
    S, D = xs.shape
    F = ffn_w_up.shape[2] * 4
    T1 = _pick(S, (1024, 512, 256))
    T2 = _pick(S, (2048, 1024, 512))
    TH = _pick(S, (512, 256))
    TB = _pick(S, (512, 256))
    TN = _pick(D, (512, 256, 128))
    TF = _pick(F, (1024, 512))
    TG = _pick(3 * D, (1536, 1024, 768))
    TW = _pick(F, (2048, 1024))

    cast = lambda w: w.astype(MXU_DTYPE)
    st_a, tok = _gather_start("gather_a", [cast(attn_w_in[0])], [1], jnp.zeros((8, LANES), F32), halves=True)
    tok, (xs_late, w_aout, w_fup, w_fdown, w_hin, w_hout) = lax.optimization_barrier(
        (tok, (xs, attn_w_out, ffn_w_up, ffn_w_down, hgrn_w_in, hgrn_w_out)))
    st_b, tok = _gather_start("gather_b", [cast(w_aout[0]), cast(w_fup[0]), cast(w_fdown[0])], [0, 1, 0], tok)
    st_c, tok = _gather_start("gather_c", [cast(w_hin[0]), cast(w_hout[0]), hgrn_norm_g, cast(w_fup[1]), cast(w_fdown[1])],
                              [1, 0, 1, 1, 0], tok)

    cos3, sin3 = _rope_tables(S)
    sel = _head_sel(D)
    sel_t = sel.T

    xc3 = _stack_classes("x_classes", xs_late, MXU_DTYPE)
    (wa_in,) = _gather_wait("gather_a_wait", st_a, tok, xc3, cos3, sin3)
    wa_in = _share_halves("share_a", wa_in, _pick(D // 2, (256, 128)))
    P3 = _attn_proj(xc3, wa_in, cos3, sin3, T2, TN)
    o3, lse3 = _attn_fwd(P3, D)
    o_att, L_att = _attn_mix(o3, lse3, sel)
    wa_out, w_up0, w_down0 = _gather_wait("gather_b_wait", st_b, L_att)
    x1, xh1, r1 = _mm_res_ln("attn_out_ln", o_att, wa_out, xs, ln_mix_g[0:1], ln_mix_b[0:1], TH, D)
    a0 = _mlp_up("mlp0_up", x1, w_up0, T1, TF, D)
    x2, xh2, r2 = _mm_res_ln("mlp0_down_ln", a0, w_down0, x1, ln_ffn_g[0:1], ln_ffn_b[0:1], TH, F)

    wh_in, wh_out, norm_g, w_up1, w_down1 = _gather_wait("gather_c_wait", st_c, r2)
    P1 = _plain_mm("hgrn_proj", x2, wh_in, "nn", F32, T1, _pick(3 * D, (1024, 768, 512)), D)
    o_h, n_h, states = _hgrn_fwd(P1, lb_logits, norm_g, TB)
    x3, xh3, r3 = _mm_res_ln("hgrn_out_ln", n_h, wh_out, x2, ln_mix_g[1:2], ln_mix_b[1:2], TH, D)
    a1 = _mlp_up("mlp1_up", x3, w_up1, T1, TF, D)
    x4, xh4, r4 = _mm_res_ln("mlp1_down_ln", a1, w_down1, x3, ln_ffn_g[1:2], ln_ffn_b[1:2], TH, F)

    wgrad = lambda name, a, dy, tm, tn: _plain_mm(name, a, dy, "tn", MXU_DTYPE, tm, tn, T1)
    sq, du4, dg_ffn1, db_ffn1 = _loss_ln_bwd(x4, tgt, xh4, r4, ln_ffn_g[1:2], TH)
    dh1 = _mlp_down_bwd("mlp1_down_bwd", du4, w_down1, a1, T1, TF, D)
    g_down1 = wgrad("g_down1", a1, du4, TW, D)
    g_up1 = wgrad("g_up1", x3, dh1, D, TW)
    sc_1, tok = _scatter_start("scatter_1", [g_down1, g_up1], [0, 1])
    du3, dg_mix1, db_mix1 = _mm_nt_res_ln_bwd("mlp1_up_bwd", dh1, w_up1, du4, xh3, r3, ln_mix_g[1:2], TH, F, tok)
    dn = _plain_mm("hgrn_out_bwd", du3, wh_out, "nt", F32, T1, D, D)
    g_hout = wgrad("g_hgrn_out", n_h, du3, D, D)
    dP1, dg_norm, dlb = _hgrn_bwd(P1, o_h, states, dn, lb_logits, norm_g, TB)
    dP1 = dP1.reshape(3 * S, D)
    g_hin = _matmul("g_hgrn_in", x2, dP1, "tn", D, D, T1, [(_sds((D, 3 * D), MXU_DTYPE), _ij_spec(D, D))], _store_epilogue,
                    b_map=lambda i, j, k: (k + j * (S // T1), 0), mnk=(D, 3 * D, S))[0]
    d_lb_logits = _lb_logits_grad(dlb, lb_logits)
    sc_2, tok = _scatter_start("scatter_2", [g_hout, g_hin], [0, 1])

    du2, dg_ffn0, db_ffn0 = _mm_nt_res_ln_bwd("hgrn_in_bwd", dP1, wh_in, du3, xh2, r2, ln_ffn_g[0:1], T1, D, tok,
                                              a_map=lambda i, j, k: (i + k * (S // T1), 0), mk=3 * D)
    dh0 = _mlp_down_bwd("mlp0_down_bwd", du2, w_down0, a0, T1, TF, D)
    g_down0 = wgrad("g_down0", a0, du2, TW, D)
    g_up0 = wgrad("g_up0", x1, dh0, D, TW)
    sc_3, tok = _scatter_start("scatter_3", [g_down0, g_up0], [0, 1])
    du1, dg_mix0, db_mix0 = _mm_nt_res_ln_bwd("mlp0_up_bwd", dh0, w_up0, du2, xh1, r1, ln_mix_g[0:1], TH, F, tok)
    do, delta = _attn_out_bwd(du1, wa_out, o_att, sel_t, TH, D)
    g_aout = wgrad("g_attn_out", o_att, du1, D, D)
    sc_5, tok = _scatter_start("scatter_5", [g_aout], [0])
    dP3 = _attn_bwd(P3, _stack_classes("do_classes", do, MXU_DTYPE), _stack_classes("lse_classes", L_att, F32),
                    _stack_classes("delta_classes", delta, F32), cos3, sin3, D, tok)
    small = jnp.concatenate([d_lb_logits, dg_mix0, dg_mix1, db_mix0, db_mix1, dg_ffn0, dg_ffn1, db_ffn0, db_ffn1,
                             dg_norm, sq, jnp.zeros((4, D), F32)], axis=0)
    small = _all_reduce_small(small, dP3)
    loss = 0.5 * jnp.sum(small[11]) / D
    grp = lambda j: j // (3 * D // TG)
    g_ain = _matmul("g_attn_in", xc3, dP3, "tn", D, TG, T1, [(_sds((D, 9 * D), MXU_DTYPE), _ij_spec(D, TG))], _store_epilogue,
                    a_map=lambda i, j, k: (k + grp(j) * (S // T1), i),
                    b_map=lambda i, j, k: (k + grp(j) * (S // T1), j % (3 * D // TG)), mnk=(D, 9 * D, S), dep=small)[0]
    sc_4, tok = _scatter_start("scatter_4", [g_ain], [1])
    dxc3 = _matmul("attn_in_bwd", dP3, wa_in, "nt", TH, D, 3 * D, [(_sds((3 * S, D), F32), _ij_spec(TH, D))], _store_epilogue,
                   b_map=lambda i, j, k: (j, k + i // (S // TH)), mnk=(3 * S, D, 3 * D), dep=tok)[0]
    grad_x = _input_grad(du1, dxc3)

    def reduced(name, state, *after):
        gs, lands = _scatter_wait(name + "_wait", state, *after)
        return [_reduce_join(f"{name}_reduce_{i}", l, _own_piece(g, ax)) for i, (l, g, ax) in enumerate(zip(lands, gs, state[4]))]

    r_down1, r_up1 = reduced("scatter_1", sc_1, grad_x)
    r_hout, r_hin = reduced("scatter_2", sc_2, r_up1)
    r_down0, r_up0 = reduced("scatter_3", sc_3, r_hin)
    (r_aout,) = reduced("scatter_5", sc_5, r_up0)

    my_chip = 2 * lax.axis_index("x") + lax.axis_index("y")
    nsh = hgrn_norm_g.shape[1]
    g_norm = lax.dynamic_slice(small[10:11], (0, my_chip * nsh), (1, nsh))

    grads, upd = {}, {}

    def update(nm, w, gs, m, v):
        upd[nm] = _adamw("adamw_" + nm, w, gs, m, v)
        grads[nm] = upd[nm][3]

    update("hgrn_w_in", hgrn_w_in, [r_hin], m_hgrn_w_in, v_hgrn_w_in)
    update("hgrn_w_out", hgrn_w_out, [r_hout], m_hgrn_w_out, v_hgrn_w_out)
    update("ffn_w_up", ffn_w_up, [r_up0, r_up1], m_ffn_w_up, v_ffn_w_up)
    update("ffn_w_down", ffn_w_down, [r_down0, r_down1], m_ffn_w_down, v_ffn_w_down)
    update("attn_w_out", attn_w_out, [r_aout], m_attn_w_out, v_attn_w_out)
    update("hgrn_norm_g", hgrn_norm_g, [g_norm], m_hgrn_norm_g, v_hgrn_norm_g)
    cat = lambda ts: jnp.concatenate(ts, axis=0)
    small_w = cat([lb_logits, ln_mix_g, ln_mix_b, ln_ffn_g, ln_ffn_b])
    small_m = cat([m_lb_logits, m_ln_mix_g, m_ln_mix_b, m_ln_ffn_g, m_ln_ffn_b])
    small_v = cat([v_lb_logits, v_ln_mix_g, v_ln_mix_b, v_ln_ffn_g, v_ln_ffn_b])
    small_upd = _adamw("adamw_small", small_w, [small[0:10]], small_m, small_v)
    for i, nm in enumerate(["lb_logits", "ln_mix_g", "ln_mix_b", "ln_ffn_g", "ln_ffn_b"]):
        grads[nm] = small[2 * i:2 * i + 2]
        upd[nm] = tuple(t[2 * i:2 * i + 2] for t in small_upd)
    done = [upd[k][2] for k in ("hgrn_w_in", "hgrn_w_out", "ffn_w_up", "ffn_w_down", "attn_w_out", "hgrn_norm_g")]
    (r_ain,) = reduced("scatter_4", sc_4, small_upd[2], *done)
    update("attn_w_in", attn_w_in, [r_ain], m_attn_w_in, v_attn_w_in)

    order = ["attn_w_in", "attn_w_out", "hgrn_w_in", "hgrn_w_out", "hgrn_norm_g", "lb_logits", "ln_mix_g", "ln_mix_b",
             "ln_ffn_g", "ln_ffn_b", "ffn_w_up", "ffn_w_down"]
    return (loss, grad_x[None], *[grads[k] for k in order], *[upd[k][0] for k in order],
            *[upd[k][1] for k in order], *[upd[k][2] for k in order])
```

```python
import math

import jax
import jax.numpy as jnp
from jax import lax
from jax.experimental import pallas as pl
from jax.experimental.pallas import tpu as pltpu

F32 = jnp.float32
BF16 = jnp.bfloat16
MXU_DTYPE = BF16

HEAD_DIM = 64
ATTN_BLK = 128
DILATIONS = (1, 4, 16)
ROPE_THETA = 10000.0
HGRN_DK = 128
HGRN_CHUNK = 64
DEPTH = 2
LN_EPS = 1e-5
RMS_EPS = 1e-6
ALPHA = (2 * DEPTH) ** 0.25
ADAM_LR, ADAM_B1, ADAM_B2, ADAM_EPS, ADAM_WD, ADAM_STEP = 0.001, 0.9, 0.999, 1e-08, 0.01, 10

LANES = 128
VMEM_LIMIT = 56 * 1024 * 1024
NEG = -1e30
MESH = pl.DeviceIdType.MESH


def _cparams(sem=None):
    return pltpu.CompilerParams(dimension_semantics=sem, vmem_limit_bytes=VMEM_LIMIT)


def _sds(shape, dtype):
    return jax.ShapeDtypeStruct(tuple(shape), dtype)


def _dg(a, b, ca, cb):
    return lax.dot_general(a, b, (((ca,), (cb,)), ((), ())), preferred_element_type=F32)


def _nn(a, b):
    return _dg(a, b, 1, 0)


def _nt(a, b):
    return _dg(a, b, 1, 1)


def _tn(a, b):
    return _dg(a, b, 0, 0)


def _split3(a):
    hi = a.astype(BF16)
    r = a - hi.astype(F32)
    mid = r.astype(BF16)
    lo = (r - mid.astype(F32)).astype(BF16)
    return hi, mid, lo


def _exact_nn(a, sel):
    hi, mid, lo = _split3(a)
    return _nn(hi, sel) + _nn(mid, sel) + _nn(lo, sel)


def _pick(n, prefs):
    for p in prefs:
        if n % p == 0:
            return p
    return n


def _matmul(name, a, b, form, tm, tn, tk, outs, epilogue, extras=(), a_map=None, b_map=None, mnk=None, dep=None,
            sem=("parallel", "parallel", "arbitrary"), split=None):
    if form == "nn":
        (M, K), N = a.shape, b.shape[1]
        a_spec = pl.BlockSpec((tm, tk), a_map or (lambda i, j, k: (i, k)))
        b_spec = pl.BlockSpec((tk, tn), b_map or (lambda i, j, k: (k, j)))
        ca, cb = 1, 0
    elif form == "nt":
        (M, K), N = a.shape, b.shape[0]
        a_spec = pl.BlockSpec((tm, tk), a_map or (lambda i, j, k: (i, k)))
        b_spec = pl.BlockSpec((tn, tk), b_map or (lambda i, j, k: (j, k)))
        ca, cb = 1, 1
    else:
        (K, M), N = a.shape, b.shape[1]
        a_spec = pl.BlockSpec((tk, tm), a_map or (lambda i, j, k: (k, i)))
        b_spec = pl.BlockSpec((tk, tn), b_map or (lambda i, j, k: (k, j)))
        ca, cb = 0, 0
    if mnk is not None:
        M, N, K = mnk
    assert M % tm == 0 and N % tn == 0 and K % tk == 0, (name, M, N, K, tm, tn, tk)
    nk = K // tk
    ne, no = len(extras), len(outs)
    deps = [] if dep is None else [dep]
    nd = len(deps)

    def body(a_ref, b_ref, *rest):
        extra_refs, out_refs = rest[:ne], rest[ne + nd:ne + nd + no]
        j = pl.program_id(1)
        if split is not None:
            kind, n = split
            assert nk == 1 and form != "tn"
            tiled = [t for _, _, *t in list(extras) + list(outs)]
            refs = list(extra_refs) + list(out_refs)
            for ci in range(n):
                if kind == "cols":
                    cs = slice(ci * (tn // n), (ci + 1) * (tn // n))
                    part = _dg(a_ref[...].astype(MXU_DTYPE), (b_ref[:, cs] if form == "nn" else b_ref[cs, :]).astype(MXU_DTYPE), ca, cb)
                    view = [r.at[:, cs] if t else r for r, t in zip(refs, tiled)]
                else:
                    rs = slice(ci * (tm // n), (ci + 1) * (tm // n))
                    part = _dg(a_ref[rs, :].astype(MXU_DTYPE), b_ref[...].astype(MXU_DTYPE), ca, cb)
                    view = [r.at[rs, :] if t else r for r, t in zip(refs, tiled)]
                epilogue(part, view[:ne], view[ne:], j, ci)
            return
        part = _dg(a_ref[...].astype(MXU_DTYPE), b_ref[...].astype(MXU_DTYPE), ca, cb)
        if nk == 1:
            epilogue(part, extra_refs, out_refs, j, 0)
            return
        acc_ref = rest[-1]
        k = pl.program_id(2)

        @pl.when(k == 0)
        def _():
            acc_ref[...] = part

        @pl.when(k > 0)
        def _():
            acc_ref[...] += part

        @pl.when(k == nk - 1)
        def _():
            epilogue(acc_ref[...], extra_refs, out_refs, j, 0)

    res = pl.pallas_call(
        body,
        name=name,
        grid=(M // tm, N // tn, nk),
        in_specs=[a_spec, b_spec] + [s for _, s, *_ in extras] + [pl.BlockSpec(memory_space=pl.ANY)] * nd,
        out_specs=[s for _, s, *_ in outs],
        out_shape=[o for o, *_ in outs],
        scratch_shapes=[pltpu.VMEM((tm, tn), F32)] if nk > 1 else [],
        compiler_params=_cparams(sem),
    )(a, b, *[e for e, *_ in extras], *deps)
    return res


def _ij_spec(tm, tn):
    return pl.BlockSpec((tm, tn), lambda i, j, k: (i, j))


def _store_epilogue(acc, extra_refs, out_refs, j, ci):
    out_refs[0][...] = acc.astype(out_refs[0].dtype)


def _plain_mm(name, a, b, form, out_dtype, tm, tn, tk):
    M = a.shape[1] if form == "tn" else a.shape[0]
    N = b.shape[0] if form == "nt" else b.shape[1]
    return _matmul(name, a, b, form, tm, tn, tk, [(_sds((M, N), out_dtype), _ij_spec(tm, tn))], _store_epilogue)[0]


def _class_slabs(S):
    assert DILATIONS[0] == 1
    return [(g, d, r, S // d) for g, d in enumerate(DILATIONS) if d > 1 for r in range(d)]


def _stack_classes(name, t, out_dtype):
    S, W = t.shape

    def body(x_ref, o_ref):
        o_ref[0:S, :] = x_ref[...].astype(out_dtype)
        for g, d, r, n in _class_slabs(S):
            o_ref[g * S + r * n:g * S + (r + 1) * n, :] = x_ref[pl.ds(r, n, stride=d), :].astype(out_dtype)

    return pl.pallas_call(
        body,
        name=name,
        grid=(W // LANES,),
        in_specs=[pl.BlockSpec((S, LANES), lambda j: (0, j))],
        out_specs=pl.BlockSpec((3 * S, LANES), lambda j: (0, j)),
        out_shape=_sds((3 * S, W), out_dtype),
        compiler_params=_cparams(("parallel",)),
    )(t)


def _rope_tables(seq):
    half = HEAD_DIM // 2
    inv = ROPE_THETA ** (-jnp.arange(half, dtype=F32) * (2.0 / HEAD_DIM))
    inv = jnp.tile(inv, LANES // half)
    pos = []
    for d in DILATIONS:
        row = jnp.arange(seq)
        pos.append((row % (seq // d)) * d + row // (seq // d))
    ang = jnp.concatenate(pos).astype(F32)[:, None] * inv[None, :]
    first = (jnp.arange(LANES) % HEAD_DIM) < half
    sin = jnp.sin(ang)
    return jnp.cos(ang), jnp.where(first[None, :], -sin, sin)


def _partner(x):
    half = HEAD_DIM // 2
    lane = lax.broadcasted_iota(jnp.int32, x.shape, 1)
    first = (lane % HEAD_DIM) < half
    return jnp.where(first, pltpu.roll(x, LANES - half, 1), pltpu.roll(x, half, 1))


def _attn_proj(x3, w_full, cos3, sin3, tm, tn):
    S3, D = x3.shape
    S = S3 // 3
    per_part = D // tn
    per_group = 3 * per_part

    def epilogue(acc, extra_refs, out_refs, j, ci):
        cos_ref, sin_ref = extra_refs
        o_ref = out_refs[0]
        is_rot = j // per_part < 2
        c = jnp.where(is_rot, cos_ref[...], 1.0)
        s = jnp.where(is_rot, sin_ref[...], 0.0)
        for t in range(acc.shape[1] // LANES):
            xs = acc[:, t * LANES:(t + 1) * LANES]
            o_ref[:, t * LANES:(t + 1) * LANES] = (xs * c + _partner(xs) * s).astype(o_ref.dtype)

    tab = pl.BlockSpec((tm, LANES), lambda i, j, k: (i, 0))
    return _matmul("attn_proj", x3, w_full, "nn", tm, tn, D, [(_sds((S3, 3 * D), MXU_DTYPE), _ij_spec(tm, tn), True)],
                   epilogue, extras=[(cos3, tab), (sin3, tab)],
                   b_map=lambda i, j, k: (k, j + (i // (S // tm)) * per_group), mnk=(S3, 3 * D, D),
                   split=("cols", tn // (2 * LANES)))[0]


def _head_sel(d_model):
    h = jnp.arange(LANES)[:, None]
    l = jnp.arange(d_model)[None, :]
    return (l // HEAD_DIM == h).astype(BF16)


def _class_edges(b, nblk):
    g = b // nblk
    per_class = jnp.where(g == 0, nblk // DILATIONS[0], jnp.where(g == 1, nblk // DILATIONS[1], nblk // DILATIONS[2]))
    pos = (b % nblk) % per_class
    return pos != 0, pos != per_class - 1


def _two_heads(t, top):
    zero = jnp.zeros_like(t)
    return jnp.concatenate([jnp.where(top, t, zero), jnp.where(top, zero, t)], axis=0)


def _band_mask(has_prev):
    B = ATTN_BLK
    row = lax.broadcasted_iota(jnp.int32, (2 * B, 2 * B), 0) % B
    col = lax.broadcasted_iota(jnp.int32, (2 * B, 2 * B), 1)
    in_prev = jnp.logical_and(jnp.logical_and(col < B, col >= row), has_prev)
    in_own = jnp.logical_and(col >= B, col - B <= row)
    return jnp.logical_or(in_prev, in_own)


def _attn_fwd(P3, D):
    S3 = P3.shape[0]
    B = ATTN_BLK
    nblk = S3 // 3 // B
    npairs = D // LANES
    scale = HEAD_DIM ** -0.5

    def body(q_ref, kc_ref, vc_ref, kp_ref, vp_ref, o_ref, lse_ref):
        has_prev, _ = _class_edges(pl.program_id(0), nblk)
        ok = _band_mask(has_prev)
        lane = lax.broadcasted_iota(jnp.int32, (B, LANES), 1)
        top = lane < HEAD_DIM
        lse_acc = jnp.zeros((B, LANES), F32)
        for j in range(npairs):
            sl = slice(j * LANES, (j + 1) * LANES)
            Q = _two_heads(q_ref[:, sl] * scale, top)
            K2 = jnp.concatenate([kp_ref[:, sl], kc_ref[:, sl]], axis=0)
            V2 = jnp.concatenate([vp_ref[:, sl], vc_ref[:, sl]], axis=0)
            s = jnp.where(ok, _nt(Q, K2), NEG)
            m = jnp.max(s, axis=1, keepdims=True)
            p = jnp.exp(s - m)
            l = jnp.sum(p, axis=1, keepdims=True)
            o = _nn((p * (1.0 / l)).astype(MXU_DTYPE), V2)
            o_ref[:, sl] = jnp.where(top, o[:B], o[B:])
            lse = m + jnp.log(l)
            lse_acc = jnp.where(lane == 2 * j, lse[:B], jnp.where(lane == 2 * j + 1, lse[B:], lse_acc))
        lse_ref[...] = lse_acc

    blk = lambda part, prev: pl.BlockSpec(
        (B, D), (lambda b: (jnp.maximum(b - 1, 0), part)) if prev else (lambda b: (b, part)))
    return pl.pallas_call(
        body,
        name="attn_fwd",
        grid=(3 * nblk,),
        in_specs=[blk(0, False), blk(1, False), blk(2, False), blk(1, True), blk(2, True)],
        out_specs=[pl.BlockSpec((B, D), lambda b: (b, 0)), pl.BlockSpec((B, LANES), lambda b: (b, 0))],
        out_shape=[_sds((S3, D), F32), _sds((S3, LANES), F32)],
        compiler_params=_cparams(("parallel",)),
    )(P3, P3, P3, P3, P3)


def _attn_mix(o3, lse3, sel):
    S3, D = o3.shape
    S = S3 // 3

    def body(o3_ref, lse_ref, sel_ref, o_ref, L_ref, w_ref):
        @pl.when(pl.program_id(0) == 0)
        def _():
            w_ref[0] = lse_ref[0:S, :]
            for g, d, r, n in _class_slabs(S):
                w_ref[g, pl.ds(r, n, stride=d), :] = lse_ref[g * S + r * n:g * S + (r + 1) * n, :]
            a, b, c = w_ref[0], w_ref[1], w_ref[2]
            m = jnp.maximum(jnp.maximum(a, b), c)
            L = m + jnp.log(jnp.exp(a - m) + jnp.exp(b - m) + jnp.exp(c - m))
            L_ref[...] = L
            w_ref[0] = jnp.exp(a - L)
            w_ref[1] = jnp.exp(b - L)
            w_ref[2] = jnp.exp(c - L)

        s = sel_ref[...]
        o_ref[...] = _exact_nn(w_ref[0], s) * o3_ref[0:S, :]
        for g, d, r, n in _class_slabs(S):
            rows = pl.ds(r, n, stride=d)
            o_ref[rows, :] += _exact_nn(w_ref[g, rows, :], s) * o3_ref[g * S + r * n:g * S + (r + 1) * n, :]

    return pl.pallas_call(
        body,
        name="attn_mix",
        grid=(D // LANES,),
        in_specs=[pl.BlockSpec((S3, LANES), lambda j: (0, j)), pl.BlockSpec((S3, LANES), lambda j: (0, 0)),
                  pl.BlockSpec((LANES, LANES), lambda j: (0, j))],
        out_specs=[pl.BlockSpec((S, LANES), lambda j: (0, j)), pl.BlockSpec((S, LANES), lambda j: (0, 0))],
        out_shape=[_sds((S, D), F32), _sds((S, LANES), F32)],
        scratch_shapes=[pltpu.VMEM((3, S, LANES), F32)],
        compiler_params=_cparams(("arbitrary",)),
    )(o3, lse3, sel)


def _attn_bwd(P3, do3, L3, delta3, cos3, sin3, D, dep):
    S3 = P3.shape[0]
    B = ATTN_BLK
    nblk = S3 // 3 // B
    npairs = D // LANES
    scale = HEAD_DIM ** -0.5

    def body(c_ref, p_ref, n_ref, doc_ref, don_ref, Lc_ref, Ln_ref, dc_ref, dn_ref, cos_ref, sin_ref, dep_ref, out_ref):
        has_prev, has_next = _class_edges(pl.program_id(0), nblk)
        ok = _band_mask(has_prev)
        row = lax.broadcasted_iota(jnp.int32, (2 * B, B), 0) % B
        col = lax.broadcasted_iota(jnp.int32, (2 * B, B), 1)
        ok_n = jnp.logical_and(col >= row, has_next)
        lane = lax.broadcasted_iota(jnp.int32, (B, LANES), 1)
        top = lane < HEAD_DIM
        cos_t = cos_ref[...]
        sin_inv = -sin_ref[...]
        Lc_all, Ln_all, dc_all, dn_all = Lc_ref[...], Ln_ref[...], dc_ref[...], dn_ref[...]
        pair_col = lambda t, j: jnp.concatenate([t[:, 2 * j:2 * j + 1], t[:, 2 * j + 1:2 * j + 2]], axis=0)
        for j in range(npairs):
            sl = lambda part: slice(part * D + j * LANES, part * D + (j + 1) * LANES)
            kc2, vc2 = c_ref[:, sl(1)], c_ref[:, sl(2)]
            K2 = jnp.concatenate([p_ref[:, sl(1)], kc2], axis=0)
            V2 = jnp.concatenate([p_ref[:, sl(2)], vc2], axis=0)
            Qc = _two_heads(c_ref[:, sl(0)] * scale, top)
            Qn = _two_heads(n_ref[:, sl(0)] * scale, top)
            DOc = _two_heads(doc_ref[:, j * LANES:(j + 1) * LANES].astype(MXU_DTYPE), top)
            DOn = _two_heads(don_ref[:, j * LANES:(j + 1) * LANES].astype(MXU_DTYPE), top)
            P_c = jnp.where(ok, jnp.exp(_nt(Qc, K2) - pair_col(Lc_all, j)), 0.0)
            dS_c = P_c * (_nt(DOc, V2) - pair_col(dc_all, j))
            P_n = jnp.where(ok_n, jnp.exp(_nt(Qn, kc2) - pair_col(Ln_all, j)), 0.0)
            dS_n = P_n * (_nt(DOn, vc2) - pair_col(dn_all, j))
            dq = _nn(dS_c.astype(MXU_DTYPE), K2)
            dq2 = jnp.where(top, dq[:B], dq[B:]) * scale
            Qk = jnp.concatenate([Qc, Qn], axis=0)
            DOk = jnp.concatenate([DOc, DOn], axis=0)
            dk2 = _tn(jnp.concatenate([dS_c[:, B:], dS_n], axis=0).astype(MXU_DTYPE), Qk)
            dv2 = _tn(jnp.concatenate([P_c[:, B:], P_n], axis=0).astype(MXU_DTYPE), DOk)
            out_ref[:, sl(0)] = (dq2 * cos_t + _partner(dq2) * sin_inv).astype(out_ref.dtype)
            out_ref[:, sl(1)] = (dk2 * cos_t + _partner(dk2) * sin_inv).astype(out_ref.dtype)
            out_ref[:, sl(2)] = dv2.astype(out_ref.dtype)

    cur = lambda b: b
    prv = lambda b: jnp.maximum(b - 1, 0)
    nxt = lambda b: jnp.minimum(b + 1, 3 * nblk - 1)
    spec = lambda w, f: pl.BlockSpec((B, w), lambda b: (f(b), 0))
    return pl.pallas_call(
        body,
        name="attn_bwd",
        grid=(3 * nblk,),
        in_specs=[spec(3 * D, cur), spec(3 * D, prv), spec(3 * D, nxt), spec(D, cur), spec(D, nxt),
                  spec(LANES, cur), spec(LANES, nxt), spec(LANES, cur), spec(LANES, nxt), spec(LANES, cur), spec(LANES, cur),
                  pl.BlockSpec(memory_space=pl.ANY)],
        out_specs=spec(3 * D, cur),
        out_shape=_sds((S3, 3 * D), MXU_DTYPE),
        compiler_params=_cparams(("parallel",)),
    )(P3, P3, P3, do3, do3, L3, L3, delta3, delta3, cos3, sin3, dep)


def _input_grad(du, dx3):
    S, D = du.shape

    def body(du_ref, dx_ref, o_ref):
        o_ref[...] = ALPHA * du_ref[...] + dx_ref[0:S, :]
        for g, d, r, n in _class_slabs(S):
            o_ref[pl.ds(r, n, stride=d), :] += dx_ref[g * S + r * n:g * S + (r + 1) * n, :]

    return pl.pallas_call(
        body,
        name="input_grad",
        grid=(D // LANES,),
        in_specs=[pl.BlockSpec((S, LANES), lambda j: (0, j)), pl.BlockSpec((3 * S, LANES), lambda j: (0, j))],
        out_specs=pl.BlockSpec((S, LANES), lambda j: (0, j)),
        out_shape=_sds((S, D), F32),
        compiler_params=_cparams(("parallel",)),
    )(du, dx3)


def _chunk_causal(tb):
    r = lax.broadcasted_iota(jnp.int32, (tb, tb), 0)
    c = lax.broadcasted_iota(jnp.int32, (tb, tb), 1)
    return jnp.logical_and((r // HGRN_CHUNK) == (c // HGRN_CHUNK), r >= c)


def _chunk_sums(a, lower):
    C = HGRN_CHUNK
    r = lax.broadcasted_iota(jnp.int32, (C, C), 0)
    c = lax.broadcasted_iota(jnp.int32, (C, C), 1)
    tri = ((r >= c) if lower else (r <= c)).astype(BF16)
    parts = _split3(a)
    out = []
    for ci in range(a.shape[0] // C):
        rows = slice(ci * C, (ci + 1) * C)
        out.append(_nn(tri, parts[0][rows]) + _nn(tri, parts[1][rows]) + _nn(tri, parts[2][rows]))
    return jnp.concatenate(out, axis=0)


def _chunk_last(b):
    C = HGRN_CHUNK
    return jnp.concatenate([jnp.broadcast_to(b[(ci + 1) * C - 1:(ci + 1) * C, :], (C, b.shape[1]))
                            for ci in range(b.shape[0] // C)], axis=0)


def _lower_bound(lb_ref):
    l0, l1 = lb_ref[0:1, :], lb_ref[1:2, :]
    m = jnp.maximum(l0, l1)
    e0, e1 = jnp.exp(l0 - m), jnp.exp(l1 - m)
    return e1 / (e0 + e1)


def _hgrn_gates(q_raw, z, lb):
    sg = 1.0 / (1.0 + jnp.exp(-z))
    sn = 1.0 / (1.0 + jnp.exp(z))
    f = lb + (1.0 - lb) * sg
    key = (1.0 - lb) * sn
    sq = 1.0 / (1.0 + jnp.exp(-q_raw))
    return sg, sn, f, key, sq


HGRN_HEADS_PER_STEP = 2


def _hgrn_fwd(P1, lb_logits, norm_g, tb):
    S = P1.shape[0]
    D = P1.shape[1] // 3
    K = HGRN_DK
    H = D // K
    HP = HGRN_HEADS_PER_STEP
    C = HGRN_CHUNK
    cpb = tb // C
    nt = S // tb

    def body(q_ref, f_ref, i_ref, lb_ref, g_ref, o_ref, n_ref, st_ref, state):
        t = pl.program_id(1)

        @pl.when(t == 0)
        def _():
            state[...] = jnp.zeros_like(state)

        lb_all = _lower_bound(lb_ref)
        low = _chunk_causal(tb)
        for hh in range(HP):
            lanes = slice(hh * K, (hh + 1) * K)
            q_raw, z, v = q_ref[:, lanes], f_ref[:, lanes], i_ref[:, lanes]
            sg, sn, f, key, sq = _hgrn_gates(q_raw, z, lb_all[:, lanes])
            b = _chunk_sums(jnp.log(f), lower=True)
            qd = (q_raw * sq * jnp.exp(b)).astype(MXU_DTYPE)
            kd = (key * jnp.exp(-b)).astype(MXU_DTYPE)
            kb = (key * jnp.exp(_chunk_last(b) - b)).astype(MXU_DTYPE)
            vm = v.astype(MXU_DTYPE)
            a = jnp.where(low, _nt(qd, kd), 0.0).astype(MXU_DTYPE)
            o_intra = _nn(a, vm)
            st = state[hh]
            outs = []
            for ci in range(cpb):
                rows = slice(ci * C, (ci + 1) * C)
                st_ref[hh, ci] = st
                outs.append(o_intra[rows] + _nt(qd[rows], st.astype(MXU_DTYPE)))
                st = st * jnp.exp(b[(ci + 1) * C - 1:(ci + 1) * C, :]) + _tn(vm[rows], kb[rows])
            state[hh] = st
            o = jnp.concatenate(outs, axis=0)
            o_ref[:, lanes] = o
            rs = lax.rsqrt(jnp.mean(o * o, axis=1, keepdims=True) + RMS_EPS)
            n_ref[:, lanes] = o * rs * g_ref[:, lanes]

    tok = lambda part: pl.BlockSpec((tb, HP * K), lambda h, t: (t, part * (H // HP) + h))
    vec = lambda rows: pl.BlockSpec((rows, HP * K), lambda h, t: (0, h))
    return pl.pallas_call(
        body,
        name="hgrn_fwd",
        grid=(H // HP, nt),
        in_specs=[tok(0), tok(1), tok(2), vec(2), vec(1)],
        out_specs=[tok(0), tok(0), pl.BlockSpec((HP, cpb, K, K), lambda h, t: (h, t, 0, 0))],
        out_shape=[_sds((S, D), F32), _sds((S, D), F32), _sds((H, S // C, K, K), F32)],
        scratch_shapes=[pltpu.VMEM((HP, K, K), F32)],
        compiler_params=_cparams(("parallel", "arbitrary")),
    )(P1, P1, P1, lb_logits, norm_g)


def _hgrn_bwd(P1, o_pre, states, dn, lb_logits, norm_g, tb):
    S = P1.shape[0]
    D = P1.shape[1] // 3
    K = HGRN_DK
    H = D // K
    HP = HGRN_HEADS_PER_STEP
    C = HGRN_CHUNK
    cpb = tb // C
    nt = S // tb

    def body(q_ref, f_ref, i_ref, o_ref, st_ref, dn_ref, lb_ref, g_ref, d_ref, dg_ref, dlb_ref, dstate):
        t = pl.program_id(1)

        @pl.when(t == 0)
        def _():
            dstate[...] = jnp.zeros_like(dstate)
            dg_ref[...] = jnp.zeros_like(dg_ref)
            dlb_ref[...] = jnp.zeros_like(dlb_ref)

        lb_all = _lower_bound(lb_ref)
        low = _chunk_causal(tb)
        for hh in range(HP):
            lanes = slice(hh * K, (hh + 1) * K)
            lb = lb_all[:, lanes]
            gn = g_ref[:, lanes]
            q_raw, z, v = q_ref[:, lanes], f_ref[:, lanes], i_ref[:, lanes]
            sg, sn, f, key, sq = _hgrn_gates(q_raw, z, lb)
            b = _chunk_sums(jnp.log(f), lower=True)
            e_pos, e_neg, e_rel = jnp.exp(b), jnp.exp(-b), jnp.exp(_chunk_last(b) - b)
            qd_f, kd_f, kb_f = q_raw * sq * e_pos, key * e_neg, key * e_rel
            qd, kd, kb = qd_f.astype(MXU_DTYPE), kd_f.astype(MXU_DTYPE), kb_f.astype(MXU_DTYPE)
            vm = v.astype(MXU_DTYPE)
            a = jnp.where(low, _nt(qd, kd), 0.0).astype(MXU_DTYPE)
            o = o_ref[:, lanes]
            dnn = dn_ref[:, lanes]
            rs = lax.rsqrt(jnp.mean(o * o, axis=1, keepdims=True) + RMS_EPS)
            dg_ref[:, lanes] += jnp.sum(dnn * o * rs, axis=0, keepdims=True)
            tg = dnn * gn
            dom = (rs * tg - o * (rs * rs * rs) * jnp.mean(tg * o, axis=1, keepdims=True)).astype(MXU_DTYPE)
            da = jnp.where(low, _nt(dom, vm), 0.0).astype(MXU_DTYPE)
            dv = _tn(a, dom)
            dqd = _nn(da, kd)
            dkd = _tn(da, qd)
            dst = dstate[hh]
            dv_s, dqd_s, dkb_s, dbl_s = [None] * cpb, [None] * cpb, [None] * cpb, [None] * cpb
            for ci in reversed(range(cpb)):
                rows = slice(ci * C, (ci + 1) * C)
                st = st_ref[hh, ci]
                dstm = dst.astype(MXU_DTYPE)
                dec = jnp.exp(b[(ci + 1) * C - 1:(ci + 1) * C, :])
                dv_s[ci] = _nt(kb[rows], dstm)
                dkb_s[ci] = _nn(vm[rows], dstm)
                dqd_s[ci] = _nn(dom[rows], st.astype(MXU_DTYPE))
                db_last = jnp.sum(dkb_s[ci] * kb_f[rows], axis=0, keepdims=True) + jnp.sum(dst * st, axis=0, keepdims=True) * dec
                dbl_s[ci] = jnp.broadcast_to(db_last, (C, K))
                dst = dst * dec + _tn(dom[rows], qd[rows])
            dstate[hh] = dst
            dv = dv + jnp.concatenate(dv_s, axis=0)
            dqd = dqd + jnp.concatenate(dqd_s, axis=0)
            dkb = jnp.concatenate(dkb_s, axis=0)
            dkey = dkd * e_neg + dkb * e_rel
            db = dqd * qd_f - dkd * kd_f - dkb * kb_f
            dlogf = _chunk_sums(db, lower=False) + jnp.concatenate(dbl_s, axis=0)
            gz = (1.0 - lb) * sg * sn
            d_ref[0, :, lanes] = (dqd * e_pos * (sq + q_raw * sq * (1.0 - sq))).astype(d_ref.dtype)
            d_ref[1, :, lanes] = (dlogf * gz / f - dkey * gz).astype(d_ref.dtype)
            d_ref[2, :, lanes] = dv.astype(d_ref.dtype)
            dlb_ref[:, lanes] += jnp.sum(dlogf * sn / f - dkey * sn, axis=0, keepdims=True)

    rev = lambda t: nt - 1 - t
    tok = lambda part: pl.BlockSpec((tb, HP * K), lambda h, t: (rev(t), part * (H // HP) + h))
    vec = lambda rows: pl.BlockSpec((rows, HP * K), lambda h, t: (0, h))
    outs = pl.pallas_call(
        body,
        name="hgrn_bwd",
        grid=(H // HP, nt),
        in_specs=[tok(0), tok(1), tok(2), tok(0),
                  pl.BlockSpec((HP, cpb, K, K), lambda h, t: (h, rev(t), 0, 0)),
                  tok(0), vec(2), vec(1)],
        out_specs=[pl.BlockSpec((3, tb, HP * K), lambda h, t: (0, rev(t), h)), vec(1), vec(1)],
        out_shape=[_sds((3, S, D), MXU_DTYPE)] + [_sds((1, D), F32)] * 2,
        scratch_shapes=[pltpu.VMEM((HP, K, K), F32)],
        compiler_params=_cparams(("parallel", "arbitrary")),
    )(P1, P1, P1, o_pre, states, dn, lb_logits, norm_g)
    return outs


def _lb_logits_grad(dlb, lb_logits):
    def body(d_ref, l_ref, o_ref):
        s1 = _lower_bound(l_ref)
        d = d_ref[...]
        o_ref[0:1, :] = -(1.0 - s1) * s1 * d
        o_ref[1:2, :] = s1 * (1.0 - s1) * d

    return pl.pallas_call(body, name="lb_logits_grad", out_shape=_sds(lb_logits.shape, F32))(dlb, lb_logits)


def _ln_epilogue(acc, extra_refs, out_refs, j, ci):
    res_ref, g_ref, b_ref = extra_refs
    x_ref, xhat_ref, rstd_ref = out_refs
    u = ALPHA * res_ref[...] + acc
    mu = jnp.mean(u, axis=1, keepdims=True)
    cen = u - mu
    rstd = lax.rsqrt(jnp.mean(cen * cen, axis=1, keepdims=True) + LN_EPS)
    xhat = cen * rstd
    xhat_ref[...] = xhat
    x_ref[...] = xhat * g_ref[...] + b_ref[...]
    rstd_ref[...] = rstd


def _mm_res_ln(name, a, w_full, res, g, b, tm, tk):
    S, D = res.shape
    row = pl.BlockSpec((tm, D), lambda i, j, k: (i, 0))
    vec = pl.BlockSpec((1, D), lambda i, j, k: (0, 0))
    outs = [(_sds((S, D), F32), row, True), (_sds((S, D), F32), row, True),
            (_sds((S, 1), F32), pl.BlockSpec((tm, 1), lambda i, j, k: (i, 0)), True)]
    return _matmul(name, a, w_full, "nn", tm, D, tk, outs, _ln_epilogue, extras=[(res, row, True), (g, vec), (b, vec)],
                   split=("rows", 2) if tk == a.shape[1] else None)


def _ln_bwd_rows(dy, xh, rstd, g, first, du_ref, dg_ref, db_ref):
    if first is not None:
        @pl.when(first)
        def _():
            dg_ref[...] = jnp.zeros_like(dg_ref)
            db_ref[...] = jnp.zeros_like(db_ref)

    dg_ref[...] += jnp.sum(dy * xh, axis=0, keepdims=True)
    db_ref[...] += jnp.sum(dy, axis=0, keepdims=True)
    dxh = dy * g
    m1 = jnp.mean(dxh, axis=1, keepdims=True)
    m2 = jnp.mean(dxh * xh, axis=1, keepdims=True)
    du_ref[...] = rstd * (dxh - m1 - xh * m2)


def _loss_ln_bwd(y, target, xhat, rstd, g, tm):
    S, D = y.shape

    def body(y_ref, t_ref, xh_ref, r_ref, g_ref, sq_ref, du_ref, dg_ref, db_ref):
        first = pl.program_id(0) == 0

        @pl.when(first)
        def _():
            sq_ref[...] = jnp.zeros_like(sq_ref)

        e = y_ref[...] - t_ref[...]
        sq_ref[...] += jnp.sum(e * e, axis=0, keepdims=True)
        _ln_bwd_rows(e / D, xh_ref[...], r_ref[...], g_ref[...], first, du_ref, dg_ref, db_ref)

    row = pl.BlockSpec((tm, D), lambda i: (i, 0))
    vec = pl.BlockSpec((1, D), lambda i: (0, 0))
    return pl.pallas_call(
        body,
        name="loss_ln_bwd",
        grid=(S // tm,),
        in_specs=[row, row, row, pl.BlockSpec((tm, 1), lambda i: (i, 0)), vec],
        out_specs=[vec, row, vec, vec],
        out_shape=[_sds((1, D), F32), _sds((S, D), F32), _sds((1, D), F32), _sds((1, D), F32)],
        compiler_params=_cparams(("arbitrary",)),
    )(y, target, xhat, rstd, g)


def _mlp_up(name, x, w_up, tm, tn, tk):
    S = x.shape[0]
    F = w_up.shape[1]

    def epilogue(acc, extra_refs, out_refs, j, ci):
        r = jnp.maximum(acc, 0.0)
        out_refs[0][...] = (r * r).astype(out_refs[0].dtype)

    return _matmul(name, x, w_up, "nn", tm, tn, tk, [(_sds((S, F), MXU_DTYPE), _ij_spec(tm, tn), True)], epilogue,
                   split=("cols", 2))[0]


def _mlp_down_bwd(name, dy, w_down, a, tm, tn, tk):
    S, F = a.shape

    def epilogue(acc, extra_refs, out_refs, j, ci):
        out_refs[0][...] = (acc * (2.0 * jnp.sqrt(extra_refs[0][...].astype(F32)))).astype(out_refs[0].dtype)

    return _matmul(name, dy, w_down, "nt", tm, tn, tk, [(_sds((S, F), MXU_DTYPE), _ij_spec(tm, tn), True)], epilogue,
                   extras=[(a, _ij_spec(tm, tn), True)], split=("cols", 2))[0]


def _mm_nt_res_ln_bwd(name, dy, w, du, xhat, rstd, g, tm, tk, dep, a_map=None, mk=None):
    S, D = du.shape

    def epilogue(acc, extra_refs, out_refs, j, ci):
        du_ref, xh_ref, r_ref, g_ref = extra_refs
        first = (pl.program_id(0) == 0) if ci == 0 else None
        _ln_bwd_rows(ALPHA * du_ref[...] + acc, xh_ref[...], r_ref[...], g_ref[...], first, *out_refs)

    row = pl.BlockSpec((tm, D), lambda i, j, k: (i, 0))
    vec = pl.BlockSpec((1, D), lambda i, j, k: (0, 0))
    return _matmul(name, dy, w, "nt", tm, D, tk,
                   [(_sds((S, D), F32), row, True), (_sds((1, D), F32), vec), (_sds((1, D), F32), vec)], epilogue,
                   extras=[(du, row, True), (xhat, row, True), (rstd, pl.BlockSpec((tm, 1), lambda i, j, k: (i, 0)), True), (g, vec)],
                   a_map=a_map, mnk=None if mk is None else (S, D, mk), dep=dep, sem=("arbitrary", "arbitrary", "arbitrary"),
                   split=("rows", 2) if mk is None else None)


def _attn_out_bwd(du, w_out, o, sel_t, tm, tk):
    S, D = o.shape

    def epilogue(acc, extra_refs, out_refs, j, ci):
        out_refs[0][...] = acc
        out_refs[1][...] = _exact_nn(acc * extra_refs[0][...], extra_refs[1][...])

    row = pl.BlockSpec((tm, D), lambda i, j, k: (i, 0))
    slim = pl.BlockSpec((tm, LANES), lambda i, j, k: (i, 0))
    return _matmul("attn_out_bwd", du, w_out, "nt", tm, D, tk,
                   [(_sds((S, D), F32), row, True), (_sds((S, LANES), F32), slim, True)], epilogue,
                   extras=[(o, row, True), (sel_t, pl.BlockSpec((D, LANES), lambda i, j, k: (0, 0)))], split=("rows", 2))


def _adamw(name, w, gs, m, v):
    shape = w.shape
    cols = shape[-1]
    rows = math.prod(shape[:-1])
    w2, m2, v2 = (t.reshape(rows, cols) for t in (w, m, v))
    gs2 = [g.reshape(-1, cols) for g in gs]
    ng = len(gs2)
    tr = _pick(rows // ng, (256, 128, 64, 32, 16, 8))
    per = rows // ng // tr
    c1 = 1.0 - ADAM_B1 ** ADAM_STEP
    c2 = 1.0 - ADAM_B2 ** ADAM_STEP

    def body(w_ref, m_ref, v_ref, *rest):
        g_refs, (d_ref, nm_ref, nv_ref), g_out = rest[:ng], rest[ng:ng + 3], rest[ng + 3:]
        gg = g_refs[0][...]
        if ng == 2:
            gg = jnp.where(pl.program_id(0) < per, gg, g_refs[1][...])
            g_out[0][...] = gg
        nm = ADAM_B1 * m_ref[...] + (1.0 - ADAM_B1) * gg
        nv = ADAM_B2 * v_ref[...] + (1.0 - ADAM_B2) * (gg * gg)
        nm_ref[...] = nm
        nv_ref[...] = nv
        d_ref[...] = -ADAM_LR * ((nm / c1) / (jnp.sqrt(nv / c2) + ADAM_EPS) + ADAM_WD * w_ref[...])

    blk = pl.BlockSpec((tr, cols), lambda i: (i, 0))
    g_specs = [blk] if ng == 1 else [pl.BlockSpec((tr, cols), lambda i: (jnp.minimum(i, per - 1), 0)),
                                     pl.BlockSpec((tr, cols), lambda i: (jnp.maximum(i - per, 0), 0))]
    nout = 3 if ng == 1 else 4
    outs = pl.pallas_call(
        body,
        name=name,
        grid=(rows // tr,),
        in_specs=[blk] * 3 + g_specs,
        out_specs=[blk] * nout,
        out_shape=[_sds((rows, cols), F32)] * nout,
        compiler_params=_cparams(("parallel",)),
    )(w2, m2, v2, *gs2)
    g_full = outs[3] if ng == 2 else gs2[0]
    return tuple(o.reshape(shape) for o in (outs[0], outs[1], outs[2], g_full))


HBM = pl.BlockSpec(memory_space=pl.ANY)


def _shard_slice(ref, axis, size, index):
    idx = [slice(None)] * len(ref.shape)
    idx[axis] = pl.ds(pl.multiple_of(index * size, 8), size)
    return ref.at[tuple(idx)]


def _share_halves(name, full, tr):
    R, W4 = full.shape
    W, h = W4 // 4, R // 2
    steps = [(k, t) for k in range(3) for t in range(h // tr)]

    def body(f_in, f_ref, buf, lsem, ssem, rsem):
        x, y, c = lax.axis_index("x"), lax.axis_index("y"), lax.axis_index("c")
        sibling = (x, y, 1 - c)
        chips = [(1 - x, y), (x, 1 - y), (1 - x, 1 - y)]

        def tile(k, t):
            px, py = chips[k]
            return f_ref.at[pl.ds(pl.multiple_of(c * h + t * tr, 8), tr), pl.ds(pl.multiple_of((2 * px + py) * W, LANES), W)]

        sends = []
        for s, (k, t) in enumerate(steps):
            slot = s % 2
            if s >= 2:
                sends[s - 2].wait_send()
            lc = pltpu.make_async_copy(tile(k, t), buf.at[slot], lsem.at[slot])
            lc.start()
            lc.wait()
            rc = pltpu.make_async_remote_copy(src_ref=buf.at[slot], dst_ref=tile(k, t), send_sem=ssem.at[slot], recv_sem=rsem,
                                              device_id=sibling, device_id_type=MESH)
            rc.start()
            sends.append(rc)
        for rc in sends[-2:]:
            rc.wait_send()
        whole = f_ref.at[pl.ds(0, h), pl.ds(0, 3 * W)]
        pltpu.make_async_remote_copy(src_ref=whole, dst_ref=whole, send_sem=ssem.at[0], recv_sem=rsem,
                                     device_id=sibling, device_id_type=MESH).wait_recv()

    return pl.pallas_call(
        body,
        name=name,
        in_specs=[HBM],
        out_specs=HBM,
        out_shape=_sds(full.shape, full.dtype),
        input_output_aliases={0: 0},
        scratch_shapes=[pltpu.VMEM((2, tr, W), full.dtype), pltpu.SemaphoreType.DMA((2,)), pltpu.SemaphoreType.DMA((2,)),
                        pltpu.SemaphoreType.DMA(())],
    )(full)


IN_HBM = pl.BlockSpec(memory_space=pltpu.HBM)
IN_SEM = pl.BlockSpec(memory_space=pltpu.SEMAPHORE)
DATAFLOW = pltpu.SideEffectType.DATAFLOW_SIDE_EFFECTING


def _hbm(t):
    return pltpu.with_memory_space_constraint(t, pltpu.HBM)


def _token_spec():
    return pl.BlockSpec(memory_space=pltpu.VMEM)


def _gather_copies(s_refs, f_refs, axes, halves, send, recv, loc, arrival):
    x, y, c = lax.axis_index("x"), lax.axis_index("y"), lax.axis_index("c")
    chips = [(1 - x, y), (x, 1 - y), (1 - x, 1 - y)]
    local, remote = [], []
    for a in range(len(s_refs)):
        size = s_refs[a].shape[axes[a]]
        local.append(pltpu.make_async_copy(s_refs[a], _shard_slice(f_refs[a], axes[a], size, 2 * x + y), loc.at[a]))
        for k, (px, py) in enumerate(chips):
            block = (2 * px + py) if arrival else (2 * x + y)
            src, dst = s_refs[a], _shard_slice(f_refs[a], axes[a], size, block)
            if halves:
                assert axes[a] == 1 and len(s_refs[a].shape) == 2
                h = s_refs[a].shape[0] // 2
                rows = pl.ds(pl.multiple_of(c * h, 8), h)
                src = s_refs[a].at[rows, :]
                dst = f_refs[a].at[rows, pl.ds(pl.multiple_of(block * size, LANES), size)]
            remote.append(pltpu.make_async_remote_copy(src_ref=src, dst_ref=dst, send_sem=send.at[3 * a + k],
                                                       recv_sem=recv.at[3 * a + k], device_id=(px, py, c), device_id_type=MESH))
    return local, remote


def _gather_start(name, shards, axes, after, halves=False):
    n = len(shards)
    fulls = []
    for s, ax in zip(shards, axes):
        fs = list(s.shape)
        fs[ax] *= 4
        fulls.append(lax.empty(tuple(fs), s.dtype))

    def body(*refs):
        s_refs, f_refs = refs[:n], refs[n:2 * n]
        send, recv, loc, token = refs[2 * n + 1], refs[2 * n + 2], refs[2 * n + 3], refs[-1]
        local, remote = _gather_copies(s_refs, f_refs, axes, halves, send, recv, loc, arrival=False)
        for cp in remote + local:
            cp.start()
        token[...] = jnp.zeros_like(token)

    outs = pl.pallas_call(
        body,
        name=name,
        out_shape=(pltpu.SemaphoreType.DMA((3 * n,)), pltpu.SemaphoreType.DMA((3 * n,)), pltpu.SemaphoreType.DMA((n,)),
                   *[pltpu.HBM(t.shape, t.dtype) for t in shards + fulls], _sds((8, LANES), F32)),
        in_specs=[IN_HBM] * (2 * n) + [HBM],
        out_specs=(IN_SEM, IN_SEM, IN_SEM, *[IN_HBM] * (2 * n), _token_spec()),
        input_output_aliases={i: 3 + i for i in range(2 * n)},
        compiler_params=pltpu.CompilerParams(has_side_effects=DATAFLOW),
    )(*[_hbm(t) for t in shards + fulls], after)
    return (outs[0], outs[1], outs[2], list(outs[3:3 + n]), list(outs[3 + n:3 + 2 * n]), axes, halves), outs[-1]


def _gather_wait(name, state, *after):
    send, recv, loc, s_thru, f_thru, axes, halves = state
    n = len(s_thru)

    def body(*refs):
        s_refs, f_refs = refs[:n], refs[n:2 * n]
        local, remote = _gather_copies(s_refs, f_refs, axes, halves, refs[2 * n], refs[2 * n + 1], refs[2 * n + 2], arrival=True)
        for cp in local:
            cp.wait()
        for cp in remote:
            cp.wait_send()
            cp.wait_recv()

    outs = pl.pallas_call(
        body,
        name=name,
        out_shape=tuple(pltpu.HBM(t.shape, t.dtype) for t in s_thru + f_thru),
        in_specs=[IN_HBM] * (2 * n) + [IN_SEM, IN_SEM, IN_SEM] + [HBM] * len(after),
        out_specs=tuple([IN_HBM] * (2 * n)),
        input_output_aliases={i: i for i in range(2 * n)},
        compiler_params=pltpu.CompilerParams(has_side_effects=DATAFLOW),
    )(*s_thru, *f_thru, send, recv, loc, *after)
    return list(outs[n:2 * n])


FLIPS = [(fx, fy, fc) for fx in (0, 1) for fy in (0, 1) for fc in (0, 1)][1:]


def _piece_shape(shape, axis):
    ps = list(shape)
    if axis == 0:
        ps[0] //= 8
    else:
        ps[0] //= 2
        ps[axis] //= 4
    return tuple(ps)


def _piece(ref, axis, q, c):
    shape = ref.shape
    idx = [slice(None)] * len(shape)
    if axis == 0:
        h = shape[0] // 8
        idx[0] = pl.ds(pl.multiple_of((2 * q + c) * h, 8), h)
    else:
        h, w = shape[0] // 2, shape[axis] // 4
        idx[0] = pl.ds(c * h, h)
        idx[axis] = pl.ds(pl.multiple_of(q * w, LANES if axis == len(shape) - 1 else 8), w)
    return ref.at[tuple(idx)]


def _own_piece(g, axis):
    ps = _piece_shape(g.shape, axis)
    q, c = 2 * lax.axis_index("x") + lax.axis_index("y"), lax.axis_index("c")
    start = [0] * len(ps)
    if axis == 0:
        start[0] = (2 * q + c) * ps[0]
    else:
        start[0] = c * ps[0]
        start[axis] = q * ps[axis]
    return lax.dynamic_slice(g, start, ps)


def _scatter_copies(g_refs, l_refs, axes, send, recv):
    x, y, c = lax.axis_index("x"), lax.axis_index("y"), lax.axis_index("c")
    out = []
    for a in range(len(g_refs)):
        for k, (fx, fy, fc) in enumerate(FLIPS):
            tx, ty, tc = x ^ fx, y ^ fy, c ^ fc
            out.append(pltpu.make_async_remote_copy(
                src_ref=_piece(g_refs[a], axes[a], 2 * tx + ty, tc), dst_ref=l_refs[a].at[k],
                send_sem=send.at[7 * a + k], recv_sem=recv.at[7 * a + k], device_id=(tx, ty, tc), device_id_type=MESH))
    return out


def _scatter_start(name, grads, axes):
    n = len(grads)
    lands = [lax.empty((7,) + _piece_shape(g.shape, ax), g.dtype) for g, ax in zip(grads, axes)]

    def body(*refs):
        g_refs, l_refs = refs[:n], refs[n:2 * n]
        send, recv, token = refs[2 * n], refs[2 * n + 1], refs[-1]
        for cp in _scatter_copies(g_refs, l_refs, axes, send, recv):
            cp.start()
        token[...] = jnp.zeros_like(token)

    outs = pl.pallas_call(
        body,
        name=name,
        out_shape=(pltpu.SemaphoreType.DMA((7 * n,)), pltpu.SemaphoreType.DMA((7 * n,)),
                   *[pltpu.HBM(t.shape, t.dtype) for t in grads + lands], _sds((8, LANES), F32)),
        in_specs=[IN_HBM] * (2 * n),
        out_specs=(IN_SEM, IN_SEM, *[IN_HBM] * (2 * n), _token_spec()),
        input_output_aliases={i: 2 + i for i in range(2 * n)},
        compiler_params=pltpu.CompilerParams(has_side_effects=DATAFLOW),
    )(*[_hbm(t) for t in grads + lands])
    return (outs[0], outs[1], list(outs[2:2 + n]), list(outs[2 + n:2 + 2 * n]), axes), outs[-1]


def _scatter_wait(name, state, *after):
    send, recv, g_thru, l_thru, axes = state
    n = len(g_thru)

    def body(*refs):
        g_refs, l_refs = refs[:n], refs[n:2 * n]
        for cp in _scatter_copies(g_refs, l_refs, axes, refs[2 * n], refs[2 * n + 1]):
            cp.wait_send()
            cp.wait_recv()

    outs = pl.pallas_call(
        body,
        name=name,
        out_shape=tuple(pltpu.HBM(t.shape, t.dtype) for t in g_thru + l_thru),
        in_specs=[IN_HBM] * (2 * n) + [IN_SEM, IN_SEM] + [HBM] * len(after),
        out_specs=tuple([IN_HBM] * (2 * n)),
        input_output_aliases={i: i for i in range(2 * n)},
        compiler_params=pltpu.CompilerParams(has_side_effects=DATAFLOW),
    )(*g_thru, *l_thru, send, recv, *after)
    return list(outs[:n]), list(outs[n:2 * n])


def _reduce_join(name, landing, own):
    piece = own.shape
    C = piece[-1]
    R = math.prod(piece[:-1])
    l3 = landing.reshape(7, R, C)
    own2 = own.reshape(R, C)
    tr = _pick(R, [t for t in (512, 256, 128, 64, 32, 16, 8) if t * C <= 256 * 1024])
    nsteps = R // tr

    def body(own_ref, l_ref, o_ref, buf, send, loc, recv):
        i = pl.program_id(0)
        x, y, c = lax.axis_index("x"), lax.axis_index("y"), lax.axis_index("c")
        sibling = (x, y, 1 - c)

        def copies(slot, step):
            dst = o_ref.at[pl.ds(pl.multiple_of(c * R + step * tr, 8), tr), :]
            return (pltpu.make_async_copy(buf.at[slot], dst, loc.at[slot]),
                    pltpu.make_async_remote_copy(src_ref=buf.at[slot], dst_ref=dst, send_sem=send.at[slot], recv_sem=recv,
                                                 device_id=sibling, device_id_type=MESH))

        @pl.when(i >= 2)
        def _():
            lc, rc = copies(i % 2, i - 2)
            lc.wait()
            rc.wait_send()

        acc = own_ref[...].astype(F32)
        for s in range(7):
            acc = acc + l_ref[s].astype(F32)
        buf[i % 2] = acc
        lc, rc = copies(i % 2, i)
        lc.start()
        rc.start()

        @pl.when(i == nsteps - 1)
        def _():
            for st in range(max(nsteps - 2, 0), nsteps):
                lc, rc = copies(st % 2, st)
                lc.wait()
                rc.wait_send()
            theirs = o_ref.at[pl.ds(pl.multiple_of((1 - c) * R, 8), R), :]
            pltpu.make_async_remote_copy(src_ref=theirs, dst_ref=theirs, send_sem=send.at[0], recv_sem=recv,
                                         device_id=sibling, device_id_type=MESH).wait_recv()

    return pl.pallas_call(
        body,
        name=name,
        grid=(nsteps,),
        in_specs=[pl.BlockSpec((tr, C), lambda i: (i, 0)), pl.BlockSpec((7, tr, C), lambda i: (0, i, 0))],
        out_specs=HBM,
        out_shape=_sds((2 * R, C), F32),
        scratch_shapes=[pltpu.VMEM((2, tr, C), F32), pltpu.SemaphoreType.DMA((2,)), pltpu.SemaphoreType.DMA((2,)),
                        pltpu.SemaphoreType.DMA(())],
        compiler_params=_cparams(("arbitrary",)),
    )(own2, l3)


def _all_reduce_small(v, dep):
    R, D = v.shape

    def body(v_ref, dep_ref, o_ref, land, send, recv):
        x, y, c = lax.axis_index("x"), lax.axis_index("y"), lax.axis_index("c")
        my_slot = 4 * x + 2 * y + c
        land[my_slot] = v_ref[...]
        for k, (fx, fy, fc) in enumerate(FLIPS):
            tx, ty, tc = x ^ fx, y ^ fy, c ^ fc
            pltpu.make_async_remote_copy(src_ref=v_ref, dst_ref=land.at[my_slot], send_sem=send.at[k], recv_sem=recv.at[k],
                                         device_id=(tx, ty, tc), device_id_type=MESH).start()
        for k, (fx, fy, fc) in enumerate(FLIPS):
            tx, ty, tc = x ^ fx, y ^ fy, c ^ fc
            cp = pltpu.make_async_remote_copy(src_ref=v_ref, dst_ref=land.at[4 * tx + 2 * ty + tc], send_sem=send.at[k],
                                              recv_sem=recv.at[k], device_id=(tx, ty, tc), device_id_type=MESH)
            cp.wait_send()
            cp.wait_recv()
        acc = land[0]
        for s in range(1, 8):
            acc = acc + land[s]
        o_ref[...] = acc

    return pl.pallas_call(
        body,
        name="all_reduce_small",
        in_specs=[pl.BlockSpec(memory_space=pltpu.VMEM), pl.BlockSpec(memory_space=pl.ANY)],
        out_specs=pl.BlockSpec(memory_space=pltpu.VMEM),
        out_shape=_sds((R, D), F32),
        scratch_shapes=[pltpu.VMEM((8, R, D), F32), pltpu.SemaphoreType.DMA((7,)), pltpu.SemaphoreType.DMA((7,))],
    )(v, dep)


def kernel(x, attn_w_in, attn_w_out, hgrn_w_in, hgrn_w_out, hgrn_norm_g, lb_logits, ln_mix_g, ln_mix_b, ln_ffn_g, ln_ffn_b, ffn_w_up, ffn_w_down, loss_target, m_attn_w_in, m_attn_w_out, m_hgrn_w_in, m_hgrn_w_out, m_hgrn_norm_g, m_lb_logits, m_ln_mix_g, m_ln_mix_b, m_ln_ffn_g, m_ln_ffn_b, m_ffn_w_up, m_ffn_w_down, v_attn_w_in, v_attn_w_out, v_hgrn_w_in, v_hgrn_w_out, v_hgrn_norm_g, v_lb_logits, v_ln_mix_g, v_ln_mix_b, v_ln_ffn_g, v_ln_ffn_b, v_ffn_w_up, v_ffn_w_down):
    xs = x[0]
    tgt = loss_target[0]
    S, D = xs.shape
    F = ffn_w_up.shape[2] * 4
    T1 = _pick(S, (1024, 512, 256))
    T2 = _pick(S, (2048, 1024, 512))
    TH = _pick(S, (512, 256))
    TB = _pick(S, (512, 256))
    TN = _pick(D, (512, 256, 128))
    TF = _pick(F, (1024, 512))
    TG = _pick(3 * D, (1536, 1024, 768))
    TW = _pick(F, (2048, 1024))

    cast = lambda w: w.astype(MXU_DTYPE)
    st_a, tok = _gather_start("gather_a", [cast(attn_w_in[0])], [1], jnp.zeros((8, LANES), F32), halves=True)
    tok, (xs_late, w_aout, w_fup, w_fdown, w_hin, w_hout) = lax.optimization_barrier(
        (tok, (xs, attn_w_out, ffn_w_up, ffn_w_down, hgrn_w_in, hgrn_w_out)))
    st_b, tok = _gather_start("gather_b", [cast(w_aout[0]), cast(w_fup[0]), cast(w_fdown[0])], [0, 1, 0], tok)
    st_c, tok = _gather_start("gather_c", [cast(w_hin[0]), cast(w_hout[0]), hgrn_norm_g, cast(w_fup[1]), cast(w_fdown[1])],
                              [1, 0, 1, 1, 0], tok)

    cos3, sin3 = _rope_tables(S)
    sel = _head_sel(D)
    sel_t = sel.T

    xc3 = _stack_classes("x_classes", xs_late, MXU_DTYPE)
    (wa_in,) = _gather_wait("gather_a_wait", st_a, tok, xc3, cos3, sin3)
    wa_in = _share_halves("share_a", wa_in, _pick(D // 2, (256, 128)))
    P3 = _attn_proj(xc3, wa_in, cos3, sin3, T2, TN)
    o3, lse3 = _attn_fwd(P3, D)
    o_att, L_att = _attn_mix(o3, lse3, sel)
    wa_out, w_up0, w_down0 = _gather_wait("gather_b_wait", st_b, L_att)
    x1, xh1, r1 = _mm_res_ln("attn_out_ln", o_att, wa_out, xs, ln_mix_g[0:1], ln_mix_b[0:1], TH, D)
    a0 = _mlp_up("mlp0_up", x1, w_up0, T1, TF, D)
    x2, xh2, r2 = _mm_res_ln("mlp0_down_ln", a0, w_down0, x1, ln_ffn_g[0:1], ln_ffn_b[0:1], TH, F)

    wh_in, wh_out, norm_g, w_up1, w_down1 = _gather_wait("gather_c_wait", st_c, r2)
    P1 = _plain_mm("hgrn_proj", x2, wh_in, "nn", F32, T1, _pick(3 * D, (1024, 768, 512)), D)
    o_h, n_h, states = _hgrn_fwd(P1, lb_logits, norm_g, TB)
    x3, xh3, r3 = _mm_res_ln("hgrn_out_ln", n_h, wh_out, x2, ln_mix_g[1:2], ln_mix_b[1:2], TH, D)
    a1 = _mlp_up("mlp1_up", x3, w_up1, T1, TF, D)
    x4, xh4, r4 = _mm_res_ln("mlp1_down_ln", a1, w_down1, x3, ln_ffn_g[1:2], ln_ffn_b[1:2], TH, F)

    wgrad = lambda name, a, dy, tm, tn: _plain_mm(name, a, dy, "tn", MXU_DTYPE, tm, tn, T1)
    sq, du4, dg_ffn1, db_ffn1 = _loss_ln_bwd(x4, tgt, xh4, r4, ln_ffn_g[1:2], TH)
    dh1 = _mlp_down_bwd("mlp1_down_bwd", du4, w_down1, a1, T1, TF, D)
    g_down1 = wgrad("g_down1", a1, du4, TW, D)
    g_up1 = wgrad("g_up1", x3, dh1, D, TW)
    sc_1, tok = _scatter_start("scatter_1", [g_down1, g_up1], [0, 1])
    du3, dg_mix1, db_mix1 = _mm_nt_res_ln_bwd("mlp1_up_bwd", dh1, w_up1, du4, xh3, r3, ln_mix_g[1:2], TH, F, tok)
    dn = _plain_mm("hgrn_out_bwd", du3, wh_out, "nt", F32, T1, D, D)
    g_hout = wgrad("g_hgrn_out", n_h, du3, D, D)
    dP1, dg_norm, dlb = _hgrn_bwd(P1, o_h, states, dn, lb_logits, norm_g, TB)
    dP1 = dP1.reshape(3 * S, D)
    g_hin = _matmul("g_hgrn_in", x2, dP1, "tn", D, D, T1, [(_sds((D, 3 * D), MXU_DTYPE), _ij_spec(D, D))], _store_epilogue,
                    b_map=lambda i, j, k: (k + j * (S // T1), 0), mnk=(D, 3 * D, S))[0]
    d_lb_logits = _lb_logits_grad(dlb, lb_logits)
    sc_2, tok = _scatter_start("scatter_2", [g_hout, g_hin], [0, 1])

    du2, dg_ffn0, db_ffn0 = _mm_nt_res_ln_bwd("hgrn_in_bwd", dP1, wh_in, du3, xh2, r2, ln_ffn_g[0:1], T1, D, tok,
                                              a_map=lambda i, j, k: (i + k * (S // T1), 0), mk=3 * D)
    dh0 = _mlp_down_bwd("mlp0_down_bwd", du2, w_down0, a0, T1, TF, D)
    g_down0 = wgrad("g_down0", a0, du2, TW, D)
    g_up0 = wgrad("g_up0", x1, dh0, D, TW)
    sc_3, tok = _scatter_start("scatter_3", [g_down0, g_up0], [0, 1])
    du1, dg_mix0, db_mix0 = _mm_nt_res_ln_bwd("mlp0_up_bwd", dh0, w_up0, du2, xh1, r1, ln_mix_g[0:1], TH, F, tok)
    do, delta = _attn_out_bwd(du1, wa_out, o_att, sel_t, TH, D)
    g_aout = wgrad("g_attn_out", o_att, du1, D, D)
    sc_5, tok = _scatter_start("scatter_5", [g_aout], [0])
    dP3 = _attn_bwd(P3, _stack_classes("do_classes", do, MXU_DTYPE), _stack_classes("lse_classes", L_att, F32),
                    _stack_classes("delta_classes", delta, F32), cos3, sin3, D, tok)
    small = jnp.concatenate([d_lb_logits, dg_mix0, dg_mix1, db_mix0, db_mix1, dg_ffn0, dg_ffn1, db_ffn0, db_ffn1,
                             dg_norm, sq, jnp.zeros((4, D), F32)], axis=0)
    small = _all_reduce_small(small, dP3)
    loss = 0.5 * jnp.sum(small[11]) / D
    grp = lambda j: j // (3 * D // TG)
    g_ain = _matmul("g_attn_in", xc3, dP3, "tn", D, TG, T1, [(_sds((D, 9 * D), MXU_DTYPE), _ij_spec(D, TG))], _store_epilogue,
                    a_map=lambda i, j, k: (k + grp(j) * (S // T1), i),
                    b_map=lambda i, j, k: (k + grp(j) * (S // T1), j % (3 * D // TG)), mnk=(D, 9 * D, S), dep=small)[0]
    sc_4, tok = _scatter_start("scatter_4", [g_ain], [1])
    dxc3 = _matmul("attn_in_bwd", dP3, wa_in, "nt", TH, D, 3 * D, [(_sds((3 * S, D), F32), _ij_spec(TH, D))], _store_epilogue,
                   b_map=lambda i, j, k: (j, k + i // (S // TH)), mnk=(3 * S, D, 3 * D), dep=tok)[0]
    grad_x = _input_grad(du1, dxc3)

    def reduced(name, state, *after):
        gs, lands = _scatter_wait(name + "_wait", state, *after)
        return [_reduce_join(f"{name}_reduce_{i}", l, _own_piece(g, ax)) for i, (l, g, ax) in enumerate(zip(lands, gs, state[4]))]

    r_down1, r_up1 = reduced("scatter_1", sc_1, grad_x)
    r_hout, r_hin = reduced("scatter_2", sc_2, r_up1)
    r_down0, r_up0 = reduced("scatter_3", sc_3, r_hin)
    (r_aout,) = reduced("scatter_5", sc_5, r_up0)

    my_chip = 2 * lax.axis_index("x") + lax.axis_index("y")
    nsh = hgrn_norm_g.shape[1]
    g_norm = lax.dynamic_slice(small[10:11], (0, my_chip * nsh), (1, nsh))

    grads, upd = {}, {}

    def update(nm, w, gs, m, v):
        upd[nm] = _adamw("adamw_" + nm, w, gs, m, v)
        grads[nm] = upd[nm][3]

    update("hgrn_w_in", hgrn_w_in, [r_hin], m_hgrn_w_in, v_hgrn_w_in)
    update("hgrn_w_out", hgrn_w_out, [r_hout], m_hgrn_w_out, v_hgrn_w_out)
    update("ffn_w_up", ffn_w_up, [r_up0, r_up1], m_ffn_w_up, v_ffn_w_up)
    update("ffn_w_down", ffn_w_down, [r_down0, r_down1], m_ffn_w_down, v_ffn_w_down)
    update("attn_w_out", attn_w_out, [r_aout], m_attn_w_out, v_attn_w_out)
    update("hgrn_norm_g", hgrn_norm_g, [g_norm], m_hgrn_norm_g, v_hgrn_norm_g)
    cat = lambda ts: jnp.concatenate(ts, axis=0)
    small_w = cat([lb_logits, ln_mix_g, ln_mix_b, ln_ffn_g, ln_ffn_b])
    small_m = cat([m_lb_logits, m_ln_mix_g, m_ln_mix_b, m_ln_ffn_g, m_ln_ffn_b])
    small_v = cat([v_lb_logits, v_ln_mix_g, v_ln_mix_b, v_ln_ffn_g, v_ln_ffn_b])
    small_upd = _adamw("adamw_small", small_w, [small[0:10]], small_m, small_v)
    for i, nm in enumerate(["lb_logits", "ln_mix_g", "ln_mix_b", "ln_ffn_g", "ln_ffn_b"]):
        grads[nm] = small[2 * i:2 * i + 2]
        upd[nm] = tuple(t[2 * i:2 * i + 2] for t in small_upd)
    done = [upd[k][2] for k in ("hgrn_w_in", "hgrn_w_out", "ffn_w_up", "ffn_w_down", "attn_w_out", "hgrn_norm_g")]
    (r_ain,) = reduced("scatter_4", sc_4, small_upd[2], *done)
    update("attn_w_in", attn_w_in, [r_ain], m_attn_w_in, v_attn_w_in)

    order = ["attn_w_in", "attn_w_out", "hgrn_w_in", "hgrn_w_out", "hgrn_norm_g", "lb_logits", "ln_mix_g", "ln_mix_b",
             "ln_ffn_g", "ln_ffn_b", "ffn_w_up", "ffn_w_down"]
    return (loss, grad_x[None], *[grads[k] for k in order], *[upd[k][0] for k in order],
            *[upd[k][1] for k in order], *[upd[k][2] for k in order])
```

```python
import math

import jax
import jax.numpy as jnp
from jax import lax
from jax.experimental import pallas as pl
from jax.experimental.pallas import tpu as pltpu

F32 = jnp.float32
BF16 = jnp.bfloat16
MXU_DTYPE = BF16

HEAD_DIM = 64
ATTN_BLK = 128
DILATIONS = (1, 4, 16)
ROPE_THETA = 10000.0
HGRN_DK = 128
HGRN_CHUNK = 64
DEPTH = 2
LN_EPS = 1e-5
RMS_EPS = 1e-6
ALPHA = (2 * DEPTH) ** 0.25
ADAM_LR, ADAM_B1, ADAM_B2, ADAM_EPS, ADAM_WD, ADAM_STEP = 0.001, 0.9, 0.999, 1e-08, 0.01, 10

LANES = 128
VMEM_LIMIT = 56 * 1024 * 1024
NEG = -1e30
MESH = pl.DeviceIdType.MESH


def _cparams(sem=None):
    return pltpu.CompilerParams(dimension_semantics=sem, vmem_limit_bytes=VMEM_LIMIT)


def _sds(shape, dtype):
    return jax.ShapeDtypeStruct(tuple(shape), dtype)


def _dg(a, b, ca, cb):
    return lax.dot_general(a, b, (((ca,), (cb,)), ((), ())), preferred_element_type=F32)


def _nn(a, b):
    return _dg(a, b, 1, 0)


def _nt(a, b):
    return _dg(a, b, 1, 1)


def _tn(a, b):
    return _dg(a, b, 0, 0)


def _split3(a):
    hi = a.astype(BF16)
    r = a - hi.astype(F32)
    mid = r.astype(BF16)
    lo = (r - mid.astype(F32)).astype(BF16)
    return hi, mid, lo


def _exact_nn(a, sel):
    hi, mid, lo = _split3(a)
    return _nn(hi, sel) + _nn(mid, sel) + _nn(lo, sel)


def _pick(n, prefs):
    for p in prefs:
        if n % p == 0:
            return p
    return n


def _matmul(name, a, b, form, tm, tn, tk, outs, epilogue, extras=(), a_map=None, b_map=None, mnk=None, dep=None,
            sem=("parallel", "parallel", "arbitrary"), split=None):
    if form == "nn":
        (M, K), N = a.shape, b.shape[1]
        a_spec = pl.BlockSpec((tm, tk), a_map or (lambda i, j, k: (i, k)))
        b_spec = pl.BlockSpec((tk, tn), b_map or (lambda i, j, k: (k, j)))
        ca, cb = 1, 0
    elif form == "nt":
        (M, K), N = a.shape, b.shape[0]
        a_spec = pl.BlockSpec((tm, tk), a_map or (lambda i, j, k: (i, k)))
        b_spec = pl.BlockSpec((tn, tk), b_map or (lambda i, j, k: (j, k)))
        ca, cb = 1, 1
    else:
        (K, M), N = a.shape, b.shape[1]
        a_spec = pl.BlockSpec((tk, tm), a_map or (lambda i, j, k: (k, i)))
        b_spec = pl.BlockSpec((tk, tn), b_map or (lambda i, j, k: (k, j)))
        ca, cb = 0, 0
    if mnk is not None:
        M, N, K = mnk
    assert M % tm == 0 and N % tn == 0 and K % tk == 0, (name, M, N, K, tm, tn, tk)
    nk = K // tk
    ne, no = len(extras), len(outs)
    deps = [] if dep is None else [dep]
    nd = len(deps)

    def body(a_ref, b_ref, *rest):
        extra_refs, out_refs = rest[:ne], rest[ne + nd:ne + nd + no]
        j = pl.program_id(1)
        if split is not None:
            kind, n = split
            assert nk == 1 and form != "tn"
            tiled = [t for _, _, *t in list(extras) + list(outs)]
            refs = list(extra_refs) + list(out_refs)
            for ci in range(n):
                if kind == "cols":
                    cs = slice(ci * (tn // n), (ci + 1) * (tn // n))
                    part = _dg(a_ref[...].astype(MXU_DTYPE), (b_ref[:, cs] if form == "nn" else b_ref[cs, :]).astype(MXU_DTYPE), ca, cb)
                    view = [r.at[:, cs] if t else r for r, t in zip(refs, tiled)]
                else:
                    rs = slice(ci * (tm // n), (ci + 1) * (tm // n))
                    part = _dg(a_ref[rs, :].astype(MXU_DTYPE), b_ref[...].astype(MXU_DTYPE), ca, cb)
                    view = [r.at[rs, :] if t else r for r, t in zip(refs, tiled)]
                epilogue(part, view[:ne], view[ne:], j, ci)
            return
        part = _dg(a_ref[...].astype(MXU_DTYPE), b_ref[...].astype(MXU_DTYPE), ca, cb)
        if nk == 1:
            epilogue(part, extra_refs, out_refs, j, 0)
            return
        acc_ref = rest[-1]
        k = pl.program_id(2)

        @pl.when(k == 0)
        def _():
            acc_ref[...] = part

        @pl.when(k > 0)
        def _():
            acc_ref[...] += part

        @pl.when(k == nk - 1)
        def _():
            epilogue(acc_ref[...], extra_refs, out_refs, j, 0)

    res = pl.pallas_call(
        body,
        name=name,
        grid=(M // tm, N // tn, nk),
        in_specs=[a_spec, b_spec] + [s for _, s, *_ in extras] + [pl.BlockSpec(memory_space=pl.ANY)] * nd,
        out_specs=[s for _, s, *_ in outs],
        out_shape=[o for o, *_ in outs],
        scratch_shapes=[pltpu.VMEM((tm, tn), F32)] if nk > 1 else [],
        compiler_params=_cparams(sem),
    )(a, b, *[e for e, *_ in extras], *deps)
    return res


def _ij_spec(tm, tn):
    return pl.BlockSpec((tm, tn), lambda i, j, k: (i, j))


def _store_epilogue(acc, extra_refs, out_refs, j, ci):
    out_refs[0][...] = acc.astype(out_refs[0].dtype)


def _plain_mm(name, a, b, form, out_dtype, tm, tn, tk):
    M = a.shape[1] if form == "tn" else a.shape[0]
    N = b.shape[0] if form == "nt" else b.shape[1]
    return _matmul(name, a, b, form, tm, tn, tk, [(_sds((M, N), out_dtype), _ij_spec(tm, tn))], _store_epilogue)[0]


def _class_slabs(S):
    assert DILATIONS[0] == 1
    return [(g, d, r, S // d) for g, d in enumerate(DILATIONS) if d > 1 for r in range(d)]


def _stack_classes(name, t, out_dtype):
    S, W = t.shape

    def body(x_ref, o_ref):
        o_ref[0:S, :] = x_ref[...].astype(out_dtype)
        for g, d, r, n in _class_slabs(S):
            o_ref[g * S + r * n:g * S + (r + 1) * n, :] = x_ref[pl.ds(r, n, stride=d), :].astype(out_dtype)

    return pl.pallas_call(
        body,
        name=name,
        grid=(W // LANES,),
        in_specs=[pl.BlockSpec((S, LANES), lambda j: (0, j))],
        out_specs=pl.BlockSpec((3 * S, LANES), lambda j: (0, j)),
        out_shape=_sds((3 * S, W), out_dtype),
        compiler_params=_cparams(("parallel",)),
    )(t)


def _rope_tables(seq):
    half = HEAD_DIM // 2
    inv = ROPE_THETA ** (-jnp.arange(half, dtype=F32) * (2.0 / HEAD_DIM))
    inv = jnp.tile(inv, LANES // half)
    pos = []
    for d in DILATIONS:
        row = jnp.arange(seq)
        pos.append((row % (seq // d)) * d + row // (seq // d))
    ang = jnp.concatenate(pos).astype(F32)[:, None] * inv[None, :]
    first = (jnp.arange(LANES) % HEAD_DIM) < half
    sin = jnp.sin(ang)
    return jnp.cos(ang), jnp.where(first[None, :], -sin, sin)


def _partner(x):
    half = HEAD_DIM // 2
    lane = lax.broadcasted_iota(jnp.int32, x.shape, 1)
    first = (lane % HEAD_DIM) < half
    return jnp.where(first, pltpu.roll(x, LANES - half, 1), pltpu.roll(x, half, 1))


def _attn_proj(x3, w_full, cos3, sin3, tm, tn):
    S3, D = x3.shape
    S = S3 // 3
    per_part = D // tn
    per_group = 3 * per_part

    def epilogue(acc, extra_refs, out_refs, j, ci):
        cos_ref, sin_ref = extra_refs
        o_ref = out_refs[0]
        is_rot = j // per_part < 2
        c = jnp.where(is_rot, cos_ref[...], 1.0)
        s = jnp.where(is_rot, sin_ref[...], 0.0)
        for t in range(acc.shape[1] // LANES):
            xs = acc[:, t * LANES:(t + 1) * LANES]
            o_ref[:, t * LANES:(t + 1) * LANES] = (xs * c + _partner(xs) * s).astype(o_ref.dtype)

    tab = pl.BlockSpec((tm, LANES), lambda i, j, k: (i, 0))
    return _matmul("attn_proj", x3, w_full, "nn", tm, tn, D, [(_sds((S3, 3 * D), MXU_DTYPE), _ij_spec(tm, tn), True)],
                   epilogue, extras=[(cos3, tab), (sin3, tab)],
                   b_map=lambda i, j, k: (k, j + (i // (S // tm)) * per_group), mnk=(S3, 3 * D, D),
                   split=("cols", tn // (2 * LANES)))[0]


def _head_sel(d_model):
    h = jnp.arange(LANES)[:, None]
    l = jnp.arange(d_model)[None, :]
    return (l // HEAD_DIM == h).astype(BF16)


def _class_edges(b, nblk):
    g = b // nblk
    per_class = jnp.where(g == 0, nblk // DILATIONS[0], jnp.where(g == 1, nblk // DILATIONS[1], nblk // DILATIONS[2]))
    pos = (b % nblk) % per_class
    return pos != 0, pos != per_class - 1


def _two_heads(t, top):
    zero = jnp.zeros_like(t)
    return jnp.concatenate([jnp.where(top, t, zero), jnp.where(top, zero, t)], axis=0)


def _band_mask(has_prev):
    B = ATTN_BLK
    row = lax.broadcasted_iota(jnp.int32, (2 * B, 2 * B), 0) % B
    col = lax.broadcasted_iota(jnp.int32, (2 * B, 2 * B), 1)
    in_prev = jnp.logical_and(jnp.logical_and(col < B, col >= row), has_prev)
    in_own = jnp.logical_and(col >= B, col - B <= row)
    return jnp.logical_or(in_prev, in_own)


def _attn_fwd(P3, D):
    S3 = P3.shape[0]
    B = ATTN_BLK
    nblk = S3 // 3 // B
    npairs = D // LANES
    scale = HEAD_DIM ** -0.5

    def body(q_ref, kc_ref, vc_ref, kp_ref, vp_ref, o_ref, lse_ref):
        has_prev, _ = _class_edges(pl.program_id(0), nblk)
        ok = _band_mask(has_prev)
        lane = lax.broadcasted_iota(jnp.int32, (B, LANES), 1)
        top = lane < HEAD_DIM
        lse_acc = jnp.zeros((B, LANES), F32)
        for j in range(npairs):
            sl = slice(j * LANES, (j + 1) * LANES)
            Q = _two_heads(q_ref[:, sl] * scale, top)
            K2 = jnp.concatenate([kp_ref[:, sl], kc_ref[:, sl]], axis=0)
            V2 = jnp.concatenate([vp_ref[:, sl], vc_ref[:, sl]], axis=0)
            s = jnp.where(ok, _nt(Q, K2), NEG)
            m = jnp.max(s, axis=1, keepdims=True)
            p = jnp.exp(s - m)
            l = jnp.sum(p, axis=1, keepdims=True)
            o = _nn((p * (1.0 / l)).astype(MXU_DTYPE), V2)
            o_ref[:, sl] = jnp.where(top, o[:B], o[B:])
            lse = m + jnp.log(l)
            lse_acc = jnp.where(lane == 2 * j, lse[:B], jnp.where(lane == 2 * j + 1, lse[B:], lse_acc))
        lse_ref[...] = lse_acc

    blk = lambda part, prev: pl.BlockSpec(
        (B, D), (lambda b: (jnp.maximum(b - 1, 0), part)) if prev else (lambda b: (b, part)))
    return pl.pallas_call(
        body,
        name="attn_fwd",
        grid=(3 * nblk,),
        in_specs=[blk(0, False), blk(1, False), blk(2, False), blk(1, True), blk(2, True)],
        out_specs=[pl.BlockSpec((B, D), lambda b: (b, 0)), pl.BlockSpec((B, LANES), lambda b: (b, 0))],
        out_shape=[_sds((S3, D), F32), _sds((S3, LANES), F32)],
        compiler_params=_cparams(("parallel",)),
    )(P3, P3, P3, P3, P3)


def _attn_mix(o3, lse3, sel):
    S3, D = o3.shape
    S = S3 // 3

    def body(o3_ref, lse_ref, sel_ref, o_ref, L_ref, w_ref):
        @pl.when(pl.program_id(0) == 0)
        def _():
            w_ref[0] = lse_ref[0:S, :]
            for g, d, r, n in _class_slabs(S):
                w_ref[g, pl.ds(r, n, stride=d), :] = lse_ref[g * S + r * n:g * S + (r + 1) * n, :]
            a, b, c = w_ref[0], w_ref[1], w_ref[2]
            m = jnp.maximum(jnp.maximum(a, b), c)
            L = m + jnp.log(jnp.exp(a - m) + jnp.exp(b - m) + jnp.exp(c - m))
            L_ref[...] = L
            w_ref[0] = jnp.exp(a - L)
            w_ref[1] = jnp.exp(b - L)
            w_ref[2] = jnp.exp(c - L)

        s = sel_ref[...]
        o_ref[...] = _exact_nn(w_ref[0], s) * o3_ref[0:S, :]
        for g, d, r, n in _class_slabs(S):
            rows = pl.ds(r, n, stride=d)
            o_ref[rows, :] += _exact_nn(w_ref[g, rows, :], s) * o3_ref[g * S + r * n:g * S + (r + 1) * n, :]

    return pl.pallas_call(
        body,
        name="attn_mix",
        grid=(D // LANES,),
        in_specs=[pl.BlockSpec((S3, LANES), lambda j: (0, j)), pl.BlockSpec((S3, LANES), lambda j: (0, 0)),
                  pl.BlockSpec((LANES, LANES), lambda j: (0, j))],
        out_specs=[pl.BlockSpec((S, LANES), lambda j: (0, j)), pl.BlockSpec((S, LANES), lambda j: (0, 0))],
        out_shape=[_sds((S, D), F32), _sds((S, LANES), F32)],
        scratch_shapes=[pltpu.VMEM((3, S, LANES), F32)],
        compiler_params=_cparams(("arbitrary",)),
    )(o3, lse3, sel)


def _attn_bwd(P3, do3, L3, delta3, cos3, sin3, D, dep):
    S3 = P3.shape[0]
    B = ATTN_BLK
    nblk = S3 // 3 // B
    npairs = D // LANES
    scale = HEAD_DIM ** -0.5

    def body(c_ref, kp_ref, vp_ref, qn_ref, doc_ref, don_ref, Lc_ref, Ln_ref, dc_ref, dn_ref, cos_ref, sin_ref, dep_ref, out_ref):
        has_prev, has_next = _class_edges(pl.program_id(0), nblk)
        ok = _band_mask(has_prev)
        row = lax.broadcasted_iota(jnp.int32, (2 * B, B), 0) % B
        col = lax.broadcasted_iota(jnp.int32, (2 * B, B), 1)
        ok_n = jnp.logical_and(col >= row, has_next)
        lane = lax.broadcasted_iota(jnp.int32, (B, LANES), 1)
        top = lane < HEAD_DIM
        cos_t = cos_ref[...]
        sin_inv = -sin_ref[...]
        Lc_all, Ln_all, dc_all, dn_all = Lc_ref[...], Ln_ref[...], dc_ref[...], dn_ref[...]
        pair_col = lambda t, j: jnp.concatenate([t[:, 2 * j:2 * j + 1], t[:, 2 * j + 1:2 * j + 2]], axis=0)
        for j in range(npairs):
            sl = lambda part: slice(part * D + j * LANES, part * D + (j + 1) * LANES)
            pj = slice(j * LANES, (j + 1) * LANES)
            kc2, vc2 = c_ref[:, sl(1)], c_ref[:, sl(2)]
            K2 = jnp.concatenate([kp_ref[:, pj], kc2], axis=0)
            V2 = jnp.concatenate([vp_ref[:, pj], vc2], axis=0)
            Qc = _two_heads(c_ref[:, sl(0)] * scale, top)
            Qn = _two_heads(qn_ref[:, pj] * scale, top)
            DOc = _two_heads(doc_ref[:, pj].astype(MXU_DTYPE), top)
            DOn = _two_heads(don_ref[:, pj].astype(MXU_DTYPE), top)
            P_c = jnp.where(ok, jnp.exp(_nt(Qc, K2) - pair_col(Lc_all, j)), 0.0)
            dS_c = P_c * (_nt(DOc, V2) - pair_col(dc_all, j))
            P_n = jnp.where(ok_n, jnp.exp(_nt(Qn, kc2) - pair_col(Ln_all, j)), 0.0)
            dS_n = P_n * (_nt(DOn, vc2) - pair_col(dn_all, j))
            dq = _nn(dS_c.astype(MXU_DTYPE), K2)
            dq2 = jnp.where(top, dq[:B], dq[B:]) * scale
            Qk = jnp.concatenate([Qc, Qn], axis=0)
            DOk = jnp.concatenate([DOc, DOn], axis=0)
            dk2 = _tn(jnp.concatenate([dS_c[:, B:], dS_n], axis=0).astype(MXU_DTYPE), Qk)
            dv2 = _tn(jnp.concatenate([P_c[:, B:], P_n], axis=0).astype(MXU_DTYPE), DOk)
            out_ref[:, sl(0)] = (dq2 * cos_t + _partner(dq2) * sin_inv).astype(out_ref.dtype)
            out_ref[:, sl(1)] = (dk2 * cos_t + _partner(dk2) * sin_inv).astype(out_ref.dtype)
            out_ref[:, sl(2)] = dv2.astype(out_ref.dtype)

    cur = lambda b: b
    prv = lambda b: jnp.maximum(b - 1, 0)
    nxt = lambda b: jnp.minimum(b + 1, 3 * nblk - 1)
    spec = lambda w, f, part=0: pl.BlockSpec((B, w), lambda b: (f(b), part))
    return pl.pallas_call(
        body,
        name="attn_bwd",
        grid=(3 * nblk,),
        in_specs=[spec(3 * D, cur), spec(D, prv, 1), spec(D, prv, 2), spec(D, nxt, 0), spec(D, cur), spec(D, nxt),
                  spec(LANES, cur), spec(LANES, nxt), spec(LANES, cur), spec(LANES, nxt), spec(LANES, cur), spec(LANES, cur),
                  pl.BlockSpec(memory_space=pl.ANY)],
        out_specs=spec(3 * D, cur),
        out_shape=_sds((S3, 3 * D), MXU_DTYPE),
        compiler_params=_cparams(("parallel",)),
    )(P3, P3, P3, P3, do3, do3, L3, L3, delta3, delta3, cos3, sin3, dep)


def _input_grad(du, dx3):
    S, D = du.shape

    def body(du_ref, dx_ref, o_ref):
        o_ref[...] = ALPHA * du_ref[...] + dx_ref[0:S, :]
        for g, d, r, n in _class_slabs(S):
            o_ref[pl.ds(r, n, stride=d), :] += dx_ref[g * S + r * n:g * S + (r + 1) * n, :]

    return pl.pallas_call(
        body,
        name="input_grad",
        grid=(D // LANES,),
        in_specs=[pl.BlockSpec((S, LANES), lambda j: (0, j)), pl.BlockSpec((3 * S, LANES), lambda j: (0, j))],
        out_specs=pl.BlockSpec((S, LANES), lambda j: (0, j)),
        out_shape=_sds((S, D), F32),
        compiler_params=_cparams(("parallel",)),
    )(du, dx3)


def _chunk_causal(tb):
    r = lax.broadcasted_iota(jnp.int32, (tb, tb), 0)
    c = lax.broadcasted_iota(jnp.int32, (tb, tb), 1)
    return jnp.logical_and((r // HGRN_CHUNK) == (c // HGRN_CHUNK), r >= c)


def _chunk_sums(a, lower):
    C = HGRN_CHUNK
    r = lax.broadcasted_iota(jnp.int32, (C, C), 0)
    c = lax.broadcasted_iota(jnp.int32, (C, C), 1)
    tri = ((r >= c) if lower else (r <= c)).astype(BF16)
    parts = _split3(a)
    out = []
    for ci in range(a.shape[0] // C):
        rows = slice(ci * C, (ci + 1) * C)
        out.append(_nn(tri, parts[0][rows]) + _nn(tri, parts[1][rows]) + _nn(tri, parts[2][rows]))
    return jnp.concatenate(out, axis=0)


def _chunk_last(b):
    C = HGRN_CHUNK
    return jnp.concatenate([jnp.broadcast_to(b[(ci + 1) * C - 1:(ci + 1) * C, :], (C, b.shape[1]))
                            for ci in range(b.shape[0] // C)], axis=0)


def _lower_bound(lb_ref):
    l0, l1 = lb_ref[0:1, :], lb_ref[1:2, :]
    m = jnp.maximum(l0, l1)
    e0, e1 = jnp.exp(l0 - m), jnp.exp(l1 - m)
    return e1 / (e0 + e1)


def _hgrn_gates(q_raw, z, lb):
    sg = 1.0 / (1.0 + jnp.exp(-z))
    sn = 1.0 / (1.0 + jnp.exp(z))
    f = lb + (1.0 - lb) * sg
    key = (1.0 - lb) * sn
    sq = 1.0 / (1.0 + jnp.exp(-q_raw))
    return sg, sn, f, key, sq


HGRN_HEADS_PER_STEP = 2


def _hgrn_fwd(P1, lb_logits, norm_g, tb):
    S = P1.shape[0]
    D = P1.shape[1] // 3
    K = HGRN_DK
    H = D // K
    HP = HGRN_HEADS_PER_STEP
    C = HGRN_CHUNK
    cpb = tb // C
    nt = S // tb

    def body(q_ref, f_ref, i_ref, lb_ref, g_ref, o_ref, n_ref, st_ref, state):
        t = pl.program_id(1)

        @pl.when(t == 0)
        def _():
            state[...] = jnp.zeros_like(state)

        lb_all = _lower_bound(lb_ref)
        low = _chunk_causal(tb)
        for hh in range(HP):
            lanes = slice(hh * K, (hh + 1) * K)
            q_raw, z, v = q_ref[:, lanes], f_ref[:, lanes], i_ref[:, lanes]
            sg, sn, f, key, sq = _hgrn_gates(q_raw, z, lb_all[:, lanes])
            b = _chunk_sums(jnp.log(f), lower=True)
            qd = (q_raw * sq * jnp.exp(b)).astype(MXU_DTYPE)
            kd = (key * jnp.exp(-b)).astype(MXU_DTYPE)
            kb = (key * jnp.exp(_chunk_last(b) - b)).astype(MXU_DTYPE)
            vm = v.astype(MXU_DTYPE)
            a = jnp.where(low, _nt(qd, kd), 0.0).astype(MXU_DTYPE)
            o_intra = _nn(a, vm)
            st = state[hh]
            outs = []
            for ci in range(cpb):
                rows = slice(ci * C, (ci + 1) * C)
                st_ref[hh, ci] = st
                outs.append(o_intra[rows] + _nt(qd[rows], st.astype(MXU_DTYPE)))
                st = st * jnp.exp(b[(ci + 1) * C - 1:(ci + 1) * C, :]) + _tn(vm[rows], kb[rows])
            state[hh] = st
            o = jnp.concatenate(outs, axis=0)
            o_ref[:, lanes] = o
            rs = lax.rsqrt(jnp.mean(o * o, axis=1, keepdims=True) + RMS_EPS)
            n_ref[:, lanes] = o * rs * g_ref[:, lanes]

    tok = lambda part: pl.BlockSpec((tb, HP * K), lambda h, t: (t, part * (H // HP) + h))
    vec = lambda rows: pl.BlockSpec((rows, HP * K), lambda h, t: (0, h))
    return pl.pallas_call(
        body,
        name="hgrn_fwd",
        grid=(H // HP, nt),
        in_specs=[tok(0), tok(1), tok(2), vec(2), vec(1)],
        out_specs=[tok(0), tok(0), pl.BlockSpec((HP, cpb, K, K), lambda h, t: (h, t, 0, 0))],
        out_shape=[_sds((S, D), F32), _sds((S, D), F32), _sds((H, S // C, K, K), F32)],
        scratch_shapes=[pltpu.VMEM((HP, K, K), F32)],
        compiler_params=_cparams(("parallel", "arbitrary")),
    )(P1, P1, P1, lb_logits, norm_g)


def _hgrn_bwd(P1, o_pre, states, dn, lb_logits, norm_g, tb):
    S = P1.shape[0]
    D = P1.shape[1] // 3
    K = HGRN_DK
    H = D // K
    HP = HGRN_HEADS_PER_STEP
    C = HGRN_CHUNK
    cpb = tb // C
    nt = S // tb

    def body(q_ref, f_ref, i_ref, o_ref, st_ref, dn_ref, lb_ref, g_ref, d_ref, dg_ref, dlb_ref, dstate):
        t = pl.program_id(1)

        @pl.when(t == 0)
        def _():
            dstate[...] = jnp.zeros_like(dstate)
            dg_ref[...] = jnp.zeros_like(dg_ref)
            dlb_ref[...] = jnp.zeros_like(dlb_ref)

        lb_all = _lower_bound(lb_ref)
        low = _chunk_causal(tb)
        for hh in range(HP):
            lanes = slice(hh * K, (hh + 1) * K)
            lb = lb_all[:, lanes]
            gn = g_ref[:, lanes]
            q_raw, z, v = q_ref[:, lanes], f_ref[:, lanes], i_ref[:, lanes]
            sg, sn, f, key, sq = _hgrn_gates(q_raw, z, lb)
            b = _chunk_sums(jnp.log(f), lower=True)
            e_pos, e_neg, e_rel = jnp.exp(b), jnp.exp(-b), jnp.exp(_chunk_last(b) - b)
            qd_f, kd_f, kb_f = q_raw * sq * e_pos, key * e_neg, key * e_rel
            qd, kd, kb = qd_f.astype(MXU_DTYPE), kd_f.astype(MXU_DTYPE), kb_f.astype(MXU_DTYPE)
            vm = v.astype(MXU_DTYPE)
            a = jnp.where(low, _nt(qd, kd), 0.0).astype(MXU_DTYPE)
            o = o_ref[:, lanes]
            dnn = dn_ref[:, lanes]
            rs = lax.rsqrt(jnp.mean(o * o, axis=1, keepdims=True) + RMS_EPS)
            dg_ref[:, lanes] += jnp.sum(dnn * o * rs, axis=0, keepdims=True)
            tg = dnn * gn
            dom = (rs * tg - o * (rs * rs * rs) * jnp.mean(tg * o, axis=1, keepdims=True)).astype(MXU_DTYPE)
            da = jnp.where(low, _nt(dom, vm), 0.0).astype(MXU_DTYPE)
            dv = _tn(a, dom)
            dqd = _nn(da, kd)
            dkd = _tn(da, qd)
            dst = dstate[hh]
            dv_s, dqd_s, dkb_s, dbl_s = [None] * cpb, [None] * cpb, [None] * cpb, [None] * cpb
            for ci in reversed(range(cpb)):
                rows = slice(ci * C, (ci + 1) * C)
                st = st_ref[hh, ci]
                dstm = dst.astype(MXU_DTYPE)
                dec = jnp.exp(b[(ci + 1) * C - 1:(ci + 1) * C, :])
                dv_s[ci] = _nt(kb[rows], dstm)
                dkb_s[ci] = _nn(vm[rows], dstm)
                dqd_s[ci] = _nn(dom[rows], st.astype(MXU_DTYPE))
                db_last = jnp.sum(dkb_s[ci] * kb_f[rows], axis=0, keepdims=True) + jnp.sum(dst * st, axis=0, keepdims=True) * dec
                dbl_s[ci] = jnp.broadcast_to(db_last, (C, K))
                dst = dst * dec + _tn(dom[rows], qd[rows])
            dstate[hh] = dst
            dv = dv + jnp.concatenate(dv_s, axis=0)
            dqd = dqd + jnp.concatenate(dqd_s, axis=0)
            dkb = jnp.concatenate(dkb_s, axis=0)
            dkey = dkd * e_neg + dkb * e_rel
            db = dqd * qd_f - dkd * kd_f - dkb * kb_f
            dlogf = _chunk_sums(db, lower=False) + jnp.concatenate(dbl_s, axis=0)
            gz = (1.0 - lb) * sg * sn
            d_ref[0, :, lanes] = (dqd * e_pos * (sq + q_raw * sq * (1.0 - sq))).astype(d_ref.dtype)
            d_ref[1, :, lanes] = (dlogf * gz / f - dkey * gz).astype(d_ref.dtype)
            d_ref[2, :, lanes] = dv.astype(d_ref.dtype)
            dlb_ref[:, lanes] += jnp.sum(dlogf * sn / f - dkey * sn, axis=0, keepdims=True)

    rev = lambda t: nt - 1 - t
    tok = lambda part: pl.BlockSpec((tb, HP * K), lambda h, t: (rev(t), part * (H // HP) + h))
    vec = lambda rows: pl.BlockSpec((rows, HP * K), lambda h, t: (0, h))
    outs = pl.pallas_call(
        body,
        name="hgrn_bwd",
        grid=(H // HP, nt),
        in_specs=[tok(0), tok(1), tok(2), tok(0),
                  pl.BlockSpec((HP, cpb, K, K), lambda h, t: (h, rev(t), 0, 0)),
                  tok(0), vec(2), vec(1)],
        out_specs=[pl.BlockSpec((3, tb, HP * K), lambda h, t: (0, rev(t), h)), vec(1), vec(1)],
        out_shape=[_sds((3, S, D), MXU_DTYPE)] + [_sds((1, D), F32)] * 2,
        scratch_shapes=[pltpu.VMEM((HP, K, K), F32)],
        compiler_params=_cparams(("parallel", "arbitrary")),
    )(P1, P1, P1, o_pre, states, dn, lb_logits, norm_g)
    return outs


def _lb_logits_grad(dlb, lb_logits):
    def body(d_ref, l_ref, o_ref):
        s1 = _lower_bound(l_ref)
        d = d_ref[...]
        o_ref[0:1, :] = -(1.0 - s1) * s1 * d
        o_ref[1:2, :] = s1 * (1.0 - s1) * d

    return pl.pallas_call(body, name="lb_logits_grad", out_shape=_sds(lb_logits.shape, F32))(dlb, lb_logits)


def _ln_epilogue(acc, extra_refs, out_refs, j, ci):
    res_ref, g_ref, b_ref = extra_refs
    x_ref, xm_ref, xhat_ref, rstd_ref = out_refs
    u = ALPHA * res_ref[...] + acc
    mu = jnp.mean(u, axis=1, keepdims=True)
    cen = u - mu
    rstd = lax.rsqrt(jnp.mean(cen * cen, axis=1, keepdims=True) + LN_EPS)
    xhat = cen * rstd
    xhat_ref[...] = xhat
    x = xhat * g_ref[...] + b_ref[...]
    x_ref[...] = x
    xm_ref[...] = x.astype(xm_ref.dtype)
    rstd_ref[...] = rstd


def _mm_res_ln(name, a, w_full, res, g, b, tm, tk):
    S, D = res.shape
    row = pl.BlockSpec((tm, D), lambda i, j, k: (i, 0))
    vec = pl.BlockSpec((1, D), lambda i, j, k: (0, 0))
    outs = [(_sds((S, D), F32), row, True), (_sds((S, D), MXU_DTYPE), row, True), (_sds((S, D), F32), row, True),
            (_sds((S, 1), F32), pl.BlockSpec((tm, 1), lambda i, j, k: (i, 0)), True)]
    return _matmul(name, a, w_full, "nn", tm, D, tk, outs, _ln_epilogue, extras=[(res, row, True), (g, vec), (b, vec)],
                   split=("rows", 2) if tk == a.shape[1] else None)


def _ln_bwd_rows(dy, xh, rstd, g, first, du_ref, dum_ref, dg_ref, db_ref):
    if first is not None:
        @pl.when(first)
        def _():
            dg_ref[...] = jnp.zeros_like(dg_ref)
            db_ref[...] = jnp.zeros_like(db_ref)

    dg_ref[...] += jnp.sum(dy * xh, axis=0, keepdims=True)
    db_ref[...] += jnp.sum(dy, axis=0, keepdims=True)
    dxh = dy * g
    m1 = jnp.mean(dxh, axis=1, keepdims=True)
    m2 = jnp.mean(dxh * xh, axis=1, keepdims=True)
    du = rstd * (dxh - m1 - xh * m2)
    du_ref[...] = du
    dum_ref[...] = du.astype(dum_ref.dtype)


def _loss_ln_bwd(y, target, xhat, rstd, g, tm):
    S, D = y.shape

    def body(y_ref, t_ref, xh_ref, r_ref, g_ref, sq_ref, du_ref, dum_ref, dg_ref, db_ref):
        first = pl.program_id(0) == 0

        @pl.when(first)
        def _():
            sq_ref[...] = jnp.zeros_like(sq_ref)

        e = y_ref[...] - t_ref[...]
        sq_ref[...] += jnp.sum(e * e, axis=0, keepdims=True)
        _ln_bwd_rows(e / D, xh_ref[...], r_ref[...], g_ref[...], first, du_ref, dum_ref, dg_ref, db_ref)

    row = pl.BlockSpec((tm, D), lambda i: (i, 0))
    vec = pl.BlockSpec((1, D), lambda i: (0, 0))
    return pl.pallas_call(
        body,
        name="loss_ln_bwd",
        grid=(S // tm,),
        in_specs=[row, row, row, pl.BlockSpec((tm, 1), lambda i: (i, 0)), vec],
        out_specs=[vec, row, row, vec, vec],
        out_shape=[_sds((1, D), F32), _sds((S, D), F32), _sds((S, D), MXU_DTYPE), _sds((1, D), F32), _sds((1, D), F32)],
        compiler_params=_cparams(("arbitrary",)),
    )(y, target, xhat, rstd, g)


def _mlp_up(name, x, w_up, tm, tn, tk):
    S = x.shape[0]
    F = w_up.shape[1]

    def epilogue(acc, extra_refs, out_refs, j, ci):
        r = jnp.maximum(acc, 0.0)
        out_refs[0][...] = (r * r).astype(out_refs[0].dtype)

    return _matmul(name, x, w_up, "nn", tm, tn, tk, [(_sds((S, F), MXU_DTYPE), _ij_spec(tm, tn), True)], epilogue,
                   split=("cols", 2))[0]


def _mlp_down_bwd(name, dy, w_down, a, tm, tn, tk):
    S, F = a.shape

    def epilogue(acc, extra_refs, out_refs, j, ci):
        out_refs[0][...] = (acc * (2.0 * jnp.sqrt(extra_refs[0][...].astype(F32)))).astype(out_refs[0].dtype)

    return _matmul(name, dy, w_down, "nt", tm, tn, tk, [(_sds((S, F), MXU_DTYPE), _ij_spec(tm, tn), True)], epilogue,
                   extras=[(a, _ij_spec(tm, tn), True)], split=("cols", 2))[0]


def _mm_nt_res_ln_bwd(name, dy, w, du, xhat, rstd, g, tm, tk, dep, a_map=None, mk=None):
    S, D = du.shape

    def epilogue(acc, extra_refs, out_refs, j, ci):
        du_ref, xh_ref, r_ref, g_ref = extra_refs
        first = (pl.program_id(0) == 0) if ci == 0 else None
        _ln_bwd_rows(ALPHA * du_ref[...] + acc, xh_ref[...], r_ref[...], g_ref[...], first, *out_refs)

    row = pl.BlockSpec((tm, D), lambda i, j, k: (i, 0))
    vec = pl.BlockSpec((1, D), lambda i, j, k: (0, 0))
    return _matmul(name, dy, w, "nt", tm, D, tk,
                   [(_sds((S, D), F32), row, True), (_sds((S, D), MXU_DTYPE), row, True), (_sds((1, D), F32), vec),
                    (_sds((1, D), F32), vec)], epilogue,
                   extras=[(du, row, True), (xhat, row, True), (rstd, pl.BlockSpec((tm, 1), lambda i, j, k: (i, 0)), True), (g, vec)],
                   a_map=a_map, mnk=None if mk is None else (S, D, mk), dep=dep, sem=("arbitrary", "arbitrary", "arbitrary"),
                   split=("rows", 2) if mk is None else None)


def _attn_out_bwd(du, w_out, o, sel_t, tm, tk):
    S, D = o.shape

    def epilogue(acc, extra_refs, out_refs, j, ci):
        out_refs[0][...] = acc
        out_refs[1][...] = _exact_nn(acc * extra_refs[0][...], extra_refs[1][...])

    row = pl.BlockSpec((tm, D), lambda i, j, k: (i, 0))
    slim = pl.BlockSpec((tm, LANES), lambda i, j, k: (i, 0))
    return _matmul("attn_out_bwd", du, w_out, "nt", tm, D, tk,
                   [(_sds((S, D), F32), row, True), (_sds((S, LANES), F32), slim, True)], epilogue,
                   extras=[(o, row, True), (sel_t, pl.BlockSpec((D, LANES), lambda i, j, k: (0, 0)))], split=("rows", 2))


def _adamw(name, w, gs, m, v):
    shape = w.shape
    cols = shape[-1]
    rows = math.prod(shape[:-1])
    w2, m2, v2 = (t.reshape(rows, cols) for t in (w, m, v))
    gs2 = [g.reshape(-1, cols) for g in gs]
    ng = len(gs2)
    tr = _pick(rows // ng, (256, 128, 64, 32, 16, 8))
    per = rows // ng // tr
    c1 = 1.0 - ADAM_B1 ** ADAM_STEP
    c2 = 1.0 - ADAM_B2 ** ADAM_STEP

    def body(w_ref, m_ref, v_ref, *rest):
        g_refs, (d_ref, nm_ref, nv_ref), g_out = rest[:ng], rest[ng:ng + 3], rest[ng + 3:]
        gg = g_refs[0][...]
        if ng == 2:
            gg = jnp.where(pl.program_id(0) < per, gg, g_refs[1][...])
            g_out[0][...] = gg
        nm = ADAM_B1 * m_ref[...] + (1.0 - ADAM_B1) * gg
        nv = ADAM_B2 * v_ref[...] + (1.0 - ADAM_B2) * (gg * gg)
        nm_ref[...] = nm
        nv_ref[...] = nv
        d_ref[...] = -ADAM_LR * ((nm / c1) / (jnp.sqrt(nv / c2) + ADAM_EPS) + ADAM_WD * w_ref[...])

    blk = pl.BlockSpec((tr, cols), lambda i: (i, 0))
    g_specs = [blk] if ng == 1 else [pl.BlockSpec((tr, cols), lambda i: (jnp.minimum(i, per - 1), 0)),
                                     pl.BlockSpec((tr, cols), lambda i: (jnp.maximum(i - per, 0), 0))]
    nout = 3 if ng == 1 else 4
    outs = pl.pallas_call(
        body,
        name=name,
        grid=(rows // tr,),
        in_specs=[blk] * 3 + g_specs,
        out_specs=[blk] * nout,
        out_shape=[_sds((rows, cols), F32)] * nout,
        compiler_params=_cparams(("parallel",)),
    )(w2, m2, v2, *gs2)
    g_full = outs[3] if ng == 2 else gs2[0]
    return tuple(o.reshape(shape) for o in (outs[0], outs[1], outs[2], g_full))


HBM = pl.BlockSpec(memory_space=pl.ANY)


def _shard_slice(ref, axis, size, index):
    idx = [slice(None)] * len(ref.shape)
    idx[axis] = pl.ds(pl.multiple_of(index * size, 8), size)
    return ref.at[tuple(idx)]


def _share_halves(name, full, tr):
    R, W4 = full.shape
    W, h = W4 // 4, R // 2
    steps = [(k, t) for k in range(3) for t in range(h // tr)]

    def body(f_in, f_ref, buf, lsem, ssem, rsem):
        x, y, c = lax.axis_index("x"), lax.axis_index("y"), lax.axis_index("c")
        sibling = (x, y, 1 - c)
        chips = [(1 - x, y), (x, 1 - y), (1 - x, 1 - y)]

        def tile(k, t):
            px, py = chips[k]
            return f_ref.at[pl.ds(pl.multiple_of(c * h + t * tr, 8), tr), pl.ds(pl.multiple_of((2 * px + py) * W, LANES), W)]

        sends = []
        for s, (k, t) in enumerate(steps):
            slot = s % 2
            if s >= 2:
                sends[s - 2].wait_send()
            lc = pltpu.make_async_copy(tile(k, t), buf.at[slot], lsem.at[slot])
            lc.start()
            lc.wait()
            rc = pltpu.make_async_remote_copy(src_ref=buf.at[slot], dst_ref=tile(k, t), send_sem=ssem.at[slot], recv_sem=rsem,
                                              device_id=sibling, device_id_type=MESH)
            rc.start()
            sends.append(rc)
        for rc in sends[-2:]:
            rc.wait_send()
        whole = f_ref.at[pl.ds(0, h), pl.ds(0, 3 * W)]
        pltpu.make_async_remote_copy(src_ref=whole, dst_ref=whole, send_sem=ssem.at[0], recv_sem=rsem,
                                     device_id=sibling, device_id_type=MESH).wait_recv()

    return pl.pallas_call(
        body,
        name=name,
        in_specs=[HBM],
        out_specs=HBM,
        out_shape=_sds(full.shape, full.dtype),
        input_output_aliases={0: 0},
        scratch_shapes=[pltpu.VMEM((2, tr, W), full.dtype), pltpu.SemaphoreType.DMA((2,)), pltpu.SemaphoreType.DMA((2,)),
                        pltpu.SemaphoreType.DMA(())],
    )(full)


IN_HBM = pl.BlockSpec(memory_space=pltpu.HBM)
IN_SEM = pl.BlockSpec(memory_space=pltpu.SEMAPHORE)
DATAFLOW = pltpu.SideEffectType.DATAFLOW_SIDE_EFFECTING


def _hbm(t):
    return pltpu.with_memory_space_constraint(t, pltpu.HBM)


def _token_spec():
    return pl.BlockSpec(memory_space=pltpu.VMEM)


def _gather_copies(s_refs, f_refs, axes, halves, send, recv, loc, arrival):
    x, y, c = lax.axis_index("x"), lax.axis_index("y"), lax.axis_index("c")
    chips = [(1 - x, y), (x, 1 - y), (1 - x, 1 - y)]
    local, remote = [], []
    for a in range(len(s_refs)):
        size = s_refs[a].shape[axes[a]]
        local.append(pltpu.make_async_copy(s_refs[a], _shard_slice(f_refs[a], axes[a], size, 2 * x + y), loc.at[a]))
        for k, (px, py) in enumerate(chips):
            block = (2 * px + py) if arrival else (2 * x + y)
            src, dst = s_refs[a], _shard_slice(f_refs[a], axes[a], size, block)
            if halves:
                assert axes[a] == 1 and len(s_refs[a].shape) == 2
                h = s_refs[a].shape[0] // 2
                rows = pl.ds(pl.multiple_of(c * h, 8), h)
                src = s_refs[a].at[rows, :]
                dst = f_refs[a].at[rows, pl.ds(pl.multiple_of(block * size, LANES), size)]
            remote.append(pltpu.make_async_remote_copy(src_ref=src, dst_ref=dst, send_sem=send.at[3 * a + k],
                                                       recv_sem=recv.at[3 * a + k], device_id=(px, py, c), device_id_type=MESH))
    return local, remote


def _gather_start(name, shards, axes, after, halves=False):
    n = len(shards)
    fulls = []
    for s, ax in zip(shards, axes):
        fs = list(s.shape)
        fs[ax] *= 4
        fulls.append(lax.empty(tuple(fs), s.dtype))

    def body(*refs):
        s_refs, f_refs = refs[:n], refs[n:2 * n]
        send, recv, loc, token = refs[2 * n + 1], refs[2 * n + 2], refs[2 * n + 3], refs[-1]
        local, remote = _gather_copies(s_refs, f_refs, axes, halves, send, recv, loc, arrival=False)
        for cp in remote + local:
            cp.start()
        token[...] = jnp.zeros_like(token)

    outs = pl.pallas_call(
        body,
        name=name,
        out_shape=(pltpu.SemaphoreType.DMA((3 * n,)), pltpu.SemaphoreType.DMA((3 * n,)), pltpu.SemaphoreType.DMA((n,)),
                   *[pltpu.HBM(t.shape, t.dtype) for t in shards + fulls], _sds((8, LANES), F32)),
        in_specs=[IN_HBM] * (2 * n) + [HBM],
        out_specs=(IN_SEM, IN_SEM, IN_SEM, *[IN_HBM] * (2 * n), _token_spec()),
        input_output_aliases={i: 3 + i for i in range(2 * n)},
        compiler_params=pltpu.CompilerParams(has_side_effects=DATAFLOW),
    )(*[_hbm(t) for t in shards + fulls], after)
    return (outs[0], outs[1], outs[2], list(outs[3:3 + n]), list(outs[3 + n:3 + 2 * n]), axes, halves), outs[-1]


def _gather_wait(name, state, *after):
    send, recv, loc, s_thru, f_thru, axes, halves = state
    n = len(s_thru)

    def body(*refs):
        s_refs, f_refs = refs[:n], refs[n:2 * n]
        local, remote = _gather_copies(s_refs, f_refs, axes, halves, refs[2 * n], refs[2 * n + 1], refs[2 * n + 2], arrival=True)
        for cp in local:
            cp.wait()
        for cp in remote:
            cp.wait_send()
            cp.wait_recv()

    outs = pl.pallas_call(
        body,
        name=name,
        out_shape=tuple(pltpu.HBM(t.shape, t.dtype) for t in s_thru + f_thru),
        in_specs=[IN_HBM] * (2 * n) + [IN_SEM, IN_SEM, IN_SEM] + [HBM] * len(after),
        out_specs=tuple([IN_HBM] * (2 * n)),
        input_output_aliases={i: i for i in range(2 * n)},
        compiler_params=pltpu.CompilerParams(has_side_effects=DATAFLOW),
    )(*s_thru, *f_thru, send, recv, loc, *after)
    return list(outs[n:2 * n])


FLIPS = [(fx, fy, fc) for fx in (0, 1) for fy in (0, 1) for fc in (0, 1)][1:]


def _piece_shape(shape, axis):
    ps = list(shape)
    if axis == 0:
        ps[0] //= 8
    else:
        ps[0] //= 2
        ps[axis] //= 4
    return tuple(ps)


def _piece(ref, axis, q, c):
    shape = ref.shape
    idx = [slice(None)] * len(shape)
    if axis == 0:
        h = shape[0] // 8
        idx[0] = pl.ds(pl.multiple_of((2 * q + c) * h, 8), h)
    else:
        h, w = shape[0] // 2, shape[axis] // 4
        idx[0] = pl.ds(c * h, h)
        idx[axis] = pl.ds(pl.multiple_of(q * w, LANES if axis == len(shape) - 1 else 8), w)
    return ref.at[tuple(idx)]


def _own_piece(g, axis):
    ps = _piece_shape(g.shape, axis)
    q, c = 2 * lax.axis_index("x") + lax.axis_index("y"), lax.axis_index("c")
    start = [0] * len(ps)
    if axis == 0:
        start[0] = (2 * q + c) * ps[0]
    else:
        start[0] = c * ps[0]
        start[axis] = q * ps[axis]
    return lax.dynamic_slice(g, start, ps)


def _scatter_copies(g_refs, l_refs, axes, send, recv):
    x, y, c = lax.axis_index("x"), lax.axis_index("y"), lax.axis_index("c")
    out = []
    for a in range(len(g_refs)):
        for k, (fx, fy, fc) in enumerate(FLIPS):
            tx, ty, tc = x ^ fx, y ^ fy, c ^ fc
            out.append(pltpu.make_async_remote_copy(
                src_ref=_piece(g_refs[a], axes[a], 2 * tx + ty, tc), dst_ref=l_refs[a].at[k],
                send_sem=send.at[7 * a + k], recv_sem=recv.at[7 * a + k], device_id=(tx, ty, tc), device_id_type=MESH))
    return out


def _scatter_start(name, grads, axes):
    n = len(grads)
    lands = [lax.empty((7,) + _piece_shape(g.shape, ax), g.dtype) for g, ax in zip(grads, axes)]

    def body(*refs):
        g_refs, l_refs = refs[:n], refs[n:2 * n]
        send, recv, token = refs[2 * n], refs[2 * n + 1], refs[-1]
        for cp in _scatter_copies(g_refs, l_refs, axes, send, recv):
            cp.start()
        token[...] = jnp.zeros_like(token)

    outs = pl.pallas_call(
        body,
        name=name,
        out_shape=(pltpu.SemaphoreType.DMA((7 * n,)), pltpu.SemaphoreType.DMA((7 * n,)),
                   *[pltpu.HBM(t.shape, t.dtype) for t in grads + lands], _sds((8, LANES), F32)),
        in_specs=[IN_HBM] * (2 * n),
        out_specs=(IN_SEM, IN_SEM, *[IN_HBM] * (2 * n), _token_spec()),
        input_output_aliases={i: 2 + i for i in range(2 * n)},
        compiler_params=pltpu.CompilerParams(has_side_effects=DATAFLOW),
    )(*[_hbm(t) for t in grads + lands])
    return (outs[0], outs[1], list(outs[2:2 + n]), list(outs[2 + n:2 + 2 * n]), axes), outs[-1]


def _scatter_wait(name, state, *after):
    send, recv, g_thru, l_thru, axes = state
    n = len(g_thru)

    def body(*refs):
        g_refs, l_refs = refs[:n], refs[n:2 * n]
        for cp in _scatter_copies(g_refs, l_refs, axes, refs[2 * n], refs[2 * n + 1]):
            cp.wait_send()
            cp.wait_recv()

    outs = pl.pallas_call(
        body,
        name=name,
        out_shape=tuple(pltpu.HBM(t.shape, t.dtype) for t in g_thru + l_thru),
        in_specs=[IN_HBM] * (2 * n) + [IN_SEM, IN_SEM] + [HBM] * len(after),
        out_specs=tuple([IN_HBM] * (2 * n)),
        input_output_aliases={i: i for i in range(2 * n)},
        compiler_params=pltpu.CompilerParams(has_side_effects=DATAFLOW),
    )(*g_thru, *l_thru, send, recv, *after)
    return list(outs[:n]), list(outs[n:2 * n])


def _reduce_join(name, landing, own):
    piece = own.shape
    C = piece[-1]
    R = math.prod(piece[:-1])
    l3 = landing.reshape(7, R, C)
    own2 = own.reshape(R, C)
    tr = _pick(R, [t for t in (512, 256, 128, 64, 32, 16, 8) if t * C <= 256 * 1024])
    nsteps = R // tr

    def body(own_ref, l_ref, o_ref, buf, send, loc, recv):
        i = pl.program_id(0)
        x, y, c = lax.axis_index("x"), lax.axis_index("y"), lax.axis_index("c")
        sibling = (x, y, 1 - c)

        def copies(slot, step):
            dst = o_ref.at[pl.ds(pl.multiple_of(c * R + step * tr, 8), tr), :]
            return (pltpu.make_async_copy(buf.at[slot], dst, loc.at[slot]),
                    pltpu.make_async_remote_copy(src_ref=buf.at[slot], dst_ref=dst, send_sem=send.at[slot], recv_sem=recv,
                                                 device_id=sibling, device_id_type=MESH))

        @pl.when(i >= 2)
        def _():
            lc, rc = copies(i % 2, i - 2)
            lc.wait()
            rc.wait_send()

        acc = own_ref[...].astype(F32)
        for s in range(7):
            acc = acc + l_ref[s].astype(F32)
        buf[i % 2] = acc
        lc, rc = copies(i % 2, i)
        lc.start()
        rc.start()

        @pl.when(i == nsteps - 1)
        def _():
            for st in range(max(nsteps - 2, 0), nsteps):
                lc, rc = copies(st % 2, st)
                lc.wait()
                rc.wait_send()
            theirs = o_ref.at[pl.ds(pl.multiple_of((1 - c) * R, 8), R), :]
            pltpu.make_async_remote_copy(src_ref=theirs, dst_ref=theirs, send_sem=send.at[0], recv_sem=recv,
                                         device_id=sibling, device_id_type=MESH).wait_recv()

    return pl.pallas_call(
        body,
        name=name,
        grid=(nsteps,),
        in_specs=[pl.BlockSpec((tr, C), lambda i: (i, 0)), pl.BlockSpec((7, tr, C), lambda i: (0, i, 0))],
        out_specs=HBM,
        out_shape=_sds((2 * R, C), F32),
        scratch_shapes=[pltpu.VMEM((2, tr, C), F32), pltpu.SemaphoreType.DMA((2,)), pltpu.SemaphoreType.DMA((2,)),
                        pltpu.SemaphoreType.DMA(())],
        compiler_params=_cparams(("arbitrary",)),
    )(own2, l3)


def _all_reduce_small(v, dep):
    R, D = v.shape

    def body(v_ref, dep_ref, o_ref, land, send, recv):
        x, y, c = lax.axis_index("x"), lax.axis_index("y"), lax.axis_index("c")
        my_slot = 4 * x + 2 * y + c
        land[my_slot] = v_ref[...]
        for k, (fx, fy, fc) in enumerate(FLIPS):
            tx, ty, tc = x ^ fx, y ^ fy, c ^ fc
            pltpu.make_async_remote_copy(src_ref=v_ref, dst_ref=land.at[my_slot], send_sem=send.at[k], recv_sem=recv.at[k],
                                         device_id=(tx, ty, tc), device_id_type=MESH).start()
        for k, (fx, fy, fc) in enumerate(FLIPS):
            tx, ty, tc = x ^ fx, y ^ fy, c ^ fc
            cp = pltpu.make_async_remote_copy(src_ref=v_ref, dst_ref=land.at[4 * tx + 2 * ty + tc], send_sem=send.at[k],
                                              recv_sem=recv.at[k], device_id=(tx, ty, tc), device_id_type=MESH)
            cp.wait_send()
            cp.wait_recv()
        acc = land[0]
        for s in range(1, 8):
            acc = acc + land[s]
        o_ref[...] = acc

    return pl.pallas_call(
        body,
        name="all_reduce_small",
        in_specs=[pl.BlockSpec(memory_space=pltpu.VMEM), pl.BlockSpec(memory_space=pl.ANY)],
        out_specs=pl.BlockSpec(memory_space=pltpu.VMEM),
        out_shape=_sds((R, D), F32),
        scratch_shapes=[pltpu.VMEM((8, R, D), F32), pltpu.SemaphoreType.DMA((7,)), pltpu.SemaphoreType.DMA((7,))],
    )(v, dep)


def kernel(x, attn_w_in, attn_w_out, hgrn_w_in, hgrn_w_out, hgrn_norm_g, lb_logits, ln_mix_g, ln_mix_b, ln_ffn_g, ln_ffn_b, ffn_w_up, ffn_w_down, loss_target, m_attn_w_in, m_attn_w_out, m_hgrn_w_in, m_hgrn_w_out, m_hgrn_norm_g, m_lb_logits, m_ln_mix_g, m_ln_mix_b, m_ln_ffn_g, m_ln_ffn_b, m_ffn_w_up, m_ffn_w_down, v_attn_w_in, v_attn_w_out, v_hgrn_w_in, v_hgrn_w_out, v_hgrn_norm_g, v_lb_logits, v_ln_mix_g, v_ln_mix_b, v_ln_ffn_g, v_ln_ffn_b, v_ffn_w_up, v_ffn_w_down):
    xs = x[0]
    tgt = loss_target[0]
    S, D = xs.shape
    F = ffn_w_up.shape[2] * 4
    T1 = _pick(S, (1024, 512, 256))
    T2 = _pick(S, (2048, 1024, 512))
    TH = _pick(S, (512, 256))
    TB = _pick(S, (512, 256))
    TN = _pick(D, (512, 256, 128))
    TF = _pick(F, (1024, 512))
    TG = _pick(3 * D, (1536, 1024, 768))
    TW = _pick(F, (2048, 1024))

    cast = lambda w: w.astype(MXU_DTYPE)
    st_a, tok = _gather_start("gather_a", [cast(attn_w_in[0])], [1], jnp.zeros((8, LANES), F32), halves=True)
    tok, (xs_late, w_aout, w_fup, w_fdown, w_hin, w_hout) = lax.optimization_barrier(
        (tok, (xs, attn_w_out, ffn_w_up, ffn_w_down, hgrn_w_in, hgrn_w_out)))
    st_b, tok = _gather_start("gather_b", [cast(w_aout[0]), cast(w_fup[0]), cast(w_fdown[0])], [0, 1, 0], tok)
    st_c, tok = _gather_start("gather_c", [cast(w_hin[0]), cast(w_hout[0]), hgrn_norm_g, cast(w_fup[1]), cast(w_fdown[1])],
                              [1, 0, 1, 1, 0], tok)

    cos3, sin3 = _rope_tables(S)
    sel = _head_sel(D)
    sel_t = sel.T

    xc3 = _stack_classes("x_classes", xs_late, MXU_DTYPE)
    (wa_in,) = _gather_wait("gather_a_wait", st_a, tok, xc3, cos3, sin3)
    wa_in = _share_halves("share_a", wa_in, _pick(D // 2, (256, 128)))
    P3 = _attn_proj(xc3, wa_in, cos3, sin3, T2, TN)
    o3, lse3 = _attn_fwd(P3, D)
    o_att, L_att = _attn_mix(o3, lse3, sel)
    wa_out, w_up0, w_down0 = _gather_wait("gather_b_wait", st_b, L_att)
    x1, xm1, xh1, r1 = _mm_res_ln("attn_out_ln", o_att, wa_out, xs, ln_mix_g[0:1], ln_mix_b[0:1], TH, D)
    a0 = _mlp_up("mlp0_up", xm1, w_up0, T2, TF, D)
    x2, xm2, xh2, r2 = _mm_res_ln("mlp0_down_ln", a0, w_down0, x1, ln_ffn_g[0:1], ln_ffn_b[0:1], TH, F)

    wh_in, wh_out, norm_g, w_up1, w_down1 = _gather_wait("gather_c_wait", st_c, r2)
    P1 = _plain_mm("hgrn_proj", xm2, wh_in, "nn", F32, T1, _pick(3 * D, (1024, 768, 512)), D)
    o_h, n_h, states = _hgrn_fwd(P1, lb_logits, norm_g, TB)
    x3, xm3, xh3, r3 = _mm_res_ln("hgrn_out_ln", n_h, wh_out, x2, ln_mix_g[1:2], ln_mix_b[1:2], TH, D)
    a1 = _mlp_up("mlp1_up", xm3, w_up1, T2, TF, D)
    x4, _, xh4, r4 = _mm_res_ln("mlp1_down_ln", a1, w_down1, x3, ln_ffn_g[1:2], ln_ffn_b[1:2], TH, F)

    wgrad = lambda name, a, dy, tm, tn: _plain_mm(name, a, dy, "tn", MXU_DTYPE, tm, tn, T1)
    sq, du4, dum4, dg_ffn1, db_ffn1 = _loss_ln_bwd(x4, tgt, xh4, r4, ln_ffn_g[1:2], TH)
    dh1 = _mlp_down_bwd("mlp1_down_bwd", dum4, w_down1, a1, T2, TF, D)
    g_down1 = wgrad("g_down1", a1, dum4, TW, D)
    g_up1 = wgrad("g_up1", xm3, dh1, D, TW)
    sc_1, tok = _scatter_start("scatter_1", [g_down1, g_up1], [0, 1])
    du3, dum3, dg_mix1, db_mix1 = _mm_nt_res_ln_bwd("mlp1_up_bwd", dh1, w_up1, du4, xh3, r3, ln_mix_g[1:2], TH, F, tok)
    dn = _plain_mm("hgrn_out_bwd", dum3, wh_out, "nt", F32, T1, D, D)
    g_hout = wgrad("g_hgrn_out", n_h, dum3, D, D)
    dP1, dg_norm, dlb = _hgrn_bwd(P1, o_h, states, dn, lb_logits, norm_g, TB)
    dP1 = dP1.reshape(3 * S, D)
    g_hin = _matmul("g_hgrn_in", xm2, dP1, "tn", D, D, T1, [(_sds((D, 3 * D), MXU_DTYPE), _ij_spec(D, D))], _store_epilogue,
                    b_map=lambda i, j, k: (k + j * (S // T1), 0), mnk=(D, 3 * D, S))[0]
    d_lb_logits = _lb_logits_grad(dlb, lb_logits)
    sc_2, tok = _scatter_start("scatter_2", [g_hout, g_hin], [0, 1])

    du2, dum2, dg_ffn0, db_ffn0 = _mm_nt_res_ln_bwd("hgrn_in_bwd", dP1, wh_in, du3, xh2, r2, ln_ffn_g[0:1], TH, D, tok,
                                                    a_map=lambda i, j, k: (i + k * (S // TH), 0), mk=3 * D)
    dh0 = _mlp_down_bwd("mlp0_down_bwd", dum2, w_down0, a0, T2, TF, D)
    g_down0 = wgrad("g_down0", a0, dum2, TW, D)
    g_up0 = wgrad("g_up0", xm1, dh0, D, TW)
    sc_3, tok = _scatter_start("scatter_3", [g_down0, g_up0], [0, 1])
    du1, dum1, dg_mix0, db_mix0 = _mm_nt_res_ln_bwd("mlp0_up_bwd", dh0, w_up0, du2, xh1, r1, ln_mix_g[0:1], TH, F, tok)
    do, delta = _attn_out_bwd(dum1, wa_out, o_att, sel_t, TH, D)
    g_aout = wgrad("g_attn_out", o_att, dum1, D, D)
    sc_5, tok = _scatter_start("scatter_5", [g_aout], [0])
    dP3 = _attn_bwd(P3, _stack_classes("do_classes", do, MXU_DTYPE), _stack_classes("lse_classes", L_att, F32),
                    _stack_classes("delta_classes", delta, F32), cos3, sin3, D, tok)
    small = jnp.concatenate([d_lb_logits, dg_mix0, dg_mix1, db_mix0, db_mix1, dg_ffn0, dg_ffn1, db_ffn0, db_ffn1,
                             dg_norm, sq, jnp.zeros((4, D), F32)], axis=0)
    small = _all_reduce_small(small, dP3)
    loss = 0.5 * jnp.sum(small[11]) / D
    grp = lambda j: j // (3 * D // TG)
    g_ain = _matmul("g_attn_in", xc3, dP3, "tn", D, TG, T1, [(_sds((D, 9 * D), MXU_DTYPE), _ij_spec(D, TG))], _store_epilogue,
                    a_map=lambda i, j, k: (k + grp(j) * (S // T1), i),
                    b_map=lambda i, j, k: (k + grp(j) * (S // T1), j % (3 * D // TG)), mnk=(D, 9 * D, S), dep=small)[0]
    sc_4, tok = _scatter_start("scatter_4", [g_ain], [1])
    dxc3 = _matmul("attn_in_bwd", dP3, wa_in, "nt", TH, D, 3 * D, [(_sds((3 * S, D), F32), _ij_spec(TH, D))], _store_epilogue,
                   b_map=lambda i, j, k: (j, k + i // (S // TH)), mnk=(3 * S, D, 3 * D), dep=tok)[0]
    grad_x = _input_grad(du1, dxc3)

    def reduced(name, state, *after):
        gs, lands = _scatter_wait(name + "_wait", state, *after)
        return [_reduce_join(f"{name}_reduce_{i}", l, _own_piece(g, ax)) for i, (l, g, ax) in enumerate(zip(lands, gs, state[4]))]

    r_down1, r_up1 = reduced("scatter_1", sc_1, grad_x)
    r_hout, r_hin = reduced("scatter_2", sc_2, r_up1)
    r_down0, r_up0 = reduced("scatter_3", sc_3, r_hin)
    (r_aout,) = reduced("scatter_5", sc_5, r_up0)

    my_chip = 2 * lax.axis_index("x") + lax.axis_index("y")
    nsh = hgrn_norm_g.shape[1]
    g_norm = lax.dynamic_slice(small[10:11], (0, my_chip * nsh), (1, nsh))

    grads, upd = {}, {}

    def update(nm, w, gs, m, v):
        upd[nm] = _adamw("adamw_" + nm, w, gs, m, v)
        grads[nm] = upd[nm][3]

    update("hgrn_w_in", hgrn_w_in, [r_hin], m_hgrn_w_in, v_hgrn_w_in)
    update("hgrn_w_out", hgrn_w_out, [r_hout], m_hgrn_w_out, v_hgrn_w_out)
    update("ffn_w_up", ffn_w_up, [r_up0, r_up1], m_ffn_w_up, v_ffn_w_up)
    update("ffn_w_down", ffn_w_down, [r_down0, r_down1], m_ffn_w_down, v_ffn_w_down)
    update("attn_w_out", attn_w_out, [r_aout], m_attn_w_out, v_attn_w_out)
    update("hgrn_norm_g", hgrn_norm_g, [g_norm], m_hgrn_norm_g, v_hgrn_norm_g)
    cat = lambda ts: jnp.concatenate(ts, axis=0)
    small_w = cat([lb_logits, ln_mix_g, ln_mix_b, ln_ffn_g, ln_ffn_b])
    small_m = cat([m_lb_logits, m_ln_mix_g, m_ln_mix_b, m_ln_ffn_g, m_ln_ffn_b])
    small_v = cat([v_lb_logits, v_ln_mix_g, v_ln_mix_b, v_ln_ffn_g, v_ln_ffn_b])
    small_upd = _adamw("adamw_small", small_w, [small[0:10]], small_m, small_v)
    for i, nm in enumerate(["lb_logits", "ln_mix_g", "ln_mix_b", "ln_ffn_g", "ln_ffn_b"]):
        grads[nm] = small[2 * i:2 * i + 2]
        upd[nm] = tuple(t[2 * i:2 * i + 2] for t in small_upd)
    done = [upd[k][2] for k in ("hgrn_w_in", "hgrn_w_out", "ffn_w_up", "ffn_w_down", "attn_w_out", "hgrn_norm_g")]
    (r_ain,) = reduced("scatter_4", sc_4, small_upd[2], *done)
    update("attn_w_in", attn_w_in, [r_ain], m_attn_w_in, v_attn_w_in)

    order = ["attn_w_in", "attn_w_out", "hgrn_w_in", "hgrn_w_out", "hgrn_norm_g", "lb_logits", "ln_mix_g", "ln_mix_b",
             "ln_ffn_g", "ln_ffn_b", "ffn_w_up", "ffn_w_down"]
    return (loss, grad_x[None], *[grads[k] for k in order], *[upd[k][0] for k in order],
            *[upd[k][1] for k in order], *[upd[k][2] for k in order])
```

```python
import math

import jax
import jax.numpy as jnp
from jax import lax
from jax.experimental import pallas as pl
from jax.experimental.pallas import tpu as pltpu

F32 = jnp.float32
BF16 = jnp.bfloat16
MXU_DTYPE = BF16

HEAD_DIM = 64
ATTN_BLK = 128
DILATIONS = (1, 4, 16)
ROPE_THETA = 10000.0
HGRN_DK = 128
HGRN_CHUNK = 64
DEPTH = 2
LN_EPS = 1e-5
RMS_EPS = 1e-6
ALPHA = (2 * DEPTH) ** 0.25
ADAM_LR, ADAM_B1, ADAM_B2, ADAM_EPS, ADAM_WD, ADAM_STEP = 0.001, 0.9, 0.999, 1e-08, 0.01, 10

LANES = 128
VMEM_LIMIT = 56 * 1024 * 1024
NEG = -1e30
MESH = pl.DeviceIdType.MESH


def _cparams(sem=None):
    return pltpu.CompilerParams(dimension_semantics=sem, vmem_limit_bytes=VMEM_LIMIT)


def _sds(shape, dtype):
    return jax.ShapeDtypeStruct(tuple(shape), dtype)


def _dg(a, b, ca, cb):
    return lax.dot_general(a, b, (((ca,), (cb,)), ((), ())), preferred_element_type=F32)


def _nn(a, b):
    return _dg(a, b, 1, 0)


def _nt(a, b):
    return _dg(a, b, 1, 1)


def _tn(a, b):
    return _dg(a, b, 0, 0)


def _split3(a):
    hi = a.astype(BF16)
    r = a - hi.astype(F32)
    mid = r.astype(BF16)
    lo = (r - mid.astype(F32)).astype(BF16)
    return hi, mid, lo


def _exact_nn(a, sel):
    hi, mid, lo = _split3(a)
    return _nn(hi, sel) + _nn(mid, sel) + _nn(lo, sel)


def _pick(n, prefs):
    for p in prefs:
        if n % p == 0:
            return p
    return n


def _matmul(name, a, b, form, tm, tn, tk, outs, epilogue, extras=(), a_map=None, b_map=None, mnk=None, dep=None,
            sem=("parallel", "parallel", "arbitrary"), split=None):
    if form == "nn":
        (M, K), N = a.shape, b.shape[1]
        a_spec = pl.BlockSpec((tm, tk), a_map or (lambda i, j, k: (i, k)))
        b_spec = pl.BlockSpec((tk, tn), b_map or (lambda i, j, k: (k, j)))
        ca, cb = 1, 0
    elif form == "nt":
        (M, K), N = a.shape, b.shape[0]
        a_spec = pl.BlockSpec((tm, tk), a_map or (lambda i, j, k: (i, k)))
        b_spec = pl.BlockSpec((tn, tk), b_map or (lambda i, j, k: (j, k)))
        ca, cb = 1, 1
    else:
        (K, M), N = a.shape, b.shape[1]
        a_spec = pl.BlockSpec((tk, tm), a_map or (lambda i, j, k: (k, i)))
        b_spec = pl.BlockSpec((tk, tn), b_map or (lambda i, j, k: (k, j)))
        ca, cb = 0, 0
    if mnk is not None:
        M, N, K = mnk
    assert M % tm == 0 and N % tn == 0 and K % tk == 0, (name, M, N, K, tm, tn, tk)
    nk = K // tk
    ne, no = len(extras), len(outs)
    deps = [] if dep is None else [dep]
    nd = len(deps)

    def body(a_ref, b_ref, *rest):
        extra_refs, out_refs = rest[:ne], rest[ne + nd:ne + nd + no]
        j = pl.program_id(1)
        if split is not None:
            kind, n = split
            assert nk == 1 and form != "tn"
            tiled = [t for _, _, *t in list(extras) + list(outs)]
            refs = list(extra_refs) + list(out_refs)
            for ci in range(n):
                if kind == "cols":
                    cs = slice(ci * (tn // n), (ci + 1) * (tn // n))
                    part = _dg(a_ref[...].astype(MXU_DTYPE), (b_ref[:, cs] if form == "nn" else b_ref[cs, :]).astype(MXU_DTYPE), ca, cb)
                    view = [r.at[:, cs] if t else r for r, t in zip(refs, tiled)]
                else:
                    rs = slice(ci * (tm // n), (ci + 1) * (tm // n))
                    part = _dg(a_ref[rs, :].astype(MXU_DTYPE), b_ref[...].astype(MXU_DTYPE), ca, cb)
                    view = [r.at[rs, :] if t else r for r, t in zip(refs, tiled)]
                epilogue(part, view[:ne], view[ne:], j, ci)
            return
        part = _dg(a_ref[...].astype(MXU_DTYPE), b_ref[...].astype(MXU_DTYPE), ca, cb)
        if nk == 1:
            epilogue(part, extra_refs, out_refs, j, 0)
            return
        acc_ref = rest[-1]
        k = pl.program_id(2)

        @pl.when(k == 0)
        def _():
            acc_ref[...] = part

        @pl.when(k > 0)
        def _():
            acc_ref[...] += part

        @pl.when(k == nk - 1)
        def _():
            epilogue(acc_ref[...], extra_refs, out_refs, j, 0)

    res = pl.pallas_call(
        body,
        name=name,
        grid=(M // tm, N // tn, nk),
        in_specs=[a_spec, b_spec] + [s for _, s, *_ in extras] + [pl.BlockSpec(memory_space=pl.ANY)] * nd,
        out_specs=[s for _, s, *_ in outs],
        out_shape=[o for o, *_ in outs],
        scratch_shapes=[pltpu.VMEM((tm, tn), F32)] if nk > 1 else [],
        compiler_params=_cparams(sem),
    )(a, b, *[e for e, *_ in extras], *deps)
    return res


def _ij_spec(tm, tn):
    return pl.BlockSpec((tm, tn), lambda i, j, k: (i, j))


def _store_epilogue(acc, extra_refs, out_refs, j, ci):
    out_refs[0][...] = acc.astype(out_refs[0].dtype)


def _plain_mm(name, a, b, form, out_dtype, tm, tn, tk):
    M = a.shape[1] if form == "tn" else a.shape[0]
    N = b.shape[0] if form == "nt" else b.shape[1]
    return _matmul(name, a, b, form, tm, tn, tk, [(_sds((M, N), out_dtype), _ij_spec(tm, tn))], _store_epilogue)[0]


def _class_slabs(S):
    assert DILATIONS[0] == 1
    return [(g, d, r, S // d) for g, d in enumerate(DILATIONS) if d > 1 for r in range(d)]


def _stack_classes(name, t, out_dtype):
    S, W = t.shape

    def body(x_ref, o_ref):
        o_ref[0:S, :] = x_ref[...].astype(out_dtype)
        for g, d, r, n in _class_slabs(S):
            o_ref[g * S + r * n:g * S + (r + 1) * n, :] = x_ref[pl.ds(r, n, stride=d), :].astype(out_dtype)

    return pl.pallas_call(
        body,
        name=name,
        grid=(W // LANES,),
        in_specs=[pl.BlockSpec((S, LANES), lambda j: (0, j))],
        out_specs=pl.BlockSpec((3 * S, LANES), lambda j: (0, j)),
        out_shape=_sds((3 * S, W), out_dtype),
        compiler_params=_cparams(("parallel",)),
    )(t)


def _rope_tables(seq):
    half = HEAD_DIM // 2
    inv = ROPE_THETA ** (-jnp.arange(half, dtype=F32) * (2.0 / HEAD_DIM))
    inv = jnp.tile(inv, LANES // half)
    pos = []
    for d in DILATIONS:
        row = jnp.arange(seq)
        pos.append((row % (seq // d)) * d + row // (seq // d))
    ang = jnp.concatenate(pos).astype(F32)[:, None] * inv[None, :]
    first = (jnp.arange(LANES) % HEAD_DIM) < half
    sin = jnp.sin(ang)
    return jnp.cos(ang), jnp.where(first[None, :], -sin, sin)


def _partner(x):
    half = HEAD_DIM // 2
    lane = lax.broadcasted_iota(jnp.int32, x.shape, 1)
    first = (lane % HEAD_DIM) < half
    return jnp.where(first, pltpu.roll(x, LANES - half, 1), pltpu.roll(x, half, 1))


def _attn_proj(x3, w_full, cos3, sin3, tm, tn):
    S3, D = x3.shape
    S = S3 // 3
    per_part = D // tn
    per_group = 3 * per_part

    def epilogue(acc, extra_refs, out_refs, j, ci):
        cos_ref, sin_ref = extra_refs
        o_ref = out_refs[0]
        is_rot = j // per_part < 2
        c = jnp.where(is_rot, cos_ref[...], 1.0)
        s = jnp.where(is_rot, sin_ref[...], 0.0)
        for t in range(acc.shape[1] // LANES):
            xs = acc[:, t * LANES:(t + 1) * LANES]
            o_ref[:, t * LANES:(t + 1) * LANES] = (xs * c + _partner(xs) * s).astype(o_ref.dtype)

    tab = pl.BlockSpec((tm, LANES), lambda i, j, k: (i, 0))
    return _matmul("attn_proj", x3, w_full, "nn", tm, tn, D, [(_sds((S3, 3 * D), MXU_DTYPE), _ij_spec(tm, tn), True)],
                   epilogue, extras=[(cos3, tab), (sin3, tab)],
                   b_map=lambda i, j, k: (k, j + (i // (S // tm)) * per_group), mnk=(S3, 3 * D, D),
                   split=("cols", tn // (2 * LANES)))[0]


def _head_sel(d_model):
    h = jnp.arange(LANES)[:, None]
    l = jnp.arange(d_model)[None, :]
    return (l // HEAD_DIM == h).astype(BF16)


def _class_edges(b, nblk):
    g = b // nblk
    per_class = jnp.where(g == 0, nblk // DILATIONS[0], jnp.where(g == 1, nblk // DILATIONS[1], nblk // DILATIONS[2]))
    pos = (b % nblk) % per_class
    return pos != 0, pos != per_class - 1


def _two_heads(t, top):
    zero = jnp.zeros_like(t)
    return jnp.concatenate([jnp.where(top, t, zero), jnp.where(top, zero, t)], axis=0)


def _band_mask(has_prev):
    B = ATTN_BLK
    row = lax.broadcasted_iota(jnp.int32, (2 * B, 2 * B), 0) % B
    col = lax.broadcasted_iota(jnp.int32, (2 * B, 2 * B), 1)
    in_prev = jnp.logical_and(jnp.logical_and(col < B, col >= row), has_prev)
    in_own = jnp.logical_and(col >= B, col - B <= row)
    return jnp.logical_or(in_prev, in_own)


def _attn_fwd(P3, D):
    S3 = P3.shape[0]
    B = ATTN_BLK
    nblk = S3 // 3 // B
    npairs = D // LANES
    scale = HEAD_DIM ** -0.5

    def body(q_ref, kc_ref, vc_ref, kp_ref, vp_ref, o_ref, lse_ref):
        has_prev, _ = _class_edges(pl.program_id(0), nblk)
        ok = _band_mask(has_prev)
        lane = lax.broadcasted_iota(jnp.int32, (B, LANES), 1)
        top = lane < HEAD_DIM
        lse_acc = jnp.zeros((B, LANES), F32)
        for j in range(npairs):
            sl = slice(j * LANES, (j + 1) * LANES)
            Q = _two_heads(q_ref[:, sl] * scale, top)
            K2 = jnp.concatenate([kp_ref[:, sl], kc_ref[:, sl]], axis=0)
            V2 = jnp.concatenate([vp_ref[:, sl], vc_ref[:, sl]], axis=0)
            s = jnp.where(ok, _nt(Q, K2), NEG)
            m = jnp.max(s, axis=1, keepdims=True)
            p = jnp.exp(s - m)
            l = jnp.sum(p, axis=1, keepdims=True)
            o = _nn((p * (1.0 / l)).astype(MXU_DTYPE), V2)
            o_ref[:, sl] = jnp.where(top, o[:B], o[B:])
            lse = m + jnp.log(l)
            lse_acc = jnp.where(lane == 2 * j, lse[:B], jnp.where(lane == 2 * j + 1, lse[B:], lse_acc))
        lse_ref[...] = lse_acc

    blk = lambda part, prev: pl.BlockSpec(
        (B, D), (lambda b: (jnp.maximum(b - 1, 0), part)) if prev else (lambda b: (b, part)))
    return pl.pallas_call(
        body,
        name="attn_fwd",
        grid=(3 * nblk,),
        in_specs=[blk(0, False), blk(1, False), blk(2, False), blk(1, True), blk(2, True)],
        out_specs=[pl.BlockSpec((B, D), lambda b: (b, 0)), pl.BlockSpec((B, LANES), lambda b: (b, 0))],
        out_shape=[_sds((S3, D), F32), _sds((S3, LANES), F32)],
        compiler_params=_cparams(("parallel",)),
    )(P3, P3, P3, P3, P3)


def _attn_mix(o3, lse3, sel):
    S3, D = o3.shape
    S = S3 // 3

    def body(o3_ref, lse_ref, sel_ref, o_ref, L_ref, w_ref):
        @pl.when(pl.program_id(0) == 0)
        def _():
            w_ref[0] = lse_ref[0:S, :]
            for g, d, r, n in _class_slabs(S):
                w_ref[g, pl.ds(r, n, stride=d), :] = lse_ref[g * S + r * n:g * S + (r + 1) * n, :]
            a, b, c = w_ref[0], w_ref[1], w_ref[2]
            m = jnp.maximum(jnp.maximum(a, b), c)
            L = m + jnp.log(jnp.exp(a - m) + jnp.exp(b - m) + jnp.exp(c - m))
            L_ref[...] = L
            w_ref[0] = jnp.exp(a - L)
            w_ref[1] = jnp.exp(b - L)
            w_ref[2] = jnp.exp(c - L)

        s = sel_ref[...]
        o_ref[...] = _exact_nn(w_ref[0], s) * o3_ref[0:S, :]
        for g, d, r, n in _class_slabs(S):
            rows = pl.ds(r, n, stride=d)
            o_ref[rows, :] += _exact_nn(w_ref[g, rows, :], s) * o3_ref[g * S + r * n:g * S + (r + 1) * n, :]

    return pl.pallas_call(
        body,
        name="attn_mix",
        grid=(D // LANES,),
        in_specs=[pl.BlockSpec((S3, LANES), lambda j: (0, j)), pl.BlockSpec((S3, LANES), lambda j: (0, 0)),
                  pl.BlockSpec((LANES, LANES), lambda j: (0, j))],
        out_specs=[pl.BlockSpec((S, LANES), lambda j: (0, j)), pl.BlockSpec((S, LANES), lambda j: (0, 0))],
        out_shape=[_sds((S, D), F32), _sds((S, LANES), F32)],
        scratch_shapes=[pltpu.VMEM((3, S, LANES), F32)],
        compiler_params=_cparams(("arbitrary",)),
    )(o3, lse3, sel)


def _attn_bwd(P3, do3, L3, delta3, cos3, sin3, D, dep):
    S3 = P3.shape[0]
    B = ATTN_BLK
    nblk = S3 // 3 // B
    npairs = D // LANES
    scale = HEAD_DIM ** -0.5

    def body(c_ref, kp_ref, vp_ref, qn_ref, doc_ref, don_ref, Lc_ref, Ln_ref, dc_ref, dn_ref, cos_ref, sin_ref, dep_ref, out_ref):
        has_prev, has_next = _class_edges(pl.program_id(0), nblk)
        ok = _band_mask(has_prev)
        row = lax.broadcasted_iota(jnp.int32, (2 * B, B), 0) % B
        col = lax.broadcasted_iota(jnp.int32, (2 * B, B), 1)
        ok_n = jnp.logical_and(col >= row, has_next)
        lane = lax.broadcasted_iota(jnp.int32, (B, LANES), 1)
        top = lane < HEAD_DIM
        cos_t = cos_ref[...]
        sin_inv = -sin_ref[...]
        Lc_all, Ln_all, dc_all, dn_all = Lc_ref[...], Ln_ref[...], dc_ref[...], dn_ref[...]
        pair_col = lambda t, j: jnp.concatenate([t[:, 2 * j:2 * j + 1], t[:, 2 * j + 1:2 * j + 2]], axis=0)
        for j in range(npairs):
            sl = lambda part: slice(part * D + j * LANES, part * D + (j + 1) * LANES)
            pj = slice(j * LANES, (j + 1) * LANES)
            kc2, vc2 = c_ref[:, sl(1)], c_ref[:, sl(2)]
            K2 = jnp.concatenate([kp_ref[:, pj], kc2], axis=0)
            V2 = jnp.concatenate([vp_ref[:, pj], vc2], axis=0)
            Qc = _two_heads(c_ref[:, sl(0)] * scale, top)
            Qn = _two_heads(qn_ref[:, pj] * scale, top)
            DOc = _two_heads(doc_ref[:, pj].astype(MXU_DTYPE), top)
            DOn = _two_heads(don_ref[:, pj].astype(MXU_DTYPE), top)
            P_c = jnp.where(ok, jnp.exp(_nt(Qc, K2) - pair_col(Lc_all, j)), 0.0)
            dS_c = P_c * (_nt(DOc, V2) - pair_col(dc_all, j))
            P_n = jnp.where(ok_n, jnp.exp(_nt(Qn, kc2) - pair_col(Ln_all, j)), 0.0)
            dS_n = P_n * (_nt(DOn, vc2) - pair_col(dn_all, j))
            dq = _nn(dS_c.astype(MXU_DTYPE), K2)
            dq2 = jnp.where(top, dq[:B], dq[B:]) * scale
            Qk = jnp.concatenate([Qc, Qn], axis=0)
            DOk = jnp.concatenate([DOc, DOn], axis=0)
            dk2 = _tn(jnp.concatenate([dS_c[:, B:], dS_n], axis=0).astype(MXU_DTYPE), Qk)
            dv2 = _tn(jnp.concatenate([P_c[:, B:], P_n], axis=0).astype(MXU_DTYPE), DOk)
            out_ref[:, sl(0)] = (dq2 * cos_t + _partner(dq2) * sin_inv).astype(out_ref.dtype)
            out_ref[:, sl(1)] = (dk2 * cos_t + _partner(dk2) * sin_inv).astype(out_ref.dtype)
            out_ref[:, sl(2)] = dv2.astype(out_ref.dtype)

    cur = lambda b: b
    prv = lambda b: jnp.maximum(b - 1, 0)
    nxt = lambda b: jnp.minimum(b + 1, 3 * nblk - 1)
    spec = lambda w, f, part=0: pl.BlockSpec((B, w), lambda b: (f(b), part))
    return pl.pallas_call(
        body,
        name="attn_bwd",
        grid=(3 * nblk,),
        in_specs=[spec(3 * D, cur), spec(D, prv, 1), spec(D, prv, 2), spec(D, nxt, 0), spec(D, cur), spec(D, nxt),
                  spec(LANES, cur), spec(LANES, nxt), spec(LANES, cur), spec(LANES, nxt), spec(LANES, cur), spec(LANES, cur),
                  pl.BlockSpec(memory_space=pl.ANY)],
        out_specs=spec(3 * D, cur),
        out_shape=_sds((S3, 3 * D), MXU_DTYPE),
        compiler_params=_cparams(("parallel",)),
    )(P3, P3, P3, P3, do3, do3, L3, L3, delta3, delta3, cos3, sin3, dep)


def _input_grad(du, dx3):
    S, D = du.shape

    def body(du_ref, dx_ref, o_ref):
        o_ref[...] = ALPHA * du_ref[...] + dx_ref[0:S, :]
        for g, d, r, n in _class_slabs(S):
            o_ref[pl.ds(r, n, stride=d), :] += dx_ref[g * S + r * n:g * S + (r + 1) * n, :]

    return pl.pallas_call(
        body,
        name="input_grad",
        grid=(D // LANES,),
        in_specs=[pl.BlockSpec((S, LANES), lambda j: (0, j)), pl.BlockSpec((3 * S, LANES), lambda j: (0, j))],
        out_specs=pl.BlockSpec((S, LANES), lambda j: (0, j)),
        out_shape=_sds((S, D), F32),
        compiler_params=_cparams(("parallel",)),
    )(du, dx3)


def _chunk_causal(tb):
    r = lax.broadcasted_iota(jnp.int32, (tb, tb), 0)
    c = lax.broadcasted_iota(jnp.int32, (tb, tb), 1)
    return jnp.logical_and((r // HGRN_CHUNK) == (c // HGRN_CHUNK), r >= c)


def _chunk_sums(a, lower):
    C = HGRN_CHUNK
    r = lax.broadcasted_iota(jnp.int32, (C, C), 0)
    c = lax.broadcasted_iota(jnp.int32, (C, C), 1)
    tri = ((r >= c) if lower else (r <= c)).astype(BF16)
    parts = _split3(a)
    out = []
    for ci in range(a.shape[0] // C):
        rows = slice(ci * C, (ci + 1) * C)
        out.append(_nn(tri, parts[0][rows]) + _nn(tri, parts[1][rows]) + _nn(tri, parts[2][rows]))
    return jnp.concatenate(out, axis=0)


def _chunk_last(b):
    C = HGRN_CHUNK
    return jnp.concatenate([jnp.broadcast_to(b[(ci + 1) * C - 1:(ci + 1) * C, :], (C, b.shape[1]))
                            for ci in range(b.shape[0] // C)], axis=0)


def _lower_bound(lb_ref):
    l0, l1 = lb_ref[0:1, :], lb_ref[1:2, :]
    m = jnp.maximum(l0, l1)
    e0, e1 = jnp.exp(l0 - m), jnp.exp(l1 - m)
    return e1 / (e0 + e1)


def _hgrn_gates(q_raw, z, lb):
    sg = 1.0 / (1.0 + jnp.exp(-z))
    sn = 1.0 / (1.0 + jnp.exp(z))
    f = lb + (1.0 - lb) * sg
    key = (1.0 - lb) * sn
    sq = 1.0 / (1.0 + jnp.exp(-q_raw))
    return sg, sn, f, key, sq


HGRN_HEADS_PER_STEP = 8


def _hgrn_fwd(P1, lb_logits, norm_g, tb):
    S = P1.shape[0]
    D = P1.shape[1] // 3
    K = HGRN_DK
    H = D // K
    HP = min(HGRN_HEADS_PER_STEP, H)
    C = HGRN_CHUNK
    cpb = tb // C
    nt = S // tb

    def body(q_ref, f_ref, i_ref, lb_ref, g_ref, o_ref, n_ref, st_ref, state):
        t = pl.program_id(1)

        @pl.when(t == 0)
        def _():
            state[...] = jnp.zeros_like(state)

        lb_all = _lower_bound(lb_ref)
        low = _chunk_causal(tb)
        for hh in range(HP):
            lanes = slice(hh * K, (hh + 1) * K)
            q_raw, z, v = q_ref[:, lanes], f_ref[:, lanes], i_ref[:, lanes]
            sg, sn, f, key, sq = _hgrn_gates(q_raw, z, lb_all[:, lanes])
            b = _chunk_sums(jnp.log(f), lower=True)
            qd = (q_raw * sq * jnp.exp(b)).astype(MXU_DTYPE)
            kd = (key * jnp.exp(-b)).astype(MXU_DTYPE)
            kb = (key * jnp.exp(_chunk_last(b) - b)).astype(MXU_DTYPE)
            vm = v.astype(MXU_DTYPE)
            a = jnp.where(low, _nt(qd, kd), 0.0).astype(MXU_DTYPE)
            o_intra = _nn(a, vm)
            st = state[hh]
            outs = []
            for ci in range(cpb):
                rows = slice(ci * C, (ci + 1) * C)
                st_ref[hh, ci] = st
                outs.append(o_intra[rows] + _nt(qd[rows], st.astype(MXU_DTYPE)))
                st = st * jnp.exp(b[(ci + 1) * C - 1:(ci + 1) * C, :]) + _tn(vm[rows], kb[rows])
            state[hh] = st
            o = jnp.concatenate(outs, axis=0)
            o_ref[:, lanes] = o
            rs = lax.rsqrt(jnp.mean(o * o, axis=1, keepdims=True) + RMS_EPS)
            n_ref[:, lanes] = o * rs * g_ref[:, lanes]

    tok = lambda part: pl.BlockSpec((tb, HP * K), lambda h, t: (t, part * (H // HP) + h))
    vec = lambda rows: pl.BlockSpec((rows, HP * K), lambda h, t: (0, h))
    return pl.pallas_call(
        body,
        name="hgrn_fwd",
        grid=(H // HP, nt),
        in_specs=[tok(0), tok(1), tok(2), vec(2), vec(1)],
        out_specs=[tok(0), tok(0), pl.BlockSpec((HP, cpb, K, K), lambda h, t: (h, t, 0, 0))],
        out_shape=[_sds((S, D), F32), _sds((S, D), F32), _sds((H, S // C, K, K), F32)],
        scratch_shapes=[pltpu.VMEM((HP, K, K), F32)],
        compiler_params=_cparams(("parallel", "arbitrary")),
    )(P1, P1, P1, lb_logits, norm_g)


def _hgrn_bwd(P1, o_pre, states, dn, lb_logits, norm_g, tb):
    S = P1.shape[0]
    D = P1.shape[1] // 3
    K = HGRN_DK
    H = D // K
    HP = min(HGRN_HEADS_PER_STEP, H)
    C = HGRN_CHUNK
    cpb = tb // C
    nt = S // tb

    def body(q_ref, f_ref, i_ref, o_ref, st_ref, dn_ref, lb_ref, g_ref, d_ref, dg_ref, dlb_ref, dstate):
        t = pl.program_id(1)

        @pl.when(t == 0)
        def _():
            dstate[...] = jnp.zeros_like(dstate)
            dg_ref[...] = jnp.zeros_like(dg_ref)
            dlb_ref[...] = jnp.zeros_like(dlb_ref)

        lb_all = _lower_bound(lb_ref)
        low = _chunk_causal(tb)
        for hh in range(HP):
            lanes = slice(hh * K, (hh + 1) * K)
            lb = lb_all[:, lanes]
            gn = g_ref[:, lanes]
            q_raw, z, v = q_ref[:, lanes], f_ref[:, lanes], i_ref[:, lanes]
            sg, sn, f, key, sq = _hgrn_gates(q_raw, z, lb)
            b = _chunk_sums(jnp.log(f), lower=True)
            e_pos, e_neg, e_rel = jnp.exp(b), jnp.exp(-b), jnp.exp(_chunk_last(b) - b)
            qd_f, kd_f, kb_f = q_raw * sq * e_pos, key * e_neg, key * e_rel
            qd, kd, kb = qd_f.astype(MXU_DTYPE), kd_f.astype(MXU_DTYPE), kb_f.astype(MXU_DTYPE)
            vm = v.astype(MXU_DTYPE)
            a = jnp.where(low, _nt(qd, kd), 0.0).astype(MXU_DTYPE)
            o = o_ref[:, lanes]
            dnn = dn_ref[:, lanes]
            rs = lax.rsqrt(jnp.mean(o * o, axis=1, keepdims=True) + RMS_EPS)
            dg_ref[:, lanes] += jnp.sum(dnn * o * rs, axis=0, keepdims=True)
            tg = dnn * gn
            dom = (rs * tg - o * (rs * rs * rs) * jnp.mean(tg * o, axis=1, keepdims=True)).astype(MXU_DTYPE)
            da = jnp.where(low, _nt(dom, vm), 0.0).astype(MXU_DTYPE)
            dv = _tn(a, dom)
            dqd = _nn(da, kd)
            dkd = _tn(da, qd)
            dst = dstate[hh]
            dv_s, dqd_s, dkb_s, dbl_s = [None] * cpb, [None] * cpb, [None] * cpb, [None] * cpb
            for ci in reversed(range(cpb)):
                rows = slice(ci * C, (ci + 1) * C)
                st = st_ref[hh, ci]
                dstm = dst.astype(MXU_DTYPE)
                dec = jnp.exp(b[(ci + 1) * C - 1:(ci + 1) * C, :])
                dv_s[ci] = _nt(kb[rows], dstm)
                dkb_s[ci] = _nn(vm[rows], dstm)
                dqd_s[ci] = _nn(dom[rows], st.astype(MXU_DTYPE))
                db_last = jnp.sum(dkb_s[ci] * kb_f[rows], axis=0, keepdims=True) + jnp.sum(dst * st, axis=0, keepdims=True) * dec
                dbl_s[ci] = jnp.broadcast_to(db_last, (C, K))
                dst = dst * dec + _tn(dom[rows], qd[rows])
            dstate[hh] = dst
            dv = dv + jnp.concatenate(dv_s, axis=0)
            dqd = dqd + jnp.concatenate(dqd_s, axis=0)
            dkb = jnp.concatenate(dkb_s, axis=0)
            dkey = dkd * e_neg + dkb * e_rel
            db = dqd * qd_f - dkd * kd_f - dkb * kb_f
            dlogf = _chunk_sums(db, lower=False) + jnp.concatenate(dbl_s, axis=0)
            gz = (1.0 - lb) * sg * sn
            d_ref[0, :, lanes] = (dqd * e_pos * (sq + q_raw * sq * (1.0 - sq))).astype(d_ref.dtype)
            d_ref[1, :, lanes] = (dlogf * gz / f - dkey * gz).astype(d_ref.dtype)
            d_ref[2, :, lanes] = dv.astype(d_ref.dtype)
            dlb_ref[:, lanes] += jnp.sum(dlogf * sn / f - dkey * sn, axis=0, keepdims=True)

    rev = lambda t: nt - 1 - t
    tok = lambda part: pl.BlockSpec((tb, HP * K), lambda h, t: (rev(t), part * (H // HP) + h))
    vec = lambda rows: pl.BlockSpec((rows, HP * K), lambda h, t: (0, h))
    outs = pl.pallas_call(
        body,
        name="hgrn_bwd",
        grid=(H // HP, nt),
        in_specs=[tok(0), tok(1), tok(2), tok(0),
                  pl.BlockSpec((HP, cpb, K, K), lambda h, t: (h, rev(t), 0, 0)),
                  tok(0), vec(2), vec(1)],
        out_specs=[pl.BlockSpec((3, tb, HP * K), lambda h, t: (0, rev(t), h)), vec(1), vec(1)],
        out_shape=[_sds((3, S, D), MXU_DTYPE)] + [_sds((1, D), F32)] * 2,
        scratch_shapes=[pltpu.VMEM((HP, K, K), F32)],
        compiler_params=_cparams(("parallel", "arbitrary")),
    )(P1, P1, P1, o_pre, states, dn, lb_logits, norm_g)
    return outs


def _lb_logits_grad(dlb, lb_logits):
    def body(d_ref, l_ref, o_ref):
        s1 = _lower_bound(l_ref)
        d = d_ref[...]
        o_ref[0:1, :] = -(1.0 - s1) * s1 * d
        o_ref[1:2, :] = s1 * (1.0 - s1) * d

    return pl.pallas_call(body, name="lb_logits_grad", out_shape=_sds(lb_logits.shape, F32))(dlb, lb_logits)


def _ln_epilogue(acc, extra_refs, out_refs, j, ci):
    res_ref, g_ref, b_ref = extra_refs
    x_ref, xm_ref, xhat_ref, rstd_ref = out_refs
    u = ALPHA * res_ref[...] + acc
    mu = jnp.mean(u, axis=1, keepdims=True)
    cen = u - mu
    rstd = lax.rsqrt(jnp.mean(cen * cen, axis=1, keepdims=True) + LN_EPS)
    xhat = cen * rstd
    xhat_ref[...] = xhat
    x = xhat * g_ref[...] + b_ref[...]
    x_ref[...] = x
    xm_ref[...] = x.astype(xm_ref.dtype)
    rstd_ref[...] = rstd


def _mm_res_ln(name, a, w_full, res, g, b, tm, tk):
    S, D = res.shape
    row = pl.BlockSpec((tm, D), lambda i, j, k: (i, 0))
    vec = pl.BlockSpec((1, D), lambda i, j, k: (0, 0))
    outs = [(_sds((S, D), F32), row, True), (_sds((S, D), MXU_DTYPE), row, True), (_sds((S, D), F32), row, True),
            (_sds((S, 1), F32), pl.BlockSpec((tm, 1), lambda i, j, k: (i, 0)), True)]
    return _matmul(name, a, w_full, "nn", tm, D, tk, outs, _ln_epilogue, extras=[(res, row, True), (g, vec), (b, vec)],
                   split=("rows", 2) if tk == a.shape[1] else None)


def _ln_bwd_rows(dy, xh, rstd, g, first, du_ref, dum_ref, dg_ref, db_ref):
    if first is not None:
        @pl.when(first)
        def _():
            dg_ref[...] = jnp.zeros_like(dg_ref)
            db_ref[...] = jnp.zeros_like(db_ref)

    dg_ref[...] += jnp.sum(dy * xh, axis=0, keepdims=True)
    db_ref[...] += jnp.sum(dy, axis=0, keepdims=True)
    dxh = dy * g
    m1 = jnp.mean(dxh, axis=1, keepdims=True)
    m2 = jnp.mean(dxh * xh, axis=1, keepdims=True)
    du = rstd * (dxh - m1 - xh * m2)
    du_ref[...] = du
    dum_ref[...] = du.astype(dum_ref.dtype)


def _loss_ln_bwd(y, target, xhat, rstd, g, tm):
    S, D = y.shape

    def body(y_ref, t_ref, xh_ref, r_ref, g_ref, sq_ref, du_ref, dum_ref, dg_ref, db_ref):
        first = pl.program_id(0) == 0

        @pl.when(first)
        def _():
            sq_ref[...] = jnp.zeros_like(sq_ref)

        e = y_ref[...] - t_ref[...]
        sq_ref[...] += jnp.sum(e * e, axis=0, keepdims=True)
        _ln_bwd_rows(e / D, xh_ref[...], r_ref[...], g_ref[...], first, du_ref, dum_ref, dg_ref, db_ref)

    row = pl.BlockSpec((tm, D), lambda i: (i, 0))
    vec = pl.BlockSpec((1, D), lambda i: (0, 0))
    return pl.pallas_call(
        body,
        name="loss_ln_bwd",
        grid=(S // tm,),
        in_specs=[row, row, row, pl.BlockSpec((tm, 1), lambda i: (i, 0)), vec],
        out_specs=[vec, row, row, vec, vec],
        out_shape=[_sds((1, D), F32), _sds((S, D), F32), _sds((S, D), MXU_DTYPE), _sds((1, D), F32), _sds((1, D), F32)],
        compiler_params=_cparams(("arbitrary",)),
    )(y, target, xhat, rstd, g)


def _mlp_up(name, x, w_up, tm, tn, tk):
    S = x.shape[0]
    F = w_up.shape[1]

    def epilogue(acc, extra_refs, out_refs, j, ci):
        r = jnp.maximum(acc, 0.0)
        out_refs[0][...] = (r * r).astype(out_refs[0].dtype)

    return _matmul(name, x, w_up, "nn", tm, tn, tk, [(_sds((S, F), MXU_DTYPE), _ij_spec(tm, tn), True)], epilogue,
                   split=("cols", 2))[0]


def _mlp_down_bwd(name, dy, w_down, a, tm, tn, tk):
    S, F = a.shape

    def epilogue(acc, extra_refs, out_refs, j, ci):
        out_refs[0][...] = (acc * (2.0 * jnp.sqrt(extra_refs[0][...].astype(F32)))).astype(out_refs[0].dtype)

    return _matmul(name, dy, w_down, "nt", tm, tn, tk, [(_sds((S, F), MXU_DTYPE), _ij_spec(tm, tn), True)], epilogue,
                   extras=[(a, _ij_spec(tm, tn), True)], split=("cols", 2))[0]


def _mm_nt_res_ln_bwd(name, dy, w, du, xhat, rstd, g, tm, tk, dep, a_map=None, mk=None):
    S, D = du.shape

    def epilogue(acc, extra_refs, out_refs, j, ci):
        du_ref, xh_ref, r_ref, g_ref = extra_refs
        first = (pl.program_id(0) == 0) if ci == 0 else None
        _ln_bwd_rows(ALPHA * du_ref[...] + acc, xh_ref[...], r_ref[...], g_ref[...], first, *out_refs)

    row = pl.BlockSpec((tm, D), lambda i, j, k: (i, 0))
    vec = pl.BlockSpec((1, D), lambda i, j, k: (0, 0))
    return _matmul(name, dy, w, "nt", tm, D, tk,
                   [(_sds((S, D), F32), row, True), (_sds((S, D), MXU_DTYPE), row, True), (_sds((1, D), F32), vec),
                    (_sds((1, D), F32), vec)], epilogue,
                   extras=[(du, row, True), (xhat, row, True), (rstd, pl.BlockSpec((tm, 1), lambda i, j, k: (i, 0)), True), (g, vec)],
                   a_map=a_map, mnk=None if mk is None else (S, D, mk), dep=dep, sem=("arbitrary", "arbitrary", "arbitrary"),
                   split=("rows", 2) if mk is None else None)


def _attn_out_bwd(du, w_out, o, sel_t, tm, tk):
    S, D = o.shape

    def epilogue(acc, extra_refs, out_refs, j, ci):
        out_refs[0][...] = acc
        out_refs[1][...] = _exact_nn(acc * extra_refs[0][...], extra_refs[1][...])

    row = pl.BlockSpec((tm, D), lambda i, j, k: (i, 0))
    slim = pl.BlockSpec((tm, LANES), lambda i, j, k: (i, 0))
    return _matmul("attn_out_bwd", du, w_out, "nt", tm, D, tk,
                   [(_sds((S, D), F32), row, True), (_sds((S, LANES), F32), slim, True)], epilogue,
                   extras=[(o, row, True), (sel_t, pl.BlockSpec((D, LANES), lambda i, j, k: (0, 0)))], split=("rows", 2))


def _adamw(name, w, gs, m, v):
    shape = w.shape
    cols = shape[-1]
    rows = math.prod(shape[:-1])
    w2, m2, v2 = (t.reshape(rows, cols) for t in (w, m, v))
    gs2 = [g.reshape(-1, cols) for g in gs]
    ng = len(gs2)
    tr = _pick(rows // ng, (256, 128, 64, 32, 16, 8))
    per = rows // ng // tr
    c1 = 1.0 - ADAM_B1 ** ADAM_STEP
    c2 = 1.0 - ADAM_B2 ** ADAM_STEP

    def body(w_ref, m_ref, v_ref, *rest):
        g_refs, (d_ref, nm_ref, nv_ref), g_out = rest[:ng], rest[ng:ng + 3], rest[ng + 3:]
        gg = g_refs[0][...]
        if ng == 2:
            gg = jnp.where(pl.program_id(0) < per, gg, g_refs[1][...])
            g_out[0][...] = gg
        nm = ADAM_B1 * m_ref[...] + (1.0 - ADAM_B1) * gg
        nv = ADAM_B2 * v_ref[...] + (1.0 - ADAM_B2) * (gg * gg)
        nm_ref[...] = nm
        nv_ref[...] = nv
        d_ref[...] = -ADAM_LR * ((nm / c1) / (jnp.sqrt(nv / c2) + ADAM_EPS) + ADAM_WD * w_ref[...])

    blk = pl.BlockSpec((tr, cols), lambda i: (i, 0))
    g_specs = [blk] if ng == 1 else [pl.BlockSpec((tr, cols), lambda i: (jnp.minimum(i, per - 1), 0)),
                                     pl.BlockSpec((tr, cols), lambda i: (jnp.maximum(i - per, 0), 0))]
    nout = 3 if ng == 1 else 4
    outs = pl.pallas_call(
        body,
        name=name,
        grid=(rows // tr,),
        in_specs=[blk] * 3 + g_specs,
        out_specs=[blk] * nout,
        out_shape=[_sds((rows, cols), F32)] * nout,
        compiler_params=_cparams(("parallel",)),
    )(w2, m2, v2, *gs2)
    g_full = outs[3] if ng == 2 else gs2[0]
    return tuple(o.reshape(shape) for o in (outs[0], outs[1], outs[2], g_full))


HBM = pl.BlockSpec(memory_space=pl.ANY)


def _shard_slice(ref, axis, size, index):
    idx = [slice(None)] * len(ref.shape)
    idx[axis] = pl.ds(pl.multiple_of(index * size, 8), size)
    return ref.at[tuple(idx)]


def _share_halves(name, full, tr):
    R, W4 = full.shape
    W, h = W4 // 4, R // 2
    steps = [(k, t) for k in range(3) for t in range(h // tr)]

    def body(f_in, f_ref, buf, lsem, ssem, rsem):
        x, y, c = lax.axis_index("x"), lax.axis_index("y"), lax.axis_index("c")
        sibling = (x, y, 1 - c)
        chips = [(1 - x, y), (x, 1 - y), (1 - x, 1 - y)]

        def tile(k, t):
            px, py = chips[k]
            return f_ref.at[pl.ds(pl.multiple_of(c * h + t * tr, 8), tr), pl.ds(pl.multiple_of((2 * px + py) * W, LANES), W)]

        sends = []
        for s, (k, t) in enumerate(steps):
            slot = s % 2
            if s >= 2:
                sends[s - 2].wait_send()
            lc = pltpu.make_async_copy(tile(k, t), buf.at[slot], lsem.at[slot])
            lc.start()
            lc.wait()
            rc = pltpu.make_async_remote_copy(src_ref=buf.at[slot], dst_ref=tile(k, t), send_sem=ssem.at[slot], recv_sem=rsem,
                                              device_id=sibling, device_id_type=MESH)
            rc.start()
            sends.append(rc)
        for rc in sends[-2:]:
            rc.wait_send()
        whole = f_ref.at[pl.ds(0, h), pl.ds(0, 3 * W)]
        pltpu.make_async_remote_copy(src_ref=whole, dst_ref=whole, send_sem=ssem.at[0], recv_sem=rsem,
                                     device_id=sibling, device_id_type=MESH).wait_recv()

    return pl.pallas_call(
        body,
        name=name,
        in_specs=[HBM],
        out_specs=HBM,
        out_shape=_sds(full.shape, full.dtype),
        input_output_aliases={0: 0},
        scratch_shapes=[pltpu.VMEM((2, tr, W), full.dtype), pltpu.SemaphoreType.DMA((2,)), pltpu.SemaphoreType.DMA((2,)),
                        pltpu.SemaphoreType.DMA(())],
    )(full)


IN_HBM = pl.BlockSpec(memory_space=pltpu.HBM)
IN_SEM = pl.BlockSpec(memory_space=pltpu.SEMAPHORE)
DATAFLOW = pltpu.SideEffectType.DATAFLOW_SIDE_EFFECTING


def _hbm(t):
    return pltpu.with_memory_space_constraint(t, pltpu.HBM)


def _token_spec():
    return pl.BlockSpec(memory_space=pltpu.VMEM)


def _gather_copies(s_refs, f_refs, axes, halves, send, recv, loc, arrival):
    x, y, c = lax.axis_index("x"), lax.axis_index("y"), lax.axis_index("c")
    chips = [(1 - x, y), (x, 1 - y), (1 - x, 1 - y)]
    local, remote = [], []
    for a in range(len(s_refs)):
        size = s_refs[a].shape[axes[a]]
        local.append(pltpu.make_async_copy(s_refs[a], _shard_slice(f_refs[a], axes[a], size, 2 * x + y), loc.at[a]))
        for k, (px, py) in enumerate(chips):
            block = (2 * px + py) if arrival else (2 * x + y)
            src, dst = s_refs[a], _shard_slice(f_refs[a], axes[a], size, block)
            if halves:
                assert axes[a] == 1 and len(s_refs[a].shape) == 2
                h = s_refs[a].shape[0] // 2
                rows = pl.ds(pl.multiple_of(c * h, 8), h)
                src = s_refs[a].at[rows, :]
                dst = f_refs[a].at[rows, pl.ds(pl.multiple_of(block * size, LANES), size)]
            remote.append(pltpu.make_async_remote_copy(src_ref=src, dst_ref=dst, send_sem=send.at[3 * a + k],
                                                       recv_sem=recv.at[3 * a + k], device_id=(px, py, c), device_id_type=MESH))
    return local, remote


def _gather_start(name, shards, axes, after, halves=False):
    n = len(shards)
    fulls = []
    for s, ax in zip(shards, axes):
        fs = list(s.shape)
        fs[ax] *= 4
        fulls.append(lax.empty(tuple(fs), s.dtype))

    def body(*refs):
        s_refs, f_refs = refs[:n], refs[n:2 * n]
        send, recv, loc, token = refs[2 * n + 1], refs[2 * n + 2], refs[2 * n + 3], refs[-1]
        local, remote = _gather_copies(s_refs, f_refs, axes, halves, send, recv, loc, arrival=False)
        for cp in remote + local:
            cp.start()
        token[...] = jnp.zeros_like(token)

    outs = pl.pallas_call(
        body,
        name=name,
        out_shape=(pltpu.SemaphoreType.DMA((3 * n,)), pltpu.SemaphoreType.DMA((3 * n,)), pltpu.SemaphoreType.DMA((n,)),
                   *[pltpu.HBM(t.shape, t.dtype) for t in shards + fulls], _sds((8, LANES), F32)),
        in_specs=[IN_HBM] * (2 * n) + [HBM],
        out_specs=(IN_SEM, IN_SEM, IN_SEM, *[IN_HBM] * (2 * n), _token_spec()),
        input_output_aliases={i: 3 + i for i in range(2 * n)},
        compiler_params=pltpu.CompilerParams(has_side_effects=DATAFLOW),
    )(*[_hbm(t) for t in shards + fulls], after)
    return (outs[0], outs[1], outs[2], list(outs[3:3 + n]), list(outs[3 + n:3 + 2 * n]), axes, halves), outs[-1]


def _gather_wait(name, state, *after):
    send, recv, loc, s_thru, f_thru, axes, halves = state
    n = len(s_thru)

    def body(*refs):
        s_refs, f_refs = refs[:n], refs[n:2 * n]
        local, remote = _gather_copies(s_refs, f_refs, axes, halves, refs[2 * n], refs[2 * n + 1], refs[2 * n + 2], arrival=True)
        for cp in local:
            cp.wait()
        for cp in remote:
            cp.wait_send()
            cp.wait_recv()

    outs = pl.pallas_call(
        body,
        name=name,
        out_shape=tuple(pltpu.HBM(t.shape, t.dtype) for t in s_thru + f_thru),
        in_specs=[IN_HBM] * (2 * n) + [IN_SEM, IN_SEM, IN_SEM] + [HBM] * len(after),
        out_specs=tuple([IN_HBM] * (2 * n)),
        input_output_aliases={i: i for i in range(2 * n)},
        compiler_params=pltpu.CompilerParams(has_side_effects=DATAFLOW),
    )(*s_thru, *f_thru, send, recv, loc, *after)
    return list(outs[n:2 * n])


FLIPS = [(fx, fy, fc) for fx in (0, 1) for fy in (0, 1) for fc in (0, 1)][1:]


def _piece_shape(shape, axis):
    ps = list(shape)
    if axis == 0:
        ps[0] //= 8
    else:
        ps[0] //= 2
        ps[axis] //= 4
    return tuple(ps)


def _piece(ref, axis, q, c):
    shape = ref.shape
    idx = [slice(None)] * len(shape)
    if axis == 0:
        h = shape[0] // 8
        idx[0] = pl.ds(pl.multiple_of((2 * q + c) * h, 8), h)
    else:
        h, w = shape[0] // 2, shape[axis] // 4
        idx[0] = pl.ds(c * h, h)
        idx[axis] = pl.ds(pl.multiple_of(q * w, LANES if axis == len(shape) - 1 else 8), w)
    return ref.at[tuple(idx)]


def _own_piece(g, axis):
    ps = _piece_shape(g.shape, axis)
    q, c = 2 * lax.axis_index("x") + lax.axis_index("y"), lax.axis_index("c")
    start = [0] * len(ps)
    if axis == 0:
        start[0] = (2 * q + c) * ps[0]
    else:
        start[0] = c * ps[0]
        start[axis] = q * ps[axis]
    return lax.dynamic_slice(g, start, ps)


def _scatter_copies(g_refs, l_refs, axes, send, recv):
    x, y, c = lax.axis_index("x"), lax.axis_index("y"), lax.axis_index("c")
    out = []
    for a in range(len(g_refs)):
        for k, (fx, fy, fc) in enumerate(FLIPS):
            tx, ty, tc = x ^ fx, y ^ fy, c ^ fc
            out.append(pltpu.make_async_remote_copy(
                src_ref=_piece(g_refs[a], axes[a], 2 * tx + ty, tc), dst_ref=l_refs[a].at[k],
                send_sem=send.at[7 * a + k], recv_sem=recv.at[7 * a + k], device_id=(tx, ty, tc), device_id_type=MESH))
    return out


def _scatter_start(name, grads, axes):
    n = len(grads)
    lands = [lax.empty((7,) + _piece_shape(g.shape, ax), g.dtype) for g, ax in zip(grads, axes)]

    def body(*refs):
        g_refs, l_refs = refs[:n], refs[n:2 * n]
        send, recv, token = refs[2 * n], refs[2 * n + 1], refs[-1]
        for cp in _scatter_copies(g_refs, l_refs, axes, send, recv):
            cp.start()
        token[...] = jnp.zeros_like(token)

    outs = pl.pallas_call(
        body,
        name=name,
        out_shape=(pltpu.SemaphoreType.DMA((7 * n,)), pltpu.SemaphoreType.DMA((7 * n,)),
                   *[pltpu.HBM(t.shape, t.dtype) for t in grads + lands], _sds((8, LANES), F32)),
        in_specs=[IN_HBM] * (2 * n),
        out_specs=(IN_SEM, IN_SEM, *[IN_HBM] * (2 * n), _token_spec()),
        input_output_aliases={i: 2 + i for i in range(2 * n)},
        compiler_params=pltpu.CompilerParams(has_side_effects=DATAFLOW),
    )(*[_hbm(t) for t in grads + lands])
    return (outs[0], outs[1], list(outs[2:2 + n]), list(outs[2 + n:2 + 2 * n]), axes), outs[-1]


def _scatter_wait(name, state, *after):
    send, recv, g_thru, l_thru, axes = state
    n = len(g_thru)

    def body(*refs):
        g_refs, l_refs = refs[:n], refs[n:2 * n]
        for cp in _scatter_copies(g_refs, l_refs, axes, refs[2 * n], refs[2 * n + 1]):
            cp.wait_send()
            cp.wait_recv()

    outs = pl.pallas_call(
        body,
        name=name,
        out_shape=tuple(pltpu.HBM(t.shape, t.dtype) for t in g_thru + l_thru),
        in_specs=[IN_HBM] * (2 * n) + [IN_SEM, IN_SEM] + [HBM] * len(after),
        out_specs=tuple([IN_HBM] * (2 * n)),
        input_output_aliases={i: i for i in range(2 * n)},
        compiler_params=pltpu.CompilerParams(has_side_effects=DATAFLOW),
    )(*g_thru, *l_thru, send, recv, *after)
    return list(outs[:n]), list(outs[n:2 * n])


def _reduce_join(name, landing, own):
    piece = own.shape
    C = piece[-1]
    R = math.prod(piece[:-1])
    l3 = landing.reshape(7, R, C)
    own2 = own.reshape(R, C)
    tr = _pick(R, [t for t in (512, 256, 128, 64, 32, 16, 8) if t * C <= 256 * 1024])
    nsteps = R // tr

    def body(own_ref, l_ref, o_ref, buf, send, loc, recv):
        i = pl.program_id(0)
        x, y, c = lax.axis_index("x"), lax.axis_index("y"), lax.axis_index("c")
        sibling = (x, y, 1 - c)

        def copies(slot, step):
            dst = o_ref.at[pl.ds(pl.multiple_of(c * R + step * tr, 8), tr), :]
            return (pltpu.make_async_copy(buf.at[slot], dst, loc.at[slot]),
                    pltpu.make_async_remote_copy(src_ref=buf.at[slot], dst_ref=dst, send_sem=send.at[slot], recv_sem=recv,
                                                 device_id=sibling, device_id_type=MESH))

        @pl.when(i >= 2)
        def _():
            lc, rc = copies(i % 2, i - 2)
            lc.wait()
            rc.wait_send()

        acc = own_ref[...].astype(F32)
        for s in range(7):
            acc = acc + l_ref[s].astype(F32)
        buf[i % 2] = acc
        lc, rc = copies(i % 2, i)
        lc.start()
        rc.start()

        @pl.when(i == nsteps - 1)
        def _():
            for st in range(max(nsteps - 2, 0), nsteps):
                lc, rc = copies(st % 2, st)
                lc.wait()
                rc.wait_send()
            theirs = o_ref.at[pl.ds(pl.multiple_of((1 - c) * R, 8), R), :]
            pltpu.make_async_remote_copy(src_ref=theirs, dst_ref=theirs, send_sem=send.at[0], recv_sem=recv,
                                         device_id=sibling, device_id_type=MESH).wait_recv()

    return pl.pallas_call(
        body,
        name=name,
        grid=(nsteps,),
        in_specs=[pl.BlockSpec((tr, C), lambda i: (i, 0)), pl.BlockSpec((7, tr, C), lambda i: (0, i, 0))],
        out_specs=HBM,
        out_shape=_sds((2 * R, C), F32),
        scratch_shapes=[pltpu.VMEM((2, tr, C), F32), pltpu.SemaphoreType.DMA((2,)), pltpu.SemaphoreType.DMA((2,)),
                        pltpu.SemaphoreType.DMA(())],
        compiler_params=_cparams(("arbitrary",)),
    )(own2, l3)


def _all_reduce_small(v, dep):
    R, D = v.shape

    def body(v_ref, dep_ref, o_ref, land, send, recv):
        x, y, c = lax.axis_index("x"), lax.axis_index("y"), lax.axis_index("c")
        my_slot = 4 * x + 2 * y + c
        land[my_slot] = v_ref[...]
        for k, (fx, fy, fc) in enumerate(FLIPS):
            tx, ty, tc = x ^ fx, y ^ fy, c ^ fc
            pltpu.make_async_remote_copy(src_ref=v_ref, dst_ref=land.at[my_slot], send_sem=send.at[k], recv_sem=recv.at[k],
                                         device_id=(tx, ty, tc), device_id_type=MESH).start()
        for k, (fx, fy, fc) in enumerate(FLIPS):
            tx, ty, tc = x ^ fx, y ^ fy, c ^ fc
            cp = pltpu.make_async_remote_copy(src_ref=v_ref, dst_ref=land.at[4 * tx + 2 * ty + tc], send_sem=send.at[k],
                                              recv_sem=recv.at[k], device_id=(tx, ty, tc), device_id_type=MESH)
            cp.wait_send()
            cp.wait_recv()
        acc = land[0]
        for s in range(1, 8):
            acc = acc + land[s]
        o_ref[...] = acc

    return pl.pallas_call(
        body,
        name="all_reduce_small",
        in_specs=[pl.BlockSpec(memory_space=pltpu.VMEM), pl.BlockSpec(memory_space=pl.ANY)],
        out_specs=pl.BlockSpec(memory_space=pltpu.VMEM),
        out_shape=_sds((R, D), F32),
        scratch_shapes=[pltpu.VMEM((8, R, D), F32), pltpu.SemaphoreType.DMA((7,)), pltpu.SemaphoreType.DMA((7,))],
    )(v, dep)


def kernel(x, attn_w_in, attn_w_out, hgrn_w_in, hgrn_w_out, hgrn_norm_g, lb_logits, ln_mix_g, ln_mix_b, ln_ffn_g, ln_ffn_b, ffn_w_up, ffn_w_down, loss_target, m_attn_w_in, m_attn_w_out, m_hgrn_w_in, m_hgrn_w_out, m_hgrn_norm_g, m_lb_logits, m_ln_mix_g, m_ln_mix_b, m_ln_ffn_g, m_ln_ffn_b, m_ffn_w_up, m_ffn_w_down, v_attn_w_in, v_attn_w_out, v_hgrn_w_in, v_hgrn_w_out, v_hgrn_norm_g, v_lb_logits, v_ln_mix_g, v_ln_mix_b, v_ln_ffn_g, v_ln_ffn_b, v_ffn_w_up, v_ffn_w_down):
    xs = x[0]
    tgt = loss_target[0]
    S, D = xs.shape
    F = ffn_w_up.shape[2] * 4
    T1 = _pick(S, (1024, 512, 256))
    T2 = _pick(S, (2048, 1024, 512))
    TH = _pick(S, (512, 256))
    TB = _pick(S, (128,))
    TN = _pick(D, (512, 256, 128))
    TF = _pick(F, (1024, 512))
    TG = _pick(3 * D, (1536, 1024, 768))
    TW = _pick(F, (2048, 1024))

    cast = lambda w: w.astype(MXU_DTYPE)
    st_a, tok = _gather_start("gather_a", [cast(attn_w_in[0])], [1], jnp.zeros((8, LANES), F32), halves=True)
    tok, (xs_late, w_aout, w_fup, w_fdown, w_hin, w_hout) = lax.optimization_barrier(
        (tok, (xs, attn_w_out, ffn_w_up, ffn_w_down, hgrn_w_in, hgrn_w_out)))
    st_b, tok = _gather_start("gather_b", [cast(w_aout[0]), cast(w_fup[0]), cast(w_fdown[0])], [0, 1, 0], tok)
    st_c, tok = _gather_start("gather_c", [cast(w_hin[0]), cast(w_hout[0]), hgrn_norm_g, cast(w_fup[1]), cast(w_fdown[1])],
                              [1, 0, 1, 1, 0], tok)

    cos3, sin3 = _rope_tables(S)
    sel = _head_sel(D)
    sel_t = sel.T

    xc3 = _stack_classes("x_classes", xs_late, MXU_DTYPE)
    (wa_in,) = _gather_wait("gather_a_wait", st_a, tok, xc3, cos3, sin3)
    wa_in = _share_halves("share_a", wa_in, _pick(D // 2, (256, 128)))
    P3 = _attn_proj(xc3, wa_in, cos3, sin3, T2, TN)
    o3, lse3 = _attn_fwd(P3, D)
    o_att, L_att = _attn_mix(o3, lse3, sel)
    wa_out, w_up0, w_down0 = _gather_wait("gather_b_wait", st_b, L_att)
    x1, xm1, xh1, r1 = _mm_res_ln("attn_out_ln", o_att, wa_out, xs, ln_mix_g[0:1], ln_mix_b[0:1], TH, D)
    a0 = _mlp_up("mlp0_up", xm1, w_up0, T2, TF, D)
    x2, xm2, xh2, r2 = _mm_res_ln("mlp0_down_ln", a0, w_down0, x1, ln_ffn_g[0:1], ln_ffn_b[0:1], TH, F)

    wh_in, wh_out, norm_g, w_up1, w_down1 = _gather_wait("gather_c_wait", st_c, r2)
    P1 = _plain_mm("hgrn_proj", xm2, wh_in, "nn", F32, T1, _pick(3 * D, (1024, 768, 512)), D)
    o_h, n_h, states = _hgrn_fwd(P1, lb_logits, norm_g, TB)
    x3, xm3, xh3, r3 = _mm_res_ln("hgrn_out_ln", n_h, wh_out, x2, ln_mix_g[1:2], ln_mix_b[1:2], TH, D)
    a1 = _mlp_up("mlp1_up", xm3, w_up1, T2, TF, D)
    x4, _, xh4, r4 = _mm_res_ln("mlp1_down_ln", a1, w_down1, x3, ln_ffn_g[1:2], ln_ffn_b[1:2], TH, F)

    wgrad = lambda name, a, dy, tm, tn: _plain_mm(name, a, dy, "tn", MXU_DTYPE, tm, tn, T1)
    sq, du4, dum4, dg_ffn1, db_ffn1 = _loss_ln_bwd(x4, tgt, xh4, r4, ln_ffn_g[1:2], TH)
    dh1 = _mlp_down_bwd("mlp1_down_bwd", dum4, w_down1, a1, T2, TF, D)
    g_down1 = wgrad("g_down1", a1, dum4, TW, D)
    g_up1 = wgrad("g_up1", xm3, dh1, D, TW)
    sc_1, tok = _scatter_start("scatter_1", [g_down1, g_up1], [0, 1])
    du3, dum3, dg_mix1, db_mix1 = _mm_nt_res_ln_bwd("mlp1_up_bwd", dh1, w_up1, du4, xh3, r3, ln_mix_g[1:2], TH, F, tok)
    dn = _plain_mm("hgrn_out_bwd", dum3, wh_out, "nt", F32, T1, D, D)
    g_hout = wgrad("g_hgrn_out", n_h, dum3, D, D)
    dP1, dg_norm, dlb = _hgrn_bwd(P1, o_h, states, dn, lb_logits, norm_g, TB)
    dP1 = dP1.reshape(3 * S, D)
    g_hin = _matmul("g_hgrn_in", xm2, dP1, "tn", D, D, T1, [(_sds((D, 3 * D), MXU_DTYPE), _ij_spec(D, D))], _store_epilogue,
                    b_map=lambda i, j, k: (k + j * (S // T1), 0), mnk=(D, 3 * D, S))[0]
    d_lb_logits = _lb_logits_grad(dlb, lb_logits)
    sc_2, tok = _scatter_start("scatter_2", [g_hout, g_hin], [0, 1])

    du2, dum2, dg_ffn0, db_ffn0 = _mm_nt_res_ln_bwd("hgrn_in_bwd", dP1, wh_in, du3, xh2, r2, ln_ffn_g[0:1], TH, D, tok,
                                                    a_map=lambda i, j, k: (i + k * (S // TH), 0), mk=3 * D)
    dh0 = _mlp_down_bwd("mlp0_down_bwd", dum2, w_down0, a0, T2, TF, D)
    g_down0 = wgrad("g_down0", a0, dum2, TW, D)
    g_up0 = wgrad("g_up0", xm1, dh0, D, TW)
    sc_3, tok = _scatter_start("scatter_3", [g_down0, g_up0], [0, 1])
    du1, dum1, dg_mix0, db_mix0 = _mm_nt_res_ln_bwd("mlp0_up_bwd", dh0, w_up0, du2, xh1, r1, ln_mix_g[0:1], TH, F, tok)
    do, delta = _attn_out_bwd(dum1, wa_out, o_att, sel_t, TH, D)
    g_aout = wgrad("g_attn_out", o_att, dum1, D, D)
    sc_5, tok = _scatter_start("scatter_5", [g_aout], [0])
    dP3 = _attn_bwd(P3, _stack_classes("do_classes", do, MXU_DTYPE), _stack_classes("lse_classes", L_att, F32),
                    _stack_classes("delta_classes", delta, F32), cos3, sin3, D, tok)
    small = jnp.concatenate([d_lb_logits, dg_mix0, dg_mix1, db_mix0, db_mix1, dg_ffn0, dg_ffn1, db_ffn0, db_ffn1,
                             dg_norm, sq, jnp.zeros((4, D), F32)], axis=0)
    small = _all_reduce_small(small, dP3)
    loss = 0.5 * jnp.sum(small[11]) / D
    grp = lambda j: j // (3 * D // TG)
    g_ain = _matmul("g_attn_in", xc3, dP3, "tn", D, TG, T1, [(_sds((D, 9 * D), MXU_DTYPE), _ij_spec(D, TG))], _store_epilogue,
                    a_map=lambda i, j, k: (k + grp(j) * (S // T1), i),
                    b_map=lambda i, j, k: (k + grp(j) * (S // T1), j % (3 * D // TG)), mnk=(D, 9 * D, S), dep=small)[0]
    sc_4, tok = _scatter_start("scatter_4", [g_ain], [1])
    dxc3 = _matmul("attn_in_bwd", dP3, wa_in, "nt", TH, D, 3 * D, [(_sds((3 * S, D), F32), _ij_spec(TH, D))], _store_epilogue,
                   b_map=lambda i, j, k: (j, k + i // (S // TH)), mnk=(3 * S, D, 3 * D), dep=tok)[0]
    grad_x = _input_grad(du1, dxc3)

    def reduced(name, state, *after):
        gs, lands = _scatter_wait(name + "_wait", state, *after)
        return [_reduce_join(f"{name}_reduce_{i}", l, _own_piece(g, ax)) for i, (l, g, ax) in enumerate(zip(lands, gs, state[4]))]

    r_down1, r_up1 = reduced("scatter_1", sc_1, grad_x)
    r_hout, r_hin = reduced("scatter_2", sc_2, r_up1)
    r_down0, r_up0 = reduced("scatter_3", sc_3, r_hin)
    (r_aout,) = reduced("scatter_5", sc_5, r_up0)

    my_chip = 2 * lax.axis_index("x") + lax.axis_index("y")
    nsh = hgrn_norm_g.shape[1]
    g_norm = lax.dynamic_slice(small[10:11], (0, my_chip * nsh), (1, nsh))

    grads, upd = {}, {}

    def update(nm, w, gs, m, v):
        upd[nm] = _adamw("adamw_" + nm, w, gs, m, v)
        grads[nm] = upd[nm][3]

    update("hgrn_w_in", hgrn_w_in, [r_hin], m_hgrn_w_in, v_hgrn_w_in)
    update("hgrn_w_out", hgrn_w_out, [r_hout], m_hgrn_w_out, v_hgrn_w_out)
    update("ffn_w_up", ffn_w_up, [r_up0, r_up1], m_ffn_w_up, v_ffn_w_up)
    update("ffn_w_down", ffn_w_down, [r_down0, r_down1], m_ffn_w_down, v_ffn_w_down)
    update("attn_w_out", attn_w_out, [r_aout], m_attn_w_out, v_attn_w_out)
    update("hgrn_norm_g", hgrn_norm_g, [g_norm], m_hgrn_norm_g, v_hgrn_norm_g)
    cat = lambda ts: jnp.concatenate(ts, axis=0)
    small_w = cat([lb_logits, ln_mix_g, ln_mix_b, ln_ffn_g, ln_ffn_b])
    small_m = cat([m_lb_logits, m_ln_mix_g, m_ln_mix_b, m_ln_ffn_g, m_ln_ffn_b])
    small_v = cat([v_lb_logits, v_ln_mix_g, v_ln_mix_b, v_ln_ffn_g, v_ln_ffn_b])
    small_upd = _adamw("adamw_small", small_w, [small[0:10]], small_m, small_v)
    for i, nm in enumerate(["lb_logits", "ln_mix_g", "ln_mix_b", "ln_ffn_g", "ln_ffn_b"]):
        grads[nm] = small[2 * i:2 * i + 2]
        upd[nm] = tuple(t[2 * i:2 * i + 2] for t in small_upd)
    done = [upd[k][2] for k in ("hgrn_w_in", "hgrn_w_out", "ffn_w_up", "ffn_w_down", "attn_w_out", "hgrn_norm_g")]
    (r_ain,) = reduced("scatter_4", sc_4, small_upd[2], *done)
    update("attn_w_in", attn_w_in, [r_ain], m_attn_w_in, v_attn_w_in)

    order = ["attn_w_in", "attn_w_out", "hgrn_w_in", "hgrn_w_out", "hgrn_norm_g", "lb_logits", "ln_mix_g", "ln_mix_b",
             "ln_ffn_g", "ln_ffn_b", "ffn_w_up", "ffn_w_down"]
    return (loss, grad_x[None], *[grads[k] for k in order], *[upd[k][0] for k in order],
            *[upd[k][1] for k in order], *[upd[k][2] for k in order])
```

```python
import math

import jax
import jax.numpy as jnp
from jax import lax
from jax.experimental import pallas as pl
from jax.experimental.pallas import tpu as pltpu

F32 = jnp.float32
BF16 = jnp.bfloat16
MXU_DTYPE = BF16

HEAD_DIM = 64
ATTN_BLK = 128
DILATIONS = (1, 4, 16)
ROPE_THETA = 10000.0
HGRN_DK = 128
HGRN_CHUNK = 64
DEPTH = 2
LN_EPS = 1e-5
RMS_EPS = 1e-6
ALPHA = (2 * DEPTH) ** 0.25
ADAM_LR, ADAM_B1, ADAM_B2, ADAM_EPS, ADAM_WD, ADAM_STEP = 0.001, 0.9, 0.999, 1e-08, 0.01, 10

LANES = 128
VMEM_LIMIT = 56 * 1024 * 1024
NEG = -1e30
MESH = pl.DeviceIdType.MESH


def _cparams(sem=None):
    return pltpu.CompilerParams(dimension_semantics=sem, vmem_limit_bytes=VMEM_LIMIT)


def _sds(shape, dtype):
    return jax.ShapeDtypeStruct(tuple(shape), dtype)


def _dg(a, b, ca, cb):
    return lax.dot_general(a, b, (((ca,), (cb,)), ((), ())), preferred_element_type=F32)


def _nn(a, b):
    return _dg(a, b, 1, 0)


def _nt(a, b):
    return _dg(a, b, 1, 1)


def _tn(a, b):
    return _dg(a, b, 0, 0)


def _split3(a):
    hi = a.astype(BF16)
    r = a - hi.astype(F32)
    mid = r.astype(BF16)
    lo = (r - mid.astype(F32)).astype(BF16)
    return hi, mid, lo


def _exact_nn(a, sel):
    hi, mid, lo = _split3(a)
    return _nn(hi, sel) + _nn(mid, sel) + _nn(lo, sel)


def _pick(n, prefs):
    for p in prefs:
        if n % p == 0:
            return p
    return n


def _matmul(name, a, b, form, tm, tn, tk, outs, epilogue, extras=(), a_map=None, b_map=None, mnk=None, dep=None,
            sem=("parallel", "parallel", "arbitrary"), split=None):
    if form == "nn":
        (M, K), N = a.shape, b.shape[1]
        a_spec = pl.BlockSpec((tm, tk), a_map or (lambda i, j, k: (i, k)))
        b_spec = pl.BlockSpec((tk, tn), b_map or (lambda i, j, k: (k, j)))
        ca, cb = 1, 0
    elif form == "nt":
        (M, K), N = a.shape, b.shape[0]
        a_spec = pl.BlockSpec((tm, tk), a_map or (lambda i, j, k: (i, k)))
        b_spec = pl.BlockSpec((tn, tk), b_map or (lambda i, j, k: (j, k)))
        ca, cb = 1, 1
    else:
        (K, M), N = a.shape, b.shape[1]
        a_spec = pl.BlockSpec((tk, tm), a_map or (lambda i, j, k: (k, i)))
        b_spec = pl.BlockSpec((tk, tn), b_map or (lambda i, j, k: (k, j)))
        ca, cb = 0, 0
    if mnk is not None:
        M, N, K = mnk
    assert M % tm == 0 and N % tn == 0 and K % tk == 0, (name, M, N, K, tm, tn, tk)
    nk = K // tk
    ne, no = len(extras), len(outs)
    deps = [] if dep is None else [dep]
    nd = len(deps)

    def body(a_ref, b_ref, *rest):
        extra_refs, out_refs = rest[:ne], rest[ne + nd:ne + nd + no]
        j = pl.program_id(1)
        if split is not None:
            kind, n = split
            assert nk == 1 and form != "tn"
            tiled = [t for _, _, *t in list(extras) + list(outs)]
            refs = list(extra_refs) + list(out_refs)
            for ci in range(n):
                if kind == "cols":
                    cs = slice(ci * (tn // n), (ci + 1) * (tn // n))
                    part = _dg(a_ref[...].astype(MXU_DTYPE), (b_ref[:, cs] if form == "nn" else b_ref[cs, :]).astype(MXU_DTYPE), ca, cb)
                    view = [r.at[:, cs] if t else r for r, t in zip(refs, tiled)]
                else:
                    rs = slice(ci * (tm // n), (ci + 1) * (tm // n))
                    part = _dg(a_ref[rs, :].astype(MXU_DTYPE), b_ref[...].astype(MXU_DTYPE), ca, cb)
                    view = [r.at[rs, :] if t else r for r, t in zip(refs, tiled)]
                epilogue(part, view[:ne], view[ne:], j, ci)
            return
        part = _dg(a_ref[...].astype(MXU_DTYPE), b_ref[...].astype(MXU_DTYPE), ca, cb)
        if nk == 1:
            epilogue(part, extra_refs, out_refs, j, 0)
            return
        acc_ref = rest[-1]
        k = pl.program_id(2)

        @pl.when(k == 0)
        def _():
            acc_ref[...] = part

        @pl.when(k > 0)
        def _():
            acc_ref[...] += part

        @pl.when(k == nk - 1)
        def _():
            epilogue(acc_ref[...], extra_refs, out_refs, j, 0)

    res = pl.pallas_call(
        body,
        name=name,
        grid=(M // tm, N // tn, nk),
        in_specs=[a_spec, b_spec] + [s for _, s, *_ in extras] + [pl.BlockSpec(memory_space=pl.ANY)] * nd,
        out_specs=[s for _, s, *_ in outs],
        out_shape=[o for o, *_ in outs],
        scratch_shapes=[pltpu.VMEM((tm, tn), F32)] if nk > 1 else [],
        compiler_params=_cparams(sem),
    )(a, b, *[e for e, *_ in extras], *deps)
    return res


def _ij_spec(tm, tn):
    return pl.BlockSpec((tm, tn), lambda i, j, k: (i, j))


def _store_epilogue(acc, extra_refs, out_refs, j, ci):
    out_refs[0][...] = acc.astype(out_refs[0].dtype)


def _plain_mm(name, a, b, form, out_dtype, tm, tn, tk):
    M = a.shape[1] if form == "tn" else a.shape[0]
    N = b.shape[0] if form == "nt" else b.shape[1]
    return _matmul(name, a, b, form, tm, tn, tk, [(_sds((M, N), out_dtype), _ij_spec(tm, tn))], _store_epilogue)[0]


def _class_slabs(S):
    assert DILATIONS[0] == 1
    return [(g, d, r, S // d) for g, d in enumerate(DILATIONS) if d > 1 for r in range(d)]


def _stack_classes(name, t, out_dtype):
    S, W = t.shape

    def body(x_ref, o_ref):
        o_ref[0:S, :] = x_ref[...].astype(out_dtype)
        for g, d, r, n in _class_slabs(S):
            o_ref[g * S + r * n:g * S + (r + 1) * n, :] = x_ref[pl.ds(r, n, stride=d), :].astype(out_dtype)

    return pl.pallas_call(
        body,
        name=name,
        grid=(W // LANES,),
        in_specs=[pl.BlockSpec((S, LANES), lambda j: (0, j))],
        out_specs=pl.BlockSpec((3 * S, LANES), lambda j: (0, j)),
        out_shape=_sds((3 * S, W), out_dtype),
        compiler_params=_cparams(("parallel",)),
    )(t)


def _rope_tables(seq):
    half = HEAD_DIM // 2
    inv = ROPE_THETA ** (-jnp.arange(half, dtype=F32) * (2.0 / HEAD_DIM))
    inv = jnp.tile(inv, LANES // half)
    pos = []
    for d in DILATIONS:
        row = jnp.arange(seq)
        pos.append((row % (seq // d)) * d + row // (seq // d))
    ang = jnp.concatenate(pos).astype(F32)[:, None] * inv[None, :]
    first = (jnp.arange(LANES) % HEAD_DIM) < half
    sin = jnp.sin(ang)
    return jnp.cos(ang), jnp.where(first[None, :], -sin, sin)


def _partner(x):
    half = HEAD_DIM // 2
    lane = lax.broadcasted_iota(jnp.int32, x.shape, 1)
    first = (lane % HEAD_DIM) < half
    return jnp.where(first, pltpu.roll(x, LANES - half, 1), pltpu.roll(x, half, 1))


def _attn_proj(x3, w_full, cos3, sin3, tm, tn):
    S3, D = x3.shape
    S = S3 // 3
    per_part = D // tn
    per_group = 3 * per_part

    def epilogue(acc, extra_refs, out_refs, j, ci):
        cos_ref, sin_ref = extra_refs
        o_ref = out_refs[0]
        is_rot = j // per_part < 2
        c = jnp.where(is_rot, cos_ref[...], 1.0)
        s = jnp.where(is_rot, sin_ref[...], 0.0)
        for t in range(acc.shape[1] // LANES):
            xs = acc[:, t * LANES:(t + 1) * LANES]
            o_ref[:, t * LANES:(t + 1) * LANES] = (xs * c + _partner(xs) * s).astype(o_ref.dtype)

    tab = pl.BlockSpec((tm, LANES), lambda i, j, k: (i, 0))
    return _matmul("attn_proj", x3, w_full, "nn", tm, tn, D, [(_sds((S3, 3 * D), MXU_DTYPE), _ij_spec(tm, tn), True)],
                   epilogue, extras=[(cos3, tab), (sin3, tab)],
                   b_map=lambda i, j, k: (k, j + (i // (S // tm)) * per_group), mnk=(S3, 3 * D, D),
                   split=("cols", tn // (2 * LANES)))[0]


def _head_sel(d_model):
    h = jnp.arange(LANES)[:, None]
    l = jnp.arange(d_model)[None, :]
    return (l // HEAD_DIM == h).astype(BF16)


def _class_edges(b, nblk):
    g = b // nblk
    per_class = jnp.where(g == 0, nblk // DILATIONS[0], jnp.where(g == 1, nblk // DILATIONS[1], nblk // DILATIONS[2]))
    pos = (b % nblk) % per_class
    return pos != 0, pos != per_class - 1


def _two_heads(t, top):
    zero = jnp.zeros_like(t)
    return jnp.concatenate([jnp.where(top, t, zero), jnp.where(top, zero, t)], axis=0)


def _band_mask(has_prev):
    B = ATTN_BLK
    row = lax.broadcasted_iota(jnp.int32, (2 * B, 2 * B), 0) % B
    col = lax.broadcasted_iota(jnp.int32, (2 * B, 2 * B), 1)
    in_prev = jnp.logical_and(jnp.logical_and(col < B, col >= row), has_prev)
    in_own = jnp.logical_and(col >= B, col - B <= row)
    return jnp.logical_or(in_prev, in_own)


def _attn_fwd(P3, D):
    S3 = P3.shape[0]
    B = ATTN_BLK
    nblk = S3 // 3 // B
    npairs = D // LANES
    scale = HEAD_DIM ** -0.5

    def body(q_ref, kc_ref, vc_ref, kp_ref, vp_ref, o_ref, lse_ref):
        has_prev, _ = _class_edges(pl.program_id(0), nblk)
        ok = _band_mask(has_prev)
        lane = lax.broadcasted_iota(jnp.int32, (B, LANES), 1)
        top = lane < HEAD_DIM
        lse_acc = jnp.zeros((B, LANES), F32)
        for j in range(npairs):
            sl = slice(j * LANES, (j + 1) * LANES)
            Q = _two_heads(q_ref[:, sl] * scale, top)
            K2 = jnp.concatenate([kp_ref[:, sl], kc_ref[:, sl]], axis=0)
            V2 = jnp.concatenate([vp_ref[:, sl], vc_ref[:, sl]], axis=0)
            s = jnp.where(ok, _nt(Q, K2), NEG)
            m = jnp.max(s, axis=1, keepdims=True)
            p = jnp.exp(s - m)
            l = jnp.sum(p, axis=1, keepdims=True)
            o = _nn((p * (1.0 / l)).astype(MXU_DTYPE), V2)
            o_ref[:, sl] = jnp.where(top, o[:B], o[B:])
            lse = m + jnp.log(l)
            lse_acc = jnp.where(lane == 2 * j, lse[:B], jnp.where(lane == 2 * j + 1, lse[B:], lse_acc))
        lse_ref[...] = lse_acc

    blk = lambda part, prev: pl.BlockSpec(
        (B, D), (lambda b: (jnp.maximum(b - 1, 0), part)) if prev else (lambda b: (b, part)))
    return pl.pallas_call(
        body,
        name="attn_fwd",
        grid=(3 * nblk,),
        in_specs=[blk(0, False), blk(1, False), blk(2, False), blk(1, True), blk(2, True)],
        out_specs=[pl.BlockSpec((B, D), lambda b: (b, 0)), pl.BlockSpec((B, LANES), lambda b: (b, 0))],
        out_shape=[_sds((S3, D), F32), _sds((S3, LANES), F32)],
        compiler_params=_cparams(("parallel",)),
    )(P3, P3, P3, P3, P3)


def _attn_mix(o3, lse3, sel):
    S3, D = o3.shape
    S = S3 // 3

    def body(o3_ref, lse_ref, sel_ref, o_ref, L_ref, w_ref):
        @pl.when(pl.program_id(0) == 0)
        def _():
            w_ref[0] = lse_ref[0:S, :]
            for g, d, r, n in _class_slabs(S):
                w_ref[g, pl.ds(r, n, stride=d), :] = lse_ref[g * S + r * n:g * S + (r + 1) * n, :]
            a, b, c = w_ref[0], w_ref[1], w_ref[2]
            m = jnp.maximum(jnp.maximum(a, b), c)
            L = m + jnp.log(jnp.exp(a - m) + jnp.exp(b - m) + jnp.exp(c - m))
            L_ref[...] = L
            w_ref[0] = jnp.exp(a - L)
            w_ref[1] = jnp.exp(b - L)
            w_ref[2] = jnp.exp(c - L)

        s = sel_ref[...]
        o_ref[...] = _exact_nn(w_ref[0], s) * o3_ref[0:S, :]
        for g, d, r, n in _class_slabs(S):
            rows = pl.ds(r, n, stride=d)
            o_ref[rows, :] += _exact_nn(w_ref[g, rows, :], s) * o3_ref[g * S + r * n:g * S + (r + 1) * n, :]

    return pl.pallas_call(
        body,
        name="attn_mix",
        grid=(D // LANES,),
        in_specs=[pl.BlockSpec((S3, LANES), lambda j: (0, j)), pl.BlockSpec((S3, LANES), lambda j: (0, 0)),
                  pl.BlockSpec((LANES, LANES), lambda j: (0, j))],
        out_specs=[pl.BlockSpec((S, LANES), lambda j: (0, j)), pl.BlockSpec((S, LANES), lambda j: (0, 0))],
        out_shape=[_sds((S, D), F32), _sds((S, LANES), F32)],
        scratch_shapes=[pltpu.VMEM((3, S, LANES), F32)],
        compiler_params=_cparams(("arbitrary",)),
    )(o3, lse3, sel)


def _attn_bwd(P3, do3, L3, delta3, cos3, sin3, D, dep):
    S3 = P3.shape[0]
    B = ATTN_BLK
    nblk = S3 // 3 // B
    npairs = D // LANES
    scale = HEAD_DIM ** -0.5

    def body(c_ref, kp_ref, vp_ref, qn_ref, doc_ref, don_ref, Lc_ref, Ln_ref, dc_ref, dn_ref, cos_ref, sin_ref, dep_ref, out_ref):
        has_prev, has_next = _class_edges(pl.program_id(0), nblk)
        ok = _band_mask(has_prev)
        row = lax.broadcasted_iota(jnp.int32, (2 * B, B), 0) % B
        col = lax.broadcasted_iota(jnp.int32, (2 * B, B), 1)
        ok_n = jnp.logical_and(col >= row, has_next)
        lane = lax.broadcasted_iota(jnp.int32, (B, LANES), 1)
        top = lane < HEAD_DIM
        cos_t = cos_ref[...]
        sin_inv = -sin_ref[...]
        Lc_all, Ln_all, dc_all, dn_all = Lc_ref[...], Ln_ref[...], dc_ref[...], dn_ref[...]
        pair_col = lambda t, j: jnp.concatenate([t[:, 2 * j:2 * j + 1], t[:, 2 * j + 1:2 * j + 2]], axis=0)
        for j in range(npairs):
            sl = lambda part: slice(part * D + j * LANES, part * D + (j + 1) * LANES)
            pj = slice(j * LANES, (j + 1) * LANES)
            kc2, vc2 = c_ref[:, sl(1)], c_ref[:, sl(2)]
            K2 = jnp.concatenate([kp_ref[:, pj], kc2], axis=0)
            V2 = jnp.concatenate([vp_ref[:, pj], vc2], axis=0)
            Qc = _two_heads(c_ref[:, sl(0)] * scale, top)
            Qn = _two_heads(qn_ref[:, pj] * scale, top)
            DOc = _two_heads(doc_ref[:, pj].astype(MXU_DTYPE), top)
            DOn = _two_heads(don_ref[:, pj].astype(MXU_DTYPE), top)
            P_c = jnp.where(ok, jnp.exp(_nt(Qc, K2) - pair_col(Lc_all, j)), 0.0)
            dS_c = P_c * (_nt(DOc, V2) - pair_col(dc_all, j))
            P_n = jnp.where(ok_n, jnp.exp(_nt(Qn, kc2) - pair_col(Ln_all, j)), 0.0)
            dS_n = P_n * (_nt(DOn, vc2) - pair_col(dn_all, j))
            dq = _nn(dS_c.astype(MXU_DTYPE), K2)
            dq2 = jnp.where(top, dq[:B], dq[B:]) * scale
            Qk = jnp.concatenate([Qc, Qn], axis=0)
            DOk = jnp.concatenate([DOc, DOn], axis=0)
            dk2 = _tn(jnp.concatenate([dS_c[:, B:], dS_n], axis=0).astype(MXU_DTYPE), Qk)
            dv2 = _tn(jnp.concatenate([P_c[:, B:], P_n], axis=0).astype(MXU_DTYPE), DOk)
            out_ref[:, sl(0)] = (dq2 * cos_t + _partner(dq2) * sin_inv).astype(out_ref.dtype)
            out_ref[:, sl(1)] = (dk2 * cos_t + _partner(dk2) * sin_inv).astype(out_ref.dtype)
            out_ref[:, sl(2)] = dv2.astype(out_ref.dtype)

    cur = lambda b: b
    prv = lambda b: jnp.maximum(b - 1, 0)
    nxt = lambda b: jnp.minimum(b + 1, 3 * nblk - 1)
    spec = lambda w, f, part=0: pl.BlockSpec((B, w), lambda b: (f(b), part))
    return pl.pallas_call(
        body,
        name="attn_bwd",
        grid=(3 * nblk,),
        in_specs=[spec(3 * D, cur), spec(D, prv, 1), spec(D, prv, 2), spec(D, nxt, 0), spec(D, cur), spec(D, nxt),
                  spec(LANES, cur), spec(LANES, nxt), spec(LANES, cur), spec(LANES, nxt), spec(LANES, cur), spec(LANES, cur),
                  pl.BlockSpec(memory_space=pl.ANY)],
        out_specs=spec(3 * D, cur),
        out_shape=_sds((S3, 3 * D), MXU_DTYPE),
        compiler_params=_cparams(("parallel",)),
    )(P3, P3, P3, P3, do3, do3, L3, L3, delta3, delta3, cos3, sin3, dep)


def _input_grad(du, dx3):
    S, D = du.shape

    def body(du_ref, dx_ref, o_ref):
        o_ref[...] = ALPHA * du_ref[...] + dx_ref[0:S, :]
        for g, d, r, n in _class_slabs(S):
            o_ref[pl.ds(r, n, stride=d), :] += dx_ref[g * S + r * n:g * S + (r + 1) * n, :]

    return pl.pallas_call(
        body,
        name="input_grad",
        grid=(D // LANES,),
        in_specs=[pl.BlockSpec((S, LANES), lambda j: (0, j)), pl.BlockSpec((3 * S, LANES), lambda j: (0, j))],
        out_specs=pl.BlockSpec((S, LANES), lambda j: (0, j)),
        out_shape=_sds((S, D), F32),
        compiler_params=_cparams(("parallel",)),
    )(du, dx3)


def _chunk_causal(tb):
    r = lax.broadcasted_iota(jnp.int32, (tb, tb), 0)
    c = lax.broadcasted_iota(jnp.int32, (tb, tb), 1)
    return jnp.logical_and((r // HGRN_CHUNK) == (c // HGRN_CHUNK), r >= c)


def _chunk_sums(a, lower):
    C = HGRN_CHUNK
    r = lax.broadcasted_iota(jnp.int32, (C, C), 0)
    c = lax.broadcasted_iota(jnp.int32, (C, C), 1)
    tri = ((r >= c) if lower else (r <= c)).astype(BF16)
    parts = _split3(a)
    out = []
    for ci in range(a.shape[0] // C):
        rows = slice(ci * C, (ci + 1) * C)
        out.append(_nn(tri, parts[0][rows]) + _nn(tri, parts[1][rows]) + _nn(tri, parts[2][rows]))
    return jnp.concatenate(out, axis=0)


def _chunk_last(b):
    C = HGRN_CHUNK
    return jnp.concatenate([jnp.broadcast_to(b[(ci + 1) * C - 1:(ci + 1) * C, :], (C, b.shape[1]))
                            for ci in range(b.shape[0] // C)], axis=0)


def _lower_bound(lb_ref):
    l0, l1 = lb_ref[0:1, :], lb_ref[1:2, :]
    m = jnp.maximum(l0, l1)
    e0, e1 = jnp.exp(l0 - m), jnp.exp(l1 - m)
    return e1 / (e0 + e1)


def _hgrn_gates(q_raw, z, lb):
    sg = 1.0 / (1.0 + jnp.exp(-z))
    sn = 1.0 / (1.0 + jnp.exp(z))
    f = lb + (1.0 - lb) * sg
    key = (1.0 - lb) * sn
    sq = 1.0 / (1.0 + jnp.exp(-q_raw))
    return sg, sn, f, key, sq


HGRN_HEADS_PER_STEP = 8


def _hgrn_fwd(P1, lb_logits, norm_g, tb):
    S = P1.shape[0]
    D = P1.shape[1] // 3
    K = HGRN_DK
    H = D // K
    HP = min(HGRN_HEADS_PER_STEP, H)
    C = HGRN_CHUNK
    cpb = tb // C
    nt = S // tb

    def body(q_ref, f_ref, i_ref, lb_ref, g_ref, o_ref, n_ref, st_ref, state):
        t = pl.program_id(1)

        @pl.when(t == 0)
        def _():
            state[...] = jnp.zeros_like(state)

        lb_all = _lower_bound(lb_ref)
        low = _chunk_causal(tb)
        for hh in range(HP):
            lanes = slice(hh * K, (hh + 1) * K)
            q_raw, z, v = q_ref[:, lanes], f_ref[:, lanes], i_ref[:, lanes]
            sg, sn, f, key, sq = _hgrn_gates(q_raw, z, lb_all[:, lanes])
            b = _chunk_sums(jnp.log(f), lower=True)
            qd = (q_raw * sq * jnp.exp(b)).astype(MXU_DTYPE)
            kd = (key * jnp.exp(-b)).astype(MXU_DTYPE)
            kb = (key * jnp.exp(_chunk_last(b) - b)).astype(MXU_DTYPE)
            vm = v.astype(MXU_DTYPE)
            a = jnp.where(low, _nt(qd, kd), 0.0).astype(MXU_DTYPE)
            o_intra = _nn(a, vm)
            st = state[hh]
            outs = []
            for ci in range(cpb):
                rows = slice(ci * C, (ci + 1) * C)
                st_ref[hh, ci] = st
                outs.append(o_intra[rows] + _nt(qd[rows], st.astype(MXU_DTYPE)))
                st = st * jnp.exp(b[(ci + 1) * C - 1:(ci + 1) * C, :]) + _tn(vm[rows], kb[rows])
            state[hh] = st
            o = jnp.concatenate(outs, axis=0)
            o_ref[:, lanes] = o
            rs = lax.rsqrt(jnp.mean(o * o, axis=1, keepdims=True) + RMS_EPS)
            n_ref[:, lanes] = o * rs * g_ref[:, lanes]

    tok = lambda part: pl.BlockSpec((tb, HP * K), lambda h, t: (t, part * (H // HP) + h))
    vec = lambda rows: pl.BlockSpec((rows, HP * K), lambda h, t: (0, h))
    return pl.pallas_call(
        body,
        name="hgrn_fwd",
        grid=(H // HP, nt),
        in_specs=[tok(0), tok(1), tok(2), vec(2), vec(1)],
        out_specs=[tok(0), tok(0), pl.BlockSpec((HP, cpb, K, K), lambda h, t: (h, t, 0, 0))],
        out_shape=[_sds((S, D), F32), _sds((S, D), F32), _sds((H, S // C, K, K), F32)],
        scratch_shapes=[pltpu.VMEM((HP, K, K), F32)],
        compiler_params=_cparams(("parallel", "arbitrary")),
    )(P1, P1, P1, lb_logits, norm_g)


def _hgrn_bwd(P1, o_pre, states, dn, lb_logits, norm_g, tb):
    S = P1.shape[0]
    D = P1.shape[1] // 3
    K = HGRN_DK
    H = D // K
    HP = min(HGRN_HEADS_PER_STEP, H)
    C = HGRN_CHUNK
    cpb = tb // C
    nt = S // tb

    def body(q_ref, f_ref, i_ref, o_ref, st_ref, dn_ref, lb_ref, g_ref, d_ref, dg_ref, dlb_ref, dstate):
        t = pl.program_id(1)

        @pl.when(t == 0)
        def _():
            dstate[...] = jnp.zeros_like(dstate)
            dg_ref[...] = jnp.zeros_like(dg_ref)
            dlb_ref[...] = jnp.zeros_like(dlb_ref)

        lb_all = _lower_bound(lb_ref)
        low = _chunk_causal(tb)
        for hh in range(HP):
            lanes = slice(hh * K, (hh + 1) * K)
            lb = lb_all[:, lanes]
            gn = g_ref[:, lanes]
            q_raw, z, v = q_ref[:, lanes], f_ref[:, lanes], i_ref[:, lanes]
            sg, sn, f, key, sq = _hgrn_gates(q_raw, z, lb)
            b = _chunk_sums(jnp.log(f), lower=True)
            e_pos, e_neg, e_rel = jnp.exp(b), jnp.exp(-b), jnp.exp(_chunk_last(b) - b)
            qd_f, kd_f, kb_f = q_raw * sq * e_pos, key * e_neg, key * e_rel
            qd, kd, kb = qd_f.astype(MXU_DTYPE), kd_f.astype(MXU_DTYPE), kb_f.astype(MXU_DTYPE)
            vm = v.astype(MXU_DTYPE)
            a = jnp.where(low, _nt(qd, kd), 0.0).astype(MXU_DTYPE)
            o = o_ref[:, lanes]
            dnn = dn_ref[:, lanes]
            rs = lax.rsqrt(jnp.mean(o * o, axis=1, keepdims=True) + RMS_EPS)
            dg_ref[:, lanes] += jnp.sum(dnn * o * rs, axis=0, keepdims=True)
            tg = dnn * gn
            dom = (rs * tg - o * (rs * rs * rs) * jnp.mean(tg * o, axis=1, keepdims=True)).astype(MXU_DTYPE)
            da = jnp.where(low, _nt(dom, vm), 0.0).astype(MXU_DTYPE)
            dv = _tn(a, dom)
            dqd = _nn(da, kd)
            dkd = _tn(da, qd)
            dst = dstate[hh]
            dv_s, dqd_s, dkb_s, dbl_s = [None] * cpb, [None] * cpb, [None] * cpb, [None] * cpb
            for ci in reversed(range(cpb)):
                rows = slice(ci * C, (ci + 1) * C)
                st = st_ref[hh, ci]
                dstm = dst.astype(MXU_DTYPE)
                dec = jnp.exp(b[(ci + 1) * C - 1:(ci + 1) * C, :])
                dv_s[ci] = _nt(kb[rows], dstm)
                dkb_s[ci] = _nn(vm[rows], dstm)
                dqd_s[ci] = _nn(dom[rows], st.astype(MXU_DTYPE))
                db_last = jnp.sum(dkb_s[ci] * kb_f[rows], axis=0, keepdims=True) + jnp.sum(dst * st, axis=0, keepdims=True) * dec
                dbl_s[ci] = jnp.broadcast_to(db_last, (C, K))
                dst = dst * dec + _tn(dom[rows], qd[rows])
            dstate[hh] = dst
            dv = dv + jnp.concatenate(dv_s, axis=0)
            dqd = dqd + jnp.concatenate(dqd_s, axis=0)
            dkb = jnp.concatenate(dkb_s, axis=0)
            dkey = dkd * e_neg + dkb * e_rel
            db = dqd * qd_f - dkd * kd_f - dkb * kb_f
            dlogf = _chunk_sums(db, lower=False) + jnp.concatenate(dbl_s, axis=0)
            gz = (1.0 - lb) * sg * sn
            d_ref[0, :, lanes] = (dqd * e_pos * (sq + q_raw * sq * (1.0 - sq))).astype(d_ref.dtype)
            d_ref[1, :, lanes] = (dlogf * gz / f - dkey * gz).astype(d_ref.dtype)
            d_ref[2, :, lanes] = dv.astype(d_ref.dtype)
            dlb_ref[:, lanes] += jnp.sum(dlogf * sn / f - dkey * sn, axis=0, keepdims=True)

    rev = lambda t: nt - 1 - t
    tok = lambda part: pl.BlockSpec((tb, HP * K), lambda h, t: (rev(t), part * (H // HP) + h))
    vec = lambda rows: pl.BlockSpec((rows, HP * K), lambda h, t: (0, h))
    outs = pl.pallas_call(
        body,
        name="hgrn_bwd",
        grid=(H // HP, nt),
        in_specs=[tok(0), tok(1), tok(2), tok(0),
                  pl.BlockSpec((HP, cpb, K, K), lambda h, t: (h, rev(t), 0, 0)),
                  tok(0), vec(2), vec(1)],
        out_specs=[pl.BlockSpec((3, tb, HP * K), lambda h, t: (0, rev(t), h)), vec(1), vec(1)],
        out_shape=[_sds((3, S, D), MXU_DTYPE)] + [_sds((1, D), F32)] * 2,
        scratch_shapes=[pltpu.VMEM((HP, K, K), F32)],
        compiler_params=_cparams(("parallel", "arbitrary")),
    )(P1, P1, P1, o_pre, states, dn, lb_logits, norm_g)
    return outs


def _lb_logits_grad(dlb, lb_logits):
    def body(d_ref, l_ref, o_ref):
        s1 = _lower_bound(l_ref)
        d = d_ref[...]
        o_ref[0:1, :] = -(1.0 - s1) * s1 * d
        o_ref[1:2, :] = s1 * (1.0 - s1) * d

    return pl.pallas_call(body, name="lb_logits_grad", out_shape=_sds(lb_logits.shape, F32))(dlb, lb_logits)


def _ln_epilogue(acc, extra_refs, out_refs, j, ci):
    res_ref, g_ref, b_ref = extra_refs
    x_ref, xm_ref, xhat_ref, rstd_ref = out_refs
    u = ALPHA * res_ref[...] + acc
    mu = jnp.mean(u, axis=1, keepdims=True)
    cen = u - mu
    rstd = lax.rsqrt(jnp.mean(cen * cen, axis=1, keepdims=True) + LN_EPS)
    xhat = cen * rstd
    xhat_ref[...] = xhat
    x = xhat * g_ref[...] + b_ref[...]
    x_ref[...] = x
    xm_ref[...] = x.astype(xm_ref.dtype)
    rstd_ref[...] = rstd


def _mm_res_ln(name, a, w_full, res, g, b, tm, tk):
    S, D = res.shape
    row = pl.BlockSpec((tm, D), lambda i, j, k: (i, 0))
    vec = pl.BlockSpec((1, D), lambda i, j, k: (0, 0))
    outs = [(_sds((S, D), F32), row, True), (_sds((S, D), MXU_DTYPE), row, True), (_sds((S, D), F32), row, True),
            (_sds((S, 1), F32), pl.BlockSpec((tm, 1), lambda i, j, k: (i, 0)), True)]
    return _matmul(name, a, w_full, "nn", tm, D, tk, outs, _ln_epilogue, extras=[(res, row, True), (g, vec), (b, vec)],
                   split=("rows", 2) if tk == a.shape[1] else None)


def _ln_bwd_rows(dy, xh, rstd, g, first, du_ref, dum_ref, dg_ref, db_ref):
    if first is not None:
        @pl.when(first)
        def _():
            dg_ref[...] = jnp.zeros_like(dg_ref)
            db_ref[...] = jnp.zeros_like(db_ref)

    dg_ref[...] += jnp.sum(dy * xh, axis=0, keepdims=True)
    db_ref[...] += jnp.sum(dy, axis=0, keepdims=True)
    dxh = dy * g
    m1 = jnp.mean(dxh, axis=1, keepdims=True)
    m2 = jnp.mean(dxh * xh, axis=1, keepdims=True)
    du = rstd * (dxh - m1 - xh * m2)
    du_ref[...] = du
    dum_ref[...] = du.astype(dum_ref.dtype)


def _loss_ln_bwd(y, target, xhat, rstd, g, tm):
    S, D = y.shape

    def body(y_ref, t_ref, xh_ref, r_ref, g_ref, sq_ref, du_ref, dum_ref, dg_ref, db_ref):
        first = pl.program_id(0) == 0

        @pl.when(first)
        def _():
            sq_ref[...] = jnp.zeros_like(sq_ref)

        e = y_ref[...] - t_ref[...]
        sq_ref[...] += jnp.sum(e * e, axis=0, keepdims=True)
        _ln_bwd_rows(e / D, xh_ref[...], r_ref[...], g_ref[...], first, du_ref, dum_ref, dg_ref, db_ref)

    row = pl.BlockSpec((tm, D), lambda i: (i, 0))
    vec = pl.BlockSpec((1, D), lambda i: (0, 0))
    return pl.pallas_call(
        body,
        name="loss_ln_bwd",
        grid=(S // tm,),
        in_specs=[row, row, row, pl.BlockSpec((tm, 1), lambda i: (i, 0)), vec],
        out_specs=[vec, row, row, vec, vec],
        out_shape=[_sds((1, D), F32), _sds((S, D), F32), _sds((S, D), MXU_DTYPE), _sds((1, D), F32), _sds((1, D), F32)],
        compiler_params=_cparams(("arbitrary",)),
    )(y, target, xhat, rstd, g)


def _mlp_up(name, x, w_up, tm, tn, tk):
    S = x.shape[0]
    F = w_up.shape[1]

    def epilogue(acc, extra_refs, out_refs, j, ci):
        r = jnp.maximum(acc, 0.0)
        out_refs[0][...] = (r * r).astype(out_refs[0].dtype)

    return _matmul(name, x, w_up, "nn", tm, tn, tk, [(_sds((S, F), MXU_DTYPE), _ij_spec(tm, tn), True)], epilogue,
                   split=("cols", 2))[0]


def _mlp_down_bwd(name, dy, w_down, a, tm, tn, tk):
    S, F = a.shape

    def epilogue(acc, extra_refs, out_refs, j, ci):
        out_refs[0][...] = (acc * (2.0 * jnp.sqrt(extra_refs[0][...].astype(F32)))).astype(out_refs[0].dtype)

    return _matmul(name, dy, w_down, "nt", tm, tn, tk, [(_sds((S, F), MXU_DTYPE), _ij_spec(tm, tn), True)], epilogue,
                   extras=[(a, _ij_spec(tm, tn), True)], split=("cols", 2))[0]


def _mm_nt_res_ln_bwd(name, dy, w, du, xhat, rstd, g, tm, tk, dep, a_map=None, mk=None):
    S, D = du.shape

    def epilogue(acc, extra_refs, out_refs, j, ci):
        du_ref, xh_ref, r_ref, g_ref = extra_refs
        first = (pl.program_id(0) == 0) if ci == 0 else None
        _ln_bwd_rows(ALPHA * du_ref[...] + acc, xh_ref[...], r_ref[...], g_ref[...], first, *out_refs)

    row = pl.BlockSpec((tm, D), lambda i, j, k: (i, 0))
    vec = pl.BlockSpec((1, D), lambda i, j, k: (0, 0))
    return _matmul(name, dy, w, "nt", tm, D, tk,
                   [(_sds((S, D), F32), row, True), (_sds((S, D), MXU_DTYPE), row, True), (_sds((1, D), F32), vec),
                    (_sds((1, D), F32), vec)], epilogue,
                   extras=[(du, row, True), (xhat, row, True), (rstd, pl.BlockSpec((tm, 1), lambda i, j, k: (i, 0)), True), (g, vec)],
                   a_map=a_map, mnk=None if mk is None else (S, D, mk), dep=dep, sem=("arbitrary", "arbitrary", "arbitrary"),
                   split=("rows", 2) if mk is None else None)


def _attn_out_bwd(du, w_out, o, sel_t, tm, tk):
    S, D = o.shape

    def epilogue(acc, extra_refs, out_refs, j, ci):
        out_refs[0][...] = acc
        out_refs[1][...] = _exact_nn(acc * extra_refs[0][...], extra_refs[1][...])

    row = pl.BlockSpec((tm, D), lambda i, j, k: (i, 0))
    slim = pl.BlockSpec((tm, LANES), lambda i, j, k: (i, 0))
    return _matmul("attn_out_bwd", du, w_out, "nt", tm, D, tk,
                   [(_sds((S, D), F32), row, True), (_sds((S, LANES), F32), slim, True)], epilogue,
                   extras=[(o, row, True), (sel_t, pl.BlockSpec((D, LANES), lambda i, j, k: (0, 0)))], split=("rows", 2))


def _adamw(name, w, g, m, v):
    shape = w.shape
    cols = shape[-1]
    rows = math.prod(shape[:-1])
    w2, g2, m2, v2 = (t.reshape(rows, cols) for t in (w, g, m, v))
    tr = _pick(rows, (256, 128, 64, 32, 16, 8))
    c1 = 1.0 - ADAM_B1 ** ADAM_STEP
    c2 = 1.0 - ADAM_B2 ** ADAM_STEP

    def body(w_ref, g_ref, m_ref, v_ref, d_ref, nm_ref, nv_ref):
        gg = g_ref[...]
        nm = ADAM_B1 * m_ref[...] + (1.0 - ADAM_B1) * gg
        nv = ADAM_B2 * v_ref[...] + (1.0 - ADAM_B2) * (gg * gg)
        nm_ref[...] = nm
        nv_ref[...] = nv
        d_ref[...] = -ADAM_LR * ((nm / c1) / (jnp.sqrt(nv / c2) + ADAM_EPS) + ADAM_WD * w_ref[...])

    blk = pl.BlockSpec((tr, cols), lambda i: (i, 0))
    outs = pl.pallas_call(
        body,
        name=name,
        grid=(rows // tr,),
        in_specs=[blk] * 4,
        out_specs=[blk] * 3,
        out_shape=[_sds((rows, cols), F32)] * 3,
        compiler_params=_cparams(("parallel",)),
    )(w2, g2, m2, v2)
    return tuple(o.reshape(shape) for o in outs)


HBM = pl.BlockSpec(memory_space=pl.ANY)


def _shard_slice(ref, axis, size, index):
    idx = [slice(None)] * len(ref.shape)
    idx[axis] = pl.ds(pl.multiple_of(index * size, 8), size)
    return ref.at[tuple(idx)]


def _share_halves(name, full, tr):
    R, W4 = full.shape
    W, h = W4 // 4, R // 2
    steps = [(k, t) for k in range(3) for t in range(h // tr)]

    def body(f_in, f_ref, buf, lsem, ssem, rsem):
        x, y, c = lax.axis_index("x"), lax.axis_index("y"), lax.axis_index("c")
        sibling = (x, y, 1 - c)
        chips = [(1 - x, y), (x, 1 - y), (1 - x, 1 - y)]

        def tile(k, t):
            px, py = chips[k]
            return f_ref.at[pl.ds(pl.multiple_of(c * h + t * tr, 8), tr), pl.ds(pl.multiple_of((2 * px + py) * W, LANES), W)]

        sends = []
        for s, (k, t) in enumerate(steps):
            slot = s % 2
            if s >= 2:
                sends[s - 2].wait_send()
            lc = pltpu.make_async_copy(tile(k, t), buf.at[slot], lsem.at[slot])
            lc.start()
            lc.wait()
            rc = pltpu.make_async_remote_copy(src_ref=buf.at[slot], dst_ref=tile(k, t), send_sem=ssem.at[slot], recv_sem=rsem,
                                              device_id=sibling, device_id_type=MESH)
            rc.start()
            sends.append(rc)
        for rc in sends[-2:]:
            rc.wait_send()
        whole = f_ref.at[pl.ds(0, h), pl.ds(0, 3 * W)]
        pltpu.make_async_remote_copy(src_ref=whole, dst_ref=whole, send_sem=ssem.at[0], recv_sem=rsem,
                                     device_id=sibling, device_id_type=MESH).wait_recv()

    return pl.pallas_call(
        body,
        name=name,
        in_specs=[HBM],
        out_specs=HBM,
        out_shape=_sds(full.shape, full.dtype),
        input_output_aliases={0: 0},
        scratch_shapes=[pltpu.VMEM((2, tr, W), full.dtype), pltpu.SemaphoreType.DMA((2,)), pltpu.SemaphoreType.DMA((2,)),
                        pltpu.SemaphoreType.DMA(())],
    )(full)


IN_HBM = pl.BlockSpec(memory_space=pltpu.HBM)
IN_SEM = pl.BlockSpec(memory_space=pltpu.SEMAPHORE)
DATAFLOW = pltpu.SideEffectType.DATAFLOW_SIDE_EFFECTING


def _hbm(t):
    return pltpu.with_memory_space_constraint(t, pltpu.HBM)


def _token_spec():
    return pl.BlockSpec(memory_space=pltpu.VMEM)


def _gather_copies(s_refs, f_refs, axes, halves, send, recv, loc, arrival):
    x, y, c = lax.axis_index("x"), lax.axis_index("y"), lax.axis_index("c")
    chips = [(1 - x, y), (x, 1 - y), (1 - x, 1 - y)]
    local, remote = [], []
    for a in range(len(s_refs)):
        size = s_refs[a].shape[axes[a]]
        local.append(pltpu.make_async_copy(s_refs[a], _shard_slice(f_refs[a], axes[a], size, 2 * x + y), loc.at[a]))
        for k, (px, py) in enumerate(chips):
            block = (2 * px + py) if arrival else (2 * x + y)
            src, dst = s_refs[a], _shard_slice(f_refs[a], axes[a], size, block)
            if halves:
                assert axes[a] == 1 and len(s_refs[a].shape) == 2
                h = s_refs[a].shape[0] // 2
                rows = pl.ds(pl.multiple_of(c * h, 8), h)
                src = s_refs[a].at[rows, :]
                dst = f_refs[a].at[rows, pl.ds(pl.multiple_of(block * size, LANES), size)]
            remote.append(pltpu.make_async_remote_copy(src_ref=src, dst_ref=dst, send_sem=send.at[3 * a + k],
                                                       recv_sem=recv.at[3 * a + k], device_id=(px, py, c), device_id_type=MESH))
    return local, remote


def _gather_start(name, shards, axes, after, halves=False):
    n = len(shards)
    fulls = []
    for s, ax in zip(shards, axes):
        fs = list(s.shape)
        fs[ax] *= 4
        fulls.append(lax.empty(tuple(fs), s.dtype))

    def body(*refs):
        s_refs, f_refs = refs[:n], refs[n:2 * n]
        send, recv, loc, token = refs[2 * n + 1], refs[2 * n + 2], refs[2 * n + 3], refs[-1]
        local, remote = _gather_copies(s_refs, f_refs, axes, halves, send, recv, loc, arrival=False)
        for cp in remote + local:
            cp.start()
        token[...] = jnp.zeros_like(token)

    outs = pl.pallas_call(
        body,
        name=name,
        out_shape=(pltpu.SemaphoreType.DMA((3 * n,)), pltpu.SemaphoreType.DMA((3 * n,)), pltpu.SemaphoreType.DMA((n,)),
                   *[pltpu.HBM(t.shape, t.dtype) for t in shards + fulls], _sds((8, LANES), F32)),
        in_specs=[IN_HBM] * (2 * n) + [HBM],
        out_specs=(IN_SEM, IN_SEM, IN_SEM, *[IN_HBM] * (2 * n), _token_spec()),
        input_output_aliases={i: 3 + i for i in range(2 * n)},
        compiler_params=pltpu.CompilerParams(has_side_effects=DATAFLOW),
    )(*[_hbm(t) for t in shards + fulls], after)
    return (outs[0], outs[1], outs[2], list(outs[3:3 + n]), list(outs[3 + n:3 + 2 * n]), axes, halves), outs[-1]


def _gather_wait(name, state, *after):
    send, recv, loc, s_thru, f_thru, axes, halves = state
    n = len(s_thru)

    def body(*refs):
        s_refs, f_refs = refs[:n], refs[n:2 * n]
        local, remote = _gather_copies(s_refs, f_refs, axes, halves, refs[2 * n], refs[2 * n + 1], refs[2 * n + 2], arrival=True)
        for cp in local:
            cp.wait()
        for cp in remote:
            cp.wait_send()
            cp.wait_recv()

    outs = pl.pallas_call(
        body,
        name=name,
        out_shape=tuple(pltpu.HBM(t.shape, t.dtype) for t in s_thru + f_thru),
        in_specs=[IN_HBM] * (2 * n) + [IN_SEM, IN_SEM, IN_SEM] + [HBM] * len(after),
        out_specs=tuple([IN_HBM] * (2 * n)),
        input_output_aliases={i: i for i in range(2 * n)},
        compiler_params=pltpu.CompilerParams(has_side_effects=DATAFLOW),
    )(*s_thru, *f_thru, send, recv, loc, *after)
    return list(outs[n:2 * n])


FLIPS = [(fx, fy, fc) for fx in (0, 1) for fy in (0, 1) for fc in (0, 1)][1:]


def _piece_shape(shape, axis):
    ps = list(shape)
    if axis == 0:
        ps[0] //= 8
    else:
        ps[0] //= 2
        ps[axis] //= 4
    return tuple(ps)


def _piece(ref, axis, q, c):
    shape = ref.shape
    idx = [slice(None)] * len(shape)
    if axis == 0:
        h = shape[0] // 8
        idx[0] = pl.ds(pl.multiple_of((2 * q + c) * h, 8), h)
    else:
        h, w = shape[0] // 2, shape[axis] // 4
        idx[0] = pl.ds(c * h, h)
        idx[axis] = pl.ds(pl.multiple_of(q * w, LANES if axis == len(shape) - 1 else 8), w)
    return ref.at[tuple(idx)]


def _own_piece(g, axis):
    ps = _piece_shape(g.shape, axis)
    q, c = 2 * lax.axis_index("x") + lax.axis_index("y"), lax.axis_index("c")
    start = [0] * len(ps)
    if axis == 0:
        start[0] = (2 * q + c) * ps[0]
    else:
        start[0] = c * ps[0]
        start[axis] = q * ps[axis]
    return lax.dynamic_slice(g, start, ps)


def _scatter_copies(g_refs, l_refs, axes, send, recv):
    x, y, c = lax.axis_index("x"), lax.axis_index("y"), lax.axis_index("c")
    out = []
    for a in range(len(g_refs)):
        for k, (fx, fy, fc) in enumerate(FLIPS):
            tx, ty, tc = x ^ fx, y ^ fy, c ^ fc
            out.append(pltpu.make_async_remote_copy(
                src_ref=_piece(g_refs[a], axes[a], 2 * tx + ty, tc), dst_ref=l_refs[a].at[k],
                send_sem=send.at[7 * a + k], recv_sem=recv.at[7 * a + k], device_id=(tx, ty, tc), device_id_type=MESH))
    return out


def _scatter_start(name, grads, axes):
    n = len(grads)
    lands = [lax.empty((7,) + _piece_shape(g.shape, ax), g.dtype) for g, ax in zip(grads, axes)]

    def body(*refs):
        g_refs, l_refs = refs[:n], refs[n:2 * n]
        send, recv, token = refs[2 * n], refs[2 * n + 1], refs[-1]
        for cp in _scatter_copies(g_refs, l_refs, axes, send, recv):
            cp.start()
        token[...] = jnp.zeros_like(token)

    outs = pl.pallas_call(
        body,
        name=name,
        out_shape=(pltpu.SemaphoreType.DMA((7 * n,)), pltpu.SemaphoreType.DMA((7 * n,)),
                   *[pltpu.HBM(t.shape, t.dtype) for t in grads + lands], _sds((8, LANES), F32)),
        in_specs=[IN_HBM] * (2 * n),
        out_specs=(IN_SEM, IN_SEM, *[IN_HBM] * (2 * n), _token_spec()),
        input_output_aliases={i: 2 + i for i in range(2 * n)},
        compiler_params=pltpu.CompilerParams(has_side_effects=DATAFLOW),
    )(*[_hbm(t) for t in grads + lands])
    return (outs[0], outs[1], list(outs[2:2 + n]), list(outs[2 + n:2 + 2 * n]), axes), outs[-1]


def _scatter_wait(name, state, *after):
    send, recv, g_thru, l_thru, axes = state
    n = len(g_thru)

    def body(*refs):
        g_refs, l_refs = refs[:n], refs[n:2 * n]
        for cp in _scatter_copies(g_refs, l_refs, axes, refs[2 * n], refs[2 * n + 1]):
            cp.wait_send()
            cp.wait_recv()

    outs = pl.pallas_call(
        body,
        name=name,
        out_shape=tuple(pltpu.HBM(t.shape, t.dtype) for t in g_thru + l_thru),
        in_specs=[IN_HBM] * (2 * n) + [IN_SEM, IN_SEM] + [HBM] * len(after),
        out_specs=tuple([IN_HBM] * (2 * n)),
        input_output_aliases={i: i for i in range(2 * n)},
        compiler_params=pltpu.CompilerParams(has_side_effects=DATAFLOW),
    )(*g_thru, *l_thru, send, recv, *after)
    return list(outs[:n]), list(outs[n:2 * n])


def _reduce_update(name, landing, own, w, m, v, row0=0, prev=None):
    piece = own.shape
    C = piece[-1]
    R = math.prod(piece[:-1])
    rows_all = math.prod(w.shape[:-1])
    l3 = landing.reshape(7, R, C)
    own2 = own.reshape(R, C)
    w2, m2, v2 = (t.reshape(rows_all, C) for t in (w, m, v))
    tr = _pick(R, [t for t in (256, 128, 64, 32, 16, 8) if t * C <= 128 * 2304])
    nsteps = R // tr
    c1 = 1.0 - ADAM_B1 ** ADAM_STEP
    c2 = 1.0 - ADAM_B2 ** ADAM_STEP
    nprev = 0 if prev is None else 4

    def body(own_ref, l_ref, w_ref, m_ref, v_ref, *rest):
        o_refs = rest[nprev:nprev + 4]
        buf, send, loc, recv = rest[nprev + 4:]
        i = pl.program_id(0)
        x, y, c = lax.axis_index("x"), lax.axis_index("y"), lax.axis_index("c")
        sibling = (x, y, 1 - c)

        def copies(slot, step):
            out = []
            for q in range(4):
                dst = o_refs[q].at[pl.ds(pl.multiple_of(row0 + c * R + step * tr, 8), tr), :]
                out.append(pltpu.make_async_copy(buf.at[slot, q], dst, loc.at[4 * slot + q]))
                out.append(pltpu.make_async_remote_copy(src_ref=buf.at[slot, q], dst_ref=dst, send_sem=send.at[4 * slot + q],
                                                        recv_sem=recv.at[q], device_id=sibling, device_id_type=MESH))
            return out

        def drain(slot, step):
            cps = copies(slot, step)
            for q in range(4):
                cps[2 * q].wait()
                cps[2 * q + 1].wait_send()

        @pl.when(i >= 2)
        def _():
            drain(i % 2, i - 2)

        g = own_ref[...].astype(F32)
        for s in range(7):
            g = g + l_ref[s].astype(F32)
        nm = ADAM_B1 * m_ref[...] + (1.0 - ADAM_B1) * g
        nv = ADAM_B2 * v_ref[...] + (1.0 - ADAM_B2) * (g * g)
        slot = i % 2
        buf[slot, 0] = g
        buf[slot, 1] = -ADAM_LR * ((nm / c1) / (jnp.sqrt(nv / c2) + ADAM_EPS) + ADAM_WD * w_ref[...])
        buf[slot, 2] = nm
        buf[slot, 3] = nv
        for cp in copies(slot, i):
            cp.start()

        @pl.when(i == nsteps - 1)
        def _():
            for st in range(max(nsteps - 2, 0), nsteps):
                drain(st % 2, st)
            for q in range(4):
                theirs = o_refs[q].at[pl.ds(pl.multiple_of(row0 + (1 - c) * R, 8), R), :]
                pltpu.make_async_remote_copy(src_ref=theirs, dst_ref=theirs, send_sem=send.at[0], recv_sem=recv.at[q],
                                             device_id=sibling, device_id_type=MESH).wait_recv()

    base = row0 // tr
    mine = pl.BlockSpec((tr, C), lambda i: (base + lax.axis_index("c") * nsteps + i, 0))
    outs = pl.pallas_call(
        body,
        name=name,
        grid=(nsteps,),
        in_specs=[pl.BlockSpec((tr, C), lambda i: (i, 0)), pl.BlockSpec((7, tr, C), lambda i: (0, i, 0)), mine, mine, mine]
        + [HBM] * nprev,
        out_specs=[HBM] * 4,
        out_shape=[_sds((rows_all, C), F32)] * 4,
        input_output_aliases={5 + q: q for q in range(nprev)},
        scratch_shapes=[pltpu.VMEM((2, 4, tr, C), F32), pltpu.SemaphoreType.DMA((8,)), pltpu.SemaphoreType.DMA((8,)),
                        pltpu.SemaphoreType.DMA((4,))],
        compiler_params=_cparams(("arbitrary",)),
    )(own2, l3, w2, m2, v2, *([] if prev is None else [t.reshape(rows_all, C) for t in prev]))
    return tuple(o.reshape(w.shape) for o in outs)


def _all_reduce_small(v, dep):
    R, D = v.shape

    def body(v_ref, dep_ref, o_ref, land, send, recv):
        x, y, c = lax.axis_index("x"), lax.axis_index("y"), lax.axis_index("c")
        my_slot = 4 * x + 2 * y + c
        land[my_slot] = v_ref[...]
        for k, (fx, fy, fc) in enumerate(FLIPS):
            tx, ty, tc = x ^ fx, y ^ fy, c ^ fc
            pltpu.make_async_remote_copy(src_ref=v_ref, dst_ref=land.at[my_slot], send_sem=send.at[k], recv_sem=recv.at[k],
                                         device_id=(tx, ty, tc), device_id_type=MESH).start()
        for k, (fx, fy, fc) in enumerate(FLIPS):
            tx, ty, tc = x ^ fx, y ^ fy, c ^ fc
            cp = pltpu.make_async_remote_copy(src_ref=v_ref, dst_ref=land.at[4 * tx + 2 * ty + tc], send_sem=send.at[k],
                                              recv_sem=recv.at[k], device_id=(tx, ty, tc), device_id_type=MESH)
            cp.wait_send()
            cp.wait_recv()
        acc = land[0]
        for s in range(1, 8):
            acc = acc + land[s]
        o_ref[...] = acc

    return pl.pallas_call(
        body,
        name="all_reduce_small",
        in_specs=[pl.BlockSpec(memory_space=pltpu.VMEM), pl.BlockSpec(memory_space=pl.ANY)],
        out_specs=pl.BlockSpec(memory_space=pltpu.VMEM),
        out_shape=_sds((R, D), F32),
        scratch_shapes=[pltpu.VMEM((8, R, D), F32), pltpu.SemaphoreType.DMA((7,)), pltpu.SemaphoreType.DMA((7,))],
    )(v, dep)


def kernel(x, attn_w_in, attn_w_out, hgrn_w_in, hgrn_w_out, hgrn_norm_g, lb_logits, ln_mix_g, ln_mix_b, ln_ffn_g, ln_ffn_b, ffn_w_up, ffn_w_down, loss_target, m_attn_w_in, m_attn_w_out, m_hgrn_w_in, m_hgrn_w_out, m_hgrn_norm_g, m_lb_logits, m_ln_mix_g, m_ln_mix_b, m_ln_ffn_g, m_ln_ffn_b, m_ffn_w_up, m_ffn_w_down, v_attn_w_in, v_attn_w_out, v_hgrn_w_in, v_hgrn_w_out, v_hgrn_norm_g, v_lb_logits, v_ln_mix_g, v_ln_mix_b, v_ln_ffn_g, v_ln_ffn_b, v_ffn_w_up, v_ffn_w_down):
    xs = x[0]
    tgt = loss_target[0]
    S, D = xs.shape
    F = ffn_w_up.shape[2] * 4
    T1 = _pick(S, (1024, 512, 256))
    T2 = _pick(S, (2048, 1024, 512))
    TH = _pick(S, (512, 256))
    TB = _pick(S, (128,))
    TN = _pick(D, (512, 256, 128))
    TF = _pick(F, (1024, 512))
    TG = _pick(3 * D, (1536, 1024, 768))
    TW = _pick(F, (2048, 1024))

    cast = lambda w: w.astype(MXU_DTYPE)
    st_a, tok = _gather_start("gather_a", [cast(attn_w_in[0])], [1], jnp.zeros((8, LANES), F32), halves=True)
    tok, (xs_late, w_aout, w_fup, w_fdown, w_hin, w_hout) = lax.optimization_barrier(
        (tok, (xs, attn_w_out, ffn_w_up, ffn_w_down, hgrn_w_in, hgrn_w_out)))
    st_b, tok = _gather_start("gather_b", [cast(w_aout[0]), cast(w_fup[0]), cast(w_fdown[0])], [0, 1, 0], tok)
    st_c, tok = _gather_start("gather_c", [cast(w_hin[0]), cast(w_hout[0]), hgrn_norm_g, cast(w_fup[1]), cast(w_fdown[1])],
                              [1, 0, 1, 1, 0], tok)

    cos3, sin3 = _rope_tables(S)
    sel = _head_sel(D)
    sel_t = sel.T

    xc3 = _stack_classes("x_classes", xs_late, MXU_DTYPE)
    (wa_in,) = _gather_wait("gather_a_wait", st_a, tok, xc3, cos3, sin3)
    wa_in = _share_halves("share_a", wa_in, _pick(D // 2, (256, 128)))
    P3 = _attn_proj(xc3, wa_in, cos3, sin3, T2, TN)
    o3, lse3 = _attn_fwd(P3, D)
    o_att, L_att = _attn_mix(o3, lse3, sel)
    wa_out, w_up0, w_down0 = _gather_wait("gather_b_wait", st_b, L_att)
    x1, xm1, xh1, r1 = _mm_res_ln("attn_out_ln", o_att, wa_out, xs, ln_mix_g[0:1], ln_mix_b[0:1], TH, D)
    a0 = _mlp_up("mlp0_up", xm1, w_up0, T2, TF, D)
    x2, xm2, xh2, r2 = _mm_res_ln("mlp0_down_ln", a0, w_down0, x1, ln_ffn_g[0:1], ln_ffn_b[0:1], TH, F)

    wh_in, wh_out, norm_g, w_up1, w_down1 = _gather_wait("gather_c_wait", st_c, r2)
    P1 = _plain_mm("hgrn_proj", xm2, wh_in, "nn", F32, T1, _pick(3 * D, (1024, 768, 512)), D)
    o_h, n_h, states = _hgrn_fwd(P1, lb_logits, norm_g, TB)
    x3, xm3, xh3, r3 = _mm_res_ln("hgrn_out_ln", n_h, wh_out, x2, ln_mix_g[1:2], ln_mix_b[1:2], TH, D)
    a1 = _mlp_up("mlp1_up", xm3, w_up1, T2, TF, D)
    x4, _, xh4, r4 = _mm_res_ln("mlp1_down_ln", a1, w_down1, x3, ln_ffn_g[1:2], ln_ffn_b[1:2], TH, F)

    wgrad = lambda name, a, dy, tm, tn: _plain_mm(name, a, dy, "tn", MXU_DTYPE, tm, tn, T1)
    sq, du4, dum4, dg_ffn1, db_ffn1 = _loss_ln_bwd(x4, tgt, xh4, r4, ln_ffn_g[1:2], TH)
    dh1 = _mlp_down_bwd("mlp1_down_bwd", dum4, w_down1, a1, T2, TF, D)
    g_down1 = wgrad("g_down1", a1, dum4, TW, D)
    g_up1 = wgrad("g_up1", xm3, dh1, D, TW)
    sc_1, tok = _scatter_start("scatter_1", [g_down1, g_up1], [0, 1])
    du3, dum3, dg_mix1, db_mix1 = _mm_nt_res_ln_bwd("mlp1_up_bwd", dh1, w_up1, du4, xh3, r3, ln_mix_g[1:2], TH, F, tok)
    dn = _plain_mm("hgrn_out_bwd", dum3, wh_out, "nt", F32, T1, D, D)
    g_hout = wgrad("g_hgrn_out", n_h, dum3, D, D)
    dP1, dg_norm, dlb = _hgrn_bwd(P1, o_h, states, dn, lb_logits, norm_g, TB)
    dP1 = dP1.reshape(3 * S, D)
    g_hin = _matmul("g_hgrn_in", xm2, dP1, "tn", D, D, T1, [(_sds((D, 3 * D), MXU_DTYPE), _ij_spec(D, D))], _store_epilogue,
                    b_map=lambda i, j, k: (k + j * (S // T1), 0), mnk=(D, 3 * D, S))[0]
    d_lb_logits = _lb_logits_grad(dlb, lb_logits)
    sc_2, tok = _scatter_start("scatter_2", [g_hout, g_hin], [0, 1])

    du2, dum2, dg_ffn0, db_ffn0 = _mm_nt_res_ln_bwd("hgrn_in_bwd", dP1, wh_in, du3, xh2, r2, ln_ffn_g[0:1], TH, D, tok,
                                                    a_map=lambda i, j, k: (i + k * (S // TH), 0), mk=3 * D)
    dh0 = _mlp_down_bwd("mlp0_down_bwd", dum2, w_down0, a0, T2, TF, D)
    g_down0 = wgrad("g_down0", a0, dum2, TW, D)
    g_up0 = wgrad("g_up0", xm1, dh0, D, TW)
    sc_3, tok = _scatter_start("scatter_3", [g_down0, g_up0], [0, 1])
    du1, dum1, dg_mix0, db_mix0 = _mm_nt_res_ln_bwd("mlp0_up_bwd", dh0, w_up0, du2, xh1, r1, ln_mix_g[0:1], TH, F, tok)
    do, delta = _attn_out_bwd(dum1, wa_out, o_att, sel_t, TH, D)
    g_aout = wgrad("g_attn_out", o_att, dum1, D, D)
    sc_5, tok = _scatter_start("scatter_5", [g_aout], [0])
    dP3 = _attn_bwd(P3, _stack_classes("do_classes", do, MXU_DTYPE), _stack_classes("lse_classes", L_att, F32),
                    _stack_classes("delta_classes", delta, F32), cos3, sin3, D, tok)
    small = jnp.concatenate([d_lb_logits, dg_mix0, dg_mix1, db_mix0, db_mix1, dg_ffn0, dg_ffn1, db_ffn0, db_ffn1,
                             dg_norm, sq, jnp.zeros((4, D), F32)], axis=0)
    small = _all_reduce_small(small, dP3)
    loss = 0.5 * jnp.sum(small[11]) / D
    grp = lambda j: j // (3 * D // TG)
    g_ain = _matmul("g_attn_in", xc3, dP3, "tn", D, TG, T1, [(_sds((D, 9 * D), MXU_DTYPE), _ij_spec(D, TG))], _store_epilogue,
                    a_map=lambda i, j, k: (k + grp(j) * (S // T1), i),
                    b_map=lambda i, j, k: (k + grp(j) * (S // T1), j % (3 * D // TG)), mnk=(D, 9 * D, S), dep=small)[0]
    sc_4, tok = _scatter_start("scatter_4", [g_ain], [1])
    dxc3 = _matmul("attn_in_bwd", dP3, wa_in, "nt", TH, D, 3 * D, [(_sds((3 * S, D), F32), _ij_spec(TH, D))], _store_epilogue,
                   b_map=lambda i, j, k: (j, k + i // (S // TH)), mnk=(3 * S, D, 3 * D), dep=tok)[0]
    grad_x = _input_grad(du1, dxc3)

    grads, upd = {}, {}

    def landed(name, state, *after):
        gs, lands = _scatter_wait(name + "_wait", state, *after)
        return [(l, _own_piece(g, ax)) for l, g, ax in zip(lands, gs, state[4])]

    def update(nm, part, w, m, v, row0=0, prev=None):
        res = _reduce_update("update_" + nm, part[0], part[1], w, m, v, row0, prev)
        grads[nm], upd[nm] = res[0], res[1:]
        return res

    p_down1, p_up1 = landed("scatter_1", sc_1, grad_x)
    r = update("ffn_w_down", p_down1, ffn_w_down, m_ffn_w_down, v_ffn_w_down, ffn_w_down.shape[1])
    q = update("ffn_w_up", p_up1, ffn_w_up, m_ffn_w_up, v_ffn_w_up, ffn_w_up.shape[1])
    p_hout, p_hin = landed("scatter_2", sc_2, q[3])
    update("hgrn_w_out", p_hout, hgrn_w_out, m_hgrn_w_out, v_hgrn_w_out)
    t = update("hgrn_w_in", p_hin, hgrn_w_in, m_hgrn_w_in, v_hgrn_w_in)
    p_down0, p_up0 = landed("scatter_3", sc_3, t[3])
    update("ffn_w_down", p_down0, ffn_w_down, m_ffn_w_down, v_ffn_w_down, 0, r)
    t = update("ffn_w_up", p_up0, ffn_w_up, m_ffn_w_up, v_ffn_w_up, 0, q)
    (p_aout,) = landed("scatter_5", sc_5, t[3])
    t = update("attn_w_out", p_aout, attn_w_out, m_attn_w_out, v_attn_w_out)

    my_chip = 2 * lax.axis_index("x") + lax.axis_index("y")
    nsh = hgrn_norm_g.shape[1]
    g_norm = lax.dynamic_slice(small[10:11], (0, my_chip * nsh), (1, nsh))
    grads["hgrn_norm_g"] = g_norm
    upd["hgrn_norm_g"] = _adamw("adamw_hgrn_norm_g", hgrn_norm_g, g_norm, m_hgrn_norm_g, v_hgrn_norm_g)
    cat = lambda ts: jnp.concatenate(ts, axis=0)
    small_w = cat([lb_logits, ln_mix_g, ln_mix_b, ln_ffn_g, ln_ffn_b])
    small_m = cat([m_lb_logits, m_ln_mix_g, m_ln_mix_b, m_ln_ffn_g, m_ln_ffn_b])
    small_v = cat([v_lb_logits, v_ln_mix_g, v_ln_mix_b, v_ln_ffn_g, v_ln_ffn_b])
    small_upd = _adamw("adamw_small", small_w, small[0:10], small_m, small_v)
    for i, nm in enumerate(["lb_logits", "ln_mix_g", "ln_mix_b", "ln_ffn_g", "ln_ffn_b"]):
        grads[nm] = small[2 * i:2 * i + 2]
        upd[nm] = tuple(s[2 * i:2 * i + 2] for s in small_upd)
    (p_ain,) = landed("scatter_4", sc_4, t[3], small_upd[2], upd["hgrn_norm_g"][2])
    update("attn_w_in", p_ain, attn_w_in, m_attn_w_in, v_attn_w_in)

    order = ["attn_w_in", "attn_w_out", "hgrn_w_in", "hgrn_w_out", "hgrn_norm_g", "lb_logits", "ln_mix_g", "ln_mix_b",
             "ln_ffn_g", "ln_ffn_b", "ffn_w_up", "ffn_w_down"]
    return (loss, grad_x[None], *[grads[k] for k in order], *[upd[k][0] for k in order],
            *[upd[k][1] for k in order], *[upd[k][2] for k in order])
```

```python
import math

import jax
import jax.numpy as jnp
from jax import lax
from jax.experimental import pallas as pl
from jax.experimental.pallas import tpu as pltpu

F32 = jnp.float32
BF16 = jnp.bfloat16
MXU_DTYPE = BF16

HEAD_DIM = 64
ATTN_BLK = 128
DILATIONS = (1, 4, 16)
ROPE_THETA = 10000.0
HGRN_DK = 128
HGRN_CHUNK = 64
DEPTH = 2
LN_EPS = 1e-5
RMS_EPS = 1e-6
ALPHA = (2 * DEPTH) ** 0.25
ADAM_LR, ADAM_B1, ADAM_B2, ADAM_EPS, ADAM_WD, ADAM_STEP = 0.001, 0.9, 0.999, 1e-08, 0.01, 10

LANES = 128
VMEM_LIMIT = 56 * 1024 * 1024
NEG = -1e30
MESH = pl.DeviceIdType.MESH


def _cparams(sem=None):
    return pltpu.CompilerParams(dimension_semantics=sem, vmem_limit_bytes=VMEM_LIMIT)


def _sds(shape, dtype):
    return jax.ShapeDtypeStruct(tuple(shape), dtype)


def _dg(a, b, ca, cb):
    return lax.dot_general(a, b, (((ca,), (cb,)), ((), ())), preferred_element_type=F32)


def _nn(a, b):
    return _dg(a, b, 1, 0)


def _nt(a, b):
    return _dg(a, b, 1, 1)


def _tn(a, b):
    return _dg(a, b, 0, 0)


def _split3(a):
    hi = a.astype(BF16)
    r = a - hi.astype(F32)
    mid = r.astype(BF16)
    lo = (r - mid.astype(F32)).astype(BF16)
    return hi, mid, lo


def _exact_nn(a, sel):
    hi, mid, lo = _split3(a)
    return _nn(hi, sel) + _nn(mid, sel) + _nn(lo, sel)


def _pick(n, prefs):
    for p in prefs:
        if n % p == 0:
            return p
    return n


def _matmul(name, a, b, form, tm, tn, tk, outs, epilogue, extras=(), a_map=None, b_map=None, mnk=None, dep=None,
            sem=("parallel", "parallel", "arbitrary"), split=None):
    if form == "nn":
        (M, K), N = a.shape, b.shape[1]
        a_spec = pl.BlockSpec((tm, tk), a_map or (lambda i, j, k: (i, k)))
        b_spec = pl.BlockSpec((tk, tn), b_map or (lambda i, j, k: (k, j)))
        ca, cb = 1, 0
    elif form == "nt":
        (M, K), N = a.shape, b.shape[0]
        a_spec = pl.BlockSpec((tm, tk), a_map or (lambda i, j, k: (i, k)))
        b_spec = pl.BlockSpec((tn, tk), b_map or (lambda i, j, k: (j, k)))
        ca, cb = 1, 1
    else:
        (K, M), N = a.shape, b.shape[1]
        a_spec = pl.BlockSpec((tk, tm), a_map or (lambda i, j, k: (k, i)))
        b_spec = pl.BlockSpec((tk, tn), b_map or (lambda i, j, k: (k, j)))
        ca, cb = 0, 0
    if mnk is not None:
        M, N, K = mnk
    assert M % tm == 0 and N % tn == 0 and K % tk == 0, (name, M, N, K, tm, tn, tk)
    nk = K // tk
    ne, no = len(extras), len(outs)
    deps = [] if dep is None else [dep]
    nd = len(deps)

    def body(a_ref, b_ref, *rest):
        extra_refs, out_refs = rest[:ne], rest[ne + nd:ne + nd + no]
        j = pl.program_id(1)
        if split is not None:
            kind, n = split
            assert nk == 1 and form != "tn"
            tiled = [t for _, _, *t in list(extras) + list(outs)]
            refs = list(extra_refs) + list(out_refs)
            for ci in range(n):
                if kind == "cols":
                    cs = slice(ci * (tn // n), (ci + 1) * (tn // n))
                    part = _dg(a_ref[...].astype(MXU_DTYPE), (b_ref[:, cs] if form == "nn" else b_ref[cs, :]).astype(MXU_DTYPE), ca, cb)
                    view = [r.at[:, cs] if t else r for r, t in zip(refs, tiled)]
                else:
                    rs = slice(ci * (tm // n), (ci + 1) * (tm // n))
                    part = _dg(a_ref[rs, :].astype(MXU_DTYPE), b_ref[...].astype(MXU_DTYPE), ca, cb)
                    view = [r.at[rs, :] if t else r for r, t in zip(refs, tiled)]
                epilogue(part, view[:ne], view[ne:], j, ci)
            return
        part = _dg(a_ref[...].astype(MXU_DTYPE), b_ref[...].astype(MXU_DTYPE), ca, cb)
        if nk == 1:
            epilogue(part, extra_refs, out_refs, j, 0)
            return
        acc_ref = rest[-1]
        k = pl.program_id(2)

        @pl.when(k == 0)
        def _():
            acc_ref[...] = part

        @pl.when(k > 0)
        def _():
            acc_ref[...] += part

        @pl.when(k == nk - 1)
        def _():
            epilogue(acc_ref[...], extra_refs, out_refs, j, 0)

    res = pl.pallas_call(
        body,
        name=name,
        grid=(M // tm, N // tn, nk),
        in_specs=[a_spec, b_spec] + [s for _, s, *_ in extras] + [pl.BlockSpec(memory_space=pl.ANY)] * nd,
        out_specs=[s for _, s, *_ in outs],
        out_shape=[o for o, *_ in outs],
        scratch_shapes=[pltpu.VMEM((tm, tn), F32)] if nk > 1 else [],
        compiler_params=_cparams(sem),
    )(a, b, *[e for e, *_ in extras], *deps)
    return res


def _ij_spec(tm, tn):
    return pl.BlockSpec((tm, tn), lambda i, j, k: (i, j))


def _store_epilogue(acc, extra_refs, out_refs, j, ci):
    out_refs[0][...] = acc.astype(out_refs[0].dtype)


def _plain_mm(name, a, b, form, out_dtype, tm, tn, tk):
    M = a.shape[1] if form == "tn" else a.shape[0]
    N = b.shape[0] if form == "nt" else b.shape[1]
    return _matmul(name, a, b, form, tm, tn, tk, [(_sds((M, N), out_dtype), _ij_spec(tm, tn))], _store_epilogue)[0]


def _class_slabs(S):
    assert DILATIONS[0] == 1
    return [(g, d, r, S // d) for g, d in enumerate(DILATIONS) if d > 1 for r in range(d)]


def _stack_classes(name, t, out_dtype):
    S, W = t.shape

    def body(x_ref, o_ref):
        o_ref[0:S, :] = x_ref[...].astype(out_dtype)
        for g, d, r, n in _class_slabs(S):
            o_ref[g * S + r * n:g * S + (r + 1) * n, :] = x_ref[pl.ds(r, n, stride=d), :].astype(out_dtype)

    return pl.pallas_call(
        body,
        name=name,
        grid=(W // LANES,),
        in_specs=[pl.BlockSpec((S, LANES), lambda j: (0, j))],
        out_specs=pl.BlockSpec((3 * S, LANES), lambda j: (0, j)),
        out_shape=_sds((3 * S, W), out_dtype),
        compiler_params=_cparams(("parallel",)),
    )(t)


def _rope_tables(seq):
    half = HEAD_DIM // 2
    inv = ROPE_THETA ** (-jnp.arange(half, dtype=F32) * (2.0 / HEAD_DIM))
    inv = jnp.tile(inv, LANES // half)
    pos = []
    for d in DILATIONS:
        row = jnp.arange(seq)
        pos.append((row % (seq // d)) * d + row // (seq // d))
    ang = jnp.concatenate(pos).astype(F32)[:, None] * inv[None, :]
    first = (jnp.arange(LANES) % HEAD_DIM) < half
    sin = jnp.sin(ang)
    return jnp.cos(ang), jnp.where(first[None, :], -sin, sin)


def _partner(x):
    half = HEAD_DIM // 2
    lane = lax.broadcasted_iota(jnp.int32, x.shape, 1)
    first = (lane % HEAD_DIM) < half
    return jnp.where(first, pltpu.roll(x, LANES - half, 1), pltpu.roll(x, half, 1))


def _attn_proj(x3, w_full, cos3, sin3, tm, tn):
    S3, D = x3.shape
    S = S3 // 3
    per_part = D // tn
    per_group = 3 * per_part

    def epilogue(acc, extra_refs, out_refs, j, ci):
        cos_ref, sin_ref = extra_refs
        o_ref = out_refs[0]
        is_rot = j // per_part < 2
        c = jnp.where(is_rot, cos_ref[...], 1.0)
        s = jnp.where(is_rot, sin_ref[...], 0.0)
        for t in range(acc.shape[1] // LANES):
            xs = acc[:, t * LANES:(t + 1) * LANES]
            o_ref[:, t * LANES:(t + 1) * LANES] = (xs * c + _partner(xs) * s).astype(o_ref.dtype)

    tab = pl.BlockSpec((tm, LANES), lambda i, j, k: (i, 0))
    return _matmul("attn_proj", x3, w_full, "nn", tm, tn, D, [(_sds((S3, 3 * D), MXU_DTYPE), _ij_spec(tm, tn), True)],
                   epilogue, extras=[(cos3, tab), (sin3, tab)],
                   b_map=lambda i, j, k: (k, j + (i // (S // tm)) * per_group), mnk=(S3, 3 * D, D),
                   split=("cols", tn // (2 * LANES)))[0]


def _head_sel(d_model):
    h = jnp.arange(LANES)[:, None]
    l = jnp.arange(d_model)[None, :]
    return (l // HEAD_DIM == h).astype(BF16)


def _class_edges(b, nblk):
    g = b // nblk
    per_class = jnp.where(g == 0, nblk // DILATIONS[0], jnp.where(g == 1, nblk // DILATIONS[1], nblk // DILATIONS[2]))
    pos = (b % nblk) % per_class
    return pos != 0, pos != per_class - 1


def _two_heads(t, top):
    zero = jnp.zeros_like(t)
    return jnp.concatenate([jnp.where(top, t, zero), jnp.where(top, zero, t)], axis=0)


def _band_mask(has_prev):
    B = ATTN_BLK
    row = lax.broadcasted_iota(jnp.int32, (2 * B, 2 * B), 0) % B
    col = lax.broadcasted_iota(jnp.int32, (2 * B, 2 * B), 1)
    in_prev = jnp.logical_and(jnp.logical_and(col < B, col >= row), has_prev)
    in_own = jnp.logical_and(col >= B, col - B <= row)
    return jnp.logical_or(in_prev, in_own)


def _attn_fwd(P3, D):
    S3 = P3.shape[0]
    B = ATTN_BLK
    nblk = S3 // 3 // B
    npairs = D // LANES
    scale = HEAD_DIM ** -0.5

    def body(q_ref, kc_ref, vc_ref, kp_ref, vp_ref, o_ref, lse_ref):
        has_prev, _ = _class_edges(pl.program_id(0), nblk)
        ok = _band_mask(has_prev)
        lane = lax.broadcasted_iota(jnp.int32, (B, LANES), 1)
        top = lane < HEAD_DIM
        lse_acc = jnp.zeros((B, LANES), F32)
        for j in range(npairs):
            sl = slice(j * LANES, (j + 1) * LANES)
            Q = _two_heads(q_ref[:, sl] * scale, top)
            K2 = jnp.concatenate([kp_ref[:, sl], kc_ref[:, sl]], axis=0)
            V2 = jnp.concatenate([vp_ref[:, sl], vc_ref[:, sl]], axis=0)
            s = jnp.where(ok, _nt(Q, K2), NEG)
            m = jnp.max(s, axis=1, keepdims=True)
            p = jnp.exp(s - m)
            l = jnp.sum(p, axis=1, keepdims=True)
            o = _nn((p * (1.0 / l)).astype(MXU_DTYPE), V2)
            o_ref[:, sl] = jnp.where(top, o[:B], o[B:])
            lse = m + jnp.log(l)
            lse_acc = jnp.where(lane == 2 * j, lse[:B], jnp.where(lane == 2 * j + 1, lse[B:], lse_acc))
        lse_ref[...] = lse_acc

    blk = lambda part, prev: pl.BlockSpec(
        (B, D), (lambda b: (jnp.maximum(b - 1, 0), part)) if prev else (lambda b: (b, part)))
    return pl.pallas_call(
        body,
        name="attn_fwd",
        grid=(3 * nblk,),
        in_specs=[blk(0, False), blk(1, False), blk(2, False), blk(1, True), blk(2, True)],
        out_specs=[pl.BlockSpec((B, D), lambda b: (b, 0)), pl.BlockSpec((B, LANES), lambda b: (b, 0))],
        out_shape=[_sds((S3, D), F32), _sds((S3, LANES), F32)],
        compiler_params=_cparams(("parallel",)),
    )(P3, P3, P3, P3, P3)


def _attn_mix(o3, lse3, sel):
    S3, D = o3.shape
    S = S3 // 3

    def body(o3_ref, lse_ref, sel_ref, o_ref, L_ref, w_ref):
        @pl.when(pl.program_id(0) == 0)
        def _():
            w_ref[0] = lse_ref[0:S, :]
            for g, d, r, n in _class_slabs(S):
                w_ref[g, pl.ds(r, n, stride=d), :] = lse_ref[g * S + r * n:g * S + (r + 1) * n, :]
            a, b, c = w_ref[0], w_ref[1], w_ref[2]
            m = jnp.maximum(jnp.maximum(a, b), c)
            L = m + jnp.log(jnp.exp(a - m) + jnp.exp(b - m) + jnp.exp(c - m))
            L_ref[...] = L
            w_ref[0] = jnp.exp(a - L)
            w_ref[1] = jnp.exp(b - L)
            w_ref[2] = jnp.exp(c - L)

        s = sel_ref[...]
        o_ref[...] = _exact_nn(w_ref[0], s) * o3_ref[0:S, :]
        for g, d, r, n in _class_slabs(S):
            rows = pl.ds(r, n, stride=d)
            o_ref[rows, :] += _exact_nn(w_ref[g, rows, :], s) * o3_ref[g * S + r * n:g * S + (r + 1) * n, :]

    return pl.pallas_call(
        body,
        name="attn_mix",
        grid=(D // LANES,),
        in_specs=[pl.BlockSpec((S3, LANES), lambda j: (0, j)), pl.BlockSpec((S3, LANES), lambda j: (0, 0)),
                  pl.BlockSpec((LANES, LANES), lambda j: (0, j))],
        out_specs=[pl.BlockSpec((S, LANES), lambda j: (0, j)), pl.BlockSpec((S, LANES), lambda j: (0, 0))],
        out_shape=[_sds((S, D), F32), _sds((S, LANES), F32)],
        scratch_shapes=[pltpu.VMEM((3, S, LANES), F32)],
        compiler_params=_cparams(("arbitrary",)),
    )(o3, lse3, sel)


def _attn_bwd(P3, do3, L3, delta3, cos3, sin3, D, dep):
    S3 = P3.shape[0]
    B = ATTN_BLK
    nblk = S3 // 3 // B
    npairs = D // LANES
    scale = HEAD_DIM ** -0.5

    def body(c_ref, kp_ref, vp_ref, qn_ref, doc_ref, don_ref, Lc_ref, Ln_ref, dc_ref, dn_ref, cos_ref, sin_ref, dep_ref, out_ref):
        has_prev, has_next = _class_edges(pl.program_id(0), nblk)
        ok = _band_mask(has_prev)
        row = lax.broadcasted_iota(jnp.int32, (2 * B, B), 0) % B
        col = lax.broadcasted_iota(jnp.int32, (2 * B, B), 1)
        ok_n = jnp.logical_and(col >= row, has_next)
        lane = lax.broadcasted_iota(jnp.int32, (B, LANES), 1)
        top = lane < HEAD_DIM
        cos_t = cos_ref[...]
        sin_inv = -sin_ref[...]
        Lc_all, Ln_all, dc_all, dn_all = Lc_ref[...], Ln_ref[...], dc_ref[...], dn_ref[...]
        pair_col = lambda t, j: jnp.concatenate([t[:, 2 * j:2 * j + 1], t[:, 2 * j + 1:2 * j + 2]], axis=0)
        for j in range(npairs):
            sl = lambda part: slice(part * D + j * LANES, part * D + (j + 1) * LANES)
            pj = slice(j * LANES, (j + 1) * LANES)
            kc2, vc2 = c_ref[:, sl(1)], c_ref[:, sl(2)]
            K2 = jnp.concatenate([kp_ref[:, pj], kc2], axis=0)
            V2 = jnp.concatenate([vp_ref[:, pj], vc2], axis=0)
            Qc = _two_heads(c_ref[:, sl(0)] * scale, top)
            Qn = _two_heads(qn_ref[:, pj] * scale, top)
            DOc = _two_heads(doc_ref[:, pj].astype(MXU_DTYPE), top)
            DOn = _two_heads(don_ref[:, pj].astype(MXU_DTYPE), top)
            P_c = jnp.where(ok, jnp.exp(_nt(Qc, K2) - pair_col(Lc_all, j)), 0.0)
            dS_c = P_c * (_nt(DOc, V2) - pair_col(dc_all, j))
            P_n = jnp.where(ok_n, jnp.exp(_nt(Qn, kc2) - pair_col(Ln_all, j)), 0.0)
            dS_n = P_n * (_nt(DOn, vc2) - pair_col(dn_all, j))
            dq = _nn(dS_c.astype(MXU_DTYPE), K2)
            dq2 = jnp.where(top, dq[:B], dq[B:]) * scale
            Qk = jnp.concatenate([Qc, Qn], axis=0)
            DOk = jnp.concatenate([DOc, DOn], axis=0)
            dk2 = _tn(jnp.concatenate([dS_c[:, B:], dS_n], axis=0).astype(MXU_DTYPE), Qk)
            dv2 = _tn(jnp.concatenate([P_c[:, B:], P_n], axis=0).astype(MXU_DTYPE), DOk)
            out_ref[:, sl(0)] = (dq2 * cos_t + _partner(dq2) * sin_inv).astype(out_ref.dtype)
            out_ref[:, sl(1)] = (dk2 * cos_t + _partner(dk2) * sin_inv).astype(out_ref.dtype)
            out_ref[:, sl(2)] = dv2.astype(out_ref.dtype)

    cur = lambda b: b
    prv = lambda b: jnp.maximum(b - 1, 0)
    nxt = lambda b: jnp.minimum(b + 1, 3 * nblk - 1)
    spec = lambda w, f, part=0: pl.BlockSpec((B, w), lambda b: (f(b), part))
    return pl.pallas_call(
        body,
        name="attn_bwd",
        grid=(3 * nblk,),
        in_specs=[spec(3 * D, cur), spec(D, prv, 1), spec(D, prv, 2), spec(D, nxt, 0), spec(D, cur), spec(D, nxt),
                  spec(LANES, cur), spec(LANES, nxt), spec(LANES, cur), spec(LANES, nxt), spec(LANES, cur), spec(LANES, cur),
                  pl.BlockSpec(memory_space=pl.ANY)],
        out_specs=spec(3 * D, cur),
        out_shape=_sds((S3, 3 * D), MXU_DTYPE),
        compiler_params=_cparams(("parallel",)),
    )(P3, P3, P3, P3, do3, do3, L3, L3, delta3, delta3, cos3, sin3, dep)


def _input_grad(du, dx3):
    S, D = du.shape

    def body(du_ref, dx_ref, o_ref):
        o_ref[...] = ALPHA * du_ref[...] + dx_ref[0:S, :]
        for g, d, r, n in _class_slabs(S):
            o_ref[pl.ds(r, n, stride=d), :] += dx_ref[g * S + r * n:g * S + (r + 1) * n, :]

    return pl.pallas_call(
        body,
        name="input_grad",
        grid=(D // LANES,),
        in_specs=[pl.BlockSpec((S, LANES), lambda j: (0, j)), pl.BlockSpec((3 * S, LANES), lambda j: (0, j))],
        out_specs=pl.BlockSpec((S, LANES), lambda j: (0, j)),
        out_shape=_sds((S, D), F32),
        compiler_params=_cparams(("parallel",)),
    )(du, dx3)


def _chunk_causal(tb):
    r = lax.broadcasted_iota(jnp.int32, (tb, tb), 0)
    c = lax.broadcasted_iota(jnp.int32, (tb, tb), 1)
    return jnp.logical_and((r // HGRN_CHUNK) == (c // HGRN_CHUNK), r >= c)


def _chunk_sums(a, lower):
    C = HGRN_CHUNK
    r = lax.broadcasted_iota(jnp.int32, (C, C), 0)
    c = lax.broadcasted_iota(jnp.int32, (C, C), 1)
    tri = ((r >= c) if lower else (r <= c)).astype(BF16)
    parts = _split3(a)
    out = []
    for ci in range(a.shape[0] // C):
        rows = slice(ci * C, (ci + 1) * C)
        out.append(_nn(tri, parts[0][rows]) + _nn(tri, parts[1][rows]) + _nn(tri, parts[2][rows]))
    return jnp.concatenate(out, axis=0)


def _chunk_last(b):
    C = HGRN_CHUNK
    return jnp.concatenate([jnp.broadcast_to(b[(ci + 1) * C - 1:(ci + 1) * C, :], (C, b.shape[1]))
                            for ci in range(b.shape[0] // C)], axis=0)


def _lower_bound(lb_ref):
    l0, l1 = lb_ref[0:1, :], lb_ref[1:2, :]
    m = jnp.maximum(l0, l1)
    e0, e1 = jnp.exp(l0 - m), jnp.exp(l1 - m)
    return e1 / (e0 + e1)


def _hgrn_gates(q_raw, z, lb):
    sg = 1.0 / (1.0 + jnp.exp(-z))
    sn = 1.0 / (1.0 + jnp.exp(z))
    f = lb + (1.0 - lb) * sg
    key = (1.0 - lb) * sn
    sq = 1.0 / (1.0 + jnp.exp(-q_raw))
    return sg, sn, f, key, sq


def _hgrn_fwd(P1, lb_logits, norm_g, tb):
    S = P1.shape[0]
    D = P1.shape[1] // 3
    K = HGRN_DK
    H = D // K
    HP = H
    C = HGRN_CHUNK
    cpb = tb // C
    nt = S // tb

    def body(q_ref, f_ref, i_ref, lb_ref, g_ref, o_ref, n_ref, st_ref, state):
        t = pl.program_id(1)

        @pl.when(t == 0)
        def _():
            state[...] = jnp.zeros_like(state)

        lb_all = _lower_bound(lb_ref)
        low = _chunk_causal(tb)
        for hh in range(HP):
            lanes = slice(hh * K, (hh + 1) * K)
            q_raw, z, v = q_ref[:, lanes], f_ref[:, lanes], i_ref[:, lanes]
            sg, sn, f, key, sq = _hgrn_gates(q_raw, z, lb_all[:, lanes])
            b = _chunk_sums(jnp.log(f), lower=True)
            qd = (q_raw * sq * jnp.exp(b)).astype(MXU_DTYPE)
            kd = (key * jnp.exp(-b)).astype(MXU_DTYPE)
            kb = (key * jnp.exp(_chunk_last(b) - b)).astype(MXU_DTYPE)
            vm = v.astype(MXU_DTYPE)
            a = jnp.where(low, _nt(qd, kd), 0.0).astype(MXU_DTYPE)
            o_intra = _nn(a, vm)
            st = state[hh]
            outs = []
            for ci in range(cpb):
                rows = slice(ci * C, (ci + 1) * C)
                st_ref[hh, ci] = st
                outs.append(o_intra[rows] + _nt(qd[rows], st.astype(MXU_DTYPE)))
                st = st * jnp.exp(b[(ci + 1) * C - 1:(ci + 1) * C, :]) + _tn(vm[rows], kb[rows])
            state[hh] = st
            o = jnp.concatenate(outs, axis=0)
            o_ref[:, lanes] = o
            rs = lax.rsqrt(jnp.mean(o * o, axis=1, keepdims=True) + RMS_EPS)
            n_ref[:, lanes] = o * rs * g_ref[:, lanes]

    tok = lambda part: pl.BlockSpec((tb, HP * K), lambda h, t: (t, part * (H // HP) + h))
    vec = lambda rows: pl.BlockSpec((rows, HP * K), lambda h, t: (0, h))
    return pl.pallas_call(
        body,
        name="hgrn_fwd",
        grid=(H // HP, nt),
        in_specs=[tok(0), tok(1), tok(2), vec(2), vec(1)],
        out_specs=[tok(0), tok(0), pl.BlockSpec((HP, cpb, K, K), lambda h, t: (h, t, 0, 0))],
        out_shape=[_sds((S, D), F32), _sds((S, D), F32), _sds((H, S // C, K, K), F32)],
        scratch_shapes=[pltpu.VMEM((HP, K, K), F32)],
        compiler_params=_cparams(("parallel", "arbitrary")),
    )(P1, P1, P1, lb_logits, norm_g)


def _hgrn_bwd(P1, o_pre, states, dn, lb_logits, norm_g, tb):
    S = P1.shape[0]
    D = P1.shape[1] // 3
    K = HGRN_DK
    H = D // K
    HP = H
    C = HGRN_CHUNK
    cpb = tb // C
    nt = S // tb

    def body(q_ref, f_ref, i_ref, o_ref, st_ref, dn_ref, lb_ref, g_ref, d_ref, dg_ref, dlb_ref, dstate):
        t = pl.program_id(1)

        @pl.when(t == 0)
        def _():
            dstate[...] = jnp.zeros_like(dstate)
            dg_ref[...] = jnp.zeros_like(dg_ref)
            dlb_ref[...] = jnp.zeros_like(dlb_ref)

        lb_all = _lower_bound(lb_ref)
        low = _chunk_causal(tb)
        for hh in range(HP):
            lanes = slice(hh * K, (hh + 1) * K)
            lb = lb_all[:, lanes]
            gn = g_ref[:, lanes]
            q_raw, z, v = q_ref[:, lanes], f_ref[:, lanes], i_ref[:, lanes]
            sg, sn, f, key, sq = _hgrn_gates(q_raw, z, lb)
            b = _chunk_sums(jnp.log(f), lower=True)
            e_pos, e_neg, e_rel = jnp.exp(b), jnp.exp(-b), jnp.exp(_chunk_last(b) - b)
            qd_f, kd_f, kb_f = q_raw * sq * e_pos, key * e_neg, key * e_rel
            qd, kd, kb = qd_f.astype(MXU_DTYPE), kd_f.astype(MXU_DTYPE), kb_f.astype(MXU_DTYPE)
            vm = v.astype(MXU_DTYPE)
            a = jnp.where(low, _nt(qd, kd), 0.0).astype(MXU_DTYPE)
            o = o_ref[:, lanes]
            dnn = dn_ref[:, lanes]
            rs = lax.rsqrt(jnp.mean(o * o, axis=1, keepdims=True) + RMS_EPS)
            dg_ref[:, lanes] += jnp.sum(dnn * o * rs, axis=0, keepdims=True)
            tg = dnn * gn
            dom = (rs * tg - o * (rs * rs * rs) * jnp.mean(tg * o, axis=1, keepdims=True)).astype(MXU_DTYPE)
            da = jnp.where(low, _nt(dom, vm), 0.0).astype(MXU_DTYPE)
            dv = _tn(a, dom)
            dqd = _nn(da, kd)
            dkd = _tn(da, qd)
            dst = dstate[hh]
            dv_s, dqd_s, dkb_s, dbl_s = [None] * cpb, [None] * cpb, [None] * cpb, [None] * cpb
            for ci in reversed(range(cpb)):
                rows = slice(ci * C, (ci + 1) * C)
                st = st_ref[hh, ci]
                dstm = dst.astype(MXU_DTYPE)
                dec = jnp.exp(b[(ci + 1) * C - 1:(ci + 1) * C, :])
                dv_s[ci] = _nt(kb[rows], dstm)
                dkb_s[ci] = _nn(vm[rows], dstm)
                dqd_s[ci] = _nn(dom[rows], st.astype(MXU_DTYPE))
                db_last = jnp.sum(dkb_s[ci] * kb_f[rows], axis=0, keepdims=True) + jnp.sum(dst * st, axis=0, keepdims=True) * dec
                dbl_s[ci] = jnp.broadcast_to(db_last, (C, K))
                dst = dst * dec + _tn(dom[rows], qd[rows])
            dstate[hh] = dst
            dv = dv + jnp.concatenate(dv_s, axis=0)
            dqd = dqd + jnp.concatenate(dqd_s, axis=0)
            dkb = jnp.concatenate(dkb_s, axis=0)
            dkey = dkd * e_neg + dkb * e_rel
            db = dqd * qd_f - dkd * kd_f - dkb * kb_f
            dlogf = _chunk_sums(db, lower=False) + jnp.concatenate(dbl_s, axis=0)
            gz = (1.0 - lb) * sg * sn
            col = lambda part: slice(part * D + hh * K, part * D + (hh + 1) * K)
            d_ref[:, col(0)] = (dqd * e_pos * (sq + q_raw * sq * (1.0 - sq))).astype(d_ref.dtype)
            d_ref[:, col(1)] = (dlogf * gz / f - dkey * gz).astype(d_ref.dtype)
            d_ref[:, col(2)] = dv.astype(d_ref.dtype)
            dlb_ref[:, lanes] += jnp.sum(dlogf * sn / f - dkey * sn, axis=0, keepdims=True)

    rev = lambda t: nt - 1 - t
    tok = lambda part: pl.BlockSpec((tb, HP * K), lambda h, t: (rev(t), part * (H // HP) + h))
    vec = lambda rows: pl.BlockSpec((rows, HP * K), lambda h, t: (0, h))
    outs = pl.pallas_call(
        body,
        name="hgrn_bwd",
        grid=(H // HP, nt),
        in_specs=[tok(0), tok(1), tok(2), tok(0),
                  pl.BlockSpec((HP, cpb, K, K), lambda h, t: (h, rev(t), 0, 0)),
                  tok(0), vec(2), vec(1)],
        out_specs=[pl.BlockSpec((tb, 3 * D), lambda h, t: (rev(t), 0)), vec(1), vec(1)],
        out_shape=[_sds((S, 3 * D), MXU_DTYPE)] + [_sds((1, D), F32)] * 2,
        scratch_shapes=[pltpu.VMEM((HP, K, K), F32)],
        compiler_params=_cparams(("parallel", "arbitrary")),
    )(P1, P1, P1, o_pre, states, dn, lb_logits, norm_g)
    return outs


def _lb_logits_grad(dlb, lb_logits):
    def body(d_ref, l_ref, o_ref):
        s1 = _lower_bound(l_ref)
        d = d_ref[...]
        o_ref[0:1, :] = -(1.0 - s1) * s1 * d
        o_ref[1:2, :] = s1 * (1.0 - s1) * d

    return pl.pallas_call(body, name="lb_logits_grad", out_shape=_sds(lb_logits.shape, F32))(dlb, lb_logits)


def _ln_epilogue(acc, extra_refs, out_refs, j, ci):
    res_ref, g_ref, b_ref = extra_refs
    x_ref, xm_ref, xhat_ref, rstd_ref = out_refs
    u = ALPHA * res_ref[...] + acc
    mu = jnp.mean(u, axis=1, keepdims=True)
    cen = u - mu
    rstd = lax.rsqrt(jnp.mean(cen * cen, axis=1, keepdims=True) + LN_EPS)
    xhat = cen * rstd
    xhat_ref[...] = xhat
    x = xhat * g_ref[...] + b_ref[...]
    x_ref[...] = x
    xm_ref[...] = x.astype(xm_ref.dtype)
    rstd_ref[...] = rstd


def _mm_res_ln(name, a, w_full, res, g, b, tm, tk):
    S, D = res.shape
    row = pl.BlockSpec((tm, D), lambda i, j, k: (i, 0))
    vec = pl.BlockSpec((1, D), lambda i, j, k: (0, 0))
    outs = [(_sds((S, D), F32), row, True), (_sds((S, D), MXU_DTYPE), row, True), (_sds((S, D), F32), row, True),
            (_sds((S, 1), F32), pl.BlockSpec((tm, 1), lambda i, j, k: (i, 0)), True)]
    return _matmul(name, a, w_full, "nn", tm, D, tk, outs, _ln_epilogue, extras=[(res, row, True), (g, vec), (b, vec)],
                   split=("rows", 2) if tk == a.shape[1] else None)


def _ln_bwd_rows(dy, xh, rstd, g, first, du_ref, dum_ref, dg_ref, db_ref):
    if first is not None:
        @pl.when(first)
        def _():
            dg_ref[...] = jnp.zeros_like(dg_ref)
            db_ref[...] = jnp.zeros_like(db_ref)

    dg_ref[...] += jnp.sum(dy * xh, axis=0, keepdims=True)
    db_ref[...] += jnp.sum(dy, axis=0, keepdims=True)
    dxh = dy * g
    m1 = jnp.mean(dxh, axis=1, keepdims=True)
    m2 = jnp.mean(dxh * xh, axis=1, keepdims=True)
    du = rstd * (dxh - m1 - xh * m2)
    du_ref[...] = du
    dum_ref[...] = du.astype(dum_ref.dtype)


def _loss_ln_bwd(y, target, xhat, rstd, g, tm):
    S, D = y.shape

    def body(y_ref, t_ref, xh_ref, r_ref, g_ref, sq_ref, du_ref, dum_ref, dg_ref, db_ref):
        first = pl.program_id(0) == 0

        @pl.when(first)
        def _():
            sq_ref[...] = jnp.zeros_like(sq_ref)

        e = y_ref[...] - t_ref[...]
        sq_ref[...] += jnp.sum(e * e, axis=0, keepdims=True)
        _ln_bwd_rows(e / D, xh_ref[...], r_ref[...], g_ref[...], first, du_ref, dum_ref, dg_ref, db_ref)

    row = pl.BlockSpec((tm, D), lambda i: (i, 0))
    vec = pl.BlockSpec((1, D), lambda i: (0, 0))
    return pl.pallas_call(
        body,
        name="loss_ln_bwd",
        grid=(S // tm,),
        in_specs=[row, row, row, pl.BlockSpec((tm, 1), lambda i: (i, 0)), vec],
        out_specs=[vec, row, row, vec, vec],
        out_shape=[_sds((1, D), F32), _sds((S, D), F32), _sds((S, D), MXU_DTYPE), _sds((1, D), F32), _sds((1, D), F32)],
        compiler_params=_cparams(("arbitrary",)),
    )(y, target, xhat, rstd, g)


def _mlp_up(name, x, w_up, tm, tn, tk):
    S = x.shape[0]
    F = w_up.shape[1]

    def epilogue(acc, extra_refs, out_refs, j, ci):
        r = jnp.maximum(acc, 0.0)
        out_refs[0][...] = (r * r).astype(out_refs[0].dtype)

    return _matmul(name, x, w_up, "nn", tm, tn, tk, [(_sds((S, F), MXU_DTYPE), _ij_spec(tm, tn), True)], epilogue,
                   split=("cols", 2))[0]


def _mlp_down_bwd(name, dy, w_down, a, tm, tn, tk):
    S, F = a.shape

    def epilogue(acc, extra_refs, out_refs, j, ci):
        out_refs[0][...] = (acc * (2.0 * jnp.sqrt(extra_refs[0][...].astype(F32)))).astype(out_refs[0].dtype)

    return _matmul(name, dy, w_down, "nt", tm, tn, tk, [(_sds((S, F), MXU_DTYPE), _ij_spec(tm, tn), True)], epilogue,
                   extras=[(a, _ij_spec(tm, tn), True)], split=("cols", 2))[0]


def _mm_nt_res_ln_bwd(name, dy, w, du, xhat, rstd, g, tm, tk, dep):
    S, D = du.shape

    def epilogue(acc, extra_refs, out_refs, j, ci):
        du_ref, xh_ref, r_ref, g_ref = extra_refs
        first = (pl.program_id(0) == 0) if ci == 0 else None
        _ln_bwd_rows(ALPHA * du_ref[...] + acc, xh_ref[...], r_ref[...], g_ref[...], first, *out_refs)

    row = pl.BlockSpec((tm, D), lambda i, j, k: (i, 0))
    vec = pl.BlockSpec((1, D), lambda i, j, k: (0, 0))
    return _matmul(name, dy, w, "nt", tm, D, tk,
                   [(_sds((S, D), F32), row, True), (_sds((S, D), MXU_DTYPE), row, True), (_sds((1, D), F32), vec),
                    (_sds((1, D), F32), vec)], epilogue,
                   extras=[(du, row, True), (xhat, row, True), (rstd, pl.BlockSpec((tm, 1), lambda i, j, k: (i, 0)), True), (g, vec)],
                   dep=dep, sem=("arbitrary", "arbitrary", "arbitrary"), split=("rows", 2))


def _attn_out_bwd(du, w_out, o, sel_t, tm, tk):
    S, D = o.shape

    def epilogue(acc, extra_refs, out_refs, j, ci):
        out_refs[0][...] = acc
        out_refs[1][...] = _exact_nn(acc * extra_refs[0][...], extra_refs[1][...])

    row = pl.BlockSpec((tm, D), lambda i, j, k: (i, 0))
    slim = pl.BlockSpec((tm, LANES), lambda i, j, k: (i, 0))
    return _matmul("attn_out_bwd", du, w_out, "nt", tm, D, tk,
                   [(_sds((S, D), F32), row, True), (_sds((S, LANES), F32), slim, True)], epilogue,
                   extras=[(o, row, True), (sel_t, pl.BlockSpec((D, LANES), lambda i, j, k: (0, 0)))], split=("rows", 2))


def _adamw(name, w, gs, m, v):
    shape = w.shape
    cols = shape[-1]
    rows = math.prod(shape[:-1])
    w2, m2, v2 = (t.reshape(rows, cols) for t in (w, m, v))
    gs2 = [g.reshape(-1, cols) for g in gs]
    ng = len(gs2)
    tr = _pick(rows // ng, (256, 128, 64, 32, 16, 8))
    per = rows // ng // tr
    c1 = 1.0 - ADAM_B1 ** ADAM_STEP
    c2 = 1.0 - ADAM_B2 ** ADAM_STEP

    def body(w_ref, m_ref, v_ref, *rest):
        g_refs, (d_ref, nm_ref, nv_ref), g_out = rest[:ng], rest[ng:ng + 3], rest[ng + 3:]
        gg = g_refs[0][...]
        if ng == 2:
            gg = jnp.where(pl.program_id(0) < per, gg, g_refs[1][...])
            g_out[0][...] = gg
        nm = ADAM_B1 * m_ref[...] + (1.0 - ADAM_B1) * gg
        nv = ADAM_B2 * v_ref[...] + (1.0 - ADAM_B2) * (gg * gg)
        nm_ref[...] = nm
        nv_ref[...] = nv
        d_ref[...] = -ADAM_LR * ((nm / c1) / (jnp.sqrt(nv / c2) + ADAM_EPS) + ADAM_WD * w_ref[...])

    blk = pl.BlockSpec((tr, cols), lambda i: (i, 0))
    g_specs = [blk] if ng == 1 else [pl.BlockSpec((tr, cols), lambda i: (jnp.minimum(i, per - 1), 0)),
                                     pl.BlockSpec((tr, cols), lambda i: (jnp.maximum(i - per, 0), 0))]
    nout = 3 if ng == 1 else 4
    outs = pl.pallas_call(
        body,
        name=name,
        grid=(rows // tr,),
        in_specs=[blk] * 3 + g_specs,
        out_specs=[blk] * nout,
        out_shape=[_sds((rows, cols), F32)] * nout,
        compiler_params=_cparams(("parallel",)),
    )(w2, m2, v2, *gs2)
    g_full = outs[3] if ng == 2 else gs2[0]
    return tuple(o.reshape(shape) for o in (outs[0], outs[1], outs[2], g_full))


HBM = pl.BlockSpec(memory_space=pl.ANY)


def _shard_slice(ref, axis, size, index):
    idx = [slice(None)] * len(ref.shape)
    idx[axis] = pl.ds(pl.multiple_of(index * size, 8), size)
    return ref.at[tuple(idx)]


def _share_halves(name, full, tr):
    R, W4 = full.shape
    W, h = W4 // 4, R // 2
    steps = [(k, t) for k in range(3) for t in range(h // tr)]

    def body(f_in, f_ref, buf, lsem, ssem, rsem):
        x, y, c = lax.axis_index("x"), lax.axis_index("y"), lax.axis_index("c")
        sibling = (x, y, 1 - c)
        chips = [(1 - x, y), (x, 1 - y), (1 - x, 1 - y)]

        def tile(k, t):
            px, py = chips[k]
            return f_ref.at[pl.ds(pl.multiple_of(c * h + t * tr, 8), tr), pl.ds(pl.multiple_of((2 * px + py) * W, LANES), W)]

        sends = []
        for s, (k, t) in enumerate(steps):
            slot = s % 2
            if s >= 2:
                sends[s - 2].wait_send()
            lc = pltpu.make_async_copy(tile(k, t), buf.at[slot], lsem.at[slot])
            lc.start()
            lc.wait()
            rc = pltpu.make_async_remote_copy(src_ref=buf.at[slot], dst_ref=tile(k, t), send_sem=ssem.at[slot], recv_sem=rsem,
                                              device_id=sibling, device_id_type=MESH)
            rc.start()
            sends.append(rc)
        for rc in sends[-2:]:
            rc.wait_send()
        whole = f_ref.at[pl.ds(0, h), pl.ds(0, 3 * W)]
        pltpu.make_async_remote_copy(src_ref=whole, dst_ref=whole, send_sem=ssem.at[0], recv_sem=rsem,
                                     device_id=sibling, device_id_type=MESH).wait_recv()

    return pl.pallas_call(
        body,
        name=name,
        in_specs=[HBM],
        out_specs=HBM,
        out_shape=_sds(full.shape, full.dtype),
        input_output_aliases={0: 0},
        scratch_shapes=[pltpu.VMEM((2, tr, W), full.dtype), pltpu.SemaphoreType.DMA((2,)), pltpu.SemaphoreType.DMA((2,)),
                        pltpu.SemaphoreType.DMA(())],
    )(full)


IN_HBM = pl.BlockSpec(memory_space=pltpu.HBM)
IN_SEM = pl.BlockSpec(memory_space=pltpu.SEMAPHORE)
DATAFLOW = pltpu.SideEffectType.DATAFLOW_SIDE_EFFECTING


def _hbm(t):
    return pltpu.with_memory_space_constraint(t, pltpu.HBM)


def _token_spec():
    return pl.BlockSpec(memory_space=pltpu.VMEM)


def _gather_copies(s_refs, f_refs, axes, halves, send, recv, loc, arrival):
    x, y, c = lax.axis_index("x"), lax.axis_index("y"), lax.axis_index("c")
    chips = [(1 - x, y), (x, 1 - y), (1 - x, 1 - y)]
    local, remote = [], []
    for a in range(len(s_refs)):
        size = s_refs[a].shape[axes[a]]
        local.append(pltpu.make_async_copy(s_refs[a], _shard_slice(f_refs[a], axes[a], size, 2 * x + y), loc.at[a]))
        for k, (px, py) in enumerate(chips):
            block = (2 * px + py) if arrival else (2 * x + y)
            src, dst = s_refs[a], _shard_slice(f_refs[a], axes[a], size, block)
            if halves:
                assert axes[a] == 1 and len(s_refs[a].shape) == 2
                h = s_refs[a].shape[0] // 2
                rows = pl.ds(pl.multiple_of(c * h, 8), h)
                src = s_refs[a].at[rows, :]
                dst = f_refs[a].at[rows, pl.ds(pl.multiple_of(block * size, LANES), size)]
            remote.append(pltpu.make_async_remote_copy(src_ref=src, dst_ref=dst, send_sem=send.at[3 * a + k],
                                                       recv_sem=recv.at[3 * a + k], device_id=(px, py, c), device_id_type=MESH))
    return local, remote


def _gather_start(name, shards, axes, after, halves=False):
    n = len(shards)
    fulls = []
    for s, ax in zip(shards, axes):
        fs = list(s.shape)
        fs[ax] *= 4
        fulls.append(lax.empty(tuple(fs), s.dtype))

    def body(*refs):
        s_refs, f_refs = refs[:n], refs[n:2 * n]
        send, recv, loc, token = refs[2 * n + 1], refs[2 * n + 2], refs[2 * n + 3], refs[-1]
        local, remote = _gather_copies(s_refs, f_refs, axes, halves, send, recv, loc, arrival=False)
        for cp in remote + local:
            cp.start()
        token[...] = jnp.zeros_like(token)

    outs = pl.pallas_call(
        body,
        name=name,
        out_shape=(pltpu.SemaphoreType.DMA((3 * n,)), pltpu.SemaphoreType.DMA((3 * n,)), pltpu.SemaphoreType.DMA((n,)),
                   *[pltpu.HBM(t.shape, t.dtype) for t in shards + fulls], _sds((8, LANES), F32)),
        in_specs=[IN_HBM] * (2 * n) + [HBM],
        out_specs=(IN_SEM, IN_SEM, IN_SEM, *[IN_HBM] * (2 * n), _token_spec()),
        input_output_aliases={i: 3 + i for i in range(2 * n)},
        compiler_params=pltpu.CompilerParams(has_side_effects=DATAFLOW),
    )(*[_hbm(t) for t in shards + fulls], after)
    return (outs[0], outs[1], outs[2], list(outs[3:3 + n]), list(outs[3 + n:3 + 2 * n]), axes, halves), outs[-1]


def _gather_wait(name, state, *after):
    send, recv, loc, s_thru, f_thru, axes, halves = state
    n = len(s_thru)

    def body(*refs):
        s_refs, f_refs = refs[:n], refs[n:2 * n]
        local, remote = _gather_copies(s_refs, f_refs, axes, halves, refs[2 * n], refs[2 * n + 1], refs[2 * n + 2], arrival=True)
        for cp in local:
            cp.wait()
        for cp in remote:
            cp.wait_send()
            cp.wait_recv()

    outs = pl.pallas_call(
        body,
        name=name,
        out_shape=tuple(pltpu.HBM(t.shape, t.dtype) for t in s_thru + f_thru),
        in_specs=[IN_HBM] * (2 * n) + [IN_SEM, IN_SEM, IN_SEM] + [HBM] * len(after),
        out_specs=tuple([IN_HBM] * (2 * n)),
        input_output_aliases={i: i for i in range(2 * n)},
        compiler_params=pltpu.CompilerParams(has_side_effects=DATAFLOW),
    )(*s_thru, *f_thru, send, recv, loc, *after)
    return list(outs[n:2 * n])


FLIPS = [(fx, fy, fc) for fx in (0, 1) for fy in (0, 1) for fc in (0, 1)][1:]


def _piece_shape(shape, axis):
    ps = list(shape)
    if axis == 0:
        ps[0] //= 8
    else:
        ps[0] //= 2
        ps[axis] //= 4
    return tuple(ps)


def _piece(ref, axis, q, c):
    shape = ref.shape
    idx = [slice(None)] * len(shape)
    if axis == 0:
        h = shape[0] // 8
        idx[0] = pl.ds(pl.multiple_of((2 * q + c) * h, 8), h)
    else:
        h, w = shape[0] // 2, shape[axis] // 4
        idx[0] = pl.ds(c * h, h)
        idx[axis] = pl.ds(pl.multiple_of(q * w, LANES if axis == len(shape) - 1 else 8), w)
    return ref.at[tuple(idx)]


def _own_piece(g, axis):
    ps = _piece_shape(g.shape, axis)
    q, c = 2 * lax.axis_index("x") + lax.axis_index("y"), lax.axis_index("c")
    start = [0] * len(ps)
    if axis == 0:
        start[0] = (2 * q + c) * ps[0]
    else:
        start[0] = c * ps[0]
        start[axis] = q * ps[axis]
    return lax.dynamic_slice(g, start, ps)


def _scatter_copies(g_refs, l_refs, axes, send, recv):
    x, y, c = lax.axis_index("x"), lax.axis_index("y"), lax.axis_index("c")
    out = []
    for a in range(len(g_refs)):
        for k, (fx, fy, fc) in enumerate(FLIPS):
            tx, ty, tc = x ^ fx, y ^ fy, c ^ fc
            out.append(pltpu.make_async_remote_copy(
                src_ref=_piece(g_refs[a], axes[a], 2 * tx + ty, tc), dst_ref=l_refs[a].at[k],
                send_sem=send.at[7 * a + k], recv_sem=recv.at[7 * a + k], device_id=(tx, ty, tc), device_id_type=MESH))
    return out


def _scatter_start(name, grads, axes):
    n = len(grads)
    lands = [lax.empty((7,) + _piece_shape(g.shape, ax), g.dtype) for g, ax in zip(grads, axes)]

    def body(*refs):
        g_refs, l_refs = refs[:n], refs[n:2 * n]
        send, recv, token = refs[2 * n], refs[2 * n + 1], refs[-1]
        for cp in _scatter_copies(g_refs, l_refs, axes, send, recv):
            cp.start()
        token[...] = jnp.zeros_like(token)

    outs = pl.pallas_call(
        body,
        name=name,
        out_shape=(pltpu.SemaphoreType.DMA((7 * n,)), pltpu.SemaphoreType.DMA((7 * n,)),
                   *[pltpu.HBM(t.shape, t.dtype) for t in grads + lands], _sds((8, LANES), F32)),
        in_specs=[IN_HBM] * (2 * n),
        out_specs=(IN_SEM, IN_SEM, *[IN_HBM] * (2 * n), _token_spec()),
        input_output_aliases={i: 2 + i for i in range(2 * n)},
        compiler_params=pltpu.CompilerParams(has_side_effects=DATAFLOW),
    )(*[_hbm(t) for t in grads + lands])
    return (outs[0], outs[1], list(outs[2:2 + n]), list(outs[2 + n:2 + 2 * n]), axes), outs[-1]


def _scatter_wait(name, state, *after):
    send, recv, g_thru, l_thru, axes = state
    n = len(g_thru)

    def body(*refs):
        g_refs, l_refs = refs[:n], refs[n:2 * n]
        for cp in _scatter_copies(g_refs, l_refs, axes, refs[2 * n], refs[2 * n + 1]):
            cp.wait_send()
            cp.wait_recv()

    outs = pl.pallas_call(
        body,
        name=name,
        out_shape=tuple(pltpu.HBM(t.shape, t.dtype) for t in g_thru + l_thru),
        in_specs=[IN_HBM] * (2 * n) + [IN_SEM, IN_SEM] + [HBM] * len(after),
        out_specs=tuple([IN_HBM] * (2 * n)),
        input_output_aliases={i: i for i in range(2 * n)},
        compiler_params=pltpu.CompilerParams(has_side_effects=DATAFLOW),
    )(*g_thru, *l_thru, send, recv, *after)
    return list(outs[:n]), list(outs[n:2 * n])


def _reduce_join(name, landing, own):
    piece = own.shape
    C = piece[-1]
    R = math.prod(piece[:-1])
    l3 = landing.reshape(7, R, C)
    own2 = own.reshape(R, C)
    tr = _pick(R, [t for t in (512, 256, 128, 64, 32, 16, 8) if t * C <= 256 * 1024])
    nsteps = R // tr

    def body(own_ref, l_ref, o_ref, buf, send, loc, recv):
        i = pl.program_id(0)
        x, y, c = lax.axis_index("x"), lax.axis_index("y"), lax.axis_index("c")
        sibling = (x, y, 1 - c)

        def copies(slot, step):
            dst = o_ref.at[pl.ds(pl.multiple_of(c * R + step * tr, 8), tr), :]
            return (pltpu.make_async_copy(buf.at[slot], dst, loc.at[slot]),
                    pltpu.make_async_remote_copy(src_ref=buf.at[slot], dst_ref=dst, send_sem=send.at[slot], recv_sem=recv,
                                                 device_id=sibling, device_id_type=MESH))

        @pl.when(i >= 2)
        def _():
            lc, rc = copies(i % 2, i - 2)
            lc.wait()
            rc.wait_send()

        acc = own_ref[...].astype(F32)
        for s in range(7):
            acc = acc + l_ref[s].astype(F32)
        buf[i % 2] = acc
        lc, rc = copies(i % 2, i)
        lc.start()
        rc.start()

        @pl.when(i == nsteps - 1)
        def _():
            for st in range(max(nsteps - 2, 0), nsteps):
                lc, rc = copies(st % 2, st)
                lc.wait()
                rc.wait_send()
            theirs = o_ref.at[pl.ds(pl.multiple_of((1 - c) * R, 8), R), :]
            pltpu.make_async_remote_copy(src_ref=theirs, dst_ref=theirs, send_sem=send.at[0], recv_sem=recv,
                                         device_id=sibling, device_id_type=MESH).wait_recv()

    return pl.pallas_call(
        body,
        name=name,
        grid=(nsteps,),
        in_specs=[pl.BlockSpec((tr, C), lambda i: (i, 0)), pl.BlockSpec((7, tr, C), lambda i: (0, i, 0))],
        out_specs=HBM,
        out_shape=_sds((2 * R, C), F32),
        scratch_shapes=[pltpu.VMEM((2, tr, C), F32), pltpu.SemaphoreType.DMA((2,)), pltpu.SemaphoreType.DMA((2,)),
                        pltpu.SemaphoreType.DMA(())],
        compiler_params=_cparams(("arbitrary",)),
    )(own2, l3)


def _all_reduce_small(v, dep):
    R, D = v.shape

    def body(v_ref, dep_ref, o_ref, land, send, recv):
        x, y, c = lax.axis_index("x"), lax.axis_index("y"), lax.axis_index("c")
        my_slot = 4 * x + 2 * y + c
        land[my_slot] = v_ref[...]
        for k, (fx, fy, fc) in enumerate(FLIPS):
            tx, ty, tc = x ^ fx, y ^ fy, c ^ fc
            pltpu.make_async_remote_copy(src_ref=v_ref, dst_ref=land.at[my_slot], send_sem=send.at[k], recv_sem=recv.at[k],
                                         device_id=(tx, ty, tc), device_id_type=MESH).start()
        for k, (fx, fy, fc) in enumerate(FLIPS):
            tx, ty, tc = x ^ fx, y ^ fy, c ^ fc
            cp = pltpu.make_async_remote_copy(src_ref=v_ref, dst_ref=land.at[4 * tx + 2 * ty + tc], send_sem=send.at[k],
                                              recv_sem=recv.at[k], device_id=(tx, ty, tc), device_id_type=MESH)
            cp.wait_send()
            cp.wait_recv()
        acc = land[0]
        for s in range(1, 8):
            acc = acc + land[s]
        o_ref[...] = acc

    return pl.pallas_call(
        body,
        name="all_reduce_small",
        in_specs=[pl.BlockSpec(memory_space=pltpu.VMEM), pl.BlockSpec(memory_space=pl.ANY)],
        out_specs=pl.BlockSpec(memory_space=pltpu.VMEM),
        out_shape=_sds((R, D), F32),
        scratch_shapes=[pltpu.VMEM((8, R, D), F32), pltpu.SemaphoreType.DMA((7,)), pltpu.SemaphoreType.DMA((7,))],
    )(v, dep)


def kernel(x, attn_w_in, attn_w_out, hgrn_w_in, hgrn_w_out, hgrn_norm_g, lb_logits, ln_mix_g, ln_mix_b, ln_ffn_g, ln_ffn_b, ffn_w_up, ffn_w_down, loss_target, m_attn_w_in, m_attn_w_out, m_hgrn_w_in, m_hgrn_w_out, m_hgrn_norm_g, m_lb_logits, m_ln_mix_g, m_ln_mix_b, m_ln_ffn_g, m_ln_ffn_b, m_ffn_w_up, m_ffn_w_down, v_attn_w_in, v_attn_w_out, v_hgrn_w_in, v_hgrn_w_out, v_hgrn_norm_g, v_lb_logits, v_ln_mix_g, v_ln_mix_b, v_ln_ffn_g, v_ln_ffn_b, v_ffn_w_up, v_ffn_w_down):
    xs = x[0]
    tgt = loss_target[0]
    S, D = xs.shape
    F = ffn_w_up.shape[2] * 4
    T1 = _pick(S, (1024, 512, 256))
    T2 = _pick(S, (2048, 1024, 512))
    TH = _pick(S, (512, 256))
    TB = _pick(S, (128,))
    TN = _pick(D, (512, 256, 128))
    TF = _pick(F, (1024, 512))
    TG = _pick(3 * D, (1536, 1024, 768))
    TW = _pick(F, (2048, 1024))

    cast = lambda w: w.astype(MXU_DTYPE)
    st_a, tok = _gather_start("gather_a", [cast(attn_w_in[0])], [1], jnp.zeros((8, LANES), F32), halves=True)
    tok, (xs_late, w_aout, w_fup, w_fdown, w_hin, w_hout) = lax.optimization_barrier(
        (tok, (xs, attn_w_out, ffn_w_up, ffn_w_down, hgrn_w_in, hgrn_w_out)))
    st_b, tok = _gather_start("gather_b", [cast(w_aout[0]), cast(w_fup[0]), cast(w_fdown[0])], [0, 1, 0], tok)
    st_c, tok = _gather_start("gather_c", [cast(w_hin[0]), cast(w_hout[0]), hgrn_norm_g, cast(w_fup[1]), cast(w_fdown[1])],
                              [1, 0, 1, 1, 0], tok)

    cos3, sin3 = _rope_tables(S)
    sel = _head_sel(D)
    sel_t = sel.T

    xc3 = _stack_classes("x_classes", xs_late, MXU_DTYPE)
    (wa_in,) = _gather_wait("gather_a_wait", st_a, tok, xc3, cos3, sin3)
    wa_in = _share_halves("share_a", wa_in, _pick(D // 2, (256, 128)))
    P3 = _attn_proj(xc3, wa_in, cos3, sin3, T2, TN)
    o3, lse3 = _attn_fwd(P3, D)
    o_att, L_att = _attn_mix(o3, lse3, sel)
    wa_out, w_up0, w_down0 = _gather_wait("gather_b_wait", st_b, L_att)
    x1, xm1, xh1, r1 = _mm_res_ln("attn_out_ln", o_att, wa_out, xs, ln_mix_g[0:1], ln_mix_b[0:1], TH, D)
    a0 = _mlp_up("mlp0_up", xm1, w_up0, T2, TF, D)
    x2, xm2, xh2, r2 = _mm_res_ln("mlp0_down_ln", a0, w_down0, x1, ln_ffn_g[0:1], ln_ffn_b[0:1], TH, F)

    wh_in, wh_out, norm_g, w_up1, w_down1 = _gather_wait("gather_c_wait", st_c, r2)
    P1 = _plain_mm("hgrn_proj", xm2, wh_in, "nn", F32, T1, _pick(3 * D, (1024, 768, 512)), D)
    o_h, n_h, states = _hgrn_fwd(P1, lb_logits, norm_g, TB)
    x3, xm3, xh3, r3 = _mm_res_ln("hgrn_out_ln", n_h, wh_out, x2, ln_mix_g[1:2], ln_mix_b[1:2], TH, D)
    a1 = _mlp_up("mlp1_up", xm3, w_up1, T2, TF, D)
    x4, _, xh4, r4 = _mm_res_ln("mlp1_down_ln", a1, w_down1, x3, ln_ffn_g[1:2], ln_ffn_b[1:2], TH, F)

    wgrad = lambda name, a, dy, tm, tn: _plain_mm(name, a, dy, "tn", MXU_DTYPE, tm, tn, T1)
    sq, du4, dum4, dg_ffn1, db_ffn1 = _loss_ln_bwd(x4, tgt, xh4, r4, ln_ffn_g[1:2], TH)
    dh1 = _mlp_down_bwd("mlp1_down_bwd", dum4, w_down1, a1, T2, TF, D)
    g_down1 = wgrad("g_down1", a1, dum4, TW, D)
    g_up1 = wgrad("g_up1", xm3, dh1, D, TW)
    sc_1, tok = _scatter_start("scatter_1", [g_down1, g_up1], [0, 1])
    du3, dum3, dg_mix1, db_mix1 = _mm_nt_res_ln_bwd("mlp1_up_bwd", dh1, w_up1, du4, xh3, r3, ln_mix_g[1:2], TH, F, tok)
    dn = _plain_mm("hgrn_out_bwd", dum3, wh_out, "nt", F32, T1, D, D)
    g_hout = wgrad("g_hgrn_out", n_h, dum3, D, D)
    dP1, dg_norm, dlb = _hgrn_bwd(P1, o_h, states, dn, lb_logits, norm_g, TB)
    g_hin = wgrad("g_hgrn_in", xm2, dP1, D, TG)
    d_lb_logits = _lb_logits_grad(dlb, lb_logits)
    sc_2, tok = _scatter_start("scatter_2", [g_hout, g_hin], [0, 1])

    du2, dum2, dg_ffn0, db_ffn0 = _mm_nt_res_ln_bwd("hgrn_in_bwd", dP1, wh_in, du3, xh2, r2, ln_ffn_g[0:1], TH, 3 * D, tok)
    dh0 = _mlp_down_bwd("mlp0_down_bwd", dum2, w_down0, a0, T2, TF, D)
    g_down0 = wgrad("g_down0", a0, dum2, TW, D)
    g_up0 = wgrad("g_up0", xm1, dh0, D, TW)
    sc_3, tok = _scatter_start("scatter_3", [g_down0, g_up0], [0, 1])
    du1, dum1, dg_mix0, db_mix0 = _mm_nt_res_ln_bwd("mlp0_up_bwd", dh0, w_up0, du2, xh1, r1, ln_mix_g[0:1], TH, F, tok)
    do, delta = _attn_out_bwd(dum1, wa_out, o_att, sel_t, TH, D)
    g_aout = wgrad("g_attn_out", o_att, dum1, D, D)
    sc_5, tok = _scatter_start("scatter_5", [g_aout], [0])
    dP3 = _attn_bwd(P3, _stack_classes("do_classes", do, MXU_DTYPE), _stack_classes("lse_classes", L_att, F32),
                    _stack_classes("delta_classes", delta, F32), cos3, sin3, D, tok)
    small = jnp.concatenate([d_lb_logits, dg_mix0, dg_mix1, db_mix0, db_mix1, dg_ffn0, dg_ffn1, db_ffn0, db_ffn1,
                             dg_norm, sq, jnp.zeros((4, D), F32)], axis=0)
    small = _all_reduce_small(small, dP3)
    loss = 0.5 * jnp.sum(small[11]) / D
    grp = lambda j: j // (3 * D // TG)
    g_ain = _matmul("g_attn_in", xc3, dP3, "tn", D, TG, T1, [(_sds((D, 9 * D), MXU_DTYPE), _ij_spec(D, TG))], _store_epilogue,
                    a_map=lambda i, j, k: (k + grp(j) * (S // T1), i),
                    b_map=lambda i, j, k: (k + grp(j) * (S // T1), j % (3 * D // TG)), mnk=(D, 9 * D, S), dep=small)[0]
    sc_4, tok = _scatter_start("scatter_4", [g_ain], [1])
    dxc3 = _matmul("attn_in_bwd", dP3, wa_in, "nt", T1, D, 3 * D, [(_sds((3 * S, D), F32), _ij_spec(T1, D))], _store_epilogue,
                   b_map=lambda i, j, k: (j, k + i // (S // T1)), mnk=(3 * S, D, 3 * D), dep=tok)[0]
    grad_x = _input_grad(du1, dxc3)

    def reduced(name, state, *after):
        gs, lands = _scatter_wait(name + "_wait", state, *after)
        return [_reduce_join(f"{name}_reduce_{i}", l, _own_piece(g, ax)) for i, (l, g, ax) in enumerate(zip(lands, gs, state[4]))]

    r_down1, r_up1 = reduced("scatter_1", sc_1, grad_x)
    r_hout, r_hin = reduced("scatter_2", sc_2, r_up1)
    r_down0, r_up0 = reduced("scatter_3", sc_3, r_hin)
    (r_aout,) = reduced("scatter_5", sc_5, r_up0)

    my_chip = 2 * lax.axis_index("x") + lax.axis_index("y")
    nsh = hgrn_norm_g.shape[1]
    g_norm = lax.dynamic_slice(small[10:11], (0, my_chip * nsh), (1, nsh))

    grads, upd = {}, {}

    def update(nm, w, gs, m, v):
        upd[nm] = _adamw("adamw_" + nm, w, gs, m, v)
        grads[nm] = upd[nm][3]

    update("hgrn_w_in", hgrn_w_in, [r_hin], m_hgrn_w_in, v_hgrn_w_in)
    update("hgrn_w_out", hgrn_w_out, [r_hout], m_hgrn_w_out, v_hgrn_w_out)
    update("ffn_w_up", ffn_w_up, [r_up0, r_up1], m_ffn_w_up, v_ffn_w_up)
    update("ffn_w_down", ffn_w_down, [r_down0, r_down1], m_ffn_w_down, v_ffn_w_down)
    update("attn_w_out", attn_w_out, [r_aout], m_attn_w_out, v_attn_w_out)
    update("hgrn_norm_g", hgrn_norm_g, [g_norm], m_hgrn_norm_g, v_hgrn_norm_g)
    cat = lambda ts: jnp.concatenate(ts, axis=0)
    small_w = cat([lb_logits, ln_mix_g, ln_mix_b, ln_ffn_g, ln_ffn_b])
    small_m = cat([m_lb_logits, m_ln_mix_g, m_ln_mix_b, m_ln_ffn_g, m_ln_ffn_b])
    small_v = cat([v_lb_logits, v_ln_mix_g, v_ln_mix_b, v_ln_ffn_g, v_ln_ffn_b])
    small_upd = _adamw("adamw_small", small_w, [small[0:10]], small_m, small_v)
    for i, nm in enumerate(["lb_logits", "ln_mix_g", "ln_mix_b", "ln_ffn_g", "ln_ffn_b"]):
        grads[nm] = small[2 * i:2 * i + 2]
        upd[nm] = tuple(t[2 * i:2 * i + 2] for t in small_upd)
    done = [upd[k][2] for k in ("hgrn_w_in", "hgrn_w_out", "ffn_w_up", "ffn_w_down", "attn_w_out", "hgrn_norm_g")]
    (r_ain,) = reduced("scatter_4", sc_4, small_upd[2], *done)
    update("attn_w_in", attn_w_in, [r_ain], m_attn_w_in, v_attn_w_in)

    order = ["attn_w_in", "attn_w_out", "hgrn_w_in", "hgrn_w_out", "hgrn_norm_g", "lb_logits", "ln_mix_g", "ln_mix_b",
             "ln_ffn_g", "ln_ffn_b", "ffn_w_up", "ffn_w_down"]
    return (loss, grad_x[None], *[grads[k] for k in order], *[upd[k][0] for k in order],
            *[upd[k][1] for k in order], *[upd[k][2] for k in order])
```

```python
import math

import jax
import jax.numpy as jnp
from jax import lax
from jax.experimental import pallas as pl
from jax.experimental.pallas import tpu as pltpu

F32 = jnp.float32
BF16 = jnp.bfloat16
MXU_DTYPE = BF16

HEAD_DIM = 64
ATTN_BLK = 128
DILATIONS = (1, 4, 16)
ROPE_THETA = 10000.0
HGRN_DK = 128
HGRN_CHUNK = 64
DEPTH = 2
LN_EPS = 1e-5
RMS_EPS = 1e-6
ALPHA = (2 * DEPTH) ** 0.25
ADAM_LR, ADAM_B1, ADAM_B2, ADAM_EPS, ADAM_WD, ADAM_STEP = 0.001, 0.9, 0.999, 1e-08, 0.01, 10

LANES = 128
VMEM_LIMIT = 56 * 1024 * 1024
NEG = -1e30
MESH = pl.DeviceIdType.MESH


def _cparams(sem=None):
    return pltpu.CompilerParams(dimension_semantics=sem, vmem_limit_bytes=VMEM_LIMIT)


def _sds(shape, dtype):
    return jax.ShapeDtypeStruct(tuple(shape), dtype)


def _dg(a, b, ca, cb):
    return lax.dot_general(a, b, (((ca,), (cb,)), ((), ())), preferred_element_type=F32)


def _nn(a, b):
    return _dg(a, b, 1, 0)


def _nt(a, b):
    return _dg(a, b, 1, 1)


def _tn(a, b):
    return _dg(a, b, 0, 0)


def _split3(a):
    hi = a.astype(BF16)
    r = a - hi.astype(F32)
    mid = r.astype(BF16)
    lo = (r - mid.astype(F32)).astype(BF16)
    return hi, mid, lo


def _exact_nn(a, sel):
    hi, mid, lo = _split3(a)
    return _nn(hi, sel) + _nn(mid, sel) + _nn(lo, sel)


def _pick(n, prefs):
    for p in prefs:
        if n % p == 0:
            return p
    return n


def _matmul(name, a, b, form, tm, tn, tk, outs, epilogue, extras=(), a_map=None, b_map=None, mnk=None, dep=None,
            sem=("parallel", "parallel", "arbitrary"), split=None, alias_dep=False):
    if form == "nn":
        (M, K), N = a.shape, b.shape[1]
        a_spec = pl.BlockSpec((tm, tk), a_map or (lambda i, j, k: (i, k)))
        b_spec = pl.BlockSpec((tk, tn), b_map or (lambda i, j, k: (k, j)))
        ca, cb = 1, 0
    elif form == "nt":
        (M, K), N = a.shape, b.shape[0]
        a_spec = pl.BlockSpec((tm, tk), a_map or (lambda i, j, k: (i, k)))
        b_spec = pl.BlockSpec((tn, tk), b_map or (lambda i, j, k: (j, k)))
        ca, cb = 1, 1
    else:
        (K, M), N = a.shape, b.shape[1]
        a_spec = pl.BlockSpec((tk, tm), a_map or (lambda i, j, k: (k, i)))
        b_spec = pl.BlockSpec((tk, tn), b_map or (lambda i, j, k: (k, j)))
        ca, cb = 0, 0
    if mnk is not None:
        M, N, K = mnk
    assert M % tm == 0 and N % tn == 0 and K % tk == 0, (name, M, N, K, tm, tn, tk)
    nk = K // tk
    ne, no = len(extras), len(outs)
    deps = [] if dep is None else [dep]
    nd = len(deps)

    def body(a_ref, b_ref, *rest):
        extra_refs, out_refs = rest[:ne], rest[ne + nd:ne + nd + no]
        j = pl.program_id(1)
        if split is not None:
            kind, n = split
            assert nk == 1 and form != "tn"
            tiled = [t for _, _, *t in list(extras) + list(outs)]
            refs = list(extra_refs) + list(out_refs)
            for ci in range(n):
                if kind == "cols":
                    cs = slice(ci * (tn // n), (ci + 1) * (tn // n))
                    part = _dg(a_ref[...].astype(MXU_DTYPE), (b_ref[:, cs] if form == "nn" else b_ref[cs, :]).astype(MXU_DTYPE), ca, cb)
                    view = [r.at[:, cs] if t else r for r, t in zip(refs, tiled)]
                else:
                    rs = slice(ci * (tm // n), (ci + 1) * (tm // n))
                    part = _dg(a_ref[rs, :].astype(MXU_DTYPE), b_ref[...].astype(MXU_DTYPE), ca, cb)
                    view = [r.at[rs, :] if t else r for r, t in zip(refs, tiled)]
                epilogue(part, view[:ne], view[ne:], j, ci)
            return
        part = _dg(a_ref[...].astype(MXU_DTYPE), b_ref[...].astype(MXU_DTYPE), ca, cb)
        if nk == 1:
            epilogue(part, extra_refs, out_refs, j, 0)
            return
        acc_ref = rest[-1]
        k = pl.program_id(2)

        @pl.when(k == 0)
        def _():
            acc_ref[...] = part

        @pl.when(k > 0)
        def _():
            acc_ref[...] += part

        @pl.when(k == nk - 1)
        def _():
            epilogue(acc_ref[...], extra_refs, out_refs, j, 0)

    res = pl.pallas_call(
        body,
        name=name,
        grid=(M // tm, N // tn, nk),
        in_specs=[a_spec, b_spec] + [s for _, s, *_ in extras] + [pl.BlockSpec(memory_space=pl.ANY)] * nd,
        out_specs=[s for _, s, *_ in outs],
        out_shape=[o for o, *_ in outs],
        scratch_shapes=[pltpu.VMEM((tm, tn), F32)] if nk > 1 else [],
        input_output_aliases={2 + ne: 0} if alias_dep else {},
        compiler_params=_cparams(sem),
    )(a, b, *[e for e, *_ in extras], *deps)
    return res


def _ij_spec(tm, tn):
    return pl.BlockSpec((tm, tn), lambda i, j, k: (i, j))


def _store_epilogue(acc, extra_refs, out_refs, j, ci):
    out_refs[0][...] = acc.astype(out_refs[0].dtype)


def _plain_mm(name, a, b, form, out_dtype, tm, tn, tk):
    M = a.shape[1] if form == "tn" else a.shape[0]
    N = b.shape[0] if form == "nt" else b.shape[1]
    return _matmul(name, a, b, form, tm, tn, tk, [(_sds((M, N), out_dtype), _ij_spec(tm, tn))], _store_epilogue)[0]


def _class_slabs(S):
    assert DILATIONS[0] == 1
    return [(g, d, r, S // d) for g, d in enumerate(DILATIONS) if d > 1 for r in range(d)]


def _stack_classes(name, t, out_dtype):
    S, W = t.shape

    def body(x_ref, o_ref):
        o_ref[0:S, :] = x_ref[...].astype(out_dtype)
        for g, d, r, n in _class_slabs(S):
            o_ref[g * S + r * n:g * S + (r + 1) * n, :] = x_ref[pl.ds(r, n, stride=d), :].astype(out_dtype)

    return pl.pallas_call(
        body,
        name=name,
        grid=(W // LANES,),
        in_specs=[pl.BlockSpec((S, LANES), lambda j: (0, j))],
        out_specs=pl.BlockSpec((3 * S, LANES), lambda j: (0, j)),
        out_shape=_sds((3 * S, W), out_dtype),
        compiler_params=_cparams(("parallel",)),
    )(t)


def _rope_tables(seq):
    half = HEAD_DIM // 2
    inv = ROPE_THETA ** (-jnp.arange(half, dtype=F32) * (2.0 / HEAD_DIM))
    inv = jnp.tile(inv, LANES // half)
    pos = []
    for d in DILATIONS:
        row = jnp.arange(seq)
        pos.append((row % (seq // d)) * d + row // (seq // d))
    ang = jnp.concatenate(pos).astype(F32)[:, None] * inv[None, :]
    first = (jnp.arange(LANES) % HEAD_DIM) < half
    sin = jnp.sin(ang)
    return jnp.cos(ang), jnp.where(first[None, :], -sin, sin)


def _partner(x):
    half = HEAD_DIM // 2
    lane = lax.broadcasted_iota(jnp.int32, x.shape, 1)
    first = (lane % HEAD_DIM) < half
    return jnp.where(first, pltpu.roll(x, LANES - half, 1), pltpu.roll(x, half, 1))


def _attn_proj(name, x3, w, cos3, sin3, tm, prev, dep=None):
    S3, D = x3.shape
    S = S3 // 3
    tn = 3 * D // 4
    nrow = S // tm
    local = prev is None

    def tile(j):
        q = 2 * lax.axis_index("x") + lax.axis_index("y")
        c0 = 3 * q + j if local else j + 3 * (j >= 3 * q).astype(jnp.int32)
        return c0, c0 // 4, c0 % 4

    def epilogue(acc, extra_refs, out_refs, j, ci):
        cos_ref, sin_ref = extra_refs
        o_ref = out_refs[0]
        _, _, place = tile(j)
        width = acc.shape[1]
        assert D % width == 0
        is_rot = (place * tn + ci * width) // D < 2
        c = jnp.where(is_rot, cos_ref[...], 1.0)
        s = jnp.where(is_rot, sin_ref[...], 0.0)
        for t in range(width // LANES):
            xs = acc[:, t * LANES:(t + 1) * LANES]
            o_ref[:, t * LANES:(t + 1) * LANES] = (xs * c + _partner(xs) * s).astype(o_ref.dtype)

    rows = lambda i, j: tile(j)[1] * nrow + i
    tab = pl.BlockSpec((tm, LANES), lambda i, j, k: (rows(i, j), 0))
    out = pl.BlockSpec((tm, tn), lambda i, j, k: (rows(i, j), tile(j)[2]))
    ntiles = 3 if local else 9
    return _matmul(name, x3, w, "nn", tm, tn, D, [(_sds((S3, 3 * D), MXU_DTYPE), out, True)], epilogue,
                   extras=[(cos3, tab), (sin3, tab)], a_map=lambda i, j, k: (rows(i, j), k),
                   b_map=lambda i, j, k: (k, j if local else tile(j)[0]), mnk=(nrow * tm, ntiles * tn, D),
                   dep=dep if local else prev, alias_dep=not local, split=("cols", 3))[0]


def _head_sel(d_model):
    h = jnp.arange(LANES)[:, None]
    l = jnp.arange(d_model)[None, :]
    return (l // HEAD_DIM == h).astype(BF16)


def _class_edges(b, nblk):
    g = b // nblk
    per_class = jnp.where(g == 0, nblk // DILATIONS[0], jnp.where(g == 1, nblk // DILATIONS[1], nblk // DILATIONS[2]))
    pos = (b % nblk) % per_class
    return pos != 0, pos != per_class - 1


def _two_heads(t, top):
    zero = jnp.zeros_like(t)
    return jnp.concatenate([jnp.where(top, t, zero), jnp.where(top, zero, t)], axis=0)


def _band_mask(has_prev):
    B = ATTN_BLK
    row = lax.broadcasted_iota(jnp.int32, (2 * B, 2 * B), 0) % B
    col = lax.broadcasted_iota(jnp.int32, (2 * B, 2 * B), 1)
    in_prev = jnp.logical_and(jnp.logical_and(col < B, col >= row), has_prev)
    in_own = jnp.logical_and(col >= B, col - B <= row)
    return jnp.logical_or(in_prev, in_own)


def _attn_fwd(P3, D):
    S3 = P3.shape[0]
    B = ATTN_BLK
    nblk = S3 // 3 // B
    npairs = D // LANES
    scale = HEAD_DIM ** -0.5

    def body(q_ref, kc_ref, vc_ref, kp_ref, vp_ref, o_ref, lse_ref):
        has_prev, _ = _class_edges(pl.program_id(0), nblk)
        ok = _band_mask(has_prev)
        lane = lax.broadcasted_iota(jnp.int32, (B, LANES), 1)
        top = lane < HEAD_DIM
        lse_acc = jnp.zeros((B, LANES), F32)
        for j in range(npairs):
            sl = slice(j * LANES, (j + 1) * LANES)
            Q = _two_heads(q_ref[:, sl] * scale, top)
            K2 = jnp.concatenate([kp_ref[:, sl], kc_ref[:, sl]], axis=0)
            V2 = jnp.concatenate([vp_ref[:, sl], vc_ref[:, sl]], axis=0)
            s = jnp.where(ok, _nt(Q, K2), NEG)
            m = jnp.max(s, axis=1, keepdims=True)
            p = jnp.exp(s - m)
            l = jnp.sum(p, axis=1, keepdims=True)
            o = _nn((p * (1.0 / l)).astype(MXU_DTYPE), V2)
            o_ref[:, sl] = jnp.where(top, o[:B], o[B:])
            lse = m + jnp.log(l)
            lse_acc = jnp.where(lane == 2 * j, lse[:B], jnp.where(lane == 2 * j + 1, lse[B:], lse_acc))
        lse_ref[...] = lse_acc

    blk = lambda part, prev: pl.BlockSpec(
        (B, D), (lambda b: (jnp.maximum(b - 1, 0), part)) if prev else (lambda b: (b, part)))
    return pl.pallas_call(
        body,
        name="attn_fwd",
        grid=(3 * nblk,),
        in_specs=[blk(0, False), blk(1, False), blk(2, False), blk(1, True), blk(2, True)],
        out_specs=[pl.BlockSpec((B, D), lambda b: (b, 0)), pl.BlockSpec((B, LANES), lambda b: (b, 0))],
        out_shape=[_sds((S3, D), F32), _sds((S3, LANES), F32)],
        compiler_params=_cparams(("parallel",)),
    )(P3, P3, P3, P3, P3)


def _attn_mix(o3, lse3, sel):
    S3, D = o3.shape
    S = S3 // 3

    def body(o3_ref, lse_ref, sel_ref, o_ref, L_ref, w_ref):
        @pl.when(pl.program_id(0) == 0)
        def _():
            w_ref[0] = lse_ref[0:S, :]
            for g, d, r, n in _class_slabs(S):
                w_ref[g, pl.ds(r, n, stride=d), :] = lse_ref[g * S + r * n:g * S + (r + 1) * n, :]
            a, b, c = w_ref[0], w_ref[1], w_ref[2]
            m = jnp.maximum(jnp.maximum(a, b), c)
            L = m + jnp.log(jnp.exp(a - m) + jnp.exp(b - m) + jnp.exp(c - m))
            L_ref[...] = L
            w_ref[0] = jnp.exp(a - L)
            w_ref[1] = jnp.exp(b - L)
            w_ref[2] = jnp.exp(c - L)

        s = sel_ref[...]
        o_ref[...] = _exact_nn(w_ref[0], s) * o3_ref[0:S, :]
        for g, d, r, n in _class_slabs(S):
            rows = pl.ds(r, n, stride=d)
            o_ref[rows, :] += _exact_nn(w_ref[g, rows, :], s) * o3_ref[g * S + r * n:g * S + (r + 1) * n, :]

    return pl.pallas_call(
        body,
        name="attn_mix",
        grid=(D // LANES,),
        in_specs=[pl.BlockSpec((S3, LANES), lambda j: (0, j)), pl.BlockSpec((S3, LANES), lambda j: (0, 0)),
                  pl.BlockSpec((LANES, LANES), lambda j: (0, j))],
        out_specs=[pl.BlockSpec((S, LANES), lambda j: (0, j)), pl.BlockSpec((S, LANES), lambda j: (0, 0))],
        out_shape=[_sds((S, D), F32), _sds((S, LANES), F32)],
        scratch_shapes=[pltpu.VMEM((3, S, LANES), F32)],
        compiler_params=_cparams(("arbitrary",)),
    )(o3, lse3, sel)


def _attn_bwd(P3, do3, L3, delta3, cos3, sin3, D, dep):
    S3 = P3.shape[0]
    B = ATTN_BLK
    nblk = S3 // 3 // B
    npairs = D // LANES
    scale = HEAD_DIM ** -0.5

    def body(c_ref, kp_ref, vp_ref, qn_ref, doc_ref, don_ref, Lc_ref, Ln_ref, dc_ref, dn_ref, cos_ref, sin_ref, dep_ref, out_ref):
        has_prev, has_next = _class_edges(pl.program_id(0), nblk)
        ok = _band_mask(has_prev)
        row = lax.broadcasted_iota(jnp.int32, (2 * B, B), 0) % B
        col = lax.broadcasted_iota(jnp.int32, (2 * B, B), 1)
        ok_n = jnp.logical_and(col >= row, has_next)
        lane = lax.broadcasted_iota(jnp.int32, (B, LANES), 1)
        top = lane < HEAD_DIM
        cos_t = cos_ref[...]
        sin_inv = -sin_ref[...]
        Lc_all, Ln_all, dc_all, dn_all = Lc_ref[...], Ln_ref[...], dc_ref[...], dn_ref[...]
        pair_col = lambda t, j: jnp.concatenate([t[:, 2 * j:2 * j + 1], t[:, 2 * j + 1:2 * j + 2]], axis=0)
        for j in range(npairs):
            sl = lambda part: slice(part * D + j * LANES, part * D + (j + 1) * LANES)
            pj = slice(j * LANES, (j + 1) * LANES)
            kc2, vc2 = c_ref[:, sl(1)], c_ref[:, sl(2)]
            K2 = jnp.concatenate([kp_ref[:, pj], kc2], axis=0)
            V2 = jnp.concatenate([vp_ref[:, pj], vc2], axis=0)
            Qc = _two_heads(c_ref[:, sl(0)] * scale, top)
            Qn = _two_heads(qn_ref[:, pj] * scale, top)
            DOc = _two_heads(doc_ref[:, pj].astype(MXU_DTYPE), top)
            DOn = _two_heads(don_ref[:, pj].astype(MXU_DTYPE), top)
            P_c = jnp.where(ok, jnp.exp(_nt(Qc, K2) - pair_col(Lc_all, j)), 0.0)
            dS_c = P_c * (_nt(DOc, V2) - pair_col(dc_all, j))
            P_n = jnp.where(ok_n, jnp.exp(_nt(Qn, kc2) - pair_col(Ln_all, j)), 0.0)
            dS_n = P_n * (_nt(DOn, vc2) - pair_col(dn_all, j))
            dq = _nn(dS_c.astype(MXU_DTYPE), K2)
            dq2 = jnp.where(top, dq[:B], dq[B:]) * scale
            Qk = jnp.concatenate([Qc, Qn], axis=0)
            DOk = jnp.concatenate([DOc, DOn], axis=0)
            dk2 = _tn(jnp.concatenate([dS_c[:, B:], dS_n], axis=0).astype(MXU_DTYPE), Qk)
            dv2 = _tn(jnp.concatenate([P_c[:, B:], P_n], axis=0).astype(MXU_DTYPE), DOk)
            out_ref[:, sl(0)] = (dq2 * cos_t + _partner(dq2) * sin_inv).astype(out_ref.dtype)
            out_ref[:, sl(1)] = (dk2 * cos_t + _partner(dk2) * sin_inv).astype(out_ref.dtype)
            out_ref[:, sl(2)] = dv2.astype(out_ref.dtype)

    cur = lambda b: b
    prv = lambda b: jnp.maximum(b - 1, 0)
    nxt = lambda b: jnp.minimum(b + 1, 3 * nblk - 1)
    spec = lambda w, f, part=0: pl.BlockSpec((B, w), lambda b: (f(b), part))
    return pl.pallas_call(
        body,
        name="attn_bwd",
        grid=(3 * nblk,),
        in_specs=[spec(3 * D, cur), spec(D, prv, 1), spec(D, prv, 2), spec(D, nxt, 0), spec(D, cur), spec(D, nxt),
                  spec(LANES, cur), spec(LANES, nxt), spec(LANES, cur), spec(LANES, nxt), spec(LANES, cur), spec(LANES, cur),
                  pl.BlockSpec(memory_space=pl.ANY)],
        out_specs=spec(3 * D, cur),
        out_shape=_sds((S3, 3 * D), MXU_DTYPE),
        compiler_params=_cparams(("parallel",)),
    )(P3, P3, P3, P3, do3, do3, L3, L3, delta3, delta3, cos3, sin3, dep)


def _input_grad(du, dx3):
    S, D = du.shape

    def body(du_ref, dx_ref, o_ref):
        o_ref[...] = ALPHA * du_ref[...] + dx_ref[0:S, :]
        for g, d, r, n in _class_slabs(S):
            o_ref[pl.ds(r, n, stride=d), :] += dx_ref[g * S + r * n:g * S + (r + 1) * n, :]

    return pl.pallas_call(
        body,
        name="input_grad",
        grid=(D // LANES,),
        in_specs=[pl.BlockSpec((S, LANES), lambda j: (0, j)), pl.BlockSpec((3 * S, LANES), lambda j: (0, j))],
        out_specs=pl.BlockSpec((S, LANES), lambda j: (0, j)),
        out_shape=_sds((S, D), F32),
        compiler_params=_cparams(("parallel",)),
    )(du, dx3)


def _chunk_causal(tb):
    r = lax.broadcasted_iota(jnp.int32, (tb, tb), 0)
    c = lax.broadcasted_iota(jnp.int32, (tb, tb), 1)
    return jnp.logical_and((r // HGRN_CHUNK) == (c // HGRN_CHUNK), r >= c)


def _chunk_sums(a, lower):
    C = HGRN_CHUNK
    r = lax.broadcasted_iota(jnp.int32, (C, C), 0)
    c = lax.broadcasted_iota(jnp.int32, (C, C), 1)
    tri = ((r >= c) if lower else (r <= c)).astype(BF16)
    parts = _split3(a)
    out = []
    for ci in range(a.shape[0] // C):
        rows = slice(ci * C, (ci + 1) * C)
        out.append(_nn(tri, parts[0][rows]) + _nn(tri, parts[1][rows]) + _nn(tri, parts[2][rows]))
    return jnp.concatenate(out, axis=0)


def _chunk_last(b):
    C = HGRN_CHUNK
    return jnp.concatenate([jnp.broadcast_to(b[(ci + 1) * C - 1:(ci + 1) * C, :], (C, b.shape[1]))
                            for ci in range(b.shape[0] // C)], axis=0)


def _lower_bound(lb_ref):
    l0, l1 = lb_ref[0:1, :], lb_ref[1:2, :]
    m = jnp.maximum(l0, l1)
    e0, e1 = jnp.exp(l0 - m), jnp.exp(l1 - m)
    return e1 / (e0 + e1)


def _hgrn_gates(q_raw, z, lb):
    sg = 1.0 / (1.0 + jnp.exp(-z))
    sn = 1.0 / (1.0 + jnp.exp(z))
    f = lb + (1.0 - lb) * sg
    key = (1.0 - lb) * sn
    sq = 1.0 / (1.0 + jnp.exp(-q_raw))
    return sg, sn, f, key, sq


def _hgrn_fwd(P1, lb_logits, norm_g, tb):
    S = P1.shape[0]
    D = P1.shape[1] // 3
    K = HGRN_DK
    H = D // K
    HP = H
    C = HGRN_CHUNK
    cpb = tb // C
    nt = S // tb

    def body(q_ref, f_ref, i_ref, lb_ref, g_ref, o_ref, n_ref, st_ref, state):
        t = pl.program_id(1)

        @pl.when(t == 0)
        def _():
            state[...] = jnp.zeros_like(state)

        lb_all = _lower_bound(lb_ref)
        low = _chunk_causal(tb)
        for hh in range(HP):
            lanes = slice(hh * K, (hh + 1) * K)
            q_raw, z, v = q_ref[:, lanes], f_ref[:, lanes], i_ref[:, lanes]
            sg, sn, f, key, sq = _hgrn_gates(q_raw, z, lb_all[:, lanes])
            b = _chunk_sums(jnp.log(f), lower=True)
            qd = (q_raw * sq * jnp.exp(b)).astype(MXU_DTYPE)
            kd = (key * jnp.exp(-b)).astype(MXU_DTYPE)
            kb = (key * jnp.exp(_chunk_last(b) - b)).astype(MXU_DTYPE)
            vm = v.astype(MXU_DTYPE)
            a = jnp.where(low, _nt(qd, kd), 0.0).astype(MXU_DTYPE)
            o_intra = _nn(a, vm)
            st = state[hh]
            outs = []
            for ci in range(cpb):
                rows = slice(ci * C, (ci + 1) * C)
                st_ref[hh, ci] = st
                outs.append(o_intra[rows] + _nt(qd[rows], st.astype(MXU_DTYPE)))
                st = st * jnp.exp(b[(ci + 1) * C - 1:(ci + 1) * C, :]) + _tn(vm[rows], kb[rows])
            state[hh] = st
            o = jnp.concatenate(outs, axis=0)
            o_ref[:, lanes] = o
            rs = lax.rsqrt(jnp.mean(o * o, axis=1, keepdims=True) + RMS_EPS)
            n_ref[:, lanes] = o * rs * g_ref[:, lanes]

    tok = lambda part: pl.BlockSpec((tb, HP * K), lambda h, t: (t, part * (H // HP) + h))
    vec = lambda rows: pl.BlockSpec((rows, HP * K), lambda h, t: (0, h))
    return pl.pallas_call(
        body,
        name="hgrn_fwd",
        grid=(H // HP, nt),
        in_specs=[tok(0), tok(1), tok(2), vec(2), vec(1)],
        out_specs=[tok(0), tok(0), pl.BlockSpec((HP, cpb, K, K), lambda h, t: (h, t, 0, 0))],
        out_shape=[_sds((S, D), F32), _sds((S, D), F32), _sds((H, S // C, K, K), F32)],
        scratch_shapes=[pltpu.VMEM((HP, K, K), F32)],
        compiler_params=_cparams(("parallel", "arbitrary")),
    )(P1, P1, P1, lb_logits, norm_g)


def _hgrn_bwd(P1, o_pre, states, dn, lb_logits, norm_g, tb):
    S = P1.shape[0]
    D = P1.shape[1] // 3
    K = HGRN_DK
    H = D // K
    HP = H
    C = HGRN_CHUNK
    cpb = tb // C
    nt = S // tb

    def body(q_ref, f_ref, i_ref, o_ref, st_ref, dn_ref, lb_ref, g_ref, d_ref, dg_ref, dlb_ref, dstate):
        t = pl.program_id(1)

        @pl.when(t == 0)
        def _():
            dstate[...] = jnp.zeros_like(dstate)
            dg_ref[...] = jnp.zeros_like(dg_ref)
            dlb_ref[...] = jnp.zeros_like(dlb_ref)

        lb_all = _lower_bound(lb_ref)
        low = _chunk_causal(tb)
        for hh in range(HP):
            lanes = slice(hh * K, (hh + 1) * K)
            lb = lb_all[:, lanes]
            gn = g_ref[:, lanes]
            q_raw, z, v = q_ref[:, lanes], f_ref[:, lanes], i_ref[:, lanes]
            sg, sn, f, key, sq = _hgrn_gates(q_raw, z, lb)
            b = _chunk_sums(jnp.log(f), lower=True)
            e_pos, e_neg, e_rel = jnp.exp(b), jnp.exp(-b), jnp.exp(_chunk_last(b) - b)
            qd_f, kd_f, kb_f = q_raw * sq * e_pos, key * e_neg, key * e_rel
            qd, kd, kb = qd_f.astype(MXU_DTYPE), kd_f.astype(MXU_DTYPE), kb_f.astype(MXU_DTYPE)
            vm = v.astype(MXU_DTYPE)
            a = jnp.where(low, _nt(qd, kd), 0.0).astype(MXU_DTYPE)
            o = o_ref[:, lanes]
            dnn = dn_ref[:, lanes]
            rs = lax.rsqrt(jnp.mean(o * o, axis=1, keepdims=True) + RMS_EPS)
            dg_ref[:, lanes] += jnp.sum(dnn * o * rs, axis=0, keepdims=True)
            tg = dnn * gn
            dom = (rs * tg - o * (rs * rs * rs) * jnp.mean(tg * o, axis=1, keepdims=True)).astype(MXU_DTYPE)
            da = jnp.where(low, _nt(dom, vm), 0.0).astype(MXU_DTYPE)
            dv = _tn(a, dom)
            dqd = _nn(da, kd)
            dkd = _tn(da, qd)
            dst = dstate[hh]
            dv_s, dqd_s, dkb_s, dbl_s = [None] * cpb, [None] * cpb, [None] * cpb, [None] * cpb
            for ci in reversed(range(cpb)):
                rows = slice(ci * C, (ci + 1) * C)
                st = st_ref[hh, ci]
                dstm = dst.astype(MXU_DTYPE)
                dec = jnp.exp(b[(ci + 1) * C - 1:(ci + 1) * C, :])
                dv_s[ci] = _nt(kb[rows], dstm)
                dkb_s[ci] = _nn(vm[rows], dstm)
                dqd_s[ci] = _nn(dom[rows], st.astype(MXU_DTYPE))
                db_last = jnp.sum(dkb_s[ci] * kb_f[rows], axis=0, keepdims=True) + jnp.sum(dst * st, axis=0, keepdims=True) * dec
                dbl_s[ci] = jnp.broadcast_to(db_last, (C, K))
                dst = dst * dec + _tn(dom[rows], qd[rows])
            dstate[hh] = dst
            dv = dv + jnp.concatenate(dv_s, axis=0)
            dqd = dqd + jnp.concatenate(dqd_s, axis=0)
            dkb = jnp.concatenate(dkb_s, axis=0)
            dkey = dkd * e_neg + dkb * e_rel
            db = dqd * qd_f - dkd * kd_f - dkb * kb_f
            dlogf = _chunk_sums(db, lower=False) + jnp.concatenate(dbl_s, axis=0)
            gz = (1.0 - lb) * sg * sn
            col = lambda part: slice(part * D + hh * K, part * D + (hh + 1) * K)
            d_ref[:, col(0)] = (dqd * e_pos * (sq + q_raw * sq * (1.0 - sq))).astype(d_ref.dtype)
            d_ref[:, col(1)] = (dlogf * gz / f - dkey * gz).astype(d_ref.dtype)
            d_ref[:, col(2)] = dv.astype(d_ref.dtype)
            dlb_ref[:, lanes] += jnp.sum(dlogf * sn / f - dkey * sn, axis=0, keepdims=True)

    rev = lambda t: nt - 1 - t
    tok = lambda part: pl.BlockSpec((tb, HP * K), lambda h, t: (rev(t), part * (H // HP) + h))
    vec = lambda rows: pl.BlockSpec((rows, HP * K), lambda h, t: (0, h))
    outs = pl.pallas_call(
        body,
        name="hgrn_bwd",
        grid=(H // HP, nt),
        in_specs=[tok(0), tok(1), tok(2), tok(0),
                  pl.BlockSpec((HP, cpb, K, K), lambda h, t: (h, rev(t), 0, 0)),
                  tok(0), vec(2), vec(1)],
        out_specs=[pl.BlockSpec((tb, 3 * D), lambda h, t: (rev(t), 0)), vec(1), vec(1)],
        out_shape=[_sds((S, 3 * D), MXU_DTYPE)] + [_sds((1, D), F32)] * 2,
        scratch_shapes=[pltpu.VMEM((HP, K, K), F32)],
        compiler_params=_cparams(("parallel", "arbitrary")),
    )(P1, P1, P1, o_pre, states, dn, lb_logits, norm_g)
    return outs


def _lb_logits_grad(dlb, lb_logits):
    def body(d_ref, l_ref, o_ref):
        s1 = _lower_bound(l_ref)
        d = d_ref[...]
        o_ref[0:1, :] = -(1.0 - s1) * s1 * d
        o_ref[1:2, :] = s1 * (1.0 - s1) * d

    return pl.pallas_call(body, name="lb_logits_grad", out_shape=_sds(lb_logits.shape, F32))(dlb, lb_logits)


def _ln_epilogue(acc, extra_refs, out_refs, j, ci):
    res_ref, g_ref, b_ref = extra_refs
    x_ref, xm_ref, xhat_ref, rstd_ref = out_refs
    u = ALPHA * res_ref[...] + acc
    mu = jnp.mean(u, axis=1, keepdims=True)
    cen = u - mu
    rstd = lax.rsqrt(jnp.mean(cen * cen, axis=1, keepdims=True) + LN_EPS)
    xhat = cen * rstd
    xhat_ref[...] = xhat
    x = xhat * g_ref[...] + b_ref[...]
    x_ref[...] = x
    xm_ref[...] = x.astype(xm_ref.dtype)
    rstd_ref[...] = rstd


def _mm_res_ln(name, a, w_full, res, g, b, tm, tk):
    S, D = res.shape
    row = pl.BlockSpec((tm, D), lambda i, j, k: (i, 0))
    vec = pl.BlockSpec((1, D), lambda i, j, k: (0, 0))
    outs = [(_sds((S, D), F32), row, True), (_sds((S, D), MXU_DTYPE), row, True), (_sds((S, D), F32), row, True),
            (_sds((S, 1), F32), pl.BlockSpec((tm, 1), lambda i, j, k: (i, 0)), True)]
    return _matmul(name, a, w_full, "nn", tm, D, tk, outs, _ln_epilogue, extras=[(res, row, True), (g, vec), (b, vec)],
                   split=("rows", 2) if tk == a.shape[1] else None)


def _ln_bwd_rows(dy, xh, rstd, g, first, du_ref, dum_ref, dg_ref, db_ref):
    if first is not None:
        @pl.when(first)
        def _():
            dg_ref[...] = jnp.zeros_like(dg_ref)
            db_ref[...] = jnp.zeros_like(db_ref)

    dg_ref[...] += jnp.sum(dy * xh, axis=0, keepdims=True)
    db_ref[...] += jnp.sum(dy, axis=0, keepdims=True)
    dxh = dy * g
    m1 = jnp.mean(dxh, axis=1, keepdims=True)
    m2 = jnp.mean(dxh * xh, axis=1, keepdims=True)
    du = rstd * (dxh - m1 - xh * m2)
    du_ref[...] = du
    dum_ref[...] = du.astype(dum_ref.dtype)


def _loss_ln_bwd(y, target, xhat, rstd, g, tm):
    S, D = y.shape

    def body(y_ref, t_ref, xh_ref, r_ref, g_ref, sq_ref, du_ref, dum_ref, dg_ref, db_ref):
        first = pl.program_id(0) == 0

        @pl.when(first)
        def _():
            sq_ref[...] = jnp.zeros_like(sq_ref)

        e = y_ref[...] - t_ref[...]
        sq_ref[...] += jnp.sum(e * e, axis=0, keepdims=True)
        _ln_bwd_rows(e / D, xh_ref[...], r_ref[...], g_ref[...], first, du_ref, dum_ref, dg_ref, db_ref)

    row = pl.BlockSpec((tm, D), lambda i: (i, 0))
    vec = pl.BlockSpec((1, D), lambda i: (0, 0))
    return pl.pallas_call(
        body,
        name="loss_ln_bwd",
        grid=(S // tm,),
        in_specs=[row, row, row, pl.BlockSpec((tm, 1), lambda i: (i, 0)), vec],
        out_specs=[vec, row, row, vec, vec],
        out_shape=[_sds((1, D), F32), _sds((S, D), F32), _sds((S, D), MXU_DTYPE), _sds((1, D), F32), _sds((1, D), F32)],
        compiler_params=_cparams(("arbitrary",)),
    )(y, target, xhat, rstd, g)


def _mlp_up(name, x, w_up, tm, tn, tk):
    S = x.shape[0]
    F = w_up.shape[1]

    def epilogue(acc, extra_refs, out_refs, j, ci):
        r = jnp.maximum(acc, 0.0)
        out_refs[0][...] = (r * r).astype(out_refs[0].dtype)

    return _matmul(name, x, w_up, "nn", tm, tn, tk, [(_sds((S, F), MXU_DTYPE), _ij_spec(tm, tn), True)], epilogue,
                   split=("cols", 2))[0]


def _mlp_down_bwd(name, dy, w_down, a, tm, tn, tk):
    S, F = a.shape

    def epilogue(acc, extra_refs, out_refs, j, ci):
        out_refs[0][...] = (acc * (2.0 * jnp.sqrt(extra_refs[0][...].astype(F32)))).astype(out_refs[0].dtype)

    return _matmul(name, dy, w_down, "nt", tm, tn, tk, [(_sds((S, F), MXU_DTYPE), _ij_spec(tm, tn), True)], epilogue,
                   extras=[(a, _ij_spec(tm, tn), True)], split=("cols", 2))[0]


def _mm_nt_res_ln_bwd(name, dy, w, du, xhat, rstd, g, tm, tk, dep):
    S, D = du.shape

    def epilogue(acc, extra_refs, out_refs, j, ci):
        du_ref, xh_ref, r_ref, g_ref = extra_refs
        first = (pl.program_id(0) == 0) if ci == 0 else None
        _ln_bwd_rows(ALPHA * du_ref[...] + acc, xh_ref[...], r_ref[...], g_ref[...], first, *out_refs)

    row = pl.BlockSpec((tm, D), lambda i, j, k: (i, 0))
    vec = pl.BlockSpec((1, D), lambda i, j, k: (0, 0))
    return _matmul(name, dy, w, "nt", tm, D, tk,
                   [(_sds((S, D), F32), row, True), (_sds((S, D), MXU_DTYPE), row, True), (_sds((1, D), F32), vec),
                    (_sds((1, D), F32), vec)], epilogue,
                   extras=[(du, row, True), (xhat, row, True), (rstd, pl.BlockSpec((tm, 1), lambda i, j, k: (i, 0)), True), (g, vec)],
                   dep=dep, sem=("arbitrary", "arbitrary", "arbitrary"), split=("rows", 2))


def _attn_out_bwd(du, w_out, o, sel_t, tm, tk):
    S, D = o.shape

    def epilogue(acc, extra_refs, out_refs, j, ci):
        out_refs[0][...] = acc
        out_refs[1][...] = _exact_nn(acc * extra_refs[0][...], extra_refs[1][...])

    row = pl.BlockSpec((tm, D), lambda i, j, k: (i, 0))
    slim = pl.BlockSpec((tm, LANES), lambda i, j, k: (i, 0))
    return _matmul("attn_out_bwd", du, w_out, "nt", tm, D, tk,
                   [(_sds((S, D), F32), row, True), (_sds((S, LANES), F32), slim, True)], epilogue,
                   extras=[(o, row, True), (sel_t, pl.BlockSpec((D, LANES), lambda i, j, k: (0, 0)))], split=("rows", 2))


def _adamw(name, w, gs, m, v):
    shape = w.shape
    cols = shape[-1]
    rows = math.prod(shape[:-1])
    w2, m2, v2 = (t.reshape(rows, cols) for t in (w, m, v))
    gs2 = [g.reshape(-1, cols) for g in gs]
    ng = len(gs2)
    tr = _pick(rows // ng, (256, 128, 64, 32, 16, 8))
    per = rows // ng // tr
    c1 = 1.0 - ADAM_B1 ** ADAM_STEP
    c2 = 1.0 - ADAM_B2 ** ADAM_STEP

    def body(w_ref, m_ref, v_ref, *rest):
        g_refs, (d_ref, nm_ref, nv_ref), g_out = rest[:ng], rest[ng:ng + 3], rest[ng + 3:]
        gg = g_refs[0][...]
        if ng == 2:
            gg = jnp.where(pl.program_id(0) < per, gg, g_refs[1][...])
            g_out[0][...] = gg
        nm = ADAM_B1 * m_ref[...] + (1.0 - ADAM_B1) * gg
        nv = ADAM_B2 * v_ref[...] + (1.0 - ADAM_B2) * (gg * gg)
        nm_ref[...] = nm
        nv_ref[...] = nv
        d_ref[...] = -ADAM_LR * ((nm / c1) / (jnp.sqrt(nv / c2) + ADAM_EPS) + ADAM_WD * w_ref[...])

    blk = pl.BlockSpec((tr, cols), lambda i: (i, 0))
    g_specs = [blk] if ng == 1 else [pl.BlockSpec((tr, cols), lambda i: (jnp.minimum(i, per - 1), 0)),
                                     pl.BlockSpec((tr, cols), lambda i: (jnp.maximum(i - per, 0), 0))]
    nout = 3 if ng == 1 else 4
    outs = pl.pallas_call(
        body,
        name=name,
        grid=(rows // tr,),
        in_specs=[blk] * 3 + g_specs,
        out_specs=[blk] * nout,
        out_shape=[_sds((rows, cols), F32)] * nout,
        compiler_params=_cparams(("parallel",)),
    )(w2, m2, v2, *gs2)
    g_full = outs[3] if ng == 2 else gs2[0]
    return tuple(o.reshape(shape) for o in (outs[0], outs[1], outs[2], g_full))


HBM = pl.BlockSpec(memory_space=pl.ANY)


def _shard_slice(ref, axis, size, index):
    idx = [slice(None)] * len(ref.shape)
    idx[axis] = pl.ds(pl.multiple_of(index * size, 8), size)
    return ref.at[tuple(idx)]


def _share_halves(name, full, tr):
    R, W4 = full.shape
    W, h = W4 // 4, R // 2
    steps = [(k, t) for k in range(3) for t in range(h // tr)]

    def body(f_in, f_ref, buf, lsem, ssem, rsem):
        x, y, c = lax.axis_index("x"), lax.axis_index("y"), lax.axis_index("c")
        sibling = (x, y, 1 - c)
        chips = [(1 - x, y), (x, 1 - y), (1 - x, 1 - y)]

        def tile(k, t):
            px, py = chips[k]
            return f_ref.at[pl.ds(pl.multiple_of(c * h + t * tr, 8), tr), pl.ds(pl.multiple_of((2 * px + py) * W, LANES), W)]

        sends = []
        for s, (k, t) in enumerate(steps):
            slot = s % 2
            if s >= 2:
                sends[s - 2].wait_send()
            lc = pltpu.make_async_copy(tile(k, t), buf.at[slot], lsem.at[slot])
            lc.start()
            lc.wait()
            rc = pltpu.make_async_remote_copy(src_ref=buf.at[slot], dst_ref=tile(k, t), send_sem=ssem.at[slot], recv_sem=rsem,
                                              device_id=sibling, device_id_type=MESH)
            rc.start()
            sends.append(rc)
        for rc in sends[-2:]:
            rc.wait_send()
        whole = f_ref.at[pl.ds(0, h), pl.ds(0, 3 * W)]
        pltpu.make_async_remote_copy(src_ref=whole, dst_ref=whole, send_sem=ssem.at[0], recv_sem=rsem,
                                     device_id=sibling, device_id_type=MESH).wait_recv()

    return pl.pallas_call(
        body,
        name=name,
        in_specs=[HBM],
        out_specs=HBM,
        out_shape=_sds(full.shape, full.dtype),
        input_output_aliases={0: 0},
        scratch_shapes=[pltpu.VMEM((2, tr, W), full.dtype), pltpu.SemaphoreType.DMA((2,)), pltpu.SemaphoreType.DMA((2,)),
                        pltpu.SemaphoreType.DMA(())],
    )(full)


IN_HBM = pl.BlockSpec(memory_space=pltpu.HBM)
IN_SEM = pl.BlockSpec(memory_space=pltpu.SEMAPHORE)
DATAFLOW = pltpu.SideEffectType.DATAFLOW_SIDE_EFFECTING


def _hbm(t):
    return pltpu.with_memory_space_constraint(t, pltpu.HBM)


def _token_spec():
    return pl.BlockSpec(memory_space=pltpu.VMEM)


def _gather_copies(s_refs, f_refs, axes, halves, send, recv, loc, arrival):
    x, y, c = lax.axis_index("x"), lax.axis_index("y"), lax.axis_index("c")
    chips = [(1 - x, y), (x, 1 - y), (1 - x, 1 - y)]
    local, remote = [], []
    for a in range(len(s_refs)):
        size = s_refs[a].shape[axes[a]]
        local.append(pltpu.make_async_copy(s_refs[a], _shard_slice(f_refs[a], axes[a], size, 2 * x + y), loc.at[a]))
        for k, (px, py) in enumerate(chips):
            block = (2 * px + py) if arrival else (2 * x + y)
            src, dst = s_refs[a], _shard_slice(f_refs[a], axes[a], size, block)
            if halves:
                assert axes[a] == 1 and len(s_refs[a].shape) == 2
                h = s_refs[a].shape[0] // 2
                rows = pl.ds(pl.multiple_of(c * h, 8), h)
                src = s_refs[a].at[rows, :]
                dst = f_refs[a].at[rows, pl.ds(pl.multiple_of(block * size, LANES), size)]
            remote.append(pltpu.make_async_remote_copy(src_ref=src, dst_ref=dst, send_sem=send.at[3 * a + k],
                                                       recv_sem=recv.at[3 * a + k], device_id=(px, py, c), device_id_type=MESH))
    return local, remote


def _gather_start(name, shards, axes, after, halves=False):
    n = len(shards)
    fulls = []
    for s, ax in zip(shards, axes):
        fs = list(s.shape)
        fs[ax] *= 4
        fulls.append(lax.empty(tuple(fs), s.dtype))

    def body(*refs):
        s_refs, f_refs = refs[:n], refs[n:2 * n]
        send, recv, loc, token = refs[2 * n + 1], refs[2 * n + 2], refs[2 * n + 3], refs[-1]
        local, remote = _gather_copies(s_refs, f_refs, axes, halves, send, recv, loc, arrival=False)
        for cp in remote + local:
            cp.start()
        token[...] = jnp.zeros_like(token)

    outs = pl.pallas_call(
        body,
        name=name,
        out_shape=(pltpu.SemaphoreType.DMA((3 * n,)), pltpu.SemaphoreType.DMA((3 * n,)), pltpu.SemaphoreType.DMA((n,)),
                   *[pltpu.HBM(t.shape, t.dtype) for t in shards + fulls], _sds((8, LANES), F32)),
        in_specs=[IN_HBM] * (2 * n) + [HBM],
        out_specs=(IN_SEM, IN_SEM, IN_SEM, *[IN_HBM] * (2 * n), _token_spec()),
        input_output_aliases={i: 3 + i for i in range(2 * n)},
        compiler_params=pltpu.CompilerParams(has_side_effects=DATAFLOW),
    )(*[_hbm(t) for t in shards + fulls], after)
    return (outs[0], outs[1], outs[2], list(outs[3:3 + n]), list(outs[3 + n:3 + 2 * n]), axes, halves), outs[-1]


def _gather_wait(name, state, *after):
    send, recv, loc, s_thru, f_thru, axes, halves = state
    n = len(s_thru)

    def body(*refs):
        s_refs, f_refs = refs[:n], refs[n:2 * n]
        local, remote = _gather_copies(s_refs, f_refs, axes, halves, refs[2 * n], refs[2 * n + 1], refs[2 * n + 2], arrival=True)
        for cp in local:
            cp.wait()
        for cp in remote:
            cp.wait_send()
            cp.wait_recv()

    outs = pl.pallas_call(
        body,
        name=name,
        out_shape=tuple(pltpu.HBM(t.shape, t.dtype) for t in s_thru + f_thru),
        in_specs=[IN_HBM] * (2 * n) + [IN_SEM, IN_SEM, IN_SEM] + [HBM] * len(after),
        out_specs=tuple([IN_HBM] * (2 * n)),
        input_output_aliases={i: i for i in range(2 * n)},
        compiler_params=pltpu.CompilerParams(has_side_effects=DATAFLOW),
    )(*s_thru, *f_thru, send, recv, loc, *after)
    return list(outs[n:2 * n])


FLIPS = [(fx, fy, fc) for fx in (0, 1) for fy in (0, 1) for fc in (0, 1)][1:]


def _piece_shape(shape, axis):
    ps = list(shape)
    if axis == 0:
        ps[0] //= 8
    else:
        ps[0] //= 2
        ps[axis] //= 4
    return tuple(ps)


def _piece(ref, axis, q, c):
    shape = ref.shape
    idx = [slice(None)] * len(shape)
    if axis == 0:
        h = shape[0] // 8
        idx[0] = pl.ds(pl.multiple_of((2 * q + c) * h, 8), h)
    else:
        h, w = shape[0] // 2, shape[axis] // 4
        idx[0] = pl.ds(c * h, h)
        idx[axis] = pl.ds(pl.multiple_of(q * w, LANES if axis == len(shape) - 1 else 8), w)
    return ref.at[tuple(idx)]


def _own_piece(g, axis):
    ps = _piece_shape(g.shape, axis)
    q, c = 2 * lax.axis_index("x") + lax.axis_index("y"), lax.axis_index("c")
    start = [0] * len(ps)
    if axis == 0:
        start[0] = (2 * q + c) * ps[0]
    else:
        start[0] = c * ps[0]
        start[axis] = q * ps[axis]
    return lax.dynamic_slice(g, start, ps)


def _scatter_copies(g_refs, l_refs, axes, send, recv):
    x, y, c = lax.axis_index("x"), lax.axis_index("y"), lax.axis_index("c")
    out = []
    for a in range(len(g_refs)):
        for k, (fx, fy, fc) in enumerate(FLIPS):
            tx, ty, tc = x ^ fx, y ^ fy, c ^ fc
            out.append(pltpu.make_async_remote_copy(
                src_ref=_piece(g_refs[a], axes[a], 2 * tx + ty, tc), dst_ref=l_refs[a].at[k],
                send_sem=send.at[7 * a + k], recv_sem=recv.at[7 * a + k], device_id=(tx, ty, tc), device_id_type=MESH))
    return out


def _scatter_start(name, grads, axes):
    n = len(grads)
    lands = [lax.empty((7,) + _piece_shape(g.shape, ax), g.dtype) for g, ax in zip(grads, axes)]

    def body(*refs):
        g_refs, l_refs = refs[:n], refs[n:2 * n]
        send, recv, token = refs[2 * n], refs[2 * n + 1], refs[-1]
        for cp in _scatter_copies(g_refs, l_refs, axes, send, recv):
            cp.start()
        token[...] = jnp.zeros_like(token)

    outs = pl.pallas_call(
        body,
        name=name,
        out_shape=(pltpu.SemaphoreType.DMA((7 * n,)), pltpu.SemaphoreType.DMA((7 * n,)),
                   *[pltpu.HBM(t.shape, t.dtype) for t in grads + lands], _sds((8, LANES), F32)),
        in_specs=[IN_HBM] * (2 * n),
        out_specs=(IN_SEM, IN_SEM, *[IN_HBM] * (2 * n), _token_spec()),
        input_output_aliases={i: 2 + i for i in range(2 * n)},
        compiler_params=pltpu.CompilerParams(has_side_effects=DATAFLOW),
    )(*[_hbm(t) for t in grads + lands])
    return (outs[0], outs[1], list(outs[2:2 + n]), list(outs[2 + n:2 + 2 * n]), axes), outs[-1]


def _scatter_wait(name, state, *after):
    send, recv, g_thru, l_thru, axes = state
    n = len(g_thru)

    def body(*refs):
        g_refs, l_refs = refs[:n], refs[n:2 * n]
        for cp in _scatter_copies(g_refs, l_refs, axes, refs[2 * n], refs[2 * n + 1]):
            cp.wait_send()
            cp.wait_recv()

    outs = pl.pallas_call(
        body,
        name=name,
        out_shape=tuple(pltpu.HBM(t.shape, t.dtype) for t in g_thru + l_thru),
        in_specs=[IN_HBM] * (2 * n) + [IN_SEM, IN_SEM] + [HBM] * len(after),
        out_specs=tuple([IN_HBM] * (2 * n)),
        input_output_aliases={i: i for i in range(2 * n)},
        compiler_params=pltpu.CompilerParams(has_side_effects=DATAFLOW),
    )(*g_thru, *l_thru, send, recv, *after)
    return list(outs[:n]), list(outs[n:2 * n])


def _reduce_join(name, landing, own):
    piece = own.shape
    C = piece[-1]
    R = math.prod(piece[:-1])
    l3 = landing.reshape(7, R, C)
    own2 = own.reshape(R, C)
    tr = _pick(R, [t for t in (512, 256, 128, 64, 32, 16, 8) if t * C <= 256 * 1024])
    nsteps = R // tr

    def body(own_ref, l_ref, o_ref, buf, send, loc, recv):
        i = pl.program_id(0)
        x, y, c = lax.axis_index("x"), lax.axis_index("y"), lax.axis_index("c")
        sibling = (x, y, 1 - c)

        def copies(slot, step):
            dst = o_ref.at[pl.ds(pl.multiple_of(c * R + step * tr, 8), tr), :]
            return (pltpu.make_async_copy(buf.at[slot], dst, loc.at[slot]),
                    pltpu.make_async_remote_copy(src_ref=buf.at[slot], dst_ref=dst, send_sem=send.at[slot], recv_sem=recv,
                                                 device_id=sibling, device_id_type=MESH))

        @pl.when(i >= 2)
        def _():
            lc, rc = copies(i % 2, i - 2)
            lc.wait()
            rc.wait_send()

        acc = own_ref[...].astype(F32)
        for s in range(7):
            acc = acc + l_ref[s].astype(F32)
        buf[i % 2] = acc
        lc, rc = copies(i % 2, i)
        lc.start()
        rc.start()

        @pl.when(i == nsteps - 1)
        def _():
            for st in range(max(nsteps - 2, 0), nsteps):
                lc, rc = copies(st % 2, st)
                lc.wait()
                rc.wait_send()
            theirs = o_ref.at[pl.ds(pl.multiple_of((1 - c) * R, 8), R), :]
            pltpu.make_async_remote_copy(src_ref=theirs, dst_ref=theirs, send_sem=send.at[0], recv_sem=recv,
                                         device_id=sibling, device_id_type=MESH).wait_recv()

    return pl.pallas_call(
        body,
        name=name,
        grid=(nsteps,),
        in_specs=[pl.BlockSpec((tr, C), lambda i: (i, 0)), pl.BlockSpec((7, tr, C), lambda i: (0, i, 0))],
        out_specs=HBM,
        out_shape=_sds((2 * R, C), F32),
        scratch_shapes=[pltpu.VMEM((2, tr, C), F32), pltpu.SemaphoreType.DMA((2,)), pltpu.SemaphoreType.DMA((2,)),
                        pltpu.SemaphoreType.DMA(())],
        compiler_params=_cparams(("arbitrary",)),
    )(own2, l3)


def _all_reduce_small(v, dep):
    R, D = v.shape

    def body(v_ref, dep_ref, o_ref, land, send, recv):
        x, y, c = lax.axis_index("x"), lax.axis_index("y"), lax.axis_index("c")
        my_slot = 4 * x + 2 * y + c
        land[my_slot] = v_ref[...]
        for k, (fx, fy, fc) in enumerate(FLIPS):
            tx, ty, tc = x ^ fx, y ^ fy, c ^ fc
            pltpu.make_async_remote_copy(src_ref=v_ref, dst_ref=land.at[my_slot], send_sem=send.at[k], recv_sem=recv.at[k],
                                         device_id=(tx, ty, tc), device_id_type=MESH).start()
        for k, (fx, fy, fc) in enumerate(FLIPS):
            tx, ty, tc = x ^ fx, y ^ fy, c ^ fc
            cp = pltpu.make_async_remote_copy(src_ref=v_ref, dst_ref=land.at[4 * tx + 2 * ty + tc], send_sem=send.at[k],
                                              recv_sem=recv.at[k], device_id=(tx, ty, tc), device_id_type=MESH)
            cp.wait_send()
            cp.wait_recv()
        acc = land[0]
        for s in range(1, 8):
            acc = acc + land[s]
        o_ref[...] = acc

    return pl.pallas_call(
        body,
        name="all_reduce_small",
        in_specs=[pl.BlockSpec(memory_space=pltpu.VMEM), pl.BlockSpec(memory_space=pl.ANY)],
        out_specs=pl.BlockSpec(memory_space=pltpu.VMEM),
        out_shape=_sds((R, D), F32),
        scratch_shapes=[pltpu.VMEM((8, R, D), F32), pltpu.SemaphoreType.DMA((7,)), pltpu.SemaphoreType.DMA((7,))],
    )(v, dep)


def kernel(x, attn_w_in, attn_w_out, hgrn_w_in, hgrn_w_out, hgrn_norm_g, lb_logits, ln_mix_g, ln_mix_b, ln_ffn_g, ln_ffn_b, ffn_w_up, ffn_w_down, loss_target, m_attn_w_in, m_attn_w_out, m_hgrn_w_in, m_hgrn_w_out, m_hgrn_norm_g, m_lb_logits, m_ln_mix_g, m_ln_mix_b, m_ln_ffn_g, m_ln_ffn_b, m_ffn_w_up, m_ffn_w_down, v_attn_w_in, v_attn_w_out, v_hgrn_w_in, v_hgrn_w_out, v_hgrn_norm_g, v_lb_logits, v_ln_mix_g, v_ln_mix_b, v_ln_ffn_g, v_ln_ffn_b, v_ffn_w_up, v_ffn_w_down):
    xs = x[0]
    tgt = loss_target[0]
    S, D = xs.shape
    F = ffn_w_up.shape[2] * 4
    T1 = _pick(S, (1024, 512, 256))
    T2 = _pick(S, (2048, 1024, 512))
    TH = _pick(S, (512, 256))
    TB = _pick(S, (128,))
    TF = _pick(F, (1024, 512))
    TG = _pick(3 * D, (1536, 1024, 768))
    TW = _pick(F, (2048, 1024))

    cast = lambda w: w.astype(MXU_DTYPE)
    st_a, tok = _gather_start("gather_a", [cast(attn_w_in[0])], [1], jnp.zeros((8, LANES), F32), halves=True)
    tok, (xs_late, w_aout, w_fup, w_fdown, w_hin, w_hout) = lax.optimization_barrier(
        (tok, (xs, attn_w_out, ffn_w_up, ffn_w_down, hgrn_w_in, hgrn_w_out)))
    st_b, tok = _gather_start("gather_b", [cast(w_aout[0]), cast(w_fup[0]), cast(w_fdown[0])], [0, 1, 0], tok)
    st_c, tok = _gather_start("gather_c", [cast(w_hin[0]), cast(w_hout[0]), hgrn_norm_g, cast(w_fup[1]), cast(w_fdown[1])],
                              [1, 0, 1, 1, 0], tok)

    cos3, sin3 = _rope_tables(S)
    sel = _head_sel(D)
    sel_t = sel.T

    xc3 = _stack_classes("x_classes", xs_late, MXU_DTYPE)
    P3 = _attn_proj("attn_proj_own", xc3, st_a[3][0], cos3, sin3, T2, None, tok)
    (wa_in,) = _gather_wait("gather_a_wait", st_a, P3)
    wa_in = _share_halves("share_a", wa_in, _pick(D // 2, (256, 128)))
    P3 = _attn_proj("attn_proj", xc3, wa_in, cos3, sin3, T2, P3)
    o3, lse3 = _attn_fwd(P3, D)
    o_att, L_att = _attn_mix(o3, lse3, sel)
    wa_out, w_up0, w_down0 = _gather_wait("gather_b_wait", st_b, L_att)
    x1, xm1, xh1, r1 = _mm_res_ln("attn_out_ln", o_att, wa_out, xs, ln_mix_g[0:1], ln_mix_b[0:1], TH, D)
    a0 = _mlp_up("mlp0_up", xm1, w_up0, T2, TF, D)
    x2, xm2, xh2, r2 = _mm_res_ln("mlp0_down_ln", a0, w_down0, x1, ln_ffn_g[0:1], ln_ffn_b[0:1], TH, F)

    wh_in, wh_out, norm_g, w_up1, w_down1 = _gather_wait("gather_c_wait", st_c, r2)
    P1 = _plain_mm("hgrn_proj", xm2, wh_in, "nn", F32, T1, _pick(3 * D, (1024, 768, 512)), D)
    o_h, n_h, states = _hgrn_fwd(P1, lb_logits, norm_g, TB)
    x3, xm3, xh3, r3 = _mm_res_ln("hgrn_out_ln", n_h, wh_out, x2, ln_mix_g[1:2], ln_mix_b[1:2], TH, D)
    a1 = _mlp_up("mlp1_up", xm3, w_up1, T2, TF, D)
    x4, _, xh4, r4 = _mm_res_ln("mlp1_down_ln", a1, w_down1, x3, ln_ffn_g[1:2], ln_ffn_b[1:2], TH, F)

    wgrad = lambda name, a, dy, tm, tn: _plain_mm(name, a, dy, "tn", MXU_DTYPE, tm, tn, T1)
    sq, du4, dum4, dg_ffn1, db_ffn1 = _loss_ln_bwd(x4, tgt, xh4, r4, ln_ffn_g[1:2], TH)
    dh1 = _mlp_down_bwd("mlp1_down_bwd", dum4, w_down1, a1, T2, TF, D)
    g_down1 = wgrad("g_down1", a1, dum4, TW, D)
    g_up1 = wgrad("g_up1", xm3, dh1, D, TW)
    sc_1, tok = _scatter_start("scatter_1", [g_down1, g_up1], [0, 1])
    du3, dum3, dg_mix1, db_mix1 = _mm_nt_res_ln_bwd("mlp1_up_bwd", dh1, w_up1, du4, xh3, r3, ln_mix_g[1:2], TH, F, tok)
    dn = _plain_mm("hgrn_out_bwd", dum3, wh_out, "nt", F32, T1, D, D)
    g_hout = wgrad("g_hgrn_out", n_h, dum3, D, D)
    dP1, dg_norm, dlb = _hgrn_bwd(P1, o_h, states, dn, lb_logits, norm_g, TB)
    g_hin = wgrad("g_hgrn_in", xm2, dP1, D, TG)
    d_lb_logits = _lb_logits_grad(dlb, lb_logits)
    sc_2, tok = _scatter_start("scatter_2", [g_hout, g_hin], [0, 1])

    du2, dum2, dg_ffn0, db_ffn0 = _mm_nt_res_ln_bwd("hgrn_in_bwd", dP1, wh_in, du3, xh2, r2, ln_ffn_g[0:1], TH, 3 * D, tok)
    dh0 = _mlp_down_bwd("mlp0_down_bwd", dum2, w_down0, a0, T2, TF, D)
    g_down0 = wgrad("g_down0", a0, dum2, TW, D)
    g_up0 = wgrad("g_up0", xm1, dh0, D, TW)
    sc_3, tok = _scatter_start("scatter_3", [g_down0, g_up0], [0, 1])
    du1, dum1, dg_mix0, db_mix0 = _mm_nt_res_ln_bwd("mlp0_up_bwd", dh0, w_up0, du2, xh1, r1, ln_mix_g[0:1], TH, F, tok)
    do, delta = _attn_out_bwd(dum1, wa_out, o_att, sel_t, TH, D)
    g_aout = wgrad("g_attn_out", o_att, dum1, D, D)
    sc_5, tok = _scatter_start("scatter_5", [g_aout], [0])
    dP3 = _attn_bwd(P3, _stack_classes("do_classes", do, MXU_DTYPE), _stack_classes("lse_classes", L_att, F32),
                    _stack_classes("delta_classes", delta, F32), cos3, sin3, D, tok)
    small = jnp.concatenate([d_lb_logits, dg_mix0, dg_mix1, db_mix0, db_mix1, dg_ffn0, dg_ffn1, db_ffn0, db_ffn1,
                             dg_norm, sq, jnp.zeros((4, D), F32)], axis=0)
    small = _all_reduce_small(small, dP3)
    loss = 0.5 * jnp.sum(small[11]) / D
    grp = lambda j: j // (3 * D // TG)
    g_ain = _matmul("g_attn_in", xc3, dP3, "tn", D, TG, T1, [(_sds((D, 9 * D), MXU_DTYPE), _ij_spec(D, TG))], _store_epilogue,
                    a_map=lambda i, j, k: (k + grp(j) * (S // T1), i),
                    b_map=lambda i, j, k: (k + grp(j) * (S // T1), j % (3 * D // TG)), mnk=(D, 9 * D, S), dep=small)[0]
    sc_4, tok = _scatter_start("scatter_4", [g_ain], [1])
    dxc3 = _matmul("attn_in_bwd", dP3, wa_in, "nt", T1, D, 3 * D, [(_sds((3 * S, D), F32), _ij_spec(T1, D))], _store_epilogue,
                   b_map=lambda i, j, k: (j, k + i // (S // T1)), mnk=(3 * S, D, 3 * D), dep=tok)[0]
    grad_x = _input_grad(du1, dxc3)

    def reduced(name, state, *after):
        gs, lands = _scatter_wait(name + "_wait", state, *after)
        return [_reduce_join(f"{name}_reduce_{i}", l, _own_piece(g, ax)) for i, (l, g, ax) in enumerate(zip(lands, gs, state[4]))]

    r_down1, r_up1 = reduced("scatter_1", sc_1, grad_x)
    r_hout, r_hin = reduced("scatter_2", sc_2, r_up1)
    r_down0, r_up0 = reduced("scatter_3", sc_3, r_hin)
    (r_aout,) = reduced("scatter_5", sc_5, r_up0)

    my_chip = 2 * lax.axis_index("x") + lax.axis_index("y")
    nsh = hgrn_norm_g.shape[1]
    g_norm = lax.dynamic_slice(small[10:11], (0, my_chip * nsh), (1, nsh))

    grads, upd = {}, {}

    def update(nm, w, gs, m, v):
        upd[nm] = _adamw("adamw_" + nm, w, gs, m, v)
        grads[nm] = upd[nm][3]

    update("hgrn_w_in", hgrn_w_in, [r_hin], m_hgrn_w_in, v_hgrn_w_in)
    update("hgrn_w_out", hgrn_w_out, [r_hout], m_hgrn_w_out, v_hgrn_w_out)
    update("ffn_w_up", ffn_w_up, [r_up0, r_up1], m_ffn_w_up, v_ffn_w_up)
    update("ffn_w_down", ffn_w_down, [r_down0, r_down1], m_ffn_w_down, v_ffn_w_down)
    update("attn_w_out", attn_w_out, [r_aout], m_attn_w_out, v_attn_w_out)
    update("hgrn_norm_g", hgrn_norm_g, [g_norm], m_hgrn_norm_g, v_hgrn_norm_g)
    cat = lambda ts: jnp.concatenate(ts, axis=0)
    small_w = cat([lb_logits, ln_mix_g, ln_mix_b, ln_ffn_g, ln_ffn_b])
    small_m = cat([m_lb_logits, m_ln_mix_g, m_ln_mix_b, m_ln_ffn_g, m_ln_ffn_b])
    small_v = cat([v_lb_logits, v_ln_mix_g, v_ln_mix_b, v_ln_ffn_g, v_ln_ffn_b])
    small_upd = _adamw("adamw_small", small_w, [small[0:10]], small_m, small_v)
    for i, nm in enumerate(["lb_logits", "ln_mix_g", "ln_mix_b", "ln_ffn_g", "ln_ffn_b"]):
        grads[nm] = small[2 * i:2 * i + 2]
        upd[nm] = tuple(t[2 * i:2 * i + 2] for t in small_upd)
    done = [upd[k][2] for k in ("hgrn_w_in", "hgrn_w_out", "ffn_w_up", "ffn_w_down", "attn_w_out", "hgrn_norm_g")]
    (r_ain,) = reduced("scatter_4", sc_4, small_upd[2], *done)
    update("attn_w_in", attn_w_in, [r_ain], m_attn_w_in, v_attn_w_in)

    order = ["attn_w_in", "attn_w_out", "hgrn_w_in", "hgrn_w_out", "hgrn_norm_g", "lb_logits", "ln_mix_g", "ln_mix_b",
             "ln_ffn_g", "ln_ffn_b", "ffn_w_up", "ffn_w_down"]
    return (loss, grad_x[None], *[grads[k] for k in order], *[upd[k][0] for k in order],
            *[upd[k][1] for k in order], *[upd[k][2] for k in order])
```

```python
import math

import jax
import jax.numpy as jnp
from jax import lax
from jax.experimental import pallas as pl
from jax.experimental.pallas import tpu as pltpu

F32 = jnp.float32
BF16 = jnp.bfloat16
MXU_DTYPE = BF16

HEAD_DIM = 64
ATTN_BLK = 128
DILATIONS = (1, 4, 16)
ROPE_THETA = 10000.0
HGRN_DK = 128
HGRN_CHUNK = 64
DEPTH = 2
LN_EPS = 1e-5
RMS_EPS = 1e-6
ALPHA = (2 * DEPTH) ** 0.25
ADAM_LR, ADAM_B1, ADAM_B2, ADAM_EPS, ADAM_WD, ADAM_STEP = 0.001, 0.9, 0.999, 1e-08, 0.01, 10

LANES = 128
VMEM_LIMIT = 56 * 1024 * 1024
NEG = -1e30
MESH = pl.DeviceIdType.MESH


def _cparams(sem=None):
    return pltpu.CompilerParams(dimension_semantics=sem, vmem_limit_bytes=VMEM_LIMIT)


def _sds(shape, dtype):
    return jax.ShapeDtypeStruct(tuple(shape), dtype)


def _dg(a, b, ca, cb):
    return lax.dot_general(a, b, (((ca,), (cb,)), ((), ())), preferred_element_type=F32)


def _nn(a, b):
    return _dg(a, b, 1, 0)


def _nt(a, b):
    return _dg(a, b, 1, 1)


def _tn(a, b):
    return _dg(a, b, 0, 0)


def _split3(a):
    hi = a.astype(BF16)
    r = a - hi.astype(F32)
    mid = r.astype(BF16)
    lo = (r - mid.astype(F32)).astype(BF16)
    return hi, mid, lo


def _exact_nn(a, sel):
    hi, mid, lo = _split3(a)
    return _nn(hi, sel) + _nn(mid, sel) + _nn(lo, sel)


def _pick(n, prefs):
    for p in prefs:
        if n % p == 0:
            return p
    return n


def _matmul(name, a, b, form, tm, tn, tk, outs, epilogue, extras=(), a_map=None, b_map=None, mnk=None, dep=None,
            sem=("parallel", "parallel", "arbitrary"), split=None, alias_dep=False):
    if form == "nn":
        (M, K), N = a.shape, b.shape[1]
        a_spec = pl.BlockSpec((tm, tk), a_map or (lambda i, j, k: (i, k)))
        b_spec = pl.BlockSpec((tk, tn), b_map or (lambda i, j, k: (k, j)))
        ca, cb = 1, 0
    elif form == "nt":
        (M, K), N = a.shape, b.shape[0]
        a_spec = pl.BlockSpec((tm, tk), a_map or (lambda i, j, k: (i, k)))
        b_spec = pl.BlockSpec((tn, tk), b_map or (lambda i, j, k: (j, k)))
        ca, cb = 1, 1
    else:
        (K, M), N = a.shape, b.shape[1]
        a_spec = pl.BlockSpec((tk, tm), a_map or (lambda i, j, k: (k, i)))
        b_spec = pl.BlockSpec((tk, tn), b_map or (lambda i, j, k: (k, j)))
        ca, cb = 0, 0
    if mnk is not None:
        M, N, K = mnk
    assert M % tm == 0 and N % tn == 0 and K % tk == 0, (name, M, N, K, tm, tn, tk)
    nk = K // tk
    ne, no = len(extras), len(outs)
    deps = [] if dep is None else [dep]
    nd = len(deps)

    def body(a_ref, b_ref, *rest):
        extra_refs, out_refs = rest[:ne], rest[ne + nd:ne + nd + no]
        j = pl.program_id(1)
        if split is not None:
            kind, n = split
            assert nk == 1 and form != "tn"
            tiled = [t for _, _, *t in list(extras) + list(outs)]
            refs = list(extra_refs) + list(out_refs)
            for ci in range(n):
                if kind == "cols":
                    cs = slice(ci * (tn // n), (ci + 1) * (tn // n))
                    part = _dg(a_ref[...].astype(MXU_DTYPE), (b_ref[:, cs] if form == "nn" else b_ref[cs, :]).astype(MXU_DTYPE), ca, cb)
                    view = [r.at[:, cs] if t else r for r, t in zip(refs, tiled)]
                else:
                    rs = slice(ci * (tm // n), (ci + 1) * (tm // n))
                    part = _dg(a_ref[rs, :].astype(MXU_DTYPE), b_ref[...].astype(MXU_DTYPE), ca, cb)
                    view = [r.at[rs, :] if t else r for r, t in zip(refs, tiled)]
                epilogue(part, view[:ne], view[ne:], j, ci)
            return
        part = _dg(a_ref[...].astype(MXU_DTYPE), b_ref[...].astype(MXU_DTYPE), ca, cb)
        if nk == 1:
            epilogue(part, extra_refs, out_refs, j, 0)
            return
        acc_ref = rest[-1]
        k = pl.program_id(2)

        @pl.when(k == 0)
        def _():
            acc_ref[...] = part

        @pl.when(k > 0)
        def _():
            acc_ref[...] += part

        @pl.when(k == nk - 1)
        def _():
            epilogue(acc_ref[...], extra_refs, out_refs, j, 0)

    res = pl.pallas_call(
        body,
        name=name,
        grid=(M // tm, N // tn, nk),
        in_specs=[a_spec, b_spec] + [s for _, s, *_ in extras] + [pl.BlockSpec(memory_space=pl.ANY)] * nd,
        out_specs=[s for _, s, *_ in outs],
        out_shape=[o for o, *_ in outs],
        scratch_shapes=[pltpu.VMEM((tm, tn), F32)] if nk > 1 else [],
        input_output_aliases={2 + ne: 0} if alias_dep else {},
        compiler_params=_cparams(sem),
    )(a, b, *[e for e, *_ in extras], *deps)
    return res


def _ij_spec(tm, tn):
    return pl.BlockSpec((tm, tn), lambda i, j, k: (i, j))


def _store_epilogue(acc, extra_refs, out_refs, j, ci):
    out_refs[0][...] = acc.astype(out_refs[0].dtype)


def _plain_mm(name, a, b, form, out_dtype, tm, tn, tk):
    M = a.shape[1] if form == "tn" else a.shape[0]
    N = b.shape[0] if form == "nt" else b.shape[1]
    return _matmul(name, a, b, form, tm, tn, tk, [(_sds((M, N), out_dtype), _ij_spec(tm, tn))], _store_epilogue)[0]


def _class_slabs(S):
    assert DILATIONS[0] == 1
    return [(g, d, r, S // d) for g, d in enumerate(DILATIONS) if d > 1 for r in range(d)]


def _stack_classes(name, t, out_dtype):
    S, W = t.shape

    def body(x_ref, o_ref):
        o_ref[0:S, :] = x_ref[...].astype(out_dtype)
        for g, d, r, n in _class_slabs(S):
            o_ref[g * S + r * n:g * S + (r + 1) * n, :] = x_ref[pl.ds(r, n, stride=d), :].astype(out_dtype)

    return pl.pallas_call(
        body,
        name=name,
        grid=(W // LANES,),
        in_specs=[pl.BlockSpec((S, LANES), lambda j: (0, j))],
        out_specs=pl.BlockSpec((3 * S, LANES), lambda j: (0, j)),
        out_shape=_sds((3 * S, W), out_dtype),
        compiler_params=_cparams(("parallel",)),
    )(t)


def _rope_tables(seq):
    half = HEAD_DIM // 2
    inv = ROPE_THETA ** (-jnp.arange(half, dtype=F32) * (2.0 / HEAD_DIM))
    inv = jnp.tile(inv, LANES // half)
    pos = []
    for d in DILATIONS:
        row = jnp.arange(seq)
        pos.append((row % (seq // d)) * d + row // (seq // d))
    ang = jnp.concatenate(pos).astype(F32)[:, None] * inv[None, :]
    first = (jnp.arange(LANES) % HEAD_DIM) < half
    sin = jnp.sin(ang)
    return jnp.cos(ang), jnp.where(first[None, :], -sin, sin)


def _partner(x):
    half = HEAD_DIM // 2
    lane = lax.broadcasted_iota(jnp.int32, x.shape, 1)
    first = (lane % HEAD_DIM) < half
    return jnp.where(first, pltpu.roll(x, LANES - half, 1), pltpu.roll(x, half, 1))


def _attn_proj(name, x3, w, cos3, sin3, tm, prev, dep=None):
    S3, D = x3.shape
    S = S3 // 3
    tn = 3 * D // 4
    nrow = S // tm
    local = prev is None

    def tile(j):
        q = 2 * lax.axis_index("x") + lax.axis_index("y")
        c0 = 3 * q + j if local else j + 3 * (j >= 3 * q).astype(jnp.int32)
        return c0, c0 // 4, c0 % 4

    def epilogue(acc, extra_refs, out_refs, j, ci):
        cos_ref, sin_ref = extra_refs
        o_ref = out_refs[0]
        _, _, place = tile(j)
        width = acc.shape[1]
        assert D % width == 0
        is_rot = (place * tn + ci * width) // D < 2
        c = jnp.where(is_rot, cos_ref[...], 1.0)
        s = jnp.where(is_rot, sin_ref[...], 0.0)
        for t in range(width // LANES):
            xs = acc[:, t * LANES:(t + 1) * LANES]
            o_ref[:, t * LANES:(t + 1) * LANES] = (xs * c + _partner(xs) * s).astype(o_ref.dtype)

    rows = lambda i, j: tile(j)[1] * nrow + i
    tab = pl.BlockSpec((tm, LANES), lambda i, j, k: (rows(i, j), 0))
    out = pl.BlockSpec((tm, tn), lambda i, j, k: (rows(i, j), tile(j)[2]))
    ntiles = 3 if local else 9
    return _matmul(name, x3, w, "nn", tm, tn, D, [(_sds((S3, 3 * D), MXU_DTYPE), out, True)], epilogue,
                   extras=[(cos3, tab), (sin3, tab)], a_map=lambda i, j, k: (rows(i, j), k),
                   b_map=lambda i, j, k: (k, j if local else tile(j)[0]), mnk=(nrow * tm, ntiles * tn, D),
                   dep=dep if local else prev, alias_dep=not local, split=("cols", 3))[0]


def _head_sel(d_model):
    h = jnp.arange(LANES)[:, None]
    l = jnp.arange(d_model)[None, :]
    return (l // HEAD_DIM == h).astype(BF16)


def _class_edges(b, nblk):
    g = b // nblk
    per_class = jnp.where(g == 0, nblk // DILATIONS[0], jnp.where(g == 1, nblk // DILATIONS[1], nblk // DILATIONS[2]))
    pos = (b % nblk) % per_class
    return pos != 0, pos != per_class - 1


def _two_heads(t, top):
    zero = jnp.zeros_like(t)
    return jnp.concatenate([jnp.where(top, t, zero), jnp.where(top, zero, t)], axis=0)


def _band_mask(has_prev):
    B = ATTN_BLK
    row = lax.broadcasted_iota(jnp.int32, (2 * B, 2 * B), 0) % B
    col = lax.broadcasted_iota(jnp.int32, (2 * B, 2 * B), 1)
    in_prev = jnp.logical_and(jnp.logical_and(col < B, col >= row), has_prev)
    in_own = jnp.logical_and(col >= B, col - B <= row)
    return jnp.logical_or(in_prev, in_own)


def _attn_fwd(P3, D):
    S3 = P3.shape[0]
    B = ATTN_BLK
    nblk = S3 // 3 // B
    npairs = D // LANES
    scale = HEAD_DIM ** -0.5

    def body(q_ref, kc_ref, vc_ref, kp_ref, vp_ref, o_ref, lse_ref):
        has_prev, _ = _class_edges(pl.program_id(0), nblk)
        ok = _band_mask(has_prev)
        lane = lax.broadcasted_iota(jnp.int32, (B, LANES), 1)
        top = lane < HEAD_DIM
        lse_acc = jnp.zeros((B, LANES), F32)
        for j in range(npairs):
            sl = slice(j * LANES, (j + 1) * LANES)
            Q = _two_heads(q_ref[:, sl] * scale, top)
            K2 = jnp.concatenate([kp_ref[:, sl], kc_ref[:, sl]], axis=0)
            V2 = jnp.concatenate([vp_ref[:, sl], vc_ref[:, sl]], axis=0)
            s = jnp.where(ok, _nt(Q, K2), NEG)
            m = jnp.max(s, axis=1, keepdims=True)
            p = jnp.exp(s - m)
            l = jnp.sum(p, axis=1, keepdims=True)
            o = _nn((p * (1.0 / l)).astype(MXU_DTYPE), V2)
            o_ref[:, sl] = jnp.where(top, o[:B], o[B:])
            lse = m + jnp.log(l)
            lse_acc = jnp.where(lane == 2 * j, lse[:B], jnp.where(lane == 2 * j + 1, lse[B:], lse_acc))
        lse_ref[...] = lse_acc

    blk = lambda part, prev: pl.BlockSpec(
        (B, D), (lambda b: (jnp.maximum(b - 1, 0), part)) if prev else (lambda b: (b, part)))
    return pl.pallas_call(
        body,
        name="attn_fwd",
        grid=(3 * nblk,),
        in_specs=[blk(0, False), blk(1, False), blk(2, False), blk(1, True), blk(2, True)],
        out_specs=[pl.BlockSpec((B, D), lambda b: (b, 0)), pl.BlockSpec((B, LANES), lambda b: (b, 0))],
        out_shape=[_sds((S3, D), F32), _sds((S3, LANES), F32)],
        compiler_params=_cparams(("parallel",)),
    )(P3, P3, P3, P3, P3)


def _attn_mix(o3, lse3, sel):
    S3, D = o3.shape
    S = S3 // 3

    def body(o3_ref, lse_ref, sel_ref, o_ref, L_ref, w_ref):
        @pl.when(pl.program_id(0) == 0)
        def _():
            w_ref[0] = lse_ref[0:S, :]
            for g, d, r, n in _class_slabs(S):
                w_ref[g, pl.ds(r, n, stride=d), :] = lse_ref[g * S + r * n:g * S + (r + 1) * n, :]
            a, b, c = w_ref[0], w_ref[1], w_ref[2]
            m = jnp.maximum(jnp.maximum(a, b), c)
            L = m + jnp.log(jnp.exp(a - m) + jnp.exp(b - m) + jnp.exp(c - m))
            L_ref[...] = L
            w_ref[0] = jnp.exp(a - L)
            w_ref[1] = jnp.exp(b - L)
            w_ref[2] = jnp.exp(c - L)

        s = sel_ref[...]
        o_ref[...] = _exact_nn(w_ref[0], s) * o3_ref[0:S, :]
        for g, d, r, n in _class_slabs(S):
            rows = pl.ds(r, n, stride=d)
            o_ref[rows, :] += _exact_nn(w_ref[g, rows, :], s) * o3_ref[g * S + r * n:g * S + (r + 1) * n, :]

    return pl.pallas_call(
        body,
        name="attn_mix",
        grid=(D // LANES,),
        in_specs=[pl.BlockSpec((S3, LANES), lambda j: (0, j)), pl.BlockSpec((S3, LANES), lambda j: (0, 0)),
                  pl.BlockSpec((LANES, LANES), lambda j: (0, j))],
        out_specs=[pl.BlockSpec((S, LANES), lambda j: (0, j)), pl.BlockSpec((S, LANES), lambda j: (0, 0))],
        out_shape=[_sds((S, D), F32), _sds((S, LANES), F32)],
        scratch_shapes=[pltpu.VMEM((3, S, LANES), F32)],
        compiler_params=_cparams(("arbitrary",)),
    )(o3, lse3, sel)


def _attn_bwd(P3, do3, L3, delta3, cos3, sin3, D, dep):
    S3 = P3.shape[0]
    B = ATTN_BLK
    nblk = S3 // 3 // B
    npairs = D // LANES
    scale = HEAD_DIM ** -0.5

    def body(c_ref, kp_ref, vp_ref, qn_ref, doc_ref, don_ref, Lc_ref, Ln_ref, dc_ref, dn_ref, cos_ref, sin_ref, dep_ref, out_ref):
        has_prev, has_next = _class_edges(pl.program_id(0), nblk)
        ok = _band_mask(has_prev)
        row = lax.broadcasted_iota(jnp.int32, (2 * B, B), 0) % B
        col = lax.broadcasted_iota(jnp.int32, (2 * B, B), 1)
        ok_n = jnp.logical_and(col >= row, has_next)
        lane = lax.broadcasted_iota(jnp.int32, (B, LANES), 1)
        top = lane < HEAD_DIM
        cos_t = cos_ref[...]
        sin_inv = -sin_ref[...]
        Lc_all, Ln_all, dc_all, dn_all = Lc_ref[...], Ln_ref[...], dc_ref[...], dn_ref[...]
        pair_col = lambda t, j: jnp.concatenate([t[:, 2 * j:2 * j + 1], t[:, 2 * j + 1:2 * j + 2]], axis=0)
        for j in range(npairs):
            sl = lambda part: slice(part * D + j * LANES, part * D + (j + 1) * LANES)
            pj = slice(j * LANES, (j + 1) * LANES)
            kc2, vc2 = c_ref[:, sl(1)], c_ref[:, sl(2)]
            K2 = jnp.concatenate([kp_ref[:, pj], kc2], axis=0)
            V2 = jnp.concatenate([vp_ref[:, pj], vc2], axis=0)
            Qc = _two_heads(c_ref[:, sl(0)] * scale, top)
            Qn = _two_heads(qn_ref[:, pj] * scale, top)
            DOc = _two_heads(doc_ref[:, pj].astype(MXU_DTYPE), top)
            DOn = _two_heads(don_ref[:, pj].astype(MXU_DTYPE), top)
            P_c = jnp.where(ok, jnp.exp(_nt(Qc, K2) - pair_col(Lc_all, j)), 0.0)
            dS_c = P_c * (_nt(DOc, V2) - pair_col(dc_all, j))
            P_n = jnp.where(ok_n, jnp.exp(_nt(Qn, kc2) - pair_col(Ln_all, j)), 0.0)
            dS_n = P_n * (_nt(DOn, vc2) - pair_col(dn_all, j))
            dq = _nn(dS_c.astype(MXU_DTYPE), K2)
            dq2 = jnp.where(top, dq[:B], dq[B:]) * scale
            Qk = jnp.concatenate([Qc, Qn], axis=0)
            DOk = jnp.concatenate([DOc, DOn], axis=0)
            dk2 = _tn(jnp.concatenate([dS_c[:, B:], dS_n], axis=0).astype(MXU_DTYPE), Qk)
            dv2 = _tn(jnp.concatenate([P_c[:, B:], P_n], axis=0).astype(MXU_DTYPE), DOk)
            out_ref[:, sl(0)] = (dq2 * cos_t + _partner(dq2) * sin_inv).astype(out_ref.dtype)
            out_ref[:, sl(1)] = (dk2 * cos_t + _partner(dk2) * sin_inv).astype(out_ref.dtype)
            out_ref[:, sl(2)] = dv2.astype(out_ref.dtype)

    cur = lambda b: b
    prv = lambda b: jnp.maximum(b - 1, 0)
    nxt = lambda b: jnp.minimum(b + 1, 3 * nblk - 1)
    spec = lambda w, f, part=0: pl.BlockSpec((B, w), lambda b: (f(b), part))
    return pl.pallas_call(
        body,
        name="attn_bwd",
        grid=(3 * nblk,),
        in_specs=[spec(3 * D, cur), spec(D, prv, 1), spec(D, prv, 2), spec(D, nxt, 0), spec(D, cur), spec(D, nxt),
                  spec(LANES, cur), spec(LANES, nxt), spec(LANES, cur), spec(LANES, nxt), spec(LANES, cur), spec(LANES, cur),
                  pl.BlockSpec(memory_space=pl.ANY)],
        out_specs=spec(3 * D, cur),
        out_shape=_sds((S3, 3 * D), MXU_DTYPE),
        compiler_params=_cparams(("parallel",)),
    )(P3, P3, P3, P3, do3, do3, L3, L3, delta3, delta3, cos3, sin3, dep)


def _input_grad(du, dx3):
    S, D = du.shape

    def body(du_ref, dx_ref, o_ref):
        o_ref[...] = ALPHA * du_ref[...] + dx_ref[0:S, :]
        for g, d, r, n in _class_slabs(S):
            o_ref[pl.ds(r, n, stride=d), :] += dx_ref[g * S + r * n:g * S + (r + 1) * n, :]

    return pl.pallas_call(
        body,
        name="input_grad",
        grid=(D // LANES,),
        in_specs=[pl.BlockSpec((S, LANES), lambda j: (0, j)), pl.BlockSpec((3 * S, LANES), lambda j: (0, j))],
        out_specs=pl.BlockSpec((S, LANES), lambda j: (0, j)),
        out_shape=_sds((S, D), F32),
        compiler_params=_cparams(("parallel",)),
    )(du, dx3)


def _chunk_causal(tb):
    r = lax.broadcasted_iota(jnp.int32, (tb, tb), 0)
    c = lax.broadcasted_iota(jnp.int32, (tb, tb), 1)
    return jnp.logical_and((r // HGRN_CHUNK) == (c // HGRN_CHUNK), r >= c)


def _chunk_sums(a, lower):
    C = HGRN_CHUNK
    r = lax.broadcasted_iota(jnp.int32, (C, C), 0)
    c = lax.broadcasted_iota(jnp.int32, (C, C), 1)
    tri = ((r >= c) if lower else (r <= c)).astype(BF16)
    parts = _split3(a)
    out = []
    for ci in range(a.shape[0] // C):
        rows = slice(ci * C, (ci + 1) * C)
        out.append(_nn(tri, parts[0][rows]) + _nn(tri, parts[1][rows]) + _nn(tri, parts[2][rows]))
    return jnp.concatenate(out, axis=0)


def _chunk_last(b):
    C = HGRN_CHUNK
    return jnp.concatenate([jnp.broadcast_to(b[(ci + 1) * C - 1:(ci + 1) * C, :], (C, b.shape[1]))
                            for ci in range(b.shape[0] // C)], axis=0)


def _lower_bound(lb_ref):
    l0, l1 = lb_ref[0:1, :], lb_ref[1:2, :]
    m = jnp.maximum(l0, l1)
    e0, e1 = jnp.exp(l0 - m), jnp.exp(l1 - m)
    return e1 / (e0 + e1)


def _hgrn_gates(q_raw, z, lb):
    sg = 1.0 / (1.0 + jnp.exp(-z))
    sn = 1.0 / (1.0 + jnp.exp(z))
    f = lb + (1.0 - lb) * sg
    key = (1.0 - lb) * sn
    sq = 1.0 / (1.0 + jnp.exp(-q_raw))
    return sg, sn, f, key, sq


def _hgrn_fwd(P1, lb_logits, norm_g, tb):
    S = P1.shape[0]
    D = P1.shape[1] // 3
    K = HGRN_DK
    H = D // K
    HP = H
    C = HGRN_CHUNK
    cpb = tb // C
    nt = S // tb

    def body(q_ref, f_ref, i_ref, lb_ref, g_ref, o_ref, n_ref, st_ref, state):
        t = pl.program_id(1)

        @pl.when(t == 0)
        def _():
            state[...] = jnp.zeros_like(state)

        lb_all = _lower_bound(lb_ref)
        low = _chunk_causal(tb)
        for hh in range(HP):
            lanes = slice(hh * K, (hh + 1) * K)
            q_raw, z, v = q_ref[:, lanes], f_ref[:, lanes], i_ref[:, lanes]
            sg, sn, f, key, sq = _hgrn_gates(q_raw, z, lb_all[:, lanes])
            b = _chunk_sums(jnp.log(f), lower=True)
            qd = (q_raw * sq * jnp.exp(b)).astype(MXU_DTYPE)
            kd = (key * jnp.exp(-b)).astype(MXU_DTYPE)
            kb = (key * jnp.exp(_chunk_last(b) - b)).astype(MXU_DTYPE)
            vm = v.astype(MXU_DTYPE)
            a = jnp.where(low, _nt(qd, kd), 0.0).astype(MXU_DTYPE)
            o_intra = _nn(a, vm)
            st = state[hh]
            outs = []
            for ci in range(cpb):
                rows = slice(ci * C, (ci + 1) * C)
                st_ref[hh, ci] = st
                outs.append(o_intra[rows] + _nt(qd[rows], st.astype(MXU_DTYPE)))
                st = st * jnp.exp(b[(ci + 1) * C - 1:(ci + 1) * C, :]) + _tn(vm[rows], kb[rows])
            state[hh] = st
            o = jnp.concatenate(outs, axis=0)
            o_ref[:, lanes] = o
            rs = lax.rsqrt(jnp.mean(o * o, axis=1, keepdims=True) + RMS_EPS)
            n_ref[:, lanes] = o * rs * g_ref[:, lanes]

    tok = lambda part: pl.BlockSpec((tb, HP * K), lambda h, t: (t, part * (H // HP) + h))
    vec = lambda rows: pl.BlockSpec((rows, HP * K), lambda h, t: (0, h))
    return pl.pallas_call(
        body,
        name="hgrn_fwd",
        grid=(H // HP, nt),
        in_specs=[tok(0), tok(1), tok(2), vec(2), vec(1)],
        out_specs=[tok(0), tok(0), pl.BlockSpec((HP, cpb, K, K), lambda h, t: (h, t, 0, 0))],
        out_shape=[_sds((S, D), F32), _sds((S, D), F32), _sds((H, S // C, K, K), F32)],
        scratch_shapes=[pltpu.VMEM((HP, K, K), F32)],
        compiler_params=_cparams(("parallel", "arbitrary")),
    )(P1, P1, P1, lb_logits, norm_g)


def _hgrn_bwd(P1, o_pre, states, dn, lb_logits, norm_g, tb):
    S = P1.shape[0]
    D = P1.shape[1] // 3
    K = HGRN_DK
    H = D // K
    HP = H
    C = HGRN_CHUNK
    cpb = tb // C
    nt = S // tb

    def body(q_ref, f_ref, i_ref, o_ref, st_ref, dn_ref, lb_ref, g_ref, d_ref, dg_ref, dlb_ref, dstate):
        t = pl.program_id(1)

        @pl.when(t == 0)
        def _():
            dstate[...] = jnp.zeros_like(dstate)
            dg_ref[...] = jnp.zeros_like(dg_ref)
            dlb_ref[...] = jnp.zeros_like(dlb_ref)

        lb_all = _lower_bound(lb_ref)
        low = _chunk_causal(tb)
        for hh in range(HP):
            lanes = slice(hh * K, (hh + 1) * K)
            lb = lb_all[:, lanes]
            gn = g_ref[:, lanes]
            q_raw, z, v = q_ref[:, lanes], f_ref[:, lanes], i_ref[:, lanes]
            sg, sn, f, key, sq = _hgrn_gates(q_raw, z, lb)
            b = _chunk_sums(jnp.log(f), lower=True)
            e_pos, e_neg, e_rel = jnp.exp(b), jnp.exp(-b), jnp.exp(_chunk_last(b) - b)
            qd_f, kd_f, kb_f = q_raw * sq * e_pos, key * e_neg, key * e_rel
            qd, kd, kb = qd_f.astype(MXU_DTYPE), kd_f.astype(MXU_DTYPE), kb_f.astype(MXU_DTYPE)
            vm = v.astype(MXU_DTYPE)
            a = jnp.where(low, _nt(qd, kd), 0.0).astype(MXU_DTYPE)
            o = o_ref[:, lanes]
            dnn = dn_ref[:, lanes]
            rs = lax.rsqrt(jnp.mean(o * o, axis=1, keepdims=True) + RMS_EPS)
            dg_ref[:, lanes] += jnp.sum(dnn * o * rs, axis=0, keepdims=True)
            tg = dnn * gn
            dom = (rs * tg - o * (rs * rs * rs) * jnp.mean(tg * o, axis=1, keepdims=True)).astype(MXU_DTYPE)
            da = jnp.where(low, _nt(dom, vm), 0.0).astype(MXU_DTYPE)
            dv = _tn(a, dom)
            dqd = _nn(da, kd)
            dkd = _tn(da, qd)
            dst = dstate[hh]
            dv_s, dqd_s, dkb_s, dbl_s = [None] * cpb, [None] * cpb, [None] * cpb, [None] * cpb
            for ci in reversed(range(cpb)):
                rows = slice(ci * C, (ci + 1) * C)
                st = st_ref[hh, ci]
                dstm = dst.astype(MXU_DTYPE)
                dec = jnp.exp(b[(ci + 1) * C - 1:(ci + 1) * C, :])
                dv_s[ci] = _nt(kb[rows], dstm)
                dkb_s[ci] = _nn(vm[rows], dstm)
                dqd_s[ci] = _nn(dom[rows], st.astype(MXU_DTYPE))
                db_last = jnp.sum(dkb_s[ci] * kb_f[rows], axis=0, keepdims=True) + jnp.sum(dst * st, axis=0, keepdims=True) * dec
                dbl_s[ci] = jnp.broadcast_to(db_last, (C, K))
                dst = dst * dec + _tn(dom[rows], qd[rows])
            dstate[hh] = dst
            dv = dv + jnp.concatenate(dv_s, axis=0)
            dqd = dqd + jnp.concatenate(dqd_s, axis=0)
            dkb = jnp.concatenate(dkb_s, axis=0)
            dkey = dkd * e_neg + dkb * e_rel
            db = dqd * qd_f - dkd * kd_f - dkb * kb_f
            dlogf = _chunk_sums(db, lower=False) + jnp.concatenate(dbl_s, axis=0)
            gz = (1.0 - lb) * sg * sn
            col = lambda part: slice(part * D + hh * K, part * D + (hh + 1) * K)
            d_ref[:, col(0)] = (dqd * e_pos * (sq + q_raw * sq * (1.0 - sq))).astype(d_ref.dtype)
            d_ref[:, col(1)] = (dlogf * gz / f - dkey * gz).astype(d_ref.dtype)
            d_ref[:, col(2)] = dv.astype(d_ref.dtype)
            dlb_ref[:, lanes] += jnp.sum(dlogf * sn / f - dkey * sn, axis=0, keepdims=True)

    rev = lambda t: nt - 1 - t
    tok = lambda part: pl.BlockSpec((tb, HP * K), lambda h, t: (rev(t), part * (H // HP) + h))
    vec = lambda rows: pl.BlockSpec((rows, HP * K), lambda h, t: (0, h))
    outs = pl.pallas_call(
        body,
        name="hgrn_bwd",
        grid=(H // HP, nt),
        in_specs=[tok(0), tok(1), tok(2), tok(0),
                  pl.BlockSpec((HP, cpb, K, K), lambda h, t: (h, rev(t), 0, 0)),
                  tok(0), vec(2), vec(1)],
        out_specs=[pl.BlockSpec((tb, 3 * D), lambda h, t: (rev(t), 0)), vec(1), vec(1)],
        out_shape=[_sds((S, 3 * D), MXU_DTYPE)] + [_sds((1, D), F32)] * 2,
        scratch_shapes=[pltpu.VMEM((HP, K, K), F32)],
        compiler_params=_cparams(("parallel", "arbitrary")),
    )(P1, P1, P1, o_pre, states, dn, lb_logits, norm_g)
    return outs


def _lb_logits_grad(dlb, lb_logits):
    def body(d_ref, l_ref, o_ref):
        s1 = _lower_bound(l_ref)
        d = d_ref[...]
        o_ref[0:1, :] = -(1.0 - s1) * s1 * d
        o_ref[1:2, :] = s1 * (1.0 - s1) * d

    return pl.pallas_call(body, name="lb_logits_grad", out_shape=_sds(lb_logits.shape, F32))(dlb, lb_logits)


def _ln_epilogue(acc, extra_refs, out_refs, j, ci):
    res_ref, g_ref, b_ref = extra_refs
    x_ref, xm_ref, xhat_ref, rstd_ref = out_refs
    u = ALPHA * res_ref[...] + acc
    mu = jnp.mean(u, axis=1, keepdims=True)
    cen = u - mu
    rstd = lax.rsqrt(jnp.mean(cen * cen, axis=1, keepdims=True) + LN_EPS)
    xhat = cen * rstd
    xhat_ref[...] = xhat
    x = xhat * g_ref[...] + b_ref[...]
    x_ref[...] = x
    xm_ref[...] = x.astype(xm_ref.dtype)
    rstd_ref[...] = rstd


def _mm_res_ln(name, a, w_full, res, g, b, tm, tk):
    S, D = res.shape
    row = pl.BlockSpec((tm, D), lambda i, j, k: (i, 0))
    vec = pl.BlockSpec((1, D), lambda i, j, k: (0, 0))
    outs = [(_sds((S, D), F32), row, True), (_sds((S, D), MXU_DTYPE), row, True), (_sds((S, D), F32), row, True),
            (_sds((S, 1), F32), pl.BlockSpec((tm, 1), lambda i, j, k: (i, 0)), True)]
    return _matmul(name, a, w_full, "nn", tm, D, tk, outs, _ln_epilogue, extras=[(res, row, True), (g, vec), (b, vec)],
                   split=("rows", 2) if tk == a.shape[1] else None)


def _ln_bwd_rows(dy, xh, rstd, g, first, du_ref, dum_ref, dg_ref, db_ref):
    if first is not None:
        @pl.when(first)
        def _():
            dg_ref[...] = jnp.zeros_like(dg_ref)
            db_ref[...] = jnp.zeros_like(db_ref)

    dg_ref[...] += jnp.sum(dy * xh, axis=0, keepdims=True)
    db_ref[...] += jnp.sum(dy, axis=0, keepdims=True)
    dxh = dy * g
    m1 = jnp.mean(dxh, axis=1, keepdims=True)
    m2 = jnp.mean(dxh * xh, axis=1, keepdims=True)
    du = rstd * (dxh - m1 - xh * m2)
    du_ref[...] = du
    dum_ref[...] = du.astype(dum_ref.dtype)


def _loss_ln_bwd(y, target, xhat, rstd, g, tm):
    S, D = y.shape

    def body(y_ref, t_ref, xh_ref, r_ref, g_ref, sq_ref, du_ref, dum_ref, dg_ref, db_ref):
        first = pl.program_id(0) == 0

        @pl.when(first)
        def _():
            sq_ref[...] = jnp.zeros_like(sq_ref)

        e = y_ref[...] - t_ref[...]
        sq_ref[...] += jnp.sum(e * e, axis=0, keepdims=True)
        _ln_bwd_rows(e / D, xh_ref[...], r_ref[...], g_ref[...], first, du_ref, dum_ref, dg_ref, db_ref)

    row = pl.BlockSpec((tm, D), lambda i: (i, 0))
    vec = pl.BlockSpec((1, D), lambda i: (0, 0))
    return pl.pallas_call(
        body,
        name="loss_ln_bwd",
        grid=(S // tm,),
        in_specs=[row, row, row, pl.BlockSpec((tm, 1), lambda i: (i, 0)), vec],
        out_specs=[vec, row, row, vec, vec],
        out_shape=[_sds((1, D), F32), _sds((S, D), F32), _sds((S, D), MXU_DTYPE), _sds((1, D), F32), _sds((1, D), F32)],
        compiler_params=_cparams(("arbitrary",)),
    )(y, target, xhat, rstd, g)


def _mlp_up(name, x, w_up, tm, tn, tk):
    S = x.shape[0]
    F = w_up.shape[1]

    def epilogue(acc, extra_refs, out_refs, j, ci):
        r = jnp.maximum(acc, 0.0)
        out_refs[0][...] = (r * r).astype(out_refs[0].dtype)

    return _matmul(name, x, w_up, "nn", tm, tn, tk, [(_sds((S, F), MXU_DTYPE), _ij_spec(tm, tn), True)], epilogue,
                   split=("cols", 2))[0]


def _mlp_down_bwd(name, dy, w_down, a, tm, tn, tk):
    S, F = a.shape

    def epilogue(acc, extra_refs, out_refs, j, ci):
        out_refs[0][...] = (acc * (2.0 * jnp.sqrt(extra_refs[0][...].astype(F32)))).astype(out_refs[0].dtype)

    return _matmul(name, dy, w_down, "nt", tm, tn, tk, [(_sds((S, F), MXU_DTYPE), _ij_spec(tm, tn), True)], epilogue,
                   extras=[(a, _ij_spec(tm, tn), True)], split=("cols", 2))[0]


def _mm_nt_res_ln_bwd(name, dy, w, du, xhat, rstd, g, tm, tk, dep):
    S, D = du.shape

    def epilogue(acc, extra_refs, out_refs, j, ci):
        du_ref, xh_ref, r_ref, g_ref = extra_refs
        first = (pl.program_id(0) == 0) if ci == 0 else None
        _ln_bwd_rows(ALPHA * du_ref[...] + acc, xh_ref[...], r_ref[...], g_ref[...], first, *out_refs)

    row = pl.BlockSpec((tm, D), lambda i, j, k: (i, 0))
    vec = pl.BlockSpec((1, D), lambda i, j, k: (0, 0))
    return _matmul(name, dy, w, "nt", tm, D, tk,
                   [(_sds((S, D), F32), row, True), (_sds((S, D), MXU_DTYPE), row, True), (_sds((1, D), F32), vec),
                    (_sds((1, D), F32), vec)], epilogue,
                   extras=[(du, row, True), (xhat, row, True), (rstd, pl.BlockSpec((tm, 1), lambda i, j, k: (i, 0)), True), (g, vec)],
                   dep=dep, sem=("arbitrary", "arbitrary", "arbitrary"), split=("rows", 2))


def _attn_out_bwd(du, w_out, o, sel_t, tm, tk):
    S, D = o.shape

    def epilogue(acc, extra_refs, out_refs, j, ci):
        out_refs[0][...] = acc
        out_refs[1][...] = _exact_nn(acc * extra_refs[0][...], extra_refs[1][...])

    row = pl.BlockSpec((tm, D), lambda i, j, k: (i, 0))
    slim = pl.BlockSpec((tm, LANES), lambda i, j, k: (i, 0))
    return _matmul("attn_out_bwd", du, w_out, "nt", tm, D, tk,
                   [(_sds((S, D), F32), row, True), (_sds((S, LANES), F32), slim, True)], epilogue,
                   extras=[(o, row, True), (sel_t, pl.BlockSpec((D, LANES), lambda i, j, k: (0, 0)))], split=("rows", 2))


def _adamw(name, w, gs, m, v):
    shape = w.shape
    cols = shape[-1]
    rows = math.prod(shape[:-1])
    w2, m2, v2 = (t.reshape(rows, cols) for t in (w, m, v))
    gs2 = [g.reshape(-1, cols) for g in gs]
    ng = len(gs2)
    tr = _pick(rows // ng, (256, 128, 64, 32, 16, 8))
    per = rows // ng // tr
    c1 = 1.0 - ADAM_B1 ** ADAM_STEP
    c2 = 1.0 - ADAM_B2 ** ADAM_STEP

    def body(w_ref, m_ref, v_ref, *rest):
        g_refs, (d_ref, nm_ref, nv_ref), g_out = rest[:ng], rest[ng:ng + 3], rest[ng + 3:]
        gg = g_refs[0][...]
        if ng == 2:
            gg = jnp.where(pl.program_id(0) < per, gg, g_refs[1][...])
        g_out[0][...] = gg
        nm = ADAM_B1 * m_ref[...] + (1.0 - ADAM_B1) * gg
        nv = ADAM_B2 * v_ref[...] + (1.0 - ADAM_B2) * (gg * gg)
        nm_ref[...] = nm
        nv_ref[...] = nv
        d_ref[...] = -ADAM_LR * ((nm / c1) / (jnp.sqrt(nv / c2) + ADAM_EPS) + ADAM_WD * w_ref[...])

    blk = pl.BlockSpec((tr, cols), lambda i: (i, 0))
    g_specs = [blk] if ng == 1 else [pl.BlockSpec((tr, cols), lambda i: (jnp.minimum(i, per - 1), 0)),
                                     pl.BlockSpec((tr, cols), lambda i: (jnp.maximum(i - per, 0), 0))]
    outs = pl.pallas_call(
        body,
        name=name,
        grid=(rows // tr,),
        in_specs=[blk] * 3 + g_specs,
        out_specs=[blk] * 4,
        out_shape=[_sds((rows, cols), F32)] * 4,
        compiler_params=_cparams(("parallel",)),
    )(w2, m2, v2, *gs2)
    return tuple(o.reshape(shape) for o in outs)


HBM = pl.BlockSpec(memory_space=pl.ANY)


def _shard_slice(ref, axis, size, index):
    idx = [slice(None)] * len(ref.shape)
    idx[axis] = pl.ds(pl.multiple_of(index * size, 8), size)
    return ref.at[tuple(idx)]


def _share_halves(name, full, tr):
    R, W4 = full.shape
    W, h = W4 // 4, R // 2
    steps = [(k, t) for k in range(3) for t in range(h // tr)]

    def body(f_in, f_ref, buf, lsem, ssem, rsem):
        x, y, c = lax.axis_index("x"), lax.axis_index("y"), lax.axis_index("c")
        sibling = (x, y, 1 - c)
        chips = [(1 - x, y), (x, 1 - y), (1 - x, 1 - y)]

        def tile(k, t):
            px, py = chips[k]
            return f_ref.at[pl.ds(pl.multiple_of(c * h + t * tr, 8), tr), pl.ds(pl.multiple_of((2 * px + py) * W, LANES), W)]

        sends = []
        for s, (k, t) in enumerate(steps):
            slot = s % 2
            if s >= 2:
                sends[s - 2].wait_send()
            lc = pltpu.make_async_copy(tile(k, t), buf.at[slot], lsem.at[slot])
            lc.start()
            lc.wait()
            rc = pltpu.make_async_remote_copy(src_ref=buf.at[slot], dst_ref=tile(k, t), send_sem=ssem.at[slot], recv_sem=rsem,
                                              device_id=sibling, device_id_type=MESH)
            rc.start()
            sends.append(rc)
        for rc in sends[-2:]:
            rc.wait_send()
        whole = f_ref.at[pl.ds(0, h), pl.ds(0, 3 * W)]
        pltpu.make_async_remote_copy(src_ref=whole, dst_ref=whole, send_sem=ssem.at[0], recv_sem=rsem,
                                     device_id=sibling, device_id_type=MESH).wait_recv()

    return pl.pallas_call(
        body,
        name=name,
        in_specs=[HBM],
        out_specs=HBM,
        out_shape=_sds(full.shape, full.dtype),
        input_output_aliases={0: 0},
        scratch_shapes=[pltpu.VMEM((2, tr, W), full.dtype), pltpu.SemaphoreType.DMA((2,)), pltpu.SemaphoreType.DMA((2,)),
                        pltpu.SemaphoreType.DMA(())],
    )(full)


IN_HBM = pl.BlockSpec(memory_space=pltpu.HBM)
IN_SEM = pl.BlockSpec(memory_space=pltpu.SEMAPHORE)
DATAFLOW = pltpu.SideEffectType.DATAFLOW_SIDE_EFFECTING


def _hbm(t):
    return pltpu.with_memory_space_constraint(t, pltpu.HBM)


def _token_spec():
    return pl.BlockSpec(memory_space=pltpu.VMEM)


def _gather_copies(s_refs, f_refs, axes, halves, send, recv, loc, arrival):
    x, y, c = lax.axis_index("x"), lax.axis_index("y"), lax.axis_index("c")
    chips = [(1 - x, y), (x, 1 - y), (1 - x, 1 - y)]
    local, remote = [], []
    for a in range(len(s_refs)):
        size = s_refs[a].shape[axes[a]]
        local.append(pltpu.make_async_copy(s_refs[a], _shard_slice(f_refs[a], axes[a], size, 2 * x + y), loc.at[a]))
        for k, (px, py) in enumerate(chips):
            block = (2 * px + py) if arrival else (2 * x + y)
            src, dst = s_refs[a], _shard_slice(f_refs[a], axes[a], size, block)
            if halves:
                assert axes[a] == 1 and len(s_refs[a].shape) == 2
                h = s_refs[a].shape[0] // 2
                rows = pl.ds(pl.multiple_of(c * h, 8), h)
                src = s_refs[a].at[rows, :]
                dst = f_refs[a].at[rows, pl.ds(pl.multiple_of(block * size, LANES), size)]
            remote.append(pltpu.make_async_remote_copy(src_ref=src, dst_ref=dst, send_sem=send.at[3 * a + k],
                                                       recv_sem=recv.at[3 * a + k], device_id=(px, py, c), device_id_type=MESH))
    return local, remote


def _gather_start(name, shards, axes, after, halves=False):
    n = len(shards)
    fulls = []
    for s, ax in zip(shards, axes):
        fs = list(s.shape)
        fs[ax] *= 4
        fulls.append(lax.empty(tuple(fs), s.dtype))

    def body(*refs):
        s_refs, f_refs = refs[:n], refs[n:2 * n]
        send, recv, loc, token = refs[2 * n + 1], refs[2 * n + 2], refs[2 * n + 3], refs[-1]
        local, remote = _gather_copies(s_refs, f_refs, axes, halves, send, recv, loc, arrival=False)
        for cp in remote + local:
            cp.start()
        token[...] = jnp.zeros_like(token)

    outs = pl.pallas_call(
        body,
        name=name,
        out_shape=(pltpu.SemaphoreType.DMA((3 * n,)), pltpu.SemaphoreType.DMA((3 * n,)), pltpu.SemaphoreType.DMA((n,)),
                   *[pltpu.HBM(t.shape, t.dtype) for t in shards + fulls], _sds((8, LANES), F32)),
        in_specs=[IN_HBM] * (2 * n) + [HBM],
        out_specs=(IN_SEM, IN_SEM, IN_SEM, *[IN_HBM] * (2 * n), _token_spec()),
        input_output_aliases={i: 3 + i for i in range(2 * n)},
        compiler_params=pltpu.CompilerParams(has_side_effects=DATAFLOW),
    )(*[_hbm(t) for t in shards + fulls], after)
    return (outs[0], outs[1], outs[2], list(outs[3:3 + n]), list(outs[3 + n:3 + 2 * n]), axes, halves), outs[-1]


def _gather_wait(name, state, *after):
    send, recv, loc, s_thru, f_thru, axes, halves = state
    n = len(s_thru)

    def body(*refs):
        s_refs, f_refs = refs[:n], refs[n:2 * n]
        local, remote = _gather_copies(s_refs, f_refs, axes, halves, refs[2 * n], refs[2 * n + 1], refs[2 * n + 2], arrival=True)
        for cp in local:
            cp.wait()
        for cp in remote:
            cp.wait_send()
            cp.wait_recv()

    outs = pl.pallas_call(
        body,
        name=name,
        out_shape=tuple(pltpu.HBM(t.shape, t.dtype) for t in s_thru + f_thru),
        in_specs=[IN_HBM] * (2 * n) + [IN_SEM, IN_SEM, IN_SEM] + [HBM] * len(after),
        out_specs=tuple([IN_HBM] * (2 * n)),
        input_output_aliases={i: i for i in range(2 * n)},
        compiler_params=pltpu.CompilerParams(has_side_effects=DATAFLOW),
    )(*s_thru, *f_thru, send, recv, loc, *after)
    return list(outs[n:2 * n])


FLIPS = [(fx, fy, fc) for fx in (0, 1) for fy in (0, 1) for fc in (0, 1)][1:]


def _piece_shape(shape, axis):
    ps = list(shape)
    if axis == 0:
        ps[0] //= 8
    else:
        ps[0] //= 2
        ps[axis] //= 4
    return tuple(ps)


def _piece(ref, axis, q, c):
    shape = ref.shape
    idx = [slice(None)] * len(shape)
    if axis == 0:
        h = shape[0] // 8
        idx[0] = pl.ds(pl.multiple_of((2 * q + c) * h, 8), h)
    else:
        h, w = shape[0] // 2, shape[axis] // 4
        idx[0] = pl.ds(c * h, h)
        idx[axis] = pl.ds(pl.multiple_of(q * w, LANES if axis == len(shape) - 1 else 8), w)
    return ref.at[tuple(idx)]


def _scatter_copies(g_refs, l_refs, axes, send, recv):
    x, y, c = lax.axis_index("x"), lax.axis_index("y"), lax.axis_index("c")
    out = []
    for a in range(len(g_refs)):
        for k, (fx, fy, fc) in enumerate(FLIPS):
            tx, ty, tc = x ^ fx, y ^ fy, c ^ fc
            out.append(pltpu.make_async_remote_copy(
                src_ref=_piece(g_refs[a], axes[a], 2 * tx + ty, tc), dst_ref=l_refs[a].at[k],
                send_sem=send.at[7 * a + k], recv_sem=recv.at[7 * a + k], device_id=(tx, ty, tc), device_id_type=MESH))
    return out


def _scatter_start(name, grads, axes):
    n = len(grads)
    lands = [lax.empty((7,) + _piece_shape(g.shape, ax), g.dtype) for g, ax in zip(grads, axes)]

    def body(*refs):
        g_refs, l_refs = refs[:n], refs[n:2 * n]
        send, recv, token = refs[2 * n], refs[2 * n + 1], refs[-1]
        for cp in _scatter_copies(g_refs, l_refs, axes, send, recv):
            cp.start()
        token[...] = jnp.zeros_like(token)

    outs = pl.pallas_call(
        body,
        name=name,
        out_shape=(pltpu.SemaphoreType.DMA((7 * n,)), pltpu.SemaphoreType.DMA((7 * n,)),
                   *[pltpu.HBM(t.shape, t.dtype) for t in grads + lands], _sds((8, LANES), F32)),
        in_specs=[IN_HBM] * (2 * n),
        out_specs=(IN_SEM, IN_SEM, *[IN_HBM] * (2 * n), _token_spec()),
        input_output_aliases={i: 2 + i for i in range(2 * n)},
        compiler_params=pltpu.CompilerParams(has_side_effects=DATAFLOW),
    )(*[_hbm(t) for t in grads + lands])
    return (outs[0], outs[1], list(outs[2:2 + n]), list(outs[2 + n:2 + 2 * n]), axes), outs[-1]


def _scatter_wait(name, state, *after):
    send, recv, g_thru, l_thru, axes = state
    n = len(g_thru)

    def body(*refs):
        g_refs, l_refs = refs[:n], refs[n:2 * n]
        for cp in _scatter_copies(g_refs, l_refs, axes, refs[2 * n], refs[2 * n + 1]):
            cp.wait_send()
            cp.wait_recv()

    outs = pl.pallas_call(
        body,
        name=name,
        out_shape=tuple(pltpu.HBM(t.shape, t.dtype) for t in g_thru + l_thru),
        in_specs=[IN_HBM] * (2 * n) + [IN_SEM, IN_SEM] + [HBM] * len(after),
        out_specs=tuple([IN_HBM] * (2 * n)),
        input_output_aliases={i: i for i in range(2 * n)},
        compiler_params=pltpu.CompilerParams(has_side_effects=DATAFLOW),
    )(*g_thru, *l_thru, send, recv, *after)
    return list(outs[:n]), list(outs[n:2 * n])


def _reduce_join(name, landing, g, axis):
    R, C = _piece_shape(g.shape, axis)
    l3 = landing.reshape(7, R, C)
    tr = _pick(R, [t for t in (512, 256, 128, 64, 32, 16, 8) if t * C <= 256 * 1024])
    nsteps = R // tr

    def own_block(i):
        q, c = 2 * lax.axis_index("x") + lax.axis_index("y"), lax.axis_index("c")
        return ((2 * q + c) * nsteps + i, 0) if axis == 0 else (c * nsteps + i, q)

    def body(own_ref, l_ref, o_ref, buf, send, loc, recv):
        i = pl.program_id(0)
        x, y, c = lax.axis_index("x"), lax.axis_index("y"), lax.axis_index("c")
        sibling = (x, y, 1 - c)

        def copies(slot, step):
            dst = o_ref.at[pl.ds(pl.multiple_of(c * R + step * tr, 8), tr), :]
            return (pltpu.make_async_copy(buf.at[slot], dst, loc.at[slot]),
                    pltpu.make_async_remote_copy(src_ref=buf.at[slot], dst_ref=dst, send_sem=send.at[slot], recv_sem=recv,
                                                 device_id=sibling, device_id_type=MESH))

        @pl.when(i >= 2)
        def _():
            lc, rc = copies(i % 2, i - 2)
            lc.wait()
            rc.wait_send()

        acc = own_ref[...].astype(F32)
        for s in range(7):
            acc = acc + l_ref[s].astype(F32)
        buf[i % 2] = acc
        lc, rc = copies(i % 2, i)
        lc.start()
        rc.start()

        @pl.when(i == nsteps - 1)
        def _():
            for st in range(max(nsteps - 2, 0), nsteps):
                lc, rc = copies(st % 2, st)
                lc.wait()
                rc.wait_send()
            theirs = o_ref.at[pl.ds(pl.multiple_of((1 - c) * R, 8), R), :]
            pltpu.make_async_remote_copy(src_ref=theirs, dst_ref=theirs, send_sem=send.at[0], recv_sem=recv,
                                         device_id=sibling, device_id_type=MESH).wait_recv()

    return pl.pallas_call(
        body,
        name=name,
        grid=(nsteps,),
        in_specs=[pl.BlockSpec((tr, C), own_block), pl.BlockSpec((7, tr, C), lambda i: (0, i, 0))],
        out_specs=HBM,
        out_shape=_sds((2 * R, C), F32),
        scratch_shapes=[pltpu.VMEM((2, tr, C), F32), pltpu.SemaphoreType.DMA((2,)), pltpu.SemaphoreType.DMA((2,)),
                        pltpu.SemaphoreType.DMA(())],
        compiler_params=_cparams(("arbitrary",)),
    )(g, l3)


def _all_reduce_small(v, dep):
    R, D = v.shape

    def body(v_ref, dep_ref, o_ref, land, send, recv):
        x, y, c = lax.axis_index("x"), lax.axis_index("y"), lax.axis_index("c")
        my_slot = 4 * x + 2 * y + c
        land[my_slot] = v_ref[...]
        for k, (fx, fy, fc) in enumerate(FLIPS):
            tx, ty, tc = x ^ fx, y ^ fy, c ^ fc
            pltpu.make_async_remote_copy(src_ref=v_ref, dst_ref=land.at[my_slot], send_sem=send.at[k], recv_sem=recv.at[k],
                                         device_id=(tx, ty, tc), device_id_type=MESH).start()
        for k, (fx, fy, fc) in enumerate(FLIPS):
            tx, ty, tc = x ^ fx, y ^ fy, c ^ fc
            cp = pltpu.make_async_remote_copy(src_ref=v_ref, dst_ref=land.at[4 * tx + 2 * ty + tc], send_sem=send.at[k],
                                              recv_sem=recv.at[k], device_id=(tx, ty, tc), device_id_type=MESH)
            cp.wait_send()
            cp.wait_recv()
        acc = land[0]
        for s in range(1, 8):
            acc = acc + land[s]
        o_ref[...] = acc

    return pl.pallas_call(
        body,
        name="all_reduce_small",
        in_specs=[pl.BlockSpec(memory_space=pltpu.VMEM), pl.BlockSpec(memory_space=pl.ANY)],
        out_specs=pl.BlockSpec(memory_space=pltpu.VMEM),
        out_shape=_sds((R, D), F32),
        scratch_shapes=[pltpu.VMEM((8, R, D), F32), pltpu.SemaphoreType.DMA((7,)), pltpu.SemaphoreType.DMA((7,))],
    )(v, dep)


def kernel(x, attn_w_in, attn_w_out, hgrn_w_in, hgrn_w_out, hgrn_norm_g, lb_logits, ln_mix_g, ln_mix_b, ln_ffn_g, ln_ffn_b, ffn_w_up, ffn_w_down, loss_target, m_attn_w_in, m_attn_w_out, m_hgrn_w_in, m_hgrn_w_out, m_hgrn_norm_g, m_lb_logits, m_ln_mix_g, m_ln_mix_b, m_ln_ffn_g, m_ln_ffn_b, m_ffn_w_up, m_ffn_w_down, v_attn_w_in, v_attn_w_out, v_hgrn_w_in, v_hgrn_w_out, v_hgrn_norm_g, v_lb_logits, v_ln_mix_g, v_ln_mix_b, v_ln_ffn_g, v_ln_ffn_b, v_ffn_w_up, v_ffn_w_down):
    xs = x[0]
    tgt = loss_target[0]
    S, D = xs.shape
    F = ffn_w_up.shape[2] * 4
    T1 = _pick(S, (1024, 512, 256))
    T2 = _pick(S, (2048, 1024, 512))
    TH = _pick(S, (512, 256))
    TB = _pick(S, (128,))
    TF = _pick(F, (1024, 512))
    TG = _pick(3 * D, (1536, 1024, 768))
    TW = _pick(F, (2048, 1024))

    cast = lambda w: w.astype(MXU_DTYPE)
    st_a, tok = _gather_start("gather_a", [cast(attn_w_in[0])], [1], jnp.zeros((8, LANES), F32), halves=True)
    tok, (xs_late, w_aout, w_fup, w_fdown, w_hin, w_hout) = lax.optimization_barrier(
        (tok, (xs, attn_w_out, ffn_w_up, ffn_w_down, hgrn_w_in, hgrn_w_out)))
    st_b, tok = _gather_start("gather_b", [cast(w_aout[0]), cast(w_fup[0]), cast(w_fdown[0])], [0, 1, 0], tok)
    st_c, tok = _gather_start("gather_c", [cast(w_hin[0]), cast(w_hout[0]), hgrn_norm_g, cast(w_fup[1]), cast(w_fdown[1])],
                              [1, 0, 1, 1, 0], tok)

    cos3, sin3 = _rope_tables(S)
    sel = _head_sel(D)
    sel_t = sel.T

    xc3 = _stack_classes("x_classes", xs_late, MXU_DTYPE)
    P3 = _attn_proj("attn_proj_own", xc3, st_a[3][0], cos3, sin3, T2, None, tok)
    (wa_in,) = _gather_wait("gather_a_wait", st_a, P3)
    wa_in = _share_halves("share_a", wa_in, _pick(D // 2, (256, 128)))
    P3 = _attn_proj("attn_proj", xc3, wa_in, cos3, sin3, T2, P3)
    o3, lse3 = _attn_fwd(P3, D)
    o_att, L_att = _attn_mix(o3, lse3, sel)
    wa_out, w_up0, w_down0 = _gather_wait("gather_b_wait", st_b, L_att)
    x1, xm1, xh1, r1 = _mm_res_ln("attn_out_ln", o_att, wa_out, xs, ln_mix_g[0:1], ln_mix_b[0:1], TH, D)
    a0 = _mlp_up("mlp0_up", xm1, w_up0, T2, TF, D)
    x2, xm2, xh2, r2 = _mm_res_ln("mlp0_down_ln", a0, w_down0, x1, ln_ffn_g[0:1], ln_ffn_b[0:1], TH, F)

    wh_in, wh_out, norm_g, w_up1, w_down1 = _gather_wait("gather_c_wait", st_c, r2)
    P1 = _plain_mm("hgrn_proj", xm2, wh_in, "nn", F32, T1, _pick(3 * D, (1024, 768, 512)), D)
    o_h, n_h, states = _hgrn_fwd(P1, lb_logits, norm_g, TB)
    x3, xm3, xh3, r3 = _mm_res_ln("hgrn_out_ln", n_h, wh_out, x2, ln_mix_g[1:2], ln_mix_b[1:2], TH, D)
    a1 = _mlp_up("mlp1_up", xm3, w_up1, T2, TF, D)
    x4, _, xh4, r4 = _mm_res_ln("mlp1_down_ln", a1, w_down1, x3, ln_ffn_g[1:2], ln_ffn_b[1:2], TH, F)

    wgrad = lambda name, a, dy, tm, tn: _plain_mm(name, a, dy, "tn", MXU_DTYPE, tm, tn, T1)
    sq, du4, dum4, dg_ffn1, db_ffn1 = _loss_ln_bwd(x4, tgt, xh4, r4, ln_ffn_g[1:2], TH)
    dh1 = _mlp_down_bwd("mlp1_down_bwd", dum4, w_down1, a1, T2, TF, D)
    g_down1 = wgrad("g_down1", a1, dum4, TW, D)
    g_up1 = wgrad("g_up1", xm3, dh1, D, TW)
    sc_1, tok = _scatter_start("scatter_1", [g_down1, g_up1], [0, 1])
    du3, dum3, dg_mix1, db_mix1 = _mm_nt_res_ln_bwd("mlp1_up_bwd", dh1, w_up1, du4, xh3, r3, ln_mix_g[1:2], TH, F, tok)
    dn = _plain_mm("hgrn_out_bwd", dum3, wh_out, "nt", F32, T1, D, D)
    g_hout = wgrad("g_hgrn_out", n_h, dum3, D, D)
    dP1, dg_norm, dlb = _hgrn_bwd(P1, o_h, states, dn, lb_logits, norm_g, TB)
    g_hin = wgrad("g_hgrn_in", xm2, dP1, D, TG)
    d_lb_logits = _lb_logits_grad(dlb, lb_logits)
    sc_2, tok = _scatter_start("scatter_2", [g_hout, g_hin], [0, 1])

    du2, dum2, dg_ffn0, db_ffn0 = _mm_nt_res_ln_bwd("hgrn_in_bwd", dP1, wh_in, du3, xh2, r2, ln_ffn_g[0:1], TH, 3 * D, tok)
    dh0 = _mlp_down_bwd("mlp0_down_bwd", dum2, w_down0, a0, T2, TF, D)
    g_down0 = wgrad("g_down0", a0, dum2, TW, D)
    g_up0 = wgrad("g_up0", xm1, dh0, D, TW)
    sc_3, tok = _scatter_start("scatter_3", [g_down0, g_up0], [0, 1])
    du1, dum1, dg_mix0, db_mix0 = _mm_nt_res_ln_bwd("mlp0_up_bwd", dh0, w_up0, du2, xh1, r1, ln_mix_g[0:1], TH, F, tok)
    do, delta = _attn_out_bwd(dum1, wa_out, o_att, sel_t, TH, D)
    g_aout = wgrad("g_attn_out", o_att, dum1, D, D)
    sc_5, tok = _scatter_start("scatter_5", [g_aout], [0])
    dP3 = _attn_bwd(P3, _stack_classes("do_classes", do, MXU_DTYPE), _stack_classes("lse_classes", L_att, F32),
                    _stack_classes("delta_classes", delta, F32), cos3, sin3, D, tok)
    small = jnp.concatenate([d_lb_logits, dg_mix0, dg_mix1, db_mix0, db_mix1, dg_ffn0, dg_ffn1, db_ffn0, db_ffn1,
                             dg_norm, sq, jnp.zeros((4, D), F32)], axis=0)
    small = _all_reduce_small(small, dP3)
    loss = 0.5 * jnp.sum(small[11]) / D
    grp = lambda j: j // (3 * D // TG)
    g_ain = _matmul("g_attn_in", xc3, dP3, "tn", D, TG, T1, [(_sds((D, 9 * D), MXU_DTYPE), _ij_spec(D, TG))], _store_epilogue,
                    a_map=lambda i, j, k: (k + grp(j) * (S // T1), i),
                    b_map=lambda i, j, k: (k + grp(j) * (S // T1), j % (3 * D // TG)), mnk=(D, 9 * D, S), dep=small)[0]
    sc_4, tok = _scatter_start("scatter_4", [g_ain], [1])
    dxc3 = _matmul("attn_in_bwd", dP3, wa_in, "nt", T1, D, 3 * D, [(_sds((3 * S, D), F32), _ij_spec(T1, D))], _store_epilogue,
                   b_map=lambda i, j, k: (j, k + i // (S // T1)), mnk=(3 * S, D, 3 * D), dep=tok)[0]
    grad_x = _input_grad(du1, dxc3)

    def reduced(name, state, *after):
        gs, lands = _scatter_wait(name + "_wait", state, *after)
        return [_reduce_join(f"{name}_reduce_{i}", l, g, ax) for i, (l, g, ax) in enumerate(zip(lands, gs, state[4]))]

    r_down1, r_up1 = reduced("scatter_1", sc_1, grad_x)
    r_hout, r_hin = reduced("scatter_2", sc_2, r_up1)
    r_down0, r_up0 = reduced("scatter_3", sc_3, r_hin)
    (r_aout,) = reduced("scatter_5", sc_5, r_up0)

    my_chip = 2 * lax.axis_index("x") + lax.axis_index("y")
    nsh = hgrn_norm_g.shape[1]
    g_norm = lax.dynamic_slice(small[10:11], (0, my_chip * nsh), (1, nsh))

    grads, upd = {}, {}

    def update(nm, w, gs, m, v):
        upd[nm] = _adamw("adamw_" + nm, w, gs, m, v)
        grads[nm] = upd[nm][3]

    update("hgrn_w_in", hgrn_w_in, [r_hin], m_hgrn_w_in, v_hgrn_w_in)
    update("hgrn_w_out", hgrn_w_out, [r_hout], m_hgrn_w_out, v_hgrn_w_out)
    update("ffn_w_up", ffn_w_up, [r_up0, r_up1], m_ffn_w_up, v_ffn_w_up)
    update("ffn_w_down", ffn_w_down, [r_down0, r_down1], m_ffn_w_down, v_ffn_w_down)
    update("attn_w_out", attn_w_out, [r_aout], m_attn_w_out, v_attn_w_out)
    update("hgrn_norm_g", hgrn_norm_g, [g_norm], m_hgrn_norm_g, v_hgrn_norm_g)
    cat = lambda ts: jnp.concatenate(ts, axis=0)
    small_w = cat([lb_logits, ln_mix_g, ln_mix_b, ln_ffn_g, ln_ffn_b])
    small_m = cat([m_lb_logits, m_ln_mix_g, m_ln_mix_b, m_ln_ffn_g, m_ln_ffn_b])
    small_v = cat([v_lb_logits, v_ln_mix_g, v_ln_mix_b, v_ln_ffn_g, v_ln_ffn_b])
    small_upd = _adamw("adamw_small", small_w, [small[0:10]], small_m, small_v)
    for i, nm in enumerate(["lb_logits", "ln_mix_g", "ln_mix_b", "ln_ffn_g", "ln_ffn_b"]):
        grads[nm] = small[2 * i:2 * i + 2]
        upd[nm] = tuple(t[2 * i:2 * i + 2] for t in small_upd)
    done = [upd[k][2] for k in ("hgrn_w_in", "hgrn_w_out", "ffn_w_up", "ffn_w_down", "attn_w_out", "hgrn_norm_g")]
    (r_ain,) = reduced("scatter_4", sc_4, small_upd[2], *done)
    update("attn_w_in", attn_w_in, [r_ain], m_attn_w_in, v_attn_w_in)

    order = ["attn_w_in", "attn_w_out", "hgrn_w_in", "hgrn_w_out", "hgrn_norm_g", "lb_logits", "ln_mix_g", "ln_mix_b",
             "ln_ffn_g", "ln_ffn_b", "ffn_w_up", "ffn_w_down"]
    return (loss, grad_x[None], *[grads[k] for k in order], *[upd[k][0] for k in order],
            *[upd[k][1] for k in order], *[upd[k][2] for k in order])
```

```python
import math

import jax
import jax.numpy as jnp
from jax import lax
from jax.experimental import pallas as pl
from jax.experimental.pallas import tpu as pltpu

F32 = jnp.float32
BF16 = jnp.bfloat16
MXU_DTYPE = BF16

HEAD_DIM = 64
ATTN_BLK = 128
DILATIONS = (1, 4, 16)
ROPE_THETA = 10000.0
HGRN_DK = 128
HGRN_CHUNK = 64
DEPTH = 2
LN_EPS = 1e-5
RMS_EPS = 1e-6
ALPHA = (2 * DEPTH) ** 0.25
ADAM_LR, ADAM_B1, ADAM_B2, ADAM_EPS, ADAM_WD, ADAM_STEP = 0.001, 0.9, 0.999, 1e-08, 0.01, 10

LANES = 128
VMEM_LIMIT = 56 * 1024 * 1024
NEG = -1e30
MESH = pl.DeviceIdType.MESH


def _cparams(sem=None):
    return pltpu.CompilerParams(dimension_semantics=sem, vmem_limit_bytes=VMEM_LIMIT)


def _sds(shape, dtype):
    return jax.ShapeDtypeStruct(tuple(shape), dtype)


def _dg(a, b, ca, cb):
    return lax.dot_general(a, b, (((ca,), (cb,)), ((), ())), preferred_element_type=F32)


def _nn(a, b):
    return _dg(a, b, 1, 0)


def _nt(a, b):
    return _dg(a, b, 1, 1)


def _tn(a, b):
    return _dg(a, b, 0, 0)


def _split3(a):
    hi = a.astype(BF16)
    r = a - hi.astype(F32)
    mid = r.astype(BF16)
    lo = (r - mid.astype(F32)).astype(BF16)
    return hi, mid, lo


def _exact_nn(a, sel):
    hi, mid, lo = _split3(a)
    return _nn(hi, sel) + _nn(mid, sel) + _nn(lo, sel)


def _pick(n, prefs):
    for p in prefs:
        if n % p == 0:
            return p
    return n


def _matmul(name, a, b, form, tm, tn, tk, outs, epilogue, extras=(), a_map=None, b_map=None, mnk=None, dep=None,
            sem=("parallel", "parallel", "arbitrary"), split=None, alias_dep=False):
    if form == "nn":
        (M, K), N = a.shape, b.shape[1]
        a_spec = pl.BlockSpec((tm, tk), a_map or (lambda i, j, k: (i, k)))
        b_spec = pl.BlockSpec((tk, tn), b_map or (lambda i, j, k: (k, j)))
        ca, cb = 1, 0
    elif form == "nt":
        (M, K), N = a.shape, b.shape[0]
        a_spec = pl.BlockSpec((tm, tk), a_map or (lambda i, j, k: (i, k)))
        b_spec = pl.BlockSpec((tn, tk), b_map or (lambda i, j, k: (j, k)))
        ca, cb = 1, 1
    else:
        (K, M), N = a.shape, b.shape[1]
        a_spec = pl.BlockSpec((tk, tm), a_map or (lambda i, j, k: (k, i)))
        b_spec = pl.BlockSpec((tk, tn), b_map or (lambda i, j, k: (k, j)))
        ca, cb = 0, 0
    if mnk is not None:
        M, N, K = mnk
    assert M % tm == 0 and N % tn == 0 and K % tk == 0, (name, M, N, K, tm, tn, tk)
    nk = K // tk
    ne, no = len(extras), len(outs)
    deps = [] if dep is None else [dep]
    nd = len(deps)

    def body(a_ref, b_ref, *rest):
        extra_refs, out_refs = rest[:ne], rest[ne + nd:ne + nd + no]
        j = pl.program_id(1)
        if split is not None:
            kind, n = split
            assert nk == 1 and form != "tn"
            tiled = [t for _, _, *t in list(extras) + list(outs)]
            refs = list(extra_refs) + list(out_refs)
            for ci in range(n):
                if kind == "cols":
                    cs = slice(ci * (tn // n), (ci + 1) * (tn // n))
                    part = _dg(a_ref[...].astype(MXU_DTYPE), (b_ref[:, cs] if form == "nn" else b_ref[cs, :]).astype(MXU_DTYPE), ca, cb)
                    view = [r.at[:, cs] if t else r for r, t in zip(refs, tiled)]
                else:
                    rs = slice(ci * (tm // n), (ci + 1) * (tm // n))
                    part = _dg(a_ref[rs, :].astype(MXU_DTYPE), b_ref[...].astype(MXU_DTYPE), ca, cb)
                    view = [r.at[rs, :] if t else r for r, t in zip(refs, tiled)]
                epilogue(part, view[:ne], view[ne:], j, ci)
            return
        part = _dg(a_ref[...].astype(MXU_DTYPE), b_ref[...].astype(MXU_DTYPE), ca, cb)
        if nk == 1:
            epilogue(part, extra_refs, out_refs, j, 0)
            return
        acc_ref = rest[-1]
        k = pl.program_id(2)

        @pl.when(k == 0)
        def _():
            acc_ref[...] = part

        @pl.when(k > 0)
        def _():
            acc_ref[...] += part

        @pl.when(k == nk - 1)
        def _():
            epilogue(acc_ref[...], extra_refs, out_refs, j, 0)

    res = pl.pallas_call(
        body,
        name=name,
        grid=(M // tm, N // tn, nk),
        in_specs=[a_spec, b_spec] + [s for _, s, *_ in extras] + [pl.BlockSpec(memory_space=pl.ANY)] * nd,
        out_specs=[s for _, s, *_ in outs],
        out_shape=[o for o, *_ in outs],
        scratch_shapes=[pltpu.VMEM((tm, tn), F32)] if nk > 1 else [],
        input_output_aliases={2 + ne: 0} if alias_dep else {},
        compiler_params=_cparams(sem),
    )(a, b, *[e for e, *_ in extras], *deps)
    return res


def _ij_spec(tm, tn):
    return pl.BlockSpec((tm, tn), lambda i, j, k: (i, j))


def _store_epilogue(acc, extra_refs, out_refs, j, ci):
    out_refs[0][...] = acc.astype(out_refs[0].dtype)


def _plain_mm(name, a, b, form, out_dtype, tm, tn, tk):
    M = a.shape[1] if form == "tn" else a.shape[0]
    N = b.shape[0] if form == "nt" else b.shape[1]
    return _matmul(name, a, b, form, tm, tn, tk, [(_sds((M, N), out_dtype), _ij_spec(tm, tn))], _store_epilogue)[0]


def _class_slabs(S):
    assert DILATIONS[0] == 1
    return [(g, d, r, S // d) for g, d in enumerate(DILATIONS) if d > 1 for r in range(d)]


def _stack_classes(name, t, out_dtype):
    S, W = t.shape

    def body(x_ref, o_ref):
        o_ref[0:S, :] = x_ref[...].astype(out_dtype)
        for g, d, r, n in _class_slabs(S):
            o_ref[g * S + r * n:g * S + (r + 1) * n, :] = x_ref[pl.ds(r, n, stride=d), :].astype(out_dtype)

    return pl.pallas_call(
        body,
        name=name,
        grid=(W // LANES,),
        in_specs=[pl.BlockSpec((S, LANES), lambda j: (0, j))],
        out_specs=pl.BlockSpec((3 * S, LANES), lambda j: (0, j)),
        out_shape=_sds((3 * S, W), out_dtype),
        compiler_params=_cparams(("parallel",)),
    )(t)


def _rope_tables(seq):
    half = HEAD_DIM // 2
    inv = ROPE_THETA ** (-jnp.arange(half, dtype=F32) * (2.0 / HEAD_DIM))
    inv = jnp.tile(inv, LANES // half)
    pos = []
    for d in DILATIONS:
        row = jnp.arange(seq)
        pos.append((row % (seq // d)) * d + row // (seq // d))
    ang = jnp.concatenate(pos).astype(F32)[:, None] * inv[None, :]
    first = (jnp.arange(LANES) % HEAD_DIM) < half
    sin = jnp.sin(ang)
    return jnp.cos(ang), jnp.where(first[None, :], -sin, sin)


def _partner(x):
    half = HEAD_DIM // 2
    lane = lax.broadcasted_iota(jnp.int32, x.shape, 1)
    first = (lane % HEAD_DIM) < half
    return jnp.where(first, pltpu.roll(x, LANES - half, 1), pltpu.roll(x, half, 1))


def _attn_proj(name, x3, w, cos3, sin3, tm, prev, dep=None):
    S3, D = x3.shape
    S = S3 // 3
    tn = 3 * D // 4
    nrow = S // tm
    local = prev is None

    def tile(j):
        q = 2 * lax.axis_index("x") + lax.axis_index("y")
        c0 = 3 * q + j if local else j + 3 * (j >= 3 * q).astype(jnp.int32)
        return c0, c0 // 4, c0 % 4

    def epilogue(acc, extra_refs, out_refs, j, ci):
        cos_ref, sin_ref = extra_refs
        o_ref = out_refs[0]
        _, _, place = tile(j)
        width = acc.shape[1]
        assert D % width == 0
        is_rot = (place * tn + ci * width) // D < 2
        c = jnp.where(is_rot, cos_ref[...], 1.0)
        s = jnp.where(is_rot, sin_ref[...], 0.0)
        for t in range(width // LANES):
            xs = acc[:, t * LANES:(t + 1) * LANES]
            o_ref[:, t * LANES:(t + 1) * LANES] = (xs * c + _partner(xs) * s).astype(o_ref.dtype)

    rows = lambda i, j: tile(j)[1] * nrow + i
    tab = pl.BlockSpec((tm, LANES), lambda i, j, k: (rows(i, j), 0))
    out = pl.BlockSpec((tm, tn), lambda i, j, k: (rows(i, j), tile(j)[2]))
    ntiles = 3 if local else 9
    return _matmul(name, x3, w, "nn", tm, tn, D, [(_sds((S3, 3 * D), MXU_DTYPE), out, True)], epilogue,
                   extras=[(cos3, tab), (sin3, tab)], a_map=lambda i, j, k: (rows(i, j), k),
                   b_map=lambda i, j, k: (k, j if local else tile(j)[0]), mnk=(nrow * tm, ntiles * tn, D),
                   dep=dep if local else prev, alias_dep=not local, split=("cols", 3))[0]


def _head_sel(d_model):
    h = jnp.arange(LANES)[:, None]
    l = jnp.arange(d_model)[None, :]
    return (l // HEAD_DIM == h).astype(BF16)


def _class_edges(b, nblk):
    g = b // nblk
    per_class = jnp.where(g == 0, nblk // DILATIONS[0], jnp.where(g == 1, nblk // DILATIONS[1], nblk // DILATIONS[2]))
    pos = (b % nblk) % per_class
    return pos != 0, pos != per_class - 1


def _two_heads(t, top):
    zero = jnp.zeros_like(t)
    return jnp.concatenate([jnp.where(top, t, zero), jnp.where(top, zero, t)], axis=0)


def _band_mask(has_prev):
    B = ATTN_BLK
    row = lax.broadcasted_iota(jnp.int32, (2 * B, 2 * B), 0) % B
    col = lax.broadcasted_iota(jnp.int32, (2 * B, 2 * B), 1)
    in_prev = jnp.logical_and(jnp.logical_and(col < B, col >= row), has_prev)
    in_own = jnp.logical_and(col >= B, col - B <= row)
    return jnp.logical_or(in_prev, in_own)


def _attn_fwd(P3, D):
    S3 = P3.shape[0]
    B = ATTN_BLK
    nblk = S3 // 3 // B
    npairs = D // LANES
    scale = HEAD_DIM ** -0.5

    def body(q_ref, kc_ref, vc_ref, kp_ref, vp_ref, o_ref, lse_ref):
        has_prev, _ = _class_edges(pl.program_id(0), nblk)
        ok = _band_mask(has_prev)
        lane = lax.broadcasted_iota(jnp.int32, (B, LANES), 1)
        top = lane < HEAD_DIM
        lse_acc = jnp.zeros((B, LANES), F32)
        for j in range(npairs):
            sl = slice(j * LANES, (j + 1) * LANES)
            Q = _two_heads(q_ref[:, sl] * scale, top)
            K2 = jnp.concatenate([kp_ref[:, sl], kc_ref[:, sl]], axis=0)
            V2 = jnp.concatenate([vp_ref[:, sl], vc_ref[:, sl]], axis=0)
            s = jnp.where(ok, _nt(Q, K2), NEG)
            m = jnp.max(s, axis=1, keepdims=True)
            p = jnp.exp(s - m)
            l = jnp.sum(p, axis=1, keepdims=True)
            o = _nn((p * (1.0 / l)).astype(MXU_DTYPE), V2)
            o_ref[:, sl] = jnp.where(top, o[:B], o[B:])
            lse = m + jnp.log(l)
            lse_acc = jnp.where(lane == 2 * j, lse[:B], jnp.where(lane == 2 * j + 1, lse[B:], lse_acc))
        lse_ref[...] = lse_acc

    blk = lambda part, prev: pl.BlockSpec(
        (B, D), (lambda b: (jnp.maximum(b - 1, 0), part)) if prev else (lambda b: (b, part)))
    return pl.pallas_call(
        body,
        name="attn_fwd",
        grid=(3 * nblk,),
        in_specs=[blk(0, False), blk(1, False), blk(2, False), blk(1, True), blk(2, True)],
        out_specs=[pl.BlockSpec((B, D), lambda b: (b, 0)), pl.BlockSpec((B, LANES), lambda b: (b, 0))],
        out_shape=[_sds((S3, D), F32), _sds((S3, LANES), F32)],
        compiler_params=_cparams(("parallel",)),
    )(P3, P3, P3, P3, P3)


def _attn_mix(o3, lse3, sel):
    S3, D = o3.shape
    S = S3 // 3

    def body(o3_ref, lse_ref, sel_ref, o_ref, L_ref, w_ref):
        @pl.when(pl.program_id(0) == 0)
        def _():
            w_ref[0] = lse_ref[0:S, :]
            for g, d, r, n in _class_slabs(S):
                w_ref[g, pl.ds(r, n, stride=d), :] = lse_ref[g * S + r * n:g * S + (r + 1) * n, :]
            a, b, c = w_ref[0], w_ref[1], w_ref[2]
            m = jnp.maximum(jnp.maximum(a, b), c)
            L = m + jnp.log(jnp.exp(a - m) + jnp.exp(b - m) + jnp.exp(c - m))
            L_ref[...] = L
            w_ref[0] = jnp.exp(a - L)
            w_ref[1] = jnp.exp(b - L)
            w_ref[2] = jnp.exp(c - L)

        s = sel_ref[...]
        o_ref[...] = _exact_nn(w_ref[0], s) * o3_ref[0:S, :]
        for g, d, r, n in _class_slabs(S):
            rows = pl.ds(r, n, stride=d)
            o_ref[rows, :] += _exact_nn(w_ref[g, rows, :], s) * o3_ref[g * S + r * n:g * S + (r + 1) * n, :]

    return pl.pallas_call(
        body,
        name="attn_mix",
        grid=(D // LANES,),
        in_specs=[pl.BlockSpec((S3, LANES), lambda j: (0, j)), pl.BlockSpec((S3, LANES), lambda j: (0, 0)),
                  pl.BlockSpec((LANES, LANES), lambda j: (0, j))],
        out_specs=[pl.BlockSpec((S, LANES), lambda j: (0, j)), pl.BlockSpec((S, LANES), lambda j: (0, 0))],
        out_shape=[_sds((S, D), F32), _sds((S, LANES), F32)],
        scratch_shapes=[pltpu.VMEM((3, S, LANES), F32)],
        compiler_params=_cparams(("arbitrary",)),
    )(o3, lse3, sel)


def _attn_bwd(P3, do3, L3, delta3, cos3, sin3, D, dep):
    S3 = P3.shape[0]
    B = ATTN_BLK
    nblk = S3 // 3 // B
    npairs = D // LANES
    scale = HEAD_DIM ** -0.5

    def body(c_ref, kp_ref, vp_ref, qn_ref, doc_ref, don_ref, Lc_ref, Ln_ref, dc_ref, dn_ref, cos_ref, sin_ref, dep_ref, out_ref):
        has_prev, has_next = _class_edges(pl.program_id(0), nblk)
        ok = _band_mask(has_prev)
        row = lax.broadcasted_iota(jnp.int32, (2 * B, B), 0) % B
        col = lax.broadcasted_iota(jnp.int32, (2 * B, B), 1)
        ok_n = jnp.logical_and(col >= row, has_next)
        lane = lax.broadcasted_iota(jnp.int32, (B, LANES), 1)
        top = lane < HEAD_DIM
        cos_t = cos_ref[...]
        sin_inv = -sin_ref[...]
        Lc_all, Ln_all, dc_all, dn_all = Lc_ref[...], Ln_ref[...], dc_ref[...], dn_ref[...]
        pair_col = lambda t, j: jnp.concatenate([t[:, 2 * j:2 * j + 1], t[:, 2 * j + 1:2 * j + 2]], axis=0)
        for j in range(npairs):
            sl = lambda part: slice(part * D + j * LANES, part * D + (j + 1) * LANES)
            pj = slice(j * LANES, (j + 1) * LANES)
            kc2, vc2 = c_ref[:, sl(1)], c_ref[:, sl(2)]
            K2 = jnp.concatenate([kp_ref[:, pj], kc2], axis=0)
            V2 = jnp.concatenate([vp_ref[:, pj], vc2], axis=0)
            Qc = _two_heads(c_ref[:, sl(0)] * scale, top)
            Qn = _two_heads(qn_ref[:, pj] * scale, top)
            DOc = _two_heads(doc_ref[:, pj].astype(MXU_DTYPE), top)
            DOn = _two_heads(don_ref[:, pj].astype(MXU_DTYPE), top)
            P_c = jnp.where(ok, jnp.exp(_nt(Qc, K2) - pair_col(Lc_all, j)), 0.0)
            dS_c = P_c * (_nt(DOc, V2) - pair_col(dc_all, j))
            P_n = jnp.where(ok_n, jnp.exp(_nt(Qn, kc2) - pair_col(Ln_all, j)), 0.0)
            dS_n = P_n * (_nt(DOn, vc2) - pair_col(dn_all, j))
            dq = _nn(dS_c.astype(MXU_DTYPE), K2)
            dq2 = jnp.where(top, dq[:B], dq[B:]) * scale
            Qk = jnp.concatenate([Qc, Qn], axis=0)
            DOk = jnp.concatenate([DOc, DOn], axis=0)
            dk2 = _tn(jnp.concatenate([dS_c[:, B:], dS_n], axis=0).astype(MXU_DTYPE), Qk)
            dv2 = _tn(jnp.concatenate([P_c[:, B:], P_n], axis=0).astype(MXU_DTYPE), DOk)
            out_ref[:, sl(0)] = (dq2 * cos_t + _partner(dq2) * sin_inv).astype(out_ref.dtype)
            out_ref[:, sl(1)] = (dk2 * cos_t + _partner(dk2) * sin_inv).astype(out_ref.dtype)
            out_ref[:, sl(2)] = dv2.astype(out_ref.dtype)

    cur = lambda b: b
    prv = lambda b: jnp.maximum(b - 1, 0)
    nxt = lambda b: jnp.minimum(b + 1, 3 * nblk - 1)
    spec = lambda w, f, part=0: pl.BlockSpec((B, w), lambda b: (f(b), part))
    return pl.pallas_call(
        body,
        name="attn_bwd",
        grid=(3 * nblk,),
        in_specs=[spec(3 * D, cur), spec(D, prv, 1), spec(D, prv, 2), spec(D, nxt, 0), spec(D, cur), spec(D, nxt),
                  spec(LANES, cur), spec(LANES, nxt), spec(LANES, cur), spec(LANES, nxt), spec(LANES, cur), spec(LANES, cur),
                  pl.BlockSpec(memory_space=pl.ANY)],
        out_specs=spec(3 * D, cur),
        out_shape=_sds((S3, 3 * D), MXU_DTYPE),
        compiler_params=_cparams(("parallel",)),
    )(P3, P3, P3, P3, do3, do3, L3, L3, delta3, delta3, cos3, sin3, dep)


def _input_grad(du, dx3):
    S, D = du.shape

    def body(du_ref, dx_ref, o_ref):
        o_ref[...] = ALPHA * du_ref[...] + dx_ref[0:S, :]
        for g, d, r, n in _class_slabs(S):
            o_ref[pl.ds(r, n, stride=d), :] += dx_ref[g * S + r * n:g * S + (r + 1) * n, :]

    return pl.pallas_call(
        body,
        name="input_grad",
        grid=(D // LANES,),
        in_specs=[pl.BlockSpec((S, LANES), lambda j: (0, j)), pl.BlockSpec((3 * S, LANES), lambda j: (0, j))],
        out_specs=pl.BlockSpec((S, LANES), lambda j: (0, j)),
        out_shape=_sds((S, D), F32),
        compiler_params=_cparams(("parallel",)),
    )(du, dx3)


def _chunk_causal(tb):
    r = lax.broadcasted_iota(jnp.int32, (tb, tb), 0)
    c = lax.broadcasted_iota(jnp.int32, (tb, tb), 1)
    return jnp.logical_and((r // HGRN_CHUNK) == (c // HGRN_CHUNK), r >= c)


def _chunk_sums(a, lower):
    C = HGRN_CHUNK
    r = lax.broadcasted_iota(jnp.int32, (C, C), 0)
    c = lax.broadcasted_iota(jnp.int32, (C, C), 1)
    tri = ((r >= c) if lower else (r <= c)).astype(BF16)
    parts = _split3(a)
    out = []
    for ci in range(a.shape[0] // C):
        rows = slice(ci * C, (ci + 1) * C)
        out.append(_nn(tri, parts[0][rows]) + _nn(tri, parts[1][rows]) + _nn(tri, parts[2][rows]))
    return jnp.concatenate(out, axis=0)


def _chunk_last(b):
    C = HGRN_CHUNK
    return jnp.concatenate([jnp.broadcast_to(b[(ci + 1) * C - 1:(ci + 1) * C, :], (C, b.shape[1]))
                            for ci in range(b.shape[0] // C)], axis=0)


def _lower_bound(lb_ref):
    l0, l1 = lb_ref[0:1, :], lb_ref[1:2, :]
    m = jnp.maximum(l0, l1)
    e0, e1 = jnp.exp(l0 - m), jnp.exp(l1 - m)
    return e1 / (e0 + e1)


def _hgrn_gates(q_raw, z, lb):
    sg = 1.0 / (1.0 + jnp.exp(-z))
    sn = 1.0 / (1.0 + jnp.exp(z))
    f = lb + (1.0 - lb) * sg
    key = (1.0 - lb) * sn
    sq = 1.0 / (1.0 + jnp.exp(-q_raw))
    return sg, sn, f, key, sq


def _hgrn_fwd(P1, lb_logits, norm_g, tb):
    S = P1.shape[0]
    D = P1.shape[1] // 3
    K = HGRN_DK
    H = D // K
    HP = H
    C = HGRN_CHUNK
    cpb = tb // C
    nt = S // tb

    def body(q_ref, f_ref, i_ref, lb_ref, g_ref, o_ref, n_ref, st_ref, state):
        t = pl.program_id(1)

        @pl.when(t == 0)
        def _():
            state[...] = jnp.zeros_like(state)

        lb_all = _lower_bound(lb_ref)
        low = _chunk_causal(tb)
        for hh in range(HP):
            lanes = slice(hh * K, (hh + 1) * K)
            q_raw, z, v = q_ref[:, lanes], f_ref[:, lanes], i_ref[:, lanes]
            sg, sn, f, key, sq = _hgrn_gates(q_raw, z, lb_all[:, lanes])
            b = _chunk_sums(jnp.log(f), lower=True)
            qd = (q_raw * sq * jnp.exp(b)).astype(MXU_DTYPE)
            kd = (key * jnp.exp(-b)).astype(MXU_DTYPE)
            kb = (key * jnp.exp(_chunk_last(b) - b)).astype(MXU_DTYPE)
            vm = v.astype(MXU_DTYPE)
            a = jnp.where(low, _nt(qd, kd), 0.0).astype(MXU_DTYPE)
            o_intra = _nn(a, vm)
            st = state[hh]
            outs = []
            for ci in range(cpb):
                rows = slice(ci * C, (ci + 1) * C)
                st_ref[hh, ci] = st
                outs.append(o_intra[rows] + _nt(qd[rows], st.astype(MXU_DTYPE)))
                st = st * jnp.exp(b[(ci + 1) * C - 1:(ci + 1) * C, :]) + _tn(vm[rows], kb[rows])
            state[hh] = st
            o = jnp.concatenate(outs, axis=0)
            o_ref[:, lanes] = o
            rs = lax.rsqrt(jnp.mean(o * o, axis=1, keepdims=True) + RMS_EPS)
            n_ref[:, lanes] = o * rs * g_ref[:, lanes]

    tok = lambda part: pl.BlockSpec((tb, HP * K), lambda h, t: (t, part * (H // HP) + h))
    vec = lambda rows: pl.BlockSpec((rows, HP * K), lambda h, t: (0, h))
    return pl.pallas_call(
        body,
        name="hgrn_fwd",
        grid=(H // HP, nt),
        in_specs=[tok(0), tok(1), tok(2), vec(2), vec(1)],
        out_specs=[tok(0), tok(0), pl.BlockSpec((HP, cpb, K, K), lambda h, t: (h, t, 0, 0))],
        out_shape=[_sds((S, D), F32), _sds((S, D), F32), _sds((H, S // C, K, K), F32)],
        scratch_shapes=[pltpu.VMEM((HP, K, K), F32)],
        compiler_params=_cparams(("parallel", "arbitrary")),
    )(P1, P1, P1, lb_logits, norm_g)


def _hgrn_bwd(P1, o_pre, states, dn, lb_logits, norm_g, tb):
    S = P1.shape[0]
    D = P1.shape[1] // 3
    K = HGRN_DK
    H = D // K
    HP = H
    C = HGRN_CHUNK
    cpb = tb // C
    nt = S // tb

    def body(q_ref, f_ref, i_ref, o_ref, st_ref, dn_ref, lb_ref, g_ref, d_ref, dg_ref, dlb_ref, dstate):
        t = pl.program_id(1)

        @pl.when(t == 0)
        def _():
            dstate[...] = jnp.zeros_like(dstate)
            dg_ref[...] = jnp.zeros_like(dg_ref)
            dlb_ref[...] = jnp.zeros_like(dlb_ref)

        lb_all = _lower_bound(lb_ref)
        low = _chunk_causal(tb)
        for hh in range(HP):
            lanes = slice(hh * K, (hh + 1) * K)
            lb = lb_all[:, lanes]
            gn = g_ref[:, lanes]
            q_raw, z, v = q_ref[:, lanes], f_ref[:, lanes], i_ref[:, lanes]
            sg, sn, f, key, sq = _hgrn_gates(q_raw, z, lb)
            b = _chunk_sums(jnp.log(f), lower=True)
            e_pos, e_neg, e_rel = jnp.exp(b), jnp.exp(-b), jnp.exp(_chunk_last(b) - b)
            qd_f, kd_f, kb_f = q_raw * sq * e_pos, key * e_neg, key * e_rel
            qd, kd, kb = qd_f.astype(MXU_DTYPE), kd_f.astype(MXU_DTYPE), kb_f.astype(MXU_DTYPE)
            vm = v.astype(MXU_DTYPE)
            a = jnp.where(low, _nt(qd, kd), 0.0).astype(MXU_DTYPE)
            o = o_ref[:, lanes]
            dnn = dn_ref[:, lanes]
            rs = lax.rsqrt(jnp.mean(o * o, axis=1, keepdims=True) + RMS_EPS)
            dg_ref[:, lanes] += jnp.sum(dnn * o * rs, axis=0, keepdims=True)
            tg = dnn * gn
            dom = (rs * tg - o * (rs * rs * rs) * jnp.mean(tg * o, axis=1, keepdims=True)).astype(MXU_DTYPE)
            da = jnp.where(low, _nt(dom, vm), 0.0).astype(MXU_DTYPE)
            dv = _tn(a, dom)
            dqd = _nn(da, kd)
            dkd = _tn(da, qd)
            dst = dstate[hh]
            dv_s, dqd_s, dkb_s, dbl_s = [None] * cpb, [None] * cpb, [None] * cpb, [None] * cpb
            for ci in reversed(range(cpb)):
                rows = slice(ci * C, (ci + 1) * C)
                st = st_ref[hh, ci]
                dstm = dst.astype(MXU_DTYPE)
                dec = jnp.exp(b[(ci + 1) * C - 1:(ci + 1) * C, :])
                dv_s[ci] = _nt(kb[rows], dstm)
                dkb_s[ci] = _nn(vm[rows], dstm)
                dqd_s[ci] = _nn(dom[rows], st.astype(MXU_DTYPE))
                db_last = jnp.sum(dkb_s[ci] * kb_f[rows], axis=0, keepdims=True) + jnp.sum(dst * st, axis=0, keepdims=True) * dec
                dbl_s[ci] = jnp.broadcast_to(db_last, (C, K))
                dst = dst * dec + _tn(dom[rows], qd[rows])
            dstate[hh] = dst
            dv = dv + jnp.concatenate(dv_s, axis=0)
            dqd = dqd + jnp.concatenate(dqd_s, axis=0)
            dkb = jnp.concatenate(dkb_s, axis=0)
            dkey = dkd * e_neg + dkb * e_rel
            db = dqd * qd_f - dkd * kd_f - dkb * kb_f
            dlogf = _chunk_sums(db, lower=False) + jnp.concatenate(dbl_s, axis=0)
            gz = (1.0 - lb) * sg * sn
            col = lambda part: slice(part * D + hh * K, part * D + (hh + 1) * K)
            d_ref[:, col(0)] = (dqd * e_pos * (sq + q_raw * sq * (1.0 - sq))).astype(d_ref.dtype)
            d_ref[:, col(1)] = (dlogf * gz / f - dkey * gz).astype(d_ref.dtype)
            d_ref[:, col(2)] = dv.astype(d_ref.dtype)
            dlb_ref[:, lanes] += jnp.sum(dlogf * sn / f - dkey * sn, axis=0, keepdims=True)

    rev = lambda t: nt - 1 - t
    tok = lambda part: pl.BlockSpec((tb, HP * K), lambda h, t: (rev(t), part * (H // HP) + h))
    vec = lambda rows: pl.BlockSpec((rows, HP * K), lambda h, t: (0, h))
    outs = pl.pallas_call(
        body,
        name="hgrn_bwd",
        grid=(H // HP, nt),
        in_specs=[tok(0), tok(1), tok(2), tok(0),
                  pl.BlockSpec((HP, cpb, K, K), lambda h, t: (h, rev(t), 0, 0)),
                  tok(0), vec(2), vec(1)],
        out_specs=[pl.BlockSpec((tb, 3 * D), lambda h, t: (rev(t), 0)), vec(1), vec(1)],
        out_shape=[_sds((S, 3 * D), MXU_DTYPE)] + [_sds((1, D), F32)] * 2,
        scratch_shapes=[pltpu.VMEM((HP, K, K), F32)],
        compiler_params=_cparams(("parallel", "arbitrary")),
    )(P1, P1, P1, o_pre, states, dn, lb_logits, norm_g)
    return outs


def _lb_logits_grad(dlb, lb_logits):
    def body(d_ref, l_ref, o_ref):
        s1 = _lower_bound(l_ref)
        d = d_ref[...]
        o_ref[0:1, :] = -(1.0 - s1) * s1 * d
        o_ref[1:2, :] = s1 * (1.0 - s1) * d

    return pl.pallas_call(body, name="lb_logits_grad", out_shape=_sds(lb_logits.shape, F32))(dlb, lb_logits)


def _mm_res_ln(name, a, w_full, res, g, b, tm, tk):
    from_ln = isinstance(res, tuple)
    S, D = (res[0] if from_ln else res).shape

    def epilogue(acc, extra_refs, out_refs, j, ci):
        g_ref, b_ref = extra_refs[:2]
        xm_ref, xhat_ref, rstd_ref = out_refs
        r = extra_refs[2][...] * extra_refs[3][...] + extra_refs[4][...] if from_ln else extra_refs[2][...]
        u = ALPHA * r + acc
        mu = jnp.mean(u, axis=1, keepdims=True)
        cen = u - mu
        rstd = lax.rsqrt(jnp.mean(cen * cen, axis=1, keepdims=True) + LN_EPS)
        xhat = cen * rstd
        xhat_ref[...] = xhat
        xm_ref[...] = (xhat * g_ref[...] + b_ref[...]).astype(xm_ref.dtype)
        rstd_ref[...] = rstd

    row = pl.BlockSpec((tm, D), lambda i, j, k: (i, 0))
    vec = pl.BlockSpec((1, D), lambda i, j, k: (0, 0))
    outs = [(_sds((S, D), MXU_DTYPE), row, True), (_sds((S, D), F32), row, True),
            (_sds((S, 1), F32), pl.BlockSpec((tm, 1), lambda i, j, k: (i, 0)), True)]
    res_extras = [(res[0], row, True), (res[1], vec), (res[2], vec)] if from_ln else [(res, row, True)]
    return _matmul(name, a, w_full, "nn", tm, D, tk, outs, epilogue, extras=[(g, vec), (b, vec)] + res_extras,
                   split=("rows", 2) if tk == a.shape[1] else None)


def _ln_bwd_rows(dy, xh, rstd, g, first, du_ref, dum_ref, dg_ref, db_ref):
    if first is not None:
        @pl.when(first)
        def _():
            dg_ref[...] = jnp.zeros_like(dg_ref)
            db_ref[...] = jnp.zeros_like(db_ref)

    dg_ref[...] += jnp.sum(dy * xh, axis=0, keepdims=True)
    db_ref[...] += jnp.sum(dy, axis=0, keepdims=True)
    dxh = dy * g
    m1 = jnp.mean(dxh, axis=1, keepdims=True)
    m2 = jnp.mean(dxh * xh, axis=1, keepdims=True)
    du = rstd * (dxh - m1 - xh * m2)
    du_ref[...] = du
    dum_ref[...] = du.astype(dum_ref.dtype)


def _loss_ln_bwd(target, xhat, rstd, g, b, tm):
    S, D = xhat.shape

    def body(t_ref, xh_ref, r_ref, g_ref, b_ref, sq_ref, du_ref, dum_ref, dg_ref, db_ref):
        first = pl.program_id(0) == 0

        @pl.when(first)
        def _():
            sq_ref[...] = jnp.zeros_like(sq_ref)

        xh = xh_ref[...]
        e = xh * g_ref[...] + b_ref[...] - t_ref[...]
        sq_ref[...] += jnp.sum(e * e, axis=0, keepdims=True)
        _ln_bwd_rows(e / D, xh, r_ref[...], g_ref[...], first, du_ref, dum_ref, dg_ref, db_ref)

    row = pl.BlockSpec((tm, D), lambda i: (i, 0))
    vec = pl.BlockSpec((1, D), lambda i: (0, 0))
    return pl.pallas_call(
        body,
        name="loss_ln_bwd",
        grid=(S // tm,),
        in_specs=[row, row, pl.BlockSpec((tm, 1), lambda i: (i, 0)), vec, vec],
        out_specs=[vec, row, row, vec, vec],
        out_shape=[_sds((1, D), F32), _sds((S, D), F32), _sds((S, D), MXU_DTYPE), _sds((1, D), F32), _sds((1, D), F32)],
        compiler_params=_cparams(("arbitrary",)),
    )(target, xhat, rstd, g, b)


def _mlp_up(name, x, w_up, tm, tn, tk):
    S = x.shape[0]
    F = w_up.shape[1]

    def epilogue(acc, extra_refs, out_refs, j, ci):
        r = jnp.maximum(acc, 0.0)
        out_refs[0][...] = (r * r).astype(out_refs[0].dtype)

    return _matmul(name, x, w_up, "nn", tm, tn, tk, [(_sds((S, F), MXU_DTYPE), _ij_spec(tm, tn), True)], epilogue,
                   split=("cols", 2))[0]


def _mlp_down_bwd(name, dy, w_down, a, tm, tn, tk):
    S, F = a.shape

    def epilogue(acc, extra_refs, out_refs, j, ci):
        out_refs[0][...] = (acc * (2.0 * jnp.sqrt(extra_refs[0][...].astype(F32)))).astype(out_refs[0].dtype)

    return _matmul(name, dy, w_down, "nt", tm, tn, tk, [(_sds((S, F), MXU_DTYPE), _ij_spec(tm, tn), True)], epilogue,
                   extras=[(a, _ij_spec(tm, tn), True)], split=("cols", 2))[0]


def _mm_nt_res_ln_bwd(name, dy, w, du, xhat, rstd, g, tm, tk, dep):
    S, D = du.shape

    def epilogue(acc, extra_refs, out_refs, j, ci):
        du_ref, xh_ref, r_ref, g_ref = extra_refs
        first = (pl.program_id(0) == 0) if ci == 0 else None
        _ln_bwd_rows(ALPHA * du_ref[...] + acc, xh_ref[...], r_ref[...], g_ref[...], first, *out_refs)

    row = pl.BlockSpec((tm, D), lambda i, j, k: (i, 0))
    vec = pl.BlockSpec((1, D), lambda i, j, k: (0, 0))
    return _matmul(name, dy, w, "nt", tm, D, tk,
                   [(_sds((S, D), F32), row, True), (_sds((S, D), MXU_DTYPE), row, True), (_sds((1, D), F32), vec),
                    (_sds((1, D), F32), vec)], epilogue,
                   extras=[(du, row, True), (xhat, row, True), (rstd, pl.BlockSpec((tm, 1), lambda i, j, k: (i, 0)), True), (g, vec)],
                   dep=dep, sem=("arbitrary", "arbitrary", "arbitrary"), split=("rows", 2))


def _attn_out_bwd(du, w_out, o, sel_t, tm, tk):
    S, D = o.shape

    def epilogue(acc, extra_refs, out_refs, j, ci):
        out_refs[0][...] = acc
        out_refs[1][...] = _exact_nn(acc * extra_refs[0][...], extra_refs[1][...])

    row = pl.BlockSpec((tm, D), lambda i, j, k: (i, 0))
    slim = pl.BlockSpec((tm, LANES), lambda i, j, k: (i, 0))
    return _matmul("attn_out_bwd", du, w_out, "nt", tm, D, tk,
                   [(_sds((S, D), F32), row, True), (_sds((S, LANES), F32), slim, True)], epilogue,
                   extras=[(o, row, True), (sel_t, pl.BlockSpec((D, LANES), lambda i, j, k: (0, 0)))], split=("rows", 2))


def _adamw(name, w, gs, m, v):
    shape = w.shape
    cols = shape[-1]
    rows = math.prod(shape[:-1])
    w2, m2, v2 = (t.reshape(rows, cols) for t in (w, m, v))
    gs2 = [g.reshape(-1, cols) for g in gs]
    ng = len(gs2)
    tr = _pick(rows // ng, (256, 128, 64, 32, 16, 8))
    per = rows // ng // tr
    c1 = 1.0 - ADAM_B1 ** ADAM_STEP
    c2 = 1.0 - ADAM_B2 ** ADAM_STEP

    def body(w_ref, m_ref, v_ref, *rest):
        g_refs, (d_ref, nm_ref, nv_ref), g_out = rest[:ng], rest[ng:ng + 3], rest[ng + 3:]
        gg = g_refs[0][...]
        if ng == 2:
            gg = jnp.where(pl.program_id(0) < per, gg, g_refs[1][...])
        g_out[0][...] = gg
        nm = ADAM_B1 * m_ref[...] + (1.0 - ADAM_B1) * gg
        nv = ADAM_B2 * v_ref[...] + (1.0 - ADAM_B2) * (gg * gg)
        nm_ref[...] = nm
        nv_ref[...] = nv
        d_ref[...] = -ADAM_LR * ((nm / c1) / (jnp.sqrt(nv / c2) + ADAM_EPS) + ADAM_WD * w_ref[...])

    blk = pl.BlockSpec((tr, cols), lambda i: (i, 0))
    g_specs = [blk] if ng == 1 else [pl.BlockSpec((tr, cols), lambda i: (jnp.minimum(i, per - 1), 0)),
                                     pl.BlockSpec((tr, cols), lambda i: (jnp.maximum(i - per, 0), 0))]
    outs = pl.pallas_call(
        body,
        name=name,
        grid=(rows // tr,),
        in_specs=[blk] * 3 + g_specs,
        out_specs=[blk] * 4,
        out_shape=[_sds((rows, cols), F32)] * 4,
        compiler_params=_cparams(("parallel",)),
    )(w2, m2, v2, *gs2)
    return tuple(o.reshape(shape) for o in outs)


HBM = pl.BlockSpec(memory_space=pl.ANY)


def _shard_slice(ref, axis, size, index):
    idx = [slice(None)] * len(ref.shape)
    idx[axis] = pl.ds(pl.multiple_of(index * size, 8), size)
    return ref.at[tuple(idx)]


def _share_halves(name, full, tr):
    R, W4 = full.shape
    W, h = W4 // 4, R // 2
    steps = [(k, t) for k in range(3) for t in range(h // tr)]

    def body(f_in, f_ref, buf, lsem, ssem, rsem):
        x, y, c = lax.axis_index("x"), lax.axis_index("y"), lax.axis_index("c")
        sibling = (x, y, 1 - c)
        chips = [(1 - x, y), (x, 1 - y), (1 - x, 1 - y)]

        def tile(k, t):
            px, py = chips[k]
            return f_ref.at[pl.ds(pl.multiple_of(c * h + t * tr, 8), tr), pl.ds(pl.multiple_of((2 * px + py) * W, LANES), W)]

        sends = []
        for s, (k, t) in enumerate(steps):
            slot = s % 2
            if s >= 2:
                sends[s - 2].wait_send()
            lc = pltpu.make_async_copy(tile(k, t), buf.at[slot], lsem.at[slot])
            lc.start()
            lc.wait()
            rc = pltpu.make_async_remote_copy(src_ref=buf.at[slot], dst_ref=tile(k, t), send_sem=ssem.at[slot], recv_sem=rsem,
                                              device_id=sibling, device_id_type=MESH)
            rc.start()
            sends.append(rc)
        for rc in sends[-2:]:
            rc.wait_send()
        whole = f_ref.at[pl.ds(0, h), pl.ds(0, 3 * W)]
        pltpu.make_async_remote_copy(src_ref=whole, dst_ref=whole, send_sem=ssem.at[0], recv_sem=rsem,
                                     device_id=sibling, device_id_type=MESH).wait_recv()

    return pl.pallas_call(
        body,
        name=name,
        in_specs=[HBM],
        out_specs=HBM,
        out_shape=_sds(full.shape, full.dtype),
        input_output_aliases={0: 0},
        scratch_shapes=[pltpu.VMEM((2, tr, W), full.dtype), pltpu.SemaphoreType.DMA((2,)), pltpu.SemaphoreType.DMA((2,)),
                        pltpu.SemaphoreType.DMA(())],
    )(full)


IN_HBM = pl.BlockSpec(memory_space=pltpu.HBM)
IN_SEM = pl.BlockSpec(memory_space=pltpu.SEMAPHORE)
DATAFLOW = pltpu.SideEffectType.DATAFLOW_SIDE_EFFECTING


def _hbm(t):
    return pltpu.with_memory_space_constraint(t, pltpu.HBM)


def _token_spec():
    return pl.BlockSpec(memory_space=pltpu.VMEM)


def _gather_copies(s_refs, f_refs, axes, halves, send, recv, loc, arrival):
    x, y, c = lax.axis_index("x"), lax.axis_index("y"), lax.axis_index("c")
    chips = [(1 - x, y), (x, 1 - y), (1 - x, 1 - y)]
    local, remote = [], []
    for a in range(len(s_refs)):
        size = s_refs[a].shape[axes[a]]
        local.append(pltpu.make_async_copy(s_refs[a], _shard_slice(f_refs[a], axes[a], size, 2 * x + y), loc.at[a]))
        for k, (px, py) in enumerate(chips):
            block = (2 * px + py) if arrival else (2 * x + y)
            src, dst = s_refs[a], _shard_slice(f_refs[a], axes[a], size, block)
            if halves:
                assert axes[a] == 1 and len(s_refs[a].shape) == 2
                h = s_refs[a].shape[0] // 2
                rows = pl.ds(pl.multiple_of(c * h, 8), h)
                src = s_refs[a].at[rows, :]
                dst = f_refs[a].at[rows, pl.ds(pl.multiple_of(block * size, LANES), size)]
            remote.append(pltpu.make_async_remote_copy(src_ref=src, dst_ref=dst, send_sem=send.at[3 * a + k],
                                                       recv_sem=recv.at[3 * a + k], device_id=(px, py, c), device_id_type=MESH))
    return local, remote


def _gather_start(name, shards, axes, after, halves=False):
    n = len(shards)
    fulls = []
    for s, ax in zip(shards, axes):
        fs = list(s.shape)
        fs[ax] *= 4
        fulls.append(lax.empty(tuple(fs), s.dtype))

    def body(*refs):
        s_refs, f_refs = refs[:n], refs[n:2 * n]
        send, recv, loc, token = refs[2 * n + 1], refs[2 * n + 2], refs[2 * n + 3], refs[-1]
        local, remote = _gather_copies(s_refs, f_refs, axes, halves, send, recv, loc, arrival=False)
        for cp in remote + local:
            cp.start()
        token[...] = jnp.zeros_like(token)

    outs = pl.pallas_call(
        body,
        name=name,
        out_shape=(pltpu.SemaphoreType.DMA((3 * n,)), pltpu.SemaphoreType.DMA((3 * n,)), pltpu.SemaphoreType.DMA((n,)),
                   *[pltpu.HBM(t.shape, t.dtype) for t in shards + fulls], _sds((8, LANES), F32)),
        in_specs=[IN_HBM] * (2 * n) + [HBM],
        out_specs=(IN_SEM, IN_SEM, IN_SEM, *[IN_HBM] * (2 * n), _token_spec()),
        input_output_aliases={i: 3 + i for i in range(2 * n)},
        compiler_params=pltpu.CompilerParams(has_side_effects=DATAFLOW),
    )(*[_hbm(t) for t in shards + fulls], after)
    return (outs[0], outs[1], outs[2], list(outs[3:3 + n]), list(outs[3 + n:3 + 2 * n]), axes, halves), outs[-1]


def _gather_wait(name, state, *after):
    send, recv, loc, s_thru, f_thru, axes, halves = state
    n = len(s_thru)

    def body(*refs):
        s_refs, f_refs = refs[:n], refs[n:2 * n]
        local, remote = _gather_copies(s_refs, f_refs, axes, halves, refs[2 * n], refs[2 * n + 1], refs[2 * n + 2], arrival=True)
        for cp in local:
            cp.wait()
        for cp in remote:
            cp.wait_send()
            cp.wait_recv()

    outs = pl.pallas_call(
        body,
        name=name,
        out_shape=tuple(pltpu.HBM(t.shape, t.dtype) for t in s_thru + f_thru),
        in_specs=[IN_HBM] * (2 * n) + [IN_SEM, IN_SEM, IN_SEM] + [HBM] * len(after),
        out_specs=tuple([IN_HBM] * (2 * n)),
        input_output_aliases={i: i for i in range(2 * n)},
        compiler_params=pltpu.CompilerParams(has_side_effects=DATAFLOW),
    )(*s_thru, *f_thru, send, recv, loc, *after)
    return list(outs[n:2 * n])


FLIPS = [(fx, fy, fc) for fx in (0, 1) for fy in (0, 1) for fc in (0, 1)][1:]


def _piece_shape(shape, axis):
    ps = list(shape)
    if axis == 0:
        ps[0] //= 8
    else:
        ps[0] //= 2
        ps[axis] //= 4
    return tuple(ps)


def _piece(ref, axis, q, c):
    shape = ref.shape
    idx = [slice(None)] * len(shape)
    if axis == 0:
        h = shape[0] // 8
        idx[0] = pl.ds(pl.multiple_of((2 * q + c) * h, 8), h)
    else:
        h, w = shape[0] // 2, shape[axis] // 4
        idx[0] = pl.ds(c * h, h)
        idx[axis] = pl.ds(pl.multiple_of(q * w, LANES if axis == len(shape) - 1 else 8), w)
    return ref.at[tuple(idx)]


def _scatter_copies(g_refs, l_refs, axes, send, recv):
    x, y, c = lax.axis_index("x"), lax.axis_index("y"), lax.axis_index("c")
    out = []
    for a in range(len(g_refs)):
        for k, (fx, fy, fc) in enumerate(FLIPS):
            tx, ty, tc = x ^ fx, y ^ fy, c ^ fc
            out.append(pltpu.make_async_remote_copy(
                src_ref=_piece(g_refs[a], axes[a], 2 * tx + ty, tc), dst_ref=l_refs[a].at[k],
                send_sem=send.at[7 * a + k], recv_sem=recv.at[7 * a + k], device_id=(tx, ty, tc), device_id_type=MESH))
    return out


def _scatter_start(name, grads, axes):
    n = len(grads)
    lands = [lax.empty((7,) + _piece_shape(g.shape, ax), g.dtype) for g, ax in zip(grads, axes)]

    def body(*refs):
        g_refs, l_refs = refs[:n], refs[n:2 * n]
        send, recv, token = refs[2 * n], refs[2 * n + 1], refs[-1]
        for cp in _scatter_copies(g_refs, l_refs, axes, send, recv):
            cp.start()
        token[...] = jnp.zeros_like(token)

    outs = pl.pallas_call(
        body,
        name=name,
        out_shape=(pltpu.SemaphoreType.DMA((7 * n,)), pltpu.SemaphoreType.DMA((7 * n,)),
                   *[pltpu.HBM(t.shape, t.dtype) for t in grads + lands], _sds((8, LANES), F32)),
        in_specs=[IN_HBM] * (2 * n),
        out_specs=(IN_SEM, IN_SEM, *[IN_HBM] * (2 * n), _token_spec()),
        input_output_aliases={i: 2 + i for i in range(2 * n)},
        compiler_params=pltpu.CompilerParams(has_side_effects=DATAFLOW),
    )(*[_hbm(t) for t in grads + lands])
    return (outs[0], outs[1], list(outs[2:2 + n]), list(outs[2 + n:2 + 2 * n]), axes), outs[-1]


def _scatter_wait(name, state, *after):
    send, recv, g_thru, l_thru, axes = state
    n = len(g_thru)

    def body(*refs):
        g_refs, l_refs = refs[:n], refs[n:2 * n]
        for cp in _scatter_copies(g_refs, l_refs, axes, refs[2 * n], refs[2 * n + 1]):
            cp.wait_send()
            cp.wait_recv()

    outs = pl.pallas_call(
        body,
        name=name,
        out_shape=tuple(pltpu.HBM(t.shape, t.dtype) for t in g_thru + l_thru),
        in_specs=[IN_HBM] * (2 * n) + [IN_SEM, IN_SEM] + [HBM] * len(after),
        out_specs=tuple([IN_HBM] * (2 * n)),
        input_output_aliases={i: i for i in range(2 * n)},
        compiler_params=pltpu.CompilerParams(has_side_effects=DATAFLOW),
    )(*g_thru, *l_thru, send, recv, *after)
    return list(outs[:n]), list(outs[n:2 * n])


def _reduce_join(name, landing, g, axis):
    R, C = _piece_shape(g.shape, axis)
    l3 = landing.reshape(7, R, C)
    tr = _pick(R, [t for t in (512, 256, 128, 64, 32, 16, 8) if t * C <= 256 * 1024])
    nsteps = R // tr

    def own_block(i):
        q, c = 2 * lax.axis_index("x") + lax.axis_index("y"), lax.axis_index("c")
        return ((2 * q + c) * nsteps + i, 0) if axis == 0 else (c * nsteps + i, q)

    def body(own_ref, l_ref, o_ref, buf, send, loc, recv):
        i = pl.program_id(0)
        x, y, c = lax.axis_index("x"), lax.axis_index("y"), lax.axis_index("c")
        sibling = (x, y, 1 - c)

        def copies(slot, step):
            dst = o_ref.at[pl.ds(pl.multiple_of(c * R + step * tr, 8), tr), :]
            return (pltpu.make_async_copy(buf.at[slot], dst, loc.at[slot]),
                    pltpu.make_async_remote_copy(src_ref=buf.at[slot], dst_ref=dst, send_sem=send.at[slot], recv_sem=recv,
                                                 device_id=sibling, device_id_type=MESH))

        @pl.when(i >= 2)
        def _():
            lc, rc = copies(i % 2, i - 2)
            lc.wait()
            rc.wait_send()

        acc = own_ref[...].astype(F32)
        for s in range(7):
            acc = acc + l_ref[s].astype(F32)
        buf[i % 2] = acc
        lc, rc = copies(i % 2, i)
        lc.start()
        rc.start()

        @pl.when(i == nsteps - 1)
        def _():
            for st in range(max(nsteps - 2, 0), nsteps):
                lc, rc = copies(st % 2, st)
                lc.wait()
                rc.wait_send()
            theirs = o_ref.at[pl.ds(pl.multiple_of((1 - c) * R, 8), R), :]
            pltpu.make_async_remote_copy(src_ref=theirs, dst_ref=theirs, send_sem=send.at[0], recv_sem=recv,
                                         device_id=sibling, device_id_type=MESH).wait_recv()

    return pl.pallas_call(
        body,
        name=name,
        grid=(nsteps,),
        in_specs=[pl.BlockSpec((tr, C), own_block), pl.BlockSpec((7, tr, C), lambda i: (0, i, 0))],
        out_specs=HBM,
        out_shape=_sds((2 * R, C), F32),
        scratch_shapes=[pltpu.VMEM((2, tr, C), F32), pltpu.SemaphoreType.DMA((2,)), pltpu.SemaphoreType.DMA((2,)),
                        pltpu.SemaphoreType.DMA(())],
        compiler_params=_cparams(("arbitrary",)),
    )(g, l3)


def _all_reduce_small(v, dep):
    R, D = v.shape

    def body(v_ref, dep_ref, o_ref, land, send, recv):
        x, y, c = lax.axis_index("x"), lax.axis_index("y"), lax.axis_index("c")
        my_slot = 4 * x + 2 * y + c
        land[my_slot] = v_ref[...]
        for k, (fx, fy, fc) in enumerate(FLIPS):
            tx, ty, tc = x ^ fx, y ^ fy, c ^ fc
            pltpu.make_async_remote_copy(src_ref=v_ref, dst_ref=land.at[my_slot], send_sem=send.at[k], recv_sem=recv.at[k],
                                         device_id=(tx, ty, tc), device_id_type=MESH).start()
        for k, (fx, fy, fc) in enumerate(FLIPS):
            tx, ty, tc = x ^ fx, y ^ fy, c ^ fc
            cp = pltpu.make_async_remote_copy(src_ref=v_ref, dst_ref=land.at[4 * tx + 2 * ty + tc], send_sem=send.at[k],
                                              recv_sem=recv.at[k], device_id=(tx, ty, tc), device_id_type=MESH)
            cp.wait_send()
            cp.wait_recv()
        acc = land[0]
        for s in range(1, 8):
            acc = acc + land[s]
        o_ref[...] = acc

    return pl.pallas_call(
        body,
        name="all_reduce_small",
        in_specs=[pl.BlockSpec(memory_space=pltpu.VMEM), pl.BlockSpec(memory_space=pl.ANY)],
        out_specs=pl.BlockSpec(memory_space=pltpu.VMEM),
        out_shape=_sds((R, D), F32),
        scratch_shapes=[pltpu.VMEM((8, R, D), F32), pltpu.SemaphoreType.DMA((7,)), pltpu.SemaphoreType.DMA((7,))],
    )(v, dep)


def kernel(x, attn_w_in, attn_w_out, hgrn_w_in, hgrn_w_out, hgrn_norm_g, lb_logits, ln_mix_g, ln_mix_b, ln_ffn_g, ln_ffn_b, ffn_w_up, ffn_w_down, loss_target, m_attn_w_in, m_attn_w_out, m_hgrn_w_in, m_hgrn_w_out, m_hgrn_norm_g, m_lb_logits, m_ln_mix_g, m_ln_mix_b, m_ln_ffn_g, m_ln_ffn_b, m_ffn_w_up, m_ffn_w_down, v_attn_w_in, v_attn_w_out, v_hgrn_w_in, v_hgrn_w_out, v_hgrn_norm_g, v_lb_logits, v_ln_mix_g, v_ln_mix_b, v_ln_ffn_g, v_ln_ffn_b, v_ffn_w_up, v_ffn_w_down):
    xs = x[0]
    tgt = loss_target[0]
    S, D = xs.shape
    F = ffn_w_up.shape[2] * 4
    T1 = _pick(S, (1024, 512, 256))
    T2 = _pick(S, (2048, 1024, 512))
    TH = _pick(S, (512, 256))
    TB = _pick(S, (128,))
    TF = _pick(F, (1024, 512))
    TG = _pick(3 * D, (1536, 1024, 768))
    TW = _pick(F, (2048, 1024))

    cast = lambda w: w.astype(MXU_DTYPE)
    st_a, tok = _gather_start("gather_a", [cast(attn_w_in[0])], [1], jnp.zeros((8, LANES), F32), halves=True)
    tok, (xs_late, w_aout, w_fup, w_fdown, w_hin, w_hout) = lax.optimization_barrier(
        (tok, (xs, attn_w_out, ffn_w_up, ffn_w_down, hgrn_w_in, hgrn_w_out)))
    st_b, tok = _gather_start("gather_b", [cast(w_aout[0]), cast(w_fup[0]), cast(w_fdown[0])], [0, 1, 0], tok)
    st_c, tok = _gather_start("gather_c", [cast(w_hin[0]), cast(w_hout[0]), hgrn_norm_g, cast(w_fup[1]), cast(w_fdown[1])],
                              [1, 0, 1, 1, 0], tok)

    cos3, sin3 = _rope_tables(S)
    sel = _head_sel(D)
    sel_t = sel.T

    xc3 = _stack_classes("x_classes", xs_late, MXU_DTYPE)
    P3 = _attn_proj("attn_proj_own", xc3, st_a[3][0], cos3, sin3, T2, None, tok)
    (wa_in,) = _gather_wait("gather_a_wait", st_a, P3)
    wa_in = _share_halves("share_a", wa_in, _pick(D // 2, (256, 128)))
    P3 = _attn_proj("attn_proj", xc3, wa_in, cos3, sin3, T2, P3)
    o3, lse3 = _attn_fwd(P3, D)
    o_att, L_att = _attn_mix(o3, lse3, sel)
    wa_out, w_up0, w_down0 = _gather_wait("gather_b_wait", st_b, L_att)
    ln1 = (ln_mix_g[0:1], ln_mix_b[0:1])
    ln2 = (ln_ffn_g[0:1], ln_ffn_b[0:1])
    ln3 = (ln_mix_g[1:2], ln_mix_b[1:2])
    ln4 = (ln_ffn_g[1:2], ln_ffn_b[1:2])
    xm1, xh1, r1 = _mm_res_ln("attn_out_ln", o_att, wa_out, xs, *ln1, TH, D)
    a0 = _mlp_up("mlp0_up", xm1, w_up0, T2, TF, D)
    xm2, xh2, r2 = _mm_res_ln("mlp0_down_ln", a0, w_down0, (xh1, *ln1), *ln2, TH, F)

    wh_in, wh_out, norm_g, w_up1, w_down1 = _gather_wait("gather_c_wait", st_c, r2)
    P1 = _plain_mm("hgrn_proj", xm2, wh_in, "nn", F32, T1, _pick(3 * D, (1024, 768, 512)), D)
    o_h, n_h, states = _hgrn_fwd(P1, lb_logits, norm_g, TB)
    xm3, xh3, r3 = _mm_res_ln("hgrn_out_ln", n_h, wh_out, (xh2, *ln2), *ln3, TH, D)
    a1 = _mlp_up("mlp1_up", xm3, w_up1, T2, TF, D)
    _, xh4, r4 = _mm_res_ln("mlp1_down_ln", a1, w_down1, (xh3, *ln3), *ln4, TH, F)

    wgrad = lambda name, a, dy, tm, tn: _plain_mm(name, a, dy, "tn", MXU_DTYPE, tm, tn, T1)
    sq, du4, dum4, dg_ffn1, db_ffn1 = _loss_ln_bwd(tgt, xh4, r4, *ln4, TH)
    dh1 = _mlp_down_bwd("mlp1_down_bwd", dum4, w_down1, a1, T2, TF, D)
    g_down1 = wgrad("g_down1", a1, dum4, TW, D)
    g_up1 = wgrad("g_up1", xm3, dh1, D, TW)
    sc_1, tok = _scatter_start("scatter_1", [g_down1, g_up1], [0, 1])
    du3, dum3, dg_mix1, db_mix1 = _mm_nt_res_ln_bwd("mlp1_up_bwd", dh1, w_up1, du4, xh3, r3, ln_mix_g[1:2], TH, F, tok)
    dn = _plain_mm("hgrn_out_bwd", dum3, wh_out, "nt", F32, T1, D, D)
    g_hout = wgrad("g_hgrn_out", n_h, dum3, D, D)
    dP1, dg_norm, dlb = _hgrn_bwd(P1, o_h, states, dn, lb_logits, norm_g, TB)
    g_hin = wgrad("g_hgrn_in", xm2, dP1, D, TG)
    d_lb_logits = _lb_logits_grad(dlb, lb_logits)
    sc_2, tok = _scatter_start("scatter_2", [g_hout, g_hin], [0, 1])

    du2, dum2, dg_ffn0, db_ffn0 = _mm_nt_res_ln_bwd("hgrn_in_bwd", dP1, wh_in, du3, xh2, r2, ln_ffn_g[0:1], TH, 3 * D, tok)
    dh0 = _mlp_down_bwd("mlp0_down_bwd", dum2, w_down0, a0, T2, TF, D)
    g_down0 = wgrad("g_down0", a0, dum2, TW, D)
    g_up0 = wgrad("g_up0", xm1, dh0, D, TW)
    sc_3, tok = _scatter_start("scatter_3", [g_down0, g_up0], [0, 1])
    du1, dum1, dg_mix0, db_mix0 = _mm_nt_res_ln_bwd("mlp0_up_bwd", dh0, w_up0, du2, xh1, r1, ln_mix_g[0:1], TH, F, tok)
    do, delta = _attn_out_bwd(dum1, wa_out, o_att, sel_t, TH, D)
    g_aout = wgrad("g_attn_out", o_att, dum1, D, D)
    sc_5, tok = _scatter_start("scatter_5", [g_aout], [0])
    dP3 = _attn_bwd(P3, _stack_classes("do_classes", do, MXU_DTYPE), _stack_classes("lse_classes", L_att, F32),
                    _stack_classes("delta_classes", delta, F32), cos3, sin3, D, tok)
    small = jnp.concatenate([d_lb_logits, dg_mix0, dg_mix1, db_mix0, db_mix1, dg_ffn0, dg_ffn1, db_ffn0, db_ffn1,
                             dg_norm, sq, jnp.zeros((4, D), F32)], axis=0)
    small = _all_reduce_small(small, dP3)
    loss = 0.5 * jnp.sum(small[11]) / D
    grp = lambda j: j // (3 * D // TG)
    g_ain = _matmul("g_attn_in", xc3, dP3, "tn", D, TG, T1, [(_sds((D, 9 * D), MXU_DTYPE), _ij_spec(D, TG))], _store_epilogue,
                    a_map=lambda i, j, k: (k + grp(j) * (S // T1), i),
                    b_map=lambda i, j, k: (k + grp(j) * (S // T1), j % (3 * D // TG)), mnk=(D, 9 * D, S), dep=small)[0]
    sc_4, tok = _scatter_start("scatter_4", [g_ain], [1])
    dxc3 = _matmul("attn_in_bwd", dP3, wa_in, "nt", T1, D, 3 * D, [(_sds((3 * S, D), F32), _ij_spec(T1, D))], _store_epilogue,
                   b_map=lambda i, j, k: (j, k + i // (S // T1)), mnk=(3 * S, D, 3 * D), dep=tok)[0]
    grad_x = _input_grad(du1, dxc3)

    def reduced(name, state, *after):
        gs, lands = _scatter_wait(name + "_wait", state, *after)
        return [_reduce_join(f"{name}_reduce_{i}", l, g, ax) for i, (l, g, ax) in enumerate(zip(lands, gs, state[4]))]

    r_down1, r_up1 = reduced("scatter_1", sc_1, grad_x)
    r_hout, r_hin = reduced("scatter_2", sc_2, r_up1)
    r_down0, r_up0 = reduced("scatter_3", sc_3, r_hin)
    (r_aout,) = reduced("scatter_5", sc_5, r_up0)

    my_chip = 2 * lax.axis_index("x") + lax.axis_index("y")
    nsh = hgrn_norm_g.shape[1]
    g_norm = lax.dynamic_slice(small[10:11], (0, my_chip * nsh), (1, nsh))

    grads, upd = {}, {}

    def update(nm, w, gs, m, v):
        upd[nm] = _adamw("adamw_" + nm, w, gs, m, v)
        grads[nm] = upd[nm][3]

    update("hgrn_w_in", hgrn_w_in, [r_hin], m_hgrn_w_in, v_hgrn_w_in)
    update("hgrn_w_out", hgrn_w_out, [r_hout], m_hgrn_w_out, v_hgrn_w_out)
    update("ffn_w_up", ffn_w_up, [r_up0, r_up1], m_ffn_w_up, v_ffn_w_up)
    update("ffn_w_down", ffn_w_down, [r_down0, r_down1], m_ffn_w_down, v_ffn_w_down)
    update("attn_w_out", attn_w_out, [r_aout], m_attn_w_out, v_attn_w_out)
    update("hgrn_norm_g", hgrn_norm_g, [g_norm], m_hgrn_norm_g, v_hgrn_norm_g)
    cat = lambda ts: jnp.concatenate(ts, axis=0)
    small_w = cat([lb_logits, ln_mix_g, ln_mix_b, ln_ffn_g, ln_ffn_b])
    small_m = cat([m_lb_logits, m_ln_mix_g, m_ln_mix_b, m_ln_ffn_g, m_ln_ffn_b])
    small_v = cat([v_lb_logits, v_ln_mix_g, v_ln_mix_b, v_ln_ffn_g, v_ln_ffn_b])
    small_upd = _adamw("adamw_small", small_w, [small[0:10]], small_m, small_v)
    for i, nm in enumerate(["lb_logits", "ln_mix_g", "ln_mix_b", "ln_ffn_g", "ln_ffn_b"]):
        grads[nm] = small[2 * i:2 * i + 2]
        upd[nm] = tuple(t[2 * i:2 * i + 2] for t in small_upd)
    done = [upd[k][2] for k in ("hgrn_w_in", "hgrn_w_out", "ffn_w_up", "ffn_w_down", "attn_w_out", "hgrn_norm_g")]
    (r_ain,) = reduced("scatter_4", sc_4, small_upd[2], *done)
    update("attn_w_in", attn_w_in, [r_ain], m_attn_w_in, v_attn_w_in)

    order = ["attn_w_in", "attn_w_out", "hgrn_w_in", "hgrn_w_out", "hgrn_norm_g", "lb_logits", "ln_mix_g", "ln_mix_b",
             "ln_ffn_g", "ln_ffn_b", "ffn_w_up", "ffn_w_down"]
    return (loss, grad_x[None], *[grads[k] for k in order], *[upd[k][0] for k in order],
            *[upd[k][1] for k in order], *[upd[k][2] for k in order])
```

```python
import math

import jax
import jax.numpy as jnp
from jax import lax
from jax.experimental import pallas as pl
from jax.experimental.pallas import tpu as pltpu

F32 = jnp.float32
BF16 = jnp.bfloat16
MXU_DTYPE = BF16

HEAD_DIM = 64
ATTN_BLK = 128
DILATIONS = (1, 4, 16)
ROPE_THETA = 10000.0
HGRN_DK = 128
HGRN_CHUNK = 64
DEPTH = 2
LN_EPS = 1e-5
RMS_EPS = 1e-6
ALPHA = (2 * DEPTH) ** 0.25
ADAM_LR, ADAM_B1, ADAM_B2, ADAM_EPS, ADAM_WD, ADAM_STEP = 0.001, 0.9, 0.999, 1e-08, 0.01, 10

LANES = 128
VMEM_LIMIT = 56 * 1024 * 1024
NEG = -1e30
MESH = pl.DeviceIdType.MESH


def _cparams(sem=None):
    return pltpu.CompilerParams(dimension_semantics=sem, vmem_limit_bytes=VMEM_LIMIT)


def _sds(shape, dtype):
    return jax.ShapeDtypeStruct(tuple(shape), dtype)


def _dg(a, b, ca, cb):
    return lax.dot_general(a, b, (((ca,), (cb,)), ((), ())), preferred_element_type=F32)


def _nn(a, b):
    return _dg(a, b, 1, 0)


def _nt(a, b):
    return _dg(a, b, 1, 1)


def _tn(a, b):
    return _dg(a, b, 0, 0)


def _split3(a):
    hi = a.astype(BF16)
    r = a - hi.astype(F32)
    mid = r.astype(BF16)
    lo = (r - mid.astype(F32)).astype(BF16)
    return hi, mid, lo


def _exact_nn(a, sel):
    hi, mid, lo = _split3(a)
    return _nn(hi, sel) + _nn(mid, sel) + _nn(lo, sel)


def _pick(n, prefs):
    for p in prefs:
        if n % p == 0:
            return p
    return n


def _matmul(name, a, b, form, tm, tn, tk, outs, epilogue, extras=(), a_map=None, b_map=None, mnk=None, dep=None,
            sem=("parallel", "parallel", "arbitrary"), split=None, alias_dep=False):
    if form == "nn":
        (M, K), N = a.shape, b.shape[1]
        a_spec = pl.BlockSpec((tm, tk), a_map or (lambda i, j, k: (i, k)))
        b_spec = pl.BlockSpec((tk, tn), b_map or (lambda i, j, k: (k, j)))
        ca, cb = 1, 0
    elif form == "nt":
        (M, K), N = a.shape, b.shape[0]
        a_spec = pl.BlockSpec((tm, tk), a_map or (lambda i, j, k: (i, k)))
        b_spec = pl.BlockSpec((tn, tk), b_map or (lambda i, j, k: (j, k)))
        ca, cb = 1, 1
    else:
        (K, M), N = a.shape, b.shape[1]
        a_spec = pl.BlockSpec((tk, tm), a_map or (lambda i, j, k: (k, i)))
        b_spec = pl.BlockSpec((tk, tn), b_map or (lambda i, j, k: (k, j)))
        ca, cb = 0, 0
    if mnk is not None:
        M, N, K = mnk
    assert M % tm == 0 and N % tn == 0 and K % tk == 0, (name, M, N, K, tm, tn, tk)
    nk = K // tk
    ne, no = len(extras), len(outs)
    deps = [] if dep is None else [dep]
    nd = len(deps)

    def body(a_ref, b_ref, *rest):
        extra_refs, out_refs = rest[:ne], rest[ne + nd:ne + nd + no]
        j = pl.program_id(1)
        if split is not None:
            kind, n = split
            assert nk == 1 and form != "tn"
            tiled = [t for _, _, *t in list(extras) + list(outs)]
            refs = list(extra_refs) + list(out_refs)
            for ci in range(n):
                if kind == "cols":
                    cs = slice(ci * (tn // n), (ci + 1) * (tn // n))
                    part = _dg(a_ref[...].astype(MXU_DTYPE), (b_ref[:, cs] if form == "nn" else b_ref[cs, :]).astype(MXU_DTYPE), ca, cb)
                    view = [r.at[:, cs] if t else r for r, t in zip(refs, tiled)]
                else:
                    rs = slice(ci * (tm // n), (ci + 1) * (tm // n))
                    part = _dg(a_ref[rs, :].astype(MXU_DTYPE), b_ref[...].astype(MXU_DTYPE), ca, cb)
                    view = [r.at[rs, :] if t else r for r, t in zip(refs, tiled)]
                epilogue(part, view[:ne], view[ne:], j, ci)
            return
        part = _dg(a_ref[...].astype(MXU_DTYPE), b_ref[...].astype(MXU_DTYPE), ca, cb)
        if nk == 1:
            epilogue(part, extra_refs, out_refs, j, 0)
            return
        acc_ref = rest[-1]
        k = pl.program_id(2)

        @pl.when(k == 0)
        def _():
            acc_ref[...] = part

        @pl.when(k > 0)
        def _():
            acc_ref[...] += part

        @pl.when(k == nk - 1)
        def _():
            epilogue(acc_ref[...], extra_refs, out_refs, j, 0)

    res = pl.pallas_call(
        body,
        name=name,
        grid=(M // tm, N // tn, nk),
        in_specs=[a_spec, b_spec] + [s for _, s, *_ in extras] + [pl.BlockSpec(memory_space=pl.ANY)] * nd,
        out_specs=[s for _, s, *_ in outs],
        out_shape=[o for o, *_ in outs],
        scratch_shapes=[pltpu.VMEM((tm, tn), F32)] if nk > 1 else [],
        input_output_aliases={2 + ne: 0} if alias_dep else {},
        compiler_params=_cparams(sem),
    )(a, b, *[e for e, *_ in extras], *deps)
    return res


def _ij_spec(tm, tn):
    return pl.BlockSpec((tm, tn), lambda i, j, k: (i, j))


def _store_epilogue(acc, extra_refs, out_refs, j, ci):
    out_refs[0][...] = acc.astype(out_refs[0].dtype)


def _plain_mm(name, a, b, form, out_dtype, tm, tn, tk):
    M = a.shape[1] if form == "tn" else a.shape[0]
    N = b.shape[0] if form == "nt" else b.shape[1]
    return _matmul(name, a, b, form, tm, tn, tk, [(_sds((M, N), out_dtype), _ij_spec(tm, tn))], _store_epilogue)[0]


def _class_slabs(S):
    assert DILATIONS[0] == 1
    return [(g, d, r, S // d) for g, d in enumerate(DILATIONS) if d > 1 for r in range(d)]


def _stack_classes(name, t, out_dtype):
    S, W = t.shape

    def body(x_ref, o_ref):
        o_ref[0:S, :] = x_ref[...].astype(out_dtype)
        for g, d, r, n in _class_slabs(S):
            o_ref[g * S + r * n:g * S + (r + 1) * n, :] = x_ref[pl.ds(r, n, stride=d), :].astype(out_dtype)

    return pl.pallas_call(
        body,
        name=name,
        grid=(W // LANES,),
        in_specs=[pl.BlockSpec((S, LANES), lambda j: (0, j))],
        out_specs=pl.BlockSpec((3 * S, LANES), lambda j: (0, j)),
        out_shape=_sds((3 * S, W), out_dtype),
        compiler_params=_cparams(("parallel",)),
    )(t)


def _rope_tables(seq):
    half = HEAD_DIM // 2
    inv = ROPE_THETA ** (-jnp.arange(half, dtype=F32) * (2.0 / HEAD_DIM))
    inv = jnp.tile(inv, LANES // half)
    pos = []
    for d in DILATIONS:
        row = jnp.arange(seq)
        pos.append((row % (seq // d)) * d + row // (seq // d))
    ang = jnp.concatenate(pos).astype(F32)[:, None] * inv[None, :]
    first = (jnp.arange(LANES) % HEAD_DIM) < half
    sin = jnp.sin(ang)
    return jnp.cos(ang), jnp.where(first[None, :], -sin, sin)


def _partner(x):
    half = HEAD_DIM // 2
    lane = lax.broadcasted_iota(jnp.int32, x.shape, 1)
    first = (lane % HEAD_DIM) < half
    return jnp.where(first, pltpu.roll(x, LANES - half, 1), pltpu.roll(x, half, 1))


def _attn_proj(name, x3, w, cos3, sin3, tm, prev, dep=None):
    S3, D = x3.shape
    S = S3 // 3
    tn = 3 * D // 4
    nrow = S // tm
    local = prev is None

    def tile(j):
        q = 2 * lax.axis_index("x") + lax.axis_index("y")
        c0 = 3 * q + j if local else j + 3 * (j >= 3 * q).astype(jnp.int32)
        return c0, c0 // 4, c0 % 4

    def epilogue(acc, extra_refs, out_refs, j, ci):
        cos_ref, sin_ref = extra_refs
        o_ref = out_refs[0]
        _, _, place = tile(j)
        width = acc.shape[1]
        assert D % width == 0
        is_rot = (place * tn + ci * width) // D < 2
        c = jnp.where(is_rot, cos_ref[...], 1.0)
        s = jnp.where(is_rot, sin_ref[...], 0.0)
        for t in range(width // LANES):
            xs = acc[:, t * LANES:(t + 1) * LANES]
            o_ref[:, t * LANES:(t + 1) * LANES] = (xs * c + _partner(xs) * s).astype(o_ref.dtype)

    rows = lambda i, j: tile(j)[1] * nrow + i
    tab = pl.BlockSpec((tm, LANES), lambda i, j, k: (rows(i, j), 0))
    out = pl.BlockSpec((tm, tn), lambda i, j, k: (rows(i, j), tile(j)[2]))
    ntiles = 3 if local else 9
    return _matmul(name, x3, w, "nn", tm, tn, D, [(_sds((S3, 3 * D), MXU_DTYPE), out, True)], epilogue,
                   extras=[(cos3, tab), (sin3, tab)], a_map=lambda i, j, k: (rows(i, j), k),
                   b_map=lambda i, j, k: (k, j if local else tile(j)[0]), mnk=(nrow * tm, ntiles * tn, D),
                   dep=dep if local else prev, alias_dep=not local, split=("cols", 3))[0]


def _head_sel(d_model):
    h = jnp.arange(LANES)[:, None]
    l = jnp.arange(d_model)[None, :]
    return (l // HEAD_DIM == h).astype(BF16)


def _class_edges(b, nblk):
    g = b // nblk
    per_class = jnp.where(g == 0, nblk // DILATIONS[0], jnp.where(g == 1, nblk // DILATIONS[1], nblk // DILATIONS[2]))
    pos = (b % nblk) % per_class
    return pos != 0, pos != per_class - 1


def _two_heads(t, top):
    zero = jnp.zeros_like(t)
    return jnp.concatenate([jnp.where(top, t, zero), jnp.where(top, zero, t)], axis=0)


def _band_mask(has_prev):
    B = ATTN_BLK
    row = lax.broadcasted_iota(jnp.int32, (2 * B, 2 * B), 0) % B
    col = lax.broadcasted_iota(jnp.int32, (2 * B, 2 * B), 1)
    in_prev = jnp.logical_and(jnp.logical_and(col < B, col >= row), has_prev)
    in_own = jnp.logical_and(col >= B, col - B <= row)
    return jnp.logical_or(in_prev, in_own)


def _attn_fwd(P3, D):
    S3 = P3.shape[0]
    B = ATTN_BLK
    nblk = S3 // 3 // B
    npairs = D // LANES
    scale = HEAD_DIM ** -0.5

    def body(q_ref, kc_ref, vc_ref, kp_ref, vp_ref, o_ref, lse_ref):
        has_prev, _ = _class_edges(pl.program_id(0), nblk)
        bias = jnp.where(_band_mask(has_prev), 0.0, NEG)
        lane = lax.broadcasted_iota(jnp.int32, (B, LANES), 1)
        top = lane < HEAD_DIM
        lse_acc = jnp.zeros((B, LANES), F32)
        for j in range(npairs):
            sl = slice(j * LANES, (j + 1) * LANES)
            Q = _two_heads(q_ref[:, sl] * scale, top)
            K2 = jnp.concatenate([kp_ref[:, sl], kc_ref[:, sl]], axis=0)
            V2 = jnp.concatenate([vp_ref[:, sl], vc_ref[:, sl]], axis=0)
            s = _nt(Q, K2) + bias
            m = jnp.max(s, axis=1, keepdims=True)
            p = jnp.exp(s - m)
            l = jnp.sum(p, axis=1, keepdims=True)
            o = _nn(p.astype(MXU_DTYPE), V2) * (1.0 / l)
            o_ref[:, sl] = jnp.where(top, o[:B], o[B:])
            lse = m + jnp.log(l)
            lse_acc = jnp.where(lane == 2 * j, lse[:B], jnp.where(lane == 2 * j + 1, lse[B:], lse_acc))
        lse_ref[...] = lse_acc

    blk = lambda part, prev: pl.BlockSpec(
        (B, D), (lambda b: (jnp.maximum(b - 1, 0), part)) if prev else (lambda b: (b, part)))
    return pl.pallas_call(
        body,
        name="attn_fwd",
        grid=(3 * nblk,),
        in_specs=[blk(0, False), blk(1, False), blk(2, False), blk(1, True), blk(2, True)],
        out_specs=[pl.BlockSpec((B, D), lambda b: (b, 0)), pl.BlockSpec((B, LANES), lambda b: (b, 0))],
        out_shape=[_sds((S3, D), F32), _sds((S3, LANES), F32)],
        compiler_params=_cparams(("parallel",)),
    )(P3, P3, P3, P3, P3)


def _attn_mix(o3, lse3, sel):
    S3, D = o3.shape
    S = S3 // 3

    def body(o3_ref, lse_ref, sel_ref, o_ref, L_ref, w_ref):
        @pl.when(pl.program_id(0) == 0)
        def _():
            w_ref[0] = lse_ref[0:S, :]
            for g, d, r, n in _class_slabs(S):
                w_ref[g, pl.ds(r, n, stride=d), :] = lse_ref[g * S + r * n:g * S + (r + 1) * n, :]
            a, b, c = w_ref[0], w_ref[1], w_ref[2]
            m = jnp.maximum(jnp.maximum(a, b), c)
            L = m + jnp.log(jnp.exp(a - m) + jnp.exp(b - m) + jnp.exp(c - m))
            L_ref[...] = L
            w_ref[0] = jnp.exp(a - L)
            w_ref[1] = jnp.exp(b - L)
            w_ref[2] = jnp.exp(c - L)

        s = sel_ref[...]
        o_ref[...] = _exact_nn(w_ref[0], s) * o3_ref[0:S, :]
        for g, d, r, n in _class_slabs(S):
            rows = pl.ds(r, n, stride=d)
            o_ref[rows, :] += _exact_nn(w_ref[g, rows, :], s) * o3_ref[g * S + r * n:g * S + (r + 1) * n, :]

    return pl.pallas_call(
        body,
        name="attn_mix",
        grid=(D // LANES,),
        in_specs=[pl.BlockSpec((S3, LANES), lambda j: (0, j)), pl.BlockSpec((S3, LANES), lambda j: (0, 0)),
                  pl.BlockSpec((LANES, LANES), lambda j: (0, j))],
        out_specs=[pl.BlockSpec((S, LANES), lambda j: (0, j)), pl.BlockSpec((S, LANES), lambda j: (0, 0))],
        out_shape=[_sds((S, D), F32), _sds((S, LANES), F32)],
        scratch_shapes=[pltpu.VMEM((3, S, LANES), F32)],
        compiler_params=_cparams(("arbitrary",)),
    )(o3, lse3, sel)


def _attn_bwd(P3, do3, L3, delta3, cos3, sin3, D, dep):
    S3 = P3.shape[0]
    B = ATTN_BLK
    nblk = S3 // 3 // B
    npairs = D // LANES
    scale = HEAD_DIM ** -0.5

    def body(c_ref, kp_ref, vp_ref, qn_ref, doc_ref, don_ref, Lc_ref, Ln_ref, dc_ref, dn_ref, cos_ref, sin_ref, dep_ref, out_ref):
        has_prev, has_next = _class_edges(pl.program_id(0), nblk)
        bias = jnp.where(_band_mask(has_prev), 0.0, NEG)
        row = lax.broadcasted_iota(jnp.int32, (2 * B, B), 0) % B
        col = lax.broadcasted_iota(jnp.int32, (2 * B, B), 1)
        bias_n = jnp.where(jnp.logical_and(col >= row, has_next), 0.0, NEG)
        lane = lax.broadcasted_iota(jnp.int32, (B, LANES), 1)
        top = lane < HEAD_DIM
        cos_t = cos_ref[...]
        sin_inv = -sin_ref[...]
        Lc_all, Ln_all, dc_all, dn_all = Lc_ref[...], Ln_ref[...], dc_ref[...], dn_ref[...]
        pair_col = lambda t, j: jnp.concatenate([t[:, 2 * j:2 * j + 1], t[:, 2 * j + 1:2 * j + 2]], axis=0)
        for j in range(npairs):
            sl = lambda part: slice(part * D + j * LANES, part * D + (j + 1) * LANES)
            pj = slice(j * LANES, (j + 1) * LANES)
            kc2, vc2 = c_ref[:, sl(1)], c_ref[:, sl(2)]
            K2 = jnp.concatenate([kp_ref[:, pj], kc2], axis=0)
            V2 = jnp.concatenate([vp_ref[:, pj], vc2], axis=0)
            Qc = _two_heads(c_ref[:, sl(0)] * scale, top)
            Qn = _two_heads(qn_ref[:, pj] * scale, top)
            DOc = _two_heads(doc_ref[:, pj].astype(MXU_DTYPE), top)
            DOn = _two_heads(don_ref[:, pj].astype(MXU_DTYPE), top)
            P_c = jnp.exp(_nt(Qc, K2) + bias - pair_col(Lc_all, j))
            dS_c = P_c * (_nt(DOc, V2) - pair_col(dc_all, j))
            P_n = jnp.exp(_nt(Qn, kc2) + bias_n - pair_col(Ln_all, j))
            dS_n = P_n * (_nt(DOn, vc2) - pair_col(dn_all, j))
            dq = _nn(dS_c.astype(MXU_DTYPE), K2)
            dq2 = jnp.where(top, dq[:B], dq[B:]) * scale
            Qk = jnp.concatenate([Qc, Qn], axis=0)
            DOk = jnp.concatenate([DOc, DOn], axis=0)
            dk2 = _tn(jnp.concatenate([dS_c[:, B:], dS_n], axis=0).astype(MXU_DTYPE), Qk)
            dv2 = _tn(jnp.concatenate([P_c[:, B:], P_n], axis=0).astype(MXU_DTYPE), DOk)
            out_ref[:, sl(0)] = (dq2 * cos_t + _partner(dq2) * sin_inv).astype(out_ref.dtype)
            out_ref[:, sl(1)] = (dk2 * cos_t + _partner(dk2) * sin_inv).astype(out_ref.dtype)
            out_ref[:, sl(2)] = dv2.astype(out_ref.dtype)

    cur = lambda b: b
    prv = lambda b: jnp.maximum(b - 1, 0)
    nxt = lambda b: jnp.minimum(b + 1, 3 * nblk - 1)
    spec = lambda w, f, part=0: pl.BlockSpec((B, w), lambda b: (f(b), part))
    return pl.pallas_call(
        body,
        name="attn_bwd",
        grid=(3 * nblk,),
        in_specs=[spec(3 * D, cur), spec(D, prv, 1), spec(D, prv, 2), spec(D, nxt, 0), spec(D, cur), spec(D, nxt),
                  spec(LANES, cur), spec(LANES, nxt), spec(LANES, cur), spec(LANES, nxt), spec(LANES, cur), spec(LANES, cur),
                  pl.BlockSpec(memory_space=pl.ANY)],
        out_specs=spec(3 * D, cur),
        out_shape=_sds((S3, 3 * D), MXU_DTYPE),
        compiler_params=_cparams(("parallel",)),
    )(P3, P3, P3, P3, do3, do3, L3, L3, delta3, delta3, cos3, sin3, dep)


def _input_grad(du, dx3):
    S, D = du.shape

    def body(du_ref, dx_ref, o_ref):
        o_ref[...] = ALPHA * du_ref[...] + dx_ref[0:S, :]
        for g, d, r, n in _class_slabs(S):
            o_ref[pl.ds(r, n, stride=d), :] += dx_ref[g * S + r * n:g * S + (r + 1) * n, :]

    return pl.pallas_call(
        body,
        name="input_grad",
        grid=(D // LANES,),
        in_specs=[pl.BlockSpec((S, LANES), lambda j: (0, j)), pl.BlockSpec((3 * S, LANES), lambda j: (0, j))],
        out_specs=pl.BlockSpec((S, LANES), lambda j: (0, j)),
        out_shape=_sds((S, D), F32),
        compiler_params=_cparams(("parallel",)),
    )(du, dx3)


def _chunk_causal(tb):
    r = lax.broadcasted_iota(jnp.int32, (tb, tb), 0)
    c = lax.broadcasted_iota(jnp.int32, (tb, tb), 1)
    return jnp.logical_and((r // HGRN_CHUNK) == (c // HGRN_CHUNK), r >= c)


def _chunk_sums(a, lower):
    C = HGRN_CHUNK
    r = lax.broadcasted_iota(jnp.int32, (C, C), 0)
    c = lax.broadcasted_iota(jnp.int32, (C, C), 1)
    tri = ((r >= c) if lower else (r <= c)).astype(BF16)
    parts = _split3(a)
    out = []
    for ci in range(a.shape[0] // C):
        rows = slice(ci * C, (ci + 1) * C)
        out.append(_nn(tri, parts[0][rows]) + _nn(tri, parts[1][rows]) + _nn(tri, parts[2][rows]))
    return jnp.concatenate(out, axis=0)


def _chunk_last(b):
    C = HGRN_CHUNK
    return jnp.concatenate([jnp.broadcast_to(b[(ci + 1) * C - 1:(ci + 1) * C, :], (C, b.shape[1]))
                            for ci in range(b.shape[0] // C)], axis=0)


def _lower_bound(lb_ref):
    l0, l1 = lb_ref[0:1, :], lb_ref[1:2, :]
    m = jnp.maximum(l0, l1)
    e0, e1 = jnp.exp(l0 - m), jnp.exp(l1 - m)
    return e1 / (e0 + e1)


def _hgrn_gates(q_raw, z, lb):
    sg = 1.0 / (1.0 + jnp.exp(-z))
    sn = 1.0 / (1.0 + jnp.exp(z))
    f = lb + (1.0 - lb) * sg
    key = (1.0 - lb) * sn
    sq = 1.0 / (1.0 + jnp.exp(-q_raw))
    return sg, sn, f, key, sq


def _hgrn_fwd(P1, lb_logits, norm_g, tb):
    S = P1.shape[0]
    D = P1.shape[1] // 3
    K = HGRN_DK
    H = D // K
    HP = H
    C = HGRN_CHUNK
    cpb = tb // C
    nt = S // tb

    def body(q_ref, f_ref, i_ref, lb_ref, g_ref, o_ref, n_ref, st_ref, state):
        t = pl.program_id(1)

        @pl.when(t == 0)
        def _():
            state[...] = jnp.zeros_like(state)

        lb_all = _lower_bound(lb_ref)
        low = _chunk_causal(tb)
        for hh in range(HP):
            lanes = slice(hh * K, (hh + 1) * K)
            q_raw, z, v = q_ref[:, lanes], f_ref[:, lanes], i_ref[:, lanes]
            sg, sn, f, key, sq = _hgrn_gates(q_raw, z, lb_all[:, lanes])
            b = _chunk_sums(jnp.log(f), lower=True)
            qd = (q_raw * sq * jnp.exp(b)).astype(MXU_DTYPE)
            kd = (key * jnp.exp(-b)).astype(MXU_DTYPE)
            kb = (key * jnp.exp(_chunk_last(b) - b)).astype(MXU_DTYPE)
            vm = v.astype(MXU_DTYPE)
            a = jnp.where(low, _nt(qd, kd), 0.0).astype(MXU_DTYPE)
            o_intra = _nn(a, vm)
            st = state[hh]
            outs = []
            for ci in range(cpb):
                rows = slice(ci * C, (ci + 1) * C)
                st_ref[hh, ci] = st
                outs.append(o_intra[rows] + _nt(qd[rows], st.astype(MXU_DTYPE)))
                st = st * jnp.exp(b[(ci + 1) * C - 1:(ci + 1) * C, :]) + _tn(vm[rows], kb[rows])
            state[hh] = st
            o = jnp.concatenate(outs, axis=0)
            o_ref[:, lanes] = o
            rs = lax.rsqrt(jnp.mean(o * o, axis=1, keepdims=True) + RMS_EPS)
            n_ref[:, lanes] = o * rs * g_ref[:, lanes]

    tok = lambda part: pl.BlockSpec((tb, HP * K), lambda h, t: (t, part * (H // HP) + h))
    vec = lambda rows: pl.BlockSpec((rows, HP * K), lambda h, t: (0, h))
    return pl.pallas_call(
        body,
        name="hgrn_fwd",
        grid=(H // HP, nt),
        in_specs=[tok(0), tok(1), tok(2), vec(2), vec(1)],
        out_specs=[tok(0), tok(0), pl.BlockSpec((HP, cpb, K, K), lambda h, t: (h, t, 0, 0))],
        out_shape=[_sds((S, D), F32), _sds((S, D), F32), _sds((H, S // C, K, K), F32)],
        scratch_shapes=[pltpu.VMEM((HP, K, K), F32)],
        compiler_params=_cparams(("parallel", "arbitrary")),
    )(P1, P1, P1, lb_logits, norm_g)


def _hgrn_bwd(P1, o_pre, states, dn, lb_logits, norm_g, tb):
    S = P1.shape[0]
    D = P1.shape[1] // 3
    K = HGRN_DK
    H = D // K
    HP = H
    C = HGRN_CHUNK
    cpb = tb // C
    nt = S // tb

    def body(q_ref, f_ref, i_ref, o_ref, st_ref, dn_ref, lb_ref, g_ref, d_ref, dg_ref, dlb_ref, dstate):
        t = pl.program_id(1)

        @pl.when(t == 0)
        def _():
            dstate[...] = jnp.zeros_like(dstate)
            dg_ref[...] = jnp.zeros_like(dg_ref)
            dlb_ref[...] = jnp.zeros_like(dlb_ref)

        lb_all = _lower_bound(lb_ref)
        low = _chunk_causal(tb)
        for hh in range(HP):
            lanes = slice(hh * K, (hh + 1) * K)
            lb = lb_all[:, lanes]
            gn = g_ref[:, lanes]
            q_raw, z, v = q_ref[:, lanes], f_ref[:, lanes], i_ref[:, lanes]
            sg, sn, f, key, sq = _hgrn_gates(q_raw, z, lb)
            b = _chunk_sums(jnp.log(f), lower=True)
            e_pos, e_neg, e_rel = jnp.exp(b), jnp.exp(-b), jnp.exp(_chunk_last(b) - b)
            qd_f, kd_f, kb_f = q_raw * sq * e_pos, key * e_neg, key * e_rel
            qd, kd, kb = qd_f.astype(MXU_DTYPE), kd_f.astype(MXU_DTYPE), kb_f.astype(MXU_DTYPE)
            vm = v.astype(MXU_DTYPE)
            a = jnp.where(low, _nt(qd, kd), 0.0).astype(MXU_DTYPE)
            o = o_ref[:, lanes]
            dnn = dn_ref[:, lanes]
            rs = lax.rsqrt(jnp.mean(o * o, axis=1, keepdims=True) + RMS_EPS)
            dg_ref[:, lanes] += jnp.sum(dnn * o * rs, axis=0, keepdims=True)
            tg = dnn * gn
            dom = (rs * tg - o * (rs * rs * rs) * jnp.mean(tg * o, axis=1, keepdims=True)).astype(MXU_DTYPE)
            da = jnp.where(low, _nt(dom, vm), 0.0).astype(MXU_DTYPE)
            dv = _tn(a, dom)
            dqd = _nn(da, kd)
            dkd = _tn(da, qd)
            dst = dstate[hh]
            dv_s, dqd_s, dkb_s, dbl_s = [None] * cpb, [None] * cpb, [None] * cpb, [None] * cpb
            for ci in reversed(range(cpb)):
                rows = slice(ci * C, (ci + 1) * C)
                st = st_ref[hh, ci]
                dstm = dst.astype(MXU_DTYPE)
                dec = jnp.exp(b[(ci + 1) * C - 1:(ci + 1) * C, :])
                dv_s[ci] = _nt(kb[rows], dstm)
                dkb_s[ci] = _nn(vm[rows], dstm)
                dqd_s[ci] = _nn(dom[rows], st.astype(MXU_DTYPE))
                db_last = jnp.sum(dkb_s[ci] * kb_f[rows], axis=0, keepdims=True) + jnp.sum(dst * st, axis=0, keepdims=True) * dec
                dbl_s[ci] = jnp.broadcast_to(db_last, (C, K))
                dst = dst * dec + _tn(dom[rows], qd[rows])
            dstate[hh] = dst
            dv = dv + jnp.concatenate(dv_s, axis=0)
            dqd = dqd + jnp.concatenate(dqd_s, axis=0)
            dkb = jnp.concatenate(dkb_s, axis=0)
            dkey = dkd * e_neg + dkb * e_rel
            db = dqd * qd_f - dkd * kd_f - dkb * kb_f
            dlogf = _chunk_sums(db, lower=False) + jnp.concatenate(dbl_s, axis=0)
            gz = (1.0 - lb) * sg * sn
            col = lambda part: slice(part * D + hh * K, part * D + (hh + 1) * K)
            d_ref[:, col(0)] = (dqd * e_pos * (sq + q_raw * sq * (1.0 - sq))).astype(d_ref.dtype)
            d_ref[:, col(1)] = (dlogf * gz / f - dkey * gz).astype(d_ref.dtype)
            d_ref[:, col(2)] = dv.astype(d_ref.dtype)
            dlb_ref[:, lanes] += jnp.sum(dlogf * sn / f - dkey * sn, axis=0, keepdims=True)

    rev = lambda t: nt - 1 - t
    tok = lambda part: pl.BlockSpec((tb, HP * K), lambda h, t: (rev(t), part * (H // HP) + h))
    vec = lambda rows: pl.BlockSpec((rows, HP * K), lambda h, t: (0, h))
    outs = pl.pallas_call(
        body,
        name="hgrn_bwd",
        grid=(H // HP, nt),
        in_specs=[tok(0), tok(1), tok(2), tok(0),
                  pl.BlockSpec((HP, cpb, K, K), lambda h, t: (h, rev(t), 0, 0)),
                  tok(0), vec(2), vec(1)],
        out_specs=[pl.BlockSpec((tb, 3 * D), lambda h, t: (rev(t), 0)), vec(1), vec(1)],
        out_shape=[_sds((S, 3 * D), MXU_DTYPE)] + [_sds((1, D), F32)] * 2,
        scratch_shapes=[pltpu.VMEM((HP, K, K), F32)],
        compiler_params=_cparams(("parallel", "arbitrary")),
    )(P1, P1, P1, o_pre, states, dn, lb_logits, norm_g)
    return outs


def _lb_logits_grad(dlb, lb_logits):
    def body(d_ref, l_ref, o_ref):
        s1 = _lower_bound(l_ref)
        d = d_ref[...]
        o_ref[0:1, :] = -(1.0 - s1) * s1 * d
        o_ref[1:2, :] = s1 * (1.0 - s1) * d

    return pl.pallas_call(body, name="lb_logits_grad", out_shape=_sds(lb_logits.shape, F32))(dlb, lb_logits)


def _mm_res_ln(name, a, w_full, res, g, b, tm, tk):
    from_ln = isinstance(res, tuple)
    S, D = (res[0] if from_ln else res).shape

    def epilogue(acc, extra_refs, out_refs, j, ci):
        g_ref, b_ref = extra_refs[:2]
        xm_ref, xhat_ref, rstd_ref = out_refs
        r = extra_refs[2][...] * extra_refs[3][...] + extra_refs[4][...] if from_ln else extra_refs[2][...]
        u = ALPHA * r + acc
        mu = jnp.mean(u, axis=1, keepdims=True)
        cen = u - mu
        rstd = lax.rsqrt(jnp.mean(cen * cen, axis=1, keepdims=True) + LN_EPS)
        xhat = cen * rstd
        xhat_ref[...] = xhat
        xm_ref[...] = (xhat * g_ref[...] + b_ref[...]).astype(xm_ref.dtype)
        rstd_ref[...] = rstd

    row = pl.BlockSpec((tm, D), lambda i, j, k: (i, 0))
    vec = pl.BlockSpec((1, D), lambda i, j, k: (0, 0))
    outs = [(_sds((S, D), MXU_DTYPE), row, True), (_sds((S, D), F32), row, True),
            (_sds((S, 1), F32), pl.BlockSpec((tm, 1), lambda i, j, k: (i, 0)), True)]
    res_extras = [(res[0], row, True), (res[1], vec), (res[2], vec)] if from_ln else [(res, row, True)]
    return _matmul(name, a, w_full, "nn", tm, D, tk, outs, epilogue, extras=[(g, vec), (b, vec)] + res_extras,
                   split=("rows", 2) if tk == a.shape[1] else None)


def _ln_bwd_rows(dy, xh, rstd, g, first, du_ref, dum_ref, dg_ref, db_ref):
    if first is not None:
        @pl.when(first)
        def _():
            dg_ref[...] = jnp.zeros_like(dg_ref)
            db_ref[...] = jnp.zeros_like(db_ref)

    dg_ref[...] += jnp.sum(dy * xh, axis=0, keepdims=True)
    db_ref[...] += jnp.sum(dy, axis=0, keepdims=True)
    dxh = dy * g
    m1 = jnp.mean(dxh, axis=1, keepdims=True)
    m2 = jnp.mean(dxh * xh, axis=1, keepdims=True)
    du = rstd * (dxh - m1 - xh * m2)
    du_ref[...] = du
    dum_ref[...] = du.astype(dum_ref.dtype)


def _loss_ln_bwd(target, xhat, rstd, g, b, tm):
    S, D = xhat.shape

    def body(t_ref, xh_ref, r_ref, g_ref, b_ref, sq_ref, du_ref, dum_ref, dg_ref, db_ref):
        first = pl.program_id(0) == 0

        @pl.when(first)
        def _():
            sq_ref[...] = jnp.zeros_like(sq_ref)

        xh = xh_ref[...]
        e = xh * g_ref[...] + b_ref[...] - t_ref[...]
        sq_ref[...] += jnp.sum(e * e, axis=0, keepdims=True)
        _ln_bwd_rows(e / D, xh, r_ref[...], g_ref[...], first, du_ref, dum_ref, dg_ref, db_ref)

    row = pl.BlockSpec((tm, D), lambda i: (i, 0))
    vec = pl.BlockSpec((1, D), lambda i: (0, 0))
    return pl.pallas_call(
        body,
        name="loss_ln_bwd",
        grid=(S // tm,),
        in_specs=[row, row, pl.BlockSpec((tm, 1), lambda i: (i, 0)), vec, vec],
        out_specs=[vec, row, row, vec, vec],
        out_shape=[_sds((1, D), F32), _sds((S, D), F32), _sds((S, D), MXU_DTYPE), _sds((1, D), F32), _sds((1, D), F32)],
        compiler_params=_cparams(("arbitrary",)),
    )(target, xhat, rstd, g, b)


def _mlp_up(name, x, w_up, tm, tn, tk):
    S = x.shape[0]
    F = w_up.shape[1]

    def epilogue(acc, extra_refs, out_refs, j, ci):
        r = jnp.maximum(acc, 0.0)
        out_refs[0][...] = (r * r).astype(out_refs[0].dtype)

    return _matmul(name, x, w_up, "nn", tm, tn, tk, [(_sds((S, F), MXU_DTYPE), _ij_spec(tm, tn), True)], epilogue,
                   split=("cols", 2))[0]


def _mlp_down_bwd(name, dy, w_down, a, tm, tn, tk):
    S, F = a.shape

    def epilogue(acc, extra_refs, out_refs, j, ci):
        out_refs[0][...] = (acc * (2.0 * jnp.sqrt(extra_refs[0][...].astype(F32)))).astype(out_refs[0].dtype)

    return _matmul(name, dy, w_down, "nt", tm, tn, tk, [(_sds((S, F), MXU_DTYPE), _ij_spec(tm, tn), True)], epilogue,
                   extras=[(a, _ij_spec(tm, tn), True)], split=("cols", 2))[0]


def _mm_nt_res_ln_bwd(name, dy, w, du, xhat, rstd, g, tm, tk, dep):
    S, D = du.shape

    def epilogue(acc, extra_refs, out_refs, j, ci):
        du_ref, xh_ref, r_ref, g_ref = extra_refs
        first = (pl.program_id(0) == 0) if ci == 0 else None
        _ln_bwd_rows(ALPHA * du_ref[...] + acc, xh_ref[...], r_ref[...], g_ref[...], first, *out_refs)

    row = pl.BlockSpec((tm, D), lambda i, j, k: (i, 0))
    vec = pl.BlockSpec((1, D), lambda i, j, k: (0, 0))
    return _matmul(name, dy, w, "nt", tm, D, tk,
                   [(_sds((S, D), F32), row, True), (_sds((S, D), MXU_DTYPE), row, True), (_sds((1, D), F32), vec),
                    (_sds((1, D), F32), vec)], epilogue,
                   extras=[(du, row, True), (xhat, row, True), (rstd, pl.BlockSpec((tm, 1), lambda i, j, k: (i, 0)), True), (g, vec)],
                   dep=dep, sem=("arbitrary", "arbitrary", "arbitrary"), split=("rows", 2))


def _attn_out_bwd(du, w_out, o, sel_t, tm, tk):
    S, D = o.shape

    def epilogue(acc, extra_refs, out_refs, j, ci):
        out_refs[0][...] = acc
        out_refs[1][...] = _exact_nn(acc * extra_refs[0][...], extra_refs[1][...])

    row = pl.BlockSpec((tm, D), lambda i, j, k: (i, 0))
    slim = pl.BlockSpec((tm, LANES), lambda i, j, k: (i, 0))
    return _matmul("attn_out_bwd", du, w_out, "nt", tm, D, tk,
                   [(_sds((S, D), F32), row, True), (_sds((S, LANES), F32), slim, True)], epilogue,
                   extras=[(o, row, True), (sel_t, pl.BlockSpec((D, LANES), lambda i, j, k: (0, 0)))], split=("rows", 2))


def _adamw(name, w, gs, m, v):
    shape = w.shape
    cols = shape[-1]
    rows = math.prod(shape[:-1])
    w2, m2, v2 = (t.reshape(rows, cols) for t in (w, m, v))
    gs2 = [g.reshape(-1, cols) for g in gs]
    ng = len(gs2)
    tr = _pick(rows // ng, (256, 128, 64, 32, 16, 8))
    per = rows // ng // tr
    c1 = 1.0 - ADAM_B1 ** ADAM_STEP
    c2 = 1.0 - ADAM_B2 ** ADAM_STEP

    def body(w_ref, m_ref, v_ref, *rest):
        g_refs, (d_ref, nm_ref, nv_ref), g_out = rest[:ng], rest[ng:ng + 3], rest[ng + 3:]
        gg = g_refs[0][...]
        if ng == 2:
            gg = jnp.where(pl.program_id(0) < per, gg, g_refs[1][...])
        g_out[0][...] = gg
        nm = ADAM_B1 * m_ref[...] + (1.0 - ADAM_B1) * gg
        nv = ADAM_B2 * v_ref[...] + (1.0 - ADAM_B2) * (gg * gg)
        nm_ref[...] = nm
        nv_ref[...] = nv
        d_ref[...] = -ADAM_LR * ((nm / c1) / (jnp.sqrt(nv / c2) + ADAM_EPS) + ADAM_WD * w_ref[...])

    blk = pl.BlockSpec((tr, cols), lambda i: (i, 0))
    g_specs = [blk] if ng == 1 else [pl.BlockSpec((tr, cols), lambda i: (jnp.minimum(i, per - 1), 0)),
                                     pl.BlockSpec((tr, cols), lambda i: (jnp.maximum(i - per, 0), 0))]
    outs = pl.pallas_call(
        body,
        name=name,
        grid=(rows // tr,),
        in_specs=[blk] * 3 + g_specs,
        out_specs=[blk] * 4,
        out_shape=[_sds((rows, cols), F32)] * 4,
        compiler_params=_cparams(("parallel",)),
    )(w2, m2, v2, *gs2)
    return tuple(o.reshape(shape) for o in outs)


HBM = pl.BlockSpec(memory_space=pl.ANY)


def _shard_slice(ref, axis, size, index):
    idx = [slice(None)] * len(ref.shape)
    idx[axis] = pl.ds(pl.multiple_of(index * size, 8), size)
    return ref.at[tuple(idx)]


def _share_halves(name, full, tr):
    R, W4 = full.shape
    W, h = W4 // 4, R // 2
    steps = [(k, t) for k in range(3) for t in range(h // tr)]

    def body(f_in, f_ref, buf, lsem, ssem, rsem):
        x, y, c = lax.axis_index("x"), lax.axis_index("y"), lax.axis_index("c")
        sibling = (x, y, 1 - c)
        chips = [(1 - x, y), (x, 1 - y), (1 - x, 1 - y)]

        def tile(k, t):
            px, py = chips[k]
            return f_ref.at[pl.ds(pl.multiple_of(c * h + t * tr, 8), tr), pl.ds(pl.multiple_of((2 * px + py) * W, LANES), W)]

        sends = []
        for s, (k, t) in enumerate(steps):
            slot = s % 2
            if s >= 2:
                sends[s - 2].wait_send()
            lc = pltpu.make_async_copy(tile(k, t), buf.at[slot], lsem.at[slot])
            lc.start()
            lc.wait()
            rc = pltpu.make_async_remote_copy(src_ref=buf.at[slot], dst_ref=tile(k, t), send_sem=ssem.at[slot], recv_sem=rsem,
                                              device_id=sibling, device_id_type=MESH)
            rc.start()
            sends.append(rc)
        for rc in sends[-2:]:
            rc.wait_send()
        whole = f_ref.at[pl.ds(0, h), pl.ds(0, 3 * W)]
        pltpu.make_async_remote_copy(src_ref=whole, dst_ref=whole, send_sem=ssem.at[0], recv_sem=rsem,
                                     device_id=sibling, device_id_type=MESH).wait_recv()

    return pl.pallas_call(
        body,
        name=name,
        in_specs=[HBM],
        out_specs=HBM,
        out_shape=_sds(full.shape, full.dtype),
        input_output_aliases={0: 0},
        scratch_shapes=[pltpu.VMEM((2, tr, W), full.dtype), pltpu.SemaphoreType.DMA((2,)), pltpu.SemaphoreType.DMA((2,)),
                        pltpu.SemaphoreType.DMA(())],
    )(full)


IN_HBM = pl.BlockSpec(memory_space=pltpu.HBM)
IN_SEM = pl.BlockSpec(memory_space=pltpu.SEMAPHORE)
DATAFLOW = pltpu.SideEffectType.DATAFLOW_SIDE_EFFECTING


def _hbm(t):
    return pltpu.with_memory_space_constraint(t, pltpu.HBM)


def _token_spec():
    return pl.BlockSpec(memory_space=pltpu.VMEM)


def _gather_copies(s_refs, f_refs, axes, halves, send, recv, loc, arrival):
    x, y, c = lax.axis_index("x"), lax.axis_index("y"), lax.axis_index("c")
    chips = [(1 - x, y), (x, 1 - y), (1 - x, 1 - y)]
    local, remote = [], []
    for a in range(len(s_refs)):
        size = s_refs[a].shape[axes[a]]
        local.append(pltpu.make_async_copy(s_refs[a], _shard_slice(f_refs[a], axes[a], size, 2 * x + y), loc.at[a]))
        for k, (px, py) in enumerate(chips):
            block = (2 * px + py) if arrival else (2 * x + y)
            src, dst = s_refs[a], _shard_slice(f_refs[a], axes[a], size, block)
            if halves:
                assert axes[a] == 1 and len(s_refs[a].shape) == 2
                h = s_refs[a].shape[0] // 2
                rows = pl.ds(pl.multiple_of(c * h, 8), h)
                src = s_refs[a].at[rows, :]
                dst = f_refs[a].at[rows, pl.ds(pl.multiple_of(block * size, LANES), size)]
            remote.append(pltpu.make_async_remote_copy(src_ref=src, dst_ref=dst, send_sem=send.at[3 * a + k],
                                                       recv_sem=recv.at[3 * a + k], device_id=(px, py, c), device_id_type=MESH))
    return local, remote


def _gather_start(name, shards, axes, after, halves=False):
    n = len(shards)
    fulls = []
    for s, ax in zip(shards, axes):
        fs = list(s.shape)
        fs[ax] *= 4
        fulls.append(lax.empty(tuple(fs), s.dtype))

    def body(*refs):
        s_refs, f_refs = refs[:n], refs[n:2 * n]
        send, recv, loc, token = refs[2 * n + 1], refs[2 * n + 2], refs[2 * n + 3], refs[-1]
        local, remote = _gather_copies(s_refs, f_refs, axes, halves, send, recv, loc, arrival=False)
        for cp in remote + local:
            cp.start()
        token[...] = jnp.zeros_like(token)

    outs = pl.pallas_call(
        body,
        name=name,
        out_shape=(pltpu.SemaphoreType.DMA((3 * n,)), pltpu.SemaphoreType.DMA((3 * n,)), pltpu.SemaphoreType.DMA((n,)),
                   *[pltpu.HBM(t.shape, t.dtype) for t in shards + fulls], _sds((8, LANES), F32)),
        in_specs=[IN_HBM] * (2 * n) + [HBM],
        out_specs=(IN_SEM, IN_SEM, IN_SEM, *[IN_HBM] * (2 * n), _token_spec()),
        input_output_aliases={i: 3 + i for i in range(2 * n)},
        compiler_params=pltpu.CompilerParams(has_side_effects=DATAFLOW),
    )(*[_hbm(t) for t in shards + fulls], after)
    return (outs[0], outs[1], outs[2], list(outs[3:3 + n]), list(outs[3 + n:3 + 2 * n]), axes, halves), outs[-1]


def _gather_wait(name, state, *after):
    send, recv, loc, s_thru, f_thru, axes, halves = state
    n = len(s_thru)

    def body(*refs):
        s_refs, f_refs = refs[:n], refs[n:2 * n]
        local, remote = _gather_copies(s_refs, f_refs, axes, halves, refs[2 * n], refs[2 * n + 1], refs[2 * n + 2], arrival=True)
        for cp in local:
            cp.wait()
        for cp in remote:
            cp.wait_send()
            cp.wait_recv()

    outs = pl.pallas_call(
        body,
        name=name,
        out_shape=tuple(pltpu.HBM(t.shape, t.dtype) for t in s_thru + f_thru),
        in_specs=[IN_HBM] * (2 * n) + [IN_SEM, IN_SEM, IN_SEM] + [HBM] * len(after),
        out_specs=tuple([IN_HBM] * (2 * n)),
        input_output_aliases={i: i for i in range(2 * n)},
        compiler_params=pltpu.CompilerParams(has_side_effects=DATAFLOW),
    )(*s_thru, *f_thru, send, recv, loc, *after)
    return list(outs[n:2 * n])


FLIPS = [(fx, fy, fc) for fx in (0, 1) for fy in (0, 1) for fc in (0, 1)][1:]


def _piece_shape(shape, axis):
    ps = list(shape)
    if axis == 0:
        ps[0] //= 8
    else:
        ps[0] //= 2
        ps[axis] //= 4
    return tuple(ps)


def _piece(ref, axis, q, c):
    shape = ref.shape
    idx = [slice(None)] * len(shape)
    if axis == 0:
        h = shape[0] // 8
        idx[0] = pl.ds(pl.multiple_of((2 * q + c) * h, 8), h)
    else:
        h, w = shape[0] // 2, shape[axis] // 4
        idx[0] = pl.ds(c * h, h)
        idx[axis] = pl.ds(pl.multiple_of(q * w, LANES if axis == len(shape) - 1 else 8), w)
    return ref.at[tuple(idx)]


def _scatter_copies(g_refs, l_refs, axes, send, recv):
    x, y, c = lax.axis_index("x"), lax.axis_index("y"), lax.axis_index("c")
    out = []
    for a in range(len(g_refs)):
        for k, (fx, fy, fc) in enumerate(FLIPS):
            tx, ty, tc = x ^ fx, y ^ fy, c ^ fc
            out.append(pltpu.make_async_remote_copy(
                src_ref=_piece(g_refs[a], axes[a], 2 * tx + ty, tc), dst_ref=l_refs[a].at[k],
                send_sem=send.at[7 * a + k], recv_sem=recv.at[7 * a + k], device_id=(tx, ty, tc), device_id_type=MESH))
    return out


def _scatter_start(name, grads, axes):
    n = len(grads)
    lands = [lax.empty((7,) + _piece_shape(g.shape, ax), g.dtype) for g, ax in zip(grads, axes)]

    def body(*refs):
        g_refs, l_refs = refs[:n], refs[n:2 * n]
        send, recv, token = refs[2 * n], refs[2 * n + 1], refs[-1]
        for cp in _scatter_copies(g_refs, l_refs, axes, send, recv):
            cp.start()
        token[...] = jnp.zeros_like(token)

    outs = pl.pallas_call(
        body,
        name=name,
        out_shape=(pltpu.SemaphoreType.DMA((7 * n,)), pltpu.SemaphoreType.DMA((7 * n,)),
                   *[pltpu.HBM(t.shape, t.dtype) for t in grads + lands], _sds((8, LANES), F32)),
        in_specs=[IN_HBM] * (2 * n),
        out_specs=(IN_SEM, IN_SEM, *[IN_HBM] * (2 * n), _token_spec()),
        input_output_aliases={i: 2 + i for i in range(2 * n)},
        compiler_params=pltpu.CompilerParams(has_side_effects=DATAFLOW),
    )(*[_hbm(t) for t in grads + lands])
    return (outs[0], outs[1], list(outs[2:2 + n]), list(outs[2 + n:2 + 2 * n]), axes), outs[-1]


def _scatter_wait(name, state, *after):
    send, recv, g_thru, l_thru, axes = state
    n = len(g_thru)

    def body(*refs):
        g_refs, l_refs = refs[:n], refs[n:2 * n]
        for cp in _scatter_copies(g_refs, l_refs, axes, refs[2 * n], refs[2 * n + 1]):
            cp.wait_send()
            cp.wait_recv()

    outs = pl.pallas_call(
        body,
        name=name,
        out_shape=tuple(pltpu.HBM(t.shape, t.dtype) for t in g_thru + l_thru),
        in_specs=[IN_HBM] * (2 * n) + [IN_SEM, IN_SEM] + [HBM] * len(after),
        out_specs=tuple([IN_HBM] * (2 * n)),
        input_output_aliases={i: i for i in range(2 * n)},
        compiler_params=pltpu.CompilerParams(has_side_effects=DATAFLOW),
    )(*g_thru, *l_thru, send, recv, *after)
    return list(outs[:n]), list(outs[n:2 * n])


def _reduce_join(name, landing, g, axis):
    R, C = _piece_shape(g.shape, axis)
    l3 = landing.reshape(7, R, C)
    tr = _pick(R, [t for t in (512, 256, 128, 64, 32, 16, 8) if t * C <= 256 * 1024])
    nsteps = R // tr

    def own_block(i):
        q, c = 2 * lax.axis_index("x") + lax.axis_index("y"), lax.axis_index("c")
        return ((2 * q + c) * nsteps + i, 0) if axis == 0 else (c * nsteps + i, q)

    def body(own_ref, l_ref, o_ref, buf, send, loc, recv):
        i = pl.program_id(0)
        x, y, c = lax.axis_index("x"), lax.axis_index("y"), lax.axis_index("c")
        sibling = (x, y, 1 - c)

        def copies(slot, step):
            dst = o_ref.at[pl.ds(pl.multiple_of(c * R + step * tr, 8), tr), :]
            return (pltpu.make_async_copy(buf.at[slot], dst, loc.at[slot]),
                    pltpu.make_async_remote_copy(src_ref=buf.at[slot], dst_ref=dst, send_sem=send.at[slot], recv_sem=recv,
                                                 device_id=sibling, device_id_type=MESH))

        @pl.when(i >= 2)
        def _():
            lc, rc = copies(i % 2, i - 2)
            lc.wait()
            rc.wait_send()

        acc = own_ref[...].astype(F32)
        for s in range(7):
            acc = acc + l_ref[s].astype(F32)
        buf[i % 2] = acc
        lc, rc = copies(i % 2, i)
        lc.start()
        rc.start()

        @pl.when(i == nsteps - 1)
        def _():
            for st in range(max(nsteps - 2, 0), nsteps):
                lc, rc = copies(st % 2, st)
                lc.wait()
                rc.wait_send()
            theirs = o_ref.at[pl.ds(pl.multiple_of((1 - c) * R, 8), R), :]
            pltpu.make_async_remote_copy(src_ref=theirs, dst_ref=theirs, send_sem=send.at[0], recv_sem=recv,
                                         device_id=sibling, device_id_type=MESH).wait_recv()

    return pl.pallas_call(
        body,
        name=name,
        grid=(nsteps,),
        in_specs=[pl.BlockSpec((tr, C), own_block), pl.BlockSpec((7, tr, C), lambda i: (0, i, 0))],
        out_specs=HBM,
        out_shape=_sds((2 * R, C), F32),
        scratch_shapes=[pltpu.VMEM((2, tr, C), F32), pltpu.SemaphoreType.DMA((2,)), pltpu.SemaphoreType.DMA((2,)),
                        pltpu.SemaphoreType.DMA(())],
        compiler_params=_cparams(("arbitrary",)),
    )(g, l3)


def _all_reduce_small(v, dep):
    R, D = v.shape

    def body(v_ref, dep_ref, o_ref, land, send, recv):
        x, y, c = lax.axis_index("x"), lax.axis_index("y"), lax.axis_index("c")
        my_slot = 4 * x + 2 * y + c
        land[my_slot] = v_ref[...]
        for k, (fx, fy, fc) in enumerate(FLIPS):
            tx, ty, tc = x ^ fx, y ^ fy, c ^ fc
            pltpu.make_async_remote_copy(src_ref=v_ref, dst_ref=land.at[my_slot], send_sem=send.at[k], recv_sem=recv.at[k],
                                         device_id=(tx, ty, tc), device_id_type=MESH).start()
        for k, (fx, fy, fc) in enumerate(FLIPS):
            tx, ty, tc = x ^ fx, y ^ fy, c ^ fc
            cp = pltpu.make_async_remote_copy(src_ref=v_ref, dst_ref=land.at[4 * tx + 2 * ty + tc], send_sem=send.at[k],
                                              recv_sem=recv.at[k], device_id=(tx, ty, tc), device_id_type=MESH)
            cp.wait_send()
            cp.wait_recv()
        acc = land[0]
        for s in range(1, 8):
            acc = acc + land[s]
        o_ref[...] = acc

    return pl.pallas_call(
        body,
        name="all_reduce_small",
        in_specs=[pl.BlockSpec(memory_space=pltpu.VMEM), pl.BlockSpec(memory_space=pl.ANY)],
        out_specs=pl.BlockSpec(memory_space=pltpu.VMEM),
        out_shape=_sds((R, D), F32),
        scratch_shapes=[pltpu.VMEM((8, R, D), F32), pltpu.SemaphoreType.DMA((7,)), pltpu.SemaphoreType.DMA((7,))],
    )(v, dep)


def kernel(x, attn_w_in, attn_w_out, hgrn_w_in, hgrn_w_out, hgrn_norm_g, lb_logits, ln_mix_g, ln_mix_b, ln_ffn_g, ln_ffn_b, ffn_w_up, ffn_w_down, loss_target, m_attn_w_in, m_attn_w_out, m_hgrn_w_in, m_hgrn_w_out, m_hgrn_norm_g, m_lb_logits, m_ln_mix_g, m_ln_mix_b, m_ln_ffn_g, m_ln_ffn_b, m_ffn_w_up, m_ffn_w_down, v_attn_w_in, v_attn_w_out, v_hgrn_w_in, v_hgrn_w_out, v_hgrn_norm_g, v_lb_logits, v_ln_mix_g, v_ln_mix_b, v_ln_ffn_g, v_ln_ffn_b, v_ffn_w_up, v_ffn_w_down):
    xs = x[0]
    tgt = loss_target[0]
    S, D = xs.shape
    F = ffn_w_up.shape[2] * 4
    T1 = _pick(S, (1024, 512, 256))
    T2 = _pick(S, (2048, 1024, 512))
    TH = _pick(S, (512, 256))
    TB = _pick(S, (128,))
    TF = _pick(F, (1024, 512))
    TG = _pick(3 * D, (1536, 1024, 768))
    TW = _pick(F, (2048, 1024))

    cast = lambda w: w.astype(MXU_DTYPE)
    st_a, tok = _gather_start("gather_a", [cast(attn_w_in[0])], [1], jnp.zeros((8, LANES), F32), halves=True)
    tok, (xs_late, w_aout, w_fup, w_fdown, w_hin, w_hout) = lax.optimization_barrier(
        (tok, (xs, attn_w_out, ffn_w_up, ffn_w_down, hgrn_w_in, hgrn_w_out)))
    st_b, tok = _gather_start("gather_b", [cast(w_aout[0]), cast(w_fup[0]), cast(w_fdown[0])], [0, 1, 0], tok)
    st_c, tok = _gather_start("gather_c", [cast(w_hin[0]), cast(w_hout[0]), hgrn_norm_g, cast(w_fup[1]), cast(w_fdown[1])],
                              [1, 0, 1, 1, 0], tok)

    cos3, sin3 = _rope_tables(S)
    sel = _head_sel(D)
    sel_t = sel.T

    xc3 = _stack_classes("x_classes", xs_late, MXU_DTYPE)
    P3 = _attn_proj("attn_proj_own", xc3, st_a[3][0], cos3, sin3, T2, None, tok)
    (wa_in,) = _gather_wait("gather_a_wait", st_a, P3)
    wa_in = _share_halves("share_a", wa_in, _pick(D // 2, (256, 128)))
    P3 = _attn_proj("attn_proj", xc3, wa_in, cos3, sin3, T2, P3)
    o3, lse3 = _attn_fwd(P3, D)
    o_att, L_att = _attn_mix(o3, lse3, sel)
    wa_out, w_up0, w_down0 = _gather_wait("gather_b_wait", st_b, L_att)
    ln1 = (ln_mix_g[0:1], ln_mix_b[0:1])
    ln2 = (ln_ffn_g[0:1], ln_ffn_b[0:1])
    ln3 = (ln_mix_g[1:2], ln_mix_b[1:2])
    ln4 = (ln_ffn_g[1:2], ln_ffn_b[1:2])
    xm1, xh1, r1 = _mm_res_ln("attn_out_ln", o_att, wa_out, xs, *ln1, TH, D)
    a0 = _mlp_up("mlp0_up", xm1, w_up0, T2, TF, D)
    xm2, xh2, r2 = _mm_res_ln("mlp0_down_ln", a0, w_down0, (xh1, *ln1), *ln2, TH, F)

    wh_in, wh_out, norm_g, w_up1, w_down1 = _gather_wait("gather_c_wait", st_c, r2)
    P1 = _plain_mm("hgrn_proj", xm2, wh_in, "nn", F32, T1, _pick(3 * D, (1024, 768, 512)), D)
    o_h, n_h, states = _hgrn_fwd(P1, lb_logits, norm_g, TB)
    xm3, xh3, r3 = _mm_res_ln("hgrn_out_ln", n_h, wh_out, (xh2, *ln2), *ln3, TH, D)
    a1 = _mlp_up("mlp1_up", xm3, w_up1, T2, TF, D)
    _, xh4, r4 = _mm_res_ln("mlp1_down_ln", a1, w_down1, (xh3, *ln3), *ln4, TH, F)

    wgrad = lambda name, a, dy, tm, tn: _plain_mm(name, a, dy, "tn", MXU_DTYPE, tm, tn, T1)
    sq, du4, dum4, dg_ffn1, db_ffn1 = _loss_ln_bwd(tgt, xh4, r4, *ln4, TH)
    dh1 = _mlp_down_bwd("mlp1_down_bwd", dum4, w_down1, a1, T2, TF, D)
    g_down1 = wgrad("g_down1", a1, dum4, TW, D)
    g_up1 = wgrad("g_up1", xm3, dh1, D, TW)
    sc_1, tok = _scatter_start("scatter_1", [g_down1, g_up1], [0, 1])
    du3, dum3, dg_mix1, db_mix1 = _mm_nt_res_ln_bwd("mlp1_up_bwd", dh1, w_up1, du4, xh3, r3, ln_mix_g[1:2], TH, F, tok)
    dn = _plain_mm("hgrn_out_bwd", dum3, wh_out, "nt", F32, T1, D, D)
    g_hout = wgrad("g_hgrn_out", n_h, dum3, D, D)
    dP1, dg_norm, dlb = _hgrn_bwd(P1, o_h, states, dn, lb_logits, norm_g, TB)
    g_hin = wgrad("g_hgrn_in", xm2, dP1, D, TG)
    d_lb_logits = _lb_logits_grad(dlb, lb_logits)
    sc_2, tok = _scatter_start("scatter_2", [g_hout, g_hin], [0, 1])

    du2, dum2, dg_ffn0, db_ffn0 = _mm_nt_res_ln_bwd("hgrn_in_bwd", dP1, wh_in, du3, xh2, r2, ln_ffn_g[0:1], TH, 3 * D, tok)
    dh0 = _mlp_down_bwd("mlp0_down_bwd", dum2, w_down0, a0, T2, TF, D)
    g_down0 = wgrad("g_down0", a0, dum2, TW, D)
    g_up0 = wgrad("g_up0", xm1, dh0, D, TW)
    sc_3, tok = _scatter_start("scatter_3", [g_down0, g_up0], [0, 1])
    du1, dum1, dg_mix0, db_mix0 = _mm_nt_res_ln_bwd("mlp0_up_bwd", dh0, w_up0, du2, xh1, r1, ln_mix_g[0:1], TH, F, tok)
    do, delta = _attn_out_bwd(dum1, wa_out, o_att, sel_t, TH, D)
    g_aout = wgrad("g_attn_out", o_att, dum1, D, D)
    sc_5, tok = _scatter_start("scatter_5", [g_aout], [0])
    dP3 = _attn_bwd(P3, _stack_classes("do_classes", do, MXU_DTYPE), _stack_classes("lse_classes", L_att, F32),
                    _stack_classes("delta_classes", delta, F32), cos3, sin3, D, tok)
    small = jnp.concatenate([d_lb_logits, dg_mix0, dg_mix1, db_mix0, db_mix1, dg_ffn0, dg_ffn1, db_ffn0, db_ffn1,
                             dg_norm, sq, jnp.zeros((4, D), F32)], axis=0)
    small = _all_reduce_small(small, dP3)
    loss = 0.5 * jnp.sum(small[11]) / D
    grp = lambda j: j // (3 * D // TG)
    g_ain = _matmul("g_attn_in", xc3, dP3, "tn", D, TG, T1, [(_sds((D, 9 * D), MXU_DTYPE), _ij_spec(D, TG))], _store_epilogue,
                    a_map=lambda i, j, k: (k + grp(j) * (S // T1), i),
                    b_map=lambda i, j, k: (k + grp(j) * (S // T1), j % (3 * D // TG)), mnk=(D, 9 * D, S), dep=small)[0]
    sc_4, tok = _scatter_start("scatter_4", [g_ain], [1])
    dxc3 = _matmul("attn_in_bwd", dP3, wa_in, "nt", T1, D, 3 * D, [(_sds((3 * S, D), F32), _ij_spec(T1, D))], _store_epilogue,
                   b_map=lambda i, j, k: (j, k + i // (S // T1)), mnk=(3 * S, D, 3 * D), dep=tok)[0]
    grad_x = _input_grad(du1, dxc3)

    def reduced(name, state, *after):
        gs, lands = _scatter_wait(name + "_wait", state, *after)
        return [_reduce_join(f"{name}_reduce_{i}", l, g, ax) for i, (l, g, ax) in enumerate(zip(lands, gs, state[4]))]

    r_down1, r_up1 = reduced("scatter_1", sc_1, grad_x)
    r_hout, r_hin = reduced("scatter_2", sc_2, r_up1)
    r_down0, r_up0 = reduced("scatter_3", sc_3, r_hin)
    (r_aout,) = reduced("scatter_5", sc_5, r_up0)

    my_chip = 2 * lax.axis_index("x") + lax.axis_index("y")
    nsh = hgrn_norm_g.shape[1]
    g_norm = lax.dynamic_slice(small[10:11], (0, my_chip * nsh), (1, nsh))

    grads, upd = {}, {}

    def update(nm, w, gs, m, v):
        upd[nm] = _adamw("adamw_" + nm, w, gs, m, v)
        grads[nm] = upd[nm][3]

    update("hgrn_w_in", hgrn_w_in, [r_hin], m_hgrn_w_in, v_hgrn_w_in)
    update("hgrn_w_out", hgrn_w_out, [r_hout], m_hgrn_w_out, v_hgrn_w_out)
    update("ffn_w_up", ffn_w_up, [r_up0, r_up1], m_ffn_w_up, v_ffn_w_up)
    update("ffn_w_down", ffn_w_down, [r_down0, r_down1], m_ffn_w_down, v_ffn_w_down)
    update("attn_w_out", attn_w_out, [r_aout], m_attn_w_out, v_attn_w_out)
    update("hgrn_norm_g", hgrn_norm_g, [g_norm], m_hgrn_norm_g, v_hgrn_norm_g)
    cat = lambda ts: jnp.concatenate(ts, axis=0)
    small_w = cat([lb_logits, ln_mix_g, ln_mix_b, ln_ffn_g, ln_ffn_b])
    small_m = cat([m_lb_logits, m_ln_mix_g, m_ln_mix_b, m_ln_ffn_g, m_ln_ffn_b])
    small_v = cat([v_lb_logits, v_ln_mix_g, v_ln_mix_b, v_ln_ffn_g, v_ln_ffn_b])
    small_upd = _adamw("adamw_small", small_w, [small[0:10]], small_m, small_v)
    for i, nm in enumerate(["lb_logits", "ln_mix_g", "ln_mix_b", "ln_ffn_g", "ln_ffn_b"]):
        grads[nm] = small[2 * i:2 * i + 2]
        upd[nm] = tuple(t[2 * i:2 * i + 2] for t in small_upd)
    done = [upd[k][2] for k in ("hgrn_w_in", "hgrn_w_out", "ffn_w_up", "ffn_w_down", "attn_w_out", "hgrn_norm_g")]
    (r_ain,) = reduced("scatter_4", sc_4, small_upd[2], *done)
    update("attn_w_in", attn_w_in, [r_ain], m_attn_w_in, v_attn_w_in)

    order = ["attn_w_in", "attn_w_out", "hgrn_w_in", "hgrn_w_out", "hgrn_norm_g", "lb_logits", "ln_mix_g", "ln_mix_b",
             "ln_ffn_g", "ln_ffn_b", "ffn_w_up", "ffn_w_down"]
    return (loss, grad_x[None], *[grads[k] for k in order], *[upd[k][0] for k in order],
            *[upd[k][1] for k in order], *[upd[k][2] for k in order])
```

```python
import math

import jax
import jax.numpy as jnp
from jax import lax
from jax.experimental import pallas as pl
from jax.experimental.pallas import tpu as pltpu

F32 = jnp.float32
BF16 = jnp.bfloat16
MXU_DTYPE = BF16

HEAD_DIM = 64
ATTN_BLK = 128
DILATIONS = (1, 4, 16)
ROPE_THETA = 10000.0
HGRN_DK = 128
HGRN_CHUNK = 64
DEPTH = 2
LN_EPS = 1e-5
RMS_EPS = 1e-6
ALPHA = (2 * DEPTH) ** 0.25
ADAM_LR, ADAM_B1, ADAM_B2, ADAM_EPS, ADAM_WD, ADAM_STEP = 0.001, 0.9, 0.999, 1e-08, 0.01, 10

LANES = 128
VMEM_LIMIT = 56 * 1024 * 1024
NEG = -1e30
MESH = pl.DeviceIdType.MESH


def _cparams(sem=None):
    return pltpu.CompilerParams(dimension_semantics=sem, vmem_limit_bytes=VMEM_LIMIT)


def _sds(shape, dtype):
    return jax.ShapeDtypeStruct(tuple(shape), dtype)


def _dg(a, b, ca, cb):
    return lax.dot_general(a, b, (((ca,), (cb,)), ((), ())), preferred_element_type=F32)


def _nn(a, b):
    return _dg(a, b, 1, 0)


def _nt(a, b):
    return _dg(a, b, 1, 1)


def _tn(a, b):
    return _dg(a, b, 0, 0)


def _split3(a):
    hi = a.astype(BF16)
    r = a - hi.astype(F32)
    mid = r.astype(BF16)
    lo = (r - mid.astype(F32)).astype(BF16)
    return hi, mid, lo


def _exact_nn(a, sel):
    hi, mid, lo = _split3(a)
    return _nn(hi, sel) + _nn(mid, sel) + _nn(lo, sel)


def _pick(n, prefs):
    for p in prefs:
        if n % p == 0:
            return p
    return n


def _matmul(name, a, b, form, tm, tn, tk, outs, epilogue, extras=(), a_map=None, b_map=None, mnk=None, dep=None,
            sem=("parallel", "parallel", "arbitrary"), split=None, alias_dep=False):
    if form == "nn":
        (M, K), N = a.shape, b.shape[1]
        a_spec = pl.BlockSpec((tm, tk), a_map or (lambda i, j, k: (i, k)))
        b_spec = pl.BlockSpec((tk, tn), b_map or (lambda i, j, k: (k, j)))
        ca, cb = 1, 0
    elif form == "nt":
        (M, K), N = a.shape, b.shape[0]
        a_spec = pl.BlockSpec((tm, tk), a_map or (lambda i, j, k: (i, k)))
        b_spec = pl.BlockSpec((tn, tk), b_map or (lambda i, j, k: (j, k)))
        ca, cb = 1, 1
    else:
        (K, M), N = a.shape, b.shape[1]
        a_spec = pl.BlockSpec((tk, tm), a_map or (lambda i, j, k: (k, i)))
        b_spec = pl.BlockSpec((tk, tn), b_map or (lambda i, j, k: (k, j)))
        ca, cb = 0, 0
    if mnk is not None:
        M, N, K = mnk
    assert M % tm == 0 and N % tn == 0 and K % tk == 0, (name, M, N, K, tm, tn, tk)
    nk = K // tk
    ne, no = len(extras), len(outs)
    deps = [] if dep is None else [dep]
    nd = len(deps)

    def body(a_ref, b_ref, *rest):
        extra_refs, out_refs = rest[:ne], rest[ne + nd:ne + nd + no]
        j = pl.program_id(1)
        if split is not None:
            kind, n = split
            assert nk == 1 and form != "tn"
            tiled = [t for _, _, *t in list(extras) + list(outs)]
            refs = list(extra_refs) + list(out_refs)
            for ci in range(n):
                if kind == "cols":
                    cs = slice(ci * (tn // n), (ci + 1) * (tn // n))
                    part = _dg(a_ref[...].astype(MXU_DTYPE), (b_ref[:, cs] if form == "nn" else b_ref[cs, :]).astype(MXU_DTYPE), ca, cb)
                    view = [r.at[:, cs] if t else r for r, t in zip(refs, tiled)]
                else:
                    rs = slice(ci * (tm // n), (ci + 1) * (tm // n))
                    part = _dg(a_ref[rs, :].astype(MXU_DTYPE), b_ref[...].astype(MXU_DTYPE), ca, cb)
                    view = [r.at[rs, :] if t else r for r, t in zip(refs, tiled)]
                epilogue(part, view[:ne], view[ne:], j, ci)
            return
        part = _dg(a_ref[...].astype(MXU_DTYPE), b_ref[...].astype(MXU_DTYPE), ca, cb)
        if nk == 1:
            epilogue(part, extra_refs, out_refs, j, 0)
            return
        acc_ref = rest[-1]
        k = pl.program_id(2)

        @pl.when(k == 0)
        def _():
            acc_ref[...] = part

        @pl.when(k > 0)
        def _():
            acc_ref[...] += part

        @pl.when(k == nk - 1)
        def _():
            epilogue(acc_ref[...], extra_refs, out_refs, j, 0)

    res = pl.pallas_call(
        body,
        name=name,
        grid=(M // tm, N // tn, nk),
        in_specs=[a_spec, b_spec] + [s for _, s, *_ in extras] + [pl.BlockSpec(memory_space=pl.ANY)] * nd,
        out_specs=[s for _, s, *_ in outs],
        out_shape=[o for o, *_ in outs],
        scratch_shapes=[pltpu.VMEM((tm, tn), F32)] if nk > 1 else [],
        input_output_aliases={2 + ne: 0} if alias_dep else {},
        compiler_params=_cparams(sem),
    )(a, b, *[e for e, *_ in extras], *deps)
    return res


def _ij_spec(tm, tn):
    return pl.BlockSpec((tm, tn), lambda i, j, k: (i, j))


def _store_epilogue(acc, extra_refs, out_refs, j, ci):
    out_refs[0][...] = acc.astype(out_refs[0].dtype)


def _plain_mm(name, a, b, form, out_dtype, tm, tn, tk):
    M = a.shape[1] if form == "tn" else a.shape[0]
    N = b.shape[0] if form == "nt" else b.shape[1]
    return _matmul(name, a, b, form, tm, tn, tk, [(_sds((M, N), out_dtype), _ij_spec(tm, tn))], _store_epilogue)[0]


def _class_slabs(S):
    assert DILATIONS[0] == 1
    return [(g, d, r, S // d) for g, d in enumerate(DILATIONS) if d > 1 for r in range(d)]


def _stack_classes(name, t, out_dtype):
    S, W = t.shape

    def body(x_ref, o_ref):
        o_ref[0:S, :] = x_ref[...].astype(out_dtype)
        for g, d, r, n in _class_slabs(S):
            o_ref[g * S + r * n:g * S + (r + 1) * n, :] = x_ref[pl.ds(r, n, stride=d), :].astype(out_dtype)

    return pl.pallas_call(
        body,
        name=name,
        grid=(W // LANES,),
        in_specs=[pl.BlockSpec((S, LANES), lambda j: (0, j))],
        out_specs=pl.BlockSpec((3 * S, LANES), lambda j: (0, j)),
        out_shape=_sds((3 * S, W), out_dtype),
        compiler_params=_cparams(("parallel",)),
    )(t)


def _rope_tables(seq):
    half = HEAD_DIM // 2
    inv = ROPE_THETA ** (-jnp.arange(half, dtype=F32) * (2.0 / HEAD_DIM))
    inv = jnp.tile(inv, LANES // half)
    pos = []
    for d in DILATIONS:
        row = jnp.arange(seq)
        pos.append((row % (seq // d)) * d + row // (seq // d))
    ang = jnp.concatenate(pos).astype(F32)[:, None] * inv[None, :]
    first = (jnp.arange(LANES) % HEAD_DIM) < half
    sin = jnp.sin(ang)
    return jnp.cos(ang), jnp.where(first[None, :], -sin, sin)


def _partner(x):
    half = HEAD_DIM // 2
    lane = lax.broadcasted_iota(jnp.int32, x.shape, 1)
    first = (lane % HEAD_DIM) < half
    return jnp.where(first, pltpu.roll(x, LANES - half, 1), pltpu.roll(x, half, 1))


def _attn_proj(name, x3, w, cos3, sin3, tm, prev, dep=None):
    S3, D = x3.shape
    S = S3 // 3
    tn = 3 * D // 4
    nrow = S // tm
    local = prev is None

    def tile(j):
        q = 2 * lax.axis_index("x") + lax.axis_index("y")
        c0 = 3 * q + j if local else j + 3 * (j >= 3 * q).astype(jnp.int32)
        return c0, c0 // 4, c0 % 4

    def epilogue(acc, extra_refs, out_refs, j, ci):
        cos_ref, sin_ref = extra_refs
        o_ref = out_refs[0]
        _, _, place = tile(j)
        width = acc.shape[1]
        assert D % width == 0
        is_rot = (place * tn + ci * width) // D < 2
        c = jnp.where(is_rot, cos_ref[...], 1.0)
        s = jnp.where(is_rot, sin_ref[...], 0.0)
        for t in range(width // LANES):
            xs = acc[:, t * LANES:(t + 1) * LANES]
            o_ref[:, t * LANES:(t + 1) * LANES] = (xs * c + _partner(xs) * s).astype(o_ref.dtype)

    rows = lambda i, j: tile(j)[1] * nrow + i
    tab = pl.BlockSpec((tm, LANES), lambda i, j, k: (rows(i, j), 0))
    out = pl.BlockSpec((tm, tn), lambda i, j, k: (rows(i, j), tile(j)[2]))
    ntiles = 3 if local else 9
    return _matmul(name, x3, w, "nn", tm, tn, D, [(_sds((S3, 3 * D), MXU_DTYPE), out, True)], epilogue,
                   extras=[(cos3, tab), (sin3, tab)], a_map=lambda i, j, k: (rows(i, j), k),
                   b_map=lambda i, j, k: (k, j if local else tile(j)[0]), mnk=(nrow * tm, ntiles * tn, D),
                   dep=dep if local else prev, alias_dep=not local, split=("cols", 3))[0]


def _head_sel(d_model):
    h = jnp.arange(LANES)[:, None]
    l = jnp.arange(d_model)[None, :]
    return (l // HEAD_DIM == h).astype(BF16)


def _class_edges(b, nblk):
    g = b // nblk
    per_class = jnp.where(g == 0, nblk // DILATIONS[0], jnp.where(g == 1, nblk // DILATIONS[1], nblk // DILATIONS[2]))
    pos = (b % nblk) % per_class
    return pos != 0, pos != per_class - 1


def _two_heads(t, top):
    zero = jnp.zeros_like(t)
    return jnp.concatenate([jnp.where(top, t, zero), jnp.where(top, zero, t)], axis=0)


def _band_mask(has_prev):
    B = ATTN_BLK
    row = lax.broadcasted_iota(jnp.int32, (2 * B, 2 * B), 0) % B
    col = lax.broadcasted_iota(jnp.int32, (2 * B, 2 * B), 1)
    in_prev = jnp.logical_and(jnp.logical_and(col < B, col >= row), has_prev)
    in_own = jnp.logical_and(col >= B, col - B <= row)
    return jnp.logical_or(in_prev, in_own)


def _attn_fwd(P3, D):
    S3 = P3.shape[0]
    B = ATTN_BLK
    nblk = S3 // 3 // B
    npairs = D // LANES
    scale = HEAD_DIM ** -0.5

    def body(q_ref, kc_ref, vc_ref, kp_ref, vp_ref, o_ref, lse_ref):
        has_prev, _ = _class_edges(pl.program_id(0), nblk)
        bias = jnp.where(_band_mask(has_prev), 0.0, NEG)
        lane = lax.broadcasted_iota(jnp.int32, (B, LANES), 1)
        top = lane < HEAD_DIM
        lse_acc = jnp.zeros((B, LANES), F32)
        for j in range(npairs):
            sl = slice(j * LANES, (j + 1) * LANES)
            Q = _two_heads(q_ref[:, sl] * scale, top)
            K2 = jnp.concatenate([kp_ref[:, sl], kc_ref[:, sl]], axis=0)
            V2 = jnp.concatenate([vp_ref[:, sl], vc_ref[:, sl]], axis=0)
            s = _nt(Q, K2) + bias
            m = jnp.max(s, axis=1, keepdims=True)
            p = jnp.exp(s - m)
            l = jnp.sum(p, axis=1, keepdims=True)
            o = _nn(p.astype(MXU_DTYPE), V2) * (1.0 / l)
            o_ref[:, sl] = jnp.where(top, o[:B], o[B:])
            lse = m + jnp.log(l)
            lse_acc = jnp.where(lane == 2 * j, lse[:B], jnp.where(lane == 2 * j + 1, lse[B:], lse_acc))
        lse_ref[...] = lse_acc

    blk = lambda part, prev: pl.BlockSpec(
        (B, D), (lambda b: (jnp.maximum(b - 1, 0), part)) if prev else (lambda b: (b, part)))
    return pl.pallas_call(
        body,
        name="attn_fwd",
        grid=(3 * nblk,),
        in_specs=[blk(0, False), blk(1, False), blk(2, False), blk(1, True), blk(2, True)],
        out_specs=[pl.BlockSpec((B, D), lambda b: (b, 0)), pl.BlockSpec((B, LANES), lambda b: (b, 0))],
        out_shape=[_sds((S3, D), F32), _sds((S3, LANES), F32)],
        compiler_params=_cparams(("parallel",)),
    )(P3, P3, P3, P3, P3)


def _attn_mix(o3, lse3, sel):
    S3, D = o3.shape
    S = S3 // 3

    def body(o3_ref, lse_ref, sel_ref, o_ref, L_ref, w_ref):
        @pl.when(pl.program_id(0) == 0)
        def _():
            w_ref[0] = lse_ref[0:S, :]
            for g, d, r, n in _class_slabs(S):
                w_ref[g, pl.ds(r, n, stride=d), :] = lse_ref[g * S + r * n:g * S + (r + 1) * n, :]
            a, b, c = w_ref[0], w_ref[1], w_ref[2]
            m = jnp.maximum(jnp.maximum(a, b), c)
            L = m + jnp.log(jnp.exp(a - m) + jnp.exp(b - m) + jnp.exp(c - m))
            L_ref[...] = L
            w_ref[0] = jnp.exp(a - L)
            w_ref[1] = jnp.exp(b - L)
            w_ref[2] = jnp.exp(c - L)

        s = sel_ref[...]
        o_ref[...] = _exact_nn(w_ref[0], s) * o3_ref[0:S, :]
        for g, d, r, n in _class_slabs(S):
            rows = pl.ds(r, n, stride=d)
            o_ref[rows, :] += _exact_nn(w_ref[g, rows, :], s) * o3_ref[g * S + r * n:g * S + (r + 1) * n, :]

    return pl.pallas_call(
        body,
        name="attn_mix",
        grid=(D // LANES,),
        in_specs=[pl.BlockSpec((S3, LANES), lambda j: (0, j)), pl.BlockSpec((S3, LANES), lambda j: (0, 0)),
                  pl.BlockSpec((LANES, LANES), lambda j: (0, j))],
        out_specs=[pl.BlockSpec((S, LANES), lambda j: (0, j)), pl.BlockSpec((S, LANES), lambda j: (0, 0))],
        out_shape=[_sds((S, D), F32), _sds((S, LANES), F32)],
        scratch_shapes=[pltpu.VMEM((3, S, LANES), F32)],
        compiler_params=_cparams(("arbitrary",)),
    )(o3, lse3, sel)


def _attn_bwd(P3, do3, L3, delta3, cos3, sin3, D, dep):
    S3 = P3.shape[0]
    B = ATTN_BLK
    nblk = S3 // 3 // B
    npairs = D // LANES
    scale = HEAD_DIM ** -0.5

    def body(c_ref, kp_ref, vp_ref, qn_ref, doc_ref, don_ref, Lc_ref, Ln_ref, dc_ref, dn_ref, cos_ref, sin_ref, dep_ref, out_ref):
        has_prev, has_next = _class_edges(pl.program_id(0), nblk)
        bias = jnp.where(_band_mask(has_prev), 0.0, NEG)
        row = lax.broadcasted_iota(jnp.int32, (2 * B, B), 0) % B
        col = lax.broadcasted_iota(jnp.int32, (2 * B, B), 1)
        bias_n = jnp.where(jnp.logical_and(col >= row, has_next), 0.0, NEG)
        lane = lax.broadcasted_iota(jnp.int32, (B, LANES), 1)
        top = lane < HEAD_DIM
        cos_t = cos_ref[...]
        sin_inv = -sin_ref[...]
        Lc_all, Ln_all, dc_all, dn_all = Lc_ref[...], Ln_ref[...], dc_ref[...], dn_ref[...]
        pair_col = lambda t, j: jnp.concatenate([t[:, 2 * j:2 * j + 1], t[:, 2 * j + 1:2 * j + 2]], axis=0)
        for j in range(npairs):
            sl = lambda part: slice(part * D + j * LANES, part * D + (j + 1) * LANES)
            pj = slice(j * LANES, (j + 1) * LANES)
            kc2, vc2 = c_ref[:, sl(1)], c_ref[:, sl(2)]
            K2 = jnp.concatenate([kp_ref[:, pj], kc2], axis=0)
            V2 = jnp.concatenate([vp_ref[:, pj], vc2], axis=0)
            Qc = _two_heads(c_ref[:, sl(0)] * scale, top)
            Qn = _two_heads(qn_ref[:, pj] * scale, top)
            DOc = _two_heads(doc_ref[:, pj].astype(MXU_DTYPE), top)
            DOn = _two_heads(don_ref[:, pj].astype(MXU_DTYPE), top)
            P_c = jnp.exp(_nt(Qc, K2) + bias - pair_col(Lc_all, j))
            dS_c = P_c * (_nt(DOc, V2) - pair_col(dc_all, j))
            P_n = jnp.exp(_nt(Qn, kc2) + bias_n - pair_col(Ln_all, j))
            dS_n = P_n * (_nt(DOn, vc2) - pair_col(dn_all, j))
            dq = _nn(dS_c.astype(MXU_DTYPE), K2)
            dq2 = jnp.where(top, dq[:B], dq[B:]) * scale
            Qk = jnp.concatenate([Qc, Qn], axis=0)
            DOk = jnp.concatenate([DOc, DOn], axis=0)
            dk2 = _tn(jnp.concatenate([dS_c[:, B:], dS_n], axis=0).astype(MXU_DTYPE), Qk)
            dv2 = _tn(jnp.concatenate([P_c[:, B:], P_n], axis=0).astype(MXU_DTYPE), DOk)
            out_ref[:, sl(0)] = (dq2 * cos_t + _partner(dq2) * sin_inv).astype(out_ref.dtype)
            out_ref[:, sl(1)] = (dk2 * cos_t + _partner(dk2) * sin_inv).astype(out_ref.dtype)
            out_ref[:, sl(2)] = dv2.astype(out_ref.dtype)

    cur = lambda b: b
    prv = lambda b: jnp.maximum(b - 1, 0)
    nxt = lambda b: jnp.minimum(b + 1, 3 * nblk - 1)
    spec = lambda w, f, part=0: pl.BlockSpec((B, w), lambda b: (f(b), part))
    return pl.pallas_call(
        body,
        name="attn_bwd",
        grid=(3 * nblk,),
        in_specs=[spec(3 * D, cur), spec(D, prv, 1), spec(D, prv, 2), spec(D, nxt, 0), spec(D, cur), spec(D, nxt),
                  spec(LANES, cur), spec(LANES, nxt), spec(LANES, cur), spec(LANES, nxt), spec(LANES, cur), spec(LANES, cur),
                  pl.BlockSpec(memory_space=pl.ANY)],
        out_specs=spec(3 * D, cur),
        out_shape=_sds((S3, 3 * D), MXU_DTYPE),
        compiler_params=_cparams(("parallel",)),
    )(P3, P3, P3, P3, do3, do3, L3, L3, delta3, delta3, cos3, sin3, dep)


def _input_grad(du, dx3):
    S, D = du.shape

    def body(du_ref, dx_ref, o_ref):
        o_ref[...] = ALPHA * du_ref[...] + dx_ref[0:S, :]
        for g, d, r, n in _class_slabs(S):
            o_ref[pl.ds(r, n, stride=d), :] += dx_ref[g * S + r * n:g * S + (r + 1) * n, :]

    return pl.pallas_call(
        body,
        name="input_grad",
        grid=(D // LANES,),
        in_specs=[pl.BlockSpec((S, LANES), lambda j: (0, j)), pl.BlockSpec((3 * S, LANES), lambda j: (0, j))],
        out_specs=pl.BlockSpec((S, LANES), lambda j: (0, j)),
        out_shape=_sds((S, D), F32),
        compiler_params=_cparams(("parallel",)),
    )(du, dx3)


def _chunk_causal(tb):
    r = lax.broadcasted_iota(jnp.int32, (tb, tb), 0)
    c = lax.broadcasted_iota(jnp.int32, (tb, tb), 1)
    return jnp.logical_and((r // HGRN_CHUNK) == (c // HGRN_CHUNK), r >= c)


def _chunk_sums(a, lower):
    C = HGRN_CHUNK
    r = lax.broadcasted_iota(jnp.int32, (C, C), 0)
    c = lax.broadcasted_iota(jnp.int32, (C, C), 1)
    tri = ((r >= c) if lower else (r <= c)).astype(BF16)
    parts = _split3(a)
    out = []
    for ci in range(a.shape[0] // C):
        rows = slice(ci * C, (ci + 1) * C)
        out.append(_nn(tri, parts[0][rows]) + _nn(tri, parts[1][rows]) + _nn(tri, parts[2][rows]))
    return jnp.concatenate(out, axis=0)


def _chunk_last(b):
    C = HGRN_CHUNK
    return jnp.concatenate([jnp.broadcast_to(b[(ci + 1) * C - 1:(ci + 1) * C, :], (C, b.shape[1]))
                            for ci in range(b.shape[0] // C)], axis=0)


def _lower_bound(lb_ref):
    l0, l1 = lb_ref[0:1, :], lb_ref[1:2, :]
    m = jnp.maximum(l0, l1)
    e0, e1 = jnp.exp(l0 - m), jnp.exp(l1 - m)
    return e1 / (e0 + e1)


def _hgrn_gates(q_raw, z, lb):
    sg = 1.0 / (1.0 + jnp.exp(-z))
    sn = 1.0 / (1.0 + jnp.exp(z))
    f = lb + (1.0 - lb) * sg
    key = (1.0 - lb) * sn
    sq = 1.0 / (1.0 + jnp.exp(-q_raw))
    return sg, sn, f, key, sq


def _hgrn_fwd(P1, lb_logits, norm_g, tb):
    S = P1.shape[0]
    D = P1.shape[1] // 3
    K = HGRN_DK
    H = D // K
    HP = H
    C = HGRN_CHUNK
    cpb = tb // C
    nt = S // tb

    def body(q_ref, f_ref, i_ref, lb_ref, g_ref, o_ref, n_ref, st_ref, state):
        t = pl.program_id(1)

        @pl.when(t == 0)
        def _():
            state[...] = jnp.zeros_like(state)

        lb_all = _lower_bound(lb_ref)
        low = _chunk_causal(tb)
        for hh in range(HP):
            lanes = slice(hh * K, (hh + 1) * K)
            q_raw, z, v = q_ref[:, lanes], f_ref[:, lanes], i_ref[:, lanes]
            sg, sn, f, key, sq = _hgrn_gates(q_raw, z, lb_all[:, lanes])
            b = _chunk_sums(jnp.log(f), lower=True)
            qd = (q_raw * sq * jnp.exp(b)).astype(MXU_DTYPE)
            kd = (key * jnp.exp(-b)).astype(MXU_DTYPE)
            kb = (key * jnp.exp(_chunk_last(b) - b)).astype(MXU_DTYPE)
            vm = v.astype(MXU_DTYPE)
            a = jnp.where(low, _nt(qd, kd), 0.0).astype(MXU_DTYPE)
            o_intra = _nn(a, vm)
            st = state[hh]
            outs = []
            for ci in range(cpb):
                rows = slice(ci * C, (ci + 1) * C)
                st_ref[hh, ci] = st
                outs.append(o_intra[rows] + _nt(qd[rows], st.astype(MXU_DTYPE)))
                st = st * jnp.exp(b[(ci + 1) * C - 1:(ci + 1) * C, :]) + _tn(vm[rows], kb[rows])
            state[hh] = st
            o = jnp.concatenate(outs, axis=0)
            o_ref[:, lanes] = o
            rs = lax.rsqrt(jnp.mean(o * o, axis=1, keepdims=True) + RMS_EPS)
            n_ref[:, lanes] = o * rs * g_ref[:, lanes]

    tok = lambda part: pl.BlockSpec((tb, HP * K), lambda h, t: (t, part * (H // HP) + h))
    vec = lambda rows: pl.BlockSpec((rows, HP * K), lambda h, t: (0, h))
    return pl.pallas_call(
        body,
        name="hgrn_fwd",
        grid=(H // HP, nt),
        in_specs=[tok(0), tok(1), tok(2), vec(2), vec(1)],
        out_specs=[tok(0), tok(0), pl.BlockSpec((HP, cpb, K, K), lambda h, t: (h, t, 0, 0))],
        out_shape=[_sds((S, D), F32), _sds((S, D), F32), _sds((H, S // C, K, K), F32)],
        scratch_shapes=[pltpu.VMEM((HP, K, K), F32)],
        compiler_params=_cparams(("parallel", "arbitrary")),
    )(P1, P1, P1, lb_logits, norm_g)


def _hgrn_bwd(P1, o_pre, states, dn, lb_logits, norm_g, tb):
    S = P1.shape[0]
    D = P1.shape[1] // 3
    K = HGRN_DK
    H = D // K
    HP = H
    C = HGRN_CHUNK
    cpb = tb // C
    nt = S // tb

    def body(q_ref, f_ref, i_ref, o_ref, st_ref, dn_ref, lb_ref, g_ref, d_ref, dg_ref, dlb_ref, dstate):
        t = pl.program_id(1)

        @pl.when(t == 0)
        def _():
            dstate[...] = jnp.zeros_like(dstate)
            dg_ref[...] = jnp.zeros_like(dg_ref)
            dlb_ref[...] = jnp.zeros_like(dlb_ref)

        lb_all = _lower_bound(lb_ref)
        low = _chunk_causal(tb)
        for hh in range(HP):
            lanes = slice(hh * K, (hh + 1) * K)
            lb = lb_all[:, lanes]
            gn = g_ref[:, lanes]
            q_raw, z, v = q_ref[:, lanes], f_ref[:, lanes], i_ref[:, lanes]
            sg, sn, f, key, sq = _hgrn_gates(q_raw, z, lb)
            b = _chunk_sums(jnp.log(f), lower=True)
            e_pos, e_neg, e_rel = jnp.exp(b), jnp.exp(-b), jnp.exp(_chunk_last(b) - b)
            qd_f, kd_f, kb_f = q_raw * sq * e_pos, key * e_neg, key * e_rel
            qd, kd, kb = qd_f.astype(MXU_DTYPE), kd_f.astype(MXU_DTYPE), kb_f.astype(MXU_DTYPE)
            vm = v.astype(MXU_DTYPE)
            a = jnp.where(low, _nt(qd, kd), 0.0).astype(MXU_DTYPE)
            o = o_ref[:, lanes]
            dnn = dn_ref[:, lanes]
            rs = lax.rsqrt(jnp.mean(o * o, axis=1, keepdims=True) + RMS_EPS)
            dg_ref[:, lanes] += jnp.sum(dnn * o * rs, axis=0, keepdims=True)
            tg = dnn * gn
            dom = (rs * tg - o * (rs * rs * rs) * jnp.mean(tg * o, axis=1, keepdims=True)).astype(MXU_DTYPE)
            da = jnp.where(low, _nt(dom, vm), 0.0).astype(MXU_DTYPE)
            dv = _tn(a, dom)
            dqd = _nn(da, kd)
            dkd = _tn(da, qd)
            dst = dstate[hh]
            dv_s, dqd_s, dkb_s, dbl_s = [None] * cpb, [None] * cpb, [None] * cpb, [None] * cpb
            for ci in reversed(range(cpb)):
                rows = slice(ci * C, (ci + 1) * C)
                st = st_ref[hh, ci]
                dstm = dst.astype(MXU_DTYPE)
                dec = jnp.exp(b[(ci + 1) * C - 1:(ci + 1) * C, :])
                dv_s[ci] = _nt(kb[rows], dstm)
                dkb_s[ci] = _nn(vm[rows], dstm)
                dqd_s[ci] = _nn(dom[rows], st.astype(MXU_DTYPE))
                db_last = jnp.sum(dkb_s[ci] * kb_f[rows], axis=0, keepdims=True) + jnp.sum(dst * st, axis=0, keepdims=True) * dec
                dbl_s[ci] = jnp.broadcast_to(db_last, (C, K))
                dst = dst * dec + _tn(dom[rows], qd[rows])
            dstate[hh] = dst
            dv = dv + jnp.concatenate(dv_s, axis=0)
            dqd = dqd + jnp.concatenate(dqd_s, axis=0)
            dkb = jnp.concatenate(dkb_s, axis=0)
            dkey = dkd * e_neg + dkb * e_rel
            db = dqd * qd_f - dkd * kd_f - dkb * kb_f
            dlogf = _chunk_sums(db, lower=False) + jnp.concatenate(dbl_s, axis=0)
            gz = (1.0 - lb) * sg * sn
            col = lambda part: slice(part * D + hh * K, part * D + (hh + 1) * K)
            d_ref[:, col(0)] = (dqd * e_pos * (sq + q_raw * sq * (1.0 - sq))).astype(d_ref.dtype)
            d_ref[:, col(1)] = (dlogf * gz / f - dkey * gz).astype(d_ref.dtype)
            d_ref[:, col(2)] = dv.astype(d_ref.dtype)
            dlb_ref[:, lanes] += jnp.sum(dlogf * sn / f - dkey * sn, axis=0, keepdims=True)

    rev = lambda t: nt - 1 - t
    tok = lambda part: pl.BlockSpec((tb, HP * K), lambda h, t: (rev(t), part * (H // HP) + h))
    vec = lambda rows: pl.BlockSpec((rows, HP * K), lambda h, t: (0, h))
    outs = pl.pallas_call(
        body,
        name="hgrn_bwd",
        grid=(H // HP, nt),
        in_specs=[tok(0), tok(1), tok(2), tok(0),
                  pl.BlockSpec((HP, cpb, K, K), lambda h, t: (h, rev(t), 0, 0)),
                  tok(0), vec(2), vec(1)],
        out_specs=[pl.BlockSpec((tb, 3 * D), lambda h, t: (rev(t), 0)), vec(1), vec(1)],
        out_shape=[_sds((S, 3 * D), MXU_DTYPE)] + [_sds((1, D), F32)] * 2,
        scratch_shapes=[pltpu.VMEM((HP, K, K), F32)],
        compiler_params=_cparams(("parallel", "arbitrary")),
    )(P1, P1, P1, o_pre, states, dn, lb_logits, norm_g)
    return outs


def _lb_logits_grad(dlb, lb_logits):
    def body(d_ref, l_ref, o_ref):
        s1 = _lower_bound(l_ref)
        d = d_ref[...]
        o_ref[0:1, :] = -(1.0 - s1) * s1 * d
        o_ref[1:2, :] = s1 * (1.0 - s1) * d

    return pl.pallas_call(body, name="lb_logits_grad", out_shape=_sds(lb_logits.shape, F32))(dlb, lb_logits)


def _mm_res_ln(name, a, w_full, res, g, b, tm, tk):
    from_ln = isinstance(res, tuple)
    S, D = (res[0] if from_ln else res).shape

    def epilogue(acc, extra_refs, out_refs, j, ci):
        g_ref, b_ref = extra_refs[:2]
        xm_ref, xhat_ref, rstd_ref = out_refs
        r = extra_refs[2][...] * extra_refs[3][...] + extra_refs[4][...] if from_ln else extra_refs[2][...]
        u = ALPHA * r + acc
        mu = jnp.mean(u, axis=1, keepdims=True)
        cen = u - mu
        rstd = lax.rsqrt(jnp.mean(cen * cen, axis=1, keepdims=True) + LN_EPS)
        xhat = cen * rstd
        xhat_ref[...] = xhat
        xm_ref[...] = (xhat * g_ref[...] + b_ref[...]).astype(xm_ref.dtype)
        rstd_ref[...] = rstd

    row = pl.BlockSpec((tm, D), lambda i, j, k: (i, 0))
    vec = pl.BlockSpec((1, D), lambda i, j, k: (0, 0))
    outs = [(_sds((S, D), MXU_DTYPE), row, True), (_sds((S, D), F32), row, True),
            (_sds((S, 1), F32), pl.BlockSpec((tm, 1), lambda i, j, k: (i, 0)), True)]
    res_extras = [(res[0], row, True), (res[1], vec), (res[2], vec)] if from_ln else [(res, row, True)]
    return _matmul(name, a, w_full, "nn", tm, D, tk, outs, epilogue, extras=[(g, vec), (b, vec)] + res_extras,
                   split=("rows", 2) if tk == a.shape[1] else None)


def _ln_bwd_rows(dy, xh, rstd, g, first, du_ref, dum_ref, dg_ref, db_ref):
    if first is not None:
        @pl.when(first)
        def _():
            dg_ref[...] = jnp.zeros_like(dg_ref)
            db_ref[...] = jnp.zeros_like(db_ref)

    dg_ref[...] += jnp.sum(dy * xh, axis=0, keepdims=True)
    db_ref[...] += jnp.sum(dy, axis=0, keepdims=True)
    dxh = dy * g
    m1 = jnp.mean(dxh, axis=1, keepdims=True)
    m2 = jnp.mean(dxh * xh, axis=1, keepdims=True)
    du = rstd * (dxh - m1 - xh * m2)
    du_ref[...] = du
    dum_ref[...] = du.astype(dum_ref.dtype)


def _loss_ln_bwd(target, xhat, rstd, g, b, tm):
    S, D = xhat.shape

    def body(t_ref, xh_ref, r_ref, g_ref, b_ref, sq_ref, du_ref, dum_ref, dg_ref, db_ref):
        first = pl.program_id(0) == 0

        @pl.when(first)
        def _():
            sq_ref[...] = jnp.zeros_like(sq_ref)

        xh = xh_ref[...]
        e = xh * g_ref[...] + b_ref[...] - t_ref[...]
        sq_ref[...] += jnp.sum(e * e, axis=0, keepdims=True)
        _ln_bwd_rows(e / D, xh, r_ref[...], g_ref[...], first, du_ref, dum_ref, dg_ref, db_ref)

    row = pl.BlockSpec((tm, D), lambda i: (i, 0))
    vec = pl.BlockSpec((1, D), lambda i: (0, 0))
    return pl.pallas_call(
        body,
        name="loss_ln_bwd",
        grid=(S // tm,),
        in_specs=[row, row, pl.BlockSpec((tm, 1), lambda i: (i, 0)), vec, vec],
        out_specs=[vec, row, row, vec, vec],
        out_shape=[_sds((1, D), F32), _sds((S, D), F32), _sds((S, D), MXU_DTYPE), _sds((1, D), F32), _sds((1, D), F32)],
        compiler_params=_cparams(("arbitrary",)),
    )(target, xhat, rstd, g, b)


def _mlp_up(name, x, w_up, tm, tn, tk):
    S = x.shape[0]
    F = w_up.shape[1]

    def epilogue(acc, extra_refs, out_refs, j, ci):
        r = jnp.maximum(acc, 0.0)
        out_refs[0][...] = (r * r).astype(out_refs[0].dtype)

    return _matmul(name, x, w_up, "nn", tm, tn, tk, [(_sds((S, F), MXU_DTYPE), _ij_spec(tm, tn), True)], epilogue,
                   split=("cols", 2))[0]


def _mlp_down_bwd(name, dy, w_down, a, tm, tn, tk):
    S, F = a.shape

    def epilogue(acc, extra_refs, out_refs, j, ci):
        out_refs[0][...] = (acc * (2.0 * jnp.sqrt(extra_refs[0][...].astype(F32)))).astype(out_refs[0].dtype)

    return _matmul(name, dy, w_down, "nt", tm, tn, tk, [(_sds((S, F), MXU_DTYPE), _ij_spec(tm, tn), True)], epilogue,
                   extras=[(a, _ij_spec(tm, tn), True)], split=("cols", 2))[0]


def _mm_nt_res_ln_bwd(name, dy, w, du, xhat, rstd, g, tm, tk, dep):
    S, D = du.shape

    def epilogue(acc, extra_refs, out_refs, j, ci):
        du_ref, xh_ref, r_ref, g_ref = extra_refs
        first = (pl.program_id(0) == 0) if ci == 0 else None
        _ln_bwd_rows(ALPHA * du_ref[...] + acc, xh_ref[...], r_ref[...], g_ref[...], first, *out_refs)

    row = pl.BlockSpec((tm, D), lambda i, j, k: (i, 0))
    vec = pl.BlockSpec((1, D), lambda i, j, k: (0, 0))
    return _matmul(name, dy, w, "nt", tm, D, tk,
                   [(_sds((S, D), F32), row, True), (_sds((S, D), MXU_DTYPE), row, True), (_sds((1, D), F32), vec),
                    (_sds((1, D), F32), vec)], epilogue,
                   extras=[(du, row, True), (xhat, row, True), (rstd, pl.BlockSpec((tm, 1), lambda i, j, k: (i, 0)), True), (g, vec)],
                   dep=dep, sem=("arbitrary", "arbitrary", "arbitrary"), split=("rows", 2))


def _attn_out_bwd(du, w_out, o, sel_t, tm, tk):
    S, D = o.shape

    def epilogue(acc, extra_refs, out_refs, j, ci):
        out_refs[0][...] = acc
        out_refs[1][...] = _exact_nn(acc * extra_refs[0][...], extra_refs[1][...])

    row = pl.BlockSpec((tm, D), lambda i, j, k: (i, 0))
    slim = pl.BlockSpec((tm, LANES), lambda i, j, k: (i, 0))
    return _matmul("attn_out_bwd", du, w_out, "nt", tm, D, tk,
                   [(_sds((S, D), F32), row, True), (_sds((S, LANES), F32), slim, True)], epilogue,
                   extras=[(o, row, True), (sel_t, pl.BlockSpec((D, LANES), lambda i, j, k: (0, 0)))], split=("rows", 2))


def _adamw(name, w, gs, m, v):
    shape = w.shape
    cols = shape[-1]
    rows = math.prod(shape[:-1])
    w2, m2, v2 = (t.reshape(rows, cols) for t in (w, m, v))
    gs2 = [g.reshape(-1, cols) for g in gs]
    ng = len(gs2)
    tr = _pick(rows // ng, (256, 128, 64, 32, 16, 8))
    per = rows // ng // tr
    c1 = 1.0 - ADAM_B1 ** ADAM_STEP
    c2 = 1.0 - ADAM_B2 ** ADAM_STEP

    def body(w_ref, m_ref, v_ref, *rest):
        g_refs, (d_ref, nm_ref, nv_ref), g_out = rest[:ng], rest[ng:ng + 3], rest[ng + 3:]
        gg = g_refs[0][...]
        if ng == 2:
            gg = jnp.where(pl.program_id(0) < per, gg, g_refs[1][...])
        g_out[0][...] = gg
        nm = ADAM_B1 * m_ref[...] + (1.0 - ADAM_B1) * gg
        nv = ADAM_B2 * v_ref[...] + (1.0 - ADAM_B2) * (gg * gg)
        nm_ref[...] = nm
        nv_ref[...] = nv
        d_ref[...] = -ADAM_LR * ((nm / c1) / (jnp.sqrt(nv / c2) + ADAM_EPS) + ADAM_WD * w_ref[...])

    blk = pl.BlockSpec((tr, cols), lambda i: (i, 0))
    g_specs = [blk] if ng == 1 else [pl.BlockSpec((tr, cols), lambda i: (jnp.minimum(i, per - 1), 0)),
                                     pl.BlockSpec((tr, cols), lambda i: (jnp.maximum(i - per, 0), 0))]
    outs = pl.pallas_call(
        body,
        name=name,
        grid=(rows // tr,),
        in_specs=[blk] * 3 + g_specs,
        out_specs=[blk] * 4,
        out_shape=[_sds((rows, cols), F32)] * 4,
        compiler_params=_cparams(("parallel",)),
    )(w2, m2, v2, *gs2)
    return tuple(o.reshape(shape) for o in outs)


HBM = pl.BlockSpec(memory_space=pl.ANY)


def _shard_slice(ref, axis, size, index):
    idx = [slice(None)] * len(ref.shape)
    idx[axis] = pl.ds(pl.multiple_of(index * size, 8), size)
    return ref.at[tuple(idx)]


def _share_halves(name, full, tr):
    R, W4 = full.shape
    W, h = W4 // 4, R // 2
    steps = [(k, t) for k in range(3) for t in range(h // tr)]

    def body(f_in, f_ref, buf, lsem, ssem, rsem):
        x, y, c = lax.axis_index("x"), lax.axis_index("y"), lax.axis_index("c")
        sibling = (x, y, 1 - c)
        chips = [(1 - x, y), (x, 1 - y), (1 - x, 1 - y)]

        def tile(k, t):
            px, py = chips[k]
            return f_ref.at[pl.ds(pl.multiple_of(c * h + t * tr, 8), tr), pl.ds(pl.multiple_of((2 * px + py) * W, LANES), W)]

        sends = []
        for s, (k, t) in enumerate(steps):
            slot = s % 2
            if s >= 2:
                sends[s - 2].wait_send()
            lc = pltpu.make_async_copy(tile(k, t), buf.at[slot], lsem.at[slot])
            lc.start()
            lc.wait()
            rc = pltpu.make_async_remote_copy(src_ref=buf.at[slot], dst_ref=tile(k, t), send_sem=ssem.at[slot], recv_sem=rsem,
                                              device_id=sibling, device_id_type=MESH)
            rc.start()
            sends.append(rc)
        for rc in sends[-2:]:
            rc.wait_send()
        whole = f_ref.at[pl.ds(0, h), pl.ds(0, 3 * W)]
        pltpu.make_async_remote_copy(src_ref=whole, dst_ref=whole, send_sem=ssem.at[0], recv_sem=rsem,
                                     device_id=sibling, device_id_type=MESH).wait_recv()

    return pl.pallas_call(
        body,
        name=name,
        in_specs=[HBM],
        out_specs=HBM,
        out_shape=_sds(full.shape, full.dtype),
        input_output_aliases={0: 0},
        scratch_shapes=[pltpu.VMEM((2, tr, W), full.dtype), pltpu.SemaphoreType.DMA((2,)), pltpu.SemaphoreType.DMA((2,)),
                        pltpu.SemaphoreType.DMA(())],
    )(full)


IN_HBM = pl.BlockSpec(memory_space=pltpu.HBM)
IN_SEM = pl.BlockSpec(memory_space=pltpu.SEMAPHORE)
DATAFLOW = pltpu.SideEffectType.DATAFLOW_SIDE_EFFECTING


def _hbm(t):
    return pltpu.with_memory_space_constraint(t, pltpu.HBM)


def _token_spec():
    return pl.BlockSpec(memory_space=pltpu.VMEM)


def _gather_copies(s_refs, f_refs, axes, halves, send, recv, loc, arrival):
    x, y, c = lax.axis_index("x"), lax.axis_index("y"), lax.axis_index("c")
    chips = [(1 - x, y), (x, 1 - y), (1 - x, 1 - y)]
    local, remote = [], []
    for a in range(len(s_refs)):
        size = s_refs[a].shape[axes[a]]
        local.append(pltpu.make_async_copy(s_refs[a], _shard_slice(f_refs[a], axes[a], size, 2 * x + y), loc.at[a]))
        for k, (px, py) in enumerate(chips):
            block = (2 * px + py) if arrival else (2 * x + y)
            src, dst = s_refs[a], _shard_slice(f_refs[a], axes[a], size, block)
            if halves:
                assert axes[a] == 1 and len(s_refs[a].shape) == 2
                h = s_refs[a].shape[0] // 2
                rows = pl.ds(pl.multiple_of(c * h, 8), h)
                src = s_refs[a].at[rows, :]
                dst = f_refs[a].at[rows, pl.ds(pl.multiple_of(block * size, LANES), size)]
            remote.append(pltpu.make_async_remote_copy(src_ref=src, dst_ref=dst, send_sem=send.at[3 * a + k],
                                                       recv_sem=recv.at[3 * a + k], device_id=(px, py, c), device_id_type=MESH))
    return local, remote


def _gather_start(name, shards, axes, after, halves=False):
    n = len(shards)
    fulls = []
    for s, ax in zip(shards, axes):
        fs = list(s.shape)
        fs[ax] *= 4
        fulls.append(lax.empty(tuple(fs), s.dtype))

    def body(*refs):
        s_refs, f_refs = refs[:n], refs[n:2 * n]
        send, recv, loc, token = refs[2 * n + 1], refs[2 * n + 2], refs[2 * n + 3], refs[-1]
        local, remote = _gather_copies(s_refs, f_refs, axes, halves, send, recv, loc, arrival=False)
        for cp in remote + local:
            cp.start()
        token[...] = jnp.zeros_like(token)

    outs = pl.pallas_call(
        body,
        name=name,
        out_shape=(pltpu.SemaphoreType.DMA((3 * n,)), pltpu.SemaphoreType.DMA((3 * n,)), pltpu.SemaphoreType.DMA((n,)),
                   *[pltpu.HBM(t.shape, t.dtype) for t in shards + fulls], _sds((8, LANES), F32)),
        in_specs=[IN_HBM] * (2 * n) + [HBM],
        out_specs=(IN_SEM, IN_SEM, IN_SEM, *[IN_HBM] * (2 * n), _token_spec()),
        input_output_aliases={i: 3 + i for i in range(2 * n)},
        compiler_params=pltpu.CompilerParams(has_side_effects=DATAFLOW),
    )(*[_hbm(t) for t in shards + fulls], after)
    return (outs[0], outs[1], outs[2], list(outs[3:3 + n]), list(outs[3 + n:3 + 2 * n]), axes, halves), outs[-1]


def _gather_wait(name, state, *after):
    send, recv, loc, s_thru, f_thru, axes, halves = state
    n = len(s_thru)

    def body(*refs):
        s_refs, f_refs = refs[:n], refs[n:2 * n]
        local, remote = _gather_copies(s_refs, f_refs, axes, halves, refs[2 * n], refs[2 * n + 1], refs[2 * n + 2], arrival=True)
        for cp in local:
            cp.wait()
        for cp in remote:
            cp.wait_send()
            cp.wait_recv()

    outs = pl.pallas_call(
        body,
        name=name,
        out_shape=tuple(pltpu.HBM(t.shape, t.dtype) for t in s_thru + f_thru),
        in_specs=[IN_HBM] * (2 * n) + [IN_SEM, IN_SEM, IN_SEM] + [HBM] * len(after),
        out_specs=tuple([IN_HBM] * (2 * n)),
        input_output_aliases={i: i for i in range(2 * n)},
        compiler_params=pltpu.CompilerParams(has_side_effects=DATAFLOW),
    )(*s_thru, *f_thru, send, recv, loc, *after)
    return list(outs[n:2 * n])


FLIPS = [(fx, fy, fc) for fx in (0, 1) for fy in (0, 1) for fc in (0, 1)][1:]


def _piece_shape(shape, axis):
    ps = list(shape)
    if axis == 0:
        ps[0] //= 8
    else:
        ps[0] //= 2
        ps[axis] //= 4
    return tuple(ps)


def _piece(ref, axis, q, c):
    shape = ref.shape
    idx = [slice(None)] * len(shape)
    if axis == 0:
        h = shape[0] // 8
        idx[0] = pl.ds(pl.multiple_of((2 * q + c) * h, 8), h)
    else:
        h, w = shape[0] // 2, shape[axis] // 4
        idx[0] = pl.ds(c * h, h)
        idx[axis] = pl.ds(pl.multiple_of(q * w, LANES if axis == len(shape) - 1 else 8), w)
    return ref.at[tuple(idx)]


def _scatter_copies(g_refs, l_refs, axes, send, recv):
    x, y, c = lax.axis_index("x"), lax.axis_index("y"), lax.axis_index("c")
    out = []
    for a in range(len(g_refs)):
        for k, (fx, fy, fc) in enumerate(FLIPS):
            tx, ty, tc = x ^ fx, y ^ fy, c ^ fc
            out.append(pltpu.make_async_remote_copy(
                src_ref=_piece(g_refs[a], axes[a], 2 * tx + ty, tc), dst_ref=l_refs[a].at[k],
                send_sem=send.at[7 * a + k], recv_sem=recv.at[7 * a + k], device_id=(tx, ty, tc), device_id_type=MESH))
    return out


def _scatter_start(name, grads, axes):
    n = len(grads)
    lands = [lax.empty((7,) + _piece_shape(g.shape, ax), g.dtype) for g, ax in zip(grads, axes)]

    def body(*refs):
        g_refs, l_refs = refs[:n], refs[n:2 * n]
        send, recv, token = refs[2 * n], refs[2 * n + 1], refs[-1]
        for cp in _scatter_copies(g_refs, l_refs, axes, send, recv):
            cp.start()
        token[...] = jnp.zeros_like(token)

    outs = pl.pallas_call(
        body,
        name=name,
        out_shape=(pltpu.SemaphoreType.DMA((7 * n,)), pltpu.SemaphoreType.DMA((7 * n,)),
                   *[pltpu.HBM(t.shape, t.dtype) for t in grads + lands], _sds((8, LANES), F32)),
        in_specs=[IN_HBM] * (2 * n),
        out_specs=(IN_SEM, IN_SEM, *[IN_HBM] * (2 * n), _token_spec()),
        input_output_aliases={i: 2 + i for i in range(2 * n)},
        compiler_params=pltpu.CompilerParams(has_side_effects=DATAFLOW),
    )(*[_hbm(t) for t in grads + lands])
    return (outs[0], outs[1], list(outs[2:2 + n]), list(outs[2 + n:2 + 2 * n]), axes), outs[-1]


def _scatter_wait(name, state, *after):
    send, recv, g_thru, l_thru, axes = state
    n = len(g_thru)

    def body(*refs):
        g_refs, l_refs = refs[:n], refs[n:2 * n]
        for cp in _scatter_copies(g_refs, l_refs, axes, refs[2 * n], refs[2 * n + 1]):
            cp.wait_send()
            cp.wait_recv()

    outs = pl.pallas_call(
        body,
        name=name,
        out_shape=tuple(pltpu.HBM(t.shape, t.dtype) for t in g_thru + l_thru),
        in_specs=[IN_HBM] * (2 * n) + [IN_SEM, IN_SEM] + [HBM] * len(after),
        out_specs=tuple([IN_HBM] * (2 * n)),
        input_output_aliases={i: i for i in range(2 * n)},
        compiler_params=pltpu.CompilerParams(has_side_effects=DATAFLOW),
    )(*g_thru, *l_thru, send, recv, *after)
    return list(outs[:n]), list(outs[n:2 * n])


def _reduce_join(name, landing, g, axis):
    R, C = _piece_shape(g.shape, axis)
    l3 = landing.reshape(7, R, C)
    tr = _pick(R, [t for t in (512, 256, 128, 64, 32, 16, 8) if t * C <= 256 * 1024])
    nsteps = R // tr

    def own_block(i):
        q, c = 2 * lax.axis_index("x") + lax.axis_index("y"), lax.axis_index("c")
        return ((2 * q + c) * nsteps + i, 0) if axis == 0 else (c * nsteps + i, q)

    def body(own_ref, l_ref, o_ref, buf, send, loc, recv):
        i = pl.program_id(0)
        x, y, c = lax.axis_index("x"), lax.axis_index("y"), lax.axis_index("c")
        sibling = (x, y, 1 - c)

        def copies(slot, step):
            dst = o_ref.at[pl.ds(pl.multiple_of(c * R + step * tr, 8), tr), :]
            return (pltpu.make_async_copy(buf.at[slot], dst, loc.at[slot]),
                    pltpu.make_async_remote_copy(src_ref=buf.at[slot], dst_ref=dst, send_sem=send.at[slot], recv_sem=recv,
                                                 device_id=sibling, device_id_type=MESH))

        @pl.when(i >= 2)
        def _():
            lc, rc = copies(i % 2, i - 2)
            lc.wait()
            rc.wait_send()

        acc = own_ref[...].astype(F32)
        for s in range(7):
            acc = acc + l_ref[s].astype(F32)
        buf[i % 2] = acc
        lc, rc = copies(i % 2, i)
        lc.start()
        rc.start()

        @pl.when(i == nsteps - 1)
        def _():
            for st in range(max(nsteps - 2, 0), nsteps):
                lc, rc = copies(st % 2, st)
                lc.wait()
                rc.wait_send()
            theirs = o_ref.at[pl.ds(pl.multiple_of((1 - c) * R, 8), R), :]
            pltpu.make_async_remote_copy(src_ref=theirs, dst_ref=theirs, send_sem=send.at[0], recv_sem=recv,
                                         device_id=sibling, device_id_type=MESH).wait_recv()

    return pl.pallas_call(
        body,
        name=name,
        grid=(nsteps,),
        in_specs=[pl.BlockSpec((tr, C), own_block), pl.BlockSpec((7, tr, C), lambda i: (0, i, 0))],
        out_specs=HBM,
        out_shape=_sds((2 * R, C), F32),
        scratch_shapes=[pltpu.VMEM((2, tr, C), F32), pltpu.SemaphoreType.DMA((2,)), pltpu.SemaphoreType.DMA((2,)),
                        pltpu.SemaphoreType.DMA(())],
        compiler_params=_cparams(("arbitrary",)),
    )(g, l3)


def _all_reduce_small(v, dep):
    R, D = v.shape

    def body(v_ref, dep_ref, o_ref, land, send, recv):
        x, y, c = lax.axis_index("x"), lax.axis_index("y"), lax.axis_index("c")
        my_slot = 4 * x + 2 * y + c
        land[my_slot] = v_ref[...]
        for k, (fx, fy, fc) in enumerate(FLIPS):
            tx, ty, tc = x ^ fx, y ^ fy, c ^ fc
            pltpu.make_async_remote_copy(src_ref=v_ref, dst_ref=land.at[my_slot], send_sem=send.at[k], recv_sem=recv.at[k],
                                         device_id=(tx, ty, tc), device_id_type=MESH).start()
        for k, (fx, fy, fc) in enumerate(FLIPS):
            tx, ty, tc = x ^ fx, y ^ fy, c ^ fc
            cp = pltpu.make_async_remote_copy(src_ref=v_ref, dst_ref=land.at[4 * tx + 2 * ty + tc], send_sem=send.at[k],
                                              recv_sem=recv.at[k], device_id=(tx, ty, tc), device_id_type=MESH)
            cp.wait_send()
            cp.wait_recv()
        acc = land[0]
        for s in range(1, 8):
            acc = acc + land[s]
        o_ref[...] = acc

    return pl.pallas_call(
        body,
        name="all_reduce_small",
        in_specs=[pl.BlockSpec(memory_space=pltpu.VMEM), pl.BlockSpec(memory_space=pl.ANY)],
        out_specs=pl.BlockSpec(memory_space=pltpu.VMEM),
        out_shape=_sds((R, D), F32),
        scratch_shapes=[pltpu.VMEM((8, R, D), F32), pltpu.SemaphoreType.DMA((7,)), pltpu.SemaphoreType.DMA((7,))],
    )(v, dep)


def kernel(x, attn_w_in, attn_w_out, hgrn_w_in, hgrn_w_out, hgrn_norm_g, lb_logits, ln_mix_g, ln_mix_b, ln_ffn_g, ln_ffn_b, ffn_w_up, ffn_w_down, loss_target, m_attn_w_in, m_attn_w_out, m_hgrn_w_in, m_hgrn_w_out, m_hgrn_norm_g, m_lb_logits, m_ln_mix_g, m_ln_mix_b, m_ln_ffn_g, m_ln_ffn_b, m_ffn_w_up, m_ffn_w_down, v_attn_w_in, v_attn_w_out, v_hgrn_w_in, v_hgrn_w_out, v_hgrn_norm_g, v_lb_logits, v_ln_mix_g, v_ln_mix_b, v_ln_ffn_g, v_ln_ffn_b, v_ffn_w_up, v_ffn_w_down):
    xs = x[0]
    tgt = loss_target[0]
    S, D = xs.shape
    F = ffn_w_up.shape[2] * 4
    T1 = _pick(S, (1024, 512, 256))
    T2 = _pick(S, (2048, 1024, 512))
    TH = _pick(S, (512, 256))
    TB = _pick(S, (128,))
    TF = _pick(F, (1024, 512))
    TG = _pick(3 * D, (1536, 1024, 768))
    TW = _pick(F, (2048, 1024))

    cast = lambda w: w.astype(MXU_DTYPE)
    st_a, tok = _gather_start("gather_a", [cast(attn_w_in[0])], [1], jnp.zeros((8, LANES), F32), halves=True)
    tok, (xs_late, w_aout, w_fup, w_fdown, w_hin, w_hout) = lax.optimization_barrier(
        (tok, (xs, attn_w_out, ffn_w_up, ffn_w_down, hgrn_w_in, hgrn_w_out)))
    st_b, tok = _gather_start("gather_b", [cast(w_aout[0]), cast(w_fup[0]), cast(w_fdown[0])], [0, 1, 0], tok)
    st_c, tok = _gather_start("gather_c", [cast(w_hin[0]), cast(w_hout[0]), hgrn_norm_g, cast(w_fup[1]), cast(w_fdown[1])],
                              [1, 0, 1, 1, 0], tok)

    cos3, sin3 = _rope_tables(S)
    sel = _head_sel(D)
    sel_t = sel.T

    xc3 = _stack_classes("x_classes", xs_late, MXU_DTYPE)
    P3 = _attn_proj("attn_proj_own", xc3, st_a[3][0], cos3, sin3, T2, None, tok)
    (wa_in,) = _gather_wait("gather_a_wait", st_a, P3)
    wa_in = _share_halves("share_a", wa_in, _pick(D // 2, (256, 128)))
    P3 = _attn_proj("attn_proj", xc3, wa_in, cos3, sin3, T2, P3)
    o3, lse3 = _attn_fwd(P3, D)
    o_att, L_att = _attn_mix(o3, lse3, sel)
    wa_out, w_up0, w_down0 = _gather_wait("gather_b_wait", st_b, L_att)
    ln1 = (ln_mix_g[0:1], ln_mix_b[0:1])
    ln2 = (ln_ffn_g[0:1], ln_ffn_b[0:1])
    ln3 = (ln_mix_g[1:2], ln_mix_b[1:2])
    ln4 = (ln_ffn_g[1:2], ln_ffn_b[1:2])
    xm1, xh1, r1 = _mm_res_ln("attn_out_ln", o_att, wa_out, xs, *ln1, TH, D)
    a0 = _mlp_up("mlp0_up", xm1, w_up0, T2, TF, D)
    xm2, xh2, r2 = _mm_res_ln("mlp0_down_ln", a0, w_down0, (xh1, *ln1), *ln2, TH, F)

    wh_in, wh_out, norm_g, w_up1, w_down1 = _gather_wait("gather_c_wait", st_c, r2)
    P1 = _plain_mm("hgrn_proj", xm2, wh_in, "nn", F32, T1, _pick(3 * D, (1024, 768, 512)), D)
    o_h, n_h, states = _hgrn_fwd(P1, lb_logits, norm_g, TB)
    xm3, xh3, r3 = _mm_res_ln("hgrn_out_ln", n_h, wh_out, (xh2, *ln2), *ln3, TH, D)
    a1 = _mlp_up("mlp1_up", xm3, w_up1, T2, TF, D)
    _, xh4, r4 = _mm_res_ln("mlp1_down_ln", a1, w_down1, (xh3, *ln3), *ln4, TH, F)

    wgrad = lambda name, a, dy, tm, tn, tk=T1: _plain_mm(name, a, dy, "tn", MXU_DTYPE, tm, tn, tk)
    sq, du4, dum4, dg_ffn1, db_ffn1 = _loss_ln_bwd(tgt, xh4, r4, *ln4, TH)
    dh1 = _mlp_down_bwd("mlp1_down_bwd", dum4, w_down1, a1, T2, TF, D)
    g_down1 = wgrad("g_down1", a1, dum4, TF, D, S)
    g_up1 = wgrad("g_up1", xm3, dh1, D, TF, S)
    sc_1, tok = _scatter_start("scatter_1", [g_down1, g_up1], [0, 1])
    du3, dum3, dg_mix1, db_mix1 = _mm_nt_res_ln_bwd("mlp1_up_bwd", dh1, w_up1, du4, xh3, r3, ln_mix_g[1:2], TH, F, tok)
    dn = _plain_mm("hgrn_out_bwd", dum3, wh_out, "nt", F32, T1, D, D)
    g_hout = wgrad("g_hgrn_out", n_h, dum3, D, D)
    dP1, dg_norm, dlb = _hgrn_bwd(P1, o_h, states, dn, lb_logits, norm_g, TB)
    g_hin = wgrad("g_hgrn_in", xm2, dP1, D, D, S)
    d_lb_logits = _lb_logits_grad(dlb, lb_logits)
    sc_2, tok = _scatter_start("scatter_2", [g_hout, g_hin], [0, 1])

    du2, dum2, dg_ffn0, db_ffn0 = _mm_nt_res_ln_bwd("hgrn_in_bwd", dP1, wh_in, du3, xh2, r2, ln_ffn_g[0:1], TH, 3 * D, tok)
    dh0 = _mlp_down_bwd("mlp0_down_bwd", dum2, w_down0, a0, T2, TF, D)
    g_down0 = wgrad("g_down0", a0, dum2, TF, D, S)
    g_up0 = wgrad("g_up0", xm1, dh0, D, TF, S)
    sc_3, tok = _scatter_start("scatter_3", [g_down0, g_up0], [0, 1])
    du1, dum1, dg_mix0, db_mix0 = _mm_nt_res_ln_bwd("mlp0_up_bwd", dh0, w_up0, du2, xh1, r1, ln_mix_g[0:1], TH, F, tok)
    do, delta = _attn_out_bwd(dum1, wa_out, o_att, sel_t, TH, D)
    g_aout = wgrad("g_attn_out", o_att, dum1, D, D)
    sc_5, tok = _scatter_start("scatter_5", [g_aout], [0])
    dP3 = _attn_bwd(P3, _stack_classes("do_classes", do, MXU_DTYPE), _stack_classes("lse_classes", L_att, F32),
                    _stack_classes("delta_classes", delta, F32), cos3, sin3, D, tok)
    small = jnp.concatenate([d_lb_logits, dg_mix0, dg_mix1, db_mix0, db_mix1, dg_ffn0, dg_ffn1, db_ffn0, db_ffn1,
                             dg_norm, sq, jnp.zeros((4, D), F32)], axis=0)
    small = _all_reduce_small(small, dP3)
    loss = 0.5 * jnp.sum(small[11]) / D
    grp = lambda j: j // (3 * D // TG)
    g_ain = _matmul("g_attn_in", xc3, dP3, "tn", D, TG, T1, [(_sds((D, 9 * D), MXU_DTYPE), _ij_spec(D, TG))], _store_epilogue,
                    a_map=lambda i, j, k: (k + grp(j) * (S // T1), i),
                    b_map=lambda i, j, k: (k + grp(j) * (S // T1), j % (3 * D // TG)), mnk=(D, 9 * D, S), dep=small)[0]
    sc_4, tok = _scatter_start("scatter_4", [g_ain], [1])
    dxc3 = _matmul("attn_in_bwd", dP3, wa_in, "nt", T1, D, 3 * D, [(_sds((3 * S, D), F32), _ij_spec(T1, D))], _store_epilogue,
                   b_map=lambda i, j, k: (j, k + i // (S // T1)), mnk=(3 * S, D, 3 * D), dep=tok)[0]
    grad_x = _input_grad(du1, dxc3)

    def reduced(name, state, *after):
        gs, lands = _scatter_wait(name + "_wait", state, *after)
        return [_reduce_join(f"{name}_reduce_{i}", l, g, ax) for i, (l, g, ax) in enumerate(zip(lands, gs, state[4]))]

    r_down1, r_up1 = reduced("scatter_1", sc_1, grad_x)
    r_hout, r_hin = reduced("scatter_2", sc_2, r_up1)
    r_down0, r_up0 = reduced("scatter_3", sc_3, r_hin)
    (r_aout,) = reduced("scatter_5", sc_5, r_up0)

    my_chip = 2 * lax.axis_index("x") + lax.axis_index("y")
    nsh = hgrn_norm_g.shape[1]
    g_norm = lax.dynamic_slice(small[10:11], (0, my_chip * nsh), (1, nsh))

    grads, upd = {}, {}

    def update(nm, w, gs, m, v):
        upd[nm] = _adamw("adamw_" + nm, w, gs, m, v)
        grads[nm] = upd[nm][3]

    update("hgrn_w_in", hgrn_w_in, [r_hin], m_hgrn_w_in, v_hgrn_w_in)
    update("hgrn_w_out", hgrn_w_out, [r_hout], m_hgrn_w_out, v_hgrn_w_out)
    update("ffn_w_up", ffn_w_up, [r_up0, r_up1], m_ffn_w_up, v_ffn_w_up)
    update("ffn_w_down", ffn_w_down, [r_down0, r_down1], m_ffn_w_down, v_ffn_w_down)
    update("attn_w_out", attn_w_out, [r_aout], m_attn_w_out, v_attn_w_out)
    update("hgrn_norm_g", hgrn_norm_g, [g_norm], m_hgrn_norm_g, v_hgrn_norm_g)
    cat = lambda ts: jnp.concatenate(ts, axis=0)
    small_w = cat([lb_logits, ln_mix_g, ln_mix_b, ln_ffn_g, ln_ffn_b])
    small_m = cat([m_lb_logits, m_ln_mix_g, m_ln_mix_b, m_ln_ffn_g, m_ln_ffn_b])
    small_v = cat([v_lb_logits, v_ln_mix_g, v_ln_mix_b, v_ln_ffn_g, v_ln_ffn_b])
    small_upd = _adamw("adamw_small", small_w, [small[0:10]], small_m, small_v)
    for i, nm in enumerate(["lb_logits", "ln_mix_g", "ln_mix_b", "ln_ffn_g", "ln_ffn_b"]):
        grads[nm] = small[2 * i:2 * i + 2]
        upd[nm] = tuple(t[2 * i:2 * i + 2] for t in small_upd)
    done = [upd[k][2] for k in ("hgrn_w_in", "hgrn_w_out", "ffn_w_up", "ffn_w_down", "attn_w_out", "hgrn_norm_g")]
    (r_ain,) = reduced("scatter_4", sc_4, small_upd[2], *done)
    update("attn_w_in", attn_w_in, [r_ain], m_attn_w_in, v_attn_w_in)

    order = ["attn_w_in", "attn_w_out", "hgrn_w_in", "hgrn_w_out", "hgrn_norm_g", "lb_logits", "ln_mix_g", "ln_mix_b",
             "ln_ffn_g", "ln_ffn_b", "ffn_w_up", "ffn_w_down"]
    return (loss, grad_x[None], *[grads[k] for k in order], *[upd[k][0] for k in order],
            *[upd[k][1] for k in order], *[upd[k][2] for k in order])
```

```python
import math

import jax
import jax.numpy as jnp
from jax import lax
from jax.experimental import pallas as pl
from jax.experimental.pallas import tpu as pltpu

F32 = jnp.float32
BF16 = jnp.bfloat16
MXU_DTYPE = BF16

HEAD_DIM = 64
ATTN_BLK = 128
DILATIONS = (1, 4, 16)
ROPE_THETA = 10000.0
HGRN_DK = 128
HGRN_CHUNK = 64
DEPTH = 2
LN_EPS = 1e-5
RMS_EPS = 1e-6
ALPHA = (2 * DEPTH) ** 0.25
ADAM_LR, ADAM_B1, ADAM_B2, ADAM_EPS, ADAM_WD, ADAM_STEP = 0.001, 0.9, 0.999, 1e-08, 0.01, 10

LANES = 128
VMEM_LIMIT = 56 * 1024 * 1024
NEG = -1e30
MESH = pl.DeviceIdType.MESH


def _cparams(sem=None):
    return pltpu.CompilerParams(dimension_semantics=sem, vmem_limit_bytes=VMEM_LIMIT)


def _sds(shape, dtype):
    return jax.ShapeDtypeStruct(tuple(shape), dtype)


def _dg(a, b, ca, cb):
    return lax.dot_general(a, b, (((ca,), (cb,)), ((), ())), preferred_element_type=F32)


def _nn(a, b):
    return _dg(a, b, 1, 0)


def _nt(a, b):
    return _dg(a, b, 1, 1)


def _tn(a, b):
    return _dg(a, b, 0, 0)


def _split3(a):
    hi = a.astype(BF16)
    r = a - hi.astype(F32)
    mid = r.astype(BF16)
    lo = (r - mid.astype(F32)).astype(BF16)
    return hi, mid, lo


def _exact_nn(a, sel):
    hi, mid, lo = _split3(a)
    return _nn(hi, sel) + _nn(mid, sel) + _nn(lo, sel)


def _pick(n, prefs):
    for p in prefs:
        if n % p == 0:
            return p
    return n


def _matmul(name, a, b, form, tm, tn, tk, outs, epilogue, extras=(), a_map=None, b_map=None, mnk=None, dep=None,
            sem=("parallel", "parallel", "arbitrary"), split=None, alias_dep=False):
    if form == "nn":
        (M, K), N = a.shape, b.shape[1]
        a_spec = pl.BlockSpec((tm, tk), a_map or (lambda i, j, k: (i, k)))
        b_spec = pl.BlockSpec((tk, tn), b_map or (lambda i, j, k: (k, j)))
        ca, cb = 1, 0
    elif form == "nt":
        (M, K), N = a.shape, b.shape[0]
        a_spec = pl.BlockSpec((tm, tk), a_map or (lambda i, j, k: (i, k)))
        b_spec = pl.BlockSpec((tn, tk), b_map or (lambda i, j, k: (j, k)))
        ca, cb = 1, 1
    else:
        (K, M), N = a.shape, b.shape[1]
        a_spec = pl.BlockSpec((tk, tm), a_map or (lambda i, j, k: (k, i)))
        b_spec = pl.BlockSpec((tk, tn), b_map or (lambda i, j, k: (k, j)))
        ca, cb = 0, 0
    if mnk is not None:
        M, N, K = mnk
    assert M % tm == 0 and N % tn == 0 and K % tk == 0, (name, M, N, K, tm, tn, tk)
    nk = K // tk
    ne, no = len(extras), len(outs)
    deps = [] if dep is None else [dep]
    nd = len(deps)

    def body(a_ref, b_ref, *rest):
        extra_refs, out_refs = rest[:ne], rest[ne + nd:ne + nd + no]
        j = pl.program_id(1)
        if split is not None:
            kind, n = split
            assert nk == 1 and form != "tn"
            tiled = [t for _, _, *t in list(extras) + list(outs)]
            refs = list(extra_refs) + list(out_refs)
            for ci in range(n):
                if kind == "cols":
                    cs = slice(ci * (tn // n), (ci + 1) * (tn // n))
                    part = _dg(a_ref[...].astype(MXU_DTYPE), (b_ref[:, cs] if form == "nn" else b_ref[cs, :]).astype(MXU_DTYPE), ca, cb)
                    view = [r.at[:, cs] if t else r for r, t in zip(refs, tiled)]
                else:
                    rs = slice(ci * (tm // n), (ci + 1) * (tm // n))
                    part = _dg(a_ref[rs, :].astype(MXU_DTYPE), b_ref[...].astype(MXU_DTYPE), ca, cb)
                    view = [r.at[rs, :] if t else r for r, t in zip(refs, tiled)]
                epilogue(part, view[:ne], view[ne:], j, ci)
            return
        part = _dg(a_ref[...].astype(MXU_DTYPE), b_ref[...].astype(MXU_DTYPE), ca, cb)
        if nk == 1:
            epilogue(part, extra_refs, out_refs, j, 0)
            return
        acc_ref = rest[-1]
        k = pl.program_id(2)

        @pl.when(k == 0)
        def _():
            acc_ref[...] = part

        @pl.when(k > 0)
        def _():
            acc_ref[...] += part

        @pl.when(k == nk - 1)
        def _():
            epilogue(acc_ref[...], extra_refs, out_refs, j, 0)

    res = pl.pallas_call(
        body,
        name=name,
        grid=(M // tm, N // tn, nk),
        in_specs=[a_spec, b_spec] + [s for _, s, *_ in extras] + [pl.BlockSpec(memory_space=pl.ANY)] * nd,
        out_specs=[s for _, s, *_ in outs],
        out_shape=[o for o, *_ in outs],
        scratch_shapes=[pltpu.VMEM((tm, tn), F32)] if nk > 1 else [],
        input_output_aliases={2 + ne: 0} if alias_dep else {},
        compiler_params=_cparams(sem),
    )(a, b, *[e for e, *_ in extras], *deps)
    return res


def _ij_spec(tm, tn):
    return pl.BlockSpec((tm, tn), lambda i, j, k: (i, j))


def _store_epilogue(acc, extra_refs, out_refs, j, ci):
    out_refs[0][...] = acc.astype(out_refs[0].dtype)


def _plain_mm(name, a, b, form, out_dtype, tm, tn, tk):
    M = a.shape[1] if form == "tn" else a.shape[0]
    N = b.shape[0] if form == "nt" else b.shape[1]
    return _matmul(name, a, b, form, tm, tn, tk, [(_sds((M, N), out_dtype), _ij_spec(tm, tn))], _store_epilogue)[0]


def _class_slabs(S):
    assert DILATIONS[0] == 1
    return [(g, d, r, S // d) for g, d in enumerate(DILATIONS) if d > 1 for r in range(d)]


def _stack_classes(name, t, out_dtype):
    S, W = t.shape

    def body(x_ref, o_ref):
        o_ref[0:S, :] = x_ref[...].astype(out_dtype)
        for g, d, r, n in _class_slabs(S):
            o_ref[g * S + r * n:g * S + (r + 1) * n, :] = x_ref[pl.ds(r, n, stride=d), :].astype(out_dtype)

    return pl.pallas_call(
        body,
        name=name,
        grid=(W // LANES,),
        in_specs=[pl.BlockSpec((S, LANES), lambda j: (0, j))],
        out_specs=pl.BlockSpec((3 * S, LANES), lambda j: (0, j)),
        out_shape=_sds((3 * S, W), out_dtype),
        compiler_params=_cparams(("parallel",)),
    )(t)


def _rope_tables(seq):
    half = HEAD_DIM // 2
    inv = ROPE_THETA ** (-jnp.arange(half, dtype=F32) * (2.0 / HEAD_DIM))
    inv = jnp.tile(inv, LANES // half)
    pos = []
    for d in DILATIONS:
        row = jnp.arange(seq)
        pos.append((row % (seq // d)) * d + row // (seq // d))
    ang = jnp.concatenate(pos).astype(F32)[:, None] * inv[None, :]
    first = (jnp.arange(LANES) % HEAD_DIM) < half
    sin = jnp.sin(ang)
    return jnp.cos(ang), jnp.where(first[None, :], -sin, sin)


def _partner(x):
    half = HEAD_DIM // 2
    lane = lax.broadcasted_iota(jnp.int32, x.shape, 1)
    first = (lane % HEAD_DIM) < half
    return jnp.where(first, pltpu.roll(x, LANES - half, 1), pltpu.roll(x, half, 1))


def _attn_proj(name, x3, w, cos3, sin3, tm, prev, dep=None):
    S3, D = x3.shape
    S = S3 // 3
    tn = 3 * D // 4
    nrow = S // tm
    local = prev is None

    def tile(j):
        q = 2 * lax.axis_index("x") + lax.axis_index("y")
        c0 = 3 * q + j if local else j + 3 * (j >= 3 * q).astype(jnp.int32)
        return c0, c0 // 4, c0 % 4

    def epilogue(acc, extra_refs, out_refs, j, ci):
        cos_ref, sin_ref = extra_refs
        o_ref = out_refs[0]
        _, _, place = tile(j)
        width = acc.shape[1]
        assert D % width == 0
        is_rot = (place * tn + ci * width) // D < 2
        c = jnp.where(is_rot, cos_ref[...], 1.0)
        s = jnp.where(is_rot, sin_ref[...], 0.0)
        for t in range(width // LANES):
            xs = acc[:, t * LANES:(t + 1) * LANES]
            o_ref[:, t * LANES:(t + 1) * LANES] = (xs * c + _partner(xs) * s).astype(o_ref.dtype)

    rows = lambda i, j: tile(j)[1] * nrow + i
    tab = pl.BlockSpec((tm, LANES), lambda i, j, k: (rows(i, j), 0))
    out = pl.BlockSpec((tm, tn), lambda i, j, k: (rows(i, j), tile(j)[2]))
    ntiles = 3 if local else 9
    return _matmul(name, x3, w, "nn", tm, tn, D, [(_sds((S3, 3 * D), MXU_DTYPE), out, True)], epilogue,
                   extras=[(cos3, tab), (sin3, tab)], a_map=lambda i, j, k: (rows(i, j), k),
                   b_map=lambda i, j, k: (k, j if local else tile(j)[0]), mnk=(nrow * tm, ntiles * tn, D),
                   dep=dep if local else prev, alias_dep=not local, split=("cols", 3))[0]


def _head_sel(d_model):
    h = jnp.arange(LANES)[:, None]
    l = jnp.arange(d_model)[None, :]
    return (l // HEAD_DIM == h).astype(BF16)


def _class_edges(b, nblk):
    g = b // nblk
    per_class = jnp.where(g == 0, nblk // DILATIONS[0], jnp.where(g == 1, nblk // DILATIONS[1], nblk // DILATIONS[2]))
    pos = (b % nblk) % per_class
    return pos != 0, pos != per_class - 1


def _two_heads(t, top):
    zero = jnp.zeros_like(t)
    return jnp.concatenate([jnp.where(top, t, zero), jnp.where(top, zero, t)], axis=0)


def _band_mask(has_prev):
    B = ATTN_BLK
    row = lax.broadcasted_iota(jnp.int32, (2 * B, 2 * B), 0) % B
    col = lax.broadcasted_iota(jnp.int32, (2 * B, 2 * B), 1)
    in_prev = jnp.logical_and(jnp.logical_and(col < B, col >= row), has_prev)
    in_own = jnp.logical_and(col >= B, col - B <= row)
    return jnp.logical_or(in_prev, in_own)


def _attn_fwd(P3, D):
    S3 = P3.shape[0]
    B = ATTN_BLK
    nblk = S3 // 3 // B
    npairs = D // LANES
    scale = HEAD_DIM ** -0.5

    def body(q_ref, kc_ref, vc_ref, kp_ref, vp_ref, o_ref, lse_ref):
        has_prev, _ = _class_edges(pl.program_id(0), nblk)
        bias = jnp.where(_band_mask(has_prev), 0.0, NEG)
        lane = lax.broadcasted_iota(jnp.int32, (B, LANES), 1)
        top = lane < HEAD_DIM
        lse_acc = jnp.zeros((B, LANES), F32)
        for j in range(npairs):
            sl = slice(j * LANES, (j + 1) * LANES)
            Q = _two_heads(q_ref[:, sl] * scale, top)
            K2 = jnp.concatenate([kp_ref[:, sl], kc_ref[:, sl]], axis=0)
            V2 = jnp.concatenate([vp_ref[:, sl], vc_ref[:, sl]], axis=0)
            s = _nt(Q, K2) + bias
            m = jnp.max(s, axis=1, keepdims=True)
            p = jnp.exp(s - m)
            l = jnp.sum(p, axis=1, keepdims=True)
            o = _nn(p.astype(MXU_DTYPE), V2) * (1.0 / l)
            o_ref[:, sl] = jnp.where(top, o[:B], o[B:])
            lse = m + jnp.log(l)
            lse_acc = jnp.where(lane == 2 * j, lse[:B], jnp.where(lane == 2 * j + 1, lse[B:], lse_acc))
        lse_ref[...] = lse_acc

    blk = lambda part, prev: pl.BlockSpec(
        (B, D), (lambda b: (jnp.maximum(b - 1, 0), part)) if prev else (lambda b: (b, part)))
    return pl.pallas_call(
        body,
        name="attn_fwd",
        grid=(3 * nblk,),
        in_specs=[blk(0, False), blk(1, False), blk(2, False), blk(1, True), blk(2, True)],
        out_specs=[pl.BlockSpec((B, D), lambda b: (b, 0)), pl.BlockSpec((B, LANES), lambda b: (b, 0))],
        out_shape=[_sds((S3, D), F32), _sds((S3, LANES), F32)],
        compiler_params=_cparams(("parallel",)),
    )(P3, P3, P3, P3, P3)


def _attn_mix(o3, lse3, sel):
    S3, D = o3.shape
    S = S3 // 3

    def body(o3_ref, lse_ref, sel_ref, o_ref, L_ref, w_ref):
        @pl.when(pl.program_id(0) == 0)
        def _():
            w_ref[0] = lse_ref[0:S, :]
            for g, d, r, n in _class_slabs(S):
                w_ref[g, pl.ds(r, n, stride=d), :] = lse_ref[g * S + r * n:g * S + (r + 1) * n, :]
            a, b, c = w_ref[0], w_ref[1], w_ref[2]
            m = jnp.maximum(jnp.maximum(a, b), c)
            L = m + jnp.log(jnp.exp(a - m) + jnp.exp(b - m) + jnp.exp(c - m))
            L_ref[...] = L
            w_ref[0] = jnp.exp(a - L)
            w_ref[1] = jnp.exp(b - L)
            w_ref[2] = jnp.exp(c - L)

        s = sel_ref[...]
        o_ref[...] = _exact_nn(w_ref[0], s) * o3_ref[0:S, :]
        for g, d, r, n in _class_slabs(S):
            rows = pl.ds(r, n, stride=d)
            o_ref[rows, :] += _exact_nn(w_ref[g, rows, :], s) * o3_ref[g * S + r * n:g * S + (r + 1) * n, :]

    return pl.pallas_call(
        body,
        name="attn_mix",
        grid=(D // LANES,),
        in_specs=[pl.BlockSpec((S3, LANES), lambda j: (0, j)), pl.BlockSpec((S3, LANES), lambda j: (0, 0)),
                  pl.BlockSpec((LANES, LANES), lambda j: (0, j))],
        out_specs=[pl.BlockSpec((S, LANES), lambda j: (0, j)), pl.BlockSpec((S, LANES), lambda j: (0, 0))],
        out_shape=[_sds((S, D), F32), _sds((S, LANES), F32)],
        scratch_shapes=[pltpu.VMEM((3, S, LANES), F32)],
        compiler_params=_cparams(("arbitrary",)),
    )(o3, lse3, sel)


def _attn_bwd(P3, do3, L3, delta3, cos3, sin3, D, dep):
    S3 = P3.shape[0]
    B = ATTN_BLK
    nblk = S3 // 3 // B
    npairs = D // LANES
    scale = HEAD_DIM ** -0.5

    def body(c_ref, kp_ref, vp_ref, qn_ref, doc_ref, don_ref, Lc_ref, Ln_ref, dc_ref, dn_ref, cos_ref, sin_ref, dep_ref, out_ref):
        has_prev, has_next = _class_edges(pl.program_id(0), nblk)
        bias = jnp.where(_band_mask(has_prev), 0.0, NEG)
        row = lax.broadcasted_iota(jnp.int32, (2 * B, B), 0) % B
        col = lax.broadcasted_iota(jnp.int32, (2 * B, B), 1)
        bias_n = jnp.where(jnp.logical_and(col >= row, has_next), 0.0, NEG)
        lane = lax.broadcasted_iota(jnp.int32, (B, LANES), 1)
        top = lane < HEAD_DIM
        cos_t = cos_ref[...]
        sin_inv = -sin_ref[...]
        Lc_all, Ln_all, dc_all, dn_all = Lc_ref[...], Ln_ref[...], dc_ref[...], dn_ref[...]
        pair_col = lambda t, j: jnp.concatenate([t[:, 2 * j:2 * j + 1], t[:, 2 * j + 1:2 * j + 2]], axis=0)
        for j in range(npairs):
            sl = lambda part: slice(part * D + j * LANES, part * D + (j + 1) * LANES)
            pj = slice(j * LANES, (j + 1) * LANES)
            kc2, vc2 = c_ref[:, sl(1)], c_ref[:, sl(2)]
            K2 = jnp.concatenate([kp_ref[:, pj], kc2], axis=0)
            V2 = jnp.concatenate([vp_ref[:, pj], vc2], axis=0)
            Qc = _two_heads(c_ref[:, sl(0)] * scale, top)
            Qn = _two_heads(qn_ref[:, pj] * scale, top)
            DOc = _two_heads(doc_ref[:, pj].astype(MXU_DTYPE), top)
            DOn = _two_heads(don_ref[:, pj].astype(MXU_DTYPE), top)
            P_c = jnp.exp(_nt(Qc, K2) + bias - pair_col(Lc_all, j))
            dS_c = P_c * (_nt(DOc, V2) - pair_col(dc_all, j))
            P_n = jnp.exp(_nt(Qn, kc2) + bias_n - pair_col(Ln_all, j))
            dS_n = P_n * (_nt(DOn, vc2) - pair_col(dn_all, j))
            dq = _nn(dS_c.astype(MXU_DTYPE), K2)
            dq2 = jnp.where(top, dq[:B], dq[B:]) * scale
            Qk = jnp.concatenate([Qc, Qn], axis=0)
            DOk = jnp.concatenate([DOc, DOn], axis=0)
            dk2 = _tn(jnp.concatenate([dS_c[:, B:], dS_n], axis=0).astype(MXU_DTYPE), Qk)
            dv2 = _tn(jnp.concatenate([P_c[:, B:], P_n], axis=0).astype(MXU_DTYPE), DOk)
            out_ref[:, sl(0)] = (dq2 * cos_t + _partner(dq2) * sin_inv).astype(out_ref.dtype)
            out_ref[:, sl(1)] = (dk2 * cos_t + _partner(dk2) * sin_inv).astype(out_ref.dtype)
            out_ref[:, sl(2)] = dv2.astype(out_ref.dtype)

    cur = lambda b: b
    prv = lambda b: jnp.maximum(b - 1, 0)
    nxt = lambda b: jnp.minimum(b + 1, 3 * nblk - 1)
    spec = lambda w, f, part=0: pl.BlockSpec((B, w), lambda b: (f(b), part))
    return pl.pallas_call(
        body,
        name="attn_bwd",
        grid=(3 * nblk,),
        in_specs=[spec(3 * D, cur), spec(D, prv, 1), spec(D, prv, 2), spec(D, nxt, 0), spec(D, cur), spec(D, nxt),
                  spec(LANES, cur), spec(LANES, nxt), spec(LANES, cur), spec(LANES, nxt), spec(LANES, cur), spec(LANES, cur),
                  pl.BlockSpec(memory_space=pl.ANY)],
        out_specs=spec(3 * D, cur),
        out_shape=_sds((S3, 3 * D), MXU_DTYPE),
        compiler_params=_cparams(("parallel",)),
    )(P3, P3, P3, P3, do3, do3, L3, L3, delta3, delta3, cos3, sin3, dep)


def _input_grad(du, dx3):
    S, D = du.shape

    def body(du_ref, dx_ref, o_ref):
        o_ref[...] = ALPHA * du_ref[...] + dx_ref[0:S, :]
        for g, d, r, n in _class_slabs(S):
            o_ref[pl.ds(r, n, stride=d), :] += dx_ref[g * S + r * n:g * S + (r + 1) * n, :]

    return pl.pallas_call(
        body,
        name="input_grad",
        grid=(D // LANES,),
        in_specs=[pl.BlockSpec((S, LANES), lambda j: (0, j)), pl.BlockSpec((3 * S, LANES), lambda j: (0, j))],
        out_specs=pl.BlockSpec((S, LANES), lambda j: (0, j)),
        out_shape=_sds((S, D), F32),
        compiler_params=_cparams(("parallel",)),
    )(du, dx3)


def _chunk_causal(tb):
    r = lax.broadcasted_iota(jnp.int32, (tb, tb), 0)
    c = lax.broadcasted_iota(jnp.int32, (tb, tb), 1)
    return jnp.logical_and((r // HGRN_CHUNK) == (c // HGRN_CHUNK), r >= c)


def _chunk_sums(a, lower):
    C = HGRN_CHUNK
    r = lax.broadcasted_iota(jnp.int32, (C, C), 0)
    c = lax.broadcasted_iota(jnp.int32, (C, C), 1)
    tri = ((r >= c) if lower else (r <= c)).astype(BF16)
    parts = _split3(a)
    out = []
    for ci in range(a.shape[0] // C):
        rows = slice(ci * C, (ci + 1) * C)
        out.append(_nn(tri, parts[0][rows]) + _nn(tri, parts[1][rows]) + _nn(tri, parts[2][rows]))
    return jnp.concatenate(out, axis=0)


def _chunk_last(b):
    C = HGRN_CHUNK
    return jnp.concatenate([jnp.broadcast_to(b[(ci + 1) * C - 1:(ci + 1) * C, :], (C, b.shape[1]))
                            for ci in range(b.shape[0] // C)], axis=0)


def _lower_bound(lb_ref):
    l0, l1 = lb_ref[0:1, :], lb_ref[1:2, :]
    m = jnp.maximum(l0, l1)
    e0, e1 = jnp.exp(l0 - m), jnp.exp(l1 - m)
    return e1 / (e0 + e1)


def _hgrn_gates(q_raw, z, lb):
    sg = 1.0 / (1.0 + jnp.exp(-z))
    sn = 1.0 / (1.0 + jnp.exp(z))
    f = lb + (1.0 - lb) * sg
    key = (1.0 - lb) * sn
    sq = 1.0 / (1.0 + jnp.exp(-q_raw))
    return sg, sn, f, key, sq


def _hgrn_fwd(P1, lb_logits, norm_g, tb):
    S = P1.shape[0]
    D = P1.shape[1] // 3
    K = HGRN_DK
    H = D // K
    HP = H
    C = HGRN_CHUNK
    cpb = tb // C
    nt = S // tb

    def body(q_ref, f_ref, i_ref, lb_ref, g_ref, o_ref, n_ref, st_ref, state):
        t = pl.program_id(1)

        @pl.when(t == 0)
        def _():
            state[...] = jnp.zeros_like(state)

        lb_all = _lower_bound(lb_ref)
        low = _chunk_causal(tb)
        for hh in range(HP):
            lanes = slice(hh * K, (hh + 1) * K)
            q_raw, z, v = q_ref[:, lanes], f_ref[:, lanes], i_ref[:, lanes]
            sg, sn, f, key, sq = _hgrn_gates(q_raw, z, lb_all[:, lanes])
            b = _chunk_sums(jnp.log(f), lower=True)
            qd = (q_raw * sq * jnp.exp(b)).astype(MXU_DTYPE)
            kd = (key * jnp.exp(-b)).astype(MXU_DTYPE)
            kb = (key * jnp.exp(_chunk_last(b) - b)).astype(MXU_DTYPE)
            vm = v.astype(MXU_DTYPE)
            a = jnp.where(low, _nt(qd, kd), 0.0).astype(MXU_DTYPE)
            o_intra = _nn(a, vm)
            st = state[hh]
            outs = []
            for ci in range(cpb):
                rows = slice(ci * C, (ci + 1) * C)
                st_ref[hh, ci] = st
                outs.append(o_intra[rows] + _nt(qd[rows], st.astype(MXU_DTYPE)))
                st = st * jnp.exp(b[(ci + 1) * C - 1:(ci + 1) * C, :]) + _tn(vm[rows], kb[rows])
            state[hh] = st
            o = jnp.concatenate(outs, axis=0)
            o_ref[:, lanes] = o
            rs = lax.rsqrt(jnp.mean(o * o, axis=1, keepdims=True) + RMS_EPS)
            n_ref[:, lanes] = o * rs * g_ref[:, lanes]

    tok = lambda part: pl.BlockSpec((tb, HP * K), lambda h, t: (t, part * (H // HP) + h))
    vec = lambda rows: pl.BlockSpec((rows, HP * K), lambda h, t: (0, h))
    return pl.pallas_call(
        body,
        name="hgrn_fwd",
        grid=(H // HP, nt),
        in_specs=[tok(0), tok(1), tok(2), vec(2), vec(1)],
        out_specs=[tok(0), tok(0), pl.BlockSpec((HP, cpb, K, K), lambda h, t: (h, t, 0, 0))],
        out_shape=[_sds((S, D), F32), _sds((S, D), F32), _sds((H, S // C, K, K), F32)],
        scratch_shapes=[pltpu.VMEM((HP, K, K), F32)],
        compiler_params=_cparams(("parallel", "arbitrary")),
    )(P1, P1, P1, lb_logits, norm_g)


def _hgrn_bwd(P1, o_pre, states, dn, lb_logits, norm_g, tb):
    S = P1.shape[0]
    D = P1.shape[1] // 3
    K = HGRN_DK
    H = D // K
    HP = H
    C = HGRN_CHUNK
    cpb = tb // C
    nt = S // tb

    def body(q_ref, f_ref, i_ref, o_ref, st_ref, dn_ref, lb_ref, g_ref, d_ref, dg_ref, dlb_ref, dstate):
        t = pl.program_id(1)

        @pl.when(t == 0)
        def _():
            dstate[...] = jnp.zeros_like(dstate)
            dg_ref[...] = jnp.zeros_like(dg_ref)
            dlb_ref[...] = jnp.zeros_like(dlb_ref)

        lb_all = _lower_bound(lb_ref)
        low = _chunk_causal(tb)
        for hh in range(HP):
            lanes = slice(hh * K, (hh + 1) * K)
            lb = lb_all[:, lanes]
            gn = g_ref[:, lanes]
            q_raw, z, v = q_ref[:, lanes], f_ref[:, lanes], i_ref[:, lanes]
            sg, sn, f, key, sq = _hgrn_gates(q_raw, z, lb)
            b = _chunk_sums(jnp.log(f), lower=True)
            e_pos, e_neg, e_rel = jnp.exp(b), jnp.exp(-b), jnp.exp(_chunk_last(b) - b)
            qd_f, kd_f, kb_f = q_raw * sq * e_pos, key * e_neg, key * e_rel
            qd, kd, kb = qd_f.astype(MXU_DTYPE), kd_f.astype(MXU_DTYPE), kb_f.astype(MXU_DTYPE)
            vm = v.astype(MXU_DTYPE)
            a = jnp.where(low, _nt(qd, kd), 0.0).astype(MXU_DTYPE)
            o = o_ref[:, lanes]
            dnn = dn_ref[:, lanes]
            rs = lax.rsqrt(jnp.mean(o * o, axis=1, keepdims=True) + RMS_EPS)
            dg_ref[:, lanes] += jnp.sum(dnn * o * rs, axis=0, keepdims=True)
            tg = dnn * gn
            dom = (rs * tg - o * (rs * rs * rs) * jnp.mean(tg * o, axis=1, keepdims=True)).astype(MXU_DTYPE)
            da = jnp.where(low, _nt(dom, vm), 0.0).astype(MXU_DTYPE)
            dv = _tn(a, dom)
            dqd = _nn(da, kd)
            dkd = _tn(da, qd)
            dst = dstate[hh]
            dv_s, dqd_s, dkb_s, dbl_s = [None] * cpb, [None] * cpb, [None] * cpb, [None] * cpb
            for ci in reversed(range(cpb)):
                rows = slice(ci * C, (ci + 1) * C)
                st = st_ref[hh, ci]
                dstm = dst.astype(MXU_DTYPE)
                dec = jnp.exp(b[(ci + 1) * C - 1:(ci + 1) * C, :])
                dv_s[ci] = _nt(kb[rows], dstm)
                dkb_s[ci] = _nn(vm[rows], dstm)
                dqd_s[ci] = _nn(dom[rows], st.astype(MXU_DTYPE))
                db_last = jnp.sum(dkb_s[ci] * kb_f[rows], axis=0, keepdims=True) + jnp.sum(dst * st, axis=0, keepdims=True) * dec
                dbl_s[ci] = jnp.broadcast_to(db_last, (C, K))
                dst = dst * dec + _tn(dom[rows], qd[rows])
            dstate[hh] = dst
            dv = dv + jnp.concatenate(dv_s, axis=0)
            dqd = dqd + jnp.concatenate(dqd_s, axis=0)
            dkb = jnp.concatenate(dkb_s, axis=0)
            dkey = dkd * e_neg + dkb * e_rel
            db = dqd * qd_f - dkd * kd_f - dkb * kb_f
            dlogf = _chunk_sums(db, lower=False) + jnp.concatenate(dbl_s, axis=0)
            gz = (1.0 - lb) * sg * sn
            col = lambda part: slice(part * D + hh * K, part * D + (hh + 1) * K)
            d_ref[:, col(0)] = (dqd * e_pos * (sq + q_raw * sq * (1.0 - sq))).astype(d_ref.dtype)
            d_ref[:, col(1)] = (dlogf * gz / f - dkey * gz).astype(d_ref.dtype)
            d_ref[:, col(2)] = dv.astype(d_ref.dtype)
            dlb_ref[:, lanes] += jnp.sum(dlogf * sn / f - dkey * sn, axis=0, keepdims=True)

    rev = lambda t: nt - 1 - t
    tok = lambda part: pl.BlockSpec((tb, HP * K), lambda h, t: (rev(t), part * (H // HP) + h))
    vec = lambda rows: pl.BlockSpec((rows, HP * K), lambda h, t: (0, h))
    outs = pl.pallas_call(
        body,
        name="hgrn_bwd",
        grid=(H // HP, nt),
        in_specs=[tok(0), tok(1), tok(2), tok(0),
                  pl.BlockSpec((HP, cpb, K, K), lambda h, t: (h, rev(t), 0, 0)),
                  tok(0), vec(2), vec(1)],
        out_specs=[pl.BlockSpec((tb, 3 * D), lambda h, t: (rev(t), 0)), vec(1), vec(1)],
        out_shape=[_sds((S, 3 * D), MXU_DTYPE)] + [_sds((1, D), F32)] * 2,
        scratch_shapes=[pltpu.VMEM((HP, K, K), F32)],
        compiler_params=_cparams(("parallel", "arbitrary")),
    )(P1, P1, P1, o_pre, states, dn, lb_logits, norm_g)
    return outs


def _lb_logits_grad(dlb, lb_logits):
    def body(d_ref, l_ref, o_ref):
        s1 = _lower_bound(l_ref)
        d = d_ref[...]
        o_ref[0:1, :] = -(1.0 - s1) * s1 * d
        o_ref[1:2, :] = s1 * (1.0 - s1) * d

    return pl.pallas_call(body, name="lb_logits_grad", out_shape=_sds(lb_logits.shape, F32))(dlb, lb_logits)


def _mm_res_ln(name, a, w_full, res, g, b, tm, tk):
    from_ln = isinstance(res, tuple)
    S, D = (res[0] if from_ln else res).shape

    def epilogue(acc, extra_refs, out_refs, j, ci):
        g_ref, b_ref = extra_refs[:2]
        xm_ref, xhat_ref, rstd_ref = out_refs
        r = extra_refs[2][...] * extra_refs[3][...] + extra_refs[4][...] if from_ln else extra_refs[2][...]
        u = ALPHA * r + acc
        mu = jnp.mean(u, axis=1, keepdims=True)
        cen = u - mu
        rstd = lax.rsqrt(jnp.mean(cen * cen, axis=1, keepdims=True) + LN_EPS)
        xhat = cen * rstd
        xhat_ref[...] = xhat
        xm_ref[...] = (xhat * g_ref[...] + b_ref[...]).astype(xm_ref.dtype)
        rstd_ref[...] = rstd

    row = pl.BlockSpec((tm, D), lambda i, j, k: (i, 0))
    vec = pl.BlockSpec((1, D), lambda i, j, k: (0, 0))
    outs = [(_sds((S, D), MXU_DTYPE), row, True), (_sds((S, D), F32), row, True),
            (_sds((S, 1), F32), pl.BlockSpec((tm, 1), lambda i, j, k: (i, 0)), True)]
    res_extras = [(res[0], row, True), (res[1], vec), (res[2], vec)] if from_ln else [(res, row, True)]
    return _matmul(name, a, w_full, "nn", tm, D, tk, outs, epilogue, extras=[(g, vec), (b, vec)] + res_extras,
                   split=("rows", 2) if tk == a.shape[1] else None)


def _ln_bwd_rows(dy, xh, rstd, g, first, du_ref, dum_ref, dg_ref, db_ref):
    if first is not None:
        @pl.when(first)
        def _():
            dg_ref[...] = jnp.zeros_like(dg_ref)
            db_ref[...] = jnp.zeros_like(db_ref)

    dg_ref[...] += jnp.sum(dy * xh, axis=0, keepdims=True)
    db_ref[...] += jnp.sum(dy, axis=0, keepdims=True)
    dxh = dy * g
    m1 = jnp.mean(dxh, axis=1, keepdims=True)
    m2 = jnp.mean(dxh * xh, axis=1, keepdims=True)
    du = rstd * (dxh - m1 - xh * m2)
    du_ref[...] = du
    dum_ref[...] = du.astype(dum_ref.dtype)


def _loss_ln_bwd(target, xhat, rstd, g, b, tm):
    S, D = xhat.shape

    def body(t_ref, xh_ref, r_ref, g_ref, b_ref, sq_ref, du_ref, dum_ref, dg_ref, db_ref):
        first = pl.program_id(0) == 0

        @pl.when(first)
        def _():
            sq_ref[...] = jnp.zeros_like(sq_ref)

        xh = xh_ref[...]
        e = xh * g_ref[...] + b_ref[...] - t_ref[...]
        sq_ref[...] += jnp.sum(e * e, axis=0, keepdims=True)
        _ln_bwd_rows(e / D, xh, r_ref[...], g_ref[...], first, du_ref, dum_ref, dg_ref, db_ref)

    row = pl.BlockSpec((tm, D), lambda i: (i, 0))
    vec = pl.BlockSpec((1, D), lambda i: (0, 0))
    return pl.pallas_call(
        body,
        name="loss_ln_bwd",
        grid=(S // tm,),
        in_specs=[row, row, pl.BlockSpec((tm, 1), lambda i: (i, 0)), vec, vec],
        out_specs=[vec, row, row, vec, vec],
        out_shape=[_sds((1, D), F32), _sds((S, D), F32), _sds((S, D), MXU_DTYPE), _sds((1, D), F32), _sds((1, D), F32)],
        compiler_params=_cparams(("arbitrary",)),
    )(target, xhat, rstd, g, b)


def _mlp_up(name, x, w_up, tm, tn, tk):
    S = x.shape[0]
    F = w_up.shape[1]

    def epilogue(acc, extra_refs, out_refs, j, ci):
        r = jnp.maximum(acc, 0.0)
        out_refs[0][...] = (r * r).astype(out_refs[0].dtype)

    return _matmul(name, x, w_up, "nn", tm, tn, tk, [(_sds((S, F), MXU_DTYPE), _ij_spec(tm, tn), True)], epilogue,
                   split=("cols", 2))[0]


def _mlp_down_bwd(name, dy, w_down, a, tm, tn, tk):
    S, F = a.shape

    def epilogue(acc, extra_refs, out_refs, j, ci):
        out_refs[0][...] = (acc * (2.0 * jnp.sqrt(extra_refs[0][...].astype(F32)))).astype(out_refs[0].dtype)

    return _matmul(name, dy, w_down, "nt", tm, tn, tk, [(_sds((S, F), MXU_DTYPE), _ij_spec(tm, tn), True)], epilogue,
                   extras=[(a, _ij_spec(tm, tn), True)], split=("cols", 2))[0]


def _mm_nt_res_ln_bwd(name, dy, w, du, xhat, rstd, g, tm, tk, dep):
    S, D = du.shape

    def epilogue(acc, extra_refs, out_refs, j, ci):
        du_ref, xh_ref, r_ref, g_ref = extra_refs
        first = (pl.program_id(0) == 0) if ci == 0 else None
        _ln_bwd_rows(ALPHA * du_ref[...] + acc, xh_ref[...], r_ref[...], g_ref[...], first, *out_refs)

    row = pl.BlockSpec((tm, D), lambda i, j, k: (i, 0))
    vec = pl.BlockSpec((1, D), lambda i, j, k: (0, 0))
    return _matmul(name, dy, w, "nt", tm, D, tk,
                   [(_sds((S, D), F32), row, True), (_sds((S, D), MXU_DTYPE), row, True), (_sds((1, D), F32), vec),
                    (_sds((1, D), F32), vec)], epilogue,
                   extras=[(du, row, True), (xhat, row, True), (rstd, pl.BlockSpec((tm, 1), lambda i, j, k: (i, 0)), True), (g, vec)],
                   dep=dep, sem=("arbitrary", "arbitrary", "arbitrary"), split=("rows", 2))


def _attn_out_bwd(du, w_out, o, sel_t, tm, tk):
    S, D = o.shape

    def epilogue(acc, extra_refs, out_refs, j, ci):
        out_refs[0][...] = acc
        out_refs[1][...] = _exact_nn(acc * extra_refs[0][...], extra_refs[1][...])

    row = pl.BlockSpec((tm, D), lambda i, j, k: (i, 0))
    slim = pl.BlockSpec((tm, LANES), lambda i, j, k: (i, 0))
    return _matmul("attn_out_bwd", du, w_out, "nt", tm, D, tk,
                   [(_sds((S, D), F32), row, True), (_sds((S, LANES), F32), slim, True)], epilogue,
                   extras=[(o, row, True), (sel_t, pl.BlockSpec((D, LANES), lambda i, j, k: (0, 0)))], split=("rows", 2))


def _adamw(name, w, gs, m, v):
    shape = w.shape
    cols = shape[-1]
    rows = math.prod(shape[:-1])
    w2, m2, v2 = (t.reshape(rows, cols) for t in (w, m, v))
    gs2 = [g.reshape(-1, cols) for g in gs]
    ng = len(gs2)
    tr = _pick(rows // ng, (256, 128, 64, 32, 16, 8))
    per = rows // ng // tr
    c1 = 1.0 - ADAM_B1 ** ADAM_STEP
    c2 = 1.0 - ADAM_B2 ** ADAM_STEP

    def body(w_ref, m_ref, v_ref, *rest):
        g_refs, (d_ref, nm_ref, nv_ref), g_out = rest[:ng], rest[ng:ng + 3], rest[ng + 3:]
        gg = g_refs[0][...]
        if ng == 2:
            gg = jnp.where(pl.program_id(0) < per, gg, g_refs[1][...])
        g_out[0][...] = gg
        nm = ADAM_B1 * m_ref[...] + (1.0 - ADAM_B1) * gg
        nv = ADAM_B2 * v_ref[...] + (1.0 - ADAM_B2) * (gg * gg)
        nm_ref[...] = nm
        nv_ref[...] = nv
        d_ref[...] = -ADAM_LR * ((nm / c1) / (jnp.sqrt(nv / c2) + ADAM_EPS) + ADAM_WD * w_ref[...])

    blk = pl.BlockSpec((tr, cols), lambda i: (i, 0))
    g_specs = [blk] if ng == 1 else [pl.BlockSpec((tr, cols), lambda i: (jnp.minimum(i, per - 1), 0)),
                                     pl.BlockSpec((tr, cols), lambda i: (jnp.maximum(i - per, 0), 0))]
    outs = pl.pallas_call(
        body,
        name=name,
        grid=(rows // tr,),
        in_specs=[blk] * 3 + g_specs,
        out_specs=[blk] * 4,
        out_shape=[_sds((rows, cols), F32)] * 4,
        compiler_params=_cparams(("parallel",)),
    )(w2, m2, v2, *gs2)
    return tuple(o.reshape(shape) for o in outs)


HBM = pl.BlockSpec(memory_space=pl.ANY)


def _shard_slice(ref, axis, size, index):
    idx = [slice(None)] * len(ref.shape)
    idx[axis] = pl.ds(pl.multiple_of(index * size, 8), size)
    return ref.at[tuple(idx)]


def _share_halves(name, full, tr):
    R, W4 = full.shape
    W, h = W4 // 4, R // 2
    steps = [(k, t) for k in range(3) for t in range(h // tr)]

    def body(f_in, f_ref, buf, lsem, ssem, rsem):
        x, y, c = lax.axis_index("x"), lax.axis_index("y"), lax.axis_index("c")
        sibling = (x, y, 1 - c)
        chips = [(1 - x, y), (x, 1 - y), (1 - x, 1 - y)]

        def tile(k, t):
            px, py = chips[k]
            return f_ref.at[pl.ds(pl.multiple_of(c * h + t * tr, 8), tr), pl.ds(pl.multiple_of((2 * px + py) * W, LANES), W)]

        sends = []
        for s, (k, t) in enumerate(steps):
            slot = s % 2
            if s >= 2:
                sends[s - 2].wait_send()
            lc = pltpu.make_async_copy(tile(k, t), buf.at[slot], lsem.at[slot])
            lc.start()
            lc.wait()
            rc = pltpu.make_async_remote_copy(src_ref=buf.at[slot], dst_ref=tile(k, t), send_sem=ssem.at[slot], recv_sem=rsem,
                                              device_id=sibling, device_id_type=MESH)
            rc.start()
            sends.append(rc)
        for rc in sends[-2:]:
            rc.wait_send()
        whole = f_ref.at[pl.ds(0, h), pl.ds(0, 3 * W)]
        pltpu.make_async_remote_copy(src_ref=whole, dst_ref=whole, send_sem=ssem.at[0], recv_sem=rsem,
                                     device_id=sibling, device_id_type=MESH).wait_recv()

    return pl.pallas_call(
        body,
        name=name,
        in_specs=[HBM],
        out_specs=HBM,
        out_shape=_sds(full.shape, full.dtype),
        input_output_aliases={0: 0},
        scratch_shapes=[pltpu.VMEM((2, tr, W), full.dtype), pltpu.SemaphoreType.DMA((2,)), pltpu.SemaphoreType.DMA((2,)),
                        pltpu.SemaphoreType.DMA(())],
    )(full)


IN_HBM = pl.BlockSpec(memory_space=pltpu.HBM)
IN_SEM = pl.BlockSpec(memory_space=pltpu.SEMAPHORE)
DATAFLOW = pltpu.SideEffectType.DATAFLOW_SIDE_EFFECTING


def _hbm(t):
    return pltpu.with_memory_space_constraint(t, pltpu.HBM)


def _token_spec():
    return pl.BlockSpec(memory_space=pltpu.VMEM)


def _gather_copies(s_refs, f_refs, axes, halves, send, recv, loc, arrival):
    x, y, c = lax.axis_index("x"), lax.axis_index("y"), lax.axis_index("c")
    chips = [(1 - x, y), (x, 1 - y), (1 - x, 1 - y)]
    local, remote = [], []
    for a in range(len(s_refs)):
        size = s_refs[a].shape[axes[a]]
        local.append(pltpu.make_async_copy(s_refs[a], _shard_slice(f_refs[a], axes[a], size, 2 * x + y), loc.at[a]))
        for k, (px, py) in enumerate(chips):
            block = (2 * px + py) if arrival else (2 * x + y)
            src, dst = s_refs[a], _shard_slice(f_refs[a], axes[a], size, block)
            if halves:
                assert axes[a] == 1 and len(s_refs[a].shape) == 2
                h = s_refs[a].shape[0] // 2
                rows = pl.ds(pl.multiple_of(c * h, 8), h)
                src = s_refs[a].at[rows, :]
                dst = f_refs[a].at[rows, pl.ds(pl.multiple_of(block * size, LANES), size)]
            remote.append(pltpu.make_async_remote_copy(src_ref=src, dst_ref=dst, send_sem=send.at[3 * a + k],
                                                       recv_sem=recv.at[3 * a + k], device_id=(px, py, c), device_id_type=MESH))
    return local, remote


def _gather_start(name, shards, axes, after, halves=False):
    n = len(shards)
    fulls = []
    for s, ax in zip(shards, axes):
        fs = list(s.shape)
        fs[ax] *= 4
        fulls.append(lax.empty(tuple(fs), s.dtype))

    def body(*refs):
        s_refs, f_refs = refs[:n], refs[n:2 * n]
        send, recv, loc, token = refs[2 * n + 1], refs[2 * n + 2], refs[2 * n + 3], refs[-1]
        local, remote = _gather_copies(s_refs, f_refs, axes, halves, send, recv, loc, arrival=False)
        for cp in remote + local:
            cp.start()
        token[...] = jnp.zeros_like(token)

    outs = pl.pallas_call(
        body,
        name=name,
        out_shape=(pltpu.SemaphoreType.DMA((3 * n,)), pltpu.SemaphoreType.DMA((3 * n,)), pltpu.SemaphoreType.DMA((n,)),
                   *[pltpu.HBM(t.shape, t.dtype) for t in shards + fulls], _sds((8, LANES), F32)),
        in_specs=[IN_HBM] * (2 * n) + [HBM],
        out_specs=(IN_SEM, IN_SEM, IN_SEM, *[IN_HBM] * (2 * n), _token_spec()),
        input_output_aliases={i: 3 + i for i in range(2 * n)},
        compiler_params=pltpu.CompilerParams(has_side_effects=DATAFLOW),
    )(*[_hbm(t) for t in shards + fulls], after)
    return (outs[0], outs[1], outs[2], list(outs[3:3 + n]), list(outs[3 + n:3 + 2 * n]), axes, halves), outs[-1]


def _gather_wait(name, state, *after):
    send, recv, loc, s_thru, f_thru, axes, halves = state
    n = len(s_thru)

    def body(*refs):
        s_refs, f_refs = refs[:n], refs[n:2 * n]
        local, remote = _gather_copies(s_refs, f_refs, axes, halves, refs[2 * n], refs[2 * n + 1], refs[2 * n + 2], arrival=True)
        for cp in local:
            cp.wait()
        for cp in remote:
            cp.wait_send()
            cp.wait_recv()

    outs = pl.pallas_call(
        body,
        name=name,
        out_shape=tuple(pltpu.HBM(t.shape, t.dtype) for t in s_thru + f_thru),
        in_specs=[IN_HBM] * (2 * n) + [IN_SEM, IN_SEM, IN_SEM] + [HBM] * len(after),
        out_specs=tuple([IN_HBM] * (2 * n)),
        input_output_aliases={i: i for i in range(2 * n)},
        compiler_params=pltpu.CompilerParams(has_side_effects=DATAFLOW),
    )(*s_thru, *f_thru, send, recv, loc, *after)
    return list(outs[n:2 * n])


FLIPS = [(fx, fy, fc) for fx in (0, 1) for fy in (0, 1) for fc in (0, 1)][1:]


def _piece_shape(shape, axis):
    ps = list(shape)
    if axis == 0:
        ps[0] //= 8
    else:
        ps[0] //= 2
        ps[axis] //= 4
    return tuple(ps)


def _piece(ref, axis, q, c):
    shape = ref.shape
    idx = [slice(None)] * len(shape)
    if axis == 0:
        h = shape[0] // 8
        idx[0] = pl.ds(pl.multiple_of((2 * q + c) * h, 8), h)
    else:
        h, w = shape[0] // 2, shape[axis] // 4
        idx[0] = pl.ds(c * h, h)
        idx[axis] = pl.ds(pl.multiple_of(q * w, LANES if axis == len(shape) - 1 else 8), w)
    return ref.at[tuple(idx)]


def _scatter_copies(g_refs, l_refs, axes, send, recv):
    x, y, c = lax.axis_index("x"), lax.axis_index("y"), lax.axis_index("c")
    out = []
    for a in range(len(g_refs)):
        for k, (fx, fy, fc) in enumerate(FLIPS):
            tx, ty, tc = x ^ fx, y ^ fy, c ^ fc
            out.append(pltpu.make_async_remote_copy(
                src_ref=_piece(g_refs[a], axes[a], 2 * tx + ty, tc), dst_ref=l_refs[a].at[k],
                send_sem=send.at[7 * a + k], recv_sem=recv.at[7 * a + k], device_id=(tx, ty, tc), device_id_type=MESH))
    return out


def _scatter_start(name, grads, axes):
    n = len(grads)
    lands = [lax.empty((7,) + _piece_shape(g.shape, ax), g.dtype) for g, ax in zip(grads, axes)]

    def body(*refs):
        g_refs, l_refs = refs[:n], refs[n:2 * n]
        send, recv, token = refs[2 * n], refs[2 * n + 1], refs[-1]
        for cp in _scatter_copies(g_refs, l_refs, axes, send, recv):
            cp.start()
        token[...] = jnp.zeros_like(token)

    outs = pl.pallas_call(
        body,
        name=name,
        out_shape=(pltpu.SemaphoreType.DMA((7 * n,)), pltpu.SemaphoreType.DMA((7 * n,)),
                   *[pltpu.HBM(t.shape, t.dtype) for t in grads + lands], _sds((8, LANES), F32)),
        in_specs=[IN_HBM] * (2 * n),
        out_specs=(IN_SEM, IN_SEM, *[IN_HBM] * (2 * n), _token_spec()),
        input_output_aliases={i: 2 + i for i in range(2 * n)},
        compiler_params=pltpu.CompilerParams(has_side_effects=DATAFLOW),
    )(*[_hbm(t) for t in grads + lands])
    return (outs[0], outs[1], list(outs[2:2 + n]), list(outs[2 + n:2 + 2 * n]), axes), outs[-1]


def _scatter_wait(name, state, *after):
    send, recv, g_thru, l_thru, axes = state
    n = len(g_thru)

    def body(*refs):
        g_refs, l_refs = refs[:n], refs[n:2 * n]
        for cp in _scatter_copies(g_refs, l_refs, axes, refs[2 * n], refs[2 * n + 1]):
            cp.wait_send()
            cp.wait_recv()

    outs = pl.pallas_call(
        body,
        name=name,
        out_shape=tuple(pltpu.HBM(t.shape, t.dtype) for t in g_thru + l_thru),
        in_specs=[IN_HBM] * (2 * n) + [IN_SEM, IN_SEM] + [HBM] * len(after),
        out_specs=tuple([IN_HBM] * (2 * n)),
        input_output_aliases={i: i for i in range(2 * n)},
        compiler_params=pltpu.CompilerParams(has_side_effects=DATAFLOW),
    )(*g_thru, *l_thru, send, recv, *after)
    return list(outs[:n]), list(outs[n:2 * n])


def _reduce_join(name, landing, g, axis):
    R, C = _piece_shape(g.shape, axis)
    l3 = landing.reshape(7, R, C)
    tr = _pick(R, [t for t in (512, 256, 128, 64, 32, 16, 8) if t * C <= 256 * 1024])
    nsteps = R // tr

    def own_block(i):
        q, c = 2 * lax.axis_index("x") + lax.axis_index("y"), lax.axis_index("c")
        return ((2 * q + c) * nsteps + i, 0) if axis == 0 else (c * nsteps + i, q)

    def body(own_ref, l_ref, o_ref, buf, send, loc, recv):
        i = pl.program_id(0)
        x, y, c = lax.axis_index("x"), lax.axis_index("y"), lax.axis_index("c")
        sibling = (x, y, 1 - c)

        def copies(slot, step):
            dst = o_ref.at[pl.ds(pl.multiple_of(c * R + step * tr, 8), tr), :]
            return (pltpu.make_async_copy(buf.at[slot], dst, loc.at[slot]),
                    pltpu.make_async_remote_copy(src_ref=buf.at[slot], dst_ref=dst, send_sem=send.at[slot], recv_sem=recv,
                                                 device_id=sibling, device_id_type=MESH))

        @pl.when(i >= 2)
        def _():
            lc, rc = copies(i % 2, i - 2)
            lc.wait()
            rc.wait_send()

        acc = own_ref[...].astype(F32)
        for s in range(7):
            acc = acc + l_ref[s].astype(F32)
        buf[i % 2] = acc
        lc, rc = copies(i % 2, i)
        lc.start()
        rc.start()

        @pl.when(i == nsteps - 1)
        def _():
            for st in range(max(nsteps - 2, 0), nsteps):
                lc, rc = copies(st % 2, st)
                lc.wait()
                rc.wait_send()
            theirs = o_ref.at[pl.ds(pl.multiple_of((1 - c) * R, 8), R), :]
            pltpu.make_async_remote_copy(src_ref=theirs, dst_ref=theirs, send_sem=send.at[0], recv_sem=recv,
                                         device_id=sibling, device_id_type=MESH).wait_recv()

    return pl.pallas_call(
        body,
        name=name,
        grid=(nsteps,),
        in_specs=[pl.BlockSpec((tr, C), own_block), pl.BlockSpec((7, tr, C), lambda i: (0, i, 0))],
        out_specs=HBM,
        out_shape=_sds((2 * R, C), F32),
        scratch_shapes=[pltpu.VMEM((2, tr, C), F32), pltpu.SemaphoreType.DMA((2,)), pltpu.SemaphoreType.DMA((2,)),
                        pltpu.SemaphoreType.DMA(())],
        compiler_params=_cparams(("arbitrary",)),
    )(g, l3)


def _all_reduce_small(v, dep):
    R, D = v.shape

    def body(v_ref, dep_ref, o_ref, land, send, recv):
        x, y, c = lax.axis_index("x"), lax.axis_index("y"), lax.axis_index("c")
        my_slot = 4 * x + 2 * y + c
        land[my_slot] = v_ref[...]
        for k, (fx, fy, fc) in enumerate(FLIPS):
            tx, ty, tc = x ^ fx, y ^ fy, c ^ fc
            pltpu.make_async_remote_copy(src_ref=v_ref, dst_ref=land.at[my_slot], send_sem=send.at[k], recv_sem=recv.at[k],
                                         device_id=(tx, ty, tc), device_id_type=MESH).start()
        for k, (fx, fy, fc) in enumerate(FLIPS):
            tx, ty, tc = x ^ fx, y ^ fy, c ^ fc
            cp = pltpu.make_async_remote_copy(src_ref=v_ref, dst_ref=land.at[4 * tx + 2 * ty + tc], send_sem=send.at[k],
                                              recv_sem=recv.at[k], device_id=(tx, ty, tc), device_id_type=MESH)
            cp.wait_send()
            cp.wait_recv()
        acc = land[0]
        for s in range(1, 8):
            acc = acc + land[s]
        o_ref[...] = acc

    return pl.pallas_call(
        body,
        name="all_reduce_small",
        in_specs=[pl.BlockSpec(memory_space=pltpu.VMEM), pl.BlockSpec(memory_space=pl.ANY)],
        out_specs=pl.BlockSpec(memory_space=pltpu.VMEM),
        out_shape=_sds((R, D), F32),
        scratch_shapes=[pltpu.VMEM((8, R, D), F32), pltpu.SemaphoreType.DMA((7,)), pltpu.SemaphoreType.DMA((7,))],
    )(v, dep)


def kernel(x, attn_w_in, attn_w_out, hgrn_w_in, hgrn_w_out, hgrn_norm_g, lb_logits, ln_mix_g, ln_mix_b, ln_ffn_g, ln_ffn_b, ffn_w_up, ffn_w_down, loss_target, m_attn_w_in, m_attn_w_out, m_hgrn_w_in, m_hgrn_w_out, m_hgrn_norm_g, m_lb_logits, m_ln_mix_g, m_ln_mix_b, m_ln_ffn_g, m_ln_ffn_b, m_ffn_w_up, m_ffn_w_down, v_attn_w_in, v_attn_w_out, v_hgrn_w_in, v_hgrn_w_out, v_hgrn_norm_g, v_lb_logits, v_ln_mix_g, v_ln_mix_b, v_ln_ffn_g, v_ln_ffn_b, v_ffn_w_up, v_ffn_w_down):
    xs = x[0]
    tgt = loss_target[0]
    S, D = xs.shape
    F = ffn_w_up.shape[2] * 4
    T1 = _pick(S, (1024, 512, 256))
    T2 = _pick(S, (2048, 1024, 512))
    TH = _pick(S, (512, 256))
    TB = _pick(S, (128,))
    TF = _pick(F, (1024, 512))
    TG = 3 * D // 4

    cast = lambda w: w.astype(MXU_DTYPE)
    st_a, tok = _gather_start("gather_a", [cast(attn_w_in[0])], [1], jnp.zeros((8, LANES), F32), halves=True)
    tok, (xs_late, w_aout, w_fup, w_fdown, w_hin, w_hout) = lax.optimization_barrier(
        (tok, (xs, attn_w_out, ffn_w_up, ffn_w_down, hgrn_w_in, hgrn_w_out)))
    st_b, tok = _gather_start("gather_b", [cast(w_aout[0]), cast(w_fup[0]), cast(w_fdown[0])], [0, 1, 0], tok)
    st_c, tok = _gather_start("gather_c", [cast(w_hin[0]), cast(w_hout[0]), hgrn_norm_g, cast(w_fup[1]), cast(w_fdown[1])],
                              [1, 0, 1, 1, 0], tok)

    cos3, sin3 = _rope_tables(S)
    sel = _head_sel(D)
    sel_t = sel.T

    xc3 = _stack_classes("x_classes", xs_late, MXU_DTYPE)
    P3 = _attn_proj("attn_proj_own", xc3, st_a[3][0], cos3, sin3, T2, None, tok)
    (wa_in,) = _gather_wait("gather_a_wait", st_a, P3)
    wa_in = _share_halves("share_a", wa_in, _pick(D // 2, (256, 128)))
    P3 = _attn_proj("attn_proj", xc3, wa_in, cos3, sin3, T2, P3)
    o3, lse3 = _attn_fwd(P3, D)
    o_att, L_att = _attn_mix(o3, lse3, sel)
    wa_out, w_up0, w_down0 = _gather_wait("gather_b_wait", st_b, L_att)
    ln1 = (ln_mix_g[0:1], ln_mix_b[0:1])
    ln2 = (ln_ffn_g[0:1], ln_ffn_b[0:1])
    ln3 = (ln_mix_g[1:2], ln_mix_b[1:2])
    ln4 = (ln_ffn_g[1:2], ln_ffn_b[1:2])
    xm1, xh1, r1 = _mm_res_ln("attn_out_ln", o_att, wa_out, xs, *ln1, TH, D)
    a0 = _mlp_up("mlp0_up", xm1, w_up0, T2, TF, D)
    xm2, xh2, r2 = _mm_res_ln("mlp0_down_ln", a0, w_down0, (xh1, *ln1), *ln2, TH, F)

    wh_in, wh_out, norm_g, w_up1, w_down1 = _gather_wait("gather_c_wait", st_c, r2)
    P1 = _plain_mm("hgrn_proj", xm2, wh_in, "nn", F32, T1, _pick(3 * D, (1024, 768, 512)), D)
    o_h, n_h, states = _hgrn_fwd(P1, lb_logits, norm_g, TB)
    xm3, xh3, r3 = _mm_res_ln("hgrn_out_ln", n_h, wh_out, (xh2, *ln2), *ln3, TH, D)
    a1 = _mlp_up("mlp1_up", xm3, w_up1, T2, TF, D)
    _, xh4, r4 = _mm_res_ln("mlp1_down_ln", a1, w_down1, (xh3, *ln3), *ln4, TH, F)

    wgrad = lambda name, a, dy, tm, tn, tk=T1: _plain_mm(name, a, dy, "tn", MXU_DTYPE, tm, tn, tk)
    sq, du4, dum4, dg_ffn1, db_ffn1 = _loss_ln_bwd(tgt, xh4, r4, *ln4, TH)
    dh1 = _mlp_down_bwd("mlp1_down_bwd", dum4, w_down1, a1, T2, TF, D)
    g_down1 = wgrad("g_down1", a1, dum4, TF, D, S)
    g_up1 = wgrad("g_up1", xm3, dh1, D, TF, S)
    sc_1, tok = _scatter_start("scatter_1", [g_down1, g_up1], [0, 1])
    du3, dum3, dg_mix1, db_mix1 = _mm_nt_res_ln_bwd("mlp1_up_bwd", dh1, w_up1, du4, xh3, r3, ln_mix_g[1:2], TH, F, tok)
    dn = _plain_mm("hgrn_out_bwd", dum3, wh_out, "nt", F32, T1, D, D)
    g_hout = wgrad("g_hgrn_out", n_h, dum3, D, D)
    dP1, dg_norm, dlb = _hgrn_bwd(P1, o_h, states, dn, lb_logits, norm_g, TB)
    g_hin = wgrad("g_hgrn_in", xm2, dP1, D, D, S)
    d_lb_logits = _lb_logits_grad(dlb, lb_logits)
    sc_2, tok = _scatter_start("scatter_2", [g_hout, g_hin], [0, 1])

    du2, dum2, dg_ffn0, db_ffn0 = _mm_nt_res_ln_bwd("hgrn_in_bwd", dP1, wh_in, du3, xh2, r2, ln_ffn_g[0:1], TH, 3 * D, tok)
    dh0 = _mlp_down_bwd("mlp0_down_bwd", dum2, w_down0, a0, T2, TF, D)
    g_down0 = wgrad("g_down0", a0, dum2, TF, D, S)
    g_up0 = wgrad("g_up0", xm1, dh0, D, TF, S)
    sc_3, tok = _scatter_start("scatter_3", [g_down0, g_up0], [0, 1])
    du1, dum1, dg_mix0, db_mix0 = _mm_nt_res_ln_bwd("mlp0_up_bwd", dh0, w_up0, du2, xh1, r1, ln_mix_g[0:1], TH, F, tok)
    do, delta = _attn_out_bwd(dum1, wa_out, o_att, sel_t, TH, D)
    g_aout = wgrad("g_attn_out", o_att, dum1, D, D)
    sc_5, tok = _scatter_start("scatter_5", [g_aout], [0])
    dP3 = _attn_bwd(P3, _stack_classes("do_classes", do, MXU_DTYPE), _stack_classes("lse_classes", L_att, F32),
                    _stack_classes("delta_classes", delta, F32), cos3, sin3, D, tok)
    small = jnp.concatenate([d_lb_logits, dg_mix0, dg_mix1, db_mix0, db_mix1, dg_ffn0, dg_ffn1, db_ffn0, db_ffn1,
                             dg_norm, sq, jnp.zeros((4, D), F32)], axis=0)
    small = _all_reduce_small(small, dP3)
    loss = 0.5 * jnp.sum(small[11]) / D
    grp = lambda j: j // (3 * D // TG)
    g_ain = _matmul("g_attn_in", xc3, dP3, "tn", D, TG, S, [(_sds((D, 9 * D), MXU_DTYPE), _ij_spec(D, TG))], _store_epilogue,
                    a_map=lambda i, j, k: (grp(j), i),
                    b_map=lambda i, j, k: (grp(j), j % (3 * D // TG)), mnk=(D, 9 * D, S), dep=small)[0]
    sc_4, tok = _scatter_start("scatter_4", [g_ain], [1])
    dxc3 = _matmul("attn_in_bwd", dP3, wa_in, "nt", T1, D, 3 * D, [(_sds((3 * S, D), F32), _ij_spec(T1, D))], _store_epilogue,
                   b_map=lambda i, j, k: (j, k + i // (S // T1)), mnk=(3 * S, D, 3 * D), dep=tok)[0]
    grad_x = _input_grad(du1, dxc3)

    def reduced(name, state, *after):
        gs, lands = _scatter_wait(name + "_wait", state, *after)
        return [_reduce_join(f"{name}_reduce_{i}", l, g, ax) for i, (l, g, ax) in enumerate(zip(lands, gs, state[4]))]

    r_down1, r_up1 = reduced("scatter_1", sc_1, grad_x)
    r_hout, r_hin = reduced("scatter_2", sc_2, r_up1)
    r_down0, r_up0 = reduced("scatter_3", sc_3, r_hin)
    (r_aout,) = reduced("scatter_5", sc_5, r_up0)

    my_chip = 2 * lax.axis_index("x") + lax.axis_index("y")
    nsh = hgrn_norm_g.shape[1]
    g_norm = lax.dynamic_slice(small[10:11], (0, my_chip * nsh), (1, nsh))

    grads, upd = {}, {}

    def update(nm, w, gs, m, v):
        upd[nm] = _adamw("adamw_" + nm, w, gs, m, v)
        grads[nm] = upd[nm][3]

    update("hgrn_w_in", hgrn_w_in, [r_hin], m_hgrn_w_in, v_hgrn_w_in)
    update("hgrn_w_out", hgrn_w_out, [r_hout], m_hgrn_w_out, v_hgrn_w_out)
    update("ffn_w_up", ffn_w_up, [r_up0, r_up1], m_ffn_w_up, v_ffn_w_up)
    update("ffn_w_down", ffn_w_down, [r_down0, r_down1], m_ffn_w_down, v_ffn_w_down)
    update("attn_w_out", attn_w_out, [r_aout], m_attn_w_out, v_attn_w_out)
    update("hgrn_norm_g", hgrn_norm_g, [g_norm], m_hgrn_norm_g, v_hgrn_norm_g)
    cat = lambda ts: jnp.concatenate(ts, axis=0)
    small_w = cat([lb_logits, ln_mix_g, ln_mix_b, ln_ffn_g, ln_ffn_b])
    small_m = cat([m_lb_logits, m_ln_mix_g, m_ln_mix_b, m_ln_ffn_g, m_ln_ffn_b])
    small_v = cat([v_lb_logits, v_ln_mix_g, v_ln_mix_b, v_ln_ffn_g, v_ln_ffn_b])
    small_upd = _adamw("adamw_small", small_w, [small[0:10]], small_m, small_v)
    for i, nm in enumerate(["lb_logits", "ln_mix_g", "ln_mix_b", "ln_ffn_g", "ln_ffn_b"]):
        grads[nm] = small[2 * i:2 * i + 2]
        upd[nm] = tuple(t[2 * i:2 * i + 2] for t in small_upd)
    done = [upd[k][2] for k in ("hgrn_w_in", "hgrn_w_out", "ffn_w_up", "ffn_w_down", "attn_w_out", "hgrn_norm_g")]
    (r_ain,) = reduced("scatter_4", sc_4, small_upd[2], *done)
    update("attn_w_in", attn_w_in, [r_ain], m_attn_w_in, v_attn_w_in)

    order = ["attn_w_in", "attn_w_out", "hgrn_w_in", "hgrn_w_out", "hgrn_norm_g", "lb_logits", "ln_mix_g", "ln_mix_b",
             "ln_ffn_g", "ln_ffn_b", "ffn_w_up", "ffn_w_down"]
    return (loss, grad_x[None], *[grads[k] for k in order], *[upd[k][0] for k in order],
            *[upd[k][1] for k in order], *[upd[k][2] for k in order])
```

```python
import math

import jax
import jax.numpy as jnp
from jax import lax
from jax.experimental import pallas as pl
from jax.experimental.pallas import tpu as pltpu

F32 = jnp.float32
BF16 = jnp.bfloat16
MXU_DTYPE = BF16

HEAD_DIM = 64
ATTN_BLK = 128
DILATIONS = (1, 4, 16)
ROPE_THETA = 10000.0
HGRN_DK = 128
HGRN_CHUNK = 64
DEPTH = 2
LN_EPS = 1e-5
RMS_EPS = 1e-6
ALPHA = (2 * DEPTH) ** 0.25
ADAM_LR, ADAM_B1, ADAM_B2, ADAM_EPS, ADAM_WD, ADAM_STEP = 0.001, 0.9, 0.999, 1e-08, 0.01, 10

LANES = 128
VMEM_LIMIT = 56 * 1024 * 1024
NEG = -1e30
MESH = pl.DeviceIdType.MESH


def _cparams(sem=None):
    return pltpu.CompilerParams(dimension_semantics=sem, vmem_limit_bytes=VMEM_LIMIT)


def _sds(shape, dtype):
    return jax.ShapeDtypeStruct(tuple(shape), dtype)


def _dg(a, b, ca, cb):
    return lax.dot_general(a, b, (((ca,), (cb,)), ((), ())), preferred_element_type=F32)


def _nn(a, b):
    return _dg(a, b, 1, 0)


def _nt(a, b):
    return _dg(a, b, 1, 1)


def _tn(a, b):
    return _dg(a, b, 0, 0)


def _split3(a):
    hi = a.astype(BF16)
    r = a - hi.astype(F32)
    mid = r.astype(BF16)
    lo = (r - mid.astype(F32)).astype(BF16)
    return hi, mid, lo


def _exact_nn(a, sel):
    hi, mid, lo = _split3(a)
    return _nn(hi, sel) + _nn(mid, sel) + _nn(lo, sel)


def _pick(n, prefs):
    for p in prefs:
        if n % p == 0:
            return p
    return n


def _matmul(name, a, b, form, tm, tn, tk, outs, epilogue, extras=(), a_map=None, b_map=None, mnk=None, dep=None,
            sem=("parallel", "parallel", "arbitrary"), split=None, alias_dep=False):
    if form == "nn":
        (M, K), N = a.shape, b.shape[1]
        a_spec = pl.BlockSpec((tm, tk), a_map or (lambda i, j, k: (i, k)))
        b_spec = pl.BlockSpec((tk, tn), b_map or (lambda i, j, k: (k, j)))
        ca, cb = 1, 0
    elif form == "nt":
        (M, K), N = a.shape, b.shape[0]
        a_spec = pl.BlockSpec((tm, tk), a_map or (lambda i, j, k: (i, k)))
        b_spec = pl.BlockSpec((tn, tk), b_map or (lambda i, j, k: (j, k)))
        ca, cb = 1, 1
    else:
        (K, M), N = a.shape, b.shape[1]
        a_spec = pl.BlockSpec((tk, tm), a_map or (lambda i, j, k: (k, i)))
        b_spec = pl.BlockSpec((tk, tn), b_map or (lambda i, j, k: (k, j)))
        ca, cb = 0, 0
    if mnk is not None:
        M, N, K = mnk
    assert M % tm == 0 and N % tn == 0 and K % tk == 0, (name, M, N, K, tm, tn, tk)
    nk = K // tk
    ne, no = len(extras), len(outs)
    deps = [] if dep is None else [dep]
    nd = len(deps)

    def body(a_ref, b_ref, *rest):
        extra_refs, out_refs = rest[:ne], rest[ne + nd:ne + nd + no]
        j = pl.program_id(1)
        if split is not None:
            kind, n = split
            assert nk == 1 and form != "tn"
            tiled = [t for _, _, *t in list(extras) + list(outs)]
            refs = list(extra_refs) + list(out_refs)
            for ci in range(n):
                if kind == "cols":
                    cs = slice(ci * (tn // n), (ci + 1) * (tn // n))
                    part = _dg(a_ref[...].astype(MXU_DTYPE), (b_ref[:, cs] if form == "nn" else b_ref[cs, :]).astype(MXU_DTYPE), ca, cb)
                    view = [r.at[:, cs] if t else r for r, t in zip(refs, tiled)]
                else:
                    rs = slice(ci * (tm // n), (ci + 1) * (tm // n))
                    part = _dg(a_ref[rs, :].astype(MXU_DTYPE), b_ref[...].astype(MXU_DTYPE), ca, cb)
                    view = [r.at[rs, :] if t else r for r, t in zip(refs, tiled)]
                epilogue(part, view[:ne], view[ne:], j, ci)
            return
        part = _dg(a_ref[...].astype(MXU_DTYPE), b_ref[...].astype(MXU_DTYPE), ca, cb)
        if nk == 1:
            epilogue(part, extra_refs, out_refs, j, 0)
            return
        acc_ref = rest[-1]
        k = pl.program_id(2)

        @pl.when(k == 0)
        def _():
            acc_ref[...] = part

        @pl.when(k > 0)
        def _():
            acc_ref[...] += part

        @pl.when(k == nk - 1)
        def _():
            epilogue(acc_ref[...], extra_refs, out_refs, j, 0)

    res = pl.pallas_call(
        body,
        name=name,
        grid=(M // tm, N // tn, nk),
        in_specs=[a_spec, b_spec] + [s for _, s, *_ in extras] + [pl.BlockSpec(memory_space=pl.ANY)] * nd,
        out_specs=[s for _, s, *_ in outs],
        out_shape=[o for o, *_ in outs],
        scratch_shapes=[pltpu.VMEM((tm, tn), F32)] if nk > 1 else [],
        input_output_aliases={2 + ne: 0} if alias_dep else {},
        compiler_params=_cparams(sem),
    )(a, b, *[e for e, *_ in extras], *deps)
    return res


def _ij_spec(tm, tn):
    return pl.BlockSpec((tm, tn), lambda i, j, k: (i, j))


def _store_epilogue(acc, extra_refs, out_refs, j, ci):
    out_refs[0][...] = acc.astype(out_refs[0].dtype)


def _plain_mm(name, a, b, form, out_dtype, tm, tn, tk):
    M = a.shape[1] if form == "tn" else a.shape[0]
    N = b.shape[0] if form == "nt" else b.shape[1]
    return _matmul(name, a, b, form, tm, tn, tk, [(_sds((M, N), out_dtype), _ij_spec(tm, tn))], _store_epilogue)[0]


def _class_slabs(S):
    assert DILATIONS[0] == 1
    return [(g, d, r, S // d) for g, d in enumerate(DILATIONS) if d > 1 for r in range(d)]


def _stack_classes(name, t, out_dtype):
    S, W = t.shape

    def body(x_ref, o_ref):
        o_ref[0:S, :] = x_ref[...].astype(out_dtype)
        for g, d, r, n in _class_slabs(S):
            o_ref[g * S + r * n:g * S + (r + 1) * n, :] = x_ref[pl.ds(r, n, stride=d), :].astype(out_dtype)

    return pl.pallas_call(
        body,
        name=name,
        grid=(W // LANES,),
        in_specs=[pl.BlockSpec((S, LANES), lambda j: (0, j))],
        out_specs=pl.BlockSpec((3 * S, LANES), lambda j: (0, j)),
        out_shape=_sds((3 * S, W), out_dtype),
        compiler_params=_cparams(("parallel",)),
    )(t)


def _rope_tables(seq):
    half = HEAD_DIM // 2
    inv = ROPE_THETA ** (-jnp.arange(half, dtype=F32) * (2.0 / HEAD_DIM))
    inv = jnp.tile(inv, LANES // half)
    pos = []
    for d in DILATIONS:
        row = jnp.arange(seq)
        pos.append((row % (seq // d)) * d + row // (seq // d))
    ang = jnp.concatenate(pos).astype(F32)[:, None] * inv[None, :]
    first = (jnp.arange(LANES) % HEAD_DIM) < half
    sin = jnp.sin(ang)
    return jnp.cos(ang), jnp.where(first[None, :], -sin, sin)


def _partner(x):
    half = HEAD_DIM // 2
    lane = lax.broadcasted_iota(jnp.int32, x.shape, 1)
    first = (lane % HEAD_DIM) < half
    return jnp.where(first, pltpu.roll(x, LANES - half, 1), pltpu.roll(x, half, 1))


def _attn_proj(name, x3, w, cos3, sin3, tm, prev, dep=None):
    S3, D = x3.shape
    S = S3 // 3
    tn = 3 * D // 4
    nrow = S // tm
    local = prev is None

    def tile(j):
        q = 2 * lax.axis_index("x") + lax.axis_index("y")
        c0 = 3 * q + j if local else j + 3 * (j >= 3 * q).astype(jnp.int32)
        return c0, c0 // 4, c0 % 4

    def epilogue(acc, extra_refs, out_refs, j, ci):
        cos_ref, sin_ref = extra_refs
        o_ref = out_refs[0]
        _, _, place = tile(j)
        width = acc.shape[1]
        assert D % width == 0
        is_rot = (place * tn + ci * width) // D < 2
        c = jnp.where(is_rot, cos_ref[...], 1.0)
        s = jnp.where(is_rot, sin_ref[...], 0.0)
        for t in range(width // LANES):
            xs = acc[:, t * LANES:(t + 1) * LANES]
            o_ref[:, t * LANES:(t + 1) * LANES] = (xs * c + _partner(xs) * s).astype(o_ref.dtype)

    rows = lambda i, j: tile(j)[1] * nrow + i
    tab = pl.BlockSpec((tm, LANES), lambda i, j, k: (rows(i, j), 0))
    out = pl.BlockSpec((tm, tn), lambda i, j, k: (rows(i, j), tile(j)[2]))
    ntiles = 3 if local else 9
    return _matmul(name, x3, w, "nn", tm, tn, D, [(_sds((S3, 3 * D), MXU_DTYPE), out, True)], epilogue,
                   extras=[(cos3, tab), (sin3, tab)], a_map=lambda i, j, k: (rows(i, j), k),
                   b_map=lambda i, j, k: (k, j if local else tile(j)[0]), mnk=(nrow * tm, ntiles * tn, D),
                   dep=dep if local else prev, alias_dep=not local, split=("cols", 3))[0]


def _head_sel(d_model):
    h = jnp.arange(LANES)[:, None]
    l = jnp.arange(d_model)[None, :]
    return (l // HEAD_DIM == h).astype(BF16)


def _class_edges(b, nblk):
    g = b // nblk
    per_class = jnp.where(g == 0, nblk // DILATIONS[0], jnp.where(g == 1, nblk // DILATIONS[1], nblk // DILATIONS[2]))
    pos = (b % nblk) % per_class
    return pos != 0, pos != per_class - 1


def _two_heads(t, top):
    zero = jnp.zeros_like(t)
    return jnp.concatenate([jnp.where(top, t, zero), jnp.where(top, zero, t)], axis=0)


def _band_mask(has_prev):
    B = ATTN_BLK
    row = lax.broadcasted_iota(jnp.int32, (2 * B, 2 * B), 0) % B
    col = lax.broadcasted_iota(jnp.int32, (2 * B, 2 * B), 1)
    in_prev = jnp.logical_and(jnp.logical_and(col < B, col >= row), has_prev)
    in_own = jnp.logical_and(col >= B, col - B <= row)
    return jnp.logical_or(in_prev, in_own)


def _attn_fwd(P3, D):
    S3 = P3.shape[0]
    B = ATTN_BLK
    nblk = S3 // 3 // B
    npairs = D // LANES
    scale = HEAD_DIM ** -0.5

    def body(q_ref, kc_ref, vc_ref, kp_ref, vp_ref, o_ref, lse_ref):
        has_prev, _ = _class_edges(pl.program_id(0), nblk)
        bias = jnp.where(_band_mask(has_prev), 0.0, NEG)
        lane = lax.broadcasted_iota(jnp.int32, (B, LANES), 1)
        top = lane < HEAD_DIM
        lse_acc = jnp.zeros((B, LANES), F32)
        for j in range(npairs):
            sl = slice(j * LANES, (j + 1) * LANES)
            Q = _two_heads(q_ref[:, sl] * scale, top)
            K2 = jnp.concatenate([kp_ref[:, sl], kc_ref[:, sl]], axis=0)
            V2 = jnp.concatenate([vp_ref[:, sl], vc_ref[:, sl]], axis=0)
            s = _nt(Q, K2) + bias
            m = jnp.max(s, axis=1, keepdims=True)
            p = jnp.exp(s - m)
            l = jnp.sum(p, axis=1, keepdims=True)
            o = _nn(p.astype(MXU_DTYPE), V2) * (1.0 / l)
            o_ref[:, sl] = jnp.where(top, o[:B], o[B:])
            lse = m + jnp.log(l)
            lse_acc = jnp.where(lane == 2 * j, lse[:B], jnp.where(lane == 2 * j + 1, lse[B:], lse_acc))
        lse_ref[...] = lse_acc

    blk = lambda part, prev: pl.BlockSpec(
        (B, D), (lambda b: (jnp.maximum(b - 1, 0), part)) if prev else (lambda b: (b, part)))
    return pl.pallas_call(
        body,
        name="attn_fwd",
        grid=(3 * nblk,),
        in_specs=[blk(0, False), blk(1, False), blk(2, False), blk(1, True), blk(2, True)],
        out_specs=[pl.BlockSpec((B, D), lambda b: (b, 0)), pl.BlockSpec((B, LANES), lambda b: (b, 0))],
        out_shape=[_sds((S3, D), F32), _sds((S3, LANES), F32)],
        compiler_params=_cparams(("parallel",)),
    )(P3, P3, P3, P3, P3)


def _attn_mix(o3, lse3, sel):
    S3, D = o3.shape
    S = S3 // 3

    def body(o3_ref, lse_ref, sel_ref, o_ref, L_ref, w_ref):
        @pl.when(pl.program_id(0) == 0)
        def _():
            w_ref[0] = lse_ref[0:S, :]
            for g, d, r, n in _class_slabs(S):
                w_ref[g, pl.ds(r, n, stride=d), :] = lse_ref[g * S + r * n:g * S + (r + 1) * n, :]
            a, b, c = w_ref[0], w_ref[1], w_ref[2]
            m = jnp.maximum(jnp.maximum(a, b), c)
            L = m + jnp.log(jnp.exp(a - m) + jnp.exp(b - m) + jnp.exp(c - m))
            L_ref[...] = L
            w_ref[0] = jnp.exp(a - L)
            w_ref[1] = jnp.exp(b - L)
            w_ref[2] = jnp.exp(c - L)

        s = sel_ref[...]
        o_ref[...] = _exact_nn(w_ref[0], s) * o3_ref[0:S, :]
        for g, d, r, n in _class_slabs(S):
            rows = pl.ds(r, n, stride=d)
            o_ref[rows, :] += _exact_nn(w_ref[g, rows, :], s) * o3_ref[g * S + r * n:g * S + (r + 1) * n, :]

    return pl.pallas_call(
        body,
        name="attn_mix",
        grid=(D // LANES,),
        in_specs=[pl.BlockSpec((S3, LANES), lambda j: (0, j)), pl.BlockSpec((S3, LANES), lambda j: (0, 0)),
                  pl.BlockSpec((LANES, LANES), lambda j: (0, j))],
        out_specs=[pl.BlockSpec((S, LANES), lambda j: (0, j)), pl.BlockSpec((S, LANES), lambda j: (0, 0))],
        out_shape=[_sds((S, D), F32), _sds((S, LANES), F32)],
        scratch_shapes=[pltpu.VMEM((3, S, LANES), F32)],
        compiler_params=_cparams(("arbitrary",)),
    )(o3, lse3, sel)


def _attn_bwd(P3, do3, L3, delta3, cos3, sin3, D, dep):
    S3 = P3.shape[0]
    B = ATTN_BLK
    nblk = S3 // 3 // B
    npairs = D // LANES
    scale = HEAD_DIM ** -0.5

    def body(c_ref, kp_ref, vp_ref, qn_ref, doc_ref, don_ref, Lc_ref, Ln_ref, dc_ref, dn_ref, cos_ref, sin_ref, dep_ref, out_ref):
        has_prev, has_next = _class_edges(pl.program_id(0), nblk)
        bias = jnp.where(_band_mask(has_prev), 0.0, NEG)
        row = lax.broadcasted_iota(jnp.int32, (2 * B, B), 0) % B
        col = lax.broadcasted_iota(jnp.int32, (2 * B, B), 1)
        bias_n = jnp.where(jnp.logical_and(col >= row, has_next), 0.0, NEG)
        lane = lax.broadcasted_iota(jnp.int32, (B, LANES), 1)
        top = lane < HEAD_DIM
        cos_t = cos_ref[...]
        sin_inv = -sin_ref[...]
        Lc_all, Ln_all, dc_all, dn_all = Lc_ref[...], Ln_ref[...], dc_ref[...], dn_ref[...]
        pair_col = lambda t, j: jnp.concatenate([t[:, 2 * j:2 * j + 1], t[:, 2 * j + 1:2 * j + 2]], axis=0)
        for j in range(npairs):
            sl = lambda part: slice(part * D + j * LANES, part * D + (j + 1) * LANES)
            pj = slice(j * LANES, (j + 1) * LANES)
            kc2, vc2 = c_ref[:, sl(1)], c_ref[:, sl(2)]
            K2 = jnp.concatenate([kp_ref[:, pj], kc2], axis=0)
            V2 = jnp.concatenate([vp_ref[:, pj], vc2], axis=0)
            Qc = _two_heads(c_ref[:, sl(0)] * scale, top)
            Qn = _two_heads(qn_ref[:, pj] * scale, top)
            DOc = _two_heads(doc_ref[:, pj].astype(MXU_DTYPE), top)
            DOn = _two_heads(don_ref[:, pj].astype(MXU_DTYPE), top)
            P_c = jnp.exp(_nt(Qc, K2) + bias - pair_col(Lc_all, j))
            dS_c = P_c * (_nt(DOc, V2) - pair_col(dc_all, j))
            P_n = jnp.exp(_nt(Qn, kc2) + bias_n - pair_col(Ln_all, j))
            dS_n = P_n * (_nt(DOn, vc2) - pair_col(dn_all, j))
            dq = _nn(dS_c.astype(MXU_DTYPE), K2)
            dq2 = jnp.where(top, dq[:B], dq[B:]) * scale
            Qk = jnp.concatenate([Qc, Qn], axis=0)
            DOk = jnp.concatenate([DOc, DOn], axis=0)
            dk2 = _tn(jnp.concatenate([dS_c[:, B:], dS_n], axis=0).astype(MXU_DTYPE), Qk)
            dv2 = _tn(jnp.concatenate([P_c[:, B:], P_n], axis=0).astype(MXU_DTYPE), DOk)
            out_ref[:, sl(0)] = (dq2 * cos_t + _partner(dq2) * sin_inv).astype(out_ref.dtype)
            out_ref[:, sl(1)] = (dk2 * cos_t + _partner(dk2) * sin_inv).astype(out_ref.dtype)
            out_ref[:, sl(2)] = dv2.astype(out_ref.dtype)

    cur = lambda b: b
    prv = lambda b: jnp.maximum(b - 1, 0)
    nxt = lambda b: jnp.minimum(b + 1, 3 * nblk - 1)
    spec = lambda w, f, part=0: pl.BlockSpec((B, w), lambda b: (f(b), part))
    return pl.pallas_call(
        body,
        name="attn_bwd",
        grid=(3 * nblk,),
        in_specs=[spec(3 * D, cur), spec(D, prv, 1), spec(D, prv, 2), spec(D, nxt, 0), spec(D, cur), spec(D, nxt),
                  spec(LANES, cur), spec(LANES, nxt), spec(LANES, cur), spec(LANES, nxt), spec(LANES, cur), spec(LANES, cur),
                  pl.BlockSpec(memory_space=pl.ANY)],
        out_specs=spec(3 * D, cur),
        out_shape=_sds((S3, 3 * D), MXU_DTYPE),
        compiler_params=_cparams(("parallel",)),
    )(P3, P3, P3, P3, do3, do3, L3, L3, delta3, delta3, cos3, sin3, dep)


def _input_grad(du, dx3):
    S, D = du.shape

    def body(du_ref, dx_ref, o_ref):
        o_ref[...] = ALPHA * du_ref[...] + dx_ref[0:S, :]
        for g, d, r, n in _class_slabs(S):
            o_ref[pl.ds(r, n, stride=d), :] += dx_ref[g * S + r * n:g * S + (r + 1) * n, :]

    return pl.pallas_call(
        body,
        name="input_grad",
        grid=(D // LANES,),
        in_specs=[pl.BlockSpec((S, LANES), lambda j: (0, j)), pl.BlockSpec((3 * S, LANES), lambda j: (0, j))],
        out_specs=pl.BlockSpec((S, LANES), lambda j: (0, j)),
        out_shape=_sds((S, D), F32),
        compiler_params=_cparams(("parallel",)),
    )(du, dx3)


def _chunk_causal(tb):
    r = lax.broadcasted_iota(jnp.int32, (tb, tb), 0)
    c = lax.broadcasted_iota(jnp.int32, (tb, tb), 1)
    return jnp.logical_and((r // HGRN_CHUNK) == (c // HGRN_CHUNK), r >= c)


def _chunk_sums(a, lower):
    C = HGRN_CHUNK
    r = lax.broadcasted_iota(jnp.int32, (C, C), 0)
    c = lax.broadcasted_iota(jnp.int32, (C, C), 1)
    tri = ((r >= c) if lower else (r <= c)).astype(BF16)
    parts = _split3(a)
    out = []
    for ci in range(a.shape[0] // C):
        rows = slice(ci * C, (ci + 1) * C)
        out.append(_nn(tri, parts[0][rows]) + _nn(tri, parts[1][rows]) + _nn(tri, parts[2][rows]))
    return jnp.concatenate(out, axis=0)


def _chunk_last(b):
    C = HGRN_CHUNK
    return jnp.concatenate([jnp.broadcast_to(b[(ci + 1) * C - 1:(ci + 1) * C, :], (C, b.shape[1]))
                            for ci in range(b.shape[0] // C)], axis=0)


def _lower_bound(lb_ref):
    l0, l1 = lb_ref[0:1, :], lb_ref[1:2, :]
    m = jnp.maximum(l0, l1)
    e0, e1 = jnp.exp(l0 - m), jnp.exp(l1 - m)
    return e1 / (e0 + e1)


def _hgrn_gates(q_raw, z, lb):
    sg = 1.0 / (1.0 + jnp.exp(-z))
    sn = 1.0 / (1.0 + jnp.exp(z))
    f = lb + (1.0 - lb) * sg
    key = (1.0 - lb) * sn
    sq = 1.0 / (1.0 + jnp.exp(-q_raw))
    return sg, sn, f, key, sq


def _hgrn_fwd(P1, lb_logits, norm_g, tb):
    S = P1.shape[0]
    D = P1.shape[1] // 3
    K = HGRN_DK
    H = D // K
    HP = H
    C = HGRN_CHUNK
    cpb = tb // C
    nt = S // tb

    def body(q_ref, f_ref, i_ref, lb_ref, g_ref, o_ref, n_ref, st_ref, state):
        t = pl.program_id(1)

        @pl.when(t == 0)
        def _():
            state[...] = jnp.zeros_like(state)

        lb_all = _lower_bound(lb_ref)
        low = _chunk_causal(tb)
        for hh in range(HP):
            lanes = slice(hh * K, (hh + 1) * K)
            q_raw, z, v = q_ref[:, lanes], f_ref[:, lanes], i_ref[:, lanes]
            sg, sn, f, key, sq = _hgrn_gates(q_raw, z, lb_all[:, lanes])
            b = _chunk_sums(jnp.log(f), lower=True)
            qd = (q_raw * sq * jnp.exp(b)).astype(MXU_DTYPE)
            kd = (key * jnp.exp(-b)).astype(MXU_DTYPE)
            kb = (key * jnp.exp(_chunk_last(b) - b)).astype(MXU_DTYPE)
            vm = v.astype(MXU_DTYPE)
            a = jnp.where(low, _nt(qd, kd), 0.0).astype(MXU_DTYPE)
            o_intra = _nn(a, vm)
            st = state[hh]
            outs = []
            for ci in range(cpb):
                rows = slice(ci * C, (ci + 1) * C)
                st_ref[hh, ci] = st
                outs.append(o_intra[rows] + _nt(qd[rows], st.astype(MXU_DTYPE)))
                st = st * jnp.exp(b[(ci + 1) * C - 1:(ci + 1) * C, :]) + _tn(vm[rows], kb[rows])
            state[hh] = st
            o = jnp.concatenate(outs, axis=0)
            o_ref[:, lanes] = o
            rs = lax.rsqrt(jnp.mean(o * o, axis=1, keepdims=True) + RMS_EPS)
            n_ref[:, lanes] = o * rs * g_ref[:, lanes]

    tok = lambda part: pl.BlockSpec((tb, HP * K), lambda h, t: (t, part * (H // HP) + h))
    vec = lambda rows: pl.BlockSpec((rows, HP * K), lambda h, t: (0, h))
    return pl.pallas_call(
        body,
        name="hgrn_fwd",
        grid=(H // HP, nt),
        in_specs=[tok(0), tok(1), tok(2), vec(2), vec(1)],
        out_specs=[tok(0), tok(0), pl.BlockSpec((HP, cpb, K, K), lambda h, t: (h, t, 0, 0))],
        out_shape=[_sds((S, D), F32), _sds((S, D), F32), _sds((H, S // C, K, K), F32)],
        scratch_shapes=[pltpu.VMEM((HP, K, K), F32)],
        compiler_params=_cparams(("parallel", "arbitrary")),
    )(P1, P1, P1, lb_logits, norm_g)


def _hgrn_bwd(P1, o_pre, states, dn, lb_logits, norm_g, tb):
    S = P1.shape[0]
    D = P1.shape[1] // 3
    K = HGRN_DK
    H = D // K
    HP = H
    C = HGRN_CHUNK
    cpb = tb // C
    nt = S // tb

    def body(q_ref, f_ref, i_ref, o_ref, st_ref, dn_ref, lb_ref, g_ref, d_ref, dg_ref, dlb_ref, dstate):
        t = pl.program_id(1)

        @pl.when(t == 0)
        def _():
            dstate[...] = jnp.zeros_like(dstate)
            dg_ref[...] = jnp.zeros_like(dg_ref)
            dlb_ref[...] = jnp.zeros_like(dlb_ref)

        lb_all = _lower_bound(lb_ref)
        low = _chunk_causal(tb)
        for hh in range(HP):
            lanes = slice(hh * K, (hh + 1) * K)
            lb = lb_all[:, lanes]
            gn = g_ref[:, lanes]
            q_raw, z, v = q_ref[:, lanes], f_ref[:, lanes], i_ref[:, lanes]
            sg, sn, f, key, sq = _hgrn_gates(q_raw, z, lb)
            b = _chunk_sums(jnp.log(f), lower=True)
            e_pos, e_neg, e_rel = jnp.exp(b), jnp.exp(-b), jnp.exp(_chunk_last(b) - b)
            qd_f, kd_f, kb_f = q_raw * sq * e_pos, key * e_neg, key * e_rel
            qd, kd, kb = qd_f.astype(MXU_DTYPE), kd_f.astype(MXU_DTYPE), kb_f.astype(MXU_DTYPE)
            vm = v.astype(MXU_DTYPE)
            a = jnp.where(low, _nt(qd, kd), 0.0).astype(MXU_DTYPE)
            o = o_ref[:, lanes]
            dnn = dn_ref[:, lanes]
            rs = lax.rsqrt(jnp.mean(o * o, axis=1, keepdims=True) + RMS_EPS)
            dg_ref[:, lanes] += jnp.sum(dnn * o * rs, axis=0, keepdims=True)
            tg = dnn * gn
            dom = (rs * tg - o * (rs * rs * rs) * jnp.mean(tg * o, axis=1, keepdims=True)).astype(MXU_DTYPE)
            da = jnp.where(low, _nt(dom, vm), 0.0).astype(MXU_DTYPE)
            dv = _tn(a, dom)
            dqd = _nn(da, kd)
            dkd = _tn(da, qd)
            dst = dstate[hh]
            dv_s, dqd_s, dkb_s, dbl_s = [None] * cpb, [None] * cpb, [None] * cpb, [None] * cpb
            for ci in reversed(range(cpb)):
                rows = slice(ci * C, (ci + 1) * C)
                st = st_ref[hh, ci]
                dstm = dst.astype(MXU_DTYPE)
                dec = jnp.exp(b[(ci + 1) * C - 1:(ci + 1) * C, :])
                dv_s[ci] = _nt(kb[rows], dstm)
                dkb_s[ci] = _nn(vm[rows], dstm)
                dqd_s[ci] = _nn(dom[rows], st.astype(MXU_DTYPE))
                db_last = jnp.sum(dkb_s[ci] * kb_f[rows], axis=0, keepdims=True) + jnp.sum(dst * st, axis=0, keepdims=True) * dec
                dbl_s[ci] = jnp.broadcast_to(db_last, (C, K))
                dst = dst * dec + _tn(dom[rows], qd[rows])
            dstate[hh] = dst
            dv = dv + jnp.concatenate(dv_s, axis=0)
            dqd = dqd + jnp.concatenate(dqd_s, axis=0)
            dkb = jnp.concatenate(dkb_s, axis=0)
            dkey = dkd * e_neg + dkb * e_rel
            db = dqd * qd_f - dkd * kd_f - dkb * kb_f
            dlogf = _chunk_sums(db, lower=False) + jnp.concatenate(dbl_s, axis=0)
            gz = (1.0 - lb) * sg * sn
            col = lambda part: slice(part * D + hh * K, part * D + (hh + 1) * K)
            d_ref[:, col(0)] = (dqd * e_pos * (sq + q_raw * sq * (1.0 - sq))).astype(d_ref.dtype)
            d_ref[:, col(1)] = (dlogf * gz / f - dkey * gz).astype(d_ref.dtype)
            d_ref[:, col(2)] = dv.astype(d_ref.dtype)
            dlb_ref[:, lanes] += jnp.sum(dlogf * sn / f - dkey * sn, axis=0, keepdims=True)

    rev = lambda t: nt - 1 - t
    tok = lambda part: pl.BlockSpec((tb, HP * K), lambda h, t: (rev(t), part * (H // HP) + h))
    vec = lambda rows: pl.BlockSpec((rows, HP * K), lambda h, t: (0, h))
    outs = pl.pallas_call(
        body,
        name="hgrn_bwd",
        grid=(H // HP, nt),
        in_specs=[tok(0), tok(1), tok(2), tok(0),
                  pl.BlockSpec((HP, cpb, K, K), lambda h, t: (h, rev(t), 0, 0)),
                  tok(0), vec(2), vec(1)],
        out_specs=[pl.BlockSpec((tb, 3 * D), lambda h, t: (rev(t), 0)), vec(1), vec(1)],
        out_shape=[_sds((S, 3 * D), MXU_DTYPE)] + [_sds((1, D), F32)] * 2,
        scratch_shapes=[pltpu.VMEM((HP, K, K), F32)],
        compiler_params=_cparams(("parallel", "arbitrary")),
    )(P1, P1, P1, o_pre, states, dn, lb_logits, norm_g)
    return outs


def _lb_logits_grad(dlb, lb_logits):
    def body(d_ref, l_ref, o_ref):
        s1 = _lower_bound(l_ref)
        d = d_ref[...]
        o_ref[0:1, :] = -(1.0 - s1) * s1 * d
        o_ref[1:2, :] = s1 * (1.0 - s1) * d

    return pl.pallas_call(body, name="lb_logits_grad", out_shape=_sds(lb_logits.shape, F32))(dlb, lb_logits)


def _mm_res_ln(name, a, w_full, res, g, b, tm, tk):
    from_ln = isinstance(res, tuple)
    S, D = (res[0] if from_ln else res).shape

    def epilogue(acc, extra_refs, out_refs, j, ci):
        g_ref, b_ref = extra_refs[:2]
        xm_ref, xhat_ref, rstd_ref = out_refs
        r = extra_refs[2][...] * extra_refs[3][...] + extra_refs[4][...] if from_ln else extra_refs[2][...]
        u = ALPHA * r + acc
        mu = jnp.mean(u, axis=1, keepdims=True)
        cen = u - mu
        rstd = lax.rsqrt(jnp.mean(cen * cen, axis=1, keepdims=True) + LN_EPS)
        xhat = cen * rstd
        xhat_ref[...] = xhat
        xm_ref[...] = (xhat * g_ref[...] + b_ref[...]).astype(xm_ref.dtype)
        rstd_ref[...] = rstd

    row = pl.BlockSpec((tm, D), lambda i, j, k: (i, 0))
    vec = pl.BlockSpec((1, D), lambda i, j, k: (0, 0))
    outs = [(_sds((S, D), MXU_DTYPE), row, True), (_sds((S, D), F32), row, True),
            (_sds((S, 1), F32), pl.BlockSpec((tm, 1), lambda i, j, k: (i, 0)), True)]
    res_extras = [(res[0], row, True), (res[1], vec), (res[2], vec)] if from_ln else [(res, row, True)]
    return _matmul(name, a, w_full, "nn", tm, D, tk, outs, epilogue, extras=[(g, vec), (b, vec)] + res_extras,
                   split=("rows", 2) if tk == a.shape[1] else None)


def _ln_bwd_rows(dy, xh, rstd, g, first, du_ref, dum_ref, dg_ref, db_ref):
    if first is not None:
        @pl.when(first)
        def _():
            dg_ref[...] = jnp.zeros_like(dg_ref)
            db_ref[...] = jnp.zeros_like(db_ref)

    dg_ref[...] += jnp.sum(dy * xh, axis=0, keepdims=True)
    db_ref[...] += jnp.sum(dy, axis=0, keepdims=True)
    dxh = dy * g
    m1 = jnp.mean(dxh, axis=1, keepdims=True)
    m2 = jnp.mean(dxh * xh, axis=1, keepdims=True)
    du = rstd * (dxh - m1 - xh * m2)
    du_ref[...] = du
    dum_ref[...] = du.astype(dum_ref.dtype)


def _loss_ln_bwd(target, xhat, rstd, g, b, tm):
    S, D = xhat.shape

    def body(t_ref, xh_ref, r_ref, g_ref, b_ref, sq_ref, du_ref, dum_ref, dg_ref, db_ref):
        first = pl.program_id(0) == 0

        @pl.when(first)
        def _():
            sq_ref[...] = jnp.zeros_like(sq_ref)

        xh = xh_ref[...]
        e = xh * g_ref[...] + b_ref[...] - t_ref[...]
        sq_ref[...] += jnp.sum(e * e, axis=0, keepdims=True)
        _ln_bwd_rows(e / D, xh, r_ref[...], g_ref[...], first, du_ref, dum_ref, dg_ref, db_ref)

    row = pl.BlockSpec((tm, D), lambda i: (i, 0))
    vec = pl.BlockSpec((1, D), lambda i: (0, 0))
    return pl.pallas_call(
        body,
        name="loss_ln_bwd",
        grid=(S // tm,),
        in_specs=[row, row, pl.BlockSpec((tm, 1), lambda i: (i, 0)), vec, vec],
        out_specs=[vec, row, row, vec, vec],
        out_shape=[_sds((1, D), F32), _sds((S, D), F32), _sds((S, D), MXU_DTYPE), _sds((1, D), F32), _sds((1, D), F32)],
        compiler_params=_cparams(("arbitrary",)),
    )(target, xhat, rstd, g, b)


def _mlp_up(name, x, w_up, tm, tn, tk):
    S = x.shape[0]
    F = w_up.shape[1]

    def epilogue(acc, extra_refs, out_refs, j, ci):
        r = jnp.maximum(acc, 0.0)
        out_refs[0][...] = (r * r).astype(out_refs[0].dtype)

    return _matmul(name, x, w_up, "nn", tm, tn, tk, [(_sds((S, F), MXU_DTYPE), _ij_spec(tm, tn), True)], epilogue,
                   split=("cols", 2))[0]


def _mlp_down_bwd(name, dy, w_down, a, tm, tn, tk):
    S, F = a.shape

    def epilogue(acc, extra_refs, out_refs, j, ci):
        out_refs[0][...] = (acc * (2.0 * jnp.sqrt(extra_refs[0][...]).astype(F32))).astype(out_refs[0].dtype)

    return _matmul(name, dy, w_down, "nt", tm, tn, tk, [(_sds((S, F), MXU_DTYPE), _ij_spec(tm, tn), True)], epilogue,
                   extras=[(a, _ij_spec(tm, tn), True)], split=("cols", 2))[0]


def _mm_nt_res_ln_bwd(name, dy, w, du, xhat, rstd, g, tm, tk, dep):
    S, D = du.shape

    def epilogue(acc, extra_refs, out_refs, j, ci):
        du_ref, xh_ref, r_ref, g_ref = extra_refs
        first = (pl.program_id(0) == 0) if ci == 0 else None
        _ln_bwd_rows(ALPHA * du_ref[...] + acc, xh_ref[...], r_ref[...], g_ref[...], first, *out_refs)

    row = pl.BlockSpec((tm, D), lambda i, j, k: (i, 0))
    vec = pl.BlockSpec((1, D), lambda i, j, k: (0, 0))
    return _matmul(name, dy, w, "nt", tm, D, tk,
                   [(_sds((S, D), F32), row, True), (_sds((S, D), MXU_DTYPE), row, True), (_sds((1, D), F32), vec),
                    (_sds((1, D), F32), vec)], epilogue,
                   extras=[(du, row, True), (xhat, row, True), (rstd, pl.BlockSpec((tm, 1), lambda i, j, k: (i, 0)), True), (g, vec)],
                   dep=dep, sem=("arbitrary", "arbitrary", "arbitrary"), split=("rows", 2))


def _attn_out_bwd(du, w_out, o, sel_t, tm, tk):
    S, D = o.shape

    def epilogue(acc, extra_refs, out_refs, j, ci):
        out_refs[0][...] = acc
        out_refs[1][...] = _exact_nn(acc * extra_refs[0][...], extra_refs[1][...])

    row = pl.BlockSpec((tm, D), lambda i, j, k: (i, 0))
    slim = pl.BlockSpec((tm, LANES), lambda i, j, k: (i, 0))
    return _matmul("attn_out_bwd", du, w_out, "nt", tm, D, tk,
                   [(_sds((S, D), F32), row, True), (_sds((S, LANES), F32), slim, True)], epilogue,
                   extras=[(o, row, True), (sel_t, pl.BlockSpec((D, LANES), lambda i, j, k: (0, 0)))], split=("rows", 2))


def _adamw(name, w, gs, m, v):
    shape = w.shape
    cols = shape[-1]
    rows = math.prod(shape[:-1])
    w2, m2, v2 = (t.reshape(rows, cols) for t in (w, m, v))
    gs2 = [g.reshape(-1, cols) for g in gs]
    ng = len(gs2)
    tr = _pick(rows // ng, (256, 128, 64, 32, 16, 8))
    per = rows // ng // tr
    c1 = 1.0 - ADAM_B1 ** ADAM_STEP
    c2 = 1.0 - ADAM_B2 ** ADAM_STEP

    def body(w_ref, m_ref, v_ref, *rest):
        g_refs, (d_ref, nm_ref, nv_ref), g_out = rest[:ng], rest[ng:ng + 3], rest[ng + 3:]
        gg = g_refs[0][...]
        if ng == 2:
            gg = jnp.where(pl.program_id(0) < per, gg, g_refs[1][...])
        g_out[0][...] = gg
        nm = ADAM_B1 * m_ref[...] + (1.0 - ADAM_B1) * gg
        nv = ADAM_B2 * v_ref[...] + (1.0 - ADAM_B2) * (gg * gg)
        nm_ref[...] = nm
        nv_ref[...] = nv
        d_ref[...] = -ADAM_LR * ((nm / c1) / (jnp.sqrt(nv / c2) + ADAM_EPS) + ADAM_WD * w_ref[...])

    blk = pl.BlockSpec((tr, cols), lambda i: (i, 0))
    g_specs = [blk] if ng == 1 else [pl.BlockSpec((tr, cols), lambda i: (jnp.minimum(i, per - 1), 0)),
                                     pl.BlockSpec((tr, cols), lambda i: (jnp.maximum(i - per, 0), 0))]
    outs = pl.pallas_call(
        body,
        name=name,
        grid=(rows // tr,),
        in_specs=[blk] * 3 + g_specs,
        out_specs=[blk] * 4,
        out_shape=[_sds((rows, cols), F32)] * 4,
        compiler_params=_cparams(("parallel",)),
    )(w2, m2, v2, *gs2)
    return tuple(o.reshape(shape) for o in outs)


HBM = pl.BlockSpec(memory_space=pl.ANY)


def _shard_slice(ref, axis, size, index):
    idx = [slice(None)] * len(ref.shape)
    idx[axis] = pl.ds(pl.multiple_of(index * size, 8), size)
    return ref.at[tuple(idx)]


def _share_halves(name, full, tr):
    R, W4 = full.shape
    W, h = W4 // 4, R // 2
    steps = [(k, t) for k in range(3) for t in range(h // tr)]

    def body(f_in, f_ref, buf, lsem, ssem, rsem):
        x, y, c = lax.axis_index("x"), lax.axis_index("y"), lax.axis_index("c")
        sibling = (x, y, 1 - c)
        chips = [(1 - x, y), (x, 1 - y), (1 - x, 1 - y)]

        def tile(k, t):
            px, py = chips[k]
            return f_ref.at[pl.ds(pl.multiple_of(c * h + t * tr, 8), tr), pl.ds(pl.multiple_of((2 * px + py) * W, LANES), W)]

        sends = []
        for s, (k, t) in enumerate(steps):
            slot = s % 2
            if s >= 2:
                sends[s - 2].wait_send()
            lc = pltpu.make_async_copy(tile(k, t), buf.at[slot], lsem.at[slot])
            lc.start()
            lc.wait()
            rc = pltpu.make_async_remote_copy(src_ref=buf.at[slot], dst_ref=tile(k, t), send_sem=ssem.at[slot], recv_sem=rsem,
                                              device_id=sibling, device_id_type=MESH)
            rc.start()
            sends.append(rc)
        for rc in sends[-2:]:
            rc.wait_send()
        whole = f_ref.at[pl.ds(0, h), pl.ds(0, 3 * W)]
        pltpu.make_async_remote_copy(src_ref=whole, dst_ref=whole, send_sem=ssem.at[0], recv_sem=rsem,
                                     device_id=sibling, device_id_type=MESH).wait_recv()

    return pl.pallas_call(
        body,
        name=name,
        in_specs=[HBM],
        out_specs=HBM,
        out_shape=_sds(full.shape, full.dtype),
        input_output_aliases={0: 0},
        scratch_shapes=[pltpu.VMEM((2, tr, W), full.dtype), pltpu.SemaphoreType.DMA((2,)), pltpu.SemaphoreType.DMA((2,)),
                        pltpu.SemaphoreType.DMA(())],
    )(full)


IN_HBM = pl.BlockSpec(memory_space=pltpu.HBM)
IN_SEM = pl.BlockSpec(memory_space=pltpu.SEMAPHORE)
DATAFLOW = pltpu.SideEffectType.DATAFLOW_SIDE_EFFECTING


def _hbm(t):
    return pltpu.with_memory_space_constraint(t, pltpu.HBM)


def _token_spec():
    return pl.BlockSpec(memory_space=pltpu.VMEM)


def _gather_copies(s_refs, f_refs, axes, halves, send, recv, loc, arrival):
    x, y, c = lax.axis_index("x"), lax.axis_index("y"), lax.axis_index("c")
    chips = [(1 - x, y), (x, 1 - y), (1 - x, 1 - y)]
    local, remote = [], []
    for a in range(len(s_refs)):
        size = s_refs[a].shape[axes[a]]
        local.append(pltpu.make_async_copy(s_refs[a], _shard_slice(f_refs[a], axes[a], size, 2 * x + y), loc.at[a]))
        for k, (px, py) in enumerate(chips):
            block = (2 * px + py) if arrival else (2 * x + y)
            src, dst = s_refs[a], _shard_slice(f_refs[a], axes[a], size, block)
            if halves:
                assert axes[a] == 1 and len(s_refs[a].shape) == 2
                h = s_refs[a].shape[0] // 2
                rows = pl.ds(pl.multiple_of(c * h, 8), h)
                src = s_refs[a].at[rows, :]
                dst = f_refs[a].at[rows, pl.ds(pl.multiple_of(block * size, LANES), size)]
            remote.append(pltpu.make_async_remote_copy(src_ref=src, dst_ref=dst, send_sem=send.at[3 * a + k],
                                                       recv_sem=recv.at[3 * a + k], device_id=(px, py, c), device_id_type=MESH))
    return local, remote


def _gather_start(name, shards, axes, after, halves=False):
    n = len(shards)
    fulls = []
    for s, ax in zip(shards, axes):
        fs = list(s.shape)
        fs[ax] *= 4
        fulls.append(lax.empty(tuple(fs), s.dtype))

    def body(*refs):
        s_refs, f_refs = refs[:n], refs[n:2 * n]
        send, recv, loc, token = refs[2 * n + 1], refs[2 * n + 2], refs[2 * n + 3], refs[-1]
        local, remote = _gather_copies(s_refs, f_refs, axes, halves, send, recv, loc, arrival=False)
        for cp in remote + local:
            cp.start()
        token[...] = jnp.zeros_like(token)

    outs = pl.pallas_call(
        body,
        name=name,
        out_shape=(pltpu.SemaphoreType.DMA((3 * n,)), pltpu.SemaphoreType.DMA((3 * n,)), pltpu.SemaphoreType.DMA((n,)),
                   *[pltpu.HBM(t.shape, t.dtype) for t in shards + fulls], _sds((8, LANES), F32)),
        in_specs=[IN_HBM] * (2 * n) + [HBM],
        out_specs=(IN_SEM, IN_SEM, IN_SEM, *[IN_HBM] * (2 * n), _token_spec()),
        input_output_aliases={i: 3 + i for i in range(2 * n)},
        compiler_params=pltpu.CompilerParams(has_side_effects=DATAFLOW),
    )(*[_hbm(t) for t in shards + fulls], after)
    return (outs[0], outs[1], outs[2], list(outs[3:3 + n]), list(outs[3 + n:3 + 2 * n]), axes, halves), outs[-1]


def _gather_wait(name, state, *after):
    send, recv, loc, s_thru, f_thru, axes, halves = state
    n = len(s_thru)

    def body(*refs):
        s_refs, f_refs = refs[:n], refs[n:2 * n]
        local, remote = _gather_copies(s_refs, f_refs, axes, halves, refs[2 * n], refs[2 * n + 1], refs[2 * n + 2], arrival=True)
        for cp in local:
            cp.wait()
        for cp in remote:
            cp.wait_send()
            cp.wait_recv()

    outs = pl.pallas_call(
        body,
        name=name,
        out_shape=tuple(pltpu.HBM(t.shape, t.dtype) for t in s_thru + f_thru),
        in_specs=[IN_HBM] * (2 * n) + [IN_SEM, IN_SEM, IN_SEM] + [HBM] * len(after),
        out_specs=tuple([IN_HBM] * (2 * n)),
        input_output_aliases={i: i for i in range(2 * n)},
        compiler_params=pltpu.CompilerParams(has_side_effects=DATAFLOW),
    )(*s_thru, *f_thru, send, recv, loc, *after)
    return list(outs[n:2 * n])


FLIPS = [(fx, fy, fc) for fx in (0, 1) for fy in (0, 1) for fc in (0, 1)][1:]


def _piece_shape(shape, axis):
    ps = list(shape)
    if axis == 0:
        ps[0] //= 8
    else:
        ps[0] //= 2
        ps[axis] //= 4
    return tuple(ps)


def _piece(ref, axis, q, c):
    shape = ref.shape
    idx = [slice(None)] * len(shape)
    if axis == 0:
        h = shape[0] // 8
        idx[0] = pl.ds(pl.multiple_of((2 * q + c) * h, 8), h)
    else:
        h, w = shape[0] // 2, shape[axis] // 4
        idx[0] = pl.ds(c * h, h)
        idx[axis] = pl.ds(pl.multiple_of(q * w, LANES if axis == len(shape) - 1 else 8), w)
    return ref.at[tuple(idx)]


def _scatter_copies(g_refs, l_refs, axes, send, recv):
    x, y, c = lax.axis_index("x"), lax.axis_index("y"), lax.axis_index("c")
    out = []
    for a in range(len(g_refs)):
        for k, (fx, fy, fc) in enumerate(FLIPS):
            tx, ty, tc = x ^ fx, y ^ fy, c ^ fc
            out.append(pltpu.make_async_remote_copy(
                src_ref=_piece(g_refs[a], axes[a], 2 * tx + ty, tc), dst_ref=l_refs[a].at[k],
                send_sem=send.at[7 * a + k], recv_sem=recv.at[7 * a + k], device_id=(tx, ty, tc), device_id_type=MESH))
    return out


def _scatter_start(name, grads, axes):
    n = len(grads)
    lands = [lax.empty((7,) + _piece_shape(g.shape, ax), g.dtype) for g, ax in zip(grads, axes)]

    def body(*refs):
        g_refs, l_refs = refs[:n], refs[n:2 * n]
        send, recv, token = refs[2 * n], refs[2 * n + 1], refs[-1]
        for cp in _scatter_copies(g_refs, l_refs, axes, send, recv):
            cp.start()
        token[...] = jnp.zeros_like(token)

    outs = pl.pallas_call(
        body,
        name=name,
        out_shape=(pltpu.SemaphoreType.DMA((7 * n,)), pltpu.SemaphoreType.DMA((7 * n,)),
                   *[pltpu.HBM(t.shape, t.dtype) for t in grads + lands], _sds((8, LANES), F32)),
        in_specs=[IN_HBM] * (2 * n),
        out_specs=(IN_SEM, IN_SEM, *[IN_HBM] * (2 * n), _token_spec()),
        input_output_aliases={i: 2 + i for i in range(2 * n)},
        compiler_params=pltpu.CompilerParams(has_side_effects=DATAFLOW),
    )(*[_hbm(t) for t in grads + lands])
    return (outs[0], outs[1], list(outs[2:2 + n]), list(outs[2 + n:2 + 2 * n]), axes), outs[-1]


def _scatter_wait(name, state, *after):
    send, recv, g_thru, l_thru, axes = state
    n = len(g_thru)

    def body(*refs):
        g_refs, l_refs = refs[:n], refs[n:2 * n]
        for cp in _scatter_copies(g_refs, l_refs, axes, refs[2 * n], refs[2 * n + 1]):
            cp.wait_send()
            cp.wait_recv()

    outs = pl.pallas_call(
        body,
        name=name,
        out_shape=tuple(pltpu.HBM(t.shape, t.dtype) for t in g_thru + l_thru),
        in_specs=[IN_HBM] * (2 * n) + [IN_SEM, IN_SEM] + [HBM] * len(after),
        out_specs=tuple([IN_HBM] * (2 * n)),
        input_output_aliases={i: i for i in range(2 * n)},
        compiler_params=pltpu.CompilerParams(has_side_effects=DATAFLOW),
    )(*g_thru, *l_thru, send, recv, *after)
    return list(outs[:n]), list(outs[n:2 * n])


def _reduce_join(name, landing, g, axis):
    R, C = _piece_shape(g.shape, axis)
    l3 = landing.reshape(7, R, C)
    tr = _pick(R, [t for t in (512, 256, 128, 64, 32, 16, 8) if t * C <= 256 * 1024])
    nsteps = R // tr

    def own_block(i):
        q, c = 2 * lax.axis_index("x") + lax.axis_index("y"), lax.axis_index("c")
        return ((2 * q + c) * nsteps + i, 0) if axis == 0 else (c * nsteps + i, q)

    def body(own_ref, l_ref, o_ref, buf, send, loc, recv):
        i = pl.program_id(0)
        x, y, c = lax.axis_index("x"), lax.axis_index("y"), lax.axis_index("c")
        sibling = (x, y, 1 - c)

        def copies(slot, step):
            dst = o_ref.at[pl.ds(pl.multiple_of(c * R + step * tr, 8), tr), :]
            return (pltpu.make_async_copy(buf.at[slot], dst, loc.at[slot]),
                    pltpu.make_async_remote_copy(src_ref=buf.at[slot], dst_ref=dst, send_sem=send.at[slot], recv_sem=recv,
                                                 device_id=sibling, device_id_type=MESH))

        @pl.when(i >= 2)
        def _():
            lc, rc = copies(i % 2, i - 2)
            lc.wait()
            rc.wait_send()

        acc = own_ref[...].astype(F32)
        for s in range(7):
            acc = acc + l_ref[s].astype(F32)
        buf[i % 2] = acc
        lc, rc = copies(i % 2, i)
        lc.start()
        rc.start()

        @pl.when(i == nsteps - 1)
        def _():
            for st in range(max(nsteps - 2, 0), nsteps):
                lc, rc = copies(st % 2, st)
                lc.wait()
                rc.wait_send()
            theirs = o_ref.at[pl.ds(pl.multiple_of((1 - c) * R, 8), R), :]
            pltpu.make_async_remote_copy(src_ref=theirs, dst_ref=theirs, send_sem=send.at[0], recv_sem=recv,
                                         device_id=sibling, device_id_type=MESH).wait_recv()

    return pl.pallas_call(
        body,
        name=name,
        grid=(nsteps,),
        in_specs=[pl.BlockSpec((tr, C), own_block), pl.BlockSpec((7, tr, C), lambda i: (0, i, 0))],
        out_specs=HBM,
        out_shape=_sds((2 * R, C), F32),
        scratch_shapes=[pltpu.VMEM((2, tr, C), F32), pltpu.SemaphoreType.DMA((2,)), pltpu.SemaphoreType.DMA((2,)),
                        pltpu.SemaphoreType.DMA(())],
        compiler_params=_cparams(("arbitrary",)),
    )(g, l3)


def _all_reduce_small(v, dep):
    R, D = v.shape

    def body(v_ref, dep_ref, o_ref, land, send, recv):
        x, y, c = lax.axis_index("x"), lax.axis_index("y"), lax.axis_index("c")
        my_slot = 4 * x + 2 * y + c
        land[my_slot] = v_ref[...]
        for k, (fx, fy, fc) in enumerate(FLIPS):
            tx, ty, tc = x ^ fx, y ^ fy, c ^ fc
            pltpu.make_async_remote_copy(src_ref=v_ref, dst_ref=land.at[my_slot], send_sem=send.at[k], recv_sem=recv.at[k],
                                         device_id=(tx, ty, tc), device_id_type=MESH).start()
        for k, (fx, fy, fc) in enumerate(FLIPS):
            tx, ty, tc = x ^ fx, y ^ fy, c ^ fc
            cp = pltpu.make_async_remote_copy(src_ref=v_ref, dst_ref=land.at[4 * tx + 2 * ty + tc], send_sem=send.at[k],
                                              recv_sem=recv.at[k], device_id=(tx, ty, tc), device_id_type=MESH)
            cp.wait_send()
            cp.wait_recv()
        acc = land[0]
        for s in range(1, 8):
            acc = acc + land[s]
        o_ref[...] = acc

    return pl.pallas_call(
        body,
        name="all_reduce_small",
        in_specs=[pl.BlockSpec(memory_space=pltpu.VMEM), pl.BlockSpec(memory_space=pl.ANY)],
        out_specs=pl.BlockSpec(memory_space=pltpu.VMEM),
        out_shape=_sds((R, D), F32),
        scratch_shapes=[pltpu.VMEM((8, R, D), F32), pltpu.SemaphoreType.DMA((7,)), pltpu.SemaphoreType.DMA((7,))],
    )(v, dep)


def kernel(x, attn_w_in, attn_w_out, hgrn_w_in, hgrn_w_out, hgrn_norm_g, lb_logits, ln_mix_g, ln_mix_b, ln_ffn_g, ln_ffn_b, ffn_w_up, ffn_w_down, loss_target, m_attn_w_in, m_attn_w_out, m_hgrn_w_in, m_hgrn_w_out, m_hgrn_norm_g, m_lb_logits, m_ln_mix_g, m_ln_mix_b, m_ln_ffn_g, m_ln_ffn_b, m_ffn_w_up, m_ffn_w_down, v_attn_w_in, v_attn_w_out, v_hgrn_w_in, v_hgrn_w_out, v_hgrn_norm_g, v_lb_logits, v_ln_mix_g, v_ln_mix_b, v_ln_ffn_g, v_ln_ffn_b, v_ffn_w_up, v_ffn_w_down):
    xs = x[0]
    tgt = loss_target[0]
    S, D = xs.shape
    F = ffn_w_up.shape[2] * 4
    T1 = _pick(S, (1024, 512, 256))
    T2 = _pick(S, (2048, 1024, 512))
    TH = _pick(S, (512, 256))
    TB = _pick(S, (128,))
    TF = _pick(F, (1024, 512))
    TG = 3 * D // 4

    cast = lambda w: w.astype(MXU_DTYPE)
    st_a, tok = _gather_start("gather_a", [cast(attn_w_in[0])], [1], jnp.zeros((8, LANES), F32), halves=True)
    tok, (xs_late, w_aout, w_fup, w_fdown, w_hin, w_hout) = lax.optimization_barrier(
        (tok, (xs, attn_w_out, ffn_w_up, ffn_w_down, hgrn_w_in, hgrn_w_out)))
    st_b, tok = _gather_start("gather_b", [cast(w_aout[0]), cast(w_fup[0]), cast(w_fdown[0])], [0, 1, 0], tok)
    st_c, tok = _gather_start("gather_c", [cast(w_hin[0]), cast(w_hout[0]), hgrn_norm_g, cast(w_fup[1]), cast(w_fdown[1])],
                              [1, 0, 1, 1, 0], tok)

    cos3, sin3 = _rope_tables(S)
    sel = _head_sel(D)
    sel_t = sel.T

    xc3 = _stack_classes("x_classes", xs_late, MXU_DTYPE)
    P3 = _attn_proj("attn_proj_own", xc3, st_a[3][0], cos3, sin3, T2, None, tok)
    (wa_in,) = _gather_wait("gather_a_wait", st_a, P3)
    wa_in = _share_halves("share_a", wa_in, _pick(D // 2, (256, 128)))
    P3 = _attn_proj("attn_proj", xc3, wa_in, cos3, sin3, T2, P3)
    o3, lse3 = _attn_fwd(P3, D)
    o_att, L_att = _attn_mix(o3, lse3, sel)
    wa_out, w_up0, w_down0 = _gather_wait("gather_b_wait", st_b, L_att)
    ln1 = (ln_mix_g[0:1], ln_mix_b[0:1])
    ln2 = (ln_ffn_g[0:1], ln_ffn_b[0:1])
    ln3 = (ln_mix_g[1:2], ln_mix_b[1:2])
    ln4 = (ln_ffn_g[1:2], ln_ffn_b[1:2])
    xm1, xh1, r1 = _mm_res_ln("attn_out_ln", o_att, wa_out, xs, *ln1, TH, D)
    a0 = _mlp_up("mlp0_up", xm1, w_up0, T2, TF, D)
    xm2, xh2, r2 = _mm_res_ln("mlp0_down_ln", a0, w_down0, (xh1, *ln1), *ln2, TH, F)

    wh_in, wh_out, norm_g, w_up1, w_down1 = _gather_wait("gather_c_wait", st_c, r2)
    P1 = _plain_mm("hgrn_proj", xm2, wh_in, "nn", F32, T1, _pick(3 * D, (1024, 768, 512)), D)
    o_h, n_h, states = _hgrn_fwd(P1, lb_logits, norm_g, TB)
    xm3, xh3, r3 = _mm_res_ln("hgrn_out_ln", n_h, wh_out, (xh2, *ln2), *ln3, TH, D)
    a1 = _mlp_up("mlp1_up", xm3, w_up1, T2, TF, D)
    _, xh4, r4 = _mm_res_ln("mlp1_down_ln", a1, w_down1, (xh3, *ln3), *ln4, TH, F)

    wgrad = lambda name, a, dy, tm, tn, tk=T1: _plain_mm(name, a, dy, "tn", MXU_DTYPE, tm, tn, tk)
    sq, du4, dum4, dg_ffn1, db_ffn1 = _loss_ln_bwd(tgt, xh4, r4, *ln4, TH)
    dh1 = _mlp_down_bwd("mlp1_down_bwd", dum4, w_down1, a1, T2, TF, D)
    g_down1 = wgrad("g_down1", a1, dum4, TF, D, S)
    g_up1 = wgrad("g_up1", xm3, dh1, D, TF, S)
    sc_1, tok = _scatter_start("scatter_1", [g_down1, g_up1], [0, 1])
    du3, dum3, dg_mix1, db_mix1 = _mm_nt_res_ln_bwd("mlp1_up_bwd", dh1, w_up1, du4, xh3, r3, ln_mix_g[1:2], TH, F, tok)
    dn = _plain_mm("hgrn_out_bwd", dum3, wh_out, "nt", F32, T1, D, D)
    g_hout = wgrad("g_hgrn_out", n_h, dum3, D, D)
    dP1, dg_norm, dlb = _hgrn_bwd(P1, o_h, states, dn, lb_logits, norm_g, TB)
    g_hin = wgrad("g_hgrn_in", xm2, dP1, D, D, S)
    d_lb_logits = _lb_logits_grad(dlb, lb_logits)
    sc_2, tok = _scatter_start("scatter_2", [g_hout, g_hin], [0, 1])

    du2, dum2, dg_ffn0, db_ffn0 = _mm_nt_res_ln_bwd("hgrn_in_bwd", dP1, wh_in, du3, xh2, r2, ln_ffn_g[0:1], TH, 3 * D, tok)
    dh0 = _mlp_down_bwd("mlp0_down_bwd", dum2, w_down0, a0, T2, TF, D)
    g_down0 = wgrad("g_down0", a0, dum2, TF, D, S)
    g_up0 = wgrad("g_up0", xm1, dh0, D, TF, S)
    sc_3, tok = _scatter_start("scatter_3", [g_down0, g_up0], [0, 1])
    du1, dum1, dg_mix0, db_mix0 = _mm_nt_res_ln_bwd("mlp0_up_bwd", dh0, w_up0, du2, xh1, r1, ln_mix_g[0:1], TH, F, tok)
    do, delta = _attn_out_bwd(dum1, wa_out, o_att, sel_t, TH, D)
    g_aout = wgrad("g_attn_out", o_att, dum1, D, D)
    sc_5, tok = _scatter_start("scatter_5", [g_aout], [0])
    dP3 = _attn_bwd(P3, _stack_classes("do_classes", do, MXU_DTYPE), _stack_classes("lse_classes", L_att, F32),
                    _stack_classes("delta_classes", delta, F32), cos3, sin3, D, tok)
    small = jnp.concatenate([d_lb_logits, dg_mix0, dg_mix1, db_mix0, db_mix1, dg_ffn0, dg_ffn1, db_ffn0, db_ffn1,
                             dg_norm, sq, jnp.zeros((4, D), F32)], axis=0)
    small = _all_reduce_small(small, dP3)
    loss = 0.5 * jnp.sum(small[11]) / D
    grp = lambda j: j // (3 * D // TG)
    g_ain = _matmul("g_attn_in", xc3, dP3, "tn", D, TG, S, [(_sds((D, 9 * D), MXU_DTYPE), _ij_spec(D, TG))], _store_epilogue,
                    a_map=lambda i, j, k: (grp(j), i),
                    b_map=lambda i, j, k: (grp(j), j % (3 * D // TG)), mnk=(D, 9 * D, S), dep=small)[0]
    sc_4, tok = _scatter_start("scatter_4", [g_ain], [1])
    dxc3 = _matmul("attn_in_bwd", dP3, wa_in, "nt", T1, D, 3 * D, [(_sds((3 * S, D), F32), _ij_spec(T1, D))], _store_epilogue,
                   b_map=lambda i, j, k: (j, k + i // (S // T1)), mnk=(3 * S, D, 3 * D), dep=tok)[0]
    grad_x = _input_grad(du1, dxc3)

    def reduced(name, state, *after):
        gs, lands = _scatter_wait(name + "_wait", state, *after)
        return [_reduce_join(f"{name}_reduce_{i}", l, g, ax) for i, (l, g, ax) in enumerate(zip(lands, gs, state[4]))]

    r_down1, r_up1 = reduced("scatter_1", sc_1, grad_x)
    r_hout, r_hin = reduced("scatter_2", sc_2, r_up1)
    r_down0, r_up0 = reduced("scatter_3", sc_3, r_hin)
    (r_aout,) = reduced("scatter_5", sc_5, r_up0)

    my_chip = 2 * lax.axis_index("x") + lax.axis_index("y")
    nsh = hgrn_norm_g.shape[1]
    g_norm = lax.dynamic_slice(small[10:11], (0, my_chip * nsh), (1, nsh))

    grads, upd = {}, {}

    def update(nm, w, gs, m, v):
        upd[nm] = _adamw("adamw_" + nm, w, gs, m, v)
        grads[nm] = upd[nm][3]

    update("hgrn_w_in", hgrn_w_in, [r_hin], m_hgrn_w_in, v_hgrn_w_in)
    update("hgrn_w_out", hgrn_w_out, [r_hout], m_hgrn_w_out, v_hgrn_w_out)
    update("ffn_w_up", ffn_w_up, [r_up0, r_up1], m_ffn_w_up, v_ffn_w_up)
    update("ffn_w_down", ffn_w_down, [r_down0, r_down1], m_ffn_w_down, v_ffn_w_down)
    update("attn_w_out", attn_w_out, [r_aout], m_attn_w_out, v_attn_w_out)
    update("hgrn_norm_g", hgrn_norm_g, [g_norm], m_hgrn_norm_g, v_hgrn_norm_g)
    cat = lambda ts: jnp.concatenate(ts, axis=0)
    small_w = cat([lb_logits, ln_mix_g, ln_mix_b, ln_ffn_g, ln_ffn_b])
    small_m = cat([m_lb_logits, m_ln_mix_g, m_ln_mix_b, m_ln_ffn_g, m_ln_ffn_b])
    small_v = cat([v_lb_logits, v_ln_mix_g, v_ln_mix_b, v_ln_ffn_g, v_ln_ffn_b])
    small_upd = _adamw("adamw_small", small_w, [small[0:10]], small_m, small_v)
    for i, nm in enumerate(["lb_logits", "ln_mix_g", "ln_mix_b", "ln_ffn_g", "ln_ffn_b"]):
        grads[nm] = small[2 * i:2 * i + 2]
        upd[nm] = tuple(t[2 * i:2 * i + 2] for t in small_upd)
    done = [upd[k][2] for k in ("hgrn_w_in", "hgrn_w_out", "ffn_w_up", "ffn_w_down", "attn_w_out", "hgrn_norm_g")]
    (r_ain,) = reduced("scatter_4", sc_4, small_upd[2], *done)
    update("attn_w_in", attn_w_in, [r_ain], m_attn_w_in, v_attn_w_in)

    order = ["attn_w_in", "attn_w_out", "hgrn_w_in", "hgrn_w_out", "hgrn_norm_g", "lb_logits", "ln_mix_g", "ln_mix_b",
             "ln_ffn_g", "ln_ffn_b", "ffn_w_up", "ffn_w_down"]
    return (loss, grad_x[None], *[grads[k] for k in order], *[upd[k][0] for k in order],
            *[upd[k][1] for k in order], *[upd[k][2] for k in order])
```

```python
import math

import jax
import jax.numpy as jnp
from jax import lax
from jax.experimental import pallas as pl
from jax.experimental.pallas import tpu as pltpu

F32 = jnp.float32
BF16 = jnp.bfloat16
MXU_DTYPE = BF16

HEAD_DIM = 64
ATTN_BLK = 128
DILATIONS = (1, 4, 16)
ROPE_THETA = 10000.0
HGRN_DK = 128
HGRN_CHUNK = 64
DEPTH = 2
LN_EPS = 1e-5
RMS_EPS = 1e-6
ALPHA = (2 * DEPTH) ** 0.25
ADAM_LR, ADAM_B1, ADAM_B2, ADAM_EPS, ADAM_WD, ADAM_STEP = 0.001, 0.9, 0.999, 1e-08, 0.01, 10

LANES = 128
VMEM_LIMIT = 56 * 1024 * 1024
NEG = -1e30
MESH = pl.DeviceIdType.MESH


def _cparams(sem=None):
    return pltpu.CompilerParams(dimension_semantics=sem, vmem_limit_bytes=VMEM_LIMIT)


def _sds(shape, dtype):
    return jax.ShapeDtypeStruct(tuple(shape), dtype)


def _dg(a, b, ca, cb):
    return lax.dot_general(a, b, (((ca,), (cb,)), ((), ())), preferred_element_type=F32)


def _nn(a, b):
    return _dg(a, b, 1, 0)


def _nt(a, b):
    return _dg(a, b, 1, 1)


def _tn(a, b):
    return _dg(a, b, 0, 0)


def _split3(a):
    hi = a.astype(BF16)
    r = a - hi.astype(F32)
    mid = r.astype(BF16)
    lo = (r - mid.astype(F32)).astype(BF16)
    return hi, mid, lo


def _exact_nn(a, sel):
    hi, mid, lo = _split3(a)
    return _nn(hi, sel) + _nn(mid, sel) + _nn(lo, sel)


def _pick(n, prefs):
    for p in prefs:
        if n % p == 0:
            return p
    return n


def _matmul(name, a, b, form, tm, tn, tk, outs, epilogue, extras=(), a_map=None, b_map=None, mnk=None, dep=None,
            sem=("parallel", "parallel", "arbitrary"), split=None, alias_dep=False):
    if form == "nn":
        (M, K), N = a.shape, b.shape[1]
        a_spec = pl.BlockSpec((tm, tk), a_map or (lambda i, j, k: (i, k)))
        b_spec = pl.BlockSpec((tk, tn), b_map or (lambda i, j, k: (k, j)))
        ca, cb = 1, 0
    elif form == "nt":
        (M, K), N = a.shape, b.shape[0]
        a_spec = pl.BlockSpec((tm, tk), a_map or (lambda i, j, k: (i, k)))
        b_spec = pl.BlockSpec((tn, tk), b_map or (lambda i, j, k: (j, k)))
        ca, cb = 1, 1
    else:
        (K, M), N = a.shape, b.shape[1]
        a_spec = pl.BlockSpec((tk, tm), a_map or (lambda i, j, k: (k, i)))
        b_spec = pl.BlockSpec((tk, tn), b_map or (lambda i, j, k: (k, j)))
        ca, cb = 0, 0
    if mnk is not None:
        M, N, K = mnk
    assert M % tm == 0 and N % tn == 0 and K % tk == 0, (name, M, N, K, tm, tn, tk)
    nk = K // tk
    ne, no = len(extras), len(outs)
    deps = [] if dep is None else [dep]
    nd = len(deps)

    def body(a_ref, b_ref, *rest):
        extra_refs, out_refs = rest[:ne], rest[ne + nd:ne + nd + no]
        j = pl.program_id(1)
        if split is not None:
            kind, n = split
            assert nk == 1 and form != "tn"
            tiled = [t for _, _, *t in list(extras) + list(outs)]
            refs = list(extra_refs) + list(out_refs)
            for ci in range(n):
                if kind == "cols":
                    cs = slice(ci * (tn // n), (ci + 1) * (tn // n))
                    part = _dg(a_ref[...].astype(MXU_DTYPE), (b_ref[:, cs] if form == "nn" else b_ref[cs, :]).astype(MXU_DTYPE), ca, cb)
                    view = [r.at[:, cs] if t else r for r, t in zip(refs, tiled)]
                else:
                    rs = slice(ci * (tm // n), (ci + 1) * (tm // n))
                    part = _dg(a_ref[rs, :].astype(MXU_DTYPE), b_ref[...].astype(MXU_DTYPE), ca, cb)
                    view = [r.at[rs, :] if t else r for r, t in zip(refs, tiled)]
                epilogue(part, view[:ne], view[ne:], j, ci)
            return
        part = _dg(a_ref[...].astype(MXU_DTYPE), b_ref[...].astype(MXU_DTYPE), ca, cb)
        if nk == 1:
            epilogue(part, extra_refs, out_refs, j, 0)
            return
        acc_ref = rest[-1]
        k = pl.program_id(2)

        @pl.when(k == 0)
        def _():
            acc_ref[...] = part

        @pl.when(k > 0)
        def _():
            acc_ref[...] += part

        @pl.when(k == nk - 1)
        def _():
            epilogue(acc_ref[...], extra_refs, out_refs, j, 0)

    res = pl.pallas_call(
        body,
        name=name,
        grid=(M // tm, N // tn, nk),
        in_specs=[a_spec, b_spec] + [s for _, s, *_ in extras] + [pl.BlockSpec(memory_space=pl.ANY)] * nd,
        out_specs=[s for _, s, *_ in outs],
        out_shape=[o for o, *_ in outs],
        scratch_shapes=[pltpu.VMEM((tm, tn), F32)] if nk > 1 else [],
        input_output_aliases={2 + ne: 0} if alias_dep else {},
        compiler_params=_cparams(sem),
    )(a, b, *[e for e, *_ in extras], *deps)
    return res


def _ij_spec(tm, tn):
    return pl.BlockSpec((tm, tn), lambda i, j, k: (i, j))


def _store_epilogue(acc, extra_refs, out_refs, j, ci):
    out_refs[0][...] = acc.astype(out_refs[0].dtype)


def _plain_mm(name, a, b, form, out_dtype, tm, tn, tk):
    M = a.shape[1] if form == "tn" else a.shape[0]
    N = b.shape[0] if form == "nt" else b.shape[1]
    return _matmul(name, a, b, form, tm, tn, tk, [(_sds((M, N), out_dtype), _ij_spec(tm, tn))], _store_epilogue)[0]


def _class_slabs(S):
    assert DILATIONS[0] == 1
    return [(g, d, r, S // d) for g, d in enumerate(DILATIONS) if d > 1 for r in range(d)]


def _stack_classes(name, t, out_dtype):
    S, W = t.shape

    def body(x_ref, o_ref):
        o_ref[0:S, :] = x_ref[...].astype(out_dtype)
        for g, d, r, n in _class_slabs(S):
            o_ref[g * S + r * n:g * S + (r + 1) * n, :] = x_ref[pl.ds(r, n, stride=d), :].astype(out_dtype)

    return pl.pallas_call(
        body,
        name=name,
        grid=(W // LANES,),
        in_specs=[pl.BlockSpec((S, LANES), lambda j: (0, j))],
        out_specs=pl.BlockSpec((3 * S, LANES), lambda j: (0, j)),
        out_shape=_sds((3 * S, W), out_dtype),
        compiler_params=_cparams(("parallel",)),
    )(t)


def _rope_tables(seq):
    half = HEAD_DIM // 2
    inv = ROPE_THETA ** (-jnp.arange(half, dtype=F32) * (2.0 / HEAD_DIM))
    inv = jnp.tile(inv, LANES // half)
    pos = []
    for d in DILATIONS:
        row = jnp.arange(seq)
        pos.append((row % (seq // d)) * d + row // (seq // d))
    ang = jnp.concatenate(pos).astype(F32)[:, None] * inv[None, :]
    first = (jnp.arange(LANES) % HEAD_DIM) < half
    sin = jnp.sin(ang)
    return jnp.cos(ang), jnp.where(first[None, :], -sin, sin)


def _partner(x):
    half = HEAD_DIM // 2
    lane = lax.broadcasted_iota(jnp.int32, x.shape, 1)
    first = (lane % HEAD_DIM) < half
    return jnp.where(first, pltpu.roll(x, LANES - half, 1), pltpu.roll(x, half, 1))


def _attn_proj(name, x3, w, cos3, sin3, tm, prev, dep=None):
    S3, D = x3.shape
    S = S3 // 3
    tn = 3 * D // 4
    nrow = S // tm
    local = prev is None

    def tile(j):
        q = 2 * lax.axis_index("x") + lax.axis_index("y")
        c0 = 3 * q + j if local else j + 3 * (j >= 3 * q).astype(jnp.int32)
        return c0, c0 // 4, c0 % 4

    def epilogue(acc, extra_refs, out_refs, j, ci):
        cos_ref, sin_ref = extra_refs
        o_ref = out_refs[0]
        _, _, place = tile(j)
        width = acc.shape[1]
        assert D % width == 0
        is_rot = (place * tn + ci * width) // D < 2
        c = jnp.where(is_rot, cos_ref[...], 1.0)
        s = jnp.where(is_rot, sin_ref[...], 0.0)
        for t in range(width // LANES):
            xs = acc[:, t * LANES:(t + 1) * LANES]
            o_ref[:, t * LANES:(t + 1) * LANES] = (xs * c + _partner(xs) * s).astype(o_ref.dtype)

    rows = lambda i, j: tile(j)[1] * nrow + i
    tab = pl.BlockSpec((tm, LANES), lambda i, j, k: (rows(i, j), 0))
    out = pl.BlockSpec((tm, tn), lambda i, j, k: (rows(i, j), tile(j)[2]))
    ntiles = 3 if local else 9
    return _matmul(name, x3, w, "nn", tm, tn, D, [(_sds((S3, 3 * D), MXU_DTYPE), out, True)], epilogue,
                   extras=[(cos3, tab), (sin3, tab)], a_map=lambda i, j, k: (rows(i, j), k),
                   b_map=lambda i, j, k: (k, j if local else tile(j)[0]), mnk=(nrow * tm, ntiles * tn, D),
                   dep=dep if local else prev, alias_dep=not local, split=("cols", 3))[0]


def _head_sel(d_model):
    h = jnp.arange(LANES)[:, None]
    l = jnp.arange(d_model)[None, :]
    return (l // HEAD_DIM == h).astype(BF16)


def _class_edges(b, nblk):
    g = b // nblk
    per_class = jnp.where(g == 0, nblk // DILATIONS[0], jnp.where(g == 1, nblk // DILATIONS[1], nblk // DILATIONS[2]))
    pos = (b % nblk) % per_class
    return pos != 0, pos != per_class - 1


def _two_heads(t, top):
    zero = jnp.zeros_like(t)
    return jnp.concatenate([jnp.where(top, t, zero), jnp.where(top, zero, t)], axis=0)


def _band_mask(has_prev):
    B = ATTN_BLK
    row = lax.broadcasted_iota(jnp.int32, (2 * B, 2 * B), 0) % B
    col = lax.broadcasted_iota(jnp.int32, (2 * B, 2 * B), 1)
    in_prev = jnp.logical_and(jnp.logical_and(col < B, col >= row), has_prev)
    in_own = jnp.logical_and(col >= B, col - B <= row)
    return jnp.logical_or(in_prev, in_own)


def _attn_fwd(P3, D):
    S3 = P3.shape[0]
    B = ATTN_BLK
    nblk = S3 // 3 // B
    npairs = D // LANES
    scale = HEAD_DIM ** -0.5

    def body(q_ref, kc_ref, vc_ref, kp_ref, vp_ref, o_ref, lse_ref):
        has_prev, _ = _class_edges(pl.program_id(0), nblk)
        bias = jnp.where(_band_mask(has_prev), 0.0, NEG)
        lane = lax.broadcasted_iota(jnp.int32, (B, LANES), 1)
        top = lane < HEAD_DIM
        lse_acc = jnp.zeros((B, LANES), F32)
        for j in range(npairs):
            sl = slice(j * LANES, (j + 1) * LANES)
            Q = _two_heads(q_ref[:, sl] * scale, top)
            K2 = jnp.concatenate([kp_ref[:, sl], kc_ref[:, sl]], axis=0)
            V2 = jnp.concatenate([vp_ref[:, sl], vc_ref[:, sl]], axis=0)
            s = _nt(Q, K2) + bias
            m = jnp.max(s, axis=1, keepdims=True)
            p = jnp.exp(s - m)
            l = jnp.sum(p, axis=1, keepdims=True)
            o = _nn(p.astype(MXU_DTYPE), V2) * (1.0 / l)
            o_ref[:, sl] = jnp.where(top, o[:B], o[B:])
            lse = m + jnp.log(l)
            lse_acc = jnp.where(lane == 2 * j, lse[:B], jnp.where(lane == 2 * j + 1, lse[B:], lse_acc))
        lse_ref[...] = lse_acc

    blk = lambda part, prev: pl.BlockSpec(
        (B, D), (lambda b: (jnp.maximum(b - 1, 0), part)) if prev else (lambda b: (b, part)))
    return pl.pallas_call(
        body,
        name="attn_fwd",
        grid=(3 * nblk,),
        in_specs=[blk(0, False), blk(1, False), blk(2, False), blk(1, True), blk(2, True)],
        out_specs=[pl.BlockSpec((B, D), lambda b: (b, 0)), pl.BlockSpec((B, LANES), lambda b: (b, 0))],
        out_shape=[_sds((S3, D), F32), _sds((S3, LANES), F32)],
        compiler_params=_cparams(("parallel",)),
    )(P3, P3, P3, P3, P3)


def _attn_mix(o3, lse3, sel):
    S3, D = o3.shape
    S = S3 // 3

    def body(o3_ref, lse_ref, sel_ref, o_ref, L_ref, w_ref):
        @pl.when(pl.program_id(0) == 0)
        def _():
            w_ref[0] = lse_ref[0:S, :]
            for g, d, r, n in _class_slabs(S):
                w_ref[g, pl.ds(r, n, stride=d), :] = lse_ref[g * S + r * n:g * S + (r + 1) * n, :]
            a, b, c = w_ref[0], w_ref[1], w_ref[2]
            m = jnp.maximum(jnp.maximum(a, b), c)
            L = m + jnp.log(jnp.exp(a - m) + jnp.exp(b - m) + jnp.exp(c - m))
            L_ref[...] = L
            w_ref[0] = jnp.exp(a - L)
            w_ref[1] = jnp.exp(b - L)
            w_ref[2] = jnp.exp(c - L)

        s = sel_ref[...]
        o_ref[...] = _exact_nn(w_ref[0], s) * o3_ref[0:S, :]
        for g, d, r, n in _class_slabs(S):
            rows = pl.ds(r, n, stride=d)
            o_ref[rows, :] += _exact_nn(w_ref[g, rows, :], s) * o3_ref[g * S + r * n:g * S + (r + 1) * n, :]

    return pl.pallas_call(
        body,
        name="attn_mix",
        grid=(D // LANES,),
        in_specs=[pl.BlockSpec((S3, LANES), lambda j: (0, j)), pl.BlockSpec((S3, LANES), lambda j: (0, 0)),
                  pl.BlockSpec((LANES, LANES), lambda j: (0, j))],
        out_specs=[pl.BlockSpec((S, LANES), lambda j: (0, j)), pl.BlockSpec((S, LANES), lambda j: (0, 0))],
        out_shape=[_sds((S, D), F32), _sds((S, LANES), F32)],
        scratch_shapes=[pltpu.VMEM((3, S, LANES), F32)],
        compiler_params=_cparams(("arbitrary",)),
    )(o3, lse3, sel)


def _attn_bwd(P3, do3, L3, delta3, cos3, sin3, D, dep):
    S3 = P3.shape[0]
    B = ATTN_BLK
    nblk = S3 // 3 // B
    npairs = D // LANES
    scale = HEAD_DIM ** -0.5

    def body(c_ref, kp_ref, vp_ref, qn_ref, doc_ref, don_ref, Lc_ref, Ln_ref, dc_ref, dn_ref, cos_ref, sin_ref, dep_ref, out_ref):
        has_prev, has_next = _class_edges(pl.program_id(0), nblk)
        bias = jnp.where(_band_mask(has_prev), 0.0, NEG)
        row = lax.broadcasted_iota(jnp.int32, (2 * B, B), 0) % B
        col = lax.broadcasted_iota(jnp.int32, (2 * B, B), 1)
        bias_n = jnp.where(jnp.logical_and(col >= row, has_next), 0.0, NEG)
        lane = lax.broadcasted_iota(jnp.int32, (B, LANES), 1)
        top = lane < HEAD_DIM
        cos_t = cos_ref[...]
        sin_inv = -sin_ref[...]
        Lc_all, Ln_all, dc_all, dn_all = Lc_ref[...], Ln_ref[...], dc_ref[...], dn_ref[...]
        pair_col = lambda t, j: jnp.concatenate([t[:, 2 * j:2 * j + 1], t[:, 2 * j + 1:2 * j + 2]], axis=0)
        for j in range(npairs):
            sl = lambda part: slice(part * D + j * LANES, part * D + (j + 1) * LANES)
            pj = slice(j * LANES, (j + 1) * LANES)
            kc2, vc2 = c_ref[:, sl(1)], c_ref[:, sl(2)]
            K2 = jnp.concatenate([kp_ref[:, pj], kc2], axis=0)
            V2 = jnp.concatenate([vp_ref[:, pj], vc2], axis=0)
            Qc = _two_heads(c_ref[:, sl(0)] * scale, top)
            Qn = _two_heads(qn_ref[:, pj] * scale, top)
            DOc = _two_heads(doc_ref[:, pj].astype(MXU_DTYPE), top)
            DOn = _two_heads(don_ref[:, pj].astype(MXU_DTYPE), top)
            P_c = jnp.exp(_nt(Qc, K2) + bias - pair_col(Lc_all, j))
            dS_c = P_c * (_nt(DOc, V2) - pair_col(dc_all, j))
            P_n = jnp.exp(_nt(Qn, kc2) + bias_n - pair_col(Ln_all, j))
            dS_n = P_n * (_nt(DOn, vc2) - pair_col(dn_all, j))
            dq = _nn(dS_c.astype(MXU_DTYPE), K2)
            dq2 = jnp.where(top, dq[:B], dq[B:]) * scale
            Qk = jnp.concatenate([Qc, Qn], axis=0)
            DOk = jnp.concatenate([DOc, DOn], axis=0)
            dk2 = _tn(jnp.concatenate([dS_c[:, B:], dS_n], axis=0).astype(MXU_DTYPE), Qk)
            dv2 = _tn(jnp.concatenate([P_c[:, B:], P_n], axis=0).astype(MXU_DTYPE), DOk)
            out_ref[:, sl(0)] = (dq2 * cos_t + _partner(dq2) * sin_inv).astype(out_ref.dtype)
            out_ref[:, sl(1)] = (dk2 * cos_t + _partner(dk2) * sin_inv).astype(out_ref.dtype)
            out_ref[:, sl(2)] = dv2.astype(out_ref.dtype)

    cur = lambda b: b
    prv = lambda b: jnp.maximum(b - 1, 0)
    nxt = lambda b: jnp.minimum(b + 1, 3 * nblk - 1)
    spec = lambda w, f, part=0: pl.BlockSpec((B, w), lambda b: (f(b), part))
    return pl.pallas_call(
        body,
        name="attn_bwd",
        grid=(3 * nblk,),
        in_specs=[spec(3 * D, cur), spec(D, prv, 1), spec(D, prv, 2), spec(D, nxt, 0), spec(D, cur), spec(D, nxt),
                  spec(LANES, cur), spec(LANES, nxt), spec(LANES, cur), spec(LANES, nxt), spec(LANES, cur), spec(LANES, cur),
                  pl.BlockSpec(memory_space=pl.ANY)],
        out_specs=spec(3 * D, cur),
        out_shape=_sds((S3, 3 * D), MXU_DTYPE),
        compiler_params=_cparams(("parallel",)),
    )(P3, P3, P3, P3, do3, do3, L3, L3, delta3, delta3, cos3, sin3, dep)


def _input_grad(du, dx3):
    S, D = du.shape

    def body(du_ref, dx_ref, o_ref):
        o_ref[...] = ALPHA * du_ref[...] + dx_ref[0:S, :]
        for g, d, r, n in _class_slabs(S):
            o_ref[pl.ds(r, n, stride=d), :] += dx_ref[g * S + r * n:g * S + (r + 1) * n, :]

    return pl.pallas_call(
        body,
        name="input_grad",
        grid=(D // LANES,),
        in_specs=[pl.BlockSpec((S, LANES), lambda j: (0, j)), pl.BlockSpec((3 * S, LANES), lambda j: (0, j))],
        out_specs=pl.BlockSpec((S, LANES), lambda j: (0, j)),
        out_shape=_sds((S, D), F32),
        compiler_params=_cparams(("parallel",)),
    )(du, dx3)


def _chunk_causal(tb):
    r = lax.broadcasted_iota(jnp.int32, (tb, tb), 0)
    c = lax.broadcasted_iota(jnp.int32, (tb, tb), 1)
    return jnp.logical_and((r // HGRN_CHUNK) == (c // HGRN_CHUNK), r >= c)


def _chunk_sums(a, lower):
    C = HGRN_CHUNK
    r = lax.broadcasted_iota(jnp.int32, (C, C), 0)
    c = lax.broadcasted_iota(jnp.int32, (C, C), 1)
    tri = ((r >= c) if lower else (r <= c)).astype(BF16)
    parts = _split3(a)
    out = []
    for ci in range(a.shape[0] // C):
        rows = slice(ci * C, (ci + 1) * C)
        out.append(_nn(tri, parts[0][rows]) + _nn(tri, parts[1][rows]) + _nn(tri, parts[2][rows]))
    return jnp.concatenate(out, axis=0)


def _chunk_last(b):
    C = HGRN_CHUNK
    return jnp.concatenate([jnp.broadcast_to(b[(ci + 1) * C - 1:(ci + 1) * C, :], (C, b.shape[1]))
                            for ci in range(b.shape[0] // C)], axis=0)


def _lower_bound(lb_ref):
    l0, l1 = lb_ref[0:1, :], lb_ref[1:2, :]
    m = jnp.maximum(l0, l1)
    e0, e1 = jnp.exp(l0 - m), jnp.exp(l1 - m)
    return e1 / (e0 + e1)


def _hgrn_gates(q_raw, z, lb):
    sg = 1.0 / (1.0 + jnp.exp(-z))
    sn = 1.0 / (1.0 + jnp.exp(z))
    f = lb + (1.0 - lb) * sg
    key = (1.0 - lb) * sn
    sq = 1.0 / (1.0 + jnp.exp(-q_raw))
    return sg, sn, f, key, sq


def _hgrn_fwd(P1, lb_logits, norm_g, tb):
    S = P1.shape[0]
    D = P1.shape[1] // 3
    K = HGRN_DK
    H = D // K
    HP = H
    C = HGRN_CHUNK
    cpb = tb // C
    nt = S // tb

    def body(q_ref, f_ref, i_ref, lb_ref, g_ref, o_ref, n_ref, st_ref, state):
        t = pl.program_id(1)

        @pl.when(t == 0)
        def _():
            state[...] = jnp.zeros_like(state)

        lb_all = _lower_bound(lb_ref)
        low = _chunk_causal(tb)
        for hh in range(HP):
            lanes = slice(hh * K, (hh + 1) * K)
            q_raw, z, v = q_ref[:, lanes], f_ref[:, lanes], i_ref[:, lanes]
            sg, sn, f, key, sq = _hgrn_gates(q_raw, z, lb_all[:, lanes])
            b = _chunk_sums(jnp.log(f), lower=True)
            qd = (q_raw * sq * jnp.exp(b)).astype(MXU_DTYPE)
            kd = (key * jnp.exp(-b)).astype(MXU_DTYPE)
            kb = (key * jnp.exp(_chunk_last(b) - b)).astype(MXU_DTYPE)
            vm = v.astype(MXU_DTYPE)
            a = jnp.where(low, _nt(qd, kd), 0.0).astype(MXU_DTYPE)
            o_intra = _nn(a, vm)
            st = state[hh]
            outs = []
            for ci in range(cpb):
                rows = slice(ci * C, (ci + 1) * C)
                st_ref[hh, ci] = st
                outs.append(o_intra[rows] + _nt(qd[rows], st.astype(MXU_DTYPE)))
                st = st * jnp.exp(b[(ci + 1) * C - 1:(ci + 1) * C, :]) + _tn(vm[rows], kb[rows])
            state[hh] = st
            o = jnp.concatenate(outs, axis=0)
            o_ref[:, lanes] = o
            rs = lax.rsqrt(jnp.mean(o * o, axis=1, keepdims=True) + RMS_EPS)
            n_ref[:, lanes] = o * rs * g_ref[:, lanes]

    tok = lambda part: pl.BlockSpec((tb, HP * K), lambda h, t: (t, part * (H // HP) + h))
    vec = lambda rows: pl.BlockSpec((rows, HP * K), lambda h, t: (0, h))
    return pl.pallas_call(
        body,
        name="hgrn_fwd",
        grid=(H // HP, nt),
        in_specs=[tok(0), tok(1), tok(2), vec(2), vec(1)],
        out_specs=[tok(0), tok(0), pl.BlockSpec((HP, cpb, K, K), lambda h, t: (h, t, 0, 0))],
        out_shape=[_sds((S, D), F32), _sds((S, D), F32), _sds((H, S // C, K, K), F32)],
        scratch_shapes=[pltpu.VMEM((HP, K, K), F32)],
        compiler_params=_cparams(("parallel", "arbitrary")),
    )(P1, P1, P1, lb_logits, norm_g)


def _hgrn_bwd(P1, o_pre, states, dn, lb_logits, norm_g, tb):
    S = P1.shape[0]
    D = P1.shape[1] // 3
    K = HGRN_DK
    H = D // K
    HP = H
    C = HGRN_CHUNK
    cpb = tb // C
    nt = S // tb

    def body(q_ref, f_ref, i_ref, o_ref, st_ref, dn_ref, lb_ref, g_ref, d_ref, dg_ref, dlb_ref, dstate):
        t = pl.program_id(1)

        @pl.when(t == 0)
        def _():
            dstate[...] = jnp.zeros_like(dstate)
            dg_ref[...] = jnp.zeros_like(dg_ref)
            dlb_ref[...] = jnp.zeros_like(dlb_ref)

        lb_all = _lower_bound(lb_ref)
        low = _chunk_causal(tb)
        for hh in range(HP):
            lanes = slice(hh * K, (hh + 1) * K)
            lb = lb_all[:, lanes]
            gn = g_ref[:, lanes]
            q_raw, z, v = q_ref[:, lanes], f_ref[:, lanes], i_ref[:, lanes]
            sg, sn, f, key, sq = _hgrn_gates(q_raw, z, lb)
            b = _chunk_sums(jnp.log(f), lower=True)
            e_pos, e_neg, e_rel = jnp.exp(b), jnp.exp(-b), jnp.exp(_chunk_last(b) - b)
            qd_f, kd_f, kb_f = q_raw * sq * e_pos, key * e_neg, key * e_rel
            qd, kd, kb = qd_f.astype(MXU_DTYPE), kd_f.astype(MXU_DTYPE), kb_f.astype(MXU_DTYPE)
            vm = v.astype(MXU_DTYPE)
            a = jnp.where(low, _nt(qd, kd), 0.0).astype(MXU_DTYPE)
            o = o_ref[:, lanes]
            dnn = dn_ref[:, lanes]
            rs = lax.rsqrt(jnp.mean(o * o, axis=1, keepdims=True) + RMS_EPS)
            dg_ref[:, lanes] += jnp.sum(dnn * o * rs, axis=0, keepdims=True)
            tg = dnn * gn
            dom = (rs * tg - o * (rs * rs * rs) * jnp.mean(tg * o, axis=1, keepdims=True)).astype(MXU_DTYPE)
            da = jnp.where(low, _nt(dom, vm), 0.0).astype(MXU_DTYPE)
            dv = _tn(a, dom)
            dqd = _nn(da, kd)
            dkd = _tn(da, qd)
            dst = dstate[hh]
            dv_s, dqd_s, dkb_s, dbl_s = [None] * cpb, [None] * cpb, [None] * cpb, [None] * cpb
            for ci in reversed(range(cpb)):
                rows = slice(ci * C, (ci + 1) * C)
                st = st_ref[hh, ci]
                dstm = dst.astype(MXU_DTYPE)
                dec = jnp.exp(b[(ci + 1) * C - 1:(ci + 1) * C, :])
                dv_s[ci] = _nt(kb[rows], dstm)
                dkb_s[ci] = _nn(vm[rows], dstm)
                dqd_s[ci] = _nn(dom[rows], st.astype(MXU_DTYPE))
                db_last = jnp.sum(dkb_s[ci] * kb_f[rows], axis=0, keepdims=True) + jnp.sum(dst * st, axis=0, keepdims=True) * dec
                dbl_s[ci] = jnp.broadcast_to(db_last, (C, K))
                dst = dst * dec + _tn(dom[rows], qd[rows])
            dstate[hh] = dst
            dv = dv + jnp.concatenate(dv_s, axis=0)
            dqd = dqd + jnp.concatenate(dqd_s, axis=0)
            dkb = jnp.concatenate(dkb_s, axis=0)
            dkey = dkd * e_neg + dkb * e_rel
            db = dqd * qd_f - dkd * kd_f - dkb * kb_f
            dlogf = _chunk_sums(db, lower=False) + jnp.concatenate(dbl_s, axis=0)
            gz = (1.0 - lb) * sg * sn
            col = lambda part: slice(part * D + hh * K, part * D + (hh + 1) * K)
            d_ref[:, col(0)] = (dqd * e_pos * (sq + q_raw * sq * (1.0 - sq))).astype(d_ref.dtype)
            d_ref[:, col(1)] = (dlogf * gz / f - dkey * gz).astype(d_ref.dtype)
            d_ref[:, col(2)] = dv.astype(d_ref.dtype)
            dlb_ref[:, lanes] += jnp.sum(dlogf * sn / f - dkey * sn, axis=0, keepdims=True)

    rev = lambda t: nt - 1 - t
    tok = lambda part: pl.BlockSpec((tb, HP * K), lambda h, t: (rev(t), part * (H // HP) + h))
    vec = lambda rows: pl.BlockSpec((rows, HP * K), lambda h, t: (0, h))
    outs = pl.pallas_call(
        body,
        name="hgrn_bwd",
        grid=(H // HP, nt),
        in_specs=[tok(0), tok(1), tok(2), tok(0),
                  pl.BlockSpec((HP, cpb, K, K), lambda h, t: (h, rev(t), 0, 0)),
                  tok(0), vec(2), vec(1)],
        out_specs=[pl.BlockSpec((tb, 3 * D), lambda h, t: (rev(t), 0)), vec(1), vec(1)],
        out_shape=[_sds((S, 3 * D), MXU_DTYPE)] + [_sds((1, D), F32)] * 2,
        scratch_shapes=[pltpu.VMEM((HP, K, K), F32)],
        compiler_params=_cparams(("parallel", "arbitrary")),
    )(P1, P1, P1, o_pre, states, dn, lb_logits, norm_g)
    return outs


def _lb_logits_grad(dlb, lb_logits):
    def body(d_ref, l_ref, o_ref):
        s1 = _lower_bound(l_ref)
        d = d_ref[...]
        o_ref[0:1, :] = -(1.0 - s1) * s1 * d
        o_ref[1:2, :] = s1 * (1.0 - s1) * d

    return pl.pallas_call(body, name="lb_logits_grad", out_shape=_sds(lb_logits.shape, F32))(dlb, lb_logits)


def _mm_res_ln(name, a, w_full, res, g, b, tm, tk):
    from_ln = isinstance(res, tuple)
    S, D = (res[0] if from_ln else res).shape

    def epilogue(acc, extra_refs, out_refs, j, ci):
        g_ref, b_ref = extra_refs[:2]
        xm_ref, xhat_ref, rstd_ref = out_refs
        r = extra_refs[2][...] * extra_refs[3][...] + extra_refs[4][...] if from_ln else extra_refs[2][...]
        u = ALPHA * r + acc
        mu = jnp.mean(u, axis=1, keepdims=True)
        cen = u - mu
        rstd = lax.rsqrt(jnp.mean(cen * cen, axis=1, keepdims=True) + LN_EPS)
        xhat = cen * rstd
        xhat_ref[...] = xhat
        xm_ref[...] = (xhat * g_ref[...] + b_ref[...]).astype(xm_ref.dtype)
        rstd_ref[...] = rstd

    row = pl.BlockSpec((tm, D), lambda i, j, k: (i, 0))
    vec = pl.BlockSpec((1, D), lambda i, j, k: (0, 0))
    outs = [(_sds((S, D), MXU_DTYPE), row, True), (_sds((S, D), F32), row, True),
            (_sds((S, 1), F32), pl.BlockSpec((tm, 1), lambda i, j, k: (i, 0)), True)]
    res_extras = [(res[0], row, True), (res[1], vec), (res[2], vec)] if from_ln else [(res, row, True)]
    return _matmul(name, a, w_full, "nn", tm, D, tk, outs, epilogue, extras=[(g, vec), (b, vec)] + res_extras,
                   split=("rows", 2) if tk == a.shape[1] else None)


def _ln_bwd_rows(dy, xh, rstd, g, first, du_ref, dum_ref, dg_ref, db_ref):
    if first is not None:
        @pl.when(first)
        def _():
            dg_ref[...] = jnp.zeros_like(dg_ref)
            db_ref[...] = jnp.zeros_like(db_ref)

    dg_ref[...] += jnp.sum(dy * xh, axis=0, keepdims=True)
    db_ref[...] += jnp.sum(dy, axis=0, keepdims=True)
    dxh = dy * g
    m1 = jnp.mean(dxh, axis=1, keepdims=True)
    m2 = jnp.mean(dxh * xh, axis=1, keepdims=True)
    du = rstd * (dxh - m1 - xh * m2)
    du_ref[...] = du
    dum_ref[...] = du.astype(dum_ref.dtype)


def _loss_ln_bwd(target, xhat, rstd, g, b, tm):
    S, D = xhat.shape

    def body(t_ref, xh_ref, r_ref, g_ref, b_ref, sq_ref, du_ref, dum_ref, dg_ref, db_ref):
        first = pl.program_id(0) == 0

        @pl.when(first)
        def _():
            sq_ref[...] = jnp.zeros_like(sq_ref)

        xh = xh_ref[...]
        e = xh * g_ref[...] + b_ref[...] - t_ref[...]
        sq_ref[...] += jnp.sum(e * e, axis=0, keepdims=True)
        _ln_bwd_rows(e / D, xh, r_ref[...], g_ref[...], first, du_ref, dum_ref, dg_ref, db_ref)

    row = pl.BlockSpec((tm, D), lambda i: (i, 0))
    vec = pl.BlockSpec((1, D), lambda i: (0, 0))
    return pl.pallas_call(
        body,
        name="loss_ln_bwd",
        grid=(S // tm,),
        in_specs=[row, row, pl.BlockSpec((tm, 1), lambda i: (i, 0)), vec, vec],
        out_specs=[vec, row, row, vec, vec],
        out_shape=[_sds((1, D), F32), _sds((S, D), F32), _sds((S, D), MXU_DTYPE), _sds((1, D), F32), _sds((1, D), F32)],
        compiler_params=_cparams(("arbitrary",)),
    )(target, xhat, rstd, g, b)


def _mlp_up(name, x, w_up, tm, tn, tk):
    S = x.shape[0]
    F = w_up.shape[1]

    def epilogue(acc, extra_refs, out_refs, j, ci):
        r = jnp.maximum(acc, 0.0)
        out_refs[0][...] = (r * r).astype(out_refs[0].dtype)

    return _matmul(name, x, w_up, "nn", tm, tn, tk, [(_sds((S, F), MXU_DTYPE), _ij_spec(tm, tn), True)], epilogue,
                   split=("cols", 2))[0]


def _mlp_down_bwd(name, dy, w_down, a, tm, tn, tk):
    S, F = a.shape

    def epilogue(acc, extra_refs, out_refs, j, ci):
        out_refs[0][...] = (acc * (2.0 * jnp.sqrt(extra_refs[0][...]).astype(F32))).astype(out_refs[0].dtype)

    return _matmul(name, dy, w_down, "nt", tm, tn, tk, [(_sds((S, F), MXU_DTYPE), _ij_spec(tm, tn), True)], epilogue,
                   extras=[(a, _ij_spec(tm, tn), True)], split=("cols", 2))[0]


def _mm_nt_res_ln_bwd(name, dy, w, du, xhat, rstd, g, tm, tk, dep):
    S, D = du.shape

    def epilogue(acc, extra_refs, out_refs, j, ci):
        du_ref, xh_ref, r_ref, g_ref = extra_refs
        first = (pl.program_id(0) == 0) if ci == 0 else None
        _ln_bwd_rows(ALPHA * du_ref[...] + acc, xh_ref[...], r_ref[...], g_ref[...], first, *out_refs)

    row = pl.BlockSpec((tm, D), lambda i, j, k: (i, 0))
    vec = pl.BlockSpec((1, D), lambda i, j, k: (0, 0))
    return _matmul(name, dy, w, "nt", tm, D, tk,
                   [(_sds((S, D), F32), row, True), (_sds((S, D), MXU_DTYPE), row, True), (_sds((1, D), F32), vec),
                    (_sds((1, D), F32), vec)], epilogue,
                   extras=[(du, row, True), (xhat, row, True), (rstd, pl.BlockSpec((tm, 1), lambda i, j, k: (i, 0)), True), (g, vec)],
                   dep=dep, sem=("arbitrary", "arbitrary", "arbitrary"), split=("rows", 2))


def _attn_out_bwd(du, w_out, o, sel_t, tm, tk):
    S, D = o.shape

    def epilogue(acc, extra_refs, out_refs, j, ci):
        out_refs[0][...] = acc
        out_refs[1][...] = _exact_nn(acc * extra_refs[0][...], extra_refs[1][...])

    row = pl.BlockSpec((tm, D), lambda i, j, k: (i, 0))
    slim = pl.BlockSpec((tm, LANES), lambda i, j, k: (i, 0))
    return _matmul("attn_out_bwd", du, w_out, "nt", tm, D, tk,
                   [(_sds((S, D), F32), row, True), (_sds((S, LANES), F32), slim, True)], epilogue,
                   extras=[(o, row, True), (sel_t, pl.BlockSpec((D, LANES), lambda i, j, k: (0, 0)))], split=("rows", 2))


def _adamw(name, w, gs, m, v):
    shape = w.shape
    cols = shape[-1]
    rows = math.prod(shape[:-1])
    w2, m2, v2 = (t.reshape(rows, cols) for t in (w, m, v))
    gs2 = [g.reshape(-1, cols) for g in gs]
    ng = len(gs2)
    tr = _pick(rows // ng, (256, 128, 64, 32, 16, 8))
    per = rows // ng // tr
    c1 = 1.0 - ADAM_B1 ** ADAM_STEP
    c2 = 1.0 - ADAM_B2 ** ADAM_STEP

    def body(w_ref, m_ref, v_ref, *rest):
        g_refs, (d_ref, nm_ref, nv_ref), g_out = rest[:ng], rest[ng:ng + 3], rest[ng + 3:]
        gg = g_refs[0][...]
        if ng == 2:
            gg = jnp.where(pl.program_id(0) < per, gg, g_refs[1][...])
        g_out[0][...] = gg
        nm = ADAM_B1 * m_ref[...] + (1.0 - ADAM_B1) * gg
        nv = ADAM_B2 * v_ref[...] + (1.0 - ADAM_B2) * (gg * gg)
        nm_ref[...] = nm
        nv_ref[...] = nv
        d_ref[...] = -ADAM_LR * ((nm / c1) / (jnp.sqrt(nv / c2) + ADAM_EPS) + ADAM_WD * w_ref[...])

    blk = pl.BlockSpec((tr, cols), lambda i: (i, 0))
    g_specs = [blk] if ng == 1 else [pl.BlockSpec((tr, cols), lambda i: (jnp.minimum(i, per - 1), 0)),
                                     pl.BlockSpec((tr, cols), lambda i: (jnp.maximum(i - per, 0), 0))]
    outs = pl.pallas_call(
        body,
        name=name,
        grid=(rows // tr,),
        in_specs=[blk] * 3 + g_specs,
        out_specs=[blk] * 4,
        out_shape=[_sds((rows, cols), F32)] * 4,
        compiler_params=_cparams(("parallel",)),
    )(w2, m2, v2, *gs2)
    return tuple(o.reshape(shape) for o in outs)


HBM = pl.BlockSpec(memory_space=pl.ANY)


def _shard_slice(ref, axis, size, index):
    idx = [slice(None)] * len(ref.shape)
    idx[axis] = pl.ds(pl.multiple_of(index * size, 8), size)
    return ref.at[tuple(idx)]


def _share_halves(name, full, tr):
    R, W4 = full.shape
    W, h = W4 // 4, R // 2
    steps = [(k, t) for k in range(3) for t in range(h // tr)]

    def body(f_in, f_ref, buf, lsem, ssem, rsem):
        x, y, c = lax.axis_index("x"), lax.axis_index("y"), lax.axis_index("c")
        sibling = (x, y, 1 - c)
        chips = [(1 - x, y), (x, 1 - y), (1 - x, 1 - y)]

        def tile(k, t):
            px, py = chips[k]
            return f_ref.at[pl.ds(pl.multiple_of(c * h + t * tr, 8), tr), pl.ds(pl.multiple_of((2 * px + py) * W, LANES), W)]

        sends = []
        for s, (k, t) in enumerate(steps):
            slot = s % 2
            if s >= 2:
                sends[s - 2].wait_send()
            lc = pltpu.make_async_copy(tile(k, t), buf.at[slot], lsem.at[slot])
            lc.start()
            lc.wait()
            rc = pltpu.make_async_remote_copy(src_ref=buf.at[slot], dst_ref=tile(k, t), send_sem=ssem.at[slot], recv_sem=rsem,
                                              device_id=sibling, device_id_type=MESH)
            rc.start()
            sends.append(rc)
        for rc in sends[-2:]:
            rc.wait_send()
        whole = f_ref.at[pl.ds(0, h), pl.ds(0, 3 * W)]
        pltpu.make_async_remote_copy(src_ref=whole, dst_ref=whole, send_sem=ssem.at[0], recv_sem=rsem,
                                     device_id=sibling, device_id_type=MESH).wait_recv()

    return pl.pallas_call(
        body,
        name=name,
        in_specs=[HBM],
        out_specs=HBM,
        out_shape=_sds(full.shape, full.dtype),
        input_output_aliases={0: 0},
        scratch_shapes=[pltpu.VMEM((2, tr, W), full.dtype), pltpu.SemaphoreType.DMA((2,)), pltpu.SemaphoreType.DMA((2,)),
                        pltpu.SemaphoreType.DMA(())],
    )(full)


IN_HBM = pl.BlockSpec(memory_space=pltpu.HBM)
IN_SEM = pl.BlockSpec(memory_space=pltpu.SEMAPHORE)
DATAFLOW = pltpu.SideEffectType.DATAFLOW_SIDE_EFFECTING


def _hbm(t):
    return pltpu.with_memory_space_constraint(t, pltpu.HBM)


def _token_spec():
    return pl.BlockSpec(memory_space=pltpu.VMEM)


def _gather_copies(s_refs, f_refs, axes, halves, send, recv, loc, arrival):
    x, y, c = lax.axis_index("x"), lax.axis_index("y"), lax.axis_index("c")
    chips = [(1 - x, y), (x, 1 - y), (1 - x, 1 - y)]
    local, remote = [], []
    for a in range(len(s_refs)):
        size = s_refs[a].shape[axes[a]]
        local.append(pltpu.make_async_copy(s_refs[a], _shard_slice(f_refs[a], axes[a], size, 2 * x + y), loc.at[a]))
        for k, (px, py) in enumerate(chips):
            block = (2 * px + py) if arrival else (2 * x + y)
            src, dst = s_refs[a], _shard_slice(f_refs[a], axes[a], size, block)
            if halves:
                assert axes[a] == 1 and len(s_refs[a].shape) == 2
                h = s_refs[a].shape[0] // 2
                rows = pl.ds(pl.multiple_of(c * h, 8), h)
                src = s_refs[a].at[rows, :]
                dst = f_refs[a].at[rows, pl.ds(pl.multiple_of(block * size, LANES), size)]
            remote.append(pltpu.make_async_remote_copy(src_ref=src, dst_ref=dst, send_sem=send.at[3 * a + k],
                                                       recv_sem=recv.at[3 * a + k], device_id=(px, py, c), device_id_type=MESH))
    return local, remote


def _gather_start(name, shards, axes, after, halves=False):
    n = len(shards)
    fulls = []
    for s, ax in zip(shards, axes):
        fs = list(s.shape)
        fs[ax] *= 4
        fulls.append(lax.empty(tuple(fs), s.dtype))

    def body(*refs):
        s_refs, f_refs = refs[:n], refs[n:2 * n]
        send, recv, loc, token = refs[2 * n + 1], refs[2 * n + 2], refs[2 * n + 3], refs[-1]
        local, remote = _gather_copies(s_refs, f_refs, axes, halves, send, recv, loc, arrival=False)
        for cp in remote + local:
            cp.start()
        token[...] = jnp.zeros_like(token)

    outs = pl.pallas_call(
        body,
        name=name,
        out_shape=(pltpu.SemaphoreType.DMA((3 * n,)), pltpu.SemaphoreType.DMA((3 * n,)), pltpu.SemaphoreType.DMA((n,)),
                   *[pltpu.HBM(t.shape, t.dtype) for t in shards + fulls], _sds((8, LANES), F32)),
        in_specs=[IN_HBM] * (2 * n) + [HBM],
        out_specs=(IN_SEM, IN_SEM, IN_SEM, *[IN_HBM] * (2 * n), _token_spec()),
        input_output_aliases={i: 3 + i for i in range(2 * n)},
        compiler_params=pltpu.CompilerParams(has_side_effects=DATAFLOW),
    )(*[_hbm(t) for t in shards + fulls], after)
    return (outs[0], outs[1], outs[2], list(outs[3:3 + n]), list(outs[3 + n:3 + 2 * n]), axes, halves), outs[-1]


def _gather_wait(name, state, *after):
    send, recv, loc, s_thru, f_thru, axes, halves = state
    n = len(s_thru)

    def body(*refs):
        s_refs, f_refs = refs[:n], refs[n:2 * n]
        local, remote = _gather_copies(s_refs, f_refs, axes, halves, refs[2 * n], refs[2 * n + 1], refs[2 * n + 2], arrival=True)
        for cp in local:
            cp.wait()
        for cp in remote:
            cp.wait_send()
            cp.wait_recv()

    outs = pl.pallas_call(
        body,
        name=name,
        out_shape=tuple(pltpu.HBM(t.shape, t.dtype) for t in s_thru + f_thru),
        in_specs=[IN_HBM] * (2 * n) + [IN_SEM, IN_SEM, IN_SEM] + [HBM] * len(after),
        out_specs=tuple([IN_HBM] * (2 * n)),
        input_output_aliases={i: i for i in range(2 * n)},
        compiler_params=pltpu.CompilerParams(has_side_effects=DATAFLOW),
    )(*s_thru, *f_thru, send, recv, loc, *after)
    return list(outs[n:2 * n])


FLIPS = [(fx, fy, fc) for fx in (0, 1) for fy in (0, 1) for fc in (0, 1)][1:]


def _piece_shape(shape, axis):
    ps = list(shape)
    if axis == 0:
        ps[0] //= 8
    else:
        ps[0] //= 2
        ps[axis] //= 4
    return tuple(ps)


def _piece(ref, axis, q, c):
    shape = ref.shape
    idx = [slice(None)] * len(shape)
    if axis == 0:
        h = shape[0] // 8
        idx[0] = pl.ds(pl.multiple_of((2 * q + c) * h, 8), h)
    else:
        h, w = shape[0] // 2, shape[axis] // 4
        idx[0] = pl.ds(c * h, h)
        idx[axis] = pl.ds(pl.multiple_of(q * w, LANES if axis == len(shape) - 1 else 8), w)
    return ref.at[tuple(idx)]


def _scatter_copies(g_refs, l_refs, axes, send, recv):
    x, y, c = lax.axis_index("x"), lax.axis_index("y"), lax.axis_index("c")
    out = []
    for a in range(len(g_refs)):
        for k, (fx, fy, fc) in enumerate(FLIPS):
            tx, ty, tc = x ^ fx, y ^ fy, c ^ fc
            out.append(pltpu.make_async_remote_copy(
                src_ref=_piece(g_refs[a], axes[a], 2 * tx + ty, tc), dst_ref=l_refs[a].at[k],
                send_sem=send.at[7 * a + k], recv_sem=recv.at[7 * a + k], device_id=(tx, ty, tc), device_id_type=MESH))
    return out


def _scatter_start(name, grads, axes):
    n = len(grads)
    lands = [lax.empty((7,) + _piece_shape(g.shape, ax), g.dtype) for g, ax in zip(grads, axes)]

    def body(*refs):
        g_refs, l_refs = refs[:n], refs[n:2 * n]
        send, recv, token = refs[2 * n], refs[2 * n + 1], refs[-1]
        for cp in _scatter_copies(g_refs, l_refs, axes, send, recv):
            cp.start()
        token[...] = jnp.zeros_like(token)

    outs = pl.pallas_call(
        body,
        name=name,
        out_shape=(pltpu.SemaphoreType.DMA((7 * n,)), pltpu.SemaphoreType.DMA((7 * n,)),
                   *[pltpu.HBM(t.shape, t.dtype) for t in grads + lands], _sds((8, LANES), F32)),
        in_specs=[IN_HBM] * (2 * n),
        out_specs=(IN_SEM, IN_SEM, *[IN_HBM] * (2 * n), _token_spec()),
        input_output_aliases={i: 2 + i for i in range(2 * n)},
        compiler_params=pltpu.CompilerParams(has_side_effects=DATAFLOW),
    )(*[_hbm(t) for t in grads + lands])
    return (outs[0], outs[1], list(outs[2:2 + n]), list(outs[2 + n:2 + 2 * n]), axes), outs[-1]


def _scatter_wait(name, state, *after):
    send, recv, g_thru, l_thru, axes = state
    n = len(g_thru)

    def body(*refs):
        g_refs, l_refs = refs[:n], refs[n:2 * n]
        for cp in _scatter_copies(g_refs, l_refs, axes, refs[2 * n], refs[2 * n + 1]):
            cp.wait_send()
            cp.wait_recv()

    outs = pl.pallas_call(
        body,
        name=name,
        out_shape=tuple(pltpu.HBM(t.shape, t.dtype) for t in g_thru + l_thru),
        in_specs=[IN_HBM] * (2 * n) + [IN_SEM, IN_SEM] + [HBM] * len(after),
        out_specs=tuple([IN_HBM] * (2 * n)),
        input_output_aliases={i: i for i in range(2 * n)},
        compiler_params=pltpu.CompilerParams(has_side_effects=DATAFLOW),
    )(*g_thru, *l_thru, send, recv, *after)
    return list(outs[:n]), list(outs[n:2 * n])


def _reduce_join(name, landing, g, axis):
    R, C = _piece_shape(g.shape, axis)
    l3 = landing.reshape(7, R, C)
    tr = _pick(R, [t for t in (512, 256, 128, 64, 32, 16, 8) if t * C <= 256 * 1024])
    nsteps = R // tr

    def own_block(i):
        q, c = 2 * lax.axis_index("x") + lax.axis_index("y"), lax.axis_index("c")
        return ((2 * q + c) * nsteps + i, 0) if axis == 0 else (c * nsteps + i, q)

    def body(own_ref, l_ref, o_ref, buf, send, loc, recv):
        i = pl.program_id(0)
        x, y, c = lax.axis_index("x"), lax.axis_index("y"), lax.axis_index("c")
        sibling = (x, y, 1 - c)

        def copies(slot, step):
            dst = o_ref.at[pl.ds(pl.multiple_of(c * R + step * tr, 8), tr), :]
            return (pltpu.make_async_copy(buf.at[slot], dst, loc.at[slot]),
                    pltpu.make_async_remote_copy(src_ref=buf.at[slot], dst_ref=dst, send_sem=send.at[slot], recv_sem=recv,
                                                 device_id=sibling, device_id_type=MESH))

        @pl.when(i >= 2)
        def _():
            lc, rc = copies(i % 2, i - 2)
            lc.wait()
            rc.wait_send()

        acc = own_ref[...].astype(F32)
        for s in range(7):
            acc = acc + l_ref[s].astype(F32)
        buf[i % 2] = acc
        lc, rc = copies(i % 2, i)
        lc.start()
        rc.start()

        @pl.when(i == nsteps - 1)
        def _():
            for st in range(max(nsteps - 2, 0), nsteps):
                lc, rc = copies(st % 2, st)
                lc.wait()
                rc.wait_send()
            theirs = o_ref.at[pl.ds(pl.multiple_of((1 - c) * R, 8), R), :]
            pltpu.make_async_remote_copy(src_ref=theirs, dst_ref=theirs, send_sem=send.at[0], recv_sem=recv,
                                         device_id=sibling, device_id_type=MESH).wait_recv()

    return pl.pallas_call(
        body,
        name=name,
        grid=(nsteps,),
        in_specs=[pl.BlockSpec((tr, C), own_block), pl.BlockSpec((7, tr, C), lambda i: (0, i, 0))],
        out_specs=HBM,
        out_shape=_sds((2 * R, C), F32),
        scratch_shapes=[pltpu.VMEM((2, tr, C), F32), pltpu.SemaphoreType.DMA((2,)), pltpu.SemaphoreType.DMA((2,)),
                        pltpu.SemaphoreType.DMA(())],
        compiler_params=_cparams(("arbitrary",)),
    )(g, l3)


def _all_reduce_small(v, dep):
    R, D = v.shape

    def body(v_ref, dep_ref, o_ref, land, send, recv):
        x, y, c = lax.axis_index("x"), lax.axis_index("y"), lax.axis_index("c")
        my_slot = 4 * x + 2 * y + c
        land[my_slot] = v_ref[...]
        for k, (fx, fy, fc) in enumerate(FLIPS):
            tx, ty, tc = x ^ fx, y ^ fy, c ^ fc
            pltpu.make_async_remote_copy(src_ref=v_ref, dst_ref=land.at[my_slot], send_sem=send.at[k], recv_sem=recv.at[k],
                                         device_id=(tx, ty, tc), device_id_type=MESH).start()
        for k, (fx, fy, fc) in enumerate(FLIPS):
            tx, ty, tc = x ^ fx, y ^ fy, c ^ fc
            cp = pltpu.make_async_remote_copy(src_ref=v_ref, dst_ref=land.at[4 * tx + 2 * ty + tc], send_sem=send.at[k],
                                              recv_sem=recv.at[k], device_id=(tx, ty, tc), device_id_type=MESH)
            cp.wait_send()
            cp.wait_recv()
        acc = land[0]
        for s in range(1, 8):
            acc = acc + land[s]
        o_ref[...] = acc

    return pl.pallas_call(
        body,
        name="all_reduce_small",
        in_specs=[pl.BlockSpec(memory_space=pltpu.VMEM), pl.BlockSpec(memory_space=pl.ANY)],
        out_specs=pl.BlockSpec(memory_space=pltpu.VMEM),
        out_shape=_sds((R, D), F32),
        scratch_shapes=[pltpu.VMEM((8, R, D), F32), pltpu.SemaphoreType.DMA((7,)), pltpu.SemaphoreType.DMA((7,))],
    )(v, dep)


def kernel(x, attn_w_in, attn_w_out, hgrn_w_in, hgrn_w_out, hgrn_norm_g, lb_logits, ln_mix_g, ln_mix_b, ln_ffn_g, ln_ffn_b, ffn_w_up, ffn_w_down, loss_target, m_attn_w_in, m_attn_w_out, m_hgrn_w_in, m_hgrn_w_out, m_hgrn_norm_g, m_lb_logits, m_ln_mix_g, m_ln_mix_b, m_ln_ffn_g, m_ln_ffn_b, m_ffn_w_up, m_ffn_w_down, v_attn_w_in, v_attn_w_out, v_hgrn_w_in, v_hgrn_w_out, v_hgrn_norm_g, v_lb_logits, v_ln_mix_g, v_ln_mix_b, v_ln_ffn_g, v_ln_ffn_b, v_ffn_w_up, v_ffn_w_down):
    xs = x[0]
    tgt = loss_target[0]
    S, D = xs.shape
    F = ffn_w_up.shape[2] * 4
    T1 = _pick(S, (1024, 512, 256))
    T2 = _pick(S, (2048, 1024, 512))
    TH = _pick(S, (512, 256))
    TB = _pick(S, (128,))
    TF = _pick(F, (1024, 512))
    TG = 3 * D // 4

    cast = lambda w: w.astype(MXU_DTYPE)
    st_a, tok = _gather_start("gather_a", [cast(attn_w_in[0])], [1], jnp.zeros((8, LANES), F32), halves=True)
    tok, (xs_late, w_aout, w_fup, w_fdown, w_hin, w_hout) = lax.optimization_barrier(
        (tok, (xs, attn_w_out, ffn_w_up, ffn_w_down, hgrn_w_in, hgrn_w_out)))
    st_b, tok = _gather_start("gather_b", [cast(w_aout[0]), cast(w_fup[0]), cast(w_fdown[0])], [0, 1, 0], tok)
    st_c, tok = _gather_start("gather_c", [cast(w_hin[0]), cast(w_hout[0]), hgrn_norm_g, cast(w_fup[1]), cast(w_fdown[1])],
                              [1, 0, 1, 1, 0], tok)

    cos3, sin3 = _rope_tables(S)
    sel = _head_sel(D)
    sel_t = sel.T

    xc3 = _stack_classes("x_classes", xs_late, MXU_DTYPE)
    P3 = _attn_proj("attn_proj_own", xc3, st_a[3][0], cos3, sin3, T2, None, tok)
    (wa_in,) = _gather_wait("gather_a_wait", st_a, P3)
    wa_in = _share_halves("share_a", wa_in, _pick(D // 2, (256, 128)))
    P3 = _attn_proj("attn_proj", xc3, wa_in, cos3, sin3, T2, P3)
    o3, lse3 = _attn_fwd(P3, D)
    o_att, L_att = _attn_mix(o3, lse3, sel)
    wa_out, w_up0, w_down0 = _gather_wait("gather_b_wait", st_b, L_att)
    ln1 = (ln_mix_g[0:1], ln_mix_b[0:1])
    ln2 = (ln_ffn_g[0:1], ln_ffn_b[0:1])
    ln3 = (ln_mix_g[1:2], ln_mix_b[1:2])
    ln4 = (ln_ffn_g[1:2], ln_ffn_b[1:2])
    xm1, xh1, r1 = _mm_res_ln("attn_out_ln", o_att, wa_out, xs, *ln1, T1, D)
    a0 = _mlp_up("mlp0_up", xm1, w_up0, T2, TF, D)
    xm2, xh2, r2 = _mm_res_ln("mlp0_down_ln", a0, w_down0, (xh1, *ln1), *ln2, TH, F)

    wh_in, wh_out, norm_g, w_up1, w_down1 = _gather_wait("gather_c_wait", st_c, r2)
    P1 = _plain_mm("hgrn_proj", xm2, wh_in, "nn", F32, T1, _pick(3 * D, (1024, 768, 512)), D)
    o_h, n_h, states = _hgrn_fwd(P1, lb_logits, norm_g, TB)
    xm3, xh3, r3 = _mm_res_ln("hgrn_out_ln", n_h, wh_out, (xh2, *ln2), *ln3, T1, D)
    a1 = _mlp_up("mlp1_up", xm3, w_up1, T2, TF, D)
    _, xh4, r4 = _mm_res_ln("mlp1_down_ln", a1, w_down1, (xh3, *ln3), *ln4, TH, F)

    wgrad = lambda name, a, dy, tm, tn, tk=T1: _plain_mm(name, a, dy, "tn", MXU_DTYPE, tm, tn, tk)
    sq, du4, dum4, dg_ffn1, db_ffn1 = _loss_ln_bwd(tgt, xh4, r4, *ln4, TH)
    dh1 = _mlp_down_bwd("mlp1_down_bwd", dum4, w_down1, a1, T2, TF, D)
    g_down1 = wgrad("g_down1", a1, dum4, TF, D, S)
    g_up1 = wgrad("g_up1", xm3, dh1, D, TF, S)
    sc_1, tok = _scatter_start("scatter_1", [g_down1, g_up1], [0, 1])
    du3, dum3, dg_mix1, db_mix1 = _mm_nt_res_ln_bwd("mlp1_up_bwd", dh1, w_up1, du4, xh3, r3, ln_mix_g[1:2], TH, F, tok)
    dn = _plain_mm("hgrn_out_bwd", dum3, wh_out, "nt", F32, T1, D, D)
    g_hout = wgrad("g_hgrn_out", n_h, dum3, D, D)
    dP1, dg_norm, dlb = _hgrn_bwd(P1, o_h, states, dn, lb_logits, norm_g, TB)
    g_hin = wgrad("g_hgrn_in", xm2, dP1, D, D, S)
    d_lb_logits = _lb_logits_grad(dlb, lb_logits)
    sc_2, tok = _scatter_start("scatter_2", [g_hout, g_hin], [0, 1])

    du2, dum2, dg_ffn0, db_ffn0 = _mm_nt_res_ln_bwd("hgrn_in_bwd", dP1, wh_in, du3, xh2, r2, ln_ffn_g[0:1], TH, 3 * D, tok)
    dh0 = _mlp_down_bwd("mlp0_down_bwd", dum2, w_down0, a0, T2, TF, D)
    g_down0 = wgrad("g_down0", a0, dum2, TF, D, S)
    g_up0 = wgrad("g_up0", xm1, dh0, D, TF, S)
    sc_3, tok = _scatter_start("scatter_3", [g_down0, g_up0], [0, 1])
    du1, dum1, dg_mix0, db_mix0 = _mm_nt_res_ln_bwd("mlp0_up_bwd", dh0, w_up0, du2, xh1, r1, ln_mix_g[0:1], TH, F, tok)
    do, delta = _attn_out_bwd(dum1, wa_out, o_att, sel_t, T1, D)
    g_aout = wgrad("g_attn_out", o_att, dum1, D, D)
    sc_5, tok = _scatter_start("scatter_5", [g_aout], [0])
    dP3 = _attn_bwd(P3, _stack_classes("do_classes", do, MXU_DTYPE), _stack_classes("lse_classes", L_att, F32),
                    _stack_classes("delta_classes", delta, F32), cos3, sin3, D, tok)
    small = jnp.concatenate([d_lb_logits, dg_mix0, dg_mix1, db_mix0, db_mix1, dg_ffn0, dg_ffn1, db_ffn0, db_ffn1,
                             dg_norm, sq, jnp.zeros((4, D), F32)], axis=0)
    small = _all_reduce_small(small, dP3)
    loss = 0.5 * jnp.sum(small[11]) / D
    grp = lambda j: j // (3 * D // TG)
    g_ain = _matmul("g_attn_in", xc3, dP3, "tn", D, TG, S, [(_sds((D, 9 * D), MXU_DTYPE), _ij_spec(D, TG))], _store_epilogue,
                    a_map=lambda i, j, k: (grp(j), i),
                    b_map=lambda i, j, k: (grp(j), j % (3 * D // TG)), mnk=(D, 9 * D, S), dep=small)[0]
    sc_4, tok = _scatter_start("scatter_4", [g_ain], [1])
    dxc3 = _matmul("attn_in_bwd", dP3, wa_in, "nt", T1, D, 3 * D, [(_sds((3 * S, D), F32), _ij_spec(T1, D))], _store_epilogue,
                   b_map=lambda i, j, k: (j, k + i // (S // T1)), mnk=(3 * S, D, 3 * D), dep=tok)[0]
    grad_x = _input_grad(du1, dxc3)

    def reduced(name, state, *after):
        gs, lands = _scatter_wait(name + "_wait", state, *after)
        return [_reduce_join(f"{name}_reduce_{i}", l, g, ax) for i, (l, g, ax) in enumerate(zip(lands, gs, state[4]))]

    r_down1, r_up1 = reduced("scatter_1", sc_1, grad_x)
    r_hout, r_hin = reduced("scatter_2", sc_2, r_up1)
    r_down0, r_up0 = reduced("scatter_3", sc_3, r_hin)
    (r_aout,) = reduced("scatter_5", sc_5, r_up0)

    my_chip = 2 * lax.axis_index("x") + lax.axis_index("y")
    nsh = hgrn_norm_g.shape[1]
    g_norm = lax.dynamic_slice(small[10:11], (0, my_chip * nsh), (1, nsh))

    grads, upd = {}, {}

    def update(nm, w, gs, m, v):
        upd[nm] = _adamw("adamw_" + nm, w, gs, m, v)
        grads[nm] = upd[nm][3]

    update("hgrn_w_in", hgrn_w_in, [r_hin], m_hgrn_w_in, v_hgrn_w_in)
    update("hgrn_w_out", hgrn_w_out, [r_hout], m_hgrn_w_out, v_hgrn_w_out)
    update("ffn_w_up", ffn_w_up, [r_up0, r_up1], m_ffn_w_up, v_ffn_w_up)
    update("ffn_w_down", ffn_w_down, [r_down0, r_down1], m_ffn_w_down, v_ffn_w_down)
    update("attn_w_out", attn_w_out, [r_aout], m_attn_w_out, v_attn_w_out)
    update("hgrn_norm_g", hgrn_norm_g, [g_norm], m_hgrn_norm_g, v_hgrn_norm_g)
    cat = lambda ts: jnp.concatenate(ts, axis=0)
    small_w = cat([lb_logits, ln_mix_g, ln_mix_b, ln_ffn_g, ln_ffn_b])
    small_m = cat([m_lb_logits, m_ln_mix_g, m_ln_mix_b, m_ln_ffn_g, m_ln_ffn_b])
    small_v = cat([v_lb_logits, v_ln_mix_g, v_ln_mix_b, v_ln_ffn_g, v_ln_ffn_b])
    small_upd = _adamw("adamw_small", small_w, [small[0:10]], small_m, small_v)
    for i, nm in enumerate(["lb_logits", "ln_mix_g", "ln_mix_b", "ln_ffn_g", "ln_ffn_b"]):
        grads[nm] = small[2 * i:2 * i + 2]
        upd[nm] = tuple(t[2 * i:2 * i + 2] for t in small_upd)
    done = [upd[k][2] for k in ("hgrn_w_in", "hgrn_w_out", "ffn_w_up", "ffn_w_down", "attn_w_out", "hgrn_norm_g")]
    (r_ain,) = reduced("scatter_4", sc_4, small_upd[2], *done)
    update("attn_w_in", attn_w_in, [r_ain], m_attn_w_in, v_attn_w_in)

    order = ["attn_w_in", "attn_w_out", "hgrn_w_in", "hgrn_w_out", "hgrn_norm_g", "lb_logits", "ln_mix_g", "ln_mix_b",
             "ln_ffn_g", "ln_ffn_b", "ffn_w_up", "ffn_w_down"]
    return (loss, grad_x[None], *[grads[k] for k in order], *[upd[k][0] for k in order],
            *[upd[k][1] for k in order], *[upd[k][2] for k in order])
```

```python
import math

import jax
import jax.numpy as jnp
from jax import lax
from jax.experimental import pallas as pl
from jax.experimental.pallas import tpu as pltpu

F32 = jnp.float32
BF16 = jnp.bfloat16
MXU_DTYPE = BF16

HEAD_DIM = 64
ATTN_BLK = 128
DILATIONS = (1, 4, 16)
ROPE_THETA = 10000.0
HGRN_DK = 128
HGRN_CHUNK = 64
DEPTH = 2
LN_EPS = 1e-5
RMS_EPS = 1e-6
ALPHA = (2 * DEPTH) ** 0.25
ADAM_LR, ADAM_B1, ADAM_B2, ADAM_EPS, ADAM_WD, ADAM_STEP = 0.001, 0.9, 0.999, 1e-08, 0.01, 10

LANES = 128
VMEM_LIMIT = 56 * 1024 * 1024
NEG = -1e30
MESH = pl.DeviceIdType.MESH


def _cparams(sem=None):
    return pltpu.CompilerParams(dimension_semantics=sem, vmem_limit_bytes=VMEM_LIMIT)


def _sds(shape, dtype):
    return jax.ShapeDtypeStruct(tuple(shape), dtype)


def _dg(a, b, ca, cb):
    return lax.dot_general(a, b, (((ca,), (cb,)), ((), ())), preferred_element_type=F32)


def _nn(a, b):
    return _dg(a, b, 1, 0)


def _nt(a, b):
    return _dg(a, b, 1, 1)


def _tn(a, b):
    return _dg(a, b, 0, 0)


def _split3(a):
    hi = a.astype(BF16)
    r = a - hi.astype(F32)
    mid = r.astype(BF16)
    lo = (r - mid.astype(F32)).astype(BF16)
    return hi, mid, lo


def _exact_nn(a, sel):
    hi, mid, lo = _split3(a)
    return _nn(hi, sel) + _nn(mid, sel) + _nn(lo, sel)


def _pick(n, prefs):
    for p in prefs:
        if n % p == 0:
            return p
    return n


def _matmul(name, a, b, form, tm, tn, tk, outs, epilogue, extras=(), a_map=None, b_map=None, mnk=None, dep=None,
            sem=("parallel", "parallel", "arbitrary"), split=None, alias_dep=False):
    if form == "nn":
        (M, K), N = a.shape, b.shape[1]
        a_spec = pl.BlockSpec((tm, tk), a_map or (lambda i, j, k: (i, k)))
        b_spec = pl.BlockSpec((tk, tn), b_map or (lambda i, j, k: (k, j)))
        ca, cb = 1, 0
    elif form == "nt":
        (M, K), N = a.shape, b.shape[0]
        a_spec = pl.BlockSpec((tm, tk), a_map or (lambda i, j, k: (i, k)))
        b_spec = pl.BlockSpec((tn, tk), b_map or (lambda i, j, k: (j, k)))
        ca, cb = 1, 1
    else:
        (K, M), N = a.shape, b.shape[1]
        a_spec = pl.BlockSpec((tk, tm), a_map or (lambda i, j, k: (k, i)))
        b_spec = pl.BlockSpec((tk, tn), b_map or (lambda i, j, k: (k, j)))
        ca, cb = 0, 0
    if mnk is not None:
        M, N, K = mnk
    assert M % tm == 0 and N % tn == 0 and K % tk == 0, (name, M, N, K, tm, tn, tk)
    nk = K // tk
    ne, no = len(extras), len(outs)
    deps = [] if dep is None else [dep]
    nd = len(deps)

    def body(a_ref, b_ref, *rest):
        extra_refs, out_refs = rest[:ne], rest[ne + nd:ne + nd + no]
        j = pl.program_id(1)
        if split is not None:
            kind, n = split
            assert nk == 1 and form != "tn"
            tiled = [t for _, _, *t in list(extras) + list(outs)]
            refs = list(extra_refs) + list(out_refs)
            for ci in range(n):
                if kind == "cols":
                    cs = slice(ci * (tn // n), (ci + 1) * (tn // n))
                    part = _dg(a_ref[...].astype(MXU_DTYPE), (b_ref[:, cs] if form == "nn" else b_ref[cs, :]).astype(MXU_DTYPE), ca, cb)
                    view = [r.at[:, cs] if t else r for r, t in zip(refs, tiled)]
                else:
                    rs = slice(ci * (tm // n), (ci + 1) * (tm // n))
                    part = _dg(a_ref[rs, :].astype(MXU_DTYPE), b_ref[...].astype(MXU_DTYPE), ca, cb)
                    view = [r.at[rs, :] if t else r for r, t in zip(refs, tiled)]
                epilogue(part, view[:ne], view[ne:], j, ci)
            return
        part = _dg(a_ref[...].astype(MXU_DTYPE), b_ref[...].astype(MXU_DTYPE), ca, cb)
        if nk == 1:
            epilogue(part, extra_refs, out_refs, j, 0)
            return
        acc_ref = rest[-1]
        k = pl.program_id(2)

        @pl.when(k == 0)
        def _():
            acc_ref[...] = part

        @pl.when(k > 0)
        def _():
            acc_ref[...] += part

        @pl.when(k == nk - 1)
        def _():
            epilogue(acc_ref[...], extra_refs, out_refs, j, 0)

    res = pl.pallas_call(
        body,
        name=name,
        grid=(M // tm, N // tn, nk),
        in_specs=[a_spec, b_spec] + [s for _, s, *_ in extras] + [pl.BlockSpec(memory_space=pl.ANY)] * nd,
        out_specs=[s for _, s, *_ in outs],
        out_shape=[o for o, *_ in outs],
        scratch_shapes=[pltpu.VMEM((tm, tn), F32)] if nk > 1 else [],
        input_output_aliases={2 + ne: 0} if alias_dep else {},
        compiler_params=_cparams(sem),
    )(a, b, *[e for e, *_ in extras], *deps)
    return res


def _ij_spec(tm, tn):
    return pl.BlockSpec((tm, tn), lambda i, j, k: (i, j))


def _store_epilogue(acc, extra_refs, out_refs, j, ci):
    out_refs[0][...] = acc.astype(out_refs[0].dtype)


def _plain_mm(name, a, b, form, out_dtype, tm, tn, tk):
    M = a.shape[1] if form == "tn" else a.shape[0]
    N = b.shape[0] if form == "nt" else b.shape[1]
    return _matmul(name, a, b, form, tm, tn, tk, [(_sds((M, N), out_dtype), _ij_spec(tm, tn))], _store_epilogue)[0]


def _class_slabs(S):
    assert DILATIONS[0] == 1
    return [(g, d, r, S // d) for g, d in enumerate(DILATIONS) if d > 1 for r in range(d)]


def _stack_classes(name, t, out_dtype):
    S, W = t.shape

    def body(x_ref, o_ref):
        o_ref[0:S, :] = x_ref[...].astype(out_dtype)
        for g, d, r, n in _class_slabs(S):
            o_ref[g * S + r * n:g * S + (r + 1) * n, :] = x_ref[pl.ds(r, n, stride=d), :].astype(out_dtype)

    return pl.pallas_call(
        body,
        name=name,
        grid=(W // LANES,),
        in_specs=[pl.BlockSpec((S, LANES), lambda j: (0, j))],
        out_specs=pl.BlockSpec((3 * S, LANES), lambda j: (0, j)),
        out_shape=_sds((3 * S, W), out_dtype),
        compiler_params=_cparams(("parallel",)),
    )(t)


def _rope_tables(seq):
    half = HEAD_DIM // 2
    inv = ROPE_THETA ** (-jnp.arange(half, dtype=F32) * (2.0 / HEAD_DIM))
    inv = jnp.tile(inv, LANES // half)
    pos = []
    for d in DILATIONS:
        row = jnp.arange(seq)
        pos.append((row % (seq // d)) * d + row // (seq // d))
    ang = jnp.concatenate(pos).astype(F32)[:, None] * inv[None, :]
    first = (jnp.arange(LANES) % HEAD_DIM) < half
    sin = jnp.sin(ang)
    return jnp.cos(ang), jnp.where(first[None, :], -sin, sin)


def _partner(x):
    half = HEAD_DIM // 2
    lane = lax.broadcasted_iota(jnp.int32, x.shape, 1)
    first = (lane % HEAD_DIM) < half
    return jnp.where(first, pltpu.roll(x, LANES - half, 1), pltpu.roll(x, half, 1))


def _attn_proj(name, x3, w, cos3, sin3, tm, prev, dep=None):
    S3, D = x3.shape
    S = S3 // 3
    tn = 3 * D // 4
    nrow = S // tm
    local = prev is None

    def tile(j):
        q = 2 * lax.axis_index("x") + lax.axis_index("y")
        c0 = 3 * q + j if local else j + 3 * (j >= 3 * q).astype(jnp.int32)
        return c0, c0 // 4, c0 % 4

    def epilogue(acc, extra_refs, out_refs, j, ci):
        cos_ref, sin_ref = extra_refs
        o_ref = out_refs[0]
        _, _, place = tile(j)
        width = acc.shape[1]
        assert D % width == 0
        is_rot = (place * tn + ci * width) // D < 2
        c = jnp.where(is_rot, cos_ref[...], 1.0)
        s = jnp.where(is_rot, sin_ref[...], 0.0)
        for t in range(width // LANES):
            xs = acc[:, t * LANES:(t + 1) * LANES]
            o_ref[:, t * LANES:(t + 1) * LANES] = (xs * c + _partner(xs) * s).astype(o_ref.dtype)

    rows = lambda i, j: tile(j)[1] * nrow + i
    tab = pl.BlockSpec((tm, LANES), lambda i, j, k: (rows(i, j), 0))
    out = pl.BlockSpec((tm, tn), lambda i, j, k: (rows(i, j), tile(j)[2]))
    ntiles = 3 if local else 9
    return _matmul(name, x3, w, "nn", tm, tn, D, [(_sds((S3, 3 * D), MXU_DTYPE), out, True)], epilogue,
                   extras=[(cos3, tab), (sin3, tab)], a_map=lambda i, j, k: (rows(i, j), k),
                   b_map=lambda i, j, k: (k, j if local else tile(j)[0]), mnk=(nrow * tm, ntiles * tn, D),
                   dep=dep if local else prev, alias_dep=not local, split=("cols", 3))[0]


def _head_sel(d_model):
    h = jnp.arange(LANES)[:, None]
    l = jnp.arange(d_model)[None, :]
    return (l // HEAD_DIM == h).astype(BF16)


def _class_edges(b, nblk):
    g = b // nblk
    per_class = jnp.where(g == 0, nblk // DILATIONS[0], jnp.where(g == 1, nblk // DILATIONS[1], nblk // DILATIONS[2]))
    pos = (b % nblk) % per_class
    return pos != 0, pos != per_class - 1


def _two_heads(t, top):
    zero = jnp.zeros_like(t)
    return jnp.concatenate([jnp.where(top, t, zero), jnp.where(top, zero, t)], axis=0)


def _band_mask(has_prev):
    B = ATTN_BLK
    row = lax.broadcasted_iota(jnp.int32, (2 * B, 2 * B), 0) % B
    col = lax.broadcasted_iota(jnp.int32, (2 * B, 2 * B), 1)
    in_prev = jnp.logical_and(jnp.logical_and(col < B, col >= row), has_prev)
    in_own = jnp.logical_and(col >= B, col - B <= row)
    return jnp.logical_or(in_prev, in_own)


def _attn_fwd(P3, D):
    S3 = P3.shape[0]
    B = ATTN_BLK
    nblk = S3 // 3 // B
    npairs = D // LANES
    scale = HEAD_DIM ** -0.5

    def body(q_ref, kc_ref, vc_ref, kp_ref, vp_ref, o_ref, lse_ref):
        has_prev, _ = _class_edges(pl.program_id(0), nblk)
        bias = jnp.where(_band_mask(has_prev), 0.0, NEG)
        lane = lax.broadcasted_iota(jnp.int32, (B, LANES), 1)
        top = lane < HEAD_DIM
        lse_acc = jnp.zeros((B, LANES), F32)
        for j in range(npairs):
            sl = slice(j * LANES, (j + 1) * LANES)
            Q = _two_heads(q_ref[:, sl] * scale, top)
            K2 = jnp.concatenate([kp_ref[:, sl], kc_ref[:, sl]], axis=0)
            V2 = jnp.concatenate([vp_ref[:, sl], vc_ref[:, sl]], axis=0)
            s = _nt(Q, K2) + bias
            m = jnp.max(s, axis=1, keepdims=True)
            p = jnp.exp(s - m)
            l = jnp.sum(p, axis=1, keepdims=True)
            o = _nn(p.astype(MXU_DTYPE), V2) * (1.0 / l)
            o_ref[:, sl] = jnp.where(top, o[:B], o[B:])
            lse = m + jnp.log(l)
            lse_acc = jnp.where(lane == 2 * j, lse[:B], jnp.where(lane == 2 * j + 1, lse[B:], lse_acc))
        lse_ref[...] = lse_acc

    blk = lambda part, prev: pl.BlockSpec(
        (B, D), (lambda b: (jnp.maximum(b - 1, 0), part)) if prev else (lambda b: (b, part)))
    return pl.pallas_call(
        body,
        name="attn_fwd",
        grid=(3 * nblk,),
        in_specs=[blk(0, False), blk(1, False), blk(2, False), blk(1, True), blk(2, True)],
        out_specs=[pl.BlockSpec((B, D), lambda b: (b, 0)), pl.BlockSpec((B, LANES), lambda b: (b, 0))],
        out_shape=[_sds((S3, D), F32), _sds((S3, LANES), F32)],
        compiler_params=_cparams(("parallel",)),
    )(P3, P3, P3, P3, P3)


def _attn_mix(o3, lse3, sel):
    S3, D = o3.shape
    S = S3 // 3

    def body(o3_ref, lse_ref, sel_ref, o_ref, L_ref, w_ref):
        @pl.when(pl.program_id(0) == 0)
        def _():
            w_ref[0] = lse_ref[0:S, :]
            for g, d, r, n in _class_slabs(S):
                w_ref[g, pl.ds(r, n, stride=d), :] = lse_ref[g * S + r * n:g * S + (r + 1) * n, :]
            a, b, c = w_ref[0], w_ref[1], w_ref[2]
            m = jnp.maximum(jnp.maximum(a, b), c)
            L = m + jnp.log(jnp.exp(a - m) + jnp.exp(b - m) + jnp.exp(c - m))
            L_ref[...] = L
            w_ref[0] = jnp.exp(a - L)
            w_ref[1] = jnp.exp(b - L)
            w_ref[2] = jnp.exp(c - L)

        s = sel_ref[...]
        o_ref[...] = _exact_nn(w_ref[0], s) * o3_ref[0:S, :]
        for g, d, r, n in _class_slabs(S):
            rows = pl.ds(r, n, stride=d)
            o_ref[rows, :] += _exact_nn(w_ref[g, rows, :], s) * o3_ref[g * S + r * n:g * S + (r + 1) * n, :]

    return pl.pallas_call(
        body,
        name="attn_mix",
        grid=(D // LANES,),
        in_specs=[pl.BlockSpec((S3, LANES), lambda j: (0, j)), pl.BlockSpec((S3, LANES), lambda j: (0, 0)),
                  pl.BlockSpec((LANES, LANES), lambda j: (0, j))],
        out_specs=[pl.BlockSpec((S, LANES), lambda j: (0, j)), pl.BlockSpec((S, LANES), lambda j: (0, 0))],
        out_shape=[_sds((S, D), F32), _sds((S, LANES), F32)],
        scratch_shapes=[pltpu.VMEM((3, S, LANES), F32)],
        compiler_params=_cparams(("arbitrary",)),
    )(o3, lse3, sel)


def _attn_bwd(P3, do3, L3, delta3, cos3, sin3, D, dep):
    S3 = P3.shape[0]
    B = ATTN_BLK
    nblk = S3 // 3 // B
    npairs = D // LANES
    scale = HEAD_DIM ** -0.5

    def body(c_ref, kp_ref, vp_ref, qn_ref, doc_ref, don_ref, Lc_ref, Ln_ref, dc_ref, dn_ref, cos_ref, sin_ref, dep_ref, out_ref):
        has_prev, has_next = _class_edges(pl.program_id(0), nblk)
        bias = jnp.where(_band_mask(has_prev), 0.0, NEG)
        row = lax.broadcasted_iota(jnp.int32, (2 * B, B), 0) % B
        col = lax.broadcasted_iota(jnp.int32, (2 * B, B), 1)
        bias_n = jnp.where(jnp.logical_and(col >= row, has_next), 0.0, NEG)
        lane = lax.broadcasted_iota(jnp.int32, (B, LANES), 1)
        top = lane < HEAD_DIM
        cos_t = cos_ref[...]
        sin_inv = -sin_ref[...]
        Lc_all, Ln_all, dc_all, dn_all = Lc_ref[...], Ln_ref[...], dc_ref[...], dn_ref[...]
        pair_col = lambda t, j: jnp.concatenate([t[:, 2 * j:2 * j + 1], t[:, 2 * j + 1:2 * j + 2]], axis=0)
        for j in range(npairs):
            sl = lambda part: slice(part * D + j * LANES, part * D + (j + 1) * LANES)
            pj = slice(j * LANES, (j + 1) * LANES)
            kc2, vc2 = c_ref[:, sl(1)], c_ref[:, sl(2)]
            K2 = jnp.concatenate([kp_ref[:, pj], kc2], axis=0)
            V2 = jnp.concatenate([vp_ref[:, pj], vc2], axis=0)
            qc2, qn2 = c_ref[:, sl(0)] * scale, qn_ref[:, pj] * scale
            doc2, don2 = doc_ref[:, pj].astype(MXU_DTYPE), don_ref[:, pj].astype(MXU_DTYPE)
            zero = jnp.zeros_like(qc2)
            dq_h, dk2, dv2 = [], 0.0, 0.0
            for h in range(2):
                mh = top if h == 0 else jnp.logical_not(top)
                hh = 2 * j + h
                Qc, Qn = jnp.where(mh, qc2, zero), jnp.where(mh, qn2, zero)
                DOc, DOn = jnp.where(mh, doc2, zero), jnp.where(mh, don2, zero)
                P_c = jnp.exp(_nt(Qc, K2) + bias[:B] - Lc_all[:, hh:hh + 1])
                dS_c = P_c * (_nt(DOc, V2) - dc_all[:, hh:hh + 1])
                P_n = jnp.exp(_nt(Qn, kc2) + bias_n[:B] - Ln_all[:, hh:hh + 1])
                dS_n = P_n * (_nt(DOn, vc2) - dn_all[:, hh:hh + 1])
                dq_h.append(_nn(dS_c.astype(MXU_DTYPE), K2))
                dk2 = dk2 + _tn(jnp.concatenate([dS_c[:, B:], dS_n], axis=0).astype(MXU_DTYPE), jnp.concatenate([Qc, Qn], axis=0))
                dv2 = dv2 + _tn(jnp.concatenate([P_c[:, B:], P_n], axis=0).astype(MXU_DTYPE), jnp.concatenate([DOc, DOn], axis=0))
            dq2 = jnp.where(top, dq_h[0], dq_h[1]) * scale
            out_ref[:, sl(0)] = (dq2 * cos_t + _partner(dq2) * sin_inv).astype(out_ref.dtype)
            out_ref[:, sl(1)] = (dk2 * cos_t + _partner(dk2) * sin_inv).astype(out_ref.dtype)
            out_ref[:, sl(2)] = dv2.astype(out_ref.dtype)

    cur = lambda b: b
    prv = lambda b: jnp.maximum(b - 1, 0)
    nxt = lambda b: jnp.minimum(b + 1, 3 * nblk - 1)
    spec = lambda w, f, part=0: pl.BlockSpec((B, w), lambda b: (f(b), part))
    return pl.pallas_call(
        body,
        name="attn_bwd",
        grid=(3 * nblk,),
        in_specs=[spec(3 * D, cur), spec(D, prv, 1), spec(D, prv, 2), spec(D, nxt, 0), spec(D, cur), spec(D, nxt),
                  spec(LANES, cur), spec(LANES, nxt), spec(LANES, cur), spec(LANES, nxt), spec(LANES, cur), spec(LANES, cur),
                  pl.BlockSpec(memory_space=pl.ANY)],
        out_specs=spec(3 * D, cur),
        out_shape=_sds((S3, 3 * D), MXU_DTYPE),
        compiler_params=_cparams(("parallel",)),
    )(P3, P3, P3, P3, do3, do3, L3, L3, delta3, delta3, cos3, sin3, dep)


def _input_grad(du, dx3):
    S, D = du.shape

    def body(du_ref, dx_ref, o_ref):
        o_ref[...] = ALPHA * du_ref[...] + dx_ref[0:S, :]
        for g, d, r, n in _class_slabs(S):
            o_ref[pl.ds(r, n, stride=d), :] += dx_ref[g * S + r * n:g * S + (r + 1) * n, :]

    return pl.pallas_call(
        body,
        name="input_grad",
        grid=(D // LANES,),
        in_specs=[pl.BlockSpec((S, LANES), lambda j: (0, j)), pl.BlockSpec((3 * S, LANES), lambda j: (0, j))],
        out_specs=pl.BlockSpec((S, LANES), lambda j: (0, j)),
        out_shape=_sds((S, D), F32),
        compiler_params=_cparams(("parallel",)),
    )(du, dx3)


def _chunk_causal(tb):
    r = lax.broadcasted_iota(jnp.int32, (tb, tb), 0)
    c = lax.broadcasted_iota(jnp.int32, (tb, tb), 1)
    return jnp.logical_and((r // HGRN_CHUNK) == (c // HGRN_CHUNK), r >= c)


def _chunk_sums(a, lower):
    C = HGRN_CHUNK
    r = lax.broadcasted_iota(jnp.int32, (C, C), 0)
    c = lax.broadcasted_iota(jnp.int32, (C, C), 1)
    tri = ((r >= c) if lower else (r <= c)).astype(BF16)
    parts = _split3(a)
    out = []
    for ci in range(a.shape[0] // C):
        rows = slice(ci * C, (ci + 1) * C)
        out.append(_nn(tri, parts[0][rows]) + _nn(tri, parts[1][rows]) + _nn(tri, parts[2][rows]))
    return jnp.concatenate(out, axis=0)


def _chunk_last(b):
    C = HGRN_CHUNK
    return jnp.concatenate([jnp.broadcast_to(b[(ci + 1) * C - 1:(ci + 1) * C, :], (C, b.shape[1]))
                            for ci in range(b.shape[0] // C)], axis=0)


def _lower_bound(lb_ref):
    l0, l1 = lb_ref[0:1, :], lb_ref[1:2, :]
    m = jnp.maximum(l0, l1)
    e0, e1 = jnp.exp(l0 - m), jnp.exp(l1 - m)
    return e1 / (e0 + e1)


def _hgrn_gates(q_raw, z, lb):
    sg = 1.0 / (1.0 + jnp.exp(-z))
    sn = 1.0 / (1.0 + jnp.exp(z))
    f = lb + (1.0 - lb) * sg
    key = (1.0 - lb) * sn
    sq = 1.0 / (1.0 + jnp.exp(-q_raw))
    return sg, sn, f, key, sq


def _hgrn_fwd(P1, lb_logits, norm_g, tb):
    S = P1.shape[0]
    D = P1.shape[1] // 3
    K = HGRN_DK
    H = D // K
    HP = H
    C = HGRN_CHUNK
    cpb = tb // C
    nt = S // tb

    def body(q_ref, f_ref, i_ref, lb_ref, g_ref, o_ref, n_ref, st_ref, state):
        t = pl.program_id(1)

        @pl.when(t == 0)
        def _():
            state[...] = jnp.zeros_like(state)

        lb_all = _lower_bound(lb_ref)
        low = _chunk_causal(tb)
        for hh in range(HP):
            lanes = slice(hh * K, (hh + 1) * K)
            q_raw, z, v = q_ref[:, lanes], f_ref[:, lanes], i_ref[:, lanes]
            sg, sn, f, key, sq = _hgrn_gates(q_raw, z, lb_all[:, lanes])
            b = _chunk_sums(jnp.log(f), lower=True)
            qd = (q_raw * sq * jnp.exp(b)).astype(MXU_DTYPE)
            kd = (key * jnp.exp(-b)).astype(MXU_DTYPE)
            kb = (key * jnp.exp(_chunk_last(b) - b)).astype(MXU_DTYPE)
            vm = v.astype(MXU_DTYPE)
            a = jnp.where(low, _nt(qd, kd), 0.0).astype(MXU_DTYPE)
            o_intra = _nn(a, vm)
            st = state[hh]
            outs = []
            for ci in range(cpb):
                rows = slice(ci * C, (ci + 1) * C)
                st_ref[hh, ci] = st
                outs.append(o_intra[rows] + _nt(qd[rows], st.astype(MXU_DTYPE)))
                st = st * jnp.exp(b[(ci + 1) * C - 1:(ci + 1) * C, :]) + _tn(vm[rows], kb[rows])
            state[hh] = st
            o = jnp.concatenate(outs, axis=0)
            o_ref[:, lanes] = o
            rs = lax.rsqrt(jnp.mean(o * o, axis=1, keepdims=True) + RMS_EPS)
            n_ref[:, lanes] = o * rs * g_ref[:, lanes]

    tok = lambda part: pl.BlockSpec((tb, HP * K), lambda h, t: (t, part * (H // HP) + h))
    vec = lambda rows: pl.BlockSpec((rows, HP * K), lambda h, t: (0, h))
    return pl.pallas_call(
        body,
        name="hgrn_fwd",
        grid=(H // HP, nt),
        in_specs=[tok(0), tok(1), tok(2), vec(2), vec(1)],
        out_specs=[tok(0), tok(0), pl.BlockSpec((HP, cpb, K, K), lambda h, t: (h, t, 0, 0))],
        out_shape=[_sds((S, D), F32), _sds((S, D), F32), _sds((H, S // C, K, K), F32)],
        scratch_shapes=[pltpu.VMEM((HP, K, K), F32)],
        compiler_params=_cparams(("parallel", "arbitrary")),
    )(P1, P1, P1, lb_logits, norm_g)


def _hgrn_bwd(P1, o_pre, states, dn, lb_logits, norm_g, tb):
    S = P1.shape[0]
    D = P1.shape[1] // 3
    K = HGRN_DK
    H = D // K
    HP = H
    C = HGRN_CHUNK
    cpb = tb // C
    nt = S // tb

    def body(q_ref, f_ref, i_ref, o_ref, st_ref, dn_ref, lb_ref, g_ref, d_ref, dg_ref, dlb_ref, dstate):
        t = pl.program_id(1)

        @pl.when(t == 0)
        def _():
            dstate[...] = jnp.zeros_like(dstate)
            dg_ref[...] = jnp.zeros_like(dg_ref)
            dlb_ref[...] = jnp.zeros_like(dlb_ref)

        lb_all = _lower_bound(lb_ref)
        low = _chunk_causal(tb)
        for hh in range(HP):
            lanes = slice(hh * K, (hh + 1) * K)
            lb = lb_all[:, lanes]
            gn = g_ref[:, lanes]
            q_raw, z, v = q_ref[:, lanes], f_ref[:, lanes], i_ref[:, lanes]
            sg, sn, f, key, sq = _hgrn_gates(q_raw, z, lb)
            b = _chunk_sums(jnp.log(f), lower=True)
            e_pos, e_neg, e_rel = jnp.exp(b), jnp.exp(-b), jnp.exp(_chunk_last(b) - b)
            qd_f, kd_f, kb_f = q_raw * sq * e_pos, key * e_neg, key * e_rel
            qd, kd, kb = qd_f.astype(MXU_DTYPE), kd_f.astype(MXU_DTYPE), kb_f.astype(MXU_DTYPE)
            vm = v.astype(MXU_DTYPE)
            a = jnp.where(low, _nt(qd, kd), 0.0).astype(MXU_DTYPE)
            o = o_ref[:, lanes]
            dnn = dn_ref[:, lanes]
            rs = lax.rsqrt(jnp.mean(o * o, axis=1, keepdims=True) + RMS_EPS)
            dg_ref[:, lanes] += jnp.sum(dnn * o * rs, axis=0, keepdims=True)
            tg = dnn * gn
            dom = (rs * tg - o * (rs * rs * rs) * jnp.mean(tg * o, axis=1, keepdims=True)).astype(MXU_DTYPE)
            da = jnp.where(low, _nt(dom, vm), 0.0).astype(MXU_DTYPE)
            dv = _tn(a, dom)
            dqd = _nn(da, kd)
            dkd = _tn(da, qd)
            dst = dstate[hh]
            dv_s, dqd_s, dkb_s, dbl_s = [None] * cpb, [None] * cpb, [None] * cpb, [None] * cpb
            for ci in reversed(range(cpb)):
                rows = slice(ci * C, (ci + 1) * C)
                st = st_ref[hh, ci]
                dstm = dst.astype(MXU_DTYPE)
                dec = jnp.exp(b[(ci + 1) * C - 1:(ci + 1) * C, :])
                dv_s[ci] = _nt(kb[rows], dstm)
                dkb_s[ci] = _nn(vm[rows], dstm)
                dqd_s[ci] = _nn(dom[rows], st.astype(MXU_DTYPE))
                db_last = jnp.sum(dkb_s[ci] * kb_f[rows], axis=0, keepdims=True) + jnp.sum(dst * st, axis=0, keepdims=True) * dec
                dbl_s[ci] = jnp.broadcast_to(db_last, (C, K))
                dst = dst * dec + _tn(dom[rows], qd[rows])
            dstate[hh] = dst
            dv = dv + jnp.concatenate(dv_s, axis=0)
            dqd = dqd + jnp.concatenate(dqd_s, axis=0)
            dkb = jnp.concatenate(dkb_s, axis=0)
            dkey = dkd * e_neg + dkb * e_rel
            db = dqd * qd_f - dkd * kd_f - dkb * kb_f
            dlogf = _chunk_sums(db, lower=False) + jnp.concatenate(dbl_s, axis=0)
            gz = (1.0 - lb) * sg * sn
            col = lambda part: slice(part * D + hh * K, part * D + (hh + 1) * K)
            d_ref[:, col(0)] = (dqd * e_pos * (sq + q_raw * sq * (1.0 - sq))).astype(d_ref.dtype)
            d_ref[:, col(1)] = (dlogf * gz / f - dkey * gz).astype(d_ref.dtype)
            d_ref[:, col(2)] = dv.astype(d_ref.dtype)
            dlb_ref[:, lanes] += jnp.sum(dlogf * sn / f - dkey * sn, axis=0, keepdims=True)

    rev = lambda t: nt - 1 - t
    tok = lambda part: pl.BlockSpec((tb, HP * K), lambda h, t: (rev(t), part * (H // HP) + h))
    vec = lambda rows: pl.BlockSpec((rows, HP * K), lambda h, t: (0, h))
    outs = pl.pallas_call(
        body,
        name="hgrn_bwd",
        grid=(H // HP, nt),
        in_specs=[tok(0), tok(1), tok(2), tok(0),
                  pl.BlockSpec((HP, cpb, K, K), lambda h, t: (h, rev(t), 0, 0)),
                  tok(0), vec(2), vec(1)],
        out_specs=[pl.BlockSpec((tb, 3 * D), lambda h, t: (rev(t), 0)), vec(1), vec(1)],
        out_shape=[_sds((S, 3 * D), MXU_DTYPE)] + [_sds((1, D), F32)] * 2,
        scratch_shapes=[pltpu.VMEM((HP, K, K), F32)],
        compiler_params=_cparams(("parallel", "arbitrary")),
    )(P1, P1, P1, o_pre, states, dn, lb_logits, norm_g)
    return outs


def _lb_logits_grad(dlb, lb_logits):
    def body(d_ref, l_ref, o_ref):
        s1 = _lower_bound(l_ref)
        d = d_ref[...]
        o_ref[0:1, :] = -(1.0 - s1) * s1 * d
        o_ref[1:2, :] = s1 * (1.0 - s1) * d

    return pl.pallas_call(body, name="lb_logits_grad", out_shape=_sds(lb_logits.shape, F32))(dlb, lb_logits)


def _mm_res_ln(name, a, w_full, res, g, b, tm, tk):
    from_ln = isinstance(res, tuple)
    S, D = (res[0] if from_ln else res).shape

    def epilogue(acc, extra_refs, out_refs, j, ci):
        g_ref, b_ref = extra_refs[:2]
        xm_ref, xhat_ref, rstd_ref = out_refs
        r = extra_refs[2][...] * extra_refs[3][...] + extra_refs[4][...] if from_ln else extra_refs[2][...]
        u = ALPHA * r + acc
        mu = jnp.mean(u, axis=1, keepdims=True)
        cen = u - mu
        rstd = lax.rsqrt(jnp.mean(cen * cen, axis=1, keepdims=True) + LN_EPS)
        xhat = cen * rstd
        xhat_ref[...] = xhat
        xm_ref[...] = (xhat * g_ref[...] + b_ref[...]).astype(xm_ref.dtype)
        rstd_ref[...] = rstd

    row = pl.BlockSpec((tm, D), lambda i, j, k: (i, 0))
    vec = pl.BlockSpec((1, D), lambda i, j, k: (0, 0))
    outs = [(_sds((S, D), MXU_DTYPE), row, True), (_sds((S, D), F32), row, True),
            (_sds((S, 1), F32), pl.BlockSpec((tm, 1), lambda i, j, k: (i, 0)), True)]
    res_extras = [(res[0], row, True), (res[1], vec), (res[2], vec)] if from_ln else [(res, row, True)]
    return _matmul(name, a, w_full, "nn", tm, D, tk, outs, epilogue, extras=[(g, vec), (b, vec)] + res_extras,
                   split=("rows", 2) if tk == a.shape[1] else None)


def _ln_bwd_rows(dy, xh, rstd, g, first, du_ref, dum_ref, dg_ref, db_ref):
    if first is not None:
        @pl.when(first)
        def _():
            dg_ref[...] = jnp.zeros_like(dg_ref)
            db_ref[...] = jnp.zeros_like(db_ref)

    dg_ref[...] += jnp.sum(dy * xh, axis=0, keepdims=True)
    db_ref[...] += jnp.sum(dy, axis=0, keepdims=True)
    dxh = dy * g
    m1 = jnp.mean(dxh, axis=1, keepdims=True)
    m2 = jnp.mean(dxh * xh, axis=1, keepdims=True)
    du = rstd * (dxh - m1 - xh * m2)
    du_ref[...] = du
    dum_ref[...] = du.astype(dum_ref.dtype)


def _loss_ln_bwd(target, xhat, rstd, g, b, tm):
    S, D = xhat.shape

    def body(t_ref, xh_ref, r_ref, g_ref, b_ref, sq_ref, du_ref, dum_ref, dg_ref, db_ref):
        first = pl.program_id(0) == 0

        @pl.when(first)
        def _():
            sq_ref[...] = jnp.zeros_like(sq_ref)

        xh = xh_ref[...]
        e = xh * g_ref[...] + b_ref[...] - t_ref[...]
        sq_ref[...] += jnp.sum(e * e, axis=0, keepdims=True)
        _ln_bwd_rows(e / D, xh, r_ref[...], g_ref[...], first, du_ref, dum_ref, dg_ref, db_ref)

    row = pl.BlockSpec((tm, D), lambda i: (i, 0))
    vec = pl.BlockSpec((1, D), lambda i: (0, 0))
    return pl.pallas_call(
        body,
        name="loss_ln_bwd",
        grid=(S // tm,),
        in_specs=[row, row, pl.BlockSpec((tm, 1), lambda i: (i, 0)), vec, vec],
        out_specs=[vec, row, row, vec, vec],
        out_shape=[_sds((1, D), F32), _sds((S, D), F32), _sds((S, D), MXU_DTYPE), _sds((1, D), F32), _sds((1, D), F32)],
        compiler_params=_cparams(("arbitrary",)),
    )(target, xhat, rstd, g, b)


def _mlp_up(name, x, w_up, tm, tn, tk):
    S = x.shape[0]
    F = w_up.shape[1]

    def epilogue(acc, extra_refs, out_refs, j, ci):
        r = jnp.maximum(acc, 0.0)
        out_refs[0][...] = (r * r).astype(out_refs[0].dtype)

    return _matmul(name, x, w_up, "nn", tm, tn, tk, [(_sds((S, F), MXU_DTYPE), _ij_spec(tm, tn), True)], epilogue,
                   split=("cols", 2))[0]


def _mlp_down_bwd(name, dy, w_down, a, tm, tn, tk):
    S, F = a.shape

    def epilogue(acc, extra_refs, out_refs, j, ci):
        out_refs[0][...] = (acc * (2.0 * jnp.sqrt(extra_refs[0][...]).astype(F32))).astype(out_refs[0].dtype)

    return _matmul(name, dy, w_down, "nt", tm, tn, tk, [(_sds((S, F), MXU_DTYPE), _ij_spec(tm, tn), True)], epilogue,
                   extras=[(a, _ij_spec(tm, tn), True)], split=("cols", 2))[0]


def _mm_nt_res_ln_bwd(name, dy, w, du, xhat, rstd, g, tm, tk, dep):
    S, D = du.shape

    def epilogue(acc, extra_refs, out_refs, j, ci):
        du_ref, xh_ref, r_ref, g_ref = extra_refs
        first = (pl.program_id(0) == 0) if ci == 0 else None
        _ln_bwd_rows(ALPHA * du_ref[...] + acc, xh_ref[...], r_ref[...], g_ref[...], first, *out_refs)

    row = pl.BlockSpec((tm, D), lambda i, j, k: (i, 0))
    vec = pl.BlockSpec((1, D), lambda i, j, k: (0, 0))
    return _matmul(name, dy, w, "nt", tm, D, tk,
                   [(_sds((S, D), F32), row, True), (_sds((S, D), MXU_DTYPE), row, True), (_sds((1, D), F32), vec),
                    (_sds((1, D), F32), vec)], epilogue,
                   extras=[(du, row, True), (xhat, row, True), (rstd, pl.BlockSpec((tm, 1), lambda i, j, k: (i, 0)), True), (g, vec)],
                   dep=dep, sem=("arbitrary", "arbitrary", "arbitrary"), split=("rows", 2))


def _attn_out_bwd(du, w_out, o, sel_t, tm, tk):
    S, D = o.shape

    def epilogue(acc, extra_refs, out_refs, j, ci):
        out_refs[0][...] = acc
        out_refs[1][...] = _exact_nn(acc * extra_refs[0][...], extra_refs[1][...])

    row = pl.BlockSpec((tm, D), lambda i, j, k: (i, 0))
    slim = pl.BlockSpec((tm, LANES), lambda i, j, k: (i, 0))
    return _matmul("attn_out_bwd", du, w_out, "nt", tm, D, tk,
                   [(_sds((S, D), F32), row, True), (_sds((S, LANES), F32), slim, True)], epilogue,
                   extras=[(o, row, True), (sel_t, pl.BlockSpec((D, LANES), lambda i, j, k: (0, 0)))], split=("rows", 2))


def _adamw(name, w, gs, m, v):
    shape = w.shape
    cols = shape[-1]
    rows = math.prod(shape[:-1])
    w2, m2, v2 = (t.reshape(rows, cols) for t in (w, m, v))
    gs2 = [g.reshape(-1, cols) for g in gs]
    ng = len(gs2)
    tr = _pick(rows // ng, (256, 128, 64, 32, 16, 8))
    per = rows // ng // tr
    c1 = 1.0 - ADAM_B1 ** ADAM_STEP
    c2 = 1.0 - ADAM_B2 ** ADAM_STEP

    def body(w_ref, m_ref, v_ref, *rest):
        g_refs, (d_ref, nm_ref, nv_ref), g_out = rest[:ng], rest[ng:ng + 3], rest[ng + 3:]
        gg = g_refs[0][...]
        if ng == 2:
            gg = jnp.where(pl.program_id(0) < per, gg, g_refs[1][...])
        g_out[0][...] = gg
        nm = ADAM_B1 * m_ref[...] + (1.0 - ADAM_B1) * gg
        nv = ADAM_B2 * v_ref[...] + (1.0 - ADAM_B2) * (gg * gg)
        nm_ref[...] = nm
        nv_ref[...] = nv
        d_ref[...] = -ADAM_LR * ((nm / c1) / (jnp.sqrt(nv / c2) + ADAM_EPS) + ADAM_WD * w_ref[...])

    blk = pl.BlockSpec((tr, cols), lambda i: (i, 0))
    g_specs = [blk] if ng == 1 else [pl.BlockSpec((tr, cols), lambda i: (jnp.minimum(i, per - 1), 0)),
                                     pl.BlockSpec((tr, cols), lambda i: (jnp.maximum(i - per, 0), 0))]
    outs = pl.pallas_call(
        body,
        name=name,
        grid=(rows // tr,),
        in_specs=[blk] * 3 + g_specs,
        out_specs=[blk] * 4,
        out_shape=[_sds((rows, cols), F32)] * 4,
        compiler_params=_cparams(("parallel",)),
    )(w2, m2, v2, *gs2)
    return tuple(o.reshape(shape) for o in outs)


HBM = pl.BlockSpec(memory_space=pl.ANY)


def _shard_slice(ref, axis, size, index):
    idx = [slice(None)] * len(ref.shape)
    idx[axis] = pl.ds(pl.multiple_of(index * size, 8), size)
    return ref.at[tuple(idx)]


def _share_halves(name, full, tr):
    R, W4 = full.shape
    W, h = W4 // 4, R // 2
    steps = [(k, t) for k in range(3) for t in range(h // tr)]

    def body(f_in, f_ref, buf, lsem, ssem, rsem):
        x, y, c = lax.axis_index("x"), lax.axis_index("y"), lax.axis_index("c")
        sibling = (x, y, 1 - c)
        chips = [(1 - x, y), (x, 1 - y), (1 - x, 1 - y)]

        def tile(k, t):
            px, py = chips[k]
            return f_ref.at[pl.ds(pl.multiple_of(c * h + t * tr, 8), tr), pl.ds(pl.multiple_of((2 * px + py) * W, LANES), W)]

        sends = []
        for s, (k, t) in enumerate(steps):
            slot = s % 2
            if s >= 2:
                sends[s - 2].wait_send()
            lc = pltpu.make_async_copy(tile(k, t), buf.at[slot], lsem.at[slot])
            lc.start()
            lc.wait()
            rc = pltpu.make_async_remote_copy(src_ref=buf.at[slot], dst_ref=tile(k, t), send_sem=ssem.at[slot], recv_sem=rsem,
                                              device_id=sibling, device_id_type=MESH)
            rc.start()
            sends.append(rc)
        for rc in sends[-2:]:
            rc.wait_send()
        whole = f_ref.at[pl.ds(0, h), pl.ds(0, 3 * W)]
        pltpu.make_async_remote_copy(src_ref=whole, dst_ref=whole, send_sem=ssem.at[0], recv_sem=rsem,
                                     device_id=sibling, device_id_type=MESH).wait_recv()

    return pl.pallas_call(
        body,
        name=name,
        in_specs=[HBM],
        out_specs=HBM,
        out_shape=_sds(full.shape, full.dtype),
        input_output_aliases={0: 0},
        scratch_shapes=[pltpu.VMEM((2, tr, W), full.dtype), pltpu.SemaphoreType.DMA((2,)), pltpu.SemaphoreType.DMA((2,)),
                        pltpu.SemaphoreType.DMA(())],
    )(full)


IN_HBM = pl.BlockSpec(memory_space=pltpu.HBM)
IN_SEM = pl.BlockSpec(memory_space=pltpu.SEMAPHORE)
DATAFLOW = pltpu.SideEffectType.DATAFLOW_SIDE_EFFECTING


def _hbm(t):
    return pltpu.with_memory_space_constraint(t, pltpu.HBM)


def _token_spec():
    return pl.BlockSpec(memory_space=pltpu.VMEM)


def _gather_copies(s_refs, f_refs, axes, halves, send, recv, loc, arrival):
    x, y, c = lax.axis_index("x"), lax.axis_index("y"), lax.axis_index("c")
    chips = [(1 - x, y), (x, 1 - y), (1 - x, 1 - y)]
    local, remote = [], []
    for a in range(len(s_refs)):
        size = s_refs[a].shape[axes[a]]
        local.append(pltpu.make_async_copy(s_refs[a], _shard_slice(f_refs[a], axes[a], size, 2 * x + y), loc.at[a]))
        for k, (px, py) in enumerate(chips):
            block = (2 * px + py) if arrival else (2 * x + y)
            src, dst = s_refs[a], _shard_slice(f_refs[a], axes[a], size, block)
            if halves:
                assert axes[a] == 1 and len(s_refs[a].shape) == 2
                h = s_refs[a].shape[0] // 2
                rows = pl.ds(pl.multiple_of(c * h, 8), h)
                src = s_refs[a].at[rows, :]
                dst = f_refs[a].at[rows, pl.ds(pl.multiple_of(block * size, LANES), size)]
            remote.append(pltpu.make_async_remote_copy(src_ref=src, dst_ref=dst, send_sem=send.at[3 * a + k],
                                                       recv_sem=recv.at[3 * a + k], device_id=(px, py, c), device_id_type=MESH))
    return local, remote


def _gather_start(name, shards, axes, after, halves=False):
    n = len(shards)
    fulls = []
    for s, ax in zip(shards, axes):
        fs = list(s.shape)
        fs[ax] *= 4
        fulls.append(lax.empty(tuple(fs), s.dtype))

    def body(*refs):
        s_refs, f_refs = refs[:n], refs[n:2 * n]
        send, recv, loc, token = refs[2 * n + 1], refs[2 * n + 2], refs[2 * n + 3], refs[-1]
        local, remote = _gather_copies(s_refs, f_refs, axes, halves, send, recv, loc, arrival=False)
        for cp in remote + local:
            cp.start()
        token[...] = jnp.zeros_like(token)

    outs = pl.pallas_call(
        body,
        name=name,
        out_shape=(pltpu.SemaphoreType.DMA((3 * n,)), pltpu.SemaphoreType.DMA((3 * n,)), pltpu.SemaphoreType.DMA((n,)),
                   *[pltpu.HBM(t.shape, t.dtype) for t in shards + fulls], _sds((8, LANES), F32)),
        in_specs=[IN_HBM] * (2 * n) + [HBM],
        out_specs=(IN_SEM, IN_SEM, IN_SEM, *[IN_HBM] * (2 * n), _token_spec()),
        input_output_aliases={i: 3 + i for i in range(2 * n)},
        compiler_params=pltpu.CompilerParams(has_side_effects=DATAFLOW),
    )(*[_hbm(t) for t in shards + fulls], after)
    return (outs[0], outs[1], outs[2], list(outs[3:3 + n]), list(outs[3 + n:3 + 2 * n]), axes, halves), outs[-1]


def _gather_wait(name, state, *after):
    send, recv, loc, s_thru, f_thru, axes, halves = state
    n = len(s_thru)

    def body(*refs):
        s_refs, f_refs = refs[:n], refs[n:2 * n]
        local, remote = _gather_copies(s_refs, f_refs, axes, halves, refs[2 * n], refs[2 * n + 1], refs[2 * n + 2], arrival=True)
        for cp in local:
            cp.wait()
        for cp in remote:
            cp.wait_send()
            cp.wait_recv()

    outs = pl.pallas_call(
        body,
        name=name,
        out_shape=tuple(pltpu.HBM(t.shape, t.dtype) for t in s_thru + f_thru),
        in_specs=[IN_HBM] * (2 * n) + [IN_SEM, IN_SEM, IN_SEM] + [HBM] * len(after),
        out_specs=tuple([IN_HBM] * (2 * n)),
        input_output_aliases={i: i for i in range(2 * n)},
        compiler_params=pltpu.CompilerParams(has_side_effects=DATAFLOW),
    )(*s_thru, *f_thru, send, recv, loc, *after)
    return list(outs[n:2 * n])


FLIPS = [(fx, fy, fc) for fx in (0, 1) for fy in (0, 1) for fc in (0, 1)][1:]


def _piece_shape(shape, axis):
    ps = list(shape)
    if axis == 0:
        ps[0] //= 8
    else:
        ps[0] //= 2
        ps[axis] //= 4
    return tuple(ps)


def _piece(ref, axis, q, c):
    shape = ref.shape
    idx = [slice(None)] * len(shape)
    if axis == 0:
        h = shape[0] // 8
        idx[0] = pl.ds(pl.multiple_of((2 * q + c) * h, 8), h)
    else:
        h, w = shape[0] // 2, shape[axis] // 4
        idx[0] = pl.ds(c * h, h)
        idx[axis] = pl.ds(pl.multiple_of(q * w, LANES if axis == len(shape) - 1 else 8), w)
    return ref.at[tuple(idx)]


def _scatter_copies(g_refs, l_refs, axes, send, recv):
    x, y, c = lax.axis_index("x"), lax.axis_index("y"), lax.axis_index("c")
    out = []
    for a in range(len(g_refs)):
        for k, (fx, fy, fc) in enumerate(FLIPS):
            tx, ty, tc = x ^ fx, y ^ fy, c ^ fc
            out.append(pltpu.make_async_remote_copy(
                src_ref=_piece(g_refs[a], axes[a], 2 * tx + ty, tc), dst_ref=l_refs[a].at[k],
                send_sem=send.at[7 * a + k], recv_sem=recv.at[7 * a + k], device_id=(tx, ty, tc), device_id_type=MESH))
    return out


def _scatter_start(name, grads, axes):
    n = len(grads)
    lands = [lax.empty((7,) + _piece_shape(g.shape, ax), g.dtype) for g, ax in zip(grads, axes)]

    def body(*refs):
        g_refs, l_refs = refs[:n], refs[n:2 * n]
        send, recv, token = refs[2 * n], refs[2 * n + 1], refs[-1]
        for cp in _scatter_copies(g_refs, l_refs, axes, send, recv):
            cp.start()
        token[...] = jnp.zeros_like(token)

    outs = pl.pallas_call(
        body,
        name=name,
        out_shape=(pltpu.SemaphoreType.DMA((7 * n,)), pltpu.SemaphoreType.DMA((7 * n,)),
                   *[pltpu.HBM(t.shape, t.dtype) for t in grads + lands], _sds((8, LANES), F32)),
        in_specs=[IN_HBM] * (2 * n),
        out_specs=(IN_SEM, IN_SEM, *[IN_HBM] * (2 * n), _token_spec()),
        input_output_aliases={i: 2 + i for i in range(2 * n)},
        compiler_params=pltpu.CompilerParams(has_side_effects=DATAFLOW),
    )(*[_hbm(t) for t in grads + lands])
    return (outs[0], outs[1], list(outs[2:2 + n]), list(outs[2 + n:2 + 2 * n]), axes), outs[-1]


def _scatter_wait(name, state, *after):
    send, recv, g_thru, l_thru, axes = state
    n = len(g_thru)

    def body(*refs):
        g_refs, l_refs = refs[:n], refs[n:2 * n]
        for cp in _scatter_copies(g_refs, l_refs, axes, refs[2 * n], refs[2 * n + 1]):
            cp.wait_send()
            cp.wait_recv()

    outs = pl.pallas_call(
        body,
        name=name,
        out_shape=tuple(pltpu.HBM(t.shape, t.dtype) for t in g_thru + l_thru),
        in_specs=[IN_HBM] * (2 * n) + [IN_SEM, IN_SEM] + [HBM] * len(after),
        out_specs=tuple([IN_HBM] * (2 * n)),
        input_output_aliases={i: i for i in range(2 * n)},
        compiler_params=pltpu.CompilerParams(has_side_effects=DATAFLOW),
    )(*g_thru, *l_thru, send, recv, *after)
    return list(outs[:n]), list(outs[n:2 * n])


def _reduce_join(name, landing, g, axis):
    R, C = _piece_shape(g.shape, axis)
    l3 = landing.reshape(7, R, C)
    tr = _pick(R, [t for t in (512, 256, 128, 64, 32, 16, 8) if t * C <= 256 * 1024])
    nsteps = R // tr

    def own_block(i):
        q, c = 2 * lax.axis_index("x") + lax.axis_index("y"), lax.axis_index("c")
        return ((2 * q + c) * nsteps + i, 0) if axis == 0 else (c * nsteps + i, q)

    def body(own_ref, l_ref, o_ref, buf, send, loc, recv):
        i = pl.program_id(0)
        x, y, c = lax.axis_index("x"), lax.axis_index("y"), lax.axis_index("c")
        sibling = (x, y, 1 - c)

        def copies(slot, step):
            dst = o_ref.at[pl.ds(pl.multiple_of(c * R + step * tr, 8), tr), :]
            return (pltpu.make_async_copy(buf.at[slot], dst, loc.at[slot]),
                    pltpu.make_async_remote_copy(src_ref=buf.at[slot], dst_ref=dst, send_sem=send.at[slot], recv_sem=recv,
                                                 device_id=sibling, device_id_type=MESH))

        @pl.when(i >= 2)
        def _():
            lc, rc = copies(i % 2, i - 2)
            lc.wait()
            rc.wait_send()

        acc = own_ref[...].astype(F32)
        for s in range(7):
            acc = acc + l_ref[s].astype(F32)
        buf[i % 2] = acc
        lc, rc = copies(i % 2, i)
        lc.start()
        rc.start()

        @pl.when(i == nsteps - 1)
        def _():
            for st in range(max(nsteps - 2, 0), nsteps):
                lc, rc = copies(st % 2, st)
                lc.wait()
                rc.wait_send()
            theirs = o_ref.at[pl.ds(pl.multiple_of((1 - c) * R, 8), R), :]
            pltpu.make_async_remote_copy(src_ref=theirs, dst_ref=theirs, send_sem=send.at[0], recv_sem=recv,
                                         device_id=sibling, device_id_type=MESH).wait_recv()

    return pl.pallas_call(
        body,
        name=name,
        grid=(nsteps,),
        in_specs=[pl.BlockSpec((tr, C), own_block), pl.BlockSpec((7, tr, C), lambda i: (0, i, 0))],
        out_specs=HBM,
        out_shape=_sds((2 * R, C), F32),
        scratch_shapes=[pltpu.VMEM((2, tr, C), F32), pltpu.SemaphoreType.DMA((2,)), pltpu.SemaphoreType.DMA((2,)),
                        pltpu.SemaphoreType.DMA(())],
        compiler_params=_cparams(("arbitrary",)),
    )(g, l3)


def _all_reduce_small(v, dep):
    R, D = v.shape

    def body(v_ref, dep_ref, o_ref, land, send, recv):
        x, y, c = lax.axis_index("x"), lax.axis_index("y"), lax.axis_index("c")
        my_slot = 4 * x + 2 * y + c
        land[my_slot] = v_ref[...]
        for k, (fx, fy, fc) in enumerate(FLIPS):
            tx, ty, tc = x ^ fx, y ^ fy, c ^ fc
            pltpu.make_async_remote_copy(src_ref=v_ref, dst_ref=land.at[my_slot], send_sem=send.at[k], recv_sem=recv.at[k],
                                         device_id=(tx, ty, tc), device_id_type=MESH).start()
        for k, (fx, fy, fc) in enumerate(FLIPS):
            tx, ty, tc = x ^ fx, y ^ fy, c ^ fc
            cp = pltpu.make_async_remote_copy(src_ref=v_ref, dst_ref=land.at[4 * tx + 2 * ty + tc], send_sem=send.at[k],
                                              recv_sem=recv.at[k], device_id=(tx, ty, tc), device_id_type=MESH)
            cp.wait_send()
            cp.wait_recv()
        acc = land[0]
        for s in range(1, 8):
            acc = acc + land[s]
        o_ref[...] = acc

    return pl.pallas_call(
        body,
        name="all_reduce_small",
        in_specs=[pl.BlockSpec(memory_space=pltpu.VMEM), pl.BlockSpec(memory_space=pl.ANY)],
        out_specs=pl.BlockSpec(memory_space=pltpu.VMEM),
        out_shape=_sds((R, D), F32),
        scratch_shapes=[pltpu.VMEM((8, R, D), F32), pltpu.SemaphoreType.DMA((7,)), pltpu.SemaphoreType.DMA((7,))],
    )(v, dep)


def kernel(x, attn_w_in, attn_w_out, hgrn_w_in, hgrn_w_out, hgrn_norm_g, lb_logits, ln_mix_g, ln_mix_b, ln_ffn_g, ln_ffn_b, ffn_w_up, ffn_w_down, loss_target, m_attn_w_in, m_attn_w_out, m_hgrn_w_in, m_hgrn_w_out, m_hgrn_norm_g, m_lb_logits, m_ln_mix_g, m_ln_mix_b, m_ln_ffn_g, m_ln_ffn_b, m_ffn_w_up, m_ffn_w_down, v_attn_w_in, v_attn_w_out, v_hgrn_w_in, v_hgrn_w_out, v_hgrn_norm_g, v_lb_logits, v_ln_mix_g, v_ln_mix_b, v_ln_ffn_g, v_ln_ffn_b, v_ffn_w_up, v_ffn_w_down):
    xs = x[0]
    tgt = loss_target[0]
    S, D = xs.shape
    F = ffn_w_up.shape[2] * 4
    T1 = _pick(S, (1024, 512, 256))
    T2 = _pick(S, (2048, 1024, 512))
    TH = _pick(S, (512, 256))
    TB = _pick(S, (128,))
    TF = _pick(F, (1024, 512))
    TG = 3 * D // 4

    cast = lambda w: w.astype(MXU_DTYPE)
    st_a, tok = _gather_start("gather_a", [cast(attn_w_in[0])], [1], jnp.zeros((8, LANES), F32), halves=True)
    tok, (xs_late, w_aout, w_fup, w_fdown, w_hin, w_hout) = lax.optimization_barrier(
        (tok, (xs, attn_w_out, ffn_w_up, ffn_w_down, hgrn_w_in, hgrn_w_out)))
    st_b, tok = _gather_start("gather_b", [cast(w_aout[0]), cast(w_fup[0]), cast(w_fdown[0])], [0, 1, 0], tok)
    st_c, tok = _gather_start("gather_c", [cast(w_hin[0]), cast(w_hout[0]), hgrn_norm_g, cast(w_fup[1]), cast(w_fdown[1])],
                              [1, 0, 1, 1, 0], tok)

    cos3, sin3 = _rope_tables(S)
    sel = _head_sel(D)
    sel_t = sel.T

    xc3 = _stack_classes("x_classes", xs_late, MXU_DTYPE)
    P3 = _attn_proj("attn_proj_own", xc3, st_a[3][0], cos3, sin3, T2, None, tok)
    (wa_in,) = _gather_wait("gather_a_wait", st_a, P3)
    wa_in = _share_halves("share_a", wa_in, _pick(D // 2, (256, 128)))
    P3 = _attn_proj("attn_proj", xc3, wa_in, cos3, sin3, T2, P3)
    o3, lse3 = _attn_fwd(P3, D)
    o_att, L_att = _attn_mix(o3, lse3, sel)
    wa_out, w_up0, w_down0 = _gather_wait("gather_b_wait", st_b, L_att)
    ln1 = (ln_mix_g[0:1], ln_mix_b[0:1])
    ln2 = (ln_ffn_g[0:1], ln_ffn_b[0:1])
    ln3 = (ln_mix_g[1:2], ln_mix_b[1:2])
    ln4 = (ln_ffn_g[1:2], ln_ffn_b[1:2])
    xm1, xh1, r1 = _mm_res_ln("attn_out_ln", o_att, wa_out, xs, *ln1, TH, D)
    a0 = _mlp_up("mlp0_up", xm1, w_up0, T2, TF, D)
    xm2, xh2, r2 = _mm_res_ln("mlp0_down_ln", a0, w_down0, (xh1, *ln1), *ln2, TH, F)

    wh_in, wh_out, norm_g, w_up1, w_down1 = _gather_wait("gather_c_wait", st_c, r2)
    P1 = _plain_mm("hgrn_proj", xm2, wh_in, "nn", F32, T1, _pick(3 * D, (1024, 768, 512)), D)
    o_h, n_h, states = _hgrn_fwd(P1, lb_logits, norm_g, TB)
    xm3, xh3, r3 = _mm_res_ln("hgrn_out_ln", n_h, wh_out, (xh2, *ln2), *ln3, TH, D)
    a1 = _mlp_up("mlp1_up", xm3, w_up1, T2, TF, D)
    _, xh4, r4 = _mm_res_ln("mlp1_down_ln", a1, w_down1, (xh3, *ln3), *ln4, TH, F)

    wgrad = lambda name, a, dy, tm, tn, tk=T1: _plain_mm(name, a, dy, "tn", MXU_DTYPE, tm, tn, tk)
    sq, du4, dum4, dg_ffn1, db_ffn1 = _loss_ln_bwd(tgt, xh4, r4, *ln4, TH)
    dh1 = _mlp_down_bwd("mlp1_down_bwd", dum4, w_down1, a1, T2, TF, D)
    g_down1 = wgrad("g_down1", a1, dum4, TF, D, S)
    g_up1 = wgrad("g_up1", xm3, dh1, D, TF, S)
    sc_1, tok = _scatter_start("scatter_1", [g_down1, g_up1], [0, 1])
    du3, dum3, dg_mix1, db_mix1 = _mm_nt_res_ln_bwd("mlp1_up_bwd", dh1, w_up1, du4, xh3, r3, ln_mix_g[1:2], TH, F, tok)
    dn = _plain_mm("hgrn_out_bwd", dum3, wh_out, "nt", F32, T1, D, D)
    g_hout = wgrad("g_hgrn_out", n_h, dum3, D, D)
    dP1, dg_norm, dlb = _hgrn_bwd(P1, o_h, states, dn, lb_logits, norm_g, TB)
    g_hin = wgrad("g_hgrn_in", xm2, dP1, D, D, S)
    d_lb_logits = _lb_logits_grad(dlb, lb_logits)
    sc_2, tok = _scatter_start("scatter_2", [g_hout, g_hin], [0, 1])

    du2, dum2, dg_ffn0, db_ffn0 = _mm_nt_res_ln_bwd("hgrn_in_bwd", dP1, wh_in, du3, xh2, r2, ln_ffn_g[0:1], TH, 3 * D, tok)
    dh0 = _mlp_down_bwd("mlp0_down_bwd", dum2, w_down0, a0, T2, TF, D)
    g_down0 = wgrad("g_down0", a0, dum2, TF, D, S)
    g_up0 = wgrad("g_up0", xm1, dh0, D, TF, S)
    sc_3, tok = _scatter_start("scatter_3", [g_down0, g_up0], [0, 1])
    du1, dum1, dg_mix0, db_mix0 = _mm_nt_res_ln_bwd("mlp0_up_bwd", dh0, w_up0, du2, xh1, r1, ln_mix_g[0:1], TH, F, tok)
    do, delta = _attn_out_bwd(dum1, wa_out, o_att, sel_t, TH, D)
    g_aout = wgrad("g_attn_out", o_att, dum1, D, D)
    sc_5, tok = _scatter_start("scatter_5", [g_aout], [0])
    dP3 = _attn_bwd(P3, _stack_classes("do_classes", do, MXU_DTYPE), _stack_classes("lse_classes", L_att, F32),
                    _stack_classes("delta_classes", delta, F32), cos3, sin3, D, tok)
    small = jnp.concatenate([d_lb_logits, dg_mix0, dg_mix1, db_mix0, db_mix1, dg_ffn0, dg_ffn1, db_ffn0, db_ffn1,
                             dg_norm, sq, jnp.zeros((4, D), F32)], axis=0)
    small = _all_reduce_small(small, dP3)
    loss = 0.5 * jnp.sum(small[11]) / D
    grp = lambda j: j // (3 * D // TG)
    g_ain = _matmul("g_attn_in", xc3, dP3, "tn", D, TG, S, [(_sds((D, 9 * D), MXU_DTYPE), _ij_spec(D, TG))], _store_epilogue,
                    a_map=lambda i, j, k: (grp(j), i),
                    b_map=lambda i, j, k: (grp(j), j % (3 * D // TG)), mnk=(D, 9 * D, S), dep=small)[0]
    sc_4, tok = _scatter_start("scatter_4", [g_ain], [1])
    dxc3 = _matmul("attn_in_bwd", dP3, wa_in, "nt", T1, D, 3 * D, [(_sds((3 * S, D), F32), _ij_spec(T1, D))], _store_epilogue,
                   b_map=lambda i, j, k: (j, k + i // (S // T1)), mnk=(3 * S, D, 3 * D), dep=tok)[0]
    grad_x = _input_grad(du1, dxc3)

    def reduced(name, state, *after):
        gs, lands = _scatter_wait(name + "_wait", state, *after)
        return [_reduce_join(f"{name}_reduce_{i}", l, g, ax) for i, (l, g, ax) in enumerate(zip(lands, gs, state[4]))]

    r_down1, r_up1 = reduced("scatter_1", sc_1, grad_x)
    r_hout, r_hin = reduced("scatter_2", sc_2, r_up1)
    r_down0, r_up0 = reduced("scatter_3", sc_3, r_hin)
    (r_aout,) = reduced("scatter_5", sc_5, r_up0)

    my_chip = 2 * lax.axis_index("x") + lax.axis_index("y")
    nsh = hgrn_norm_g.shape[1]
    g_norm = lax.dynamic_slice(small[10:11], (0, my_chip * nsh), (1, nsh))

    grads, upd = {}, {}

    def update(nm, w, gs, m, v):
        upd[nm] = _adamw("adamw_" + nm, w, gs, m, v)
        grads[nm] = upd[nm][3]

    update("hgrn_w_in", hgrn_w_in, [r_hin], m_hgrn_w_in, v_hgrn_w_in)
    update("hgrn_w_out", hgrn_w_out, [r_hout], m_hgrn_w_out, v_hgrn_w_out)
    update("ffn_w_up", ffn_w_up, [r_up0, r_up1], m_ffn_w_up, v_ffn_w_up)
    update("ffn_w_down", ffn_w_down, [r_down0, r_down1], m_ffn_w_down, v_ffn_w_down)
    update("attn_w_out", attn_w_out, [r_aout], m_attn_w_out, v_attn_w_out)
    update("hgrn_norm_g", hgrn_norm_g, [g_norm], m_hgrn_norm_g, v_hgrn_norm_g)
    cat = lambda ts: jnp.concatenate(ts, axis=0)
    small_w = cat([lb_logits, ln_mix_g, ln_mix_b, ln_ffn_g, ln_ffn_b])
    small_m = cat([m_lb_logits, m_ln_mix_g, m_ln_mix_b, m_ln_ffn_g, m_ln_ffn_b])
    small_v = cat([v_lb_logits, v_ln_mix_g, v_ln_mix_b, v_ln_ffn_g, v_ln_ffn_b])
    small_upd = _adamw("adamw_small", small_w, [small[0:10]], small_m, small_v)
    for i, nm in enumerate(["lb_logits", "ln_mix_g", "ln_mix_b", "ln_ffn_g", "ln_ffn_b"]):
        grads[nm] = small[2 * i:2 * i + 2]
        upd[nm] = tuple(t[2 * i:2 * i + 2] for t in small_upd)
    done = [upd[k][2] for k in ("hgrn_w_in", "hgrn_w_out", "ffn_w_up", "ffn_w_down", "attn_w_out", "hgrn_norm_g")]
    (r_ain,) = reduced("scatter_4", sc_4, small_upd[2], *done)
    update("attn_w_in", attn_w_in, [r_ain], m_attn_w_in, v_attn_w_in)

    order = ["attn_w_in", "attn_w_out", "hgrn_w_in", "hgrn_w_out", "hgrn_norm_g", "lb_logits", "ln_mix_g", "ln_mix_b",
             "ln_ffn_g", "ln_ffn_b", "ffn_w_up", "ffn_w_down"]
    return (loss, grad_x[None], *[grads[k] for k in order], *[upd[k][0] for k in order],
            *[upd[k][1] for k in order], *[upd[k][2] for k in order])
```

```python
import math

import jax
import jax.numpy as jnp
from jax import lax
from jax.experimental import pallas as pl
from jax.experimental.pallas import tpu as pltpu

F32 = jnp.float32
BF16 = jnp.bfloat16
MXU_DTYPE = BF16

HEAD_DIM = 64
ATTN_BLK = 128
DILATIONS = (1, 4, 16)
ROPE_THETA = 10000.0
HGRN_DK = 128
HGRN_CHUNK = 64
DEPTH = 2
LN_EPS = 1e-5
RMS_EPS = 1e-6
ALPHA = (2 * DEPTH) ** 0.25
ADAM_LR, ADAM_B1, ADAM_B2, ADAM_EPS, ADAM_WD, ADAM_STEP = 0.001, 0.9, 0.999, 1e-08, 0.01, 10

LANES = 128
VMEM_LIMIT = 56 * 1024 * 1024
NEG = -1e30
MESH = pl.DeviceIdType.MESH


def _cparams(sem=None):
    return pltpu.CompilerParams(dimension_semantics=sem, vmem_limit_bytes=VMEM_LIMIT)


def _sds(shape, dtype):
    return jax.ShapeDtypeStruct(tuple(shape), dtype)


def _dg(a, b, ca, cb):
    return lax.dot_general(a, b, (((ca,), (cb,)), ((), ())), preferred_element_type=F32)


def _nn(a, b):
    return _dg(a, b, 1, 0)


def _nt(a, b):
    return _dg(a, b, 1, 1)


def _tn(a, b):
    return _dg(a, b, 0, 0)


def _split3(a):
    hi = a.astype(BF16)
    r = a - hi.astype(F32)
    mid = r.astype(BF16)
    lo = (r - mid.astype(F32)).astype(BF16)
    return hi, mid, lo


def _exact_nn(a, sel):
    hi, mid, lo = _split3(a)
    return _nn(hi, sel) + _nn(mid, sel) + _nn(lo, sel)


def _pick(n, prefs):
    for p in prefs:
        if n % p == 0:
            return p
    return n


def _matmul(name, a, b, form, tm, tn, tk, outs, epilogue, extras=(), a_map=None, b_map=None, mnk=None, dep=None,
            sem=("parallel", "parallel", "arbitrary"), split=None, alias_dep=False):
    if form == "nn":
        (M, K), N = a.shape, b.shape[1]
        a_spec = pl.BlockSpec((tm, tk), a_map or (lambda i, j, k: (i, k)))
        b_spec = pl.BlockSpec((tk, tn), b_map or (lambda i, j, k: (k, j)))
        ca, cb = 1, 0
    elif form == "nt":
        (M, K), N = a.shape, b.shape[0]
        a_spec = pl.BlockSpec((tm, tk), a_map or (lambda i, j, k: (i, k)))
        b_spec = pl.BlockSpec((tn, tk), b_map or (lambda i, j, k: (j, k)))
        ca, cb = 1, 1
    else:
        (K, M), N = a.shape, b.shape[1]
        a_spec = pl.BlockSpec((tk, tm), a_map or (lambda i, j, k: (k, i)))
        b_spec = pl.BlockSpec((tk, tn), b_map or (lambda i, j, k: (k, j)))
        ca, cb = 0, 0
    if mnk is not None:
        M, N, K = mnk
    assert M % tm == 0 and N % tn == 0 and K % tk == 0, (name, M, N, K, tm, tn, tk)
    nk = K // tk
    ne, no = len(extras), len(outs)
    deps = [] if dep is None else [dep]
    nd = len(deps)

    def body(a_ref, b_ref, *rest):
        extra_refs, out_refs = rest[:ne], rest[ne + nd:ne + nd + no]
        j = pl.program_id(1)
        if split is not None:
            kind, n = split
            assert nk == 1 and form != "tn"
            tiled = [t for _, _, *t in list(extras) + list(outs)]
            refs = list(extra_refs) + list(out_refs)
            for ci in range(n):
                if kind == "cols":
                    cs = slice(ci * (tn // n), (ci + 1) * (tn // n))
                    part = _dg(a_ref[...].astype(MXU_DTYPE), (b_ref[:, cs] if form == "nn" else b_ref[cs, :]).astype(MXU_DTYPE), ca, cb)
                    view = [r.at[:, cs] if t else r for r, t in zip(refs, tiled)]
                else:
                    rs = slice(ci * (tm // n), (ci + 1) * (tm // n))
                    part = _dg(a_ref[rs, :].astype(MXU_DTYPE), b_ref[...].astype(MXU_DTYPE), ca, cb)
                    view = [r.at[rs, :] if t else r for r, t in zip(refs, tiled)]
                epilogue(part, view[:ne], view[ne:], j, ci)
            return
        part = _dg(a_ref[...].astype(MXU_DTYPE), b_ref[...].astype(MXU_DTYPE), ca, cb)
        if nk == 1:
            epilogue(part, extra_refs, out_refs, j, 0)
            return
        acc_ref = rest[-1]
        k = pl.program_id(2)

        @pl.when(k == 0)
        def _():
            acc_ref[...] = part

        @pl.when(k > 0)
        def _():
            acc_ref[...] += part

        @pl.when(k == nk - 1)
        def _():
            epilogue(acc_ref[...], extra_refs, out_refs, j, 0)

    res = pl.pallas_call(
        body,
        name=name,
        grid=(M // tm, N // tn, nk),
        in_specs=[a_spec, b_spec] + [s for _, s, *_ in extras] + [pl.BlockSpec(memory_space=pl.ANY)] * nd,
        out_specs=[s for _, s, *_ in outs],
        out_shape=[o for o, *_ in outs],
        scratch_shapes=[pltpu.VMEM((tm, tn), F32)] if nk > 1 else [],
        input_output_aliases={2 + ne: 0} if alias_dep else {},
        compiler_params=_cparams(sem),
    )(a, b, *[e for e, *_ in extras], *deps)
    return res


def _ij_spec(tm, tn):
    return pl.BlockSpec((tm, tn), lambda i, j, k: (i, j))


def _store_epilogue(acc, extra_refs, out_refs, j, ci):
    out_refs[0][...] = acc.astype(out_refs[0].dtype)


def _plain_mm(name, a, b, form, out_dtype, tm, tn, tk):
    M = a.shape[1] if form == "tn" else a.shape[0]
    N = b.shape[0] if form == "nt" else b.shape[1]
    return _matmul(name, a, b, form, tm, tn, tk, [(_sds((M, N), out_dtype), _ij_spec(tm, tn))], _store_epilogue)[0]


def _class_slabs(S):
    assert DILATIONS[0] == 1
    return [(g, d, r, S // d) for g, d in enumerate(DILATIONS) if d > 1 for r in range(d)]


def _stack_classes(name, t, out_dtype):
    S, W = t.shape

    def body(x_ref, o_ref):
        o_ref[0:S, :] = x_ref[...].astype(out_dtype)
        for g, d, r, n in _class_slabs(S):
            o_ref[g * S + r * n:g * S + (r + 1) * n, :] = x_ref[pl.ds(r, n, stride=d), :].astype(out_dtype)

    return pl.pallas_call(
        body,
        name=name,
        grid=(W // LANES,),
        in_specs=[pl.BlockSpec((S, LANES), lambda j: (0, j))],
        out_specs=pl.BlockSpec((3 * S, LANES), lambda j: (0, j)),
        out_shape=_sds((3 * S, W), out_dtype),
        compiler_params=_cparams(("parallel",)),
    )(t)


def _rope_tables(seq):
    half = HEAD_DIM // 2
    inv = ROPE_THETA ** (-jnp.arange(half, dtype=F32) * (2.0 / HEAD_DIM))
    inv = jnp.tile(inv, LANES // half)
    pos = []
    for d in DILATIONS:
        row = jnp.arange(seq)
        pos.append((row % (seq // d)) * d + row // (seq // d))
    ang = jnp.concatenate(pos).astype(F32)[:, None] * inv[None, :]
    first = (jnp.arange(LANES) % HEAD_DIM) < half
    sin = jnp.sin(ang)
    return jnp.cos(ang), jnp.where(first[None, :], -sin, sin)


def _partner(x):
    half = HEAD_DIM // 2
    lane = lax.broadcasted_iota(jnp.int32, x.shape, 1)
    first = (lane % HEAD_DIM) < half
    return jnp.where(first, pltpu.roll(x, LANES - half, 1), pltpu.roll(x, half, 1))


def _attn_proj(name, x3, w, cos3, sin3, tm, prev, dep=None):
    S3, D = x3.shape
    S = S3 // 3
    tn = 3 * D // 4
    nrow = S // tm
    local = prev is None

    def tile(j):
        q = 2 * lax.axis_index("x") + lax.axis_index("y")
        c0 = 3 * q + j if local else j + 3 * (j >= 3 * q).astype(jnp.int32)
        return c0, c0 // 4, c0 % 4

    def epilogue(acc, extra_refs, out_refs, j, ci):
        cos_ref, sin_ref = extra_refs
        o_ref = out_refs[0]
        _, _, place = tile(j)
        width = acc.shape[1]
        assert D % width == 0
        is_rot = (place * tn + ci * width) // D < 2
        c = jnp.where(is_rot, cos_ref[...], 1.0)
        s = jnp.where(is_rot, sin_ref[...], 0.0)
        for t in range(width // LANES):
            xs = acc[:, t * LANES:(t + 1) * LANES]
            o_ref[:, t * LANES:(t + 1) * LANES] = (xs * c + _partner(xs) * s).astype(o_ref.dtype)

    rows = lambda i, j: tile(j)[1] * nrow + i
    tab = pl.BlockSpec((tm, LANES), lambda i, j, k: (rows(i, j), 0))
    out = pl.BlockSpec((tm, tn), lambda i, j, k: (rows(i, j), tile(j)[2]))
    ntiles = 3 if local else 9
    return _matmul(name, x3, w, "nn", tm, tn, D, [(_sds((S3, 3 * D), MXU_DTYPE), out, True)], epilogue,
                   extras=[(cos3, tab), (sin3, tab)], a_map=lambda i, j, k: (rows(i, j), k),
                   b_map=lambda i, j, k: (k, j if local else tile(j)[0]), mnk=(nrow * tm, ntiles * tn, D),
                   dep=dep if local else prev, alias_dep=not local, split=("cols", 3))[0]


def _head_sel(d_model):
    h = jnp.arange(LANES)[:, None]
    l = jnp.arange(d_model)[None, :]
    return (l // HEAD_DIM == h).astype(BF16)


def _class_edges(b, nblk):
    g = b // nblk
    per_class = jnp.where(g == 0, nblk // DILATIONS[0], jnp.where(g == 1, nblk // DILATIONS[1], nblk // DILATIONS[2]))
    pos = (b % nblk) % per_class
    return pos != 0, pos != per_class - 1


def _two_heads(t, top):
    zero = jnp.zeros_like(t)
    return jnp.concatenate([jnp.where(top, t, zero), jnp.where(top, zero, t)], axis=0)


def _band_mask(has_prev):
    B = ATTN_BLK
    row = lax.broadcasted_iota(jnp.int32, (2 * B, 2 * B), 0) % B
    col = lax.broadcasted_iota(jnp.int32, (2 * B, 2 * B), 1)
    in_prev = jnp.logical_and(jnp.logical_and(col < B, col >= row), has_prev)
    in_own = jnp.logical_and(col >= B, col - B <= row)
    return jnp.logical_or(in_prev, in_own)


def _attn_fwd(P3, D):
    S3 = P3.shape[0]
    B = ATTN_BLK
    nblk = S3 // 3 // B
    npairs = D // LANES
    scale = HEAD_DIM ** -0.5

    def body(q_ref, kc_ref, vc_ref, kp_ref, vp_ref, o_ref, lse_ref):
        has_prev, _ = _class_edges(pl.program_id(0), nblk)
        bias = jnp.where(_band_mask(has_prev), 0.0, NEG)
        lane = lax.broadcasted_iota(jnp.int32, (B, LANES), 1)
        top = lane < HEAD_DIM
        lse_acc = jnp.zeros((B, LANES), F32)
        for j in range(npairs):
            sl = slice(j * LANES, (j + 1) * LANES)
            Q = _two_heads(q_ref[:, sl] * scale, top)
            K2 = jnp.concatenate([kp_ref[:, sl], kc_ref[:, sl]], axis=0)
            V2 = jnp.concatenate([vp_ref[:, sl], vc_ref[:, sl]], axis=0)
            s = _nt(Q, K2) + bias
            m = jnp.max(s, axis=1, keepdims=True)
            p = jnp.exp(s - m)
            l = jnp.sum(p, axis=1, keepdims=True)
            o = _nn(p.astype(MXU_DTYPE), V2) * (1.0 / l)
            o_ref[:, sl] = jnp.where(top, o[:B], o[B:])
            lse = m + jnp.log(l)
            lse_acc = jnp.where(lane == 2 * j, lse[:B], jnp.where(lane == 2 * j + 1, lse[B:], lse_acc))
        lse_ref[...] = lse_acc

    blk = lambda part, prev: pl.BlockSpec(
        (B, D), (lambda b: (jnp.maximum(b - 1, 0), part)) if prev else (lambda b: (b, part)))
    return pl.pallas_call(
        body,
        name="attn_fwd",
        grid=(3 * nblk,),
        in_specs=[blk(0, False), blk(1, False), blk(2, False), blk(1, True), blk(2, True)],
        out_specs=[pl.BlockSpec((B, D), lambda b: (b, 0)), pl.BlockSpec((B, LANES), lambda b: (b, 0))],
        out_shape=[_sds((S3, D), F32), _sds((S3, LANES), F32)],
        compiler_params=_cparams(("parallel",)),
    )(P3, P3, P3, P3, P3)


def _attn_mix(o3, lse3, sel):
    S3, D = o3.shape
    S = S3 // 3

    def body(o3_ref, lse_ref, sel_ref, o_ref, L_ref, w_ref):
        @pl.when(pl.program_id(0) == 0)
        def _():
            w_ref[0] = lse_ref[0:S, :]
            for g, d, r, n in _class_slabs(S):
                w_ref[g, pl.ds(r, n, stride=d), :] = lse_ref[g * S + r * n:g * S + (r + 1) * n, :]
            a, b, c = w_ref[0], w_ref[1], w_ref[2]
            m = jnp.maximum(jnp.maximum(a, b), c)
            L = m + jnp.log(jnp.exp(a - m) + jnp.exp(b - m) + jnp.exp(c - m))
            L_ref[...] = L
            w_ref[0] = jnp.exp(a - L)
            w_ref[1] = jnp.exp(b - L)
            w_ref[2] = jnp.exp(c - L)

        s = sel_ref[...]
        o_ref[...] = _exact_nn(w_ref[0], s) * o3_ref[0:S, :]
        for g, d, r, n in _class_slabs(S):
            rows = pl.ds(r, n, stride=d)
            o_ref[rows, :] += _exact_nn(w_ref[g, rows, :], s) * o3_ref[g * S + r * n:g * S + (r + 1) * n, :]

    return pl.pallas_call(
        body,
        name="attn_mix",
        grid=(D // LANES,),
        in_specs=[pl.BlockSpec((S3, LANES), lambda j: (0, j)), pl.BlockSpec((S3, LANES), lambda j: (0, 0)),
                  pl.BlockSpec((LANES, LANES), lambda j: (0, j))],
        out_specs=[pl.BlockSpec((S, LANES), lambda j: (0, j)), pl.BlockSpec((S, LANES), lambda j: (0, 0))],
        out_shape=[_sds((S, D), F32), _sds((S, LANES), F32)],
        scratch_shapes=[pltpu.VMEM((3, S, LANES), F32)],
        compiler_params=_cparams(("arbitrary",)),
    )(o3, lse3, sel)


def _attn_bwd(P3, do3, L3, delta3, cos3, sin3, D, dep):
    S3 = P3.shape[0]
    B = ATTN_BLK
    nblk = S3 // 3 // B
    npairs = D // LANES
    scale = HEAD_DIM ** -0.5

    def body(c_ref, kp_ref, vp_ref, qn_ref, doc_ref, don_ref, Lc_ref, Ln_ref, dc_ref, dn_ref, cos_ref, sin_ref, dep_ref, out_ref):
        has_prev, has_next = _class_edges(pl.program_id(0), nblk)
        bias = jnp.where(_band_mask(has_prev), 0.0, NEG)
        row = lax.broadcasted_iota(jnp.int32, (2 * B, B), 0) % B
        col = lax.broadcasted_iota(jnp.int32, (2 * B, B), 1)
        bias_n = jnp.where(jnp.logical_and(col >= row, has_next), 0.0, NEG)
        lane = lax.broadcasted_iota(jnp.int32, (B, LANES), 1)
        top = lane < HEAD_DIM
        cos_t = cos_ref[...]
        sin_inv = -sin_ref[...]
        Lc_all, Ln_all, dc_all, dn_all = Lc_ref[...], Ln_ref[...], dc_ref[...], dn_ref[...]
        pair_col = lambda t, j: jnp.concatenate([t[:, 2 * j:2 * j + 1], t[:, 2 * j + 1:2 * j + 2]], axis=0)
        for j in range(npairs):
            sl = lambda part: slice(part * D + j * LANES, part * D + (j + 1) * LANES)
            pj = slice(j * LANES, (j + 1) * LANES)
            kc2, vc2 = c_ref[:, sl(1)], c_ref[:, sl(2)]
            K2 = jnp.concatenate([kp_ref[:, pj], kc2], axis=0)
            V2 = jnp.concatenate([vp_ref[:, pj], vc2], axis=0)
            Qc = _two_heads(c_ref[:, sl(0)] * scale, top)
            Qn = _two_heads(qn_ref[:, pj] * scale, top)
            DOc = _two_heads(doc_ref[:, pj].astype(MXU_DTYPE), top)
            DOn = _two_heads(don_ref[:, pj].astype(MXU_DTYPE), top)
            P_c = jnp.exp(_nt(Qc, K2) + bias - pair_col(Lc_all, j))
            dS_c = P_c * (_nt(DOc, V2) - pair_col(dc_all, j))
            P_n = jnp.exp(_nt(Qn, kc2) + bias_n - pair_col(Ln_all, j))
            dS_n = P_n * (_nt(DOn, vc2) - pair_col(dn_all, j))
            dq = _nn(dS_c.astype(MXU_DTYPE), K2)
            dq2 = jnp.where(top, dq[:B], dq[B:]) * scale
            Qk = jnp.concatenate([Qc, Qn], axis=0)
            DOk = jnp.concatenate([DOc, DOn], axis=0)
            dk2 = _tn(jnp.concatenate([dS_c[:, B:], dS_n], axis=0).astype(MXU_DTYPE), Qk)
            dv2 = _tn(jnp.concatenate([P_c[:, B:], P_n], axis=0).astype(MXU_DTYPE), DOk)
            out_ref[:, sl(0)] = (dq2 * cos_t + _partner(dq2) * sin_inv).astype(out_ref.dtype)
            out_ref[:, sl(1)] = (dk2 * cos_t + _partner(dk2) * sin_inv).astype(out_ref.dtype)
            out_ref[:, sl(2)] = dv2.astype(out_ref.dtype)

    cur = lambda b: b
    prv = lambda b: jnp.maximum(b - 1, 0)
    nxt = lambda b: jnp.minimum(b + 1, 3 * nblk - 1)
    spec = lambda w, f, part=0: pl.BlockSpec((B, w), lambda b: (f(b), part))
    return pl.pallas_call(
        body,
        name="attn_bwd",
        grid=(3 * nblk,),
        in_specs=[spec(3 * D, cur), spec(D, prv, 1), spec(D, prv, 2), spec(D, nxt, 0), spec(D, cur), spec(D, nxt),
                  spec(LANES, cur), spec(LANES, nxt), spec(LANES, cur), spec(LANES, nxt), spec(LANES, cur), spec(LANES, cur),
                  pl.BlockSpec(memory_space=pl.ANY)],
        out_specs=spec(3 * D, cur),
        out_shape=_sds((S3, 3 * D), MXU_DTYPE),
        compiler_params=_cparams(("parallel",)),
    )(P3, P3, P3, P3, do3, do3, L3, L3, delta3, delta3, cos3, sin3, dep)


def _input_grad(du, dx3):
    S, D = du.shape

    def body(du_ref, dx_ref, o_ref):
        o_ref[...] = ALPHA * du_ref[...] + dx_ref[0:S, :]
        for g, d, r, n in _class_slabs(S):
            o_ref[pl.ds(r, n, stride=d), :] += dx_ref[g * S + r * n:g * S + (r + 1) * n, :]

    return pl.pallas_call(
        body,
        name="input_grad",
        grid=(D // LANES,),
        in_specs=[pl.BlockSpec((S, LANES), lambda j: (0, j)), pl.BlockSpec((3 * S, LANES), lambda j: (0, j))],
        out_specs=pl.BlockSpec((S, LANES), lambda j: (0, j)),
        out_shape=_sds((S, D), F32),
        compiler_params=_cparams(("parallel",)),
    )(du, dx3)


def _chunk_causal(tb):
    r = lax.broadcasted_iota(jnp.int32, (tb, tb), 0)
    c = lax.broadcasted_iota(jnp.int32, (tb, tb), 1)
    return jnp.logical_and((r // HGRN_CHUNK) == (c // HGRN_CHUNK), r >= c)


def _chunk_sums(a, lower):
    C = HGRN_CHUNK
    r = lax.broadcasted_iota(jnp.int32, (C, C), 0)
    c = lax.broadcasted_iota(jnp.int32, (C, C), 1)
    tri = ((r >= c) if lower else (r <= c)).astype(BF16)
    parts = _split3(a)
    out = []
    for ci in range(a.shape[0] // C):
        rows = slice(ci * C, (ci + 1) * C)
        out.append(_nn(tri, parts[0][rows]) + _nn(tri, parts[1][rows]) + _nn(tri, parts[2][rows]))
    return jnp.concatenate(out, axis=0)


def _chunk_last(b):
    C = HGRN_CHUNK
    return jnp.concatenate([jnp.broadcast_to(b[(ci + 1) * C - 1:(ci + 1) * C, :], (C, b.shape[1]))
                            for ci in range(b.shape[0] // C)], axis=0)


def _lower_bound(lb_ref):
    l0, l1 = lb_ref[0:1, :], lb_ref[1:2, :]
    m = jnp.maximum(l0, l1)
    e0, e1 = jnp.exp(l0 - m), jnp.exp(l1 - m)
    return e1 / (e0 + e1)


def _hgrn_gates(q_raw, z, lb):
    sg = 1.0 / (1.0 + jnp.exp(-z))
    sn = 1.0 / (1.0 + jnp.exp(z))
    f = lb + (1.0 - lb) * sg
    key = (1.0 - lb) * sn
    sq = 1.0 / (1.0 + jnp.exp(-q_raw))
    return sg, sn, f, key, sq


def _hgrn_fwd(P1, lb_logits, norm_g, tb):
    S = P1.shape[0]
    D = P1.shape[1] // 3
    K = HGRN_DK
    H = D // K
    HP = H
    C = HGRN_CHUNK
    cpb = tb // C
    nt = S // tb

    def body(q_ref, f_ref, i_ref, lb_ref, g_ref, o_ref, n_ref, st_ref, state):
        t = pl.program_id(1)

        @pl.when(t == 0)
        def _():
            state[...] = jnp.zeros_like(state)

        lb_all = _lower_bound(lb_ref)
        low = _chunk_causal(tb)
        for hh in range(HP):
            lanes = slice(hh * K, (hh + 1) * K)
            q_raw, z, v = q_ref[:, lanes], f_ref[:, lanes], i_ref[:, lanes]
            sg, sn, f, key, sq = _hgrn_gates(q_raw, z, lb_all[:, lanes])
            b = _chunk_sums(jnp.log(f), lower=True)
            qd = (q_raw * sq * jnp.exp(b)).astype(MXU_DTYPE)
            kd = (key * jnp.exp(-b)).astype(MXU_DTYPE)
            kb = (key * jnp.exp(_chunk_last(b) - b)).astype(MXU_DTYPE)
            vm = v.astype(MXU_DTYPE)
            a = jnp.where(low, _nt(qd, kd), 0.0).astype(MXU_DTYPE)
            o_intra = _nn(a, vm)
            st = state[hh]
            outs = []
            for ci in range(cpb):
                rows = slice(ci * C, (ci + 1) * C)
                st_ref[hh, ci] = st
                outs.append(o_intra[rows] + _nt(qd[rows], st.astype(MXU_DTYPE)))
                st = st * jnp.exp(b[(ci + 1) * C - 1:(ci + 1) * C, :]) + _tn(vm[rows], kb[rows])
            state[hh] = st
            o = jnp.concatenate(outs, axis=0)
            o_ref[:, lanes] = o
            rs = lax.rsqrt(jnp.mean(o * o, axis=1, keepdims=True) + RMS_EPS)
            n_ref[:, lanes] = (o * rs * g_ref[:, lanes]).astype(n_ref.dtype)

    tok = lambda part: pl.BlockSpec((tb, HP * K), lambda h, t: (t, part * (H // HP) + h))
    vec = lambda rows: pl.BlockSpec((rows, HP * K), lambda h, t: (0, h))
    return pl.pallas_call(
        body,
        name="hgrn_fwd",
        grid=(H // HP, nt),
        in_specs=[tok(0), tok(1), tok(2), vec(2), vec(1)],
        out_specs=[tok(0), tok(0), pl.BlockSpec((HP, cpb, K, K), lambda h, t: (h, t, 0, 0))],
        out_shape=[_sds((S, D), F32), _sds((S, D), MXU_DTYPE), _sds((H, S // C, K, K), F32)],
        scratch_shapes=[pltpu.VMEM((HP, K, K), F32)],
        compiler_params=_cparams(("parallel", "arbitrary")),
    )(P1, P1, P1, lb_logits, norm_g)


def _hgrn_bwd(P1, o_pre, states, dn, lb_logits, norm_g, tb):
    S = P1.shape[0]
    D = P1.shape[1] // 3
    K = HGRN_DK
    H = D // K
    HP = H
    C = HGRN_CHUNK
    cpb = tb // C
    nt = S // tb

    def body(q_ref, f_ref, i_ref, o_ref, st_ref, dn_ref, lb_ref, g_ref, d_ref, dg_ref, dlb_ref, dstate):
        t = pl.program_id(1)

        @pl.when(t == 0)
        def _():
            dstate[...] = jnp.zeros_like(dstate)
            dg_ref[...] = jnp.zeros_like(dg_ref)
            dlb_ref[...] = jnp.zeros_like(dlb_ref)

        lb_all = _lower_bound(lb_ref)
        low = _chunk_causal(tb)
        for hh in range(HP):
            lanes = slice(hh * K, (hh + 1) * K)
            lb = lb_all[:, lanes]
            gn = g_ref[:, lanes]
            q_raw, z, v = q_ref[:, lanes], f_ref[:, lanes], i_ref[:, lanes]
            sg, sn, f, key, sq = _hgrn_gates(q_raw, z, lb)
            b = _chunk_sums(jnp.log(f), lower=True)
            e_pos, e_neg, e_rel = jnp.exp(b), jnp.exp(-b), jnp.exp(_chunk_last(b) - b)
            qd_f, kd_f, kb_f = q_raw * sq * e_pos, key * e_neg, key * e_rel
            qd, kd, kb = qd_f.astype(MXU_DTYPE), kd_f.astype(MXU_DTYPE), kb_f.astype(MXU_DTYPE)
            vm = v.astype(MXU_DTYPE)
            a = jnp.where(low, _nt(qd, kd), 0.0).astype(MXU_DTYPE)
            o = o_ref[:, lanes]
            dnn = dn_ref[:, lanes]
            rs = lax.rsqrt(jnp.mean(o * o, axis=1, keepdims=True) + RMS_EPS)
            dg_ref[:, lanes] += jnp.sum(dnn * o * rs, axis=0, keepdims=True)
            tg = dnn * gn
            dom = (rs * tg - o * (rs * rs * rs) * jnp.mean(tg * o, axis=1, keepdims=True)).astype(MXU_DTYPE)
            da = jnp.where(low, _nt(dom, vm), 0.0).astype(MXU_DTYPE)
            dv = _tn(a, dom)
            dqd = _nn(da, kd)
            dkd = _tn(da, qd)
            dst = dstate[hh]
            dv_s, dqd_s, dkb_s, dbl_s = [None] * cpb, [None] * cpb, [None] * cpb, [None] * cpb
            for ci in reversed(range(cpb)):
                rows = slice(ci * C, (ci + 1) * C)
                st = st_ref[hh, ci]
                dstm = dst.astype(MXU_DTYPE)
                dec = jnp.exp(b[(ci + 1) * C - 1:(ci + 1) * C, :])
                dv_s[ci] = _nt(kb[rows], dstm)
                dkb_s[ci] = _nn(vm[rows], dstm)
                dqd_s[ci] = _nn(dom[rows], st.astype(MXU_DTYPE))
                db_last = jnp.sum(dkb_s[ci] * kb_f[rows], axis=0, keepdims=True) + jnp.sum(dst * st, axis=0, keepdims=True) * dec
                dbl_s[ci] = jnp.broadcast_to(db_last, (C, K))
                dst = dst * dec + _tn(dom[rows], qd[rows])
            dstate[hh] = dst
            dv = dv + jnp.concatenate(dv_s, axis=0)
            dqd = dqd + jnp.concatenate(dqd_s, axis=0)
            dkb = jnp.concatenate(dkb_s, axis=0)
            dkey = dkd * e_neg + dkb * e_rel
            db = dqd * qd_f - dkd * kd_f - dkb * kb_f
            dlogf = _chunk_sums(db, lower=False) + jnp.concatenate(dbl_s, axis=0)
            gz = (1.0 - lb) * sg * sn
            col = lambda part: slice(part * D + hh * K, part * D + (hh + 1) * K)
            d_ref[:, col(0)] = (dqd * e_pos * (sq + q_raw * sq * (1.0 - sq))).astype(d_ref.dtype)
            d_ref[:, col(1)] = (dlogf * gz / f - dkey * gz).astype(d_ref.dtype)
            d_ref[:, col(2)] = dv.astype(d_ref.dtype)
            dlb_ref[:, lanes] += jnp.sum(dlogf * sn / f - dkey * sn, axis=0, keepdims=True)

    rev = lambda t: nt - 1 - t
    tok = lambda part: pl.BlockSpec((tb, HP * K), lambda h, t: (rev(t), part * (H // HP) + h))
    vec = lambda rows: pl.BlockSpec((rows, HP * K), lambda h, t: (0, h))
    outs = pl.pallas_call(
        body,
        name="hgrn_bwd",
        grid=(H // HP, nt),
        in_specs=[tok(0), tok(1), tok(2), tok(0),
                  pl.BlockSpec((HP, cpb, K, K), lambda h, t: (h, rev(t), 0, 0)),
                  tok(0), vec(2), vec(1)],
        out_specs=[pl.BlockSpec((tb, 3 * D), lambda h, t: (rev(t), 0)), vec(1), vec(1)],
        out_shape=[_sds((S, 3 * D), MXU_DTYPE)] + [_sds((1, D), F32)] * 2,
        scratch_shapes=[pltpu.VMEM((HP, K, K), F32)],
        compiler_params=_cparams(("parallel", "arbitrary")),
    )(P1, P1, P1, o_pre, states, dn, lb_logits, norm_g)
    return outs


def _lb_logits_grad(dlb, lb_logits):
    def body(d_ref, l_ref, o_ref):
        s1 = _lower_bound(l_ref)
        d = d_ref[...]
        o_ref[0:1, :] = -(1.0 - s1) * s1 * d
        o_ref[1:2, :] = s1 * (1.0 - s1) * d

    return pl.pallas_call(body, name="lb_logits_grad", out_shape=_sds(lb_logits.shape, F32))(dlb, lb_logits)


def _mm_res_ln(name, a, w_full, res, g, b, tm, tk):
    from_ln = isinstance(res, tuple)
    S, D = (res[0] if from_ln else res).shape

    def epilogue(acc, extra_refs, out_refs, j, ci):
        g_ref, b_ref = extra_refs[:2]
        xm_ref, xhat_ref, rstd_ref = out_refs
        r = extra_refs[2][...] * extra_refs[3][...] + extra_refs[4][...] if from_ln else extra_refs[2][...]
        u = ALPHA * r + acc
        mu = jnp.mean(u, axis=1, keepdims=True)
        cen = u - mu
        rstd = lax.rsqrt(jnp.mean(cen * cen, axis=1, keepdims=True) + LN_EPS)
        xhat = cen * rstd
        xhat_ref[...] = xhat
        xm_ref[...] = (xhat * g_ref[...] + b_ref[...]).astype(xm_ref.dtype)
        rstd_ref[...] = rstd

    row = pl.BlockSpec((tm, D), lambda i, j, k: (i, 0))
    vec = pl.BlockSpec((1, D), lambda i, j, k: (0, 0))
    outs = [(_sds((S, D), MXU_DTYPE), row, True), (_sds((S, D), F32), row, True),
            (_sds((S, 1), F32), pl.BlockSpec((tm, 1), lambda i, j, k: (i, 0)), True)]
    res_extras = [(res[0], row, True), (res[1], vec), (res[2], vec)] if from_ln else [(res, row, True)]
    return _matmul(name, a, w_full, "nn", tm, D, tk, outs, epilogue, extras=[(g, vec), (b, vec)] + res_extras,
                   split=("rows", 2) if tk == a.shape[1] else None)


def _ln_bwd_rows(dy, xh, rstd, g, first, du_ref, dum_ref, dg_ref, db_ref):
    if first is not None:
        @pl.when(first)
        def _():
            dg_ref[...] = jnp.zeros_like(dg_ref)
            db_ref[...] = jnp.zeros_like(db_ref)

    dg_ref[...] += jnp.sum(dy * xh, axis=0, keepdims=True)
    db_ref[...] += jnp.sum(dy, axis=0, keepdims=True)
    dxh = dy * g
    m1 = jnp.mean(dxh, axis=1, keepdims=True)
    m2 = jnp.mean(dxh * xh, axis=1, keepdims=True)
    du = rstd * (dxh - m1 - xh * m2)
    du_ref[...] = du
    dum_ref[...] = du.astype(dum_ref.dtype)


def _loss_ln_bwd(target, xhat, rstd, g, b, tm):
    S, D = xhat.shape

    def body(t_ref, xh_ref, r_ref, g_ref, b_ref, sq_ref, du_ref, dum_ref, dg_ref, db_ref):
        first = pl.program_id(0) == 0

        @pl.when(first)
        def _():
            sq_ref[...] = jnp.zeros_like(sq_ref)

        xh = xh_ref[...]
        e = xh * g_ref[...] + b_ref[...] - t_ref[...]
        sq_ref[...] += jnp.sum(e * e, axis=0, keepdims=True)
        _ln_bwd_rows(e / D, xh, r_ref[...], g_ref[...], first, du_ref, dum_ref, dg_ref, db_ref)

    row = pl.BlockSpec((tm, D), lambda i: (i, 0))
    vec = pl.BlockSpec((1, D), lambda i: (0, 0))
    return pl.pallas_call(
        body,
        name="loss_ln_bwd",
        grid=(S // tm,),
        in_specs=[row, row, pl.BlockSpec((tm, 1), lambda i: (i, 0)), vec, vec],
        out_specs=[vec, row, row, vec, vec],
        out_shape=[_sds((1, D), F32), _sds((S, D), F32), _sds((S, D), MXU_DTYPE), _sds((1, D), F32), _sds((1, D), F32)],
        compiler_params=_cparams(("arbitrary",)),
    )(target, xhat, rstd, g, b)


def _mlp_up(name, x, w_up, tm, tn, tk):
    S = x.shape[0]
    F = w_up.shape[1]

    def epilogue(acc, extra_refs, out_refs, j, ci):
        r = jnp.maximum(acc, 0.0)
        out_refs[0][...] = (r * r).astype(out_refs[0].dtype)

    return _matmul(name, x, w_up, "nn", tm, tn, tk, [(_sds((S, F), MXU_DTYPE), _ij_spec(tm, tn), True)], epilogue,
                   split=("cols", 2))[0]


def _mlp_down_bwd(name, dy, w_down, a, tm, tn, tk):
    S, F = a.shape

    def epilogue(acc, extra_refs, out_refs, j, ci):
        out_refs[0][...] = (acc * (2.0 * jnp.sqrt(extra_refs[0][...]).astype(F32))).astype(out_refs[0].dtype)

    return _matmul(name, dy, w_down, "nt", tm, tn, tk, [(_sds((S, F), MXU_DTYPE), _ij_spec(tm, tn), True)], epilogue,
                   extras=[(a, _ij_spec(tm, tn), True)], split=("cols", 2))[0]


def _mm_nt_res_ln_bwd(name, dy, w, du, xhat, rstd, g, tm, tk, dep):
    S, D = du.shape

    def epilogue(acc, extra_refs, out_refs, j, ci):
        du_ref, xh_ref, r_ref, g_ref = extra_refs
        first = (pl.program_id(0) == 0) if ci == 0 else None
        _ln_bwd_rows(ALPHA * du_ref[...] + acc, xh_ref[...], r_ref[...], g_ref[...], first, *out_refs)

    row = pl.BlockSpec((tm, D), lambda i, j, k: (i, 0))
    vec = pl.BlockSpec((1, D), lambda i, j, k: (0, 0))
    return _matmul(name, dy, w, "nt", tm, D, tk,
                   [(_sds((S, D), F32), row, True), (_sds((S, D), MXU_DTYPE), row, True), (_sds((1, D), F32), vec),
                    (_sds((1, D), F32), vec)], epilogue,
                   extras=[(du, row, True), (xhat, row, True), (rstd, pl.BlockSpec((tm, 1), lambda i, j, k: (i, 0)), True), (g, vec)],
                   dep=dep, sem=("arbitrary", "arbitrary", "arbitrary"), split=("rows", 2))


def _attn_out_bwd(du, w_out, o, sel_t, tm, tk):
    S, D = o.shape

    def epilogue(acc, extra_refs, out_refs, j, ci):
        out_refs[0][...] = acc
        out_refs[1][...] = _exact_nn(acc * extra_refs[0][...], extra_refs[1][...])

    row = pl.BlockSpec((tm, D), lambda i, j, k: (i, 0))
    slim = pl.BlockSpec((tm, LANES), lambda i, j, k: (i, 0))
    return _matmul("attn_out_bwd", du, w_out, "nt", tm, D, tk,
                   [(_sds((S, D), F32), row, True), (_sds((S, LANES), F32), slim, True)], epilogue,
                   extras=[(o, row, True), (sel_t, pl.BlockSpec((D, LANES), lambda i, j, k: (0, 0)))], split=("rows", 2))


def _adamw(name, w, gs, m, v):
    shape = w.shape
    cols = shape[-1]
    rows = math.prod(shape[:-1])
    w2, m2, v2 = (t.reshape(rows, cols) for t in (w, m, v))
    gs2 = [g.reshape(-1, cols) for g in gs]
    ng = len(gs2)
    tr = _pick(rows // ng, (256, 128, 64, 32, 16, 8))
    per = rows // ng // tr
    c1 = 1.0 - ADAM_B1 ** ADAM_STEP
    c2 = 1.0 - ADAM_B2 ** ADAM_STEP

    def body(w_ref, m_ref, v_ref, *rest):
        g_refs, (d_ref, nm_ref, nv_ref), g_out = rest[:ng], rest[ng:ng + 3], rest[ng + 3:]
        gg = g_refs[0][...]
        if ng == 2:
            gg = jnp.where(pl.program_id(0) < per, gg, g_refs[1][...])
        g_out[0][...] = gg
        nm = ADAM_B1 * m_ref[...] + (1.0 - ADAM_B1) * gg
        nv = ADAM_B2 * v_ref[...] + (1.0 - ADAM_B2) * (gg * gg)
        nm_ref[...] = nm
        nv_ref[...] = nv
        d_ref[...] = -ADAM_LR * ((nm / c1) / (jnp.sqrt(nv / c2) + ADAM_EPS) + ADAM_WD * w_ref[...])

    blk = pl.BlockSpec((tr, cols), lambda i: (i, 0))
    g_specs = [blk] if ng == 1 else [pl.BlockSpec((tr, cols), lambda i: (jnp.minimum(i, per - 1), 0)),
                                     pl.BlockSpec((tr, cols), lambda i: (jnp.maximum(i - per, 0), 0))]
    outs = pl.pallas_call(
        body,
        name=name,
        grid=(rows // tr,),
        in_specs=[blk] * 3 + g_specs,
        out_specs=[blk] * 4,
        out_shape=[_sds((rows, cols), F32)] * 4,
        compiler_params=_cparams(("parallel",)),
    )(w2, m2, v2, *gs2)
    return tuple(o.reshape(shape) for o in outs)


HBM = pl.BlockSpec(memory_space=pl.ANY)


def _shard_slice(ref, axis, size, index):
    idx = [slice(None)] * len(ref.shape)
    idx[axis] = pl.ds(pl.multiple_of(index * size, 8), size)
    return ref.at[tuple(idx)]


def _share_halves(name, full, tr):
    R, W4 = full.shape
    W, h = W4 // 4, R // 2
    steps = [(k, t) for k in range(3) for t in range(h // tr)]

    def body(f_in, f_ref, buf, lsem, ssem, rsem):
        x, y, c = lax.axis_index("x"), lax.axis_index("y"), lax.axis_index("c")
        sibling = (x, y, 1 - c)
        chips = [(1 - x, y), (x, 1 - y), (1 - x, 1 - y)]

        def tile(k, t):
            px, py = chips[k]
            return f_ref.at[pl.ds(pl.multiple_of(c * h + t * tr, 8), tr), pl.ds(pl.multiple_of((2 * px + py) * W, LANES), W)]

        sends = []
        for s, (k, t) in enumerate(steps):
            slot = s % 2
            if s >= 2:
                sends[s - 2].wait_send()
            lc = pltpu.make_async_copy(tile(k, t), buf.at[slot], lsem.at[slot])
            lc.start()
            lc.wait()
            rc = pltpu.make_async_remote_copy(src_ref=buf.at[slot], dst_ref=tile(k, t), send_sem=ssem.at[slot], recv_sem=rsem,
                                              device_id=sibling, device_id_type=MESH)
            rc.start()
            sends.append(rc)
        for rc in sends[-2:]:
            rc.wait_send()
        whole = f_ref.at[pl.ds(0, h), pl.ds(0, 3 * W)]
        pltpu.make_async_remote_copy(src_ref=whole, dst_ref=whole, send_sem=ssem.at[0], recv_sem=rsem,
                                     device_id=sibling, device_id_type=MESH).wait_recv()

    return pl.pallas_call(
        body,
        name=name,
        in_specs=[HBM],
        out_specs=HBM,
        out_shape=_sds(full.shape, full.dtype),
        input_output_aliases={0: 0},
        scratch_shapes=[pltpu.VMEM((2, tr, W), full.dtype), pltpu.SemaphoreType.DMA((2,)), pltpu.SemaphoreType.DMA((2,)),
                        pltpu.SemaphoreType.DMA(())],
    )(full)


IN_HBM = pl.BlockSpec(memory_space=pltpu.HBM)
IN_SEM = pl.BlockSpec(memory_space=pltpu.SEMAPHORE)
DATAFLOW = pltpu.SideEffectType.DATAFLOW_SIDE_EFFECTING


def _hbm(t):
    return pltpu.with_memory_space_constraint(t, pltpu.HBM)


def _token_spec():
    return pl.BlockSpec(memory_space=pltpu.VMEM)


def _gather_copies(s_refs, f_refs, axes, halves, send, recv, loc, arrival):
    x, y, c = lax.axis_index("x"), lax.axis_index("y"), lax.axis_index("c")
    chips = [(1 - x, y), (x, 1 - y), (1 - x, 1 - y)]
    local, remote = [], []
    for a in range(len(s_refs)):
        size = s_refs[a].shape[axes[a]]
        local.append(pltpu.make_async_copy(s_refs[a], _shard_slice(f_refs[a], axes[a], size, 2 * x + y), loc.at[a]))
        for k, (px, py) in enumerate(chips):
            block = (2 * px + py) if arrival else (2 * x + y)
            src, dst = s_refs[a], _shard_slice(f_refs[a], axes[a], size, block)
            if halves:
                assert axes[a] == 1 and len(s_refs[a].shape) == 2
                h = s_refs[a].shape[0] // 2
                rows = pl.ds(pl.multiple_of(c * h, 8), h)
                src = s_refs[a].at[rows, :]
                dst = f_refs[a].at[rows, pl.ds(pl.multiple_of(block * size, LANES), size)]
            remote.append(pltpu.make_async_remote_copy(src_ref=src, dst_ref=dst, send_sem=send.at[3 * a + k],
                                                       recv_sem=recv.at[3 * a + k], device_id=(px, py, c), device_id_type=MESH))
    return local, remote


def _gather_start(name, shards, axes, after, halves=False):
    n = len(shards)
    fulls = []
    for s, ax in zip(shards, axes):
        fs = list(s.shape)
        fs[ax] *= 4
        fulls.append(lax.empty(tuple(fs), s.dtype))

    def body(*refs):
        s_refs, f_refs = refs[:n], refs[n:2 * n]
        send, recv, loc, token = refs[2 * n + 1], refs[2 * n + 2], refs[2 * n + 3], refs[-1]
        local, remote = _gather_copies(s_refs, f_refs, axes, halves, send, recv, loc, arrival=False)
        for cp in remote + local:
            cp.start()
        token[...] = jnp.zeros_like(token)

    outs = pl.pallas_call(
        body,
        name=name,
        out_shape=(pltpu.SemaphoreType.DMA((3 * n,)), pltpu.SemaphoreType.DMA((3 * n,)), pltpu.SemaphoreType.DMA((n,)),
                   *[pltpu.HBM(t.shape, t.dtype) for t in shards + fulls], _sds((8, LANES), F32)),
        in_specs=[IN_HBM] * (2 * n) + [HBM],
        out_specs=(IN_SEM, IN_SEM, IN_SEM, *[IN_HBM] * (2 * n), _token_spec()),
        input_output_aliases={i: 3 + i for i in range(2 * n)},
        compiler_params=pltpu.CompilerParams(has_side_effects=DATAFLOW),
    )(*[_hbm(t) for t in shards + fulls], after)
    return (outs[0], outs[1], outs[2], list(outs[3:3 + n]), list(outs[3 + n:3 + 2 * n]), axes, halves), outs[-1]


def _gather_wait(name, state, *after):
    send, recv, loc, s_thru, f_thru, axes, halves = state
    n = len(s_thru)

    def body(*refs):
        s_refs, f_refs = refs[:n], refs[n:2 * n]
        local, remote = _gather_copies(s_refs, f_refs, axes, halves, refs[2 * n], refs[2 * n + 1], refs[2 * n + 2], arrival=True)
        for cp in local:
            cp.wait()
        for cp in remote:
            cp.wait_send()
            cp.wait_recv()

    outs = pl.pallas_call(
        body,
        name=name,
        out_shape=tuple(pltpu.HBM(t.shape, t.dtype) for t in s_thru + f_thru),
        in_specs=[IN_HBM] * (2 * n) + [IN_SEM, IN_SEM, IN_SEM] + [HBM] * len(after),
        out_specs=tuple([IN_HBM] * (2 * n)),
        input_output_aliases={i: i for i in range(2 * n)},
        compiler_params=pltpu.CompilerParams(has_side_effects=DATAFLOW),
    )(*s_thru, *f_thru, send, recv, loc, *after)
    return list(outs[n:2 * n])


FLIPS = [(fx, fy, fc) for fx in (0, 1) for fy in (0, 1) for fc in (0, 1)][1:]


def _piece_shape(shape, axis):
    ps = list(shape)
    if axis == 0:
        ps[0] //= 8
    else:
        ps[0] //= 2
        ps[axis] //= 4
    return tuple(ps)


def _piece(ref, axis, q, c):
    shape = ref.shape
    idx = [slice(None)] * len(shape)
    if axis == 0:
        h = shape[0] // 8
        idx[0] = pl.ds(pl.multiple_of((2 * q + c) * h, 8), h)
    else:
        h, w = shape[0] // 2, shape[axis] // 4
        idx[0] = pl.ds(c * h, h)
        idx[axis] = pl.ds(pl.multiple_of(q * w, LANES if axis == len(shape) - 1 else 8), w)
    return ref.at[tuple(idx)]


def _scatter_copies(g_refs, l_refs, axes, send, recv):
    x, y, c = lax.axis_index("x"), lax.axis_index("y"), lax.axis_index("c")
    out = []
    for a in range(len(g_refs)):
        for k, (fx, fy, fc) in enumerate(FLIPS):
            tx, ty, tc = x ^ fx, y ^ fy, c ^ fc
            out.append(pltpu.make_async_remote_copy(
                src_ref=_piece(g_refs[a], axes[a], 2 * tx + ty, tc), dst_ref=l_refs[a].at[k],
                send_sem=send.at[7 * a + k], recv_sem=recv.at[7 * a + k], device_id=(tx, ty, tc), device_id_type=MESH))
    return out


def _scatter_start(name, grads, axes):
    n = len(grads)
    lands = [lax.empty((7,) + _piece_shape(g.shape, ax), g.dtype) for g, ax in zip(grads, axes)]

    def body(*refs):
        g_refs, l_refs = refs[:n], refs[n:2 * n]
        send, recv, token = refs[2 * n], refs[2 * n + 1], refs[-1]
        for cp in _scatter_copies(g_refs, l_refs, axes, send, recv):
            cp.start()
        token[...] = jnp.zeros_like(token)

    outs = pl.pallas_call(
        body,
        name=name,
        out_shape=(pltpu.SemaphoreType.DMA((7 * n,)), pltpu.SemaphoreType.DMA((7 * n,)),
                   *[pltpu.HBM(t.shape, t.dtype) for t in grads + lands], _sds((8, LANES), F32)),
        in_specs=[IN_HBM] * (2 * n),
        out_specs=(IN_SEM, IN_SEM, *[IN_HBM] * (2 * n), _token_spec()),
        input_output_aliases={i: 2 + i for i in range(2 * n)},
        compiler_params=pltpu.CompilerParams(has_side_effects=DATAFLOW),
    )(*[_hbm(t) for t in grads + lands])
    return (outs[0], outs[1], list(outs[2:2 + n]), list(outs[2 + n:2 + 2 * n]), axes), outs[-1]


def _scatter_wait(name, state, *after):
    send, recv, g_thru, l_thru, axes = state
    n = len(g_thru)

    def body(*refs):
        g_refs, l_refs = refs[:n], refs[n:2 * n]
        for cp in _scatter_copies(g_refs, l_refs, axes, refs[2 * n], refs[2 * n + 1]):
            cp.wait_send()
            cp.wait_recv()

    outs = pl.pallas_call(
        body,
        name=name,
        out_shape=tuple(pltpu.HBM(t.shape, t.dtype) for t in g_thru + l_thru),
        in_specs=[IN_HBM] * (2 * n) + [IN_SEM, IN_SEM] + [HBM] * len(after),
        out_specs=tuple([IN_HBM] * (2 * n)),
        input_output_aliases={i: i for i in range(2 * n)},
        compiler_params=pltpu.CompilerParams(has_side_effects=DATAFLOW),
    )(*g_thru, *l_thru, send, recv, *after)
    return list(outs[:n]), list(outs[n:2 * n])


def _reduce_join(name, landing, g, axis):
    R, C = _piece_shape(g.shape, axis)
    l3 = landing.reshape(7, R, C)
    tr = _pick(R, [t for t in (512, 256, 128, 64, 32, 16, 8) if t * C <= 256 * 1024])
    nsteps = R // tr

    def own_block(i):
        q, c = 2 * lax.axis_index("x") + lax.axis_index("y"), lax.axis_index("c")
        return ((2 * q + c) * nsteps + i, 0) if axis == 0 else (c * nsteps + i, q)

    def body(own_ref, l_ref, o_ref, buf, send, loc, recv):
        i = pl.program_id(0)
        x, y, c = lax.axis_index("x"), lax.axis_index("y"), lax.axis_index("c")
        sibling = (x, y, 1 - c)

        def copies(slot, step):
            dst = o_ref.at[pl.ds(pl.multiple_of(c * R + step * tr, 8), tr), :]
            return (pltpu.make_async_copy(buf.at[slot], dst, loc.at[slot]),
                    pltpu.make_async_remote_copy(src_ref=buf.at[slot], dst_ref=dst, send_sem=send.at[slot], recv_sem=recv,
                                                 device_id=sibling, device_id_type=MESH))

        @pl.when(i >= 2)
        def _():
            lc, rc = copies(i % 2, i - 2)
            lc.wait()
            rc.wait_send()

        acc = own_ref[...].astype(F32)
        for s in range(7):
            acc = acc + l_ref[s].astype(F32)
        buf[i % 2] = acc
        lc, rc = copies(i % 2, i)
        lc.start()
        rc.start()

        @pl.when(i == nsteps - 1)
        def _():
            for st in range(max(nsteps - 2, 0), nsteps):
                lc, rc = copies(st % 2, st)
                lc.wait()
                rc.wait_send()
            theirs = o_ref.at[pl.ds(pl.multiple_of((1 - c) * R, 8), R), :]
            pltpu.make_async_remote_copy(src_ref=theirs, dst_ref=theirs, send_sem=send.at[0], recv_sem=recv,
                                         device_id=sibling, device_id_type=MESH).wait_recv()

    return pl.pallas_call(
        body,
        name=name,
        grid=(nsteps,),
        in_specs=[pl.BlockSpec((tr, C), own_block), pl.BlockSpec((7, tr, C), lambda i: (0, i, 0))],
        out_specs=HBM,
        out_shape=_sds((2 * R, C), F32),
        scratch_shapes=[pltpu.VMEM((2, tr, C), F32), pltpu.SemaphoreType.DMA((2,)), pltpu.SemaphoreType.DMA((2,)),
                        pltpu.SemaphoreType.DMA(())],
        compiler_params=_cparams(("arbitrary",)),
    )(g, l3)


def _all_reduce_small(v, dep):
    R, D = v.shape

    def body(v_ref, dep_ref, o_ref, land, send, recv):
        x, y, c = lax.axis_index("x"), lax.axis_index("y"), lax.axis_index("c")
        my_slot = 4 * x + 2 * y + c
        land[my_slot] = v_ref[...]
        for k, (fx, fy, fc) in enumerate(FLIPS):
            tx, ty, tc = x ^ fx, y ^ fy, c ^ fc
            pltpu.make_async_remote_copy(src_ref=v_ref, dst_ref=land.at[my_slot], send_sem=send.at[k], recv_sem=recv.at[k],
                                         device_id=(tx, ty, tc), device_id_type=MESH).start()
        for k, (fx, fy, fc) in enumerate(FLIPS):
            tx, ty, tc = x ^ fx, y ^ fy, c ^ fc
            cp = pltpu.make_async_remote_copy(src_ref=v_ref, dst_ref=land.at[4 * tx + 2 * ty + tc], send_sem=send.at[k],
                                              recv_sem=recv.at[k], device_id=(tx, ty, tc), device_id_type=MESH)
            cp.wait_send()
            cp.wait_recv()
        acc = land[0]
        for s in range(1, 8):
            acc = acc + land[s]
        o_ref[...] = acc

    return pl.pallas_call(
        body,
        name="all_reduce_small",
        in_specs=[pl.BlockSpec(memory_space=pltpu.VMEM), pl.BlockSpec(memory_space=pl.ANY)],
        out_specs=pl.BlockSpec(memory_space=pltpu.VMEM),
        out_shape=_sds((R, D), F32),
        scratch_shapes=[pltpu.VMEM((8, R, D), F32), pltpu.SemaphoreType.DMA((7,)), pltpu.SemaphoreType.DMA((7,))],
    )(v, dep)


def kernel(x, attn_w_in, attn_w_out, hgrn_w_in, hgrn_w_out, hgrn_norm_g, lb_logits, ln_mix_g, ln_mix_b, ln_ffn_g, ln_ffn_b, ffn_w_up, ffn_w_down, loss_target, m_attn_w_in, m_attn_w_out, m_hgrn_w_in, m_hgrn_w_out, m_hgrn_norm_g, m_lb_logits, m_ln_mix_g, m_ln_mix_b, m_ln_ffn_g, m_ln_ffn_b, m_ffn_w_up, m_ffn_w_down, v_attn_w_in, v_attn_w_out, v_hgrn_w_in, v_hgrn_w_out, v_hgrn_norm_g, v_lb_logits, v_ln_mix_g, v_ln_mix_b, v_ln_ffn_g, v_ln_ffn_b, v_ffn_w_up, v_ffn_w_down):
    xs = x[0]
    tgt = loss_target[0]
    S, D = xs.shape
    F = ffn_w_up.shape[2] * 4
    T1 = _pick(S, (1024, 512, 256))
    T2 = _pick(S, (2048, 1024, 512))
    TH = _pick(S, (512, 256))
    TB = _pick(S, (128,))
    TF = _pick(F, (1024, 512))
    TG = 3 * D // 4

    cast = lambda w: w.astype(MXU_DTYPE)
    st_a, tok = _gather_start("gather_a", [cast(attn_w_in[0])], [1], jnp.zeros((8, LANES), F32), halves=True)
    tok, (xs_late, w_aout, w_fup, w_fdown, w_hin, w_hout) = lax.optimization_barrier(
        (tok, (xs, attn_w_out, ffn_w_up, ffn_w_down, hgrn_w_in, hgrn_w_out)))
    st_b, tok = _gather_start("gather_b", [cast(w_aout[0]), cast(w_fup[0]), cast(w_fdown[0])], [0, 1, 0], tok)
    st_c, tok = _gather_start("gather_c", [cast(w_hin[0]), cast(w_hout[0]), hgrn_norm_g, cast(w_fup[1]), cast(w_fdown[1])],
                              [1, 0, 1, 1, 0], tok)

    cos3, sin3 = _rope_tables(S)
    sel = _head_sel(D)
    sel_t = sel.T

    xc3 = _stack_classes("x_classes", xs_late, MXU_DTYPE)
    P3 = _attn_proj("attn_proj_own", xc3, st_a[3][0], cos3, sin3, T2, None, tok)
    (wa_in,) = _gather_wait("gather_a_wait", st_a, P3)
    wa_in = _share_halves("share_a", wa_in, _pick(D // 2, (256, 128)))
    P3 = _attn_proj("attn_proj", xc3, wa_in, cos3, sin3, T2, P3)
    o3, lse3 = _attn_fwd(P3, D)
    o_att, L_att = _attn_mix(o3, lse3, sel)
    wa_out, w_up0, w_down0 = _gather_wait("gather_b_wait", st_b, L_att)
    ln1 = (ln_mix_g[0:1], ln_mix_b[0:1])
    ln2 = (ln_ffn_g[0:1], ln_ffn_b[0:1])
    ln3 = (ln_mix_g[1:2], ln_mix_b[1:2])
    ln4 = (ln_ffn_g[1:2], ln_ffn_b[1:2])
    xm1, xh1, r1 = _mm_res_ln("attn_out_ln", o_att, wa_out, xs, *ln1, TH, D)
    a0 = _mlp_up("mlp0_up", xm1, w_up0, T2, TF, D)
    xm2, xh2, r2 = _mm_res_ln("mlp0_down_ln", a0, w_down0, (xh1, *ln1), *ln2, TH, F)

    wh_in, wh_out, norm_g, w_up1, w_down1 = _gather_wait("gather_c_wait", st_c, r2)
    P1 = _plain_mm("hgrn_proj", xm2, wh_in, "nn", F32, T1, _pick(3 * D, (1024, 768, 512)), D)
    o_h, n_h, states = _hgrn_fwd(P1, lb_logits, norm_g, TB)
    xm3, xh3, r3 = _mm_res_ln("hgrn_out_ln", n_h, wh_out, (xh2, *ln2), *ln3, TH, D)
    a1 = _mlp_up("mlp1_up", xm3, w_up1, T2, TF, D)
    _, xh4, r4 = _mm_res_ln("mlp1_down_ln", a1, w_down1, (xh3, *ln3), *ln4, TH, F)

    wgrad = lambda name, a, dy, tm, tn, tk=T1: _plain_mm(name, a, dy, "tn", MXU_DTYPE, tm, tn, tk)
    sq, du4, dum4, dg_ffn1, db_ffn1 = _loss_ln_bwd(tgt, xh4, r4, *ln4, TH)
    dh1 = _mlp_down_bwd("mlp1_down_bwd", dum4, w_down1, a1, T2, TF, D)
    g_down1 = wgrad("g_down1", a1, dum4, TF, D, S)
    g_up1 = wgrad("g_up1", xm3, dh1, D, TF, S)
    sc_1, tok = _scatter_start("scatter_1", [g_down1, g_up1], [0, 1])
    du3, dum3, dg_mix1, db_mix1 = _mm_nt_res_ln_bwd("mlp1_up_bwd", dh1, w_up1, du4, xh3, r3, ln_mix_g[1:2], TH, F, tok)
    dn = _plain_mm("hgrn_out_bwd", dum3, wh_out, "nt", F32, T1, D, D)
    g_hout = wgrad("g_hgrn_out", n_h, dum3, D, D)
    dP1, dg_norm, dlb = _hgrn_bwd(P1, o_h, states, dn, lb_logits, norm_g, TB)
    g_hin = wgrad("g_hgrn_in", xm2, dP1, D, D, S)
    d_lb_logits = _lb_logits_grad(dlb, lb_logits)
    sc_2, tok = _scatter_start("scatter_2", [g_hout, g_hin], [0, 1])

    du2, dum2, dg_ffn0, db_ffn0 = _mm_nt_res_ln_bwd("hgrn_in_bwd", dP1, wh_in, du3, xh2, r2, ln_ffn_g[0:1], TH, 3 * D, tok)
    dh0 = _mlp_down_bwd("mlp0_down_bwd", dum2, w_down0, a0, T2, TF, D)
    g_down0 = wgrad("g_down0", a0, dum2, TF, D, S)
    g_up0 = wgrad("g_up0", xm1, dh0, D, TF, S)
    sc_3, tok = _scatter_start("scatter_3", [g_down0, g_up0], [0, 1])
    du1, dum1, dg_mix0, db_mix0 = _mm_nt_res_ln_bwd("mlp0_up_bwd", dh0, w_up0, du2, xh1, r1, ln_mix_g[0:1], TH, F, tok)
    do, delta = _attn_out_bwd(dum1, wa_out, o_att, sel_t, TH, D)
    g_aout = wgrad("g_attn_out", o_att, dum1, D, D)
    sc_5, tok = _scatter_start("scatter_5", [g_aout], [0])
    dP3 = _attn_bwd(P3, _stack_classes("do_classes", do, MXU_DTYPE), _stack_classes("lse_classes", L_att, F32),
                    _stack_classes("delta_classes", delta, F32), cos3, sin3, D, tok)
    small = jnp.concatenate([d_lb_logits, dg_mix0, dg_mix1, db_mix0, db_mix1, dg_ffn0, dg_ffn1, db_ffn0, db_ffn1,
                             dg_norm, sq, jnp.zeros((4, D), F32)], axis=0)
    small = _all_reduce_small(small, dP3)
    loss = 0.5 * jnp.sum(small[11]) / D
    grp = lambda j: j // (3 * D // TG)
    g_ain = _matmul("g_attn_in", xc3, dP3, "tn", D, TG, S, [(_sds((D, 9 * D), MXU_DTYPE), _ij_spec(D, TG))], _store_epilogue,
                    a_map=lambda i, j, k: (grp(j), i),
                    b_map=lambda i, j, k: (grp(j), j % (3 * D // TG)), mnk=(D, 9 * D, S), dep=small)[0]
    sc_4, tok = _scatter_start("scatter_4", [g_ain], [1])
    dxc3 = _matmul("attn_in_bwd", dP3, wa_in, "nt", T1, D, 3 * D, [(_sds((3 * S, D), F32), _ij_spec(T1, D))], _store_epilogue,
                   b_map=lambda i, j, k: (j, k + i // (S // T1)), mnk=(3 * S, D, 3 * D), dep=tok)[0]
    grad_x = _input_grad(du1, dxc3)

    def reduced(name, state, *after):
        gs, lands = _scatter_wait(name + "_wait", state, *after)
        return [_reduce_join(f"{name}_reduce_{i}", l, g, ax) for i, (l, g, ax) in enumerate(zip(lands, gs, state[4]))]

    r_down1, r_up1 = reduced("scatter_1", sc_1, grad_x)
    r_hout, r_hin = reduced("scatter_2", sc_2, r_up1)
    r_down0, r_up0 = reduced("scatter_3", sc_3, r_hin)
    (r_aout,) = reduced("scatter_5", sc_5, r_up0)

    my_chip = 2 * lax.axis_index("x") + lax.axis_index("y")
    nsh = hgrn_norm_g.shape[1]
    g_norm = lax.dynamic_slice(small[10:11], (0, my_chip * nsh), (1, nsh))

    grads, upd = {}, {}

    def update(nm, w, gs, m, v):
        upd[nm] = _adamw("adamw_" + nm, w, gs, m, v)
        grads[nm] = upd[nm][3]

    update("hgrn_w_in", hgrn_w_in, [r_hin], m_hgrn_w_in, v_hgrn_w_in)
    update("hgrn_w_out", hgrn_w_out, [r_hout], m_hgrn_w_out, v_hgrn_w_out)
    update("ffn_w_up", ffn_w_up, [r_up0, r_up1], m_ffn_w_up, v_ffn_w_up)
    update("ffn_w_down", ffn_w_down, [r_down0, r_down1], m_ffn_w_down, v_ffn_w_down)
    update("attn_w_out", attn_w_out, [r_aout], m_attn_w_out, v_attn_w_out)
    update("hgrn_norm_g", hgrn_norm_g, [g_norm], m_hgrn_norm_g, v_hgrn_norm_g)
    cat = lambda ts: jnp.concatenate(ts, axis=0)
    small_w = cat([lb_logits, ln_mix_g, ln_mix_b, ln_ffn_g, ln_ffn_b])
    small_m = cat([m_lb_logits, m_ln_mix_g, m_ln_mix_b, m_ln_ffn_g, m_ln_ffn_b])
    small_v = cat([v_lb_logits, v_ln_mix_g, v_ln_mix_b, v_ln_ffn_g, v_ln_ffn_b])
    small_upd = _adamw("adamw_small", small_w, [small[0:10]], small_m, small_v)
    for i, nm in enumerate(["lb_logits", "ln_mix_g", "ln_mix_b", "ln_ffn_g", "ln_ffn_b"]):
        grads[nm] = small[2 * i:2 * i + 2]
        upd[nm] = tuple(t[2 * i:2 * i + 2] for t in small_upd)
    done = [upd[k][2] for k in ("hgrn_w_in", "hgrn_w_out", "ffn_w_up", "ffn_w_down", "attn_w_out", "hgrn_norm_g")]
    (r_ain,) = reduced("scatter_4", sc_4, small_upd[2], *done)
    update("attn_w_in", attn_w_in, [r_ain], m_attn_w_in, v_attn_w_in)

    order = ["attn_w_in", "attn_w_out", "hgrn_w_in", "hgrn_w_out", "hgrn_norm_g", "lb_logits", "ln_mix_g", "ln_mix_b",
             "ln_ffn_g", "ln_ffn_b", "ffn_w_up", "ffn_w_down"]
    return (loss, grad_x[None], *[grads[k] for k in order], *[upd[k][0] for k in order],
            *[upd[k][1] for k in order], *[upd[k][2] for k in order])
```
